```python
import jax, jax.numpy as jnp
from jax import lax
import numpy as np

D_MODEL = 1024
BATCH = 8
SEQ = 8192
DEPTH = 1

GRID_W = 64
CTX_LEN = 256
HEAD_DIM = 64
N_HEADS_TOTAL = D_MODEL // HEAD_DIM
ATT_HEADS = N_HEADS_TOTAL // 2
ATT_KV_HEADS = ATT_HEADS // 4
WINDOW = 128
BLOCK = 128
ROPE_BASE = 10000.0
GLA_HEADS = N_HEADS_TOTAL - ATT_HEADS
GLA_DV = HEAD_DIM
GLA_DK = HEAD_DIM // 2
GLA_CHUNK = 64
GATE_RANK = 16
GATE_TAU = 16.0
MIX_WIDTH = ATT_HEADS * HEAD_DIM + GLA_HEADS * GLA_DV
FFN_HIDDEN = -(-8 * D_MODEL // (3 * 256)) * 256
SPLIT_SIZES = (ATT_HEADS * HEAD_DIM, ATT_KV_HEADS * HEAD_DIM, ATT_KV_HEADS * HEAD_DIM,
               GLA_HEADS * GLA_DK, GLA_HEADS * GLA_DK, GLA_HEADS * GLA_DV, GLA_HEADS * GLA_DV,
               GATE_RANK, GATE_RANK)
IN_COLS = sum(SPLIT_SIZES)
NEG_INF = -1e30

kernel_name = 'hybrid_window_gqa_gla_dit_block'


def rmsnorm(x, gain, eps=1e-6):
    x32 = x.astype(jnp.float32)
    y = x32 * lax.rsqrt(jnp.mean(x32 * x32, axis=-1, keepdims=True) + eps)
    return y.astype(x.dtype) * gain


def modulate(h, shift, scale):
    return h * (1.0 + scale) + shift


def heads(t, n):
    return t.reshape(t.shape[:-1] + (n, t.shape[-1] // n))


def flip(t):
    return jnp.flip(t, axis=1)


def split_columns(p):
    idx = np.cumsum(SPLIT_SIZES)[:-1].tolist()
    return jnp.split(p, idx, axis=-1)


def axial_rope_tables(n_tokens):
    ROWS = n_tokens // GRID_W
    row = jnp.repeat(jnp.arange(ROWS), GRID_W).astype(jnp.float32)
    col = jnp.tile(jnp.arange(GRID_W), ROWS).astype(jnp.float32)
    half = HEAD_DIM // 2
    inv_freq = ROPE_BASE ** (-jnp.arange(0, half, 2, dtype=jnp.float32) / half)
    ang_r = row[:, None] * inv_freq[None, :]
    ang_c = col[:, None] * inv_freq[None, :]
    ang = jnp.concatenate([ang_r, ang_r, ang_c, ang_c], axis=-1)
    return jnp.cos(ang), jnp.sin(ang)


def apply_rope(x, cos, sin):
    shp = x.shape
    xr = x.reshape(shp[:-1] + (2, 2, HEAD_DIM // 4))
    rot = jnp.concatenate([-xr[..., 1:2, :], xr[..., 0:1, :]], axis=-2).reshape(shp)
    return x * cos[:, None, :].astype(x.dtype) + rot * sin[:, None, :].astype(x.dtype)


def softmax_with_sink(logits, sink):
    sink_col = jnp.broadcast_to(sink, logits.shape[:-1] + (1,))
    p = jax.nn.softmax(jnp.concatenate([logits, sink_col], axis=-1), axis=-1)
    return p[..., :-1]


def window_attention(q, k, v, k_ctx, v_ctx, sink):
    B, S, H, dh = q.shape
    G = H // ATT_KV_HEADS
    nb = S // BLOCK
    scale = dh ** -0.5
    qb = q.reshape(B, nb, BLOCK, ATT_KV_HEADS, G, dh)
    pad = ((0, 0), (BLOCK, BLOCK), (0, 0), (0, 0))
    kp = jnp.pad(k, pad).reshape(B, nb + 2, BLOCK, ATT_KV_HEADS, dh)
    vp = jnp.pad(v, pad).reshape(B, nb + 2, BLOCK, ATT_KV_HEADS, dh)
    kw = jnp.concatenate([kp[:, :-2], kp[:, 1:-1], kp[:, 2:]], axis=2)
    vw = jnp.concatenate([vp[:, :-2], vp[:, 1:-1], vp[:, 2:]], axis=2)
    s_win = jnp.einsum('bnqhgd,bnkhd->bhgnqk', qb, kw).astype(jnp.float32) * scale
    qpos = jnp.arange(nb)[:, None] * BLOCK + jnp.arange(BLOCK)[None, :]
    kpos = (jnp.arange(nb)[:, None] - 1) * BLOCK + jnp.arange(3 * BLOCK)[None, :]
    rel = kpos[:, None, :] - qpos[:, :, None]
    mask = (jnp.abs(rel) <= WINDOW) & (kpos[:, None, :] >= 0) & (kpos[:, None, :] < S)
    s_win = jnp.where(mask, s_win, NEG_INF)
    s_ctx = jnp.einsum('bnqhgd,bchd->bhgnqc', qb, k_ctx).astype(jnp.float32) * scale
    sink_b = sink.astype(jnp.float32).reshape(ATT_KV_HEADS, G)[None, :, :, None, None, None]
    p = softmax_with_sink(jnp.concatenate([s_win, s_ctx], axis=-1), sink_b).astype(v.dtype)
    n_win = 3 * BLOCK
    o = (jnp.einsum('bhgnqk,bnkhd->bnqhgd', p[..., :n_win], vw)
         + jnp.einsum('bhgnqc,bchd->bnqhgd', p[..., n_win:], v_ctx))
    return o.reshape(B, S, H * dh)


def context_attention(q, k, v, sink):
    B, C, H, dh = q.shape
    G = H // ATT_KV_HEADS
    qg = q.reshape(B, C, ATT_KV_HEADS, G, dh)
    s = jnp.einsum('bqhgd,bkhd->bhgqk', qg, k).astype(jnp.float32) * dh ** -0.5
    sink_b = sink.astype(jnp.float32).reshape(ATT_KV_HEADS, G)[None, :, :, None, None]
    p = softmax_with_sink(s, sink_b).astype(v.dtype)
    return jnp.einsum('bhgqk,bkhd->bqhgd', p, v).reshape(B, C, H * dh)


def log_decay(z, w_gate, b_gate):
    logits = (z @ w_gate + b_gate).astype(jnp.float32)
    return heads(jax.nn.log_sigmoid(logits) / GATE_TAU, GLA_HEADS)


def gla_chunked(q, k, v, log_a, s0):
    B, T, H, DK = q.shape
    DV = v.shape[-1]
    n = T // GLA_CHUNK
    f32 = jnp.float32
    qc = q.astype(f32).reshape(B, n, GLA_CHUNK, H, DK)
    kc = k.astype(f32).reshape(B, n, GLA_CHUNK, H, DK)
    vc = v.astype(f32).reshape(B, n, GLA_CHUNK, H, DV)
    b = jnp.cumsum(log_a.astype(f32).reshape(B, n, GLA_CHUNK, H, DK), axis=2)
    b_last = b[:, :, -1:]
    q_dec = qc * jnp.exp(b)
    k_inv = kc * jnp.exp(-b)
    lower = jnp.tril(jnp.ones((GLA_CHUNK, GLA_CHUNK), dtype=bool))
    A = jnp.where(lower, jnp.einsum('bnihk,bnjhk->bnhij', q_dec, k_inv), 0.0)
    intra = jnp.einsum('bnhij,bnjhv->bnihv', A, vc)
    dS = jnp.einsum('bnjhk,bnjhv->bnhkv', kc * jnp.exp(b_last - b), vc)
    decay = jnp.exp(b_last[:, :, 0])

    def step(state, inp):
        d, ds = inp
        return d[..., None] * state + ds, state

    s_final, s_before = lax.scan(step, s0, (jnp.moveaxis(decay, 1, 0), jnp.moveaxis(dS, 1, 0)))
    inter = jnp.einsum('bnihk,nbhkv->bnihv', q_dec, s_before)
    o = (intra + inter).reshape(B, T, H, DV).astype(v.dtype)
    return o, s_final


def gla_final_state(k, v, log_a):
    b = jnp.cumsum(log_a.astype(jnp.float32), axis=1)
    w = jnp.exp(b[:, -1:] - b)
    return jnp.einsum('bthk,bthv->bhkv', k.astype(jnp.float32) * w, v.astype(jnp.float32))


def gla_output(o, gate, g_norm):
    return rmsnorm(o, g_norm).reshape(gate.shape) * jax.nn.silu(gate)


def swiglu(h, w_in, w_out):
    g, u = jnp.split(h @ w_in, 2, axis=-1)
    return (jax.nn.silu(g) * u) @ w_out


def _fwd_setup_inputs(seed: int = 0) -> dict:
    key = jax.random.key(seed)
    ks = jax.random.split(key, 24)
    L, D = DEPTH, D_MODEL
    nrm = jax.random.normal
    f32 = jnp.float32
    return {
        'x': nrm(ks[0], (BATCH, SEQ, D), f32),
        'c': nrm(ks[1], (BATCH, D), f32),
        'ctx': nrm(ks[2], (BATCH, CTX_LEN, D), f32),
        'c_ctx': nrm(ks[3], (D,), f32),
        'w_ada': nrm(ks[4], (L, D, 6 * D), f32) * (0.5 * D ** -0.5),
        'b_ada': nrm(ks[5], (L, 6 * D), f32) * 0.02,
        'g_pre_mix': 1.0 + 0.05 * nrm(ks[6], (L, D), f32),
        'g_post_mix': 1.0 + 0.05 * nrm(ks[7], (L, D), f32),
        'g_pre_ffn': 1.0 + 0.05 * nrm(ks[8], (L, D), f32),
        'g_post_ffn': 1.0 + 0.05 * nrm(ks[9], (L, D), f32),
        'w_in': nrm(ks[10], (L, D, IN_COLS), f32) * D ** -0.5,
        'attn_sink': 0.5 * nrm(ks[11], (L, ATT_HEADS), f32),
        'w_gate_fwd': nrm(ks[12], (L, GATE_RANK, GLA_HEADS * GLA_DK), f32) * GATE_RANK ** -0.5,
        'b_gate_fwd': 0.5 * nrm(ks[13], (L, GLA_HEADS * GLA_DK), f32),
        'w_gate_bwd': nrm(ks[14], (L, GATE_RANK, GLA_HEADS * GLA_DK), f32) * GATE_RANK ** -0.5,
        'b_gate_bwd': 0.5 * nrm(ks[15], (L, GLA_HEADS * GLA_DK), f32),
        'g_gla_norm': 1.0 + 0.05 * nrm(ks[16], (L, GLA_DV), f32),
        'w_out': nrm(ks[17], (L, MIX_WIDTH, D), f32) * MIX_WIDTH ** -0.5,
        'w_ffn_in': nrm(ks[18], (L, D, 2 * FFN_HIDDEN), f32) * D ** -0.5,
        'w_ffn_out': nrm(ks[19], (L, FFN_HIDDEN, D), f32) * FFN_HIDDEN ** -0.5,
    }


def _fwd_reference(x, c, ctx, c_ctx, w_ada, b_ada, g_pre_mix, g_post_mix, g_pre_ffn, g_post_ffn,
              w_in, attn_sink, w_gate_fwd, b_gate_fwd, w_gate_bwd, b_gate_bwd, g_gla_norm,
              w_out, w_ffn_in, w_ffn_out):
    B, S, _ = x.shape
    cos, sin = axial_rope_tables(S)
    zero_state = jnp.zeros((B, GLA_HEADS, GLA_DK, GLA_DV), jnp.float32)
    for l in range(DEPTH):
        need_ctx_out = l < DEPTH - 1
        ada = jax.nn.silu(c) @ w_ada[l] + b_ada[l]
        ada_c = jax.nn.silu(c_ctx) @ w_ada[l] + b_ada[l]
        sh1, sc1, gt1, sh2, sc2, gt2 = jnp.split(ada[:, None, :], 6, axis=-1)
        sh1c, sc1c, gt1c, sh2c, sc2c, gt2c = jnp.split(ada_c, 6, axis=-1)

        h = modulate(rmsnorm(x, g_pre_mix[l]), sh1, sc1)
        hc = modulate(rmsnorm(ctx, g_pre_mix[l]), sh1c, sc1c)
        q, k, v, gq, gk, gv, gg, zf, zb = split_columns(h @ w_in[l])
        qc, kc, vc, gqc, gkc, gvc, ggc, zfc, zbc = split_columns(hc @ w_in[l])

        q_h = apply_rope(heads(q, ATT_HEADS), cos, sin)
        k_h = apply_rope(heads(k, ATT_KV_HEADS), cos, sin)
        v_h = heads(v, ATT_KV_HEADS)
        kc_h = heads(kc, ATT_KV_HEADS)
        vc_h = heads(vc, ATT_KV_HEADS)
        attn_lat = window_attention(q_h, k_h, v_h, kc_h, vc_h, attn_sink[l])

        gkc_h = heads(gkc, GLA_HEADS)
        gvc_h = heads(gvc, GLA_HEADS)
        la_fc = log_decay(zfc, w_gate_fwd[l], b_gate_fwd[l])
        la_bc = log_decay(zbc, w_gate_bwd[l], b_gate_bwd[l])
        if need_ctx_out:
            gqc_h = heads(gqc, GLA_HEADS) * GLA_DK ** -0.5
            oc_f, s_f = gla_chunked(gqc_h, gkc_h, gvc_h, la_fc, zero_state)
            oc_b, s_b = gla_chunked(flip(gqc_h), flip(gkc_h), flip(gvc_h), flip(la_bc), zero_state)
            gla_ctx = gla_output(oc_f + flip(oc_b), ggc, g_gla_norm[l])
            attn_ctx = context_attention(heads(qc, ATT_HEADS), kc_h, vc_h, attn_sink[l])
        else:
            s_f = gla_final_state(gkc_h, gvc_h, la_fc)
            s_b = gla_final_state(flip(gkc_h), flip(gvc_h), flip(la_bc))

        gq_h = heads(gq, GLA_HEADS) * GLA_DK ** -0.5
        gk_h = heads(gk, GLA_HEADS)
        gv_h = heads(gv, GLA_HEADS)
        la_f = log_decay(zf, w_gate_fwd[l], b_gate_fwd[l])
        la_b = log_decay(zb, w_gate_bwd[l], b_gate_bwd[l])
        o_f, _ = gla_chunked(gq_h, gk_h, gv_h, la_f, s_f)
        o_b, _ = gla_chunked(flip(gq_h), flip(gk_h), flip(gv_h), flip(la_b), s_b)
        gla_lat = gla_output(o_f + flip(o_b), gg, g_gla_norm[l])

        y = jnp.concatenate([attn_lat, gla_lat], axis=-1) @ w_out[l]
        x = x + gt1 * rmsnorm(y, g_post_mix[l])

        f = swiglu(modulate(rmsnorm(x, g_pre_ffn[l]), sh2, sc2), w_ffn_in[l], w_ffn_out[l])
        x = x + gt2 * rmsnorm(f, g_post_ffn[l])

        if need_ctx_out:
            yc = jnp.concatenate([attn_ctx, gla_ctx], axis=-1) @ w_out[l]
            ctx = ctx + gt1c * rmsnorm(yc, g_post_mix[l])
            fc = swiglu(modulate(rmsnorm(ctx, g_pre_ffn[l]), sh2c, sc2c), w_ffn_in[l], w_ffn_out[l])
            ctx = ctx + gt2c * rmsnorm(fc, g_post_ffn[l])
    return x


import jax as _jax
import jax.numpy as _jnp

TWIN_FORMAT = 'train_step'
FWD_PARAMS = ['x', 'c', 'ctx', 'c_ctx', 'w_ada', 'b_ada', 'g_pre_mix', 'g_post_mix', 'g_pre_ffn', 'g_post_ffn', 'w_in', 'attn_sink', 'w_gate_fwd', 'b_gate_fwd', 'w_gate_bwd', 'b_gate_bwd', 'g_gla_norm', 'w_out', 'w_ffn_in', 'w_ffn_out']
TWIN_WEIGHTS = ['c_ctx', 'w_ada', 'b_ada', 'g_pre_mix', 'g_post_mix', 'g_pre_ffn', 'g_post_ffn', 'w_in', 'attn_sink', 'w_gate_fwd', 'b_gate_fwd', 'w_gate_bwd', 'b_gate_bwd', 'g_gla_norm', 'w_out', 'w_ffn_in', 'w_ffn_out']
TWIN_DIFF_INPUT = 'x'
TWIN_INPUTS = ['x', 'c', 'ctx', 'c_ctx', 'w_ada', 'b_ada', 'g_pre_mix', 'g_post_mix', 'g_pre_ffn', 'g_post_ffn', 'w_in', 'attn_sink', 'w_gate_fwd', 'b_gate_fwd', 'w_gate_bwd', 'b_gate_bwd', 'g_gla_norm', 'w_out', 'w_ffn_in', 'w_ffn_out', 'loss_target', 'm_c_ctx', 'm_w_ada', 'm_b_ada', 'm_g_pre_mix', 'm_g_post_mix', 'm_g_pre_ffn', 'm_g_post_ffn', 'm_w_in', 'm_attn_sink', 'm_w_gate_fwd', 'm_b_gate_fwd', 'm_w_gate_bwd', 'm_b_gate_bwd', 'm_g_gla_norm', 'm_w_out', 'm_w_ffn_in', 'm_w_ffn_out', 'v_c_ctx', 'v_w_ada', 'v_b_ada', 'v_g_pre_mix', 'v_g_post_mix', 'v_g_pre_ffn', 'v_g_post_ffn', 'v_w_in', 'v_attn_sink', 'v_w_gate_fwd', 'v_b_gate_fwd', 'v_w_gate_bwd', 'v_b_gate_bwd', 'v_g_gla_norm', 'v_w_out', 'v_w_ffn_in', 'v_w_ffn_out']
TWIN_OUTPUTS = ['loss', 'grad_x', 'grad_c_ctx', 'grad_w_ada', 'grad_b_ada', 'grad_g_pre_mix', 'grad_g_post_mix', 'grad_g_pre_ffn', 'grad_g_post_ffn', 'grad_w_in', 'grad_attn_sink', 'grad_w_gate_fwd', 'grad_b_gate_fwd', 'grad_w_gate_bwd', 'grad_b_gate_bwd', 'grad_g_gla_norm', 'grad_w_out', 'grad_w_ffn_in', 'grad_w_ffn_out', 'delta_c_ctx', 'delta_w_ada', 'delta_b_ada', 'delta_g_pre_mix', 'delta_g_post_mix', 'delta_g_pre_ffn', 'delta_g_post_ffn', 'delta_w_in', 'delta_attn_sink', 'delta_w_gate_fwd', 'delta_b_gate_fwd', 'delta_w_gate_bwd', 'delta_b_gate_bwd', 'delta_g_gla_norm', 'delta_w_out', 'delta_w_ffn_in', 'delta_w_ffn_out', 'new_m_c_ctx', 'new_m_w_ada', 'new_m_b_ada', 'new_m_g_pre_mix', 'new_m_g_post_mix', 'new_m_g_pre_ffn', 'new_m_g_post_ffn', 'new_m_w_in', 'new_m_attn_sink', 'new_m_w_gate_fwd', 'new_m_b_gate_fwd', 'new_m_w_gate_bwd', 'new_m_b_gate_bwd', 'new_m_g_gla_norm', 'new_m_w_out', 'new_m_w_ffn_in', 'new_m_w_ffn_out', 'new_v_c_ctx', 'new_v_w_ada', 'new_v_b_ada', 'new_v_g_pre_mix', 'new_v_g_post_mix', 'new_v_g_pre_ffn', 'new_v_g_post_ffn', 'new_v_w_in', 'new_v_attn_sink', 'new_v_w_gate_fwd', 'new_v_b_gate_fwd', 'new_v_w_gate_bwd', 'new_v_b_gate_bwd', 'new_v_g_gla_norm', 'new_v_w_out', 'new_v_w_ffn_in', 'new_v_w_ffn_out']
TWIN_LEAF_KINDS = {'loss': 'loss', 'grad_x': 'grad_x', 'grad_c_ctx': 'grad_w', 'grad_w_ada': 'grad_w', 'grad_b_ada': 'grad_w', 'grad_g_pre_mix': 'grad_w', 'grad_g_post_mix': 'grad_w', 'grad_g_pre_ffn': 'grad_w', 'grad_g_post_ffn': 'grad_w', 'grad_w_in': 'grad_w', 'grad_attn_sink': 'grad_w', 'grad_w_gate_fwd': 'grad_w', 'grad_b_gate_fwd': 'grad_w', 'grad_w_gate_bwd': 'grad_w', 'grad_b_gate_bwd': 'grad_w', 'grad_g_gla_norm': 'grad_w', 'grad_w_out': 'grad_w', 'grad_w_ffn_in': 'grad_w', 'grad_w_ffn_out': 'grad_w', 'delta_c_ctx': 'delta_w', 'delta_w_ada': 'delta_w', 'delta_b_ada': 'delta_w', 'delta_g_pre_mix': 'delta_w', 'delta_g_post_mix': 'delta_w', 'delta_g_pre_ffn': 'delta_w', 'delta_g_post_ffn': 'delta_w', 'delta_w_in': 'delta_w', 'delta_attn_sink': 'delta_w', 'delta_w_gate_fwd': 'delta_w', 'delta_b_gate_fwd': 'delta_w', 'delta_w_gate_bwd': 'delta_w', 'delta_b_gate_bwd': 'delta_w', 'delta_g_gla_norm': 'delta_w', 'delta_w_out': 'delta_w', 'delta_w_ffn_in': 'delta_w', 'delta_w_ffn_out': 'delta_w', 'new_m_c_ctx': 'new_m', 'new_m_w_ada': 'new_m', 'new_m_b_ada': 'new_m', 'new_m_g_pre_mix': 'new_m', 'new_m_g_post_mix': 'new_m', 'new_m_g_pre_ffn': 'new_m', 'new_m_g_post_ffn': 'new_m', 'new_m_w_in': 'new_m', 'new_m_attn_sink': 'new_m', 'new_m_w_gate_fwd': 'new_m', 'new_m_b_gate_fwd': 'new_m', 'new_m_w_gate_bwd': 'new_m', 'new_m_b_gate_bwd': 'new_m', 'new_m_g_gla_norm': 'new_m', 'new_m_w_out': 'new_m', 'new_m_w_ffn_in': 'new_m', 'new_m_w_ffn_out': 'new_m', 'new_v_c_ctx': 'new_v', 'new_v_w_ada': 'new_v', 'new_v_b_ada': 'new_v', 'new_v_g_pre_mix': 'new_v', 'new_v_g_post_mix': 'new_v', 'new_v_g_pre_ffn': 'new_v', 'new_v_g_post_ffn': 'new_v', 'new_v_w_in': 'new_v', 'new_v_attn_sink': 'new_v', 'new_v_w_gate_fwd': 'new_v', 'new_v_b_gate_fwd': 'new_v', 'new_v_w_gate_bwd': 'new_v', 'new_v_b_gate_bwd': 'new_v', 'new_v_g_gla_norm': 'new_v', 'new_v_w_out': 'new_v', 'new_v_w_ffn_in': 'new_v', 'new_v_w_ffn_out': 'new_v'}


def _forward(args):
    return _fwd_reference(*[args[k] for k in FWD_PARAMS])


def _output_shape():
    def fwd():
        inp = _fwd_setup_inputs(0)
        return _fwd_reference(*[inp[k] for k in FWD_PARAMS])
    out = _jax.eval_shape(fwd)
    return out.shape, out.dtype

N_MICROBATCH = 1
ADAM_LR = 0.001
ADAM_B1 = 0.9
ADAM_B2 = 0.999
ADAM_EPS = 1e-08
ADAM_WD = 0.01
ADAM_STEP = 10
PER_EXAMPLE_BATCH_AXIS = {'x': 0, 'c': 0, 'ctx': 0, 'loss_target': 0}
SHARED_INPUTS = []
_WEIGHT_DTYPES = {'c_ctx': _jnp.float32, 'w_ada': _jnp.float32, 'b_ada': _jnp.float32, 'g_pre_mix': _jnp.float32, 'g_post_mix': _jnp.float32, 'g_pre_ffn': _jnp.float32, 'g_post_ffn': _jnp.float32, 'w_in': _jnp.float32, 'attn_sink': _jnp.float32, 'w_gate_fwd': _jnp.float32, 'b_gate_fwd': _jnp.float32, 'w_gate_bwd': _jnp.float32, 'b_gate_bwd': _jnp.float32, 'g_gla_norm': _jnp.float32, 'w_out': _jnp.float32, 'w_ffn_in': _jnp.float32, 'w_ffn_out': _jnp.float32}
MOMENT_SCALE = {'c_ctx': 1.833829e-01, 'w_ada': 2.431568e+00, 'b_ada': 5.318138e+00, 'g_pre_mix': 2.602808e-01, 'g_post_mix': 6.662049e+00, 'g_pre_ffn': 1.700006e-01, 'g_post_ffn': 6.714643e+00, 'w_in': 2.214845e-01, 'attn_sink': 3.435154e-03, 'w_gate_fwd': 3.289351e-02, 'b_gate_fwd': 9.521473e-02, 'w_gate_bwd': 3.445068e-02, 'b_gate_bwd': 8.172382e-02, 'g_gla_norm': 4.838121e-01, 'w_out': 2.633774e-01, 'w_ffn_in': 8.498164e-02, 'w_ffn_out': 1.552056e-01}


def _to_microbatches(a, axis):
    t = _jnp.moveaxis(a, axis, 0)
    t = t.reshape((N_MICROBATCH, t.shape[0] // N_MICROBATCH) + t.shape[1:])
    return _jnp.moveaxis(t, 1, axis + 1)


def setup_inputs(seed: int = 0) -> dict:
    inp = _fwd_setup_inputs(seed)
    key = _jax.random.fold_in(_jax.random.key(seed), 7919)
    shape, _ = _output_shape()
    out = dict(inp)
    out["loss_target"] = _jax.random.normal(_jax.random.fold_in(key, 0), shape, _jnp.float32)
    for i, name in enumerate(TWIN_WEIGHTS):
        w = inp[name].astype(_jnp.float32)
        if MOMENT_SCALE is None:
            s = _jnp.sqrt(_jnp.mean(_jnp.square(w)) + 1e-30)
        else:
            s = MOMENT_SCALE[name]
        km, kv = _jax.random.split(_jax.random.fold_in(key, i + 1))
        out[name] = w
        out["m_" + name] = s * _jax.random.normal(km, w.shape, _jnp.float32)
        out["v_" + name] = (s * s) * _jax.random.uniform(kv, w.shape, _jnp.float32, 0.5, 1.5)
    if N_MICROBATCH > 1:
        for name, axis in PER_EXAMPLE_BATCH_AXIS.items():
            out[name] = _to_microbatches(out[name], axis)
    return {'x': out['x'], 'c': out['c'], 'ctx': out['ctx'], 'c_ctx': out['c_ctx'], 'w_ada': out['w_ada'], 'b_ada': out['b_ada'], 'g_pre_mix': out['g_pre_mix'], 'g_post_mix': out['g_post_mix'], 'g_pre_ffn': out['g_pre_ffn'], 'g_post_ffn': out['g_post_ffn'], 'w_in': out['w_in'], 'attn_sink': out['attn_sink'], 'w_gate_fwd': out['w_gate_fwd'], 'b_gate_fwd': out['b_gate_fwd'], 'w_gate_bwd': out['w_gate_bwd'], 'b_gate_bwd': out['b_gate_bwd'], 'g_gla_norm': out['g_gla_norm'], 'w_out': out['w_out'], 'w_ffn_in': out['w_ffn_in'], 'w_ffn_out': out['w_ffn_out'], 'loss_target': out['loss_target'], 'm_c_ctx': out['m_c_ctx'], 'm_w_ada': out['m_w_ada'], 'm_b_ada': out['m_b_ada'], 'm_g_pre_mix': out['m_g_pre_mix'], 'm_g_post_mix': out['m_g_post_mix'], 'm_g_pre_ffn': out['m_g_pre_ffn'], 'm_g_post_ffn': out['m_g_post_ffn'], 'm_w_in': out['m_w_in'], 'm_attn_sink': out['m_attn_sink'], 'm_w_gate_fwd': out['m_w_gate_fwd'], 'm_b_gate_fwd': out['m_b_gate_fwd'], 'm_w_gate_bwd': out['m_w_gate_bwd'], 'm_b_gate_bwd': out['m_b_gate_bwd'], 'm_g_gla_norm': out['m_g_gla_norm'], 'm_w_out': out['m_w_out'], 'm_w_ffn_in': out['m_w_ffn_in'], 'm_w_ffn_out': out['m_w_ffn_out'], 'v_c_ctx': out['v_c_ctx'], 'v_w_ada': out['v_w_ada'], 'v_b_ada': out['v_b_ada'], 'v_g_pre_mix': out['v_g_pre_mix'], 'v_g_post_mix': out['v_g_post_mix'], 'v_g_pre_ffn': out['v_g_pre_ffn'], 'v_g_post_ffn': out['v_g_post_ffn'], 'v_w_in': out['v_w_in'], 'v_attn_sink': out['v_attn_sink'], 'v_w_gate_fwd': out['v_w_gate_fwd'], 'v_b_gate_fwd': out['v_b_gate_fwd'], 'v_w_gate_bwd': out['v_w_gate_bwd'], 'v_b_gate_bwd': out['v_b_gate_bwd'], 'v_g_gla_norm': out['v_g_gla_norm'], 'v_w_out': out['v_w_out'], 'v_w_ffn_in': out['v_w_ffn_in'], 'v_w_ffn_out': out['v_w_ffn_out']}


def _loss(weights, diff, rest, loss_target):
    with _jax.named_scope("forward"):
        args = {**rest, TWIN_DIFF_INPUT: diff, **{k: w.astype(_WEIGHT_DTYPES[k]) for k, w in weights.items()}}
        y = _forward(args)
    with _jax.named_scope("loss_head"):
        err = _jnp.square(y.astype(_jnp.float32) - loss_target)
        return 0.5 * _jnp.sum(_jnp.mean(err, axis=-1)) if err.ndim else 0.5 * err


def _adamw(w, g, m, v):
    m = ADAM_B1 * m + (1.0 - ADAM_B1) * g
    v = ADAM_B2 * v + (1.0 - ADAM_B2) * _jnp.square(g)
    m_hat = m / (1.0 - ADAM_B1 ** ADAM_STEP)
    v_hat = v / (1.0 - ADAM_B2 ** ADAM_STEP)
    delta = -ADAM_LR * (m_hat / (_jnp.sqrt(v_hat) + ADAM_EPS) + ADAM_WD * w)
    return delta, m, v


def reference(x, c, ctx, c_ctx, w_ada, b_ada, g_pre_mix, g_post_mix, g_pre_ffn, g_post_ffn, w_in, attn_sink, w_gate_fwd, b_gate_fwd, w_gate_bwd, b_gate_bwd, g_gla_norm, w_out, w_ffn_in, w_ffn_out, loss_target, m_c_ctx, m_w_ada, m_b_ada, m_g_pre_mix, m_g_post_mix, m_g_pre_ffn, m_g_post_ffn, m_w_in, m_attn_sink, m_w_gate_fwd, m_b_gate_fwd, m_w_gate_bwd, m_b_gate_bwd, m_g_gla_norm, m_w_out, m_w_ffn_in, m_w_ffn_out, v_c_ctx, v_w_ada, v_b_ada, v_g_pre_mix, v_g_post_mix, v_g_pre_ffn, v_g_post_ffn, v_w_in, v_attn_sink, v_w_gate_fwd, v_b_gate_fwd, v_w_gate_bwd, v_b_gate_bwd, v_g_gla_norm, v_w_out, v_w_ffn_in, v_w_ffn_out):
    given = dict(x=x, c=c, ctx=ctx, c_ctx=c_ctx, w_ada=w_ada, b_ada=b_ada, g_pre_mix=g_pre_mix, g_post_mix=g_post_mix, g_pre_ffn=g_pre_ffn, g_post_ffn=g_post_ffn, w_in=w_in, attn_sink=attn_sink, w_gate_fwd=w_gate_fwd, b_gate_fwd=b_gate_fwd, w_gate_bwd=w_gate_bwd, b_gate_bwd=b_gate_bwd, g_gla_norm=g_gla_norm, w_out=w_out, w_ffn_in=w_ffn_in, w_ffn_out=w_ffn_out, loss_target=loss_target, m_c_ctx=m_c_ctx, m_w_ada=m_w_ada, m_b_ada=m_b_ada, m_g_pre_mix=m_g_pre_mix, m_g_post_mix=m_g_post_mix, m_g_pre_ffn=m_g_pre_ffn, m_g_post_ffn=m_g_post_ffn, m_w_in=m_w_in, m_attn_sink=m_attn_sink, m_w_gate_fwd=m_w_gate_fwd, m_b_gate_fwd=m_b_gate_fwd, m_w_gate_bwd=m_w_gate_bwd, m_b_gate_bwd=m_b_gate_bwd, m_g_gla_norm=m_g_gla_norm, m_w_out=m_w_out, m_w_ffn_in=m_w_ffn_in, m_w_ffn_out=m_w_ffn_out, v_c_ctx=v_c_ctx, v_w_ada=v_w_ada, v_b_ada=v_b_ada, v_g_pre_mix=v_g_pre_mix, v_g_post_mix=v_g_post_mix, v_g_pre_ffn=v_g_pre_ffn, v_g_post_ffn=v_g_post_ffn, v_w_in=v_w_in, v_attn_sink=v_attn_sink, v_w_gate_fwd=v_w_gate_fwd, v_b_gate_fwd=v_b_gate_fwd, v_w_gate_bwd=v_w_gate_bwd, v_b_gate_bwd=v_b_gate_bwd, v_g_gla_norm=v_g_gla_norm, v_w_out=v_w_out, v_w_ffn_in=v_w_ffn_in, v_w_ffn_out=v_w_ffn_out)
    weights = {n: given[n] for n in TWIN_WEIGHTS}
    shared = {n: given[n] for n in SHARED_INPUTS}
    per_example = {n: given[n] for n in ['x', 'c', 'ctx']}
    grad_fn = _jax.value_and_grad(_loss, argnums=(0, 1))

    def one_microbatch(ex, loss_target):
        ex = dict(ex)
        diff = ex.pop(TWIN_DIFF_INPUT)
        return grad_fn(weights, diff, {**shared, **ex}, loss_target)

    if N_MICROBATCH == 1:
        loss, (grad_w, grad_x) = one_microbatch(per_example, given["loss_target"])
    else:
        def body(carry, xs):
            loss_sum, grad_sum = carry
            l_k, (gw_k, gx_k) = one_microbatch(xs[0], xs[1])
            with _jax.named_scope("update"):
                return (loss_sum + l_k, _jax.tree.map(_jnp.add, grad_sum, gw_k)), gx_k

        init = (_jnp.zeros((), _jnp.float32), _jax.tree.map(_jnp.zeros_like, weights))
        (loss, grad_w), grad_x = _jax.lax.scan(body, init, (per_example, given["loss_target"]))
    with _jax.named_scope("update"):
        delta_w, new_m, new_v = {}, {}, {}
        for n in TWIN_WEIGHTS:
            delta_w[n], new_m[n], new_v[n] = _adamw(weights[n], grad_w[n], given["m_" + n], given["v_" + n])
    return (loss, grad_x, *[grad_w[n] for n in TWIN_WEIGHTS], *[delta_w[n] for n in TWIN_WEIGHTS],
            *[new_m[n] for n in TWIN_WEIGHTS], *[new_v[n] for n in TWIN_WEIGHTS])
```

```python
import functools

import jax
import jax.numpy as jnp
import numpy as np
from jax import lax
from jax.experimental import pallas as pl
from jax.experimental.pallas import tpu as pltpu

F32 = jnp.float32
BF16 = jnp.bfloat16
MESH = pl.DeviceIdType.MESH

HEAD_DIM = 64
ATT_HEADS = 8
ATT_KV_HEADS = 2
ATT_GROUP = ATT_HEADS // ATT_KV_HEADS
WINDOW = 128
BLOCK = 128
GRID_W = 64
ROPE_BASE = 10000.0
GLA_HEADS = 8
GLA_DK = 32
GLA_DV = 64
GLA_CHUNK = 64
GATE_RANK = 16
GATE_TAU = 16.0
NEG_INF = -1e30
QW = ATT_HEADS * HEAD_DIM
KVW = ATT_KV_HEADS * HEAD_DIM
GKW = GLA_HEADS * GLA_DK
GVW = GLA_HEADS * GLA_DV
IN_COLS = QW + 2 * KVW + 2 * GKW + 2 * GVW + 2 * GATE_RANK
LANES = 128
IN_PAD = IN_COLS + LANES - 2 * GATE_RANK
C_Q, C_GV, C_GG = 0, QW, QW + GVW
C_K = C_GG + GVW
C_V = C_K + KVW
C_GQ = C_V + KVW
C_GK = C_GQ + GKW
C_Z = C_GK + GKW
MIX = QW + GVW

ADAM_LR, ADAM_B1, ADAM_B2, ADAM_EPS, ADAM_WD, ADAM_STEP = 0.001, 0.9, 0.999, 1e-08, 0.01, 10

VMEM_LIMIT = 56 * 1024 * 1024


def _cp(*sem):
    return pltpu.CompilerParams(dimension_semantics=sem, vmem_limit_bytes=VMEM_LIMIT)


def _pick(n, cands):
    for t in cands:
        if n % t == 0:
            return t
    return n


_DIMS = {"nn": (((1,), (0,)), ((), ())), "nt": (((1,), (1,)), ((), ())), "tn": (((0,), (0,)), ((), ()))}


def _raw_dot(mode, a, b, hi):
    if hi:
        return lax.dot_general(a.astype(F32), b.astype(F32), _DIMS[mode], precision=lax.Precision.HIGHEST,
                               preferred_element_type=F32)
    return lax.dot_general(a.astype(BF16), b.astype(BF16), _DIMS[mode], preferred_element_type=F32)


def _make_dot(mode, hi):
    @jax.custom_vjp
    def dot(a, b):
        return _raw_dot(mode, a, b, hi)

    def fwd(a, b):
        return _raw_dot(mode, a, b, hi), (a, b)

    def bwd(res, dc):
        a, b = res
        if mode == "nn":
            return _raw_dot("nt", dc, b, hi), _raw_dot("tn", a, dc, hi)
        if mode == "nt":
            return _raw_dot("nn", dc, b, hi), _raw_dot("tn", dc, a, hi)
        return _raw_dot("nt", b, dc, hi), _raw_dot("nn", a, dc, hi)

    dot.defvjp(fwd, bwd)
    return dot


_nn, _nt, _tn = _make_dot("nn", False), _make_dot("nt", False), _make_dot("tn", False)
_nn_hi = _make_dot("nn", True)


def _mm(name, a, b, mode, out_dtype=F32):
    if mode == "nn":
        (m, k), n = a.shape, b.shape[1]
    elif mode == "nt":
        (m, k), n = a.shape, b.shape[0]
    else:
        (k, m), n = a.shape, b.shape[1]
    tm = _pick(m, (512, 256, 128))
    tn = n if n <= 3072 else _pick(n, (1024, 512, 256, 128))
    tk = k if k <= 3072 else _pick(k, (1024, 512, 256, 128))
    if mode == "tn":
        tk = _pick(k, (512, 256, 128))
    nk = k // tk

    def body(a_ref, b_ref, o_ref, acc_ref):
        kk = pl.program_id(2)

        @pl.when(kk == 0)
        def _():
            acc_ref[...] = jnp.zeros_like(acc_ref)

        acc_ref[...] += _raw_dot(mode, a_ref[...], b_ref[...], False)

        @pl.when(kk == nk - 1)
        def _():
            o_ref[...] = acc_ref[...].astype(o_ref.dtype)

    if mode == "nn":
        a_spec = pl.BlockSpec((tm, tk), lambda i, j, kk: (i, kk))
        b_spec = pl.BlockSpec((tk, tn), lambda i, j, kk: (kk, j))
    elif mode == "nt":
        a_spec = pl.BlockSpec((tm, tk), lambda i, j, kk: (i, kk))
        b_spec = pl.BlockSpec((tn, tk), lambda i, j, kk: (j, kk))
    else:
        a_spec = pl.BlockSpec((tk, tm), lambda i, j, kk: (kk, i))
        b_spec = pl.BlockSpec((tk, tn), lambda i, j, kk: (kk, j))
    return pl.pallas_call(
        body, name=name, grid=(m // tm, n // tn, nk),
        in_specs=[a_spec, b_spec], out_specs=pl.BlockSpec((tm, tn), lambda i, j, kk: (i, j)),
        out_shape=jax.ShapeDtypeStruct((m, n), out_dtype),
        scratch_shapes=[pltpu.VMEM((tm, tn), F32)],
        compiler_params=_cp("parallel", "parallel", "arbitrary"),
    )(a, b)


def _rowwise(name, fn, rows, row_ins, full_ins, row_outs, acc_outs, tm=None):
    tm = tm or _pick(rows, (512, 256, 128))
    n_r, n_f, n_o, n_a = len(row_ins), len(full_ins), len(row_outs), len(acc_outs)

    def body(*refs):
        ins, outs = refs[:n_r + n_f], refs[n_r + n_f:]
        vals = [r[...].astype(F32) for r in ins]
        ro, ao = fn(*vals)
        for r, val in zip(outs[:n_o], ro):
            r[...] = val.astype(r.dtype)
        if n_a:
            @pl.when(pl.program_id(0) == 0)
            def _():
                for r in outs[n_o:]:
                    r[...] = jnp.zeros_like(r)

            for r, val in zip(outs[n_o:], ao):
                r[...] += val

    in_specs = [pl.BlockSpec((tm, w), functools.partial(lambda i, cb: (i, cb), cb=cb)) for _, w, cb in row_ins]
    in_specs += [pl.BlockSpec(a.shape, lambda i: (0, 0)) for a in full_ins]
    out_specs = [pl.BlockSpec((tm, w), lambda i: (i, 0)) for w, _ in row_outs]
    out_specs += [pl.BlockSpec(s, lambda i: (0, 0)) for s in acc_outs]
    out_shape = [jax.ShapeDtypeStruct((rows, w), dt) for w, dt in row_outs]
    out_shape += [jax.ShapeDtypeStruct(s, F32) for s in acc_outs]
    return pl.pallas_call(
        body, name=name, grid=(rows // tm,), in_specs=in_specs, out_specs=out_specs, out_shape=out_shape,
        compiler_params=_cp("arbitrary" if n_a else "parallel"),
    )(*[a for a, _, _ in row_ins], *full_ins)


def _rn(x):
    return x * lax.rsqrt(jnp.mean(x * x, axis=-1, keepdims=True) + 1e-6)


def _sigmoid(t):
    return 1.0 / (1.0 + jnp.exp(-t))


def _f_norm_mod(x, g, sh, sc):
    return _rn(x) * g * (1.0 + sc) + sh


def _f_post_res(xr, y, g, gate):
    return xr + gate * (_rn(y) * g)


def _f_swiglu(g, u):
    return g * _sigmoid(g) * u


def _logsig(u):
    return jnp.minimum(u, 0.0) - jnp.log(1.0 + jnp.exp(-jnp.abs(u)))


def _f_gate(z, wf, wb, bf, bb):
    return _logsig(_nn(z, wf) + bf) / GATE_TAU, _logsig(_nn(z, wb) + bb) / GATE_TAU


def _f_gla_out(of, ob, gg, gt, bd):
    o = of + ob
    ms = _nn_hi(o * o, bd)
    return o * lax.rsqrt(ms + 1e-6) * gt * (gg * _sigmoid(gg))


def _norm_mod(name, x, g, sh, sc):
    rows, d = x.shape
    return _rowwise(name, lambda x, g, sh, sc: ((_f_norm_mod(x, g, sh, sc),), ()), rows,
                    [(x, d, 0)], [g, sh, sc], [(d, BF16)], [])[0]


def _norm_mod_bwd(name, dh, dres, x, g, sh, sc):
    rows, d = x.shape

    def fn(dh, dres, x, g, sh, sc):
        _, vjp = jax.vjp(_f_norm_mod, x, g, sh, sc)
        dx, dg, dsh, dsc = vjp(dh)
        return (dx + dres,), (dg, dsh, dsc)

    return _rowwise(name, fn, rows, [(dh, d, 0), (dres, d, 0), (x, d, 0)], [g, sh, sc], [(d, F32)],
                    [(1, d)] * 3)


def _post_res(name, xr, y, g, gate):
    rows, d = xr.shape
    return _rowwise(name, lambda xr, y, g, gate: ((_f_post_res(xr, y, g, gate),), ()), rows,
                    [(xr, d, 0), (y, d, 0)], [g, gate], [(d, F32)], [])[0]


def _post_res_bwd(name, dxo, y, g, gate):
    rows, d = y.shape

    def fn(dxo, y, g, gate):
        _, vjp = jax.vjp(lambda y, g, gate: _f_post_res(jnp.zeros_like(y), y, g, gate), y, g, gate)
        dy, dg, dgate = vjp(dxo)
        return (dy,), (dg, dgate)

    return _rowwise(name, fn, rows, [(dxo, d, 0), (y, d, 0)], [g, gate], [(d, BF16)], [(1, d)] * 2)


def _post_res_loss(name, xr, y, g, gate, target):
    rows, d = xr.shape

    def fn(xr, y, target, g, gate):
        diff = _f_post_res(xr, y, g, gate) - target
        part = 0.5 * jnp.sum(jnp.mean(diff * diff, axis=-1, keepdims=True), axis=0, keepdims=True)
        return (diff * (1.0 / d),), (jnp.broadcast_to(part, (1, LANES)),)

    return _rowwise(name, fn, rows, [(xr, d, 0), (y, d, 0), (target, d, 0)], [g, gate], [(d, F32)], [(1, LANES)])


def _swiglu(name, u):
    rows, f2 = u.shape
    f = f2 // 2
    return _rowwise(name, lambda g, u: ((_f_swiglu(g, u),), ()), rows, [(u, f, 0), (u, f, 1)], [], [(f, BF16)], [],
                    tm=_pick(rows, (256, 128)))[0]


def _swiglu_bwd(name, da, u):
    rows, f2 = u.shape
    f = f2 // 2

    def fn(da, g, u):
        _, vjp = jax.vjp(_f_swiglu, g, u)
        return vjp(da), ()

    dg, du = _rowwise(name, fn, rows, [(da, f, 0), (u, f, 0), (u, f, 1)], [], [(f, BF16), (f, BF16)], [],
                      tm=_pick(rows, (256, 128)))
    return jnp.concatenate([dg, du], axis=1)


def _gate_fwd(name, p, wf, wb, bf, bb):
    rows = p.shape[0]
    return _rowwise(name, lambda z, wf, wb, bf, bb: (_f_gate(z, wf, wb, bf, bb), ()), rows,
                    [(p, LANES, C_Z // LANES)], [wf, wb, bf, bb], [(GKW, F32)] * 2, [])


def _gate_bwd(name, p, dla_f, dla_b, wf, wb, bf, bb):
    rows = p.shape[0]

    def fn(z, dlf, dlb, wf, wb, bf, bb):
        _, vjp = jax.vjp(_f_gate, z, wf, wb, bf, bb)
        dz, dwf, dwb, dbf, dbb = vjp((dlf, dlb))
        return (dz,), (dwf, dwb, dbf, dbb)

    return _rowwise(name, fn, rows, [(p, LANES, C_Z // LANES), (dla_f, GKW, 0), (dla_b, GKW, 0)],
                    [wf, wb, bf, bb], [(LANES, BF16)], [(LANES, GKW), (LANES, GKW), (1, GKW), (1, GKW)])


def _head_mean_matrix():
    h = np.arange(GVW) // GLA_DV
    return jnp.asarray((h[:, None] == h[None, :]).astype(np.float32) / GLA_DV)


def _gla_out(name, of, ob, p, gt):
    rows = of.shape[0]
    bd = _head_mean_matrix()
    return _rowwise(name, lambda of, ob, gg, gt, bd: ((_f_gla_out(of, ob, gg, gt, bd),), ()), rows,
                    [(of, GVW, 0), (ob, GVW, 0), (p, GVW, C_GG // GVW)], [gt, bd], [(GVW, BF16)], [])[0]


def _gla_out_bwd(name, dmix, of, ob, p, gt):
    rows = of.shape[0]
    bd = _head_mean_matrix()

    def fn(dm, of, ob, gg, gt, bd):
        _, vjp = jax.vjp(lambda of, gg, gt: _f_gla_out(of, ob, gg, gt, bd), of, gg, gt)
        do, dgg, dgt = vjp(dm)
        return (do, dgg), (dgt,)

    return _rowwise(name, fn, rows, [(dmix, GVW, 1), (of, GVW, 0), (ob, GVW, 0), (p, GVW, C_GG // GVW)], [gt, bd],
                    [(GVW, F32), (GVW, BF16)], [(1, GVW)])


def _rope_tables(n_tokens):
    t = jnp.arange(n_tokens)
    row = (t // GRID_W).astype(F32)
    col = (t % GRID_W).astype(F32)
    half = HEAD_DIM // 2
    inv_freq = ROPE_BASE ** (-jnp.arange(0, half, 2, dtype=F32) / half)
    ang_r = row[:, None] * inv_freq[None, :]
    ang_c = col[:, None] * inv_freq[None, :]
    ang = jnp.concatenate([ang_r, ang_r, ang_c, ang_c], axis=-1)
    sign = jnp.concatenate([-jnp.ones((16,), F32), jnp.ones((16,), F32)] * 2)
    cos, sin = jnp.cos(ang), jnp.sin(ang) * sign[None, :]
    return jnp.tile(cos, (1, 2)), jnp.tile(sin, (1, 2))


def _rot_pairs(x):
    w = x.shape[-1]
    lane = lax.broadcasted_iota(jnp.int32, x.shape, x.ndim - 1)
    return jnp.where((lane % 32) < 16, pltpu.roll(x, w - 16, x.ndim - 1), pltpu.roll(x, 16, x.ndim - 1))


def _rope_apply(x, cos, sin_signed, inverse):
    reps = x.shape[-1] // LANES
    cos = jnp.concatenate([cos] * reps, axis=-1) if reps > 1 else cos
    sin = jnp.concatenate([sin_signed] * reps, axis=-1) if reps > 1 else sin_signed
    if inverse:
        return x * cos + _rot_pairs(x * sin)
    return x * cos + _rot_pairs(x) * sin


def _rope_fwd(name, p, cos, sin):
    rows = p.shape[0]

    def fn(q, k, v, cos, sin):
        return (_rope_apply(q, cos, sin, False), _rope_apply(k, cos, sin, False), v), ()

    return _rowwise(name, fn, rows, [(p, QW, 0), (p, KVW, C_K // KVW), (p, KVW, C_V // KVW), (cos, LANES, 0),
                                     (sin, LANES, 0)], [], [(QW, BF16), (KVW, BF16), (KVW, BF16)], [])


def _rope_bwd(name, dq, dk, cos, sin):
    rows = dq.shape[0]

    def fn(dq, dk, cos, sin):
        return (_rope_apply(dq, cos, sin, True), _rope_apply(dk, cos, sin, True)), ()

    return _rowwise(name, fn, rows, [(dq, QW, 0), (dk, KVW, 0), (cos, LANES, 0), (sin, LANES, 0)], [],
                    [(QW, BF16), (KVW, BF16)], [])


def _f_attn(qs, kws, vws, kcs, vcs, sink, n, n_tokens):
    i = lax.broadcasted_iota(jnp.int32, (BLOCK, 3 * BLOCK), 0)
    j = lax.broadcasted_iota(jnp.int32, (BLOCK, 3 * BLOCK), 1)
    kpos = (n - 1) * BLOCK + j
    mask = (jnp.abs(j - BLOCK - i) <= WINDOW) & (kpos >= 0) & (kpos < n_tokens)
    head_id = lax.broadcasted_iota(jnp.int32, (1, ATT_HEADS), 1)
    scale = HEAD_DIM ** -0.5
    outs = []
    for hq in range(ATT_HEADS):
        h = hq // ATT_GROUP
        s_w = jnp.where(mask, _nt(qs[hq], kws[h]) * scale, NEG_INF)
        s_c = _nt(qs[hq], kcs[h]) * scale
        sk = jnp.sum(jnp.where(head_id == hq, sink, 0.0), axis=-1, keepdims=True)
        m = lax.stop_gradient(jnp.maximum(jnp.maximum(jnp.max(s_w, axis=-1, keepdims=True),
                                                      jnp.max(s_c, axis=-1, keepdims=True)), sk))
        pw, pc = jnp.exp(s_w - m), jnp.exp(s_c - m)
        den = jnp.sum(pw, axis=-1, keepdims=True) + jnp.sum(pc, axis=-1, keepdims=True) + jnp.exp(sk - m)
        outs.append((_nn(pw, vws[h]) + _nn(pc, vcs[h])) / den)
    return tuple(outs)


def _attn_loads(n, q_ref, kp_ref, vp_ref, kc_ref, vc_ref):
    r0 = pl.multiple_of(n * BLOCK, BLOCK)
    hs = lambda h: slice(h * HEAD_DIM, (h + 1) * HEAD_DIM)
    qs = [q_ref[:, hs(h)].astype(F32) for h in range(ATT_HEADS)]
    kws = [kp_ref[pl.ds(r0, 3 * BLOCK), hs(h)].astype(F32) for h in range(ATT_KV_HEADS)]
    vws = [vp_ref[pl.ds(r0, 3 * BLOCK), hs(h)].astype(F32) for h in range(ATT_KV_HEADS)]
    kcs = [kc_ref[:, hs(h)].astype(F32) for h in range(ATT_KV_HEADS)]
    vcs = [vc_ref[:, hs(h)].astype(F32) for h in range(ATT_KV_HEADS)]
    return r0, hs, qs, kws, vws, kcs, vcs


def _attn_specs(s, c):
    full = lambda shape: pl.BlockSpec(shape, lambda n: (0, 0))
    return [pl.BlockSpec((BLOCK, QW), lambda n: (n, 0)), full((s + 2 * BLOCK, KVW)), full((s + 2 * BLOCK, KVW)),
            full((c, KVW)), full((c, KVW)), full((1, ATT_HEADS))]


def _attn_fwd(q, kp, vp, kc, vc, sink):
    s, c = q.shape[0], kc.shape[0]

    def body(q_ref, kp_ref, vp_ref, kc_ref, vc_ref, sink_ref, o_ref):
        n = pl.program_id(0)
        _, hs, qs, kws, vws, kcs, vcs = _attn_loads(n, q_ref, kp_ref, vp_ref, kc_ref, vc_ref)
        outs = _f_attn(qs, kws, vws, kcs, vcs, sink_ref[...], n, s)
        for h in range(ATT_HEADS):
            o_ref[:, hs(h)] = outs[h].astype(o_ref.dtype)

    return pl.pallas_call(
        body, name="attn_fwd", grid=(s // BLOCK,), in_specs=_attn_specs(s, c),
        out_specs=pl.BlockSpec((BLOCK, QW), lambda n: (n, 0)), out_shape=jax.ShapeDtypeStruct((s, QW), BF16),
        compiler_params=_cp("parallel"),
    )(q, kp, vp, kc, vc, sink)


def _attn_bwd(do, q, kp, vp, kc, vc, sink):
    s, c = q.shape[0], kc.shape[0]

    def body(do_ref, q_ref, kp_ref, vp_ref, kc_ref, vc_ref, sink_ref, dq_ref, dkp_ref, dvp_ref, dkc_ref, dvc_ref,
             dsink_ref):
        n = pl.program_id(0)

        @pl.when(n == 0)
        def _():
            for r in (dkp_ref, dvp_ref, dkc_ref, dvc_ref, dsink_ref):
                r[...] = jnp.zeros_like(r)

        r0, hs, qs, kws, vws, kcs, vcs = _attn_loads(n, q_ref, kp_ref, vp_ref, kc_ref, vc_ref)
        _, vjp = jax.vjp(lambda qs, kws, vws, kcs, vcs, sink: _f_attn(qs, kws, vws, kcs, vcs, sink, n, s),
                         qs, kws, vws, kcs, vcs, sink_ref[...])
        dqs, dkws, dvws, dkcs, dvcs, dsink = vjp(tuple(do_ref[:, hs(h)].astype(F32) for h in range(ATT_HEADS)))
        for h in range(ATT_HEADS):
            dq_ref[:, hs(h)] = dqs[h]
        for h in range(ATT_KV_HEADS):
            dkp_ref[pl.ds(r0, 3 * BLOCK), hs(h)] += dkws[h]
            dvp_ref[pl.ds(r0, 3 * BLOCK), hs(h)] += dvws[h]
            dkc_ref[:, hs(h)] += dkcs[h]
            dvc_ref[:, hs(h)] += dvcs[h]
        dsink_ref[...] += dsink

    full = lambda shape: pl.BlockSpec(shape, lambda n: (0, 0))
    return pl.pallas_call(
        body, name="attn_bwd", grid=(s // BLOCK,),
        in_specs=[pl.BlockSpec((BLOCK, QW), lambda n: (n, 0))] + _attn_specs(s, c),
        out_specs=[pl.BlockSpec((BLOCK, QW), lambda n: (n, 0)), full((s + 2 * BLOCK, KVW)), full((s + 2 * BLOCK, KVW)),
                   full((c, KVW)), full((c, KVW)), full((1, ATT_HEADS))],
        out_shape=[jax.ShapeDtypeStruct((s, QW), F32), jax.ShapeDtypeStruct((s + 2 * BLOCK, KVW), F32),
                   jax.ShapeDtypeStruct((s + 2 * BLOCK, KVW), F32), jax.ShapeDtypeStruct((c, KVW), F32),
                   jax.ShapeDtypeStruct((c, KVW), F32), jax.ShapeDtypeStruct((1, ATT_HEADS), F32)],
        compiler_params=_cp("arbitrary"),
    )(do, q, kp, vp, kc, vc, sink)


def _gla_masks():
    hk = np.arange(GKW) // GLA_DK
    hv = np.arange(GVW) // GLA_DV
    head_k = (np.arange(GLA_HEADS)[:, None] == hk[None, :]).astype(np.float32)
    head_v = (np.arange(GLA_HEADS)[:, None] == hv[None, :]).astype(np.float32)
    bd_t = (hv[:, None] == hk[None, :]).astype(np.float32)
    return jnp.asarray(head_k), jnp.asarray(head_v), jnp.asarray(bd_t)


def _tri(n, rev, strict=False):
    i = lax.broadcasted_iota(jnp.int32, (n, n), 0)
    j = lax.broadcasted_iota(jnp.int32, (n, n), 1)
    if strict:
        keep = (j > i) if rev else (j < i)
    else:
        keep = (j >= i) if rev else (j <= i)
    return keep


def _f_gla_chunk(q, k, v, la, st, head_k, head_v, bd_t, rev):
    keep = _tri(GLA_CHUNK, rev)
    b = _nn_hi(keep.astype(F32), la)
    bl = jnp.sum(la, axis=0, keepdims=True)
    qd = q * (GLA_DK ** -0.5) * jnp.exp(b)
    ki = k * jnp.exp(-b)
    kd = k * jnp.exp(bl - b)
    q_heads = (qd[None, :, :] * head_k[:, None, :]).reshape(GLA_HEADS * GLA_CHUNK, GKW)
    a_all = _nt(q_heads, ki).reshape(GLA_HEADS, GLA_CHUNK, GLA_CHUNK)
    a_all = jnp.where(keep[None, :, :], a_all, 0.0).reshape(GLA_HEADS * GLA_CHUNK, GLA_CHUNK)
    o_all = _nn(a_all, v).reshape(GLA_HEADS, GLA_CHUNK, GVW)
    intra = jnp.sum(o_all * head_v[:, None, :], axis=0)
    inter = _nt(qd, st)
    st_new = st * jnp.exp(bl) + bd_t * _tn(v, kd)
    return intra + inter, st_new


def _gla_specs(s, tb, order):
    return [pl.BlockSpec((tb, GKW), lambda i: (order(i), C_GQ // GKW)),
            pl.BlockSpec((tb, GKW), lambda i: (order(i), C_GK // GKW)),
            pl.BlockSpec((tb, GVW), lambda i: (order(i), C_GV // GVW)),
            pl.BlockSpec((tb, GKW), lambda i: (order(i), 0))]


GLA_BLOCK_CHUNKS = 2


def _gla_fwd(name, p, la, st0, rev):
    s = p.shape[0]
    tb = GLA_BLOCK_CHUNKS * GLA_CHUNK
    nblk = s // tb
    order = (lambda i: nblk - 1 - i) if rev else (lambda i: i)
    masks = _gla_masks()

    def body(q_ref, k_ref, v_ref, la_ref, st0_ref, hk_ref, hv_ref, bd_ref, o_ref, sts_ref, st_ref):
        @pl.when(pl.program_id(0) == 0)
        def _():
            st_ref[...] = st0_ref[...]

        st = st_ref[...]
        sts_ref[0] = st
        chunks = range(GLA_BLOCK_CHUNKS)
        for ci in (reversed(chunks) if rev else chunks):
            rows = slice(ci * GLA_CHUNK, (ci + 1) * GLA_CHUNK)
            o, st = _f_gla_chunk(q_ref[rows, :], k_ref[rows, :], v_ref[rows, :], la_ref[rows, :], st,
                                 hk_ref[...], hv_ref[...], bd_ref[...], rev)
            o_ref[rows, :] = o
        st_ref[...] = st

    full = lambda a: pl.BlockSpec(a.shape, lambda i: (0,) * a.ndim)
    return pl.pallas_call(
        body, name=name, grid=(nblk,),
        in_specs=_gla_specs(s, tb, order) + [full(st0)] + [full(m) for m in masks],
        out_specs=[pl.BlockSpec((tb, GVW), lambda i: (order(i), 0)),
                   pl.BlockSpec((1, GVW, GKW), lambda i: (order(i), 0, 0))],
        out_shape=[jax.ShapeDtypeStruct((s, GVW), F32), jax.ShapeDtypeStruct((nblk, GVW, GKW), F32)],
        scratch_shapes=[pltpu.VMEM((GVW, GKW), F32)],
        compiler_params=_cp("arbitrary"),
    )(p, p, p, la, st0, *masks)


def _gla_bwd(name, p, la, sts, do, prev, rev):
    s = p.shape[0]
    tb = GLA_BLOCK_CHUNKS * GLA_CHUNK
    nblk = s // tb
    order = (lambda i: i) if rev else (lambda i: nblk - 1 - i)
    masks = _gla_masks()
    n_prev = 0 if prev is None else 3

    def body(*refs):
        q_ref, k_ref, v_ref, la_ref, sts_ref, do_ref, hk_ref, hv_ref, bd_ref = refs[:9]
        prev_refs = refs[9:9 + n_prev]
        dq_ref, dk_ref, dv_ref, dla_ref, dst0_ref, dst_ref = refs[9 + n_prev:]

        @pl.when(pl.program_id(0) == 0)
        def _():
            dst_ref[...] = jnp.zeros_like(dst_ref)

        def block(q, k, v, la, st):
            outs = [None] * GLA_BLOCK_CHUNKS
            chunks = range(GLA_BLOCK_CHUNKS)
            for ci in (reversed(chunks) if rev else chunks):
                rows = slice(ci * GLA_CHUNK, (ci + 1) * GLA_CHUNK)
                outs[ci], st = _f_gla_chunk(q[ci], k[ci], v[ci], la[ci], st, hk_ref[...], hv_ref[...], bd_ref[...],
                                            rev)
            return tuple(outs), st

        split = lambda r: tuple(r[ci * GLA_CHUNK:(ci + 1) * GLA_CHUNK, :].astype(F32)
                                for ci in range(GLA_BLOCK_CHUNKS))
        _, vjp = jax.vjp(block, split(q_ref), split(k_ref), split(v_ref), split(la_ref), sts_ref[0])
        dq, dk, dv, dla, dst = vjp((split(do_ref), dst_ref[...]))
        for ci in range(GLA_BLOCK_CHUNKS):
            rows = slice(ci * GLA_CHUNK, (ci + 1) * GLA_CHUNK)
            if n_prev:
                dq_ref[rows, :] = dq[ci] + prev_refs[0][rows, :]
                dk_ref[rows, :] = dk[ci] + prev_refs[1][rows, :]
                dv_ref[rows, :] = dv[ci] + prev_refs[2][rows, :]
            else:
                dq_ref[rows, :], dk_ref[rows, :], dv_ref[rows, :] = dq[ci], dk[ci], dv[ci]
            dla_ref[rows, :] = dla[ci]
        dst_ref[...] = dst
        dst0_ref[...] = dst

    full = lambda a: pl.BlockSpec(a.shape, lambda i: (0,) * a.ndim)
    blk = lambda w: pl.BlockSpec((tb, w), lambda i: (order(i), 0))
    prev_specs = [blk(GKW), blk(GKW), blk(GVW)] if n_prev else []
    return pl.pallas_call(
        body, name=name, grid=(nblk,),
        in_specs=_gla_specs(s, tb, order) + [pl.BlockSpec((1, GVW, GKW), lambda i: (order(i), 0, 0)), blk(GVW)]
        + [full(m) for m in masks] + prev_specs,
        out_specs=[blk(GKW), blk(GKW), blk(GVW), blk(GKW), pl.BlockSpec((GVW, GKW), lambda i: (0, 0))],
        out_shape=[jax.ShapeDtypeStruct((s, GKW), F32), jax.ShapeDtypeStruct((s, GKW), F32),
                   jax.ShapeDtypeStruct((s, GVW), F32), jax.ShapeDtypeStruct((s, GKW), F32),
                   jax.ShapeDtypeStruct((GVW, GKW), F32)],
        scratch_shapes=[pltpu.VMEM((GVW, GKW), F32)],
        compiler_params=_cp("arbitrary"),
    )(p, p, p, la, sts, do, *masks, *(prev or ()))


def _f_ctx_state(k, v, la_f, la_b, bd_t):
    c = k.shape[0]
    after = _nn_hi(_tri(c, True, strict=True).astype(F32), la_f)
    before = _nn_hi(_tri(c, False, strict=True).astype(F32), la_b)
    return bd_t * _tn(v, k * jnp.exp(after)), bd_t * _tn(v, k * jnp.exp(before))


def _ctx_state(pc, la_f, la_b):
    c = pc.shape[0]
    bd_t = _gla_masks()[2]

    def body(k_ref, v_ref, lf_ref, lb_ref, bd_ref, sf_ref, sb_ref):
        sf_ref[...], sb_ref[...] = _f_ctx_state(k_ref[...], v_ref[...], lf_ref[...], lb_ref[...], bd_ref[...])

    full = lambda a: pl.BlockSpec(a.shape, lambda i: (0, 0))
    return pl.pallas_call(
        body, name="ctx_state_fwd", grid=(1,),
        in_specs=[pl.BlockSpec((c, GKW), lambda i: (0, C_GK // GKW)), pl.BlockSpec((c, GVW), lambda i: (0, C_GV // GVW)),
                  full(la_f), full(la_b), full(bd_t)],
        out_specs=[pl.BlockSpec((GVW, GKW), lambda i: (0, 0))] * 2,
        out_shape=[jax.ShapeDtypeStruct((GVW, GKW), F32)] * 2,
        compiler_params=_cp("arbitrary"),
    )(pc, pc, la_f, la_b, bd_t)


def _ctx_state_bwd(pc, la_f, la_b, dsf, dsb):
    c = pc.shape[0]
    bd_t = _gla_masks()[2]

    def body(k_ref, v_ref, lf_ref, lb_ref, bd_ref, dsf_ref, dsb_ref, dk_ref, dv_ref, dlf_ref, dlb_ref):
        _, vjp = jax.vjp(lambda k, v, lf, lb: _f_ctx_state(k, v, lf, lb, bd_ref[...]),
                         k_ref[...], v_ref[...], lf_ref[...], lb_ref[...])
        dk, dv, dlf, dlb = vjp((dsf_ref[...], dsb_ref[...]))
        dk_ref[...], dv_ref[...] = dk.astype(BF16), dv.astype(BF16)
        dlf_ref[...], dlb_ref[...] = dlf, dlb

    full = lambda a: pl.BlockSpec(a.shape, lambda i: (0, 0))
    return pl.pallas_call(
        body, name="ctx_state_bwd", grid=(1,),
        in_specs=[pl.BlockSpec((c, GKW), lambda i: (0, C_GK // GKW)), pl.BlockSpec((c, GVW), lambda i: (0, C_GV // GVW)),
                  full(la_f), full(la_b), full(bd_t), full(dsf), full(dsb)],
        out_specs=[pl.BlockSpec((c, GKW), lambda i: (0, 0)), pl.BlockSpec((c, GVW), lambda i: (0, 0)),
                   pl.BlockSpec((c, GKW), lambda i: (0, 0)), pl.BlockSpec((c, GKW), lambda i: (0, 0))],
        out_shape=[jax.ShapeDtypeStruct((c, GKW), BF16), jax.ShapeDtypeStruct((c, GVW), BF16),
                   jax.ShapeDtypeStruct((c, GKW), F32), jax.ShapeDtypeStruct((c, GKW), F32)],
        compiler_params=_cp("arbitrary"),
    )(pc, pc, la_f, la_b, bd_t, dsf, dsb)


_SRC_COLS = ((0, QW), (QW + 2 * KVW + 2 * GKW, GVW), (QW + 2 * KVW + 2 * GKW + GVW, GVW), (QW, KVW), (QW + KVW, KVW),
             (QW + 2 * KVW, GKW), (QW + 2 * KVW + GKW, GKW), (IN_COLS - 2 * GATE_RANK, 2 * GATE_RANK))
_DST_COLS = (C_Q, C_GV, C_GG, C_K, C_V, C_GQ, C_GK, C_Z)


def _pack_w_in(w_in):
    parts = [w_in[:, s:s + n] for s, n in _SRC_COLS]
    parts.append(jnp.zeros((w_in.shape[0], IN_PAD - C_Z - 2 * GATE_RANK), w_in.dtype))
    return jnp.concatenate(parts, axis=1)


def _unpack_w_in_grad(g):
    by_src = sorted(zip(_SRC_COLS, _DST_COLS))
    return jnp.concatenate([g[:, d:d + n] for (_, n), d in by_src], axis=1)


def _prep_weights(w_in, w_out, w_ffn_in, w_ffn_out, w_gate_fwd, w_gate_bwd):
    pad_rows = lambda w, at: jnp.zeros((LANES, GKW), F32).at[at:at + GATE_RANK].set(w)
    return {"w_in": _pack_w_in(w_in).astype(BF16), "w_out": w_out.astype(BF16), "w_ffn_in": w_ffn_in.astype(BF16),
            "w_ffn_out": w_ffn_out.astype(BF16), "wg_f": pad_rows(w_gate_fwd, 0), "wg_b": pad_rows(w_gate_bwd, GATE_RANK)}


def _local_step(x, ctx, target, ada, ada_c, w):
    s, d = x.shape
    sh1, sc1, gt1, sh2, sc2, gt2 = [ada[:, i * d:(i + 1) * d] for i in range(6)]
    sh1c, sc1c = ada_c[:, :d], ada_c[:, d:2 * d]
    cos, sin = _rope_tables(s)
    gt = jnp.tile(w["g_gla_norm"], (1, GLA_HEADS))

    h = _norm_mod("pre_mix", x, w["g_pre_mix"], sh1, sc1)
    hc = _norm_mod("pre_mix_ctx", ctx, w["g_pre_mix"], sh1c, sc1c)
    p = _mm("proj_in", h, w["w_in"], "nn")
    pc = _mm("proj_in_ctx", hc, w["w_in"], "nn")
    q_rot, k_rot, v_b = _rope_fwd("rope", p, cos, sin)
    pad = ((BLOCK, BLOCK), (0, 0))
    kp, vp = jnp.pad(k_rot, pad), jnp.pad(v_b, pad)
    kc, vc = pc[:, C_K:C_K + KVW].astype(BF16), pc[:, C_V:C_V + KVW].astype(BF16)
    attn = _attn_fwd(q_rot, kp, vp, kc, vc, w["attn_sink"])
    gate_w = (w["wg_f"], w["wg_b"], w["b_gate_fwd"], w["b_gate_bwd"])
    la_f, la_b = _gate_fwd("gate", p, *gate_w)
    la_fc, la_bc = _gate_fwd("gate_ctx", pc, *gate_w)
    st_f0, st_b0 = _ctx_state(pc, la_fc, la_bc)
    o_f, sts_f = _gla_fwd("gla_fwd_f", p, la_f, st_f0, False)
    o_b, sts_b = _gla_fwd("gla_fwd_b", p, la_b, st_b0, True)
    gla = _gla_out("gla_out", o_f, o_b, p, gt)
    mix = jnp.concatenate([attn, gla], axis=1)
    y = _mm("proj_out", mix, w["w_out"], "nn")
    x1 = _post_res("post_mix", x, y, w["g_post_mix"], gt1)
    h2 = _norm_mod("pre_ffn", x1, w["g_pre_ffn"], sh2, sc2)
    u = _mm("ffn_in", h2, w["w_ffn_in"], "nn")
    a = _swiglu("swiglu", u)
    f = _mm("ffn_out", a, w["w_ffn_out"], "nn")
    dx2, loss = _post_res_loss("post_ffn_loss", x1, f, w["g_post_ffn"], gt2, target)

    g = {}
    df, g["g_post_ffn"], dgt2 = _post_res_bwd("post_ffn_bwd", dx2, f, w["g_post_ffn"], gt2)
    da = _mm("ffn_out_dx", df, w["w_ffn_out"], "nt")
    g["w_ffn_out"] = _mm("ffn_out_dw", a, df, "tn")
    du = _swiglu_bwd("swiglu_bwd", da, u)
    dh2 = _mm("ffn_in_dx", du, w["w_ffn_in"], "nt")
    g["w_ffn_in"] = _mm("ffn_in_dw", h2, du, "tn")
    dx1, g["g_pre_ffn"], dsh2, dsc2 = _norm_mod_bwd("pre_ffn_bwd", dh2, dx2, x1, w["g_pre_ffn"], sh2, sc2)
    dy, g["g_post_mix"], dgt1 = _post_res_bwd("post_mix_bwd", dx1, y, w["g_post_mix"], gt1)
    dmix = _mm("proj_out_dx", dy, w["w_out"], "nt", BF16)
    g["w_out"] = _mm("proj_out_dw", mix, dy, "tn")
    d_o, dgg, dgt = _gla_out_bwd("gla_out_bwd", dmix, o_f, o_b, p, gt)
    g["g_gla_norm"] = jnp.sum(dgt.reshape(GLA_HEADS, GLA_DV), axis=0, keepdims=True)
    dgq, dgk, dgv, dla_f, dst_f0 = _gla_bwd("gla_bwd_f", p, la_f, sts_f, d_o, None, False)
    dgq, dgk, dgv, dla_b, dst_b0 = _gla_bwd("gla_bwd_b", p, la_b, sts_b, d_o, (dgq, dgk, dgv), True)
    dgkc, dgvc, dla_fc, dla_bc = _ctx_state_bwd(pc, la_fc, la_bc, dst_f0, dst_b0)
    dz, dwf, dwb, dbf, dbb = _gate_bwd("gate_bwd", p, dla_f, dla_b, *gate_w)
    dzc, dwfc, dwbc, dbfc, dbbc = _gate_bwd("gate_ctx_bwd", pc, dla_fc, dla_bc, *gate_w)
    g["w_gate_fwd"] = (dwf + dwfc)[:GATE_RANK]
    g["w_gate_bwd"] = (dwb + dwbc)[GATE_RANK:2 * GATE_RANK]
    g["b_gate_fwd"], g["b_gate_bwd"] = dbf + dbfc, dbb + dbbc
    dq_rot, dkp, dvp, dkc, dvc, g["attn_sink"] = _attn_bwd(dmix, q_rot, kp, vp, kc, vc, w["attn_sink"])
    dq, dk = _rope_bwd("rope_bwd", dq_rot, dkp[BLOCK:BLOCK + s], cos, sin)
    dp = jnp.concatenate([dq, dgv.astype(BF16), dgg, dk, dvp[BLOCK:BLOCK + s].astype(BF16), dgq.astype(BF16),
                          dgk.astype(BF16), dz], axis=1)
    c_rows = ctx.shape[0]
    zeros = lambda n: jnp.zeros((c_rows, n), BF16)
    dpc = jnp.concatenate([zeros(QW), dgvc, zeros(GVW), dkc.astype(BF16), dvc.astype(BF16), zeros(GKW), dgkc, dzc],
                          axis=1)
    dh = _mm("proj_in_dx", dp, w["w_in"], "nt")
    dhc = _mm("proj_in_ctx_dx", dpc, w["w_in"], "nt")
    g["w_in"] = _mm("proj_in_dw", jnp.concatenate([hc, h], axis=0), jnp.concatenate([dpc, dp], axis=0), "tn")
    dx, dg_a, dsh1, dsc1 = _norm_mod_bwd("pre_mix_bwd", dh, dx1, x, w["g_pre_mix"], sh1, sc1)
    _, dg_b, dsh1c, dsc1c = _norm_mod_bwd("pre_mix_ctx_bwd", dhc, jnp.zeros_like(dhc), ctx, w["g_pre_mix"], sh1c,
                                          sc1c)
    g["g_pre_mix"] = dg_a + dg_b
    d_ada = jnp.concatenate([dsh1, dsc1, dgt1, dsh2, dsc2, dgt2], axis=1)
    d_ada_c = jnp.concatenate([dsh1c, dsc1c, jnp.zeros((1, 4 * d), F32)], axis=1)
    return loss, dx, g, d_ada, d_ada_c


HBM = pl.BlockSpec(memory_space=pltpu.HBM)
N_DEV, N_CHIP = 8, 4


def _place():
    x, y, c = lax.axis_index("x"), lax.axis_index("y"), lax.axis_index("c")
    return x, y, c, [(1 - x, y), (x, 1 - y), (1 - x, 1 - y)]


def _row_tile(n, mult, cap):
    return max(t for t in range(mult, min(n, cap) + 1, mult) if n % t == 0)


def _ag_small(name, v):
    def body(v_ref, out_ref, send_sems, recv_sems):
        x, y, c, _ = _place()
        out_ref[4 * x + 2 * y + c] = v_ref[...]

        def peer(r):
            return ((1 - x) if r & 4 else x, (1 - y) if r & 2 else y, (1 - c) if r & 1 else c)

        def copy(r, block):
            px, py, pc = block
            return pltpu.make_async_remote_copy(
                src_ref=v_ref, dst_ref=out_ref.at[4 * px + 2 * py + pc], send_sem=send_sems.at[r - 1],
                recv_sem=recv_sems.at[r - 1], device_id=peer(r), device_id_type=MESH)

        sends = [copy(r, (x, y, c)) for r in range(1, N_DEV)]
        for cp in sends:
            cp.start()
        for r in range(1, N_DEV):
            copy(r, peer(r)).wait_recv()
        for cp in sends:
            cp.wait_send()

    return pl.pallas_call(
        body, name=name, out_shape=jax.ShapeDtypeStruct((N_DEV,) + v.shape, v.dtype),
        in_specs=[pl.BlockSpec(memory_space=pltpu.VMEM)], out_specs=pl.BlockSpec(memory_space=pltpu.VMEM),
        scratch_shapes=[pltpu.SemaphoreType.DMA((N_DEV - 1,)), pltpu.SemaphoreType.DMA((N_DEV - 1,))],
    )(v)


def _halves(c, rows, mult):
    hr = rows // 2
    return pl.ds(pl.multiple_of(c * hr, mult), hr), pl.ds(pl.multiple_of((1 - c) * hr, mult), hr)


def _ag_shards(name, shard):
    rows = shard.shape[0]

    def body(w_ref, out_ref, send_sems, recv_sems, local_sem):
        x, y, c, chips = _place()
        mine_half, other_half = _halves(c, rows, 16)
        me = 2 * x + y
        mine = pltpu.make_async_copy(w_ref, out_ref.at[me], local_sem)
        mine.start()

        def copy(k, src, chip, half, to):
            return pltpu.make_async_remote_copy(
                src_ref=src, dst_ref=out_ref.at[chip, half], send_sem=send_sems.at[k], recv_sem=recv_sems.at[k],
                device_id=to, device_id_type=MESH)

        first = [copy(j, w_ref.at[mine_half], me, mine_half, (px, py, c)) for j, (px, py) in enumerate(chips)]
        for cp in first:
            cp.start()
        passed = []
        for j, (px, py) in enumerate(chips):
            pk = 2 * px + py
            copy(j, w_ref.at[mine_half], pk, mine_half, (px, py, c)).wait_recv()
            cp = copy(3 + j, out_ref.at[pk, mine_half], pk, mine_half, (x, y, 1 - c))
            cp.start()
            passed.append(cp)
        for j, (px, py) in enumerate(chips):
            copy(3 + j, w_ref.at[mine_half], 2 * px + py, other_half, (x, y, 1 - c)).wait_recv()
        for cp in first + passed:
            cp.wait_send()
        mine.wait()

    return pl.pallas_call(
        body, name=name, out_shape=jax.ShapeDtypeStruct((N_CHIP,) + shard.shape, shard.dtype),
        in_specs=[HBM], out_specs=HBM,
        scratch_shapes=[pltpu.SemaphoreType.DMA((6,)), pltpu.SemaphoreType.DMA((6,)), pltpu.SemaphoreType.DMA],
    )(shard)


def _swap_half(name, g):
    n_sh, rows, n = g.shape

    def body(g_ref, a_ref, send_sem, recv_sem):
        x, y, c, _ = _place()
        _, other_half = _halves(c, rows, 8)
        cp = pltpu.make_async_remote_copy(
            src_ref=g_ref.at[pl.ds(0, n_sh), other_half], dst_ref=a_ref, send_sem=send_sem, recv_sem=recv_sem,
            device_id=(x, y, 1 - c), device_id_type=MESH)
        cp.start()
        cp.wait()

    return pl.pallas_call(
        body, name=name, out_shape=jax.ShapeDtypeStruct((n_sh, rows // 2, n), g.dtype), in_specs=[HBM], out_specs=HBM,
        scratch_shapes=[pltpu.SemaphoreType.DMA, pltpu.SemaphoreType.DMA],
    )(g)


def _add_half(name, g, a, c_idx):
    n_sh, hr, n = a.shape
    tr = _row_tile(hr, 16, 256)
    nb = hr // tr

    def body(c_ref, g_ref, a_ref, o_ref):
        o_ref[...] = (g_ref[...] + a_ref[...]).astype(o_ref.dtype)

    return pl.pallas_call(
        body, name=name, out_shape=jax.ShapeDtypeStruct(a.shape, BF16),
        grid_spec=pltpu.PrefetchScalarGridSpec(
            num_scalar_prefetch=1, grid=(n_sh, nb),
            in_specs=[pl.BlockSpec((1, tr, n), lambda s, i, c_ref: (s, c_ref[0] * nb + i, 0)),
                      pl.BlockSpec((1, tr, n), lambda s, i, c_ref: (s, i, 0))],
            out_specs=pl.BlockSpec((1, tr, n), lambda s, i, c_ref: (s, i, 0))),
        compiler_params=_cp("parallel", "parallel"),
    )(c_idx, g, a)


def _scatter_chips(name, h):
    def body(h_ref, b_ref, send_sems, recv_sems, local_sem):
        x, y, c, chips = _place()
        me = 2 * x + y
        mine = pltpu.make_async_copy(h_ref.at[me], b_ref.at[me], local_sem)
        mine.start()

        def copy(j, src_block, dst_block, to):
            return pltpu.make_async_remote_copy(
                src_ref=h_ref.at[src_block], dst_ref=b_ref.at[dst_block], send_sem=send_sems.at[j],
                recv_sem=recv_sems.at[j], device_id=to, device_id_type=MESH)

        sends = [copy(j, 2 * px + py, me, (px, py, c)) for j, (px, py) in enumerate(chips)]
        for cp in sends:
            cp.start()
        for j, (px, py) in enumerate(chips):
            copy(j, me, 2 * px + py, (px, py, c)).wait_recv()
        for cp in sends:
            cp.wait_send()
        mine.wait()

    return pl.pallas_call(
        body, name=name, out_shape=jax.ShapeDtypeStruct(h.shape, h.dtype), in_specs=[HBM], out_specs=HBM,
        scratch_shapes=[pltpu.SemaphoreType.DMA((3,)), pltpu.SemaphoreType.DMA((3,)), pltpu.SemaphoreType.DMA],
    )(h)


def _sum_chips(name, b):
    n_sh, hr, n = b.shape
    tr = _row_tile(hr, 16, 256)

    def body(b0, b1, b2, b3, o_ref):
        o_ref[...] = ((b0[0].astype(F32) + b1[0].astype(F32)) + b2[0].astype(F32)) + b3[0].astype(F32)

    return pl.pallas_call(
        body, name=name, grid=(hr // tr,), out_shape=jax.ShapeDtypeStruct((hr, n), F32),
        in_specs=[pl.BlockSpec((1, tr, n), functools.partial(lambda i, k: (k, i, 0), k=k)) for k in range(n_sh)],
        out_specs=pl.BlockSpec((tr, n), lambda i: (i, 0)), compiler_params=_cp("parallel"),
    )(b, b, b, b)


def _share_half(name, f):
    hr, n = f.shape

    def body(f_ref, out_ref, send_sem, recv_sem, local_sem):
        x, y, c, _ = _place()
        mine_half, other_half = _halves(c, 2 * hr, 8)
        mine = pltpu.make_async_copy(f_ref, out_ref.at[mine_half], local_sem)
        mine.start()

        def copy(half):
            return pltpu.make_async_remote_copy(
                src_ref=f_ref, dst_ref=out_ref.at[half], send_sem=send_sem, recv_sem=recv_sem,
                device_id=(x, y, 1 - c), device_id_type=MESH)

        send = copy(mine_half)
        send.start()
        copy(other_half).wait_recv()
        send.wait_send()
        mine.wait()

    return pl.pallas_call(
        body, name=name, out_shape=jax.ShapeDtypeStruct((2 * hr, n), f.dtype), in_specs=[HBM], out_specs=HBM,
        scratch_shapes=[pltpu.SemaphoreType.DMA, pltpu.SemaphoreType.DMA, pltpu.SemaphoreType.DMA],
    )(f)


def _reduce_shards(name, g, c_idx):
    a = _swap_half(name + "_swap", g)
    h = _add_half(name + "_pair", g, a, c_idx)
    b = _scatter_chips(name + "_scatter", h)
    f = _sum_chips(name + "_sum", b)
    return _share_half(name + "_share", f)


def _f_adamw(w, g, m, v):
    m = ADAM_B1 * m + (1.0 - ADAM_B1) * g
    v = ADAM_B2 * v + (1.0 - ADAM_B2) * (g * g)
    m_hat = m / (1.0 - ADAM_B1 ** ADAM_STEP)
    v_hat = v / (1.0 - ADAM_B2 ** ADAM_STEP)
    return -ADAM_LR * (m_hat / (jnp.sqrt(v_hat) + ADAM_EPS) + ADAM_WD * w), m, v


def _adamw(name, w, g, m, v):
    rows, n = w.shape
    return _rowwise(name, lambda w, g, m, v: (_f_adamw(w, g, m, v), ()), rows, [(t, n, 0) for t in (w, g, m, v)], [],
                    [(n, F32)] * 3, [], tm=_row_tile(rows, 8, 256))


def _pack_rows(parts):
    rows = []
    for t in parts:
        t = t.reshape(-1)
        rows.append(jnp.pad(t, (0, -t.shape[0] % LANES)).reshape(-1, LANES))
    out = jnp.concatenate(rows, axis=0)
    return jnp.pad(out, ((0, -out.shape[0] % 8), (0, 0)))


def _unpack_rows(packed, shapes):
    out, r = [], 0
    for shp in shapes:
        n = int(np.prod(shp))
        nr = -(-n // LANES)
        out.append(packed[r:r + nr].reshape(-1)[:n].reshape(shp))
        r += nr
    return out


def _sum_blocks(name, g):
    def body(g_ref, o_ref):
        acc = g_ref[0]
        for k in range(1, g.shape[0]):
            acc = acc + g_ref[k]
        o_ref[...] = acc

    return pl.pallas_call(body, name=name, out_shape=jax.ShapeDtypeStruct(g.shape[1:], F32))(g)


def _silu(t):
    return t * _sigmoid(t)


def _ada_fwd(cc, w_ada):
    n = w_ada.shape[1]
    tn = _row_tile(n, LANES, 512)

    def body(cc_ref, w_ref, o_ref):
        o_ref[...] = _nn(_silu(cc_ref[...]), w_ref[...])

    return pl.pallas_call(
        body, name="ada_fwd", grid=(n // tn,), out_shape=jax.ShapeDtypeStruct((cc.shape[0], n), F32),
        in_specs=[pl.BlockSpec(cc.shape, lambda j: (0, 0)), pl.BlockSpec((w_ada.shape[0], tn), lambda j: (0, j))],
        out_specs=pl.BlockSpec((cc.shape[0], tn), lambda j: (0, j)), compiler_params=_cp("parallel"),
    )(cc, w_ada)


def _ada_bwd(cc, dm, w_ada):
    d, n = w_ada.shape
    tn = _row_tile(n, LANES, 512)

    def body(cc_ref, dm_ref, w_ref, gw_ref, ds_ref):
        @pl.when(pl.program_id(0) == 0)
        def _():
            ds_ref[...] = jnp.zeros_like(ds_ref)

        gw_ref[...] = _raw_dot("tn", _silu(cc_ref[...]), dm_ref[...], True)
        ds_ref[...] += _raw_dot("nt", dm_ref[...], w_ref[...], False)

    return pl.pallas_call(
        body, name="ada_bwd", grid=(n // tn,),
        out_shape=[jax.ShapeDtypeStruct((d, n), F32), jax.ShapeDtypeStruct(cc.shape, F32)],
        in_specs=[pl.BlockSpec(cc.shape, lambda j: (0, 0)), pl.BlockSpec((cc.shape[0], tn), lambda j: (0, j)),
                  pl.BlockSpec((d, tn), lambda j: (0, j))],
        out_specs=[pl.BlockSpec((d, tn), lambda j: (0, j)), pl.BlockSpec(cc.shape, lambda j: (0, 0))],
        compiler_params=_cp("arbitrary"),
    )(cc, dm, w_ada)


def _c_ctx_grad(parts, c_ctx):
    def body(p_ref, c_ref, o_ref):
        ds = ((p_ref[0] + p_ref[1]) + p_ref[2]) + p_ref[3]
        _, vjp = jax.vjp(_silu, c_ref[...])
        o_ref[...] = vjp(ds)[0]

    return pl.pallas_call(body, name="c_ctx_grad", out_shape=jax.ShapeDtypeStruct(c_ctx.shape, F32))(parts, c_ctx)


def kernel(x, c, ctx, c_ctx, w_ada, b_ada, g_pre_mix, g_post_mix, g_pre_ffn, g_post_ffn, w_in, attn_sink, w_gate_fwd, b_gate_fwd, w_gate_bwd, b_gate_bwd, g_gla_norm, w_out, w_ffn_in, w_ffn_out, loss_target, m_c_ctx, m_w_ada, m_b_ada, m_g_pre_mix, m_g_post_mix, m_g_pre_ffn, m_g_post_ffn, m_w_in, m_attn_sink, m_w_gate_fwd, m_b_gate_fwd, m_w_gate_bwd, m_b_gate_bwd, m_g_gla_norm, m_w_out, m_w_ffn_in, m_w_ffn_out, v_c_ctx, v_w_ada, v_b_ada, v_g_pre_mix, v_g_post_mix, v_g_pre_ffn, v_g_post_ffn, v_w_in, v_attn_sink, v_w_gate_fwd, v_b_gate_fwd, v_w_gate_bwd, v_b_gate_bwd, v_g_gla_norm, v_w_out, v_w_ffn_in, v_w_ffn_out):
    xi, yi, ci = lax.axis_index("x"), lax.axis_index("y"), lax.axis_index("c")
    dev, chip = 4 * xi + 2 * yi + ci, 2 * xi + yi
    c_idx = jnp.reshape(ci, (1,)).astype(jnp.int32)
    d = x.shape[-1]
    n_ada, n_in, n_f = w_ada.shape[-1], w_in.shape[-1], w_ffn_in.shape[-1]
    r_out, r_f = w_out.shape[1], w_ffn_out.shape[1]
    n_gate = w_gate_fwd.shape[-1]
    by_chip = lambda t: t[0::2]

    rc = -(-d // LANES)
    g1 = _ag_small("gather_cond", _pack_rows([c[0], w_gate_fwd[0], w_gate_bwd[0]]))
    c_all = g1[:, :rc].reshape(N_DEV, -1)[:, :d]
    gr = GATE_RANK * n_gate // LANES
    gate_full = lambda off: jnp.transpose(by_chip(g1)[:, off:off + gr].reshape(N_CHIP, GATE_RANK, n_gate),
                                          (1, 0, 2)).reshape(GATE_RANK, N_CHIP * n_gate)
    wgf, wgb = gate_full(rc), gate_full(rc + gr)
    cc = jnp.concatenate([c_all, c_ctx[None, :], jnp.zeros((7, d), F32)], axis=0)

    g2 = _ag_small("gather_ada", _ada_fwd(cc, w_ada[0]).reshape(-1, LANES))
    ada_all = jnp.transpose(by_chip(g2).reshape(N_CHIP, 16, n_ada), (1, 0, 2)).reshape(16, N_CHIP * n_ada) + b_ada
    ada = lax.dynamic_slice(ada_all, (dev, 0), (1, N_CHIP * n_ada))
    ada_c = ada_all[N_DEV:N_DEV + 1]

    colg = _ag_shards("gather_cols", jnp.concatenate([w_in[0], w_ffn_in[0]], axis=1).astype(BF16))
    rowg = _ag_shards("gather_rows", jnp.concatenate([w_out[0], w_ffn_out[0]], axis=0).astype(BF16))
    w = _prep_weights(jnp.concatenate([colg[k, :, :n_in] for k in range(N_CHIP)], axis=1),
                      jnp.concatenate([rowg[k, :r_out] for k in range(N_CHIP)], axis=0),
                      jnp.concatenate([colg[k, :, n_in:] for k in range(N_CHIP)], axis=1),
                      jnp.concatenate([rowg[k, r_out:] for k in range(N_CHIP)], axis=0), wgf, wgb)
    w.update(g_pre_mix=g_pre_mix, g_post_mix=g_post_mix, g_pre_ffn=g_pre_ffn, g_post_ffn=g_post_ffn,
             attn_sink=attn_sink, b_gate_fwd=b_gate_fwd, b_gate_bwd=b_gate_bwd, g_gla_norm=g_gla_norm)

    loss_lanes, grad_x, g, d_ada, d_ada_c = _local_step(x[0], ctx[0], loss_target[0], ada, ada_c, w)

    small = ("g_pre_mix", "g_post_mix", "g_pre_ffn", "g_post_ffn", "attn_sink", "b_gate_fwd", "b_gate_bwd",
             "g_gla_norm", "w_gate_fwd", "w_gate_bwd")
    shapes = [(1, 6 * d)] * 2 + [g[n].shape for n in small]
    g3 = _ag_small("gather_small_grads", _pack_rows([d_ada, d_ada_c] + [g[n] for n in small]))
    tot = dict(zip(("d_ada", "d_ada_c") + small, _unpack_rows(_sum_blocks("sum_small_grads", g3), shapes)))
    r_ada = 6 * d // LANES
    dm = jnp.concatenate([g3[:, :r_ada].reshape(N_DEV, 6 * d), tot["d_ada_c"], jnp.zeros((7, 6 * d), F32)], axis=0)
    grads = {n: tot[n] for n in small[:8]}
    grads["b_ada"] = _sum_blocks("sum_b_ada", dm.reshape(16, r_ada, LANES)).reshape(1, 6 * d)
    grads["w_gate_fwd"] = lax.dynamic_slice(tot["w_gate_fwd"], (0, chip * n_gate), (GATE_RANK, n_gate))[None]
    grads["w_gate_bwd"] = lax.dynamic_slice(tot["w_gate_bwd"], (0, chip * n_gate), (GATE_RANK, n_gate))[None]
    gw_ada, dsc = _ada_bwd(cc, lax.dynamic_slice(dm, (0, chip * n_ada), (16, n_ada)), w_ada[0])
    grads["w_ada"] = gw_ada[None]
    g4 = _ag_small("gather_c_ctx", _pack_rows([dsc[N_DEV]]))
    grads["c_ctx"] = _c_ctx_grad(by_chip(g4), _pack_rows([c_ctx])).reshape(-1)[:d]

    g_w_in = _unpack_w_in_grad(g["w_in"])
    gcol = jnp.stack([jnp.concatenate([g_w_in[:, k * n_in:(k + 1) * n_in], g["w_ffn_in"][:, k * n_f:(k + 1) * n_f]],
                                      axis=1) for k in range(N_CHIP)])
    grow = jnp.stack([jnp.concatenate([g["w_out"][k * r_out:(k + 1) * r_out], g["w_ffn_out"][k * r_f:(k + 1) * r_f]],
                                      axis=0) for k in range(N_CHIP)])
    rcol = _reduce_shards("reduce_cols", gcol, c_idx)
    rrow = _reduce_shards("reduce_rows", grow, c_idx)
    grads["w_in"], grads["w_ffn_in"] = rcol[None, :, :n_in], rcol[None, :, n_in:]
    grads["w_out"], grads["w_ffn_out"] = rrow[None, :r_out], rrow[None, r_out:]

    names = ("c_ctx", "w_ada", "b_ada", "g_pre_mix", "g_post_mix", "g_pre_ffn", "g_post_ffn", "w_in", "attn_sink",
             "w_gate_fwd", "b_gate_fwd", "w_gate_bwd", "b_gate_bwd", "g_gla_norm", "w_out", "w_ffn_in", "w_ffn_out")
    weights = dict(zip(names, (c_ctx, w_ada, b_ada, g_pre_mix, g_post_mix, g_pre_ffn, g_post_ffn, w_in, attn_sink,
                               w_gate_fwd, b_gate_fwd, w_gate_bwd, b_gate_bwd, g_gla_norm, w_out, w_ffn_in,
                               w_ffn_out)))
    m_in = dict(zip(names, (m_c_ctx, m_w_ada, m_b_ada, m_g_pre_mix, m_g_post_mix, m_g_pre_ffn, m_g_post_ffn, m_w_in,
                            m_attn_sink, m_w_gate_fwd, m_b_gate_fwd, m_w_gate_bwd, m_b_gate_bwd, m_g_gla_norm,
                            m_w_out, m_w_ffn_in, m_w_ffn_out)))
    v_in = dict(zip(names, (v_c_ctx, v_w_ada, v_b_ada, v_g_pre_mix, v_g_post_mix, v_g_pre_ffn, v_g_post_ffn, v_w_in,
                            v_attn_sink, v_w_gate_fwd, v_b_gate_fwd, v_w_gate_bwd, v_b_gate_bwd, v_g_gla_norm,
                            v_w_out, v_w_ffn_in, v_w_ffn_out)))
    large = ("w_ada", "w_in", "w_out", "w_ffn_in", "w_ffn_out")
    tiny = tuple(n for n in names if n not in large)
    delta, new_m, new_v = {}, {}, {}
    for n in large:
        dl, nm, nv = _adamw("adamw_" + n, weights[n][0], grads[n][0], m_in[n][0], v_in[n][0])
        delta[n], new_m[n], new_v[n] = dl[None], nm[None], nv[None]
    tiny_shapes = [weights[n].shape for n in tiny]
    packed = [_pack_rows([t[n] for n in tiny]) for t in (weights, grads, m_in, v_in)]
    for out, res in zip((delta, new_m, new_v), _adamw("adamw_small", *packed)):
        out.update(zip(tiny, _unpack_rows(res, tiny_shapes)))
    for n in tiny:
        grads[n] = grads[n].reshape(weights[n].shape)

    loss = lax.psum(loss_lanes[0, 0], ("x", "y", "c"))
    return (loss, grad_x[None], *[grads[n] for n in names], *[delta[n] for n in names], *[new_m[n] for n in names],
            *[new_v[n] for n in names])
```

```python
import functools

import jax
import jax.numpy as jnp
import numpy as np
from jax import lax
from jax.experimental import pallas as pl
from jax.experimental.pallas import tpu as pltpu

F32 = jnp.float32
BF16 = jnp.bfloat16
MESH = pl.DeviceIdType.MESH

HEAD_DIM = 64
ATT_HEADS = 8
ATT_KV_HEADS = 2
ATT_GROUP = ATT_HEADS // ATT_KV_HEADS
WINDOW = 128
BLOCK = 128
GRID_W = 64
ROPE_BASE = 10000.0
GLA_HEADS = 8
GLA_DK = 32
GLA_DV = 64
GLA_CHUNK = 64
GATE_RANK = 16
GATE_TAU = 16.0
NEG_INF = -1e30
QW = ATT_HEADS * HEAD_DIM
KVW = ATT_KV_HEADS * HEAD_DIM
GKW = GLA_HEADS * GLA_DK
GVW = GLA_HEADS * GLA_DV
IN_COLS = QW + 2 * KVW + 2 * GKW + 2 * GVW + 2 * GATE_RANK
LANES = 128
IN_PAD = IN_COLS + LANES - 2 * GATE_RANK
C_Q, C_GV, C_GG = 0, QW, QW + GVW
C_K = C_GG + GVW
C_V = C_K + KVW
C_GQ = C_V + KVW
C_GK = C_GQ + GKW
C_Z = C_GK + GKW
MIX = QW + GVW

ADAM_LR, ADAM_B1, ADAM_B2, ADAM_EPS, ADAM_WD, ADAM_STEP = 0.001, 0.9, 0.999, 1e-08, 0.01, 10

VMEM_LIMIT = 56 * 1024 * 1024


def _cp(*sem):
    return pltpu.CompilerParams(dimension_semantics=sem, vmem_limit_bytes=VMEM_LIMIT)


def _pick(n, cands):
    for t in cands:
        if n % t == 0:
            return t
    return n


_DIMS = {"nn": (((1,), (0,)), ((), ())), "nt": (((1,), (1,)), ((), ())), "tn": (((0,), (0,)), ((), ()))}


def _raw_dot(mode, a, b, hi):
    if hi:
        return lax.dot_general(a.astype(F32), b.astype(F32), _DIMS[mode], precision=lax.Precision.HIGHEST,
                               preferred_element_type=F32)
    return lax.dot_general(a.astype(BF16), b.astype(BF16), _DIMS[mode], preferred_element_type=F32)


def _make_dot(mode, hi):
    @jax.custom_vjp
    def dot(a, b):
        return _raw_dot(mode, a, b, hi)

    def fwd(a, b):
        return _raw_dot(mode, a, b, hi), (a, b)

    def bwd(res, dc):
        a, b = res
        if mode == "nn":
            return _raw_dot("nt", dc, b, hi), _raw_dot("tn", a, dc, hi)
        if mode == "nt":
            return _raw_dot("nn", dc, b, hi), _raw_dot("tn", dc, a, hi)
        return _raw_dot("nt", b, dc, hi), _raw_dot("nn", a, dc, hi)

    dot.defvjp(fwd, bwd)
    return dot


_nn, _nt, _tn = _make_dot("nn", False), _make_dot("nt", False), _make_dot("tn", False)
_nn_hi = _make_dot("nn", True)


MM_VMEM_BUDGET = 44 * 1024 * 1024


def _halvings(n):
    out = [n]
    while out[-1] % (2 * LANES) == 0:
        out.append(out[-1] // 2)
    return out


def _mm_tiles(mode, m, n, k, a_bytes, b_bytes, o_bytes):
    tms = [t for t in dict.fromkeys((m, 2048, 1024, 512, 256, 128)) if m % t == 0 and t <= 4096] or [m]
    tks = ([t for t in (512, 256, 128) if k % t == 0] or [k]) if mode == "tn" else _halvings(k)
    for tn in _halvings(n):
        for tk in tks:
            for tm in tms:
                acc = tm * tn * 4 if (k // tk > 1 and o_bytes != 4) else 0
                if 2 * (tm * tk * a_bytes + tk * tn * b_bytes + tm * tn * o_bytes) + acc <= MM_VMEM_BUDGET:
                    return tm, tn, tk
    return tms[-1], _halvings(n)[-1], tks[-1]


def _mm(name, a, b, mode, out_dtype=F32):
    if mode == "nn":
        (m, k), n = a.shape, b.shape[1]
    elif mode == "nt":
        (m, k), n = a.shape, b.shape[0]
    else:
        (k, m), n = a.shape, b.shape[1]
    tm, tn, tk = _mm_tiles(mode, m, n, k, a.dtype.itemsize, b.dtype.itemsize, jnp.dtype(out_dtype).itemsize)
    nk = k // tk
    use_acc = nk > 1 and out_dtype != F32

    def body(a_ref, b_ref, o_ref, *acc):
        part = _raw_dot(mode, a_ref[...], b_ref[...], False)
        if nk == 1:
            o_ref[...] = part.astype(o_ref.dtype)
            return
        acc_ref = acc[0] if use_acc else o_ref
        kk = pl.program_id(2)

        @pl.when(kk == 0)
        def _():
            acc_ref[...] = part

        @pl.when(kk > 0)
        def _():
            acc_ref[...] += part

        if use_acc:
            @pl.when(kk == nk - 1)
            def _():
                o_ref[...] = acc_ref[...].astype(o_ref.dtype)

    if mode == "nn":
        a_spec = pl.BlockSpec((tm, tk), lambda i, j, kk: (i, kk))
        b_spec = pl.BlockSpec((tk, tn), lambda i, j, kk: (kk, j))
    elif mode == "nt":
        a_spec = pl.BlockSpec((tm, tk), lambda i, j, kk: (i, kk))
        b_spec = pl.BlockSpec((tn, tk), lambda i, j, kk: (j, kk))
    else:
        a_spec = pl.BlockSpec((tk, tm), lambda i, j, kk: (kk, i))
        b_spec = pl.BlockSpec((tk, tn), lambda i, j, kk: (kk, j))
    return pl.pallas_call(
        body, name=name, grid=(m // tm, n // tn, nk),
        in_specs=[a_spec, b_spec], out_specs=pl.BlockSpec((tm, tn), lambda i, j, kk: (i, j)),
        out_shape=jax.ShapeDtypeStruct((m, n), out_dtype),
        scratch_shapes=[pltpu.VMEM((tm, tn), F32)] if use_acc else [],
        compiler_params=_cp("parallel", "parallel", "arbitrary"),
    )(a, b)


def _rowwise(name, fn, rows, row_ins, full_ins, row_outs, acc_outs, tm=None):
    tm = tm or _pick(rows, (512, 256, 128))
    n_r, n_f, n_o, n_a = len(row_ins), len(full_ins), len(row_outs), len(acc_outs)

    def body(*refs):
        ins, outs = refs[:n_r + n_f], refs[n_r + n_f:]
        vals = [r[...].astype(F32) for r in ins]
        ro, ao = fn(*vals)
        for r, val in zip(outs[:n_o], ro):
            r[...] = val.astype(r.dtype)
        if n_a:
            @pl.when(pl.program_id(0) == 0)
            def _():
                for r in outs[n_o:]:
                    r[...] = jnp.zeros_like(r)

            for r, val in zip(outs[n_o:], ao):
                r[...] += val

    in_specs = [pl.BlockSpec((tm, w), functools.partial(lambda i, cb: (i, cb), cb=cb)) for _, w, cb in row_ins]
    in_specs += [pl.BlockSpec(a.shape, lambda i: (0, 0)) for a in full_ins]
    out_specs = [pl.BlockSpec((tm, w), lambda i: (i, 0)) for w, _ in row_outs]
    out_specs += [pl.BlockSpec(s, lambda i: (0, 0)) for s in acc_outs]
    out_shape = [jax.ShapeDtypeStruct((rows, w), dt) for w, dt in row_outs]
    out_shape += [jax.ShapeDtypeStruct(s, F32) for s in acc_outs]
    return pl.pallas_call(
        body, name=name, grid=(rows // tm,), in_specs=in_specs, out_specs=out_specs, out_shape=out_shape,
        compiler_params=_cp("arbitrary" if n_a else "parallel"),
    )(*[a for a, _, _ in row_ins], *full_ins)


def _rn(x):
    return x * lax.rsqrt(jnp.mean(x * x, axis=-1, keepdims=True) + 1e-6)


def _sigmoid(t):
    return 1.0 / (1.0 + jnp.exp(-t))


def _f_norm_mod(x, g, sh, sc):
    return _rn(x) * g * (1.0 + sc) + sh


def _f_post_res(xr, y, g, gate):
    return xr + gate * (_rn(y) * g)


def _f_swiglu(g, u):
    return g * _sigmoid(g) * u


def _logsig(u):
    return jnp.minimum(u, 0.0) - jnp.log(1.0 + jnp.exp(-jnp.abs(u)))


def _f_gate(z, wf, wb, bf, bb):
    return _logsig(_nn(z, wf) + bf) / GATE_TAU, _logsig(_nn(z, wb) + bb) / GATE_TAU


def _f_gla_out(of, ob, gg, gt, bd):
    o = of + ob
    ms = _nn_hi(o * o, bd)
    return o * lax.rsqrt(ms + 1e-6) * gt * (gg * _sigmoid(gg))


def _norm_mod(name, x, g, sh, sc):
    rows, d = x.shape
    return _rowwise(name, lambda x, g, sh, sc: ((_f_norm_mod(x, g, sh, sc),), ()), rows,
                    [(x, d, 0)], [g, sh, sc], [(d, BF16)], [])[0]


def _norm_mod_bwd(name, dh, dres, x, g, sh, sc):
    rows, d = x.shape

    def fn(dh, dres, x, g, sh, sc):
        _, vjp = jax.vjp(_f_norm_mod, x, g, sh, sc)
        dx, dg, dsh, dsc = vjp(dh)
        return (dx + dres,), (dg, dsh, dsc)

    return _rowwise(name, fn, rows, [(dh, d, 0), (dres, d, 0), (x, d, 0)], [g, sh, sc], [(d, F32)],
                    [(1, d)] * 3)


def _post_res(name, xr, y, g, gate):
    rows, d = xr.shape
    return _rowwise(name, lambda xr, y, g, gate: ((_f_post_res(xr, y, g, gate),), ()), rows,
                    [(xr, d, 0), (y, d, 0)], [g, gate], [(d, F32)], [])[0]


def _post_res_bwd(name, dxo, y, g, gate):
    rows, d = y.shape

    def fn(dxo, y, g, gate):
        _, vjp = jax.vjp(lambda y, g, gate: _f_post_res(jnp.zeros_like(y), y, g, gate), y, g, gate)
        dy, dg, dgate = vjp(dxo)
        return (dy,), (dg, dgate)

    return _rowwise(name, fn, rows, [(dxo, d, 0), (y, d, 0)], [g, gate], [(d, BF16)], [(1, d)] * 2)


def _post_res_loss(name, xr, y, g, gate, target):
    rows, d = xr.shape

    def fn(xr, y, target, g, gate):
        diff = _f_post_res(xr, y, g, gate) - target
        part = 0.5 * jnp.sum(jnp.mean(diff * diff, axis=-1, keepdims=True), axis=0, keepdims=True)
        return (diff * (1.0 / d),), (jnp.broadcast_to(part, (1, LANES)),)

    return _rowwise(name, fn, rows, [(xr, d, 0), (y, d, 0), (target, d, 0)], [g, gate], [(d, F32)], [(1, LANES)])


def _swiglu(name, u):
    rows, f2 = u.shape
    f = f2 // 2
    return _rowwise(name, lambda g, u: ((_f_swiglu(g, u),), ()), rows, [(u, f, 0), (u, f, 1)], [], [(f, BF16)], [],
                    tm=_pick(rows, (256, 128)))[0]


def _swiglu_bwd(name, da, u):
    rows, f2 = u.shape
    f = f2 // 2

    def fn(da, g, u):
        _, vjp = jax.vjp(_f_swiglu, g, u)
        return (jnp.concatenate(vjp(da), axis=1),), ()

    return _rowwise(name, fn, rows, [(da, f, 0), (u, f, 0), (u, f, 1)], [], [(f2, BF16)], [],
                    tm=_pick(rows, (256, 128)))[0]


def _gate_fwd(name, p, wf, wb, bf, bb):
    rows = p.shape[0]
    return _rowwise(name, lambda z, wf, wb, bf, bb: (_f_gate(z, wf, wb, bf, bb), ()), rows,
                    [(p, LANES, C_Z // LANES)], [wf, wb, bf, bb], [(GKW, F32)] * 2, [])


def _gate_bwd(name, p, dla_f, dla_b, wf, wb, bf, bb):
    rows = p.shape[0]

    def fn(z, dlf, dlb, wf, wb, bf, bb):
        _, vjp = jax.vjp(_f_gate, z, wf, wb, bf, bb)
        dz, dwf, dwb, dbf, dbb = vjp((dlf, dlb))
        return (dz,), (dwf, dwb, dbf, dbb)

    return _rowwise(name, fn, rows, [(p, LANES, C_Z // LANES), (dla_f, GKW, 0), (dla_b, GKW, 0)],
                    [wf, wb, bf, bb], [(LANES, BF16)], [(LANES, GKW), (LANES, GKW), (1, GKW), (1, GKW)])


def _head_mean_matrix():
    h = np.arange(GVW) // GLA_DV
    return jnp.asarray((h[:, None] == h[None, :]).astype(np.float32) / GLA_DV)


def _gla_out(name, attn, of, ob, p, gt):
    rows = of.shape[0]
    bd = _head_mean_matrix()
    fn = lambda attn, of, ob, gg, gt, bd: ((jnp.concatenate([attn, _f_gla_out(of, ob, gg, gt, bd)], axis=1),), ())
    return _rowwise(name, fn, rows, [(attn, QW, 0), (of, GVW, 0), (ob, GVW, 0), (p, GVW, C_GG // GVW)], [gt, bd],
                    [(MIX, BF16)], [])[0]


def _gla_out_bwd(name, dmix, of, ob, p, gt):
    rows = of.shape[0]
    bd = _head_mean_matrix()

    def fn(dm, of, ob, gg, gt, bd):
        _, vjp = jax.vjp(lambda of, gg, gt: _f_gla_out(of, ob, gg, gt, bd), of, gg, gt)
        do, dgg, dgt = vjp(dm)
        return (do, dgg), (dgt,)

    return _rowwise(name, fn, rows, [(dmix, GVW, 1), (of, GVW, 0), (ob, GVW, 0), (p, GVW, C_GG // GVW)], [gt, bd],
                    [(GVW, F32), (GVW, BF16)], [(1, GVW)])


def _rope_tables(n_tokens):
    t = jnp.arange(n_tokens)
    row = (t // GRID_W).astype(F32)
    col = (t % GRID_W).astype(F32)
    half = HEAD_DIM // 2
    inv_freq = ROPE_BASE ** (-jnp.arange(0, half, 2, dtype=F32) / half)
    ang_r = row[:, None] * inv_freq[None, :]
    ang_c = col[:, None] * inv_freq[None, :]
    ang = jnp.concatenate([ang_r, ang_r, ang_c, ang_c], axis=-1)
    sign = jnp.concatenate([-jnp.ones((16,), F32), jnp.ones((16,), F32)] * 2)
    cos, sin = jnp.cos(ang), jnp.sin(ang) * sign[None, :]
    return jnp.tile(cos, (1, 2)), jnp.tile(sin, (1, 2))


def _rot_pairs(x):
    w = x.shape[-1]
    lane = lax.broadcasted_iota(jnp.int32, x.shape, x.ndim - 1)
    return jnp.where((lane % 32) < 16, pltpu.roll(x, w - 16, x.ndim - 1), pltpu.roll(x, 16, x.ndim - 1))


def _rope_apply(x, cos, sin_signed, inverse):
    reps = x.shape[-1] // LANES
    cos = jnp.concatenate([cos] * reps, axis=-1) if reps > 1 else cos
    sin = jnp.concatenate([sin_signed] * reps, axis=-1) if reps > 1 else sin_signed
    if inverse:
        return x * cos + _rot_pairs(x * sin)
    return x * cos + _rot_pairs(x) * sin


def _rope_fwd(name, p, cos, sin):
    rows = p.shape[0]

    def fn(q, k, v, cos, sin):
        return (_rope_apply(q, cos, sin, False), _rope_apply(k, cos, sin, False), v), ()

    return _rowwise(name, fn, rows, [(p, QW, 0), (p, KVW, C_K // KVW), (p, KVW, C_V // KVW), (cos, LANES, 0),
                                     (sin, LANES, 0)], [], [(QW, BF16), (KVW, BF16), (KVW, BF16)], [])


def _rope_bwd(name, dq, dk, cos, sin):
    rows = dq.shape[0]

    def fn(dq, dk, cos, sin):
        return (_rope_apply(dq, cos, sin, True), _rope_apply(dk, cos, sin, True)), ()

    return _rowwise(name, fn, rows, [(dq, QW, 0), (dk, KVW, 0), (cos, LANES, 0), (sin, LANES, 0)], [],
                    [(QW, BF16), (KVW, BF16)], [])


def _f_attn(qs, kws, vws, kcs, vcs, sink, n, n_tokens):
    i = lax.broadcasted_iota(jnp.int32, (BLOCK, 3 * BLOCK), 0)
    j = lax.broadcasted_iota(jnp.int32, (BLOCK, 3 * BLOCK), 1)
    kpos = (n - 1) * BLOCK + j
    mask = (jnp.abs(j - BLOCK - i) <= WINDOW) & (kpos >= 0) & (kpos < n_tokens)
    head_id = lax.broadcasted_iota(jnp.int32, (1, ATT_HEADS), 1)
    scale = HEAD_DIM ** -0.5
    outs = []
    for hq in range(ATT_HEADS):
        h = hq // ATT_GROUP
        s_w = jnp.where(mask, _nt(qs[hq], kws[h]) * scale, NEG_INF)
        s_c = _nt(qs[hq], kcs[h]) * scale
        sk = jnp.sum(jnp.where(head_id == hq, sink, 0.0), axis=-1, keepdims=True)
        m = lax.stop_gradient(jnp.maximum(jnp.maximum(jnp.max(s_w, axis=-1, keepdims=True),
                                                      jnp.max(s_c, axis=-1, keepdims=True)), sk))
        pw, pc = jnp.exp(s_w - m), jnp.exp(s_c - m)
        den = jnp.sum(pw, axis=-1, keepdims=True) + jnp.sum(pc, axis=-1, keepdims=True) + jnp.exp(sk - m)
        outs.append((_nn(pw, vws[h]) + _nn(pc, vcs[h])) / den)
    return tuple(outs)


def _attn_loads(n, q_ref, kp_ref, vp_ref, kc_ref, vc_ref):
    r0 = pl.multiple_of(n * BLOCK, BLOCK)
    hs = lambda h: slice(h * HEAD_DIM, (h + 1) * HEAD_DIM)
    qs = [q_ref[:, hs(h)].astype(F32) for h in range(ATT_HEADS)]
    kws = [kp_ref[pl.ds(r0, 3 * BLOCK), hs(h)].astype(F32) for h in range(ATT_KV_HEADS)]
    vws = [vp_ref[pl.ds(r0, 3 * BLOCK), hs(h)].astype(F32) for h in range(ATT_KV_HEADS)]
    kcs = [kc_ref[:, hs(h)].astype(F32) for h in range(ATT_KV_HEADS)]
    vcs = [vc_ref[:, hs(h)].astype(F32) for h in range(ATT_KV_HEADS)]
    return r0, hs, qs, kws, vws, kcs, vcs


def _attn_specs(s, c):
    full = lambda shape: pl.BlockSpec(shape, lambda n: (0, 0))
    return [pl.BlockSpec((BLOCK, QW), lambda n: (n, 0)), full((s + 2 * BLOCK, KVW)), full((s + 2 * BLOCK, KVW)),
            full((c, KVW)), full((c, KVW)), full((1, ATT_HEADS))]


def _attn_fwd(q, kp, vp, kc, vc, sink):
    s, c = q.shape[0], kc.shape[0]

    def body(q_ref, kp_ref, vp_ref, kc_ref, vc_ref, sink_ref, o_ref):
        n = pl.program_id(0)
        _, hs, qs, kws, vws, kcs, vcs = _attn_loads(n, q_ref, kp_ref, vp_ref, kc_ref, vc_ref)
        outs = _f_attn(qs, kws, vws, kcs, vcs, sink_ref[...], n, s)
        for h in range(ATT_HEADS):
            o_ref[:, hs(h)] = outs[h].astype(o_ref.dtype)

    return pl.pallas_call(
        body, name="attn_fwd", grid=(s // BLOCK,), in_specs=_attn_specs(s, c),
        out_specs=pl.BlockSpec((BLOCK, QW), lambda n: (n, 0)), out_shape=jax.ShapeDtypeStruct((s, QW), BF16),
        compiler_params=_cp("parallel"),
    )(q, kp, vp, kc, vc, sink)


def _attn_bwd(do, q, kp, vp, kc, vc, sink):
    s, c = q.shape[0], kc.shape[0]

    def body(do_ref, q_ref, kp_ref, vp_ref, kc_ref, vc_ref, sink_ref, dq_ref, dkp_ref, dvp_ref, dkc_ref, dvc_ref,
             dsink_ref):
        n = pl.program_id(0)

        @pl.when(n == 0)
        def _():
            for r in (dkp_ref, dvp_ref, dkc_ref, dvc_ref, dsink_ref):
                r[...] = jnp.zeros_like(r)

        r0, hs, qs, kws, vws, kcs, vcs = _attn_loads(n, q_ref, kp_ref, vp_ref, kc_ref, vc_ref)
        _, vjp = jax.vjp(lambda qs, kws, vws, kcs, vcs, sink: _f_attn(qs, kws, vws, kcs, vcs, sink, n, s),
                         qs, kws, vws, kcs, vcs, sink_ref[...])
        dqs, dkws, dvws, dkcs, dvcs, dsink = vjp(tuple(do_ref[:, hs(h)].astype(F32) for h in range(ATT_HEADS)))
        for h in range(ATT_HEADS):
            dq_ref[:, hs(h)] = dqs[h]
        for h in range(ATT_KV_HEADS):
            dkp_ref[pl.ds(r0, 3 * BLOCK), hs(h)] += dkws[h]
            dvp_ref[pl.ds(r0, 3 * BLOCK), hs(h)] += dvws[h]
            dkc_ref[:, hs(h)] += dkcs[h]
            dvc_ref[:, hs(h)] += dvcs[h]
        dsink_ref[...] += dsink

    full = lambda shape: pl.BlockSpec(shape, lambda n: (0, 0))
    return pl.pallas_call(
        body, name="attn_bwd", grid=(s // BLOCK,),
        in_specs=[pl.BlockSpec((BLOCK, QW), lambda n: (n, 0))] + _attn_specs(s, c),
        out_specs=[pl.BlockSpec((BLOCK, QW), lambda n: (n, 0)), full((s + 2 * BLOCK, KVW)), full((s + 2 * BLOCK, KVW)),
                   full((c, KVW)), full((c, KVW)), full((1, ATT_HEADS))],
        out_shape=[jax.ShapeDtypeStruct((s, QW), F32), jax.ShapeDtypeStruct((s + 2 * BLOCK, KVW), F32),
                   jax.ShapeDtypeStruct((s + 2 * BLOCK, KVW), F32), jax.ShapeDtypeStruct((c, KVW), F32),
                   jax.ShapeDtypeStruct((c, KVW), F32), jax.ShapeDtypeStruct((1, ATT_HEADS), F32)],
        compiler_params=_cp("arbitrary"),
    )(do, q, kp, vp, kc, vc, sink)


def _gla_masks():
    hk = np.arange(GKW) // GLA_DK
    hv = np.arange(GVW) // GLA_DV
    head_k = (np.arange(GLA_HEADS)[:, None] == hk[None, :]).astype(np.float32)
    head_v = (np.arange(GLA_HEADS)[:, None] == hv[None, :]).astype(np.float32)
    bd_t = (hv[:, None] == hk[None, :]).astype(np.float32)
    return jnp.asarray(head_k), jnp.asarray(head_v), jnp.asarray(bd_t)


def _tri(n, rev, strict=False):
    i = lax.broadcasted_iota(jnp.int32, (n, n), 0)
    j = lax.broadcasted_iota(jnp.int32, (n, n), 1)
    if strict:
        keep = (j > i) if rev else (j < i)
    else:
        keep = (j >= i) if rev else (j <= i)
    return keep


def _f_gla_chunk(q, k, v, la, st, head_k, head_v, bd_t, rev):
    keep = _tri(GLA_CHUNK, rev)
    b = _nn_hi(keep.astype(F32), la)
    bl = jnp.sum(la, axis=0, keepdims=True)
    qd = q * (GLA_DK ** -0.5) * jnp.exp(b)
    ki = k * jnp.exp(-b)
    kd = k * jnp.exp(bl - b)
    q_heads = (qd[None, :, :] * head_k[:, None, :]).reshape(GLA_HEADS * GLA_CHUNK, GKW)
    a_all = _nt(q_heads, ki).reshape(GLA_HEADS, GLA_CHUNK, GLA_CHUNK)
    a_all = jnp.where(keep[None, :, :], a_all, 0.0).reshape(GLA_HEADS * GLA_CHUNK, GLA_CHUNK)
    o_all = _nn(a_all, v).reshape(GLA_HEADS, GLA_CHUNK, GVW)
    intra = jnp.sum(o_all * head_v[:, None, :], axis=0)
    inter = _nt(qd, st)
    st_new = st * jnp.exp(bl) + bd_t * _tn(v, kd)
    return intra + inter, st_new


def _gla_specs(s, tb, order):
    return [pl.BlockSpec((tb, GKW), lambda i: (order(i), C_GQ // GKW)),
            pl.BlockSpec((tb, GKW), lambda i: (order(i), C_GK // GKW)),
            pl.BlockSpec((tb, GVW), lambda i: (order(i), C_GV // GVW)),
            pl.BlockSpec((tb, GKW), lambda i: (order(i), 0))]


GLA_BLOCK_CHUNKS = 2


def _gla_fwd(name, p, la, st0, rev):
    s = p.shape[0]
    tb = GLA_BLOCK_CHUNKS * GLA_CHUNK
    nblk = s // tb
    order = (lambda i: nblk - 1 - i) if rev else (lambda i: i)
    masks = _gla_masks()

    def body(q_ref, k_ref, v_ref, la_ref, st0_ref, hk_ref, hv_ref, bd_ref, o_ref, sts_ref, st_ref):
        @pl.when(pl.program_id(0) == 0)
        def _():
            st_ref[...] = st0_ref[...]

        st = st_ref[...]
        sts_ref[0] = st
        chunks = range(GLA_BLOCK_CHUNKS)
        for ci in (reversed(chunks) if rev else chunks):
            rows = slice(ci * GLA_CHUNK, (ci + 1) * GLA_CHUNK)
            o, st = _f_gla_chunk(q_ref[rows, :], k_ref[rows, :], v_ref[rows, :], la_ref[rows, :], st,
                                 hk_ref[...], hv_ref[...], bd_ref[...], rev)
            o_ref[rows, :] = o
        st_ref[...] = st

    full = lambda a: pl.BlockSpec(a.shape, lambda i: (0,) * a.ndim)
    return pl.pallas_call(
        body, name=name, grid=(nblk,),
        in_specs=_gla_specs(s, tb, order) + [full(st0)] + [full(m) for m in masks],
        out_specs=[pl.BlockSpec((tb, GVW), lambda i: (order(i), 0)),
                   pl.BlockSpec((1, GVW, GKW), lambda i: (order(i), 0, 0))],
        out_shape=[jax.ShapeDtypeStruct((s, GVW), F32), jax.ShapeDtypeStruct((nblk, GVW, GKW), F32)],
        scratch_shapes=[pltpu.VMEM((GVW, GKW), F32)],
        compiler_params=_cp("arbitrary"),
    )(p, p, p, la, st0, *masks)


def _gla_bwd(name, p, la, sts, do, prev, rev):
    s = p.shape[0]
    tb = GLA_BLOCK_CHUNKS * GLA_CHUNK
    nblk = s // tb
    order = (lambda i: i) if rev else (lambda i: nblk - 1 - i)
    masks = _gla_masks()
    n_prev = 0 if prev is None else 3

    def body(*refs):
        q_ref, k_ref, v_ref, la_ref, sts_ref, do_ref, hk_ref, hv_ref, bd_ref = refs[:9]
        prev_refs = refs[9:9 + n_prev]
        dq_ref, dk_ref, dv_ref, dla_ref, dst0_ref, dst_ref = refs[9 + n_prev:]

        @pl.when(pl.program_id(0) == 0)
        def _():
            dst_ref[...] = jnp.zeros_like(dst_ref)

        def block(q, k, v, la, st):
            outs = [None] * GLA_BLOCK_CHUNKS
            chunks = range(GLA_BLOCK_CHUNKS)
            for ci in (reversed(chunks) if rev else chunks):
                rows = slice(ci * GLA_CHUNK, (ci + 1) * GLA_CHUNK)
                outs[ci], st = _f_gla_chunk(q[ci], k[ci], v[ci], la[ci], st, hk_ref[...], hv_ref[...], bd_ref[...],
                                            rev)
            return tuple(outs), st

        split = lambda r: tuple(r[ci * GLA_CHUNK:(ci + 1) * GLA_CHUNK, :].astype(F32)
                                for ci in range(GLA_BLOCK_CHUNKS))
        _, vjp = jax.vjp(block, split(q_ref), split(k_ref), split(v_ref), split(la_ref), sts_ref[0])
        dq, dk, dv, dla, dst = vjp((split(do_ref), dst_ref[...]))
        for ci in range(GLA_BLOCK_CHUNKS):
            rows = slice(ci * GLA_CHUNK, (ci + 1) * GLA_CHUNK)
            if n_prev:
                dq_ref[rows, :] = dq[ci] + prev_refs[0][rows, :]
                dk_ref[rows, :] = dk[ci] + prev_refs[1][rows, :]
                dv_ref[rows, :] = dv[ci] + prev_refs[2][rows, :]
            else:
                dq_ref[rows, :], dk_ref[rows, :], dv_ref[rows, :] = dq[ci], dk[ci], dv[ci]
            dla_ref[rows, :] = dla[ci]
        dst_ref[...] = dst
        dst0_ref[...] = dst

    full = lambda a: pl.BlockSpec(a.shape, lambda i: (0,) * a.ndim)
    blk = lambda w: pl.BlockSpec((tb, w), lambda i: (order(i), 0))
    prev_specs = [blk(GKW), blk(GKW), blk(GVW)] if n_prev else []
    return pl.pallas_call(
        body, name=name, grid=(nblk,),
        in_specs=_gla_specs(s, tb, order) + [pl.BlockSpec((1, GVW, GKW), lambda i: (order(i), 0, 0)), blk(GVW)]
        + [full(m) for m in masks] + prev_specs,
        out_specs=[blk(GKW), blk(GKW), blk(GVW), blk(GKW), pl.BlockSpec((GVW, GKW), lambda i: (0, 0))],
        out_shape=[jax.ShapeDtypeStruct((s, GKW), F32), jax.ShapeDtypeStruct((s, GKW), F32),
                   jax.ShapeDtypeStruct((s, GVW), F32), jax.ShapeDtypeStruct((s, GKW), F32),
                   jax.ShapeDtypeStruct((GVW, GKW), F32)],
        scratch_shapes=[pltpu.VMEM((GVW, GKW), F32)],
        compiler_params=_cp("arbitrary"),
    )(p, p, p, la, sts, do, *masks, *(prev or ()))


def _f_ctx_state(k, v, la_f, la_b, bd_t):
    c = k.shape[0]
    after = _nn_hi(_tri(c, True, strict=True).astype(F32), la_f)
    before = _nn_hi(_tri(c, False, strict=True).astype(F32), la_b)
    return bd_t * _tn(v, k * jnp.exp(after)), bd_t * _tn(v, k * jnp.exp(before))


def _ctx_state(pc, la_f, la_b):
    c = pc.shape[0]
    bd_t = _gla_masks()[2]

    def body(k_ref, v_ref, lf_ref, lb_ref, bd_ref, sf_ref, sb_ref):
        sf_ref[...], sb_ref[...] = _f_ctx_state(k_ref[...], v_ref[...], lf_ref[...], lb_ref[...], bd_ref[...])

    full = lambda a: pl.BlockSpec(a.shape, lambda i: (0, 0))
    return pl.pallas_call(
        body, name="ctx_state_fwd", grid=(1,),
        in_specs=[pl.BlockSpec((c, GKW), lambda i: (0, C_GK // GKW)), pl.BlockSpec((c, GVW), lambda i: (0, C_GV // GVW)),
                  full(la_f), full(la_b), full(bd_t)],
        out_specs=[pl.BlockSpec((GVW, GKW), lambda i: (0, 0))] * 2,
        out_shape=[jax.ShapeDtypeStruct((GVW, GKW), F32)] * 2,
        compiler_params=_cp("arbitrary"),
    )(pc, pc, la_f, la_b, bd_t)


def _ctx_state_bwd(pc, la_f, la_b, dsf, dsb):
    c = pc.shape[0]
    bd_t = _gla_masks()[2]

    def body(k_ref, v_ref, lf_ref, lb_ref, bd_ref, dsf_ref, dsb_ref, dk_ref, dv_ref, dlf_ref, dlb_ref):
        _, vjp = jax.vjp(lambda k, v, lf, lb: _f_ctx_state(k, v, lf, lb, bd_ref[...]),
                         k_ref[...], v_ref[...], lf_ref[...], lb_ref[...])
        dk, dv, dlf, dlb = vjp((dsf_ref[...], dsb_ref[...]))
        dk_ref[...], dv_ref[...] = dk.astype(BF16), dv.astype(BF16)
        dlf_ref[...], dlb_ref[...] = dlf, dlb

    full = lambda a: pl.BlockSpec(a.shape, lambda i: (0, 0))
    return pl.pallas_call(
        body, name="ctx_state_bwd", grid=(1,),
        in_specs=[pl.BlockSpec((c, GKW), lambda i: (0, C_GK // GKW)), pl.BlockSpec((c, GVW), lambda i: (0, C_GV // GVW)),
                  full(la_f), full(la_b), full(bd_t), full(dsf), full(dsb)],
        out_specs=[pl.BlockSpec((c, GKW), lambda i: (0, 0)), pl.BlockSpec((c, GVW), lambda i: (0, 0)),
                   pl.BlockSpec((c, GKW), lambda i: (0, 0)), pl.BlockSpec((c, GKW), lambda i: (0, 0))],
        out_shape=[jax.ShapeDtypeStruct((c, GKW), BF16), jax.ShapeDtypeStruct((c, GVW), BF16),
                   jax.ShapeDtypeStruct((c, GKW), F32), jax.ShapeDtypeStruct((c, GKW), F32)],
        compiler_params=_cp("arbitrary"),
    )(pc, pc, la_f, la_b, bd_t, dsf, dsb)


_SRC_COLS = ((0, QW), (QW + 2 * KVW + 2 * GKW, GVW), (QW + 2 * KVW + 2 * GKW + GVW, GVW), (QW, KVW), (QW + KVW, KVW),
             (QW + 2 * KVW, GKW), (QW + 2 * KVW + GKW, GKW), (IN_COLS - 2 * GATE_RANK, 2 * GATE_RANK))
_DST_COLS = (C_Q, C_GV, C_GG, C_K, C_V, C_GQ, C_GK, C_Z)


def _pack_w_in(w_in):
    parts = [w_in[:, s:s + n] for s, n in _SRC_COLS]
    parts.append(jnp.zeros((w_in.shape[0], IN_PAD - C_Z - 2 * GATE_RANK), w_in.dtype))
    return jnp.concatenate(parts, axis=1)


def _unpack_w_in_grad(g):
    by_src = sorted(zip(_SRC_COLS, _DST_COLS))
    return jnp.concatenate([g[:, d:d + n] for (_, n), d in by_src], axis=1)


def _prep_weights(w_in, w_out, w_ffn_in, w_ffn_out, w_gate_fwd, w_gate_bwd):
    pad_rows = lambda w, at: jnp.zeros((LANES, GKW), F32).at[at:at + GATE_RANK].set(w)
    return {"w_in": _pack_w_in(w_in).astype(BF16), "w_out": w_out.astype(BF16), "w_ffn_in": w_ffn_in.astype(BF16),
            "w_ffn_out": w_ffn_out.astype(BF16), "wg_f": pad_rows(w_gate_fwd, 0), "wg_b": pad_rows(w_gate_bwd, GATE_RANK)}


def _local_step(x, ctx, target, ada, ada_c, w):
    s, d = x.shape
    sh1, sc1, gt1, sh2, sc2, gt2 = [ada[:, i * d:(i + 1) * d] for i in range(6)]
    sh1c, sc1c = ada_c[:, :d], ada_c[:, d:2 * d]
    cos, sin = _rope_tables(s)
    gt = jnp.tile(w["g_gla_norm"], (1, GLA_HEADS))

    h = _norm_mod("pre_mix", x, w["g_pre_mix"], sh1, sc1)
    hc = _norm_mod("pre_mix_ctx", ctx, w["g_pre_mix"], sh1c, sc1c)
    p = _mm("proj_in", h, w["w_in"], "nn")
    pc = _mm("proj_in_ctx", hc, w["w_in"], "nn")
    q_rot, k_rot, v_b = _rope_fwd("rope", p, cos, sin)
    pad = ((BLOCK, BLOCK), (0, 0))
    kp, vp = jnp.pad(k_rot, pad), jnp.pad(v_b, pad)
    kc, vc = pc[:, C_K:C_K + KVW].astype(BF16), pc[:, C_V:C_V + KVW].astype(BF16)
    attn = _attn_fwd(q_rot, kp, vp, kc, vc, w["attn_sink"])
    gate_w = (w["wg_f"], w["wg_b"], w["b_gate_fwd"], w["b_gate_bwd"])
    la_f, la_b = _gate_fwd("gate", p, *gate_w)
    la_fc, la_bc = _gate_fwd("gate_ctx", pc, *gate_w)
    st_f0, st_b0 = _ctx_state(pc, la_fc, la_bc)
    o_f, sts_f = _gla_fwd("gla_fwd_f", p, la_f, st_f0, False)
    o_b, sts_b = _gla_fwd("gla_fwd_b", p, la_b, st_b0, True)
    mix = _gla_out("gla_out", attn, o_f, o_b, p, gt)
    y = _mm("proj_out", mix, w["w_out"], "nn")
    x1 = _post_res("post_mix", x, y, w["g_post_mix"], gt1)
    h2 = _norm_mod("pre_ffn", x1, w["g_pre_ffn"], sh2, sc2)
    u = _mm("ffn_in", h2, w["w_ffn_in"], "nn")
    a = _swiglu("swiglu", u)
    f = _mm("ffn_out", a, w["w_ffn_out"], "nn")
    dx2, loss = _post_res_loss("post_ffn_loss", x1, f, w["g_post_ffn"], gt2, target)

    g = {}
    df, g["g_post_ffn"], dgt2 = _post_res_bwd("post_ffn_bwd", dx2, f, w["g_post_ffn"], gt2)
    da = _mm("ffn_out_dx", df, w["w_ffn_out"], "nt")
    g["w_ffn_out"] = _mm("ffn_out_dw", a, df, "tn")
    du = _swiglu_bwd("swiglu_bwd", da, u)
    dh2 = _mm("ffn_in_dx", du, w["w_ffn_in"], "nt")
    g["w_ffn_in"] = _mm("ffn_in_dw", h2, du, "tn")
    dx1, g["g_pre_ffn"], dsh2, dsc2 = _norm_mod_bwd("pre_ffn_bwd", dh2, dx2, x1, w["g_pre_ffn"], sh2, sc2)
    dy, g["g_post_mix"], dgt1 = _post_res_bwd("post_mix_bwd", dx1, y, w["g_post_mix"], gt1)
    dmix = _mm("proj_out_dx", dy, w["w_out"], "nt", BF16)
    g["w_out"] = _mm("proj_out_dw", mix, dy, "tn")
    d_o, dgg, dgt = _gla_out_bwd("gla_out_bwd", dmix, o_f, o_b, p, gt)
    g["g_gla_norm"] = jnp.sum(dgt.reshape(GLA_HEADS, GLA_DV), axis=0, keepdims=True)
    dgq, dgk, dgv, dla_f, dst_f0 = _gla_bwd("gla_bwd_f", p, la_f, sts_f, d_o, None, False)
    dgq, dgk, dgv, dla_b, dst_b0 = _gla_bwd("gla_bwd_b", p, la_b, sts_b, d_o, (dgq, dgk, dgv), True)
    dgkc, dgvc, dla_fc, dla_bc = _ctx_state_bwd(pc, la_fc, la_bc, dst_f0, dst_b0)
    dz, dwf, dwb, dbf, dbb = _gate_bwd("gate_bwd", p, dla_f, dla_b, *gate_w)
    dzc, dwfc, dwbc, dbfc, dbbc = _gate_bwd("gate_ctx_bwd", pc, dla_fc, dla_bc, *gate_w)
    g["w_gate_fwd"] = (dwf + dwfc)[:GATE_RANK]
    g["w_gate_bwd"] = (dwb + dwbc)[GATE_RANK:2 * GATE_RANK]
    g["b_gate_fwd"], g["b_gate_bwd"] = dbf + dbfc, dbb + dbbc
    dq_rot, dkp, dvp, dkc, dvc, g["attn_sink"] = _attn_bwd(dmix, q_rot, kp, vp, kc, vc, w["attn_sink"])
    dq, dk = _rope_bwd("rope_bwd", dq_rot, dkp[BLOCK:BLOCK + s], cos, sin)
    dp = jnp.concatenate([dq, dgv.astype(BF16), dgg, dk, dvp[BLOCK:BLOCK + s].astype(BF16), dgq.astype(BF16),
                          dgk.astype(BF16), dz], axis=1)
    c_rows = ctx.shape[0]
    zeros = lambda n: jnp.zeros((c_rows, n), BF16)
    dpc = jnp.concatenate([zeros(QW), dgvc, zeros(GVW), dkc.astype(BF16), dvc.astype(BF16), zeros(GKW), dgkc, dzc],
                          axis=1)
    dh = _mm("proj_in_dx", dp, w["w_in"], "nt")
    dhc = _mm("proj_in_ctx_dx", dpc, w["w_in"], "nt")
    g["w_in"] = _mm("proj_in_dw", jnp.concatenate([hc, h], axis=0), jnp.concatenate([dpc, dp], axis=0), "tn")
    dx, dg_a, dsh1, dsc1 = _norm_mod_bwd("pre_mix_bwd", dh, dx1, x, w["g_pre_mix"], sh1, sc1)
    _, dg_b, dsh1c, dsc1c = _norm_mod_bwd("pre_mix_ctx_bwd", dhc, jnp.zeros_like(dhc), ctx, w["g_pre_mix"], sh1c,
                                          sc1c)
    g["g_pre_mix"] = dg_a + dg_b
    d_ada = jnp.concatenate([dsh1, dsc1, dgt1, dsh2, dsc2, dgt2], axis=1)
    d_ada_c = jnp.concatenate([dsh1c, dsc1c, jnp.zeros((1, 4 * d), F32)], axis=1)
    return loss, dx, g, d_ada, d_ada_c


HBM = pl.BlockSpec(memory_space=pltpu.HBM)
N_DEV, N_CHIP = 8, 4


def _place():
    x, y, c = lax.axis_index("x"), lax.axis_index("y"), lax.axis_index("c")
    return x, y, c, [(1 - x, y), (x, 1 - y), (1 - x, 1 - y)]


def _row_tile(n, mult, cap):
    return max(t for t in range(mult, min(n, cap) + 1, mult) if n % t == 0)


def _ag_small(name, v):
    def body(v_ref, out_ref, send_sems, recv_sems):
        x, y, c, _ = _place()
        out_ref[4 * x + 2 * y + c] = v_ref[...]

        def peer(r):
            return ((1 - x) if r & 4 else x, (1 - y) if r & 2 else y, (1 - c) if r & 1 else c)

        def copy(r, block):
            px, py, pc = block
            return pltpu.make_async_remote_copy(
                src_ref=v_ref, dst_ref=out_ref.at[4 * px + 2 * py + pc], send_sem=send_sems.at[r - 1],
                recv_sem=recv_sems.at[r - 1], device_id=peer(r), device_id_type=MESH)

        sends = [copy(r, (x, y, c)) for r in range(1, N_DEV)]
        for cp in sends:
            cp.start()
        for r in range(1, N_DEV):
            copy(r, peer(r)).wait_recv()
        for cp in sends:
            cp.wait_send()

    return pl.pallas_call(
        body, name=name, out_shape=jax.ShapeDtypeStruct((N_DEV,) + v.shape, v.dtype),
        in_specs=[pl.BlockSpec(memory_space=pltpu.VMEM)], out_specs=pl.BlockSpec(memory_space=pltpu.VMEM),
        scratch_shapes=[pltpu.SemaphoreType.DMA((N_DEV - 1,)), pltpu.SemaphoreType.DMA((N_DEV - 1,))],
    )(v)


def _halves(c, rows, mult):
    hr = rows // 2
    return pl.ds(pl.multiple_of(c * hr, mult), hr), pl.ds(pl.multiple_of((1 - c) * hr, mult), hr)


def _ag_shards(name, shard):
    rows = shard.shape[0]

    def body(w_ref, out_ref, send_sems, recv_sems, local_sem):
        x, y, c, chips = _place()
        mine_half, other_half = _halves(c, rows, 16)
        me = 2 * x + y
        mine = pltpu.make_async_copy(w_ref, out_ref.at[me], local_sem)
        mine.start()

        def copy(k, src, chip, half, to):
            return pltpu.make_async_remote_copy(
                src_ref=src, dst_ref=out_ref.at[chip, half], send_sem=send_sems.at[k], recv_sem=recv_sems.at[k],
                device_id=to, device_id_type=MESH)

        first = [copy(j, w_ref.at[mine_half], me, mine_half, (px, py, c)) for j, (px, py) in enumerate(chips)]
        for cp in first:
            cp.start()
        passed = []
        for j, (px, py) in enumerate(chips):
            pk = 2 * px + py
            copy(j, w_ref.at[mine_half], pk, mine_half, (px, py, c)).wait_recv()
            cp = copy(3 + j, out_ref.at[pk, mine_half], pk, mine_half, (x, y, 1 - c))
            cp.start()
            passed.append(cp)
        for j, (px, py) in enumerate(chips):
            copy(3 + j, w_ref.at[mine_half], 2 * px + py, other_half, (x, y, 1 - c)).wait_recv()
        for cp in first + passed:
            cp.wait_send()
        mine.wait()

    return pl.pallas_call(
        body, name=name, out_shape=jax.ShapeDtypeStruct((N_CHIP,) + shard.shape, shard.dtype),
        in_specs=[HBM], out_specs=HBM,
        scratch_shapes=[pltpu.SemaphoreType.DMA((6,)), pltpu.SemaphoreType.DMA((6,)), pltpu.SemaphoreType.DMA],
    )(shard)


def _swap_half(name, g):
    n_sh, rows, n = g.shape

    def body(g_ref, a_ref, send_sem, recv_sem):
        x, y, c, _ = _place()
        _, other_half = _halves(c, rows, 8)
        cp = pltpu.make_async_remote_copy(
            src_ref=g_ref.at[pl.ds(0, n_sh), other_half], dst_ref=a_ref, send_sem=send_sem, recv_sem=recv_sem,
            device_id=(x, y, 1 - c), device_id_type=MESH)
        cp.start()
        cp.wait()

    return pl.pallas_call(
        body, name=name, out_shape=jax.ShapeDtypeStruct((n_sh, rows // 2, n), g.dtype), in_specs=[HBM], out_specs=HBM,
        scratch_shapes=[pltpu.SemaphoreType.DMA, pltpu.SemaphoreType.DMA],
    )(g)


def _add_half(name, g, a, c_idx):
    n_sh, hr, n = a.shape
    tr = _row_tile(hr, 16, 256)
    nb = hr // tr

    def body(c_ref, g_ref, a_ref, o_ref):
        o_ref[...] = (g_ref[...] + a_ref[...]).astype(o_ref.dtype)

    return pl.pallas_call(
        body, name=name, out_shape=jax.ShapeDtypeStruct(a.shape, BF16),
        grid_spec=pltpu.PrefetchScalarGridSpec(
            num_scalar_prefetch=1, grid=(n_sh, nb),
            in_specs=[pl.BlockSpec((1, tr, n), lambda s, i, c_ref: (s, c_ref[0] * nb + i, 0)),
                      pl.BlockSpec((1, tr, n), lambda s, i, c_ref: (s, i, 0))],
            out_specs=pl.BlockSpec((1, tr, n), lambda s, i, c_ref: (s, i, 0))),
        compiler_params=_cp("parallel", "parallel"),
    )(c_idx, g, a)


def _scatter_chips(name, h):
    def body(h_ref, b_ref, send_sems, recv_sems, local_sem):
        x, y, c, chips = _place()
        me = 2 * x + y
        mine = pltpu.make_async_copy(h_ref.at[me], b_ref.at[me], local_sem)
        mine.start()

        def copy(j, src_block, dst_block, to):
            return pltpu.make_async_remote_copy(
                src_ref=h_ref.at[src_block], dst_ref=b_ref.at[dst_block], send_sem=send_sems.at[j],
                recv_sem=recv_sems.at[j], device_id=to, device_id_type=MESH)

        sends = [copy(j, 2 * px + py, me, (px, py, c)) for j, (px, py) in enumerate(chips)]
        for cp in sends:
            cp.start()
        for j, (px, py) in enumerate(chips):
            copy(j, me, 2 * px + py, (px, py, c)).wait_recv()
        for cp in sends:
            cp.wait_send()
        mine.wait()

    return pl.pallas_call(
        body, name=name, out_shape=jax.ShapeDtypeStruct(h.shape, h.dtype), in_specs=[HBM], out_specs=HBM,
        scratch_shapes=[pltpu.SemaphoreType.DMA((3,)), pltpu.SemaphoreType.DMA((3,)), pltpu.SemaphoreType.DMA],
    )(h)


def _sum_chips(name, b):
    n_sh, hr, n = b.shape
    tr = _row_tile(hr, 16, 256)

    def body(b0, b1, b2, b3, o_ref):
        o_ref[...] = ((b0[0].astype(F32) + b1[0].astype(F32)) + b2[0].astype(F32)) + b3[0].astype(F32)

    return pl.pallas_call(
        body, name=name, grid=(hr // tr,), out_shape=jax.ShapeDtypeStruct((hr, n), F32),
        in_specs=[pl.BlockSpec((1, tr, n), functools.partial(lambda i, k: (k, i, 0), k=k)) for k in range(n_sh)],
        out_specs=pl.BlockSpec((tr, n), lambda i: (i, 0)), compiler_params=_cp("parallel"),
    )(b, b, b, b)


def _share_half(name, f):
    hr, n = f.shape

    def body(f_ref, out_ref, send_sem, recv_sem, local_sem):
        x, y, c, _ = _place()
        mine_half, other_half = _halves(c, 2 * hr, 8)
        mine = pltpu.make_async_copy(f_ref, out_ref.at[mine_half], local_sem)
        mine.start()

        def copy(half):
            return pltpu.make_async_remote_copy(
                src_ref=f_ref, dst_ref=out_ref.at[half], send_sem=send_sem, recv_sem=recv_sem,
                device_id=(x, y, 1 - c), device_id_type=MESH)

        send = copy(mine_half)
        send.start()
        copy(other_half).wait_recv()
        send.wait_send()
        mine.wait()

    return pl.pallas_call(
        body, name=name, out_shape=jax.ShapeDtypeStruct((2 * hr, n), f.dtype), in_specs=[HBM], out_specs=HBM,
        scratch_shapes=[pltpu.SemaphoreType.DMA, pltpu.SemaphoreType.DMA, pltpu.SemaphoreType.DMA],
    )(f)


def _reduce_shards(name, g, c_idx):
    a = _swap_half(name + "_swap", g)
    h = _add_half(name + "_pair", g, a, c_idx)
    b = _scatter_chips(name + "_scatter", h)
    f = _sum_chips(name + "_sum", b)
    return _share_half(name + "_share", f)


def _f_adamw(w, g, m, v):
    m = ADAM_B1 * m + (1.0 - ADAM_B1) * g
    v = ADAM_B2 * v + (1.0 - ADAM_B2) * (g * g)
    m_hat = m / (1.0 - ADAM_B1 ** ADAM_STEP)
    v_hat = v / (1.0 - ADAM_B2 ** ADAM_STEP)
    return -ADAM_LR * (m_hat / (jnp.sqrt(v_hat) + ADAM_EPS) + ADAM_WD * w), m, v


def _adamw(name, w, g, m, v):
    rows, n = w.shape
    return _rowwise(name, lambda w, g, m, v: (_f_adamw(w, g, m, v), ()), rows, [(t, n, 0) for t in (w, g, m, v)], [],
                    [(n, F32)] * 3, [], tm=_row_tile(rows, 8, 256))


def _pack_rows(parts):
    rows = []
    for t in parts:
        t = t.reshape(-1)
        rows.append(jnp.pad(t, (0, -t.shape[0] % LANES)).reshape(-1, LANES))
    out = jnp.concatenate(rows, axis=0)
    return jnp.pad(out, ((0, -out.shape[0] % 8), (0, 0)))


def _unpack_rows(packed, shapes):
    out, r = [], 0
    for shp in shapes:
        n = int(np.prod(shp))
        nr = -(-n // LANES)
        out.append(packed[r:r + nr].reshape(-1)[:n].reshape(shp))
        r += nr
    return out


def _sum_blocks(name, g):
    def body(g_ref, o_ref):
        acc = g_ref[0]
        for k in range(1, g.shape[0]):
            acc = acc + g_ref[k]
        o_ref[...] = acc

    return pl.pallas_call(body, name=name, out_shape=jax.ShapeDtypeStruct(g.shape[1:], F32))(g)


def _silu(t):
    return t * _sigmoid(t)


def _ada_fwd(cc, w_ada):
    n = w_ada.shape[1]
    tn = _row_tile(n, LANES, 512)

    def body(cc_ref, w_ref, o_ref):
        o_ref[...] = _nn(_silu(cc_ref[...]), w_ref[...])

    return pl.pallas_call(
        body, name="ada_fwd", grid=(n // tn,), out_shape=jax.ShapeDtypeStruct((cc.shape[0], n), F32),
        in_specs=[pl.BlockSpec(cc.shape, lambda j: (0, 0)), pl.BlockSpec((w_ada.shape[0], tn), lambda j: (0, j))],
        out_specs=pl.BlockSpec((cc.shape[0], tn), lambda j: (0, j)), compiler_params=_cp("parallel"),
    )(cc, w_ada)


def _ada_bwd(cc, dm, w_ada):
    d, n = w_ada.shape
    tn = _row_tile(n, LANES, 512)

    def body(cc_ref, dm_ref, w_ref, gw_ref, ds_ref):
        @pl.when(pl.program_id(0) == 0)
        def _():
            ds_ref[...] = jnp.zeros_like(ds_ref)

        gw_ref[...] = _raw_dot("tn", _silu(cc_ref[...]), dm_ref[...], True)
        ds_ref[...] += _raw_dot("nt", dm_ref[...], w_ref[...], False)

    return pl.pallas_call(
        body, name="ada_bwd", grid=(n // tn,),
        out_shape=[jax.ShapeDtypeStruct((d, n), F32), jax.ShapeDtypeStruct(cc.shape, F32)],
        in_specs=[pl.BlockSpec(cc.shape, lambda j: (0, 0)), pl.BlockSpec((cc.shape[0], tn), lambda j: (0, j)),
                  pl.BlockSpec((d, tn), lambda j: (0, j))],
        out_specs=[pl.BlockSpec((d, tn), lambda j: (0, j)), pl.BlockSpec(cc.shape, lambda j: (0, 0))],
        compiler_params=_cp("arbitrary"),
    )(cc, dm, w_ada)


def _c_ctx_grad(parts, c_ctx):
    def body(p_ref, c_ref, o_ref):
        ds = ((p_ref[0] + p_ref[1]) + p_ref[2]) + p_ref[3]
        _, vjp = jax.vjp(_silu, c_ref[...])
        o_ref[...] = vjp(ds)[0]

    return pl.pallas_call(body, name="c_ctx_grad", out_shape=jax.ShapeDtypeStruct(c_ctx.shape, F32))(parts, c_ctx)


def kernel(x, c, ctx, c_ctx, w_ada, b_ada, g_pre_mix, g_post_mix, g_pre_ffn, g_post_ffn, w_in, attn_sink, w_gate_fwd, b_gate_fwd, w_gate_bwd, b_gate_bwd, g_gla_norm, w_out, w_ffn_in, w_ffn_out, loss_target, m_c_ctx, m_w_ada, m_b_ada, m_g_pre_mix, m_g_post_mix, m_g_pre_ffn, m_g_post_ffn, m_w_in, m_attn_sink, m_w_gate_fwd, m_b_gate_fwd, m_w_gate_bwd, m_b_gate_bwd, m_g_gla_norm, m_w_out, m_w_ffn_in, m_w_ffn_out, v_c_ctx, v_w_ada, v_b_ada, v_g_pre_mix, v_g_post_mix, v_g_pre_ffn, v_g_post_ffn, v_w_in, v_attn_sink, v_w_gate_fwd, v_b_gate_fwd, v_w_gate_bwd, v_b_gate_bwd, v_g_gla_norm, v_w_out, v_w_ffn_in, v_w_ffn_out):
    xi, yi, ci = lax.axis_index("x"), lax.axis_index("y"), lax.axis_index("c")
    dev, chip = 4 * xi + 2 * yi + ci, 2 * xi + yi
    c_idx = jnp.reshape(ci, (1,)).astype(jnp.int32)
    d = x.shape[-1]
    n_ada, n_in, n_f = w_ada.shape[-1], w_in.shape[-1], w_ffn_in.shape[-1]
    r_out, r_f = w_out.shape[1], w_ffn_out.shape[1]
    n_gate = w_gate_fwd.shape[-1]
    by_chip = lambda t: t[0::2]

    rc = -(-d // LANES)
    g1 = _ag_small("gather_cond", _pack_rows([c[0], w_gate_fwd[0], w_gate_bwd[0]]))
    c_all = g1[:, :rc].reshape(N_DEV, -1)[:, :d]
    gr = GATE_RANK * n_gate // LANES
    gate_full = lambda off: jnp.transpose(by_chip(g1)[:, off:off + gr].reshape(N_CHIP, GATE_RANK, n_gate),
                                          (1, 0, 2)).reshape(GATE_RANK, N_CHIP * n_gate)
    wgf, wgb = gate_full(rc), gate_full(rc + gr)
    cc = jnp.concatenate([c_all, c_ctx[None, :], jnp.zeros((7, d), F32)], axis=0)

    g2 = _ag_small("gather_ada", _ada_fwd(cc, w_ada[0]).reshape(-1, LANES))
    ada_all = jnp.transpose(by_chip(g2).reshape(N_CHIP, 16, n_ada), (1, 0, 2)).reshape(16, N_CHIP * n_ada) + b_ada
    ada = lax.dynamic_slice(ada_all, (dev, 0), (1, N_CHIP * n_ada))
    ada_c = ada_all[N_DEV:N_DEV + 1]

    colg = _ag_shards("gather_cols", jnp.concatenate([w_in[0], w_ffn_in[0]], axis=1).astype(BF16))
    rowg = _ag_shards("gather_rows", jnp.concatenate([w_out[0], w_ffn_out[0]], axis=0).astype(BF16))
    w = _prep_weights(jnp.concatenate([colg[k, :, :n_in] for k in range(N_CHIP)], axis=1),
                      jnp.concatenate([rowg[k, :r_out] for k in range(N_CHIP)], axis=0),
                      jnp.concatenate([colg[k, :, n_in:] for k in range(N_CHIP)], axis=1),
                      jnp.concatenate([rowg[k, r_out:] for k in range(N_CHIP)], axis=0), wgf, wgb)
    w.update(g_pre_mix=g_pre_mix, g_post_mix=g_post_mix, g_pre_ffn=g_pre_ffn, g_post_ffn=g_post_ffn,
             attn_sink=attn_sink, b_gate_fwd=b_gate_fwd, b_gate_bwd=b_gate_bwd, g_gla_norm=g_gla_norm)

    loss_lanes, grad_x, g, d_ada, d_ada_c = _local_step(x[0], ctx[0], loss_target[0], ada, ada_c, w)

    small = ("g_pre_mix", "g_post_mix", "g_pre_ffn", "g_post_ffn", "attn_sink", "b_gate_fwd", "b_gate_bwd",
             "g_gla_norm", "w_gate_fwd", "w_gate_bwd")
    shapes = [(1, 6 * d)] * 2 + [g[n].shape for n in small]
    g3 = _ag_small("gather_small_grads", _pack_rows([d_ada, d_ada_c] + [g[n] for n in small]))
    tot = dict(zip(("d_ada", "d_ada_c") + small, _unpack_rows(_sum_blocks("sum_small_grads", g3), shapes)))
    r_ada = 6 * d // LANES
    dm = jnp.concatenate([g3[:, :r_ada].reshape(N_DEV, 6 * d), tot["d_ada_c"], jnp.zeros((7, 6 * d), F32)], axis=0)
    grads = {n: tot[n] for n in small[:8]}
    grads["b_ada"] = _sum_blocks("sum_b_ada", dm.reshape(16, r_ada, LANES)).reshape(1, 6 * d)
    grads["w_gate_fwd"] = lax.dynamic_slice(tot["w_gate_fwd"], (0, chip * n_gate), (GATE_RANK, n_gate))[None]
    grads["w_gate_bwd"] = lax.dynamic_slice(tot["w_gate_bwd"], (0, chip * n_gate), (GATE_RANK, n_gate))[None]
    gw_ada, dsc = _ada_bwd(cc, lax.dynamic_slice(dm, (0, chip * n_ada), (16, n_ada)), w_ada[0])
    grads["w_ada"] = gw_ada[None]
    g4 = _ag_small("gather_c_ctx", _pack_rows([dsc[N_DEV]]))
    grads["c_ctx"] = _c_ctx_grad(by_chip(g4), _pack_rows([c_ctx])).reshape(-1)[:d]

    g_w_in = _unpack_w_in_grad(g["w_in"])
    gcol = jnp.stack([jnp.concatenate([g_w_in[:, k * n_in:(k + 1) * n_in], g["w_ffn_in"][:, k * n_f:(k + 1) * n_f]],
                                      axis=1) for k in range(N_CHIP)])
    grow = jnp.stack([jnp.concatenate([g["w_out"][k * r_out:(k + 1) * r_out], g["w_ffn_out"][k * r_f:(k + 1) * r_f]],
                                      axis=0) for k in range(N_CHIP)])
    rcol = _reduce_shards("reduce_cols", gcol, c_idx)
    rrow = _reduce_shards("reduce_rows", grow, c_idx)
    grads["w_in"], grads["w_ffn_in"] = rcol[None, :, :n_in], rcol[None, :, n_in:]
    grads["w_out"], grads["w_ffn_out"] = rrow[None, :r_out], rrow[None, r_out:]

    names = ("c_ctx", "w_ada", "b_ada", "g_pre_mix", "g_post_mix", "g_pre_ffn", "g_post_ffn", "w_in", "attn_sink",
             "w_gate_fwd", "b_gate_fwd", "w_gate_bwd", "b_gate_bwd", "g_gla_norm", "w_out", "w_ffn_in", "w_ffn_out")
    weights = dict(zip(names, (c_ctx, w_ada, b_ada, g_pre_mix, g_post_mix, g_pre_ffn, g_post_ffn, w_in, attn_sink,
                               w_gate_fwd, b_gate_fwd, w_gate_bwd, b_gate_bwd, g_gla_norm, w_out, w_ffn_in,
                               w_ffn_out)))
    m_in = dict(zip(names, (m_c_ctx, m_w_ada, m_b_ada, m_g_pre_mix, m_g_post_mix, m_g_pre_ffn, m_g_post_ffn, m_w_in,
                            m_attn_sink, m_w_gate_fwd, m_b_gate_fwd, m_w_gate_bwd, m_b_gate_bwd, m_g_gla_norm,
                            m_w_out, m_w_ffn_in, m_w_ffn_out)))
    v_in = dict(zip(names, (v_c_ctx, v_w_ada, v_b_ada, v_g_pre_mix, v_g_post_mix, v_g_pre_ffn, v_g_post_ffn, v_w_in,
                            v_attn_sink, v_w_gate_fwd, v_b_gate_fwd, v_w_gate_bwd, v_b_gate_bwd, v_g_gla_norm,
                            v_w_out, v_w_ffn_in, v_w_ffn_out)))
    large = ("w_ada", "w_in", "w_out", "w_ffn_in", "w_ffn_out")
    tiny = tuple(n for n in names if n not in large)
    delta, new_m, new_v = {}, {}, {}
    for n in large:
        dl, nm, nv = _adamw("adamw_" + n, weights[n][0], grads[n][0], m_in[n][0], v_in[n][0])
        delta[n], new_m[n], new_v[n] = dl[None], nm[None], nv[None]
    tiny_shapes = [weights[n].shape for n in tiny]
    packed = [_pack_rows([t[n] for n in tiny]) for t in (weights, grads, m_in, v_in)]
    for out, res in zip((delta, new_m, new_v), _adamw("adamw_small", *packed)):
        out.update(zip(tiny, _unpack_rows(res, tiny_shapes)))
    for n in tiny:
        grads[n] = grads[n].reshape(weights[n].shape)

    loss = lax.psum(loss_lanes[0, 0], ("x", "y", "c"))
    return (loss, grad_x[None], *[grads[n] for n in names], *[delta[n] for n in names], *[new_m[n] for n in names],
            *[new_v[n] for n in names])
```

```python
import functools

import jax
import jax.numpy as jnp
import numpy as np
from jax import lax
from jax.experimental import pallas as pl
from jax.experimental.pallas import tpu as pltpu

F32 = jnp.float32
BF16 = jnp.bfloat16
MESH = pl.DeviceIdType.MESH

HEAD_DIM = 64
ATT_HEADS = 8
ATT_KV_HEADS = 2
ATT_GROUP = ATT_HEADS // ATT_KV_HEADS
WINDOW = 128
BLOCK = 128
GRID_W = 64
ROPE_BASE = 10000.0
GLA_HEADS = 8
GLA_DK = 32
GLA_DV = 64
GLA_CHUNK = 64
GATE_RANK = 16
GATE_TAU = 16.0
NEG_INF = -1e30
QW = ATT_HEADS * HEAD_DIM
KVW = ATT_KV_HEADS * HEAD_DIM
GKW = GLA_HEADS * GLA_DK
GVW = GLA_HEADS * GLA_DV
IN_COLS = QW + 2 * KVW + 2 * GKW + 2 * GVW + 2 * GATE_RANK
LANES = 128
IN_PAD = IN_COLS + LANES - 2 * GATE_RANK
C_Q, C_GV, C_GG = 0, QW, QW + GVW
C_K = C_GG + GVW
C_V = C_K + KVW
C_GQ = C_V + KVW
C_GK = C_GQ + GKW
C_Z = C_GK + GKW
MIX = QW + GVW

ADAM_LR, ADAM_B1, ADAM_B2, ADAM_EPS, ADAM_WD, ADAM_STEP = 0.001, 0.9, 0.999, 1e-08, 0.01, 10

VMEM_LIMIT = 56 * 1024 * 1024


def _cp(*sem):
    return pltpu.CompilerParams(dimension_semantics=sem, vmem_limit_bytes=VMEM_LIMIT)


def _pick(n, cands):
    for t in cands:
        if n % t == 0:
            return t
    return n


_DIMS = {"nn": (((1,), (0,)), ((), ())), "nt": (((1,), (1,)), ((), ())), "tn": (((0,), (0,)), ((), ()))}


def _raw_dot(mode, a, b, hi):
    if hi:
        return lax.dot_general(a.astype(F32), b.astype(F32), _DIMS[mode], precision=lax.Precision.HIGHEST,
                               preferred_element_type=F32)
    return lax.dot_general(a.astype(BF16), b.astype(BF16), _DIMS[mode], preferred_element_type=F32)


def _make_dot(mode, hi):
    @jax.custom_vjp
    def dot(a, b):
        return _raw_dot(mode, a, b, hi)

    def fwd(a, b):
        return _raw_dot(mode, a, b, hi), (a, b)

    def bwd(res, dc):
        a, b = res
        if mode == "nn":
            return _raw_dot("nt", dc, b, hi), _raw_dot("tn", a, dc, hi)
        if mode == "nt":
            return _raw_dot("nn", dc, b, hi), _raw_dot("tn", dc, a, hi)
        return _raw_dot("nt", b, dc, hi), _raw_dot("nn", a, dc, hi)

    dot.defvjp(fwd, bwd)
    return dot


_nn, _nt, _tn = _make_dot("nn", False), _make_dot("nt", False), _make_dot("tn", False)
_nn_hi = _make_dot("nn", True)


MM_VMEM_BUDGET = 44 * 1024 * 1024


def _halvings(n):
    out = [n]
    while out[-1] % (2 * LANES) == 0:
        out.append(out[-1] // 2)
    return out


def _mm_tiles(mode, m, n, k, a_bytes, b_bytes, o_bytes):
    tms = [t for t in dict.fromkeys((m, m // 2, 2048, 1024, 512, 256, 128))
           if m % t == 0 and t % (LANES if mode == "tn" else 16) == 0 and t <= 4096] or [m]
    tks = ([t for t in (512, 256, 128) if k % t == 0] or [k]) if mode == "tn" else _halvings(k)
    for tn in _halvings(n):
        for tk in tks:
            for tm in tms:
                acc = tm * tn * 4 if (k // tk > 1 and o_bytes != 4) else 0
                if 2 * (tm * tk * a_bytes + tk * tn * b_bytes + tm * tn * o_bytes) + acc <= MM_VMEM_BUDGET:
                    return tm, tn, tk
    return tms[-1], _halvings(n)[-1], tks[-1]


def _mm(name, a, b, mode, out_dtype=F32):
    if mode == "nn":
        (m, k), n = a.shape, b.shape[1]
    elif mode == "nt":
        (m, k), n = a.shape, b.shape[0]
    else:
        (k, m), n = a.shape, b.shape[1]
    tm, tn, tk = _mm_tiles(mode, m, n, k, a.dtype.itemsize, b.dtype.itemsize, jnp.dtype(out_dtype).itemsize)
    nk = k // tk
    use_acc = nk > 1 and out_dtype != F32

    def body(a_ref, b_ref, o_ref, *acc):
        part = _raw_dot(mode, a_ref[...], b_ref[...], False)
        if nk == 1:
            o_ref[...] = part.astype(o_ref.dtype)
            return
        acc_ref = acc[0] if use_acc else o_ref
        kk = pl.program_id(2)

        @pl.when(kk == 0)
        def _():
            acc_ref[...] = part

        @pl.when(kk > 0)
        def _():
            acc_ref[...] += part

        if use_acc:
            @pl.when(kk == nk - 1)
            def _():
                o_ref[...] = acc_ref[...].astype(o_ref.dtype)

    if mode == "nn":
        a_spec = pl.BlockSpec((tm, tk), lambda i, j, kk: (i, kk))
        b_spec = pl.BlockSpec((tk, tn), lambda i, j, kk: (kk, j))
    elif mode == "nt":
        a_spec = pl.BlockSpec((tm, tk), lambda i, j, kk: (i, kk))
        b_spec = pl.BlockSpec((tn, tk), lambda i, j, kk: (j, kk))
    else:
        a_spec = pl.BlockSpec((tk, tm), lambda i, j, kk: (kk, i))
        b_spec = pl.BlockSpec((tk, tn), lambda i, j, kk: (kk, j))
    return pl.pallas_call(
        body, name=name, grid=(m // tm, n // tn, nk),
        in_specs=[a_spec, b_spec], out_specs=pl.BlockSpec((tm, tn), lambda i, j, kk: (i, j)),
        out_shape=jax.ShapeDtypeStruct((m, n), out_dtype),
        scratch_shapes=[pltpu.VMEM((tm, tn), F32)] if use_acc else [],
        compiler_params=_cp("parallel", "parallel", "arbitrary"),
    )(a, b)


def _rowwise(name, fn, rows, row_ins, full_ins, row_outs, acc_outs, tm=None):
    tm = tm or _pick(rows, (512, 256, 128))
    n_r, n_f, n_o, n_a = len(row_ins), len(full_ins), len(row_outs), len(acc_outs)

    def body(*refs):
        ins, outs = refs[:n_r + n_f], refs[n_r + n_f:]
        vals = [r[...].astype(F32) for r in ins]
        ro, ao = fn(*vals)
        for r, val in zip(outs[:n_o], ro):
            r[...] = val.astype(r.dtype)
        if n_a:
            @pl.when(pl.program_id(0) == 0)
            def _():
                for r in outs[n_o:]:
                    r[...] = jnp.zeros_like(r)

            for r, val in zip(outs[n_o:], ao):
                r[...] += val

    in_specs = [pl.BlockSpec((tm, w), functools.partial(lambda i, cb: (i, cb), cb=cb)) for _, w, cb in row_ins]
    in_specs += [pl.BlockSpec(a.shape, lambda i: (0, 0)) for a in full_ins]
    out_specs = [pl.BlockSpec((tm, w), lambda i: (i, 0)) for w, _ in row_outs]
    out_specs += [pl.BlockSpec(s, lambda i: (0, 0)) for s in acc_outs]
    out_shape = [jax.ShapeDtypeStruct((rows, w), dt) for w, dt in row_outs]
    out_shape += [jax.ShapeDtypeStruct(s, F32) for s in acc_outs]
    return pl.pallas_call(
        body, name=name, grid=(rows // tm,), in_specs=in_specs, out_specs=out_specs, out_shape=out_shape,
        compiler_params=_cp("arbitrary" if n_a else "parallel"),
    )(*[a for a, _, _ in row_ins], *full_ins)


def _rn(x):
    return x * lax.rsqrt(jnp.mean(x * x, axis=-1, keepdims=True) + 1e-6)


def _sigmoid(t):
    return 1.0 / (1.0 + jnp.exp(-t))


def _f_norm_mod(x, g, sh, sc):
    return _rn(x) * g * (1.0 + sc) + sh


def _f_post_res(xr, y, g, gate):
    return xr + gate * (_rn(y) * g)


def _f_swiglu(g, u):
    return g * _sigmoid(g) * u


def _logsig(u):
    return jnp.minimum(u, 0.0) - jnp.log(1.0 + jnp.exp(-jnp.abs(u)))


def _f_gate(z, wf, wb, bf, bb):
    return _logsig(_nn(z, wf) + bf) / GATE_TAU, _logsig(_nn(z, wb) + bb) / GATE_TAU


def _f_gla_out(of, ob, gg, gt, bd):
    o = of + ob
    ms = _nn_hi(o * o, bd)
    return o * lax.rsqrt(ms + 1e-6) * gt * (gg * _sigmoid(gg))


def _norm_mod(name, x, g, sh, sc):
    rows, d = x.shape
    return _rowwise(name, lambda x, g, sh, sc: ((_f_norm_mod(x, g, sh, sc),), ()), rows,
                    [(x, d, 0)], [g, sh, sc], [(d, BF16)], [])[0]


def _norm_mod_bwd(name, dh, dres, x, g, sh, sc):
    rows, d = x.shape

    def fn(dh, dres, x, g, sh, sc):
        _, vjp = jax.vjp(_f_norm_mod, x, g, sh, sc)
        dx, dg, dsh, dsc = vjp(dh)
        return (dx + dres,), (dg, dsh, dsc)

    return _rowwise(name, fn, rows, [(dh, d, 0), (dres, d, 0), (x, d, 0)], [g, sh, sc], [(d, F32)],
                    [(1, d)] * 3)


def _post_res(name, xr, y, g, gate):
    rows, d = xr.shape
    return _rowwise(name, lambda xr, y, g, gate: ((_f_post_res(xr, y, g, gate),), ()), rows,
                    [(xr, d, 0), (y, d, 0)], [g, gate], [(d, F32)], [])[0]


def _post_res_bwd(name, dxo, y, g, gate):
    rows, d = y.shape

    def fn(dxo, y, g, gate):
        _, vjp = jax.vjp(lambda y, g, gate: _f_post_res(jnp.zeros_like(y), y, g, gate), y, g, gate)
        dy, dg, dgate = vjp(dxo)
        return (dy,), (dg, dgate)

    return _rowwise(name, fn, rows, [(dxo, d, 0), (y, d, 0)], [g, gate], [(d, BF16)], [(1, d)] * 2)


def _post_res_loss(name, xr, y, g, gate, target):
    rows, d = xr.shape

    def fn(xr, y, target, g, gate):
        diff = _f_post_res(xr, y, g, gate) - target
        part = 0.5 * jnp.sum(jnp.mean(diff * diff, axis=-1, keepdims=True), axis=0, keepdims=True)
        return (diff * (1.0 / d),), (jnp.broadcast_to(part, (1, LANES)),)

    return _rowwise(name, fn, rows, [(xr, d, 0), (y, d, 0), (target, d, 0)], [g, gate], [(d, F32)], [(1, LANES)])


def _swiglu(name, u):
    rows, f2 = u.shape
    f = f2 // 2
    return _rowwise(name, lambda g, u: ((_f_swiglu(g, u),), ()), rows, [(u, f, 0), (u, f, 1)], [], [(f, BF16)], [],
                    tm=_pick(rows, (256, 128)))[0]


def _swiglu_bwd(name, da, u):
    rows, f2 = u.shape
    f = f2 // 2

    def fn(da, g, u):
        _, vjp = jax.vjp(_f_swiglu, g, u)
        return (jnp.concatenate(vjp(da), axis=1),), ()

    return _rowwise(name, fn, rows, [(da, f, 0), (u, f, 0), (u, f, 1)], [], [(f2, BF16)], [],
                    tm=_pick(rows, (256, 128)))[0]


def _gate_fwd(name, p, wf, wb, bf, bb):
    rows = p.shape[0]
    return _rowwise(name, lambda z, wf, wb, bf, bb: (_f_gate(z, wf, wb, bf, bb), ()), rows,
                    [(p, LANES, C_Z // LANES)], [wf, wb, bf, bb], [(GKW, F32)] * 2, [])


def _gate_bwd(name, p, dla_f, dla_b, wf, wb, bf, bb):
    rows = p.shape[0]

    def fn(z, dlf, dlb, wf, wb, bf, bb):
        _, vjp = jax.vjp(_f_gate, z, wf, wb, bf, bb)
        dz, dwf, dwb, dbf, dbb = vjp((dlf, dlb))
        return (dz,), (dwf, dwb, dbf, dbb)

    return _rowwise(name, fn, rows, [(p, LANES, C_Z // LANES), (dla_f, GKW, 0), (dla_b, GKW, 0)],
                    [wf, wb, bf, bb], [(LANES, BF16)], [(LANES, GKW), (LANES, GKW), (1, GKW), (1, GKW)])


def _head_mean_matrix():
    h = np.arange(GVW) // GLA_DV
    return jnp.asarray((h[:, None] == h[None, :]).astype(np.float32) / GLA_DV)


def _gla_out(name, attn, of, ob, p, gt):
    rows = of.shape[0]
    bd = _head_mean_matrix()
    fn = lambda attn, of, ob, gg, gt, bd: ((jnp.concatenate([attn, _f_gla_out(of, ob, gg, gt, bd)], axis=1),), ())
    return _rowwise(name, fn, rows, [(attn, QW, 0), (of, GVW, 0), (ob, GVW, 0), (p, GVW, C_GG // GVW)], [gt, bd],
                    [(MIX, BF16)], [])[0]


def _gla_out_bwd(name, dmix, of, ob, p, gt):
    rows = of.shape[0]
    bd = _head_mean_matrix()

    def fn(dm, of, ob, gg, gt, bd):
        _, vjp = jax.vjp(lambda of, gg, gt: _f_gla_out(of, ob, gg, gt, bd), of, gg, gt)
        do, dgg, dgt = vjp(dm)
        return (do, dgg), (dgt,)

    return _rowwise(name, fn, rows, [(dmix, GVW, 1), (of, GVW, 0), (ob, GVW, 0), (p, GVW, C_GG // GVW)], [gt, bd],
                    [(GVW, F32), (GVW, BF16)], [(1, GVW)])


def _rope_tables(n_tokens):
    t = jnp.arange(n_tokens)
    row = (t // GRID_W).astype(F32)
    col = (t % GRID_W).astype(F32)
    half = HEAD_DIM // 2
    inv_freq = ROPE_BASE ** (-jnp.arange(0, half, 2, dtype=F32) / half)
    ang_r = row[:, None] * inv_freq[None, :]
    ang_c = col[:, None] * inv_freq[None, :]
    ang = jnp.concatenate([ang_r, ang_r, ang_c, ang_c], axis=-1)
    sign = jnp.concatenate([-jnp.ones((16,), F32), jnp.ones((16,), F32)] * 2)
    cos, sin = jnp.cos(ang), jnp.sin(ang) * sign[None, :]
    return jnp.tile(cos, (1, 2)), jnp.tile(sin, (1, 2))


def _rot_pairs(x):
    w = x.shape[-1]
    lane = lax.broadcasted_iota(jnp.int32, x.shape, x.ndim - 1)
    return jnp.where((lane % 32) < 16, pltpu.roll(x, w - 16, x.ndim - 1), pltpu.roll(x, 16, x.ndim - 1))


def _rope_apply(x, cos, sin_signed, inverse):
    reps = x.shape[-1] // LANES
    cos = jnp.concatenate([cos] * reps, axis=-1) if reps > 1 else cos
    sin = jnp.concatenate([sin_signed] * reps, axis=-1) if reps > 1 else sin_signed
    if inverse:
        return x * cos + _rot_pairs(x * sin)
    return x * cos + _rot_pairs(x) * sin


def _rope_fwd(name, p, cos, sin):
    rows = p.shape[0]

    def fn(q, k, v, cos, sin):
        return (_rope_apply(q, cos, sin, False), _rope_apply(k, cos, sin, False), v), ()

    return _rowwise(name, fn, rows, [(p, QW, 0), (p, KVW, C_K // KVW), (p, KVW, C_V // KVW), (cos, LANES, 0),
                                     (sin, LANES, 0)], [], [(QW, BF16), (KVW, BF16), (KVW, BF16)], [])


def _rope_bwd(name, dq, dk, cos, sin):
    rows = dq.shape[0]

    def fn(dq, dk, cos, sin):
        return (_rope_apply(dq, cos, sin, True), _rope_apply(dk, cos, sin, True)), ()

    return _rowwise(name, fn, rows, [(dq, QW, 0), (dk, KVW, 0), (cos, LANES, 0), (sin, LANES, 0)], [],
                    [(QW, BF16), (KVW, BF16)], [])


def _f_attn(qs, kws, vws, kcs, vcs, sink, n, n_tokens):
    i = lax.broadcasted_iota(jnp.int32, (BLOCK, 3 * BLOCK), 0)
    j = lax.broadcasted_iota(jnp.int32, (BLOCK, 3 * BLOCK), 1)
    kpos = (n - 1) * BLOCK + j
    mask = (jnp.abs(j - BLOCK - i) <= WINDOW) & (kpos >= 0) & (kpos < n_tokens)
    head_id = lax.broadcasted_iota(jnp.int32, (1, ATT_HEADS), 1)
    scale = HEAD_DIM ** -0.5
    outs = []
    for hq in range(ATT_HEADS):
        h = hq // ATT_GROUP
        s_w = jnp.where(mask, _nt(qs[hq], kws[h]) * scale, NEG_INF)
        s_c = _nt(qs[hq], kcs[h]) * scale
        sk = jnp.sum(jnp.where(head_id == hq, sink, 0.0), axis=-1, keepdims=True)
        m = lax.stop_gradient(jnp.maximum(jnp.maximum(jnp.max(s_w, axis=-1, keepdims=True),
                                                      jnp.max(s_c, axis=-1, keepdims=True)), sk))
        pw, pc = jnp.exp(s_w - m), jnp.exp(s_c - m)
        den = jnp.sum(pw, axis=-1, keepdims=True) + jnp.sum(pc, axis=-1, keepdims=True) + jnp.exp(sk - m)
        outs.append((_nn(pw, vws[h]) + _nn(pc, vcs[h])) / den)
    return tuple(outs)


def _attn_loads(n, q_ref, kp_ref, vp_ref, kc_ref, vc_ref):
    r0 = pl.multiple_of(n * BLOCK, BLOCK)
    hs = lambda h: slice(h * HEAD_DIM, (h + 1) * HEAD_DIM)
    qs = [q_ref[:, hs(h)].astype(F32) for h in range(ATT_HEADS)]
    kws = [kp_ref[pl.ds(r0, 3 * BLOCK), hs(h)].astype(F32) for h in range(ATT_KV_HEADS)]
    vws = [vp_ref[pl.ds(r0, 3 * BLOCK), hs(h)].astype(F32) for h in range(ATT_KV_HEADS)]
    kcs = [kc_ref[:, hs(h)].astype(F32) for h in range(ATT_KV_HEADS)]
    vcs = [vc_ref[:, hs(h)].astype(F32) for h in range(ATT_KV_HEADS)]
    return r0, hs, qs, kws, vws, kcs, vcs


def _attn_specs(s, c):
    full = lambda shape: pl.BlockSpec(shape, lambda n: (0, 0))
    return [pl.BlockSpec((BLOCK, QW), lambda n: (n, 0)), full((s + 2 * BLOCK, KVW)), full((s + 2 * BLOCK, KVW)),
            full((c, KVW)), full((c, KVW)), full((1, ATT_HEADS))]


def _attn_fwd(q, kp, vp, kc, vc, sink):
    s, c = q.shape[0], kc.shape[0]

    def body(q_ref, kp_ref, vp_ref, kc_ref, vc_ref, sink_ref, o_ref):
        n = pl.program_id(0)
        _, hs, qs, kws, vws, kcs, vcs = _attn_loads(n, q_ref, kp_ref, vp_ref, kc_ref, vc_ref)
        outs = _f_attn(qs, kws, vws, kcs, vcs, sink_ref[...], n, s)
        for h in range(ATT_HEADS):
            o_ref[:, hs(h)] = outs[h].astype(o_ref.dtype)

    return pl.pallas_call(
        body, name="attn_fwd", grid=(s // BLOCK,), in_specs=_attn_specs(s, c),
        out_specs=pl.BlockSpec((BLOCK, QW), lambda n: (n, 0)), out_shape=jax.ShapeDtypeStruct((s, QW), BF16),
        compiler_params=_cp("parallel"),
    )(q, kp, vp, kc, vc, sink)


def _attn_bwd(do, q, kp, vp, kc, vc, sink):
    s, c = q.shape[0], kc.shape[0]

    def body(do_ref, q_ref, kp_ref, vp_ref, kc_ref, vc_ref, sink_ref, dq_ref, dkp_ref, dvp_ref, dkc_ref, dvc_ref,
             dsink_ref):
        n = pl.program_id(0)

        @pl.when(n == 0)
        def _():
            for r in (dkp_ref, dvp_ref, dkc_ref, dvc_ref, dsink_ref):
                r[...] = jnp.zeros_like(r)

        r0, hs, qs, kws, vws, kcs, vcs = _attn_loads(n, q_ref, kp_ref, vp_ref, kc_ref, vc_ref)
        _, vjp = jax.vjp(lambda qs, kws, vws, kcs, vcs, sink: _f_attn(qs, kws, vws, kcs, vcs, sink, n, s),
                         qs, kws, vws, kcs, vcs, sink_ref[...])
        dqs, dkws, dvws, dkcs, dvcs, dsink = vjp(tuple(do_ref[:, hs(h)].astype(F32) for h in range(ATT_HEADS)))
        for h in range(ATT_HEADS):
            dq_ref[:, hs(h)] = dqs[h]
        for h in range(ATT_KV_HEADS):
            dkp_ref[pl.ds(r0, 3 * BLOCK), hs(h)] += dkws[h]
            dvp_ref[pl.ds(r0, 3 * BLOCK), hs(h)] += dvws[h]
            dkc_ref[:, hs(h)] += dkcs[h]
            dvc_ref[:, hs(h)] += dvcs[h]
        dsink_ref[...] += dsink

    full = lambda shape: pl.BlockSpec(shape, lambda n: (0, 0))
    return pl.pallas_call(
        body, name="attn_bwd", grid=(s // BLOCK,),
        in_specs=[pl.BlockSpec((BLOCK, QW), lambda n: (n, 0))] + _attn_specs(s, c),
        out_specs=[pl.BlockSpec((BLOCK, QW), lambda n: (n, 0)), full((s + 2 * BLOCK, KVW)), full((s + 2 * BLOCK, KVW)),
                   full((c, KVW)), full((c, KVW)), full((1, ATT_HEADS))],
        out_shape=[jax.ShapeDtypeStruct((s, QW), F32), jax.ShapeDtypeStruct((s + 2 * BLOCK, KVW), F32),
                   jax.ShapeDtypeStruct((s + 2 * BLOCK, KVW), F32), jax.ShapeDtypeStruct((c, KVW), F32),
                   jax.ShapeDtypeStruct((c, KVW), F32), jax.ShapeDtypeStruct((1, ATT_HEADS), F32)],
        compiler_params=_cp("arbitrary"),
    )(do, q, kp, vp, kc, vc, sink)


def _gla_masks():
    hk = np.arange(GKW) // GLA_DK
    hv = np.arange(GVW) // GLA_DV
    head_k = (np.arange(GLA_HEADS)[:, None] == hk[None, :]).astype(np.float32)
    head_v = (np.arange(GLA_HEADS)[:, None] == hv[None, :]).astype(np.float32)
    bd_t = (hv[:, None] == hk[None, :]).astype(np.float32)
    return jnp.asarray(head_k), jnp.asarray(head_v), jnp.asarray(bd_t)


def _tri(n, rev, strict=False):
    i = lax.broadcasted_iota(jnp.int32, (n, n), 0)
    j = lax.broadcasted_iota(jnp.int32, (n, n), 1)
    if strict:
        keep = (j > i) if rev else (j < i)
    else:
        keep = (j >= i) if rev else (j <= i)
    return keep


def _f_gla_chunk(q, k, v, la, st, head_k, head_v, bd_t, rev):
    keep = _tri(GLA_CHUNK, rev)
    b = _nn_hi(keep.astype(F32), la)
    bl = jnp.sum(la, axis=0, keepdims=True)
    qd = q * (GLA_DK ** -0.5) * jnp.exp(b)
    ki = k * jnp.exp(-b)
    kd = k * jnp.exp(bl - b)
    q_heads = (qd[None, :, :] * head_k[:, None, :]).reshape(GLA_HEADS * GLA_CHUNK, GKW)
    a_all = _nt(q_heads, ki).reshape(GLA_HEADS, GLA_CHUNK, GLA_CHUNK)
    a_all = jnp.where(keep[None, :, :], a_all, 0.0).reshape(GLA_HEADS * GLA_CHUNK, GLA_CHUNK)
    o_all = _nn(a_all, v).reshape(GLA_HEADS, GLA_CHUNK, GVW)
    intra = jnp.sum(o_all * head_v[:, None, :], axis=0)
    inter = _nt(qd, st)
    st_new = st * jnp.exp(bl) + bd_t * _tn(v, kd)
    return intra + inter, st_new


def _gla_specs(s, tb, order):
    return [pl.BlockSpec((tb, GKW), lambda i: (order(i), C_GQ // GKW)),
            pl.BlockSpec((tb, GKW), lambda i: (order(i), C_GK // GKW)),
            pl.BlockSpec((tb, GVW), lambda i: (order(i), C_GV // GVW)),
            pl.BlockSpec((tb, GKW), lambda i: (order(i), 0))]


GLA_BLOCK_CHUNKS = 2


def _gla_fwd(name, p, la, st0, rev):
    s = p.shape[0]
    tb = GLA_BLOCK_CHUNKS * GLA_CHUNK
    nblk = s // tb
    order = (lambda i: nblk - 1 - i) if rev else (lambda i: i)
    masks = _gla_masks()

    def body(q_ref, k_ref, v_ref, la_ref, st0_ref, hk_ref, hv_ref, bd_ref, o_ref, sts_ref, st_ref):
        @pl.when(pl.program_id(0) == 0)
        def _():
            st_ref[...] = st0_ref[...]

        st = st_ref[...]
        sts_ref[0] = st
        chunks = range(GLA_BLOCK_CHUNKS)
        for ci in (reversed(chunks) if rev else chunks):
            rows = slice(ci * GLA_CHUNK, (ci + 1) * GLA_CHUNK)
            o, st = _f_gla_chunk(q_ref[rows, :], k_ref[rows, :], v_ref[rows, :], la_ref[rows, :], st,
                                 hk_ref[...], hv_ref[...], bd_ref[...], rev)
            o_ref[rows, :] = o
        st_ref[...] = st

    full = lambda a: pl.BlockSpec(a.shape, lambda i: (0,) * a.ndim)
    return pl.pallas_call(
        body, name=name, grid=(nblk,),
        in_specs=_gla_specs(s, tb, order) + [full(st0)] + [full(m) for m in masks],
        out_specs=[pl.BlockSpec((tb, GVW), lambda i: (order(i), 0)),
                   pl.BlockSpec((1, GVW, GKW), lambda i: (order(i), 0, 0))],
        out_shape=[jax.ShapeDtypeStruct((s, GVW), F32), jax.ShapeDtypeStruct((nblk, GVW, GKW), F32)],
        scratch_shapes=[pltpu.VMEM((GVW, GKW), F32)],
        compiler_params=_cp("arbitrary"),
    )(p, p, p, la, st0, *masks)


def _gla_bwd(name, p, la, sts, do, prev, rev):
    s = p.shape[0]
    tb = GLA_BLOCK_CHUNKS * GLA_CHUNK
    nblk = s // tb
    order = (lambda i: i) if rev else (lambda i: nblk - 1 - i)
    masks = _gla_masks()
    n_prev = 0 if prev is None else 3

    def body(*refs):
        q_ref, k_ref, v_ref, la_ref, sts_ref, do_ref, hk_ref, hv_ref, bd_ref = refs[:9]
        prev_refs = refs[9:9 + n_prev]
        dq_ref, dk_ref, dv_ref, dla_ref, dst0_ref, dst_ref = refs[9 + n_prev:]

        @pl.when(pl.program_id(0) == 0)
        def _():
            dst_ref[...] = jnp.zeros_like(dst_ref)

        def block(q, k, v, la, st):
            outs = [None] * GLA_BLOCK_CHUNKS
            chunks = range(GLA_BLOCK_CHUNKS)
            for ci in (reversed(chunks) if rev else chunks):
                rows = slice(ci * GLA_CHUNK, (ci + 1) * GLA_CHUNK)
                outs[ci], st = _f_gla_chunk(q[ci], k[ci], v[ci], la[ci], st, hk_ref[...], hv_ref[...], bd_ref[...],
                                            rev)
            return tuple(outs), st

        split = lambda r: tuple(r[ci * GLA_CHUNK:(ci + 1) * GLA_CHUNK, :].astype(F32)
                                for ci in range(GLA_BLOCK_CHUNKS))
        _, vjp = jax.vjp(block, split(q_ref), split(k_ref), split(v_ref), split(la_ref), sts_ref[0])
        dq, dk, dv, dla, dst = vjp((split(do_ref), dst_ref[...]))
        for ci in range(GLA_BLOCK_CHUNKS):
            rows = slice(ci * GLA_CHUNK, (ci + 1) * GLA_CHUNK)
            if n_prev:
                dq_ref[rows, :] = dq[ci] + prev_refs[0][rows, :]
                dk_ref[rows, :] = dk[ci] + prev_refs[1][rows, :]
                dv_ref[rows, :] = dv[ci] + prev_refs[2][rows, :]
            else:
                dq_ref[rows, :], dk_ref[rows, :], dv_ref[rows, :] = dq[ci], dk[ci], dv[ci]
            dla_ref[rows, :] = dla[ci]
        dst_ref[...] = dst
        dst0_ref[...] = dst

    full = lambda a: pl.BlockSpec(a.shape, lambda i: (0,) * a.ndim)
    blk = lambda w: pl.BlockSpec((tb, w), lambda i: (order(i), 0))
    prev_specs = [blk(GKW), blk(GKW), blk(GVW)] if n_prev else []
    return pl.pallas_call(
        body, name=name, grid=(nblk,),
        in_specs=_gla_specs(s, tb, order) + [pl.BlockSpec((1, GVW, GKW), lambda i: (order(i), 0, 0)), blk(GVW)]
        + [full(m) for m in masks] + prev_specs,
        out_specs=[blk(GKW), blk(GKW), blk(GVW), blk(GKW), pl.BlockSpec((GVW, GKW), lambda i: (0, 0))],
        out_shape=[jax.ShapeDtypeStruct((s, GKW), F32), jax.ShapeDtypeStruct((s, GKW), F32),
                   jax.ShapeDtypeStruct((s, GVW), F32), jax.ShapeDtypeStruct((s, GKW), F32),
                   jax.ShapeDtypeStruct((GVW, GKW), F32)],
        scratch_shapes=[pltpu.VMEM((GVW, GKW), F32)],
        compiler_params=_cp("arbitrary"),
    )(p, p, p, la, sts, do, *masks, *(prev or ()))


def _f_ctx_state(k, v, la_f, la_b, bd_t):
    c = k.shape[0]
    after = _nn_hi(_tri(c, True, strict=True).astype(F32), la_f)
    before = _nn_hi(_tri(c, False, strict=True).astype(F32), la_b)
    return bd_t * _tn(v, k * jnp.exp(after)), bd_t * _tn(v, k * jnp.exp(before))


def _ctx_state(pc, la_f, la_b):
    c = pc.shape[0]
    bd_t = _gla_masks()[2]

    def body(k_ref, v_ref, lf_ref, lb_ref, bd_ref, sf_ref, sb_ref):
        sf_ref[...], sb_ref[...] = _f_ctx_state(k_ref[...], v_ref[...], lf_ref[...], lb_ref[...], bd_ref[...])

    full = lambda a: pl.BlockSpec(a.shape, lambda i: (0, 0))
    return pl.pallas_call(
        body, name="ctx_state_fwd", grid=(1,),
        in_specs=[pl.BlockSpec((c, GKW), lambda i: (0, C_GK // GKW)), pl.BlockSpec((c, GVW), lambda i: (0, C_GV // GVW)),
                  full(la_f), full(la_b), full(bd_t)],
        out_specs=[pl.BlockSpec((GVW, GKW), lambda i: (0, 0))] * 2,
        out_shape=[jax.ShapeDtypeStruct((GVW, GKW), F32)] * 2,
        compiler_params=_cp("arbitrary"),
    )(pc, pc, la_f, la_b, bd_t)


def _ctx_state_bwd(pc, la_f, la_b, dsf, dsb):
    c = pc.shape[0]
    bd_t = _gla_masks()[2]

    def body(k_ref, v_ref, lf_ref, lb_ref, bd_ref, dsf_ref, dsb_ref, dk_ref, dv_ref, dlf_ref, dlb_ref):
        _, vjp = jax.vjp(lambda k, v, lf, lb: _f_ctx_state(k, v, lf, lb, bd_ref[...]),
                         k_ref[...], v_ref[...], lf_ref[...], lb_ref[...])
        dk, dv, dlf, dlb = vjp((dsf_ref[...], dsb_ref[...]))
        dk_ref[...], dv_ref[...] = dk.astype(BF16), dv.astype(BF16)
        dlf_ref[...], dlb_ref[...] = dlf, dlb

    full = lambda a: pl.BlockSpec(a.shape, lambda i: (0, 0))
    return pl.pallas_call(
        body, name="ctx_state_bwd", grid=(1,),
        in_specs=[pl.BlockSpec((c, GKW), lambda i: (0, C_GK // GKW)), pl.BlockSpec((c, GVW), lambda i: (0, C_GV // GVW)),
                  full(la_f), full(la_b), full(bd_t), full(dsf), full(dsb)],
        out_specs=[pl.BlockSpec((c, GKW), lambda i: (0, 0)), pl.BlockSpec((c, GVW), lambda i: (0, 0)),
                   pl.BlockSpec((c, GKW), lambda i: (0, 0)), pl.BlockSpec((c, GKW), lambda i: (0, 0))],
        out_shape=[jax.ShapeDtypeStruct((c, GKW), BF16), jax.ShapeDtypeStruct((c, GVW), BF16),
                   jax.ShapeDtypeStruct((c, GKW), F32), jax.ShapeDtypeStruct((c, GKW), F32)],
        compiler_params=_cp("arbitrary"),
    )(pc, pc, la_f, la_b, bd_t, dsf, dsb)


_SRC_COLS = ((0, QW), (QW + 2 * KVW + 2 * GKW, GVW), (QW + 2 * KVW + 2 * GKW + GVW, GVW), (QW, KVW), (QW + KVW, KVW),
             (QW + 2 * KVW, GKW), (QW + 2 * KVW + GKW, GKW), (IN_COLS - 2 * GATE_RANK, 2 * GATE_RANK))
_DST_COLS = (C_Q, C_GV, C_GG, C_K, C_V, C_GQ, C_GK, C_Z)


def _pack_w_in(w_in):
    parts = [w_in[:, s:s + n] for s, n in _SRC_COLS]
    parts.append(jnp.zeros((w_in.shape[0], IN_PAD - C_Z - 2 * GATE_RANK), w_in.dtype))
    return jnp.concatenate(parts, axis=1)


def _unpack_w_in_grad(g):
    by_src = sorted(zip(_SRC_COLS, _DST_COLS))
    return jnp.concatenate([g[:, d:d + n] for (_, n), d in by_src], axis=1)


def _prep_weights(w_in, w_gate_fwd, w_gate_bwd):
    pad_rows = lambda w, at: jnp.zeros((LANES, GKW), F32).at[at:at + GATE_RANK].set(w)
    return {"w_in": _pack_w_in(w_in).astype(BF16), "wg_f": pad_rows(w_gate_fwd, 0),
            "wg_b": pad_rows(w_gate_bwd, GATE_RANK)}


def _local_step(x, ctx, target, ada, ada_c, w, late_weights, reduce_behind=None):
    s, d = x.shape
    sh1, sc1, gt1, sh2, sc2, gt2 = [ada[:, i * d:(i + 1) * d] for i in range(6)]
    sh1c, sc1c = ada_c[:, :d], ada_c[:, d:2 * d]
    cos, sin = _rope_tables(s)
    gt = jnp.tile(w["g_gla_norm"], (1, GLA_HEADS))

    h = _norm_mod("pre_mix", x, w["g_pre_mix"], sh1, sc1)
    hc = _norm_mod("pre_mix_ctx", ctx, w["g_pre_mix"], sh1c, sc1c)
    p = _mm("proj_in", h, w["w_in"], "nn")
    pc = _mm("proj_in_ctx", hc, w["w_in"], "nn")
    q_rot, k_rot, v_b = _rope_fwd("rope", p, cos, sin)
    pad = ((BLOCK, BLOCK), (0, 0))
    kp, vp = jnp.pad(k_rot, pad), jnp.pad(v_b, pad)
    kc, vc = pc[:, C_K:C_K + KVW].astype(BF16), pc[:, C_V:C_V + KVW].astype(BF16)
    attn = _attn_fwd(q_rot, kp, vp, kc, vc, w["attn_sink"])
    gate_w = (w["wg_f"], w["wg_b"], w["b_gate_fwd"], w["b_gate_bwd"])
    la_f, la_b = _gate_fwd("gate", p, *gate_w)
    la_fc, la_bc = _gate_fwd("gate_ctx", pc, *gate_w)
    st_f0, st_b0 = _ctx_state(pc, la_fc, la_bc)
    o_f, sts_f = _gla_fwd("gla_fwd_f", p, la_f, st_f0, False)
    o_b, sts_b = _gla_fwd("gla_fwd_b", p, la_b, st_b0, True)
    mix = _gla_out("gla_out", attn, o_f, o_b, p, gt)
    w_out, w_ffn_in_t, w_ffn_out = late_weights(attn)
    y = _mm("proj_out", mix, w_out, "nn")
    x1 = _post_res("post_mix", x, y, w["g_post_mix"], gt1)
    h2 = _norm_mod("pre_ffn", x1, w["g_pre_ffn"], sh2, sc2)
    u = _mm("ffn_in", h2, w_ffn_in_t, "nt")
    a = _swiglu("swiglu", u)
    f = _mm("ffn_out", a, w_ffn_out, "nn")
    dx2, loss = _post_res_loss("post_ffn_loss", x1, f, w["g_post_ffn"], gt2, target)

    g = {}
    df, g["g_post_ffn"], dgt2 = _post_res_bwd("post_ffn_bwd", dx2, f, w["g_post_ffn"], gt2)
    da = _mm("ffn_out_dx", df, w_ffn_out, "nt")
    g["w_ffn_out"] = _mm("ffn_out_dw", a, df, "tn")
    du = _swiglu_bwd("swiglu_bwd", da, u)
    dh2 = _mm("ffn_in_dx", du, w_ffn_in_t, "nn")
    g["w_ffn_in_t"] = _mm("ffn_in_dw", du, h2, "tn")
    dx1, g["g_pre_ffn"], dsh2, dsc2 = _norm_mod_bwd("pre_ffn_bwd", dh2, dx2, x1, w["g_pre_ffn"], sh2, sc2)
    dy, g["g_post_mix"], dgt1 = _post_res_bwd("post_mix_bwd", dx1, y, w["g_post_mix"], gt1)
    dmix = _mm("proj_out_dx", dy, w_out, "nt", BF16)
    g["w_out"] = _mm("proj_out_dw", mix, dy, "tn")
    rb = reduce_behind
    if rb is not None:
        dmix = _behind(dmix, rb.start(g["w_out"], g["w_ffn_out"], g["w_ffn_in_t"]))
    d_o, dgg, dgt = _gla_out_bwd("gla_out_bwd", dmix, o_f, o_b, p, gt)
    g["g_gla_norm"] = jnp.sum(dgt.reshape(GLA_HEADS, GLA_DV), axis=0, keepdims=True)
    if rb is not None:
        d_o = _behind(d_o, rb.pair(dgg))
    dgq, dgk, dgv, dla_f, dst_f0 = _gla_bwd("gla_bwd_f", p, la_f, sts_f, d_o, None, False)
    dgq, dgk, dgv, dla_b, dst_b0 = _gla_bwd("gla_bwd_b", p, la_b, sts_b, d_o, (dgq, dgk, dgv), True)
    if rb is not None:
        dmix = _behind(dmix, rb.total(dgq))
    dgkc, dgvc, dla_fc, dla_bc = _ctx_state_bwd(pc, la_fc, la_bc, dst_f0, dst_b0)
    dz, dwf, dwb, dbf, dbb = _gate_bwd("gate_bwd", p, dla_f, dla_b, *gate_w)
    dzc, dwfc, dwbc, dbfc, dbbc = _gate_bwd("gate_ctx_bwd", pc, dla_fc, dla_bc, *gate_w)
    g["w_gate_fwd"] = (dwf + dwfc)[:GATE_RANK]
    g["w_gate_bwd"] = (dwb + dwbc)[GATE_RANK:2 * GATE_RANK]
    g["b_gate_fwd"], g["b_gate_bwd"] = dbf + dbfc, dbb + dbbc
    dq_rot, dkp, dvp, dkc, dvc, g["attn_sink"] = _attn_bwd(dmix, q_rot, kp, vp, kc, vc, w["attn_sink"])
    if rb is not None:
        g["behind"] = rb.result(dq_rot)
    dq, dk = _rope_bwd("rope_bwd", dq_rot, dkp[BLOCK:BLOCK + s], cos, sin)
    dp = jnp.concatenate([dq, dgv.astype(BF16), dgg, dk, dvp[BLOCK:BLOCK + s].astype(BF16), dgq.astype(BF16),
                          dgk.astype(BF16), dz], axis=1)
    c_rows = ctx.shape[0]
    zeros = lambda n: jnp.zeros((c_rows, n), BF16)
    dpc = jnp.concatenate([zeros(QW), dgvc, zeros(GVW), dkc.astype(BF16), dvc.astype(BF16), zeros(GKW), dgkc, dzc],
                          axis=1)
    dh = _mm("proj_in_dx", dp, w["w_in"], "nt")
    dhc = _mm("proj_in_ctx_dx", dpc, w["w_in"], "nt")
    g["w_in"] = _mm("proj_in_dw", jnp.concatenate([hc, h], axis=0), jnp.concatenate([dpc, dp], axis=0), "tn")
    dx, dg_a, dsh1, dsc1 = _norm_mod_bwd("pre_mix_bwd", dh, dx1, x, w["g_pre_mix"], sh1, sc1)
    _, dg_b, dsh1c, dsc1c = _norm_mod_bwd("pre_mix_ctx_bwd", dhc, jnp.zeros_like(dhc), ctx, w["g_pre_mix"], sh1c,
                                          sc1c)
    g["g_pre_mix"] = dg_a + dg_b
    d_ada = jnp.concatenate([dsh1, dsc1, dgt1, dsh2, dsc2, dgt2], axis=1)
    d_ada_c = jnp.concatenate([dsh1c, dsc1c, jnp.zeros((1, 4 * d), F32)], axis=1)
    return loss, dx, g, d_ada, d_ada_c


HBM = pl.BlockSpec(memory_space=pltpu.HBM)
N_DEV, N_CHIP = 8, 4


def _place():
    x, y, c = lax.axis_index("x"), lax.axis_index("y"), lax.axis_index("c")
    return x, y, c, [(1 - x, y), (x, 1 - y), (1 - x, 1 - y)]


def _row_tile(n, mult, cap):
    return max(t for t in range(mult, min(n, cap) + 1, mult) if n % t == 0)


def _ag_small(name, v):
    def body(v_ref, out_ref, send_sems, recv_sems):
        x, y, c, _ = _place()
        out_ref[4 * x + 2 * y + c] = v_ref[...]

        def peer(r):
            return ((1 - x) if r & 4 else x, (1 - y) if r & 2 else y, (1 - c) if r & 1 else c)

        def copy(r, block):
            px, py, pc = block
            return pltpu.make_async_remote_copy(
                src_ref=v_ref, dst_ref=out_ref.at[4 * px + 2 * py + pc], send_sem=send_sems.at[r - 1],
                recv_sem=recv_sems.at[r - 1], device_id=peer(r), device_id_type=MESH)

        sends = [copy(r, (x, y, c)) for r in range(1, N_DEV)]
        for cp in sends:
            cp.start()
        for r in range(1, N_DEV):
            copy(r, peer(r)).wait_recv()
        for cp in sends:
            cp.wait_send()

    return pl.pallas_call(
        body, name=name, out_shape=jax.ShapeDtypeStruct((N_DEV,) + v.shape, v.dtype),
        in_specs=[pl.BlockSpec(memory_space=pltpu.VMEM)], out_specs=pl.BlockSpec(memory_space=pltpu.VMEM),
        scratch_shapes=[pltpu.SemaphoreType.DMA((N_DEV - 1,)), pltpu.SemaphoreType.DMA((N_DEV - 1,))],
    )(v)


def _halves(c, rows, mult):
    hr = rows // 2
    return pl.ds(pl.multiple_of(c * hr, mult), hr), pl.ds(pl.multiple_of((1 - c) * hr, mult), hr)


def _ag_shards(name, shard):
    rows = shard.shape[0]

    def body(w_ref, out_ref, send_sems, recv_sems, local_sem):
        x, y, c, chips = _place()
        mine_half, other_half = _halves(c, rows, 16)
        me = 2 * x + y
        mine = pltpu.make_async_copy(w_ref, out_ref.at[me], local_sem)
        mine.start()

        def copy(k, src, chip, half, to):
            return pltpu.make_async_remote_copy(
                src_ref=src, dst_ref=out_ref.at[chip, half], send_sem=send_sems.at[k], recv_sem=recv_sems.at[k],
                device_id=to, device_id_type=MESH)

        first = [copy(j, w_ref.at[mine_half], me, mine_half, (px, py, c)) for j, (px, py) in enumerate(chips)]
        for cp in first:
            cp.start()
        passed = []
        for j, (px, py) in enumerate(chips):
            pk = 2 * px + py
            copy(j, w_ref.at[mine_half], pk, mine_half, (px, py, c)).wait_recv()
            cp = copy(3 + j, out_ref.at[pk, mine_half], pk, mine_half, (x, y, 1 - c))
            cp.start()
            passed.append(cp)
        for j, (px, py) in enumerate(chips):
            copy(3 + j, w_ref.at[mine_half], 2 * px + py, other_half, (x, y, 1 - c)).wait_recv()
        for cp in first + passed:
            cp.wait_send()
        mine.wait()

    return pl.pallas_call(
        body, name=name, out_shape=jax.ShapeDtypeStruct((N_CHIP,) + shard.shape, shard.dtype),
        in_specs=[HBM], out_specs=HBM,
        scratch_shapes=[pltpu.SemaphoreType.DMA((6,)), pltpu.SemaphoreType.DMA((6,)), pltpu.SemaphoreType.DMA],
    )(shard)


def _swap_half(name, g):
    n_sh, rows, n = g.shape

    def body(g_ref, a_ref, send_sem, recv_sem):
        x, y, c, _ = _place()
        _, other_half = _halves(c, rows, 8)
        cp = pltpu.make_async_remote_copy(
            src_ref=g_ref.at[pl.ds(0, n_sh), other_half], dst_ref=a_ref, send_sem=send_sem, recv_sem=recv_sem,
            device_id=(x, y, 1 - c), device_id_type=MESH)
        cp.start()
        cp.wait()

    return pl.pallas_call(
        body, name=name, out_shape=jax.ShapeDtypeStruct((n_sh, rows // 2, n), g.dtype), in_specs=[HBM], out_specs=HBM,
        scratch_shapes=[pltpu.SemaphoreType.DMA, pltpu.SemaphoreType.DMA],
    )(g)


def _add_half(name, g, a, c_idx):
    n_sh, hr, n = a.shape
    tr = _row_tile(hr, 16, 256)
    nb = hr // tr

    def body(c_ref, g_ref, a_ref, o_ref):
        o_ref[...] = (g_ref[...] + a_ref[...]).astype(o_ref.dtype)

    return pl.pallas_call(
        body, name=name, out_shape=jax.ShapeDtypeStruct(a.shape, BF16),
        grid_spec=pltpu.PrefetchScalarGridSpec(
            num_scalar_prefetch=1, grid=(n_sh, nb),
            in_specs=[pl.BlockSpec((1, tr, n), lambda s, i, c_ref: (s, c_ref[0] * nb + i, 0)),
                      pl.BlockSpec((1, tr, n), lambda s, i, c_ref: (s, i, 0))],
            out_specs=pl.BlockSpec((1, tr, n), lambda s, i, c_ref: (s, i, 0))),
        compiler_params=_cp("parallel", "parallel"),
    )(c_idx, g, a)


def _scatter_chips(name, h):
    def body(h_ref, b_ref, send_sems, recv_sems, local_sem):
        x, y, c, chips = _place()
        me = 2 * x + y
        mine = pltpu.make_async_copy(h_ref.at[me], b_ref.at[me], local_sem)
        mine.start()

        def copy(j, src_block, dst_block, to):
            return pltpu.make_async_remote_copy(
                src_ref=h_ref.at[src_block], dst_ref=b_ref.at[dst_block], send_sem=send_sems.at[j],
                recv_sem=recv_sems.at[j], device_id=to, device_id_type=MESH)

        sends = [copy(j, 2 * px + py, me, (px, py, c)) for j, (px, py) in enumerate(chips)]
        for cp in sends:
            cp.start()
        for j, (px, py) in enumerate(chips):
            copy(j, me, 2 * px + py, (px, py, c)).wait_recv()
        for cp in sends:
            cp.wait_send()
        mine.wait()

    return pl.pallas_call(
        body, name=name, out_shape=jax.ShapeDtypeStruct(h.shape, h.dtype), in_specs=[HBM], out_specs=HBM,
        scratch_shapes=[pltpu.SemaphoreType.DMA((3,)), pltpu.SemaphoreType.DMA((3,)), pltpu.SemaphoreType.DMA],
    )(h)


def _sum_chips(name, b):
    n_sh, hr, n = b.shape
    tr = _row_tile(hr, 16, 256)

    def body(b0, b1, b2, b3, o_ref):
        o_ref[...] = ((b0[0].astype(F32) + b1[0].astype(F32)) + b2[0].astype(F32)) + b3[0].astype(F32)

    return pl.pallas_call(
        body, name=name, grid=(hr // tr,), out_shape=jax.ShapeDtypeStruct((hr, n), F32),
        in_specs=[pl.BlockSpec((1, tr, n), functools.partial(lambda i, k: (k, i, 0), k=k)) for k in range(n_sh)],
        out_specs=pl.BlockSpec((tr, n), lambda i: (i, 0)), compiler_params=_cp("parallel"),
    )(b, b, b, b)


def _share_half(name, f):
    hr, n = f.shape

    def body(f_ref, out_ref, send_sem, recv_sem, local_sem):
        x, y, c, _ = _place()
        mine_half, other_half = _halves(c, 2 * hr, 8)
        mine = pltpu.make_async_copy(f_ref, out_ref.at[mine_half], local_sem)
        mine.start()

        def copy(half):
            return pltpu.make_async_remote_copy(
                src_ref=f_ref, dst_ref=out_ref.at[half], send_sem=send_sem, recv_sem=recv_sem,
                device_id=(x, y, 1 - c), device_id_type=MESH)

        send = copy(mine_half)
        send.start()
        copy(other_half).wait_recv()
        send.wait_send()
        mine.wait()

    return pl.pallas_call(
        body, name=name, out_shape=jax.ShapeDtypeStruct((2 * hr, n), f.dtype), in_specs=[HBM], out_specs=HBM,
        scratch_shapes=[pltpu.SemaphoreType.DMA, pltpu.SemaphoreType.DMA, pltpu.SemaphoreType.DMA],
    )(f)


def _reduce_shards(name, g, c_idx):
    a = _swap_half(name + "_swap", g)
    h = _add_half(name + "_pair", g, a, c_idx)
    b = _scatter_chips(name + "_scatter", h)
    f = _sum_chips(name + "_sum", b)
    return _share_half(name + "_share", f)


SEM = pl.BlockSpec(memory_space=pltpu.SEMAPHORE)
ANY = pl.BlockSpec(memory_space=pl.ANY)
DATAFLOW = pltpu.SideEffectType.DATAFLOW_SIDE_EFFECTING


def _remote(src, dst, send_sems, recv_sems, k, to):
    return pltpu.make_async_remote_copy(src_ref=src, dst_ref=dst, send_sem=send_sems.at[k], recv_sem=recv_sems.at[k],
                                        device_id=to, device_id_type=MESH)


def _split_copy(name, src, land_shape, land_dtype, n, plan):
    def start_body(src_ref, land_ref, send_sems, recv_sems, src_thru, land_thru, token):
        for cp in plan(src_ref, land_ref, send_sems, recv_sems)[0]:
            cp.start()
        token[...] = jnp.zeros_like(token)

    sems = pltpu.SemaphoreType.DMA((n,))
    send_sems, recv_sems, src_thru, land_thru, token = pl.pallas_call(
        start_body, name=name + "_start",
        out_shape=(sems, sems, pltpu.HBM(src.shape, src.dtype), pltpu.HBM(land_shape, land_dtype),
                   jax.ShapeDtypeStruct((8, LANES), F32)),
        in_specs=(HBM, HBM), out_specs=(SEM, SEM, HBM, HBM, pl.BlockSpec(memory_space=pltpu.VMEM)),
        input_output_aliases={0: 2, 1: 3}, compiler_params=pltpu.CompilerParams(has_side_effects=DATAFLOW),
    )(pltpu.with_memory_space_constraint(src, pltpu.HBM),
      pltpu.with_memory_space_constraint(lax.empty(land_shape, land_dtype), pltpu.HBM))

    def wait(after):
        def wait_body(src_ref, land_ref, send_sems, recv_sems, after_ref, src_out, land_out):
            sent, received = plan(src_ref, land_ref, send_sems, recv_sems)
            for cp in sent:
                cp.wait_send()
            for cp in received:
                cp.wait_recv()

        return pl.pallas_call(
            wait_body, name=name + "_wait",
            out_shape=(pltpu.HBM(src.shape, src.dtype), pltpu.HBM(land_shape, land_dtype)),
            in_specs=(HBM, HBM, SEM, SEM, ANY), out_specs=(HBM, HBM), input_output_aliases={0: 0, 1: 1},
            compiler_params=pltpu.CompilerParams(has_side_effects=DATAFLOW),
        )(src_thru, land_thru, send_sems, recv_sems, after)

    return token, wait


def _behind(x, token):
    return lax.optimization_barrier((x, token))[0]


def _plan_gather(src_ref, land_ref, send_sems, recv_sems):
    x, y, c, chips = _place()
    sent = [_remote(src_ref, land_ref.at[2 * x + y], send_sems, recv_sems, j, (px, py, c))
            for j, (px, py) in enumerate(chips)]
    received = [_remote(src_ref, land_ref.at[2 * px + py], send_sems, recv_sems, j, (px, py, c))
                for j, (px, py) in enumerate(chips)]
    return sent, received


def _plan_swap(src_ref, land_ref, send_sems, recv_sems):
    x, y, c, _ = _place()
    _, other_half = _halves(c, src_ref.shape[1], 8)
    cp = _remote(src_ref.at[pl.ds(0, src_ref.shape[0]), other_half], land_ref, send_sems, recv_sems, 0, (x, y, 1 - c))
    return [cp], [cp]


def _plan_scatter(src_ref, land_ref, send_sems, recv_sems):
    x, y, c, chips = _place()
    sent = [_remote(src_ref.at[2 * px + py], land_ref.at[2 * x + y], send_sems, recv_sems, j, (px, py, c))
            for j, (px, py) in enumerate(chips)]
    received = [_remote(src_ref.at[2 * px + py], land_ref.at[2 * px + py], send_sems, recv_sems, j, (px, py, c))
                for j, (px, py) in enumerate(chips)]
    return sent, received


def _plan_share(src_ref, land_ref, send_sems, recv_sems):
    x, y, c, _ = _place()
    mine_half, other_half = _halves(c, land_ref.shape[0], 8)
    return ([_remote(src_ref, land_ref.at[mine_half], send_sems, recv_sems, 0, (x, y, 1 - c))],
            [_remote(src_ref, land_ref.at[other_half], send_sems, recv_sems, 0, (x, y, 1 - c))])


class _GatherBehind:
    def __init__(self, name, shard, chip):
        self.chip = chip
        self.token, self.wait = _split_copy(name, shard, (N_CHIP,) + shard.shape, shard.dtype, 3, _plan_gather)

    def result(self, after):
        shard, land = self.wait(after)
        return lax.dynamic_update_slice(land, shard[None], (self.chip, 0, 0))


class _ReduceBehind:
    def __init__(self, name, chip, c, c_idx):
        self.name, self.chip, self.c, self.c_idx = name, chip, c, c_idx

    def start(self, *grads):
        g = jnp.stack([jnp.concatenate([t[k * (t.shape[0] // N_CHIP):(k + 1) * (t.shape[0] // N_CHIP)] for t in grads],
                                       axis=0) for k in range(N_CHIP)])
        n_sh, rows, n = g.shape
        token, self.wait = _split_copy(self.name + "_swap", g, (n_sh, rows // 2, n), g.dtype, 1, _plan_swap)
        return token

    def pair(self, after):
        g, a = self.wait(after)
        h = _add_half(self.name + "_pair", g, a, self.c_idx)
        token, self.wait = _split_copy(self.name + "_scatter", h, h.shape, h.dtype, 3, _plan_scatter)
        return token

    def total(self, after):
        h, b = self.wait(after)
        b = lax.dynamic_update_slice(b, lax.dynamic_slice_in_dim(h, self.chip, 1, axis=0), (self.chip, 0, 0))
        f = _sum_chips(self.name + "_sum", b)
        token, self.wait = _split_copy(self.name + "_share", f, (2 * f.shape[0], f.shape[1]), f.dtype, 1,
                                       _plan_share)
        return token

    def result(self, after):
        f, out = self.wait(after)
        return lax.dynamic_update_slice(out, f, (self.c * f.shape[0], 0))


def _f_adamw(w, g, m, v):
    m = ADAM_B1 * m + (1.0 - ADAM_B1) * g
    v = ADAM_B2 * v + (1.0 - ADAM_B2) * (g * g)
    m_hat = m / (1.0 - ADAM_B1 ** ADAM_STEP)
    v_hat = v / (1.0 - ADAM_B2 ** ADAM_STEP)
    return -ADAM_LR * (m_hat / (jnp.sqrt(v_hat) + ADAM_EPS) + ADAM_WD * w), m, v


def _adamw(name, w, g, m, v):
    rows, n = w.shape
    return _rowwise(name, lambda w, g, m, v: (_f_adamw(w, g, m, v), ()), rows, [(t, n, 0) for t in (w, g, m, v)], [],
                    [(n, F32)] * 3, [], tm=_row_tile(rows, 8, 256))


def _pack_rows(parts):
    rows = []
    for t in parts:
        t = t.reshape(-1)
        rows.append(jnp.pad(t, (0, -t.shape[0] % LANES)).reshape(-1, LANES))
    out = jnp.concatenate(rows, axis=0)
    return jnp.pad(out, ((0, -out.shape[0] % 8), (0, 0)))


def _unpack_rows(packed, shapes):
    out, r = [], 0
    for shp in shapes:
        n = int(np.prod(shp))
        nr = -(-n // LANES)
        out.append(packed[r:r + nr].reshape(-1)[:n].reshape(shp))
        r += nr
    return out


def _sum_blocks(name, g):
    def body(g_ref, o_ref):
        acc = g_ref[0]
        for k in range(1, g.shape[0]):
            acc = acc + g_ref[k]
        o_ref[...] = acc

    return pl.pallas_call(body, name=name, out_shape=jax.ShapeDtypeStruct(g.shape[1:], F32))(g)


def _silu(t):
    return t * _sigmoid(t)


def _ada_fwd(cc, w_ada):
    n = w_ada.shape[1]
    tn = _row_tile(n, LANES, 512)

    def body(cc_ref, w_ref, o_ref):
        o_ref[...] = _nn(_silu(cc_ref[...]), w_ref[...])

    return pl.pallas_call(
        body, name="ada_fwd", grid=(n // tn,), out_shape=jax.ShapeDtypeStruct((cc.shape[0], n), F32),
        in_specs=[pl.BlockSpec(cc.shape, lambda j: (0, 0)), pl.BlockSpec((w_ada.shape[0], tn), lambda j: (0, j))],
        out_specs=pl.BlockSpec((cc.shape[0], tn), lambda j: (0, j)), compiler_params=_cp("parallel"),
    )(cc, w_ada)


def _ada_bwd(cc, dm, w_ada):
    d, n = w_ada.shape
    tn = _row_tile(n, LANES, 512)

    def body(cc_ref, dm_ref, w_ref, gw_ref, ds_ref):
        @pl.when(pl.program_id(0) == 0)
        def _():
            ds_ref[...] = jnp.zeros_like(ds_ref)

        gw_ref[...] = _raw_dot("tn", _silu(cc_ref[...]), dm_ref[...], True)
        ds_ref[...] += _raw_dot("nt", dm_ref[...], w_ref[...], False)

    return pl.pallas_call(
        body, name="ada_bwd", grid=(n // tn,),
        out_shape=[jax.ShapeDtypeStruct((d, n), F32), jax.ShapeDtypeStruct(cc.shape, F32)],
        in_specs=[pl.BlockSpec(cc.shape, lambda j: (0, 0)), pl.BlockSpec((cc.shape[0], tn), lambda j: (0, j)),
                  pl.BlockSpec((d, tn), lambda j: (0, j))],
        out_specs=[pl.BlockSpec((d, tn), lambda j: (0, j)), pl.BlockSpec(cc.shape, lambda j: (0, 0))],
        compiler_params=_cp("arbitrary"),
    )(cc, dm, w_ada)


def _c_ctx_grad(parts, c_ctx):
    def body(p_ref, c_ref, o_ref):
        ds = ((p_ref[0] + p_ref[1]) + p_ref[2]) + p_ref[3]
        _, vjp = jax.vjp(_silu, c_ref[...])
        o_ref[...] = vjp(ds)[0]

    return pl.pallas_call(body, name="c_ctx_grad", out_shape=jax.ShapeDtypeStruct(c_ctx.shape, F32))(parts, c_ctx)


def kernel(x, c, ctx, c_ctx, w_ada, b_ada, g_pre_mix, g_post_mix, g_pre_ffn, g_post_ffn, w_in, attn_sink, w_gate_fwd, b_gate_fwd, w_gate_bwd, b_gate_bwd, g_gla_norm, w_out, w_ffn_in, w_ffn_out, loss_target, m_c_ctx, m_w_ada, m_b_ada, m_g_pre_mix, m_g_post_mix, m_g_pre_ffn, m_g_post_ffn, m_w_in, m_attn_sink, m_w_gate_fwd, m_b_gate_fwd, m_w_gate_bwd, m_b_gate_bwd, m_g_gla_norm, m_w_out, m_w_ffn_in, m_w_ffn_out, v_c_ctx, v_w_ada, v_b_ada, v_g_pre_mix, v_g_post_mix, v_g_pre_ffn, v_g_post_ffn, v_w_in, v_attn_sink, v_w_gate_fwd, v_b_gate_fwd, v_w_gate_bwd, v_b_gate_bwd, v_g_gla_norm, v_w_out, v_w_ffn_in, v_w_ffn_out):
    xi, yi, ci = lax.axis_index("x"), lax.axis_index("y"), lax.axis_index("c")
    dev, chip = 4 * xi + 2 * yi + ci, 2 * xi + yi
    c_idx = jnp.reshape(ci, (1,)).astype(jnp.int32)
    d = x.shape[-1]
    n_ada, n_in, n_f = w_ada.shape[-1], w_in.shape[-1], w_ffn_in.shape[-1]
    r_out, r_f = w_out.shape[1], w_ffn_out.shape[1]
    n_gate = w_gate_fwd.shape[-1]
    by_chip = lambda t: t[0::2]

    late = _GatherBehind("gather_late", jnp.concatenate(
        [w_out[0], w_ffn_out[0], jnp.transpose(w_ffn_in[0])], axis=0).astype(BF16), chip)

    def late_weights(after):
        t = late.result(after)
        r1, r2 = r_out, r_out + r_f
        return (t[:, :r1].reshape(N_CHIP * r_out, d), t[:, r2:].reshape(N_CHIP * n_f, d),
                t[:, r1:r2].reshape(N_CHIP * r_f, d))

    rc = -(-d // LANES)
    g1 = _ag_small("gather_cond", _pack_rows([_behind(c[0], late.token), w_gate_fwd[0], w_gate_bwd[0]]))
    c_all = g1[:, :rc].reshape(N_DEV, -1)[:, :d]
    gr = GATE_RANK * n_gate // LANES
    gate_full = lambda off: jnp.transpose(by_chip(g1)[:, off:off + gr].reshape(N_CHIP, GATE_RANK, n_gate),
                                          (1, 0, 2)).reshape(GATE_RANK, N_CHIP * n_gate)
    wgf, wgb = gate_full(rc), gate_full(rc + gr)
    cc = jnp.concatenate([c_all, c_ctx[None, :], jnp.zeros((7, d), F32)], axis=0)

    g2 = _ag_small("gather_ada", _ada_fwd(cc, w_ada[0]).reshape(-1, LANES))
    ada_all = jnp.transpose(by_chip(g2).reshape(N_CHIP, 16, n_ada), (1, 0, 2)).reshape(16, N_CHIP * n_ada) + b_ada
    ada = lax.dynamic_slice(ada_all, (dev, 0), (1, N_CHIP * n_ada))
    ada_c = ada_all[N_DEV:N_DEV + 1]

    w_in_g = _ag_shards("gather_w_in", w_in[0].astype(BF16))
    w = _prep_weights(jnp.concatenate([w_in_g[k] for k in range(N_CHIP)], axis=1), wgf, wgb)
    w.update(g_pre_mix=g_pre_mix, g_post_mix=g_post_mix, g_pre_ffn=g_pre_ffn, g_post_ffn=g_post_ffn,
             attn_sink=attn_sink, b_gate_fwd=b_gate_fwd, b_gate_bwd=b_gate_bwd, g_gla_norm=g_gla_norm)

    reduce_behind = _ReduceBehind("reduce_late", chip, ci, c_idx)
    loss_lanes, grad_x, g, d_ada, d_ada_c = _local_step(x[0], ctx[0], loss_target[0], ada, ada_c, w, late_weights,
                                                        reduce_behind)

    small = ("g_pre_mix", "g_post_mix", "g_pre_ffn", "g_post_ffn", "attn_sink", "b_gate_fwd", "b_gate_bwd",
             "g_gla_norm", "w_gate_fwd", "w_gate_bwd")
    shapes = [(1, 6 * d)] * 2 + [g[n].shape for n in small]
    g3 = _ag_small("gather_small_grads", _pack_rows([d_ada, d_ada_c] + [g[n] for n in small]))
    tot = dict(zip(("d_ada", "d_ada_c") + small, _unpack_rows(_sum_blocks("sum_small_grads", g3), shapes)))
    r_ada = 6 * d // LANES
    dm = jnp.concatenate([g3[:, :r_ada].reshape(N_DEV, 6 * d), tot["d_ada_c"], jnp.zeros((7, 6 * d), F32)], axis=0)
    grads = {n: tot[n] for n in small[:8]}
    grads["b_ada"] = _sum_blocks("sum_b_ada", dm.reshape(16, r_ada, LANES)).reshape(1, 6 * d)
    grads["w_gate_fwd"] = lax.dynamic_slice(tot["w_gate_fwd"], (0, chip * n_gate), (GATE_RANK, n_gate))[None]
    grads["w_gate_bwd"] = lax.dynamic_slice(tot["w_gate_bwd"], (0, chip * n_gate), (GATE_RANK, n_gate))[None]
    gw_ada, dsc = _ada_bwd(cc, lax.dynamic_slice(dm, (0, chip * n_ada), (16, n_ada)), w_ada[0])
    grads["w_ada"] = gw_ada[None]
    g4 = _ag_small("gather_c_ctx", _pack_rows([dsc[N_DEV]]))
    grads["c_ctx"] = _c_ctx_grad(by_chip(g4), _pack_rows([c_ctx])).reshape(-1)[:d]

    g_w_in = _unpack_w_in_grad(g["w_in"])
    grads["w_in"] = _reduce_shards("reduce_w_in", jnp.stack([g_w_in[:, k * n_in:(k + 1) * n_in]
                                                             for k in range(N_CHIP)]), c_idx)[None]
    behind = g["behind"]
    grads["w_out"], grads["w_ffn_out"] = behind[None, :r_out], behind[None, r_out:r_out + r_f]
    grads["w_ffn_in"] = jnp.transpose(behind[r_out + r_f:])[None]

    names = ("c_ctx", "w_ada", "b_ada", "g_pre_mix", "g_post_mix", "g_pre_ffn", "g_post_ffn", "w_in", "attn_sink",
             "w_gate_fwd", "b_gate_fwd", "w_gate_bwd", "b_gate_bwd", "g_gla_norm", "w_out", "w_ffn_in", "w_ffn_out")
    weights = dict(zip(names, (c_ctx, w_ada, b_ada, g_pre_mix, g_post_mix, g_pre_ffn, g_post_ffn, w_in, attn_sink,
                               w_gate_fwd, b_gate_fwd, w_gate_bwd, b_gate_bwd, g_gla_norm, w_out, w_ffn_in,
                               w_ffn_out)))
    m_in = dict(zip(names, (m_c_ctx, m_w_ada, m_b_ada, m_g_pre_mix, m_g_post_mix, m_g_pre_ffn, m_g_post_ffn, m_w_in,
                            m_attn_sink, m_w_gate_fwd, m_b_gate_fwd, m_w_gate_bwd, m_b_gate_bwd, m_g_gla_norm,
                            m_w_out, m_w_ffn_in, m_w_ffn_out)))
    v_in = dict(zip(names, (v_c_ctx, v_w_ada, v_b_ada, v_g_pre_mix, v_g_post_mix, v_g_pre_ffn, v_g_post_ffn, v_w_in,
                            v_attn_sink, v_w_gate_fwd, v_b_gate_fwd, v_w_gate_bwd, v_b_gate_bwd, v_g_gla_norm,
                            v_w_out, v_w_ffn_in, v_w_ffn_out)))
    large = ("w_ada", "w_in", "w_out", "w_ffn_in", "w_ffn_out")
    tiny = tuple(n for n in names if n not in large)
    delta, new_m, new_v = {}, {}, {}
    for n in large:
        dl, nm, nv = _adamw("adamw_" + n, weights[n][0], grads[n][0], m_in[n][0], v_in[n][0])
        delta[n], new_m[n], new_v[n] = dl[None], nm[None], nv[None]
    tiny_shapes = [weights[n].shape for n in tiny]
    packed = [_pack_rows([t[n] for n in tiny]) for t in (weights, grads, m_in, v_in)]
    for out, res in zip((delta, new_m, new_v), _adamw("adamw_small", *packed)):
        out.update(zip(tiny, _unpack_rows(res, tiny_shapes)))
    for n in tiny:
        grads[n] = grads[n].reshape(weights[n].shape)

    loss = lax.psum(loss_lanes[0, 0], ("x", "y", "c"))
    return (loss, grad_x[None], *[grads[n] for n in names], *[delta[n] for n in names], *[new_m[n] for n in names],
            *[new_v[n] for n in names])
```

```python
import functools

import jax
import jax.numpy as jnp
import numpy as np
from jax import lax
from jax.experimental import pallas as pl
from jax.experimental.pallas import tpu as pltpu

F32 = jnp.float32
BF16 = jnp.bfloat16
MESH = pl.DeviceIdType.MESH

HEAD_DIM = 64
ATT_HEADS = 8
ATT_KV_HEADS = 2
ATT_GROUP = ATT_HEADS // ATT_KV_HEADS
WINDOW = 128
BLOCK = 128
GRID_W = 64
ROPE_BASE = 10000.0
GLA_HEADS = 8
GLA_DK = 32
GLA_DV = 64
GLA_CHUNK = 64
GATE_RANK = 16
GATE_TAU = 16.0
NEG_INF = -1e30
QW = ATT_HEADS * HEAD_DIM
KVW = ATT_KV_HEADS * HEAD_DIM
GKW = GLA_HEADS * GLA_DK
GVW = GLA_HEADS * GLA_DV
IN_COLS = QW + 2 * KVW + 2 * GKW + 2 * GVW + 2 * GATE_RANK
LANES = 128
IN_PAD = IN_COLS + LANES - 2 * GATE_RANK
C_Q, C_GV, C_GG = 0, QW, QW + GVW
C_K = C_GG + GVW
C_V = C_K + KVW
C_GQ = C_V + KVW
C_GK = C_GQ + GKW
C_Z = C_GK + GKW
MIX = QW + GVW

ADAM_LR, ADAM_B1, ADAM_B2, ADAM_EPS, ADAM_WD, ADAM_STEP = 0.001, 0.9, 0.999, 1e-08, 0.01, 10

VMEM_LIMIT = 56 * 1024 * 1024


def _cp(*sem):
    return pltpu.CompilerParams(dimension_semantics=sem, vmem_limit_bytes=VMEM_LIMIT)


def _pick(n, cands):
    for t in cands:
        if n % t == 0:
            return t
    return n


_DIMS = {"nn": (((1,), (0,)), ((), ())), "nt": (((1,), (1,)), ((), ())), "tn": (((0,), (0,)), ((), ()))}


def _raw_dot(mode, a, b, hi):
    if hi:
        return lax.dot_general(a.astype(F32), b.astype(F32), _DIMS[mode], precision=lax.Precision.HIGHEST,
                               preferred_element_type=F32)
    return lax.dot_general(a.astype(BF16), b.astype(BF16), _DIMS[mode], preferred_element_type=F32)


def _make_dot(mode, hi):
    @jax.custom_vjp
    def dot(a, b):
        return _raw_dot(mode, a, b, hi)

    def fwd(a, b):
        return _raw_dot(mode, a, b, hi), (a, b)

    def bwd(res, dc):
        a, b = res
        if mode == "nn":
            return _raw_dot("nt", dc, b, hi), _raw_dot("tn", a, dc, hi)
        if mode == "nt":
            return _raw_dot("nn", dc, b, hi), _raw_dot("tn", dc, a, hi)
        return _raw_dot("nt", b, dc, hi), _raw_dot("nn", a, dc, hi)

    dot.defvjp(fwd, bwd)
    return dot


_nn, _nt, _tn = _make_dot("nn", False), _make_dot("nt", False), _make_dot("tn", False)
_nn_hi = _make_dot("nn", True)


MM_VMEM_BUDGET = 44 * 1024 * 1024


def _halvings(n):
    out = [n]
    while out[-1] % (2 * LANES) == 0:
        out.append(out[-1] // 2)
    return out


def _mm_tiles(mode, m, n, k, a_bytes, b_bytes, o_bytes, init_bytes=0):
    tms = [t for t in dict.fromkeys((m, m // 2, 2048, 1024, 512, 256, 128))
           if m % t == 0 and t % (LANES if mode == "tn" else 16) == 0 and t <= 4096] or [m]
    tks = ([t for t in (512, 256, 128) if k % t == 0] or [k]) if mode == "tn" else _halvings(k)
    for tn in _halvings(n):
        for tk in tks:
            for tm in tms:
                acc = tm * tn * 4 if (k // tk > 1 and o_bytes != 4) else 0
                tiles = tm * tk * a_bytes + tk * tn * b_bytes + tm * tn * (o_bytes + init_bytes)
                if 2 * tiles + acc <= MM_VMEM_BUDGET:
                    return tm, tn, tk
    return tms[-1], _halvings(n)[-1], tks[-1]


def _mm(name, a, b, mode, out_dtype=F32, init=None):
    if mode == "nn":
        (m, k), n = a.shape, b.shape[1]
    elif mode == "nt":
        (m, k), n = a.shape, b.shape[0]
    else:
        (k, m), n = a.shape, b.shape[1]
    tm, tn, tk = _mm_tiles(mode, m, n, k, a.dtype.itemsize, b.dtype.itemsize, jnp.dtype(out_dtype).itemsize,
                           0 if init is None else 4)
    nk = k // tk
    use_acc = nk > 1 and out_dtype != F32

    inits = () if init is None else (init,)

    def body(a_ref, b_ref, *rest):
        o_ref, acc = rest[len(inits)], rest[len(inits) + 1:]
        part = _raw_dot(mode, a_ref[...], b_ref[...], False)
        first = lambda: part + rest[0][...] if inits else part
        if nk == 1:
            o_ref[...] = first().astype(o_ref.dtype)
            return
        acc_ref = acc[0] if use_acc else o_ref
        kk = pl.program_id(2)

        @pl.when(kk == 0)
        def _():
            acc_ref[...] = first()

        @pl.when(kk > 0)
        def _():
            acc_ref[...] += part

        if use_acc:
            @pl.when(kk == nk - 1)
            def _():
                o_ref[...] = acc_ref[...].astype(o_ref.dtype)

    if mode == "nn":
        a_spec = pl.BlockSpec((tm, tk), lambda i, j, kk: (i, kk))
        b_spec = pl.BlockSpec((tk, tn), lambda i, j, kk: (kk, j))
    elif mode == "nt":
        a_spec = pl.BlockSpec((tm, tk), lambda i, j, kk: (i, kk))
        b_spec = pl.BlockSpec((tn, tk), lambda i, j, kk: (j, kk))
    else:
        a_spec = pl.BlockSpec((tk, tm), lambda i, j, kk: (kk, i))
        b_spec = pl.BlockSpec((tk, tn), lambda i, j, kk: (kk, j))
    return pl.pallas_call(
        body, name=name, grid=(m // tm, n // tn, nk),
        in_specs=[a_spec, b_spec] + [pl.BlockSpec((tm, tn), lambda i, j, kk: (i, j))] * len(inits),
        out_specs=pl.BlockSpec((tm, tn), lambda i, j, kk: (i, j)),
        out_shape=jax.ShapeDtypeStruct((m, n), out_dtype),
        scratch_shapes=[pltpu.VMEM((tm, tn), F32)] if use_acc else [],
        compiler_params=_cp("parallel", "parallel", "arbitrary"),
    )(a, b, *inits)


def _rowwise(name, fn, rows, row_ins, full_ins, row_outs, acc_outs, tm=None):
    tm = tm or _pick(rows, (512, 256, 128))
    n_r, n_f, n_o, n_a = len(row_ins), len(full_ins), len(row_outs), len(acc_outs)

    def body(*refs):
        ins, outs = refs[:n_r + n_f], refs[n_r + n_f:]
        vals = [r[...].astype(F32) for r in ins]
        ro, ao = fn(*vals)
        for r, val in zip(outs[:n_o], ro):
            r[...] = val.astype(r.dtype)
        if n_a:
            @pl.when(pl.program_id(0) == 0)
            def _():
                for r in outs[n_o:]:
                    r[...] = jnp.zeros_like(r)

            for r, val in zip(outs[n_o:], ao):
                r[...] += val

    in_specs = [pl.BlockSpec((tm, w), functools.partial(lambda i, cb: (i, cb), cb=cb)) for _, w, cb in row_ins]
    in_specs += [pl.BlockSpec(a.shape, lambda i: (0, 0)) for a in full_ins]
    out_specs = [pl.BlockSpec((tm, w), lambda i: (i, 0)) for w, _ in row_outs]
    out_specs += [pl.BlockSpec(s, lambda i: (0, 0)) for s in acc_outs]
    out_shape = [jax.ShapeDtypeStruct((rows, w), dt) for w, dt in row_outs]
    out_shape += [jax.ShapeDtypeStruct(s, F32) for s in acc_outs]
    return pl.pallas_call(
        body, name=name, grid=(rows // tm,), in_specs=in_specs, out_specs=out_specs, out_shape=out_shape,
        compiler_params=_cp("arbitrary" if n_a else "parallel"),
    )(*[a for a, _, _ in row_ins], *full_ins)


def _rn(x):
    return x * lax.rsqrt(jnp.mean(x * x, axis=-1, keepdims=True) + 1e-6)


def _sigmoid(t):
    return 1.0 / (1.0 + jnp.exp(-t))


def _f_norm_mod(x, g, sh, sc):
    return _rn(x) * g * (1.0 + sc) + sh


def _f_post_res(xr, y, g, gate):
    return xr + gate * (_rn(y) * g)


def _f_swiglu(g, u):
    return g * _sigmoid(g) * u


def _logsig(u):
    return jnp.minimum(u, 0.0) - jnp.log(1.0 + jnp.exp(-jnp.abs(u)))


def _f_gate(z, wf, wb, bf, bb):
    return _logsig(_nn(z, wf) + bf) / GATE_TAU, _logsig(_nn(z, wb) + bb) / GATE_TAU


def _f_gla_out(of, ob, gg, gt, bd):
    o = of + ob
    ms = _nn_hi(o * o, bd)
    return o * lax.rsqrt(ms + 1e-6) * gt * (gg * _sigmoid(gg))


def _norm_mod(name, x, g, sh, sc):
    rows, d = x.shape
    return _rowwise(name, lambda x, g, sh, sc: ((_f_norm_mod(x, g, sh, sc),), ()), rows,
                    [(x, d, 0)], [g, sh, sc], [(d, BF16)], [])[0]


def _norm_mod_bwd(name, dh, dres, x, g, sh, sc):
    rows, d = x.shape

    def fn(dh, dres, x, g, sh, sc):
        _, vjp = jax.vjp(_f_norm_mod, x, g, sh, sc)
        dx, dg, dsh, dsc = vjp(dh)
        return (dx + dres,), (dg, dsh, dsc)

    return _rowwise(name, fn, rows, [(dh, d, 0), (dres, d, 0), (x, d, 0)], [g, sh, sc], [(d, F32)],
                    [(1, d)] * 3)


def _post_res(name, xr, y, g, gate):
    rows, d = xr.shape
    return _rowwise(name, lambda xr, y, g, gate: ((_f_post_res(xr, y, g, gate),), ()), rows,
                    [(xr, d, 0), (y, d, 0)], [g, gate], [(d, F32)], [])[0]


def _post_res_bwd(name, dxo, y, g, gate):
    rows, d = y.shape

    def fn(dxo, y, g, gate):
        _, vjp = jax.vjp(lambda y, g, gate: _f_post_res(jnp.zeros_like(y), y, g, gate), y, g, gate)
        dy, dg, dgate = vjp(dxo)
        return (dy,), (dg, dgate)

    return _rowwise(name, fn, rows, [(dxo, d, 0), (y, d, 0)], [g, gate], [(d, BF16)], [(1, d)] * 2)


def _post_res_loss(name, xr, y, g, gate, target):
    rows, d = xr.shape

    def fn(xr, y, target, g, gate):
        diff = _f_post_res(xr, y, g, gate) - target
        part = 0.5 * jnp.sum(jnp.mean(diff * diff, axis=-1, keepdims=True), axis=0, keepdims=True)
        return (diff * (1.0 / d),), (jnp.broadcast_to(part, (1, LANES)),)

    return _rowwise(name, fn, rows, [(xr, d, 0), (y, d, 0), (target, d, 0)], [g, gate], [(d, F32)], [(1, LANES)])


def _swiglu(name, u):
    rows, f2 = u.shape
    f = f2 // 2
    return _rowwise(name, lambda g, u: ((_f_swiglu(g, u),), ()), rows, [(u, f, 0), (u, f, 1)], [], [(f, BF16)], [],
                    tm=_pick(rows, (256, 128)))[0]


def _swiglu_bwd(name, da, u):
    rows, f2 = u.shape
    f = f2 // 2

    def fn(da, g, u):
        _, vjp = jax.vjp(_f_swiglu, g, u)
        return (jnp.concatenate(vjp(da), axis=1),), ()

    return _rowwise(name, fn, rows, [(da, f, 0), (u, f, 0), (u, f, 1)], [], [(f2, BF16)], [],
                    tm=_pick(rows, (256, 128)))[0]


def _gate_fwd(name, p, wf, wb, bf, bb):
    rows = p.shape[0]
    return _rowwise(name, lambda z, wf, wb, bf, bb: (_f_gate(z, wf, wb, bf, bb), ()), rows,
                    [(p, LANES, C_Z // LANES)], [wf, wb, bf, bb], [(GKW, F32)] * 2, [])


def _gate_bwd(name, p, dla_f, dla_b, wf, wb, bf, bb):
    rows = p.shape[0]

    def fn(z, dlf, dlb, wf, wb, bf, bb):
        _, vjp = jax.vjp(_f_gate, z, wf, wb, bf, bb)
        dz, dwf, dwb, dbf, dbb = vjp((dlf, dlb))
        return (dz,), (dwf, dwb, dbf, dbb)

    return _rowwise(name, fn, rows, [(p, LANES, C_Z // LANES), (dla_f, GKW, 0), (dla_b, GKW, 0)],
                    [wf, wb, bf, bb], [(LANES, BF16)], [(LANES, GKW), (LANES, GKW), (1, GKW), (1, GKW)])


def _head_mean_matrix():
    h = np.arange(GVW) // GLA_DV
    return jnp.asarray((h[:, None] == h[None, :]).astype(np.float32) / GLA_DV)


def _gla_out(name, attn, of, ob, p, gt):
    rows = of.shape[0]
    bd = _head_mean_matrix()
    fn = lambda attn, of, ob, gg, gt, bd: ((jnp.concatenate([attn, _f_gla_out(of, ob, gg, gt, bd)], axis=1),), ())
    return _rowwise(name, fn, rows, [(attn, QW, 0), (of, GVW, 0), (ob, GVW, 0), (p, GVW, C_GG // GVW)], [gt, bd],
                    [(MIX, BF16)], [])[0]


def _gla_out_bwd(name, dmix, of, ob, p, gt):
    rows = of.shape[0]
    bd = _head_mean_matrix()

    def fn(dm, of, ob, gg, gt, bd):
        _, vjp = jax.vjp(lambda of, gg, gt: _f_gla_out(of, ob, gg, gt, bd), of, gg, gt)
        do, dgg, dgt = vjp(dm)
        return (do, dgg), (dgt,)

    return _rowwise(name, fn, rows, [(dmix, GVW, 1), (of, GVW, 0), (ob, GVW, 0), (p, GVW, C_GG // GVW)], [gt, bd],
                    [(GVW, F32), (GVW, BF16)], [(1, GVW)])


def _rope_tables(n_tokens):
    t = jnp.arange(n_tokens)
    row = (t // GRID_W).astype(F32)
    col = (t % GRID_W).astype(F32)
    half = HEAD_DIM // 2
    inv_freq = ROPE_BASE ** (-jnp.arange(0, half, 2, dtype=F32) / half)
    ang_r = row[:, None] * inv_freq[None, :]
    ang_c = col[:, None] * inv_freq[None, :]
    ang = jnp.concatenate([ang_r, ang_r, ang_c, ang_c], axis=-1)
    sign = jnp.concatenate([-jnp.ones((16,), F32), jnp.ones((16,), F32)] * 2)
    cos, sin = jnp.cos(ang), jnp.sin(ang) * sign[None, :]
    return jnp.tile(cos, (1, 2)), jnp.tile(sin, (1, 2))


def _rot_pairs(x):
    w = x.shape[-1]
    lane = lax.broadcasted_iota(jnp.int32, x.shape, x.ndim - 1)
    return jnp.where((lane % 32) < 16, pltpu.roll(x, w - 16, x.ndim - 1), pltpu.roll(x, 16, x.ndim - 1))


def _rope_apply(x, cos, sin_signed, inverse):
    reps = x.shape[-1] // LANES
    cos = jnp.concatenate([cos] * reps, axis=-1) if reps > 1 else cos
    sin = jnp.concatenate([sin_signed] * reps, axis=-1) if reps > 1 else sin_signed
    if inverse:
        return x * cos + _rot_pairs(x * sin)
    return x * cos + _rot_pairs(x) * sin


def _rope_fwd(name, p, cos, sin):
    rows = p.shape[0]

    def fn(q, k, v, cos, sin):
        return (_rope_apply(q, cos, sin, False), _rope_apply(k, cos, sin, False), v), ()

    return _rowwise(name, fn, rows, [(p, QW, 0), (p, KVW, C_K // KVW), (p, KVW, C_V // KVW), (cos, LANES, 0),
                                     (sin, LANES, 0)], [], [(QW, BF16), (KVW, BF16), (KVW, BF16)], [])


def _rope_bwd(name, dq, dk, cos, sin):
    rows = dq.shape[0]

    def fn(dq, dk, cos, sin):
        return (_rope_apply(dq, cos, sin, True), _rope_apply(dk, cos, sin, True)), ()

    return _rowwise(name, fn, rows, [(dq, QW, 0), (dk, KVW, 0), (cos, LANES, 0), (sin, LANES, 0)], [],
                    [(QW, BF16), (KVW, BF16)], [])


def _f_attn(qs, kws, vws, kcs, vcs, sink, n, n_tokens):
    i = lax.broadcasted_iota(jnp.int32, (BLOCK, 3 * BLOCK), 0)
    j = lax.broadcasted_iota(jnp.int32, (BLOCK, 3 * BLOCK), 1)
    kpos = (n - 1) * BLOCK + j
    mask = (jnp.abs(j - BLOCK - i) <= WINDOW) & (kpos >= 0) & (kpos < n_tokens)
    head_id = lax.broadcasted_iota(jnp.int32, (1, ATT_HEADS), 1)
    scale = HEAD_DIM ** -0.5
    outs = []
    for hq in range(ATT_HEADS):
        h = hq // ATT_GROUP
        s_w = jnp.where(mask, _nt(qs[hq], kws[h]) * scale, NEG_INF)
        s_c = _nt(qs[hq], kcs[h]) * scale
        sk = jnp.sum(jnp.where(head_id == hq, sink, 0.0), axis=-1, keepdims=True)
        m = lax.stop_gradient(jnp.maximum(jnp.maximum(jnp.max(s_w, axis=-1, keepdims=True),
                                                      jnp.max(s_c, axis=-1, keepdims=True)), sk))
        pw, pc = jnp.exp(s_w - m), jnp.exp(s_c - m)
        den = jnp.sum(pw, axis=-1, keepdims=True) + jnp.sum(pc, axis=-1, keepdims=True) + jnp.exp(sk - m)
        outs.append((_nn(pw, vws[h]) + _nn(pc, vcs[h])) / den)
    return tuple(outs)


def _attn_loads(n, q_ref, kp_ref, vp_ref, kc_ref, vc_ref):
    r0 = pl.multiple_of(n * BLOCK, BLOCK)
    hs = lambda h: slice(h * HEAD_DIM, (h + 1) * HEAD_DIM)
    qs = [q_ref[:, hs(h)].astype(F32) for h in range(ATT_HEADS)]
    kws = [kp_ref[pl.ds(r0, 3 * BLOCK), hs(h)].astype(F32) for h in range(ATT_KV_HEADS)]
    vws = [vp_ref[pl.ds(r0, 3 * BLOCK), hs(h)].astype(F32) for h in range(ATT_KV_HEADS)]
    kcs = [kc_ref[:, hs(h)].astype(F32) for h in range(ATT_KV_HEADS)]
    vcs = [vc_ref[:, hs(h)].astype(F32) for h in range(ATT_KV_HEADS)]
    return r0, hs, qs, kws, vws, kcs, vcs


def _attn_specs(s, c):
    full = lambda shape: pl.BlockSpec(shape, lambda n: (0, 0))
    return [pl.BlockSpec((BLOCK, QW), lambda n: (n, 0)), full((s + 2 * BLOCK, KVW)), full((s + 2 * BLOCK, KVW)),
            full((c, KVW)), full((c, KVW)), full((1, ATT_HEADS))]


def _attn_fwd(q, kp, vp, kc, vc, sink):
    s, c = q.shape[0], kc.shape[0]

    def body(q_ref, kp_ref, vp_ref, kc_ref, vc_ref, sink_ref, o_ref):
        n = pl.program_id(0)
        _, hs, qs, kws, vws, kcs, vcs = _attn_loads(n, q_ref, kp_ref, vp_ref, kc_ref, vc_ref)
        outs = _f_attn(qs, kws, vws, kcs, vcs, sink_ref[...], n, s)
        for h in range(ATT_HEADS):
            o_ref[:, hs(h)] = outs[h].astype(o_ref.dtype)

    return pl.pallas_call(
        body, name="attn_fwd", grid=(s // BLOCK,), in_specs=_attn_specs(s, c),
        out_specs=pl.BlockSpec((BLOCK, QW), lambda n: (n, 0)), out_shape=jax.ShapeDtypeStruct((s, QW), BF16),
        compiler_params=_cp("parallel"),
    )(q, kp, vp, kc, vc, sink)


def _attn_bwd(do, q, kp, vp, kc, vc, sink):
    s, c = q.shape[0], kc.shape[0]

    def body(do_ref, q_ref, kp_ref, vp_ref, kc_ref, vc_ref, sink_ref, dq_ref, dkp_ref, dvp_ref, dkc_ref, dvc_ref,
             dsink_ref):
        n = pl.program_id(0)

        @pl.when(n == 0)
        def _():
            for r in (dkp_ref, dvp_ref, dkc_ref, dvc_ref, dsink_ref):
                r[...] = jnp.zeros_like(r)

        r0, hs, qs, kws, vws, kcs, vcs = _attn_loads(n, q_ref, kp_ref, vp_ref, kc_ref, vc_ref)
        _, vjp = jax.vjp(lambda qs, kws, vws, kcs, vcs, sink: _f_attn(qs, kws, vws, kcs, vcs, sink, n, s),
                         qs, kws, vws, kcs, vcs, sink_ref[...])
        dqs, dkws, dvws, dkcs, dvcs, dsink = vjp(tuple(do_ref[:, hs(h)].astype(F32) for h in range(ATT_HEADS)))
        for h in range(ATT_HEADS):
            dq_ref[:, hs(h)] = dqs[h]
        for h in range(ATT_KV_HEADS):
            dkp_ref[pl.ds(r0, 3 * BLOCK), hs(h)] += dkws[h]
            dvp_ref[pl.ds(r0, 3 * BLOCK), hs(h)] += dvws[h]
            dkc_ref[:, hs(h)] += dkcs[h]
            dvc_ref[:, hs(h)] += dvcs[h]
        dsink_ref[...] += dsink

    full = lambda shape: pl.BlockSpec(shape, lambda n: (0, 0))
    return pl.pallas_call(
        body, name="attn_bwd", grid=(s // BLOCK,),
        in_specs=[pl.BlockSpec((BLOCK, QW), lambda n: (n, 0))] + _attn_specs(s, c),
        out_specs=[pl.BlockSpec((BLOCK, QW), lambda n: (n, 0)), full((s + 2 * BLOCK, KVW)), full((s + 2 * BLOCK, KVW)),
                   full((c, KVW)), full((c, KVW)), full((1, ATT_HEADS))],
        out_shape=[jax.ShapeDtypeStruct((s, QW), F32), jax.ShapeDtypeStruct((s + 2 * BLOCK, KVW), F32),
                   jax.ShapeDtypeStruct((s + 2 * BLOCK, KVW), F32), jax.ShapeDtypeStruct((c, KVW), F32),
                   jax.ShapeDtypeStruct((c, KVW), F32), jax.ShapeDtypeStruct((1, ATT_HEADS), F32)],
        compiler_params=_cp("arbitrary"),
    )(do, q, kp, vp, kc, vc, sink)


def _gla_masks():
    hk = np.arange(GKW) // GLA_DK
    hv = np.arange(GVW) // GLA_DV
    head_k = (np.arange(GLA_HEADS)[:, None] == hk[None, :]).astype(np.float32)
    head_v = (np.arange(GLA_HEADS)[:, None] == hv[None, :]).astype(np.float32)
    bd_t = (hv[:, None] == hk[None, :]).astype(np.float32)
    return jnp.asarray(head_k), jnp.asarray(head_v), jnp.asarray(bd_t)


def _tri(n, rev, strict=False):
    i = lax.broadcasted_iota(jnp.int32, (n, n), 0)
    j = lax.broadcasted_iota(jnp.int32, (n, n), 1)
    if strict:
        keep = (j > i) if rev else (j < i)
    else:
        keep = (j >= i) if rev else (j <= i)
    return keep


def _f_gla_chunk(q, k, v, la, st, head_k, head_v, bd_t, rev):
    keep = _tri(GLA_CHUNK, rev)
    b = _nn_hi(keep.astype(F32), la)
    bl = jnp.sum(la, axis=0, keepdims=True)
    qd = q * (GLA_DK ** -0.5) * jnp.exp(b)
    ki = k * jnp.exp(-b)
    kd = k * jnp.exp(bl - b)
    q_heads = (qd[None, :, :] * head_k[:, None, :]).reshape(GLA_HEADS * GLA_CHUNK, GKW)
    a_all = _nt(q_heads, ki).reshape(GLA_HEADS, GLA_CHUNK, GLA_CHUNK)
    a_all = jnp.where(keep[None, :, :], a_all, 0.0).reshape(GLA_HEADS * GLA_CHUNK, GLA_CHUNK)
    o_all = _nn(a_all, v).reshape(GLA_HEADS, GLA_CHUNK, GVW)
    intra = jnp.sum(o_all * head_v[:, None, :], axis=0)
    inter = _nt(qd, st)
    st_new = st * jnp.exp(bl) + bd_t * _tn(v, kd)
    return intra + inter, st_new


def _gla_specs(s, tb, order):
    return [pl.BlockSpec((tb, GKW), lambda i: (order(i), C_GQ // GKW)),
            pl.BlockSpec((tb, GKW), lambda i: (order(i), C_GK // GKW)),
            pl.BlockSpec((tb, GVW), lambda i: (order(i), C_GV // GVW)),
            pl.BlockSpec((tb, GKW), lambda i: (order(i), 0))]


GLA_BLOCK_CHUNKS = 2


def _gla_fwd(name, p, la, st0, rev):
    s = p.shape[0]
    tb = GLA_BLOCK_CHUNKS * GLA_CHUNK
    nblk = s // tb
    order = (lambda i: nblk - 1 - i) if rev else (lambda i: i)
    masks = _gla_masks()

    def body(q_ref, k_ref, v_ref, la_ref, st0_ref, hk_ref, hv_ref, bd_ref, o_ref, sts_ref, st_ref):
        @pl.when(pl.program_id(0) == 0)
        def _():
            st_ref[...] = st0_ref[...]

        st = st_ref[...]
        sts_ref[0] = st
        chunks = range(GLA_BLOCK_CHUNKS)
        for ci in (reversed(chunks) if rev else chunks):
            rows = slice(ci * GLA_CHUNK, (ci + 1) * GLA_CHUNK)
            o, st = _f_gla_chunk(q_ref[rows, :], k_ref[rows, :], v_ref[rows, :], la_ref[rows, :], st,
                                 hk_ref[...], hv_ref[...], bd_ref[...], rev)
            o_ref[rows, :] = o
        st_ref[...] = st

    full = lambda a: pl.BlockSpec(a.shape, lambda i: (0,) * a.ndim)
    return pl.pallas_call(
        body, name=name, grid=(nblk,),
        in_specs=_gla_specs(s, tb, order) + [full(st0)] + [full(m) for m in masks],
        out_specs=[pl.BlockSpec((tb, GVW), lambda i: (order(i), 0)),
                   pl.BlockSpec((1, GVW, GKW), lambda i: (order(i), 0, 0))],
        out_shape=[jax.ShapeDtypeStruct((s, GVW), F32), jax.ShapeDtypeStruct((nblk, GVW, GKW), F32)],
        scratch_shapes=[pltpu.VMEM((GVW, GKW), F32)],
        compiler_params=_cp("arbitrary"),
    )(p, p, p, la, st0, *masks)


def _gla_bwd(name, p, la, sts, do, prev, rev, after=None):
    s = p.shape[0]
    tb = GLA_BLOCK_CHUNKS * GLA_CHUNK
    nblk = s // tb
    order = (lambda i: i) if rev else (lambda i: nblk - 1 - i)
    masks = _gla_masks()
    n_prev = 0 if prev is None else 3
    follow = () if after is None else (after,)

    def body(*refs):
        q_ref, k_ref, v_ref, la_ref, sts_ref, do_ref, hk_ref, hv_ref, bd_ref = refs[:9]
        prev_refs = refs[9:9 + n_prev]
        dq_ref, dk_ref, dv_ref, dla_ref, dst0_ref, dst_ref = refs[9 + n_prev + len(follow):]

        @pl.when(pl.program_id(0) == 0)
        def _():
            dst_ref[...] = jnp.zeros_like(dst_ref)

        def block(q, k, v, la, st):
            outs = [None] * GLA_BLOCK_CHUNKS
            chunks = range(GLA_BLOCK_CHUNKS)
            for ci in (reversed(chunks) if rev else chunks):
                rows = slice(ci * GLA_CHUNK, (ci + 1) * GLA_CHUNK)
                outs[ci], st = _f_gla_chunk(q[ci], k[ci], v[ci], la[ci], st, hk_ref[...], hv_ref[...], bd_ref[...],
                                            rev)
            return tuple(outs), st

        split = lambda r: tuple(r[ci * GLA_CHUNK:(ci + 1) * GLA_CHUNK, :].astype(F32)
                                for ci in range(GLA_BLOCK_CHUNKS))
        _, vjp = jax.vjp(block, split(q_ref), split(k_ref), split(v_ref), split(la_ref), sts_ref[0])
        dq, dk, dv, dla, dst = vjp((split(do_ref), dst_ref[...]))
        for ci in range(GLA_BLOCK_CHUNKS):
            rows = slice(ci * GLA_CHUNK, (ci + 1) * GLA_CHUNK)
            if n_prev:
                dq_ref[rows, :] = dq[ci] + prev_refs[0][rows, :]
                dk_ref[rows, :] = dk[ci] + prev_refs[1][rows, :]
                dv_ref[rows, :] = dv[ci] + prev_refs[2][rows, :]
            else:
                dq_ref[rows, :], dk_ref[rows, :], dv_ref[rows, :] = dq[ci], dk[ci], dv[ci]
            dla_ref[rows, :] = dla[ci]
        dst_ref[...] = dst
        dst0_ref[...] = dst

    full = lambda a: pl.BlockSpec(a.shape, lambda i: (0,) * a.ndim)
    blk = lambda w: pl.BlockSpec((tb, w), lambda i: (order(i), 0))
    prev_specs = [blk(GKW), blk(GKW), blk(GVW)] if n_prev else []
    return pl.pallas_call(
        body, name=name, grid=(nblk,),
        in_specs=_gla_specs(s, tb, order) + [pl.BlockSpec((1, GVW, GKW), lambda i: (order(i), 0, 0)), blk(GVW)]
        + [full(m) for m in masks] + prev_specs + [pl.BlockSpec(memory_space=pl.ANY)] * len(follow),
        out_specs=[blk(GKW), blk(GKW), blk(GVW), blk(GKW), pl.BlockSpec((GVW, GKW), lambda i: (0, 0))],
        out_shape=[jax.ShapeDtypeStruct((s, GKW), F32), jax.ShapeDtypeStruct((s, GKW), F32),
                   jax.ShapeDtypeStruct((s, GVW), F32), jax.ShapeDtypeStruct((s, GKW), F32),
                   jax.ShapeDtypeStruct((GVW, GKW), F32)],
        scratch_shapes=[pltpu.VMEM((GVW, GKW), F32)],
        compiler_params=_cp("arbitrary"),
    )(p, p, p, la, sts, do, *masks, *(prev or ()), *follow)


def _f_ctx_state(k, v, la_f, la_b, bd_t):
    c = k.shape[0]
    after = _nn_hi(_tri(c, True, strict=True).astype(F32), la_f)
    before = _nn_hi(_tri(c, False, strict=True).astype(F32), la_b)
    return bd_t * _tn(v, k * jnp.exp(after)), bd_t * _tn(v, k * jnp.exp(before))


def _ctx_state(pc, la_f, la_b):
    c = pc.shape[0]
    bd_t = _gla_masks()[2]

    def body(k_ref, v_ref, lf_ref, lb_ref, bd_ref, sf_ref, sb_ref):
        sf_ref[...], sb_ref[...] = _f_ctx_state(k_ref[...], v_ref[...], lf_ref[...], lb_ref[...], bd_ref[...])

    full = lambda a: pl.BlockSpec(a.shape, lambda i: (0, 0))
    return pl.pallas_call(
        body, name="ctx_state_fwd", grid=(1,),
        in_specs=[pl.BlockSpec((c, GKW), lambda i: (0, C_GK // GKW)), pl.BlockSpec((c, GVW), lambda i: (0, C_GV // GVW)),
                  full(la_f), full(la_b), full(bd_t)],
        out_specs=[pl.BlockSpec((GVW, GKW), lambda i: (0, 0))] * 2,
        out_shape=[jax.ShapeDtypeStruct((GVW, GKW), F32)] * 2,
        compiler_params=_cp("arbitrary"),
    )(pc, pc, la_f, la_b, bd_t)


def _ctx_state_bwd(pc, la_f, la_b, dsf, dsb):
    c = pc.shape[0]
    bd_t = _gla_masks()[2]

    def body(k_ref, v_ref, lf_ref, lb_ref, bd_ref, dsf_ref, dsb_ref, dk_ref, dv_ref, dlf_ref, dlb_ref):
        _, vjp = jax.vjp(lambda k, v, lf, lb: _f_ctx_state(k, v, lf, lb, bd_ref[...]),
                         k_ref[...], v_ref[...], lf_ref[...], lb_ref[...])
        dk, dv, dlf, dlb = vjp((dsf_ref[...], dsb_ref[...]))
        dk_ref[...], dv_ref[...] = dk.astype(BF16), dv.astype(BF16)
        dlf_ref[...], dlb_ref[...] = dlf, dlb

    full = lambda a: pl.BlockSpec(a.shape, lambda i: (0, 0))
    return pl.pallas_call(
        body, name="ctx_state_bwd", grid=(1,),
        in_specs=[pl.BlockSpec((c, GKW), lambda i: (0, C_GK // GKW)), pl.BlockSpec((c, GVW), lambda i: (0, C_GV // GVW)),
                  full(la_f), full(la_b), full(bd_t), full(dsf), full(dsb)],
        out_specs=[pl.BlockSpec((c, GKW), lambda i: (0, 0)), pl.BlockSpec((c, GVW), lambda i: (0, 0)),
                   pl.BlockSpec((c, GKW), lambda i: (0, 0)), pl.BlockSpec((c, GKW), lambda i: (0, 0))],
        out_shape=[jax.ShapeDtypeStruct((c, GKW), BF16), jax.ShapeDtypeStruct((c, GVW), BF16),
                   jax.ShapeDtypeStruct((c, GKW), F32), jax.ShapeDtypeStruct((c, GKW), F32)],
        compiler_params=_cp("arbitrary"),
    )(pc, pc, la_f, la_b, bd_t, dsf, dsb)


_SRC_COLS = ((0, QW), (QW + 2 * KVW + 2 * GKW, GVW), (QW + 2 * KVW + 2 * GKW + GVW, GVW), (QW, KVW), (QW + KVW, KVW),
             (QW + 2 * KVW, GKW), (QW + 2 * KVW + GKW, GKW), (IN_COLS - 2 * GATE_RANK, 2 * GATE_RANK))
_DST_COLS = (C_Q, C_GV, C_GG, C_K, C_V, C_GQ, C_GK, C_Z)


def _pack_w_in(w_in):
    parts = [w_in[:, s:s + n] for s, n in _SRC_COLS]
    parts.append(jnp.zeros((w_in.shape[0], IN_PAD - C_Z - 2 * GATE_RANK), w_in.dtype))
    return jnp.concatenate(parts, axis=1)


def _unpack_w_in_grad(g):
    by_src = sorted(zip(_SRC_COLS, _DST_COLS))
    return jnp.concatenate([g[:, d:d + n] for (_, n), d in by_src], axis=1)


def _prep_weights(w_in, w_gate_fwd, w_gate_bwd):
    pad_rows = lambda w, at: jnp.zeros((LANES, GKW), F32).at[at:at + GATE_RANK].set(w)
    return {"w_in": _pack_w_in(w_in).astype(BF16), "wg_f": pad_rows(w_gate_fwd, 0),
            "wg_b": pad_rows(w_gate_bwd, GATE_RANK)}


def _local_step(x, ctx, target, ada, ada_c, w, late_weights, reduce_behind=None):
    s, d = x.shape
    sh1, sc1, gt1, sh2, sc2, gt2 = [ada[:, i * d:(i + 1) * d] for i in range(6)]
    sh1c, sc1c = ada_c[:, :d], ada_c[:, d:2 * d]
    cos, sin = _rope_tables(s)
    gt = jnp.tile(w["g_gla_norm"], (1, GLA_HEADS))

    h = _norm_mod("pre_mix", x, w["g_pre_mix"], sh1, sc1)
    hc = _norm_mod("pre_mix_ctx", ctx, w["g_pre_mix"], sh1c, sc1c)
    p = _mm("proj_in", h, w["w_in"], "nn")
    pc = _mm("proj_in_ctx", hc, w["w_in"], "nn")
    q_rot, k_rot, v_b = _rope_fwd("rope", p, cos, sin)
    pad = ((BLOCK, BLOCK), (0, 0))
    kp, vp = jnp.pad(k_rot, pad), jnp.pad(v_b, pad)
    kc, vc = pc[:, C_K:C_K + KVW].astype(BF16), pc[:, C_V:C_V + KVW].astype(BF16)
    attn = _attn_fwd(q_rot, kp, vp, kc, vc, w["attn_sink"])
    gate_w = (w["wg_f"], w["wg_b"], w["b_gate_fwd"], w["b_gate_bwd"])
    la_f, la_b = _gate_fwd("gate", p, *gate_w)
    la_fc, la_bc = _gate_fwd("gate_ctx", pc, *gate_w)
    st_f0, st_b0 = _ctx_state(pc, la_fc, la_bc)
    o_f, sts_f = _gla_fwd("gla_fwd_f", p, la_f, st_f0, False)
    o_b, sts_b = _gla_fwd("gla_fwd_b", p, la_b, st_b0, True)
    mix = _gla_out("gla_out", attn, o_f, o_b, p, gt)
    w_out, w_ffn_in_t, w_ffn_out = late_weights(attn)
    y = _mm("proj_out", mix, w_out, "nn")
    x1 = _post_res("post_mix", x, y, w["g_post_mix"], gt1)
    h2 = _norm_mod("pre_ffn", x1, w["g_pre_ffn"], sh2, sc2)
    u = _mm("ffn_in", h2, w_ffn_in_t, "nt")
    a = _swiglu("swiglu", u)
    f = _mm("ffn_out", a, w_ffn_out, "nn")
    dx2, loss = _post_res_loss("post_ffn_loss", x1, f, w["g_post_ffn"], gt2, target)

    g = {}
    df, g["g_post_ffn"], dgt2 = _post_res_bwd("post_ffn_bwd", dx2, f, w["g_post_ffn"], gt2)
    da = _mm("ffn_out_dx", df, w_ffn_out, "nt")
    g["w_ffn_out"] = _mm("ffn_out_dw", a, df, "tn")
    du = _swiglu_bwd("swiglu_bwd", da, u)
    dh2 = _mm("ffn_in_dx", du, w_ffn_in_t, "nn")
    g["w_ffn_in_t"] = _mm("ffn_in_dw", du, h2, "tn")
    dx1, g["g_pre_ffn"], dsh2, dsc2 = _norm_mod_bwd("pre_ffn_bwd", dh2, dx2, x1, w["g_pre_ffn"], sh2, sc2)
    dy, g["g_post_mix"], dgt1 = _post_res_bwd("post_mix_bwd", dx1, y, w["g_post_mix"], gt1)
    dmix = _mm("proj_out_dx", dy, w_out, "nt", BF16)
    g["w_out"] = _mm("proj_out_dw", mix, dy, "tn")
    rb, sink, token = reduce_behind, w["attn_sink"], None
    if rb is not None:
        gt = _behind(gt, rb.start(g["w_out"], g["w_ffn_out"], g["w_ffn_in_t"]))
    d_o, dgg, dgt = _gla_out_bwd("gla_out_bwd", dmix, o_f, o_b, p, gt)
    g["g_gla_norm"] = jnp.sum(dgt.reshape(GLA_HEADS, GLA_DV), axis=0, keepdims=True)
    if rb is not None:
        token = rb.pair(dgg)
    dgq, dgk, dgv, dla_f, dst_f0 = _gla_bwd("gla_bwd_f", p, la_f, sts_f, d_o, None, False, token)
    dgq, dgk, dgv, dla_b, dst_b0 = _gla_bwd("gla_bwd_b", p, la_b, sts_b, d_o, (dgq, dgk, dgv), True)
    if rb is not None:
        sink = _behind(sink, rb.total(dgq))
    dgkc, dgvc, dla_fc, dla_bc = _ctx_state_bwd(pc, la_fc, la_bc, dst_f0, dst_b0)
    dz, dwf, dwb, dbf, dbb = _gate_bwd("gate_bwd", p, dla_f, dla_b, *gate_w)
    dzc, dwfc, dwbc, dbfc, dbbc = _gate_bwd("gate_ctx_bwd", pc, dla_fc, dla_bc, *gate_w)
    g["w_gate_fwd"] = (dwf + dwfc)[:GATE_RANK]
    g["w_gate_bwd"] = (dwb + dwbc)[GATE_RANK:2 * GATE_RANK]
    g["b_gate_fwd"], g["b_gate_bwd"] = dbf + dbfc, dbb + dbbc
    dq_rot, dkp, dvp, dkc, dvc, g["attn_sink"] = _attn_bwd(dmix, q_rot, kp, vp, kc, vc, sink)
    if rb is not None:
        g["behind"] = rb.result(dq_rot)
    dq, dk = _rope_bwd("rope_bwd", dq_rot, dkp[BLOCK:BLOCK + s], cos, sin)
    dp = jnp.concatenate([dq, dgv.astype(BF16), dgg, dk, dvp[BLOCK:BLOCK + s].astype(BF16), dgq.astype(BF16),
                          dgk.astype(BF16), dz], axis=1)
    c_rows = ctx.shape[0]
    zeros = lambda n: jnp.zeros((c_rows, n), BF16)
    dpc = jnp.concatenate([zeros(QW), dgvc, zeros(GVW), dkc.astype(BF16), dvc.astype(BF16), zeros(GKW), dgkc, dzc],
                          axis=1)
    dh = _mm("proj_in_dx", dp, w["w_in"], "nt")
    dhc = _mm("proj_in_ctx_dx", dpc, w["w_in"], "nt")
    g["w_in"] = _mm("proj_in_dw", h, dp, "tn", init=_mm("proj_in_ctx_dw", hc, dpc, "tn"))
    dx, dg_a, dsh1, dsc1 = _norm_mod_bwd("pre_mix_bwd", dh, dx1, x, w["g_pre_mix"], sh1, sc1)
    _, dg_b, dsh1c, dsc1c = _norm_mod_bwd("pre_mix_ctx_bwd", dhc, jnp.zeros_like(dhc), ctx, w["g_pre_mix"], sh1c,
                                          sc1c)
    g["g_pre_mix"] = dg_a + dg_b
    d_ada = jnp.concatenate([dsh1, dsc1, dgt1, dsh2, dsc2, dgt2], axis=1)
    d_ada_c = jnp.concatenate([dsh1c, dsc1c, jnp.zeros((1, 4 * d), F32)], axis=1)
    return loss, dx, g, d_ada, d_ada_c


HBM = pl.BlockSpec(memory_space=pltpu.HBM)
N_DEV, N_CHIP = 8, 4


def _place():
    x, y, c = lax.axis_index("x"), lax.axis_index("y"), lax.axis_index("c")
    return x, y, c, [(1 - x, y), (x, 1 - y), (1 - x, 1 - y)]


def _row_tile(n, mult, cap):
    return max(t for t in range(mult, min(n, cap) + 1, mult) if n % t == 0)


def _ag_small(name, v):
    def body(v_ref, out_ref, send_sems, recv_sems):
        x, y, c, _ = _place()
        out_ref[4 * x + 2 * y + c] = v_ref[...]

        def peer(r):
            return ((1 - x) if r & 4 else x, (1 - y) if r & 2 else y, (1 - c) if r & 1 else c)

        def copy(r, block):
            px, py, pc = block
            return pltpu.make_async_remote_copy(
                src_ref=v_ref, dst_ref=out_ref.at[4 * px + 2 * py + pc], send_sem=send_sems.at[r - 1],
                recv_sem=recv_sems.at[r - 1], device_id=peer(r), device_id_type=MESH)

        sends = [copy(r, (x, y, c)) for r in range(1, N_DEV)]
        for cp in sends:
            cp.start()
        for r in range(1, N_DEV):
            copy(r, peer(r)).wait_recv()
        for cp in sends:
            cp.wait_send()

    return pl.pallas_call(
        body, name=name, out_shape=jax.ShapeDtypeStruct((N_DEV,) + v.shape, v.dtype),
        in_specs=[pl.BlockSpec(memory_space=pltpu.VMEM)], out_specs=pl.BlockSpec(memory_space=pltpu.VMEM),
        scratch_shapes=[pltpu.SemaphoreType.DMA((N_DEV - 1,)), pltpu.SemaphoreType.DMA((N_DEV - 1,))],
    )(v)


def _halves(c, rows, mult):
    hr = rows // 2
    return pl.ds(pl.multiple_of(c * hr, mult), hr), pl.ds(pl.multiple_of((1 - c) * hr, mult), hr)


def _ag_shards(name, shard):
    rows = shard.shape[0]

    def body(w_ref, out_ref, send_sems, recv_sems, local_sem):
        x, y, c, chips = _place()
        mine_half, other_half = _halves(c, rows, 16)
        me = 2 * x + y
        mine = pltpu.make_async_copy(w_ref, out_ref.at[me], local_sem)
        mine.start()

        def copy(k, src, chip, half, to):
            return pltpu.make_async_remote_copy(
                src_ref=src, dst_ref=out_ref.at[chip, half], send_sem=send_sems.at[k], recv_sem=recv_sems.at[k],
                device_id=to, device_id_type=MESH)

        first = [copy(j, w_ref.at[mine_half], me, mine_half, (px, py, c)) for j, (px, py) in enumerate(chips)]
        for cp in first:
            cp.start()
        passed = []
        for j, (px, py) in enumerate(chips):
            pk = 2 * px + py
            copy(j, w_ref.at[mine_half], pk, mine_half, (px, py, c)).wait_recv()
            cp = copy(3 + j, out_ref.at[pk, mine_half], pk, mine_half, (x, y, 1 - c))
            cp.start()
            passed.append(cp)
        for j, (px, py) in enumerate(chips):
            copy(3 + j, w_ref.at[mine_half], 2 * px + py, other_half, (x, y, 1 - c)).wait_recv()
        for cp in first + passed:
            cp.wait_send()
        mine.wait()

    return pl.pallas_call(
        body, name=name, out_shape=jax.ShapeDtypeStruct((N_CHIP,) + shard.shape, shard.dtype),
        in_specs=[HBM], out_specs=HBM,
        scratch_shapes=[pltpu.SemaphoreType.DMA((6,)), pltpu.SemaphoreType.DMA((6,)), pltpu.SemaphoreType.DMA],
    )(shard)


def _swap_half(name, g):
    n_sh, rows, n = g.shape

    def body(g_ref, a_ref, send_sem, recv_sem):
        x, y, c, _ = _place()
        _, other_half = _halves(c, rows, 8)
        cp = pltpu.make_async_remote_copy(
            src_ref=g_ref.at[pl.ds(0, n_sh), other_half], dst_ref=a_ref, send_sem=send_sem, recv_sem=recv_sem,
            device_id=(x, y, 1 - c), device_id_type=MESH)
        cp.start()
        cp.wait()

    return pl.pallas_call(
        body, name=name, out_shape=jax.ShapeDtypeStruct((n_sh, rows // 2, n), g.dtype), in_specs=[HBM], out_specs=HBM,
        scratch_shapes=[pltpu.SemaphoreType.DMA, pltpu.SemaphoreType.DMA],
    )(g)


def _add_half(name, g, a, c_idx):
    n_sh, hr, n = a.shape
    tr = _row_tile(hr, 16, 1024)
    nb = hr // tr

    def body(c_ref, g_ref, a_ref, o_ref):
        o_ref[...] = (g_ref[...] + a_ref[...]).astype(o_ref.dtype)

    return pl.pallas_call(
        body, name=name, out_shape=jax.ShapeDtypeStruct(a.shape, BF16),
        grid_spec=pltpu.PrefetchScalarGridSpec(
            num_scalar_prefetch=1, grid=(n_sh, nb),
            in_specs=[pl.BlockSpec((1, tr, n), lambda s, i, c_ref: (s, c_ref[0] * nb + i, 0)),
                      pl.BlockSpec((1, tr, n), lambda s, i, c_ref: (s, i, 0))],
            out_specs=pl.BlockSpec((1, tr, n), lambda s, i, c_ref: (s, i, 0))),
        compiler_params=_cp("parallel", "parallel"),
    )(c_idx, g, a)


def _scatter_chips(name, h):
    def body(h_ref, b_ref, send_sems, recv_sems, local_sem):
        x, y, c, chips = _place()
        me = 2 * x + y
        mine = pltpu.make_async_copy(h_ref.at[me], b_ref.at[me], local_sem)
        mine.start()

        def copy(j, src_block, dst_block, to):
            return pltpu.make_async_remote_copy(
                src_ref=h_ref.at[src_block], dst_ref=b_ref.at[dst_block], send_sem=send_sems.at[j],
                recv_sem=recv_sems.at[j], device_id=to, device_id_type=MESH)

        sends = [copy(j, 2 * px + py, me, (px, py, c)) for j, (px, py) in enumerate(chips)]
        for cp in sends:
            cp.start()
        for j, (px, py) in enumerate(chips):
            copy(j, me, 2 * px + py, (px, py, c)).wait_recv()
        for cp in sends:
            cp.wait_send()
        mine.wait()

    return pl.pallas_call(
        body, name=name, out_shape=jax.ShapeDtypeStruct(h.shape, h.dtype), in_specs=[HBM], out_specs=HBM,
        scratch_shapes=[pltpu.SemaphoreType.DMA((3,)), pltpu.SemaphoreType.DMA((3,)), pltpu.SemaphoreType.DMA],
    )(h)


def _sum_chips(name, b):
    n_sh, hr, n = b.shape
    tr = _row_tile(hr, 16, 1024)

    def body(b0, b1, b2, b3, o_ref):
        o_ref[...] = ((b0[0].astype(F32) + b1[0].astype(F32)) + b2[0].astype(F32)) + b3[0].astype(F32)

    return pl.pallas_call(
        body, name=name, grid=(hr // tr,), out_shape=jax.ShapeDtypeStruct((hr, n), F32),
        in_specs=[pl.BlockSpec((1, tr, n), functools.partial(lambda i, k: (k, i, 0), k=k)) for k in range(n_sh)],
        out_specs=pl.BlockSpec((tr, n), lambda i: (i, 0)), compiler_params=_cp("parallel"),
    )(b, b, b, b)


def _share_half(name, f):
    hr, n = f.shape

    def body(f_ref, out_ref, send_sem, recv_sem, local_sem):
        x, y, c, _ = _place()
        mine_half, other_half = _halves(c, 2 * hr, 8)
        mine = pltpu.make_async_copy(f_ref, out_ref.at[mine_half], local_sem)
        mine.start()

        def copy(half):
            return pltpu.make_async_remote_copy(
                src_ref=f_ref, dst_ref=out_ref.at[half], send_sem=send_sem, recv_sem=recv_sem,
                device_id=(x, y, 1 - c), device_id_type=MESH)

        send = copy(mine_half)
        send.start()
        copy(other_half).wait_recv()
        send.wait_send()
        mine.wait()

    return pl.pallas_call(
        body, name=name, out_shape=jax.ShapeDtypeStruct((2 * hr, n), f.dtype), in_specs=[HBM], out_specs=HBM,
        scratch_shapes=[pltpu.SemaphoreType.DMA, pltpu.SemaphoreType.DMA, pltpu.SemaphoreType.DMA],
    )(f)


def _reduce_shards(name, g, c_idx):
    a = _swap_half(name + "_swap", g)
    h = _add_half(name + "_pair", g, a, c_idx)
    b = _scatter_chips(name + "_scatter", h)
    f = _sum_chips(name + "_sum", b)
    return _share_half(name + "_share", f)


SEM = pl.BlockSpec(memory_space=pltpu.SEMAPHORE)
ANY = pl.BlockSpec(memory_space=pl.ANY)
DATAFLOW = pltpu.SideEffectType.DATAFLOW_SIDE_EFFECTING


def _remote(src, dst, send_sems, recv_sems, k, to):
    return pltpu.make_async_remote_copy(src_ref=src, dst_ref=dst, send_sem=send_sems.at[k], recv_sem=recv_sems.at[k],
                                        device_id=to, device_id_type=MESH)


def _split_copy(name, src, land_shape, land_dtype, n, plan, after=None):
    after = jnp.zeros((8, LANES), F32) if after is None else after

    def start_body(src_ref, land_ref, after_ref, send_sems, recv_sems, src_thru, land_thru, token):
        for cp in plan(src_ref, land_ref, send_sems, recv_sems)[0]:
            cp.start()
        token[...] = jnp.zeros_like(token)

    sems = pltpu.SemaphoreType.DMA((n,))
    send_sems, recv_sems, src_thru, land_thru, token = pl.pallas_call(
        start_body, name=name + "_start",
        out_shape=(sems, sems, pltpu.HBM(src.shape, src.dtype), pltpu.HBM(land_shape, land_dtype),
                   jax.ShapeDtypeStruct((8, LANES), F32)),
        in_specs=(HBM, HBM, ANY), out_specs=(SEM, SEM, HBM, HBM, pl.BlockSpec(memory_space=pltpu.VMEM)),
        input_output_aliases={0: 2, 1: 3}, compiler_params=pltpu.CompilerParams(has_side_effects=DATAFLOW),
    )(pltpu.with_memory_space_constraint(src, pltpu.HBM),
      pltpu.with_memory_space_constraint(lax.empty(land_shape, land_dtype), pltpu.HBM), after)

    def wait(after):
        def wait_body(src_ref, land_ref, send_sems, recv_sems, after_ref, src_out, land_out):
            sent, received = plan(src_ref, land_ref, send_sems, recv_sems)
            for cp in sent:
                cp.wait_send()
            for cp in received:
                cp.wait_recv()

        return pl.pallas_call(
            wait_body, name=name + "_wait",
            out_shape=(pltpu.HBM(src.shape, src.dtype), pltpu.HBM(land_shape, land_dtype)),
            in_specs=(HBM, HBM, SEM, SEM, ANY), out_specs=(HBM, HBM), input_output_aliases={0: 0, 1: 1},
            compiler_params=pltpu.CompilerParams(has_side_effects=DATAFLOW),
        )(src_thru, land_thru, send_sems, recv_sems, after)

    return token, wait


def _behind(x, token):
    return x + token[0, 0]


def _plan_gather(src_ref, land_ref, send_sems, recv_sems):
    x, y, c, chips = _place()
    sent = [_remote(src_ref, land_ref.at[2 * x + y], send_sems, recv_sems, j, (px, py, c))
            for j, (px, py) in enumerate(chips)]
    received = [_remote(src_ref, land_ref.at[2 * px + py], send_sems, recv_sems, j, (px, py, c))
                for j, (px, py) in enumerate(chips)]
    return sent, received


def _plan_swap(src_ref, land_ref, send_sems, recv_sems):
    x, y, c, _ = _place()
    _, other_half = _halves(c, src_ref.shape[1], 8)
    cp = _remote(src_ref.at[pl.ds(0, src_ref.shape[0]), other_half], land_ref, send_sems, recv_sems, 0, (x, y, 1 - c))
    return [cp], [cp]


def _plan_scatter(src_ref, land_ref, send_sems, recv_sems):
    x, y, c, chips = _place()
    sent = [_remote(src_ref.at[2 * px + py], land_ref.at[2 * x + y], send_sems, recv_sems, j, (px, py, c))
            for j, (px, py) in enumerate(chips)]
    received = [_remote(src_ref.at[2 * px + py], land_ref.at[2 * px + py], send_sems, recv_sems, j, (px, py, c))
                for j, (px, py) in enumerate(chips)]
    return sent, received


def _plan_share(src_ref, land_ref, send_sems, recv_sems):
    x, y, c, _ = _place()
    mine_half, other_half = _halves(c, land_ref.shape[0], 8)
    return ([_remote(src_ref, land_ref.at[mine_half], send_sems, recv_sems, 0, (x, y, 1 - c))],
            [_remote(src_ref, land_ref.at[other_half], send_sems, recv_sems, 0, (x, y, 1 - c))])


class _GatherBehind:
    def __init__(self, name, shard, chip, after=None):
        self.chip = chip
        self.token, self.wait = _split_copy(name, shard, (N_CHIP,) + shard.shape, shard.dtype, 3, _plan_gather,
                                            after)

    def result(self, after):
        shard, land = self.wait(after)
        return lax.dynamic_update_slice(land, shard[None], (self.chip, 0, 0))


class _ReduceBehind:
    def __init__(self, name, chip, c, c_idx):
        self.name, self.chip, self.c, self.c_idx = name, chip, c, c_idx

    def start(self, *grads):
        g = jnp.stack([jnp.concatenate([t[k * (t.shape[0] // N_CHIP):(k + 1) * (t.shape[0] // N_CHIP)] for t in grads],
                                       axis=0) for k in range(N_CHIP)])
        n_sh, rows, n = g.shape
        token, self.wait = _split_copy(self.name + "_swap", g, (n_sh, rows // 2, n), g.dtype, 1, _plan_swap)
        return token

    def pair(self, after):
        g, a = self.wait(after)
        h = _add_half(self.name + "_pair", g, a, self.c_idx)
        token, self.wait = _split_copy(self.name + "_scatter", h, h.shape, h.dtype, 3, _plan_scatter)
        return token

    def total(self, after):
        h, b = self.wait(after)
        b = lax.dynamic_update_slice(b, lax.dynamic_slice_in_dim(h, self.chip, 1, axis=0), (self.chip, 0, 0))
        f = _sum_chips(self.name + "_sum", b)
        token, self.wait = _split_copy(self.name + "_share", f, (2 * f.shape[0], f.shape[1]), f.dtype, 1,
                                       _plan_share)
        return token

    def result(self, after):
        f, out = self.wait(after)
        return lax.dynamic_update_slice(out, f, (self.c * f.shape[0], 0))


def _f_adamw(w, g, m, v):
    m = ADAM_B1 * m + (1.0 - ADAM_B1) * g
    v = ADAM_B2 * v + (1.0 - ADAM_B2) * (g * g)
    m_hat = m / (1.0 - ADAM_B1 ** ADAM_STEP)
    v_hat = v / (1.0 - ADAM_B2 ** ADAM_STEP)
    return -ADAM_LR * (m_hat / (jnp.sqrt(v_hat) + ADAM_EPS) + ADAM_WD * w), m, v


def _adamw(name, w, g, m, v):
    rows, n = w.shape
    return _rowwise(name, lambda w, g, m, v: (_f_adamw(w, g, m, v), ()), rows, [(t, n, 0) for t in (w, g, m, v)], [],
                    [(n, F32)] * 3, [], tm=_row_tile(rows, 8, 256))


def _pack_rows(parts):
    rows = []
    for t in parts:
        t = t.reshape(-1)
        rows.append(jnp.pad(t, (0, -t.shape[0] % LANES)).reshape(-1, LANES))
    out = jnp.concatenate(rows, axis=0)
    return jnp.pad(out, ((0, -out.shape[0] % 8), (0, 0)))


def _unpack_rows(packed, shapes):
    out, r = [], 0
    for shp in shapes:
        n = int(np.prod(shp))
        nr = -(-n // LANES)
        out.append(packed[r:r + nr].reshape(-1)[:n].reshape(shp))
        r += nr
    return out


def _sum_blocks(name, g):
    def body(g_ref, o_ref):
        acc = g_ref[0]
        for k in range(1, g.shape[0]):
            acc = acc + g_ref[k]
        o_ref[...] = acc

    return pl.pallas_call(body, name=name, out_shape=jax.ShapeDtypeStruct(g.shape[1:], F32))(g)


def _silu(t):
    return t * _sigmoid(t)


def _ada_fwd(cc, w_ada):
    n = w_ada.shape[1]
    tn = _row_tile(n, LANES, 512)

    def body(cc_ref, w_ref, o_ref):
        o_ref[...] = _nn(_silu(cc_ref[...]), w_ref[...])

    return pl.pallas_call(
        body, name="ada_fwd", grid=(n // tn,), out_shape=jax.ShapeDtypeStruct((cc.shape[0], n), F32),
        in_specs=[pl.BlockSpec(cc.shape, lambda j: (0, 0)), pl.BlockSpec((w_ada.shape[0], tn), lambda j: (0, j))],
        out_specs=pl.BlockSpec((cc.shape[0], tn), lambda j: (0, j)), compiler_params=_cp("parallel"),
    )(cc, w_ada)


def _ada_bwd(cc, dm, w_ada):
    d, n = w_ada.shape
    tn = _row_tile(n, LANES, 512)

    def body(cc_ref, dm_ref, w_ref, gw_ref, ds_ref):
        @pl.when(pl.program_id(0) == 0)
        def _():
            ds_ref[...] = jnp.zeros_like(ds_ref)

        gw_ref[...] = _raw_dot("tn", _silu(cc_ref[...]), dm_ref[...], True)
        ds_ref[...] += _raw_dot("nt", dm_ref[...], w_ref[...], False)

    return pl.pallas_call(
        body, name="ada_bwd", grid=(n // tn,),
        out_shape=[jax.ShapeDtypeStruct((d, n), F32), jax.ShapeDtypeStruct(cc.shape, F32)],
        in_specs=[pl.BlockSpec(cc.shape, lambda j: (0, 0)), pl.BlockSpec((cc.shape[0], tn), lambda j: (0, j)),
                  pl.BlockSpec((d, tn), lambda j: (0, j))],
        out_specs=[pl.BlockSpec((d, tn), lambda j: (0, j)), pl.BlockSpec(cc.shape, lambda j: (0, 0))],
        compiler_params=_cp("arbitrary"),
    )(cc, dm, w_ada)


def _c_ctx_grad(parts, c_ctx):
    def body(p_ref, c_ref, o_ref):
        ds = ((p_ref[0] + p_ref[1]) + p_ref[2]) + p_ref[3]
        _, vjp = jax.vjp(_silu, c_ref[...])
        o_ref[...] = vjp(ds)[0]

    return pl.pallas_call(body, name="c_ctx_grad", out_shape=jax.ShapeDtypeStruct(c_ctx.shape, F32))(parts, c_ctx)


def kernel(x, c, ctx, c_ctx, w_ada, b_ada, g_pre_mix, g_post_mix, g_pre_ffn, g_post_ffn, w_in, attn_sink, w_gate_fwd, b_gate_fwd, w_gate_bwd, b_gate_bwd, g_gla_norm, w_out, w_ffn_in, w_ffn_out, loss_target, m_c_ctx, m_w_ada, m_b_ada, m_g_pre_mix, m_g_post_mix, m_g_pre_ffn, m_g_post_ffn, m_w_in, m_attn_sink, m_w_gate_fwd, m_b_gate_fwd, m_w_gate_bwd, m_b_gate_bwd, m_g_gla_norm, m_w_out, m_w_ffn_in, m_w_ffn_out, v_c_ctx, v_w_ada, v_b_ada, v_g_pre_mix, v_g_post_mix, v_g_pre_ffn, v_g_post_ffn, v_w_in, v_attn_sink, v_w_gate_fwd, v_b_gate_fwd, v_w_gate_bwd, v_b_gate_bwd, v_g_gla_norm, v_w_out, v_w_ffn_in, v_w_ffn_out):
    xi, yi, ci = lax.axis_index("x"), lax.axis_index("y"), lax.axis_index("c")
    dev, chip = 4 * xi + 2 * yi + ci, 2 * xi + yi
    c_idx = jnp.reshape(ci, (1,)).astype(jnp.int32)
    d = x.shape[-1]
    n_ada, n_in, n_f = w_ada.shape[-1], w_in.shape[-1], w_ffn_in.shape[-1]
    r_out, r_f = w_out.shape[1], w_ffn_out.shape[1]
    n_gate = w_gate_fwd.shape[-1]
    by_chip = lambda t: t[0::2]

    w_in_g = _ag_shards("gather_w_in", w_in[0].astype(BF16))
    late = _GatherBehind("gather_late", jnp.concatenate(
        [w_out[0], w_ffn_out[0], jnp.transpose(w_ffn_in[0])], axis=0).astype(BF16), chip, w_in_g)

    def late_weights(after):
        t = late.result(after)
        r1, r2 = r_out, r_out + r_f
        return (t[:, :r1].reshape(N_CHIP * r_out, d), t[:, r2:].reshape(N_CHIP * n_f, d),
                t[:, r1:r2].reshape(N_CHIP * r_f, d))

    rc = -(-d // LANES)
    g1 = _ag_small("gather_cond", _pack_rows([_behind(c[0], late.token), w_gate_fwd[0], w_gate_bwd[0]]))
    c_all = g1[:, :rc].reshape(N_DEV, -1)[:, :d]
    gr = GATE_RANK * n_gate // LANES
    gate_full = lambda off: jnp.transpose(by_chip(g1)[:, off:off + gr].reshape(N_CHIP, GATE_RANK, n_gate),
                                          (1, 0, 2)).reshape(GATE_RANK, N_CHIP * n_gate)
    wgf, wgb = gate_full(rc), gate_full(rc + gr)
    cc = jnp.concatenate([c_all, c_ctx[None, :], jnp.zeros((7, d), F32)], axis=0)

    g2 = _ag_small("gather_ada", _ada_fwd(cc, w_ada[0]).reshape(-1, LANES))
    ada_all = jnp.transpose(by_chip(g2).reshape(N_CHIP, 16, n_ada), (1, 0, 2)).reshape(16, N_CHIP * n_ada) + b_ada
    ada = lax.dynamic_slice(ada_all, (dev, 0), (1, N_CHIP * n_ada))
    ada_c = ada_all[N_DEV:N_DEV + 1]

    w = _prep_weights(jnp.concatenate([w_in_g[k] for k in range(N_CHIP)], axis=1), wgf, wgb)
    w.update(g_pre_mix=g_pre_mix, g_post_mix=g_post_mix, g_pre_ffn=g_pre_ffn, g_post_ffn=g_post_ffn,
             attn_sink=attn_sink, b_gate_fwd=b_gate_fwd, b_gate_bwd=b_gate_bwd, g_gla_norm=g_gla_norm)

    reduce_behind = _ReduceBehind("reduce_late", chip, ci, c_idx)
    loss_lanes, grad_x, g, d_ada, d_ada_c = _local_step(x[0], ctx[0], loss_target[0], ada, ada_c, w, late_weights,
                                                        reduce_behind)

    small = ("g_pre_mix", "g_post_mix", "g_pre_ffn", "g_post_ffn", "attn_sink", "b_gate_fwd", "b_gate_bwd",
             "g_gla_norm", "w_gate_fwd", "w_gate_bwd")
    shapes = [(1, 6 * d)] * 2 + [g[n].shape for n in small]
    g3 = _ag_small("gather_small_grads", _pack_rows([d_ada, d_ada_c] + [g[n] for n in small]))
    tot = dict(zip(("d_ada", "d_ada_c") + small, _unpack_rows(_sum_blocks("sum_small_grads", g3), shapes)))
    r_ada = 6 * d // LANES
    dm = jnp.concatenate([g3[:, :r_ada].reshape(N_DEV, 6 * d), tot["d_ada_c"], jnp.zeros((7, 6 * d), F32)], axis=0)
    grads = {n: tot[n] for n in small[:8]}
    grads["b_ada"] = _sum_blocks("sum_b_ada", dm.reshape(16, r_ada, LANES)).reshape(1, 6 * d)
    grads["w_gate_fwd"] = lax.dynamic_slice(tot["w_gate_fwd"], (0, chip * n_gate), (GATE_RANK, n_gate))[None]
    grads["w_gate_bwd"] = lax.dynamic_slice(tot["w_gate_bwd"], (0, chip * n_gate), (GATE_RANK, n_gate))[None]
    gw_ada, dsc = _ada_bwd(cc, lax.dynamic_slice(dm, (0, chip * n_ada), (16, n_ada)), w_ada[0])
    grads["w_ada"] = gw_ada[None]
    g4 = _ag_small("gather_c_ctx", _pack_rows([dsc[N_DEV]]))
    grads["c_ctx"] = _c_ctx_grad(by_chip(g4), _pack_rows([c_ctx])).reshape(-1)[:d]

    g_w_in = _unpack_w_in_grad(g["w_in"])
    grads["w_in"] = _reduce_shards("reduce_w_in", jnp.stack([g_w_in[:, k * n_in:(k + 1) * n_in]
                                                             for k in range(N_CHIP)]), c_idx)[None]
    behind = g["behind"]
    grads["w_out"], grads["w_ffn_out"] = behind[None, :r_out], behind[None, r_out:r_out + r_f]
    grads["w_ffn_in"] = jnp.transpose(behind[r_out + r_f:])[None]

    names = ("c_ctx", "w_ada", "b_ada", "g_pre_mix", "g_post_mix", "g_pre_ffn", "g_post_ffn", "w_in", "attn_sink",
             "w_gate_fwd", "b_gate_fwd", "w_gate_bwd", "b_gate_bwd", "g_gla_norm", "w_out", "w_ffn_in", "w_ffn_out")
    weights = dict(zip(names, (c_ctx, w_ada, b_ada, g_pre_mix, g_post_mix, g_pre_ffn, g_post_ffn, w_in, attn_sink,
                               w_gate_fwd, b_gate_fwd, w_gate_bwd, b_gate_bwd, g_gla_norm, w_out, w_ffn_in,
                               w_ffn_out)))
    m_in = dict(zip(names, (m_c_ctx, m_w_ada, m_b_ada, m_g_pre_mix, m_g_post_mix, m_g_pre_ffn, m_g_post_ffn, m_w_in,
                            m_attn_sink, m_w_gate_fwd, m_b_gate_fwd, m_w_gate_bwd, m_b_gate_bwd, m_g_gla_norm,
                            m_w_out, m_w_ffn_in, m_w_ffn_out)))
    v_in = dict(zip(names, (v_c_ctx, v_w_ada, v_b_ada, v_g_pre_mix, v_g_post_mix, v_g_pre_ffn, v_g_post_ffn, v_w_in,
                            v_attn_sink, v_w_gate_fwd, v_b_gate_fwd, v_w_gate_bwd, v_b_gate_bwd, v_g_gla_norm,
                            v_w_out, v_w_ffn_in, v_w_ffn_out)))
    large = ("w_ada", "w_in", "w_out", "w_ffn_in", "w_ffn_out")
    tiny = tuple(n for n in names if n not in large)
    delta, new_m, new_v = {}, {}, {}
    for n in large:
        dl, nm, nv = _adamw("adamw_" + n, weights[n][0], grads[n][0], m_in[n][0], v_in[n][0])
        delta[n], new_m[n], new_v[n] = dl[None], nm[None], nv[None]
    tiny_shapes = [weights[n].shape for n in tiny]
    packed = [_pack_rows([t[n] for n in tiny]) for t in (weights, grads, m_in, v_in)]
    for out, res in zip((delta, new_m, new_v), _adamw("adamw_small", *packed)):
        out.update(zip(tiny, _unpack_rows(res, tiny_shapes)))
    for n in tiny:
        grads[n] = grads[n].reshape(weights[n].shape)

    loss = lax.psum(loss_lanes[0, 0], ("x", "y", "c"))
    return (loss, grad_x[None], *[grads[n] for n in names], *[delta[n] for n in names], *[new_m[n] for n in names],
            *[new_v[n] for n in names])
```

```python
import functools

import jax
import jax.numpy as jnp
import numpy as np
from jax import lax
from jax.experimental import pallas as pl
from jax.experimental.pallas import tpu as pltpu

F32 = jnp.float32
BF16 = jnp.bfloat16
MESH = pl.DeviceIdType.MESH

HEAD_DIM = 64
ATT_HEADS = 8
ATT_KV_HEADS = 2
ATT_GROUP = ATT_HEADS // ATT_KV_HEADS
WINDOW = 128
BLOCK = 128
GRID_W = 64
ROPE_BASE = 10000.0
GLA_HEADS = 8
GLA_DK = 32
GLA_DV = 64
GLA_CHUNK = 64
GATE_RANK = 16
GATE_TAU = 16.0
NEG_INF = -1e30
QW = ATT_HEADS * HEAD_DIM
KVW = ATT_KV_HEADS * HEAD_DIM
GKW = GLA_HEADS * GLA_DK
GVW = GLA_HEADS * GLA_DV
IN_COLS = QW + 2 * KVW + 2 * GKW + 2 * GVW + 2 * GATE_RANK
LANES = 128
IN_PAD = IN_COLS + LANES - 2 * GATE_RANK
C_Q, C_GV, C_GG = 0, QW, QW + GVW
C_K = C_GG + GVW
C_V = C_K + KVW
C_GQ = C_V + KVW
C_GK = C_GQ + GKW
C_Z = C_GK + GKW
MIX = QW + GVW

ADAM_LR, ADAM_B1, ADAM_B2, ADAM_EPS, ADAM_WD, ADAM_STEP = 0.001, 0.9, 0.999, 1e-08, 0.01, 10

VMEM_LIMIT = 56 * 1024 * 1024


def _cp(*sem):
    return pltpu.CompilerParams(dimension_semantics=sem, vmem_limit_bytes=VMEM_LIMIT)


def _pick(n, cands):
    for t in cands:
        if n % t == 0:
            return t
    return n


_DIMS = {"nn": (((1,), (0,)), ((), ())), "nt": (((1,), (1,)), ((), ())), "tn": (((0,), (0,)), ((), ()))}


def _raw_dot(mode, a, b, hi):
    if hi:
        return lax.dot_general(a.astype(F32), b.astype(F32), _DIMS[mode], precision=lax.Precision.HIGHEST,
                               preferred_element_type=F32)
    return lax.dot_general(a.astype(BF16), b.astype(BF16), _DIMS[mode], preferred_element_type=F32)


def _make_dot(mode, hi):
    @jax.custom_vjp
    def dot(a, b):
        return _raw_dot(mode, a, b, hi)

    def fwd(a, b):
        return _raw_dot(mode, a, b, hi), (a, b)

    def bwd(res, dc):
        a, b = res
        if mode == "nn":
            return _raw_dot("nt", dc, b, hi), _raw_dot("tn", a, dc, hi)
        if mode == "nt":
            return _raw_dot("nn", dc, b, hi), _raw_dot("tn", dc, a, hi)
        return _raw_dot("nt", b, dc, hi), _raw_dot("nn", a, dc, hi)

    dot.defvjp(fwd, bwd)
    return dot


_nn, _nt, _tn = _make_dot("nn", False), _make_dot("nt", False), _make_dot("tn", False)
_nn_hi = _make_dot("nn", True)


MM_VMEM_BUDGET = 44 * 1024 * 1024


def _halvings(n):
    out = [n]
    while out[-1] % (2 * LANES) == 0:
        out.append(out[-1] // 2)
    return out


def _mm_tiles(mode, m, n, k, a_bytes, b_bytes, o_bytes, init_bytes=0):
    tms = [t for t in dict.fromkeys((m, m // 2, 2048, 1024, 512, 256, 128))
           if m % t == 0 and t % (LANES if mode == "tn" else 16) == 0 and t <= 4096] or [m]
    tks = ([t for t in (512, 256, 128) if k % t == 0] or [k]) if mode == "tn" else _halvings(k)
    for tn in _halvings(n):
        for tk in tks:
            for tm in tms:
                acc = tm * tn * 4 if (k // tk > 1 and o_bytes != 4) else 0
                tiles = tm * tk * a_bytes + tk * tn * b_bytes + tm * tn * (o_bytes + init_bytes)
                if 2 * tiles + acc <= MM_VMEM_BUDGET:
                    return tm, tn, tk
    return tms[-1], _halvings(n)[-1], tks[-1]


def _mm(name, a, b, mode, out_dtype=F32, init=None):
    if mode == "nn":
        (m, k), n = a.shape, b.shape[1]
    elif mode == "nt":
        (m, k), n = a.shape, b.shape[0]
    else:
        (k, m), n = a.shape, b.shape[1]
    tm, tn, tk = _mm_tiles(mode, m, n, k, a.dtype.itemsize, b.dtype.itemsize, jnp.dtype(out_dtype).itemsize,
                           0 if init is None else 4)
    nk = k // tk
    use_acc = nk > 1 and out_dtype != F32

    inits = () if init is None else (init,)

    def body(a_ref, b_ref, *rest):
        o_ref, acc = rest[len(inits)], rest[len(inits) + 1:]
        part = _raw_dot(mode, a_ref[...], b_ref[...], False)
        first = lambda: part + rest[0][...] if inits else part
        if nk == 1:
            o_ref[...] = first().astype(o_ref.dtype)
            return
        acc_ref = acc[0] if use_acc else o_ref
        kk = pl.program_id(2)

        @pl.when(kk == 0)
        def _():
            acc_ref[...] = first()

        @pl.when(kk > 0)
        def _():
            acc_ref[...] += part

        if use_acc:
            @pl.when(kk == nk - 1)
            def _():
                o_ref[...] = acc_ref[...].astype(o_ref.dtype)

    if mode == "nn":
        a_spec = pl.BlockSpec((tm, tk), lambda i, j, kk: (i, kk))
        b_spec = pl.BlockSpec((tk, tn), lambda i, j, kk: (kk, j))
    elif mode == "nt":
        a_spec = pl.BlockSpec((tm, tk), lambda i, j, kk: (i, kk))
        b_spec = pl.BlockSpec((tn, tk), lambda i, j, kk: (j, kk))
    else:
        a_spec = pl.BlockSpec((tk, tm), lambda i, j, kk: (kk, i))
        b_spec = pl.BlockSpec((tk, tn), lambda i, j, kk: (kk, j))
    return pl.pallas_call(
        body, name=name, grid=(m // tm, n // tn, nk),
        in_specs=[a_spec, b_spec] + [pl.BlockSpec((tm, tn), lambda i, j, kk: (i, j))] * len(inits),
        out_specs=pl.BlockSpec((tm, tn), lambda i, j, kk: (i, j)),
        out_shape=jax.ShapeDtypeStruct((m, n), out_dtype),
        scratch_shapes=[pltpu.VMEM((tm, tn), F32)] if use_acc else [],
        compiler_params=_cp("parallel", "parallel", "arbitrary"),
    )(a, b, *inits)


def _rowwise(name, fn, rows, row_ins, full_ins, row_outs, acc_outs, tm=None):
    tm = tm or _pick(rows, (512, 256, 128))
    n_r, n_f, n_o, n_a = len(row_ins), len(full_ins), len(row_outs), len(acc_outs)

    def body(*refs):
        ins, outs = refs[:n_r + n_f], refs[n_r + n_f:]
        vals = [r[...].astype(F32) for r in ins]
        ro, ao = fn(*vals)
        for r, val in zip(outs[:n_o], ro):
            r[...] = val.astype(r.dtype)
        if n_a:
            @pl.when(pl.program_id(0) == 0)
            def _():
                for r in outs[n_o:]:
                    r[...] = jnp.zeros_like(r)

            for r, val in zip(outs[n_o:], ao):
                r[...] += val

    in_specs = [pl.BlockSpec((tm, w), functools.partial(lambda i, cb: (i, cb), cb=cb)) for _, w, cb in row_ins]
    in_specs += [pl.BlockSpec(a.shape, lambda i: (0, 0)) for a in full_ins]
    out_specs = [pl.BlockSpec((tm, w), lambda i: (i, 0)) for w, _ in row_outs]
    out_specs += [pl.BlockSpec(s, lambda i: (0, 0)) for s in acc_outs]
    out_shape = [jax.ShapeDtypeStruct((rows, w), dt) for w, dt in row_outs]
    out_shape += [jax.ShapeDtypeStruct(s, F32) for s in acc_outs]
    return pl.pallas_call(
        body, name=name, grid=(rows // tm,), in_specs=in_specs, out_specs=out_specs, out_shape=out_shape,
        compiler_params=_cp("arbitrary" if n_a else "parallel"),
    )(*[a for a, _, _ in row_ins], *full_ins)


def _rn(x):
    return x * lax.rsqrt(jnp.mean(x * x, axis=-1, keepdims=True) + 1e-6)


def _sigmoid(t):
    return 1.0 / (1.0 + jnp.exp(-t))


def _f_norm_mod(x, g, sh, sc):
    return _rn(x) * g * (1.0 + sc) + sh


def _f_post_res(xr, y, g, gate):
    return xr + gate * (_rn(y) * g)


def _f_swiglu(g, u):
    return g * _sigmoid(g) * u


def _logsig(u):
    return jnp.minimum(u, 0.0) - jnp.log(1.0 + jnp.exp(-jnp.abs(u)))


def _f_gate(z, wf, wb, bf, bb):
    return _logsig(_nn(z, wf) + bf) / GATE_TAU, _logsig(_nn(z, wb) + bb) / GATE_TAU


def _f_gla_out(of, ob, gg, gt, bd):
    o = of + ob
    ms = _nn_hi(o * o, bd)
    return o * lax.rsqrt(ms + 1e-6) * gt * (gg * _sigmoid(gg))


def _norm_mod(name, x, g, sh, sc):
    rows, d = x.shape
    return _rowwise(name, lambda x, g, sh, sc: ((_f_norm_mod(x, g, sh, sc),), ()), rows,
                    [(x, d, 0)], [g, sh, sc], [(d, BF16)], [])[0]


def _norm_mod_bwd(name, dh, dres, x, g, sh, sc):
    rows, d = x.shape

    def fn(dh, dres, x, g, sh, sc):
        _, vjp = jax.vjp(_f_norm_mod, x, g, sh, sc)
        dx, dg, dsh, dsc = vjp(dh)
        return (dx + dres,), (dg, dsh, dsc)

    return _rowwise(name, fn, rows, [(dh, d, 0), (dres, d, 0), (x, d, 0)], [g, sh, sc], [(d, F32)],
                    [(1, d)] * 3)


def _post_res(name, xr, y, g, gate):
    rows, d = xr.shape
    return _rowwise(name, lambda xr, y, g, gate: ((_f_post_res(xr, y, g, gate),), ()), rows,
                    [(xr, d, 0), (y, d, 0)], [g, gate], [(d, F32)], [])[0]


def _post_res_bwd(name, dxo, y, g, gate):
    rows, d = y.shape

    def fn(dxo, y, g, gate):
        _, vjp = jax.vjp(lambda y, g, gate: _f_post_res(jnp.zeros_like(y), y, g, gate), y, g, gate)
        dy, dg, dgate = vjp(dxo)
        return (dy,), (dg, dgate)

    return _rowwise(name, fn, rows, [(dxo, d, 0), (y, d, 0)], [g, gate], [(d, BF16)], [(1, d)] * 2)


def _post_res_loss(name, xr, y, g, gate, target):
    rows, d = xr.shape

    def fn(xr, y, target, g, gate):
        diff = _f_post_res(xr, y, g, gate) - target
        part = 0.5 * jnp.sum(jnp.mean(diff * diff, axis=-1, keepdims=True), axis=0, keepdims=True)
        return (diff * (1.0 / d),), (jnp.broadcast_to(part, (1, LANES)),)

    return _rowwise(name, fn, rows, [(xr, d, 0), (y, d, 0), (target, d, 0)], [g, gate], [(d, F32)], [(1, LANES)])


def _swiglu(name, u):
    rows, f2 = u.shape
    f = f2 // 2
    return _rowwise(name, lambda g, u: ((_f_swiglu(g, u),), ()), rows, [(u, f, 0), (u, f, 1)], [], [(f, BF16)], [],
                    tm=_pick(rows, (512, 256, 128)))[0]


def _swiglu_bwd(name, da, u):
    rows, f2 = u.shape
    f = f2 // 2

    def fn(da, g, u):
        _, vjp = jax.vjp(_f_swiglu, g, u)
        return (jnp.concatenate(vjp(da), axis=1),), ()

    return _rowwise(name, fn, rows, [(da, f, 0), (u, f, 0), (u, f, 1)], [], [(f2, BF16)], [],
                    tm=_pick(rows, (512, 256, 128)))[0]


def _gate_fwd(name, p, wf, wb, bf, bb):
    rows = p.shape[0]
    return _rowwise(name, lambda z, wf, wb, bf, bb: (_f_gate(z, wf, wb, bf, bb), ()), rows,
                    [(p, LANES, C_Z // LANES)], [wf, wb, bf, bb], [(GKW, F32)] * 2, [])


def _gate_bwd(name, p, dla_f, dla_b, wf, wb, bf, bb):
    rows = p.shape[0]

    def fn(z, dlf, dlb, wf, wb, bf, bb):
        _, vjp = jax.vjp(_f_gate, z, wf, wb, bf, bb)
        dz, dwf, dwb, dbf, dbb = vjp((dlf, dlb))
        return (dz,), (dwf, dwb, dbf, dbb)

    return _rowwise(name, fn, rows, [(p, LANES, C_Z // LANES), (dla_f, GKW, 0), (dla_b, GKW, 0)],
                    [wf, wb, bf, bb], [(LANES, BF16)], [(LANES, GKW), (LANES, GKW), (1, GKW), (1, GKW)])


def _head_mean_matrix():
    h = np.arange(GVW) // GLA_DV
    return jnp.asarray((h[:, None] == h[None, :]).astype(np.float32) / GLA_DV)


def _gla_out(name, attn, of, ob, p, gt):
    rows = of.shape[0]
    bd = _head_mean_matrix()
    fn = lambda attn, of, ob, gg, gt, bd: ((jnp.concatenate([attn, _f_gla_out(of, ob, gg, gt, bd)], axis=1),), ())
    return _rowwise(name, fn, rows, [(attn, QW, 0), (of, GVW, 0), (ob, GVW, 0), (p, GVW, C_GG // GVW)], [gt, bd],
                    [(MIX, BF16)], [])[0]


def _gla_out_bwd(name, dmix, of, ob, p, gt):
    rows = of.shape[0]
    bd = _head_mean_matrix()

    def fn(dm, of, ob, gg, gt, bd):
        _, vjp = jax.vjp(lambda of, gg, gt: _f_gla_out(of, ob, gg, gt, bd), of, gg, gt)
        do, dgg, dgt = vjp(dm)
        return (do, dgg), (dgt,)

    return _rowwise(name, fn, rows, [(dmix, GVW, 1), (of, GVW, 0), (ob, GVW, 0), (p, GVW, C_GG // GVW)], [gt, bd],
                    [(GVW, F32), (GVW, BF16)], [(1, GVW)])


def _rope_tables(n_tokens):
    t = jnp.arange(n_tokens)
    row = (t // GRID_W).astype(F32)
    col = (t % GRID_W).astype(F32)
    half = HEAD_DIM // 2
    inv_freq = ROPE_BASE ** (-jnp.arange(0, half, 2, dtype=F32) / half)
    ang_r = row[:, None] * inv_freq[None, :]
    ang_c = col[:, None] * inv_freq[None, :]
    ang = jnp.concatenate([ang_r, ang_r, ang_c, ang_c], axis=-1)
    sign = jnp.concatenate([-jnp.ones((16,), F32), jnp.ones((16,), F32)] * 2)
    cos, sin = jnp.cos(ang), jnp.sin(ang) * sign[None, :]
    return jnp.tile(cos, (1, 2)), jnp.tile(sin, (1, 2))


def _rot_pairs(x):
    w = x.shape[-1]
    lane = lax.broadcasted_iota(jnp.int32, x.shape, x.ndim - 1)
    return jnp.where((lane % 32) < 16, pltpu.roll(x, w - 16, x.ndim - 1), pltpu.roll(x, 16, x.ndim - 1))


def _rope_apply(x, cos, sin_signed, inverse):
    reps = x.shape[-1] // LANES
    cos = jnp.concatenate([cos] * reps, axis=-1) if reps > 1 else cos
    sin = jnp.concatenate([sin_signed] * reps, axis=-1) if reps > 1 else sin_signed
    if inverse:
        return x * cos + _rot_pairs(x * sin)
    return x * cos + _rot_pairs(x) * sin


def _rope_fwd(name, p, cos, sin):
    rows = p.shape[0]

    def fn(q, k, v, cos, sin):
        return (_rope_apply(q, cos, sin, False), _rope_apply(k, cos, sin, False), v), ()

    return _rowwise(name, fn, rows, [(p, QW, 0), (p, KVW, C_K // KVW), (p, KVW, C_V // KVW), (cos, LANES, 0),
                                     (sin, LANES, 0)], [], [(QW, BF16), (KVW, BF16), (KVW, BF16)], [])


def _rope_bwd(name, dq, dk, cos, sin):
    rows = dq.shape[0]

    def fn(dq, dk, cos, sin):
        return (_rope_apply(dq, cos, sin, True), _rope_apply(dk, cos, sin, True)), ()

    return _rowwise(name, fn, rows, [(dq, QW, 0), (dk, KVW, 0), (cos, LANES, 0), (sin, LANES, 0)], [],
                    [(QW, BF16), (KVW, BF16)], [])


GROUP_ROWS = ATT_GROUP * BLOCK


def _f_attn(qs, kws, vws, kcs, vcs, sink, n, n_tokens):
    row = lax.broadcasted_iota(jnp.int32, (GROUP_ROWS, 1), 0)
    group = sum((row >= g * BLOCK).astype(jnp.int32) for g in range(1, ATT_GROUP))
    i = lax.broadcasted_iota(jnp.int32, (GROUP_ROWS, 3 * BLOCK), 0) - BLOCK * group
    j = lax.broadcasted_iota(jnp.int32, (GROUP_ROWS, 3 * BLOCK), 1)
    kpos = (n - 1) * BLOCK + j
    mask = (jnp.abs(j - BLOCK - i) <= WINDOW) & (kpos >= 0) & (kpos < n_tokens)
    head_id = lax.broadcasted_iota(jnp.int32, (1, ATT_HEADS), 1)
    scale = HEAD_DIM ** -0.5
    outs = []
    for h in range(ATT_KV_HEADS):
        sk = jnp.zeros((GROUP_ROWS, 1), F32)
        for g in range(ATT_GROUP):
            one = jnp.sum(jnp.where(head_id == h * ATT_GROUP + g, sink, 0.0), axis=-1, keepdims=True)
            sk = jnp.where(group == g, one, sk)
        q = qs[h] * scale
        s_w = jnp.where(mask, _nt(q, kws[h]), NEG_INF)
        s_c = _nt(q, kcs[h])
        m = lax.stop_gradient(jnp.maximum(jnp.maximum(jnp.max(s_w, axis=-1, keepdims=True),
                                                      jnp.max(s_c, axis=-1, keepdims=True)), sk))
        pw, pc = jnp.exp(s_w - m), jnp.exp(s_c - m)
        den = jnp.sum(pw, axis=-1, keepdims=True) + jnp.sum(pc, axis=-1, keepdims=True) + jnp.exp(sk - m)
        outs.append((_nn(pw, vws[h]) + _nn(pc, vcs[h])) / den)
    return tuple(outs)


def _group_rows(ref, h):
    hs = lambda hq: slice(hq * HEAD_DIM, (hq + 1) * HEAD_DIM)
    return jnp.concatenate([ref[:, hs(h * ATT_GROUP + g)].astype(F32) for g in range(ATT_GROUP)], axis=0)


def _ungroup_rows(ref, h, val):
    for g in range(ATT_GROUP):
        hq = h * ATT_GROUP + g
        ref[:, hq * HEAD_DIM:(hq + 1) * HEAD_DIM] = val[g * BLOCK:(g + 1) * BLOCK].astype(ref.dtype)


def _attn_loads(n, q_ref, kp_ref, vp_ref, kc_ref, vc_ref):
    r0 = pl.multiple_of(n * BLOCK, BLOCK)
    hs = lambda h: slice(h * HEAD_DIM, (h + 1) * HEAD_DIM)
    qs = [_group_rows(q_ref, h) for h in range(ATT_KV_HEADS)]
    kws = [kp_ref[pl.ds(r0, 3 * BLOCK), hs(h)].astype(F32) for h in range(ATT_KV_HEADS)]
    vws = [vp_ref[pl.ds(r0, 3 * BLOCK), hs(h)].astype(F32) for h in range(ATT_KV_HEADS)]
    kcs = [kc_ref[:, hs(h)].astype(F32) for h in range(ATT_KV_HEADS)]
    vcs = [vc_ref[:, hs(h)].astype(F32) for h in range(ATT_KV_HEADS)]
    return r0, hs, qs, kws, vws, kcs, vcs


def _attn_specs(s, c):
    full = lambda shape: pl.BlockSpec(shape, lambda n: (0, 0))
    return [pl.BlockSpec((BLOCK, QW), lambda n: (n, 0)), full((s + 2 * BLOCK, KVW)), full((s + 2 * BLOCK, KVW)),
            full((c, KVW)), full((c, KVW)), full((1, ATT_HEADS))]


def _attn_fwd(q, kp, vp, kc, vc, sink):
    s, c = q.shape[0], kc.shape[0]

    def body(q_ref, kp_ref, vp_ref, kc_ref, vc_ref, sink_ref, o_ref):
        n = pl.program_id(0)
        _, hs, qs, kws, vws, kcs, vcs = _attn_loads(n, q_ref, kp_ref, vp_ref, kc_ref, vc_ref)
        outs = _f_attn(qs, kws, vws, kcs, vcs, sink_ref[...], n, s)
        for h in range(ATT_KV_HEADS):
            _ungroup_rows(o_ref, h, outs[h])

    return pl.pallas_call(
        body, name="attn_fwd", grid=(s // BLOCK,), in_specs=_attn_specs(s, c),
        out_specs=pl.BlockSpec((BLOCK, QW), lambda n: (n, 0)), out_shape=jax.ShapeDtypeStruct((s, QW), BF16),
        compiler_params=_cp("parallel"),
    )(q, kp, vp, kc, vc, sink)


def _attn_bwd(do, q, kp, vp, kc, vc, sink):
    s, c = q.shape[0], kc.shape[0]

    def body(do_ref, q_ref, kp_ref, vp_ref, kc_ref, vc_ref, sink_ref, dq_ref, dkp_ref, dvp_ref, dkc_ref, dvc_ref,
             dsink_ref):
        n = pl.program_id(0)

        @pl.when(n == 0)
        def _():
            for r in (dkp_ref, dvp_ref, dkc_ref, dvc_ref, dsink_ref):
                r[...] = jnp.zeros_like(r)

        r0, hs, qs, kws, vws, kcs, vcs = _attn_loads(n, q_ref, kp_ref, vp_ref, kc_ref, vc_ref)
        _, vjp = jax.vjp(lambda qs, kws, vws, kcs, vcs, sink: _f_attn(qs, kws, vws, kcs, vcs, sink, n, s),
                         qs, kws, vws, kcs, vcs, sink_ref[...])
        dqs, dkws, dvws, dkcs, dvcs, dsink = vjp(tuple(_group_rows(do_ref, h) for h in range(ATT_KV_HEADS)))
        for h in range(ATT_KV_HEADS):
            _ungroup_rows(dq_ref, h, dqs[h])
            dkp_ref[pl.ds(r0, 3 * BLOCK), hs(h)] += dkws[h]
            dvp_ref[pl.ds(r0, 3 * BLOCK), hs(h)] += dvws[h]
            dkc_ref[:, hs(h)] += dkcs[h]
            dvc_ref[:, hs(h)] += dvcs[h]
        dsink_ref[...] += dsink

    full = lambda shape: pl.BlockSpec(shape, lambda n: (0, 0))
    return pl.pallas_call(
        body, name="attn_bwd", grid=(s // BLOCK,),
        in_specs=[pl.BlockSpec((BLOCK, QW), lambda n: (n, 0))] + _attn_specs(s, c),
        out_specs=[pl.BlockSpec((BLOCK, QW), lambda n: (n, 0)), full((s + 2 * BLOCK, KVW)), full((s + 2 * BLOCK, KVW)),
                   full((c, KVW)), full((c, KVW)), full((1, ATT_HEADS))],
        out_shape=[jax.ShapeDtypeStruct((s, QW), F32), jax.ShapeDtypeStruct((s + 2 * BLOCK, KVW), F32),
                   jax.ShapeDtypeStruct((s + 2 * BLOCK, KVW), F32), jax.ShapeDtypeStruct((c, KVW), F32),
                   jax.ShapeDtypeStruct((c, KVW), F32), jax.ShapeDtypeStruct((1, ATT_HEADS), F32)],
        compiler_params=_cp("arbitrary"),
    )(do, q, kp, vp, kc, vc, sink)


def _gla_masks():
    hk = np.arange(GKW) // GLA_DK
    hv = np.arange(GVW) // GLA_DV
    head_k = (np.arange(GLA_HEADS)[:, None] == hk[None, :]).astype(np.float32)
    head_v = (np.arange(GLA_HEADS)[:, None] == hv[None, :]).astype(np.float32)
    bd_t = (hv[:, None] == hk[None, :]).astype(np.float32)
    return jnp.asarray(head_k), jnp.asarray(head_v), jnp.asarray(bd_t)


def _tri(n, rev, strict=False):
    i = lax.broadcasted_iota(jnp.int32, (n, n), 0)
    j = lax.broadcasted_iota(jnp.int32, (n, n), 1)
    if strict:
        keep = (j > i) if rev else (j < i)
    else:
        keep = (j >= i) if rev else (j <= i)
    return keep


def _f_gla_chunk(q, k, v, la, st, head_k, head_v, bd_t, rev):
    keep = _tri(GLA_CHUNK, rev)
    b = _nn_hi(keep.astype(F32), la)
    bl = jnp.sum(la, axis=0, keepdims=True)
    qd = q * (GLA_DK ** -0.5) * jnp.exp(b)
    ki = k * jnp.exp(-b)
    kd = k * jnp.exp(bl - b)
    q_heads = (qd[None, :, :] * head_k[:, None, :]).reshape(GLA_HEADS * GLA_CHUNK, GKW)
    a_all = _nt(q_heads, ki).reshape(GLA_HEADS, GLA_CHUNK, GLA_CHUNK)
    a_all = jnp.where(keep[None, :, :], a_all, 0.0).reshape(GLA_HEADS * GLA_CHUNK, GLA_CHUNK)
    o_all = _nn(a_all, v).reshape(GLA_HEADS, GLA_CHUNK, GVW)
    intra = jnp.sum(o_all * head_v[:, None, :], axis=0)
    inter = _nt(qd, st)
    st_new = st * jnp.exp(bl) + bd_t * _tn(v, kd)
    return intra + inter, st_new


def _gla_specs(s, tb, order):
    return [pl.BlockSpec((tb, GKW), lambda i: (order(i), C_GQ // GKW)),
            pl.BlockSpec((tb, GKW), lambda i: (order(i), C_GK // GKW)),
            pl.BlockSpec((tb, GVW), lambda i: (order(i), C_GV // GVW)),
            pl.BlockSpec((tb, GKW), lambda i: (order(i), 0))]


GLA_BLOCK_CHUNKS = 2


def _gla_fwd(name, p, la, st0, rev):
    s = p.shape[0]
    tb = GLA_BLOCK_CHUNKS * GLA_CHUNK
    nblk = s // tb
    order = (lambda i: nblk - 1 - i) if rev else (lambda i: i)
    masks = _gla_masks()

    def body(q_ref, k_ref, v_ref, la_ref, st0_ref, hk_ref, hv_ref, bd_ref, o_ref, sts_ref, st_ref):
        @pl.when(pl.program_id(0) == 0)
        def _():
            st_ref[...] = st0_ref[...]

        st = st_ref[...]
        sts_ref[0] = st
        chunks = range(GLA_BLOCK_CHUNKS)
        for ci in (reversed(chunks) if rev else chunks):
            rows = slice(ci * GLA_CHUNK, (ci + 1) * GLA_CHUNK)
            o, st = _f_gla_chunk(q_ref[rows, :], k_ref[rows, :], v_ref[rows, :], la_ref[rows, :], st,
                                 hk_ref[...], hv_ref[...], bd_ref[...], rev)
            o_ref[rows, :] = o
        st_ref[...] = st

    full = lambda a: pl.BlockSpec(a.shape, lambda i: (0,) * a.ndim)
    return pl.pallas_call(
        body, name=name, grid=(nblk,),
        in_specs=_gla_specs(s, tb, order) + [full(st0)] + [full(m) for m in masks],
        out_specs=[pl.BlockSpec((tb, GVW), lambda i: (order(i), 0)),
                   pl.BlockSpec((1, GVW, GKW), lambda i: (order(i), 0, 0))],
        out_shape=[jax.ShapeDtypeStruct((s, GVW), F32), jax.ShapeDtypeStruct((nblk, GVW, GKW), F32)],
        scratch_shapes=[pltpu.VMEM((GVW, GKW), F32)],
        compiler_params=_cp("arbitrary"),
    )(p, p, p, la, st0, *masks)


def _gla_bwd(name, p, la, sts, do, prev, rev, after=None):
    s = p.shape[0]
    tb = GLA_BLOCK_CHUNKS * GLA_CHUNK
    nblk = s // tb
    order = (lambda i: i) if rev else (lambda i: nblk - 1 - i)
    masks = _gla_masks()
    n_prev = 0 if prev is None else 3
    follow = () if after is None else (after,)

    def body(*refs):
        q_ref, k_ref, v_ref, la_ref, sts_ref, do_ref, hk_ref, hv_ref, bd_ref = refs[:9]
        prev_refs = refs[9:9 + n_prev]
        dq_ref, dk_ref, dv_ref, dla_ref, dst0_ref, dst_ref = refs[9 + n_prev + len(follow):]

        @pl.when(pl.program_id(0) == 0)
        def _():
            dst_ref[...] = jnp.zeros_like(dst_ref)

        def block(q, k, v, la, st):
            outs = [None] * GLA_BLOCK_CHUNKS
            chunks = range(GLA_BLOCK_CHUNKS)
            for ci in (reversed(chunks) if rev else chunks):
                rows = slice(ci * GLA_CHUNK, (ci + 1) * GLA_CHUNK)
                outs[ci], st = _f_gla_chunk(q[ci], k[ci], v[ci], la[ci], st, hk_ref[...], hv_ref[...], bd_ref[...],
                                            rev)
            return tuple(outs), st

        split = lambda r: tuple(r[ci * GLA_CHUNK:(ci + 1) * GLA_CHUNK, :].astype(F32)
                                for ci in range(GLA_BLOCK_CHUNKS))
        _, vjp = jax.vjp(block, split(q_ref), split(k_ref), split(v_ref), split(la_ref), sts_ref[0])
        dq, dk, dv, dla, dst = vjp((split(do_ref), dst_ref[...]))
        for ci in range(GLA_BLOCK_CHUNKS):
            rows = slice(ci * GLA_CHUNK, (ci + 1) * GLA_CHUNK)
            if n_prev:
                dq_ref[rows, :] = dq[ci] + prev_refs[0][rows, :]
                dk_ref[rows, :] = dk[ci] + prev_refs[1][rows, :]
                dv_ref[rows, :] = dv[ci] + prev_refs[2][rows, :]
            else:
                dq_ref[rows, :], dk_ref[rows, :], dv_ref[rows, :] = dq[ci], dk[ci], dv[ci]
            dla_ref[rows, :] = dla[ci]
        dst_ref[...] = dst
        dst0_ref[...] = dst

    full = lambda a: pl.BlockSpec(a.shape, lambda i: (0,) * a.ndim)
    blk = lambda w: pl.BlockSpec((tb, w), lambda i: (order(i), 0))
    prev_specs = [blk(GKW), blk(GKW), blk(GVW)] if n_prev else []
    return pl.pallas_call(
        body, name=name, grid=(nblk,),
        in_specs=_gla_specs(s, tb, order) + [pl.BlockSpec((1, GVW, GKW), lambda i: (order(i), 0, 0)), blk(GVW)]
        + [full(m) for m in masks] + prev_specs + [pl.BlockSpec(memory_space=pl.ANY)] * len(follow),
        out_specs=[blk(GKW), blk(GKW), blk(GVW), blk(GKW), pl.BlockSpec((GVW, GKW), lambda i: (0, 0))],
        out_shape=[jax.ShapeDtypeStruct((s, GKW), F32), jax.ShapeDtypeStruct((s, GKW), F32),
                   jax.ShapeDtypeStruct((s, GVW), F32), jax.ShapeDtypeStruct((s, GKW), F32),
                   jax.ShapeDtypeStruct((GVW, GKW), F32)],
        scratch_shapes=[pltpu.VMEM((GVW, GKW), F32)],
        compiler_params=_cp("arbitrary"),
    )(p, p, p, la, sts, do, *masks, *(prev or ()), *follow)


def _f_ctx_state(k, v, la_f, la_b, bd_t):
    c = k.shape[0]
    after = _nn_hi(_tri(c, True, strict=True).astype(F32), la_f)
    before = _nn_hi(_tri(c, False, strict=True).astype(F32), la_b)
    return bd_t * _tn(v, k * jnp.exp(after)), bd_t * _tn(v, k * jnp.exp(before))


def _ctx_state(pc, la_f, la_b):
    c = pc.shape[0]
    bd_t = _gla_masks()[2]

    def body(k_ref, v_ref, lf_ref, lb_ref, bd_ref, sf_ref, sb_ref):
        sf_ref[...], sb_ref[...] = _f_ctx_state(k_ref[...], v_ref[...], lf_ref[...], lb_ref[...], bd_ref[...])

    full = lambda a: pl.BlockSpec(a.shape, lambda i: (0, 0))
    return pl.pallas_call(
        body, name="ctx_state_fwd", grid=(1,),
        in_specs=[pl.BlockSpec((c, GKW), lambda i: (0, C_GK // GKW)), pl.BlockSpec((c, GVW), lambda i: (0, C_GV // GVW)),
                  full(la_f), full(la_b), full(bd_t)],
        out_specs=[pl.BlockSpec((GVW, GKW), lambda i: (0, 0))] * 2,
        out_shape=[jax.ShapeDtypeStruct((GVW, GKW), F32)] * 2,
        compiler_params=_cp("arbitrary"),
    )(pc, pc, la_f, la_b, bd_t)


def _ctx_state_bwd(pc, la_f, la_b, dsf, dsb):
    c = pc.shape[0]
    bd_t = _gla_masks()[2]

    def body(k_ref, v_ref, lf_ref, lb_ref, bd_ref, dsf_ref, dsb_ref, dk_ref, dv_ref, dlf_ref, dlb_ref):
        _, vjp = jax.vjp(lambda k, v, lf, lb: _f_ctx_state(k, v, lf, lb, bd_ref[...]),
                         k_ref[...], v_ref[...], lf_ref[...], lb_ref[...])
        dk, dv, dlf, dlb = vjp((dsf_ref[...], dsb_ref[...]))
        dk_ref[...], dv_ref[...] = dk.astype(BF16), dv.astype(BF16)
        dlf_ref[...], dlb_ref[...] = dlf, dlb

    full = lambda a: pl.BlockSpec(a.shape, lambda i: (0, 0))
    return pl.pallas_call(
        body, name="ctx_state_bwd", grid=(1,),
        in_specs=[pl.BlockSpec((c, GKW), lambda i: (0, C_GK // GKW)), pl.BlockSpec((c, GVW), lambda i: (0, C_GV // GVW)),
                  full(la_f), full(la_b), full(bd_t), full(dsf), full(dsb)],
        out_specs=[pl.BlockSpec((c, GKW), lambda i: (0, 0)), pl.BlockSpec((c, GVW), lambda i: (0, 0)),
                   pl.BlockSpec((c, GKW), lambda i: (0, 0)), pl.BlockSpec((c, GKW), lambda i: (0, 0))],
        out_shape=[jax.ShapeDtypeStruct((c, GKW), BF16), jax.ShapeDtypeStruct((c, GVW), BF16),
                   jax.ShapeDtypeStruct((c, GKW), F32), jax.ShapeDtypeStruct((c, GKW), F32)],
        compiler_params=_cp("arbitrary"),
    )(pc, pc, la_f, la_b, bd_t, dsf, dsb)


_SRC_COLS = ((0, QW), (QW + 2 * KVW + 2 * GKW, GVW), (QW + 2 * KVW + 2 * GKW + GVW, GVW), (QW, KVW), (QW + KVW, KVW),
             (QW + 2 * KVW, GKW), (QW + 2 * KVW + GKW, GKW), (IN_COLS - 2 * GATE_RANK, 2 * GATE_RANK))
_DST_COLS = (C_Q, C_GV, C_GG, C_K, C_V, C_GQ, C_GK, C_Z)


def _pack_w_in(w_in):
    parts = [w_in[:, s:s + n] for s, n in _SRC_COLS]
    parts.append(jnp.zeros((w_in.shape[0], IN_PAD - C_Z - 2 * GATE_RANK), w_in.dtype))
    return jnp.concatenate(parts, axis=1)


def _unpack_w_in_grad(g):
    by_src = sorted(zip(_SRC_COLS, _DST_COLS))
    return jnp.concatenate([g[:, d:d + n] for (_, n), d in by_src], axis=1)


def _prep_weights(w_in, w_gate_fwd, w_gate_bwd):
    pad_rows = lambda w, at: jnp.zeros((LANES, GKW), F32).at[at:at + GATE_RANK].set(w)
    return {"w_in": _pack_w_in(w_in).astype(BF16), "wg_f": pad_rows(w_gate_fwd, 0),
            "wg_b": pad_rows(w_gate_bwd, GATE_RANK)}


def _local_step(x, ctx, target, ada, ada_c, w, late_weights, reduce_behind=None):
    s, d = x.shape
    sh1, sc1, gt1, sh2, sc2, gt2 = [ada[:, i * d:(i + 1) * d] for i in range(6)]
    sh1c, sc1c = ada_c[:, :d], ada_c[:, d:2 * d]
    cos, sin = _rope_tables(s)
    gt = jnp.tile(w["g_gla_norm"], (1, GLA_HEADS))

    h = _norm_mod("pre_mix", x, w["g_pre_mix"], sh1, sc1)
    hc = _norm_mod("pre_mix_ctx", ctx, w["g_pre_mix"], sh1c, sc1c)
    p = _mm("proj_in", h, w["w_in"], "nn")
    pc = _mm("proj_in_ctx", hc, w["w_in"], "nn")
    q_rot, k_rot, v_b = _rope_fwd("rope", p, cos, sin)
    pad = ((BLOCK, BLOCK), (0, 0))
    kp, vp = jnp.pad(k_rot, pad), jnp.pad(v_b, pad)
    kc, vc = pc[:, C_K:C_K + KVW].astype(BF16), pc[:, C_V:C_V + KVW].astype(BF16)
    attn = _attn_fwd(q_rot, kp, vp, kc, vc, w["attn_sink"])
    gate_w = (w["wg_f"], w["wg_b"], w["b_gate_fwd"], w["b_gate_bwd"])
    la_f, la_b = _gate_fwd("gate", p, *gate_w)
    la_fc, la_bc = _gate_fwd("gate_ctx", pc, *gate_w)
    st_f0, st_b0 = _ctx_state(pc, la_fc, la_bc)
    o_f, sts_f = _gla_fwd("gla_fwd_f", p, la_f, st_f0, False)
    o_b, sts_b = _gla_fwd("gla_fwd_b", p, la_b, st_b0, True)
    mix = _gla_out("gla_out", attn, o_f, o_b, p, gt)
    w_out, w_ffn_in_t, w_ffn_out = late_weights(attn)
    y = _mm("proj_out", mix, w_out, "nn")
    x1 = _post_res("post_mix", x, y, w["g_post_mix"], gt1)
    h2 = _norm_mod("pre_ffn", x1, w["g_pre_ffn"], sh2, sc2)
    u = _mm("ffn_in", h2, w_ffn_in_t, "nt", BF16)
    a = _swiglu("swiglu", u)
    f = _mm("ffn_out", a, w_ffn_out, "nn")
    dx2, loss = _post_res_loss("post_ffn_loss", x1, f, w["g_post_ffn"], gt2, target)

    g = {}
    df, g["g_post_ffn"], dgt2 = _post_res_bwd("post_ffn_bwd", dx2, f, w["g_post_ffn"], gt2)
    da = _mm("ffn_out_dx", df, w_ffn_out, "nt", BF16)
    g["w_ffn_out"] = _mm("ffn_out_dw", a, df, "tn")
    du = _swiglu_bwd("swiglu_bwd", da, u)
    dh2 = _mm("ffn_in_dx", du, w_ffn_in_t, "nn")
    g["w_ffn_in_t"] = _mm("ffn_in_dw", du, h2, "tn")
    dx1, g["g_pre_ffn"], dsh2, dsc2 = _norm_mod_bwd("pre_ffn_bwd", dh2, dx2, x1, w["g_pre_ffn"], sh2, sc2)
    dy, g["g_post_mix"], dgt1 = _post_res_bwd("post_mix_bwd", dx1, y, w["g_post_mix"], gt1)
    dmix = _mm("proj_out_dx", dy, w_out, "nt", BF16)
    g["w_out"] = _mm("proj_out_dw", mix, dy, "tn")
    rb, sink, token = reduce_behind, w["attn_sink"], None
    if rb is not None:
        gt = _behind(gt, rb.start(g["w_out"], g["w_ffn_out"], g["w_ffn_in_t"]))
    d_o, dgg, dgt = _gla_out_bwd("gla_out_bwd", dmix, o_f, o_b, p, gt)
    g["g_gla_norm"] = jnp.sum(dgt.reshape(GLA_HEADS, GLA_DV), axis=0, keepdims=True)
    if rb is not None:
        token = rb.pair(dgg)
    dgq, dgk, dgv, dla_f, dst_f0 = _gla_bwd("gla_bwd_f", p, la_f, sts_f, d_o, None, False, token)
    dgq, dgk, dgv, dla_b, dst_b0 = _gla_bwd("gla_bwd_b", p, la_b, sts_b, d_o, (dgq, dgk, dgv), True)
    if rb is not None:
        sink = _behind(sink, rb.total(dgq))
    dgkc, dgvc, dla_fc, dla_bc = _ctx_state_bwd(pc, la_fc, la_bc, dst_f0, dst_b0)
    dz, dwf, dwb, dbf, dbb = _gate_bwd("gate_bwd", p, dla_f, dla_b, *gate_w)
    dzc, dwfc, dwbc, dbfc, dbbc = _gate_bwd("gate_ctx_bwd", pc, dla_fc, dla_bc, *gate_w)
    g["w_gate_fwd"] = (dwf + dwfc)[:GATE_RANK]
    g["w_gate_bwd"] = (dwb + dwbc)[GATE_RANK:2 * GATE_RANK]
    g["b_gate_fwd"], g["b_gate_bwd"] = dbf + dbfc, dbb + dbbc
    dq_rot, dkp, dvp, dkc, dvc, g["attn_sink"] = _attn_bwd(dmix, q_rot, kp, vp, kc, vc, sink)
    if rb is not None:
        g["behind"] = rb.result(dq_rot)
    dq, dk = _rope_bwd("rope_bwd", dq_rot, dkp[BLOCK:BLOCK + s], cos, sin)
    dp = jnp.concatenate([dq, dgv.astype(BF16), dgg, dk, dvp[BLOCK:BLOCK + s].astype(BF16), dgq.astype(BF16),
                          dgk.astype(BF16), dz], axis=1)
    c_rows = ctx.shape[0]
    zeros = lambda n: jnp.zeros((c_rows, n), BF16)
    dpc = jnp.concatenate([zeros(QW), dgvc, zeros(GVW), dkc.astype(BF16), dvc.astype(BF16), zeros(GKW), dgkc, dzc],
                          axis=1)
    dh = _mm("proj_in_dx", dp, w["w_in"], "nt")
    dhc = _mm("proj_in_ctx_dx", dpc, w["w_in"], "nt")
    g["w_in"] = _mm("proj_in_dw", h, dp, "tn", init=_mm("proj_in_ctx_dw", hc, dpc, "tn"))
    dx, dg_a, dsh1, dsc1 = _norm_mod_bwd("pre_mix_bwd", dh, dx1, x, w["g_pre_mix"], sh1, sc1)
    _, dg_b, dsh1c, dsc1c = _norm_mod_bwd("pre_mix_ctx_bwd", dhc, jnp.zeros_like(dhc), ctx, w["g_pre_mix"], sh1c,
                                          sc1c)
    g["g_pre_mix"] = dg_a + dg_b
    d_ada = jnp.concatenate([dsh1, dsc1, dgt1, dsh2, dsc2, dgt2], axis=1)
    d_ada_c = jnp.concatenate([dsh1c, dsc1c, jnp.zeros((1, 4 * d), F32)], axis=1)
    return loss, dx, g, d_ada, d_ada_c


HBM = pl.BlockSpec(memory_space=pltpu.HBM)
N_DEV, N_CHIP = 8, 4


def _place():
    x, y, c = lax.axis_index("x"), lax.axis_index("y"), lax.axis_index("c")
    return x, y, c, [(1 - x, y), (x, 1 - y), (1 - x, 1 - y)]


def _row_tile(n, mult, cap):
    return max(t for t in range(mult, min(n, cap) + 1, mult) if n % t == 0)


def _ag_small(name, v, after=None):
    follow = () if after is None else (after,)

    def body(v_ref, *rest):
        out_ref, send_sems, recv_sems = rest[len(follow):]
        x, y, c, _ = _place()
        out_ref[4 * x + 2 * y + c] = v_ref[...]

        def peer(r):
            return ((1 - x) if r & 4 else x, (1 - y) if r & 2 else y, (1 - c) if r & 1 else c)

        def copy(r, block):
            px, py, pc = block
            return pltpu.make_async_remote_copy(
                src_ref=v_ref, dst_ref=out_ref.at[4 * px + 2 * py + pc], send_sem=send_sems.at[r - 1],
                recv_sem=recv_sems.at[r - 1], device_id=peer(r), device_id_type=MESH)

        sends = [copy(r, (x, y, c)) for r in range(1, N_DEV)]
        for cp in sends:
            cp.start()
        for r in range(1, N_DEV):
            copy(r, peer(r)).wait_recv()
        for cp in sends:
            cp.wait_send()

    return pl.pallas_call(
        body, name=name, out_shape=jax.ShapeDtypeStruct((N_DEV,) + v.shape, v.dtype),
        in_specs=[pl.BlockSpec(memory_space=pltpu.VMEM)] + [pl.BlockSpec(memory_space=pl.ANY)] * len(follow),
        out_specs=pl.BlockSpec(memory_space=pltpu.VMEM),
        scratch_shapes=[pltpu.SemaphoreType.DMA((N_DEV - 1,)), pltpu.SemaphoreType.DMA((N_DEV - 1,))],
    )(v, *follow)


def _halves(c, rows, mult):
    hr = rows // 2
    return pl.ds(pl.multiple_of(c * hr, mult), hr), pl.ds(pl.multiple_of((1 - c) * hr, mult), hr)


def _ag_shards(name, shard):
    rows = shard.shape[0]

    def body(w_ref, out_ref, send_sems, recv_sems, local_sem):
        x, y, c, chips = _place()
        mine_half, other_half = _halves(c, rows, 16)
        me = 2 * x + y
        mine = pltpu.make_async_copy(w_ref, out_ref.at[me], local_sem)
        mine.start()

        def copy(k, src, chip, half, to):
            return pltpu.make_async_remote_copy(
                src_ref=src, dst_ref=out_ref.at[chip, half], send_sem=send_sems.at[k], recv_sem=recv_sems.at[k],
                device_id=to, device_id_type=MESH)

        first = [copy(j, w_ref.at[mine_half], me, mine_half, (px, py, c)) for j, (px, py) in enumerate(chips)]
        for cp in first:
            cp.start()
        passed = []
        for j, (px, py) in enumerate(chips):
            pk = 2 * px + py
            copy(j, w_ref.at[mine_half], pk, mine_half, (px, py, c)).wait_recv()
            cp = copy(3 + j, out_ref.at[pk, mine_half], pk, mine_half, (x, y, 1 - c))
            cp.start()
            passed.append(cp)
        for j, (px, py) in enumerate(chips):
            copy(3 + j, w_ref.at[mine_half], 2 * px + py, other_half, (x, y, 1 - c)).wait_recv()
        for cp in first + passed:
            cp.wait_send()
        mine.wait()

    return pl.pallas_call(
        body, name=name, out_shape=jax.ShapeDtypeStruct((N_CHIP,) + shard.shape, shard.dtype),
        in_specs=[HBM], out_specs=HBM,
        scratch_shapes=[pltpu.SemaphoreType.DMA((6,)), pltpu.SemaphoreType.DMA((6,)), pltpu.SemaphoreType.DMA],
    )(shard)


def _swap_half(name, g):
    n_sh, rows, n = g.shape

    def body(g_ref, a_ref, send_sem, recv_sem):
        x, y, c, _ = _place()
        _, other_half = _halves(c, rows, 8)
        cp = pltpu.make_async_remote_copy(
            src_ref=g_ref.at[pl.ds(0, n_sh), other_half], dst_ref=a_ref, send_sem=send_sem, recv_sem=recv_sem,
            device_id=(x, y, 1 - c), device_id_type=MESH)
        cp.start()
        cp.wait()

    return pl.pallas_call(
        body, name=name, out_shape=jax.ShapeDtypeStruct((n_sh, rows // 2, n), g.dtype), in_specs=[HBM], out_specs=HBM,
        scratch_shapes=[pltpu.SemaphoreType.DMA, pltpu.SemaphoreType.DMA],
    )(g)


def _add_half(name, g, a, c_idx):
    n_sh, hr, n = a.shape
    tr = _row_tile(hr, 16, 1024)
    nb = hr // tr

    def body(c_ref, g_ref, a_ref, o_ref):
        o_ref[...] = (g_ref[...] + a_ref[...]).astype(o_ref.dtype)

    return pl.pallas_call(
        body, name=name, out_shape=jax.ShapeDtypeStruct(a.shape, BF16),
        grid_spec=pltpu.PrefetchScalarGridSpec(
            num_scalar_prefetch=1, grid=(n_sh, nb),
            in_specs=[pl.BlockSpec((1, tr, n), lambda s, i, c_ref: (s, c_ref[0] * nb + i, 0)),
                      pl.BlockSpec((1, tr, n), lambda s, i, c_ref: (s, i, 0))],
            out_specs=pl.BlockSpec((1, tr, n), lambda s, i, c_ref: (s, i, 0))),
        compiler_params=_cp("parallel", "parallel"),
    )(c_idx, g, a)


def _scatter_chips(name, h):
    def body(h_ref, b_ref, send_sems, recv_sems, local_sem):
        x, y, c, chips = _place()
        me = 2 * x + y
        mine = pltpu.make_async_copy(h_ref.at[me], b_ref.at[me], local_sem)
        mine.start()

        def copy(j, src_block, dst_block, to):
            return pltpu.make_async_remote_copy(
                src_ref=h_ref.at[src_block], dst_ref=b_ref.at[dst_block], send_sem=send_sems.at[j],
                recv_sem=recv_sems.at[j], device_id=to, device_id_type=MESH)

        sends = [copy(j, 2 * px + py, me, (px, py, c)) for j, (px, py) in enumerate(chips)]
        for cp in sends:
            cp.start()
        for j, (px, py) in enumerate(chips):
            copy(j, me, 2 * px + py, (px, py, c)).wait_recv()
        for cp in sends:
            cp.wait_send()
        mine.wait()

    return pl.pallas_call(
        body, name=name, out_shape=jax.ShapeDtypeStruct(h.shape, h.dtype), in_specs=[HBM], out_specs=HBM,
        scratch_shapes=[pltpu.SemaphoreType.DMA((3,)), pltpu.SemaphoreType.DMA((3,)), pltpu.SemaphoreType.DMA],
    )(h)


def _sum_chips(name, b):
    n_sh, hr, n = b.shape
    tr = _row_tile(hr, 16, 1024)

    def body(b0, b1, b2, b3, o_ref):
        o_ref[...] = ((b0[0].astype(F32) + b1[0].astype(F32)) + b2[0].astype(F32)) + b3[0].astype(F32)

    return pl.pallas_call(
        body, name=name, grid=(hr // tr,), out_shape=jax.ShapeDtypeStruct((hr, n), F32),
        in_specs=[pl.BlockSpec((1, tr, n), functools.partial(lambda i, k: (k, i, 0), k=k)) for k in range(n_sh)],
        out_specs=pl.BlockSpec((tr, n), lambda i: (i, 0)), compiler_params=_cp("parallel"),
    )(b, b, b, b)


def _share_half(name, f):
    hr, n = f.shape

    def body(f_ref, out_ref, send_sem, recv_sem, local_sem):
        x, y, c, _ = _place()
        mine_half, other_half = _halves(c, 2 * hr, 8)
        mine = pltpu.make_async_copy(f_ref, out_ref.at[mine_half], local_sem)
        mine.start()

        def copy(half):
            return pltpu.make_async_remote_copy(
                src_ref=f_ref, dst_ref=out_ref.at[half], send_sem=send_sem, recv_sem=recv_sem,
                device_id=(x, y, 1 - c), device_id_type=MESH)

        send = copy(mine_half)
        send.start()
        copy(other_half).wait_recv()
        send.wait_send()
        mine.wait()

    return pl.pallas_call(
        body, name=name, out_shape=jax.ShapeDtypeStruct((2 * hr, n), f.dtype), in_specs=[HBM], out_specs=HBM,
        scratch_shapes=[pltpu.SemaphoreType.DMA, pltpu.SemaphoreType.DMA, pltpu.SemaphoreType.DMA],
    )(f)


def _reduce_shards(name, g, c_idx):
    a = _swap_half(name + "_swap", g)
    h = _add_half(name + "_pair", g, a, c_idx)
    b = _scatter_chips(name + "_scatter", h)
    f = _sum_chips(name + "_sum", b)
    return _share_half(name + "_share", f)


SEM = pl.BlockSpec(memory_space=pltpu.SEMAPHORE)
ANY = pl.BlockSpec(memory_space=pl.ANY)
DATAFLOW = pltpu.SideEffectType.DATAFLOW_SIDE_EFFECTING


def _remote(src, dst, send_sems, recv_sems, k, to):
    return pltpu.make_async_remote_copy(src_ref=src, dst_ref=dst, send_sem=send_sems.at[k], recv_sem=recv_sems.at[k],
                                        device_id=to, device_id_type=MESH)


def _split_copy(name, src, land_shape, land_dtype, n, plan, after=None):
    after = jnp.zeros((8, LANES), F32) if after is None else after

    def start_body(src_ref, land_ref, after_ref, send_sems, recv_sems, src_thru, land_thru, token):
        for cp in plan(src_ref, land_ref, send_sems, recv_sems)[0]:
            cp.start()
        token[...] = jnp.zeros_like(token)

    sems = pltpu.SemaphoreType.DMA((n,))
    send_sems, recv_sems, src_thru, land_thru, token = pl.pallas_call(
        start_body, name=name + "_start",
        out_shape=(sems, sems, pltpu.HBM(src.shape, src.dtype), pltpu.HBM(land_shape, land_dtype),
                   jax.ShapeDtypeStruct((8, LANES), F32)),
        in_specs=(HBM, HBM, ANY), out_specs=(SEM, SEM, HBM, HBM, pl.BlockSpec(memory_space=pltpu.VMEM)),
        input_output_aliases={0: 2, 1: 3}, compiler_params=pltpu.CompilerParams(has_side_effects=DATAFLOW),
    )(pltpu.with_memory_space_constraint(src, pltpu.HBM),
      pltpu.with_memory_space_constraint(lax.empty(land_shape, land_dtype), pltpu.HBM), after)

    def wait(after):
        def wait_body(src_ref, land_ref, send_sems, recv_sems, after_ref, src_out, land_out):
            sent, received = plan(src_ref, land_ref, send_sems, recv_sems)
            for cp in sent:
                cp.wait_send()
            for cp in received:
                cp.wait_recv()

        return pl.pallas_call(
            wait_body, name=name + "_wait",
            out_shape=(pltpu.HBM(src.shape, src.dtype), pltpu.HBM(land_shape, land_dtype)),
            in_specs=(HBM, HBM, SEM, SEM, ANY), out_specs=(HBM, HBM), input_output_aliases={0: 0, 1: 1},
            compiler_params=pltpu.CompilerParams(has_side_effects=DATAFLOW),
        )(src_thru, land_thru, send_sems, recv_sems, after)

    return token, wait


def _behind(x, token):
    return x + token[0, 0]


def _plan_gather(src_ref, land_ref, send_sems, recv_sems):
    x, y, c, chips = _place()
    sent = [_remote(src_ref, land_ref.at[2 * x + y], send_sems, recv_sems, j, (px, py, c))
            for j, (px, py) in enumerate(chips)]
    received = [_remote(src_ref, land_ref.at[2 * px + py], send_sems, recv_sems, j, (px, py, c))
                for j, (px, py) in enumerate(chips)]
    return sent, received


def _plan_swap(src_ref, land_ref, send_sems, recv_sems):
    x, y, c, _ = _place()
    _, other_half = _halves(c, src_ref.shape[1], 8)
    cp = _remote(src_ref.at[pl.ds(0, src_ref.shape[0]), other_half], land_ref, send_sems, recv_sems, 0, (x, y, 1 - c))
    return [cp], [cp]


def _plan_scatter(src_ref, land_ref, send_sems, recv_sems):
    x, y, c, chips = _place()
    sent = [_remote(src_ref.at[2 * px + py], land_ref.at[2 * x + y], send_sems, recv_sems, j, (px, py, c))
            for j, (px, py) in enumerate(chips)]
    received = [_remote(src_ref.at[2 * px + py], land_ref.at[2 * px + py], send_sems, recv_sems, j, (px, py, c))
                for j, (px, py) in enumerate(chips)]
    return sent, received


def _plan_share(src_ref, land_ref, send_sems, recv_sems):
    x, y, c, _ = _place()
    mine_half, other_half = _halves(c, land_ref.shape[0], 8)
    return ([_remote(src_ref, land_ref.at[mine_half], send_sems, recv_sems, 0, (x, y, 1 - c))],
            [_remote(src_ref, land_ref.at[other_half], send_sems, recv_sems, 0, (x, y, 1 - c))])


class _GatherBehind:
    def __init__(self, name, shard, chip, after=None):
        self.chip = chip
        self.token, self.wait = _split_copy(name, shard, (N_CHIP,) + shard.shape, shard.dtype, 3, _plan_gather,
                                            after)

    def result(self, after):
        shard, land = self.wait(after)
        return lax.dynamic_update_slice(land, shard[None], (self.chip, 0, 0))


class _ReduceBehind:
    def __init__(self, name, chip, c, c_idx):
        self.name, self.chip, self.c, self.c_idx = name, chip, c, c_idx

    def start(self, *grads):
        g = jnp.stack([jnp.concatenate([t[k * (t.shape[0] // N_CHIP):(k + 1) * (t.shape[0] // N_CHIP)] for t in grads],
                                       axis=0) for k in range(N_CHIP)])
        n_sh, rows, n = g.shape
        token, self.wait = _split_copy(self.name + "_swap", g, (n_sh, rows // 2, n), g.dtype, 1, _plan_swap)
        return token

    def pair(self, after):
        g, a = self.wait(after)
        h = _add_half(self.name + "_pair", g, a, self.c_idx)
        token, self.wait = _split_copy(self.name + "_scatter", h, h.shape, h.dtype, 3, _plan_scatter)
        return token

    def total(self, after):
        h, b = self.wait(after)
        b = lax.dynamic_update_slice(b, lax.dynamic_slice_in_dim(h, self.chip, 1, axis=0), (self.chip, 0, 0))
        f = _sum_chips(self.name + "_sum", b)
        token, self.wait = _split_copy(self.name + "_share", f, (2 * f.shape[0], f.shape[1]), f.dtype, 1,
                                       _plan_share)
        return token

    def result(self, after):
        f, out = self.wait(after)
        return lax.dynamic_update_slice(out, f, (self.c * f.shape[0], 0))


def _f_adamw(w, g, m, v):
    m = ADAM_B1 * m + (1.0 - ADAM_B1) * g
    v = ADAM_B2 * v + (1.0 - ADAM_B2) * (g * g)
    m_hat = m / (1.0 - ADAM_B1 ** ADAM_STEP)
    v_hat = v / (1.0 - ADAM_B2 ** ADAM_STEP)
    return -ADAM_LR * (m_hat / (jnp.sqrt(v_hat) + ADAM_EPS) + ADAM_WD * w), m, v


def _adamw(name, w, g, m, v):
    rows, n = w.shape
    return _rowwise(name, lambda w, g, m, v: (_f_adamw(w, g, m, v), ()), rows, [(t, n, 0) for t in (w, g, m, v)], [],
                    [(n, F32)] * 3, [], tm=_row_tile(rows, 8, 256))


def _pack_rows(parts):
    rows = []
    for t in parts:
        t = t.reshape(-1)
        rows.append(jnp.pad(t, (0, -t.shape[0] % LANES)).reshape(-1, LANES))
    out = jnp.concatenate(rows, axis=0)
    return jnp.pad(out, ((0, -out.shape[0] % 8), (0, 0)))


def _unpack_rows(packed, shapes):
    out, r = [], 0
    for shp in shapes:
        n = int(np.prod(shp))
        nr = -(-n // LANES)
        out.append(packed[r:r + nr].reshape(-1)[:n].reshape(shp))
        r += nr
    return out


def _sum_blocks(name, g):
    def body(g_ref, o_ref):
        acc = g_ref[0]
        for k in range(1, g.shape[0]):
            acc = acc + g_ref[k]
        o_ref[...] = acc

    return pl.pallas_call(body, name=name, out_shape=jax.ShapeDtypeStruct(g.shape[1:], F32))(g)


def _silu(t):
    return t * _sigmoid(t)


def _ada_fwd(cc, w_ada):
    n = w_ada.shape[1]
    tn = _row_tile(n, LANES, 512)

    def body(cc_ref, w_ref, o_ref):
        o_ref[...] = _nn(_silu(cc_ref[...]), w_ref[...])

    return pl.pallas_call(
        body, name="ada_fwd", grid=(n // tn,), out_shape=jax.ShapeDtypeStruct((cc.shape[0], n), F32),
        in_specs=[pl.BlockSpec(cc.shape, lambda j: (0, 0)), pl.BlockSpec((w_ada.shape[0], tn), lambda j: (0, j))],
        out_specs=pl.BlockSpec((cc.shape[0], tn), lambda j: (0, j)), compiler_params=_cp("parallel"),
    )(cc, w_ada)


def _ada_bwd(cc, dm, w_ada):
    d, n = w_ada.shape
    tn = _row_tile(n, LANES, 512)

    def body(cc_ref, dm_ref, w_ref, gw_ref, ds_ref):
        @pl.when(pl.program_id(0) == 0)
        def _():
            ds_ref[...] = jnp.zeros_like(ds_ref)

        gw_ref[...] = _raw_dot("tn", _silu(cc_ref[...]), dm_ref[...], True)
        ds_ref[...] += _raw_dot("nt", dm_ref[...], w_ref[...], False)

    return pl.pallas_call(
        body, name="ada_bwd", grid=(n // tn,),
        out_shape=[jax.ShapeDtypeStruct((d, n), F32), jax.ShapeDtypeStruct(cc.shape, F32)],
        in_specs=[pl.BlockSpec(cc.shape, lambda j: (0, 0)), pl.BlockSpec((cc.shape[0], tn), lambda j: (0, j)),
                  pl.BlockSpec((d, tn), lambda j: (0, j))],
        out_specs=[pl.BlockSpec((d, tn), lambda j: (0, j)), pl.BlockSpec(cc.shape, lambda j: (0, 0))],
        compiler_params=_cp("arbitrary"),
    )(cc, dm, w_ada)


def _c_ctx_grad(parts, c_ctx):
    def body(p_ref, c_ref, o_ref):
        ds = ((p_ref[0] + p_ref[1]) + p_ref[2]) + p_ref[3]
        _, vjp = jax.vjp(_silu, c_ref[...])
        o_ref[...] = vjp(ds)[0]

    return pl.pallas_call(body, name="c_ctx_grad", out_shape=jax.ShapeDtypeStruct(c_ctx.shape, F32))(parts, c_ctx)


def kernel(x, c, ctx, c_ctx, w_ada, b_ada, g_pre_mix, g_post_mix, g_pre_ffn, g_post_ffn, w_in, attn_sink, w_gate_fwd, b_gate_fwd, w_gate_bwd, b_gate_bwd, g_gla_norm, w_out, w_ffn_in, w_ffn_out, loss_target, m_c_ctx, m_w_ada, m_b_ada, m_g_pre_mix, m_g_post_mix, m_g_pre_ffn, m_g_post_ffn, m_w_in, m_attn_sink, m_w_gate_fwd, m_b_gate_fwd, m_w_gate_bwd, m_b_gate_bwd, m_g_gla_norm, m_w_out, m_w_ffn_in, m_w_ffn_out, v_c_ctx, v_w_ada, v_b_ada, v_g_pre_mix, v_g_post_mix, v_g_pre_ffn, v_g_post_ffn, v_w_in, v_attn_sink, v_w_gate_fwd, v_b_gate_fwd, v_w_gate_bwd, v_b_gate_bwd, v_g_gla_norm, v_w_out, v_w_ffn_in, v_w_ffn_out):
    xi, yi, ci = lax.axis_index("x"), lax.axis_index("y"), lax.axis_index("c")
    dev, chip = 4 * xi + 2 * yi + ci, 2 * xi + yi
    c_idx = jnp.reshape(ci, (1,)).astype(jnp.int32)
    d = x.shape[-1]
    n_ada, n_in, n_f = w_ada.shape[-1], w_in.shape[-1], w_ffn_in.shape[-1]
    r_out, r_f = w_out.shape[1], w_ffn_out.shape[1]
    n_gate = w_gate_fwd.shape[-1]
    by_chip = lambda t: t[0::2]

    w_in_g = _ag_shards("gather_w_in", w_in[0].astype(BF16))

    rc = -(-d // LANES)
    g1 = _ag_small("gather_cond", _pack_rows([c[0], w_gate_fwd[0], w_gate_bwd[0]]), w_in_g)
    c_all = g1[:, :rc].reshape(N_DEV, -1)[:, :d]
    gr = GATE_RANK * n_gate // LANES
    gate_full = lambda off: jnp.transpose(by_chip(g1)[:, off:off + gr].reshape(N_CHIP, GATE_RANK, n_gate),
                                          (1, 0, 2)).reshape(GATE_RANK, N_CHIP * n_gate)
    wgf, wgb = gate_full(rc), gate_full(rc + gr)
    cc = jnp.concatenate([c_all, c_ctx[None, :], jnp.zeros((7, d), F32)], axis=0)

    g2 = _ag_small("gather_ada", _ada_fwd(cc, w_ada[0]).reshape(-1, LANES))
    ada_all = jnp.transpose(by_chip(g2).reshape(N_CHIP, 16, n_ada), (1, 0, 2)).reshape(16, N_CHIP * n_ada) + b_ada
    late = _GatherBehind("gather_late", jnp.concatenate(
        [w_out[0], w_ffn_out[0], jnp.transpose(w_ffn_in[0])], axis=0).astype(BF16), chip, g2)

    def late_weights(after):
        t = late.result(after)
        r1, r2 = r_out, r_out + r_f
        return (t[:, :r1].reshape(N_CHIP * r_out, d), t[:, r2:].reshape(N_CHIP * n_f, d),
                t[:, r1:r2].reshape(N_CHIP * r_f, d))

    ada_all = _behind(ada_all, late.token)
    ada = lax.dynamic_slice(ada_all, (dev, 0), (1, N_CHIP * n_ada))
    ada_c = ada_all[N_DEV:N_DEV + 1]

    w = _prep_weights(jnp.concatenate([w_in_g[k] for k in range(N_CHIP)], axis=1), wgf, wgb)
    w.update(g_pre_mix=g_pre_mix, g_post_mix=g_post_mix, g_pre_ffn=g_pre_ffn, g_post_ffn=g_post_ffn,
             attn_sink=attn_sink, b_gate_fwd=b_gate_fwd, b_gate_bwd=b_gate_bwd, g_gla_norm=g_gla_norm)

    reduce_behind = _ReduceBehind("reduce_late", chip, ci, c_idx)
    loss_lanes, grad_x, g, d_ada, d_ada_c = _local_step(x[0], ctx[0], loss_target[0], ada, ada_c, w, late_weights,
                                                        reduce_behind)

    small = ("g_pre_mix", "g_post_mix", "g_pre_ffn", "g_post_ffn", "attn_sink", "b_gate_fwd", "b_gate_bwd",
             "g_gla_norm", "w_gate_fwd", "w_gate_bwd")
    shapes = [(1, 6 * d)] * 2 + [g[n].shape for n in small]
    g3 = _ag_small("gather_small_grads", _pack_rows([d_ada, d_ada_c] + [g[n] for n in small]))
    tot = dict(zip(("d_ada", "d_ada_c") + small, _unpack_rows(_sum_blocks("sum_small_grads", g3), shapes)))
    r_ada = 6 * d // LANES
    dm = jnp.concatenate([g3[:, :r_ada].reshape(N_DEV, 6 * d), tot["d_ada_c"], jnp.zeros((7, 6 * d), F32)], axis=0)
    grads = {n: tot[n] for n in small[:8]}
    grads["b_ada"] = _sum_blocks("sum_b_ada", dm.reshape(16, r_ada, LANES)).reshape(1, 6 * d)
    grads["w_gate_fwd"] = lax.dynamic_slice(tot["w_gate_fwd"], (0, chip * n_gate), (GATE_RANK, n_gate))[None]
    grads["w_gate_bwd"] = lax.dynamic_slice(tot["w_gate_bwd"], (0, chip * n_gate), (GATE_RANK, n_gate))[None]
    gw_ada, dsc = _ada_bwd(cc, lax.dynamic_slice(dm, (0, chip * n_ada), (16, n_ada)), w_ada[0])
    grads["w_ada"] = gw_ada[None]
    g4 = _ag_small("gather_c_ctx", _pack_rows([dsc[N_DEV]]))
    grads["c_ctx"] = _c_ctx_grad(by_chip(g4), _pack_rows([c_ctx])).reshape(-1)[:d]

    g_w_in = _unpack_w_in_grad(g["w_in"])
    grads["w_in"] = _reduce_shards("reduce_w_in", jnp.stack([g_w_in[:, k * n_in:(k + 1) * n_in]
                                                             for k in range(N_CHIP)]), c_idx)[None]
    behind = g["behind"]
    grads["w_out"], grads["w_ffn_out"] = behind[None, :r_out], behind[None, r_out:r_out + r_f]
    grads["w_ffn_in"] = jnp.transpose(behind[r_out + r_f:])[None]

    names = ("c_ctx", "w_ada", "b_ada", "g_pre_mix", "g_post_mix", "g_pre_ffn", "g_post_ffn", "w_in", "attn_sink",
             "w_gate_fwd", "b_gate_fwd", "w_gate_bwd", "b_gate_bwd", "g_gla_norm", "w_out", "w_ffn_in", "w_ffn_out")
    weights = dict(zip(names, (c_ctx, w_ada, b_ada, g_pre_mix, g_post_mix, g_pre_ffn, g_post_ffn, w_in, attn_sink,
                               w_gate_fwd, b_gate_fwd, w_gate_bwd, b_gate_bwd, g_gla_norm, w_out, w_ffn_in,
                               w_ffn_out)))
    m_in = dict(zip(names, (m_c_ctx, m_w_ada, m_b_ada, m_g_pre_mix, m_g_post_mix, m_g_pre_ffn, m_g_post_ffn, m_w_in,
                            m_attn_sink, m_w_gate_fwd, m_b_gate_fwd, m_w_gate_bwd, m_b_gate_bwd, m_g_gla_norm,
                            m_w_out, m_w_ffn_in, m_w_ffn_out)))
    v_in = dict(zip(names, (v_c_ctx, v_w_ada, v_b_ada, v_g_pre_mix, v_g_post_mix, v_g_pre_ffn, v_g_post_ffn, v_w_in,
                            v_attn_sink, v_w_gate_fwd, v_b_gate_fwd, v_w_gate_bwd, v_b_gate_bwd, v_g_gla_norm,
                            v_w_out, v_w_ffn_in, v_w_ffn_out)))
    large = ("w_ada", "w_in", "w_out", "w_ffn_in", "w_ffn_out")
    tiny = tuple(n for n in names if n not in large)
    delta, new_m, new_v = {}, {}, {}
    for n in large:
        dl, nm, nv = _adamw("adamw_" + n, weights[n][0], grads[n][0], m_in[n][0], v_in[n][0])
        delta[n], new_m[n], new_v[n] = dl[None], nm[None], nv[None]
    tiny_shapes = [weights[n].shape for n in tiny]
    packed = [_pack_rows([t[n] for n in tiny]) for t in (weights, grads, m_in, v_in)]
    for out, res in zip((delta, new_m, new_v), _adamw("adamw_small", *packed)):
        out.update(zip(tiny, _unpack_rows(res, tiny_shapes)))
    for n in tiny:
        grads[n] = grads[n].reshape(weights[n].shape)

    loss = lax.psum(loss_lanes[0, 0], ("x", "y", "c"))
    return (loss, grad_x[None], *[grads[n] for n in names], *[delta[n] for n in names], *[new_m[n] for n in names],
            *[new_v[n] for n in names])
```

```python
import functools

import jax
import jax.numpy as jnp
import numpy as np
from jax import lax
from jax.experimental import pallas as pl
from jax.experimental.pallas import tpu as pltpu

F32 = jnp.float32
BF16 = jnp.bfloat16
MESH = pl.DeviceIdType.MESH

HEAD_DIM = 64
ATT_HEADS = 8
ATT_KV_HEADS = 2
ATT_GROUP = ATT_HEADS // ATT_KV_HEADS
WINDOW = 128
BLOCK = 128
GRID_W = 64
ROPE_BASE = 10000.0
GLA_HEADS = 8
GLA_DK = 32
GLA_DV = 64
GLA_CHUNK = 64
GATE_RANK = 16
GATE_TAU = 16.0
NEG_INF = -1e30
QW = ATT_HEADS * HEAD_DIM
KVW = ATT_KV_HEADS * HEAD_DIM
GKW = GLA_HEADS * GLA_DK
GVW = GLA_HEADS * GLA_DV
IN_COLS = QW + 2 * KVW + 2 * GKW + 2 * GVW + 2 * GATE_RANK
LANES = 128
IN_PAD = IN_COLS + LANES - 2 * GATE_RANK
C_Q, C_GV, C_GG = 0, QW, QW + GVW
C_K = C_GG + GVW
C_V = C_K + KVW
C_GQ = C_V + KVW
C_GK = C_GQ + GKW
C_Z = C_GK + GKW
MIX = QW + GVW

ADAM_LR, ADAM_B1, ADAM_B2, ADAM_EPS, ADAM_WD, ADAM_STEP = 0.001, 0.9, 0.999, 1e-08, 0.01, 10

VMEM_LIMIT = 56 * 1024 * 1024


def _cp(*sem):
    return pltpu.CompilerParams(dimension_semantics=sem, vmem_limit_bytes=VMEM_LIMIT)


def _pick(n, cands):
    for t in cands:
        if n % t == 0:
            return t
    return n


_DIMS = {"nn": (((1,), (0,)), ((), ())), "nt": (((1,), (1,)), ((), ())), "tn": (((0,), (0,)), ((), ()))}


def _raw_dot(mode, a, b, hi):
    if hi:
        return lax.dot_general(a.astype(F32), b.astype(F32), _DIMS[mode], precision=lax.Precision.HIGHEST,
                               preferred_element_type=F32)
    return lax.dot_general(a.astype(BF16), b.astype(BF16), _DIMS[mode], preferred_element_type=F32)


def _make_dot(mode, hi):
    @jax.custom_vjp
    def dot(a, b):
        return _raw_dot(mode, a, b, hi)

    def fwd(a, b):
        return _raw_dot(mode, a, b, hi), (a, b)

    def bwd(res, dc):
        a, b = res
        if mode == "nn":
            return _raw_dot("nt", dc, b, hi), _raw_dot("tn", a, dc, hi)
        if mode == "nt":
            return _raw_dot("nn", dc, b, hi), _raw_dot("tn", dc, a, hi)
        return _raw_dot("nt", b, dc, hi), _raw_dot("nn", a, dc, hi)

    dot.defvjp(fwd, bwd)
    return dot


_nn, _nt, _tn = _make_dot("nn", False), _make_dot("nt", False), _make_dot("tn", False)
_nn_hi = _make_dot("nn", True)


MM_VMEM_BUDGET = 44 * 1024 * 1024


def _halvings(n):
    out = [n]
    while out[-1] % (2 * LANES) == 0:
        out.append(out[-1] // 2)
    return out


def _mm_tiles(mode, m, n, k, a_bytes, b_bytes, o_bytes, init_bytes=0):
    tms = [t for t in dict.fromkeys((m, m // 2, m // 4, 2048, 1024, 512, 256, 128))
           if m % t == 0 and t % (LANES if mode == "tn" else 16) == 0 and t <= 4096] or [m]
    if mode == "tn":
        fits = [(k // tk + 0.5 * (m // tm), tm, tk)
                for tk in (4096, 2048, 1024, 512, 256, 128) if k % tk == 0 for tm in tms
                if 2 * (tk * tm * a_bytes + tk * n * b_bytes + tm * n * (o_bytes + init_bytes)) <= MM_VMEM_BUDGET]
        if fits:
            _, tm, tk = min(fits)
            return tm, n, tk
    tks = ([t for t in (512, 256, 128) if k % t == 0] or [k]) if mode == "tn" else _halvings(k)
    for tn in _halvings(n):
        for tk in tks:
            for tm in tms:
                acc = tm * tn * 4 if (k // tk > 1 and o_bytes != 4) else 0
                tiles = tm * tk * a_bytes + tk * tn * b_bytes + tm * tn * (o_bytes + init_bytes)
                if 2 * tiles + acc <= MM_VMEM_BUDGET:
                    return tm, tn, tk
    return tms[-1], _halvings(n)[-1], tks[-1]


def _mm(name, a, b, mode, out_dtype=F32, init=None):
    if mode == "nn":
        (m, k), n = a.shape, b.shape[1]
    elif mode == "nt":
        (m, k), n = a.shape, b.shape[0]
    else:
        (k, m), n = a.shape, b.shape[1]
    tm, tn, tk = _mm_tiles(mode, m, n, k, a.dtype.itemsize, b.dtype.itemsize, jnp.dtype(out_dtype).itemsize,
                           0 if init is None else 4)
    nk = k // tk
    use_acc = nk > 1 and out_dtype != F32

    inits = () if init is None else (init,)

    def body(a_ref, b_ref, *rest):
        o_ref, acc = rest[len(inits)], rest[len(inits) + 1:]
        part = _raw_dot(mode, a_ref[...], b_ref[...], False)
        first = lambda: part + rest[0][...] if inits else part
        if nk == 1:
            o_ref[...] = first().astype(o_ref.dtype)
            return
        acc_ref = acc[0] if use_acc else o_ref
        kk = pl.program_id(2)

        @pl.when(kk == 0)
        def _():
            acc_ref[...] = first()

        @pl.when(kk > 0)
        def _():
            acc_ref[...] += part

        if use_acc:
            @pl.when(kk == nk - 1)
            def _():
                o_ref[...] = acc_ref[...].astype(o_ref.dtype)

    if mode == "nn":
        a_spec = pl.BlockSpec((tm, tk), lambda i, j, kk: (i, kk))
        b_spec = pl.BlockSpec((tk, tn), lambda i, j, kk: (kk, j))
    elif mode == "nt":
        a_spec = pl.BlockSpec((tm, tk), lambda i, j, kk: (i, kk))
        b_spec = pl.BlockSpec((tn, tk), lambda i, j, kk: (j, kk))
    else:
        a_spec = pl.BlockSpec((tk, tm), lambda i, j, kk: (kk, i))
        b_spec = pl.BlockSpec((tk, tn), lambda i, j, kk: (kk, j))
    return pl.pallas_call(
        body, name=name, grid=(m // tm, n // tn, nk),
        in_specs=[a_spec, b_spec] + [pl.BlockSpec((tm, tn), lambda i, j, kk: (i, j))] * len(inits),
        out_specs=pl.BlockSpec((tm, tn), lambda i, j, kk: (i, j)),
        out_shape=jax.ShapeDtypeStruct((m, n), out_dtype),
        scratch_shapes=[pltpu.VMEM((tm, tn), F32)] if use_acc else [],
        compiler_params=_cp("parallel", "parallel", "arbitrary"),
    )(a, b, *inits)


def _rowwise(name, fn, rows, row_ins, full_ins, row_outs, acc_outs, tm=None):
    tm = tm or _pick(rows, (512, 256, 128))
    n_r, n_f, n_o, n_a = len(row_ins), len(full_ins), len(row_outs), len(acc_outs)

    def body(*refs):
        ins, outs = refs[:n_r + n_f], refs[n_r + n_f:]
        vals = [r[...].astype(F32) for r in ins]
        ro, ao = fn(*vals)
        for r, val in zip(outs[:n_o], ro):
            r[...] = val.astype(r.dtype)
        if n_a:
            @pl.when(pl.program_id(0) == 0)
            def _():
                for r in outs[n_o:]:
                    r[...] = jnp.zeros_like(r)

            for r, val in zip(outs[n_o:], ao):
                r[...] += val

    in_specs = [pl.BlockSpec((tm, w), functools.partial(lambda i, cb: (i, cb), cb=cb)) for _, w, cb in row_ins]
    in_specs += [pl.BlockSpec(a.shape, lambda i: (0, 0)) for a in full_ins]
    out_specs = [pl.BlockSpec((tm, w), lambda i: (i, 0)) for w, _ in row_outs]
    out_specs += [pl.BlockSpec(s, lambda i: (0, 0)) for s in acc_outs]
    out_shape = [jax.ShapeDtypeStruct((rows, w), dt) for w, dt in row_outs]
    out_shape += [jax.ShapeDtypeStruct(s, F32) for s in acc_outs]
    return pl.pallas_call(
        body, name=name, grid=(rows // tm,), in_specs=in_specs, out_specs=out_specs, out_shape=out_shape,
        compiler_params=_cp("arbitrary" if n_a else "parallel"),
    )(*[a for a, _, _ in row_ins], *full_ins)


def _rn(x):
    return x * lax.rsqrt(jnp.mean(x * x, axis=-1, keepdims=True) + 1e-6)


def _sigmoid(t):
    return 1.0 / (1.0 + jnp.exp(-t))


def _f_norm_mod(x, g, sh, sc):
    return _rn(x) * g * (1.0 + sc) + sh


def _f_post_res(xr, y, g, gate):
    return xr + gate * (_rn(y) * g)


def _f_swiglu(g, u):
    return g * _sigmoid(g) * u


def _logsig(u):
    return jnp.minimum(u, 0.0) - jnp.log(1.0 + jnp.exp(-jnp.abs(u)))


def _f_gate(z, wf, wb, bf, bb):
    return _logsig(_nn(z, wf) + bf) / GATE_TAU, _logsig(_nn(z, wb) + bb) / GATE_TAU


def _f_gla_out(of, ob, gg, gt, bd):
    o = of + ob
    ms = _nn_hi(o * o, bd)
    return o * lax.rsqrt(ms + 1e-6) * gt * (gg * _sigmoid(gg))


def _norm_mod(name, x, g, sh, sc):
    rows, d = x.shape
    return _rowwise(name, lambda x, g, sh, sc: ((_f_norm_mod(x, g, sh, sc),), ()), rows,
                    [(x, d, 0)], [g, sh, sc], [(d, BF16)], [])[0]


def _norm_mod_bwd(name, dh, dres, x, g, sh, sc):
    rows, d = x.shape

    def fn(dh, dres, x, g, sh, sc):
        _, vjp = jax.vjp(_f_norm_mod, x, g, sh, sc)
        dx, dg, dsh, dsc = vjp(dh)
        return (dx + dres,), (dg, dsh, dsc)

    return _rowwise(name, fn, rows, [(dh, d, 0), (dres, d, 0), (x, d, 0)], [g, sh, sc], [(d, F32)],
                    [(1, d)] * 3)


def _post_res(name, xr, y, g, gate):
    rows, d = xr.shape
    return _rowwise(name, lambda xr, y, g, gate: ((_f_post_res(xr, y, g, gate),), ()), rows,
                    [(xr, d, 0), (y, d, 0)], [g, gate], [(d, F32)], [])[0]


def _post_res_bwd(name, dxo, y, g, gate):
    rows, d = y.shape

    def fn(dxo, y, g, gate):
        _, vjp = jax.vjp(lambda y, g, gate: _f_post_res(jnp.zeros_like(y), y, g, gate), y, g, gate)
        dy, dg, dgate = vjp(dxo)
        return (dy,), (dg, dgate)

    return _rowwise(name, fn, rows, [(dxo, d, 0), (y, d, 0)], [g, gate], [(d, BF16)], [(1, d)] * 2)


def _post_res_norm_mod(name, xr, y, g_post, gate, g_pre, sh, sc):
    rows, d = xr.shape

    def fn(xr, y, g_post, gate, g_pre, sh, sc):
        x1 = _f_post_res(xr, y, g_post, gate)
        return (x1, _f_norm_mod(x1, g_pre, sh, sc)), ()

    return _rowwise(name, fn, rows, [(xr, d, 0), (y, d, 0)], [g_post, gate, g_pre, sh, sc], [(d, F32), (d, BF16)], [])


def _norm_mod_post_res_bwd(name, dh, dres, x1, y, g_pre, sh, sc, g_post, gate):
    rows, d = x1.shape

    def fn(dh, dres, x1, y, g_pre, sh, sc, g_post, gate):
        _, vjp_norm = jax.vjp(_f_norm_mod, x1, g_pre, sh, sc)
        dx1, dg_pre, dsh, dsc = vjp_norm(dh)
        dx1 = dx1 + dres
        _, vjp_res = jax.vjp(lambda y, g, gate: _f_post_res(jnp.zeros_like(y), y, g, gate), y, g_post, gate)
        dy, dg_post, dgate = vjp_res(dx1)
        return (dx1, dy), (dg_pre, dsh, dsc, dg_post, dgate)

    return _rowwise(name, fn, rows, [(dh, d, 0), (dres, d, 0), (x1, d, 0), (y, d, 0)], [g_pre, sh, sc, g_post, gate],
                    [(d, F32), (d, BF16)], [(1, d)] * 5, tm=_pick(rows, (256, 128)))


def _post_res_loss(name, xr, y, g, gate, target):
    rows, d = xr.shape

    def fn(xr, y, target, g, gate):
        x2, vjp = jax.vjp(lambda y, g, gate: _f_post_res(xr, y, g, gate), y, g, gate)
        diff = x2 - target
        part = 0.5 * jnp.sum(jnp.mean(diff * diff, axis=-1, keepdims=True), axis=0, keepdims=True)
        dx2 = diff * (1.0 / d)
        dy, dg, dgate = vjp(dx2)
        return (dx2, dy), (jnp.broadcast_to(part, (1, LANES)), dg, dgate)

    return _rowwise(name, fn, rows, [(xr, d, 0), (y, d, 0), (target, d, 0)], [g, gate], [(d, F32), (d, BF16)],
                    [(1, LANES), (1, d), (1, d)])


def _swiglu(name, u):
    rows, f2 = u.shape
    f = f2 // 2
    return _rowwise(name, lambda g, u: ((_f_swiglu(g, u),), ()), rows, [(u, f, 0), (u, f, 1)], [], [(f, BF16)], [],
                    tm=_pick(rows, (512, 256, 128)))[0]


def _swiglu_bwd(name, da, u):
    rows, f2 = u.shape
    f = f2 // 2

    def fn(da, g, u):
        _, vjp = jax.vjp(_f_swiglu, g, u)
        return (jnp.concatenate(vjp(da), axis=1),), ()

    return _rowwise(name, fn, rows, [(da, f, 0), (u, f, 0), (u, f, 1)], [], [(f2, BF16)], [],
                    tm=_pick(rows, (512, 256, 128)))[0]


def _gate_fwd(name, p, wf, wb, bf, bb):
    rows = p.shape[0]
    return _rowwise(name, lambda z, wf, wb, bf, bb: (_f_gate(z, wf, wb, bf, bb), ()), rows,
                    [(p, LANES, C_Z // LANES)], [wf, wb, bf, bb], [(GKW, F32)] * 2, [])


def _gate_bwd(name, p, dla_f, dla_b, wf, wb, bf, bb):
    rows = p.shape[0]

    def fn(z, dlf, dlb, wf, wb, bf, bb):
        _, vjp = jax.vjp(_f_gate, z, wf, wb, bf, bb)
        dz, dwf, dwb, dbf, dbb = vjp((dlf, dlb))
        return (dz,), (dwf, dwb, dbf, dbb)

    return _rowwise(name, fn, rows, [(p, LANES, C_Z // LANES), (dla_f, GKW, 0), (dla_b, GKW, 0)],
                    [wf, wb, bf, bb], [(LANES, BF16)], [(LANES, GKW), (LANES, GKW), (1, GKW), (1, GKW)])


def _head_mean_matrix():
    h = np.arange(GVW) // GLA_DV
    return jnp.asarray((h[:, None] == h[None, :]).astype(np.float32) / GLA_DV)


def _gla_out(name, attn, of, ob, p, gt):
    rows = of.shape[0]
    bd = _head_mean_matrix()
    fn = lambda attn, of, ob, gg, gt, bd: ((jnp.concatenate([attn, _f_gla_out(of, ob, gg, gt, bd)], axis=1),), ())
    return _rowwise(name, fn, rows, [(attn, QW, 0), (of, GVW, 0), (ob, GVW, 0), (p, GVW, C_GG // GVW)], [gt, bd],
                    [(MIX, BF16)], [])[0]


def _gla_out_bwd(name, dmix, of, ob, p, gt):
    rows = of.shape[0]
    bd = _head_mean_matrix()

    def fn(dm, of, ob, gg, gt, bd):
        _, vjp = jax.vjp(lambda of, gg, gt: _f_gla_out(of, ob, gg, gt, bd), of, gg, gt)
        do, dgg, dgt = vjp(dm)
        return (do, dgg), (dgt,)

    return _rowwise(name, fn, rows, [(dmix, GVW, 1), (of, GVW, 0), (ob, GVW, 0), (p, GVW, C_GG // GVW)], [gt, bd],
                    [(GVW, F32), (GVW, BF16)], [(1, GVW)])


def _rope_tables(n_tokens):
    t = jnp.arange(n_tokens)
    row = (t // GRID_W).astype(F32)
    col = (t % GRID_W).astype(F32)
    half = HEAD_DIM // 2
    inv_freq = ROPE_BASE ** (-jnp.arange(0, half, 2, dtype=F32) / half)
    ang_r = row[:, None] * inv_freq[None, :]
    ang_c = col[:, None] * inv_freq[None, :]
    ang = jnp.concatenate([ang_r, ang_r, ang_c, ang_c], axis=-1)
    sign = jnp.concatenate([-jnp.ones((16,), F32), jnp.ones((16,), F32)] * 2)
    cos, sin = jnp.cos(ang), jnp.sin(ang) * sign[None, :]
    return jnp.tile(cos, (1, 2)), jnp.tile(sin, (1, 2))


def _rot_pairs(x):
    w = x.shape[-1]
    lane = lax.broadcasted_iota(jnp.int32, x.shape, x.ndim - 1)
    return jnp.where((lane % 32) < 16, pltpu.roll(x, w - 16, x.ndim - 1), pltpu.roll(x, 16, x.ndim - 1))


def _rope_apply(x, cos, sin_signed, inverse):
    reps = x.shape[-1] // LANES
    cos = jnp.concatenate([cos] * reps, axis=-1) if reps > 1 else cos
    sin = jnp.concatenate([sin_signed] * reps, axis=-1) if reps > 1 else sin_signed
    if inverse:
        return x * cos + _rot_pairs(x * sin)
    return x * cos + _rot_pairs(x) * sin


def _rope_fwd(name, p, cos, sin):
    rows = p.shape[0]

    def fn(q, k, v, cos, sin):
        return (_rope_apply(q, cos, sin, False), _rope_apply(k, cos, sin, False), v), ()

    return _rowwise(name, fn, rows, [(p, QW, 0), (p, KVW, C_K // KVW), (p, KVW, C_V // KVW), (cos, LANES, 0),
                                     (sin, LANES, 0)], [], [(QW, BF16), (KVW, BF16), (KVW, BF16)], [])


def _rope_bwd(name, dq, dk, cos, sin):
    rows = dq.shape[0]

    def fn(dq, dk, cos, sin):
        return (_rope_apply(dq, cos, sin, True), _rope_apply(dk, cos, sin, True)), ()

    return _rowwise(name, fn, rows, [(dq, QW, 0), (dk, KVW, 0), (cos, LANES, 0), (sin, LANES, 0)], [],
                    [(QW, BF16), (KVW, BF16)], [])


GROUP_ROWS = ATT_GROUP * BLOCK


def _f_attn(qs, kws, vws, kcs, vcs, sink, n, n_tokens):
    row = lax.broadcasted_iota(jnp.int32, (GROUP_ROWS, 1), 0)
    group = sum((row >= g * BLOCK).astype(jnp.int32) for g in range(1, ATT_GROUP))
    i = lax.broadcasted_iota(jnp.int32, (GROUP_ROWS, 3 * BLOCK), 0) - BLOCK * group
    j = lax.broadcasted_iota(jnp.int32, (GROUP_ROWS, 3 * BLOCK), 1)
    kpos = (n - 1) * BLOCK + j
    mask = (jnp.abs(j - BLOCK - i) <= WINDOW) & (kpos >= 0) & (kpos < n_tokens)
    head_id = lax.broadcasted_iota(jnp.int32, (1, ATT_HEADS), 1)
    scale = HEAD_DIM ** -0.5
    outs = []
    for h in range(ATT_KV_HEADS):
        sk = jnp.zeros((GROUP_ROWS, 1), F32)
        for g in range(ATT_GROUP):
            one = jnp.sum(jnp.where(head_id == h * ATT_GROUP + g, sink, 0.0), axis=-1, keepdims=True)
            sk = jnp.where(group == g, one, sk)
        q = qs[h] * scale
        s_w = jnp.where(mask, _nt(q, kws[h]), NEG_INF)
        s_c = _nt(q, kcs[h])
        m = lax.stop_gradient(jnp.maximum(jnp.maximum(jnp.max(s_w, axis=-1, keepdims=True),
                                                      jnp.max(s_c, axis=-1, keepdims=True)), sk))
        pw, pc = jnp.exp(s_w - m), jnp.exp(s_c - m)
        den = jnp.sum(pw, axis=-1, keepdims=True) + jnp.sum(pc, axis=-1, keepdims=True) + jnp.exp(sk - m)
        outs.append((_nn(pw, vws[h]) + _nn(pc, vcs[h])) / den)
    return tuple(outs)


def _group_rows(ref, h):
    hs = lambda hq: slice(hq * HEAD_DIM, (hq + 1) * HEAD_DIM)
    return jnp.concatenate([ref[:, hs(h * ATT_GROUP + g)].astype(F32) for g in range(ATT_GROUP)], axis=0)


def _ungroup_rows(ref, h, val):
    for g in range(ATT_GROUP):
        hq = h * ATT_GROUP + g
        ref[:, hq * HEAD_DIM:(hq + 1) * HEAD_DIM] = val[g * BLOCK:(g + 1) * BLOCK].astype(ref.dtype)


def _attn_loads(n, q_ref, kp_ref, vp_ref, kc_ref, vc_ref):
    r0 = pl.multiple_of(n * BLOCK, BLOCK)
    hs = lambda h: slice(h * HEAD_DIM, (h + 1) * HEAD_DIM)
    qs = [_group_rows(q_ref, h) for h in range(ATT_KV_HEADS)]
    kws = [kp_ref[pl.ds(r0, 3 * BLOCK), hs(h)].astype(F32) for h in range(ATT_KV_HEADS)]
    vws = [vp_ref[pl.ds(r0, 3 * BLOCK), hs(h)].astype(F32) for h in range(ATT_KV_HEADS)]
    kcs = [kc_ref[:, hs(h)].astype(F32) for h in range(ATT_KV_HEADS)]
    vcs = [vc_ref[:, hs(h)].astype(F32) for h in range(ATT_KV_HEADS)]
    return r0, hs, qs, kws, vws, kcs, vcs


def _attn_specs(s, c):
    full = lambda shape: pl.BlockSpec(shape, lambda n: (0, 0))
    return [pl.BlockSpec((BLOCK, QW), lambda n: (n, 0)), full((s + 2 * BLOCK, KVW)), full((s + 2 * BLOCK, KVW)),
            full((c, KVW)), full((c, KVW)), full((1, ATT_HEADS))]


def _attn_fwd(q, kp, vp, kc, vc, sink):
    s, c = q.shape[0], kc.shape[0]

    def body(q_ref, kp_ref, vp_ref, kc_ref, vc_ref, sink_ref, o_ref):
        n = pl.program_id(0)
        _, hs, qs, kws, vws, kcs, vcs = _attn_loads(n, q_ref, kp_ref, vp_ref, kc_ref, vc_ref)
        outs = _f_attn(qs, kws, vws, kcs, vcs, sink_ref[...], n, s)
        for h in range(ATT_KV_HEADS):
            _ungroup_rows(o_ref, h, outs[h])

    return pl.pallas_call(
        body, name="attn_fwd", grid=(s // BLOCK,), in_specs=_attn_specs(s, c),
        out_specs=pl.BlockSpec((BLOCK, QW), lambda n: (n, 0)), out_shape=jax.ShapeDtypeStruct((s, QW), BF16),
        compiler_params=_cp("parallel"),
    )(q, kp, vp, kc, vc, sink)


def _attn_bwd(do, q, kp, vp, kc, vc, sink):
    s, c = q.shape[0], kc.shape[0]

    def body(do_ref, q_ref, kp_ref, vp_ref, kc_ref, vc_ref, sink_ref, dq_ref, dkp_ref, dvp_ref, dkc_ref, dvc_ref,
             dsink_ref):
        n = pl.program_id(0)

        @pl.when(n == 0)
        def _():
            for r in (dkp_ref, dvp_ref, dkc_ref, dvc_ref, dsink_ref):
                r[...] = jnp.zeros_like(r)

        r0, hs, qs, kws, vws, kcs, vcs = _attn_loads(n, q_ref, kp_ref, vp_ref, kc_ref, vc_ref)
        _, vjp = jax.vjp(lambda qs, kws, vws, kcs, vcs, sink: _f_attn(qs, kws, vws, kcs, vcs, sink, n, s),
                         qs, kws, vws, kcs, vcs, sink_ref[...])
        dqs, dkws, dvws, dkcs, dvcs, dsink = vjp(tuple(_group_rows(do_ref, h) for h in range(ATT_KV_HEADS)))
        for h in range(ATT_KV_HEADS):
            _ungroup_rows(dq_ref, h, dqs[h])
            dkp_ref[pl.ds(r0, 3 * BLOCK), hs(h)] += dkws[h]
            dvp_ref[pl.ds(r0, 3 * BLOCK), hs(h)] += dvws[h]
            dkc_ref[:, hs(h)] += dkcs[h]
            dvc_ref[:, hs(h)] += dvcs[h]
        dsink_ref[...] += dsink

    full = lambda shape: pl.BlockSpec(shape, lambda n: (0, 0))
    return pl.pallas_call(
        body, name="attn_bwd", grid=(s // BLOCK,),
        in_specs=[pl.BlockSpec((BLOCK, QW), lambda n: (n, 0))] + _attn_specs(s, c),
        out_specs=[pl.BlockSpec((BLOCK, QW), lambda n: (n, 0)), full((s + 2 * BLOCK, KVW)), full((s + 2 * BLOCK, KVW)),
                   full((c, KVW)), full((c, KVW)), full((1, ATT_HEADS))],
        out_shape=[jax.ShapeDtypeStruct((s, QW), F32), jax.ShapeDtypeStruct((s + 2 * BLOCK, KVW), F32),
                   jax.ShapeDtypeStruct((s + 2 * BLOCK, KVW), F32), jax.ShapeDtypeStruct((c, KVW), F32),
                   jax.ShapeDtypeStruct((c, KVW), F32), jax.ShapeDtypeStruct((1, ATT_HEADS), F32)],
        compiler_params=_cp("arbitrary"),
    )(do, q, kp, vp, kc, vc, sink)


def _gla_masks():
    hk = np.arange(GKW) // GLA_DK
    hv = np.arange(GVW) // GLA_DV
    head_k = (np.arange(GLA_HEADS)[:, None] == hk[None, :]).astype(np.float32)
    head_v = (np.arange(GLA_HEADS)[:, None] == hv[None, :]).astype(np.float32)
    bd_t = (hv[:, None] == hk[None, :]).astype(np.float32)
    return jnp.asarray(head_k), jnp.asarray(head_v), jnp.asarray(bd_t)


def _tri(n, rev, strict=False):
    i = lax.broadcasted_iota(jnp.int32, (n, n), 0)
    j = lax.broadcasted_iota(jnp.int32, (n, n), 1)
    if strict:
        keep = (j > i) if rev else (j < i)
    else:
        keep = (j >= i) if rev else (j <= i)
    return keep


def _f_gla_chunk(q, k, v, la, st, head_k, head_v, bd_t, rev):
    keep = _tri(GLA_CHUNK, rev)
    b = _nn_hi(keep.astype(F32), la)
    bl = jnp.sum(la, axis=0, keepdims=True)
    qd = q * (GLA_DK ** -0.5) * jnp.exp(b)
    ki = k * jnp.exp(-b)
    kd = k * jnp.exp(bl - b)
    q_heads = (qd[None, :, :] * head_k[:, None, :]).reshape(GLA_HEADS * GLA_CHUNK, GKW)
    a_all = _nt(q_heads, ki).reshape(GLA_HEADS, GLA_CHUNK, GLA_CHUNK)
    a_all = jnp.where(keep[None, :, :], a_all, 0.0).reshape(GLA_HEADS * GLA_CHUNK, GLA_CHUNK)
    o_all = _nn(a_all, v).reshape(GLA_HEADS, GLA_CHUNK, GVW)
    intra = jnp.sum(o_all * head_v[:, None, :], axis=0)
    inter = _nt(qd, st)
    st_new = st * jnp.exp(bl) + bd_t * _tn(v, kd)
    return intra + inter, st_new


def _gla_specs(s, tb, order):
    return [pl.BlockSpec((tb, GKW), lambda i: (order(i), C_GQ // GKW)),
            pl.BlockSpec((tb, GKW), lambda i: (order(i), C_GK // GKW)),
            pl.BlockSpec((tb, GVW), lambda i: (order(i), C_GV // GVW)),
            pl.BlockSpec((tb, GKW), lambda i: (order(i), 0))]


GLA_BLOCK_CHUNKS = 4


def _gla_fwd(name, p, la, st0, rev):
    s = p.shape[0]
    tb = GLA_BLOCK_CHUNKS * GLA_CHUNK
    nblk = s // tb
    order = (lambda i: nblk - 1 - i) if rev else (lambda i: i)
    masks = _gla_masks()

    def body(q_ref, k_ref, v_ref, la_ref, st0_ref, hk_ref, hv_ref, bd_ref, o_ref, sts_ref, st_ref):
        @pl.when(pl.program_id(0) == 0)
        def _():
            st_ref[...] = st0_ref[...]

        st = st_ref[...]
        sts_ref[0] = st
        chunks = range(GLA_BLOCK_CHUNKS)
        for ci in (reversed(chunks) if rev else chunks):
            rows = slice(ci * GLA_CHUNK, (ci + 1) * GLA_CHUNK)
            o, st = _f_gla_chunk(q_ref[rows, :], k_ref[rows, :], v_ref[rows, :], la_ref[rows, :], st,
                                 hk_ref[...], hv_ref[...], bd_ref[...], rev)
            o_ref[rows, :] = o
        st_ref[...] = st

    full = lambda a: pl.BlockSpec(a.shape, lambda i: (0,) * a.ndim)
    return pl.pallas_call(
        body, name=name, grid=(nblk,),
        in_specs=_gla_specs(s, tb, order) + [full(st0)] + [full(m) for m in masks],
        out_specs=[pl.BlockSpec((tb, GVW), lambda i: (order(i), 0)),
                   pl.BlockSpec((1, GVW, GKW), lambda i: (order(i), 0, 0))],
        out_shape=[jax.ShapeDtypeStruct((s, GVW), F32), jax.ShapeDtypeStruct((nblk, GVW, GKW), F32)],
        scratch_shapes=[pltpu.VMEM((GVW, GKW), F32)],
        compiler_params=_cp("arbitrary"),
    )(p, p, p, la, st0, *masks)


def _gla_bwd(name, p, la, sts, do, prev, rev, after=None):
    s = p.shape[0]
    tb = GLA_BLOCK_CHUNKS * GLA_CHUNK
    nblk = s // tb
    order = (lambda i: i) if rev else (lambda i: nblk - 1 - i)
    masks = _gla_masks()
    n_prev = 0 if prev is None else 3
    follow = () if after is None else (after,)

    def body(*refs):
        q_ref, k_ref, v_ref, la_ref, sts_ref, do_ref, hk_ref, hv_ref, bd_ref = refs[:9]
        prev_refs = refs[9:9 + n_prev]
        dq_ref, dk_ref, dv_ref, dla_ref, dst0_ref, dst_ref = refs[9 + n_prev + len(follow):]

        @pl.when(pl.program_id(0) == 0)
        def _():
            dst_ref[...] = jnp.zeros_like(dst_ref)

        def block(q, k, v, la, st):
            outs = [None] * GLA_BLOCK_CHUNKS
            chunks = range(GLA_BLOCK_CHUNKS)
            for ci in (reversed(chunks) if rev else chunks):
                rows = slice(ci * GLA_CHUNK, (ci + 1) * GLA_CHUNK)
                outs[ci], st = _f_gla_chunk(q[ci], k[ci], v[ci], la[ci], st, hk_ref[...], hv_ref[...], bd_ref[...],
                                            rev)
            return tuple(outs), st

        split = lambda r: tuple(r[ci * GLA_CHUNK:(ci + 1) * GLA_CHUNK, :].astype(F32)
                                for ci in range(GLA_BLOCK_CHUNKS))
        _, vjp = jax.vjp(block, split(q_ref), split(k_ref), split(v_ref), split(la_ref), sts_ref[0])
        dq, dk, dv, dla, dst = vjp((split(do_ref), dst_ref[...]))
        for ci in range(GLA_BLOCK_CHUNKS):
            rows = slice(ci * GLA_CHUNK, (ci + 1) * GLA_CHUNK)
            if n_prev:
                dq_ref[rows, :] = dq[ci] + prev_refs[0][rows, :]
                dk_ref[rows, :] = dk[ci] + prev_refs[1][rows, :]
                dv_ref[rows, :] = dv[ci] + prev_refs[2][rows, :]
            else:
                dq_ref[rows, :], dk_ref[rows, :], dv_ref[rows, :] = dq[ci], dk[ci], dv[ci]
            dla_ref[rows, :] = dla[ci]
        dst_ref[...] = dst
        dst0_ref[...] = dst

    full = lambda a: pl.BlockSpec(a.shape, lambda i: (0,) * a.ndim)
    blk = lambda w: pl.BlockSpec((tb, w), lambda i: (order(i), 0))
    prev_specs = [blk(GKW), blk(GKW), blk(GVW)] if n_prev else []
    return pl.pallas_call(
        body, name=name, grid=(nblk,),
        in_specs=_gla_specs(s, tb, order) + [pl.BlockSpec((1, GVW, GKW), lambda i: (order(i), 0, 0)), blk(GVW)]
        + [full(m) for m in masks] + prev_specs + [pl.BlockSpec(memory_space=pl.ANY)] * len(follow),
        out_specs=[blk(GKW), blk(GKW), blk(GVW), blk(GKW), pl.BlockSpec((GVW, GKW), lambda i: (0, 0))],
        out_shape=[jax.ShapeDtypeStruct((s, GKW), F32), jax.ShapeDtypeStruct((s, GKW), F32),
                   jax.ShapeDtypeStruct((s, GVW), F32), jax.ShapeDtypeStruct((s, GKW), F32),
                   jax.ShapeDtypeStruct((GVW, GKW), F32)],
        scratch_shapes=[pltpu.VMEM((GVW, GKW), F32)],
        compiler_params=_cp("arbitrary"),
    )(p, p, p, la, sts, do, *masks, *(prev or ()), *follow)


def _f_ctx_state(k, v, la_f, la_b, bd_t):
    c = k.shape[0]
    after = _nn_hi(_tri(c, True, strict=True).astype(F32), la_f)
    before = _nn_hi(_tri(c, False, strict=True).astype(F32), la_b)
    return bd_t * _tn(v, k * jnp.exp(after)), bd_t * _tn(v, k * jnp.exp(before))


def _ctx_state(pc, la_f, la_b):
    c = pc.shape[0]
    bd_t = _gla_masks()[2]

    def body(k_ref, v_ref, lf_ref, lb_ref, bd_ref, sf_ref, sb_ref):
        sf_ref[...], sb_ref[...] = _f_ctx_state(k_ref[...], v_ref[...], lf_ref[...], lb_ref[...], bd_ref[...])

    full = lambda a: pl.BlockSpec(a.shape, lambda i: (0, 0))
    return pl.pallas_call(
        body, name="ctx_state_fwd", grid=(1,),
        in_specs=[pl.BlockSpec((c, GKW), lambda i: (0, C_GK // GKW)), pl.BlockSpec((c, GVW), lambda i: (0, C_GV // GVW)),
                  full(la_f), full(la_b), full(bd_t)],
        out_specs=[pl.BlockSpec((GVW, GKW), lambda i: (0, 0))] * 2,
        out_shape=[jax.ShapeDtypeStruct((GVW, GKW), F32)] * 2,
        compiler_params=_cp("arbitrary"),
    )(pc, pc, la_f, la_b, bd_t)


def _ctx_state_bwd(pc, la_f, la_b, dsf, dsb):
    c = pc.shape[0]
    bd_t = _gla_masks()[2]

    def body(k_ref, v_ref, lf_ref, lb_ref, bd_ref, dsf_ref, dsb_ref, dk_ref, dv_ref, dlf_ref, dlb_ref):
        _, vjp = jax.vjp(lambda k, v, lf, lb: _f_ctx_state(k, v, lf, lb, bd_ref[...]),
                         k_ref[...], v_ref[...], lf_ref[...], lb_ref[...])
        dk, dv, dlf, dlb = vjp((dsf_ref[...], dsb_ref[...]))
        dk_ref[...], dv_ref[...] = dk.astype(BF16), dv.astype(BF16)
        dlf_ref[...], dlb_ref[...] = dlf, dlb

    full = lambda a: pl.BlockSpec(a.shape, lambda i: (0, 0))
    return pl.pallas_call(
        body, name="ctx_state_bwd", grid=(1,),
        in_specs=[pl.BlockSpec((c, GKW), lambda i: (0, C_GK // GKW)), pl.BlockSpec((c, GVW), lambda i: (0, C_GV // GVW)),
                  full(la_f), full(la_b), full(bd_t), full(dsf), full(dsb)],
        out_specs=[pl.BlockSpec((c, GKW), lambda i: (0, 0)), pl.BlockSpec((c, GVW), lambda i: (0, 0)),
                   pl.BlockSpec((c, GKW), lambda i: (0, 0)), pl.BlockSpec((c, GKW), lambda i: (0, 0))],
        out_shape=[jax.ShapeDtypeStruct((c, GKW), BF16), jax.ShapeDtypeStruct((c, GVW), BF16),
                   jax.ShapeDtypeStruct((c, GKW), F32), jax.ShapeDtypeStruct((c, GKW), F32)],
        compiler_params=_cp("arbitrary"),
    )(pc, pc, la_f, la_b, bd_t, dsf, dsb)


_SRC_COLS = ((0, QW), (QW + 2 * KVW + 2 * GKW, GVW), (QW + 2 * KVW + 2 * GKW + GVW, GVW), (QW, KVW), (QW + KVW, KVW),
             (QW + 2 * KVW, GKW), (QW + 2 * KVW + GKW, GKW), (IN_COLS - 2 * GATE_RANK, 2 * GATE_RANK))
_DST_COLS = (C_Q, C_GV, C_GG, C_K, C_V, C_GQ, C_GK, C_Z)


def _pack_w_in(w_in):
    parts = [w_in[:, s:s + n] for s, n in _SRC_COLS]
    parts.append(jnp.zeros((w_in.shape[0], IN_PAD - C_Z - 2 * GATE_RANK), w_in.dtype))
    return jnp.concatenate(parts, axis=1)


def _unpack_w_in_grad(g):
    by_src = sorted(zip(_SRC_COLS, _DST_COLS))
    return jnp.concatenate([g[:, d:d + n] for (_, n), d in by_src], axis=1)


def _prep_weights(w_in, w_gate_fwd, w_gate_bwd):
    pad_rows = lambda w, at: jnp.zeros((LANES, GKW), F32).at[at:at + GATE_RANK].set(w)
    return {"w_in": _pack_w_in(w_in).astype(BF16), "wg_f": pad_rows(w_gate_fwd, 0),
            "wg_b": pad_rows(w_gate_bwd, GATE_RANK)}


def _local_step(x, ctx, target, ada, ada_c, w, late_weights, reduce_behind=None):
    s, d = x.shape
    sh1, sc1, gt1, sh2, sc2, gt2 = [ada[:, i * d:(i + 1) * d] for i in range(6)]
    sh1c, sc1c = ada_c[:, :d], ada_c[:, d:2 * d]
    cos, sin = _rope_tables(s)
    gt = jnp.tile(w["g_gla_norm"], (1, GLA_HEADS))

    h = _norm_mod("pre_mix", x, w["g_pre_mix"], sh1, sc1)
    hc = _norm_mod("pre_mix_ctx", ctx, w["g_pre_mix"], sh1c, sc1c)
    p = _mm("proj_in", h, w["w_in"], "nn")
    pc = _mm("proj_in_ctx", hc, w["w_in"], "nn")
    q_rot, k_rot, v_b = _rope_fwd("rope", p, cos, sin)
    pad = ((BLOCK, BLOCK), (0, 0))
    kp, vp = jnp.pad(k_rot, pad), jnp.pad(v_b, pad)
    kc, vc = pc[:, C_K:C_K + KVW].astype(BF16), pc[:, C_V:C_V + KVW].astype(BF16)
    attn = _attn_fwd(q_rot, kp, vp, kc, vc, w["attn_sink"])
    gate_w = (w["wg_f"], w["wg_b"], w["b_gate_fwd"], w["b_gate_bwd"])
    la_f, la_b = _gate_fwd("gate", p, *gate_w)
    la_fc, la_bc = _gate_fwd("gate_ctx", pc, *gate_w)
    st_f0, st_b0 = _ctx_state(pc, la_fc, la_bc)
    o_f, sts_f = _gla_fwd("gla_fwd_f", p, la_f, st_f0, False)
    o_b, sts_b = _gla_fwd("gla_fwd_b", p, la_b, st_b0, True)
    mix = _gla_out("gla_out", attn, o_f, o_b, p, gt)
    w_out, w_ffn_in_t, w_ffn_out = late_weights(attn)
    y = _mm("proj_out", mix, w_out, "nn")
    x1, h2 = _post_res_norm_mod("post_mix_pre_ffn", x, y, w["g_post_mix"], gt1, w["g_pre_ffn"], sh2, sc2)
    u = _mm("ffn_in", h2, w_ffn_in_t, "nt", BF16)
    a = _swiglu("swiglu", u)
    f = _mm("ffn_out", a, w_ffn_out, "nn")
    g = {}
    dx2, df, loss, g["g_post_ffn"], dgt2 = _post_res_loss("post_ffn_loss", x1, f, w["g_post_ffn"], gt2, target)

    da = _mm("ffn_out_dx", df, w_ffn_out, "nt", BF16)
    g["w_ffn_out"] = _mm("ffn_out_dw", a, df, "tn")
    du = _swiglu_bwd("swiglu_bwd", da, u)
    dh2 = _mm("ffn_in_dx", du, w_ffn_in_t, "nn")
    g["w_ffn_in_t"] = _mm("ffn_in_dw", du, h2, "tn")
    dx1, dy, g["g_pre_ffn"], dsh2, dsc2, g["g_post_mix"], dgt1 = _norm_mod_post_res_bwd(
        "pre_ffn_post_mix_bwd", dh2, dx2, x1, y, w["g_pre_ffn"], sh2, sc2, w["g_post_mix"], gt1)
    dmix = _mm("proj_out_dx", dy, w_out, "nt", BF16)
    g["w_out"] = _mm("proj_out_dw", mix, dy, "tn")
    rb, sink, token = reduce_behind, w["attn_sink"], None
    if rb is not None:
        gt = _behind(gt, rb.start(g["w_out"], g["w_ffn_out"], g["w_ffn_in_t"]))
    d_o, dgg, dgt = _gla_out_bwd("gla_out_bwd", dmix, o_f, o_b, p, gt)
    g["g_gla_norm"] = jnp.sum(dgt.reshape(GLA_HEADS, GLA_DV), axis=0, keepdims=True)
    if rb is not None:
        token = rb.pair(dgg)
    dgq, dgk, dgv, dla_f, dst_f0 = _gla_bwd("gla_bwd_f", p, la_f, sts_f, d_o, None, False, token)
    dgq, dgk, dgv, dla_b, dst_b0 = _gla_bwd("gla_bwd_b", p, la_b, sts_b, d_o, (dgq, dgk, dgv), True)
    if rb is not None:
        sink = _behind(sink, rb.total(dgq))
    dgkc, dgvc, dla_fc, dla_bc = _ctx_state_bwd(pc, la_fc, la_bc, dst_f0, dst_b0)
    dz, dwf, dwb, dbf, dbb = _gate_bwd("gate_bwd", p, dla_f, dla_b, *gate_w)
    dzc, dwfc, dwbc, dbfc, dbbc = _gate_bwd("gate_ctx_bwd", pc, dla_fc, dla_bc, *gate_w)
    g["w_gate_fwd"] = (dwf + dwfc)[:GATE_RANK]
    g["w_gate_bwd"] = (dwb + dwbc)[GATE_RANK:2 * GATE_RANK]
    g["b_gate_fwd"], g["b_gate_bwd"] = dbf + dbfc, dbb + dbbc
    dq_rot, dkp, dvp, dkc, dvc, g["attn_sink"] = _attn_bwd(dmix, q_rot, kp, vp, kc, vc, sink)
    if rb is not None:
        g["behind"] = rb.result(dq_rot)
    dq, dk = _rope_bwd("rope_bwd", dq_rot, dkp[BLOCK:BLOCK + s], cos, sin)
    dp = jnp.concatenate([dq, dgv.astype(BF16), dgg, dk, dvp[BLOCK:BLOCK + s].astype(BF16), dgq.astype(BF16),
                          dgk.astype(BF16), dz], axis=1)
    c_rows = ctx.shape[0]
    zeros = lambda n: jnp.zeros((c_rows, n), BF16)
    dpc = jnp.concatenate([zeros(QW), dgvc, zeros(GVW), dkc.astype(BF16), dvc.astype(BF16), zeros(GKW), dgkc, dzc],
                          axis=1)
    dh = _mm("proj_in_dx", dp, w["w_in"], "nt")
    dhc = _mm("proj_in_ctx_dx", dpc, w["w_in"], "nt")
    g["w_in"] = _mm("proj_in_dw", h, dp, "tn", init=_mm("proj_in_ctx_dw", hc, dpc, "tn"))
    dx, dg_a, dsh1, dsc1 = _norm_mod_bwd("pre_mix_bwd", dh, dx1, x, w["g_pre_mix"], sh1, sc1)
    _, dg_b, dsh1c, dsc1c = _norm_mod_bwd("pre_mix_ctx_bwd", dhc, jnp.zeros_like(dhc), ctx, w["g_pre_mix"], sh1c,
                                          sc1c)
    g["g_pre_mix"] = dg_a + dg_b
    d_ada = jnp.concatenate([dsh1, dsc1, dgt1, dsh2, dsc2, dgt2], axis=1)
    d_ada_c = jnp.concatenate([dsh1c, dsc1c, jnp.zeros((1, 4 * d), F32)], axis=1)
    return loss, dx, g, d_ada, d_ada_c


HBM = pl.BlockSpec(memory_space=pltpu.HBM)
N_DEV, N_CHIP = 8, 4


def _place():
    x, y, c = lax.axis_index("x"), lax.axis_index("y"), lax.axis_index("c")
    return x, y, c, [(1 - x, y), (x, 1 - y), (1 - x, 1 - y)]


def _row_tile(n, mult, cap):
    return max(t for t in range(mult, min(n, cap) + 1, mult) if n % t == 0)


def _ag_small(name, v, after=None):
    follow = () if after is None else (after,)

    def body(v_ref, *rest):
        out_ref, send_sems, recv_sems = rest[len(follow):]
        x, y, c, _ = _place()
        out_ref[4 * x + 2 * y + c] = v_ref[...]

        def peer(r):
            return ((1 - x) if r & 4 else x, (1 - y) if r & 2 else y, (1 - c) if r & 1 else c)

        def copy(r, block):
            px, py, pc = block
            return pltpu.make_async_remote_copy(
                src_ref=v_ref, dst_ref=out_ref.at[4 * px + 2 * py + pc], send_sem=send_sems.at[r - 1],
                recv_sem=recv_sems.at[r - 1], device_id=peer(r), device_id_type=MESH)

        sends = [copy(r, (x, y, c)) for r in range(1, N_DEV)]
        for cp in sends:
            cp.start()
        for r in range(1, N_DEV):
            copy(r, peer(r)).wait_recv()
        for cp in sends:
            cp.wait_send()

    return pl.pallas_call(
        body, name=name, out_shape=jax.ShapeDtypeStruct((N_DEV,) + v.shape, v.dtype),
        in_specs=[pl.BlockSpec(memory_space=pltpu.VMEM)] + [pl.BlockSpec(memory_space=pl.ANY)] * len(follow),
        out_specs=pl.BlockSpec(memory_space=pltpu.VMEM),
        scratch_shapes=[pltpu.SemaphoreType.DMA((N_DEV - 1,)), pltpu.SemaphoreType.DMA((N_DEV - 1,))],
    )(v, *follow)


def _halves(c, rows, mult):
    hr = rows // 2
    return pl.ds(pl.multiple_of(c * hr, mult), hr), pl.ds(pl.multiple_of((1 - c) * hr, mult), hr)


def _ag_shards(name, shard):
    rows = shard.shape[0]

    def body(w_ref, out_ref, send_sems, recv_sems, local_sem):
        x, y, c, chips = _place()
        mine_half, other_half = _halves(c, rows, 16)
        me = 2 * x + y
        mine = pltpu.make_async_copy(w_ref, out_ref.at[me], local_sem)
        mine.start()

        def copy(k, src, chip, half, to):
            return pltpu.make_async_remote_copy(
                src_ref=src, dst_ref=out_ref.at[chip, half], send_sem=send_sems.at[k], recv_sem=recv_sems.at[k],
                device_id=to, device_id_type=MESH)

        first = [copy(j, w_ref.at[mine_half], me, mine_half, (px, py, c)) for j, (px, py) in enumerate(chips)]
        for cp in first:
            cp.start()
        passed = []
        for j, (px, py) in enumerate(chips):
            pk = 2 * px + py
            copy(j, w_ref.at[mine_half], pk, mine_half, (px, py, c)).wait_recv()
            cp = copy(3 + j, out_ref.at[pk, mine_half], pk, mine_half, (x, y, 1 - c))
            cp.start()
            passed.append(cp)
        for j, (px, py) in enumerate(chips):
            copy(3 + j, w_ref.at[mine_half], 2 * px + py, other_half, (x, y, 1 - c)).wait_recv()
        for cp in first + passed:
            cp.wait_send()
        mine.wait()

    return pl.pallas_call(
        body, name=name, out_shape=jax.ShapeDtypeStruct((N_CHIP,) + shard.shape, shard.dtype),
        in_specs=[HBM], out_specs=HBM,
        scratch_shapes=[pltpu.SemaphoreType.DMA((6,)), pltpu.SemaphoreType.DMA((6,)), pltpu.SemaphoreType.DMA],
    )(shard)


def _swap_half(name, g):
    n_sh, rows, n = g.shape

    def body(g_ref, a_ref, send_sem, recv_sem):
        x, y, c, _ = _place()
        _, other_half = _halves(c, rows, 8)
        cp = pltpu.make_async_remote_copy(
            src_ref=g_ref.at[pl.ds(0, n_sh), other_half], dst_ref=a_ref, send_sem=send_sem, recv_sem=recv_sem,
            device_id=(x, y, 1 - c), device_id_type=MESH)
        cp.start()
        cp.wait()

    return pl.pallas_call(
        body, name=name, out_shape=jax.ShapeDtypeStruct((n_sh, rows // 2, n), g.dtype), in_specs=[HBM], out_specs=HBM,
        scratch_shapes=[pltpu.SemaphoreType.DMA, pltpu.SemaphoreType.DMA],
    )(g)


def _add_half(name, g, a, c_idx):
    n_sh, hr, n = a.shape
    tr = _row_tile(hr, 16, 1024)
    nb = hr // tr

    def body(c_ref, g_ref, a_ref, o_ref):
        o_ref[...] = (g_ref[...] + a_ref[...]).astype(o_ref.dtype)

    return pl.pallas_call(
        body, name=name, out_shape=jax.ShapeDtypeStruct(a.shape, BF16),
        grid_spec=pltpu.PrefetchScalarGridSpec(
            num_scalar_prefetch=1, grid=(n_sh, nb),
            in_specs=[pl.BlockSpec((1, tr, n), lambda s, i, c_ref: (s, c_ref[0] * nb + i, 0)),
                      pl.BlockSpec((1, tr, n), lambda s, i, c_ref: (s, i, 0))],
            out_specs=pl.BlockSpec((1, tr, n), lambda s, i, c_ref: (s, i, 0))),
        compiler_params=_cp("parallel", "parallel"),
    )(c_idx, g, a)


def _scatter_chips(name, h):
    def body(h_ref, b_ref, send_sems, recv_sems, local_sem):
        x, y, c, chips = _place()
        me = 2 * x + y
        mine = pltpu.make_async_copy(h_ref.at[me], b_ref.at[me], local_sem)
        mine.start()

        def copy(j, src_block, dst_block, to):
            return pltpu.make_async_remote_copy(
                src_ref=h_ref.at[src_block], dst_ref=b_ref.at[dst_block], send_sem=send_sems.at[j],
                recv_sem=recv_sems.at[j], device_id=to, device_id_type=MESH)

        sends = [copy(j, 2 * px + py, me, (px, py, c)) for j, (px, py) in enumerate(chips)]
        for cp in sends:
            cp.start()
        for j, (px, py) in enumerate(chips):
            copy(j, me, 2 * px + py, (px, py, c)).wait_recv()
        for cp in sends:
            cp.wait_send()
        mine.wait()

    return pl.pallas_call(
        body, name=name, out_shape=jax.ShapeDtypeStruct(h.shape, h.dtype), in_specs=[HBM], out_specs=HBM,
        scratch_shapes=[pltpu.SemaphoreType.DMA((3,)), pltpu.SemaphoreType.DMA((3,)), pltpu.SemaphoreType.DMA],
    )(h)


def _sum_chips(name, b):
    n_sh, hr, n = b.shape
    tr = _row_tile(hr, 16, 1024)

    def body(b0, b1, b2, b3, o_ref):
        o_ref[...] = ((b0[0].astype(F32) + b1[0].astype(F32)) + b2[0].astype(F32)) + b3[0].astype(F32)

    return pl.pallas_call(
        body, name=name, grid=(hr // tr,), out_shape=jax.ShapeDtypeStruct((hr, n), F32),
        in_specs=[pl.BlockSpec((1, tr, n), functools.partial(lambda i, k: (k, i, 0), k=k)) for k in range(n_sh)],
        out_specs=pl.BlockSpec((tr, n), lambda i: (i, 0)), compiler_params=_cp("parallel"),
    )(b, b, b, b)


def _share_half(name, f):
    hr, n = f.shape

    def body(f_ref, out_ref, send_sem, recv_sem, local_sem):
        x, y, c, _ = _place()
        mine_half, other_half = _halves(c, 2 * hr, 8)
        mine = pltpu.make_async_copy(f_ref, out_ref.at[mine_half], local_sem)
        mine.start()

        def copy(half):
            return pltpu.make_async_remote_copy(
                src_ref=f_ref, dst_ref=out_ref.at[half], send_sem=send_sem, recv_sem=recv_sem,
                device_id=(x, y, 1 - c), device_id_type=MESH)

        send = copy(mine_half)
        send.start()
        copy(other_half).wait_recv()
        send.wait_send()
        mine.wait()

    return pl.pallas_call(
        body, name=name, out_shape=jax.ShapeDtypeStruct((2 * hr, n), f.dtype), in_specs=[HBM], out_specs=HBM,
        scratch_shapes=[pltpu.SemaphoreType.DMA, pltpu.SemaphoreType.DMA, pltpu.SemaphoreType.DMA],
    )(f)


def _reduce_shards(name, g, c_idx):
    a = _swap_half(name + "_swap", g)
    h = _add_half(name + "_pair", g, a, c_idx)
    b = _scatter_chips(name + "_scatter", h)
    f = _sum_chips(name + "_sum", b)
    return _share_half(name + "_share", f)


SEM = pl.BlockSpec(memory_space=pltpu.SEMAPHORE)
ANY = pl.BlockSpec(memory_space=pl.ANY)
DATAFLOW = pltpu.SideEffectType.DATAFLOW_SIDE_EFFECTING


def _remote(src, dst, send_sems, recv_sems, k, to):
    return pltpu.make_async_remote_copy(src_ref=src, dst_ref=dst, send_sem=send_sems.at[k], recv_sem=recv_sems.at[k],
                                        device_id=to, device_id_type=MESH)


def _split_copy(name, src, land_shape, land_dtype, n, plan, after=None):
    after = jnp.zeros((8, LANES), F32) if after is None else after

    def start_body(src_ref, land_ref, after_ref, send_sems, recv_sems, src_thru, land_thru, token):
        for cp in plan(src_ref, land_ref, send_sems, recv_sems)[0]:
            cp.start()
        token[...] = jnp.zeros_like(token)

    sems = pltpu.SemaphoreType.DMA((n,))
    send_sems, recv_sems, src_thru, land_thru, token = pl.pallas_call(
        start_body, name=name + "_start",
        out_shape=(sems, sems, pltpu.HBM(src.shape, src.dtype), pltpu.HBM(land_shape, land_dtype),
                   jax.ShapeDtypeStruct((8, LANES), F32)),
        in_specs=(HBM, HBM, ANY), out_specs=(SEM, SEM, HBM, HBM, pl.BlockSpec(memory_space=pltpu.VMEM)),
        input_output_aliases={0: 2, 1: 3}, compiler_params=pltpu.CompilerParams(has_side_effects=DATAFLOW),
    )(pltpu.with_memory_space_constraint(src, pltpu.HBM),
      pltpu.with_memory_space_constraint(lax.empty(land_shape, land_dtype), pltpu.HBM), after)

    def wait(after):
        def wait_body(src_ref, land_ref, send_sems, recv_sems, after_ref, src_out, land_out):
            sent, received = plan(src_ref, land_ref, send_sems, recv_sems)
            for cp in sent:
                cp.wait_send()
            for cp in received:
                cp.wait_recv()

        return pl.pallas_call(
            wait_body, name=name + "_wait",
            out_shape=(pltpu.HBM(src.shape, src.dtype), pltpu.HBM(land_shape, land_dtype)),
            in_specs=(HBM, HBM, SEM, SEM, ANY), out_specs=(HBM, HBM), input_output_aliases={0: 0, 1: 1},
            compiler_params=pltpu.CompilerParams(has_side_effects=DATAFLOW),
        )(src_thru, land_thru, send_sems, recv_sems, after)

    return token, wait


def _behind(x, token):
    return x + token[0, 0]


def _plan_gather(src_ref, land_ref, send_sems, recv_sems):
    x, y, c, chips = _place()
    sent = [_remote(src_ref, land_ref.at[2 * x + y], send_sems, recv_sems, j, (px, py, c))
            for j, (px, py) in enumerate(chips)]
    received = [_remote(src_ref, land_ref.at[2 * px + py], send_sems, recv_sems, j, (px, py, c))
                for j, (px, py) in enumerate(chips)]
    return sent, received


def _plan_swap(src_ref, land_ref, send_sems, recv_sems):
    x, y, c, _ = _place()
    _, other_half = _halves(c, src_ref.shape[1], 8)
    cp = _remote(src_ref.at[pl.ds(0, src_ref.shape[0]), other_half], land_ref, send_sems, recv_sems, 0, (x, y, 1 - c))
    return [cp], [cp]


def _plan_scatter(src_ref, land_ref, send_sems, recv_sems):
    x, y, c, chips = _place()
    sent = [_remote(src_ref.at[2 * px + py], land_ref.at[2 * x + y], send_sems, recv_sems, j, (px, py, c))
            for j, (px, py) in enumerate(chips)]
    received = [_remote(src_ref.at[2 * px + py], land_ref.at[2 * px + py], send_sems, recv_sems, j, (px, py, c))
                for j, (px, py) in enumerate(chips)]
    return sent, received


def _plan_share(src_ref, land_ref, send_sems, recv_sems):
    x, y, c, _ = _place()
    mine_half, other_half = _halves(c, land_ref.shape[0], 8)
    return ([_remote(src_ref, land_ref.at[mine_half], send_sems, recv_sems, 0, (x, y, 1 - c))],
            [_remote(src_ref, land_ref.at[other_half], send_sems, recv_sems, 0, (x, y, 1 - c))])


def _pack_shard_rows(name, parts):
    rows = [t.shape[0] // N_CHIP for t in parts]
    n, n_copies = parts[0].shape[1], N_CHIP * len(parts)

    def body(*refs):
        out_ref, sems = refs[len(parts)], refs[len(parts) + 1]
        copies, at = [], 0
        for i, r in enumerate(rows):
            for k in range(N_CHIP):
                copies.append(pltpu.make_async_copy(refs[i].at[pl.ds(k * r, r)], out_ref.at[k, pl.ds(at, r)],
                                                    sems.at[i * N_CHIP + k]))
            at += r
        for cp in copies:
            cp.start()
        for cp in copies:
            cp.wait()

    return pl.pallas_call(
        body, name=name, out_shape=jax.ShapeDtypeStruct((N_CHIP, sum(rows), n), parts[0].dtype),
        in_specs=[HBM] * len(parts), out_specs=HBM, scratch_shapes=[pltpu.SemaphoreType.DMA((n_copies,))],
    )(*parts)


class _GatherBehind:
    def __init__(self, name, shard, chip, after=None):
        self.chip = chip
        self.token, self.wait = _split_copy(name, shard, (N_CHIP,) + shard.shape, shard.dtype, 3, _plan_gather,
                                            after)

    def result(self, after):
        shard, land = self.wait(after)
        return lax.dynamic_update_slice(land, shard[None], (self.chip, 0, 0))


class _ReduceBehind:
    def __init__(self, name, chip, c, c_idx):
        self.name, self.chip, self.c, self.c_idx = name, chip, c, c_idx

    def start(self, *grads):
        g = _pack_shard_rows(self.name + "_pack", grads)
        n_sh, rows, n = g.shape
        token, self.wait = _split_copy(self.name + "_swap", g, (n_sh, rows // 2, n), g.dtype, 1, _plan_swap)
        return token

    def pair(self, after):
        g, a = self.wait(after)
        h = _add_half(self.name + "_pair", g, a, self.c_idx)
        token, self.wait = _split_copy(self.name + "_scatter", h, h.shape, h.dtype, 3, _plan_scatter)
        return token

    def total(self, after):
        h, b = self.wait(after)
        b = lax.dynamic_update_slice(b, lax.dynamic_slice_in_dim(h, self.chip, 1, axis=0), (self.chip, 0, 0))
        f = _sum_chips(self.name + "_sum", b)
        token, self.wait = _split_copy(self.name + "_share", f, (2 * f.shape[0], f.shape[1]), f.dtype, 1,
                                       _plan_share)
        return token

    def result(self, after):
        f, out = self.wait(after)
        return lax.dynamic_update_slice(out, f, (self.c * f.shape[0], 0))


def _f_adamw(w, g, m, v):
    m = ADAM_B1 * m + (1.0 - ADAM_B1) * g
    v = ADAM_B2 * v + (1.0 - ADAM_B2) * (g * g)
    m_hat = m / (1.0 - ADAM_B1 ** ADAM_STEP)
    v_hat = v / (1.0 - ADAM_B2 ** ADAM_STEP)
    return -ADAM_LR * (m_hat / (jnp.sqrt(v_hat) + ADAM_EPS) + ADAM_WD * w), m, v


def _adamw(name, w, g, m, v):
    rows, n = w.shape
    return _rowwise(name, lambda w, g, m, v: (_f_adamw(w, g, m, v), ()), rows, [(t, n, 0) for t in (w, g, m, v)], [],
                    [(n, F32)] * 3, [], tm=_row_tile(rows, 8, 256))


def _pack_rows(parts):
    rows = []
    for t in parts:
        t = t.reshape(-1)
        rows.append(jnp.pad(t, (0, -t.shape[0] % LANES)).reshape(-1, LANES))
    out = jnp.concatenate(rows, axis=0)
    return jnp.pad(out, ((0, -out.shape[0] % 8), (0, 0)))


def _unpack_rows(packed, shapes):
    out, r = [], 0
    for shp in shapes:
        n = int(np.prod(shp))
        nr = -(-n // LANES)
        out.append(packed[r:r + nr].reshape(-1)[:n].reshape(shp))
        r += nr
    return out


def _sum_blocks(name, g):
    def body(g_ref, o_ref):
        acc = g_ref[0]
        for k in range(1, g.shape[0]):
            acc = acc + g_ref[k]
        o_ref[...] = acc

    return pl.pallas_call(body, name=name, out_shape=jax.ShapeDtypeStruct(g.shape[1:], F32))(g)


def _silu(t):
    return t * _sigmoid(t)


def _ada_fwd(cc, w_ada):
    n = w_ada.shape[1]
    tn = _row_tile(n, LANES, 512)

    def body(cc_ref, w_ref, o_ref):
        o_ref[...] = _nn(_silu(cc_ref[...]), w_ref[...])

    return pl.pallas_call(
        body, name="ada_fwd", grid=(n // tn,), out_shape=jax.ShapeDtypeStruct((cc.shape[0], n), F32),
        in_specs=[pl.BlockSpec(cc.shape, lambda j: (0, 0)), pl.BlockSpec((w_ada.shape[0], tn), lambda j: (0, j))],
        out_specs=pl.BlockSpec((cc.shape[0], tn), lambda j: (0, j)), compiler_params=_cp("parallel"),
    )(cc, w_ada)


def _ada_bwd(cc, dm, w_ada):
    d, n = w_ada.shape
    tn = _row_tile(n, LANES, 512)

    def body(cc_ref, dm_ref, w_ref, gw_ref, ds_ref):
        @pl.when(pl.program_id(0) == 0)
        def _():
            ds_ref[...] = jnp.zeros_like(ds_ref)

        gw_ref[...] = _raw_dot("tn", _silu(cc_ref[...]), dm_ref[...], True)
        ds_ref[...] += _raw_dot("nt", dm_ref[...], w_ref[...], False)

    return pl.pallas_call(
        body, name="ada_bwd", grid=(n // tn,),
        out_shape=[jax.ShapeDtypeStruct((d, n), F32), jax.ShapeDtypeStruct(cc.shape, F32)],
        in_specs=[pl.BlockSpec(cc.shape, lambda j: (0, 0)), pl.BlockSpec((cc.shape[0], tn), lambda j: (0, j)),
                  pl.BlockSpec((d, tn), lambda j: (0, j))],
        out_specs=[pl.BlockSpec((d, tn), lambda j: (0, j)), pl.BlockSpec(cc.shape, lambda j: (0, 0))],
        compiler_params=_cp("arbitrary"),
    )(cc, dm, w_ada)


def _c_ctx_grad(parts, c_ctx):
    def body(p_ref, c_ref, o_ref):
        ds = ((p_ref[0] + p_ref[1]) + p_ref[2]) + p_ref[3]
        _, vjp = jax.vjp(_silu, c_ref[...])
        o_ref[...] = vjp(ds)[0]

    return pl.pallas_call(body, name="c_ctx_grad", out_shape=jax.ShapeDtypeStruct(c_ctx.shape, F32))(parts, c_ctx)


def kernel(x, c, ctx, c_ctx, w_ada, b_ada, g_pre_mix, g_post_mix, g_pre_ffn, g_post_ffn, w_in, attn_sink, w_gate_fwd, b_gate_fwd, w_gate_bwd, b_gate_bwd, g_gla_norm, w_out, w_ffn_in, w_ffn_out, loss_target, m_c_ctx, m_w_ada, m_b_ada, m_g_pre_mix, m_g_post_mix, m_g_pre_ffn, m_g_post_ffn, m_w_in, m_attn_sink, m_w_gate_fwd, m_b_gate_fwd, m_w_gate_bwd, m_b_gate_bwd, m_g_gla_norm, m_w_out, m_w_ffn_in, m_w_ffn_out, v_c_ctx, v_w_ada, v_b_ada, v_g_pre_mix, v_g_post_mix, v_g_pre_ffn, v_g_post_ffn, v_w_in, v_attn_sink, v_w_gate_fwd, v_b_gate_fwd, v_w_gate_bwd, v_b_gate_bwd, v_g_gla_norm, v_w_out, v_w_ffn_in, v_w_ffn_out):
    xi, yi, ci = lax.axis_index("x"), lax.axis_index("y"), lax.axis_index("c")
    dev, chip = 4 * xi + 2 * yi + ci, 2 * xi + yi
    c_idx = jnp.reshape(ci, (1,)).astype(jnp.int32)
    d = x.shape[-1]
    n_ada, n_in, n_f = w_ada.shape[-1], w_in.shape[-1], w_ffn_in.shape[-1]
    r_out, r_f = w_out.shape[1], w_ffn_out.shape[1]
    n_gate = w_gate_fwd.shape[-1]
    by_chip = lambda t: t[0::2]

    w_in_g = _ag_shards("gather_w_in", w_in[0].astype(BF16))

    rc = -(-d // LANES)
    g1 = _ag_small("gather_cond", _pack_rows([c[0], w_gate_fwd[0], w_gate_bwd[0]]), w_in_g)
    c_all = g1[:, :rc].reshape(N_DEV, -1)[:, :d]
    gr = GATE_RANK * n_gate // LANES
    gate_full = lambda off: jnp.transpose(by_chip(g1)[:, off:off + gr].reshape(N_CHIP, GATE_RANK, n_gate),
                                          (1, 0, 2)).reshape(GATE_RANK, N_CHIP * n_gate)
    wgf, wgb = gate_full(rc), gate_full(rc + gr)
    cc = jnp.concatenate([c_all, c_ctx[None, :], jnp.zeros((7, d), F32)], axis=0)

    g2 = _ag_small("gather_ada", _ada_fwd(cc, w_ada[0]).reshape(-1, LANES))
    ada_all = jnp.transpose(by_chip(g2).reshape(N_CHIP, 16, n_ada), (1, 0, 2)).reshape(16, N_CHIP * n_ada) + b_ada
    late = _GatherBehind("gather_late", jnp.concatenate(
        [w_out[0], w_ffn_out[0], jnp.transpose(w_ffn_in[0])], axis=0).astype(BF16), chip, g2)

    def late_weights(after):
        t = late.result(after)
        r1, r2 = r_out, r_out + r_f
        return (t[:, :r1].reshape(N_CHIP * r_out, d), t[:, r2:].reshape(N_CHIP * n_f, d),
                t[:, r1:r2].reshape(N_CHIP * r_f, d))

    ada_all = _behind(ada_all, late.token)
    ada = lax.dynamic_slice(ada_all, (dev, 0), (1, N_CHIP * n_ada))
    ada_c = ada_all[N_DEV:N_DEV + 1]

    w = _prep_weights(jnp.concatenate([w_in_g[k] for k in range(N_CHIP)], axis=1), wgf, wgb)
    w.update(g_pre_mix=g_pre_mix, g_post_mix=g_post_mix, g_pre_ffn=g_pre_ffn, g_post_ffn=g_post_ffn,
             attn_sink=attn_sink, b_gate_fwd=b_gate_fwd, b_gate_bwd=b_gate_bwd, g_gla_norm=g_gla_norm)

    reduce_behind = _ReduceBehind("reduce_late", chip, ci, c_idx)
    loss_lanes, grad_x, g, d_ada, d_ada_c = _local_step(x[0], ctx[0], loss_target[0], ada, ada_c, w, late_weights,
                                                        reduce_behind)

    small = ("g_pre_mix", "g_post_mix", "g_pre_ffn", "g_post_ffn", "attn_sink", "b_gate_fwd", "b_gate_bwd",
             "g_gla_norm", "w_gate_fwd", "w_gate_bwd")
    shapes = [(1, 6 * d)] * 2 + [g[n].shape for n in small]
    g3 = _ag_small("gather_small_grads", _pack_rows([d_ada, d_ada_c] + [g[n] for n in small]))
    tot = dict(zip(("d_ada", "d_ada_c") + small, _unpack_rows(_sum_blocks("sum_small_grads", g3), shapes)))
    r_ada = 6 * d // LANES
    dm = jnp.concatenate([g3[:, :r_ada].reshape(N_DEV, 6 * d), tot["d_ada_c"], jnp.zeros((7, 6 * d), F32)], axis=0)
    grads = {n: tot[n] for n in small[:8]}
    grads["b_ada"] = _sum_blocks("sum_b_ada", dm.reshape(16, r_ada, LANES)).reshape(1, 6 * d)
    grads["w_gate_fwd"] = lax.dynamic_slice(tot["w_gate_fwd"], (0, chip * n_gate), (GATE_RANK, n_gate))[None]
    grads["w_gate_bwd"] = lax.dynamic_slice(tot["w_gate_bwd"], (0, chip * n_gate), (GATE_RANK, n_gate))[None]
    gw_ada, dsc = _ada_bwd(cc, lax.dynamic_slice(dm, (0, chip * n_ada), (16, n_ada)), w_ada[0])
    grads["w_ada"] = gw_ada[None]
    g4 = _ag_small("gather_c_ctx", _pack_rows([dsc[N_DEV]]))
    grads["c_ctx"] = _c_ctx_grad(by_chip(g4), _pack_rows([c_ctx])).reshape(-1)[:d]

    g_w_in = _unpack_w_in_grad(g["w_in"])
    grads["w_in"] = _reduce_shards("reduce_w_in", jnp.stack([g_w_in[:, k * n_in:(k + 1) * n_in]
                                                             for k in range(N_CHIP)]), c_idx)[None]
    behind = g["behind"]
    grads["w_out"], grads["w_ffn_out"] = behind[None, :r_out], behind[None, r_out:r_out + r_f]
    grads["w_ffn_in"] = jnp.transpose(behind[r_out + r_f:])[None]

    names = ("c_ctx", "w_ada", "b_ada", "g_pre_mix", "g_post_mix", "g_pre_ffn", "g_post_ffn", "w_in", "attn_sink",
             "w_gate_fwd", "b_gate_fwd", "w_gate_bwd", "b_gate_bwd", "g_gla_norm", "w_out", "w_ffn_in", "w_ffn_out")
    weights = dict(zip(names, (c_ctx, w_ada, b_ada, g_pre_mix, g_post_mix, g_pre_ffn, g_post_ffn, w_in, attn_sink,
                               w_gate_fwd, b_gate_fwd, w_gate_bwd, b_gate_bwd, g_gla_norm, w_out, w_ffn_in,
                               w_ffn_out)))
    m_in = dict(zip(names, (m_c_ctx, m_w_ada, m_b_ada, m_g_pre_mix, m_g_post_mix, m_g_pre_ffn, m_g_post_ffn, m_w_in,
                            m_attn_sink, m_w_gate_fwd, m_b_gate_fwd, m_w_gate_bwd, m_b_gate_bwd, m_g_gla_norm,
                            m_w_out, m_w_ffn_in, m_w_ffn_out)))
    v_in = dict(zip(names, (v_c_ctx, v_w_ada, v_b_ada, v_g_pre_mix, v_g_post_mix, v_g_pre_ffn, v_g_post_ffn, v_w_in,
                            v_attn_sink, v_w_gate_fwd, v_b_gate_fwd, v_w_gate_bwd, v_b_gate_bwd, v_g_gla_norm,
                            v_w_out, v_w_ffn_in, v_w_ffn_out)))
    large = ("w_ada", "w_in", "w_out", "w_ffn_in", "w_ffn_out")
    tiny = tuple(n for n in names if n not in large)
    delta, new_m, new_v = {}, {}, {}
    for n in large:
        dl, nm, nv = _adamw("adamw_" + n, weights[n][0], grads[n][0], m_in[n][0], v_in[n][0])
        delta[n], new_m[n], new_v[n] = dl[None], nm[None], nv[None]
    tiny_shapes = [weights[n].shape for n in tiny]
    packed = [_pack_rows([t[n] for n in tiny]) for t in (weights, grads, m_in, v_in)]
    for out, res in zip((delta, new_m, new_v), _adamw("adamw_small", *packed)):
        out.update(zip(tiny, _unpack_rows(res, tiny_shapes)))
    for n in tiny:
        grads[n] = grads[n].reshape(weights[n].shape)

    loss = lax.psum(loss_lanes[0, 0], ("x", "y", "c"))
    return (loss, grad_x[None], *[grads[n] for n in names], *[delta[n] for n in names], *[new_m[n] for n in names],
            *[new_v[n] for n in names])
```

```python
import functools

import jax
import jax.numpy as jnp
import numpy as np
from jax import lax
from jax.experimental import pallas as pl
from jax.experimental.pallas import tpu as pltpu

F32 = jnp.float32
BF16 = jnp.bfloat16
MESH = pl.DeviceIdType.MESH

HEAD_DIM = 64
ATT_HEADS = 8
ATT_KV_HEADS = 2
ATT_GROUP = ATT_HEADS // ATT_KV_HEADS
WINDOW = 128
BLOCK = 128
GRID_W = 64
ROPE_BASE = 10000.0
GLA_HEADS = 8
GLA_DK = 32
GLA_DV = 64
GLA_CHUNK = 64
GATE_RANK = 16
GATE_TAU = 16.0
NEG_INF = -1e30
QW = ATT_HEADS * HEAD_DIM
KVW = ATT_KV_HEADS * HEAD_DIM
GKW = GLA_HEADS * GLA_DK
GVW = GLA_HEADS * GLA_DV
IN_COLS = QW + 2 * KVW + 2 * GKW + 2 * GVW + 2 * GATE_RANK
LANES = 128
IN_PAD = IN_COLS + LANES - 2 * GATE_RANK
C_Q, C_GV, C_GG = 0, QW, QW + GVW
C_K = C_GG + GVW
C_V = C_K + KVW
C_GQ = C_V + KVW
C_GK = C_GQ + GKW
C_Z = C_GK + GKW
MIX = QW + GVW

ADAM_LR, ADAM_B1, ADAM_B2, ADAM_EPS, ADAM_WD, ADAM_STEP = 0.001, 0.9, 0.999, 1e-08, 0.01, 10

VMEM_LIMIT = 56 * 1024 * 1024


def _cp(*sem):
    return pltpu.CompilerParams(dimension_semantics=sem, vmem_limit_bytes=VMEM_LIMIT)


def _pick(n, cands):
    for t in cands:
        if n % t == 0:
            return t
    return n


_DIMS = {"nn": (((1,), (0,)), ((), ())), "nt": (((1,), (1,)), ((), ())), "tn": (((0,), (0,)), ((), ()))}


def _raw_dot(mode, a, b, hi):
    if hi:
        return lax.dot_general(a.astype(F32), b.astype(F32), _DIMS[mode], precision=lax.Precision.HIGHEST,
                               preferred_element_type=F32)
    return lax.dot_general(a.astype(BF16), b.astype(BF16), _DIMS[mode], preferred_element_type=F32)


def _make_dot(mode, hi):
    @jax.custom_vjp
    def dot(a, b):
        return _raw_dot(mode, a, b, hi)

    def fwd(a, b):
        return _raw_dot(mode, a, b, hi), (a, b)

    def bwd(res, dc):
        a, b = res
        if mode == "nn":
            return _raw_dot("nt", dc, b, hi), _raw_dot("tn", a, dc, hi)
        if mode == "nt":
            return _raw_dot("nn", dc, b, hi), _raw_dot("tn", dc, a, hi)
        return _raw_dot("nt", b, dc, hi), _raw_dot("nn", a, dc, hi)

    dot.defvjp(fwd, bwd)
    return dot


_nn, _nt, _tn = _make_dot("nn", False), _make_dot("nt", False), _make_dot("tn", False)
_nn_hi = _make_dot("nn", True)


MM_VMEM_BUDGET = 44 * 1024 * 1024


def _halvings(n):
    out = [n]
    while out[-1] % (2 * LANES) == 0:
        out.append(out[-1] // 2)
    return out


def _mm_tiles(mode, m, n, k, a_bytes, b_bytes, o_bytes, init_bytes=0):
    tms = [t for t in dict.fromkeys((m, m // 2, m // 4, 2048, 1024, 512, 256, 128))
           if m % t == 0 and t % (LANES if mode == "tn" else 16) == 0 and t <= 4096] or [m]
    if mode == "tn":
        fits = [(k // tk + 0.5 * (m // tm), tm, tk)
                for tk in (4096, 2048, 1024, 512, 256, 128) if k % tk == 0 for tm in tms
                if 2 * (tk * tm * a_bytes + tk * n * b_bytes + tm * n * (o_bytes + init_bytes)) <= MM_VMEM_BUDGET]
        if fits:
            _, tm, tk = min(fits)
            return tm, n, tk
    tks = ([t for t in (512, 256, 128) if k % t == 0] or [k]) if mode == "tn" else _halvings(k)
    for tn in _halvings(n):
        for tk in tks:
            for tm in tms:
                acc = tm * tn * 4 if (k // tk > 1 and o_bytes != 4) else 0
                tiles = tm * tk * a_bytes + tk * tn * b_bytes + tm * tn * (o_bytes + init_bytes)
                if 2 * tiles + acc <= MM_VMEM_BUDGET:
                    return tm, tn, tk
    return tms[-1], _halvings(n)[-1], tks[-1]


def _mm(name, a, b, mode, out_dtype=F32, init=None):
    if mode == "nn":
        (m, k), n = a.shape, b.shape[1]
    elif mode == "nt":
        (m, k), n = a.shape, b.shape[0]
    else:
        (k, m), n = a.shape, b.shape[1]
    tm, tn, tk = _mm_tiles(mode, m, n, k, a.dtype.itemsize, b.dtype.itemsize, jnp.dtype(out_dtype).itemsize,
                           0 if init is None else 4)
    nk = k // tk
    use_acc = nk > 1 and out_dtype != F32

    inits = () if init is None else (init,)

    def body(a_ref, b_ref, *rest):
        o_ref, acc = rest[len(inits)], rest[len(inits) + 1:]
        part = _raw_dot(mode, a_ref[...], b_ref[...], False)
        first = lambda: part + rest[0][...] if inits else part
        if nk == 1:
            o_ref[...] = first().astype(o_ref.dtype)
            return
        acc_ref = acc[0] if use_acc else o_ref
        kk = pl.program_id(2)

        @pl.when(kk == 0)
        def _():
            acc_ref[...] = first()

        @pl.when(kk > 0)
        def _():
            acc_ref[...] += part

        if use_acc:
            @pl.when(kk == nk - 1)
            def _():
                o_ref[...] = acc_ref[...].astype(o_ref.dtype)

    if mode == "nn":
        a_spec = pl.BlockSpec((tm, tk), lambda i, j, kk: (i, kk))
        b_spec = pl.BlockSpec((tk, tn), lambda i, j, kk: (kk, j))
    elif mode == "nt":
        a_spec = pl.BlockSpec((tm, tk), lambda i, j, kk: (i, kk))
        b_spec = pl.BlockSpec((tn, tk), lambda i, j, kk: (j, kk))
    else:
        a_spec = pl.BlockSpec((tk, tm), lambda i, j, kk: (kk, i))
        b_spec = pl.BlockSpec((tk, tn), lambda i, j, kk: (kk, j))
    return pl.pallas_call(
        body, name=name, grid=(m // tm, n // tn, nk),
        in_specs=[a_spec, b_spec] + [pl.BlockSpec((tm, tn), lambda i, j, kk: (i, j))] * len(inits),
        out_specs=pl.BlockSpec((tm, tn), lambda i, j, kk: (i, j)),
        out_shape=jax.ShapeDtypeStruct((m, n), out_dtype),
        scratch_shapes=[pltpu.VMEM((tm, tn), F32)] if use_acc else [],
        compiler_params=_cp("parallel", "parallel", "arbitrary"),
    )(a, b, *inits)


def _rowwise(name, fn, rows, row_ins, full_ins, row_outs, acc_outs, tm=None):
    tm = tm or _pick(rows, (512, 256, 128))
    n_r, n_f, n_o, n_a = len(row_ins), len(full_ins), len(row_outs), len(acc_outs)

    def body(*refs):
        ins, outs = refs[:n_r + n_f], refs[n_r + n_f:]
        vals = [r[...].astype(F32) for r in ins]
        ro, ao = fn(*vals)
        for r, val in zip(outs[:n_o], ro):
            r[...] = val.astype(r.dtype)
        if n_a:
            @pl.when(pl.program_id(0) == 0)
            def _():
                for r in outs[n_o:]:
                    r[...] = jnp.zeros_like(r)

            for r, val in zip(outs[n_o:], ao):
                r[...] += val

    in_specs = [pl.BlockSpec((tm, w), functools.partial(lambda i, cb: (i, cb), cb=cb)) for _, w, cb in row_ins]
    in_specs += [pl.BlockSpec(a.shape, lambda i: (0, 0)) for a in full_ins]
    out_specs = [pl.BlockSpec((tm, w), lambda i: (i, 0)) for w, _ in row_outs]
    out_specs += [pl.BlockSpec(s, lambda i: (0, 0)) for s in acc_outs]
    out_shape = [jax.ShapeDtypeStruct((rows, w), dt) for w, dt in row_outs]
    out_shape += [jax.ShapeDtypeStruct(s, F32) for s in acc_outs]
    return pl.pallas_call(
        body, name=name, grid=(rows // tm,), in_specs=in_specs, out_specs=out_specs, out_shape=out_shape,
        compiler_params=_cp("arbitrary" if n_a else "parallel"),
    )(*[a for a, _, _ in row_ins], *full_ins)


def _rn(x):
    return x * lax.rsqrt(jnp.mean(x * x, axis=-1, keepdims=True) + 1e-6)


def _sigmoid(t):
    return 1.0 / (1.0 + jnp.exp(-t))


def _f_norm_mod(x, g, sh, sc):
    return _rn(x) * g * (1.0 + sc) + sh


def _f_post_res(xr, y, g, gate):
    return xr + gate * (_rn(y) * g)


def _f_swiglu(g, u):
    return g * _sigmoid(g) * u


def _logsig(u):
    return jnp.minimum(u, 0.0) - jnp.log(1.0 + jnp.exp(-jnp.abs(u)))


def _f_gate(z, wf, wb, bf, bb):
    return _logsig(_nn(z, wf) + bf) / GATE_TAU, _logsig(_nn(z, wb) + bb) / GATE_TAU


def _f_gla_out(of, ob, gg, gt, bd):
    o = of + ob
    ms = _nn_hi(o * o, bd)
    return o * lax.rsqrt(ms + 1e-6) * gt * (gg * _sigmoid(gg))


def _norm_mod(name, x, g, sh, sc):
    rows, d = x.shape
    return _rowwise(name, lambda x, g, sh, sc: ((_f_norm_mod(x, g, sh, sc),), ()), rows,
                    [(x, d, 0)], [g, sh, sc], [(d, BF16)], [])[0]


def _norm_mod_bwd(name, dh, dres, x, g, sh, sc):
    rows, d = x.shape

    def fn(dh, dres, x, g, sh, sc):
        _, vjp = jax.vjp(_f_norm_mod, x, g, sh, sc)
        dx, dg, dsh, dsc = vjp(dh)
        return (dx + dres,), (dg, dsh, dsc)

    return _rowwise(name, fn, rows, [(dh, d, 0), (dres, d, 0), (x, d, 0)], [g, sh, sc], [(d, F32)],
                    [(1, d)] * 3)


def _post_res(name, xr, y, g, gate):
    rows, d = xr.shape
    return _rowwise(name, lambda xr, y, g, gate: ((_f_post_res(xr, y, g, gate),), ()), rows,
                    [(xr, d, 0), (y, d, 0)], [g, gate], [(d, F32)], [])[0]


def _post_res_bwd(name, dxo, y, g, gate):
    rows, d = y.shape

    def fn(dxo, y, g, gate):
        _, vjp = jax.vjp(lambda y, g, gate: _f_post_res(jnp.zeros_like(y), y, g, gate), y, g, gate)
        dy, dg, dgate = vjp(dxo)
        return (dy,), (dg, dgate)

    return _rowwise(name, fn, rows, [(dxo, d, 0), (y, d, 0)], [g, gate], [(d, BF16)], [(1, d)] * 2)


def _post_res_norm_mod(name, xr, y, g_post, gate, g_pre, sh, sc):
    rows, d = xr.shape

    def fn(xr, y, g_post, gate, g_pre, sh, sc):
        x1 = _f_post_res(xr, y, g_post, gate)
        return (x1, _f_norm_mod(x1, g_pre, sh, sc)), ()

    return _rowwise(name, fn, rows, [(xr, d, 0), (y, d, 0)], [g_post, gate, g_pre, sh, sc], [(d, F32), (d, BF16)], [])


def _norm_mod_post_res_bwd(name, dh, dres, x1, y, g_pre, sh, sc, g_post, gate):
    rows, d = x1.shape

    def fn(dh, dres, x1, y, g_pre, sh, sc, g_post, gate):
        _, vjp_norm = jax.vjp(_f_norm_mod, x1, g_pre, sh, sc)
        dx1, dg_pre, dsh, dsc = vjp_norm(dh)
        dx1 = dx1 + dres
        _, vjp_res = jax.vjp(lambda y, g, gate: _f_post_res(jnp.zeros_like(y), y, g, gate), y, g_post, gate)
        dy, dg_post, dgate = vjp_res(dx1)
        return (dx1, dy), (dg_pre, dsh, dsc, dg_post, dgate)

    return _rowwise(name, fn, rows, [(dh, d, 0), (dres, d, 0), (x1, d, 0), (y, d, 0)], [g_pre, sh, sc, g_post, gate],
                    [(d, F32), (d, BF16)], [(1, d)] * 5, tm=_pick(rows, (256, 128)))


def _post_res_loss(name, xr, y, g, gate, target):
    rows, d = xr.shape

    def fn(xr, y, target, g, gate):
        x2, vjp = jax.vjp(lambda y, g, gate: _f_post_res(xr, y, g, gate), y, g, gate)
        diff = x2 - target
        part = 0.5 * jnp.sum(jnp.mean(diff * diff, axis=-1, keepdims=True), axis=0, keepdims=True)
        dx2 = diff * (1.0 / d)
        dy, dg, dgate = vjp(dx2)
        return (dx2, dy), (jnp.broadcast_to(part, (1, LANES)), dg, dgate)

    return _rowwise(name, fn, rows, [(xr, d, 0), (y, d, 0), (target, d, 0)], [g, gate], [(d, F32), (d, BF16)],
                    [(1, LANES), (1, d), (1, d)])


def _swiglu(name, u):
    rows, f2 = u.shape
    f = f2 // 2
    return _rowwise(name, lambda g, u: ((_f_swiglu(g, u),), ()), rows, [(u, f, 0), (u, f, 1)], [], [(f, BF16)], [],
                    tm=_pick(rows, (512, 256, 128)))[0]


def _swiglu_bwd(name, da, u):
    rows, f2 = u.shape
    f = f2 // 2

    def fn(da, g, u):
        _, vjp = jax.vjp(_f_swiglu, g, u)
        return (jnp.concatenate(vjp(da), axis=1),), ()

    return _rowwise(name, fn, rows, [(da, f, 0), (u, f, 0), (u, f, 1)], [], [(f2, BF16)], [],
                    tm=_pick(rows, (512, 256, 128)))[0]


def _gate_fwd(name, p, wf, wb, bf, bb):
    rows = p.shape[0]
    return _rowwise(name, lambda z, wf, wb, bf, bb: (_f_gate(z, wf, wb, bf, bb), ()), rows,
                    [(p, LANES, C_Z // LANES)], [wf, wb, bf, bb], [(GKW, F32)] * 2, [])


def _gate_bwd(name, p, dla_f, dla_b, wf, wb, bf, bb):
    rows = p.shape[0]

    def fn(z, dlf, dlb, wf, wb, bf, bb):
        _, vjp = jax.vjp(_f_gate, z, wf, wb, bf, bb)
        dz, dwf, dwb, dbf, dbb = vjp((dlf, dlb))
        return (dz,), (dwf, dwb, dbf, dbb)

    return _rowwise(name, fn, rows, [(p, LANES, C_Z // LANES), (dla_f, GKW, 0), (dla_b, GKW, 0)],
                    [wf, wb, bf, bb], [(LANES, BF16)], [(LANES, GKW), (LANES, GKW), (1, GKW), (1, GKW)])


def _head_mean_matrix():
    h = np.arange(GVW) // GLA_DV
    return jnp.asarray((h[:, None] == h[None, :]).astype(np.float32) / GLA_DV)


def _gla_out(name, attn, of, ob, p, gt):
    rows = of.shape[0]
    bd = _head_mean_matrix()
    fn = lambda attn, of, ob, gg, gt, bd: ((jnp.concatenate([attn, _f_gla_out(of, ob, gg, gt, bd)], axis=1),), ())
    return _rowwise(name, fn, rows, [(attn, QW, 0), (of, GVW, 0), (ob, GVW, 0), (p, GVW, C_GG // GVW)], [gt, bd],
                    [(MIX, BF16)], [])[0]


def _gla_out_bwd(name, dmix, of, ob, p, gt):
    rows = of.shape[0]
    bd = _head_mean_matrix()

    def fn(dm, of, ob, gg, gt, bd):
        _, vjp = jax.vjp(lambda of, gg, gt: _f_gla_out(of, ob, gg, gt, bd), of, gg, gt)
        do, dgg, dgt = vjp(dm)
        return (do, dgg), (dgt,)

    return _rowwise(name, fn, rows, [(dmix, GVW, 1), (of, GVW, 0), (ob, GVW, 0), (p, GVW, C_GG // GVW)], [gt, bd],
                    [(GVW, F32), (GVW, BF16)], [(1, GVW)])


def _rope_tables(n_tokens):
    t = jnp.arange(n_tokens)
    row = (t // GRID_W).astype(F32)
    col = (t % GRID_W).astype(F32)
    half = HEAD_DIM // 2
    inv_freq = ROPE_BASE ** (-jnp.arange(0, half, 2, dtype=F32) / half)
    ang_r = row[:, None] * inv_freq[None, :]
    ang_c = col[:, None] * inv_freq[None, :]
    ang = jnp.concatenate([ang_r, ang_r, ang_c, ang_c], axis=-1)
    sign = jnp.concatenate([-jnp.ones((16,), F32), jnp.ones((16,), F32)] * 2)
    cos, sin = jnp.cos(ang), jnp.sin(ang) * sign[None, :]
    return jnp.tile(cos, (1, 2)), jnp.tile(sin, (1, 2))


def _rot_pairs(x):
    w = x.shape[-1]
    lane = lax.broadcasted_iota(jnp.int32, x.shape, x.ndim - 1)
    return jnp.where((lane % 32) < 16, pltpu.roll(x, w - 16, x.ndim - 1), pltpu.roll(x, 16, x.ndim - 1))


def _rope_apply(x, cos, sin_signed, inverse):
    reps = x.shape[-1] // LANES
    cos = jnp.concatenate([cos] * reps, axis=-1) if reps > 1 else cos
    sin = jnp.concatenate([sin_signed] * reps, axis=-1) if reps > 1 else sin_signed
    if inverse:
        return x * cos + _rot_pairs(x * sin)
    return x * cos + _rot_pairs(x) * sin


def _rope_fwd(name, p, cos, sin):
    rows = p.shape[0]

    def fn(q, k, v, cos, sin):
        return (_rope_apply(q, cos, sin, False), _rope_apply(k, cos, sin, False), v), ()

    return _rowwise(name, fn, rows, [(p, QW, 0), (p, KVW, C_K // KVW), (p, KVW, C_V // KVW), (cos, LANES, 0),
                                     (sin, LANES, 0)], [], [(QW, BF16), (KVW, BF16), (KVW, BF16)], [])


def _rope_bwd(name, dq, dk, cos, sin):
    rows = dq.shape[0]

    def fn(dq, dk, cos, sin):
        return (_rope_apply(dq, cos, sin, True), _rope_apply(dk, cos, sin, True)), ()

    return _rowwise(name, fn, rows, [(dq, QW, 0), (dk, KVW, 0), (cos, LANES, 0), (sin, LANES, 0)], [],
                    [(QW, BF16), (KVW, BF16)], [])


GROUP_ROWS = ATT_GROUP * BLOCK


def _f_attn(qs, kws, vws, kcs, vcs, sink, n, n_tokens):
    row = lax.broadcasted_iota(jnp.int32, (GROUP_ROWS, 1), 0)
    group = sum((row >= g * BLOCK).astype(jnp.int32) for g in range(1, ATT_GROUP))
    i = lax.broadcasted_iota(jnp.int32, (GROUP_ROWS, 3 * BLOCK), 0) - BLOCK * group
    j = lax.broadcasted_iota(jnp.int32, (GROUP_ROWS, 3 * BLOCK), 1)
    kpos = (n - 1) * BLOCK + j
    mask = (jnp.abs(j - BLOCK - i) <= WINDOW) & (kpos >= 0) & (kpos < n_tokens)
    head_id = lax.broadcasted_iota(jnp.int32, (1, ATT_HEADS), 1)
    scale = HEAD_DIM ** -0.5
    outs = []
    for h in range(ATT_KV_HEADS):
        sk = jnp.zeros((GROUP_ROWS, 1), F32)
        for g in range(ATT_GROUP):
            one = jnp.sum(jnp.where(head_id == h * ATT_GROUP + g, sink, 0.0), axis=-1, keepdims=True)
            sk = jnp.where(group == g, one, sk)
        q = qs[h] * scale
        s_w = jnp.where(mask, _nt(q, kws[h]), NEG_INF)
        s_c = _nt(q, kcs[h])
        m = lax.stop_gradient(jnp.maximum(jnp.maximum(jnp.max(s_w, axis=-1, keepdims=True),
                                                      jnp.max(s_c, axis=-1, keepdims=True)), sk))
        pw, pc = jnp.exp(s_w - m), jnp.exp(s_c - m)
        den = jnp.sum(pw, axis=-1, keepdims=True) + jnp.sum(pc, axis=-1, keepdims=True) + jnp.exp(sk - m)
        outs.append((_nn(pw, vws[h]) + _nn(pc, vcs[h])) / den)
    return tuple(outs)


def _group_rows(ref, h):
    hs = lambda hq: slice(hq * HEAD_DIM, (hq + 1) * HEAD_DIM)
    return jnp.concatenate([ref[:, hs(h * ATT_GROUP + g)].astype(F32) for g in range(ATT_GROUP)], axis=0)


def _ungroup_rows(ref, h, val):
    for g in range(ATT_GROUP):
        hq = h * ATT_GROUP + g
        ref[:, hq * HEAD_DIM:(hq + 1) * HEAD_DIM] = val[g * BLOCK:(g + 1) * BLOCK].astype(ref.dtype)


def _attn_loads(n, q_ref, kp_ref, vp_ref, kc_ref, vc_ref):
    r0 = pl.multiple_of(n * BLOCK, BLOCK)
    hs = lambda h: slice(h * HEAD_DIM, (h + 1) * HEAD_DIM)
    qs = [_group_rows(q_ref, h) for h in range(ATT_KV_HEADS)]
    kws = [kp_ref[pl.ds(r0, 3 * BLOCK), hs(h)].astype(F32) for h in range(ATT_KV_HEADS)]
    vws = [vp_ref[pl.ds(r0, 3 * BLOCK), hs(h)].astype(F32) for h in range(ATT_KV_HEADS)]
    kcs = [kc_ref[:, hs(h)].astype(F32) for h in range(ATT_KV_HEADS)]
    vcs = [vc_ref[:, hs(h)].astype(F32) for h in range(ATT_KV_HEADS)]
    return r0, hs, qs, kws, vws, kcs, vcs


def _attn_specs(s, c):
    full = lambda shape: pl.BlockSpec(shape, lambda n: (0, 0))
    return [pl.BlockSpec((BLOCK, QW), lambda n: (n, 0)), full((s + 2 * BLOCK, KVW)), full((s + 2 * BLOCK, KVW)),
            full((c, KVW)), full((c, KVW)), full((1, ATT_HEADS))]


def _attn_fwd(q, kp, vp, kc, vc, sink):
    s, c = q.shape[0], kc.shape[0]

    def body(q_ref, kp_ref, vp_ref, kc_ref, vc_ref, sink_ref, o_ref):
        n = pl.program_id(0)
        _, hs, qs, kws, vws, kcs, vcs = _attn_loads(n, q_ref, kp_ref, vp_ref, kc_ref, vc_ref)
        outs = _f_attn(qs, kws, vws, kcs, vcs, sink_ref[...], n, s)
        for h in range(ATT_KV_HEADS):
            _ungroup_rows(o_ref, h, outs[h])

    return pl.pallas_call(
        body, name="attn_fwd", grid=(s // BLOCK,), in_specs=_attn_specs(s, c),
        out_specs=pl.BlockSpec((BLOCK, QW), lambda n: (n, 0)), out_shape=jax.ShapeDtypeStruct((s, QW), BF16),
        compiler_params=_cp("parallel"),
    )(q, kp, vp, kc, vc, sink)


def _attn_bwd(do, q, kp, vp, kc, vc, sink):
    s, c = q.shape[0], kc.shape[0]

    def body(do_ref, q_ref, kp_ref, vp_ref, kc_ref, vc_ref, sink_ref, dq_ref, dkp_ref, dvp_ref, dkc_ref, dvc_ref,
             dsink_ref):
        n = pl.program_id(0)

        @pl.when(n == 0)
        def _():
            for r in (dkp_ref, dvp_ref, dkc_ref, dvc_ref, dsink_ref):
                r[...] = jnp.zeros_like(r)

        r0, hs, qs, kws, vws, kcs, vcs = _attn_loads(n, q_ref, kp_ref, vp_ref, kc_ref, vc_ref)
        _, vjp = jax.vjp(lambda qs, kws, vws, kcs, vcs, sink: _f_attn(qs, kws, vws, kcs, vcs, sink, n, s),
                         qs, kws, vws, kcs, vcs, sink_ref[...])
        dqs, dkws, dvws, dkcs, dvcs, dsink = vjp(tuple(_group_rows(do_ref, h) for h in range(ATT_KV_HEADS)))
        for h in range(ATT_KV_HEADS):
            _ungroup_rows(dq_ref, h, dqs[h])
            dkp_ref[pl.ds(r0, 3 * BLOCK), hs(h)] += dkws[h]
            dvp_ref[pl.ds(r0, 3 * BLOCK), hs(h)] += dvws[h]
            dkc_ref[:, hs(h)] += dkcs[h]
            dvc_ref[:, hs(h)] += dvcs[h]
        dsink_ref[...] += dsink

    full = lambda shape: pl.BlockSpec(shape, lambda n: (0, 0))
    return pl.pallas_call(
        body, name="attn_bwd", grid=(s // BLOCK,),
        in_specs=[pl.BlockSpec((BLOCK, QW), lambda n: (n, 0))] + _attn_specs(s, c),
        out_specs=[pl.BlockSpec((BLOCK, QW), lambda n: (n, 0)), full((s + 2 * BLOCK, KVW)), full((s + 2 * BLOCK, KVW)),
                   full((c, KVW)), full((c, KVW)), full((1, ATT_HEADS))],
        out_shape=[jax.ShapeDtypeStruct((s, QW), F32), jax.ShapeDtypeStruct((s + 2 * BLOCK, KVW), F32),
                   jax.ShapeDtypeStruct((s + 2 * BLOCK, KVW), F32), jax.ShapeDtypeStruct((c, KVW), F32),
                   jax.ShapeDtypeStruct((c, KVW), F32), jax.ShapeDtypeStruct((1, ATT_HEADS), F32)],
        compiler_params=_cp("arbitrary"),
    )(do, q, kp, vp, kc, vc, sink)


def _gla_masks():
    hk = np.arange(GKW) // GLA_DK
    hv = np.arange(GVW) // GLA_DV
    head_k = (np.arange(GLA_HEADS)[:, None] == hk[None, :]).astype(np.float32)
    head_v = (np.arange(GLA_HEADS)[:, None] == hv[None, :]).astype(np.float32)
    bd_t = (hv[:, None] == hk[None, :]).astype(np.float32)
    return jnp.asarray(head_k), jnp.asarray(head_v), jnp.asarray(bd_t)


def _tri(n, rev, strict=False):
    i = lax.broadcasted_iota(jnp.int32, (n, n), 0)
    j = lax.broadcasted_iota(jnp.int32, (n, n), 1)
    if strict:
        keep = (j > i) if rev else (j < i)
    else:
        keep = (j >= i) if rev else (j <= i)
    return keep


def _f_gla_chunk(q, k, v, la, st, head_k, head_v, bd_t, rev):
    keep = _tri(GLA_CHUNK, rev)
    b = _nn_hi(keep.astype(F32), la)
    bl = jnp.sum(la, axis=0, keepdims=True)
    qd = q * (GLA_DK ** -0.5) * jnp.exp(b)
    ki = k * jnp.exp(-b)
    kd = k * jnp.exp(bl - b)
    q_heads = (qd[None, :, :] * head_k[:, None, :]).reshape(GLA_HEADS * GLA_CHUNK, GKW)
    a_all = _nt(q_heads, ki).reshape(GLA_HEADS, GLA_CHUNK, GLA_CHUNK)
    a_all = jnp.where(keep[None, :, :], a_all, 0.0).reshape(GLA_HEADS * GLA_CHUNK, GLA_CHUNK)
    o_all = _nn(a_all, v).reshape(GLA_HEADS, GLA_CHUNK, GVW)
    intra = jnp.sum(o_all * head_v[:, None, :], axis=0)
    inter = _nt(qd, st)
    st_new = st * jnp.exp(bl) + bd_t * _tn(v, kd)
    return intra + inter, st_new


def _gla_specs(s, tb, order):
    return [pl.BlockSpec((tb, GKW), lambda i: (order(i), C_GQ // GKW)),
            pl.BlockSpec((tb, GKW), lambda i: (order(i), C_GK // GKW)),
            pl.BlockSpec((tb, GVW), lambda i: (order(i), C_GV // GVW)),
            pl.BlockSpec((tb, GKW), lambda i: (order(i), 0))]


GLA_BLOCK_CHUNKS = 4


def _gla_fwd(name, p, la, st0, rev):
    s = p.shape[0]
    tb = GLA_BLOCK_CHUNKS * GLA_CHUNK
    nblk = s // tb
    order = (lambda i: nblk - 1 - i) if rev else (lambda i: i)
    masks = _gla_masks()

    def body(q_ref, k_ref, v_ref, la_ref, st0_ref, hk_ref, hv_ref, bd_ref, o_ref, sts_ref, st_ref):
        @pl.when(pl.program_id(0) == 0)
        def _():
            st_ref[...] = st0_ref[...]

        st = st_ref[...]
        sts_ref[0] = st
        chunks = range(GLA_BLOCK_CHUNKS)
        for ci in (reversed(chunks) if rev else chunks):
            rows = slice(ci * GLA_CHUNK, (ci + 1) * GLA_CHUNK)
            o, st = _f_gla_chunk(q_ref[rows, :], k_ref[rows, :], v_ref[rows, :], la_ref[rows, :], st,
                                 hk_ref[...], hv_ref[...], bd_ref[...], rev)
            o_ref[rows, :] = o
        st_ref[...] = st

    full = lambda a: pl.BlockSpec(a.shape, lambda i: (0,) * a.ndim)
    return pl.pallas_call(
        body, name=name, grid=(nblk,),
        in_specs=_gla_specs(s, tb, order) + [full(st0)] + [full(m) for m in masks],
        out_specs=[pl.BlockSpec((tb, GVW), lambda i: (order(i), 0)),
                   pl.BlockSpec((1, GVW, GKW), lambda i: (order(i), 0, 0))],
        out_shape=[jax.ShapeDtypeStruct((s, GVW), F32), jax.ShapeDtypeStruct((nblk, GVW, GKW), F32)],
        scratch_shapes=[pltpu.VMEM((GVW, GKW), F32)],
        compiler_params=_cp("arbitrary"),
    )(p, p, p, la, st0, *masks)


def _gla_bwd(name, p, la, sts, do, prev, rev, after=None):
    s = p.shape[0]
    tb = GLA_BLOCK_CHUNKS * GLA_CHUNK
    nblk = s // tb
    order = (lambda i: i) if rev else (lambda i: nblk - 1 - i)
    masks = _gla_masks()
    n_prev = 0 if prev is None else 3
    follow = () if after is None else (after,)

    def body(*refs):
        q_ref, k_ref, v_ref, la_ref, sts_ref, do_ref, hk_ref, hv_ref, bd_ref = refs[:9]
        prev_refs = refs[9:9 + n_prev]
        dq_ref, dk_ref, dv_ref, dla_ref, dst0_ref, dst_ref = refs[9 + n_prev + len(follow):]

        @pl.when(pl.program_id(0) == 0)
        def _():
            dst_ref[...] = jnp.zeros_like(dst_ref)

        def block(q, k, v, la, st):
            outs = [None] * GLA_BLOCK_CHUNKS
            chunks = range(GLA_BLOCK_CHUNKS)
            for ci in (reversed(chunks) if rev else chunks):
                rows = slice(ci * GLA_CHUNK, (ci + 1) * GLA_CHUNK)
                outs[ci], st = _f_gla_chunk(q[ci], k[ci], v[ci], la[ci], st, hk_ref[...], hv_ref[...], bd_ref[...],
                                            rev)
            return tuple(outs), st

        split = lambda r: tuple(r[ci * GLA_CHUNK:(ci + 1) * GLA_CHUNK, :].astype(F32)
                                for ci in range(GLA_BLOCK_CHUNKS))
        _, vjp = jax.vjp(block, split(q_ref), split(k_ref), split(v_ref), split(la_ref), sts_ref[0])
        dq, dk, dv, dla, dst = vjp((split(do_ref), dst_ref[...]))
        for ci in range(GLA_BLOCK_CHUNKS):
            rows = slice(ci * GLA_CHUNK, (ci + 1) * GLA_CHUNK)
            if n_prev:
                dq_ref[rows, :] = dq[ci] + prev_refs[0][rows, :]
                dk_ref[rows, :] = dk[ci] + prev_refs[1][rows, :]
                dv_ref[rows, :] = dv[ci] + prev_refs[2][rows, :]
            else:
                dq_ref[rows, :], dk_ref[rows, :], dv_ref[rows, :] = dq[ci], dk[ci], dv[ci]
            dla_ref[rows, :] = dla[ci]
        dst_ref[...] = dst
        dst0_ref[...] = dst

    full = lambda a: pl.BlockSpec(a.shape, lambda i: (0,) * a.ndim)
    blk = lambda w: pl.BlockSpec((tb, w), lambda i: (order(i), 0))
    prev_specs = [blk(GKW), blk(GKW), blk(GVW)] if n_prev else []
    return pl.pallas_call(
        body, name=name, grid=(nblk,),
        in_specs=_gla_specs(s, tb, order) + [pl.BlockSpec((1, GVW, GKW), lambda i: (order(i), 0, 0)), blk(GVW)]
        + [full(m) for m in masks] + prev_specs + [pl.BlockSpec(memory_space=pl.ANY)] * len(follow),
        out_specs=[blk(GKW), blk(GKW), blk(GVW), blk(GKW), pl.BlockSpec((GVW, GKW), lambda i: (0, 0))],
        out_shape=[jax.ShapeDtypeStruct((s, GKW), F32), jax.ShapeDtypeStruct((s, GKW), F32),
                   jax.ShapeDtypeStruct((s, GVW), F32), jax.ShapeDtypeStruct((s, GKW), F32),
                   jax.ShapeDtypeStruct((GVW, GKW), F32)],
        scratch_shapes=[pltpu.VMEM((GVW, GKW), F32)],
        compiler_params=_cp("arbitrary"),
    )(p, p, p, la, sts, do, *masks, *(prev or ()), *follow)


def _f_ctx_state(k, v, la_f, la_b, bd_t):
    c = k.shape[0]
    after = _nn_hi(_tri(c, True, strict=True).astype(F32), la_f)
    before = _nn_hi(_tri(c, False, strict=True).astype(F32), la_b)
    return bd_t * _tn(v, k * jnp.exp(after)), bd_t * _tn(v, k * jnp.exp(before))


def _ctx_state(pc, la_f, la_b):
    c = pc.shape[0]
    bd_t = _gla_masks()[2]

    def body(k_ref, v_ref, lf_ref, lb_ref, bd_ref, sf_ref, sb_ref):
        sf_ref[...], sb_ref[...] = _f_ctx_state(k_ref[...], v_ref[...], lf_ref[...], lb_ref[...], bd_ref[...])

    full = lambda a: pl.BlockSpec(a.shape, lambda i: (0, 0))
    return pl.pallas_call(
        body, name="ctx_state_fwd", grid=(1,),
        in_specs=[pl.BlockSpec((c, GKW), lambda i: (0, C_GK // GKW)), pl.BlockSpec((c, GVW), lambda i: (0, C_GV // GVW)),
                  full(la_f), full(la_b), full(bd_t)],
        out_specs=[pl.BlockSpec((GVW, GKW), lambda i: (0, 0))] * 2,
        out_shape=[jax.ShapeDtypeStruct((GVW, GKW), F32)] * 2,
        compiler_params=_cp("arbitrary"),
    )(pc, pc, la_f, la_b, bd_t)


def _ctx_state_bwd(pc, la_f, la_b, dsf, dsb):
    c = pc.shape[0]
    bd_t = _gla_masks()[2]

    def body(k_ref, v_ref, lf_ref, lb_ref, bd_ref, dsf_ref, dsb_ref, dk_ref, dv_ref, dlf_ref, dlb_ref):
        _, vjp = jax.vjp(lambda k, v, lf, lb: _f_ctx_state(k, v, lf, lb, bd_ref[...]),
                         k_ref[...], v_ref[...], lf_ref[...], lb_ref[...])
        dk, dv, dlf, dlb = vjp((dsf_ref[...], dsb_ref[...]))
        dk_ref[...], dv_ref[...] = dk.astype(BF16), dv.astype(BF16)
        dlf_ref[...], dlb_ref[...] = dlf, dlb

    full = lambda a: pl.BlockSpec(a.shape, lambda i: (0, 0))
    return pl.pallas_call(
        body, name="ctx_state_bwd", grid=(1,),
        in_specs=[pl.BlockSpec((c, GKW), lambda i: (0, C_GK // GKW)), pl.BlockSpec((c, GVW), lambda i: (0, C_GV // GVW)),
                  full(la_f), full(la_b), full(bd_t), full(dsf), full(dsb)],
        out_specs=[pl.BlockSpec((c, GKW), lambda i: (0, 0)), pl.BlockSpec((c, GVW), lambda i: (0, 0)),
                   pl.BlockSpec((c, GKW), lambda i: (0, 0)), pl.BlockSpec((c, GKW), lambda i: (0, 0))],
        out_shape=[jax.ShapeDtypeStruct((c, GKW), BF16), jax.ShapeDtypeStruct((c, GVW), BF16),
                   jax.ShapeDtypeStruct((c, GKW), F32), jax.ShapeDtypeStruct((c, GKW), F32)],
        compiler_params=_cp("arbitrary"),
    )(pc, pc, la_f, la_b, bd_t, dsf, dsb)


_SRC_COLS = ((0, QW), (QW + 2 * KVW + 2 * GKW, GVW), (QW + 2 * KVW + 2 * GKW + GVW, GVW), (QW, KVW), (QW + KVW, KVW),
             (QW + 2 * KVW, GKW), (QW + 2 * KVW + GKW, GKW), (IN_COLS - 2 * GATE_RANK, 2 * GATE_RANK))
_DST_COLS = (C_Q, C_GV, C_GG, C_K, C_V, C_GQ, C_GK, C_Z)


def _pack_w_in(w_in):
    parts = [w_in[:, s:s + n] for s, n in _SRC_COLS]
    parts.append(jnp.zeros((w_in.shape[0], IN_PAD - C_Z - 2 * GATE_RANK), w_in.dtype))
    return jnp.concatenate(parts, axis=1)


def _unpack_w_in_grad(g):
    by_src = sorted(zip(_SRC_COLS, _DST_COLS))
    return jnp.concatenate([g[:, d:d + n] for (_, n), d in by_src], axis=1)


def _prep_weights(w_in, w_gate_fwd, w_gate_bwd):
    pad_rows = lambda w, at: jnp.zeros((LANES, GKW), F32).at[at:at + GATE_RANK].set(w)
    return {"w_in": _pack_w_in(w_in).astype(BF16), "wg_f": pad_rows(w_gate_fwd, 0),
            "wg_b": pad_rows(w_gate_bwd, GATE_RANK)}


def _local_step(x, ctx, target, ada, ada_c, w, late_weights, reduce_behind=None):
    s, d = x.shape
    sh1, sc1, gt1, sh2, sc2, gt2 = [ada[:, i * d:(i + 1) * d] for i in range(6)]
    sh1c, sc1c = ada_c[:, :d], ada_c[:, d:2 * d]
    cos, sin = _rope_tables(s)
    gt = jnp.tile(w["g_gla_norm"], (1, GLA_HEADS))

    h = _norm_mod("pre_mix", x, w["g_pre_mix"], sh1, sc1)
    hc = _norm_mod("pre_mix_ctx", ctx, w["g_pre_mix"], sh1c, sc1c)
    p = _mm("proj_in", h, w["w_in"], "nn")
    pc = _mm("proj_in_ctx", hc, w["w_in"], "nn")
    q_rot, k_rot, v_b = _rope_fwd("rope", p, cos, sin)
    pad = ((BLOCK, BLOCK), (0, 0))
    kp, vp = jnp.pad(k_rot, pad), jnp.pad(v_b, pad)
    kc, vc = pc[:, C_K:C_K + KVW].astype(BF16), pc[:, C_V:C_V + KVW].astype(BF16)
    attn = _attn_fwd(q_rot, kp, vp, kc, vc, w["attn_sink"])
    gate_w = (w["wg_f"], w["wg_b"], w["b_gate_fwd"], w["b_gate_bwd"])
    la_f, la_b = _gate_fwd("gate", p, *gate_w)
    la_fc, la_bc = _gate_fwd("gate_ctx", pc, *gate_w)
    st_f0, st_b0 = _ctx_state(pc, la_fc, la_bc)
    o_f, sts_f = _gla_fwd("gla_fwd_f", p, la_f, st_f0, False)
    o_b, sts_b = _gla_fwd("gla_fwd_b", p, la_b, st_b0, True)
    mix = _gla_out("gla_out", attn, o_f, o_b, p, gt)
    w_out, w_ffn_in_t, w_ffn_out = late_weights(attn)
    y = _mm("proj_out", mix, w_out, "nn")
    x1, h2 = _post_res_norm_mod("post_mix_pre_ffn", x, y, w["g_post_mix"], gt1, w["g_pre_ffn"], sh2, sc2)
    u = _mm("ffn_in", h2, w_ffn_in_t, "nt", BF16)
    a = _swiglu("swiglu", u)
    f = _mm("ffn_out", a, w_ffn_out, "nn")
    g = {}
    dx2, df, loss, g["g_post_ffn"], dgt2 = _post_res_loss("post_ffn_loss", x1, f, w["g_post_ffn"], gt2, target)

    da = _mm("ffn_out_dx", df, w_ffn_out, "nt", BF16)
    g["w_ffn_out"] = _mm("ffn_out_dw", a, df, "tn")
    du = _swiglu_bwd("swiglu_bwd", da, u)
    dh2 = _mm("ffn_in_dx", du, w_ffn_in_t, "nn")
    g["w_ffn_in_t"] = _mm("ffn_in_dw", du, h2, "tn")
    dx1, dy, g["g_pre_ffn"], dsh2, dsc2, g["g_post_mix"], dgt1 = _norm_mod_post_res_bwd(
        "pre_ffn_post_mix_bwd", dh2, dx2, x1, y, w["g_pre_ffn"], sh2, sc2, w["g_post_mix"], gt1)
    dmix = _mm("proj_out_dx", dy, w_out, "nt", BF16)
    g["w_out"] = _mm("proj_out_dw", mix, dy, "tn")
    rb, sink, token = reduce_behind, w["attn_sink"], None
    if rb is not None:
        gt = _behind(gt, rb.start(g["w_ffn_in_t"], g["w_ffn_out"], g["w_out"]))
    d_o, dgg, dgt = _gla_out_bwd("gla_out_bwd", dmix, o_f, o_b, p, gt)
    g["g_gla_norm"] = jnp.sum(dgt.reshape(GLA_HEADS, GLA_DV), axis=0, keepdims=True)
    if rb is not None:
        token = rb.pair(dgg)
    dgq, dgk, dgv, dla_f, dst_f0 = _gla_bwd("gla_bwd_f", p, la_f, sts_f, d_o, None, False, token)
    dgq, dgk, dgv, dla_b, dst_b0 = _gla_bwd("gla_bwd_b", p, la_b, sts_b, d_o, (dgq, dgk, dgv), True)
    if rb is not None:
        sink = _behind(sink, rb.total(dgq))
    dgkc, dgvc, dla_fc, dla_bc = _ctx_state_bwd(pc, la_fc, la_bc, dst_f0, dst_b0)
    dz, dwf, dwb, dbf, dbb = _gate_bwd("gate_bwd", p, dla_f, dla_b, *gate_w)
    dzc, dwfc, dwbc, dbfc, dbbc = _gate_bwd("gate_ctx_bwd", pc, dla_fc, dla_bc, *gate_w)
    g["w_gate_fwd"] = (dwf + dwfc)[:GATE_RANK]
    g["w_gate_bwd"] = (dwb + dwbc)[GATE_RANK:2 * GATE_RANK]
    g["b_gate_fwd"], g["b_gate_bwd"] = dbf + dbfc, dbb + dbbc
    dq_rot, dkp, dvp, dkc, dvc, g["attn_sink"] = _attn_bwd(dmix, q_rot, kp, vp, kc, vc, sink)
    if rb is not None:
        g["behind"] = rb.result(dq_rot)
    dq, dk = _rope_bwd("rope_bwd", dq_rot, dkp[BLOCK:BLOCK + s], cos, sin)
    dp = jnp.concatenate([dq, dgv.astype(BF16), dgg, dk, dvp[BLOCK:BLOCK + s].astype(BF16), dgq.astype(BF16),
                          dgk.astype(BF16), dz], axis=1)
    c_rows = ctx.shape[0]
    zeros = lambda n: jnp.zeros((c_rows, n), BF16)
    dpc = jnp.concatenate([zeros(QW), dgvc, zeros(GVW), dkc.astype(BF16), dvc.astype(BF16), zeros(GKW), dgkc, dzc],
                          axis=1)
    dh = _mm("proj_in_dx", dp, w["w_in"], "nt")
    dhc = _mm("proj_in_ctx_dx", dpc, w["w_in"], "nt")
    g["w_in"] = _mm("proj_in_dw", h, dp, "tn", init=_mm("proj_in_ctx_dw", hc, dpc, "tn"))
    dx, dg_a, dsh1, dsc1 = _norm_mod_bwd("pre_mix_bwd", dh, dx1, x, w["g_pre_mix"], sh1, sc1)
    _, dg_b, dsh1c, dsc1c = _norm_mod_bwd("pre_mix_ctx_bwd", dhc, jnp.zeros_like(dhc), ctx, w["g_pre_mix"], sh1c,
                                          sc1c)
    g["g_pre_mix"] = dg_a + dg_b
    d_ada = jnp.concatenate([dsh1, dsc1, dgt1, dsh2, dsc2, dgt2], axis=1)
    d_ada_c = jnp.concatenate([dsh1c, dsc1c, jnp.zeros((1, 4 * d), F32)], axis=1)
    return loss, dx, g, d_ada, d_ada_c


HBM = pl.BlockSpec(memory_space=pltpu.HBM)
N_DEV, N_CHIP = 8, 4


def _place():
    x, y, c = lax.axis_index("x"), lax.axis_index("y"), lax.axis_index("c")
    return x, y, c, [(1 - x, y), (x, 1 - y), (1 - x, 1 - y)]


def _row_tile(n, mult, cap):
    return max(t for t in range(mult, min(n, cap) + 1, mult) if n % t == 0)


def _ag_small(name, v, after=None):
    follow = () if after is None else (after,)

    def body(v_ref, *rest):
        out_ref, send_sems, recv_sems = rest[len(follow):]
        x, y, c, _ = _place()
        out_ref[4 * x + 2 * y + c] = v_ref[...]

        def peer(r):
            return ((1 - x) if r & 4 else x, (1 - y) if r & 2 else y, (1 - c) if r & 1 else c)

        def copy(r, block):
            px, py, pc = block
            return pltpu.make_async_remote_copy(
                src_ref=v_ref, dst_ref=out_ref.at[4 * px + 2 * py + pc], send_sem=send_sems.at[r - 1],
                recv_sem=recv_sems.at[r - 1], device_id=peer(r), device_id_type=MESH)

        sends = [copy(r, (x, y, c)) for r in range(1, N_DEV)]
        for cp in sends:
            cp.start()
        for r in range(1, N_DEV):
            copy(r, peer(r)).wait_recv()
        for cp in sends:
            cp.wait_send()

    return pl.pallas_call(
        body, name=name, out_shape=jax.ShapeDtypeStruct((N_DEV,) + v.shape, v.dtype),
        in_specs=[pl.BlockSpec(memory_space=pltpu.VMEM)] + [pl.BlockSpec(memory_space=pl.ANY)] * len(follow),
        out_specs=pl.BlockSpec(memory_space=pltpu.VMEM),
        scratch_shapes=[pltpu.SemaphoreType.DMA((N_DEV - 1,)), pltpu.SemaphoreType.DMA((N_DEV - 1,))],
    )(v, *follow)


def _halves(c, rows, mult):
    hr = rows // 2
    return pl.ds(pl.multiple_of(c * hr, mult), hr), pl.ds(pl.multiple_of((1 - c) * hr, mult), hr)


def _ag_shards(name, shard):
    rows = shard.shape[0]

    def body(w_ref, out_ref, send_sems, recv_sems, local_sem):
        x, y, c, chips = _place()
        mine_half, other_half = _halves(c, rows, 16)
        me = 2 * x + y
        mine = pltpu.make_async_copy(w_ref, out_ref.at[me], local_sem)
        mine.start()

        def copy(k, src, chip, half, to):
            return pltpu.make_async_remote_copy(
                src_ref=src, dst_ref=out_ref.at[chip, half], send_sem=send_sems.at[k], recv_sem=recv_sems.at[k],
                device_id=to, device_id_type=MESH)

        first = [copy(j, w_ref.at[mine_half], me, mine_half, (px, py, c)) for j, (px, py) in enumerate(chips)]
        for cp in first:
            cp.start()
        passed = []
        for j, (px, py) in enumerate(chips):
            pk = 2 * px + py
            copy(j, w_ref.at[mine_half], pk, mine_half, (px, py, c)).wait_recv()
            cp = copy(3 + j, out_ref.at[pk, mine_half], pk, mine_half, (x, y, 1 - c))
            cp.start()
            passed.append(cp)
        for j, (px, py) in enumerate(chips):
            copy(3 + j, w_ref.at[mine_half], 2 * px + py, other_half, (x, y, 1 - c)).wait_recv()
        for cp in first + passed:
            cp.wait_send()
        mine.wait()

    return pl.pallas_call(
        body, name=name, out_shape=jax.ShapeDtypeStruct((N_CHIP,) + shard.shape, shard.dtype),
        in_specs=[HBM], out_specs=HBM,
        scratch_shapes=[pltpu.SemaphoreType.DMA((6,)), pltpu.SemaphoreType.DMA((6,)), pltpu.SemaphoreType.DMA],
    )(shard)


def _swap_half(name, g):
    n_sh, rows, n = g.shape

    def body(g_ref, a_ref, send_sem, recv_sem):
        x, y, c, _ = _place()
        _, other_half = _halves(c, rows, 8)
        cp = pltpu.make_async_remote_copy(
            src_ref=g_ref.at[pl.ds(0, n_sh), other_half], dst_ref=a_ref, send_sem=send_sem, recv_sem=recv_sem,
            device_id=(x, y, 1 - c), device_id_type=MESH)
        cp.start()
        cp.wait()

    return pl.pallas_call(
        body, name=name, out_shape=jax.ShapeDtypeStruct((n_sh, rows // 2, n), g.dtype), in_specs=[HBM], out_specs=HBM,
        scratch_shapes=[pltpu.SemaphoreType.DMA, pltpu.SemaphoreType.DMA],
    )(g)


def _add_half(name, g, a, c_idx):
    n_sh, hr, n = a.shape
    tr = _row_tile(hr, 16, 1024)
    nb = hr // tr

    def body(c_ref, g_ref, a_ref, o_ref):
        o_ref[...] = (g_ref[...] + a_ref[...]).astype(o_ref.dtype)

    return pl.pallas_call(
        body, name=name, out_shape=jax.ShapeDtypeStruct(a.shape, BF16),
        grid_spec=pltpu.PrefetchScalarGridSpec(
            num_scalar_prefetch=1, grid=(n_sh, nb),
            in_specs=[pl.BlockSpec((1, tr, n), lambda s, i, c_ref: (s, c_ref[0] * nb + i, 0)),
                      pl.BlockSpec((1, tr, n), lambda s, i, c_ref: (s, i, 0))],
            out_specs=pl.BlockSpec((1, tr, n), lambda s, i, c_ref: (s, i, 0))),
        compiler_params=_cp("parallel", "parallel"),
    )(c_idx, g, a)


def _scatter_chips(name, h):
    def body(h_ref, b_ref, send_sems, recv_sems, local_sem):
        x, y, c, chips = _place()
        me = 2 * x + y
        mine = pltpu.make_async_copy(h_ref.at[me], b_ref.at[me], local_sem)
        mine.start()

        def copy(j, src_block, dst_block, to):
            return pltpu.make_async_remote_copy(
                src_ref=h_ref.at[src_block], dst_ref=b_ref.at[dst_block], send_sem=send_sems.at[j],
                recv_sem=recv_sems.at[j], device_id=to, device_id_type=MESH)

        sends = [copy(j, 2 * px + py, me, (px, py, c)) for j, (px, py) in enumerate(chips)]
        for cp in sends:
            cp.start()
        for j, (px, py) in enumerate(chips):
            copy(j, me, 2 * px + py, (px, py, c)).wait_recv()
        for cp in sends:
            cp.wait_send()
        mine.wait()

    return pl.pallas_call(
        body, name=name, out_shape=jax.ShapeDtypeStruct(h.shape, h.dtype), in_specs=[HBM], out_specs=HBM,
        scratch_shapes=[pltpu.SemaphoreType.DMA((3,)), pltpu.SemaphoreType.DMA((3,)), pltpu.SemaphoreType.DMA],
    )(h)


def _sum_chips(name, b):
    n_sh, hr, n = b.shape
    tr = _row_tile(hr, 16, 1024)

    def body(b0, b1, b2, b3, o_ref):
        o_ref[...] = ((b0[0].astype(F32) + b1[0].astype(F32)) + b2[0].astype(F32)) + b3[0].astype(F32)

    return pl.pallas_call(
        body, name=name, grid=(hr // tr,), out_shape=jax.ShapeDtypeStruct((hr, n), F32),
        in_specs=[pl.BlockSpec((1, tr, n), functools.partial(lambda i, k: (k, i, 0), k=k)) for k in range(n_sh)],
        out_specs=pl.BlockSpec((tr, n), lambda i: (i, 0)), compiler_params=_cp("parallel"),
    )(b, b, b, b)


def _share_half(name, f):
    hr, n = f.shape

    def body(f_ref, out_ref, send_sem, recv_sem, local_sem):
        x, y, c, _ = _place()
        mine_half, other_half = _halves(c, 2 * hr, 8)
        mine = pltpu.make_async_copy(f_ref, out_ref.at[mine_half], local_sem)
        mine.start()

        def copy(half):
            return pltpu.make_async_remote_copy(
                src_ref=f_ref, dst_ref=out_ref.at[half], send_sem=send_sem, recv_sem=recv_sem,
                device_id=(x, y, 1 - c), device_id_type=MESH)

        send = copy(mine_half)
        send.start()
        copy(other_half).wait_recv()
        send.wait_send()
        mine.wait()

    return pl.pallas_call(
        body, name=name, out_shape=jax.ShapeDtypeStruct((2 * hr, n), f.dtype), in_specs=[HBM], out_specs=HBM,
        scratch_shapes=[pltpu.SemaphoreType.DMA, pltpu.SemaphoreType.DMA, pltpu.SemaphoreType.DMA],
    )(f)


def _reduce_shards(name, g, c_idx):
    a = _swap_half(name + "_swap", g)
    h = _add_half(name + "_pair", g, a, c_idx)
    b = _scatter_chips(name + "_scatter", h)
    f = _sum_chips(name + "_sum", b)
    return _share_half(name + "_share", f)


SEM = pl.BlockSpec(memory_space=pltpu.SEMAPHORE)
ANY = pl.BlockSpec(memory_space=pl.ANY)
DATAFLOW = pltpu.SideEffectType.DATAFLOW_SIDE_EFFECTING


def _remote(src, dst, send_sems, recv_sems, k, to):
    return pltpu.make_async_remote_copy(src_ref=src, dst_ref=dst, send_sem=send_sems.at[k], recv_sem=recv_sems.at[k],
                                        device_id=to, device_id_type=MESH)


def _split_copy(name, src, land_shape, land_dtype, n, plan, after=None):
    after = jnp.zeros((8, LANES), F32) if after is None else after

    def start_body(src_ref, land_ref, after_ref, send_sems, recv_sems, src_thru, land_thru, token):
        for cp in plan(src_ref, land_ref, send_sems, recv_sems)[0]:
            cp.start()
        token[...] = jnp.zeros_like(token)

    sems = pltpu.SemaphoreType.DMA((n,))
    send_sems, recv_sems, src_thru, land_thru, token = pl.pallas_call(
        start_body, name=name + "_start",
        out_shape=(sems, sems, pltpu.HBM(src.shape, src.dtype), pltpu.HBM(land_shape, land_dtype),
                   jax.ShapeDtypeStruct((8, LANES), F32)),
        in_specs=(HBM, HBM, ANY), out_specs=(SEM, SEM, HBM, HBM, pl.BlockSpec(memory_space=pltpu.VMEM)),
        input_output_aliases={0: 2, 1: 3}, compiler_params=pltpu.CompilerParams(has_side_effects=DATAFLOW),
    )(pltpu.with_memory_space_constraint(src, pltpu.HBM),
      pltpu.with_memory_space_constraint(lax.empty(land_shape, land_dtype), pltpu.HBM), after)

    def wait(after):
        def wait_body(src_ref, land_ref, send_sems, recv_sems, after_ref, src_out, land_out):
            sent, received = plan(src_ref, land_ref, send_sems, recv_sems)
            for cp in sent:
                cp.wait_send()
            for cp in received:
                cp.wait_recv()

        return pl.pallas_call(
            wait_body, name=name + "_wait",
            out_shape=(pltpu.HBM(src.shape, src.dtype), pltpu.HBM(land_shape, land_dtype)),
            in_specs=(HBM, HBM, SEM, SEM, ANY), out_specs=(HBM, HBM), input_output_aliases={0: 0, 1: 1},
            compiler_params=pltpu.CompilerParams(has_side_effects=DATAFLOW),
        )(src_thru, land_thru, send_sems, recv_sems, after)

    return token, wait


def _behind(x, token):
    return x + token[0, 0]


def _plan_gather(src_ref, land_ref, send_sems, recv_sems):
    x, y, c, chips = _place()
    sent = [_remote(src_ref, land_ref.at[2 * x + y], send_sems, recv_sems, j, (px, py, c))
            for j, (px, py) in enumerate(chips)]
    received = [_remote(src_ref, land_ref.at[2 * px + py], send_sems, recv_sems, j, (px, py, c))
                for j, (px, py) in enumerate(chips)]
    return sent, received


def _plan_swap(src_ref, land_ref, send_sems, recv_sems):
    x, y, c, _ = _place()
    _, other_half = _halves(c, src_ref.shape[1], 8)
    cp = _remote(src_ref.at[pl.ds(0, src_ref.shape[0]), other_half], land_ref, send_sems, recv_sems, 0, (x, y, 1 - c))
    return [cp], [cp]


def _plan_scatter(src_ref, land_ref, send_sems, recv_sems):
    x, y, c, chips = _place()
    sent = [_remote(src_ref.at[2 * px + py], land_ref.at[2 * x + y], send_sems, recv_sems, j, (px, py, c))
            for j, (px, py) in enumerate(chips)]
    received = [_remote(src_ref.at[2 * px + py], land_ref.at[2 * px + py], send_sems, recv_sems, j, (px, py, c))
                for j, (px, py) in enumerate(chips)]
    return sent, received


def _plan_share(src_ref, land_ref, send_sems, recv_sems):
    x, y, c, _ = _place()
    mine_half, other_half = _halves(c, land_ref.shape[0], 8)
    return ([_remote(src_ref, land_ref.at[mine_half], send_sems, recv_sems, 0, (x, y, 1 - c))],
            [_remote(src_ref, land_ref.at[other_half], send_sems, recv_sems, 0, (x, y, 1 - c))])


def _pack_shard_rows(name, parts):
    rows = [t.shape[0] // N_CHIP for t in parts]
    n, total = parts[0].shape[1], sum(t.shape[0] // N_CHIP for t in parts)
    slab, at = None, 0
    for i, (t, r) in enumerate(zip(parts, rows)):
        tr = max(c for c in range(8, min(r, 512) + 1, 8) if r % c == 0 and at % c == 0)
        nb, ob = r // tr, at // tr

        def body(t_ref, *rest):
            rest[-1][0] = t_ref[...]

        slab = pl.pallas_call(
            body, name=f"{name}_{i}", grid=(N_CHIP, nb), out_shape=jax.ShapeDtypeStruct((N_CHIP, total, n), t.dtype),
            in_specs=[pl.BlockSpec((tr, n), functools.partial(lambda k, j, nb: (k * nb + j, 0), nb=nb))]
            + ([] if slab is None else [pl.BlockSpec(memory_space=pl.ANY)]),
            out_specs=pl.BlockSpec((1, tr, n), functools.partial(lambda k, j, ob: (k, ob + j, 0), ob=ob)),
            input_output_aliases={} if slab is None else {1: 0}, compiler_params=_cp("parallel", "parallel"),
        )(*((t,) if slab is None else (t, slab)))
        at += r
    return slab


class _GatherBehind:
    def __init__(self, name, shard, chip, after=None):
        self.chip = chip
        self.token, self.wait = _split_copy(name, shard, (N_CHIP,) + shard.shape, shard.dtype, 3, _plan_gather,
                                            after)

    def result(self, after):
        shard, land = self.wait(after)
        return lax.dynamic_update_slice(land, shard[None], (self.chip, 0, 0))


class _ReduceBehind:
    def __init__(self, name, chip, c, c_idx):
        self.name, self.chip, self.c, self.c_idx = name, chip, c, c_idx

    def start(self, *grads):
        g = _pack_shard_rows(self.name + "_pack", grads)
        n_sh, rows, n = g.shape
        token, self.wait = _split_copy(self.name + "_swap", g, (n_sh, rows // 2, n), g.dtype, 1, _plan_swap)
        return token

    def pair(self, after):
        g, a = self.wait(after)
        h = _add_half(self.name + "_pair", g, a, self.c_idx)
        token, self.wait = _split_copy(self.name + "_scatter", h, h.shape, h.dtype, 3, _plan_scatter)
        return token

    def total(self, after):
        h, b = self.wait(after)
        b = lax.dynamic_update_slice(b, lax.dynamic_slice_in_dim(h, self.chip, 1, axis=0), (self.chip, 0, 0))
        f = _sum_chips(self.name + "_sum", b)
        token, self.wait = _split_copy(self.name + "_share", f, (2 * f.shape[0], f.shape[1]), f.dtype, 1,
                                       _plan_share)
        return token

    def result(self, after):
        f, out = self.wait(after)
        return lax.dynamic_update_slice(out, f, (self.c * f.shape[0], 0))


def _f_adamw(w, g, m, v):
    m = ADAM_B1 * m + (1.0 - ADAM_B1) * g
    v = ADAM_B2 * v + (1.0 - ADAM_B2) * (g * g)
    m_hat = m / (1.0 - ADAM_B1 ** ADAM_STEP)
    v_hat = v / (1.0 - ADAM_B2 ** ADAM_STEP)
    return -ADAM_LR * (m_hat / (jnp.sqrt(v_hat) + ADAM_EPS) + ADAM_WD * w), m, v


def _adamw(name, w, g, m, v):
    rows, n = w.shape
    return _rowwise(name, lambda w, g, m, v: (_f_adamw(w, g, m, v), ()), rows, [(t, n, 0) for t in (w, g, m, v)], [],
                    [(n, F32)] * 3, [], tm=_row_tile(rows, 8, 256))


def _pack_rows(parts):
    rows = []
    for t in parts:
        t = t.reshape(-1)
        rows.append(jnp.pad(t, (0, -t.shape[0] % LANES)).reshape(-1, LANES))
    out = jnp.concatenate(rows, axis=0)
    return jnp.pad(out, ((0, -out.shape[0] % 8), (0, 0)))


def _unpack_rows(packed, shapes):
    out, r = [], 0
    for shp in shapes:
        n = int(np.prod(shp))
        nr = -(-n // LANES)
        out.append(packed[r:r + nr].reshape(-1)[:n].reshape(shp))
        r += nr
    return out


def _sum_blocks(name, g):
    def body(g_ref, o_ref):
        acc = g_ref[0]
        for k in range(1, g.shape[0]):
            acc = acc + g_ref[k]
        o_ref[...] = acc

    return pl.pallas_call(body, name=name, out_shape=jax.ShapeDtypeStruct(g.shape[1:], F32))(g)


def _silu(t):
    return t * _sigmoid(t)


def _ada_fwd(cc, w_ada):
    n = w_ada.shape[1]
    tn = _row_tile(n, LANES, 512)

    def body(cc_ref, w_ref, o_ref):
        o_ref[...] = _nn(_silu(cc_ref[...]), w_ref[...])

    return pl.pallas_call(
        body, name="ada_fwd", grid=(n // tn,), out_shape=jax.ShapeDtypeStruct((cc.shape[0], n), F32),
        in_specs=[pl.BlockSpec(cc.shape, lambda j: (0, 0)), pl.BlockSpec((w_ada.shape[0], tn), lambda j: (0, j))],
        out_specs=pl.BlockSpec((cc.shape[0], tn), lambda j: (0, j)), compiler_params=_cp("parallel"),
    )(cc, w_ada)


def _ada_bwd(cc, dm, w_ada):
    d, n = w_ada.shape
    tn = _row_tile(n, LANES, 512)

    def body(cc_ref, dm_ref, w_ref, gw_ref, ds_ref):
        @pl.when(pl.program_id(0) == 0)
        def _():
            ds_ref[...] = jnp.zeros_like(ds_ref)

        gw_ref[...] = _raw_dot("tn", _silu(cc_ref[...]), dm_ref[...], True)
        ds_ref[...] += _raw_dot("nt", dm_ref[...], w_ref[...], False)

    return pl.pallas_call(
        body, name="ada_bwd", grid=(n // tn,),
        out_shape=[jax.ShapeDtypeStruct((d, n), F32), jax.ShapeDtypeStruct(cc.shape, F32)],
        in_specs=[pl.BlockSpec(cc.shape, lambda j: (0, 0)), pl.BlockSpec((cc.shape[0], tn), lambda j: (0, j)),
                  pl.BlockSpec((d, tn), lambda j: (0, j))],
        out_specs=[pl.BlockSpec((d, tn), lambda j: (0, j)), pl.BlockSpec(cc.shape, lambda j: (0, 0))],
        compiler_params=_cp("arbitrary"),
    )(cc, dm, w_ada)


def _c_ctx_grad(parts, c_ctx):
    def body(p_ref, c_ref, o_ref):
        ds = ((p_ref[0] + p_ref[1]) + p_ref[2]) + p_ref[3]
        _, vjp = jax.vjp(_silu, c_ref[...])
        o_ref[...] = vjp(ds)[0]

    return pl.pallas_call(body, name="c_ctx_grad", out_shape=jax.ShapeDtypeStruct(c_ctx.shape, F32))(parts, c_ctx)


def kernel(x, c, ctx, c_ctx, w_ada, b_ada, g_pre_mix, g_post_mix, g_pre_ffn, g_post_ffn, w_in, attn_sink, w_gate_fwd, b_gate_fwd, w_gate_bwd, b_gate_bwd, g_gla_norm, w_out, w_ffn_in, w_ffn_out, loss_target, m_c_ctx, m_w_ada, m_b_ada, m_g_pre_mix, m_g_post_mix, m_g_pre_ffn, m_g_post_ffn, m_w_in, m_attn_sink, m_w_gate_fwd, m_b_gate_fwd, m_w_gate_bwd, m_b_gate_bwd, m_g_gla_norm, m_w_out, m_w_ffn_in, m_w_ffn_out, v_c_ctx, v_w_ada, v_b_ada, v_g_pre_mix, v_g_post_mix, v_g_pre_ffn, v_g_post_ffn, v_w_in, v_attn_sink, v_w_gate_fwd, v_b_gate_fwd, v_w_gate_bwd, v_b_gate_bwd, v_g_gla_norm, v_w_out, v_w_ffn_in, v_w_ffn_out):
    xi, yi, ci = lax.axis_index("x"), lax.axis_index("y"), lax.axis_index("c")
    dev, chip = 4 * xi + 2 * yi + ci, 2 * xi + yi
    c_idx = jnp.reshape(ci, (1,)).astype(jnp.int32)
    d = x.shape[-1]
    n_ada, n_in, n_f = w_ada.shape[-1], w_in.shape[-1], w_ffn_in.shape[-1]
    r_out, r_f = w_out.shape[1], w_ffn_out.shape[1]
    n_gate = w_gate_fwd.shape[-1]
    by_chip = lambda t: t[0::2]

    w_in_g = _ag_shards("gather_w_in", w_in[0].astype(BF16))

    rc = -(-d // LANES)
    g1 = _ag_small("gather_cond", _pack_rows([c[0], w_gate_fwd[0], w_gate_bwd[0]]), w_in_g)
    c_all = g1[:, :rc].reshape(N_DEV, -1)[:, :d]
    gr = GATE_RANK * n_gate // LANES
    gate_full = lambda off: jnp.transpose(by_chip(g1)[:, off:off + gr].reshape(N_CHIP, GATE_RANK, n_gate),
                                          (1, 0, 2)).reshape(GATE_RANK, N_CHIP * n_gate)
    wgf, wgb = gate_full(rc), gate_full(rc + gr)
    cc = jnp.concatenate([c_all, c_ctx[None, :], jnp.zeros((7, d), F32)], axis=0)

    g2 = _ag_small("gather_ada", _ada_fwd(cc, w_ada[0]).reshape(-1, LANES))
    ada_all = jnp.transpose(by_chip(g2).reshape(N_CHIP, 16, n_ada), (1, 0, 2)).reshape(16, N_CHIP * n_ada) + b_ada
    late = _GatherBehind("gather_late", jnp.concatenate(
        [w_out[0], w_ffn_out[0], jnp.transpose(w_ffn_in[0])], axis=0).astype(BF16), chip, g2)

    def late_weights(after):
        t = late.result(after)
        r1, r2 = r_out, r_out + r_f
        return (t[:, :r1].reshape(N_CHIP * r_out, d), t[:, r2:].reshape(N_CHIP * n_f, d),
                t[:, r1:r2].reshape(N_CHIP * r_f, d))

    ada_all = _behind(ada_all, late.token)
    ada = lax.dynamic_slice(ada_all, (dev, 0), (1, N_CHIP * n_ada))
    ada_c = ada_all[N_DEV:N_DEV + 1]

    w = _prep_weights(jnp.concatenate([w_in_g[k] for k in range(N_CHIP)], axis=1), wgf, wgb)
    w.update(g_pre_mix=g_pre_mix, g_post_mix=g_post_mix, g_pre_ffn=g_pre_ffn, g_post_ffn=g_post_ffn,
             attn_sink=attn_sink, b_gate_fwd=b_gate_fwd, b_gate_bwd=b_gate_bwd, g_gla_norm=g_gla_norm)

    reduce_behind = _ReduceBehind("reduce_late", chip, ci, c_idx)
    loss_lanes, grad_x, g, d_ada, d_ada_c = _local_step(x[0], ctx[0], loss_target[0], ada, ada_c, w, late_weights,
                                                        reduce_behind)

    small = ("g_pre_mix", "g_post_mix", "g_pre_ffn", "g_post_ffn", "attn_sink", "b_gate_fwd", "b_gate_bwd",
             "g_gla_norm", "w_gate_fwd", "w_gate_bwd")
    shapes = [(1, 6 * d)] * 2 + [g[n].shape for n in small]
    g3 = _ag_small("gather_small_grads", _pack_rows([d_ada, d_ada_c] + [g[n] for n in small]))
    tot = dict(zip(("d_ada", "d_ada_c") + small, _unpack_rows(_sum_blocks("sum_small_grads", g3), shapes)))
    r_ada = 6 * d // LANES
    dm = jnp.concatenate([g3[:, :r_ada].reshape(N_DEV, 6 * d), tot["d_ada_c"], jnp.zeros((7, 6 * d), F32)], axis=0)
    grads = {n: tot[n] for n in small[:8]}
    grads["b_ada"] = _sum_blocks("sum_b_ada", dm.reshape(16, r_ada, LANES)).reshape(1, 6 * d)
    grads["w_gate_fwd"] = lax.dynamic_slice(tot["w_gate_fwd"], (0, chip * n_gate), (GATE_RANK, n_gate))[None]
    grads["w_gate_bwd"] = lax.dynamic_slice(tot["w_gate_bwd"], (0, chip * n_gate), (GATE_RANK, n_gate))[None]
    gw_ada, dsc = _ada_bwd(cc, lax.dynamic_slice(dm, (0, chip * n_ada), (16, n_ada)), w_ada[0])
    grads["w_ada"] = gw_ada[None]
    g4 = _ag_small("gather_c_ctx", _pack_rows([dsc[N_DEV]]))
    grads["c_ctx"] = _c_ctx_grad(by_chip(g4), _pack_rows([c_ctx])).reshape(-1)[:d]

    g_w_in = _unpack_w_in_grad(g["w_in"])
    grads["w_in"] = _reduce_shards("reduce_w_in", jnp.stack([g_w_in[:, k * n_in:(k + 1) * n_in]
                                                             for k in range(N_CHIP)]), c_idx)[None]
    behind = g["behind"]
    grads["w_ffn_in"] = jnp.transpose(behind[:n_f])[None]
    grads["w_ffn_out"], grads["w_out"] = behind[None, n_f:n_f + r_f], behind[None, n_f + r_f:]

    names = ("c_ctx", "w_ada", "b_ada", "g_pre_mix", "g_post_mix", "g_pre_ffn", "g_post_ffn", "w_in", "attn_sink",
             "w_gate_fwd", "b_gate_fwd", "w_gate_bwd", "b_gate_bwd", "g_gla_norm", "w_out", "w_ffn_in", "w_ffn_out")
    weights = dict(zip(names, (c_ctx, w_ada, b_ada, g_pre_mix, g_post_mix, g_pre_ffn, g_post_ffn, w_in, attn_sink,
                               w_gate_fwd, b_gate_fwd, w_gate_bwd, b_gate_bwd, g_gla_norm, w_out, w_ffn_in,
                               w_ffn_out)))
    m_in = dict(zip(names, (m_c_ctx, m_w_ada, m_b_ada, m_g_pre_mix, m_g_post_mix, m_g_pre_ffn, m_g_post_ffn, m_w_in,
                            m_attn_sink, m_w_gate_fwd, m_b_gate_fwd, m_w_gate_bwd, m_b_gate_bwd, m_g_gla_norm,
                            m_w_out, m_w_ffn_in, m_w_ffn_out)))
    v_in = dict(zip(names, (v_c_ctx, v_w_ada, v_b_ada, v_g_pre_mix, v_g_post_mix, v_g_pre_ffn, v_g_post_ffn, v_w_in,
                            v_attn_sink, v_w_gate_fwd, v_b_gate_fwd, v_w_gate_bwd, v_b_gate_bwd, v_g_gla_norm,
                            v_w_out, v_w_ffn_in, v_w_ffn_out)))
    large = ("w_ada", "w_in", "w_out", "w_ffn_in", "w_ffn_out")
    tiny = tuple(n for n in names if n not in large)
    delta, new_m, new_v = {}, {}, {}
    for n in large:
        dl, nm, nv = _adamw("adamw_" + n, weights[n][0], grads[n][0], m_in[n][0], v_in[n][0])
        delta[n], new_m[n], new_v[n] = dl[None], nm[None], nv[None]
    tiny_shapes = [weights[n].shape for n in tiny]
    packed = [_pack_rows([t[n] for n in tiny]) for t in (weights, grads, m_in, v_in)]
    for out, res in zip((delta, new_m, new_v), _adamw("adamw_small", *packed)):
        out.update(zip(tiny, _unpack_rows(res, tiny_shapes)))
    for n in tiny:
        grads[n] = grads[n].reshape(weights[n].shape)

    loss = lax.psum(loss_lanes[0, 0], ("x", "y", "c"))
    return (loss, grad_x[None], *[grads[n] for n in names], *[delta[n] for n in names], *[new_m[n] for n in names],
            *[new_v[n] for n in names])
```

```python
import functools

import jax
import jax.numpy as jnp
import numpy as np
from jax import lax
from jax.experimental import pallas as pl
from jax.experimental.pallas import tpu as pltpu

F32 = jnp.float32
BF16 = jnp.bfloat16
MESH = pl.DeviceIdType.MESH

HEAD_DIM = 64
ATT_HEADS = 8
ATT_KV_HEADS = 2
ATT_GROUP = ATT_HEADS // ATT_KV_HEADS
WINDOW = 128
BLOCK = 128
GRID_W = 64
ROPE_BASE = 10000.0
GLA_HEADS = 8
GLA_DK = 32
GLA_DV = 64
GLA_CHUNK = 64
GATE_RANK = 16
GATE_TAU = 16.0
NEG_INF = -1e30
QW = ATT_HEADS * HEAD_DIM
KVW = ATT_KV_HEADS * HEAD_DIM
GKW = GLA_HEADS * GLA_DK
GVW = GLA_HEADS * GLA_DV
IN_COLS = QW + 2 * KVW + 2 * GKW + 2 * GVW + 2 * GATE_RANK
LANES = 128
IN_PAD = IN_COLS + LANES - 2 * GATE_RANK
C_Q, C_GV, C_GG = 0, QW, QW + GVW
C_K = C_GG + GVW
C_V = C_K + KVW
C_GQ = C_V + KVW
C_GK = C_GQ + GKW
C_Z = C_GK + GKW
MIX = QW + GVW

ADAM_LR, ADAM_B1, ADAM_B2, ADAM_EPS, ADAM_WD, ADAM_STEP = 0.001, 0.9, 0.999, 1e-08, 0.01, 10

VMEM_LIMIT = 56 * 1024 * 1024


def _cp(*sem):
    return pltpu.CompilerParams(dimension_semantics=sem, vmem_limit_bytes=VMEM_LIMIT)


def _pick(n, cands):
    for t in cands:
        if n % t == 0:
            return t
    return n


_DIMS = {"nn": (((1,), (0,)), ((), ())), "nt": (((1,), (1,)), ((), ())), "tn": (((0,), (0,)), ((), ()))}


def _raw_dot(mode, a, b, hi):
    if hi:
        return lax.dot_general(a.astype(F32), b.astype(F32), _DIMS[mode], precision=lax.Precision.HIGHEST,
                               preferred_element_type=F32)
    return lax.dot_general(a.astype(BF16), b.astype(BF16), _DIMS[mode], preferred_element_type=F32)


def _make_dot(mode, hi):
    @jax.custom_vjp
    def dot(a, b):
        return _raw_dot(mode, a, b, hi)

    def fwd(a, b):
        return _raw_dot(mode, a, b, hi), (a, b)

    def bwd(res, dc):
        a, b = res
        if mode == "nn":
            return _raw_dot("nt", dc, b, hi), _raw_dot("tn", a, dc, hi)
        if mode == "nt":
            return _raw_dot("nn", dc, b, hi), _raw_dot("tn", dc, a, hi)
        return _raw_dot("nt", b, dc, hi), _raw_dot("nn", a, dc, hi)

    dot.defvjp(fwd, bwd)
    return dot


_nn, _nt, _tn = _make_dot("nn", False), _make_dot("nt", False), _make_dot("tn", False)
_nn_hi = _make_dot("nn", True)


MM_VMEM_BUDGET = 44 * 1024 * 1024


def _halvings(n):
    out = [n]
    while out[-1] % (2 * LANES) == 0:
        out.append(out[-1] // 2)
    return out


def _mm_tiles(mode, m, n, k, a_bytes, b_bytes, o_bytes, init_bytes=0):
    tms = [t for t in dict.fromkeys((m, m // 2, m // 4, 2048, 1024, 512, 256, 128))
           if m % t == 0 and t % (LANES if mode == "tn" else 16) == 0 and t <= 4096] or [m]
    if mode == "tn":
        fits = [(k // tk + 0.5 * (m // tm), tm, tk)
                for tk in (4096, 2048, 1024, 512, 256, 128) if k % tk == 0 for tm in tms
                if 2 * (tk * tm * a_bytes + tk * n * b_bytes + tm * n * (o_bytes + init_bytes)) <= MM_VMEM_BUDGET]
        if fits:
            _, tm, tk = min(fits)
            return tm, n, tk
    tks = ([t for t in (512, 256, 128) if k % t == 0] or [k]) if mode == "tn" else _halvings(k)
    for tn in _halvings(n):
        for tk in tks:
            for tm in tms:
                acc = tm * tn * 4 if (k // tk > 1 and o_bytes != 4) else 0
                tiles = tm * tk * a_bytes + tk * tn * b_bytes + tm * tn * (o_bytes + init_bytes)
                if 2 * tiles + acc <= MM_VMEM_BUDGET:
                    return tm, tn, tk
    return tms[-1], _halvings(n)[-1], tks[-1]


def _mm(name, a, b, mode, out_dtype=F32, init=None, after=None):
    follow = () if after is None else (after,)
    if mode == "nn":
        (m, k), n = a.shape, b.shape[1]
    elif mode == "nt":
        (m, k), n = a.shape, b.shape[0]
    else:
        (k, m), n = a.shape, b.shape[1]
    tm, tn, tk = _mm_tiles(mode, m, n, k, a.dtype.itemsize, b.dtype.itemsize, jnp.dtype(out_dtype).itemsize,
                           0 if init is None else 4)
    nk = k // tk
    use_acc = nk > 1 and out_dtype != F32

    inits = () if init is None else (init,)

    def body(a_ref, b_ref, *rest):
        rest = rest[:len(inits)] + rest[len(inits) + len(follow):]
        o_ref, acc = rest[len(inits)], rest[len(inits) + 1:]
        part = _raw_dot(mode, a_ref[...], b_ref[...], False)
        first = lambda: part + rest[0][...] if inits else part
        if nk == 1:
            o_ref[...] = first().astype(o_ref.dtype)
            return
        acc_ref = acc[0] if use_acc else o_ref
        kk = pl.program_id(2)

        @pl.when(kk == 0)
        def _():
            acc_ref[...] = first()

        @pl.when(kk > 0)
        def _():
            acc_ref[...] += part

        if use_acc:
            @pl.when(kk == nk - 1)
            def _():
                o_ref[...] = acc_ref[...].astype(o_ref.dtype)

    if mode == "nn":
        a_spec = pl.BlockSpec((tm, tk), lambda i, j, kk: (i, kk))
        b_spec = pl.BlockSpec((tk, tn), lambda i, j, kk: (kk, j))
    elif mode == "nt":
        a_spec = pl.BlockSpec((tm, tk), lambda i, j, kk: (i, kk))
        b_spec = pl.BlockSpec((tn, tk), lambda i, j, kk: (j, kk))
    else:
        a_spec = pl.BlockSpec((tk, tm), lambda i, j, kk: (kk, i))
        b_spec = pl.BlockSpec((tk, tn), lambda i, j, kk: (kk, j))
    return pl.pallas_call(
        body, name=name, grid=(m // tm, n // tn, nk),
        in_specs=[a_spec, b_spec] + [pl.BlockSpec((tm, tn), lambda i, j, kk: (i, j))] * len(inits)
        + [pl.BlockSpec(memory_space=pl.ANY)] * len(follow),
        out_specs=pl.BlockSpec((tm, tn), lambda i, j, kk: (i, j)),
        out_shape=jax.ShapeDtypeStruct((m, n), out_dtype),
        scratch_shapes=[pltpu.VMEM((tm, tn), F32)] if use_acc else [],
        compiler_params=_cp("parallel", "parallel", "arbitrary"),
    )(a, b, *inits, *follow)


def _rowwise(name, fn, rows, row_ins, full_ins, row_outs, acc_outs, tm=None):
    tm = tm or _pick(rows, (512, 256, 128))
    n_r, n_f, n_o, n_a = len(row_ins), len(full_ins), len(row_outs), len(acc_outs)

    def body(*refs):
        ins, outs = refs[:n_r + n_f], refs[n_r + n_f:]
        vals = [r[...].astype(F32) for r in ins]
        ro, ao = fn(*vals)
        for r, val in zip(outs[:n_o], ro):
            r[...] = val.astype(r.dtype)
        if n_a:
            @pl.when(pl.program_id(0) == 0)
            def _():
                for r in outs[n_o:]:
                    r[...] = jnp.zeros_like(r)

            for r, val in zip(outs[n_o:], ao):
                r[...] += val

    in_specs = [pl.BlockSpec((tm, w), functools.partial(lambda i, cb: (i, cb), cb=cb)) for _, w, cb in row_ins]
    in_specs += [pl.BlockSpec(a.shape, lambda i: (0, 0)) for a in full_ins]
    out_specs = [pl.BlockSpec((tm, w), lambda i: (i, 0)) for w, _ in row_outs]
    out_specs += [pl.BlockSpec(s, lambda i: (0, 0)) for s in acc_outs]
    out_shape = [jax.ShapeDtypeStruct((rows, w), dt) for w, dt in row_outs]
    out_shape += [jax.ShapeDtypeStruct(s, F32) for s in acc_outs]
    return pl.pallas_call(
        body, name=name, grid=(rows // tm,), in_specs=in_specs, out_specs=out_specs, out_shape=out_shape,
        compiler_params=_cp("arbitrary" if n_a else "parallel"),
    )(*[a for a, _, _ in row_ins], *full_ins)


def _rn(x):
    return x * lax.rsqrt(jnp.mean(x * x, axis=-1, keepdims=True) + 1e-6)


def _sigmoid(t):
    return 1.0 / (1.0 + jnp.exp(-t))


def _f_norm_mod(x, g, sh, sc):
    return _rn(x) * g * (1.0 + sc) + sh


def _f_post_res(xr, y, g, gate):
    return xr + gate * (_rn(y) * g)


def _f_swiglu(g, u):
    return g * _sigmoid(g) * u


def _logsig(u):
    return jnp.minimum(u, 0.0) - jnp.log(1.0 + jnp.exp(-jnp.abs(u)))


def _f_gate(z, wf, wb, bf, bb):
    return _logsig(_nn(z, wf) + bf) / GATE_TAU, _logsig(_nn(z, wb) + bb) / GATE_TAU


def _f_gla_out(of, ob, gg, gt, bd):
    o = of + ob
    ms = _nn_hi(o * o, bd)
    return o * lax.rsqrt(ms + 1e-6) * gt * (gg * _sigmoid(gg))


def _norm_mod(name, x, g, sh, sc):
    rows, d = x.shape
    return _rowwise(name, lambda x, g, sh, sc: ((_f_norm_mod(x, g, sh, sc),), ()), rows,
                    [(x, d, 0)], [g, sh, sc], [(d, BF16)], [])[0]


def _norm_mod_bwd(name, dh, dres, x, g, sh, sc):
    rows, d = x.shape

    def fn(dh, dres, x, g, sh, sc):
        _, vjp = jax.vjp(_f_norm_mod, x, g, sh, sc)
        dx, dg, dsh, dsc = vjp(dh)
        return (dx + dres,), (dg, dsh, dsc)

    return _rowwise(name, fn, rows, [(dh, d, 0), (dres, d, 0), (x, d, 0)], [g, sh, sc], [(d, F32)],
                    [(1, d)] * 3)


def _post_res(name, xr, y, g, gate):
    rows, d = xr.shape
    return _rowwise(name, lambda xr, y, g, gate: ((_f_post_res(xr, y, g, gate),), ()), rows,
                    [(xr, d, 0), (y, d, 0)], [g, gate], [(d, F32)], [])[0]


def _post_res_bwd(name, dxo, y, g, gate):
    rows, d = y.shape

    def fn(dxo, y, g, gate):
        _, vjp = jax.vjp(lambda y, g, gate: _f_post_res(jnp.zeros_like(y), y, g, gate), y, g, gate)
        dy, dg, dgate = vjp(dxo)
        return (dy,), (dg, dgate)

    return _rowwise(name, fn, rows, [(dxo, d, 0), (y, d, 0)], [g, gate], [(d, BF16)], [(1, d)] * 2)


def _post_res_norm_mod(name, xr, y, g_post, gate, g_pre, sh, sc):
    rows, d = xr.shape

    def fn(xr, y, g_post, gate, g_pre, sh, sc):
        x1 = _f_post_res(xr, y, g_post, gate)
        return (x1, _f_norm_mod(x1, g_pre, sh, sc)), ()

    return _rowwise(name, fn, rows, [(xr, d, 0), (y, d, 0)], [g_post, gate, g_pre, sh, sc], [(d, F32), (d, BF16)], [])


def _norm_mod_post_res_bwd(name, dh, dres, x1, y, g_pre, sh, sc, g_post, gate):
    rows, d = x1.shape

    def fn(dh, dres, x1, y, g_pre, sh, sc, g_post, gate):
        _, vjp_norm = jax.vjp(_f_norm_mod, x1, g_pre, sh, sc)
        dx1, dg_pre, dsh, dsc = vjp_norm(dh)
        dx1 = dx1 + dres
        _, vjp_res = jax.vjp(lambda y, g, gate: _f_post_res(jnp.zeros_like(y), y, g, gate), y, g_post, gate)
        dy, dg_post, dgate = vjp_res(dx1)
        return (dx1, dy), (dg_pre, dsh, dsc, dg_post, dgate)

    return _rowwise(name, fn, rows, [(dh, d, 0), (dres, d, 0), (x1, d, 0), (y, d, 0)], [g_pre, sh, sc, g_post, gate],
                    [(d, F32), (d, BF16)], [(1, d)] * 5, tm=_pick(rows, (256, 128)))


def _post_res_loss(name, xr, y, g, gate, target):
    rows, d = xr.shape

    def fn(xr, y, target, g, gate):
        x2, vjp = jax.vjp(lambda y, g, gate: _f_post_res(xr, y, g, gate), y, g, gate)
        diff = x2 - target
        part = 0.5 * jnp.sum(jnp.mean(diff * diff, axis=-1, keepdims=True), axis=0, keepdims=True)
        dx2 = diff * (1.0 / d)
        dy, dg, dgate = vjp(dx2)
        return (dx2, dy), (jnp.broadcast_to(part, (1, LANES)), dg, dgate)

    return _rowwise(name, fn, rows, [(xr, d, 0), (y, d, 0), (target, d, 0)], [g, gate], [(d, F32), (d, BF16)],
                    [(1, LANES), (1, d), (1, d)])


def _mm_rows(name, a, b, mode, fn, extras, outs):
    m, k = a.shape
    tm = _pick(m, (256, 128))

    def body(a_ref, b_ref, *rest):
        tiles = fn(_raw_dot(mode, a_ref[...], b_ref[...], False), *[e[...] for e in rest[:len(extras)]])
        for r, val in zip(rest[len(extras):], tiles):
            r[...] = val.astype(r.dtype)

    row = lambda w: pl.BlockSpec((tm, w), lambda i: (i, 0))
    return pl.pallas_call(
        body, name=name, grid=(m // tm,),
        in_specs=[row(k), pl.BlockSpec(b.shape, lambda i: (0, 0))] + [row(e.shape[1]) for e in extras],
        out_specs=[row(w) for w, _ in outs], out_shape=[jax.ShapeDtypeStruct((m, w), dt) for w, dt in outs],
        compiler_params=_cp("parallel"),
    )(a, b, *extras)


def _ffn_in_swiglu(name, h, w_t):
    f = w_t.shape[0] // 2
    fn = lambda u: (u, _f_swiglu(u[:, :f], u[:, f:]))
    return _mm_rows(name, h, w_t, "nt", fn, [], [(2 * f, BF16), (f, BF16)])


def _ffn_out_dx_swiglu_bwd(name, df, w_out, u):
    f = w_out.shape[0]

    def fn(da, u):
        u = u.astype(F32)
        _, vjp = jax.vjp(_f_swiglu, u[:, :f], u[:, f:])
        return (jnp.concatenate(vjp(da), axis=1),)

    return _mm_rows(name, df, w_out, "nt", fn, [u], [(2 * f, BF16)])[0]


def _gate_fwd(name, p, wf, wb, bf, bb):
    rows = p.shape[0]
    return _rowwise(name, lambda z, wf, wb, bf, bb: (_f_gate(z, wf, wb, bf, bb), ()), rows,
                    [(p, LANES, C_Z // LANES)], [wf, wb, bf, bb], [(GKW, F32)] * 2, [])


def _gate_bwd(name, p, dla_f, dla_b, wf, wb, bf, bb):
    rows = p.shape[0]

    def fn(z, dlf, dlb, wf, wb, bf, bb):
        _, vjp = jax.vjp(_f_gate, z, wf, wb, bf, bb)
        dz, dwf, dwb, dbf, dbb = vjp((dlf, dlb))
        return (dz,), (dwf, dwb, dbf, dbb)

    return _rowwise(name, fn, rows, [(p, LANES, C_Z // LANES), (dla_f, GKW, 0), (dla_b, GKW, 0)],
                    [wf, wb, bf, bb], [(LANES, BF16)], [(LANES, GKW), (LANES, GKW), (1, GKW), (1, GKW)])


def _head_mean_matrix():
    h = np.arange(GVW) // GLA_DV
    return jnp.asarray((h[:, None] == h[None, :]).astype(np.float32) / GLA_DV)


def _gla_out(name, attn, of, ob, p, gt):
    rows = of.shape[0]
    bd = _head_mean_matrix()
    fn = lambda attn, of, ob, gg, gt, bd: ((jnp.concatenate([attn, _f_gla_out(of, ob, gg, gt, bd)], axis=1),), ())
    return _rowwise(name, fn, rows, [(attn, QW, 0), (of, GVW, 0), (ob, GVW, 0), (p, GVW, C_GG // GVW)], [gt, bd],
                    [(MIX, BF16)], [])[0]


def _gla_out_bwd(name, dmix, of, ob, p, gt):
    rows = of.shape[0]
    bd = _head_mean_matrix()

    def fn(dm, of, ob, gg, gt, bd):
        _, vjp = jax.vjp(lambda of, gg, gt: _f_gla_out(of, ob, gg, gt, bd), of, gg, gt)
        do, dgg, dgt = vjp(dm)
        return (do, dgg), (dgt,)

    return _rowwise(name, fn, rows, [(dmix, GVW, 1), (of, GVW, 0), (ob, GVW, 0), (p, GVW, C_GG // GVW)], [gt, bd],
                    [(GVW, F32), (GVW, BF16)], [(1, GVW)])


def _rope_tables(n_tokens):
    t = jnp.arange(n_tokens)
    row = (t // GRID_W).astype(F32)
    col = (t % GRID_W).astype(F32)
    half = HEAD_DIM // 2
    inv_freq = ROPE_BASE ** (-jnp.arange(0, half, 2, dtype=F32) / half)
    ang_r = row[:, None] * inv_freq[None, :]
    ang_c = col[:, None] * inv_freq[None, :]
    ang = jnp.concatenate([ang_r, ang_r, ang_c, ang_c], axis=-1)
    sign = jnp.concatenate([-jnp.ones((16,), F32), jnp.ones((16,), F32)] * 2)
    cos, sin = jnp.cos(ang), jnp.sin(ang) * sign[None, :]
    return jnp.tile(cos, (1, 2)), jnp.tile(sin, (1, 2))


def _rot_pairs(x):
    w = x.shape[-1]
    lane = lax.broadcasted_iota(jnp.int32, x.shape, x.ndim - 1)
    return jnp.where((lane % 32) < 16, pltpu.roll(x, w - 16, x.ndim - 1), pltpu.roll(x, 16, x.ndim - 1))


def _rope_apply(x, cos, sin_signed, inverse):
    reps = x.shape[-1] // LANES
    cos = jnp.concatenate([cos] * reps, axis=-1) if reps > 1 else cos
    sin = jnp.concatenate([sin_signed] * reps, axis=-1) if reps > 1 else sin_signed
    if inverse:
        return x * cos + _rot_pairs(x * sin)
    return x * cos + _rot_pairs(x) * sin


def _rope_fwd(name, p, cos, sin):
    rows = p.shape[0]

    def fn(q, k, v, cos, sin):
        return (_rope_apply(q, cos, sin, False), _rope_apply(k, cos, sin, False), v), ()

    return _rowwise(name, fn, rows, [(p, QW, 0), (p, KVW, C_K // KVW), (p, KVW, C_V // KVW), (cos, LANES, 0),
                                     (sin, LANES, 0)], [], [(QW, BF16), (KVW, BF16), (KVW, BF16)], [])


def _rope_bwd(name, dq, dk, cos, sin):
    rows = dq.shape[0]

    def fn(dq, dk, cos, sin):
        return (_rope_apply(dq, cos, sin, True), _rope_apply(dk, cos, sin, True)), ()

    return _rowwise(name, fn, rows, [(dq, QW, 0), (dk, KVW, 0), (cos, LANES, 0), (sin, LANES, 0)], [],
                    [(QW, BF16), (KVW, BF16)], [])


GROUP_ROWS = ATT_GROUP * BLOCK


def _f_attn(qs, kws, vws, kcs, vcs, sink, n, n_tokens):
    row = lax.broadcasted_iota(jnp.int32, (GROUP_ROWS, 1), 0)
    group = sum((row >= g * BLOCK).astype(jnp.int32) for g in range(1, ATT_GROUP))
    i = lax.broadcasted_iota(jnp.int32, (GROUP_ROWS, 3 * BLOCK), 0) - BLOCK * group
    j = lax.broadcasted_iota(jnp.int32, (GROUP_ROWS, 3 * BLOCK), 1)
    kpos = (n - 1) * BLOCK + j
    mask = (jnp.abs(j - BLOCK - i) <= WINDOW) & (kpos >= 0) & (kpos < n_tokens)
    head_id = lax.broadcasted_iota(jnp.int32, (1, ATT_HEADS), 1)
    scale = HEAD_DIM ** -0.5
    outs = []
    for h in range(ATT_KV_HEADS):
        sk = jnp.zeros((GROUP_ROWS, 1), F32)
        for g in range(ATT_GROUP):
            one = jnp.sum(jnp.where(head_id == h * ATT_GROUP + g, sink, 0.0), axis=-1, keepdims=True)
            sk = jnp.where(group == g, one, sk)
        q = qs[h] * scale
        s_w = jnp.where(mask, _nt(q, kws[h]), NEG_INF)
        s_c = _nt(q, kcs[h])
        m = lax.stop_gradient(jnp.maximum(jnp.maximum(jnp.max(s_w, axis=-1, keepdims=True),
                                                      jnp.max(s_c, axis=-1, keepdims=True)), sk))
        pw, pc = jnp.exp(s_w - m), jnp.exp(s_c - m)
        den = jnp.sum(pw, axis=-1, keepdims=True) + jnp.sum(pc, axis=-1, keepdims=True) + jnp.exp(sk - m)
        outs.append((_nn(pw, vws[h]) + _nn(pc, vcs[h])) / den)
    return tuple(outs)


def _group_rows(ref, h):
    hs = lambda hq: slice(hq * HEAD_DIM, (hq + 1) * HEAD_DIM)
    return jnp.concatenate([ref[:, hs(h * ATT_GROUP + g)].astype(F32) for g in range(ATT_GROUP)], axis=0)


def _ungroup_rows(ref, h, val):
    for g in range(ATT_GROUP):
        hq = h * ATT_GROUP + g
        ref[:, hq * HEAD_DIM:(hq + 1) * HEAD_DIM] = val[g * BLOCK:(g + 1) * BLOCK].astype(ref.dtype)


def _attn_loads(n, q_ref, kp_ref, vp_ref, kc_ref, vc_ref):
    r0 = pl.multiple_of(n * BLOCK, BLOCK)
    hs = lambda h: slice(h * HEAD_DIM, (h + 1) * HEAD_DIM)
    qs = [_group_rows(q_ref, h) for h in range(ATT_KV_HEADS)]
    kws = [kp_ref[pl.ds(r0, 3 * BLOCK), hs(h)].astype(F32) for h in range(ATT_KV_HEADS)]
    vws = [vp_ref[pl.ds(r0, 3 * BLOCK), hs(h)].astype(F32) for h in range(ATT_KV_HEADS)]
    kcs = [kc_ref[:, hs(h)].astype(F32) for h in range(ATT_KV_HEADS)]
    vcs = [vc_ref[:, hs(h)].astype(F32) for h in range(ATT_KV_HEADS)]
    return r0, hs, qs, kws, vws, kcs, vcs


def _attn_specs(s, c):
    full = lambda shape: pl.BlockSpec(shape, lambda n: (0, 0))
    return [pl.BlockSpec((BLOCK, QW), lambda n: (n, 0)), full((s + 2 * BLOCK, KVW)), full((s + 2 * BLOCK, KVW)),
            full((c, KVW)), full((c, KVW)), full((1, ATT_HEADS))]


def _attn_fwd(q, kp, vp, kc, vc, sink):
    s, c = q.shape[0], kc.shape[0]

    def body(q_ref, kp_ref, vp_ref, kc_ref, vc_ref, sink_ref, o_ref):
        n = pl.program_id(0)
        _, hs, qs, kws, vws, kcs, vcs = _attn_loads(n, q_ref, kp_ref, vp_ref, kc_ref, vc_ref)
        outs = _f_attn(qs, kws, vws, kcs, vcs, sink_ref[...], n, s)
        for h in range(ATT_KV_HEADS):
            _ungroup_rows(o_ref, h, outs[h])

    return pl.pallas_call(
        body, name="attn_fwd", grid=(s // BLOCK,), in_specs=_attn_specs(s, c),
        out_specs=pl.BlockSpec((BLOCK, QW), lambda n: (n, 0)), out_shape=jax.ShapeDtypeStruct((s, QW), BF16),
        compiler_params=_cp("parallel"),
    )(q, kp, vp, kc, vc, sink)


def _attn_bwd(do, q, kp, vp, kc, vc, sink):
    s, c = q.shape[0], kc.shape[0]

    def body(do_ref, q_ref, kp_ref, vp_ref, kc_ref, vc_ref, sink_ref, dq_ref, dkp_ref, dvp_ref, dkc_ref, dvc_ref,
             dsink_ref):
        n = pl.program_id(0)

        @pl.when(n == 0)
        def _():
            for r in (dkp_ref, dvp_ref, dkc_ref, dvc_ref, dsink_ref):
                r[...] = jnp.zeros_like(r)

        r0, hs, qs, kws, vws, kcs, vcs = _attn_loads(n, q_ref, kp_ref, vp_ref, kc_ref, vc_ref)
        _, vjp = jax.vjp(lambda qs, kws, vws, kcs, vcs, sink: _f_attn(qs, kws, vws, kcs, vcs, sink, n, s),
                         qs, kws, vws, kcs, vcs, sink_ref[...])
        dqs, dkws, dvws, dkcs, dvcs, dsink = vjp(tuple(_group_rows(do_ref, h) for h in range(ATT_KV_HEADS)))
        for h in range(ATT_KV_HEADS):
            _ungroup_rows(dq_ref, h, dqs[h])
            dkp_ref[pl.ds(r0, 3 * BLOCK), hs(h)] += dkws[h]
            dvp_ref[pl.ds(r0, 3 * BLOCK), hs(h)] += dvws[h]
            dkc_ref[:, hs(h)] += dkcs[h]
            dvc_ref[:, hs(h)] += dvcs[h]
        dsink_ref[...] += dsink

    full = lambda shape: pl.BlockSpec(shape, lambda n: (0, 0))
    return pl.pallas_call(
        body, name="attn_bwd", grid=(s // BLOCK,),
        in_specs=[pl.BlockSpec((BLOCK, QW), lambda n: (n, 0))] + _attn_specs(s, c),
        out_specs=[pl.BlockSpec((BLOCK, QW), lambda n: (n, 0)), full((s + 2 * BLOCK, KVW)), full((s + 2 * BLOCK, KVW)),
                   full((c, KVW)), full((c, KVW)), full((1, ATT_HEADS))],
        out_shape=[jax.ShapeDtypeStruct((s, QW), F32), jax.ShapeDtypeStruct((s + 2 * BLOCK, KVW), F32),
                   jax.ShapeDtypeStruct((s + 2 * BLOCK, KVW), F32), jax.ShapeDtypeStruct((c, KVW), F32),
                   jax.ShapeDtypeStruct((c, KVW), F32), jax.ShapeDtypeStruct((1, ATT_HEADS), F32)],
        compiler_params=_cp("arbitrary"),
    )(do, q, kp, vp, kc, vc, sink)


def _gla_masks():
    hk = np.arange(GKW) // GLA_DK
    hv = np.arange(GVW) // GLA_DV
    head_k = (np.arange(GLA_HEADS)[:, None] == hk[None, :]).astype(np.float32)
    head_v = (np.arange(GLA_HEADS)[:, None] == hv[None, :]).astype(np.float32)
    bd_t = (hv[:, None] == hk[None, :]).astype(np.float32)
    return jnp.asarray(head_k), jnp.asarray(head_v), jnp.asarray(bd_t)


def _tri(n, rev, strict=False):
    i = lax.broadcasted_iota(jnp.int32, (n, n), 0)
    j = lax.broadcasted_iota(jnp.int32, (n, n), 1)
    if strict:
        keep = (j > i) if rev else (j < i)
    else:
        keep = (j >= i) if rev else (j <= i)
    return keep


def _f_gla_chunk(q, k, v, la, st, head_k, head_v, bd_t, rev):
    keep = _tri(GLA_CHUNK, rev)
    b = _nn_hi(keep.astype(F32), la)
    bl = jnp.sum(la, axis=0, keepdims=True)
    qd = q * (GLA_DK ** -0.5) * jnp.exp(b)
    ki = k * jnp.exp(-b)
    kd = k * jnp.exp(bl - b)
    q_heads = (qd[None, :, :] * head_k[:, None, :]).reshape(GLA_HEADS * GLA_CHUNK, GKW)
    a_all = _nt(q_heads, ki).reshape(GLA_HEADS, GLA_CHUNK, GLA_CHUNK)
    a_all = jnp.where(keep[None, :, :], a_all, 0.0).reshape(GLA_HEADS * GLA_CHUNK, GLA_CHUNK)
    o_all = _nn(a_all, v).reshape(GLA_HEADS, GLA_CHUNK, GVW)
    intra = jnp.sum(o_all * head_v[:, None, :], axis=0)
    inter = _nt(qd, st)
    st_new = st * jnp.exp(bl) + bd_t * _tn(v, kd)
    return intra + inter, st_new


def _gla_specs(s, tb, order):
    return [pl.BlockSpec((tb, GKW), lambda i: (order(i), C_GQ // GKW)),
            pl.BlockSpec((tb, GKW), lambda i: (order(i), C_GK // GKW)),
            pl.BlockSpec((tb, GVW), lambda i: (order(i), C_GV // GVW)),
            pl.BlockSpec((tb, GKW), lambda i: (order(i), 0))]


GLA_BLOCK_CHUNKS = 4


def _gla_fwd(name, p, la, st0, rev):
    s = p.shape[0]
    tb = GLA_BLOCK_CHUNKS * GLA_CHUNK
    nblk = s // tb
    order = (lambda i: nblk - 1 - i) if rev else (lambda i: i)
    masks = _gla_masks()

    def body(q_ref, k_ref, v_ref, la_ref, st0_ref, hk_ref, hv_ref, bd_ref, o_ref, sts_ref, st_ref):
        @pl.when(pl.program_id(0) == 0)
        def _():
            st_ref[...] = st0_ref[...]

        st = st_ref[...]
        sts_ref[0] = st
        chunks = range(GLA_BLOCK_CHUNKS)
        for ci in (reversed(chunks) if rev else chunks):
            rows = slice(ci * GLA_CHUNK, (ci + 1) * GLA_CHUNK)
            o, st = _f_gla_chunk(q_ref[rows, :], k_ref[rows, :], v_ref[rows, :], la_ref[rows, :], st,
                                 hk_ref[...], hv_ref[...], bd_ref[...], rev)
            o_ref[rows, :] = o
        st_ref[...] = st

    full = lambda a: pl.BlockSpec(a.shape, lambda i: (0,) * a.ndim)
    return pl.pallas_call(
        body, name=name, grid=(nblk,),
        in_specs=_gla_specs(s, tb, order) + [full(st0)] + [full(m) for m in masks],
        out_specs=[pl.BlockSpec((tb, GVW), lambda i: (order(i), 0)),
                   pl.BlockSpec((1, GVW, GKW), lambda i: (order(i), 0, 0))],
        out_shape=[jax.ShapeDtypeStruct((s, GVW), F32), jax.ShapeDtypeStruct((nblk, GVW, GKW), F32)],
        scratch_shapes=[pltpu.VMEM((GVW, GKW), F32)],
        compiler_params=_cp("arbitrary"),
    )(p, p, p, la, st0, *masks)


def _gla_bwd(name, p, la, sts, do, prev, rev, after=None):
    s = p.shape[0]
    tb = GLA_BLOCK_CHUNKS * GLA_CHUNK
    nblk = s // tb
    order = (lambda i: i) if rev else (lambda i: nblk - 1 - i)
    masks = _gla_masks()
    n_prev = 0 if prev is None else 3
    follow = () if after is None else (after,)

    def body(*refs):
        q_ref, k_ref, v_ref, la_ref, sts_ref, do_ref, hk_ref, hv_ref, bd_ref = refs[:9]
        prev_refs = refs[9:9 + n_prev]
        dq_ref, dk_ref, dv_ref, dla_ref, dst0_ref, dst_ref = refs[9 + n_prev + len(follow):]

        @pl.when(pl.program_id(0) == 0)
        def _():
            dst_ref[...] = jnp.zeros_like(dst_ref)

        def block(q, k, v, la, st):
            outs = [None] * GLA_BLOCK_CHUNKS
            chunks = range(GLA_BLOCK_CHUNKS)
            for ci in (reversed(chunks) if rev else chunks):
                rows = slice(ci * GLA_CHUNK, (ci + 1) * GLA_CHUNK)
                outs[ci], st = _f_gla_chunk(q[ci], k[ci], v[ci], la[ci], st, hk_ref[...], hv_ref[...], bd_ref[...],
                                            rev)
            return tuple(outs), st

        split = lambda r: tuple(r[ci * GLA_CHUNK:(ci + 1) * GLA_CHUNK, :].astype(F32)
                                for ci in range(GLA_BLOCK_CHUNKS))
        _, vjp = jax.vjp(block, split(q_ref), split(k_ref), split(v_ref), split(la_ref), sts_ref[0])
        dq, dk, dv, dla, dst = vjp((split(do_ref), dst_ref[...]))
        for ci in range(GLA_BLOCK_CHUNKS):
            rows = slice(ci * GLA_CHUNK, (ci + 1) * GLA_CHUNK)
            if n_prev:
                dq_ref[rows, :] = dq[ci] + prev_refs[0][rows, :]
                dk_ref[rows, :] = dk[ci] + prev_refs[1][rows, :]
                dv_ref[rows, :] = dv[ci] + prev_refs[2][rows, :]
            else:
                dq_ref[rows, :], dk_ref[rows, :], dv_ref[rows, :] = dq[ci], dk[ci], dv[ci]
            dla_ref[rows, :] = dla[ci]
        dst_ref[...] = dst
        dst0_ref[...] = dst

    full = lambda a: pl.BlockSpec(a.shape, lambda i: (0,) * a.ndim)
    blk = lambda w: pl.BlockSpec((tb, w), lambda i: (order(i), 0))
    prev_specs = [blk(GKW), blk(GKW), blk(GVW)] if n_prev else []
    return pl.pallas_call(
        body, name=name, grid=(nblk,),
        in_specs=_gla_specs(s, tb, order) + [pl.BlockSpec((1, GVW, GKW), lambda i: (order(i), 0, 0)), blk(GVW)]
        + [full(m) for m in masks] + prev_specs + [pl.BlockSpec(memory_space=pl.ANY)] * len(follow),
        out_specs=[blk(GKW), blk(GKW), blk(GVW), blk(GKW), pl.BlockSpec((GVW, GKW), lambda i: (0, 0))],
        out_shape=[jax.ShapeDtypeStruct((s, GKW), F32), jax.ShapeDtypeStruct((s, GKW), F32),
                   jax.ShapeDtypeStruct((s, GVW), F32), jax.ShapeDtypeStruct((s, GKW), F32),
                   jax.ShapeDtypeStruct((GVW, GKW), F32)],
        scratch_shapes=[pltpu.VMEM((GVW, GKW), F32)],
        compiler_params=_cp("arbitrary"),
    )(p, p, p, la, sts, do, *masks, *(prev or ()), *follow)


def _f_ctx_state(k, v, la_f, la_b, bd_t):
    c = k.shape[0]
    after = _nn_hi(_tri(c, True, strict=True).astype(F32), la_f)
    before = _nn_hi(_tri(c, False, strict=True).astype(F32), la_b)
    return bd_t * _tn(v, k * jnp.exp(after)), bd_t * _tn(v, k * jnp.exp(before))


def _ctx_state(pc, la_f, la_b):
    c = pc.shape[0]
    bd_t = _gla_masks()[2]

    def body(k_ref, v_ref, lf_ref, lb_ref, bd_ref, sf_ref, sb_ref):
        sf_ref[...], sb_ref[...] = _f_ctx_state(k_ref[...], v_ref[...], lf_ref[...], lb_ref[...], bd_ref[...])

    full = lambda a: pl.BlockSpec(a.shape, lambda i: (0, 0))
    return pl.pallas_call(
        body, name="ctx_state_fwd", grid=(1,),
        in_specs=[pl.BlockSpec((c, GKW), lambda i: (0, C_GK // GKW)), pl.BlockSpec((c, GVW), lambda i: (0, C_GV // GVW)),
                  full(la_f), full(la_b), full(bd_t)],
        out_specs=[pl.BlockSpec((GVW, GKW), lambda i: (0, 0))] * 2,
        out_shape=[jax.ShapeDtypeStruct((GVW, GKW), F32)] * 2,
        compiler_params=_cp("arbitrary"),
    )(pc, pc, la_f, la_b, bd_t)


def _ctx_state_bwd(pc, la_f, la_b, dsf, dsb):
    c = pc.shape[0]
    bd_t = _gla_masks()[2]

    def body(k_ref, v_ref, lf_ref, lb_ref, bd_ref, dsf_ref, dsb_ref, dk_ref, dv_ref, dlf_ref, dlb_ref):
        _, vjp = jax.vjp(lambda k, v, lf, lb: _f_ctx_state(k, v, lf, lb, bd_ref[...]),
                         k_ref[...], v_ref[...], lf_ref[...], lb_ref[...])
        dk, dv, dlf, dlb = vjp((dsf_ref[...], dsb_ref[...]))
        dk_ref[...], dv_ref[...] = dk.astype(BF16), dv.astype(BF16)
        dlf_ref[...], dlb_ref[...] = dlf, dlb

    full = lambda a: pl.BlockSpec(a.shape, lambda i: (0, 0))
    return pl.pallas_call(
        body, name="ctx_state_bwd", grid=(1,),
        in_specs=[pl.BlockSpec((c, GKW), lambda i: (0, C_GK // GKW)), pl.BlockSpec((c, GVW), lambda i: (0, C_GV // GVW)),
                  full(la_f), full(la_b), full(bd_t), full(dsf), full(dsb)],
        out_specs=[pl.BlockSpec((c, GKW), lambda i: (0, 0)), pl.BlockSpec((c, GVW), lambda i: (0, 0)),
                   pl.BlockSpec((c, GKW), lambda i: (0, 0)), pl.BlockSpec((c, GKW), lambda i: (0, 0))],
        out_shape=[jax.ShapeDtypeStruct((c, GKW), BF16), jax.ShapeDtypeStruct((c, GVW), BF16),
                   jax.ShapeDtypeStruct((c, GKW), F32), jax.ShapeDtypeStruct((c, GKW), F32)],
        compiler_params=_cp("arbitrary"),
    )(pc, pc, la_f, la_b, bd_t, dsf, dsb)


_SRC_COLS = ((0, QW), (QW + 2 * KVW + 2 * GKW, GVW), (QW + 2 * KVW + 2 * GKW + GVW, GVW), (QW, KVW), (QW + KVW, KVW),
             (QW + 2 * KVW, GKW), (QW + 2 * KVW + GKW, GKW), (IN_COLS - 2 * GATE_RANK, 2 * GATE_RANK))
_DST_COLS = (C_Q, C_GV, C_GG, C_K, C_V, C_GQ, C_GK, C_Z)


def _pack_w_in(w_in):
    parts = [w_in[:, s:s + n] for s, n in _SRC_COLS]
    parts.append(jnp.zeros((w_in.shape[0], IN_PAD - C_Z - 2 * GATE_RANK), w_in.dtype))
    return jnp.concatenate(parts, axis=1)


def _unpack_w_in_grad(g):
    by_src = sorted(zip(_SRC_COLS, _DST_COLS))
    return jnp.concatenate([g[:, d:d + n] for (_, n), d in by_src], axis=1)


def _prep_weights(w_in, w_gate_fwd, w_gate_bwd):
    pad_rows = lambda w, at: jnp.zeros((LANES, GKW), F32).at[at:at + GATE_RANK].set(w)
    return {"w_in": _pack_w_in(w_in).astype(BF16), "wg_f": pad_rows(w_gate_fwd, 0),
            "wg_b": pad_rows(w_gate_bwd, GATE_RANK)}


def _local_step(x, ctx, target, ada, ada_c, w, late_weights, reduce_behind=None, reduce_w_in=None):
    s, d = x.shape
    sh1, sc1, gt1, sh2, sc2, gt2 = [ada[:, i * d:(i + 1) * d] for i in range(6)]
    sh1c, sc1c = ada_c[:, :d], ada_c[:, d:2 * d]
    cos, sin = _rope_tables(s)
    gt = jnp.tile(w["g_gla_norm"], (1, GLA_HEADS))

    h = _norm_mod("pre_mix", x, w["g_pre_mix"], sh1, sc1)
    hc = _norm_mod("pre_mix_ctx", ctx, w["g_pre_mix"], sh1c, sc1c)
    p = _mm("proj_in", h, w["w_in"], "nn")
    pc = _mm("proj_in_ctx", hc, w["w_in"], "nn")
    q_rot, k_rot, v_b = _rope_fwd("rope", p, cos, sin)
    pad = ((BLOCK, BLOCK), (0, 0))
    kp, vp = jnp.pad(k_rot, pad), jnp.pad(v_b, pad)
    kc, vc = pc[:, C_K:C_K + KVW].astype(BF16), pc[:, C_V:C_V + KVW].astype(BF16)
    attn = _attn_fwd(q_rot, kp, vp, kc, vc, w["attn_sink"])
    gate_w = (w["wg_f"], w["wg_b"], w["b_gate_fwd"], w["b_gate_bwd"])
    la_f, la_b = _gate_fwd("gate", p, *gate_w)
    la_fc, la_bc = _gate_fwd("gate_ctx", pc, *gate_w)
    st_f0, st_b0 = _ctx_state(pc, la_fc, la_bc)
    o_f, sts_f = _gla_fwd("gla_fwd_f", p, la_f, st_f0, False)
    o_b, sts_b = _gla_fwd("gla_fwd_b", p, la_b, st_b0, True)
    mix = _gla_out("gla_out", attn, o_f, o_b, p, gt)
    w_out, w_ffn_in_t, w_ffn_out = late_weights(attn)
    y = _mm("proj_out", mix, w_out, "nn")
    x1, h2 = _post_res_norm_mod("post_mix_pre_ffn", x, y, w["g_post_mix"], gt1, w["g_pre_ffn"], sh2, sc2)
    u, a = _ffn_in_swiglu("ffn_in", h2, w_ffn_in_t)
    f = _mm("ffn_out", a, w_ffn_out, "nn")
    g = {}
    dx2, df, loss, g["g_post_ffn"], dgt2 = _post_res_loss("post_ffn_loss", x1, f, w["g_post_ffn"], gt2, target)

    g["w_ffn_out"] = _mm("ffn_out_dw", a, df, "tn")
    du = _ffn_out_dx_swiglu_bwd("ffn_out_dx", df, w_ffn_out, u)
    dh2 = _mm("ffn_in_dx", du, w_ffn_in_t, "nn")
    g["w_ffn_in_t"] = _mm("ffn_in_dw", du, h2, "tn")
    dx1, dy, g["g_pre_ffn"], dsh2, dsc2, g["g_post_mix"], dgt1 = _norm_mod_post_res_bwd(
        "pre_ffn_post_mix_bwd", dh2, dx2, x1, y, w["g_pre_ffn"], sh2, sc2, w["g_post_mix"], gt1)
    dmix = _mm("proj_out_dx", dy, w_out, "nt", BF16)
    g["w_out"] = _mm("proj_out_dw", mix, dy, "tn")
    rb, sink, token = reduce_behind, w["attn_sink"], None
    if rb is not None:
        gt = _behind(gt, rb.start(g["w_ffn_in_t"], g["w_ffn_out"], g["w_out"]))
    d_o, dgg, dgt = _gla_out_bwd("gla_out_bwd", dmix, o_f, o_b, p, gt)
    g["g_gla_norm"] = jnp.sum(dgt.reshape(GLA_HEADS, GLA_DV), axis=0, keepdims=True)
    if rb is not None:
        token = rb.pair(dgg)
    dgq, dgk, dgv, dla_f, dst_f0 = _gla_bwd("gla_bwd_f", p, la_f, sts_f, d_o, None, False, token)
    dgq, dgk, dgv, dla_b, dst_b0 = _gla_bwd("gla_bwd_b", p, la_b, sts_b, d_o, (dgq, dgk, dgv), True)
    if rb is not None:
        sink = _behind(sink, rb.total(dgq))
    dgkc, dgvc, dla_fc, dla_bc = _ctx_state_bwd(pc, la_fc, la_bc, dst_f0, dst_b0)
    dz, dwf, dwb, dbf, dbb = _gate_bwd("gate_bwd", p, dla_f, dla_b, *gate_w)
    dzc, dwfc, dwbc, dbfc, dbbc = _gate_bwd("gate_ctx_bwd", pc, dla_fc, dla_bc, *gate_w)
    g["w_gate_fwd"] = (dwf + dwfc)[:GATE_RANK]
    g["w_gate_bwd"] = (dwb + dwbc)[GATE_RANK:2 * GATE_RANK]
    g["b_gate_fwd"], g["b_gate_bwd"] = dbf + dbfc, dbb + dbbc
    dq_rot, dkp, dvp, dkc, dvc, g["attn_sink"] = _attn_bwd(dmix, q_rot, kp, vp, kc, vc, sink)
    if rb is not None:
        g["behind"] = rb.result(dq_rot)
    dq, dk = _rope_bwd("rope_bwd", dq_rot, dkp[BLOCK:BLOCK + s], cos, sin)
    dp = jnp.concatenate([dq, dgv.astype(BF16), dgg, dk, dvp[BLOCK:BLOCK + s].astype(BF16), dgq.astype(BF16),
                          dgk.astype(BF16), dz], axis=1)
    c_rows = ctx.shape[0]
    zeros = lambda n: jnp.zeros((c_rows, n), BF16)
    dpc = jnp.concatenate([zeros(QW), dgvc, zeros(GVW), dkc.astype(BF16), dvc.astype(BF16), zeros(GKW), dgkc, dzc],
                          axis=1)
    g["w_in"] = _mm("proj_in_dw", h, dp, "tn", init=_mm("proj_in_ctx_dw", hc, dpc, "tn"))
    token = None if reduce_w_in is None else reduce_w_in.start(g["w_in"])
    dh = _mm("proj_in_dx", dp, w["w_in"], "nt", after=token)
    dhc = _mm("proj_in_ctx_dx", dpc, w["w_in"], "nt")
    if reduce_w_in is not None:
        sh1 = _behind(sh1, reduce_w_in.pair(dh))
    dx, dg_a, dsh1, dsc1 = _norm_mod_bwd("pre_mix_bwd", dh, dx1, x, w["g_pre_mix"], sh1, sc1)
    if reduce_w_in is not None:
        dsh1 = _behind(dsh1, reduce_w_in.total(dx))
    _, dg_b, dsh1c, dsc1c = _norm_mod_bwd("pre_mix_ctx_bwd", dhc, jnp.zeros_like(dhc), ctx, w["g_pre_mix"], sh1c,
                                          sc1c)
    g["g_pre_mix"] = dg_a + dg_b
    d_ada = jnp.concatenate([dsh1, dsc1, dgt1, dsh2, dsc2, dgt2], axis=1)
    d_ada_c = jnp.concatenate([dsh1c, dsc1c, jnp.zeros((1, 4 * d), F32)], axis=1)
    return loss, dx, g, d_ada, d_ada_c


HBM = pl.BlockSpec(memory_space=pltpu.HBM)
N_DEV, N_CHIP = 8, 4


def _place():
    x, y, c = lax.axis_index("x"), lax.axis_index("y"), lax.axis_index("c")
    return x, y, c, [(1 - x, y), (x, 1 - y), (1 - x, 1 - y)]


def _row_tile(n, mult, cap):
    return max(t for t in range(mult, min(n, cap) + 1, mult) if n % t == 0)


def _ag_small(name, v, after=None):
    follow = () if after is None else (after,)

    def body(v_ref, *rest):
        out_ref, send_sems, recv_sems = rest[len(follow):]
        x, y, c, _ = _place()
        out_ref[4 * x + 2 * y + c] = v_ref[...]

        def peer(r):
            return ((1 - x) if r & 4 else x, (1 - y) if r & 2 else y, (1 - c) if r & 1 else c)

        def copy(r, block):
            px, py, pc = block
            return pltpu.make_async_remote_copy(
                src_ref=v_ref, dst_ref=out_ref.at[4 * px + 2 * py + pc], send_sem=send_sems.at[r - 1],
                recv_sem=recv_sems.at[r - 1], device_id=peer(r), device_id_type=MESH)

        sends = [copy(r, (x, y, c)) for r in range(1, N_DEV)]
        for cp in sends:
            cp.start()
        for r in range(1, N_DEV):
            copy(r, peer(r)).wait_recv()
        for cp in sends:
            cp.wait_send()

    return pl.pallas_call(
        body, name=name, out_shape=jax.ShapeDtypeStruct((N_DEV,) + v.shape, v.dtype),
        in_specs=[pl.BlockSpec(memory_space=pltpu.VMEM)] + [pl.BlockSpec(memory_space=pl.ANY)] * len(follow),
        out_specs=pl.BlockSpec(memory_space=pltpu.VMEM),
        scratch_shapes=[pltpu.SemaphoreType.DMA((N_DEV - 1,)), pltpu.SemaphoreType.DMA((N_DEV - 1,))],
    )(v, *follow)


def _halves(c, rows, mult):
    hr = rows // 2
    return pl.ds(pl.multiple_of(c * hr, mult), hr), pl.ds(pl.multiple_of((1 - c) * hr, mult), hr)


def _ag_shards(name, shard):
    rows = shard.shape[0]

    def body(w_ref, out_ref, send_sems, recv_sems, local_sem):
        x, y, c, chips = _place()
        mine_half, other_half = _halves(c, rows, 16)
        me = 2 * x + y
        mine = pltpu.make_async_copy(w_ref, out_ref.at[me], local_sem)
        mine.start()

        def copy(k, src, chip, half, to):
            return pltpu.make_async_remote_copy(
                src_ref=src, dst_ref=out_ref.at[chip, half], send_sem=send_sems.at[k], recv_sem=recv_sems.at[k],
                device_id=to, device_id_type=MESH)

        first = [copy(j, w_ref.at[mine_half], me, mine_half, (px, py, c)) for j, (px, py) in enumerate(chips)]
        for cp in first:
            cp.start()
        passed = []
        for j, (px, py) in enumerate(chips):
            pk = 2 * px + py
            copy(j, w_ref.at[mine_half], pk, mine_half, (px, py, c)).wait_recv()
            cp = copy(3 + j, out_ref.at[pk, mine_half], pk, mine_half, (x, y, 1 - c))
            cp.start()
            passed.append(cp)
        for j, (px, py) in enumerate(chips):
            copy(3 + j, w_ref.at[mine_half], 2 * px + py, other_half, (x, y, 1 - c)).wait_recv()
        for cp in first + passed:
            cp.wait_send()
        mine.wait()

    return pl.pallas_call(
        body, name=name, out_shape=jax.ShapeDtypeStruct((N_CHIP,) + shard.shape, shard.dtype),
        in_specs=[HBM], out_specs=HBM,
        scratch_shapes=[pltpu.SemaphoreType.DMA((6,)), pltpu.SemaphoreType.DMA((6,)), pltpu.SemaphoreType.DMA],
    )(shard)


def _swap_half(name, g):
    n_sh, rows, n = g.shape

    def body(g_ref, a_ref, send_sem, recv_sem):
        x, y, c, _ = _place()
        _, other_half = _halves(c, rows, 8)
        cp = pltpu.make_async_remote_copy(
            src_ref=g_ref.at[pl.ds(0, n_sh), other_half], dst_ref=a_ref, send_sem=send_sem, recv_sem=recv_sem,
            device_id=(x, y, 1 - c), device_id_type=MESH)
        cp.start()
        cp.wait()

    return pl.pallas_call(
        body, name=name, out_shape=jax.ShapeDtypeStruct((n_sh, rows // 2, n), g.dtype), in_specs=[HBM], out_specs=HBM,
        scratch_shapes=[pltpu.SemaphoreType.DMA, pltpu.SemaphoreType.DMA],
    )(g)


def _add_half(name, g, a, c_idx):
    n_sh, hr, n = a.shape
    tr = _row_tile(hr, 16, 1024)
    nb = hr // tr

    def body(c_ref, g_ref, a_ref, o_ref):
        o_ref[...] = (g_ref[...] + a_ref[...]).astype(o_ref.dtype)

    return pl.pallas_call(
        body, name=name, out_shape=jax.ShapeDtypeStruct(a.shape, BF16),
        grid_spec=pltpu.PrefetchScalarGridSpec(
            num_scalar_prefetch=1, grid=(n_sh, nb),
            in_specs=[pl.BlockSpec((1, tr, n), lambda s, i, c_ref: (s, c_ref[0] * nb + i, 0)),
                      pl.BlockSpec((1, tr, n), lambda s, i, c_ref: (s, i, 0))],
            out_specs=pl.BlockSpec((1, tr, n), lambda s, i, c_ref: (s, i, 0))),
        compiler_params=_cp("parallel", "parallel"),
    )(c_idx, g, a)


def _scatter_chips(name, h):
    def body(h_ref, b_ref, send_sems, recv_sems, local_sem):
        x, y, c, chips = _place()
        me = 2 * x + y
        mine = pltpu.make_async_copy(h_ref.at[me], b_ref.at[me], local_sem)
        mine.start()

        def copy(j, src_block, dst_block, to):
            return pltpu.make_async_remote_copy(
                src_ref=h_ref.at[src_block], dst_ref=b_ref.at[dst_block], send_sem=send_sems.at[j],
                recv_sem=recv_sems.at[j], device_id=to, device_id_type=MESH)

        sends = [copy(j, 2 * px + py, me, (px, py, c)) for j, (px, py) in enumerate(chips)]
        for cp in sends:
            cp.start()
        for j, (px, py) in enumerate(chips):
            copy(j, me, 2 * px + py, (px, py, c)).wait_recv()
        for cp in sends:
            cp.wait_send()
        mine.wait()

    return pl.pallas_call(
        body, name=name, out_shape=jax.ShapeDtypeStruct(h.shape, h.dtype), in_specs=[HBM], out_specs=HBM,
        scratch_shapes=[pltpu.SemaphoreType.DMA((3,)), pltpu.SemaphoreType.DMA((3,)), pltpu.SemaphoreType.DMA],
    )(h)


def _sum_chips(name, b):
    n_sh, hr, n = b.shape
    tr = _row_tile(hr, 16, 1024)

    def body(b0, b1, b2, b3, o_ref):
        o_ref[...] = ((b0[0].astype(F32) + b1[0].astype(F32)) + b2[0].astype(F32)) + b3[0].astype(F32)

    return pl.pallas_call(
        body, name=name, grid=(hr // tr,), out_shape=jax.ShapeDtypeStruct((hr, n), F32),
        in_specs=[pl.BlockSpec((1, tr, n), functools.partial(lambda i, k: (k, i, 0), k=k)) for k in range(n_sh)],
        out_specs=pl.BlockSpec((tr, n), lambda i: (i, 0)), compiler_params=_cp("parallel"),
    )(b, b, b, b)


def _share_half(name, f):
    hr, n = f.shape

    def body(f_ref, out_ref, send_sem, recv_sem, local_sem):
        x, y, c, _ = _place()
        mine_half, other_half = _halves(c, 2 * hr, 8)
        mine = pltpu.make_async_copy(f_ref, out_ref.at[mine_half], local_sem)
        mine.start()

        def copy(half):
            return pltpu.make_async_remote_copy(
                src_ref=f_ref, dst_ref=out_ref.at[half], send_sem=send_sem, recv_sem=recv_sem,
                device_id=(x, y, 1 - c), device_id_type=MESH)

        send = copy(mine_half)
        send.start()
        copy(other_half).wait_recv()
        send.wait_send()
        mine.wait()

    return pl.pallas_call(
        body, name=name, out_shape=jax.ShapeDtypeStruct((2 * hr, n), f.dtype), in_specs=[HBM], out_specs=HBM,
        scratch_shapes=[pltpu.SemaphoreType.DMA, pltpu.SemaphoreType.DMA, pltpu.SemaphoreType.DMA],
    )(f)


def _reduce_shards(name, g, c_idx):
    a = _swap_half(name + "_swap", g)
    h = _add_half(name + "_pair", g, a, c_idx)
    b = _scatter_chips(name + "_scatter", h)
    f = _sum_chips(name + "_sum", b)
    return _share_half(name + "_share", f)


SEM = pl.BlockSpec(memory_space=pltpu.SEMAPHORE)
ANY = pl.BlockSpec(memory_space=pl.ANY)
DATAFLOW = pltpu.SideEffectType.DATAFLOW_SIDE_EFFECTING


def _remote(src, dst, send_sems, recv_sems, k, to):
    return pltpu.make_async_remote_copy(src_ref=src, dst_ref=dst, send_sem=send_sems.at[k], recv_sem=recv_sems.at[k],
                                        device_id=to, device_id_type=MESH)


def _split_copy(name, src, land_shape, land_dtype, n, plan, after=None):
    after = jnp.zeros((8, LANES), F32) if after is None else after

    def start_body(src_ref, land_ref, after_ref, send_sems, recv_sems, src_thru, land_thru, token):
        for cp in plan(src_ref, land_ref, send_sems, recv_sems)[0]:
            cp.start()
        token[...] = jnp.zeros_like(token)

    sems = pltpu.SemaphoreType.DMA((n,))
    send_sems, recv_sems, src_thru, land_thru, token = pl.pallas_call(
        start_body, name=name + "_start",
        out_shape=(sems, sems, pltpu.HBM(src.shape, src.dtype), pltpu.HBM(land_shape, land_dtype),
                   jax.ShapeDtypeStruct((8, LANES), F32)),
        in_specs=(HBM, HBM, ANY), out_specs=(SEM, SEM, HBM, HBM, pl.BlockSpec(memory_space=pltpu.VMEM)),
        input_output_aliases={0: 2, 1: 3}, compiler_params=pltpu.CompilerParams(has_side_effects=DATAFLOW),
    )(pltpu.with_memory_space_constraint(src, pltpu.HBM),
      pltpu.with_memory_space_constraint(lax.empty(land_shape, land_dtype), pltpu.HBM), after)

    def wait(after):
        def wait_body(src_ref, land_ref, send_sems, recv_sems, after_ref, src_out, land_out):
            sent, received = plan(src_ref, land_ref, send_sems, recv_sems)
            for cp in sent:
                cp.wait_send()
            for cp in received:
                cp.wait_recv()

        return pl.pallas_call(
            wait_body, name=name + "_wait",
            out_shape=(pltpu.HBM(src.shape, src.dtype), pltpu.HBM(land_shape, land_dtype)),
            in_specs=(HBM, HBM, SEM, SEM, ANY), out_specs=(HBM, HBM), input_output_aliases={0: 0, 1: 1},
            compiler_params=pltpu.CompilerParams(has_side_effects=DATAFLOW),
        )(src_thru, land_thru, send_sems, recv_sems, after)

    return token, wait


def _behind(x, token):
    return x + token[0, 0]


def _plan_gather(src_ref, land_ref, send_sems, recv_sems):
    x, y, c, chips = _place()
    sent = [_remote(src_ref, land_ref.at[2 * x + y], send_sems, recv_sems, j, (px, py, c))
            for j, (px, py) in enumerate(chips)]
    received = [_remote(src_ref, land_ref.at[2 * px + py], send_sems, recv_sems, j, (px, py, c))
                for j, (px, py) in enumerate(chips)]
    return sent, received


def _plan_swap(src_ref, land_ref, send_sems, recv_sems):
    x, y, c, _ = _place()
    _, other_half = _halves(c, src_ref.shape[1], 8)
    cp = _remote(src_ref.at[pl.ds(0, src_ref.shape[0]), other_half], land_ref, send_sems, recv_sems, 0, (x, y, 1 - c))
    return [cp], [cp]


def _plan_scatter(src_ref, land_ref, send_sems, recv_sems):
    x, y, c, chips = _place()
    sent = [_remote(src_ref.at[2 * px + py], land_ref.at[2 * x + y], send_sems, recv_sems, j, (px, py, c))
            for j, (px, py) in enumerate(chips)]
    received = [_remote(src_ref.at[2 * px + py], land_ref.at[2 * px + py], send_sems, recv_sems, j, (px, py, c))
                for j, (px, py) in enumerate(chips)]
    return sent, received


def _plan_share(src_ref, land_ref, send_sems, recv_sems):
    x, y, c, _ = _place()
    mine_half, other_half = _halves(c, land_ref.shape[0], 8)
    return ([_remote(src_ref, land_ref.at[mine_half], send_sems, recv_sems, 0, (x, y, 1 - c))],
            [_remote(src_ref, land_ref.at[other_half], send_sems, recv_sems, 0, (x, y, 1 - c))])


def _pack_shard_rows(name, parts):
    rows = [t.shape[0] // N_CHIP for t in parts]
    n, total = parts[0].shape[1], sum(t.shape[0] // N_CHIP for t in parts)
    slab, at = None, 0
    for i, (t, r) in enumerate(zip(parts, rows)):
        tr = max(c for c in range(8, min(r, 512) + 1, 8) if r % c == 0 and at % c == 0)
        nb, ob = r // tr, at // tr

        def body(t_ref, *rest):
            rest[-1][0] = t_ref[...]

        slab = pl.pallas_call(
            body, name=f"{name}_{i}", grid=(N_CHIP, nb), out_shape=jax.ShapeDtypeStruct((N_CHIP, total, n), t.dtype),
            in_specs=[pl.BlockSpec((tr, n), functools.partial(lambda k, j, nb: (k * nb + j, 0), nb=nb))]
            + ([] if slab is None else [pl.BlockSpec(memory_space=pl.ANY)]),
            out_specs=pl.BlockSpec((1, tr, n), functools.partial(lambda k, j, ob: (k, ob + j, 0), ob=ob)),
            input_output_aliases={} if slab is None else {1: 0}, compiler_params=_cp("parallel", "parallel"),
        )(*((t,) if slab is None else (t, slab)))
        at += r
    return slab


class _GatherBehind:
    def __init__(self, name, shard, chip, after=None):
        self.chip = chip
        self.token, self.wait = _split_copy(name, shard, (N_CHIP,) + shard.shape, shard.dtype, 3, _plan_gather,
                                            after)

    def result(self, after):
        shard, land = self.wait(after)
        return lax.dynamic_update_slice(land, shard[None], (self.chip, 0, 0))


class _ReduceBehind:
    def __init__(self, name, chip, c, c_idx):
        self.name, self.chip, self.c, self.c_idx = name, chip, c, c_idx

    def start(self, *grads):
        return self.start_slab(_pack_shard_rows(self.name + "_pack", grads))

    def start_slab(self, g):
        n_sh, rows, n = g.shape
        token, self.wait = _split_copy(self.name + "_swap", g, (n_sh, rows // 2, n), g.dtype, 1, _plan_swap)
        return token

    def pair(self, after):
        g, a = self.wait(after)
        h = _add_half(self.name + "_pair", g, a, self.c_idx)
        token, self.wait = _split_copy(self.name + "_scatter", h, h.shape, h.dtype, 3, _plan_scatter)
        return token

    def total(self, after):
        h, b = self.wait(after)
        b = lax.dynamic_update_slice(b, lax.dynamic_slice_in_dim(h, self.chip, 1, axis=0), (self.chip, 0, 0))
        f = _sum_chips(self.name + "_sum", b)
        token, self.wait = _split_copy(self.name + "_share", f, (2 * f.shape[0], f.shape[1]), f.dtype, 1,
                                       _plan_share)
        return token

    def result(self, after):
        f, out = self.wait(after)
        return lax.dynamic_update_slice(out, f, (self.c * f.shape[0], 0))


class _ReduceColsBehind(_ReduceBehind):
    def start(self, g_padded):
        g = _unpack_w_in_grad(g_padded)
        n = g.shape[1] // N_CHIP
        return self.start_slab(jnp.stack([g[:, k * n:(k + 1) * n] for k in range(N_CHIP)]))


def _f_adamw(w, g, m, v):
    m = ADAM_B1 * m + (1.0 - ADAM_B1) * g
    v = ADAM_B2 * v + (1.0 - ADAM_B2) * (g * g)
    m_hat = m / (1.0 - ADAM_B1 ** ADAM_STEP)
    v_hat = v / (1.0 - ADAM_B2 ** ADAM_STEP)
    return -ADAM_LR * (m_hat / (jnp.sqrt(v_hat) + ADAM_EPS) + ADAM_WD * w), m, v


def _adamw(name, w, g, m, v):
    rows, n = w.shape
    return _rowwise(name, lambda w, g, m, v: (_f_adamw(w, g, m, v), ()), rows, [(t, n, 0) for t in (w, g, m, v)], [],
                    [(n, F32)] * 3, [], tm=_row_tile(rows, 8, 256))


def _pack_rows(parts):
    rows = []
    for t in parts:
        t = t.reshape(-1)
        rows.append(jnp.pad(t, (0, -t.shape[0] % LANES)).reshape(-1, LANES))
    out = jnp.concatenate(rows, axis=0)
    return jnp.pad(out, ((0, -out.shape[0] % 8), (0, 0)))


def _unpack_rows(packed, shapes):
    out, r = [], 0
    for shp in shapes:
        n = int(np.prod(shp))
        nr = -(-n // LANES)
        out.append(packed[r:r + nr].reshape(-1)[:n].reshape(shp))
        r += nr
    return out


def _sum_blocks(name, g):
    def body(g_ref, o_ref):
        acc = g_ref[0]
        for k in range(1, g.shape[0]):
            acc = acc + g_ref[k]
        o_ref[...] = acc

    return pl.pallas_call(body, name=name, out_shape=jax.ShapeDtypeStruct(g.shape[1:], F32))(g)


def _silu(t):
    return t * _sigmoid(t)


def _ada_fwd(cc, w_ada):
    n = w_ada.shape[1]
    tn = _row_tile(n, LANES, 512)

    def body(cc_ref, w_ref, o_ref):
        o_ref[...] = _nn(_silu(cc_ref[...]), w_ref[...])

    return pl.pallas_call(
        body, name="ada_fwd", grid=(n // tn,), out_shape=jax.ShapeDtypeStruct((cc.shape[0], n), F32),
        in_specs=[pl.BlockSpec(cc.shape, lambda j: (0, 0)), pl.BlockSpec((w_ada.shape[0], tn), lambda j: (0, j))],
        out_specs=pl.BlockSpec((cc.shape[0], tn), lambda j: (0, j)), compiler_params=_cp("parallel"),
    )(cc, w_ada)


def _ada_bwd(cc, dm, w_ada):
    d, n = w_ada.shape
    tn = _row_tile(n, LANES, 512)

    def body(cc_ref, dm_ref, w_ref, gw_ref, ds_ref):
        @pl.when(pl.program_id(0) == 0)
        def _():
            ds_ref[...] = jnp.zeros_like(ds_ref)

        gw_ref[...] = _raw_dot("tn", _silu(cc_ref[...]), dm_ref[...], True)
        ds_ref[...] += _raw_dot("nt", dm_ref[...], w_ref[...], False)

    return pl.pallas_call(
        body, name="ada_bwd", grid=(n // tn,),
        out_shape=[jax.ShapeDtypeStruct((d, n), F32), jax.ShapeDtypeStruct(cc.shape, F32)],
        in_specs=[pl.BlockSpec(cc.shape, lambda j: (0, 0)), pl.BlockSpec((cc.shape[0], tn), lambda j: (0, j)),
                  pl.BlockSpec((d, tn), lambda j: (0, j))],
        out_specs=[pl.BlockSpec((d, tn), lambda j: (0, j)), pl.BlockSpec(cc.shape, lambda j: (0, 0))],
        compiler_params=_cp("arbitrary"),
    )(cc, dm, w_ada)


def _c_ctx_grad(parts, c_ctx):
    def body(p_ref, c_ref, o_ref):
        ds = ((p_ref[0] + p_ref[1]) + p_ref[2]) + p_ref[3]
        _, vjp = jax.vjp(_silu, c_ref[...])
        o_ref[...] = vjp(ds)[0]

    return pl.pallas_call(body, name="c_ctx_grad", out_shape=jax.ShapeDtypeStruct(c_ctx.shape, F32))(parts, c_ctx)


def kernel(x, c, ctx, c_ctx, w_ada, b_ada, g_pre_mix, g_post_mix, g_pre_ffn, g_post_ffn, w_in, attn_sink, w_gate_fwd, b_gate_fwd, w_gate_bwd, b_gate_bwd, g_gla_norm, w_out, w_ffn_in, w_ffn_out, loss_target, m_c_ctx, m_w_ada, m_b_ada, m_g_pre_mix, m_g_post_mix, m_g_pre_ffn, m_g_post_ffn, m_w_in, m_attn_sink, m_w_gate_fwd, m_b_gate_fwd, m_w_gate_bwd, m_b_gate_bwd, m_g_gla_norm, m_w_out, m_w_ffn_in, m_w_ffn_out, v_c_ctx, v_w_ada, v_b_ada, v_g_pre_mix, v_g_post_mix, v_g_pre_ffn, v_g_post_ffn, v_w_in, v_attn_sink, v_w_gate_fwd, v_b_gate_fwd, v_w_gate_bwd, v_b_gate_bwd, v_g_gla_norm, v_w_out, v_w_ffn_in, v_w_ffn_out):
    xi, yi, ci = lax.axis_index("x"), lax.axis_index("y"), lax.axis_index("c")
    dev, chip = 4 * xi + 2 * yi + ci, 2 * xi + yi
    c_idx = jnp.reshape(ci, (1,)).astype(jnp.int32)
    d = x.shape[-1]
    n_ada, n_in, n_f = w_ada.shape[-1], w_in.shape[-1], w_ffn_in.shape[-1]
    r_out, r_f = w_out.shape[1], w_ffn_out.shape[1]
    n_gate = w_gate_fwd.shape[-1]
    by_chip = lambda t: t[0::2]

    w_in_g = _ag_shards("gather_w_in", w_in[0].astype(BF16))

    rc = -(-d // LANES)
    g1 = _ag_small("gather_cond", _pack_rows([c[0], w_gate_fwd[0], w_gate_bwd[0]]), w_in_g)
    c_all = g1[:, :rc].reshape(N_DEV, -1)[:, :d]
    gr = GATE_RANK * n_gate // LANES
    gate_full = lambda off: jnp.transpose(by_chip(g1)[:, off:off + gr].reshape(N_CHIP, GATE_RANK, n_gate),
                                          (1, 0, 2)).reshape(GATE_RANK, N_CHIP * n_gate)
    wgf, wgb = gate_full(rc), gate_full(rc + gr)
    cc = jnp.concatenate([c_all, c_ctx[None, :], jnp.zeros((7, d), F32)], axis=0)

    g2 = _ag_small("gather_ada", _ada_fwd(cc, w_ada[0]).reshape(-1, LANES))
    ada_all = jnp.transpose(by_chip(g2).reshape(N_CHIP, 16, n_ada), (1, 0, 2)).reshape(16, N_CHIP * n_ada) + b_ada
    late = _GatherBehind("gather_late", jnp.concatenate(
        [w_out[0], w_ffn_out[0], jnp.transpose(w_ffn_in[0])], axis=0).astype(BF16), chip, g2)

    def late_weights(after):
        t = late.result(after)
        r1, r2 = r_out, r_out + r_f
        return (t[:, :r1].reshape(N_CHIP * r_out, d), t[:, r2:].reshape(N_CHIP * n_f, d),
                t[:, r1:r2].reshape(N_CHIP * r_f, d))

    ada_all = _behind(ada_all, late.token)
    ada = lax.dynamic_slice(ada_all, (dev, 0), (1, N_CHIP * n_ada))
    ada_c = ada_all[N_DEV:N_DEV + 1]

    w = _prep_weights(jnp.concatenate([w_in_g[k] for k in range(N_CHIP)], axis=1), wgf, wgb)
    w.update(g_pre_mix=g_pre_mix, g_post_mix=g_post_mix, g_pre_ffn=g_pre_ffn, g_post_ffn=g_post_ffn,
             attn_sink=attn_sink, b_gate_fwd=b_gate_fwd, b_gate_bwd=b_gate_bwd, g_gla_norm=g_gla_norm)

    reduce_behind = _ReduceBehind("reduce_late", chip, ci, c_idx)
    reduce_w_in = _ReduceColsBehind("reduce_w_in", chip, ci, c_idx)
    loss_lanes, grad_x, g, d_ada, d_ada_c = _local_step(x[0], ctx[0], loss_target[0], ada, ada_c, w, late_weights,
                                                        reduce_behind, reduce_w_in)

    small = ("g_pre_mix", "g_post_mix", "g_pre_ffn", "g_post_ffn", "attn_sink", "b_gate_fwd", "b_gate_bwd",
             "g_gla_norm", "w_gate_fwd", "w_gate_bwd")
    shapes = [(1, 6 * d)] * 2 + [g[n].shape for n in small] + [(1, LANES)]
    g3 = _ag_small("gather_small_grads", _pack_rows([d_ada, d_ada_c] + [g[n] for n in small] + [loss_lanes]))
    tot = dict(zip(("d_ada", "d_ada_c") + small + ("loss",),
                   _unpack_rows(_sum_blocks("sum_small_grads", g3), shapes)))
    r_ada = 6 * d // LANES
    dm = jnp.concatenate([g3[:, :r_ada].reshape(N_DEV, 6 * d), tot["d_ada_c"], jnp.zeros((7, 6 * d), F32)], axis=0)
    grads = {n: tot[n] for n in small[:8]}
    grads["b_ada"] = _sum_blocks("sum_b_ada", dm.reshape(16, r_ada, LANES)).reshape(1, 6 * d)
    grads["w_gate_fwd"] = lax.dynamic_slice(tot["w_gate_fwd"], (0, chip * n_gate), (GATE_RANK, n_gate))[None]
    grads["w_gate_bwd"] = lax.dynamic_slice(tot["w_gate_bwd"], (0, chip * n_gate), (GATE_RANK, n_gate))[None]
    gw_ada, dsc = _ada_bwd(cc, lax.dynamic_slice(dm, (0, chip * n_ada), (16, n_ada)), w_ada[0])
    grads["w_ada"] = gw_ada[None]
    g4 = _ag_small("gather_c_ctx", _pack_rows([dsc[N_DEV]]))
    grads["c_ctx"] = _c_ctx_grad(by_chip(g4), _pack_rows([c_ctx])).reshape(-1)[:d]

    grads["w_in"] = reduce_w_in.result(g4)[None]
    behind = g["behind"]
    grads["w_ffn_in"] = jnp.transpose(behind[:n_f])[None]
    grads["w_ffn_out"], grads["w_out"] = behind[None, n_f:n_f + r_f], behind[None, n_f + r_f:]

    names = ("c_ctx", "w_ada", "b_ada", "g_pre_mix", "g_post_mix", "g_pre_ffn", "g_post_ffn", "w_in", "attn_sink",
             "w_gate_fwd", "b_gate_fwd", "w_gate_bwd", "b_gate_bwd", "g_gla_norm", "w_out", "w_ffn_in", "w_ffn_out")
    weights = dict(zip(names, (c_ctx, w_ada, b_ada, g_pre_mix, g_post_mix, g_pre_ffn, g_post_ffn, w_in, attn_sink,
                               w_gate_fwd, b_gate_fwd, w_gate_bwd, b_gate_bwd, g_gla_norm, w_out, w_ffn_in,
                               w_ffn_out)))
    m_in = dict(zip(names, (m_c_ctx, m_w_ada, m_b_ada, m_g_pre_mix, m_g_post_mix, m_g_pre_ffn, m_g_post_ffn, m_w_in,
                            m_attn_sink, m_w_gate_fwd, m_b_gate_fwd, m_w_gate_bwd, m_b_gate_bwd, m_g_gla_norm,
                            m_w_out, m_w_ffn_in, m_w_ffn_out)))
    v_in = dict(zip(names, (v_c_ctx, v_w_ada, v_b_ada, v_g_pre_mix, v_g_post_mix, v_g_pre_ffn, v_g_post_ffn, v_w_in,
                            v_attn_sink, v_w_gate_fwd, v_b_gate_fwd, v_w_gate_bwd, v_b_gate_bwd, v_g_gla_norm,
                            v_w_out, v_w_ffn_in, v_w_ffn_out)))
    large = ("w_ada", "w_in", "w_out", "w_ffn_in", "w_ffn_out")
    tiny = tuple(n for n in names if n not in large)
    delta, new_m, new_v = {}, {}, {}
    for n in large:
        dl, nm, nv = _adamw("adamw_" + n, weights[n][0], grads[n][0], m_in[n][0], v_in[n][0])
        delta[n], new_m[n], new_v[n] = dl[None], nm[None], nv[None]
    tiny_shapes = [weights[n].shape for n in tiny]
    packed = [_pack_rows([t[n] for n in tiny]) for t in (weights, grads, m_in, v_in)]
    for out, res in zip((delta, new_m, new_v), _adamw("adamw_small", *packed)):
        out.update(zip(tiny, _unpack_rows(res, tiny_shapes)))
    for n in tiny:
        grads[n] = grads[n].reshape(weights[n].shape)

    return (tot["loss"][0, 0], grad_x[None], *[grads[n] for n in names], *[delta[n] for n in names], *[new_m[n] for n in names],
            *[new_v[n] for n in names])
```

```python
import functools

import jax
import jax.numpy as jnp
import numpy as np
from jax import lax
from jax.experimental import pallas as pl
from jax.experimental.pallas import tpu as pltpu

F32 = jnp.float32
BF16 = jnp.bfloat16
MESH = pl.DeviceIdType.MESH

HEAD_DIM = 64
ATT_HEADS = 8
ATT_KV_HEADS = 2
ATT_GROUP = ATT_HEADS // ATT_KV_HEADS
WINDOW = 128
BLOCK = 128
GRID_W = 64
ROPE_BASE = 10000.0
GLA_HEADS = 8
GLA_DK = 32
GLA_DV = 64
GLA_CHUNK = 64
GATE_RANK = 16
GATE_TAU = 16.0
NEG_INF = -1e30
QW = ATT_HEADS * HEAD_DIM
KVW = ATT_KV_HEADS * HEAD_DIM
GKW = GLA_HEADS * GLA_DK
GVW = GLA_HEADS * GLA_DV
IN_COLS = QW + 2 * KVW + 2 * GKW + 2 * GVW + 2 * GATE_RANK
LANES = 128
IN_PAD = IN_COLS + LANES - 2 * GATE_RANK
C_Q, C_GV, C_GG = 0, QW, QW + GVW
C_K = C_GG + GVW
C_V = C_K + KVW
C_GQ = C_V + KVW
C_GK = C_GQ + GKW
C_Z = C_GK + GKW
MIX = QW + GVW

ADAM_LR, ADAM_B1, ADAM_B2, ADAM_EPS, ADAM_WD, ADAM_STEP = 0.001, 0.9, 0.999, 1e-08, 0.01, 10

VMEM_LIMIT = 56 * 1024 * 1024


def _cp(*sem):
    return pltpu.CompilerParams(dimension_semantics=sem, vmem_limit_bytes=VMEM_LIMIT)


def _pick(n, cands):
    for t in cands:
        if n % t == 0:
            return t
    return n


_DIMS = {"nn": (((1,), (0,)), ((), ())), "nt": (((1,), (1,)), ((), ())), "tn": (((0,), (0,)), ((), ()))}


def _raw_dot(mode, a, b, hi):
    dot = lambda u, v: lax.dot_general(u, v, _DIMS[mode], preferred_element_type=F32)
    if hi:
        a, b = a.astype(F32), b.astype(F32)
        a_hi, b_hi = a.astype(BF16), b.astype(BF16)
        a_lo, b_lo = (a - a_hi.astype(F32)).astype(BF16), (b - b_hi.astype(F32)).astype(BF16)
        return dot(a_hi, b_hi) + (dot(a_lo, b_hi) + dot(a_hi, b_lo))
    return dot(a.astype(BF16), b.astype(BF16))


def _make_dot(mode, hi):
    @jax.custom_vjp
    def dot(a, b):
        return _raw_dot(mode, a, b, hi)

    def fwd(a, b):
        return _raw_dot(mode, a, b, hi), (a, b)

    def bwd(res, dc):
        a, b = res
        if mode == "nn":
            return _raw_dot("nt", dc, b, hi), _raw_dot("tn", a, dc, hi)
        if mode == "nt":
            return _raw_dot("nn", dc, b, hi), _raw_dot("tn", dc, a, hi)
        return _raw_dot("nt", b, dc, hi), _raw_dot("nn", a, dc, hi)

    dot.defvjp(fwd, bwd)
    return dot


_nn, _nt, _tn = _make_dot("nn", False), _make_dot("nt", False), _make_dot("tn", False)
_nn_hi = _make_dot("nn", True)


MM_VMEM_BUDGET = 44 * 1024 * 1024


def _halvings(n):
    out = [n]
    while out[-1] % (2 * LANES) == 0:
        out.append(out[-1] // 2)
    return out


def _mm_tiles(mode, m, n, k, a_bytes, b_bytes, o_bytes, init_bytes=0):
    tms = [t for t in dict.fromkeys((m, m // 2, m // 4, 2048, 1024, 512, 256, 128))
           if m % t == 0 and t % (LANES if mode == "tn" else 16) == 0 and t <= 4096] or [m]
    if mode == "tn":
        fits = [(k // tk + 0.5 * (m // tm), tm, tk)
                for tk in (4096, 2048, 1024, 512, 256, 128) if k % tk == 0 for tm in tms
                if 2 * (tk * tm * a_bytes + tk * n * b_bytes + tm * n * (o_bytes + init_bytes)) <= MM_VMEM_BUDGET]
        if fits:
            _, tm, tk = min(fits)
            return tm, n, tk
    tks = ([t for t in (512, 256, 128) if k % t == 0] or [k]) if mode == "tn" else _halvings(k)
    for tn in _halvings(n):
        for tk in tks:
            for tm in tms:
                acc = tm * tn * 4 if (k // tk > 1 and o_bytes != 4) else 0
                tiles = tm * tk * a_bytes + tk * tn * b_bytes + tm * tn * (o_bytes + init_bytes)
                if 2 * tiles + acc <= MM_VMEM_BUDGET:
                    return tm, tn, tk
    return tms[-1], _halvings(n)[-1], tks[-1]


def _mm(name, a, b, mode, out_dtype=F32, init=None, after=None):
    follow = () if after is None else (after,)
    if mode == "nn":
        (m, k), n = a.shape, b.shape[1]
    elif mode == "nt":
        (m, k), n = a.shape, b.shape[0]
    else:
        (k, m), n = a.shape, b.shape[1]
    tm, tn, tk = _mm_tiles(mode, m, n, k, a.dtype.itemsize, b.dtype.itemsize, jnp.dtype(out_dtype).itemsize,
                           0 if init is None else 4)
    nk = k // tk
    use_acc = nk > 1 and out_dtype != F32

    inits = () if init is None else (init,)

    def body(a_ref, b_ref, *rest):
        rest = rest[:len(inits)] + rest[len(inits) + len(follow):]
        o_ref, acc = rest[len(inits)], rest[len(inits) + 1:]
        part = _raw_dot(mode, a_ref[...], b_ref[...], False)
        first = lambda: part + rest[0][...] if inits else part
        if nk == 1:
            o_ref[...] = first().astype(o_ref.dtype)
            return
        acc_ref = acc[0] if use_acc else o_ref
        kk = pl.program_id(2)

        @pl.when(kk == 0)
        def _():
            acc_ref[...] = first()

        @pl.when(kk > 0)
        def _():
            acc_ref[...] += part

        if use_acc:
            @pl.when(kk == nk - 1)
            def _():
                o_ref[...] = acc_ref[...].astype(o_ref.dtype)

    if mode == "nn":
        a_spec = pl.BlockSpec((tm, tk), lambda i, j, kk: (i, kk))
        b_spec = pl.BlockSpec((tk, tn), lambda i, j, kk: (kk, j))
    elif mode == "nt":
        a_spec = pl.BlockSpec((tm, tk), lambda i, j, kk: (i, kk))
        b_spec = pl.BlockSpec((tn, tk), lambda i, j, kk: (j, kk))
    else:
        a_spec = pl.BlockSpec((tk, tm), lambda i, j, kk: (kk, i))
        b_spec = pl.BlockSpec((tk, tn), lambda i, j, kk: (kk, j))
    return pl.pallas_call(
        body, name=name, grid=(m // tm, n // tn, nk),
        in_specs=[a_spec, b_spec] + [pl.BlockSpec((tm, tn), lambda i, j, kk: (i, j))] * len(inits)
        + [pl.BlockSpec(memory_space=pl.ANY)] * len(follow),
        out_specs=pl.BlockSpec((tm, tn), lambda i, j, kk: (i, j)),
        out_shape=jax.ShapeDtypeStruct((m, n), out_dtype),
        scratch_shapes=[pltpu.VMEM((tm, tn), F32)] if use_acc else [],
        compiler_params=_cp("parallel", "parallel", "arbitrary"),
    )(a, b, *inits, *follow)


def _rowwise(name, fn, rows, row_ins, full_ins, row_outs, acc_outs, tm=None):
    tm = tm or _pick(rows, (512, 256, 128))
    n_r, n_f, n_o, n_a = len(row_ins), len(full_ins), len(row_outs), len(acc_outs)

    def body(*refs):
        ins, outs = refs[:n_r + n_f], refs[n_r + n_f:]
        vals = [r[...].astype(F32) for r in ins]
        ro, ao = fn(*vals)
        for r, val in zip(outs[:n_o], ro):
            r[...] = val.astype(r.dtype)
        if n_a:
            @pl.when(pl.program_id(0) == 0)
            def _():
                for r in outs[n_o:]:
                    r[...] = jnp.zeros_like(r)

            for r, val in zip(outs[n_o:], ao):
                r[...] += val

    in_specs = [pl.BlockSpec((tm, w), functools.partial(lambda i, cb: (i, cb), cb=cb)) for _, w, cb in row_ins]
    in_specs += [pl.BlockSpec(a.shape, lambda i: (0, 0)) for a in full_ins]
    out_specs = [pl.BlockSpec((tm, w), lambda i: (i, 0)) for w, _ in row_outs]
    out_specs += [pl.BlockSpec(s, lambda i: (0, 0)) for s in acc_outs]
    out_shape = [jax.ShapeDtypeStruct((rows, w), dt) for w, dt in row_outs]
    out_shape += [jax.ShapeDtypeStruct(s, F32) for s in acc_outs]
    return pl.pallas_call(
        body, name=name, grid=(rows // tm,), in_specs=in_specs, out_specs=out_specs, out_shape=out_shape,
        compiler_params=_cp("arbitrary" if n_a else "parallel"),
    )(*[a for a, _, _ in row_ins], *full_ins)


def _rn(x):
    return x * lax.rsqrt(jnp.mean(x * x, axis=-1, keepdims=True) + 1e-6)


def _sigmoid(t):
    return 1.0 / (1.0 + jnp.exp(-t))


def _f_norm_mod(x, g, sh, sc):
    return _rn(x) * g * (1.0 + sc) + sh


def _f_post_res(xr, y, g, gate):
    return xr + gate * (_rn(y) * g)


def _f_swiglu(g, u):
    return g * _sigmoid(g) * u


def _logsig(u):
    return jnp.minimum(u, 0.0) - jnp.log(1.0 + jnp.exp(-jnp.abs(u)))


def _f_gate(z, wf, wb, bf, bb):
    return _logsig(_nn(z, wf) + bf) / GATE_TAU, _logsig(_nn(z, wb) + bb) / GATE_TAU


def _f_gla_out(of, ob, gg, gt, bd):
    o = of + ob
    ms = _nn_hi(o * o, bd)
    return o * lax.rsqrt(ms + 1e-6) * gt * (gg * _sigmoid(gg))


def _norm_mod(name, x, g, sh, sc):
    rows, d = x.shape
    return _rowwise(name, lambda x, g, sh, sc: ((_f_norm_mod(x, g, sh, sc),), ()), rows,
                    [(x, d, 0)], [g, sh, sc], [(d, BF16)], [])[0]


def _norm_mod_bwd(name, dh, dres, x, g, sh, sc):
    rows, d = x.shape

    def fn(dh, dres, x, g, sh, sc):
        _, vjp = jax.vjp(_f_norm_mod, x, g, sh, sc)
        dx, dg, dsh, dsc = vjp(dh)
        return (dx + dres,), (dg, dsh, dsc)

    return _rowwise(name, fn, rows, [(dh, d, 0), (dres, d, 0), (x, d, 0)], [g, sh, sc], [(d, F32)],
                    [(1, d)] * 3)


def _post_res(name, xr, y, g, gate):
    rows, d = xr.shape
    return _rowwise(name, lambda xr, y, g, gate: ((_f_post_res(xr, y, g, gate),), ()), rows,
                    [(xr, d, 0), (y, d, 0)], [g, gate], [(d, F32)], [])[0]


def _post_res_bwd(name, dxo, y, g, gate):
    rows, d = y.shape

    def fn(dxo, y, g, gate):
        _, vjp = jax.vjp(lambda y, g, gate: _f_post_res(jnp.zeros_like(y), y, g, gate), y, g, gate)
        dy, dg, dgate = vjp(dxo)
        return (dy,), (dg, dgate)

    return _rowwise(name, fn, rows, [(dxo, d, 0), (y, d, 0)], [g, gate], [(d, BF16)], [(1, d)] * 2)


def _post_res_norm_mod(name, xr, y, g_post, gate, g_pre, sh, sc):
    rows, d = xr.shape

    def fn(xr, y, g_post, gate, g_pre, sh, sc):
        x1 = _f_post_res(xr, y, g_post, gate)
        return (x1, _f_norm_mod(x1, g_pre, sh, sc)), ()

    return _rowwise(name, fn, rows, [(xr, d, 0), (y, d, 0)], [g_post, gate, g_pre, sh, sc], [(d, F32), (d, BF16)], [])


def _norm_mod_post_res_bwd(name, dh, dres, x1, y, g_pre, sh, sc, g_post, gate):
    rows, d = x1.shape

    def fn(dh, dres, x1, y, g_pre, sh, sc, g_post, gate):
        _, vjp_norm = jax.vjp(_f_norm_mod, x1, g_pre, sh, sc)
        dx1, dg_pre, dsh, dsc = vjp_norm(dh)
        dx1 = dx1 + dres
        _, vjp_res = jax.vjp(lambda y, g, gate: _f_post_res(jnp.zeros_like(y), y, g, gate), y, g_post, gate)
        dy, dg_post, dgate = vjp_res(dx1)
        return (dx1, dy), (dg_pre, dsh, dsc, dg_post, dgate)

    return _rowwise(name, fn, rows, [(dh, d, 0), (dres, d, 0), (x1, d, 0), (y, d, 0)], [g_pre, sh, sc, g_post, gate],
                    [(d, F32), (d, BF16)], [(1, d)] * 5, tm=_pick(rows, (256, 128)))


def _post_res_loss(name, xr, y, g, gate, target):
    rows, d = xr.shape

    def fn(xr, y, target, g, gate):
        x2, vjp = jax.vjp(lambda y, g, gate: _f_post_res(xr, y, g, gate), y, g, gate)
        diff = x2 - target
        part = 0.5 * jnp.sum(jnp.mean(diff * diff, axis=-1, keepdims=True), axis=0, keepdims=True)
        dx2 = diff * (1.0 / d)
        dy, dg, dgate = vjp(dx2)
        return (dx2, dy), (jnp.broadcast_to(part, (1, LANES)), dg, dgate)

    return _rowwise(name, fn, rows, [(xr, d, 0), (y, d, 0), (target, d, 0)], [g, gate], [(d, F32), (d, BF16)],
                    [(1, LANES), (1, d), (1, d)])


def _mm_rows(name, a, b, mode, fn, extras, outs):
    m, k = a.shape
    tm = _pick(m, (256, 128))

    sub = _pick(tm, (128,))

    def body(a_ref, b_ref, *rest):
        for r0 in range(0, tm, sub):
            rows = slice(r0, r0 + sub)
            tiles = fn(_raw_dot(mode, a_ref[rows, :], b_ref[...], False), *[e[rows, :] for e in rest[:len(extras)]])
            for r, val in zip(rest[len(extras):], tiles):
                r[rows, :] = val.astype(r.dtype)

    row = lambda w: pl.BlockSpec((tm, w), lambda i: (i, 0))
    return pl.pallas_call(
        body, name=name, grid=(m // tm,),
        in_specs=[row(k), pl.BlockSpec(b.shape, lambda i: (0, 0))] + [row(e.shape[1]) for e in extras],
        out_specs=[row(w) for w, _ in outs], out_shape=[jax.ShapeDtypeStruct((m, w), dt) for w, dt in outs],
        compiler_params=_cp("parallel"),
    )(a, b, *extras)


def _ffn_in_swiglu(name, h, w_t):
    f = w_t.shape[0] // 2
    fn = lambda u: (u, _f_swiglu(u[:, :f], u[:, f:]))
    return _mm_rows(name, h, w_t, "nt", fn, [], [(2 * f, BF16), (f, BF16)])


def _ffn_out_dx_swiglu_bwd(name, df, w_out, u):
    f = w_out.shape[0]

    def fn(da, u):
        u = u.astype(F32)
        _, vjp = jax.vjp(_f_swiglu, u[:, :f], u[:, f:])
        return (jnp.concatenate(vjp(da), axis=1),)

    return _mm_rows(name, df, w_out, "nt", fn, [u], [(2 * f, BF16)])[0]


def _gate_fwd(name, p, wf, wb, bf, bb):
    rows = p.shape[0]
    return _rowwise(name, lambda z, wf, wb, bf, bb: (_f_gate(z, wf, wb, bf, bb), ()), rows,
                    [(p, LANES, C_Z // LANES)], [wf, wb, bf, bb], [(GKW, F32)] * 2, [])


def _gate_bwd(name, p, dla_f, dla_b, wf, wb, bf, bb):
    rows = p.shape[0]

    def fn(z, dlf, dlb, wf, wb, bf, bb):
        _, vjp = jax.vjp(_f_gate, z, wf, wb, bf, bb)
        dz, dwf, dwb, dbf, dbb = vjp((dlf, dlb))
        return (dz,), (dwf, dwb, dbf, dbb)

    return _rowwise(name, fn, rows, [(p, LANES, C_Z // LANES), (dla_f, GKW, 0), (dla_b, GKW, 0)],
                    [wf, wb, bf, bb], [(LANES, BF16)], [(LANES, GKW), (LANES, GKW), (1, GKW), (1, GKW)])


def _head_mean_matrix():
    h = np.arange(GVW) // GLA_DV
    return jnp.asarray((h[:, None] == h[None, :]).astype(np.float32) / GLA_DV)


def _gla_out(name, attn, of, ob, p, gt):
    rows = of.shape[0]
    bd = _head_mean_matrix()
    fn = lambda attn, of, ob, gg, gt, bd: ((jnp.concatenate([attn, _f_gla_out(of, ob, gg, gt, bd)], axis=1),), ())
    return _rowwise(name, fn, rows, [(attn, QW, 0), (of, GVW, 0), (ob, GVW, 0), (p, GVW, C_GG // GVW)], [gt, bd],
                    [(MIX, BF16)], [])[0]


def _gla_out_bwd(name, dmix, of, ob, p, gt):
    rows = of.shape[0]
    bd = _head_mean_matrix()

    def fn(dm, of, ob, gg, gt, bd):
        _, vjp = jax.vjp(lambda of, gg, gt: _f_gla_out(of, ob, gg, gt, bd), of, gg, gt)
        do, dgg, dgt = vjp(dm)
        return (do, dgg), (dgt,)

    return _rowwise(name, fn, rows, [(dmix, GVW, 1), (of, GVW, 0), (ob, GVW, 0), (p, GVW, C_GG // GVW)], [gt, bd],
                    [(GVW, F32), (GVW, BF16)], [(1, GVW)])


def _rope_tables(n_tokens):
    t = jnp.arange(n_tokens)
    row = (t // GRID_W).astype(F32)
    col = (t % GRID_W).astype(F32)
    half = HEAD_DIM // 2
    inv_freq = ROPE_BASE ** (-jnp.arange(0, half, 2, dtype=F32) / half)
    ang_r = row[:, None] * inv_freq[None, :]
    ang_c = col[:, None] * inv_freq[None, :]
    ang = jnp.concatenate([ang_r, ang_r, ang_c, ang_c], axis=-1)
    sign = jnp.concatenate([-jnp.ones((16,), F32), jnp.ones((16,), F32)] * 2)
    cos, sin = jnp.cos(ang), jnp.sin(ang) * sign[None, :]
    return jnp.tile(cos, (1, 2)), jnp.tile(sin, (1, 2))


def _rot_pairs(x):
    w = x.shape[-1]
    lane = lax.broadcasted_iota(jnp.int32, x.shape, x.ndim - 1)
    return jnp.where((lane % 32) < 16, pltpu.roll(x, w - 16, x.ndim - 1), pltpu.roll(x, 16, x.ndim - 1))


def _rope_apply(x, cos, sin_signed, inverse):
    reps = x.shape[-1] // LANES
    cos = jnp.concatenate([cos] * reps, axis=-1) if reps > 1 else cos
    sin = jnp.concatenate([sin_signed] * reps, axis=-1) if reps > 1 else sin_signed
    if inverse:
        return x * cos + _rot_pairs(x * sin)
    return x * cos + _rot_pairs(x) * sin


def _rope_fwd(name, p, cos, sin):
    rows = p.shape[0]

    def fn(q, k, v, cos, sin):
        return (_rope_apply(q, cos, sin, False), _rope_apply(k, cos, sin, False), v), ()

    return _rowwise(name, fn, rows, [(p, QW, 0), (p, KVW, C_K // KVW), (p, KVW, C_V // KVW), (cos, LANES, 0),
                                     (sin, LANES, 0)], [], [(QW, BF16), (KVW, BF16), (KVW, BF16)], [])


def _rope_bwd(name, dq, dk, cos, sin):
    rows = dq.shape[0]

    def fn(dq, dk, cos, sin):
        return (_rope_apply(dq, cos, sin, True), _rope_apply(dk, cos, sin, True)), ()

    return _rowwise(name, fn, rows, [(dq, QW, 0), (dk, KVW, 0), (cos, LANES, 0), (sin, LANES, 0)], [],
                    [(QW, BF16), (KVW, BF16)], [])


GROUP_ROWS = ATT_GROUP * BLOCK


def _f_attn(qs, kws, vws, kcs, vcs, sink, n, n_tokens):
    row = lax.broadcasted_iota(jnp.int32, (GROUP_ROWS, 1), 0)
    group = sum((row >= g * BLOCK).astype(jnp.int32) for g in range(1, ATT_GROUP))
    i = lax.broadcasted_iota(jnp.int32, (GROUP_ROWS, 3 * BLOCK), 0) - BLOCK * group
    j = lax.broadcasted_iota(jnp.int32, (GROUP_ROWS, 3 * BLOCK), 1)
    kpos = (n - 1) * BLOCK + j
    mask = (jnp.abs(j - BLOCK - i) <= WINDOW) & (kpos >= 0) & (kpos < n_tokens)
    head_id = lax.broadcasted_iota(jnp.int32, (1, ATT_HEADS), 1)
    scale = HEAD_DIM ** -0.5
    outs = []
    for h in range(ATT_KV_HEADS):
        sk = jnp.zeros((GROUP_ROWS, 1), F32)
        for g in range(ATT_GROUP):
            one = jnp.sum(jnp.where(head_id == h * ATT_GROUP + g, sink, 0.0), axis=-1, keepdims=True)
            sk = jnp.where(group == g, one, sk)
        q = qs[h] * scale
        s_w = jnp.where(mask, _nt(q, kws[h]), NEG_INF)
        s_c = _nt(q, kcs[h])
        m = lax.stop_gradient(jnp.maximum(jnp.maximum(jnp.max(s_w, axis=-1, keepdims=True),
                                                      jnp.max(s_c, axis=-1, keepdims=True)), sk))
        pw, pc = jnp.exp(s_w - m), jnp.exp(s_c - m)
        den = jnp.sum(pw, axis=-1, keepdims=True) + jnp.sum(pc, axis=-1, keepdims=True) + jnp.exp(sk - m)
        outs.append((_nn(pw, vws[h]) + _nn(pc, vcs[h])) / den)
    return tuple(outs)


def _group_rows(ref, h):
    hs = lambda hq: slice(hq * HEAD_DIM, (hq + 1) * HEAD_DIM)
    return jnp.concatenate([ref[:, hs(h * ATT_GROUP + g)].astype(F32) for g in range(ATT_GROUP)], axis=0)


def _ungroup_rows(ref, h, val):
    for g in range(ATT_GROUP):
        hq = h * ATT_GROUP + g
        ref[:, hq * HEAD_DIM:(hq + 1) * HEAD_DIM] = val[g * BLOCK:(g + 1) * BLOCK].astype(ref.dtype)


def _attn_loads(n, q_ref, kp_ref, vp_ref, kc_ref, vc_ref):
    r0 = pl.multiple_of(n * BLOCK, BLOCK)
    hs = lambda h: slice(h * HEAD_DIM, (h + 1) * HEAD_DIM)
    qs = [_group_rows(q_ref, h) for h in range(ATT_KV_HEADS)]
    kws = [kp_ref[pl.ds(r0, 3 * BLOCK), hs(h)].astype(F32) for h in range(ATT_KV_HEADS)]
    vws = [vp_ref[pl.ds(r0, 3 * BLOCK), hs(h)].astype(F32) for h in range(ATT_KV_HEADS)]
    kcs = [kc_ref[:, hs(h)].astype(F32) for h in range(ATT_KV_HEADS)]
    vcs = [vc_ref[:, hs(h)].astype(F32) for h in range(ATT_KV_HEADS)]
    return r0, hs, qs, kws, vws, kcs, vcs


def _attn_specs(s, c):
    full = lambda shape: pl.BlockSpec(shape, lambda n: (0, 0))
    return [pl.BlockSpec((BLOCK, QW), lambda n: (n, 0)), full((s + 2 * BLOCK, KVW)), full((s + 2 * BLOCK, KVW)),
            full((c, KVW)), full((c, KVW)), full((1, ATT_HEADS))]


def _attn_fwd(q, kp, vp, kc, vc, sink):
    s, c = q.shape[0], kc.shape[0]

    def body(q_ref, kp_ref, vp_ref, kc_ref, vc_ref, sink_ref, o_ref):
        n = pl.program_id(0)
        _, hs, qs, kws, vws, kcs, vcs = _attn_loads(n, q_ref, kp_ref, vp_ref, kc_ref, vc_ref)
        outs = _f_attn(qs, kws, vws, kcs, vcs, sink_ref[...], n, s)
        for h in range(ATT_KV_HEADS):
            _ungroup_rows(o_ref, h, outs[h])

    return pl.pallas_call(
        body, name="attn_fwd", grid=(s // BLOCK,), in_specs=_attn_specs(s, c),
        out_specs=pl.BlockSpec((BLOCK, QW), lambda n: (n, 0)), out_shape=jax.ShapeDtypeStruct((s, QW), BF16),
        compiler_params=_cp("parallel"),
    )(q, kp, vp, kc, vc, sink)


def _attn_bwd(do, q, kp, vp, kc, vc, sink):
    s, c = q.shape[0], kc.shape[0]

    def body(do_ref, q_ref, kp_ref, vp_ref, kc_ref, vc_ref, sink_ref, dq_ref, dkp_ref, dvp_ref, dkc_ref, dvc_ref,
             dsink_ref):
        n = pl.program_id(0)

        @pl.when(n == 0)
        def _():
            for r in (dkp_ref, dvp_ref, dkc_ref, dvc_ref, dsink_ref):
                r[...] = jnp.zeros_like(r)

        r0, hs, qs, kws, vws, kcs, vcs = _attn_loads(n, q_ref, kp_ref, vp_ref, kc_ref, vc_ref)
        _, vjp = jax.vjp(lambda qs, kws, vws, kcs, vcs, sink: _f_attn(qs, kws, vws, kcs, vcs, sink, n, s),
                         qs, kws, vws, kcs, vcs, sink_ref[...])
        dqs, dkws, dvws, dkcs, dvcs, dsink = vjp(tuple(_group_rows(do_ref, h) for h in range(ATT_KV_HEADS)))
        for h in range(ATT_KV_HEADS):
            _ungroup_rows(dq_ref, h, dqs[h])
            dkp_ref[pl.ds(r0, 3 * BLOCK), hs(h)] += dkws[h]
            dvp_ref[pl.ds(r0, 3 * BLOCK), hs(h)] += dvws[h]
            dkc_ref[:, hs(h)] += dkcs[h]
            dvc_ref[:, hs(h)] += dvcs[h]
        dsink_ref[...] += dsink

    full = lambda shape: pl.BlockSpec(shape, lambda n: (0, 0))
    return pl.pallas_call(
        body, name="attn_bwd", grid=(s // BLOCK,),
        in_specs=[pl.BlockSpec((BLOCK, QW), lambda n: (n, 0))] + _attn_specs(s, c),
        out_specs=[pl.BlockSpec((BLOCK, QW), lambda n: (n, 0)), full((s + 2 * BLOCK, KVW)), full((s + 2 * BLOCK, KVW)),
                   full((c, KVW)), full((c, KVW)), full((1, ATT_HEADS))],
        out_shape=[jax.ShapeDtypeStruct((s, QW), F32), jax.ShapeDtypeStruct((s + 2 * BLOCK, KVW), F32),
                   jax.ShapeDtypeStruct((s + 2 * BLOCK, KVW), F32), jax.ShapeDtypeStruct((c, KVW), F32),
                   jax.ShapeDtypeStruct((c, KVW), F32), jax.ShapeDtypeStruct((1, ATT_HEADS), F32)],
        compiler_params=_cp("arbitrary"),
    )(do, q, kp, vp, kc, vc, sink)


def _gla_masks():
    hk = np.arange(GKW) // GLA_DK
    hv = np.arange(GVW) // GLA_DV
    head_k = (np.arange(GLA_HEADS)[:, None] == hk[None, :]).astype(np.float32)
    head_v = (np.arange(GLA_HEADS)[:, None] == hv[None, :]).astype(np.float32)
    bd_t = (hv[:, None] == hk[None, :]).astype(np.float32)
    return jnp.asarray(head_k), jnp.asarray(head_v), jnp.asarray(bd_t)


def _tri(n, rev, strict=False):
    i = lax.broadcasted_iota(jnp.int32, (n, n), 0)
    j = lax.broadcasted_iota(jnp.int32, (n, n), 1)
    if strict:
        keep = (j > i) if rev else (j < i)
    else:
        keep = (j >= i) if rev else (j <= i)
    return keep


def _f_gla_chunk(q, k, v, la, st, head_k, head_v, bd_t, rev):
    keep = _tri(GLA_CHUNK, rev)
    b = _nn_hi(keep.astype(F32), la)
    bl = jnp.sum(la, axis=0, keepdims=True)
    qd = q * (GLA_DK ** -0.5) * jnp.exp(b)
    ki = k * jnp.exp(-b)
    kd = k * jnp.exp(bl - b)
    q_heads = (qd[None, :, :] * head_k[:, None, :]).reshape(GLA_HEADS * GLA_CHUNK, GKW)
    a_all = _nt(q_heads, ki).reshape(GLA_HEADS, GLA_CHUNK, GLA_CHUNK)
    a_all = jnp.where(keep[None, :, :], a_all, 0.0).reshape(GLA_HEADS * GLA_CHUNK, GLA_CHUNK)
    o_all = _nn(a_all, v).reshape(GLA_HEADS, GLA_CHUNK, GVW)
    intra = jnp.sum(o_all * head_v[:, None, :], axis=0)
    inter = _nt(qd, st)
    st_new = st * jnp.exp(bl) + bd_t * _tn(v, kd)
    return intra + inter, st_new


def _gla_specs(s, tb, order):
    return [pl.BlockSpec((tb, GKW), lambda i: (order(i), C_GQ // GKW)),
            pl.BlockSpec((tb, GKW), lambda i: (order(i), C_GK // GKW)),
            pl.BlockSpec((tb, GVW), lambda i: (order(i), C_GV // GVW)),
            pl.BlockSpec((tb, GKW), lambda i: (order(i), 0))]


GLA_BLOCK_CHUNKS = 4


def _gla_fwd(name, p, la, st0, rev):
    s = p.shape[0]
    tb = GLA_BLOCK_CHUNKS * GLA_CHUNK
    nblk = s // tb
    order = (lambda i: nblk - 1 - i) if rev else (lambda i: i)
    masks = _gla_masks()

    def body(q_ref, k_ref, v_ref, la_ref, st0_ref, hk_ref, hv_ref, bd_ref, o_ref, sts_ref, st_ref):
        @pl.when(pl.program_id(0) == 0)
        def _():
            st_ref[...] = st0_ref[...]

        st = st_ref[...]
        sts_ref[0] = st
        chunks = range(GLA_BLOCK_CHUNKS)
        for ci in (reversed(chunks) if rev else chunks):
            rows = slice(ci * GLA_CHUNK, (ci + 1) * GLA_CHUNK)
            o, st = _f_gla_chunk(q_ref[rows, :], k_ref[rows, :], v_ref[rows, :], la_ref[rows, :], st,
                                 hk_ref[...], hv_ref[...], bd_ref[...], rev)
            o_ref[rows, :] = o
        st_ref[...] = st

    full = lambda a: pl.BlockSpec(a.shape, lambda i: (0,) * a.ndim)
    return pl.pallas_call(
        body, name=name, grid=(nblk,),
        in_specs=_gla_specs(s, tb, order) + [full(st0)] + [full(m) for m in masks],
        out_specs=[pl.BlockSpec((tb, GVW), lambda i: (order(i), 0)),
                   pl.BlockSpec((1, GVW, GKW), lambda i: (order(i), 0, 0))],
        out_shape=[jax.ShapeDtypeStruct((s, GVW), F32), jax.ShapeDtypeStruct((nblk, GVW, GKW), F32)],
        scratch_shapes=[pltpu.VMEM((GVW, GKW), F32)],
        compiler_params=_cp("arbitrary"),
    )(p, p, p, la, st0, *masks)


def _gla_bwd(name, p, la, sts, do, prev, rev, after=None):
    s = p.shape[0]
    tb = GLA_BLOCK_CHUNKS * GLA_CHUNK
    nblk = s // tb
    order = (lambda i: i) if rev else (lambda i: nblk - 1 - i)
    masks = _gla_masks()
    n_prev = 0 if prev is None else 3
    follow = () if after is None else (after,)

    def body(*refs):
        q_ref, k_ref, v_ref, la_ref, sts_ref, do_ref, hk_ref, hv_ref, bd_ref = refs[:9]
        prev_refs = refs[9:9 + n_prev]
        dq_ref, dk_ref, dv_ref, dla_ref, dst0_ref, dst_ref = refs[9 + n_prev + len(follow):]

        @pl.when(pl.program_id(0) == 0)
        def _():
            dst_ref[...] = jnp.zeros_like(dst_ref)

        def block(q, k, v, la, st):
            outs = [None] * GLA_BLOCK_CHUNKS
            chunks = range(GLA_BLOCK_CHUNKS)
            for ci in (reversed(chunks) if rev else chunks):
                rows = slice(ci * GLA_CHUNK, (ci + 1) * GLA_CHUNK)
                outs[ci], st = _f_gla_chunk(q[ci], k[ci], v[ci], la[ci], st, hk_ref[...], hv_ref[...], bd_ref[...],
                                            rev)
            return tuple(outs), st

        split = lambda r: tuple(r[ci * GLA_CHUNK:(ci + 1) * GLA_CHUNK, :].astype(F32)
                                for ci in range(GLA_BLOCK_CHUNKS))
        _, vjp = jax.vjp(block, split(q_ref), split(k_ref), split(v_ref), split(la_ref), sts_ref[0])
        dq, dk, dv, dla, dst = vjp((split(do_ref), dst_ref[...]))
        for ci in range(GLA_BLOCK_CHUNKS):
            rows = slice(ci * GLA_CHUNK, (ci + 1) * GLA_CHUNK)
            if n_prev:
                dq_ref[rows, :] = dq[ci] + prev_refs[0][rows, :]
                dk_ref[rows, :] = dk[ci] + prev_refs[1][rows, :]
                dv_ref[rows, :] = dv[ci] + prev_refs[2][rows, :]
            else:
                dq_ref[rows, :], dk_ref[rows, :], dv_ref[rows, :] = dq[ci], dk[ci], dv[ci]
            dla_ref[rows, :] = dla[ci]
        dst_ref[...] = dst
        dst0_ref[...] = dst

    full = lambda a: pl.BlockSpec(a.shape, lambda i: (0,) * a.ndim)
    blk = lambda w: pl.BlockSpec((tb, w), lambda i: (order(i), 0))
    prev_specs = [blk(GKW), blk(GKW), blk(GVW)] if n_prev else []
    return pl.pallas_call(
        body, name=name, grid=(nblk,),
        in_specs=_gla_specs(s, tb, order) + [pl.BlockSpec((1, GVW, GKW), lambda i: (order(i), 0, 0)), blk(GVW)]
        + [full(m) for m in masks] + prev_specs + [pl.BlockSpec(memory_space=pl.ANY)] * len(follow),
        out_specs=[blk(GKW), blk(GKW), blk(GVW), blk(GKW), pl.BlockSpec((GVW, GKW), lambda i: (0, 0))],
        out_shape=[jax.ShapeDtypeStruct((s, GKW), F32), jax.ShapeDtypeStruct((s, GKW), F32),
                   jax.ShapeDtypeStruct((s, GVW), F32), jax.ShapeDtypeStruct((s, GKW), F32),
                   jax.ShapeDtypeStruct((GVW, GKW), F32)],
        scratch_shapes=[pltpu.VMEM((GVW, GKW), F32)],
        compiler_params=_cp("arbitrary"),
    )(p, p, p, la, sts, do, *masks, *(prev or ()), *follow)


def _f_ctx_state(k, v, la_f, la_b, bd_t):
    c = k.shape[0]
    after = _nn_hi(_tri(c, True, strict=True).astype(F32), la_f)
    before = _nn_hi(_tri(c, False, strict=True).astype(F32), la_b)
    return bd_t * _tn(v, k * jnp.exp(after)), bd_t * _tn(v, k * jnp.exp(before))


def _ctx_state(pc, la_f, la_b):
    c = pc.shape[0]
    bd_t = _gla_masks()[2]

    def body(k_ref, v_ref, lf_ref, lb_ref, bd_ref, sf_ref, sb_ref):
        sf_ref[...], sb_ref[...] = _f_ctx_state(k_ref[...], v_ref[...], lf_ref[...], lb_ref[...], bd_ref[...])

    full = lambda a: pl.BlockSpec(a.shape, lambda i: (0, 0))
    return pl.pallas_call(
        body, name="ctx_state_fwd", grid=(1,),
        in_specs=[pl.BlockSpec((c, GKW), lambda i: (0, C_GK // GKW)), pl.BlockSpec((c, GVW), lambda i: (0, C_GV // GVW)),
                  full(la_f), full(la_b), full(bd_t)],
        out_specs=[pl.BlockSpec((GVW, GKW), lambda i: (0, 0))] * 2,
        out_shape=[jax.ShapeDtypeStruct((GVW, GKW), F32)] * 2,
        compiler_params=_cp("arbitrary"),
    )(pc, pc, la_f, la_b, bd_t)


def _ctx_state_bwd(pc, la_f, la_b, dsf, dsb):
    c = pc.shape[0]
    bd_t = _gla_masks()[2]

    def body(k_ref, v_ref, lf_ref, lb_ref, bd_ref, dsf_ref, dsb_ref, dk_ref, dv_ref, dlf_ref, dlb_ref):
        _, vjp = jax.vjp(lambda k, v, lf, lb: _f_ctx_state(k, v, lf, lb, bd_ref[...]),
                         k_ref[...], v_ref[...], lf_ref[...], lb_ref[...])
        dk, dv, dlf, dlb = vjp((dsf_ref[...], dsb_ref[...]))
        dk_ref[...], dv_ref[...] = dk.astype(BF16), dv.astype(BF16)
        dlf_ref[...], dlb_ref[...] = dlf, dlb

    full = lambda a: pl.BlockSpec(a.shape, lambda i: (0, 0))
    return pl.pallas_call(
        body, name="ctx_state_bwd", grid=(1,),
        in_specs=[pl.BlockSpec((c, GKW), lambda i: (0, C_GK // GKW)), pl.BlockSpec((c, GVW), lambda i: (0, C_GV // GVW)),
                  full(la_f), full(la_b), full(bd_t), full(dsf), full(dsb)],
        out_specs=[pl.BlockSpec((c, GKW), lambda i: (0, 0)), pl.BlockSpec((c, GVW), lambda i: (0, 0)),
                   pl.BlockSpec((c, GKW), lambda i: (0, 0)), pl.BlockSpec((c, GKW), lambda i: (0, 0))],
        out_shape=[jax.ShapeDtypeStruct((c, GKW), BF16), jax.ShapeDtypeStruct((c, GVW), BF16),
                   jax.ShapeDtypeStruct((c, GKW), F32), jax.ShapeDtypeStruct((c, GKW), F32)],
        compiler_params=_cp("arbitrary"),
    )(pc, pc, la_f, la_b, bd_t, dsf, dsb)


_SRC_COLS = ((0, QW), (QW + 2 * KVW + 2 * GKW, GVW), (QW + 2 * KVW + 2 * GKW + GVW, GVW), (QW, KVW), (QW + KVW, KVW),
             (QW + 2 * KVW, GKW), (QW + 2 * KVW + GKW, GKW), (IN_COLS - 2 * GATE_RANK, 2 * GATE_RANK))
_DST_COLS = (C_Q, C_GV, C_GG, C_K, C_V, C_GQ, C_GK, C_Z)


def _pack_w_in(w_in):
    parts = [w_in[:, s:s + n] for s, n in _SRC_COLS]
    parts.append(jnp.zeros((w_in.shape[0], IN_PAD - C_Z - 2 * GATE_RANK), w_in.dtype))
    return jnp.concatenate(parts, axis=1)


def _unpack_w_in_grad(g):
    by_src = sorted(zip(_SRC_COLS, _DST_COLS))
    return jnp.concatenate([g[:, d:d + n] for (_, n), d in by_src], axis=1)


def _prep_weights(w_in, w_gate_fwd, w_gate_bwd):
    pad_rows = lambda w, at: jnp.zeros((LANES, GKW), F32).at[at:at + GATE_RANK].set(w)
    return {"w_in": _pack_w_in(w_in).astype(BF16), "wg_f": pad_rows(w_gate_fwd, 0),
            "wg_b": pad_rows(w_gate_bwd, GATE_RANK)}


def _local_step(x, ctx, target, ada, ada_c, w, late_weights, reduce_behind=None, reduce_w_in=None):
    s, d = x.shape
    sh1, sc1, gt1, sh2, sc2, gt2 = [ada[:, i * d:(i + 1) * d] for i in range(6)]
    sh1c, sc1c = ada_c[:, :d], ada_c[:, d:2 * d]
    cos, sin = _rope_tables(s)
    gt = jnp.tile(w["g_gla_norm"], (1, GLA_HEADS))

    h = _norm_mod("pre_mix", x, w["g_pre_mix"], sh1, sc1)
    hc = _norm_mod("pre_mix_ctx", ctx, w["g_pre_mix"], sh1c, sc1c)
    p = _mm("proj_in", h, w["w_in"], "nn")
    pc = _mm("proj_in_ctx", hc, w["w_in"], "nn")
    q_rot, k_rot, v_b = _rope_fwd("rope", p, cos, sin)
    pad = ((BLOCK, BLOCK), (0, 0))
    kp, vp = jnp.pad(k_rot, pad), jnp.pad(v_b, pad)
    kc, vc = pc[:, C_K:C_K + KVW].astype(BF16), pc[:, C_V:C_V + KVW].astype(BF16)
    attn = _attn_fwd(q_rot, kp, vp, kc, vc, w["attn_sink"])
    gate_w = (w["wg_f"], w["wg_b"], w["b_gate_fwd"], w["b_gate_bwd"])
    la_f, la_b = _gate_fwd("gate", p, *gate_w)
    la_fc, la_bc = _gate_fwd("gate_ctx", pc, *gate_w)
    st_f0, st_b0 = _ctx_state(pc, la_fc, la_bc)
    o_f, sts_f = _gla_fwd("gla_fwd_f", p, la_f, st_f0, False)
    o_b, sts_b = _gla_fwd("gla_fwd_b", p, la_b, st_b0, True)
    mix = _gla_out("gla_out", attn, o_f, o_b, p, gt)
    w_out, w_ffn_in_t, w_ffn_out = late_weights(attn)
    y = _mm("proj_out", mix, w_out, "nn")
    x1, h2 = _post_res_norm_mod("post_mix_pre_ffn", x, y, w["g_post_mix"], gt1, w["g_pre_ffn"], sh2, sc2)
    u, a = _ffn_in_swiglu("ffn_in", h2, w_ffn_in_t)
    f = _mm("ffn_out", a, w_ffn_out, "nn")
    g = {}
    dx2, df, loss, g["g_post_ffn"], dgt2 = _post_res_loss("post_ffn_loss", x1, f, w["g_post_ffn"], gt2, target)

    g["w_ffn_out"] = _mm("ffn_out_dw", a, df, "tn")
    du = _ffn_out_dx_swiglu_bwd("ffn_out_dx", df, w_ffn_out, u)
    dh2 = _mm("ffn_in_dx", du, w_ffn_in_t, "nn")
    g["w_ffn_in_t"] = _mm("ffn_in_dw", du, h2, "tn")
    dx1, dy, g["g_pre_ffn"], dsh2, dsc2, g["g_post_mix"], dgt1 = _norm_mod_post_res_bwd(
        "pre_ffn_post_mix_bwd", dh2, dx2, x1, y, w["g_pre_ffn"], sh2, sc2, w["g_post_mix"], gt1)
    dmix = _mm("proj_out_dx", dy, w_out, "nt", BF16)
    g["w_out"] = _mm("proj_out_dw", mix, dy, "tn")
    rb, sink, token = reduce_behind, w["attn_sink"], None
    if rb is not None:
        gt = _behind(gt, rb.start(g["w_ffn_in_t"], g["w_ffn_out"], g["w_out"]))
    d_o, dgg, dgt = _gla_out_bwd("gla_out_bwd", dmix, o_f, o_b, p, gt)
    g["g_gla_norm"] = jnp.sum(dgt.reshape(GLA_HEADS, GLA_DV), axis=0, keepdims=True)
    if rb is not None:
        token = rb.pair(dgg)
    dgq, dgk, dgv, dla_f, dst_f0 = _gla_bwd("gla_bwd_f", p, la_f, sts_f, d_o, None, False, token)
    dgq, dgk, dgv, dla_b, dst_b0 = _gla_bwd("gla_bwd_b", p, la_b, sts_b, d_o, (dgq, dgk, dgv), True)
    if rb is not None:
        sink = _behind(sink, rb.total(dgq))
    dgkc, dgvc, dla_fc, dla_bc = _ctx_state_bwd(pc, la_fc, la_bc, dst_f0, dst_b0)
    dz, dwf, dwb, dbf, dbb = _gate_bwd("gate_bwd", p, dla_f, dla_b, *gate_w)
    dzc, dwfc, dwbc, dbfc, dbbc = _gate_bwd("gate_ctx_bwd", pc, dla_fc, dla_bc, *gate_w)
    g["w_gate_fwd"] = (dwf + dwfc)[:GATE_RANK]
    g["w_gate_bwd"] = (dwb + dwbc)[GATE_RANK:2 * GATE_RANK]
    g["b_gate_fwd"], g["b_gate_bwd"] = dbf + dbfc, dbb + dbbc
    dq_rot, dkp, dvp, dkc, dvc, g["attn_sink"] = _attn_bwd(dmix, q_rot, kp, vp, kc, vc, sink)
    if rb is not None:
        g["behind"] = rb.result(dq_rot)
    dq, dk = _rope_bwd("rope_bwd", dq_rot, dkp[BLOCK:BLOCK + s], cos, sin)
    dp = jnp.concatenate([dq, dgv.astype(BF16), dgg, dk, dvp[BLOCK:BLOCK + s].astype(BF16), dgq.astype(BF16),
                          dgk.astype(BF16), dz], axis=1)
    c_rows = ctx.shape[0]
    zeros = lambda n: jnp.zeros((c_rows, n), BF16)
    dpc = jnp.concatenate([zeros(QW), dgvc, zeros(GVW), dkc.astype(BF16), dvc.astype(BF16), zeros(GKW), dgkc, dzc],
                          axis=1)
    g["w_in"] = _mm("proj_in_dw", h, dp, "tn", init=_mm("proj_in_ctx_dw", hc, dpc, "tn"))
    token = None if reduce_w_in is None else reduce_w_in.start(g["w_in"])
    dh = _mm("proj_in_dx", dp, w["w_in"], "nt", after=token)
    dhc = _mm("proj_in_ctx_dx", dpc, w["w_in"], "nt")
    if reduce_w_in is not None:
        sh1 = _behind(sh1, reduce_w_in.pair(dh))
    dx, dg_a, dsh1, dsc1 = _norm_mod_bwd("pre_mix_bwd", dh, dx1, x, w["g_pre_mix"], sh1, sc1)
    if reduce_w_in is not None:
        dsh1 = _behind(dsh1, reduce_w_in.total(dx))
    _, dg_b, dsh1c, dsc1c = _norm_mod_bwd("pre_mix_ctx_bwd", dhc, jnp.zeros_like(dhc), ctx, w["g_pre_mix"], sh1c,
                                          sc1c)
    g["g_pre_mix"] = dg_a + dg_b
    d_ada = jnp.concatenate([dsh1, dsc1, dgt1, dsh2, dsc2, dgt2], axis=1)
    d_ada_c = jnp.concatenate([dsh1c, dsc1c, jnp.zeros((1, 4 * d), F32)], axis=1)
    return loss, dx, g, d_ada, d_ada_c


HBM = pl.BlockSpec(memory_space=pltpu.HBM)
N_DEV, N_CHIP = 8, 4


def _place():
    x, y, c = lax.axis_index("x"), lax.axis_index("y"), lax.axis_index("c")
    return x, y, c, [(1 - x, y), (x, 1 - y), (1 - x, 1 - y)]


def _row_tile(n, mult, cap):
    return max(t for t in range(mult, min(n, cap) + 1, mult) if n % t == 0)


def _ag_small(name, v, after=None):
    follow = () if after is None else (after,)

    def body(v_ref, *rest):
        out_ref, send_sems, recv_sems = rest[len(follow):]
        x, y, c, _ = _place()
        out_ref[4 * x + 2 * y + c] = v_ref[...]

        def peer(r):
            return ((1 - x) if r & 4 else x, (1 - y) if r & 2 else y, (1 - c) if r & 1 else c)

        def copy(r, block):
            px, py, pc = block
            return pltpu.make_async_remote_copy(
                src_ref=v_ref, dst_ref=out_ref.at[4 * px + 2 * py + pc], send_sem=send_sems.at[r - 1],
                recv_sem=recv_sems.at[r - 1], device_id=peer(r), device_id_type=MESH)

        sends = [copy(r, (x, y, c)) for r in range(1, N_DEV)]
        for cp in sends:
            cp.start()
        for r in range(1, N_DEV):
            copy(r, peer(r)).wait_recv()
        for cp in sends:
            cp.wait_send()

    return pl.pallas_call(
        body, name=name, out_shape=jax.ShapeDtypeStruct((N_DEV,) + v.shape, v.dtype),
        in_specs=[pl.BlockSpec(memory_space=pltpu.VMEM)] + [pl.BlockSpec(memory_space=pl.ANY)] * len(follow),
        out_specs=pl.BlockSpec(memory_space=pltpu.VMEM),
        scratch_shapes=[pltpu.SemaphoreType.DMA((N_DEV - 1,)), pltpu.SemaphoreType.DMA((N_DEV - 1,))],
    )(v, *follow)


def _halves(c, rows, mult):
    hr = rows // 2
    return pl.ds(pl.multiple_of(c * hr, mult), hr), pl.ds(pl.multiple_of((1 - c) * hr, mult), hr)


def _ag_shards(name, shard):
    rows = shard.shape[0]

    def body(w_ref, out_ref, send_sems, recv_sems, local_sem):
        x, y, c, chips = _place()
        mine_half, other_half = _halves(c, rows, 16)
        me = 2 * x + y
        mine = pltpu.make_async_copy(w_ref, out_ref.at[me], local_sem)
        mine.start()

        def copy(k, src, chip, half, to):
            return pltpu.make_async_remote_copy(
                src_ref=src, dst_ref=out_ref.at[chip, half], send_sem=send_sems.at[k], recv_sem=recv_sems.at[k],
                device_id=to, device_id_type=MESH)

        first = [copy(j, w_ref.at[mine_half], me, mine_half, (px, py, c)) for j, (px, py) in enumerate(chips)]
        for cp in first:
            cp.start()
        passed = []
        for j, (px, py) in enumerate(chips):
            pk = 2 * px + py
            copy(j, w_ref.at[mine_half], pk, mine_half, (px, py, c)).wait_recv()
            cp = copy(3 + j, out_ref.at[pk, mine_half], pk, mine_half, (x, y, 1 - c))
            cp.start()
            passed.append(cp)
        for j, (px, py) in enumerate(chips):
            copy(3 + j, w_ref.at[mine_half], 2 * px + py, other_half, (x, y, 1 - c)).wait_recv()
        for cp in first + passed:
            cp.wait_send()
        mine.wait()

    return pl.pallas_call(
        body, name=name, out_shape=jax.ShapeDtypeStruct((N_CHIP,) + shard.shape, shard.dtype),
        in_specs=[HBM], out_specs=HBM,
        scratch_shapes=[pltpu.SemaphoreType.DMA((6,)), pltpu.SemaphoreType.DMA((6,)), pltpu.SemaphoreType.DMA],
    )(shard)


def _swap_half(name, g):
    n_sh, rows, n = g.shape

    def body(g_ref, a_ref, send_sem, recv_sem):
        x, y, c, _ = _place()
        _, other_half = _halves(c, rows, 8)
        cp = pltpu.make_async_remote_copy(
            src_ref=g_ref.at[pl.ds(0, n_sh), other_half], dst_ref=a_ref, send_sem=send_sem, recv_sem=recv_sem,
            device_id=(x, y, 1 - c), device_id_type=MESH)
        cp.start()
        cp.wait()

    return pl.pallas_call(
        body, name=name, out_shape=jax.ShapeDtypeStruct((n_sh, rows // 2, n), g.dtype), in_specs=[HBM], out_specs=HBM,
        scratch_shapes=[pltpu.SemaphoreType.DMA, pltpu.SemaphoreType.DMA],
    )(g)


def _add_half(name, g, a, c_idx):
    n_sh, hr, n = a.shape
    tr = _row_tile(hr, 16, 1024)
    nb = hr // tr

    def body(c_ref, g_ref, a_ref, o_ref):
        o_ref[...] = (g_ref[...] + a_ref[...]).astype(o_ref.dtype)

    return pl.pallas_call(
        body, name=name, out_shape=jax.ShapeDtypeStruct(a.shape, BF16),
        grid_spec=pltpu.PrefetchScalarGridSpec(
            num_scalar_prefetch=1, grid=(n_sh, nb),
            in_specs=[pl.BlockSpec((1, tr, n), lambda s, i, c_ref: (s, c_ref[0] * nb + i, 0)),
                      pl.BlockSpec((1, tr, n), lambda s, i, c_ref: (s, i, 0))],
            out_specs=pl.BlockSpec((1, tr, n), lambda s, i, c_ref: (s, i, 0))),
        compiler_params=_cp("parallel", "parallel"),
    )(c_idx, g, a)


def _scatter_chips(name, h):
    def body(h_ref, b_ref, send_sems, recv_sems, local_sem):
        x, y, c, chips = _place()
        me = 2 * x + y
        mine = pltpu.make_async_copy(h_ref.at[me], b_ref.at[me], local_sem)
        mine.start()

        def copy(j, src_block, dst_block, to):
            return pltpu.make_async_remote_copy(
                src_ref=h_ref.at[src_block], dst_ref=b_ref.at[dst_block], send_sem=send_sems.at[j],
                recv_sem=recv_sems.at[j], device_id=to, device_id_type=MESH)

        sends = [copy(j, 2 * px + py, me, (px, py, c)) for j, (px, py) in enumerate(chips)]
        for cp in sends:
            cp.start()
        for j, (px, py) in enumerate(chips):
            copy(j, me, 2 * px + py, (px, py, c)).wait_recv()
        for cp in sends:
            cp.wait_send()
        mine.wait()

    return pl.pallas_call(
        body, name=name, out_shape=jax.ShapeDtypeStruct(h.shape, h.dtype), in_specs=[HBM], out_specs=HBM,
        scratch_shapes=[pltpu.SemaphoreType.DMA((3,)), pltpu.SemaphoreType.DMA((3,)), pltpu.SemaphoreType.DMA],
    )(h)


def _sum_chips(name, b):
    n_sh, hr, n = b.shape
    tr = _row_tile(hr, 16, 1024)

    def body(b0, b1, b2, b3, o_ref):
        o_ref[...] = ((b0[0].astype(F32) + b1[0].astype(F32)) + b2[0].astype(F32)) + b3[0].astype(F32)

    return pl.pallas_call(
        body, name=name, grid=(hr // tr,), out_shape=jax.ShapeDtypeStruct((hr, n), F32),
        in_specs=[pl.BlockSpec((1, tr, n), functools.partial(lambda i, k: (k, i, 0), k=k)) for k in range(n_sh)],
        out_specs=pl.BlockSpec((tr, n), lambda i: (i, 0)), compiler_params=_cp("parallel"),
    )(b, b, b, b)


def _share_half(name, f):
    hr, n = f.shape

    def body(f_ref, out_ref, send_sem, recv_sem, local_sem):
        x, y, c, _ = _place()
        mine_half, other_half = _halves(c, 2 * hr, 8)
        mine = pltpu.make_async_copy(f_ref, out_ref.at[mine_half], local_sem)
        mine.start()

        def copy(half):
            return pltpu.make_async_remote_copy(
                src_ref=f_ref, dst_ref=out_ref.at[half], send_sem=send_sem, recv_sem=recv_sem,
                device_id=(x, y, 1 - c), device_id_type=MESH)

        send = copy(mine_half)
        send.start()
        copy(other_half).wait_recv()
        send.wait_send()
        mine.wait()

    return pl.pallas_call(
        body, name=name, out_shape=jax.ShapeDtypeStruct((2 * hr, n), f.dtype), in_specs=[HBM], out_specs=HBM,
        scratch_shapes=[pltpu.SemaphoreType.DMA, pltpu.SemaphoreType.DMA, pltpu.SemaphoreType.DMA],
    )(f)


def _reduce_shards(name, g, c_idx):
    a = _swap_half(name + "_swap", g)
    h = _add_half(name + "_pair", g, a, c_idx)
    b = _scatter_chips(name + "_scatter", h)
    f = _sum_chips(name + "_sum", b)
    return _share_half(name + "_share", f)


SEM = pl.BlockSpec(memory_space=pltpu.SEMAPHORE)
ANY = pl.BlockSpec(memory_space=pl.ANY)
DATAFLOW = pltpu.SideEffectType.DATAFLOW_SIDE_EFFECTING


def _remote(src, dst, send_sems, recv_sems, k, to):
    return pltpu.make_async_remote_copy(src_ref=src, dst_ref=dst, send_sem=send_sems.at[k], recv_sem=recv_sems.at[k],
                                        device_id=to, device_id_type=MESH)


def _split_copy(name, src, land_shape, land_dtype, n, plan, after=None):
    after = jnp.zeros((8, LANES), F32) if after is None else after

    def start_body(src_ref, land_ref, after_ref, send_sems, recv_sems, src_thru, land_thru, token):
        for cp in plan(src_ref, land_ref, send_sems, recv_sems)[0]:
            cp.start()
        token[...] = jnp.zeros_like(token)

    sems = pltpu.SemaphoreType.DMA((n,))
    send_sems, recv_sems, src_thru, land_thru, token = pl.pallas_call(
        start_body, name=name + "_start",
        out_shape=(sems, sems, pltpu.HBM(src.shape, src.dtype), pltpu.HBM(land_shape, land_dtype),
                   jax.ShapeDtypeStruct((8, LANES), F32)),
        in_specs=(HBM, HBM, ANY), out_specs=(SEM, SEM, HBM, HBM, pl.BlockSpec(memory_space=pltpu.VMEM)),
        input_output_aliases={0: 2, 1: 3}, compiler_params=pltpu.CompilerParams(has_side_effects=DATAFLOW),
    )(pltpu.with_memory_space_constraint(src, pltpu.HBM),
      pltpu.with_memory_space_constraint(lax.empty(land_shape, land_dtype), pltpu.HBM), after)

    def wait(after):
        def wait_body(src_ref, land_ref, send_sems, recv_sems, after_ref, src_out, land_out):
            sent, received = plan(src_ref, land_ref, send_sems, recv_sems)
            for cp in sent:
                cp.wait_send()
            for cp in received:
                cp.wait_recv()

        return pl.pallas_call(
            wait_body, name=name + "_wait",
            out_shape=(pltpu.HBM(src.shape, src.dtype), pltpu.HBM(land_shape, land_dtype)),
            in_specs=(HBM, HBM, SEM, SEM, ANY), out_specs=(HBM, HBM), input_output_aliases={0: 0, 1: 1},
            compiler_params=pltpu.CompilerParams(has_side_effects=DATAFLOW),
        )(src_thru, land_thru, send_sems, recv_sems, after)

    return token, wait


def _behind(x, token):
    return x + token[0, 0]


def _plan_gather(src_ref, land_ref, send_sems, recv_sems):
    x, y, c, chips = _place()
    sent = [_remote(src_ref, land_ref.at[2 * x + y], send_sems, recv_sems, j, (px, py, c))
            for j, (px, py) in enumerate(chips)]
    received = [_remote(src_ref, land_ref.at[2 * px + py], send_sems, recv_sems, j, (px, py, c))
                for j, (px, py) in enumerate(chips)]
    return sent, received


def _plan_swap(src_ref, land_ref, send_sems, recv_sems):
    x, y, c, _ = _place()
    _, other_half = _halves(c, src_ref.shape[1], 8)
    cp = _remote(src_ref.at[pl.ds(0, src_ref.shape[0]), other_half], land_ref, send_sems, recv_sems, 0, (x, y, 1 - c))
    return [cp], [cp]


def _plan_scatter(src_ref, land_ref, send_sems, recv_sems):
    x, y, c, chips = _place()
    sent = [_remote(src_ref.at[2 * px + py], land_ref.at[2 * x + y], send_sems, recv_sems, j, (px, py, c))
            for j, (px, py) in enumerate(chips)]
    received = [_remote(src_ref.at[2 * px + py], land_ref.at[2 * px + py], send_sems, recv_sems, j, (px, py, c))
                for j, (px, py) in enumerate(chips)]
    return sent, received


def _plan_share(src_ref, land_ref, send_sems, recv_sems):
    x, y, c, _ = _place()
    mine_half, other_half = _halves(c, land_ref.shape[0], 8)
    return ([_remote(src_ref, land_ref.at[mine_half], send_sems, recv_sems, 0, (x, y, 1 - c))],
            [_remote(src_ref, land_ref.at[other_half], send_sems, recv_sems, 0, (x, y, 1 - c))])


def _pack_shard_rows(name, parts):
    rows = [t.shape[0] // N_CHIP for t in parts]
    n, total = parts[0].shape[1], sum(t.shape[0] // N_CHIP for t in parts)
    slab, at = None, 0
    for i, (t, r) in enumerate(zip(parts, rows)):
        tr = max(c for c in range(8, min(r, 512) + 1, 8) if r % c == 0 and at % c == 0)
        nb, ob = r // tr, at // tr

        def body(t_ref, *rest):
            rest[-1][0] = t_ref[...]

        slab = pl.pallas_call(
            body, name=f"{name}_{i}", grid=(N_CHIP, nb), out_shape=jax.ShapeDtypeStruct((N_CHIP, total, n), t.dtype),
            in_specs=[pl.BlockSpec((tr, n), functools.partial(lambda k, j, nb: (k * nb + j, 0), nb=nb))]
            + ([] if slab is None else [pl.BlockSpec(memory_space=pl.ANY)]),
            out_specs=pl.BlockSpec((1, tr, n), functools.partial(lambda k, j, ob: (k, ob + j, 0), ob=ob)),
            input_output_aliases={} if slab is None else {1: 0}, compiler_params=_cp("parallel", "parallel"),
        )(*((t,) if slab is None else (t, slab)))
        at += r
    return slab


class _GatherBehind:
    def __init__(self, name, shard, chip, after=None):
        self.chip = chip
        self.token, self.wait = _split_copy(name, shard, (N_CHIP,) + shard.shape, shard.dtype, 3, _plan_gather,
                                            after)

    def result(self, after):
        shard, land = self.wait(after)
        return lax.dynamic_update_slice(land, shard[None], (self.chip, 0, 0))


class _ReduceBehind:
    def __init__(self, name, chip, c, c_idx):
        self.name, self.chip, self.c, self.c_idx = name, chip, c, c_idx

    def start(self, *grads):
        return self.start_slab(_pack_shard_rows(self.name + "_pack", grads))

    def start_slab(self, g):
        n_sh, rows, n = g.shape
        token, self.wait = _split_copy(self.name + "_swap", g, (n_sh, rows // 2, n), g.dtype, 1, _plan_swap)
        return token

    def pair(self, after):
        g, a = self.wait(after)
        h = _add_half(self.name + "_pair", g, a, self.c_idx)
        token, self.wait = _split_copy(self.name + "_scatter", h, h.shape, h.dtype, 3, _plan_scatter)
        return token

    def total(self, after):
        h, b = self.wait(after)
        b = lax.dynamic_update_slice(b, lax.dynamic_slice_in_dim(h, self.chip, 1, axis=0), (self.chip, 0, 0))
        f = _sum_chips(self.name + "_sum", b)
        token, self.wait = _split_copy(self.name + "_share", f, (2 * f.shape[0], f.shape[1]), f.dtype, 1,
                                       _plan_share)
        return token

    def result(self, after):
        f, out = self.wait(after)
        return lax.dynamic_update_slice(out, f, (self.c * f.shape[0], 0))


class _ReduceColsBehind(_ReduceBehind):
    def start(self, g_padded):
        g = _unpack_w_in_grad(g_padded)
        n = g.shape[1] // N_CHIP
        return self.start_slab(jnp.stack([g[:, k * n:(k + 1) * n] for k in range(N_CHIP)]))


def _f_adamw(w, g, m, v):
    m = ADAM_B1 * m + (1.0 - ADAM_B1) * g
    v = ADAM_B2 * v + (1.0 - ADAM_B2) * (g * g)
    m_hat = m / (1.0 - ADAM_B1 ** ADAM_STEP)
    v_hat = v / (1.0 - ADAM_B2 ** ADAM_STEP)
    return -ADAM_LR * (m_hat / (jnp.sqrt(v_hat) + ADAM_EPS) + ADAM_WD * w), m, v


def _adamw(name, w, g, m, v):
    rows, n = w.shape
    return _rowwise(name, lambda w, g, m, v: (_f_adamw(w, g, m, v), ()), rows, [(t, n, 0) for t in (w, g, m, v)], [],
                    [(n, F32)] * 3, [], tm=_row_tile(rows, 8, 256))


def _pack_rows(parts):
    rows = []
    for t in parts:
        t = t.reshape(-1)
        rows.append(jnp.pad(t, (0, -t.shape[0] % LANES)).reshape(-1, LANES))
    out = jnp.concatenate(rows, axis=0)
    return jnp.pad(out, ((0, -out.shape[0] % 8), (0, 0)))


def _unpack_rows(packed, shapes):
    out, r = [], 0
    for shp in shapes:
        n = int(np.prod(shp))
        nr = -(-n // LANES)
        out.append(packed[r:r + nr].reshape(-1)[:n].reshape(shp))
        r += nr
    return out


def _sum_blocks(name, g):
    def body(g_ref, o_ref):
        acc = g_ref[0]
        for k in range(1, g.shape[0]):
            acc = acc + g_ref[k]
        o_ref[...] = acc

    return pl.pallas_call(body, name=name, out_shape=jax.ShapeDtypeStruct(g.shape[1:], F32))(g)


def _silu(t):
    return t * _sigmoid(t)


def _ada_fwd(cc, w_ada):
    n = w_ada.shape[1]
    tn = _row_tile(n, LANES, 512)

    def body(cc_ref, w_ref, o_ref):
        o_ref[...] = _nn(_silu(cc_ref[...]), w_ref[...])

    return pl.pallas_call(
        body, name="ada_fwd", grid=(n // tn,), out_shape=jax.ShapeDtypeStruct((cc.shape[0], n), F32),
        in_specs=[pl.BlockSpec(cc.shape, lambda j: (0, 0)), pl.BlockSpec((w_ada.shape[0], tn), lambda j: (0, j))],
        out_specs=pl.BlockSpec((cc.shape[0], tn), lambda j: (0, j)), compiler_params=_cp("parallel"),
    )(cc, w_ada)


def _ada_bwd(cc, dm, w_ada):
    d, n = w_ada.shape
    tn = _row_tile(n, LANES, 512)

    def body(cc_ref, dm_ref, w_ref, gw_ref, ds_ref):
        @pl.when(pl.program_id(0) == 0)
        def _():
            ds_ref[...] = jnp.zeros_like(ds_ref)

        gw_ref[...] = _raw_dot("tn", _silu(cc_ref[...]), dm_ref[...], True)
        ds_ref[...] += _raw_dot("nt", dm_ref[...], w_ref[...], False)

    return pl.pallas_call(
        body, name="ada_bwd", grid=(n // tn,),
        out_shape=[jax.ShapeDtypeStruct((d, n), F32), jax.ShapeDtypeStruct(cc.shape, F32)],
        in_specs=[pl.BlockSpec(cc.shape, lambda j: (0, 0)), pl.BlockSpec((cc.shape[0], tn), lambda j: (0, j)),
                  pl.BlockSpec((d, tn), lambda j: (0, j))],
        out_specs=[pl.BlockSpec((d, tn), lambda j: (0, j)), pl.BlockSpec(cc.shape, lambda j: (0, 0))],
        compiler_params=_cp("arbitrary"),
    )(cc, dm, w_ada)


def _c_ctx_grad(parts, c_ctx):
    def body(p_ref, c_ref, o_ref):
        ds = ((p_ref[0] + p_ref[1]) + p_ref[2]) + p_ref[3]
        _, vjp = jax.vjp(_silu, c_ref[...])
        o_ref[...] = vjp(ds)[0]

    return pl.pallas_call(body, name="c_ctx_grad", out_shape=jax.ShapeDtypeStruct(c_ctx.shape, F32))(parts, c_ctx)


def kernel(x, c, ctx, c_ctx, w_ada, b_ada, g_pre_mix, g_post_mix, g_pre_ffn, g_post_ffn, w_in, attn_sink, w_gate_fwd, b_gate_fwd, w_gate_bwd, b_gate_bwd, g_gla_norm, w_out, w_ffn_in, w_ffn_out, loss_target, m_c_ctx, m_w_ada, m_b_ada, m_g_pre_mix, m_g_post_mix, m_g_pre_ffn, m_g_post_ffn, m_w_in, m_attn_sink, m_w_gate_fwd, m_b_gate_fwd, m_w_gate_bwd, m_b_gate_bwd, m_g_gla_norm, m_w_out, m_w_ffn_in, m_w_ffn_out, v_c_ctx, v_w_ada, v_b_ada, v_g_pre_mix, v_g_post_mix, v_g_pre_ffn, v_g_post_ffn, v_w_in, v_attn_sink, v_w_gate_fwd, v_b_gate_fwd, v_w_gate_bwd, v_b_gate_bwd, v_g_gla_norm, v_w_out, v_w_ffn_in, v_w_ffn_out):
    xi, yi, ci = lax.axis_index("x"), lax.axis_index("y"), lax.axis_index("c")
    dev, chip = 4 * xi + 2 * yi + ci, 2 * xi + yi
    c_idx = jnp.reshape(ci, (1,)).astype(jnp.int32)
    d = x.shape[-1]
    n_ada, n_in, n_f = w_ada.shape[-1], w_in.shape[-1], w_ffn_in.shape[-1]
    r_out, r_f = w_out.shape[1], w_ffn_out.shape[1]
    n_gate = w_gate_fwd.shape[-1]
    by_chip = lambda t: t[0::2]

    w_in_g = _ag_shards("gather_w_in", w_in[0].astype(BF16))

    rc = -(-d // LANES)
    g1 = _ag_small("gather_cond", _pack_rows([c[0], w_gate_fwd[0], w_gate_bwd[0]]), w_in_g)
    c_all = g1[:, :rc].reshape(N_DEV, -1)[:, :d]
    gr = GATE_RANK * n_gate // LANES
    gate_full = lambda off: jnp.transpose(by_chip(g1)[:, off:off + gr].reshape(N_CHIP, GATE_RANK, n_gate),
                                          (1, 0, 2)).reshape(GATE_RANK, N_CHIP * n_gate)
    wgf, wgb = gate_full(rc), gate_full(rc + gr)
    cc = jnp.concatenate([c_all, c_ctx[None, :], jnp.zeros((7, d), F32)], axis=0)

    g2 = _ag_small("gather_ada", _ada_fwd(cc, w_ada[0]).reshape(-1, LANES))
    ada_all = jnp.transpose(by_chip(g2).reshape(N_CHIP, 16, n_ada), (1, 0, 2)).reshape(16, N_CHIP * n_ada) + b_ada
    late = _GatherBehind("gather_late", jnp.concatenate(
        [w_out[0], w_ffn_out[0], jnp.transpose(w_ffn_in[0])], axis=0).astype(BF16), chip, g2)

    def late_weights(after):
        t = late.result(after)
        r1, r2 = r_out, r_out + r_f
        return (t[:, :r1].reshape(N_CHIP * r_out, d), t[:, r2:].reshape(N_CHIP * n_f, d),
                t[:, r1:r2].reshape(N_CHIP * r_f, d))

    ada_all = _behind(ada_all, late.token)
    ada = lax.dynamic_slice(ada_all, (dev, 0), (1, N_CHIP * n_ada))
    ada_c = ada_all[N_DEV:N_DEV + 1]

    w = _prep_weights(jnp.concatenate([w_in_g[k] for k in range(N_CHIP)], axis=1), wgf, wgb)
    w.update(g_pre_mix=g_pre_mix, g_post_mix=g_post_mix, g_pre_ffn=g_pre_ffn, g_post_ffn=g_post_ffn,
             attn_sink=attn_sink, b_gate_fwd=b_gate_fwd, b_gate_bwd=b_gate_bwd, g_gla_norm=g_gla_norm)

    reduce_behind = _ReduceBehind("reduce_late", chip, ci, c_idx)
    reduce_w_in = _ReduceColsBehind("reduce_w_in", chip, ci, c_idx)
    loss_lanes, grad_x, g, d_ada, d_ada_c = _local_step(x[0], ctx[0], loss_target[0], ada, ada_c, w, late_weights,
                                                        reduce_behind, reduce_w_in)

    small = ("g_pre_mix", "g_post_mix", "g_pre_ffn", "g_post_ffn", "attn_sink", "b_gate_fwd", "b_gate_bwd",
             "g_gla_norm", "w_gate_fwd", "w_gate_bwd")
    shapes = [(1, 6 * d)] * 2 + [g[n].shape for n in small] + [(1, LANES)]
    g3 = _ag_small("gather_small_grads", _pack_rows([d_ada, d_ada_c] + [g[n] for n in small] + [loss_lanes]))
    tot = dict(zip(("d_ada", "d_ada_c") + small + ("loss",),
                   _unpack_rows(_sum_blocks("sum_small_grads", g3), shapes)))
    r_ada = 6 * d // LANES
    dm = jnp.concatenate([g3[:, :r_ada].reshape(N_DEV, 6 * d), tot["d_ada_c"], jnp.zeros((7, 6 * d), F32)], axis=0)
    grads = {n: tot[n] for n in small[:8]}
    grads["b_ada"] = _sum_blocks("sum_b_ada", dm.reshape(16, r_ada, LANES)).reshape(1, 6 * d)
    grads["w_gate_fwd"] = lax.dynamic_slice(tot["w_gate_fwd"], (0, chip * n_gate), (GATE_RANK, n_gate))[None]
    grads["w_gate_bwd"] = lax.dynamic_slice(tot["w_gate_bwd"], (0, chip * n_gate), (GATE_RANK, n_gate))[None]
    gw_ada, dsc = _ada_bwd(cc, lax.dynamic_slice(dm, (0, chip * n_ada), (16, n_ada)), w_ada[0])
    grads["w_ada"] = gw_ada[None]
    g4 = _ag_small("gather_c_ctx", _pack_rows([dsc[N_DEV]]))
    grads["c_ctx"] = _c_ctx_grad(by_chip(g4), _pack_rows([c_ctx])).reshape(-1)[:d]

    grads["w_in"] = reduce_w_in.result(g4)[None]
    behind = g["behind"]
    grads["w_ffn_in"] = jnp.transpose(behind[:n_f])[None]
    grads["w_ffn_out"], grads["w_out"] = behind[None, n_f:n_f + r_f], behind[None, n_f + r_f:]

    names = ("c_ctx", "w_ada", "b_ada", "g_pre_mix", "g_post_mix", "g_pre_ffn", "g_post_ffn", "w_in", "attn_sink",
             "w_gate_fwd", "b_gate_fwd", "w_gate_bwd", "b_gate_bwd", "g_gla_norm", "w_out", "w_ffn_in", "w_ffn_out")
    weights = dict(zip(names, (c_ctx, w_ada, b_ada, g_pre_mix, g_post_mix, g_pre_ffn, g_post_ffn, w_in, attn_sink,
                               w_gate_fwd, b_gate_fwd, w_gate_bwd, b_gate_bwd, g_gla_norm, w_out, w_ffn_in,
                               w_ffn_out)))
    m_in = dict(zip(names, (m_c_ctx, m_w_ada, m_b_ada, m_g_pre_mix, m_g_post_mix, m_g_pre_ffn, m_g_post_ffn, m_w_in,
                            m_attn_sink, m_w_gate_fwd, m_b_gate_fwd, m_w_gate_bwd, m_b_gate_bwd, m_g_gla_norm,
                            m_w_out, m_w_ffn_in, m_w_ffn_out)))
    v_in = dict(zip(names, (v_c_ctx, v_w_ada, v_b_ada, v_g_pre_mix, v_g_post_mix, v_g_pre_ffn, v_g_post_ffn, v_w_in,
                            v_attn_sink, v_w_gate_fwd, v_b_gate_fwd, v_w_gate_bwd, v_b_gate_bwd, v_g_gla_norm,
                            v_w_out, v_w_ffn_in, v_w_ffn_out)))
    large = ("w_ada", "w_in", "w_out", "w_ffn_in", "w_ffn_out")
    tiny = tuple(n for n in names if n not in large)
    delta, new_m, new_v = {}, {}, {}
    for n in large:
        dl, nm, nv = _adamw("adamw_" + n, weights[n][0], grads[n][0], m_in[n][0], v_in[n][0])
        delta[n], new_m[n], new_v[n] = dl[None], nm[None], nv[None]
    tiny_shapes = [weights[n].shape for n in tiny]
    packed = [_pack_rows([t[n] for n in tiny]) for t in (weights, grads, m_in, v_in)]
    for out, res in zip((delta, new_m, new_v), _adamw("adamw_small", *packed)):
        out.update(zip(tiny, _unpack_rows(res, tiny_shapes)))
    for n in tiny:
        grads[n] = grads[n].reshape(weights[n].shape)

    return (tot["loss"][0, 0], grad_x[None], *[grads[n] for n in names], *[delta[n] for n in names], *[new_m[n] for n in names],
            *[new_v[n] for n in names])
```

```python
import functools

import jax
import jax.numpy as jnp
import numpy as np
from jax import lax
from jax.experimental import pallas as pl
from jax.experimental.pallas import tpu as pltpu

F32 = jnp.float32
BF16 = jnp.bfloat16
MESH = pl.DeviceIdType.MESH

HEAD_DIM = 64
ATT_HEADS = 8
ATT_KV_HEADS = 2
ATT_GROUP = ATT_HEADS // ATT_KV_HEADS
WINDOW = 128
BLOCK = 128
GRID_W = 64
ROPE_BASE = 10000.0
GLA_HEADS = 8
GLA_DK = 32
GLA_DV = 64
GLA_CHUNK = 64
GATE_RANK = 16
GATE_TAU = 16.0
NEG_INF = -1e30
QW = ATT_HEADS * HEAD_DIM
KVW = ATT_KV_HEADS * HEAD_DIM
GKW = GLA_HEADS * GLA_DK
GVW = GLA_HEADS * GLA_DV
IN_COLS = QW + 2 * KVW + 2 * GKW + 2 * GVW + 2 * GATE_RANK
LANES = 128
IN_PAD = IN_COLS + LANES - 2 * GATE_RANK
C_Q, C_GV, C_GG = 0, QW, QW + GVW
C_K = C_GG + GVW
C_V = C_K + KVW
C_GQ = C_V + KVW
C_GK = C_GQ + GKW
C_Z = C_GK + GKW
MIX = QW + GVW

ADAM_LR, ADAM_B1, ADAM_B2, ADAM_EPS, ADAM_WD, ADAM_STEP = 0.001, 0.9, 0.999, 1e-08, 0.01, 10

VMEM_LIMIT = 56 * 1024 * 1024


def _cp(*sem):
    return pltpu.CompilerParams(dimension_semantics=sem, vmem_limit_bytes=VMEM_LIMIT)


def _pick(n, cands):
    for t in cands:
        if n % t == 0:
            return t
    return n


_DIMS = {"nn": (((1,), (0,)), ((), ())), "nt": (((1,), (1,)), ((), ())), "tn": (((0,), (0,)), ((), ()))}


def _raw_dot(mode, a, b, hi):
    dot = lambda u, v: lax.dot_general(u, v, _DIMS[mode], preferred_element_type=F32)
    if hi:
        a, b = a.astype(F32), b.astype(F32)
        a_hi, b_hi = a.astype(BF16), b.astype(BF16)
        a_lo, b_lo = (a - a_hi.astype(F32)).astype(BF16), (b - b_hi.astype(F32)).astype(BF16)
        return dot(a_hi, b_hi) + (dot(a_lo, b_hi) + dot(a_hi, b_lo))
    return dot(a.astype(BF16), b.astype(BF16))


def _make_dot(mode, hi):
    @jax.custom_vjp
    def dot(a, b):
        return _raw_dot(mode, a, b, hi)

    def fwd(a, b):
        return _raw_dot(mode, a, b, hi), (a, b)

    def bwd(res, dc):
        a, b = res
        if mode == "nn":
            return _raw_dot("nt", dc, b, hi), _raw_dot("tn", a, dc, hi)
        if mode == "nt":
            return _raw_dot("nn", dc, b, hi), _raw_dot("tn", dc, a, hi)
        return _raw_dot("nt", b, dc, hi), _raw_dot("nn", a, dc, hi)

    dot.defvjp(fwd, bwd)
    return dot


_nn, _nt, _tn = _make_dot("nn", False), _make_dot("nt", False), _make_dot("tn", False)
_nn_hi = _make_dot("nn", True)


MM_VMEM_BUDGET = 44 * 1024 * 1024


def _halvings(n):
    out = [n]
    while out[-1] % (2 * LANES) == 0:
        out.append(out[-1] // 2)
    return out


def _mm_tiles(mode, m, n, k, a_bytes, b_bytes, o_bytes, init_bytes=0):
    tms = [t for t in dict.fromkeys((m, m // 2, m // 4, 2048, 1024, 512, 256, 128))
           if m % t == 0 and t % (LANES if mode == "tn" else 16) == 0 and t <= 4096] or [m]
    if mode == "tn":
        fits = [(k // tk + 0.5 * (m // tm), tm, tk)
                for tk in (4096, 2048, 1024, 512, 256, 128) if k % tk == 0 for tm in tms
                if 2 * (tk * tm * a_bytes + tk * n * b_bytes + tm * n * (o_bytes + init_bytes)) <= MM_VMEM_BUDGET]
        if fits:
            _, tm, tk = min(fits)
            return tm, n, tk
    tks = ([t for t in (512, 256, 128) if k % t == 0] or [k]) if mode == "tn" else _halvings(k)
    for tn in _halvings(n):
        for tk in tks:
            for tm in tms:
                acc = tm * tn * 4 if (k // tk > 1 and o_bytes != 4) else 0
                tiles = tm * tk * a_bytes + tk * tn * b_bytes + tm * tn * (o_bytes + init_bytes)
                if 2 * tiles + acc <= MM_VMEM_BUDGET:
                    return tm, tn, tk
    return tms[-1], _halvings(n)[-1], tks[-1]


def _mm(name, a, b, mode, out_dtype=F32, init=None, after=None):
    follow = () if after is None else (after,)
    if mode == "nn":
        (m, k), n = a.shape, b.shape[1]
    elif mode == "nt":
        (m, k), n = a.shape, b.shape[0]
    else:
        (k, m), n = a.shape, b.shape[1]
    tm, tn, tk = _mm_tiles(mode, m, n, k, a.dtype.itemsize, b.dtype.itemsize, jnp.dtype(out_dtype).itemsize,
                           0 if init is None else 4)
    nk = k // tk
    use_acc = nk > 1 and out_dtype != F32

    inits = () if init is None else (init,)

    def body(a_ref, b_ref, *rest):
        rest = rest[:len(inits)] + rest[len(inits) + len(follow):]
        o_ref, acc = rest[len(inits)], rest[len(inits) + 1:]
        part = _raw_dot(mode, a_ref[...], b_ref[...], False)
        first = lambda: part + rest[0][...] if inits else part
        if nk == 1:
            o_ref[...] = first().astype(o_ref.dtype)
            return
        acc_ref = acc[0] if use_acc else o_ref
        kk = pl.program_id(2)

        @pl.when(kk == 0)
        def _():
            acc_ref[...] = first()

        @pl.when(kk > 0)
        def _():
            acc_ref[...] += part

        if use_acc:
            @pl.when(kk == nk - 1)
            def _():
                o_ref[...] = acc_ref[...].astype(o_ref.dtype)

    if mode == "nn":
        a_spec = pl.BlockSpec((tm, tk), lambda i, j, kk: (i, kk))
        b_spec = pl.BlockSpec((tk, tn), lambda i, j, kk: (kk, j))
    elif mode == "nt":
        a_spec = pl.BlockSpec((tm, tk), lambda i, j, kk: (i, kk))
        b_spec = pl.BlockSpec((tn, tk), lambda i, j, kk: (j, kk))
    else:
        a_spec = pl.BlockSpec((tk, tm), lambda i, j, kk: (kk, i))
        b_spec = pl.BlockSpec((tk, tn), lambda i, j, kk: (kk, j))
    return pl.pallas_call(
        body, name=name, grid=(m // tm, n // tn, nk),
        in_specs=[a_spec, b_spec] + [pl.BlockSpec((tm, tn), lambda i, j, kk: (i, j))] * len(inits)
        + [pl.BlockSpec(memory_space=pl.ANY)] * len(follow),
        out_specs=pl.BlockSpec((tm, tn), lambda i, j, kk: (i, j)),
        out_shape=jax.ShapeDtypeStruct((m, n), out_dtype),
        scratch_shapes=[pltpu.VMEM((tm, tn), F32)] if use_acc else [],
        compiler_params=_cp("parallel", "parallel", "arbitrary"),
    )(a, b, *inits, *follow)


def _rowwise(name, fn, rows, row_ins, full_ins, row_outs, acc_outs, tm=None):
    tm = tm or _pick(rows, (512, 256, 128))
    n_r, n_f, n_o, n_a = len(row_ins), len(full_ins), len(row_outs), len(acc_outs)

    def body(*refs):
        ins, outs = refs[:n_r + n_f], refs[n_r + n_f:]
        vals = [r[...].astype(F32) for r in ins]
        ro, ao = fn(*vals)
        for r, val in zip(outs[:n_o], ro):
            r[...] = val.astype(r.dtype)
        if n_a:
            @pl.when(pl.program_id(0) == 0)
            def _():
                for r in outs[n_o:]:
                    r[...] = jnp.zeros_like(r)

            for r, val in zip(outs[n_o:], ao):
                r[...] += val

    in_specs = [pl.BlockSpec((tm, w), functools.partial(lambda i, cb: (i, cb), cb=cb)) for _, w, cb in row_ins]
    in_specs += [pl.BlockSpec(a.shape, lambda i: (0, 0)) for a in full_ins]
    out_specs = [pl.BlockSpec((tm, w), lambda i: (i, 0)) for w, _ in row_outs]
    out_specs += [pl.BlockSpec(s, lambda i: (0, 0)) for s in acc_outs]
    out_shape = [jax.ShapeDtypeStruct((rows, w), dt) for w, dt in row_outs]
    out_shape += [jax.ShapeDtypeStruct(s, F32) for s in acc_outs]
    return pl.pallas_call(
        body, name=name, grid=(rows // tm,), in_specs=in_specs, out_specs=out_specs, out_shape=out_shape,
        compiler_params=_cp("arbitrary" if n_a else "parallel"),
    )(*[a for a, _, _ in row_ins], *full_ins)


def _rn(x):
    return x * lax.rsqrt(jnp.mean(x * x, axis=-1, keepdims=True) + 1e-6)


def _sigmoid(t):
    return 1.0 / (1.0 + jnp.exp(-t))


def _f_norm_mod(x, g, sh, sc):
    return _rn(x) * g * (1.0 + sc) + sh


def _f_post_res(xr, y, g, gate):
    return xr + gate * (_rn(y) * g)


def _f_swiglu(g, u):
    return g * _sigmoid(g) * u


def _logsig(u):
    return jnp.minimum(u, 0.0) - jnp.log(1.0 + jnp.exp(-jnp.abs(u)))


def _f_gate(z, wf, wb, bf, bb):
    return _logsig(_nn(z, wf) + bf) / GATE_TAU, _logsig(_nn(z, wb) + bb) / GATE_TAU


def _f_gla_out(of, ob, gg, gt, bd):
    o = of + ob
    ms = _nn_hi(o * o, bd)
    return o * lax.rsqrt(ms + 1e-6) * gt * (gg * _sigmoid(gg))


def _norm_mod(name, x, g, sh, sc):
    rows, d = x.shape
    return _rowwise(name, lambda x, g, sh, sc: ((_f_norm_mod(x, g, sh, sc),), ()), rows,
                    [(x, d, 0)], [g, sh, sc], [(d, BF16)], [])[0]


def _norm_mod_bwd(name, dh, dres, x, g, sh, sc):
    rows, d = x.shape

    def fn(dh, dres, x, g, sh, sc):
        _, vjp = jax.vjp(_f_norm_mod, x, g, sh, sc)
        dx, dg, dsh, dsc = vjp(dh)
        return (dx + dres,), (dg, dsh, dsc)

    return _rowwise(name, fn, rows, [(dh, d, 0), (dres, d, 0), (x, d, 0)], [g, sh, sc], [(d, F32)],
                    [(1, d)] * 3)


def _post_res(name, xr, y, g, gate):
    rows, d = xr.shape
    return _rowwise(name, lambda xr, y, g, gate: ((_f_post_res(xr, y, g, gate),), ()), rows,
                    [(xr, d, 0), (y, d, 0)], [g, gate], [(d, F32)], [])[0]


def _post_res_bwd(name, dxo, y, g, gate):
    rows, d = y.shape

    def fn(dxo, y, g, gate):
        _, vjp = jax.vjp(lambda y, g, gate: _f_post_res(jnp.zeros_like(y), y, g, gate), y, g, gate)
        dy, dg, dgate = vjp(dxo)
        return (dy,), (dg, dgate)

    return _rowwise(name, fn, rows, [(dxo, d, 0), (y, d, 0)], [g, gate], [(d, BF16)], [(1, d)] * 2)


def _post_res_norm_mod(name, xr, y, g_post, gate, g_pre, sh, sc):
    rows, d = xr.shape

    def fn(xr, y, g_post, gate, g_pre, sh, sc):
        x1 = _f_post_res(xr, y, g_post, gate)
        return (x1, _f_norm_mod(x1, g_pre, sh, sc)), ()

    return _rowwise(name, fn, rows, [(xr, d, 0), (y, d, 0)], [g_post, gate, g_pre, sh, sc], [(d, F32), (d, BF16)], [])


def _norm_mod_post_res_bwd(name, dh, dres, x1, y, g_pre, sh, sc, g_post, gate):
    rows, d = x1.shape

    def fn(dh, dres, x1, y, g_pre, sh, sc, g_post, gate):
        _, vjp_norm = jax.vjp(_f_norm_mod, x1, g_pre, sh, sc)
        dx1, dg_pre, dsh, dsc = vjp_norm(dh)
        dx1 = dx1 + dres
        _, vjp_res = jax.vjp(lambda y, g, gate: _f_post_res(jnp.zeros_like(y), y, g, gate), y, g_post, gate)
        dy, dg_post, dgate = vjp_res(dx1)
        return (dx1, dy), (dg_pre, dsh, dsc, dg_post, dgate)

    return _rowwise(name, fn, rows, [(dh, d, 0), (dres, d, 0), (x1, d, 0), (y, d, 0)], [g_pre, sh, sc, g_post, gate],
                    [(d, F32), (d, BF16)], [(1, d)] * 5, tm=_pick(rows, (256, 128)))


def _post_res_loss(name, xr, y, g, gate, target):
    rows, d = xr.shape

    def fn(xr, y, target, g, gate):
        x2, vjp = jax.vjp(lambda y, g, gate: _f_post_res(xr, y, g, gate), y, g, gate)
        diff = x2 - target
        part = 0.5 * jnp.sum(jnp.mean(diff * diff, axis=-1, keepdims=True), axis=0, keepdims=True)
        dx2 = diff * (1.0 / d)
        dy, dg, dgate = vjp(dx2)
        return (dx2, dy), (jnp.broadcast_to(part, (1, LANES)), dg, dgate)

    return _rowwise(name, fn, rows, [(xr, d, 0), (y, d, 0), (target, d, 0)], [g, gate], [(d, F32), (d, BF16)],
                    [(1, LANES), (1, d), (1, d)])


def _mm_rows(name, a, b, mode, fn, extras, outs):
    m, k = a.shape
    tm = _pick(m, (256, 128))

    def body(a_ref, b_ref, *rest):
        tiles = fn(_raw_dot(mode, a_ref[...], b_ref[...], False), *[e[...] for e in rest[:len(extras)]])
        for r, val in zip(rest[len(extras):], tiles):
            r[...] = val.astype(r.dtype)

    row = lambda w: pl.BlockSpec((tm, w), lambda i: (i, 0))
    return pl.pallas_call(
        body, name=name, grid=(m // tm,),
        in_specs=[row(k), pl.BlockSpec(b.shape, lambda i: (0, 0))] + [row(e.shape[1]) for e in extras],
        out_specs=[row(w) for w, _ in outs], out_shape=[jax.ShapeDtypeStruct((m, w), dt) for w, dt in outs],
        compiler_params=_cp("parallel"),
    )(a, b, *extras)


def _ffn_in_swiglu(name, h, w_t):
    f = w_t.shape[0] // 2
    fn = lambda u: (u, _f_swiglu(u[:, :f], u[:, f:]))
    return _mm_rows(name, h, w_t, "nt", fn, [], [(2 * f, BF16), (f, BF16)])


def _ffn_out_dx_swiglu_bwd(name, df, w_out, u):
    f = w_out.shape[0]

    def fn(da, u):
        u = u.astype(F32)
        _, vjp = jax.vjp(_f_swiglu, u[:, :f], u[:, f:])
        return (jnp.concatenate(vjp(da), axis=1),)

    return _mm_rows(name, df, w_out, "nt", fn, [u], [(2 * f, BF16)])[0]


def _gate_fwd(name, p, wf, wb, bf, bb):
    rows = p.shape[0]
    return _rowwise(name, lambda z, wf, wb, bf, bb: (_f_gate(z, wf, wb, bf, bb), ()), rows,
                    [(p, LANES, C_Z // LANES)], [wf, wb, bf, bb], [(GKW, F32)] * 2, [])


def _gate_bwd(name, p, dla_f, dla_b, wf, wb, bf, bb):
    rows = p.shape[0]

    def fn(z, dlf, dlb, wf, wb, bf, bb):
        _, vjp = jax.vjp(_f_gate, z, wf, wb, bf, bb)
        dz, dwf, dwb, dbf, dbb = vjp((dlf, dlb))
        return (dz,), (dwf, dwb, dbf, dbb)

    return _rowwise(name, fn, rows, [(p, LANES, C_Z // LANES), (dla_f, GKW, 0), (dla_b, GKW, 0)],
                    [wf, wb, bf, bb], [(LANES, BF16)], [(LANES, GKW), (LANES, GKW), (1, GKW), (1, GKW)])


def _head_mean_matrix():
    h = np.arange(GVW) // GLA_DV
    return jnp.asarray((h[:, None] == h[None, :]).astype(np.float32) / GLA_DV)


def _gla_out(name, attn, of, ob, p, gt):
    rows = of.shape[0]
    bd = _head_mean_matrix()
    fn = lambda attn, of, ob, gg, gt, bd: ((jnp.concatenate([attn, _f_gla_out(of, ob, gg, gt, bd)], axis=1),), ())
    return _rowwise(name, fn, rows, [(attn, QW, 0), (of, GVW, 0), (ob, GVW, 0), (p, GVW, C_GG // GVW)], [gt, bd],
                    [(MIX, BF16)], [])[0]


def _gla_out_bwd(name, dmix, of, ob, p, gt):
    rows = of.shape[0]
    bd = _head_mean_matrix()

    def fn(dm, of, ob, gg, gt, bd):
        _, vjp = jax.vjp(lambda of, gg, gt: _f_gla_out(of, ob, gg, gt, bd), of, gg, gt)
        do, dgg, dgt = vjp(dm)
        return (do, dgg), (dgt,)

    return _rowwise(name, fn, rows, [(dmix, GVW, 1), (of, GVW, 0), (ob, GVW, 0), (p, GVW, C_GG // GVW)], [gt, bd],
                    [(GVW, F32), (GVW, BF16)], [(1, GVW)])


def _rope_tables(n_tokens):
    t = jnp.arange(n_tokens)
    row = (t // GRID_W).astype(F32)
    col = (t % GRID_W).astype(F32)
    half = HEAD_DIM // 2
    inv_freq = ROPE_BASE ** (-jnp.arange(0, half, 2, dtype=F32) / half)
    ang_r = row[:, None] * inv_freq[None, :]
    ang_c = col[:, None] * inv_freq[None, :]
    ang = jnp.concatenate([ang_r, ang_r, ang_c, ang_c], axis=-1)
    sign = jnp.concatenate([-jnp.ones((16,), F32), jnp.ones((16,), F32)] * 2)
    cos, sin = jnp.cos(ang), jnp.sin(ang) * sign[None, :]
    return jnp.tile(cos, (1, 2)), jnp.tile(sin, (1, 2))


def _rot_pairs(x):
    w = x.shape[-1]
    lane = lax.broadcasted_iota(jnp.int32, x.shape, x.ndim - 1)
    return jnp.where((lane % 32) < 16, pltpu.roll(x, w - 16, x.ndim - 1), pltpu.roll(x, 16, x.ndim - 1))


def _rope_apply(x, cos, sin_signed, inverse):
    reps = x.shape[-1] // LANES
    cos = jnp.concatenate([cos] * reps, axis=-1) if reps > 1 else cos
    sin = jnp.concatenate([sin_signed] * reps, axis=-1) if reps > 1 else sin_signed
    if inverse:
        return x * cos + _rot_pairs(x * sin)
    return x * cos + _rot_pairs(x) * sin


def _rope_fwd(name, p, cos, sin):
    rows = p.shape[0]

    def fn(q, k, v, cos, sin):
        return (_rope_apply(q, cos, sin, False), _rope_apply(k, cos, sin, False), v), ()

    return _rowwise(name, fn, rows, [(p, QW, 0), (p, KVW, C_K // KVW), (p, KVW, C_V // KVW), (cos, LANES, 0),
                                     (sin, LANES, 0)], [], [(QW, BF16), (KVW, BF16), (KVW, BF16)], [])


def _proj_grad(name, dq_rot, dk_rot, dv, cos, sin, gla_f, gla_b, dgg, dz):
    rows = dq_rot.shape[0]

    def fn(dq, dk, dv, cos, sin, gqf, gkf, gvf, gqb, gkb, gvb, dgg, dz):
        parts = [_rope_apply(dq, cos, sin, True), gvf + gvb, dgg, _rope_apply(dk, cos, sin, True), dv, gqf + gqb,
                 gkf + gkb, dz]
        return (jnp.concatenate(parts, axis=1),), ()

    ins = [(dq_rot, QW), (dk_rot, KVW), (dv, KVW), (cos, LANES), (sin, LANES)]
    ins += [(t, t.shape[1]) for t in (*gla_f, *gla_b)] + [(dgg, GVW), (dz, LANES)]
    return _rowwise(name, fn, rows, [(t, w, 0) for t, w in ins], [], [(IN_PAD, BF16)], [],
                    tm=_pick(rows, (256, 128)))[0]


GROUP_ROWS = ATT_GROUP * BLOCK


def _f_attn(qs, kws, vws, kcs, vcs, sink, n, n_tokens):
    row = lax.broadcasted_iota(jnp.int32, (GROUP_ROWS, 1), 0)
    group = sum((row >= g * BLOCK).astype(jnp.int32) for g in range(1, ATT_GROUP))
    i = lax.broadcasted_iota(jnp.int32, (GROUP_ROWS, 3 * BLOCK), 0) - BLOCK * group
    j = lax.broadcasted_iota(jnp.int32, (GROUP_ROWS, 3 * BLOCK), 1)
    kpos = (n - 1) * BLOCK + j
    mask = (jnp.abs(j - BLOCK - i) <= WINDOW) & (kpos >= 0) & (kpos < n_tokens)
    head_id = lax.broadcasted_iota(jnp.int32, (1, ATT_HEADS), 1)
    scale = HEAD_DIM ** -0.5
    outs = []
    for h in range(ATT_KV_HEADS):
        sk = jnp.zeros((GROUP_ROWS, 1), F32)
        for g in range(ATT_GROUP):
            one = jnp.sum(jnp.where(head_id == h * ATT_GROUP + g, sink, 0.0), axis=-1, keepdims=True)
            sk = jnp.where(group == g, one, sk)
        q = qs[h] * scale
        s_w = jnp.where(mask, _nt(q, kws[h]), NEG_INF)
        s_c = _nt(q, kcs[h])
        m = lax.stop_gradient(jnp.maximum(jnp.maximum(jnp.max(s_w, axis=-1, keepdims=True),
                                                      jnp.max(s_c, axis=-1, keepdims=True)), sk))
        pw, pc = jnp.exp(s_w - m), jnp.exp(s_c - m)
        den = jnp.sum(pw, axis=-1, keepdims=True) + jnp.sum(pc, axis=-1, keepdims=True) + jnp.exp(sk - m)
        outs.append((_nn(pw, vws[h]) + _nn(pc, vcs[h])) / den)
    return tuple(outs)


def _group_rows(ref, h):
    hs = lambda hq: slice(hq * HEAD_DIM, (hq + 1) * HEAD_DIM)
    return jnp.concatenate([ref[:, hs(h * ATT_GROUP + g)].astype(F32) for g in range(ATT_GROUP)], axis=0)


def _ungroup_rows(ref, h, val):
    for g in range(ATT_GROUP):
        hq = h * ATT_GROUP + g
        ref[:, hq * HEAD_DIM:(hq + 1) * HEAD_DIM] = val[g * BLOCK:(g + 1) * BLOCK].astype(ref.dtype)


def _attn_loads(n, q_ref, kp_ref, vp_ref, kc_ref, vc_ref):
    r0 = pl.multiple_of(n * BLOCK, BLOCK)
    hs = lambda h: slice(h * HEAD_DIM, (h + 1) * HEAD_DIM)
    qs = [_group_rows(q_ref, h) for h in range(ATT_KV_HEADS)]
    kws = [kp_ref[pl.ds(r0, 3 * BLOCK), hs(h)].astype(F32) for h in range(ATT_KV_HEADS)]
    vws = [vp_ref[pl.ds(r0, 3 * BLOCK), hs(h)].astype(F32) for h in range(ATT_KV_HEADS)]
    kcs = [kc_ref[:, hs(h)].astype(F32) for h in range(ATT_KV_HEADS)]
    vcs = [vc_ref[:, hs(h)].astype(F32) for h in range(ATT_KV_HEADS)]
    return r0, hs, qs, kws, vws, kcs, vcs


def _attn_specs(s, c):
    full = lambda shape: pl.BlockSpec(shape, lambda n: (0, 0))
    return [pl.BlockSpec((BLOCK, QW), lambda n: (n, 0)), full((s + 2 * BLOCK, KVW)), full((s + 2 * BLOCK, KVW)),
            full((c, KVW)), full((c, KVW)), full((1, ATT_HEADS))]


def _attn_fwd(q, kp, vp, kc, vc, sink):
    s, c = q.shape[0], kc.shape[0]

    def body(q_ref, kp_ref, vp_ref, kc_ref, vc_ref, sink_ref, o_ref):
        n = pl.program_id(0)
        _, hs, qs, kws, vws, kcs, vcs = _attn_loads(n, q_ref, kp_ref, vp_ref, kc_ref, vc_ref)
        outs = _f_attn(qs, kws, vws, kcs, vcs, sink_ref[...], n, s)
        for h in range(ATT_KV_HEADS):
            _ungroup_rows(o_ref, h, outs[h])

    return pl.pallas_call(
        body, name="attn_fwd", grid=(s // BLOCK,), in_specs=_attn_specs(s, c),
        out_specs=pl.BlockSpec((BLOCK, QW), lambda n: (n, 0)), out_shape=jax.ShapeDtypeStruct((s, QW), BF16),
        compiler_params=_cp("parallel"),
    )(q, kp, vp, kc, vc, sink)


def _attn_bwd(do, q, kp, vp, kc, vc, sink):
    s, c = q.shape[0], kc.shape[0]

    def body(do_ref, q_ref, kp_ref, vp_ref, kc_ref, vc_ref, sink_ref, dq_ref, dkp_ref, dvp_ref, dkc_ref, dvc_ref,
             dsink_ref):
        n = pl.program_id(0)

        @pl.when(n == 0)
        def _():
            for r in (dkp_ref, dvp_ref, dkc_ref, dvc_ref, dsink_ref):
                r[...] = jnp.zeros_like(r)

        r0, hs, qs, kws, vws, kcs, vcs = _attn_loads(n, q_ref, kp_ref, vp_ref, kc_ref, vc_ref)
        _, vjp = jax.vjp(lambda qs, kws, vws, kcs, vcs, sink: _f_attn(qs, kws, vws, kcs, vcs, sink, n, s),
                         qs, kws, vws, kcs, vcs, sink_ref[...])
        dqs, dkws, dvws, dkcs, dvcs, dsink = vjp(tuple(_group_rows(do_ref, h) for h in range(ATT_KV_HEADS)))
        for h in range(ATT_KV_HEADS):
            _ungroup_rows(dq_ref, h, dqs[h])
            dkp_ref[pl.ds(r0, 3 * BLOCK), hs(h)] += dkws[h]
            dvp_ref[pl.ds(r0, 3 * BLOCK), hs(h)] += dvws[h]
            dkc_ref[:, hs(h)] += dkcs[h]
            dvc_ref[:, hs(h)] += dvcs[h]
        dsink_ref[...] += dsink

    full = lambda shape: pl.BlockSpec(shape, lambda n: (0, 0))
    return pl.pallas_call(
        body, name="attn_bwd", grid=(s // BLOCK,),
        in_specs=[pl.BlockSpec((BLOCK, QW), lambda n: (n, 0))] + _attn_specs(s, c),
        out_specs=[pl.BlockSpec((BLOCK, QW), lambda n: (n, 0)), full((s + 2 * BLOCK, KVW)), full((s + 2 * BLOCK, KVW)),
                   full((c, KVW)), full((c, KVW)), full((1, ATT_HEADS))],
        out_shape=[jax.ShapeDtypeStruct((s, QW), F32), jax.ShapeDtypeStruct((s + 2 * BLOCK, KVW), F32),
                   jax.ShapeDtypeStruct((s + 2 * BLOCK, KVW), F32), jax.ShapeDtypeStruct((c, KVW), F32),
                   jax.ShapeDtypeStruct((c, KVW), F32), jax.ShapeDtypeStruct((1, ATT_HEADS), F32)],
        compiler_params=_cp("arbitrary"),
    )(do, q, kp, vp, kc, vc, sink)


def _gla_masks():
    hk = np.arange(GKW) // GLA_DK
    hv = np.arange(GVW) // GLA_DV
    head_k = (np.arange(GLA_HEADS)[:, None] == hk[None, :]).astype(np.float32)
    head_v = (np.arange(GLA_HEADS)[:, None] == hv[None, :]).astype(np.float32)
    bd_t = (hv[:, None] == hk[None, :]).astype(np.float32)
    return jnp.asarray(head_k), jnp.asarray(head_v), jnp.asarray(bd_t)


def _tri(n, rev, strict=False):
    i = lax.broadcasted_iota(jnp.int32, (n, n), 0)
    j = lax.broadcasted_iota(jnp.int32, (n, n), 1)
    if strict:
        keep = (j > i) if rev else (j < i)
    else:
        keep = (j >= i) if rev else (j <= i)
    return keep


def _f_gla_chunk(q, k, v, la, st, head_k, head_v, bd_t, rev):
    keep = _tri(GLA_CHUNK, rev)
    b = _nn_hi(keep.astype(F32), la)
    bl = jnp.sum(la, axis=0, keepdims=True)
    qd = q * (GLA_DK ** -0.5) * jnp.exp(b)
    ki = k * jnp.exp(-b)
    kd = k * jnp.exp(bl - b)
    q_heads = (qd[None, :, :] * head_k[:, None, :]).reshape(GLA_HEADS * GLA_CHUNK, GKW)
    a_all = _nt(q_heads, ki).reshape(GLA_HEADS, GLA_CHUNK, GLA_CHUNK)
    a_all = jnp.where(keep[None, :, :], a_all, 0.0).reshape(GLA_HEADS * GLA_CHUNK, GLA_CHUNK)
    o_all = _nn(a_all, v).reshape(GLA_HEADS, GLA_CHUNK, GVW)
    intra = jnp.sum(o_all * head_v[:, None, :], axis=0)
    inter = _nt(qd, st)
    st_new = st * jnp.exp(bl) + bd_t * _tn(v, kd)
    return intra + inter, st_new


def _gla_specs(s, tb, order):
    return [pl.BlockSpec((tb, GKW), lambda i: (order(i), C_GQ // GKW)),
            pl.BlockSpec((tb, GKW), lambda i: (order(i), C_GK // GKW)),
            pl.BlockSpec((tb, GVW), lambda i: (order(i), C_GV // GVW)),
            pl.BlockSpec((tb, GKW), lambda i: (order(i), 0))]


GLA_BLOCK_CHUNKS = 4


def _gla_fwd(p, la_f, la_b, st_f0, st_b0):
    s = p.shape[0]
    tb = GLA_BLOCK_CHUNKS * GLA_CHUNK
    nblk = s // tb
    up, down = (lambda i: i), (lambda i: nblk - 1 - i)
    masks = _gla_masks()

    def scan(rev, q_ref, k_ref, v_ref, la_ref, o_ref, sts_ref, st_ref, consts):
        st = st_ref[...]
        sts_ref[0] = st
        chunks = range(GLA_BLOCK_CHUNKS)
        for ci in (reversed(chunks) if rev else chunks):
            rows = slice(ci * GLA_CHUNK, (ci + 1) * GLA_CHUNK)
            o, st = _f_gla_chunk(q_ref[rows, :], k_ref[rows, :], v_ref[rows, :], la_ref[rows, :], st, *consts, rev)
            o_ref[rows, :] = o
        st_ref[...] = st

    def body(qf, kf, vf, laf, qb, kb, vb, lab, stf0, stb0, hk_ref, hv_ref, bd_ref, of_ref, stsf_ref, ob_ref, stsb_ref,
             stf_ref, stb_ref):
        @pl.when(pl.program_id(0) == 0)
        def _():
            stf_ref[...] = stf0[...]
            stb_ref[...] = stb0[...]

        consts = (hk_ref[...], hv_ref[...], bd_ref[...])
        scan(False, qf, kf, vf, laf, of_ref, stsf_ref, stf_ref, consts)
        scan(True, qb, kb, vb, lab, ob_ref, stsb_ref, stb_ref, consts)

    full = lambda a: pl.BlockSpec(a.shape, lambda i: (0,) * a.ndim)
    outs = lambda order: [pl.BlockSpec((tb, GVW), lambda i: (order(i), 0)),
                          pl.BlockSpec((1, GVW, GKW), lambda i: (order(i), 0, 0))]
    return pl.pallas_call(
        body, name="gla_fwd", grid=(nblk,),
        in_specs=_gla_specs(s, tb, up) + _gla_specs(s, tb, down) + [full(st_f0), full(st_b0)]
        + [full(m) for m in masks],
        out_specs=outs(up) + outs(down),
        out_shape=[jax.ShapeDtypeStruct((s, GVW), F32), jax.ShapeDtypeStruct((nblk, GVW, GKW), F32)] * 2,
        scratch_shapes=[pltpu.VMEM((GVW, GKW), F32)] * 2,
        compiler_params=_cp("arbitrary"),
    )(p, p, p, la_f, p, p, p, la_b, st_f0, st_b0, *masks)


def _gla_bwd(p, la_f, la_b, sts_f, sts_b, do, after=None):
    s = p.shape[0]
    tb = GLA_BLOCK_CHUNKS * GLA_CHUNK
    nblk = s // tb
    up, down = (lambda i: i), (lambda i: nblk - 1 - i)
    masks = _gla_masks()
    follow = () if after is None else (after,)

    def back(rev, q_ref, k_ref, v_ref, la_ref, sts_ref, do_ref, dq_ref, dk_ref, dv_ref, dla_ref, dst0_ref, dst_ref,
             consts):
        def block(q, k, v, la, st):
            outs = [None] * GLA_BLOCK_CHUNKS
            chunks = range(GLA_BLOCK_CHUNKS)
            for ci in (reversed(chunks) if rev else chunks):
                outs[ci], st = _f_gla_chunk(q[ci], k[ci], v[ci], la[ci], st, *consts, rev)
            return tuple(outs), st

        split = lambda r: tuple(r[ci * GLA_CHUNK:(ci + 1) * GLA_CHUNK, :].astype(F32)
                                for ci in range(GLA_BLOCK_CHUNKS))
        _, vjp = jax.vjp(block, split(q_ref), split(k_ref), split(v_ref), split(la_ref), sts_ref[0])
        dq, dk, dv, dla, dst = vjp((split(do_ref), dst_ref[...]))
        for ci in range(GLA_BLOCK_CHUNKS):
            rows = slice(ci * GLA_CHUNK, (ci + 1) * GLA_CHUNK)
            dq_ref[rows, :], dk_ref[rows, :], dv_ref[rows, :], dla_ref[rows, :] = dq[ci], dk[ci], dv[ci], dla[ci]
        dst_ref[...] = dst
        dst0_ref[...] = dst

    def body(*refs):
        ins, (hk_ref, hv_ref, bd_ref) = refs[:12], refs[12:15]
        outs = refs[15 + len(follow):]

        @pl.when(pl.program_id(0) == 0)
        def _():
            outs[10][...] = jnp.zeros_like(outs[10])
            outs[11][...] = jnp.zeros_like(outs[11])

        consts = (hk_ref[...], hv_ref[...], bd_ref[...])
        back(False, *ins[:6], *outs[:5], outs[10], consts)
        back(True, *ins[6:], *outs[5:10], outs[11], consts)

    full = lambda a: pl.BlockSpec(a.shape, lambda i: (0,) * a.ndim)

    def ins(order):
        return _gla_specs(s, tb, order) + [pl.BlockSpec((1, GVW, GKW), lambda i: (order(i), 0, 0)),
                                           pl.BlockSpec((tb, GVW), lambda i: (order(i), 0))]

    def outs(order):
        blk = lambda w: pl.BlockSpec((tb, w), lambda i: (order(i), 0))
        return [blk(GKW), blk(GKW), blk(GVW), blk(GKW), pl.BlockSpec((GVW, GKW), lambda i: (0, 0))]

    shapes = [jax.ShapeDtypeStruct((s, GKW), F32), jax.ShapeDtypeStruct((s, GKW), F32),
              jax.ShapeDtypeStruct((s, GVW), F32), jax.ShapeDtypeStruct((s, GKW), F32),
              jax.ShapeDtypeStruct((GVW, GKW), F32)]
    both = pl.pallas_call(
        body, name="gla_bwd", grid=(nblk,),
        in_specs=ins(down) + ins(up) + [full(m) for m in masks] + [pl.BlockSpec(memory_space=pl.ANY)] * len(follow),
        out_specs=outs(down) + outs(up), out_shape=shapes * 2,
        scratch_shapes=[pltpu.VMEM((GVW, GKW), F32)] * 2,
        compiler_params=_cp("arbitrary"),
    )(p, p, p, la_f, sts_f, do, p, p, p, la_b, sts_b, do, *masks, *follow)
    return both[:5], both[5:]


def _f_ctx_state(k, v, la_f, la_b, bd_t):
    c = k.shape[0]
    after = _nn_hi(_tri(c, True, strict=True).astype(F32), la_f)
    before = _nn_hi(_tri(c, False, strict=True).astype(F32), la_b)
    return bd_t * _tn(v, k * jnp.exp(after)), bd_t * _tn(v, k * jnp.exp(before))


def _ctx_state(pc, la_f, la_b):
    c = pc.shape[0]
    bd_t = _gla_masks()[2]

    def body(k_ref, v_ref, lf_ref, lb_ref, bd_ref, sf_ref, sb_ref):
        sf_ref[...], sb_ref[...] = _f_ctx_state(k_ref[...], v_ref[...], lf_ref[...], lb_ref[...], bd_ref[...])

    full = lambda a: pl.BlockSpec(a.shape, lambda i: (0, 0))
    return pl.pallas_call(
        body, name="ctx_state_fwd", grid=(1,),
        in_specs=[pl.BlockSpec((c, GKW), lambda i: (0, C_GK // GKW)), pl.BlockSpec((c, GVW), lambda i: (0, C_GV // GVW)),
                  full(la_f), full(la_b), full(bd_t)],
        out_specs=[pl.BlockSpec((GVW, GKW), lambda i: (0, 0))] * 2,
        out_shape=[jax.ShapeDtypeStruct((GVW, GKW), F32)] * 2,
        compiler_params=_cp("arbitrary"),
    )(pc, pc, la_f, la_b, bd_t)


def _ctx_state_bwd(pc, la_f, la_b, dsf, dsb):
    c = pc.shape[0]
    bd_t = _gla_masks()[2]

    def body(k_ref, v_ref, lf_ref, lb_ref, bd_ref, dsf_ref, dsb_ref, dk_ref, dv_ref, dlf_ref, dlb_ref):
        _, vjp = jax.vjp(lambda k, v, lf, lb: _f_ctx_state(k, v, lf, lb, bd_ref[...]),
                         k_ref[...], v_ref[...], lf_ref[...], lb_ref[...])
        dk, dv, dlf, dlb = vjp((dsf_ref[...], dsb_ref[...]))
        dk_ref[...], dv_ref[...] = dk.astype(BF16), dv.astype(BF16)
        dlf_ref[...], dlb_ref[...] = dlf, dlb

    full = lambda a: pl.BlockSpec(a.shape, lambda i: (0, 0))
    return pl.pallas_call(
        body, name="ctx_state_bwd", grid=(1,),
        in_specs=[pl.BlockSpec((c, GKW), lambda i: (0, C_GK // GKW)), pl.BlockSpec((c, GVW), lambda i: (0, C_GV // GVW)),
                  full(la_f), full(la_b), full(bd_t), full(dsf), full(dsb)],
        out_specs=[pl.BlockSpec((c, GKW), lambda i: (0, 0)), pl.BlockSpec((c, GVW), lambda i: (0, 0)),
                   pl.BlockSpec((c, GKW), lambda i: (0, 0)), pl.BlockSpec((c, GKW), lambda i: (0, 0))],
        out_shape=[jax.ShapeDtypeStruct((c, GKW), BF16), jax.ShapeDtypeStruct((c, GVW), BF16),
                   jax.ShapeDtypeStruct((c, GKW), F32), jax.ShapeDtypeStruct((c, GKW), F32)],
        compiler_params=_cp("arbitrary"),
    )(pc, pc, la_f, la_b, bd_t, dsf, dsb)


_SRC_COLS = ((0, QW), (QW + 2 * KVW + 2 * GKW, GVW), (QW + 2 * KVW + 2 * GKW + GVW, GVW), (QW, KVW), (QW + KVW, KVW),
             (QW + 2 * KVW, GKW), (QW + 2 * KVW + GKW, GKW), (IN_COLS - 2 * GATE_RANK, 2 * GATE_RANK))
_DST_COLS = (C_Q, C_GV, C_GG, C_K, C_V, C_GQ, C_GK, C_Z)


def _pack_w_in(w_in):
    parts = [w_in[:, s:s + n] for s, n in _SRC_COLS]
    parts.append(jnp.zeros((w_in.shape[0], IN_PAD - C_Z - 2 * GATE_RANK), w_in.dtype))
    return jnp.concatenate(parts, axis=1)


def _unpack_w_in_grad(g):
    by_src = sorted(zip(_SRC_COLS, _DST_COLS))
    return jnp.concatenate([g[:, d:d + n] for (_, n), d in by_src], axis=1)


def _prep_weights(w_in, w_gate_fwd, w_gate_bwd):
    pad_rows = lambda w, at: jnp.zeros((LANES, GKW), F32).at[at:at + GATE_RANK].set(w)
    return {"w_in": _pack_w_in(w_in).astype(BF16), "wg_f": pad_rows(w_gate_fwd, 0),
            "wg_b": pad_rows(w_gate_bwd, GATE_RANK)}


def _local_step(x, ctx, target, ada, ada_c, w, late_weights, reduce_behind=None, reduce_w_in=None):
    s, d = x.shape
    sh1, sc1, gt1, sh2, sc2, gt2 = [ada[:, i * d:(i + 1) * d] for i in range(6)]
    sh1c, sc1c = ada_c[:, :d], ada_c[:, d:2 * d]
    cos, sin = _rope_tables(s)
    gt = jnp.tile(w["g_gla_norm"], (1, GLA_HEADS))

    h = _norm_mod("pre_mix", x, w["g_pre_mix"], sh1, sc1)
    hc = _norm_mod("pre_mix_ctx", ctx, w["g_pre_mix"], sh1c, sc1c)
    p = _mm("proj_in", h, w["w_in"], "nn")
    pc = _mm("proj_in_ctx", hc, w["w_in"], "nn")
    q_rot, k_rot, v_b = _rope_fwd("rope", p, cos, sin)
    pad = ((BLOCK, BLOCK), (0, 0))
    kp, vp = jnp.pad(k_rot, pad), jnp.pad(v_b, pad)
    kc, vc = pc[:, C_K:C_K + KVW].astype(BF16), pc[:, C_V:C_V + KVW].astype(BF16)
    attn = _attn_fwd(q_rot, kp, vp, kc, vc, w["attn_sink"])
    gate_w = (w["wg_f"], w["wg_b"], w["b_gate_fwd"], w["b_gate_bwd"])
    la_f, la_b = _gate_fwd("gate", p, *gate_w)
    la_fc, la_bc = _gate_fwd("gate_ctx", pc, *gate_w)
    st_f0, st_b0 = _ctx_state(pc, la_fc, la_bc)
    o_f, sts_f, o_b, sts_b = _gla_fwd(p, la_f, la_b, st_f0, st_b0)
    mix = _gla_out("gla_out", attn, o_f, o_b, p, gt)
    w_out, w_ffn_in_t, w_ffn_out = late_weights(attn)
    y = _mm("proj_out", mix, w_out, "nn")
    x1, h2 = _post_res_norm_mod("post_mix_pre_ffn", x, y, w["g_post_mix"], gt1, w["g_pre_ffn"], sh2, sc2)
    u, a = _ffn_in_swiglu("ffn_in", h2, w_ffn_in_t)
    f = _mm("ffn_out", a, w_ffn_out, "nn")
    g = {}
    dx2, df, loss, g["g_post_ffn"], dgt2 = _post_res_loss("post_ffn_loss", x1, f, w["g_post_ffn"], gt2, target)

    g["w_ffn_out"] = _mm("ffn_out_dw", a, df, "tn")
    du = _ffn_out_dx_swiglu_bwd("ffn_out_dx", df, w_ffn_out, u)
    dh2 = _mm("ffn_in_dx", du, w_ffn_in_t, "nn")
    g["w_ffn_in_t"] = _mm("ffn_in_dw", du, h2, "tn")
    dx1, dy, g["g_pre_ffn"], dsh2, dsc2, g["g_post_mix"], dgt1 = _norm_mod_post_res_bwd(
        "pre_ffn_post_mix_bwd", dh2, dx2, x1, y, w["g_pre_ffn"], sh2, sc2, w["g_post_mix"], gt1)
    dmix = _mm("proj_out_dx", dy, w_out, "nt", BF16)
    g["w_out"] = _mm("proj_out_dw", mix, dy, "tn")
    rb, sink, token = reduce_behind, w["attn_sink"], None
    if rb is not None:
        gt = _behind(gt, rb.start(g["w_ffn_in_t"], g["w_ffn_out"], g["w_out"]))
    d_o, dgg, dgt = _gla_out_bwd("gla_out_bwd", dmix, o_f, o_b, p, gt)
    g["g_gla_norm"] = jnp.sum(dgt.reshape(GLA_HEADS, GLA_DV), axis=0, keepdims=True)
    if rb is not None:
        token = rb.pair(dgg)
    gla_f, gla_b = _gla_bwd(p, la_f, la_b, sts_f, sts_b, d_o, token)
    (dla_f, dst_f0), (dla_b, dst_b0) = gla_f[3:], gla_b[3:]
    if rb is not None:
        sink = _behind(sink, rb.total(dla_b))
    dgkc, dgvc, dla_fc, dla_bc = _ctx_state_bwd(pc, la_fc, la_bc, dst_f0, dst_b0)
    dz, dwf, dwb, dbf, dbb = _gate_bwd("gate_bwd", p, dla_f, dla_b, *gate_w)
    dzc, dwfc, dwbc, dbfc, dbbc = _gate_bwd("gate_ctx_bwd", pc, dla_fc, dla_bc, *gate_w)
    g["w_gate_fwd"] = (dwf + dwfc)[:GATE_RANK]
    g["w_gate_bwd"] = (dwb + dwbc)[GATE_RANK:2 * GATE_RANK]
    g["b_gate_fwd"], g["b_gate_bwd"] = dbf + dbfc, dbb + dbbc
    dq_rot, dkp, dvp, dkc, dvc, g["attn_sink"] = _attn_bwd(dmix, q_rot, kp, vp, kc, vc, sink)
    if rb is not None:
        g["behind"] = rb.result(dq_rot)
    dp = _proj_grad("proj_grad", dq_rot, dkp[BLOCK:BLOCK + s], dvp[BLOCK:BLOCK + s], cos, sin, gla_f[:3], gla_b[:3],
                    dgg, dz)
    c_rows = ctx.shape[0]
    zeros = lambda n: jnp.zeros((c_rows, n), BF16)
    dpc = jnp.concatenate([zeros(QW), dgvc, zeros(GVW), dkc.astype(BF16), dvc.astype(BF16), zeros(GKW), dgkc, dzc],
                          axis=1)
    g["w_in"] = _mm("proj_in_dw", h, dp, "tn", init=_mm("proj_in_ctx_dw", hc, dpc, "tn"))
    token = None if reduce_w_in is None else reduce_w_in.start(g["w_in"])
    dh = _mm("proj_in_dx", dp, w["w_in"], "nt", after=token)
    dhc = _mm("proj_in_ctx_dx", dpc, w["w_in"], "nt")
    if reduce_w_in is not None:
        sh1 = _behind(sh1, reduce_w_in.pair(dh))
    dx, dg_a, dsh1, dsc1 = _norm_mod_bwd("pre_mix_bwd", dh, dx1, x, w["g_pre_mix"], sh1, sc1)
    if reduce_w_in is not None:
        dsh1 = _behind(dsh1, reduce_w_in.total(dx))
    _, dg_b, dsh1c, dsc1c = _norm_mod_bwd("pre_mix_ctx_bwd", dhc, jnp.zeros_like(dhc), ctx, w["g_pre_mix"], sh1c,
                                          sc1c)
    g["g_pre_mix"] = dg_a + dg_b
    d_ada = jnp.concatenate([dsh1, dsc1, dgt1, dsh2, dsc2, dgt2], axis=1)
    d_ada_c = jnp.concatenate([dsh1c, dsc1c, jnp.zeros((1, 4 * d), F32)], axis=1)
    return loss, dx, g, d_ada, d_ada_c


HBM = pl.BlockSpec(memory_space=pltpu.HBM)
N_DEV, N_CHIP = 8, 4


def _place():
    x, y, c = lax.axis_index("x"), lax.axis_index("y"), lax.axis_index("c")
    return x, y, c, [(1 - x, y), (x, 1 - y), (1 - x, 1 - y)]


def _row_tile(n, mult, cap):
    return max(t for t in range(mult, min(n, cap) + 1, mult) if n % t == 0)


def _ag_small(name, v, after=None):
    follow = () if after is None else (after,)

    def body(v_ref, *rest):
        out_ref, send_sems, recv_sems = rest[len(follow):]
        x, y, c, _ = _place()
        out_ref[4 * x + 2 * y + c] = v_ref[...]

        def peer(r):
            return ((1 - x) if r & 4 else x, (1 - y) if r & 2 else y, (1 - c) if r & 1 else c)

        def copy(r, block):
            px, py, pc = block
            return pltpu.make_async_remote_copy(
                src_ref=v_ref, dst_ref=out_ref.at[4 * px + 2 * py + pc], send_sem=send_sems.at[r - 1],
                recv_sem=recv_sems.at[r - 1], device_id=peer(r), device_id_type=MESH)

        sends = [copy(r, (x, y, c)) for r in range(1, N_DEV)]
        for cp in sends:
            cp.start()
        for r in range(1, N_DEV):
            copy(r, peer(r)).wait_recv()
        for cp in sends:
            cp.wait_send()

    return pl.pallas_call(
        body, name=name, out_shape=jax.ShapeDtypeStruct((N_DEV,) + v.shape, v.dtype),
        in_specs=[pl.BlockSpec(memory_space=pltpu.VMEM)] + [pl.BlockSpec(memory_space=pl.ANY)] * len(follow),
        out_specs=pl.BlockSpec(memory_space=pltpu.VMEM),
        scratch_shapes=[pltpu.SemaphoreType.DMA((N_DEV - 1,)), pltpu.SemaphoreType.DMA((N_DEV - 1,))],
    )(v, *follow)


def _halves(c, rows, mult):
    hr = rows // 2
    return pl.ds(pl.multiple_of(c * hr, mult), hr), pl.ds(pl.multiple_of((1 - c) * hr, mult), hr)


def _ag_shards(name, shard):
    rows = shard.shape[0]

    def body(w_ref, out_ref, send_sems, recv_sems, local_sem):
        x, y, c, chips = _place()
        mine_half, other_half = _halves(c, rows, 16)
        me = 2 * x + y
        mine = pltpu.make_async_copy(w_ref, out_ref.at[me], local_sem)
        mine.start()

        def copy(k, src, chip, half, to):
            return pltpu.make_async_remote_copy(
                src_ref=src, dst_ref=out_ref.at[chip, half], send_sem=send_sems.at[k], recv_sem=recv_sems.at[k],
                device_id=to, device_id_type=MESH)

        first = [copy(j, w_ref.at[mine_half], me, mine_half, (px, py, c)) for j, (px, py) in enumerate(chips)]
        for cp in first:
            cp.start()
        passed = []
        for j, (px, py) in enumerate(chips):
            pk = 2 * px + py
            copy(j, w_ref.at[mine_half], pk, mine_half, (px, py, c)).wait_recv()
            cp = copy(3 + j, out_ref.at[pk, mine_half], pk, mine_half, (x, y, 1 - c))
            cp.start()
            passed.append(cp)
        for j, (px, py) in enumerate(chips):
            copy(3 + j, w_ref.at[mine_half], 2 * px + py, other_half, (x, y, 1 - c)).wait_recv()
        for cp in first + passed:
            cp.wait_send()
        mine.wait()

    return pl.pallas_call(
        body, name=name, out_shape=jax.ShapeDtypeStruct((N_CHIP,) + shard.shape, shard.dtype),
        in_specs=[HBM], out_specs=HBM,
        scratch_shapes=[pltpu.SemaphoreType.DMA((6,)), pltpu.SemaphoreType.DMA((6,)), pltpu.SemaphoreType.DMA],
    )(shard)


def _swap_half(name, g):
    n_sh, rows, n = g.shape

    def body(g_ref, a_ref, send_sem, recv_sem):
        x, y, c, _ = _place()
        _, other_half = _halves(c, rows, 8)
        cp = pltpu.make_async_remote_copy(
            src_ref=g_ref.at[pl.ds(0, n_sh), other_half], dst_ref=a_ref, send_sem=send_sem, recv_sem=recv_sem,
            device_id=(x, y, 1 - c), device_id_type=MESH)
        cp.start()
        cp.wait()

    return pl.pallas_call(
        body, name=name, out_shape=jax.ShapeDtypeStruct((n_sh, rows // 2, n), g.dtype), in_specs=[HBM], out_specs=HBM,
        scratch_shapes=[pltpu.SemaphoreType.DMA, pltpu.SemaphoreType.DMA],
    )(g)


def _add_half(name, g, a, c_idx):
    n_sh, hr, n = a.shape
    tr = _row_tile(hr, 16, 1024)
    nb = hr // tr

    def body(c_ref, g_ref, a_ref, o_ref):
        o_ref[...] = (g_ref[...] + a_ref[...]).astype(o_ref.dtype)

    return pl.pallas_call(
        body, name=name, out_shape=jax.ShapeDtypeStruct(a.shape, BF16),
        grid_spec=pltpu.PrefetchScalarGridSpec(
            num_scalar_prefetch=1, grid=(n_sh, nb),
            in_specs=[pl.BlockSpec((1, tr, n), lambda s, i, c_ref: (s, c_ref[0] * nb + i, 0)),
                      pl.BlockSpec((1, tr, n), lambda s, i, c_ref: (s, i, 0))],
            out_specs=pl.BlockSpec((1, tr, n), lambda s, i, c_ref: (s, i, 0))),
        compiler_params=_cp("parallel", "parallel"),
    )(c_idx, g, a)


def _scatter_chips(name, h):
    def body(h_ref, b_ref, send_sems, recv_sems, local_sem):
        x, y, c, chips = _place()
        me = 2 * x + y
        mine = pltpu.make_async_copy(h_ref.at[me], b_ref.at[me], local_sem)
        mine.start()

        def copy(j, src_block, dst_block, to):
            return pltpu.make_async_remote_copy(
                src_ref=h_ref.at[src_block], dst_ref=b_ref.at[dst_block], send_sem=send_sems.at[j],
                recv_sem=recv_sems.at[j], device_id=to, device_id_type=MESH)

        sends = [copy(j, 2 * px + py, me, (px, py, c)) for j, (px, py) in enumerate(chips)]
        for cp in sends:
            cp.start()
        for j, (px, py) in enumerate(chips):
            copy(j, me, 2 * px + py, (px, py, c)).wait_recv()
        for cp in sends:
            cp.wait_send()
        mine.wait()

    return pl.pallas_call(
        body, name=name, out_shape=jax.ShapeDtypeStruct(h.shape, h.dtype), in_specs=[HBM], out_specs=HBM,
        scratch_shapes=[pltpu.SemaphoreType.DMA((3,)), pltpu.SemaphoreType.DMA((3,)), pltpu.SemaphoreType.DMA],
    )(h)


def _sum_chips(name, b):
    n_sh, hr, n = b.shape
    tr = _row_tile(hr, 16, 1024)

    def body(b0, b1, b2, b3, o_ref):
        o_ref[...] = ((b0[0].astype(F32) + b1[0].astype(F32)) + b2[0].astype(F32)) + b3[0].astype(F32)

    return pl.pallas_call(
        body, name=name, grid=(hr // tr,), out_shape=jax.ShapeDtypeStruct((hr, n), F32),
        in_specs=[pl.BlockSpec((1, tr, n), functools.partial(lambda i, k: (k, i, 0), k=k)) for k in range(n_sh)],
        out_specs=pl.BlockSpec((tr, n), lambda i: (i, 0)), compiler_params=_cp("parallel"),
    )(b, b, b, b)


def _share_half(name, f):
    hr, n = f.shape

    def body(f_ref, out_ref, send_sem, recv_sem, local_sem):
        x, y, c, _ = _place()
        mine_half, other_half = _halves(c, 2 * hr, 8)
        mine = pltpu.make_async_copy(f_ref, out_ref.at[mine_half], local_sem)
        mine.start()

        def copy(half):
            return pltpu.make_async_remote_copy(
                src_ref=f_ref, dst_ref=out_ref.at[half], send_sem=send_sem, recv_sem=recv_sem,
                device_id=(x, y, 1 - c), device_id_type=MESH)

        send = copy(mine_half)
        send.start()
        copy(other_half).wait_recv()
        send.wait_send()
        mine.wait()

    return pl.pallas_call(
        body, name=name, out_shape=jax.ShapeDtypeStruct((2 * hr, n), f.dtype), in_specs=[HBM], out_specs=HBM,
        scratch_shapes=[pltpu.SemaphoreType.DMA, pltpu.SemaphoreType.DMA, pltpu.SemaphoreType.DMA],
    )(f)


def _reduce_shards(name, g, c_idx):
    a = _swap_half(name + "_swap", g)
    h = _add_half(name + "_pair", g, a, c_idx)
    b = _scatter_chips(name + "_scatter", h)
    f = _sum_chips(name + "_sum", b)
    return _share_half(name + "_share", f)


SEM = pl.BlockSpec(memory_space=pltpu.SEMAPHORE)
ANY = pl.BlockSpec(memory_space=pl.ANY)
DATAFLOW = pltpu.SideEffectType.DATAFLOW_SIDE_EFFECTING


def _remote(src, dst, send_sems, recv_sems, k, to):
    return pltpu.make_async_remote_copy(src_ref=src, dst_ref=dst, send_sem=send_sems.at[k], recv_sem=recv_sems.at[k],
                                        device_id=to, device_id_type=MESH)


def _split_copy(name, src, land_shape, land_dtype, n, plan, after=None):
    after = jnp.zeros((8, LANES), F32) if after is None else after

    def start_body(src_ref, land_ref, after_ref, send_sems, recv_sems, src_thru, land_thru, token):
        for cp in plan(src_ref, land_ref, send_sems, recv_sems)[0]:
            cp.start()
        token[...] = jnp.zeros_like(token)

    sems = pltpu.SemaphoreType.DMA((n,))
    send_sems, recv_sems, src_thru, land_thru, token = pl.pallas_call(
        start_body, name=name + "_start",
        out_shape=(sems, sems, pltpu.HBM(src.shape, src.dtype), pltpu.HBM(land_shape, land_dtype),
                   jax.ShapeDtypeStruct((8, LANES), F32)),
        in_specs=(HBM, HBM, ANY), out_specs=(SEM, SEM, HBM, HBM, pl.BlockSpec(memory_space=pltpu.VMEM)),
        input_output_aliases={0: 2, 1: 3}, compiler_params=pltpu.CompilerParams(has_side_effects=DATAFLOW),
    )(pltpu.with_memory_space_constraint(src, pltpu.HBM),
      pltpu.with_memory_space_constraint(lax.empty(land_shape, land_dtype), pltpu.HBM), after)

    def wait(after):
        def wait_body(src_ref, land_ref, send_sems, recv_sems, after_ref, src_out, land_out):
            sent, received = plan(src_ref, land_ref, send_sems, recv_sems)
            for cp in sent:
                cp.wait_send()
            for cp in received:
                cp.wait_recv()

        return pl.pallas_call(
            wait_body, name=name + "_wait",
            out_shape=(pltpu.HBM(src.shape, src.dtype), pltpu.HBM(land_shape, land_dtype)),
            in_specs=(HBM, HBM, SEM, SEM, ANY), out_specs=(HBM, HBM), input_output_aliases={0: 0, 1: 1},
            compiler_params=pltpu.CompilerParams(has_side_effects=DATAFLOW),
        )(src_thru, land_thru, send_sems, recv_sems, after)

    return token, wait


def _behind(x, token):
    return x + token[0, 0]


def _plan_gather(src_ref, land_ref, send_sems, recv_sems):
    x, y, c, chips = _place()
    sent = [_remote(src_ref, land_ref.at[2 * x + y], send_sems, recv_sems, j, (px, py, c))
            for j, (px, py) in enumerate(chips)]
    received = [_remote(src_ref, land_ref.at[2 * px + py], send_sems, recv_sems, j, (px, py, c))
                for j, (px, py) in enumerate(chips)]
    return sent, received


def _plan_swap(src_ref, land_ref, send_sems, recv_sems):
    x, y, c, _ = _place()
    _, other_half = _halves(c, src_ref.shape[1], 8)
    cp = _remote(src_ref.at[pl.ds(0, src_ref.shape[0]), other_half], land_ref, send_sems, recv_sems, 0, (x, y, 1 - c))
    return [cp], [cp]


def _plan_scatter(src_ref, land_ref, send_sems, recv_sems):
    x, y, c, chips = _place()
    sent = [_remote(src_ref.at[2 * px + py], land_ref.at[2 * x + y], send_sems, recv_sems, j, (px, py, c))
            for j, (px, py) in enumerate(chips)]
    received = [_remote(src_ref.at[2 * px + py], land_ref.at[2 * px + py], send_sems, recv_sems, j, (px, py, c))
                for j, (px, py) in enumerate(chips)]
    return sent, received


def _plan_share(src_ref, land_ref, send_sems, recv_sems):
    x, y, c, _ = _place()
    mine_half, other_half = _halves(c, land_ref.shape[0], 8)
    return ([_remote(src_ref, land_ref.at[mine_half], send_sems, recv_sems, 0, (x, y, 1 - c))],
            [_remote(src_ref, land_ref.at[other_half], send_sems, recv_sems, 0, (x, y, 1 - c))])


def _pack_shard_rows(name, parts):
    rows = [t.shape[0] // N_CHIP for t in parts]
    n, total = parts[0].shape[1], sum(t.shape[0] // N_CHIP for t in parts)
    slab, at = None, 0
    for i, (t, r) in enumerate(zip(parts, rows)):
        tr = max(c for c in range(8, min(r, 512) + 1, 8) if r % c == 0 and at % c == 0)
        nb, ob = r // tr, at // tr

        def body(t_ref, *rest):
            rest[-1][0] = t_ref[...]

        slab = pl.pallas_call(
            body, name=f"{name}_{i}", grid=(N_CHIP, nb), out_shape=jax.ShapeDtypeStruct((N_CHIP, total, n), t.dtype),
            in_specs=[pl.BlockSpec((tr, n), functools.partial(lambda k, j, nb: (k * nb + j, 0), nb=nb))]
            + ([] if slab is None else [pl.BlockSpec(memory_space=pl.ANY)]),
            out_specs=pl.BlockSpec((1, tr, n), functools.partial(lambda k, j, ob: (k, ob + j, 0), ob=ob)),
            input_output_aliases={} if slab is None else {1: 0}, compiler_params=_cp("parallel", "parallel"),
        )(*((t,) if slab is None else (t, slab)))
        at += r
    return slab


class _GatherBehind:
    def __init__(self, name, shard, chip, after=None):
        self.chip = chip
        self.token, self.wait = _split_copy(name, shard, (N_CHIP,) + shard.shape, shard.dtype, 3, _plan_gather,
                                            after)

    def result(self, after):
        shard, land = self.wait(after)
        return lax.dynamic_update_slice(land, shard[None], (self.chip, 0, 0))


class _ReduceBehind:
    def __init__(self, name, chip, c, c_idx):
        self.name, self.chip, self.c, self.c_idx = name, chip, c, c_idx

    def start(self, *grads):
        return self.start_slab(_pack_shard_rows(self.name + "_pack", grads))

    def start_slab(self, g):
        n_sh, rows, n = g.shape
        token, self.wait = _split_copy(self.name + "_swap", g, (n_sh, rows // 2, n), g.dtype, 1, _plan_swap)
        return token

    def pair(self, after):
        g, a = self.wait(after)
        h = _add_half(self.name + "_pair", g, a, self.c_idx)
        token, self.wait = _split_copy(self.name + "_scatter", h, h.shape, h.dtype, 3, _plan_scatter)
        return token

    def total(self, after):
        h, b = self.wait(after)
        b = lax.dynamic_update_slice(b, lax.dynamic_slice_in_dim(h, self.chip, 1, axis=0), (self.chip, 0, 0))
        f = _sum_chips(self.name + "_sum", b)
        token, self.wait = _split_copy(self.name + "_share", f, (2 * f.shape[0], f.shape[1]), f.dtype, 1,
                                       _plan_share)
        return token

    def result(self, after):
        f, out = self.wait(after)
        return lax.dynamic_update_slice(out, f, (self.c * f.shape[0], 0))


class _ReduceColsBehind(_ReduceBehind):
    def start(self, g_padded):
        g = _unpack_w_in_grad(g_padded)
        n = g.shape[1] // N_CHIP
        return self.start_slab(jnp.stack([g[:, k * n:(k + 1) * n] for k in range(N_CHIP)]))


def _f_adamw(w, g, m, v):
    m = ADAM_B1 * m + (1.0 - ADAM_B1) * g
    v = ADAM_B2 * v + (1.0 - ADAM_B2) * (g * g)
    m_hat = m / (1.0 - ADAM_B1 ** ADAM_STEP)
    v_hat = v / (1.0 - ADAM_B2 ** ADAM_STEP)
    return -ADAM_LR * (m_hat / (jnp.sqrt(v_hat) + ADAM_EPS) + ADAM_WD * w), m, v


def _adamw(name, w, g, m, v):
    rows, n = w.shape
    return _rowwise(name, lambda w, g, m, v: (_f_adamw(w, g, m, v), ()), rows, [(t, n, 0) for t in (w, g, m, v)], [],
                    [(n, F32)] * 3, [], tm=_row_tile(rows, 8, 256))


def _pack_rows(parts):
    rows = []
    for t in parts:
        t = t.reshape(-1)
        rows.append(jnp.pad(t, (0, -t.shape[0] % LANES)).reshape(-1, LANES))
    out = jnp.concatenate(rows, axis=0)
    return jnp.pad(out, ((0, -out.shape[0] % 8), (0, 0)))


def _unpack_rows(packed, shapes):
    out, r = [], 0
    for shp in shapes:
        n = int(np.prod(shp))
        nr = -(-n // LANES)
        out.append(packed[r:r + nr].reshape(-1)[:n].reshape(shp))
        r += nr
    return out


def _sum_blocks(name, g):
    def body(g_ref, o_ref):
        acc = g_ref[0]
        for k in range(1, g.shape[0]):
            acc = acc + g_ref[k]
        o_ref[...] = acc

    return pl.pallas_call(body, name=name, out_shape=jax.ShapeDtypeStruct(g.shape[1:], F32))(g)


def _silu(t):
    return t * _sigmoid(t)


def _ada_fwd(cc, w_ada):
    n = w_ada.shape[1]
    tn = _row_tile(n, LANES, 512)

    def body(cc_ref, w_ref, o_ref):
        o_ref[...] = _nn(_silu(cc_ref[...]), w_ref[...])

    return pl.pallas_call(
        body, name="ada_fwd", grid=(n // tn,), out_shape=jax.ShapeDtypeStruct((cc.shape[0], n), F32),
        in_specs=[pl.BlockSpec(cc.shape, lambda j: (0, 0)), pl.BlockSpec((w_ada.shape[0], tn), lambda j: (0, j))],
        out_specs=pl.BlockSpec((cc.shape[0], tn), lambda j: (0, j)), compiler_params=_cp("parallel"),
    )(cc, w_ada)


def _ada_bwd(cc, dm, w_ada):
    d, n = w_ada.shape
    tn = _row_tile(n, LANES, 512)

    def body(cc_ref, dm_ref, w_ref, gw_ref, ds_ref):
        @pl.when(pl.program_id(0) == 0)
        def _():
            ds_ref[...] = jnp.zeros_like(ds_ref)

        gw_ref[...] = _raw_dot("tn", _silu(cc_ref[...]), dm_ref[...], True)
        ds_ref[...] += _raw_dot("nt", dm_ref[...], w_ref[...], False)

    return pl.pallas_call(
        body, name="ada_bwd", grid=(n // tn,),
        out_shape=[jax.ShapeDtypeStruct((d, n), F32), jax.ShapeDtypeStruct(cc.shape, F32)],
        in_specs=[pl.BlockSpec(cc.shape, lambda j: (0, 0)), pl.BlockSpec((cc.shape[0], tn), lambda j: (0, j)),
                  pl.BlockSpec((d, tn), lambda j: (0, j))],
        out_specs=[pl.BlockSpec((d, tn), lambda j: (0, j)), pl.BlockSpec(cc.shape, lambda j: (0, 0))],
        compiler_params=_cp("arbitrary"),
    )(cc, dm, w_ada)


def _c_ctx_grad(parts, c_ctx):
    def body(p_ref, c_ref, o_ref):
        ds = ((p_ref[0] + p_ref[1]) + p_ref[2]) + p_ref[3]
        _, vjp = jax.vjp(_silu, c_ref[...])
        o_ref[...] = vjp(ds)[0]

    return pl.pallas_call(body, name="c_ctx_grad", out_shape=jax.ShapeDtypeStruct(c_ctx.shape, F32))(parts, c_ctx)


def kernel(x, c, ctx, c_ctx, w_ada, b_ada, g_pre_mix, g_post_mix, g_pre_ffn, g_post_ffn, w_in, attn_sink, w_gate_fwd, b_gate_fwd, w_gate_bwd, b_gate_bwd, g_gla_norm, w_out, w_ffn_in, w_ffn_out, loss_target, m_c_ctx, m_w_ada, m_b_ada, m_g_pre_mix, m_g_post_mix, m_g_pre_ffn, m_g_post_ffn, m_w_in, m_attn_sink, m_w_gate_fwd, m_b_gate_fwd, m_w_gate_bwd, m_b_gate_bwd, m_g_gla_norm, m_w_out, m_w_ffn_in, m_w_ffn_out, v_c_ctx, v_w_ada, v_b_ada, v_g_pre_mix, v_g_post_mix, v_g_pre_ffn, v_g_post_ffn, v_w_in, v_attn_sink, v_w_gate_fwd, v_b_gate_fwd, v_w_gate_bwd, v_b_gate_bwd, v_g_gla_norm, v_w_out, v_w_ffn_in, v_w_ffn_out):
    xi, yi, ci = lax.axis_index("x"), lax.axis_index("y"), lax.axis_index("c")
    dev, chip = 4 * xi + 2 * yi + ci, 2 * xi + yi
    c_idx = jnp.reshape(ci, (1,)).astype(jnp.int32)
    d = x.shape[-1]
    n_ada, n_in, n_f = w_ada.shape[-1], w_in.shape[-1], w_ffn_in.shape[-1]
    r_out, r_f = w_out.shape[1], w_ffn_out.shape[1]
    n_gate = w_gate_fwd.shape[-1]
    by_chip = lambda t: t[0::2]

    w_in_g = _ag_shards("gather_w_in", w_in[0].astype(BF16))

    rc = -(-d // LANES)
    g1 = _ag_small("gather_cond", _pack_rows([c[0], w_gate_fwd[0], w_gate_bwd[0]]), w_in_g)
    c_all = g1[:, :rc].reshape(N_DEV, -1)[:, :d]
    gr = GATE_RANK * n_gate // LANES
    gate_full = lambda off: jnp.transpose(by_chip(g1)[:, off:off + gr].reshape(N_CHIP, GATE_RANK, n_gate),
                                          (1, 0, 2)).reshape(GATE_RANK, N_CHIP * n_gate)
    wgf, wgb = gate_full(rc), gate_full(rc + gr)
    cc = jnp.concatenate([c_all, c_ctx[None, :], jnp.zeros((7, d), F32)], axis=0)

    g2 = _ag_small("gather_ada", _ada_fwd(cc, w_ada[0]).reshape(-1, LANES))
    ada_all = jnp.transpose(by_chip(g2).reshape(N_CHIP, 16, n_ada), (1, 0, 2)).reshape(16, N_CHIP * n_ada) + b_ada
    late = _GatherBehind("gather_late", jnp.concatenate(
        [w_out[0], w_ffn_out[0], jnp.transpose(w_ffn_in[0])], axis=0).astype(BF16), chip, g2)

    def late_weights(after):
        t = late.result(after)
        r1, r2 = r_out, r_out + r_f
        return (t[:, :r1].reshape(N_CHIP * r_out, d), t[:, r2:].reshape(N_CHIP * n_f, d),
                t[:, r1:r2].reshape(N_CHIP * r_f, d))

    ada_all = _behind(ada_all, late.token)
    ada = lax.dynamic_slice(ada_all, (dev, 0), (1, N_CHIP * n_ada))
    ada_c = ada_all[N_DEV:N_DEV + 1]

    w = _prep_weights(jnp.concatenate([w_in_g[k] for k in range(N_CHIP)], axis=1), wgf, wgb)
    w.update(g_pre_mix=g_pre_mix, g_post_mix=g_post_mix, g_pre_ffn=g_pre_ffn, g_post_ffn=g_post_ffn,
             attn_sink=attn_sink, b_gate_fwd=b_gate_fwd, b_gate_bwd=b_gate_bwd, g_gla_norm=g_gla_norm)

    reduce_behind = _ReduceBehind("reduce_late", chip, ci, c_idx)
    reduce_w_in = _ReduceColsBehind("reduce_w_in", chip, ci, c_idx)
    loss_lanes, grad_x, g, d_ada, d_ada_c = _local_step(x[0], ctx[0], loss_target[0], ada, ada_c, w, late_weights,
                                                        reduce_behind, reduce_w_in)

    small = ("g_pre_mix", "g_post_mix", "g_pre_ffn", "g_post_ffn", "attn_sink", "b_gate_fwd", "b_gate_bwd",
             "g_gla_norm", "w_gate_fwd", "w_gate_bwd")
    shapes = [(1, 6 * d)] * 2 + [g[n].shape for n in small] + [(1, LANES)]
    g3 = _ag_small("gather_small_grads", _pack_rows([d_ada, d_ada_c] + [g[n] for n in small] + [loss_lanes]))
    tot = dict(zip(("d_ada", "d_ada_c") + small + ("loss",),
                   _unpack_rows(_sum_blocks("sum_small_grads", g3), shapes)))
    r_ada = 6 * d // LANES
    dm = jnp.concatenate([g3[:, :r_ada].reshape(N_DEV, 6 * d), tot["d_ada_c"], jnp.zeros((7, 6 * d), F32)], axis=0)
    grads = {n: tot[n] for n in small[:8]}
    grads["b_ada"] = _sum_blocks("sum_b_ada", dm.reshape(16, r_ada, LANES)).reshape(1, 6 * d)
    grads["w_gate_fwd"] = lax.dynamic_slice(tot["w_gate_fwd"], (0, chip * n_gate), (GATE_RANK, n_gate))[None]
    grads["w_gate_bwd"] = lax.dynamic_slice(tot["w_gate_bwd"], (0, chip * n_gate), (GATE_RANK, n_gate))[None]
    gw_ada, dsc = _ada_bwd(cc, lax.dynamic_slice(dm, (0, chip * n_ada), (16, n_ada)), w_ada[0])
    grads["w_ada"] = gw_ada[None]
    g4 = _ag_small("gather_c_ctx", _pack_rows([dsc[N_DEV]]))
    grads["c_ctx"] = _c_ctx_grad(by_chip(g4), _pack_rows([c_ctx])).reshape(-1)[:d]

    grads["w_in"] = reduce_w_in.result(g4)[None]
    behind = g["behind"]
    grads["w_ffn_in"] = jnp.transpose(behind[:n_f])[None]
    grads["w_ffn_out"], grads["w_out"] = behind[None, n_f:n_f + r_f], behind[None, n_f + r_f:]

    names = ("c_ctx", "w_ada", "b_ada", "g_pre_mix", "g_post_mix", "g_pre_ffn", "g_post_ffn", "w_in", "attn_sink",
             "w_gate_fwd", "b_gate_fwd", "w_gate_bwd", "b_gate_bwd", "g_gla_norm", "w_out", "w_ffn_in", "w_ffn_out")
    weights = dict(zip(names, (c_ctx, w_ada, b_ada, g_pre_mix, g_post_mix, g_pre_ffn, g_post_ffn, w_in, attn_sink,
                               w_gate_fwd, b_gate_fwd, w_gate_bwd, b_gate_bwd, g_gla_norm, w_out, w_ffn_in,
                               w_ffn_out)))
    m_in = dict(zip(names, (m_c_ctx, m_w_ada, m_b_ada, m_g_pre_mix, m_g_post_mix, m_g_pre_ffn, m_g_post_ffn, m_w_in,
                            m_attn_sink, m_w_gate_fwd, m_b_gate_fwd, m_w_gate_bwd, m_b_gate_bwd, m_g_gla_norm,
                            m_w_out, m_w_ffn_in, m_w_ffn_out)))
    v_in = dict(zip(names, (v_c_ctx, v_w_ada, v_b_ada, v_g_pre_mix, v_g_post_mix, v_g_pre_ffn, v_g_post_ffn, v_w_in,
                            v_attn_sink, v_w_gate_fwd, v_b_gate_fwd, v_w_gate_bwd, v_b_gate_bwd, v_g_gla_norm,
                            v_w_out, v_w_ffn_in, v_w_ffn_out)))
    large = ("w_ada", "w_in", "w_out", "w_ffn_in", "w_ffn_out")
    tiny = tuple(n for n in names if n not in large)
    delta, new_m, new_v = {}, {}, {}
    for n in large:
        dl, nm, nv = _adamw("adamw_" + n, weights[n][0], grads[n][0], m_in[n][0], v_in[n][0])
        delta[n], new_m[n], new_v[n] = dl[None], nm[None], nv[None]
    tiny_shapes = [weights[n].shape for n in tiny]
    packed = [_pack_rows([t[n] for n in tiny]) for t in (weights, grads, m_in, v_in)]
    for out, res in zip((delta, new_m, new_v), _adamw("adamw_small", *packed)):
        out.update(zip(tiny, _unpack_rows(res, tiny_shapes)))
    for n in tiny:
        grads[n] = grads[n].reshape(weights[n].shape)

    return (tot["loss"][0, 0], grad_x[None], *[grads[n] for n in names], *[delta[n] for n in names], *[new_m[n] for n in names],
            *[new_v[n] for n in names])
```

```python
import functools

import jax
import jax.numpy as jnp
import numpy as np
from jax import lax
from jax.experimental import pallas as pl
from jax.experimental.pallas import tpu as pltpu

F32 = jnp.float32
BF16 = jnp.bfloat16
MESH = pl.DeviceIdType.MESH

HEAD_DIM = 64
ATT_HEADS = 8
ATT_KV_HEADS = 2
ATT_GROUP = ATT_HEADS // ATT_KV_HEADS
WINDOW = 128
BLOCK = 128
GRID_W = 64
ROPE_BASE = 10000.0
GLA_HEADS = 8
GLA_DK = 32
GLA_DV = 64
GLA_CHUNK = 64
GATE_RANK = 16
GATE_TAU = 16.0
NEG_INF = -1e30
QW = ATT_HEADS * HEAD_DIM
KVW = ATT_KV_HEADS * HEAD_DIM
GKW = GLA_HEADS * GLA_DK
GVW = GLA_HEADS * GLA_DV
IN_COLS = QW + 2 * KVW + 2 * GKW + 2 * GVW + 2 * GATE_RANK
LANES = 128
IN_PAD = IN_COLS + LANES - 2 * GATE_RANK
C_Q, C_GV, C_GG = 0, QW, QW + GVW
C_K = C_GG + GVW
C_V = C_K + KVW
C_GQ = C_V + KVW
C_GK = C_GQ + GKW
C_Z = C_GK + GKW
MIX = QW + GVW

ADAM_LR, ADAM_B1, ADAM_B2, ADAM_EPS, ADAM_WD, ADAM_STEP = 0.001, 0.9, 0.999, 1e-08, 0.01, 10

VMEM_LIMIT = 56 * 1024 * 1024


def _cp(*sem):
    return pltpu.CompilerParams(dimension_semantics=sem, vmem_limit_bytes=VMEM_LIMIT)


def _pick(n, cands):
    for t in cands:
        if n % t == 0:
            return t
    return n


_DIMS = {"nn": (((1,), (0,)), ((), ())), "nt": (((1,), (1,)), ((), ())), "tn": (((0,), (0,)), ((), ()))}


def _raw_dot(mode, a, b, hi):
    dot = lambda u, v: lax.dot_general(u, v, _DIMS[mode], preferred_element_type=F32)
    if hi:
        a, b = a.astype(F32), b.astype(F32)
        a_hi, b_hi = a.astype(BF16), b.astype(BF16)
        a_lo, b_lo = (a - a_hi.astype(F32)).astype(BF16), (b - b_hi.astype(F32)).astype(BF16)
        return dot(a_hi, b_hi) + (dot(a_lo, b_hi) + dot(a_hi, b_lo))
    return dot(a.astype(BF16), b.astype(BF16))


def _make_dot(mode, hi):
    @jax.custom_vjp
    def dot(a, b):
        return _raw_dot(mode, a, b, hi)

    def fwd(a, b):
        return _raw_dot(mode, a, b, hi), (a, b)

    def bwd(res, dc):
        a, b = res
        if mode == "nn":
            return _raw_dot("nt", dc, b, hi), _raw_dot("tn", a, dc, hi)
        if mode == "nt":
            return _raw_dot("nn", dc, b, hi), _raw_dot("tn", dc, a, hi)
        return _raw_dot("nt", b, dc, hi), _raw_dot("nn", a, dc, hi)

    dot.defvjp(fwd, bwd)
    return dot


_nn, _nt, _tn = _make_dot("nn", False), _make_dot("nt", False), _make_dot("tn", False)
_nn_hi = _make_dot("nn", True)


MM_VMEM_BUDGET = 44 * 1024 * 1024


def _halvings(n):
    out = [n]
    while out[-1] % (2 * LANES) == 0:
        out.append(out[-1] // 2)
    return out


def _mm_tiles(mode, m, n, k, a_bytes, b_bytes, o_bytes, init_bytes=0):
    tms = [t for t in dict.fromkeys((m, m // 2, m // 4, 2048, 1024, 512, 256, 128))
           if m % t == 0 and t % (LANES if mode == "tn" else 16) == 0 and t <= 4096] or [m]
    if mode == "tn":
        fits = [(k // tk + 0.5 * (m // tm), tm, tk)
                for tk in (4096, 2048, 1024, 512, 256, 128) if k % tk == 0 for tm in tms
                if 2 * (tk * tm * a_bytes + tk * n * b_bytes + tm * n * (o_bytes + init_bytes)) <= MM_VMEM_BUDGET]
        if fits:
            _, tm, tk = min(fits)
            return tm, n, tk
    tks = ([t for t in (512, 256, 128) if k % t == 0] or [k]) if mode == "tn" else _halvings(k)
    for tn in _halvings(n):
        for tk in tks:
            for tm in tms:
                acc = tm * tn * 4 if (k // tk > 1 and o_bytes != 4) else 0
                tiles = tm * tk * a_bytes + tk * tn * b_bytes + tm * tn * (o_bytes + init_bytes)
                if 2 * tiles + acc <= MM_VMEM_BUDGET:
                    return tm, tn, tk
    return tms[-1], _halvings(n)[-1], tks[-1]


def _mm(name, a, b, mode, out_dtype=F32, init=None, after=None):
    follow = () if after is None else (after,)
    if mode == "nn":
        (m, k), n = a.shape, b.shape[1]
    elif mode == "nt":
        (m, k), n = a.shape, b.shape[0]
    else:
        (k, m), n = a.shape, b.shape[1]
    tm, tn, tk = _mm_tiles(mode, m, n, k, a.dtype.itemsize, b.dtype.itemsize, jnp.dtype(out_dtype).itemsize,
                           0 if init is None else 4)
    nk = k // tk
    use_acc = nk > 1 and out_dtype != F32

    inits = () if init is None else (init,)

    def body(a_ref, b_ref, *rest):
        rest = rest[:len(inits)] + rest[len(inits) + len(follow):]
        o_ref, acc = rest[len(inits)], rest[len(inits) + 1:]
        part = _raw_dot(mode, a_ref[...], b_ref[...], False)
        first = lambda: part + rest[0][...] if inits else part
        if nk == 1:
            o_ref[...] = first().astype(o_ref.dtype)
            return
        acc_ref = acc[0] if use_acc else o_ref
        kk = pl.program_id(2)

        @pl.when(kk == 0)
        def _():
            acc_ref[...] = first()

        @pl.when(kk > 0)
        def _():
            acc_ref[...] += part

        if use_acc:
            @pl.when(kk == nk - 1)
            def _():
                o_ref[...] = acc_ref[...].astype(o_ref.dtype)

    if mode == "nn":
        a_spec = pl.BlockSpec((tm, tk), lambda i, j, kk: (i, kk))
        b_spec = pl.BlockSpec((tk, tn), lambda i, j, kk: (kk, j))
    elif mode == "nt":
        a_spec = pl.BlockSpec((tm, tk), lambda i, j, kk: (i, kk))
        b_spec = pl.BlockSpec((tn, tk), lambda i, j, kk: (j, kk))
    else:
        a_spec = pl.BlockSpec((tk, tm), lambda i, j, kk: (kk, i))
        b_spec = pl.BlockSpec((tk, tn), lambda i, j, kk: (kk, j))
    return pl.pallas_call(
        body, name=name, grid=(m // tm, n // tn, nk),
        in_specs=[a_spec, b_spec] + [pl.BlockSpec((tm, tn), lambda i, j, kk: (i, j))] * len(inits)
        + [pl.BlockSpec(memory_space=pl.ANY)] * len(follow),
        out_specs=pl.BlockSpec((tm, tn), lambda i, j, kk: (i, j)),
        out_shape=jax.ShapeDtypeStruct((m, n), out_dtype),
        scratch_shapes=[pltpu.VMEM((tm, tn), F32)] if use_acc else [],
        compiler_params=_cp("parallel", "parallel", "arbitrary"),
    )(a, b, *inits, *follow)


def _rowwise(name, fn, rows, row_ins, full_ins, row_outs, acc_outs, tm=None):
    tm = tm or _pick(rows, (512, 256, 128))
    n_r, n_f, n_o, n_a = len(row_ins), len(full_ins), len(row_outs), len(acc_outs)

    def body(*refs):
        ins, outs = refs[:n_r + n_f], refs[n_r + n_f:]
        vals = [r[...].astype(F32) for r in ins]
        ro, ao = fn(*vals)
        for r, val in zip(outs[:n_o], ro):
            r[...] = val.astype(r.dtype)
        if n_a:
            @pl.when(pl.program_id(0) == 0)
            def _():
                for r in outs[n_o:]:
                    r[...] = jnp.zeros_like(r)

            for r, val in zip(outs[n_o:], ao):
                r[...] += val

    in_specs = [pl.BlockSpec((tm, w), functools.partial(lambda i, cb: (i, cb), cb=cb)) for _, w, cb in row_ins]
    in_specs += [pl.BlockSpec(a.shape, lambda i: (0, 0)) for a in full_ins]
    out_specs = [pl.BlockSpec((tm, w), lambda i: (i, 0)) for w, _ in row_outs]
    out_specs += [pl.BlockSpec(s, lambda i: (0, 0)) for s in acc_outs]
    out_shape = [jax.ShapeDtypeStruct((rows, w), dt) for w, dt in row_outs]
    out_shape += [jax.ShapeDtypeStruct(s, F32) for s in acc_outs]
    return pl.pallas_call(
        body, name=name, grid=(rows // tm,), in_specs=in_specs, out_specs=out_specs, out_shape=out_shape,
        compiler_params=_cp("arbitrary" if n_a else "parallel"),
    )(*[a for a, _, _ in row_ins], *full_ins)


def _rn(x):
    return x * lax.rsqrt(jnp.mean(x * x, axis=-1, keepdims=True) + 1e-6)


def _sigmoid(t):
    return 1.0 / (1.0 + jnp.exp(-t))


def _f_norm_mod(x, g, sh, sc):
    return _rn(x) * g * (1.0 + sc) + sh


def _f_post_res(xr, y, g, gate):
    return xr + gate * (_rn(y) * g)


def _f_swiglu(g, u):
    return g * _sigmoid(g) * u


def _logsig(u):
    return jnp.minimum(u, 0.0) - jnp.log(1.0 + jnp.exp(-jnp.abs(u)))


def _f_gate(z, wf, wb, bf, bb):
    return _logsig(_nn(z, wf) + bf) / GATE_TAU, _logsig(_nn(z, wb) + bb) / GATE_TAU


def _f_gla_out(of, ob, gg, gt, bd):
    o = of + ob
    ms = _nn_hi(o * o, bd)
    return o * lax.rsqrt(ms + 1e-6) * gt * (gg * _sigmoid(gg))


def _norm_mod(name, x, g, sh, sc):
    rows, d = x.shape
    return _rowwise(name, lambda x, g, sh, sc: ((_f_norm_mod(x, g, sh, sc),), ()), rows,
                    [(x, d, 0)], [g, sh, sc], [(d, BF16)], [])[0]


def _norm_mod_bwd(name, dh, dres, x, g, sh, sc):
    rows, d = x.shape

    def fn(dh, dres, x, g, sh, sc):
        _, vjp = jax.vjp(_f_norm_mod, x, g, sh, sc)
        dx, dg, dsh, dsc = vjp(dh)
        return (dx + dres,), (dg, dsh, dsc)

    return _rowwise(name, fn, rows, [(dh, d, 0), (dres, d, 0), (x, d, 0)], [g, sh, sc], [(d, F32)],
                    [(1, d)] * 3)


def _post_res(name, xr, y, g, gate):
    rows, d = xr.shape
    return _rowwise(name, lambda xr, y, g, gate: ((_f_post_res(xr, y, g, gate),), ()), rows,
                    [(xr, d, 0), (y, d, 0)], [g, gate], [(d, F32)], [])[0]


def _post_res_bwd(name, dxo, y, g, gate):
    rows, d = y.shape

    def fn(dxo, y, g, gate):
        _, vjp = jax.vjp(lambda y, g, gate: _f_post_res(jnp.zeros_like(y), y, g, gate), y, g, gate)
        dy, dg, dgate = vjp(dxo)
        return (dy,), (dg, dgate)

    return _rowwise(name, fn, rows, [(dxo, d, 0), (y, d, 0)], [g, gate], [(d, BF16)], [(1, d)] * 2)


def _post_res_norm_mod(name, xr, y, g_post, gate, g_pre, sh, sc):
    rows, d = xr.shape

    def fn(xr, y, g_post, gate, g_pre, sh, sc):
        x1 = _f_post_res(xr, y, g_post, gate)
        return (x1, _f_norm_mod(x1, g_pre, sh, sc)), ()

    return _rowwise(name, fn, rows, [(xr, d, 0), (y, d, 0)], [g_post, gate, g_pre, sh, sc], [(d, F32), (d, BF16)], [])


def _norm_mod_post_res_bwd(name, dh, dres, x1, y, g_pre, sh, sc, g_post, gate):
    rows, d = x1.shape

    def fn(dh, dres, x1, y, g_pre, sh, sc, g_post, gate):
        _, vjp_norm = jax.vjp(_f_norm_mod, x1, g_pre, sh, sc)
        dx1, dg_pre, dsh, dsc = vjp_norm(dh)
        dx1 = dx1 + dres
        _, vjp_res = jax.vjp(lambda y, g, gate: _f_post_res(jnp.zeros_like(y), y, g, gate), y, g_post, gate)
        dy, dg_post, dgate = vjp_res(dx1)
        return (dx1, dy), (dg_pre, dsh, dsc, dg_post, dgate)

    return _rowwise(name, fn, rows, [(dh, d, 0), (dres, d, 0), (x1, d, 0), (y, d, 0)], [g_pre, sh, sc, g_post, gate],
                    [(d, F32), (d, BF16)], [(1, d)] * 5, tm=_pick(rows, (256, 128)))


def _post_res_loss(name, xr, y, g, gate, target):
    rows, d = xr.shape

    def fn(xr, y, target, g, gate):
        x2, vjp = jax.vjp(lambda y, g, gate: _f_post_res(xr, y, g, gate), y, g, gate)
        diff = x2 - target
        part = 0.5 * jnp.sum(jnp.mean(diff * diff, axis=-1, keepdims=True), axis=0, keepdims=True)
        dx2 = diff * (1.0 / d)
        dy, dg, dgate = vjp(dx2)
        return (dx2, dy), (jnp.broadcast_to(part, (1, LANES)), dg, dgate)

    return _rowwise(name, fn, rows, [(xr, d, 0), (y, d, 0), (target, d, 0)], [g, gate], [(d, F32), (d, BF16)],
                    [(1, LANES), (1, d), (1, d)])


def _mm_rows(name, a, b, mode, fn, extras, outs):
    m, k = a.shape
    tm = _pick(m, (256, 128))

    def body(a_ref, b_ref, *rest):
        tiles = fn(_raw_dot(mode, a_ref[...], b_ref[...], False), *[e[...] for e in rest[:len(extras)]])
        for r, val in zip(rest[len(extras):], tiles):
            r[...] = val.astype(r.dtype)

    row = lambda w: pl.BlockSpec((tm, w), lambda i: (i, 0))
    return pl.pallas_call(
        body, name=name, grid=(m // tm,),
        in_specs=[row(k), pl.BlockSpec(b.shape, lambda i: (0, 0))] + [row(e.shape[1]) for e in extras],
        out_specs=[row(w) for w, _ in outs], out_shape=[jax.ShapeDtypeStruct((m, w), dt) for w, dt in outs],
        compiler_params=_cp("parallel"),
    )(a, b, *extras)


def _ffn_in_swiglu(name, h, w_t):
    f = w_t.shape[0] // 2
    fn = lambda u: (u, _f_swiglu(u[:, :f], u[:, f:]))
    return _mm_rows(name, h, w_t, "nt", fn, [], [(2 * f, BF16), (f, BF16)])


def _ffn_out_dx_swiglu_bwd(name, df, w_out, u):
    f = w_out.shape[0]

    def fn(da, u):
        u = u.astype(F32)
        _, vjp = jax.vjp(_f_swiglu, u[:, :f], u[:, f:])
        return (jnp.concatenate(vjp(da), axis=1),)

    return _mm_rows(name, df, w_out, "nt", fn, [u], [(2 * f, BF16)])[0]


def _gate_fwd(name, p, wf, wb, bf, bb):
    rows = p.shape[0]
    return _rowwise(name, lambda z, wf, wb, bf, bb: (_f_gate(z, wf, wb, bf, bb), ()), rows,
                    [(p, LANES, C_Z // LANES)], [wf, wb, bf, bb], [(GKW, F32)] * 2, [])


def _gate_bwd(name, p, dla_f, dla_b, wf, wb, bf, bb):
    rows = p.shape[0]

    def fn(z, dlf, dlb, wf, wb, bf, bb):
        _, vjp = jax.vjp(_f_gate, z, wf, wb, bf, bb)
        dz, dwf, dwb, dbf, dbb = vjp((dlf, dlb))
        return (dz,), (dwf, dwb, dbf, dbb)

    return _rowwise(name, fn, rows, [(p, LANES, C_Z // LANES), (dla_f, GKW, 0), (dla_b, GKW, 0)],
                    [wf, wb, bf, bb], [(LANES, BF16)], [(LANES, GKW), (LANES, GKW), (1, GKW), (1, GKW)])


def _head_mean_matrix():
    h = np.arange(GVW) // GLA_DV
    return jnp.asarray((h[:, None] == h[None, :]).astype(np.float32) / GLA_DV)


def _gla_out(name, attn, of, ob, p, gt):
    rows = of.shape[0]
    bd = _head_mean_matrix()
    fn = lambda attn, of, ob, gg, gt, bd: ((jnp.concatenate([attn, _f_gla_out(of, ob, gg, gt, bd)], axis=1),), ())
    return _rowwise(name, fn, rows, [(attn, QW, 0), (of, GVW, 0), (ob, GVW, 0), (p, GVW, C_GG // GVW)], [gt, bd],
                    [(MIX, BF16)], [])[0]


def _gla_out_bwd(name, dmix, of, ob, p, gt):
    rows = of.shape[0]
    bd = _head_mean_matrix()

    def fn(dm, of, ob, gg, gt, bd):
        _, vjp = jax.vjp(lambda of, gg, gt: _f_gla_out(of, ob, gg, gt, bd), of, gg, gt)
        do, dgg, dgt = vjp(dm)
        return (do, dgg), (dgt,)

    return _rowwise(name, fn, rows, [(dmix, GVW, 1), (of, GVW, 0), (ob, GVW, 0), (p, GVW, C_GG // GVW)], [gt, bd],
                    [(GVW, F32), (GVW, BF16)], [(1, GVW)])


def _rope_tables(n_tokens):
    t = jnp.arange(n_tokens)
    row = (t // GRID_W).astype(F32)
    col = (t % GRID_W).astype(F32)
    half = HEAD_DIM // 2
    inv_freq = ROPE_BASE ** (-jnp.arange(0, half, 2, dtype=F32) / half)
    ang_r = row[:, None] * inv_freq[None, :]
    ang_c = col[:, None] * inv_freq[None, :]
    ang = jnp.concatenate([ang_r, ang_r, ang_c, ang_c], axis=-1)
    sign = jnp.concatenate([-jnp.ones((16,), F32), jnp.ones((16,), F32)] * 2)
    cos, sin = jnp.cos(ang), jnp.sin(ang) * sign[None, :]
    return jnp.tile(cos, (1, 2)), jnp.tile(sin, (1, 2))


def _rot_pairs(x):
    w = x.shape[-1]
    lane = lax.broadcasted_iota(jnp.int32, x.shape, x.ndim - 1)
    return jnp.where((lane % 32) < 16, pltpu.roll(x, w - 16, x.ndim - 1), pltpu.roll(x, 16, x.ndim - 1))


def _rope_apply(x, cos, sin_signed, inverse):
    reps = x.shape[-1] // LANES
    cos = jnp.concatenate([cos] * reps, axis=-1) if reps > 1 else cos
    sin = jnp.concatenate([sin_signed] * reps, axis=-1) if reps > 1 else sin_signed
    if inverse:
        return x * cos + _rot_pairs(x * sin)
    return x * cos + _rot_pairs(x) * sin


def _rope_fwd(name, p, cos, sin):
    rows = p.shape[0]

    def fn(q, k, v, cos, sin):
        return (_rope_apply(q, cos, sin, False), _rope_apply(k, cos, sin, False), v), ()

    return _rowwise(name, fn, rows, [(p, QW, 0), (p, KVW, C_K // KVW), (p, KVW, C_V // KVW), (cos, LANES, 0),
                                     (sin, LANES, 0)], [], [(QW, BF16), (KVW, BF16), (KVW, BF16)], [])


def _proj_grad(name, dq_rot, dk_rot, dv, cos, sin, gla_f, gla_b, dgg, dz):
    rows = dq_rot.shape[0]

    def fn(dq, dk, dv, cos, sin, gqf, gkf, gvf, gqb, gkb, gvb, dgg, dz):
        parts = [_rope_apply(dq, cos, sin, True), gvf + gvb, dgg, _rope_apply(dk, cos, sin, True), dv, gqf + gqb,
                 gkf + gkb, dz]
        return (jnp.concatenate(parts, axis=1),), ()

    ins = [(dq_rot, QW), (dk_rot, KVW), (dv, KVW), (cos, LANES), (sin, LANES)]
    ins += [(t, t.shape[1]) for t in (*gla_f, *gla_b)] + [(dgg, GVW), (dz, LANES)]
    return _rowwise(name, fn, rows, [(t, w, 0) for t, w in ins], [], [(IN_PAD, BF16)], [],
                    tm=_pick(rows, (256, 128)))[0]


GROUP_ROWS = ATT_GROUP * BLOCK


def _f_attn(qs, kws, vws, kcs, vcs, sink, n, n_tokens):
    row = lax.broadcasted_iota(jnp.int32, (GROUP_ROWS, 1), 0)
    group = sum((row >= g * BLOCK).astype(jnp.int32) for g in range(1, ATT_GROUP))
    i = lax.broadcasted_iota(jnp.int32, (GROUP_ROWS, 3 * BLOCK), 0) - BLOCK * group
    j = lax.broadcasted_iota(jnp.int32, (GROUP_ROWS, 3 * BLOCK), 1)
    kpos = (n - 1) * BLOCK + j
    mask = (jnp.abs(j - BLOCK - i) <= WINDOW) & (kpos >= 0) & (kpos < n_tokens)
    head_id = lax.broadcasted_iota(jnp.int32, (1, ATT_HEADS), 1)
    scale = HEAD_DIM ** -0.5
    outs = []
    for h in range(ATT_KV_HEADS):
        sk = jnp.zeros((GROUP_ROWS, 1), F32)
        for g in range(ATT_GROUP):
            one = jnp.sum(jnp.where(head_id == h * ATT_GROUP + g, sink, 0.0), axis=-1, keepdims=True)
            sk = jnp.where(group == g, one, sk)
        q = qs[h] * scale
        s_w = jnp.where(mask, _nt(q, kws[h]), NEG_INF)
        s_c = _nt(q, kcs[h])
        m = lax.stop_gradient(jnp.maximum(jnp.maximum(jnp.max(s_w, axis=-1, keepdims=True),
                                                      jnp.max(s_c, axis=-1, keepdims=True)), sk))
        pw, pc = jnp.exp(s_w - m), jnp.exp(s_c - m)
        den = jnp.sum(pw, axis=-1, keepdims=True) + jnp.sum(pc, axis=-1, keepdims=True) + jnp.exp(sk - m)
        outs.append((_nn(pw, vws[h]) + _nn(pc, vcs[h])) / den)
    return tuple(outs)


def _group_rows(ref, h):
    hs = lambda hq: slice(hq * HEAD_DIM, (hq + 1) * HEAD_DIM)
    return jnp.concatenate([ref[:, hs(h * ATT_GROUP + g)].astype(F32) for g in range(ATT_GROUP)], axis=0)


def _ungroup_rows(ref, h, val):
    for g in range(ATT_GROUP):
        hq = h * ATT_GROUP + g
        ref[:, hq * HEAD_DIM:(hq + 1) * HEAD_DIM] = val[g * BLOCK:(g + 1) * BLOCK].astype(ref.dtype)


def _attn_loads(n, q_ref, kp_ref, vp_ref, kc_ref, vc_ref):
    r0 = pl.multiple_of(n * BLOCK, BLOCK)
    hs = lambda h: slice(h * HEAD_DIM, (h + 1) * HEAD_DIM)
    qs = [_group_rows(q_ref, h) for h in range(ATT_KV_HEADS)]
    kws = [kp_ref[pl.ds(r0, 3 * BLOCK), hs(h)].astype(F32) for h in range(ATT_KV_HEADS)]
    vws = [vp_ref[pl.ds(r0, 3 * BLOCK), hs(h)].astype(F32) for h in range(ATT_KV_HEADS)]
    kcs = [kc_ref[:, hs(h)].astype(F32) for h in range(ATT_KV_HEADS)]
    vcs = [vc_ref[:, hs(h)].astype(F32) for h in range(ATT_KV_HEADS)]
    return r0, hs, qs, kws, vws, kcs, vcs


def _attn_specs(s, c):
    full = lambda shape: pl.BlockSpec(shape, lambda n: (0, 0))
    return [pl.BlockSpec((BLOCK, QW), lambda n: (n, 0)), full((s + 2 * BLOCK, KVW)), full((s + 2 * BLOCK, KVW)),
            full((c, KVW)), full((c, KVW)), full((1, ATT_HEADS))]


def _attn_fwd(q, kp, vp, kc, vc, sink):
    s, c = q.shape[0], kc.shape[0]

    def body(q_ref, kp_ref, vp_ref, kc_ref, vc_ref, sink_ref, o_ref):
        n = pl.program_id(0)
        _, hs, qs, kws, vws, kcs, vcs = _attn_loads(n, q_ref, kp_ref, vp_ref, kc_ref, vc_ref)
        outs = _f_attn(qs, kws, vws, kcs, vcs, sink_ref[...], n, s)
        for h in range(ATT_KV_HEADS):
            _ungroup_rows(o_ref, h, outs[h])

    return pl.pallas_call(
        body, name="attn_fwd", grid=(s // BLOCK,), in_specs=_attn_specs(s, c),
        out_specs=pl.BlockSpec((BLOCK, QW), lambda n: (n, 0)), out_shape=jax.ShapeDtypeStruct((s, QW), BF16),
        compiler_params=_cp("parallel"),
    )(q, kp, vp, kc, vc, sink)


def _attn_bwd(do, q, kp, vp, kc, vc, sink):
    s, c = q.shape[0], kc.shape[0]

    def body(do_ref, q_ref, kp_ref, vp_ref, kc_ref, vc_ref, sink_ref, dq_ref, dkp_ref, dvp_ref, dkc_ref, dvc_ref,
             dsink_ref):
        n = pl.program_id(0)

        @pl.when(n == 0)
        def _():
            for r in (dkp_ref, dvp_ref, dkc_ref, dvc_ref, dsink_ref):
                r[...] = jnp.zeros_like(r)

        r0, hs, qs, kws, vws, kcs, vcs = _attn_loads(n, q_ref, kp_ref, vp_ref, kc_ref, vc_ref)
        _, vjp = jax.vjp(lambda qs, kws, vws, kcs, vcs, sink: _f_attn(qs, kws, vws, kcs, vcs, sink, n, s),
                         qs, kws, vws, kcs, vcs, sink_ref[...])
        dqs, dkws, dvws, dkcs, dvcs, dsink = vjp(tuple(_group_rows(do_ref, h) for h in range(ATT_KV_HEADS)))
        for h in range(ATT_KV_HEADS):
            _ungroup_rows(dq_ref, h, dqs[h])
            dkp_ref[pl.ds(r0, 3 * BLOCK), hs(h)] += dkws[h]
            dvp_ref[pl.ds(r0, 3 * BLOCK), hs(h)] += dvws[h]
            dkc_ref[:, hs(h)] += dkcs[h]
            dvc_ref[:, hs(h)] += dvcs[h]
        dsink_ref[...] += dsink

    full = lambda shape: pl.BlockSpec(shape, lambda n: (0, 0))
    return pl.pallas_call(
        body, name="attn_bwd", grid=(s // BLOCK,),
        in_specs=[pl.BlockSpec((BLOCK, QW), lambda n: (n, 0))] + _attn_specs(s, c),
        out_specs=[pl.BlockSpec((BLOCK, QW), lambda n: (n, 0)), full((s + 2 * BLOCK, KVW)), full((s + 2 * BLOCK, KVW)),
                   full((c, KVW)), full((c, KVW)), full((1, ATT_HEADS))],
        out_shape=[jax.ShapeDtypeStruct((s, QW), F32), jax.ShapeDtypeStruct((s + 2 * BLOCK, KVW), F32),
                   jax.ShapeDtypeStruct((s + 2 * BLOCK, KVW), F32), jax.ShapeDtypeStruct((c, KVW), F32),
                   jax.ShapeDtypeStruct((c, KVW), F32), jax.ShapeDtypeStruct((1, ATT_HEADS), F32)],
        compiler_params=_cp("arbitrary"),
    )(do, q, kp, vp, kc, vc, sink)


GLA_GROUPS = 2
GLA_GROUP_HEADS = GLA_HEADS // GLA_GROUPS
GKG, GVG = GKW // GLA_GROUPS, GVW // GLA_GROUPS


def _gla_masks(heads=GLA_HEADS):
    hk = np.arange(heads * GLA_DK) // GLA_DK
    hv = np.arange(heads * GLA_DV) // GLA_DV
    head_k = (np.arange(heads)[:, None] == hk[None, :]).astype(np.float32)
    head_v = (np.arange(heads)[:, None] == hv[None, :]).astype(np.float32)
    bd_t = (hv[:, None] == hk[None, :]).astype(np.float32)
    return jnp.asarray(head_k), jnp.asarray(head_v), jnp.asarray(bd_t)


def _group_states(st):
    return jnp.stack([st[g * GVG:(g + 1) * GVG, g * GKG:(g + 1) * GKG] for g in range(GLA_GROUPS)])


def _ungroup_states(st):
    out = jnp.zeros((GVW, GKW), st.dtype)
    for g in range(GLA_GROUPS):
        out = out.at[g * GVG:(g + 1) * GVG, g * GKG:(g + 1) * GKG].set(st[g])
    return out


def _tri(n, rev, strict=False):
    i = lax.broadcasted_iota(jnp.int32, (n, n), 0)
    j = lax.broadcasted_iota(jnp.int32, (n, n), 1)
    if strict:
        keep = (j > i) if rev else (j < i)
    else:
        keep = (j >= i) if rev else (j <= i)
    return keep


def _f_gla_chunk(q, k, v, la, st, head_k, head_v, bd_t, rev):
    heads, kw, vw = head_k.shape[0], q.shape[1], v.shape[1]
    keep = _tri(GLA_CHUNK, rev)
    b = _nn_hi(keep.astype(F32), la)
    bl = jnp.sum(la, axis=0, keepdims=True)
    qd = q * (GLA_DK ** -0.5) * jnp.exp(b)
    ki = k * jnp.exp(-b)
    kd = k * jnp.exp(bl - b)
    q_heads = (qd[None, :, :] * head_k[:, None, :]).reshape(heads * GLA_CHUNK, kw)
    a_all = _nt(q_heads, ki).reshape(heads, GLA_CHUNK, GLA_CHUNK)
    a_all = jnp.where(keep[None, :, :], a_all, 0.0).reshape(heads * GLA_CHUNK, GLA_CHUNK)
    o_all = _nn(a_all, v).reshape(heads, GLA_CHUNK, vw)
    intra = jnp.sum(o_all * head_v[:, None, :], axis=0)
    inter = _nt(qd, st)
    st_new = st * jnp.exp(bl) + bd_t * _tn(v, kd)
    return intra + inter, st_new


def _gla_specs(s, tb, order):
    return [pl.BlockSpec((tb, GKW), lambda i: (order(i), C_GQ // GKW)),
            pl.BlockSpec((tb, GKW), lambda i: (order(i), C_GK // GKW)),
            pl.BlockSpec((tb, GVW), lambda i: (order(i), C_GV // GVW)),
            pl.BlockSpec((tb, GKW), lambda i: (order(i), 0))]


GLA_BLOCK_CHUNKS = 4


def _gla_fwd(p, la_f, la_b, st_f0, st_b0):
    s = p.shape[0]
    tb = GLA_BLOCK_CHUNKS * GLA_CHUNK
    nblk = s // tb
    up, down = (lambda i: i), (lambda i: nblk - 1 - i)
    masks = _gla_masks(GLA_GROUP_HEADS)

    def scan(rev, q_ref, k_ref, v_ref, la_ref, o_ref, sts_ref, st_ref, consts):
        for g in range(GLA_GROUPS):
            gk, gv = slice(g * GKG, (g + 1) * GKG), slice(g * GVG, (g + 1) * GVG)
            st = st_ref[g]
            sts_ref[0, g] = st
            chunks = range(GLA_BLOCK_CHUNKS)
            for ci in (reversed(chunks) if rev else chunks):
                rows = slice(ci * GLA_CHUNK, (ci + 1) * GLA_CHUNK)
                o, st = _f_gla_chunk(q_ref[rows, gk], k_ref[rows, gk], v_ref[rows, gv], la_ref[rows, gk], st, *consts,
                                     rev)
                o_ref[rows, gv] = o
            st_ref[g] = st

    def body(qf, kf, vf, laf, qb, kb, vb, lab, stf0, stb0, hk_ref, hv_ref, bd_ref, of_ref, stsf_ref, ob_ref, stsb_ref,
             stf_ref, stb_ref):
        @pl.when(pl.program_id(0) == 0)
        def _():
            stf_ref[...] = stf0[...]
            stb_ref[...] = stb0[...]

        consts = (hk_ref[...], hv_ref[...], bd_ref[...])
        scan(False, qf, kf, vf, laf, of_ref, stsf_ref, stf_ref, consts)
        scan(True, qb, kb, vb, lab, ob_ref, stsb_ref, stb_ref, consts)

    full = lambda a: pl.BlockSpec(a.shape, lambda i: (0,) * a.ndim)
    outs = lambda order: [pl.BlockSpec((tb, GVW), lambda i: (order(i), 0)),
                          pl.BlockSpec((1, GLA_GROUPS, GVG, GKG), lambda i: (order(i), 0, 0, 0))]
    return pl.pallas_call(
        body, name="gla_fwd", grid=(nblk,),
        in_specs=_gla_specs(s, tb, up) + _gla_specs(s, tb, down) + [full(st_f0), full(st_b0)]
        + [full(m) for m in masks],
        out_specs=outs(up) + outs(down),
        out_shape=[jax.ShapeDtypeStruct((s, GVW), F32), jax.ShapeDtypeStruct((nblk, GLA_GROUPS, GVG, GKG), F32)] * 2,
        scratch_shapes=[pltpu.VMEM((GLA_GROUPS, GVG, GKG), F32)] * 2,
        compiler_params=_cp("arbitrary"),
    )(p, p, p, la_f, p, p, p, la_b, st_f0, st_b0, *masks)


def _gla_bwd(p, la_f, la_b, sts_f, sts_b, do, after=None):
    s = p.shape[0]
    tb = GLA_BLOCK_CHUNKS * GLA_CHUNK
    nblk = s // tb
    up, down = (lambda i: i), (lambda i: nblk - 1 - i)
    masks = _gla_masks(GLA_GROUP_HEADS)
    follow = () if after is None else (after,)

    def back(rev, q_ref, k_ref, v_ref, la_ref, sts_ref, do_ref, dq_ref, dk_ref, dv_ref, dla_ref, dst0_ref, dst_ref,
             consts):
        def block(q, k, v, la, st):
            outs = [None] * GLA_BLOCK_CHUNKS
            chunks = range(GLA_BLOCK_CHUNKS)
            for ci in (reversed(chunks) if rev else chunks):
                outs[ci], st = _f_gla_chunk(q[ci], k[ci], v[ci], la[ci], st, *consts, rev)
            return tuple(outs), st

        for g in range(GLA_GROUPS):
            gk, gv = slice(g * GKG, (g + 1) * GKG), slice(g * GVG, (g + 1) * GVG)
            split = lambda r, cols: tuple(r[ci * GLA_CHUNK:(ci + 1) * GLA_CHUNK, cols].astype(F32)
                                          for ci in range(GLA_BLOCK_CHUNKS))
            _, vjp = jax.vjp(block, split(q_ref, gk), split(k_ref, gk), split(v_ref, gv), split(la_ref, gk),
                             sts_ref[0, g])
            dq, dk, dv, dla, dst = vjp((split(do_ref, gv), dst_ref[g]))
            for ci in range(GLA_BLOCK_CHUNKS):
                rows = slice(ci * GLA_CHUNK, (ci + 1) * GLA_CHUNK)
                dq_ref[rows, gk], dk_ref[rows, gk], dv_ref[rows, gv], dla_ref[rows, gk] = dq[ci], dk[ci], dv[ci], dla[ci]
            dst_ref[g] = dst
            dst0_ref[g] = dst

    def body(*refs):
        ins, (hk_ref, hv_ref, bd_ref) = refs[:12], refs[12:15]
        outs = refs[15 + len(follow):]

        @pl.when(pl.program_id(0) == 0)
        def _():
            outs[10][...] = jnp.zeros_like(outs[10])
            outs[11][...] = jnp.zeros_like(outs[11])

        consts = (hk_ref[...], hv_ref[...], bd_ref[...])
        back(False, *ins[:6], *outs[:5], outs[10], consts)
        back(True, *ins[6:], *outs[5:10], outs[11], consts)

    full = lambda a: pl.BlockSpec(a.shape, lambda i: (0,) * a.ndim)

    def ins(order):
        return _gla_specs(s, tb, order) + [pl.BlockSpec((1, GLA_GROUPS, GVG, GKG), lambda i: (order(i), 0, 0, 0)),
                                           pl.BlockSpec((tb, GVW), lambda i: (order(i), 0))]

    def outs(order):
        blk = lambda w: pl.BlockSpec((tb, w), lambda i: (order(i), 0))
        return [blk(GKW), blk(GKW), blk(GVW), blk(GKW), pl.BlockSpec((GLA_GROUPS, GVG, GKG), lambda i: (0, 0, 0))]

    shapes = [jax.ShapeDtypeStruct((s, GKW), F32), jax.ShapeDtypeStruct((s, GKW), F32),
              jax.ShapeDtypeStruct((s, GVW), F32), jax.ShapeDtypeStruct((s, GKW), F32),
              jax.ShapeDtypeStruct((GLA_GROUPS, GVG, GKG), F32)]
    both = pl.pallas_call(
        body, name="gla_bwd", grid=(nblk,),
        in_specs=ins(down) + ins(up) + [full(m) for m in masks] + [pl.BlockSpec(memory_space=pl.ANY)] * len(follow),
        out_specs=outs(down) + outs(up), out_shape=shapes * 2,
        scratch_shapes=[pltpu.VMEM((GLA_GROUPS, GVG, GKG), F32)] * 2,
        compiler_params=_cp("arbitrary"),
    )(p, p, p, la_f, sts_f, do, p, p, p, la_b, sts_b, do, *masks, *follow)
    return both[:5], both[5:]


def _f_ctx_state(k, v, la_f, la_b, bd_t):
    c = k.shape[0]
    after = _nn_hi(_tri(c, True, strict=True).astype(F32), la_f)
    before = _nn_hi(_tri(c, False, strict=True).astype(F32), la_b)
    return bd_t * _tn(v, k * jnp.exp(after)), bd_t * _tn(v, k * jnp.exp(before))


def _ctx_state(pc, la_f, la_b):
    c = pc.shape[0]
    bd_t = _gla_masks()[2]

    def body(k_ref, v_ref, lf_ref, lb_ref, bd_ref, sf_ref, sb_ref):
        sf_ref[...], sb_ref[...] = _f_ctx_state(k_ref[...], v_ref[...], lf_ref[...], lb_ref[...], bd_ref[...])

    full = lambda a: pl.BlockSpec(a.shape, lambda i: (0, 0))
    return pl.pallas_call(
        body, name="ctx_state_fwd", grid=(1,),
        in_specs=[pl.BlockSpec((c, GKW), lambda i: (0, C_GK // GKW)), pl.BlockSpec((c, GVW), lambda i: (0, C_GV // GVW)),
                  full(la_f), full(la_b), full(bd_t)],
        out_specs=[pl.BlockSpec((GVW, GKW), lambda i: (0, 0))] * 2,
        out_shape=[jax.ShapeDtypeStruct((GVW, GKW), F32)] * 2,
        compiler_params=_cp("arbitrary"),
    )(pc, pc, la_f, la_b, bd_t)


def _ctx_state_bwd(pc, la_f, la_b, dsf, dsb):
    c = pc.shape[0]
    bd_t = _gla_masks()[2]

    def body(k_ref, v_ref, lf_ref, lb_ref, bd_ref, dsf_ref, dsb_ref, dk_ref, dv_ref, dlf_ref, dlb_ref):
        _, vjp = jax.vjp(lambda k, v, lf, lb: _f_ctx_state(k, v, lf, lb, bd_ref[...]),
                         k_ref[...], v_ref[...], lf_ref[...], lb_ref[...])
        dk, dv, dlf, dlb = vjp((dsf_ref[...], dsb_ref[...]))
        dk_ref[...], dv_ref[...] = dk.astype(BF16), dv.astype(BF16)
        dlf_ref[...], dlb_ref[...] = dlf, dlb

    full = lambda a: pl.BlockSpec(a.shape, lambda i: (0, 0))
    return pl.pallas_call(
        body, name="ctx_state_bwd", grid=(1,),
        in_specs=[pl.BlockSpec((c, GKW), lambda i: (0, C_GK // GKW)), pl.BlockSpec((c, GVW), lambda i: (0, C_GV // GVW)),
                  full(la_f), full(la_b), full(bd_t), full(dsf), full(dsb)],
        out_specs=[pl.BlockSpec((c, GKW), lambda i: (0, 0)), pl.BlockSpec((c, GVW), lambda i: (0, 0)),
                   pl.BlockSpec((c, GKW), lambda i: (0, 0)), pl.BlockSpec((c, GKW), lambda i: (0, 0))],
        out_shape=[jax.ShapeDtypeStruct((c, GKW), BF16), jax.ShapeDtypeStruct((c, GVW), BF16),
                   jax.ShapeDtypeStruct((c, GKW), F32), jax.ShapeDtypeStruct((c, GKW), F32)],
        compiler_params=_cp("arbitrary"),
    )(pc, pc, la_f, la_b, bd_t, dsf, dsb)


_SRC_COLS = ((0, QW), (QW + 2 * KVW + 2 * GKW, GVW), (QW + 2 * KVW + 2 * GKW + GVW, GVW), (QW, KVW), (QW + KVW, KVW),
             (QW + 2 * KVW, GKW), (QW + 2 * KVW + GKW, GKW), (IN_COLS - 2 * GATE_RANK, 2 * GATE_RANK))
_DST_COLS = (C_Q, C_GV, C_GG, C_K, C_V, C_GQ, C_GK, C_Z)


def _pack_w_in(w_in):
    parts = [w_in[:, s:s + n] for s, n in _SRC_COLS]
    parts.append(jnp.zeros((w_in.shape[0], IN_PAD - C_Z - 2 * GATE_RANK), w_in.dtype))
    return jnp.concatenate(parts, axis=1)


def _unpack_w_in_grad(g):
    by_src = sorted(zip(_SRC_COLS, _DST_COLS))
    return jnp.concatenate([g[:, d:d + n] for (_, n), d in by_src], axis=1)


def _prep_weights(w_in, w_gate_fwd, w_gate_bwd):
    pad_rows = lambda w, at: jnp.zeros((LANES, GKW), F32).at[at:at + GATE_RANK].set(w)
    return {"w_in": _pack_w_in(w_in).astype(BF16), "wg_f": pad_rows(w_gate_fwd, 0),
            "wg_b": pad_rows(w_gate_bwd, GATE_RANK)}


def _local_step(x, ctx, target, ada, ada_c, w, late_weights, reduce_behind=None, reduce_w_in=None):
    s, d = x.shape
    sh1, sc1, gt1, sh2, sc2, gt2 = [ada[:, i * d:(i + 1) * d] for i in range(6)]
    sh1c, sc1c = ada_c[:, :d], ada_c[:, d:2 * d]
    cos, sin = _rope_tables(s)
    gt = jnp.tile(w["g_gla_norm"], (1, GLA_HEADS))

    h = _norm_mod("pre_mix", x, w["g_pre_mix"], sh1, sc1)
    hc = _norm_mod("pre_mix_ctx", ctx, w["g_pre_mix"], sh1c, sc1c)
    p = _mm("proj_in", h, w["w_in"], "nn")
    pc = _mm("proj_in_ctx", hc, w["w_in"], "nn")
    q_rot, k_rot, v_b = _rope_fwd("rope", p, cos, sin)
    pad = ((BLOCK, BLOCK), (0, 0))
    kp, vp = jnp.pad(k_rot, pad), jnp.pad(v_b, pad)
    kc, vc = pc[:, C_K:C_K + KVW].astype(BF16), pc[:, C_V:C_V + KVW].astype(BF16)
    attn = _attn_fwd(q_rot, kp, vp, kc, vc, w["attn_sink"])
    gate_w = (w["wg_f"], w["wg_b"], w["b_gate_fwd"], w["b_gate_bwd"])
    la_f, la_b = _gate_fwd("gate", p, *gate_w)
    la_fc, la_bc = _gate_fwd("gate_ctx", pc, *gate_w)
    st_f0, st_b0 = _ctx_state(pc, la_fc, la_bc)
    o_f, sts_f, o_b, sts_b = _gla_fwd(p, la_f, la_b, _group_states(st_f0), _group_states(st_b0))
    mix = _gla_out("gla_out", attn, o_f, o_b, p, gt)
    w_out, w_ffn_in_t, w_ffn_out = late_weights(attn)
    y = _mm("proj_out", mix, w_out, "nn")
    x1, h2 = _post_res_norm_mod("post_mix_pre_ffn", x, y, w["g_post_mix"], gt1, w["g_pre_ffn"], sh2, sc2)
    u, a = _ffn_in_swiglu("ffn_in", h2, w_ffn_in_t)
    f = _mm("ffn_out", a, w_ffn_out, "nn")
    g = {}
    dx2, df, loss, g["g_post_ffn"], dgt2 = _post_res_loss("post_ffn_loss", x1, f, w["g_post_ffn"], gt2, target)

    g["w_ffn_out"] = _mm("ffn_out_dw", a, df, "tn")
    du = _ffn_out_dx_swiglu_bwd("ffn_out_dx", df, w_ffn_out, u)
    dh2 = _mm("ffn_in_dx", du, w_ffn_in_t, "nn")
    g["w_ffn_in_t"] = _mm("ffn_in_dw", du, h2, "tn")
    dx1, dy, g["g_pre_ffn"], dsh2, dsc2, g["g_post_mix"], dgt1 = _norm_mod_post_res_bwd(
        "pre_ffn_post_mix_bwd", dh2, dx2, x1, y, w["g_pre_ffn"], sh2, sc2, w["g_post_mix"], gt1)
    dmix = _mm("proj_out_dx", dy, w_out, "nt", BF16)
    g["w_out"] = _mm("proj_out_dw", mix, dy, "tn")
    rb, sink, token = reduce_behind, w["attn_sink"], None
    if rb is not None:
        gt = _behind(gt, rb.start(g["w_ffn_in_t"], g["w_ffn_out"], g["w_out"]))
    d_o, dgg, dgt = _gla_out_bwd("gla_out_bwd", dmix, o_f, o_b, p, gt)
    g["g_gla_norm"] = jnp.sum(dgt.reshape(GLA_HEADS, GLA_DV), axis=0, keepdims=True)
    if rb is not None:
        token = rb.pair(dgg)
    gla_f, gla_b = _gla_bwd(p, la_f, la_b, sts_f, sts_b, d_o, token)
    (dla_f, dst_f0), (dla_b, dst_b0) = gla_f[3:], gla_b[3:]
    dst_f0, dst_b0 = _ungroup_states(dst_f0), _ungroup_states(dst_b0)
    if rb is not None:
        sink = _behind(sink, rb.total(dla_b))
    dgkc, dgvc, dla_fc, dla_bc = _ctx_state_bwd(pc, la_fc, la_bc, dst_f0, dst_b0)
    dz, dwf, dwb, dbf, dbb = _gate_bwd("gate_bwd", p, dla_f, dla_b, *gate_w)
    dzc, dwfc, dwbc, dbfc, dbbc = _gate_bwd("gate_ctx_bwd", pc, dla_fc, dla_bc, *gate_w)
    g["w_gate_fwd"] = (dwf + dwfc)[:GATE_RANK]
    g["w_gate_bwd"] = (dwb + dwbc)[GATE_RANK:2 * GATE_RANK]
    g["b_gate_fwd"], g["b_gate_bwd"] = dbf + dbfc, dbb + dbbc
    dq_rot, dkp, dvp, dkc, dvc, g["attn_sink"] = _attn_bwd(dmix, q_rot, kp, vp, kc, vc, sink)
    if rb is not None:
        g["behind"] = rb.result(dq_rot)
    dp = _proj_grad("proj_grad", dq_rot, dkp[BLOCK:BLOCK + s], dvp[BLOCK:BLOCK + s], cos, sin, gla_f[:3], gla_b[:3],
                    dgg, dz)
    c_rows = ctx.shape[0]
    zeros = lambda n: jnp.zeros((c_rows, n), BF16)
    dpc = jnp.concatenate([zeros(QW), dgvc, zeros(GVW), dkc.astype(BF16), dvc.astype(BF16), zeros(GKW), dgkc, dzc],
                          axis=1)
    g["w_in"] = _mm("proj_in_dw", h, dp, "tn", init=_mm("proj_in_ctx_dw", hc, dpc, "tn"))
    token = None if reduce_w_in is None else reduce_w_in.start(g["w_in"])
    dh = _mm("proj_in_dx", dp, w["w_in"], "nt", after=token)
    dhc = _mm("proj_in_ctx_dx", dpc, w["w_in"], "nt")
    if reduce_w_in is not None:
        sh1 = _behind(sh1, reduce_w_in.pair(dh))
    dx, dg_a, dsh1, dsc1 = _norm_mod_bwd("pre_mix_bwd", dh, dx1, x, w["g_pre_mix"], sh1, sc1)
    if reduce_w_in is not None:
        dsh1 = _behind(dsh1, reduce_w_in.total(dx))
    _, dg_b, dsh1c, dsc1c = _norm_mod_bwd("pre_mix_ctx_bwd", dhc, jnp.zeros_like(dhc), ctx, w["g_pre_mix"], sh1c,
                                          sc1c)
    g["g_pre_mix"] = dg_a + dg_b
    d_ada = jnp.concatenate([dsh1, dsc1, dgt1, dsh2, dsc2, dgt2], axis=1)
    d_ada_c = jnp.concatenate([dsh1c, dsc1c, jnp.zeros((1, 4 * d), F32)], axis=1)
    return loss, dx, g, d_ada, d_ada_c


HBM = pl.BlockSpec(memory_space=pltpu.HBM)
N_DEV, N_CHIP = 8, 4


def _place():
    x, y, c = lax.axis_index("x"), lax.axis_index("y"), lax.axis_index("c")
    return x, y, c, [(1 - x, y), (x, 1 - y), (1 - x, 1 - y)]


def _row_tile(n, mult, cap):
    return max(t for t in range(mult, min(n, cap) + 1, mult) if n % t == 0)


def _ag_small(name, v, after=None):
    follow = () if after is None else (after,)

    def body(v_ref, *rest):
        out_ref, send_sems, recv_sems = rest[len(follow):]
        x, y, c, _ = _place()
        out_ref[4 * x + 2 * y + c] = v_ref[...]

        def peer(r):
            return ((1 - x) if r & 4 else x, (1 - y) if r & 2 else y, (1 - c) if r & 1 else c)

        def copy(r, block):
            px, py, pc = block
            return pltpu.make_async_remote_copy(
                src_ref=v_ref, dst_ref=out_ref.at[4 * px + 2 * py + pc], send_sem=send_sems.at[r - 1],
                recv_sem=recv_sems.at[r - 1], device_id=peer(r), device_id_type=MESH)

        sends = [copy(r, (x, y, c)) for r in range(1, N_DEV)]
        for cp in sends:
            cp.start()
        for r in range(1, N_DEV):
            copy(r, peer(r)).wait_recv()
        for cp in sends:
            cp.wait_send()

    return pl.pallas_call(
        body, name=name, out_shape=jax.ShapeDtypeStruct((N_DEV,) + v.shape, v.dtype),
        in_specs=[pl.BlockSpec(memory_space=pltpu.VMEM)] + [pl.BlockSpec(memory_space=pl.ANY)] * len(follow),
        out_specs=pl.BlockSpec(memory_space=pltpu.VMEM),
        scratch_shapes=[pltpu.SemaphoreType.DMA((N_DEV - 1,)), pltpu.SemaphoreType.DMA((N_DEV - 1,))],
    )(v, *follow)


def _halves(c, rows, mult):
    hr = rows // 2
    return pl.ds(pl.multiple_of(c * hr, mult), hr), pl.ds(pl.multiple_of((1 - c) * hr, mult), hr)


def _ag_shards(name, shard):
    rows = shard.shape[0]

    def body(w_ref, out_ref, send_sems, recv_sems, local_sem):
        x, y, c, chips = _place()
        mine_half, other_half = _halves(c, rows, 16)
        me = 2 * x + y
        mine = pltpu.make_async_copy(w_ref, out_ref.at[me], local_sem)
        mine.start()

        def copy(k, src, chip, half, to):
            return pltpu.make_async_remote_copy(
                src_ref=src, dst_ref=out_ref.at[chip, half], send_sem=send_sems.at[k], recv_sem=recv_sems.at[k],
                device_id=to, device_id_type=MESH)

        first = [copy(j, w_ref.at[mine_half], me, mine_half, (px, py, c)) for j, (px, py) in enumerate(chips)]
        for cp in first:
            cp.start()
        passed = []
        for j, (px, py) in enumerate(chips):
            pk = 2 * px + py
            copy(j, w_ref.at[mine_half], pk, mine_half, (px, py, c)).wait_recv()
            cp = copy(3 + j, out_ref.at[pk, mine_half], pk, mine_half, (x, y, 1 - c))
            cp.start()
            passed.append(cp)
        for j, (px, py) in enumerate(chips):
            copy(3 + j, w_ref.at[mine_half], 2 * px + py, other_half, (x, y, 1 - c)).wait_recv()
        for cp in first + passed:
            cp.wait_send()
        mine.wait()

    return pl.pallas_call(
        body, name=name, out_shape=jax.ShapeDtypeStruct((N_CHIP,) + shard.shape, shard.dtype),
        in_specs=[HBM], out_specs=HBM,
        scratch_shapes=[pltpu.SemaphoreType.DMA((6,)), pltpu.SemaphoreType.DMA((6,)), pltpu.SemaphoreType.DMA],
    )(shard)


def _swap_half(name, g):
    n_sh, rows, n = g.shape

    def body(g_ref, a_ref, send_sem, recv_sem):
        x, y, c, _ = _place()
        _, other_half = _halves(c, rows, 8)
        cp = pltpu.make_async_remote_copy(
            src_ref=g_ref.at[pl.ds(0, n_sh), other_half], dst_ref=a_ref, send_sem=send_sem, recv_sem=recv_sem,
            device_id=(x, y, 1 - c), device_id_type=MESH)
        cp.start()
        cp.wait()

    return pl.pallas_call(
        body, name=name, out_shape=jax.ShapeDtypeStruct((n_sh, rows // 2, n), g.dtype), in_specs=[HBM], out_specs=HBM,
        scratch_shapes=[pltpu.SemaphoreType.DMA, pltpu.SemaphoreType.DMA],
    )(g)


def _add_half(name, g, a, c_idx):
    n_sh, hr, n = a.shape
    tr = _row_tile(hr, 16, 1024)
    nb = hr // tr

    def body(c_ref, g_ref, a_ref, o_ref):
        o_ref[...] = (g_ref[...] + a_ref[...]).astype(o_ref.dtype)

    return pl.pallas_call(
        body, name=name, out_shape=jax.ShapeDtypeStruct(a.shape, BF16),
        grid_spec=pltpu.PrefetchScalarGridSpec(
            num_scalar_prefetch=1, grid=(n_sh, nb),
            in_specs=[pl.BlockSpec((1, tr, n), lambda s, i, c_ref: (s, c_ref[0] * nb + i, 0)),
                      pl.BlockSpec((1, tr, n), lambda s, i, c_ref: (s, i, 0))],
            out_specs=pl.BlockSpec((1, tr, n), lambda s, i, c_ref: (s, i, 0))),
        compiler_params=_cp("parallel", "parallel"),
    )(c_idx, g, a)


def _scatter_chips(name, h):
    def body(h_ref, b_ref, send_sems, recv_sems, local_sem):
        x, y, c, chips = _place()
        me = 2 * x + y
        mine = pltpu.make_async_copy(h_ref.at[me], b_ref.at[me], local_sem)
        mine.start()

        def copy(j, src_block, dst_block, to):
            return pltpu.make_async_remote_copy(
                src_ref=h_ref.at[src_block], dst_ref=b_ref.at[dst_block], send_sem=send_sems.at[j],
                recv_sem=recv_sems.at[j], device_id=to, device_id_type=MESH)

        sends = [copy(j, 2 * px + py, me, (px, py, c)) for j, (px, py) in enumerate(chips)]
        for cp in sends:
            cp.start()
        for j, (px, py) in enumerate(chips):
            copy(j, me, 2 * px + py, (px, py, c)).wait_recv()
        for cp in sends:
            cp.wait_send()
        mine.wait()

    return pl.pallas_call(
        body, name=name, out_shape=jax.ShapeDtypeStruct(h.shape, h.dtype), in_specs=[HBM], out_specs=HBM,
        scratch_shapes=[pltpu.SemaphoreType.DMA((3,)), pltpu.SemaphoreType.DMA((3,)), pltpu.SemaphoreType.DMA],
    )(h)


def _sum_chips(name, b):
    n_sh, hr, n = b.shape
    tr = _row_tile(hr, 16, 1024)

    def body(b0, b1, b2, b3, o_ref):
        o_ref[...] = ((b0[0].astype(F32) + b1[0].astype(F32)) + b2[0].astype(F32)) + b3[0].astype(F32)

    return pl.pallas_call(
        body, name=name, grid=(hr // tr,), out_shape=jax.ShapeDtypeStruct((hr, n), F32),
        in_specs=[pl.BlockSpec((1, tr, n), functools.partial(lambda i, k: (k, i, 0), k=k)) for k in range(n_sh)],
        out_specs=pl.BlockSpec((tr, n), lambda i: (i, 0)), compiler_params=_cp("parallel"),
    )(b, b, b, b)


def _share_half(name, f):
    hr, n = f.shape

    def body(f_ref, out_ref, send_sem, recv_sem, local_sem):
        x, y, c, _ = _place()
        mine_half, other_half = _halves(c, 2 * hr, 8)
        mine = pltpu.make_async_copy(f_ref, out_ref.at[mine_half], local_sem)
        mine.start()

        def copy(half):
            return pltpu.make_async_remote_copy(
                src_ref=f_ref, dst_ref=out_ref.at[half], send_sem=send_sem, recv_sem=recv_sem,
                device_id=(x, y, 1 - c), device_id_type=MESH)

        send = copy(mine_half)
        send.start()
        copy(other_half).wait_recv()
        send.wait_send()
        mine.wait()

    return pl.pallas_call(
        body, name=name, out_shape=jax.ShapeDtypeStruct((2 * hr, n), f.dtype), in_specs=[HBM], out_specs=HBM,
        scratch_shapes=[pltpu.SemaphoreType.DMA, pltpu.SemaphoreType.DMA, pltpu.SemaphoreType.DMA],
    )(f)


def _reduce_shards(name, g, c_idx):
    a = _swap_half(name + "_swap", g)
    h = _add_half(name + "_pair", g, a, c_idx)
    b = _scatter_chips(name + "_scatter", h)
    f = _sum_chips(name + "_sum", b)
    return _share_half(name + "_share", f)


SEM = pl.BlockSpec(memory_space=pltpu.SEMAPHORE)
ANY = pl.BlockSpec(memory_space=pl.ANY)
DATAFLOW = pltpu.SideEffectType.DATAFLOW_SIDE_EFFECTING


def _remote(src, dst, send_sems, recv_sems, k, to):
    return pltpu.make_async_remote_copy(src_ref=src, dst_ref=dst, send_sem=send_sems.at[k], recv_sem=recv_sems.at[k],
                                        device_id=to, device_id_type=MESH)


def _split_copy(name, src, land_shape, land_dtype, n, plan, after=None):
    after = jnp.zeros((8, LANES), F32) if after is None else after

    def start_body(src_ref, land_ref, after_ref, send_sems, recv_sems, src_thru, land_thru, token):
        for cp in plan(src_ref, land_ref, send_sems, recv_sems)[0]:
            cp.start()
        token[...] = jnp.zeros_like(token)

    sems = pltpu.SemaphoreType.DMA((n,))
    send_sems, recv_sems, src_thru, land_thru, token = pl.pallas_call(
        start_body, name=name + "_start",
        out_shape=(sems, sems, pltpu.HBM(src.shape, src.dtype), pltpu.HBM(land_shape, land_dtype),
                   jax.ShapeDtypeStruct((8, LANES), F32)),
        in_specs=(HBM, HBM, ANY), out_specs=(SEM, SEM, HBM, HBM, pl.BlockSpec(memory_space=pltpu.VMEM)),
        input_output_aliases={0: 2, 1: 3}, compiler_params=pltpu.CompilerParams(has_side_effects=DATAFLOW),
    )(pltpu.with_memory_space_constraint(src, pltpu.HBM),
      pltpu.with_memory_space_constraint(lax.empty(land_shape, land_dtype), pltpu.HBM), after)

    def wait(after):
        def wait_body(src_ref, land_ref, send_sems, recv_sems, after_ref, src_out, land_out):
            sent, received = plan(src_ref, land_ref, send_sems, recv_sems)
            for cp in sent:
                cp.wait_send()
            for cp in received:
                cp.wait_recv()

        return pl.pallas_call(
            wait_body, name=name + "_wait",
            out_shape=(pltpu.HBM(src.shape, src.dtype), pltpu.HBM(land_shape, land_dtype)),
            in_specs=(HBM, HBM, SEM, SEM, ANY), out_specs=(HBM, HBM), input_output_aliases={0: 0, 1: 1},
            compiler_params=pltpu.CompilerParams(has_side_effects=DATAFLOW),
        )(src_thru, land_thru, send_sems, recv_sems, after)

    return token, wait


def _behind(x, token):
    return x + token[0, 0]


def _plan_gather(src_ref, land_ref, send_sems, recv_sems):
    x, y, c, chips = _place()
    sent = [_remote(src_ref, land_ref.at[2 * x + y], send_sems, recv_sems, j, (px, py, c))
            for j, (px, py) in enumerate(chips)]
    received = [_remote(src_ref, land_ref.at[2 * px + py], send_sems, recv_sems, j, (px, py, c))
                for j, (px, py) in enumerate(chips)]
    return sent, received


def _plan_swap(src_ref, land_ref, send_sems, recv_sems):
    x, y, c, _ = _place()
    _, other_half = _halves(c, src_ref.shape[1], 8)
    cp = _remote(src_ref.at[pl.ds(0, src_ref.shape[0]), other_half], land_ref, send_sems, recv_sems, 0, (x, y, 1 - c))
    return [cp], [cp]


def _plan_scatter(src_ref, land_ref, send_sems, recv_sems):
    x, y, c, chips = _place()
    sent = [_remote(src_ref.at[2 * px + py], land_ref.at[2 * x + y], send_sems, recv_sems, j, (px, py, c))
            for j, (px, py) in enumerate(chips)]
    received = [_remote(src_ref.at[2 * px + py], land_ref.at[2 * px + py], send_sems, recv_sems, j, (px, py, c))
                for j, (px, py) in enumerate(chips)]
    return sent, received


def _plan_share(src_ref, land_ref, send_sems, recv_sems):
    x, y, c, _ = _place()
    mine_half, other_half = _halves(c, land_ref.shape[0], 8)
    return ([_remote(src_ref, land_ref.at[mine_half], send_sems, recv_sems, 0, (x, y, 1 - c))],
            [_remote(src_ref, land_ref.at[other_half], send_sems, recv_sems, 0, (x, y, 1 - c))])


def _pack_shard_rows(name, parts):
    rows = [t.shape[0] // N_CHIP for t in parts]
    n, total = parts[0].shape[1], sum(t.shape[0] // N_CHIP for t in parts)
    slab, at = None, 0
    for i, (t, r) in enumerate(zip(parts, rows)):
        tr = max(c for c in range(8, min(r, 512) + 1, 8) if r % c == 0 and at % c == 0)
        nb, ob = r // tr, at // tr

        def body(t_ref, *rest):
            rest[-1][0] = t_ref[...]

        slab = pl.pallas_call(
            body, name=f"{name}_{i}", grid=(N_CHIP, nb), out_shape=jax.ShapeDtypeStruct((N_CHIP, total, n), t.dtype),
            in_specs=[pl.BlockSpec((tr, n), functools.partial(lambda k, j, nb: (k * nb + j, 0), nb=nb))]
            + ([] if slab is None else [pl.BlockSpec(memory_space=pl.ANY)]),
            out_specs=pl.BlockSpec((1, tr, n), functools.partial(lambda k, j, ob: (k, ob + j, 0), ob=ob)),
            input_output_aliases={} if slab is None else {1: 0}, compiler_params=_cp("parallel", "parallel"),
        )(*((t,) if slab is None else (t, slab)))
        at += r
    return slab


class _GatherBehind:
    def __init__(self, name, shard, chip, after=None):
        self.chip = chip
        self.token, self.wait = _split_copy(name, shard, (N_CHIP,) + shard.shape, shard.dtype, 3, _plan_gather,
                                            after)

    def result(self, after):
        shard, land = self.wait(after)
        return lax.dynamic_update_slice(land, shard[None], (self.chip, 0, 0))


class _ReduceBehind:
    def __init__(self, name, chip, c, c_idx):
        self.name, self.chip, self.c, self.c_idx = name, chip, c, c_idx

    def start(self, *grads):
        return self.start_slab(_pack_shard_rows(self.name + "_pack", grads))

    def start_slab(self, g):
        n_sh, rows, n = g.shape
        token, self.wait = _split_copy(self.name + "_swap", g, (n_sh, rows // 2, n), g.dtype, 1, _plan_swap)
        return token

    def pair(self, after):
        g, a = self.wait(after)
        h = _add_half(self.name + "_pair", g, a, self.c_idx)
        token, self.wait = _split_copy(self.name + "_scatter", h, h.shape, h.dtype, 3, _plan_scatter)
        return token

    def total(self, after):
        h, b = self.wait(after)
        b = lax.dynamic_update_slice(b, lax.dynamic_slice_in_dim(h, self.chip, 1, axis=0), (self.chip, 0, 0))
        f = _sum_chips(self.name + "_sum", b)
        token, self.wait = _split_copy(self.name + "_share", f, (2 * f.shape[0], f.shape[1]), f.dtype, 1,
                                       _plan_share)
        return token

    def result(self, after):
        f, out = self.wait(after)
        return lax.dynamic_update_slice(out, f, (self.c * f.shape[0], 0))


class _ReduceColsBehind(_ReduceBehind):
    def start(self, g_padded):
        g = _unpack_w_in_grad(g_padded)
        n = g.shape[1] // N_CHIP
        return self.start_slab(jnp.stack([g[:, k * n:(k + 1) * n] for k in range(N_CHIP)]))


def _f_adamw(w, g, m, v):
    m = ADAM_B1 * m + (1.0 - ADAM_B1) * g
    v = ADAM_B2 * v + (1.0 - ADAM_B2) * (g * g)
    m_hat = m / (1.0 - ADAM_B1 ** ADAM_STEP)
    v_hat = v / (1.0 - ADAM_B2 ** ADAM_STEP)
    return -ADAM_LR * (m_hat / (jnp.sqrt(v_hat) + ADAM_EPS) + ADAM_WD * w), m, v


def _adamw(name, w, g, m, v):
    rows, n = w.shape
    return _rowwise(name, lambda w, g, m, v: (_f_adamw(w, g, m, v), ()), rows, [(t, n, 0) for t in (w, g, m, v)], [],
                    [(n, F32)] * 3, [], tm=_row_tile(rows, 8, 256))


def _pack_rows(parts):
    rows = []
    for t in parts:
        t = t.reshape(-1)
        rows.append(jnp.pad(t, (0, -t.shape[0] % LANES)).reshape(-1, LANES))
    out = jnp.concatenate(rows, axis=0)
    return jnp.pad(out, ((0, -out.shape[0] % 8), (0, 0)))


def _unpack_rows(packed, shapes):
    out, r = [], 0
    for shp in shapes:
        n = int(np.prod(shp))
        nr = -(-n // LANES)
        out.append(packed[r:r + nr].reshape(-1)[:n].reshape(shp))
        r += nr
    return out


def _sum_blocks(name, g):
    def body(g_ref, o_ref):
        acc = g_ref[0]
        for k in range(1, g.shape[0]):
            acc = acc + g_ref[k]
        o_ref[...] = acc

    return pl.pallas_call(body, name=name, out_shape=jax.ShapeDtypeStruct(g.shape[1:], F32))(g)


def _silu(t):
    return t * _sigmoid(t)


def _ada_fwd(cc, w_ada):
    n = w_ada.shape[1]
    tn = _row_tile(n, LANES, 512)

    def body(cc_ref, w_ref, o_ref):
        o_ref[...] = _nn(_silu(cc_ref[...]), w_ref[...])

    return pl.pallas_call(
        body, name="ada_fwd", grid=(n // tn,), out_shape=jax.ShapeDtypeStruct((cc.shape[0], n), F32),
        in_specs=[pl.BlockSpec(cc.shape, lambda j: (0, 0)), pl.BlockSpec((w_ada.shape[0], tn), lambda j: (0, j))],
        out_specs=pl.BlockSpec((cc.shape[0], tn), lambda j: (0, j)), compiler_params=_cp("parallel"),
    )(cc, w_ada)


def _ada_bwd(cc, dm, w_ada):
    d, n = w_ada.shape
    tn = _row_tile(n, LANES, 512)

    def body(cc_ref, dm_ref, w_ref, gw_ref, ds_ref):
        @pl.when(pl.program_id(0) == 0)
        def _():
            ds_ref[...] = jnp.zeros_like(ds_ref)

        gw_ref[...] = _raw_dot("tn", _silu(cc_ref[...]), dm_ref[...], True)
        ds_ref[...] += _raw_dot("nt", dm_ref[...], w_ref[...], False)

    return pl.pallas_call(
        body, name="ada_bwd", grid=(n // tn,),
        out_shape=[jax.ShapeDtypeStruct((d, n), F32), jax.ShapeDtypeStruct(cc.shape, F32)],
        in_specs=[pl.BlockSpec(cc.shape, lambda j: (0, 0)), pl.BlockSpec((cc.shape[0], tn), lambda j: (0, j)),
                  pl.BlockSpec((d, tn), lambda j: (0, j))],
        out_specs=[pl.BlockSpec((d, tn), lambda j: (0, j)), pl.BlockSpec(cc.shape, lambda j: (0, 0))],
        compiler_params=_cp("arbitrary"),
    )(cc, dm, w_ada)


def _c_ctx_grad(parts, c_ctx):
    def body(p_ref, c_ref, o_ref):
        ds = ((p_ref[0] + p_ref[1]) + p_ref[2]) + p_ref[3]
        _, vjp = jax.vjp(_silu, c_ref[...])
        o_ref[...] = vjp(ds)[0]

    return pl.pallas_call(body, name="c_ctx_grad", out_shape=jax.ShapeDtypeStruct(c_ctx.shape, F32))(parts, c_ctx)


def kernel(x, c, ctx, c_ctx, w_ada, b_ada, g_pre_mix, g_post_mix, g_pre_ffn, g_post_ffn, w_in, attn_sink, w_gate_fwd, b_gate_fwd, w_gate_bwd, b_gate_bwd, g_gla_norm, w_out, w_ffn_in, w_ffn_out, loss_target, m_c_ctx, m_w_ada, m_b_ada, m_g_pre_mix, m_g_post_mix, m_g_pre_ffn, m_g_post_ffn, m_w_in, m_attn_sink, m_w_gate_fwd, m_b_gate_fwd, m_w_gate_bwd, m_b_gate_bwd, m_g_gla_norm, m_w_out, m_w_ffn_in, m_w_ffn_out, v_c_ctx, v_w_ada, v_b_ada, v_g_pre_mix, v_g_post_mix, v_g_pre_ffn, v_g_post_ffn, v_w_in, v_attn_sink, v_w_gate_fwd, v_b_gate_fwd, v_w_gate_bwd, v_b_gate_bwd, v_g_gla_norm, v_w_out, v_w_ffn_in, v_w_ffn_out):
    xi, yi, ci = lax.axis_index("x"), lax.axis_index("y"), lax.axis_index("c")
    dev, chip = 4 * xi + 2 * yi + ci, 2 * xi + yi
    c_idx = jnp.reshape(ci, (1,)).astype(jnp.int32)
    d = x.shape[-1]
    n_ada, n_in, n_f = w_ada.shape[-1], w_in.shape[-1], w_ffn_in.shape[-1]
    r_out, r_f = w_out.shape[1], w_ffn_out.shape[1]
    n_gate = w_gate_fwd.shape[-1]
    by_chip = lambda t: t[0::2]

    w_in_g = _ag_shards("gather_w_in", w_in[0].astype(BF16))

    rc = -(-d // LANES)
    g1 = _ag_small("gather_cond", _pack_rows([c[0], w_gate_fwd[0], w_gate_bwd[0]]), w_in_g)
    c_all = g1[:, :rc].reshape(N_DEV, -1)[:, :d]
    gr = GATE_RANK * n_gate // LANES
    gate_full = lambda off: jnp.transpose(by_chip(g1)[:, off:off + gr].reshape(N_CHIP, GATE_RANK, n_gate),
                                          (1, 0, 2)).reshape(GATE_RANK, N_CHIP * n_gate)
    wgf, wgb = gate_full(rc), gate_full(rc + gr)
    cc = jnp.concatenate([c_all, c_ctx[None, :], jnp.zeros((7, d), F32)], axis=0)

    g2 = _ag_small("gather_ada", _ada_fwd(cc, w_ada[0]).reshape(-1, LANES))
    ada_all = jnp.transpose(by_chip(g2).reshape(N_CHIP, 16, n_ada), (1, 0, 2)).reshape(16, N_CHIP * n_ada) + b_ada
    late = _GatherBehind("gather_late", jnp.concatenate(
        [w_out[0], w_ffn_out[0], jnp.transpose(w_ffn_in[0])], axis=0).astype(BF16), chip, g2)

    def late_weights(after):
        t = late.result(after)
        r1, r2 = r_out, r_out + r_f
        return (t[:, :r1].reshape(N_CHIP * r_out, d), t[:, r2:].reshape(N_CHIP * n_f, d),
                t[:, r1:r2].reshape(N_CHIP * r_f, d))

    ada_all = _behind(ada_all, late.token)
    ada = lax.dynamic_slice(ada_all, (dev, 0), (1, N_CHIP * n_ada))
    ada_c = ada_all[N_DEV:N_DEV + 1]

    w = _prep_weights(jnp.concatenate([w_in_g[k] for k in range(N_CHIP)], axis=1), wgf, wgb)
    w.update(g_pre_mix=g_pre_mix, g_post_mix=g_post_mix, g_pre_ffn=g_pre_ffn, g_post_ffn=g_post_ffn,
             attn_sink=attn_sink, b_gate_fwd=b_gate_fwd, b_gate_bwd=b_gate_bwd, g_gla_norm=g_gla_norm)

    reduce_behind = _ReduceBehind("reduce_late", chip, ci, c_idx)
    reduce_w_in = _ReduceColsBehind("reduce_w_in", chip, ci, c_idx)
    loss_lanes, grad_x, g, d_ada, d_ada_c = _local_step(x[0], ctx[0], loss_target[0], ada, ada_c, w, late_weights,
                                                        reduce_behind, reduce_w_in)

    small = ("g_pre_mix", "g_post_mix", "g_pre_ffn", "g_post_ffn", "attn_sink", "b_gate_fwd", "b_gate_bwd",
             "g_gla_norm", "w_gate_fwd", "w_gate_bwd")
    shapes = [(1, 6 * d)] * 2 + [g[n].shape for n in small] + [(1, LANES)]
    g3 = _ag_small("gather_small_grads", _pack_rows([d_ada, d_ada_c] + [g[n] for n in small] + [loss_lanes]))
    tot = dict(zip(("d_ada", "d_ada_c") + small + ("loss",),
                   _unpack_rows(_sum_blocks("sum_small_grads", g3), shapes)))
    r_ada = 6 * d // LANES
    dm = jnp.concatenate([g3[:, :r_ada].reshape(N_DEV, 6 * d), tot["d_ada_c"], jnp.zeros((7, 6 * d), F32)], axis=0)
    grads = {n: tot[n] for n in small[:8]}
    grads["b_ada"] = _sum_blocks("sum_b_ada", dm.reshape(16, r_ada, LANES)).reshape(1, 6 * d)
    grads["w_gate_fwd"] = lax.dynamic_slice(tot["w_gate_fwd"], (0, chip * n_gate), (GATE_RANK, n_gate))[None]
    grads["w_gate_bwd"] = lax.dynamic_slice(tot["w_gate_bwd"], (0, chip * n_gate), (GATE_RANK, n_gate))[None]
    gw_ada, dsc = _ada_bwd(cc, lax.dynamic_slice(dm, (0, chip * n_ada), (16, n_ada)), w_ada[0])
    grads["w_ada"] = gw_ada[None]
    g4 = _ag_small("gather_c_ctx", _pack_rows([dsc[N_DEV]]))
    grads["c_ctx"] = _c_ctx_grad(by_chip(g4), _pack_rows([c_ctx])).reshape(-1)[:d]

    grads["w_in"] = reduce_w_in.result(g4)[None]
    behind = g["behind"]
    grads["w_ffn_in"] = jnp.transpose(behind[:n_f])[None]
    grads["w_ffn_out"], grads["w_out"] = behind[None, n_f:n_f + r_f], behind[None, n_f + r_f:]

    names = ("c_ctx", "w_ada", "b_ada", "g_pre_mix", "g_post_mix", "g_pre_ffn", "g_post_ffn", "w_in", "attn_sink",
             "w_gate_fwd", "b_gate_fwd", "w_gate_bwd", "b_gate_bwd", "g_gla_norm", "w_out", "w_ffn_in", "w_ffn_out")
    weights = dict(zip(names, (c_ctx, w_ada, b_ada, g_pre_mix, g_post_mix, g_pre_ffn, g_post_ffn, w_in, attn_sink,
                               w_gate_fwd, b_gate_fwd, w_gate_bwd, b_gate_bwd, g_gla_norm, w_out, w_ffn_in,
                               w_ffn_out)))
    m_in = dict(zip(names, (m_c_ctx, m_w_ada, m_b_ada, m_g_pre_mix, m_g_post_mix, m_g_pre_ffn, m_g_post_ffn, m_w_in,
                            m_attn_sink, m_w_gate_fwd, m_b_gate_fwd, m_w_gate_bwd, m_b_gate_bwd, m_g_gla_norm,
                            m_w_out, m_w_ffn_in, m_w_ffn_out)))
    v_in = dict(zip(names, (v_c_ctx, v_w_ada, v_b_ada, v_g_pre_mix, v_g_post_mix, v_g_pre_ffn, v_g_post_ffn, v_w_in,
                            v_attn_sink, v_w_gate_fwd, v_b_gate_fwd, v_w_gate_bwd, v_b_gate_bwd, v_g_gla_norm,
                            v_w_out, v_w_ffn_in, v_w_ffn_out)))
    large = ("w_ada", "w_in", "w_out", "w_ffn_in", "w_ffn_out")
    tiny = tuple(n for n in names if n not in large)
    delta, new_m, new_v = {}, {}, {}
    for n in large:
        dl, nm, nv = _adamw("adamw_" + n, weights[n][0], grads[n][0], m_in[n][0], v_in[n][0])
        delta[n], new_m[n], new_v[n] = dl[None], nm[None], nv[None]
    tiny_shapes = [weights[n].shape for n in tiny]
    packed = [_pack_rows([t[n] for n in tiny]) for t in (weights, grads, m_in, v_in)]
    for out, res in zip((delta, new_m, new_v), _adamw("adamw_small", *packed)):
        out.update(zip(tiny, _unpack_rows(res, tiny_shapes)))
    for n in tiny:
        grads[n] = grads[n].reshape(weights[n].shape)

    return (tot["loss"][0, 0], grad_x[None], *[grads[n] for n in names], *[delta[n] for n in names], *[new_m[n] for n in names],
            *[new_v[n] for n in names])
```

```python
import functools

import jax
import jax.numpy as jnp
import numpy as np
from jax import lax
from jax.experimental import pallas as pl
from jax.experimental.pallas import tpu as pltpu

F32 = jnp.float32
BF16 = jnp.bfloat16
MESH = pl.DeviceIdType.MESH

HEAD_DIM = 64
ATT_HEADS = 8
ATT_KV_HEADS = 2
ATT_GROUP = ATT_HEADS // ATT_KV_HEADS
WINDOW = 128
BLOCK = 128
GRID_W = 64
ROPE_BASE = 10000.0
GLA_HEADS = 8
GLA_DK = 32
GLA_DV = 64
GLA_CHUNK = 64
GATE_RANK = 16
GATE_TAU = 16.0
NEG_INF = -1e30
QW = ATT_HEADS * HEAD_DIM
KVW = ATT_KV_HEADS * HEAD_DIM
GKW = GLA_HEADS * GLA_DK
GVW = GLA_HEADS * GLA_DV
IN_COLS = QW + 2 * KVW + 2 * GKW + 2 * GVW + 2 * GATE_RANK
LANES = 128
IN_PAD = IN_COLS + LANES - 2 * GATE_RANK
C_Q, C_GV, C_GG = 0, QW, QW + GVW
C_K = C_GG + GVW
C_V = C_K + KVW
C_GQ = C_V + KVW
C_GK = C_GQ + GKW
C_Z = C_GK + GKW
MIX = QW + GVW

ADAM_LR, ADAM_B1, ADAM_B2, ADAM_EPS, ADAM_WD, ADAM_STEP = 0.001, 0.9, 0.999, 1e-08, 0.01, 10

VMEM_LIMIT = 56 * 1024 * 1024


def _cp(*sem):
    return pltpu.CompilerParams(dimension_semantics=sem, vmem_limit_bytes=VMEM_LIMIT)


def _pick(n, cands):
    for t in cands:
        if n % t == 0:
            return t
    return n


_DIMS = {"nn": (((1,), (0,)), ((), ())), "nt": (((1,), (1,)), ((), ())), "tn": (((0,), (0,)), ((), ()))}


def _raw_dot(mode, a, b, hi):
    dot = lambda u, v: lax.dot_general(u, v, _DIMS[mode], preferred_element_type=F32)
    if hi:
        a, b = a.astype(F32), b.astype(F32)
        a_hi, b_hi = a.astype(BF16), b.astype(BF16)
        a_lo, b_lo = (a - a_hi.astype(F32)).astype(BF16), (b - b_hi.astype(F32)).astype(BF16)
        return dot(a_hi, b_hi) + (dot(a_lo, b_hi) + dot(a_hi, b_lo))
    return dot(a.astype(BF16), b.astype(BF16))


def _make_dot(mode, hi):
    @jax.custom_vjp
    def dot(a, b):
        return _raw_dot(mode, a, b, hi)

    def fwd(a, b):
        return _raw_dot(mode, a, b, hi), (a, b)

    def bwd(res, dc):
        a, b = res
        if mode == "nn":
            return _raw_dot("nt", dc, b, hi), _raw_dot("tn", a, dc, hi)
        if mode == "nt":
            return _raw_dot("nn", dc, b, hi), _raw_dot("tn", dc, a, hi)
        return _raw_dot("nt", b, dc, hi), _raw_dot("nn", a, dc, hi)

    dot.defvjp(fwd, bwd)
    return dot


_nn, _nt, _tn = _make_dot("nn", False), _make_dot("nt", False), _make_dot("tn", False)
_nn_hi = _make_dot("nn", True)


MM_VMEM_BUDGET = 44 * 1024 * 1024


def _halvings(n):
    out = [n]
    while out[-1] % (2 * LANES) == 0:
        out.append(out[-1] // 2)
    return out


def _mm_tiles(mode, m, n, k, a_bytes, b_bytes, o_bytes, init_bytes=0):
    tms = [t for t in dict.fromkeys((m, m // 2, m // 4, 2048, 1024, 512, 256, 128))
           if m % t == 0 and t % (LANES if mode == "tn" else 16) == 0 and t <= 4096] or [m]
    if mode == "tn":
        fits = [(k // tk + 0.5 * (m // tm), tm, tk)
                for tk in (4096, 2048, 1024, 512, 256, 128) if k % tk == 0 for tm in tms
                if 2 * (tk * tm * a_bytes + tk * n * b_bytes + tm * n * (o_bytes + init_bytes)) <= MM_VMEM_BUDGET]
        if fits:
            _, tm, tk = min(fits)
            return tm, n, tk
    tks = ([t for t in (512, 256, 128) if k % t == 0] or [k]) if mode == "tn" else _halvings(k)
    for tn in _halvings(n):
        for tk in tks:
            for tm in tms:
                acc = tm * tn * 4 if (k // tk > 1 and o_bytes != 4) else 0
                tiles = tm * tk * a_bytes + tk * tn * b_bytes + tm * tn * (o_bytes + init_bytes)
                if 2 * tiles + acc <= MM_VMEM_BUDGET:
                    return tm, tn, tk
    return tms[-1], _halvings(n)[-1], tks[-1]


def _mm(name, a, b, mode, out_dtype=F32, init=None, after=None):
    follow = () if after is None else (after,)
    if mode == "nn":
        (m, k), n = a.shape, b.shape[1]
    elif mode == "nt":
        (m, k), n = a.shape, b.shape[0]
    else:
        (k, m), n = a.shape, b.shape[1]
    tm, tn, tk = _mm_tiles(mode, m, n, k, a.dtype.itemsize, b.dtype.itemsize, jnp.dtype(out_dtype).itemsize,
                           0 if init is None else 4)
    nk = k // tk
    use_acc = nk > 1 and out_dtype != F32

    inits = () if init is None else (init,)

    def body(a_ref, b_ref, *rest):
        rest = rest[:len(inits)] + rest[len(inits) + len(follow):]
        o_ref, acc = rest[len(inits)], rest[len(inits) + 1:]
        part = _raw_dot(mode, a_ref[...], b_ref[...], False)
        first = lambda: part + rest[0][...] if inits else part
        if nk == 1:
            o_ref[...] = first().astype(o_ref.dtype)
            return
        acc_ref = acc[0] if use_acc else o_ref
        kk = pl.program_id(2)

        @pl.when(kk == 0)
        def _():
            acc_ref[...] = first()

        @pl.when(kk > 0)
        def _():
            acc_ref[...] += part

        if use_acc:
            @pl.when(kk == nk - 1)
            def _():
                o_ref[...] = acc_ref[...].astype(o_ref.dtype)

    if mode == "nn":
        a_spec = pl.BlockSpec((tm, tk), lambda i, j, kk: (i, kk))
        b_spec = pl.BlockSpec((tk, tn), lambda i, j, kk: (kk, j))
    elif mode == "nt":
        a_spec = pl.BlockSpec((tm, tk), lambda i, j, kk: (i, kk))
        b_spec = pl.BlockSpec((tn, tk), lambda i, j, kk: (j, kk))
    else:
        a_spec = pl.BlockSpec((tk, tm), lambda i, j, kk: (kk, i))
        b_spec = pl.BlockSpec((tk, tn), lambda i, j, kk: (kk, j))
    return pl.pallas_call(
        body, name=name, grid=(m // tm, n // tn, nk),
        in_specs=[a_spec, b_spec] + [pl.BlockSpec((tm, tn), lambda i, j, kk: (i, j))] * len(inits)
        + [pl.BlockSpec(memory_space=pl.ANY)] * len(follow),
        out_specs=pl.BlockSpec((tm, tn), lambda i, j, kk: (i, j)),
        out_shape=jax.ShapeDtypeStruct((m, n), out_dtype),
        scratch_shapes=[pltpu.VMEM((tm, tn), F32)] if use_acc else [],
        compiler_params=_cp("parallel", "parallel", "arbitrary"),
    )(a, b, *inits, *follow)


def _rowwise(name, fn, rows, row_ins, full_ins, row_outs, acc_outs, tm=None):
    tm = tm or _pick(rows, (512, 256, 128))
    n_r, n_f, n_o, n_a = len(row_ins), len(full_ins), len(row_outs), len(acc_outs)

    def body(*refs):
        ins, outs = refs[:n_r + n_f], refs[n_r + n_f:]
        vals = [r[...].astype(F32) for r in ins]
        ro, ao = fn(*vals)
        for r, val in zip(outs[:n_o], ro):
            r[...] = val.astype(r.dtype)
        if n_a:
            @pl.when(pl.program_id(0) == 0)
            def _():
                for r in outs[n_o:]:
                    r[...] = jnp.zeros_like(r)

            for r, val in zip(outs[n_o:], ao):
                r[...] += val

    in_specs = [pl.BlockSpec((tm, w), functools.partial(lambda i, cb: (i, cb), cb=cb)) for _, w, cb in row_ins]
    in_specs += [pl.BlockSpec(a.shape, lambda i: (0, 0)) for a in full_ins]
    out_specs = [pl.BlockSpec((tm, w), lambda i: (i, 0)) for w, _ in row_outs]
    out_specs += [pl.BlockSpec(s, lambda i: (0, 0)) for s in acc_outs]
    out_shape = [jax.ShapeDtypeStruct((rows, w), dt) for w, dt in row_outs]
    out_shape += [jax.ShapeDtypeStruct(s, F32) for s in acc_outs]
    return pl.pallas_call(
        body, name=name, grid=(rows // tm,), in_specs=in_specs, out_specs=out_specs, out_shape=out_shape,
        compiler_params=_cp("arbitrary" if n_a else "parallel"),
    )(*[a for a, _, _ in row_ins], *full_ins)


def _rn(x):
    return x * lax.rsqrt(jnp.mean(x * x, axis=-1, keepdims=True) + 1e-6)


def _sigmoid(t):
    return 1.0 / (1.0 + jnp.exp(-t))


def _f_norm_mod(x, g, sh, sc):
    return _rn(x) * g * (1.0 + sc) + sh


def _f_post_res(xr, y, g, gate):
    return xr + gate * (_rn(y) * g)


def _f_swiglu(g, u):
    return g * _sigmoid(g) * u


def _logsig(u):
    return jnp.minimum(u, 0.0) - jnp.log(1.0 + jnp.exp(-jnp.abs(u)))


def _f_gate(z, wf, wb, bf, bb):
    return _logsig(_nn(z, wf) + bf) / GATE_TAU, _logsig(_nn(z, wb) + bb) / GATE_TAU


def _f_gla_out(of, ob, gg, gt, bd):
    o = of + ob
    ms = _nn_hi(o * o, bd)
    return o * lax.rsqrt(ms + 1e-6) * gt * (gg * _sigmoid(gg))


def _norm_mod(name, x, g, sh, sc):
    rows, d = x.shape
    return _rowwise(name, lambda x, g, sh, sc: ((_f_norm_mod(x, g, sh, sc),), ()), rows,
                    [(x, d, 0)], [g, sh, sc], [(d, BF16)], [])[0]


def _norm_mod_bwd(name, dh, dres, x, g, sh, sc):
    rows, d = x.shape

    def fn(dh, dres, x, g, sh, sc):
        _, vjp = jax.vjp(_f_norm_mod, x, g, sh, sc)
        dx, dg, dsh, dsc = vjp(dh)
        return (dx + dres,), (dg, dsh, dsc)

    return _rowwise(name, fn, rows, [(dh, d, 0), (dres, d, 0), (x, d, 0)], [g, sh, sc], [(d, F32)],
                    [(1, d)] * 3)


def _post_res(name, xr, y, g, gate):
    rows, d = xr.shape
    return _rowwise(name, lambda xr, y, g, gate: ((_f_post_res(xr, y, g, gate),), ()), rows,
                    [(xr, d, 0), (y, d, 0)], [g, gate], [(d, F32)], [])[0]


def _post_res_bwd(name, dxo, y, g, gate):
    rows, d = y.shape

    def fn(dxo, y, g, gate):
        _, vjp = jax.vjp(lambda y, g, gate: _f_post_res(jnp.zeros_like(y), y, g, gate), y, g, gate)
        dy, dg, dgate = vjp(dxo)
        return (dy,), (dg, dgate)

    return _rowwise(name, fn, rows, [(dxo, d, 0), (y, d, 0)], [g, gate], [(d, BF16)], [(1, d)] * 2)


def _post_res_norm_mod(name, xr, y, g_post, gate, g_pre, sh, sc):
    rows, d = xr.shape

    def fn(xr, y, g_post, gate, g_pre, sh, sc):
        x1 = _f_post_res(xr, y, g_post, gate)
        return (x1, _f_norm_mod(x1, g_pre, sh, sc)), ()

    return _rowwise(name, fn, rows, [(xr, d, 0), (y, d, 0)], [g_post, gate, g_pre, sh, sc], [(d, F32), (d, BF16)], [])


def _norm_mod_post_res_bwd(name, dh, dres, x1, y, g_pre, sh, sc, g_post, gate):
    rows, d = x1.shape

    def fn(dh, dres, x1, y, g_pre, sh, sc, g_post, gate):
        _, vjp_norm = jax.vjp(_f_norm_mod, x1, g_pre, sh, sc)
        dx1, dg_pre, dsh, dsc = vjp_norm(dh)
        dx1 = dx1 + dres
        _, vjp_res = jax.vjp(lambda y, g, gate: _f_post_res(jnp.zeros_like(y), y, g, gate), y, g_post, gate)
        dy, dg_post, dgate = vjp_res(dx1)
        return (dx1, dy), (dg_pre, dsh, dsc, dg_post, dgate)

    return _rowwise(name, fn, rows, [(dh, d, 0), (dres, d, 0), (x1, d, 0), (y, d, 0)], [g_pre, sh, sc, g_post, gate],
                    [(d, F32), (d, BF16)], [(1, d)] * 5, tm=_pick(rows, (256, 128)))


def _post_res_loss(name, xr, y, g, gate, target):
    rows, d = xr.shape

    def fn(xr, y, target, g, gate):
        x2, vjp = jax.vjp(lambda y, g, gate: _f_post_res(xr, y, g, gate), y, g, gate)
        diff = x2 - target
        part = 0.5 * jnp.sum(jnp.mean(diff * diff, axis=-1, keepdims=True), axis=0, keepdims=True)
        dx2 = diff * (1.0 / d)
        dy, dg, dgate = vjp(dx2)
        return (dx2, dy), (jnp.broadcast_to(part, (1, LANES)), dg, dgate)

    return _rowwise(name, fn, rows, [(xr, d, 0), (y, d, 0), (target, d, 0)], [g, gate], [(d, F32), (d, BF16)],
                    [(1, LANES), (1, d), (1, d)])


def _mm_rows(name, a, b, mode, fn, extras, outs):
    m, k = a.shape
    tm = _pick(m, (256, 128))

    def body(a_ref, b_ref, *rest):
        tiles = fn(_raw_dot(mode, a_ref[...], b_ref[...], False), *[e[...] for e in rest[:len(extras)]])
        for r, val in zip(rest[len(extras):], tiles):
            r[...] = val.astype(r.dtype)

    row = lambda w: pl.BlockSpec((tm, w), lambda i: (i, 0))
    return pl.pallas_call(
        body, name=name, grid=(m // tm,),
        in_specs=[row(k), pl.BlockSpec(b.shape, lambda i: (0, 0))] + [row(e.shape[1]) for e in extras],
        out_specs=[row(w) for w, _ in outs], out_shape=[jax.ShapeDtypeStruct((m, w), dt) for w, dt in outs],
        compiler_params=_cp("parallel"),
    )(a, b, *extras)


def _ffn_in_swiglu(name, h, w_t):
    f = w_t.shape[0] // 2
    fn = lambda u: (u, _f_swiglu(u[:, :f], u[:, f:]))
    return _mm_rows(name, h, w_t, "nt", fn, [], [(2 * f, BF16), (f, BF16)])


def _ffn_out_dx_swiglu_bwd(name, df, w_out, u):
    f = w_out.shape[0]

    def fn(da, u):
        u = u.astype(F32)
        _, vjp = jax.vjp(_f_swiglu, u[:, :f], u[:, f:])
        return (jnp.concatenate(vjp(da), axis=1),)

    return _mm_rows(name, df, w_out, "nt", fn, [u], [(2 * f, BF16)])[0]


def _gate_fwd(name, p, wf, wb, bf, bb):
    rows = p.shape[0]
    return _rowwise(name, lambda z, wf, wb, bf, bb: (_f_gate(z, wf, wb, bf, bb), ()), rows,
                    [(p, LANES, C_Z // LANES)], [wf, wb, bf, bb], [(GKW, F32)] * 2, [])


def _gate_bwd(name, p, dla_f, dla_b, wf, wb, bf, bb):
    rows = p.shape[0]

    def fn(z, dlf, dlb, wf, wb, bf, bb):
        _, vjp = jax.vjp(_f_gate, z, wf, wb, bf, bb)
        dz, dwf, dwb, dbf, dbb = vjp((dlf, dlb))
        return (dz,), (dwf, dwb, dbf, dbb)

    return _rowwise(name, fn, rows, [(p, LANES, C_Z // LANES), (dla_f, GKW, 0), (dla_b, GKW, 0)],
                    [wf, wb, bf, bb], [(LANES, BF16)], [(LANES, GKW), (LANES, GKW), (1, GKW), (1, GKW)])


def _head_mean_matrix():
    h = np.arange(GVW) // GLA_DV
    return jnp.asarray((h[:, None] == h[None, :]).astype(np.float32) / GLA_DV)


def _gla_out(name, attn, of, ob, p, gt):
    rows = of.shape[0]
    bd = _head_mean_matrix()
    fn = lambda attn, of, ob, gg, gt, bd: ((jnp.concatenate([attn, _f_gla_out(of, ob, gg, gt, bd)], axis=1),), ())
    return _rowwise(name, fn, rows, [(attn, QW, 0), (of, GVW, 0), (ob, GVW, 0), (p, GVW, C_GG // GVW)], [gt, bd],
                    [(MIX, BF16)], [])[0]


def _gla_out_bwd(name, dmix, of, ob, p, gt):
    rows = of.shape[0]
    bd = _head_mean_matrix()

    def fn(dm, of, ob, gg, gt, bd):
        _, vjp = jax.vjp(lambda of, gg, gt: _f_gla_out(of, ob, gg, gt, bd), of, gg, gt)
        do, dgg, dgt = vjp(dm)
        return (do, dgg), (dgt,)

    return _rowwise(name, fn, rows, [(dmix, GVW, 1), (of, GVW, 0), (ob, GVW, 0), (p, GVW, C_GG // GVW)], [gt, bd],
                    [(GVW, F32), (GVW, BF16)], [(1, GVW)])


def _rope_tables(n_tokens):
    t = jnp.arange(n_tokens)
    row = (t // GRID_W).astype(F32)
    col = (t % GRID_W).astype(F32)
    half = HEAD_DIM // 2
    inv_freq = ROPE_BASE ** (-jnp.arange(0, half, 2, dtype=F32) / half)
    ang_r = row[:, None] * inv_freq[None, :]
    ang_c = col[:, None] * inv_freq[None, :]
    ang = jnp.concatenate([ang_r, ang_r, ang_c, ang_c], axis=-1)
    sign = jnp.concatenate([-jnp.ones((16,), F32), jnp.ones((16,), F32)] * 2)
    cos, sin = jnp.cos(ang), jnp.sin(ang) * sign[None, :]
    return jnp.tile(cos, (1, 2)), jnp.tile(sin, (1, 2))


def _rot_pairs(x):
    w = x.shape[-1]
    lane = lax.broadcasted_iota(jnp.int32, x.shape, x.ndim - 1)
    return jnp.where((lane % 32) < 16, pltpu.roll(x, w - 16, x.ndim - 1), pltpu.roll(x, 16, x.ndim - 1))


def _rope_apply(x, cos, sin_signed, inverse):
    reps = x.shape[-1] // LANES
    cos = jnp.concatenate([cos] * reps, axis=-1) if reps > 1 else cos
    sin = jnp.concatenate([sin_signed] * reps, axis=-1) if reps > 1 else sin_signed
    if inverse:
        return x * cos + _rot_pairs(x * sin)
    return x * cos + _rot_pairs(x) * sin


def _rope_fwd(name, p, cos, sin):
    rows = p.shape[0]

    def fn(q, k, v, cos, sin):
        return (_rope_apply(q, cos, sin, False), _rope_apply(k, cos, sin, False), v), ()

    return _rowwise(name, fn, rows, [(p, QW, 0), (p, KVW, C_K // KVW), (p, KVW, C_V // KVW), (cos, LANES, 0),
                                     (sin, LANES, 0)], [], [(QW, BF16), (KVW, BF16), (KVW, BF16)], [])


def _proj_grad(name, dq_rot, dk_rot, dv, cos, sin, gla_f, gla_b, dgg, dz):
    rows = dq_rot.shape[0]

    def fn(dq, dk, dv, cos, sin, gqf, gkf, gvf, gqb, gkb, gvb, dgg, dz):
        parts = [_rope_apply(dq, cos, sin, True), gvf + gvb, dgg, _rope_apply(dk, cos, sin, True), dv, gqf + gqb,
                 gkf + gkb, dz]
        return (jnp.concatenate(parts, axis=1),), ()

    ins = [(dq_rot, QW), (dk_rot, KVW), (dv, KVW), (cos, LANES), (sin, LANES)]
    ins += [(t, t.shape[1]) for t in (*gla_f, *gla_b)] + [(dgg, GVW), (dz, LANES)]
    return _rowwise(name, fn, rows, [(t, w, 0) for t, w in ins], [], [(IN_PAD, BF16)], [],
                    tm=_pick(rows, (256, 128)))[0]


GROUP_ROWS = ATT_GROUP * BLOCK


def _f_attn(qs, kws, vws, kcs, vcs, sink, n, n_tokens):
    row = lax.broadcasted_iota(jnp.int32, (GROUP_ROWS, 1), 0)
    group = sum((row >= g * BLOCK).astype(jnp.int32) for g in range(1, ATT_GROUP))
    i = lax.broadcasted_iota(jnp.int32, (GROUP_ROWS, 3 * BLOCK), 0) - BLOCK * group
    j = lax.broadcasted_iota(jnp.int32, (GROUP_ROWS, 3 * BLOCK), 1)
    kpos = (n - 1) * BLOCK + j
    mask = (jnp.abs(j - BLOCK - i) <= WINDOW) & (kpos >= 0) & (kpos < n_tokens)
    head_id = lax.broadcasted_iota(jnp.int32, (1, ATT_HEADS), 1)
    scale = HEAD_DIM ** -0.5
    outs = []
    for h in range(ATT_KV_HEADS):
        sk = jnp.zeros((GROUP_ROWS, 1), F32)
        for g in range(ATT_GROUP):
            one = jnp.sum(jnp.where(head_id == h * ATT_GROUP + g, sink, 0.0), axis=-1, keepdims=True)
            sk = jnp.where(group == g, one, sk)
        q = qs[h] * scale
        s_w = jnp.where(mask, _nt(q, kws[h]), NEG_INF)
        s_c = _nt(q, kcs[h])
        m = lax.stop_gradient(jnp.maximum(jnp.maximum(jnp.max(s_w, axis=-1, keepdims=True),
                                                      jnp.max(s_c, axis=-1, keepdims=True)), sk))
        pw, pc = jnp.exp(s_w - m), jnp.exp(s_c - m)
        den = jnp.sum(pw, axis=-1, keepdims=True) + jnp.sum(pc, axis=-1, keepdims=True) + jnp.exp(sk - m)
        outs.append((_nn(pw, vws[h]) + _nn(pc, vcs[h])) / den)
    return tuple(outs)


def _group_rows(ref, h):
    hs = lambda hq: slice(hq * HEAD_DIM, (hq + 1) * HEAD_DIM)
    return jnp.concatenate([ref[:, hs(h * ATT_GROUP + g)].astype(F32) for g in range(ATT_GROUP)], axis=0)


def _ungroup_rows(ref, h, val):
    for g in range(ATT_GROUP):
        hq = h * ATT_GROUP + g
        ref[:, hq * HEAD_DIM:(hq + 1) * HEAD_DIM] = val[g * BLOCK:(g + 1) * BLOCK].astype(ref.dtype)


def _attn_loads(n, q_ref, kp_ref, vp_ref, kc_ref, vc_ref):
    r0 = pl.multiple_of(n * BLOCK, BLOCK)
    hs = lambda h: slice(h * HEAD_DIM, (h + 1) * HEAD_DIM)
    qs = [_group_rows(q_ref, h) for h in range(ATT_KV_HEADS)]
    kws = [kp_ref[pl.ds(r0, 3 * BLOCK), hs(h)].astype(F32) for h in range(ATT_KV_HEADS)]
    vws = [vp_ref[pl.ds(r0, 3 * BLOCK), hs(h)].astype(F32) for h in range(ATT_KV_HEADS)]
    kcs = [kc_ref[:, hs(h)].astype(F32) for h in range(ATT_KV_HEADS)]
    vcs = [vc_ref[:, hs(h)].astype(F32) for h in range(ATT_KV_HEADS)]
    return r0, hs, qs, kws, vws, kcs, vcs


def _attn_specs(s, c):
    full = lambda shape: pl.BlockSpec(shape, lambda n: (0, 0))
    return [pl.BlockSpec((BLOCK, QW), lambda n: (n, 0)), full((s + 2 * BLOCK, KVW)), full((s + 2 * BLOCK, KVW)),
            full((c, KVW)), full((c, KVW)), full((1, ATT_HEADS))]


def _attn_fwd(q, kp, vp, kc, vc, sink):
    s, c = q.shape[0], kc.shape[0]

    def body(q_ref, kp_ref, vp_ref, kc_ref, vc_ref, sink_ref, o_ref):
        n = pl.program_id(0)
        _, hs, qs, kws, vws, kcs, vcs = _attn_loads(n, q_ref, kp_ref, vp_ref, kc_ref, vc_ref)
        outs = _f_attn(qs, kws, vws, kcs, vcs, sink_ref[...], n, s)
        for h in range(ATT_KV_HEADS):
            _ungroup_rows(o_ref, h, outs[h])

    return pl.pallas_call(
        body, name="attn_fwd", grid=(s // BLOCK,), in_specs=_attn_specs(s, c),
        out_specs=pl.BlockSpec((BLOCK, QW), lambda n: (n, 0)), out_shape=jax.ShapeDtypeStruct((s, QW), BF16),
        compiler_params=_cp("parallel"),
    )(q, kp, vp, kc, vc, sink)


def _attn_bwd(do, q, kp, vp, kc, vc, sink):
    s, c = q.shape[0], kc.shape[0]

    def body(do_ref, q_ref, kp_ref, vp_ref, kc_ref, vc_ref, sink_ref, dq_ref, dkp_ref, dvp_ref, dkc_ref, dvc_ref,
             dsink_ref):
        n = pl.program_id(0)

        @pl.when(n == 0)
        def _():
            for r in (dkp_ref, dvp_ref, dkc_ref, dvc_ref, dsink_ref):
                r[...] = jnp.zeros_like(r)

        r0, hs, qs, kws, vws, kcs, vcs = _attn_loads(n, q_ref, kp_ref, vp_ref, kc_ref, vc_ref)
        _, vjp = jax.vjp(lambda qs, kws, vws, kcs, vcs, sink: _f_attn(qs, kws, vws, kcs, vcs, sink, n, s),
                         qs, kws, vws, kcs, vcs, sink_ref[...])
        dqs, dkws, dvws, dkcs, dvcs, dsink = vjp(tuple(_group_rows(do_ref, h) for h in range(ATT_KV_HEADS)))
        for h in range(ATT_KV_HEADS):
            _ungroup_rows(dq_ref, h, dqs[h])
            dkp_ref[pl.ds(r0, 3 * BLOCK), hs(h)] += dkws[h]
            dvp_ref[pl.ds(r0, 3 * BLOCK), hs(h)] += dvws[h]
            dkc_ref[:, hs(h)] += dkcs[h]
            dvc_ref[:, hs(h)] += dvcs[h]
        dsink_ref[...] += dsink

    full = lambda shape: pl.BlockSpec(shape, lambda n: (0, 0))
    return pl.pallas_call(
        body, name="attn_bwd", grid=(s // BLOCK,),
        in_specs=[pl.BlockSpec((BLOCK, QW), lambda n: (n, 0))] + _attn_specs(s, c),
        out_specs=[pl.BlockSpec((BLOCK, QW), lambda n: (n, 0)), full((s + 2 * BLOCK, KVW)), full((s + 2 * BLOCK, KVW)),
                   full((c, KVW)), full((c, KVW)), full((1, ATT_HEADS))],
        out_shape=[jax.ShapeDtypeStruct((s, QW), F32), jax.ShapeDtypeStruct((s + 2 * BLOCK, KVW), F32),
                   jax.ShapeDtypeStruct((s + 2 * BLOCK, KVW), F32), jax.ShapeDtypeStruct((c, KVW), F32),
                   jax.ShapeDtypeStruct((c, KVW), F32), jax.ShapeDtypeStruct((1, ATT_HEADS), F32)],
        compiler_params=_cp("arbitrary"),
    )(do, q, kp, vp, kc, vc, sink)


GLA_GROUPS = 1
GLA_GROUP_HEADS = GLA_HEADS // GLA_GROUPS
GKG, GVG = GKW // GLA_GROUPS, GVW // GLA_GROUPS


def _gla_masks(heads=GLA_HEADS):
    hk = np.arange(heads * GLA_DK) // GLA_DK
    hv = np.arange(heads * GLA_DV) // GLA_DV
    head_k = (np.arange(heads)[:, None] == hk[None, :]).astype(np.float32)
    head_v = (np.arange(heads)[:, None] == hv[None, :]).astype(np.float32)
    bd_t = (hv[:, None] == hk[None, :]).astype(np.float32)
    return jnp.asarray(head_k), jnp.asarray(head_v), jnp.asarray(bd_t)


def _group_states(st):
    return jnp.stack([st[g * GVG:(g + 1) * GVG, g * GKG:(g + 1) * GKG] for g in range(GLA_GROUPS)])


def _ungroup_states(st):
    out = jnp.zeros((GVW, GKW), st.dtype)
    for g in range(GLA_GROUPS):
        out = out.at[g * GVG:(g + 1) * GVG, g * GKG:(g + 1) * GKG].set(st[g])
    return out


def _tri(n, rev, strict=False):
    i = lax.broadcasted_iota(jnp.int32, (n, n), 0)
    j = lax.broadcasted_iota(jnp.int32, (n, n), 1)
    if strict:
        keep = (j > i) if rev else (j < i)
    else:
        keep = (j >= i) if rev else (j <= i)
    return keep


def _f_gla_chunk(q, k, v, la, st, head_k, head_v, bd_t, rev):
    heads, kw, vw = head_k.shape[0], q.shape[1], v.shape[1]
    keep = _tri(GLA_CHUNK, rev)
    b = _nn_hi(keep.astype(F32), la)
    bl = jnp.sum(la, axis=0, keepdims=True)
    qd = q * (GLA_DK ** -0.5) * jnp.exp(b)
    ki = k * jnp.exp(-b)
    kd = k * jnp.exp(bl - b)
    q_heads = (qd[None, :, :] * head_k[:, None, :]).reshape(heads * GLA_CHUNK, kw)
    a_all = _nt(q_heads, ki).reshape(heads, GLA_CHUNK, GLA_CHUNK)
    a_all = jnp.where(keep[None, :, :], a_all, 0.0).reshape(heads * GLA_CHUNK, GLA_CHUNK)
    o_all = _nn(a_all, v).reshape(heads, GLA_CHUNK, vw)
    intra = jnp.sum(o_all * head_v[:, None, :], axis=0)
    inter = _nt(qd, st)
    st_new = st * jnp.exp(bl) + bd_t * _tn(v, kd)
    return intra + inter, st_new


def _gla_specs(s, tb, order):
    return [pl.BlockSpec((tb, GKW), lambda i: (order(i), C_GQ // GKW)),
            pl.BlockSpec((tb, GKW), lambda i: (order(i), C_GK // GKW)),
            pl.BlockSpec((tb, GVW), lambda i: (order(i), C_GV // GVW)),
            pl.BlockSpec((tb, GKW), lambda i: (order(i), 0))]


GLA_BLOCK_CHUNKS = 4


def _gla_fwd(p, la_f, la_b, st_f0, st_b0):
    s = p.shape[0]
    tb = GLA_BLOCK_CHUNKS * GLA_CHUNK
    nblk = s // tb
    up, down = (lambda i: i), (lambda i: nblk - 1 - i)
    masks = _gla_masks(GLA_GROUP_HEADS)

    def scan(rev, q_ref, k_ref, v_ref, la_ref, o_ref, sts_ref, st_ref, consts):
        for g in range(GLA_GROUPS):
            gk, gv = slice(g * GKG, (g + 1) * GKG), slice(g * GVG, (g + 1) * GVG)
            st = st_ref[g]
            sts_ref[0, g] = st
            chunks = range(GLA_BLOCK_CHUNKS)
            for ci in (reversed(chunks) if rev else chunks):
                rows = slice(ci * GLA_CHUNK, (ci + 1) * GLA_CHUNK)
                o, st = _f_gla_chunk(q_ref[rows, gk], k_ref[rows, gk], v_ref[rows, gv], la_ref[rows, gk], st, *consts,
                                     rev)
                o_ref[rows, gv] = o
            st_ref[g] = st

    def body(qf, kf, vf, laf, qb, kb, vb, lab, stf0, stb0, hk_ref, hv_ref, bd_ref, of_ref, stsf_ref, ob_ref, stsb_ref,
             stf_ref, stb_ref):
        @pl.when(pl.program_id(0) == 0)
        def _():
            stf_ref[...] = stf0[...]
            stb_ref[...] = stb0[...]

        consts = (hk_ref[...], hv_ref[...], bd_ref[...])
        scan(False, qf, kf, vf, laf, of_ref, stsf_ref, stf_ref, consts)
        scan(True, qb, kb, vb, lab, ob_ref, stsb_ref, stb_ref, consts)

    full = lambda a: pl.BlockSpec(a.shape, lambda i: (0,) * a.ndim)
    outs = lambda order: [pl.BlockSpec((tb, GVW), lambda i: (order(i), 0)),
                          pl.BlockSpec((1, GLA_GROUPS, GVG, GKG), lambda i: (order(i), 0, 0, 0))]
    return pl.pallas_call(
        body, name="gla_fwd", grid=(nblk,),
        in_specs=_gla_specs(s, tb, up) + _gla_specs(s, tb, down) + [full(st_f0), full(st_b0)]
        + [full(m) for m in masks],
        out_specs=outs(up) + outs(down),
        out_shape=[jax.ShapeDtypeStruct((s, GVW), F32), jax.ShapeDtypeStruct((nblk, GLA_GROUPS, GVG, GKG), F32)] * 2,
        scratch_shapes=[pltpu.VMEM((GLA_GROUPS, GVG, GKG), F32)] * 2,
        compiler_params=_cp("arbitrary"),
    )(p, p, p, la_f, p, p, p, la_b, st_f0, st_b0, *masks)


def _gla_bwd(p, la_f, la_b, sts_f, sts_b, do, after=None):
    s = p.shape[0]
    tb = GLA_BLOCK_CHUNKS * GLA_CHUNK
    nblk = s // tb
    up, down = (lambda i: i), (lambda i: nblk - 1 - i)
    masks = _gla_masks(GLA_GROUP_HEADS)
    follow = () if after is None else (after,)

    def back(rev, q_ref, k_ref, v_ref, la_ref, sts_ref, do_ref, dq_ref, dk_ref, dv_ref, dla_ref, dst0_ref, dst_ref,
             consts):
        def block(q, k, v, la, st):
            outs = [None] * GLA_BLOCK_CHUNKS
            chunks = range(GLA_BLOCK_CHUNKS)
            for ci in (reversed(chunks) if rev else chunks):
                outs[ci], st = _f_gla_chunk(q[ci], k[ci], v[ci], la[ci], st, *consts, rev)
            return tuple(outs), st

        for g in range(GLA_GROUPS):
            gk, gv = slice(g * GKG, (g + 1) * GKG), slice(g * GVG, (g + 1) * GVG)
            split = lambda r, cols: tuple(r[ci * GLA_CHUNK:(ci + 1) * GLA_CHUNK, cols].astype(F32)
                                          for ci in range(GLA_BLOCK_CHUNKS))
            _, vjp = jax.vjp(block, split(q_ref, gk), split(k_ref, gk), split(v_ref, gv), split(la_ref, gk),
                             sts_ref[0, g])
            dq, dk, dv, dla, dst = vjp((split(do_ref, gv), dst_ref[g]))
            for ci in range(GLA_BLOCK_CHUNKS):
                rows = slice(ci * GLA_CHUNK, (ci + 1) * GLA_CHUNK)
                dq_ref[rows, gk], dk_ref[rows, gk], dv_ref[rows, gv], dla_ref[rows, gk] = dq[ci], dk[ci], dv[ci], dla[ci]
            dst_ref[g] = dst
            dst0_ref[g] = dst

    def body(*refs):
        ins, (hk_ref, hv_ref, bd_ref) = refs[:12], refs[12:15]
        outs = refs[15 + len(follow):]

        @pl.when(pl.program_id(0) == 0)
        def _():
            outs[10][...] = jnp.zeros_like(outs[10])
            outs[11][...] = jnp.zeros_like(outs[11])

        consts = (hk_ref[...], hv_ref[...], bd_ref[...])
        back(False, *ins[:6], *outs[:5], outs[10], consts)
        back(True, *ins[6:], *outs[5:10], outs[11], consts)

    full = lambda a: pl.BlockSpec(a.shape, lambda i: (0,) * a.ndim)

    def ins(order):
        return _gla_specs(s, tb, order) + [pl.BlockSpec((1, GLA_GROUPS, GVG, GKG), lambda i: (order(i), 0, 0, 0)),
                                           pl.BlockSpec((tb, GVW), lambda i: (order(i), 0))]

    def outs(order):
        blk = lambda w: pl.BlockSpec((tb, w), lambda i: (order(i), 0))
        return [blk(GKW), blk(GKW), blk(GVW), blk(GKW), pl.BlockSpec((GLA_GROUPS, GVG, GKG), lambda i: (0, 0, 0))]

    shapes = [jax.ShapeDtypeStruct((s, GKW), F32), jax.ShapeDtypeStruct((s, GKW), F32),
              jax.ShapeDtypeStruct((s, GVW), F32), jax.ShapeDtypeStruct((s, GKW), F32),
              jax.ShapeDtypeStruct((GLA_GROUPS, GVG, GKG), F32)]
    both = pl.pallas_call(
        body, name="gla_bwd", grid=(nblk,),
        in_specs=ins(down) + ins(up) + [full(m) for m in masks] + [pl.BlockSpec(memory_space=pl.ANY)] * len(follow),
        out_specs=outs(down) + outs(up), out_shape=shapes * 2,
        scratch_shapes=[pltpu.VMEM((GLA_GROUPS, GVG, GKG), F32)] * 2,
        compiler_params=_cp("arbitrary"),
    )(p, p, p, la_f, sts_f, do, p, p, p, la_b, sts_b, do, *masks, *follow)
    return both[:5], both[5:]


def _f_ctx_state(k, v, la_f, la_b, bd_t):
    c = k.shape[0]
    after = _nn_hi(_tri(c, True, strict=True).astype(F32), la_f)
    before = _nn_hi(_tri(c, False, strict=True).astype(F32), la_b)
    return bd_t * _tn(v, k * jnp.exp(after)), bd_t * _tn(v, k * jnp.exp(before))


def _ctx_state(pc, la_f, la_b):
    c = pc.shape[0]
    bd_t = _gla_masks()[2]

    def body(k_ref, v_ref, lf_ref, lb_ref, bd_ref, sf_ref, sb_ref):
        sf_ref[...], sb_ref[...] = _f_ctx_state(k_ref[...], v_ref[...], lf_ref[...], lb_ref[...], bd_ref[...])

    full = lambda a: pl.BlockSpec(a.shape, lambda i: (0, 0))
    return pl.pallas_call(
        body, name="ctx_state_fwd", grid=(1,),
        in_specs=[pl.BlockSpec((c, GKW), lambda i: (0, C_GK // GKW)), pl.BlockSpec((c, GVW), lambda i: (0, C_GV // GVW)),
                  full(la_f), full(la_b), full(bd_t)],
        out_specs=[pl.BlockSpec((GVW, GKW), lambda i: (0, 0))] * 2,
        out_shape=[jax.ShapeDtypeStruct((GVW, GKW), F32)] * 2,
        compiler_params=_cp("arbitrary"),
    )(pc, pc, la_f, la_b, bd_t)


def _ctx_state_bwd(pc, la_f, la_b, dsf, dsb):
    c = pc.shape[0]
    bd_t = _gla_masks()[2]

    def body(k_ref, v_ref, lf_ref, lb_ref, bd_ref, dsf_ref, dsb_ref, dk_ref, dv_ref, dlf_ref, dlb_ref):
        _, vjp = jax.vjp(lambda k, v, lf, lb: _f_ctx_state(k, v, lf, lb, bd_ref[...]),
                         k_ref[...], v_ref[...], lf_ref[...], lb_ref[...])
        dk, dv, dlf, dlb = vjp((dsf_ref[...], dsb_ref[...]))
        dk_ref[...], dv_ref[...] = dk.astype(BF16), dv.astype(BF16)
        dlf_ref[...], dlb_ref[...] = dlf, dlb

    full = lambda a: pl.BlockSpec(a.shape, lambda i: (0, 0))
    return pl.pallas_call(
        body, name="ctx_state_bwd", grid=(1,),
        in_specs=[pl.BlockSpec((c, GKW), lambda i: (0, C_GK // GKW)), pl.BlockSpec((c, GVW), lambda i: (0, C_GV // GVW)),
                  full(la_f), full(la_b), full(bd_t), full(dsf), full(dsb)],
        out_specs=[pl.BlockSpec((c, GKW), lambda i: (0, 0)), pl.BlockSpec((c, GVW), lambda i: (0, 0)),
                   pl.BlockSpec((c, GKW), lambda i: (0, 0)), pl.BlockSpec((c, GKW), lambda i: (0, 0))],
        out_shape=[jax.ShapeDtypeStruct((c, GKW), BF16), jax.ShapeDtypeStruct((c, GVW), BF16),
                   jax.ShapeDtypeStruct((c, GKW), F32), jax.ShapeDtypeStruct((c, GKW), F32)],
        compiler_params=_cp("arbitrary"),
    )(pc, pc, la_f, la_b, bd_t, dsf, dsb)


_SRC_COLS = ((0, QW), (QW + 2 * KVW + 2 * GKW, GVW), (QW + 2 * KVW + 2 * GKW + GVW, GVW), (QW, KVW), (QW + KVW, KVW),
             (QW + 2 * KVW, GKW), (QW + 2 * KVW + GKW, GKW), (IN_COLS - 2 * GATE_RANK, 2 * GATE_RANK))
_DST_COLS = (C_Q, C_GV, C_GG, C_K, C_V, C_GQ, C_GK, C_Z)


def _pack_w_in(w_in):
    parts = [w_in[:, s:s + n] for s, n in _SRC_COLS]
    parts.append(jnp.zeros((w_in.shape[0], IN_PAD - C_Z - 2 * GATE_RANK), w_in.dtype))
    return jnp.concatenate(parts, axis=1)


def _unpack_w_in_grad(g):
    by_src = sorted(zip(_SRC_COLS, _DST_COLS))
    return jnp.concatenate([g[:, d:d + n] for (_, n), d in by_src], axis=1)


def _prep_gate_weights(w_gate_fwd, w_gate_bwd):
    pad_rows = lambda w, at: jnp.zeros((LANES, GKW), F32).at[at:at + GATE_RANK].set(w)
    return {"wg_f": pad_rows(w_gate_fwd, 0), "wg_b": pad_rows(w_gate_bwd, GATE_RANK)}


def _local_step(x, ctx, target, ada, ada_c, w, late_weights, reduce_behind=None, reduce_w_in=None):
    s, d = x.shape
    sh1, sc1, gt1, sh2, sc2, gt2 = [ada[:, i * d:(i + 1) * d] for i in range(6)]
    sh1c, sc1c = ada_c[:, :d], ada_c[:, d:2 * d]
    cos, sin = _rope_tables(s)
    gt = jnp.tile(w["g_gla_norm"], (1, GLA_HEADS))

    h = _norm_mod("pre_mix", x, w["g_pre_mix"], sh1, sc1)
    hc = _norm_mod("pre_mix_ctx", ctx, w["g_pre_mix"], sh1c, sc1c)
    w_in, token = w["w_in"](h)
    p = _mm("proj_in", h, w_in, "nn", after=token)
    pc = _mm("proj_in_ctx", hc, w_in, "nn")
    q_rot, k_rot, v_b = _rope_fwd("rope", p, cos, sin)
    pad = ((BLOCK, BLOCK), (0, 0))
    kp, vp = jnp.pad(k_rot, pad), jnp.pad(v_b, pad)
    kc, vc = pc[:, C_K:C_K + KVW].astype(BF16), pc[:, C_V:C_V + KVW].astype(BF16)
    attn = _attn_fwd(q_rot, kp, vp, kc, vc, w["attn_sink"])
    gate_w = (w["wg_f"], w["wg_b"], w["b_gate_fwd"], w["b_gate_bwd"])
    la_f, la_b = _gate_fwd("gate", p, *gate_w)
    la_fc, la_bc = _gate_fwd("gate_ctx", pc, *gate_w)
    st_f0, st_b0 = _ctx_state(pc, la_fc, la_bc)
    o_f, sts_f, o_b, sts_b = _gla_fwd(p, la_f, la_b, _group_states(st_f0), _group_states(st_b0))
    mix = _gla_out("gla_out", attn, o_f, o_b, p, gt)
    w_out, w_ffn_in_t, w_ffn_out = late_weights(attn)
    y = _mm("proj_out", mix, w_out, "nn", BF16)
    x1, h2 = _post_res_norm_mod("post_mix_pre_ffn", x, y, w["g_post_mix"], gt1, w["g_pre_ffn"], sh2, sc2)
    u, a = _ffn_in_swiglu("ffn_in", h2, w_ffn_in_t)
    f = _mm("ffn_out", a, w_ffn_out, "nn", BF16)
    g = {}
    dx2, df, loss, g["g_post_ffn"], dgt2 = _post_res_loss("post_ffn_loss", x1, f, w["g_post_ffn"], gt2, target)

    g["w_ffn_out"] = _mm("ffn_out_dw", a, df, "tn")
    du = _ffn_out_dx_swiglu_bwd("ffn_out_dx", df, w_ffn_out, u)
    dh2 = _mm("ffn_in_dx", du, w_ffn_in_t, "nn")
    g["w_ffn_in_t"] = _mm("ffn_in_dw", du, h2, "tn")
    dx1, dy, g["g_pre_ffn"], dsh2, dsc2, g["g_post_mix"], dgt1 = _norm_mod_post_res_bwd(
        "pre_ffn_post_mix_bwd", dh2, dx2, x1, y, w["g_pre_ffn"], sh2, sc2, w["g_post_mix"], gt1)
    dmix = _mm("proj_out_dx", dy, w_out, "nt", BF16)
    g["w_out"] = _mm("proj_out_dw", mix, dy, "tn")
    rb, sink, token = reduce_behind, w["attn_sink"], None
    if rb is not None:
        gt = _behind(gt, rb.start(g["w_ffn_in_t"], g["w_ffn_out"], g["w_out"]))
    d_o, dgg, dgt = _gla_out_bwd("gla_out_bwd", dmix, o_f, o_b, p, gt)
    g["g_gla_norm"] = jnp.sum(dgt.reshape(GLA_HEADS, GLA_DV), axis=0, keepdims=True)
    if rb is not None:
        token = rb.pair(dgg)
    gla_f, gla_b = _gla_bwd(p, la_f, la_b, sts_f, sts_b, d_o, token)
    (dla_f, dst_f0), (dla_b, dst_b0) = gla_f[3:], gla_b[3:]
    dst_f0, dst_b0 = _ungroup_states(dst_f0), _ungroup_states(dst_b0)
    if rb is not None:
        sink = _behind(sink, rb.total(dla_b))
    dgkc, dgvc, dla_fc, dla_bc = _ctx_state_bwd(pc, la_fc, la_bc, dst_f0, dst_b0)
    dz, dwf, dwb, dbf, dbb = _gate_bwd("gate_bwd", p, dla_f, dla_b, *gate_w)
    dzc, dwfc, dwbc, dbfc, dbbc = _gate_bwd("gate_ctx_bwd", pc, dla_fc, dla_bc, *gate_w)
    g["w_gate_fwd"] = (dwf + dwfc)[:GATE_RANK]
    g["w_gate_bwd"] = (dwb + dwbc)[GATE_RANK:2 * GATE_RANK]
    g["b_gate_fwd"], g["b_gate_bwd"] = dbf + dbfc, dbb + dbbc
    dq_rot, dkp, dvp, dkc, dvc, g["attn_sink"] = _attn_bwd(dmix, q_rot, kp, vp, kc, vc, sink)
    if rb is not None:
        g["behind"] = rb.result(dq_rot)
    dp = _proj_grad("proj_grad", dq_rot, dkp[BLOCK:BLOCK + s], dvp[BLOCK:BLOCK + s], cos, sin, gla_f[:3], gla_b[:3],
                    dgg, dz)
    c_rows = ctx.shape[0]
    zeros = lambda n: jnp.zeros((c_rows, n), BF16)
    dpc = jnp.concatenate([zeros(QW), dgvc, zeros(GVW), dkc.astype(BF16), dvc.astype(BF16), zeros(GKW), dgkc, dzc],
                          axis=1)
    g["w_in"] = _mm("proj_in_dw", h, dp, "tn", init=_mm("proj_in_ctx_dw", hc, dpc, "tn"))
    token = None if reduce_w_in is None else reduce_w_in.start(g["w_in"])
    dh = _mm("proj_in_dx", dp, w_in, "nt", after=token)
    dhc = _mm("proj_in_ctx_dx", dpc, w_in, "nt")
    if reduce_w_in is not None:
        sh1 = _behind(sh1, reduce_w_in.pair(dh))
    dx, dg_a, dsh1, dsc1 = _norm_mod_bwd("pre_mix_bwd", dh, dx1, x, w["g_pre_mix"], sh1, sc1)
    if reduce_w_in is not None:
        dsh1 = _behind(dsh1, reduce_w_in.total(dx))
    _, dg_b, dsh1c, dsc1c = _norm_mod_bwd("pre_mix_ctx_bwd", dhc, jnp.zeros_like(dhc), ctx, w["g_pre_mix"], sh1c,
                                          sc1c)
    g["g_pre_mix"] = dg_a + dg_b
    d_ada = jnp.concatenate([dsh1, dsc1, dgt1, dsh2, dsc2, dgt2], axis=1)
    d_ada_c = jnp.concatenate([dsh1c, dsc1c, jnp.zeros((1, 4 * d), F32)], axis=1)
    return loss, dx, g, d_ada, d_ada_c


HBM = pl.BlockSpec(memory_space=pltpu.HBM)
N_DEV, N_CHIP = 8, 4


def _place():
    x, y, c = lax.axis_index("x"), lax.axis_index("y"), lax.axis_index("c")
    return x, y, c, [(1 - x, y), (x, 1 - y), (1 - x, 1 - y)]


def _row_tile(n, mult, cap):
    return max(t for t in range(mult, min(n, cap) + 1, mult) if n % t == 0)


def _ag_small(name, v, after=None):
    follow = () if after is None else (after,)

    def body(v_ref, *rest):
        out_ref, send_sems, recv_sems = rest[len(follow):]
        x, y, c, _ = _place()
        out_ref[4 * x + 2 * y + c] = v_ref[...]

        def peer(r):
            return ((1 - x) if r & 4 else x, (1 - y) if r & 2 else y, (1 - c) if r & 1 else c)

        def copy(r, block):
            px, py, pc = block
            return pltpu.make_async_remote_copy(
                src_ref=v_ref, dst_ref=out_ref.at[4 * px + 2 * py + pc], send_sem=send_sems.at[r - 1],
                recv_sem=recv_sems.at[r - 1], device_id=peer(r), device_id_type=MESH)

        sends = [copy(r, (x, y, c)) for r in range(1, N_DEV)]
        for cp in sends:
            cp.start()
        for r in range(1, N_DEV):
            copy(r, peer(r)).wait_recv()
        for cp in sends:
            cp.wait_send()

    return pl.pallas_call(
        body, name=name, out_shape=jax.ShapeDtypeStruct((N_DEV,) + v.shape, v.dtype),
        in_specs=[pl.BlockSpec(memory_space=pltpu.VMEM)] + [pl.BlockSpec(memory_space=pl.ANY)] * len(follow),
        out_specs=pl.BlockSpec(memory_space=pltpu.VMEM),
        scratch_shapes=[pltpu.SemaphoreType.DMA((N_DEV - 1,)), pltpu.SemaphoreType.DMA((N_DEV - 1,))],
    )(v, *follow)


def _halves(c, rows, mult):
    hr = rows // 2
    return pl.ds(pl.multiple_of(c * hr, mult), hr), pl.ds(pl.multiple_of((1 - c) * hr, mult), hr)


def _ag_shards(name, shard):
    rows = shard.shape[0]

    def body(w_ref, out_ref, send_sems, recv_sems, local_sem):
        x, y, c, chips = _place()
        mine_half, other_half = _halves(c, rows, 16)
        me = 2 * x + y
        mine = pltpu.make_async_copy(w_ref, out_ref.at[me], local_sem)
        mine.start()

        def copy(k, src, chip, half, to):
            return pltpu.make_async_remote_copy(
                src_ref=src, dst_ref=out_ref.at[chip, half], send_sem=send_sems.at[k], recv_sem=recv_sems.at[k],
                device_id=to, device_id_type=MESH)

        first = [copy(j, w_ref.at[mine_half], me, mine_half, (px, py, c)) for j, (px, py) in enumerate(chips)]
        for cp in first:
            cp.start()
        passed = []
        for j, (px, py) in enumerate(chips):
            pk = 2 * px + py
            copy(j, w_ref.at[mine_half], pk, mine_half, (px, py, c)).wait_recv()
            cp = copy(3 + j, out_ref.at[pk, mine_half], pk, mine_half, (x, y, 1 - c))
            cp.start()
            passed.append(cp)
        for j, (px, py) in enumerate(chips):
            copy(3 + j, w_ref.at[mine_half], 2 * px + py, other_half, (x, y, 1 - c)).wait_recv()
        for cp in first + passed:
            cp.wait_send()
        mine.wait()

    return pl.pallas_call(
        body, name=name, out_shape=jax.ShapeDtypeStruct((N_CHIP,) + shard.shape, shard.dtype),
        in_specs=[HBM], out_specs=HBM,
        scratch_shapes=[pltpu.SemaphoreType.DMA((6,)), pltpu.SemaphoreType.DMA((6,)), pltpu.SemaphoreType.DMA],
    )(shard)


def _swap_half(name, g):
    n_sh, rows, n = g.shape

    def body(g_ref, a_ref, send_sem, recv_sem):
        x, y, c, _ = _place()
        _, other_half = _halves(c, rows, 8)
        cp = pltpu.make_async_remote_copy(
            src_ref=g_ref.at[pl.ds(0, n_sh), other_half], dst_ref=a_ref, send_sem=send_sem, recv_sem=recv_sem,
            device_id=(x, y, 1 - c), device_id_type=MESH)
        cp.start()
        cp.wait()

    return pl.pallas_call(
        body, name=name, out_shape=jax.ShapeDtypeStruct((n_sh, rows // 2, n), g.dtype), in_specs=[HBM], out_specs=HBM,
        scratch_shapes=[pltpu.SemaphoreType.DMA, pltpu.SemaphoreType.DMA],
    )(g)


def _add_half(name, g, a, c_idx):
    n_sh, hr, n = a.shape
    tr = _row_tile(hr, 16, 1024)
    nb = hr // tr

    def body(c_ref, g_ref, a_ref, o_ref):
        o_ref[...] = (g_ref[...] + a_ref[...]).astype(o_ref.dtype)

    return pl.pallas_call(
        body, name=name, out_shape=jax.ShapeDtypeStruct(a.shape, BF16),
        grid_spec=pltpu.PrefetchScalarGridSpec(
            num_scalar_prefetch=1, grid=(n_sh, nb),
            in_specs=[pl.BlockSpec((1, tr, n), lambda s, i, c_ref: (s, c_ref[0] * nb + i, 0)),
                      pl.BlockSpec((1, tr, n), lambda s, i, c_ref: (s, i, 0))],
            out_specs=pl.BlockSpec((1, tr, n), lambda s, i, c_ref: (s, i, 0))),
        compiler_params=_cp("parallel", "parallel"),
    )(c_idx, g, a)


def _scatter_chips(name, h):
    def body(h_ref, b_ref, send_sems, recv_sems, local_sem):
        x, y, c, chips = _place()
        me = 2 * x + y
        mine = pltpu.make_async_copy(h_ref.at[me], b_ref.at[me], local_sem)
        mine.start()

        def copy(j, src_block, dst_block, to):
            return pltpu.make_async_remote_copy(
                src_ref=h_ref.at[src_block], dst_ref=b_ref.at[dst_block], send_sem=send_sems.at[j],
                recv_sem=recv_sems.at[j], device_id=to, device_id_type=MESH)

        sends = [copy(j, 2 * px + py, me, (px, py, c)) for j, (px, py) in enumerate(chips)]
        for cp in sends:
            cp.start()
        for j, (px, py) in enumerate(chips):
            copy(j, me, 2 * px + py, (px, py, c)).wait_recv()
        for cp in sends:
            cp.wait_send()
        mine.wait()

    return pl.pallas_call(
        body, name=name, out_shape=jax.ShapeDtypeStruct(h.shape, h.dtype), in_specs=[HBM], out_specs=HBM,
        scratch_shapes=[pltpu.SemaphoreType.DMA((3,)), pltpu.SemaphoreType.DMA((3,)), pltpu.SemaphoreType.DMA],
    )(h)


def _sum_chips(name, b):
    n_sh, hr, n = b.shape
    tr = _row_tile(hr, 16, 1024)

    def body(b0, b1, b2, b3, o_ref):
        o_ref[...] = ((b0[0].astype(F32) + b1[0].astype(F32)) + b2[0].astype(F32)) + b3[0].astype(F32)

    return pl.pallas_call(
        body, name=name, grid=(hr // tr,), out_shape=jax.ShapeDtypeStruct((hr, n), F32),
        in_specs=[pl.BlockSpec((1, tr, n), functools.partial(lambda i, k: (k, i, 0), k=k)) for k in range(n_sh)],
        out_specs=pl.BlockSpec((tr, n), lambda i: (i, 0)), compiler_params=_cp("parallel"),
    )(b, b, b, b)


def _share_half(name, f):
    hr, n = f.shape

    def body(f_ref, out_ref, send_sem, recv_sem, local_sem):
        x, y, c, _ = _place()
        mine_half, other_half = _halves(c, 2 * hr, 8)
        mine = pltpu.make_async_copy(f_ref, out_ref.at[mine_half], local_sem)
        mine.start()

        def copy(half):
            return pltpu.make_async_remote_copy(
                src_ref=f_ref, dst_ref=out_ref.at[half], send_sem=send_sem, recv_sem=recv_sem,
                device_id=(x, y, 1 - c), device_id_type=MESH)

        send = copy(mine_half)
        send.start()
        copy(other_half).wait_recv()
        send.wait_send()
        mine.wait()

    return pl.pallas_call(
        body, name=name, out_shape=jax.ShapeDtypeStruct((2 * hr, n), f.dtype), in_specs=[HBM], out_specs=HBM,
        scratch_shapes=[pltpu.SemaphoreType.DMA, pltpu.SemaphoreType.DMA, pltpu.SemaphoreType.DMA],
    )(f)


def _reduce_shards(name, g, c_idx):
    a = _swap_half(name + "_swap", g)
    h = _add_half(name + "_pair", g, a, c_idx)
    b = _scatter_chips(name + "_scatter", h)
    f = _sum_chips(name + "_sum", b)
    return _share_half(name + "_share", f)


SEM = pl.BlockSpec(memory_space=pltpu.SEMAPHORE)
ANY = pl.BlockSpec(memory_space=pl.ANY)
DATAFLOW = pltpu.SideEffectType.DATAFLOW_SIDE_EFFECTING


def _remote(src, dst, send_sems, recv_sems, k, to):
    return pltpu.make_async_remote_copy(src_ref=src, dst_ref=dst, send_sem=send_sems.at[k], recv_sem=recv_sems.at[k],
                                        device_id=to, device_id_type=MESH)


def _split_copy(name, src, land_shape, land_dtype, n, plan, after=None):
    after = jnp.zeros((8, LANES), F32) if after is None else after

    def start_body(src_ref, land_ref, after_ref, send_sems, recv_sems, src_thru, land_thru, token):
        for cp in plan(src_ref, land_ref, send_sems, recv_sems)[0]:
            cp.start()
        token[...] = jnp.zeros_like(token)

    sems = pltpu.SemaphoreType.DMA((n,))
    send_sems, recv_sems, src_thru, land_thru, token = pl.pallas_call(
        start_body, name=name + "_start",
        out_shape=(sems, sems, pltpu.HBM(src.shape, src.dtype), pltpu.HBM(land_shape, land_dtype),
                   jax.ShapeDtypeStruct((8, LANES), F32)),
        in_specs=(HBM, HBM, ANY), out_specs=(SEM, SEM, HBM, HBM, pl.BlockSpec(memory_space=pltpu.VMEM)),
        input_output_aliases={0: 2, 1: 3}, compiler_params=pltpu.CompilerParams(has_side_effects=DATAFLOW),
    )(pltpu.with_memory_space_constraint(src, pltpu.HBM),
      pltpu.with_memory_space_constraint(lax.empty(land_shape, land_dtype), pltpu.HBM), after)

    def wait(after):
        def wait_body(src_ref, land_ref, send_sems, recv_sems, after_ref, src_out, land_out):
            sent, received = plan(src_ref, land_ref, send_sems, recv_sems)
            for cp in sent:
                cp.wait_send()
            for cp in received:
                cp.wait_recv()

        return pl.pallas_call(
            wait_body, name=name + "_wait",
            out_shape=(pltpu.HBM(src.shape, src.dtype), pltpu.HBM(land_shape, land_dtype)),
            in_specs=(HBM, HBM, SEM, SEM, ANY), out_specs=(HBM, HBM), input_output_aliases={0: 0, 1: 1},
            compiler_params=pltpu.CompilerParams(has_side_effects=DATAFLOW),
        )(src_thru, land_thru, send_sems, recv_sems, after)

    return token, wait


def _behind(x, token):
    return x + token[0, 0]


def _plan_gather(src_ref, land_ref, send_sems, recv_sems):
    x, y, c, chips = _place()
    sent = [_remote(src_ref, land_ref.at[2 * x + y], send_sems, recv_sems, j, (px, py, c))
            for j, (px, py) in enumerate(chips)]
    received = [_remote(src_ref, land_ref.at[2 * px + py], send_sems, recv_sems, j, (px, py, c))
                for j, (px, py) in enumerate(chips)]
    return sent, received


def _plan_swap(src_ref, land_ref, send_sems, recv_sems):
    x, y, c, _ = _place()
    _, other_half = _halves(c, src_ref.shape[1], 8)
    cp = _remote(src_ref.at[pl.ds(0, src_ref.shape[0]), other_half], land_ref, send_sems, recv_sems, 0, (x, y, 1 - c))
    return [cp], [cp]


def _plan_scatter(src_ref, land_ref, send_sems, recv_sems):
    x, y, c, chips = _place()
    sent = [_remote(src_ref.at[2 * px + py], land_ref.at[2 * x + y], send_sems, recv_sems, j, (px, py, c))
            for j, (px, py) in enumerate(chips)]
    received = [_remote(src_ref.at[2 * px + py], land_ref.at[2 * px + py], send_sems, recv_sems, j, (px, py, c))
                for j, (px, py) in enumerate(chips)]
    return sent, received


def _plan_share(src_ref, land_ref, send_sems, recv_sems):
    x, y, c, _ = _place()
    mine_half, other_half = _halves(c, land_ref.shape[0], 8)
    return ([_remote(src_ref, land_ref.at[mine_half], send_sems, recv_sems, 0, (x, y, 1 - c))],
            [_remote(src_ref, land_ref.at[other_half], send_sems, recv_sems, 0, (x, y, 1 - c))])


def _pack_shard_rows(name, parts):
    rows = [t.shape[0] // N_CHIP for t in parts]
    n, total = parts[0].shape[1], sum(t.shape[0] // N_CHIP for t in parts)
    slab, at = None, 0
    for i, (t, r) in enumerate(zip(parts, rows)):
        tr = max(c for c in range(8, min(r, 512) + 1, 8) if r % c == 0 and at % c == 0)
        nb, ob = r // tr, at // tr

        def body(t_ref, *rest):
            rest[-1][0] = t_ref[...]

        slab = pl.pallas_call(
            body, name=f"{name}_{i}", grid=(N_CHIP, nb), out_shape=jax.ShapeDtypeStruct((N_CHIP, total, n), t.dtype),
            in_specs=[pl.BlockSpec((tr, n), functools.partial(lambda k, j, nb: (k * nb + j, 0), nb=nb))]
            + ([] if slab is None else [pl.BlockSpec(memory_space=pl.ANY)]),
            out_specs=pl.BlockSpec((1, tr, n), functools.partial(lambda k, j, ob: (k, ob + j, 0), ob=ob)),
            input_output_aliases={} if slab is None else {1: 0}, compiler_params=_cp("parallel", "parallel"),
        )(*((t,) if slab is None else (t, slab)))
        at += r
    return slab


class _GatherBehind:
    def __init__(self, name, shard, chip, after=None):
        self.chip = chip
        self.token, self.wait = _split_copy(name, shard, (N_CHIP,) + shard.shape, shard.dtype, 3, _plan_gather,
                                            after)

    def result(self, after):
        shard, land = self.wait(after)
        return lax.dynamic_update_slice(land, shard[None], (self.chip, 0, 0))


class _ReduceBehind:
    def __init__(self, name, chip, c, c_idx):
        self.name, self.chip, self.c, self.c_idx = name, chip, c, c_idx

    def start(self, *grads):
        return self.start_slab(_pack_shard_rows(self.name + "_pack", grads))

    def start_slab(self, g):
        n_sh, rows, n = g.shape
        token, self.wait = _split_copy(self.name + "_swap", g, (n_sh, rows // 2, n), g.dtype, 1, _plan_swap)
        return token

    def pair(self, after):
        g, a = self.wait(after)
        h = _add_half(self.name + "_pair", g, a, self.c_idx)
        token, self.wait = _split_copy(self.name + "_scatter", h, h.shape, h.dtype, 3, _plan_scatter)
        return token

    def total(self, after):
        h, b = self.wait(after)
        b = lax.dynamic_update_slice(b, lax.dynamic_slice_in_dim(h, self.chip, 1, axis=0), (self.chip, 0, 0))
        f = _sum_chips(self.name + "_sum", b)
        token, self.wait = _split_copy(self.name + "_share", f, (2 * f.shape[0], f.shape[1]), f.dtype, 1,
                                       _plan_share)
        return token

    def result(self, after):
        f, out = self.wait(after)
        return lax.dynamic_update_slice(out, f, (self.c * f.shape[0], 0))


class _ReduceColsBehind(_ReduceBehind):
    def start(self, g_padded):
        g = _unpack_w_in_grad(g_padded)
        n = g.shape[1] // N_CHIP
        return self.start_slab(jnp.stack([g[:, k * n:(k + 1) * n] for k in range(N_CHIP)]))


def _f_adamw(w, g, m, v):
    m = ADAM_B1 * m + (1.0 - ADAM_B1) * g
    v = ADAM_B2 * v + (1.0 - ADAM_B2) * (g * g)
    m_hat = m / (1.0 - ADAM_B1 ** ADAM_STEP)
    v_hat = v / (1.0 - ADAM_B2 ** ADAM_STEP)
    return -ADAM_LR * (m_hat / (jnp.sqrt(v_hat) + ADAM_EPS) + ADAM_WD * w), m, v


def _adamw(name, w, g, m, v):
    rows, n = w.shape
    return _rowwise(name, lambda w, g, m, v: (_f_adamw(w, g, m, v), ()), rows, [(t, n, 0) for t in (w, g, m, v)], [],
                    [(n, F32)] * 3, [], tm=_row_tile(rows, 8, 256))


def _pack_rows(parts):
    rows = []
    for t in parts:
        t = t.reshape(-1)
        rows.append(jnp.pad(t, (0, -t.shape[0] % LANES)).reshape(-1, LANES))
    out = jnp.concatenate(rows, axis=0)
    return jnp.pad(out, ((0, -out.shape[0] % 8), (0, 0)))


def _unpack_rows(packed, shapes):
    out, r = [], 0
    for shp in shapes:
        n = int(np.prod(shp))
        nr = -(-n // LANES)
        out.append(packed[r:r + nr].reshape(-1)[:n].reshape(shp))
        r += nr
    return out


def _sum_blocks(name, g):
    def body(g_ref, o_ref):
        acc = g_ref[0]
        for k in range(1, g.shape[0]):
            acc = acc + g_ref[k]
        o_ref[...] = acc

    return pl.pallas_call(body, name=name, out_shape=jax.ShapeDtypeStruct(g.shape[1:], F32))(g)


def _silu(t):
    return t * _sigmoid(t)


def _ada_fwd(cc, w_ada):
    n = w_ada.shape[1]
    tn = _row_tile(n, LANES, 512)

    def body(cc_ref, w_ref, o_ref):
        o_ref[...] = _nn(_silu(cc_ref[...]), w_ref[...])

    return pl.pallas_call(
        body, name="ada_fwd", grid=(n // tn,), out_shape=jax.ShapeDtypeStruct((cc.shape[0], n), F32),
        in_specs=[pl.BlockSpec(cc.shape, lambda j: (0, 0)), pl.BlockSpec((w_ada.shape[0], tn), lambda j: (0, j))],
        out_specs=pl.BlockSpec((cc.shape[0], tn), lambda j: (0, j)), compiler_params=_cp("parallel"),
    )(cc, w_ada)


def _ada_bwd(cc, dm, w_ada):
    d, n = w_ada.shape
    tn = _row_tile(n, LANES, 512)

    def body(cc_ref, dm_ref, w_ref, gw_ref, ds_ref):
        @pl.when(pl.program_id(0) == 0)
        def _():
            ds_ref[...] = jnp.zeros_like(ds_ref)

        gw_ref[...] = _raw_dot("tn", _silu(cc_ref[...]), dm_ref[...], True)
        ds_ref[...] += _raw_dot("nt", dm_ref[...], w_ref[...], False)

    return pl.pallas_call(
        body, name="ada_bwd", grid=(n // tn,),
        out_shape=[jax.ShapeDtypeStruct((d, n), F32), jax.ShapeDtypeStruct(cc.shape, F32)],
        in_specs=[pl.BlockSpec(cc.shape, lambda j: (0, 0)), pl.BlockSpec((cc.shape[0], tn), lambda j: (0, j)),
                  pl.BlockSpec((d, tn), lambda j: (0, j))],
        out_specs=[pl.BlockSpec((d, tn), lambda j: (0, j)), pl.BlockSpec(cc.shape, lambda j: (0, 0))],
        compiler_params=_cp("arbitrary"),
    )(cc, dm, w_ada)


def _c_ctx_grad(parts, c_ctx):
    def body(p_ref, c_ref, o_ref):
        ds = ((p_ref[0] + p_ref[1]) + p_ref[2]) + p_ref[3]
        _, vjp = jax.vjp(_silu, c_ref[...])
        o_ref[...] = vjp(ds)[0]

    return pl.pallas_call(body, name="c_ctx_grad", out_shape=jax.ShapeDtypeStruct(c_ctx.shape, F32))(parts, c_ctx)


def kernel(x, c, ctx, c_ctx, w_ada, b_ada, g_pre_mix, g_post_mix, g_pre_ffn, g_post_ffn, w_in, attn_sink, w_gate_fwd, b_gate_fwd, w_gate_bwd, b_gate_bwd, g_gla_norm, w_out, w_ffn_in, w_ffn_out, loss_target, m_c_ctx, m_w_ada, m_b_ada, m_g_pre_mix, m_g_post_mix, m_g_pre_ffn, m_g_post_ffn, m_w_in, m_attn_sink, m_w_gate_fwd, m_b_gate_fwd, m_w_gate_bwd, m_b_gate_bwd, m_g_gla_norm, m_w_out, m_w_ffn_in, m_w_ffn_out, v_c_ctx, v_w_ada, v_b_ada, v_g_pre_mix, v_g_post_mix, v_g_pre_ffn, v_g_post_ffn, v_w_in, v_attn_sink, v_w_gate_fwd, v_b_gate_fwd, v_w_gate_bwd, v_b_gate_bwd, v_g_gla_norm, v_w_out, v_w_ffn_in, v_w_ffn_out):
    xi, yi, ci = lax.axis_index("x"), lax.axis_index("y"), lax.axis_index("c")
    dev, chip = 4 * xi + 2 * yi + ci, 2 * xi + yi
    c_idx = jnp.reshape(ci, (1,)).astype(jnp.int32)
    d = x.shape[-1]
    n_ada, n_in, n_f = w_ada.shape[-1], w_in.shape[-1], w_ffn_in.shape[-1]
    r_out, r_f = w_out.shape[1], w_ffn_out.shape[1]
    n_gate = w_gate_fwd.shape[-1]
    by_chip = lambda t: t[0::2]

    rc = -(-d // LANES)
    g1 = _ag_small("gather_cond", _pack_rows([c[0], w_gate_fwd[0], w_gate_bwd[0]]))
    c_all = g1[:, :rc].reshape(N_DEV, -1)[:, :d]
    gr = GATE_RANK * n_gate // LANES
    gate_full = lambda off: jnp.transpose(by_chip(g1)[:, off:off + gr].reshape(N_CHIP, GATE_RANK, n_gate),
                                          (1, 0, 2)).reshape(GATE_RANK, N_CHIP * n_gate)
    wgf, wgb = gate_full(rc), gate_full(rc + gr)
    cc = jnp.concatenate([c_all, c_ctx[None, :], jnp.zeros((7, d), F32)], axis=0)

    g2 = _ag_small("gather_ada", _ada_fwd(cc, w_ada[0]).reshape(-1, LANES))
    ada_all = jnp.transpose(by_chip(g2).reshape(N_CHIP, 16, n_ada), (1, 0, 2)).reshape(16, N_CHIP * n_ada) + b_ada
    first = _GatherBehind("gather_w_in", w_in[0].astype(BF16), chip, g2)
    late_slab = jnp.concatenate([w_out[0], w_ffn_out[0], jnp.transpose(w_ffn_in[0])], axis=0).astype(BF16)
    late = []

    def first_weights(after):
        w_in_g = first.result(after)
        late.append(_GatherBehind("gather_late", late_slab, chip, w_in_g))
        return _pack_w_in(jnp.concatenate([w_in_g[k] for k in range(N_CHIP)], axis=1)), late[0].token

    def late_weights(after):
        t = late[0].result(after)
        r1, r2 = r_out, r_out + r_f
        return (t[:, :r1].reshape(N_CHIP * r_out, d), t[:, r2:].reshape(N_CHIP * n_f, d),
                t[:, r1:r2].reshape(N_CHIP * r_f, d))

    ada_all = _behind(ada_all, first.token)
    ada = lax.dynamic_slice(ada_all, (dev, 0), (1, N_CHIP * n_ada))
    ada_c = ada_all[N_DEV:N_DEV + 1]

    w = _prep_gate_weights(wgf, wgb)
    w.update(w_in=first_weights, g_pre_mix=g_pre_mix, g_post_mix=g_post_mix, g_pre_ffn=g_pre_ffn, g_post_ffn=g_post_ffn,
             attn_sink=attn_sink, b_gate_fwd=b_gate_fwd, b_gate_bwd=b_gate_bwd, g_gla_norm=g_gla_norm)

    reduce_behind = _ReduceBehind("reduce_late", chip, ci, c_idx)
    reduce_w_in = _ReduceColsBehind("reduce_w_in", chip, ci, c_idx)
    loss_lanes, grad_x, g, d_ada, d_ada_c = _local_step(x[0], ctx[0], loss_target[0], ada, ada_c, w, late_weights,
                                                        reduce_behind, reduce_w_in)

    small = ("g_pre_mix", "g_post_mix", "g_pre_ffn", "g_post_ffn", "attn_sink", "b_gate_fwd", "b_gate_bwd",
             "g_gla_norm", "w_gate_fwd", "w_gate_bwd")
    shapes = [(1, 6 * d)] * 2 + [g[n].shape for n in small] + [(1, LANES)]
    g3 = _ag_small("gather_small_grads", _pack_rows([d_ada, d_ada_c] + [g[n] for n in small] + [loss_lanes]))
    tot = dict(zip(("d_ada", "d_ada_c") + small + ("loss",),
                   _unpack_rows(_sum_blocks("sum_small_grads", g3), shapes)))
    r_ada = 6 * d // LANES
    dm = jnp.concatenate([g3[:, :r_ada].reshape(N_DEV, 6 * d), tot["d_ada_c"], jnp.zeros((7, 6 * d), F32)], axis=0)
    grads = {n: tot[n] for n in small[:8]}
    grads["b_ada"] = _sum_blocks("sum_b_ada", dm.reshape(16, r_ada, LANES)).reshape(1, 6 * d)
    grads["w_gate_fwd"] = lax.dynamic_slice(tot["w_gate_fwd"], (0, chip * n_gate), (GATE_RANK, n_gate))[None]
    grads["w_gate_bwd"] = lax.dynamic_slice(tot["w_gate_bwd"], (0, chip * n_gate), (GATE_RANK, n_gate))[None]
    gw_ada, dsc = _ada_bwd(cc, lax.dynamic_slice(dm, (0, chip * n_ada), (16, n_ada)), w_ada[0])
    grads["w_ada"] = gw_ada[None]
    g4 = _ag_small("gather_c_ctx", _pack_rows([dsc[N_DEV]]))
    grads["c_ctx"] = _c_ctx_grad(by_chip(g4), _pack_rows([c_ctx])).reshape(-1)[:d]

    grads["w_in"] = reduce_w_in.result(g4)[None]
    behind = g["behind"]
    grads["w_ffn_in"] = jnp.transpose(behind[:n_f])[None]
    grads["w_ffn_out"], grads["w_out"] = behind[None, n_f:n_f + r_f], behind[None, n_f + r_f:]

    names = ("c_ctx", "w_ada", "b_ada", "g_pre_mix", "g_post_mix", "g_pre_ffn", "g_post_ffn", "w_in", "attn_sink",
             "w_gate_fwd", "b_gate_fwd", "w_gate_bwd", "b_gate_bwd", "g_gla_norm", "w_out", "w_ffn_in", "w_ffn_out")
    weights = dict(zip(names, (c_ctx, w_ada, b_ada, g_pre_mix, g_post_mix, g_pre_ffn, g_post_ffn, w_in, attn_sink,
                               w_gate_fwd, b_gate_fwd, w_gate_bwd, b_gate_bwd, g_gla_norm, w_out, w_ffn_in,
                               w_ffn_out)))
    m_in = dict(zip(names, (m_c_ctx, m_w_ada, m_b_ada, m_g_pre_mix, m_g_post_mix, m_g_pre_ffn, m_g_post_ffn, m_w_in,
                            m_attn_sink, m_w_gate_fwd, m_b_gate_fwd, m_w_gate_bwd, m_b_gate_bwd, m_g_gla_norm,
                            m_w_out, m_w_ffn_in, m_w_ffn_out)))
    v_in = dict(zip(names, (v_c_ctx, v_w_ada, v_b_ada, v_g_pre_mix, v_g_post_mix, v_g_pre_ffn, v_g_post_ffn, v_w_in,
                            v_attn_sink, v_w_gate_fwd, v_b_gate_fwd, v_w_gate_bwd, v_b_gate_bwd, v_g_gla_norm,
                            v_w_out, v_w_ffn_in, v_w_ffn_out)))
    large = ("w_ada", "w_in", "w_out", "w_ffn_in", "w_ffn_out")
    tiny = tuple(n for n in names if n not in large)
    delta, new_m, new_v = {}, {}, {}
    for n in large:
        dl, nm, nv = _adamw("adamw_" + n, weights[n][0], grads[n][0], m_in[n][0], v_in[n][0])
        delta[n], new_m[n], new_v[n] = dl[None], nm[None], nv[None]
    tiny_shapes = [weights[n].shape for n in tiny]
    packed = [_pack_rows([t[n] for n in tiny]) for t in (weights, grads, m_in, v_in)]
    for out, res in zip((delta, new_m, new_v), _adamw("adamw_small", *packed)):
        out.update(zip(tiny, _unpack_rows(res, tiny_shapes)))
    for n in tiny:
        grads[n] = grads[n].reshape(weights[n].shape)

    return (tot["loss"][0, 0], grad_x[None], *[grads[n] for n in names], *[delta[n] for n in names], *[new_m[n] for n in names],
            *[new_v[n] for n in names])
```

```python
import functools

import jax
import jax.numpy as jnp
import numpy as np
from jax import lax
from jax.experimental import pallas as pl
from jax.experimental.pallas import tpu as pltpu

F32 = jnp.float32
BF16 = jnp.bfloat16
MESH = pl.DeviceIdType.MESH

HEAD_DIM = 64
ATT_HEADS = 8
ATT_KV_HEADS = 2
ATT_GROUP = ATT_HEADS // ATT_KV_HEADS
WINDOW = 128
BLOCK = 128
GRID_W = 64
ROPE_BASE = 10000.0
GLA_HEADS = 8
GLA_DK = 32
GLA_DV = 64
GLA_CHUNK = 64
GATE_RANK = 16
GATE_TAU = 16.0
NEG_INF = -1e30
QW = ATT_HEADS * HEAD_DIM
KVW = ATT_KV_HEADS * HEAD_DIM
GKW = GLA_HEADS * GLA_DK
GVW = GLA_HEADS * GLA_DV
IN_COLS = QW + 2 * KVW + 2 * GKW + 2 * GVW + 2 * GATE_RANK
LANES = 128
IN_PAD = IN_COLS + LANES - 2 * GATE_RANK
C_Q, C_GV, C_GG = 0, QW, QW + GVW
C_K = C_GG + GVW
C_V = C_K + KVW
C_GQ = C_V + KVW
C_GK = C_GQ + GKW
C_Z = C_GK + GKW
MIX = QW + GVW

ADAM_LR, ADAM_B1, ADAM_B2, ADAM_EPS, ADAM_WD, ADAM_STEP = 0.001, 0.9, 0.999, 1e-08, 0.01, 10

VMEM_LIMIT = 56 * 1024 * 1024


def _cp(*sem):
    return pltpu.CompilerParams(dimension_semantics=sem, vmem_limit_bytes=VMEM_LIMIT)


def _pick(n, cands):
    for t in cands:
        if n % t == 0:
            return t
    return n


_DIMS = {"nn": (((1,), (0,)), ((), ())), "nt": (((1,), (1,)), ((), ())), "tn": (((0,), (0,)), ((), ()))}


def _raw_dot(mode, a, b, hi):
    dot = lambda u, v: lax.dot_general(u, v, _DIMS[mode], preferred_element_type=F32)
    if hi:
        a, b = a.astype(F32), b.astype(F32)
        a_hi, b_hi = a.astype(BF16), b.astype(BF16)
        a_lo, b_lo = (a - a_hi.astype(F32)).astype(BF16), (b - b_hi.astype(F32)).astype(BF16)
        return dot(a_hi, b_hi) + (dot(a_lo, b_hi) + dot(a_hi, b_lo))
    return dot(a.astype(BF16), b.astype(BF16))


def _make_dot(mode, hi):
    @jax.custom_vjp
    def dot(a, b):
        return _raw_dot(mode, a, b, hi)

    def fwd(a, b):
        return _raw_dot(mode, a, b, hi), (a, b)

    def bwd(res, dc):
        a, b = res
        if mode == "nn":
            return _raw_dot("nt", dc, b, hi), _raw_dot("tn", a, dc, hi)
        if mode == "nt":
            return _raw_dot("nn", dc, b, hi), _raw_dot("tn", dc, a, hi)
        return _raw_dot("nt", b, dc, hi), _raw_dot("nn", a, dc, hi)

    dot.defvjp(fwd, bwd)
    return dot


_nn, _nt, _tn = _make_dot("nn", False), _make_dot("nt", False), _make_dot("tn", False)
_nn_hi = _make_dot("nn", True)


MM_VMEM_BUDGET = 44 * 1024 * 1024


def _halvings(n):
    out = [n]
    while out[-1] % (2 * LANES) == 0:
        out.append(out[-1] // 2)
    return out


def _mm_tiles(mode, m, n, k, a_bytes, b_bytes, o_bytes, init_bytes=0):
    tms = [t for t in dict.fromkeys((m, m // 2, m // 4, 2048, 1024, 512, 256, 128))
           if m % t == 0 and t % (LANES if mode == "tn" else 16) == 0 and t <= 4096] or [m]
    if mode == "tn":
        fits = [(k // tk + 0.5 * (m // tm), tm, tk)
                for tk in (4096, 2048, 1024, 512, 256, 128) if k % tk == 0 for tm in tms
                if 2 * (tk * tm * a_bytes + tk * n * b_bytes + tm * n * (o_bytes + init_bytes)) <= MM_VMEM_BUDGET]
        if fits:
            _, tm, tk = min(fits)
            return tm, n, tk
    tks = ([t for t in (512, 256, 128) if k % t == 0] or [k]) if mode == "tn" else _halvings(k)
    for tn in _halvings(n):
        for tk in tks:
            for tm in tms:
                acc = tm * tn * 4 if (k // tk > 1 and o_bytes != 4) else 0
                tiles = tm * tk * a_bytes + tk * tn * b_bytes + tm * tn * (o_bytes + init_bytes)
                if 2 * tiles + acc <= MM_VMEM_BUDGET:
                    return tm, tn, tk
    return tms[-1], _halvings(n)[-1], tks[-1]


def _mm(name, a, b, mode, out_dtype=F32, init=None, after=None):
    follow = () if after is None else (after,)
    if mode == "nn":
        (m, k), n = a.shape, b.shape[1]
    elif mode == "nt":
        (m, k), n = a.shape, b.shape[0]
    else:
        (k, m), n = a.shape, b.shape[1]
    tm, tn, tk = _mm_tiles(mode, m, n, k, a.dtype.itemsize, b.dtype.itemsize, jnp.dtype(out_dtype).itemsize,
                           0 if init is None else 4)
    nk = k // tk
    use_acc = nk > 1 and out_dtype != F32

    inits = () if init is None else (init,)

    def body(a_ref, b_ref, *rest):
        rest = rest[:len(inits)] + rest[len(inits) + len(follow):]
        o_ref, acc = rest[len(inits)], rest[len(inits) + 1:]
        part = _raw_dot(mode, a_ref[...], b_ref[...], False)
        first = lambda: part + rest[0][...] if inits else part
        if nk == 1:
            o_ref[...] = first().astype(o_ref.dtype)
            return
        acc_ref = acc[0] if use_acc else o_ref
        kk = pl.program_id(2)

        @pl.when(kk == 0)
        def _():
            acc_ref[...] = first()

        @pl.when(kk > 0)
        def _():
            acc_ref[...] += part

        if use_acc:
            @pl.when(kk == nk - 1)
            def _():
                o_ref[...] = acc_ref[...].astype(o_ref.dtype)

    if mode == "nn":
        a_spec = pl.BlockSpec((tm, tk), lambda i, j, kk: (i, kk))
        b_spec = pl.BlockSpec((tk, tn), lambda i, j, kk: (kk, j))
    elif mode == "nt":
        a_spec = pl.BlockSpec((tm, tk), lambda i, j, kk: (i, kk))
        b_spec = pl.BlockSpec((tn, tk), lambda i, j, kk: (j, kk))
    else:
        a_spec = pl.BlockSpec((tk, tm), lambda i, j, kk: (kk, i))
        b_spec = pl.BlockSpec((tk, tn), lambda i, j, kk: (kk, j))
    return pl.pallas_call(
        body, name=name, grid=(m // tm, n // tn, nk),
        in_specs=[a_spec, b_spec] + [pl.BlockSpec((tm, tn), lambda i, j, kk: (i, j))] * len(inits)
        + [pl.BlockSpec(memory_space=pl.ANY)] * len(follow),
        out_specs=pl.BlockSpec((tm, tn), lambda i, j, kk: (i, j)),
        out_shape=jax.ShapeDtypeStruct((m, n), out_dtype),
        scratch_shapes=[pltpu.VMEM((tm, tn), F32)] if use_acc else [],
        compiler_params=_cp("parallel", "parallel", "arbitrary"),
    )(a, b, *inits, *follow)


def _rowwise(name, fn, rows, row_ins, full_ins, row_outs, acc_outs, tm=None):
    tm = tm or _pick(rows, (512, 256, 128))
    n_r, n_f, n_o, n_a = len(row_ins), len(full_ins), len(row_outs), len(acc_outs)

    def body(*refs):
        ins, outs = refs[:n_r + n_f], refs[n_r + n_f:]
        vals = [r[...].astype(F32) for r in ins]
        ro, ao = fn(*vals)
        for r, val in zip(outs[:n_o], ro):
            r[...] = val.astype(r.dtype)
        if n_a:
            @pl.when(pl.program_id(0) == 0)
            def _():
                for r in outs[n_o:]:
                    r[...] = jnp.zeros_like(r)

            for r, val in zip(outs[n_o:], ao):
                r[...] += val

    in_specs = [pl.BlockSpec((tm, w), functools.partial(lambda i, cb: (i, cb), cb=cb)) for _, w, cb in row_ins]
    in_specs += [pl.BlockSpec(a.shape, lambda i: (0, 0)) for a in full_ins]
    out_specs = [pl.BlockSpec((tm, w), lambda i: (i, 0)) for w, _ in row_outs]
    out_specs += [pl.BlockSpec(s, lambda i: (0, 0)) for s in acc_outs]
    out_shape = [jax.ShapeDtypeStruct((rows, w), dt) for w, dt in row_outs]
    out_shape += [jax.ShapeDtypeStruct(s, F32) for s in acc_outs]
    return pl.pallas_call(
        body, name=name, grid=(rows // tm,), in_specs=in_specs, out_specs=out_specs, out_shape=out_shape,
        compiler_params=_cp("arbitrary" if n_a else "parallel"),
    )(*[a for a, _, _ in row_ins], *full_ins)


def _rn(x):
    return x * lax.rsqrt(jnp.mean(x * x, axis=-1, keepdims=True) + 1e-6)


def _sigmoid(t):
    return 1.0 / (1.0 + jnp.exp(-t))


def _f_norm_mod(x, g, sh, sc):
    return _rn(x) * g * (1.0 + sc) + sh


def _f_post_res(xr, y, g, gate):
    return xr + gate * (_rn(y) * g)


def _f_swiglu(g, u):
    return g * _sigmoid(g) * u


def _logsig(u):
    return jnp.minimum(u, 0.0) - jnp.log(1.0 + jnp.exp(-jnp.abs(u)))


def _f_gate(z, wf, wb, bf, bb):
    return _logsig(_nn(z, wf) + bf) / GATE_TAU, _logsig(_nn(z, wb) + bb) / GATE_TAU


def _f_gla_out(of, ob, gg, gt, bd):
    o = of + ob
    ms = _nn_hi(o * o, bd)
    return o * lax.rsqrt(ms + 1e-6) * gt * (gg * _sigmoid(gg))


def _norm_mod(name, x, g, sh, sc):
    rows, d = x.shape
    return _rowwise(name, lambda x, g, sh, sc: ((_f_norm_mod(x, g, sh, sc),), ()), rows,
                    [(x, d, 0)], [g, sh, sc], [(d, BF16)], [])[0]


def _norm_mod_bwd(name, dh, dres, x, g, sh, sc):
    rows, d = x.shape

    def fn(dh, dres, x, g, sh, sc):
        _, vjp = jax.vjp(_f_norm_mod, x, g, sh, sc)
        dx, dg, dsh, dsc = vjp(dh)
        return (dx + dres,), (dg, dsh, dsc)

    return _rowwise(name, fn, rows, [(dh, d, 0), (dres, d, 0), (x, d, 0)], [g, sh, sc], [(d, F32)],
                    [(1, d)] * 3)


def _post_res_norm_mod(name, xr, y, g_post, gate, g_pre, sh, sc):
    rows, d = xr.shape

    def fn(xr, y, g_post, gate, g_pre, sh, sc):
        x1 = _f_post_res(xr, y, g_post, gate)
        return (x1, _f_norm_mod(x1, g_pre, sh, sc)), ()

    return _rowwise(name, fn, rows, [(xr, d, 0), (y, d, 0)], [g_post, gate, g_pre, sh, sc], [(d, F32), (d, BF16)], [])


def _norm_mod_post_res_bwd(name, dh, dres, x1, y, g_pre, sh, sc, g_post, gate):
    rows, d = x1.shape

    def fn(dh, dres, x1, y, g_pre, sh, sc, g_post, gate):
        _, vjp_norm = jax.vjp(_f_norm_mod, x1, g_pre, sh, sc)
        dx1, dg_pre, dsh, dsc = vjp_norm(dh)
        dx1 = dx1 + dres
        _, vjp_res = jax.vjp(lambda y, g, gate: _f_post_res(jnp.zeros_like(y), y, g, gate), y, g_post, gate)
        dy, dg_post, dgate = vjp_res(dx1)
        return (dx1, dy), (dg_pre, dsh, dsc, dg_post, dgate)

    return _rowwise(name, fn, rows, [(dh, d, 0), (dres, d, 0), (x1, d, 0), (y, d, 0)], [g_pre, sh, sc, g_post, gate],
                    [(d, F32), (d, BF16)], [(1, d)] * 5, tm=_pick(rows, (256, 128)))


def _post_res_loss(name, xr, y, g, gate, target):
    rows, d = xr.shape

    def fn(xr, y, target, g, gate):
        x2, vjp = jax.vjp(lambda y, g, gate: _f_post_res(xr, y, g, gate), y, g, gate)
        diff = x2 - target
        part = 0.5 * jnp.sum(jnp.mean(diff * diff, axis=-1, keepdims=True), axis=0, keepdims=True)
        dx2 = diff * (1.0 / d)
        dy, dg, dgate = vjp(dx2)
        return (dx2, dy), (jnp.broadcast_to(part, (1, LANES)), dg, dgate)

    return _rowwise(name, fn, rows, [(xr, d, 0), (y, d, 0), (target, d, 0)], [g, gate], [(d, F32), (d, BF16)],
                    [(1, LANES), (1, d), (1, d)])


def _mm_rows(name, a, b, mode, fn, extras, outs):
    m, k = a.shape
    tm = _pick(m, (256, 128))

    def body(a_ref, b_ref, *rest):
        tiles = fn(_raw_dot(mode, a_ref[...], b_ref[...], False), *[e[...] for e in rest[:len(extras)]])
        for r, val in zip(rest[len(extras):], tiles):
            r[...] = val.astype(r.dtype)

    row = lambda w: pl.BlockSpec((tm, w), lambda i: (i, 0))
    return pl.pallas_call(
        body, name=name, grid=(m // tm,),
        in_specs=[row(k), pl.BlockSpec(b.shape, lambda i: (0, 0))] + [row(e.shape[1]) for e in extras],
        out_specs=[row(w) for w, _ in outs], out_shape=[jax.ShapeDtypeStruct((m, w), dt) for w, dt in outs],
        compiler_params=_cp("parallel"),
    )(a, b, *extras)


def _ffn_in_swiglu(name, h, w_t):
    f = w_t.shape[0] // 2
    fn = lambda u: (u, _f_swiglu(u[:, :f], u[:, f:]))
    return _mm_rows(name, h, w_t, "nt", fn, [], [(2 * f, BF16), (f, BF16)])


def _ffn_out_dx_swiglu_bwd(name, df, w_out, u):
    f = w_out.shape[0]

    def fn(da, u):
        u = u.astype(F32)
        _, vjp = jax.vjp(_f_swiglu, u[:, :f], u[:, f:])
        return (jnp.concatenate(vjp(da), axis=1),)

    return _mm_rows(name, df, w_out, "nt", fn, [u], [(2 * f, BF16)])[0]


def _gate_fwd(name, p, wf, wb, bf, bb):
    rows = p.shape[0]
    return _rowwise(name, lambda z, wf, wb, bf, bb: (_f_gate(z, wf, wb, bf, bb), ()), rows,
                    [(p, LANES, C_Z // LANES)], [wf, wb, bf, bb], [(GKW, F32)] * 2, [])


def _gate_bwd(name, p, dla_f, dla_b, wf, wb, bf, bb):
    rows = p.shape[0]

    def fn(z, dlf, dlb, wf, wb, bf, bb):
        _, vjp = jax.vjp(_f_gate, z, wf, wb, bf, bb)
        dz, dwf, dwb, dbf, dbb = vjp((dlf, dlb))
        return (dz,), (dwf, dwb, dbf, dbb)

    return _rowwise(name, fn, rows, [(p, LANES, C_Z // LANES), (dla_f, GKW, 0), (dla_b, GKW, 0)],
                    [wf, wb, bf, bb], [(LANES, BF16)], [(LANES, GKW), (LANES, GKW), (1, GKW), (1, GKW)])


def _head_mean_matrix():
    h = np.arange(GVW) // GLA_DV
    return jnp.asarray((h[:, None] == h[None, :]).astype(np.float32) / GLA_DV)


def _gla_out(name, attn, of, ob, p, gt):
    rows = of.shape[0]
    bd = _head_mean_matrix()
    fn = lambda attn, of, ob, gg, gt, bd: ((jnp.concatenate([attn, _f_gla_out(of, ob, gg, gt, bd)], axis=1),), ())
    return _rowwise(name, fn, rows, [(attn, QW, 0), (of, GVW, 0), (ob, GVW, 0), (p, GVW, C_GG // GVW)], [gt, bd],
                    [(MIX, BF16)], [])[0]


def _gla_out_bwd(name, dmix, of, ob, p, gt):
    rows = of.shape[0]
    bd = _head_mean_matrix()

    def fn(dm, of, ob, gg, gt, bd):
        _, vjp = jax.vjp(lambda of, gg, gt: _f_gla_out(of, ob, gg, gt, bd), of, gg, gt)
        do, dgg, dgt = vjp(dm)
        return (do, dgg), (dgt,)

    return _rowwise(name, fn, rows, [(dmix, GVW, 1), (of, GVW, 0), (ob, GVW, 0), (p, GVW, C_GG // GVW)], [gt, bd],
                    [(GVW, F32), (GVW, BF16)], [(1, GVW)])


def _rope_tables(n_tokens):
    t = jnp.arange(n_tokens)
    row = (t // GRID_W).astype(F32)
    col = (t % GRID_W).astype(F32)
    half = HEAD_DIM // 2
    inv_freq = ROPE_BASE ** (-jnp.arange(0, half, 2, dtype=F32) / half)
    ang_r = row[:, None] * inv_freq[None, :]
    ang_c = col[:, None] * inv_freq[None, :]
    ang = jnp.concatenate([ang_r, ang_r, ang_c, ang_c], axis=-1)
    sign = jnp.concatenate([-jnp.ones((16,), F32), jnp.ones((16,), F32)] * 2)
    cos, sin = jnp.cos(ang), jnp.sin(ang) * sign[None, :]
    return jnp.tile(cos, (1, 2)), jnp.tile(sin, (1, 2))


def _rot_pairs(x):
    w = x.shape[-1]
    lane = lax.broadcasted_iota(jnp.int32, x.shape, x.ndim - 1)
    return jnp.where((lane % 32) < 16, pltpu.roll(x, w - 16, x.ndim - 1), pltpu.roll(x, 16, x.ndim - 1))


def _rope_apply(x, cos, sin_signed, inverse):
    reps = x.shape[-1] // LANES
    cos = jnp.concatenate([cos] * reps, axis=-1) if reps > 1 else cos
    sin = jnp.concatenate([sin_signed] * reps, axis=-1) if reps > 1 else sin_signed
    if inverse:
        return x * cos + _rot_pairs(x * sin)
    return x * cos + _rot_pairs(x) * sin


def _rope_fwd(name, p, cos, sin):
    rows = p.shape[0]

    def fn(q, k, v, cos, sin):
        return (_rope_apply(q, cos, sin, False), _rope_apply(k, cos, sin, False), v), ()

    return _rowwise(name, fn, rows, [(p, QW, 0), (p, KVW, C_K // KVW), (p, KVW, C_V // KVW), (cos, LANES, 0),
                                     (sin, LANES, 0)], [], [(QW, BF16), (KVW, BF16), (KVW, BF16)], [])


def _proj_grad(name, dq_rot, dk_rot, dv, cos, sin, gla_f, gla_b, dgg, dz):
    rows = dq_rot.shape[0]

    def fn(dq, dk, dv, cos, sin, gqf, gkf, gvf, gqb, gkb, gvb, dgg, dz):
        parts = [_rope_apply(dq, cos, sin, True), gvf + gvb, dgg, _rope_apply(dk, cos, sin, True), dv, gqf + gqb,
                 gkf + gkb, dz]
        return (jnp.concatenate(parts, axis=1),), ()

    ins = [(dq_rot, QW), (dk_rot, KVW), (dv, KVW), (cos, LANES), (sin, LANES)]
    ins += [(t, t.shape[1]) for t in (*gla_f, *gla_b)] + [(dgg, GVW), (dz, LANES)]
    return _rowwise(name, fn, rows, [(t, w, 0) for t, w in ins], [], [(IN_PAD, BF16)], [],
                    tm=_pick(rows, (256, 128)))[0]


GROUP_ROWS = ATT_GROUP * BLOCK


def _f_attn(qs, kws, vws, kcs, vcs, sink, n, n_tokens):
    row = lax.broadcasted_iota(jnp.int32, (GROUP_ROWS, 1), 0)
    group = sum((row >= g * BLOCK).astype(jnp.int32) for g in range(1, ATT_GROUP))
    i = lax.broadcasted_iota(jnp.int32, (GROUP_ROWS, 3 * BLOCK), 0) - BLOCK * group
    j = lax.broadcasted_iota(jnp.int32, (GROUP_ROWS, 3 * BLOCK), 1)
    kpos = (n - 1) * BLOCK + j
    mask = (jnp.abs(j - BLOCK - i) <= WINDOW) & (kpos >= 0) & (kpos < n_tokens)
    head_id = lax.broadcasted_iota(jnp.int32, (1, ATT_HEADS), 1)
    scale = HEAD_DIM ** -0.5
    outs = []
    for h in range(ATT_KV_HEADS):
        sk = jnp.zeros((GROUP_ROWS, 1), F32)
        for g in range(ATT_GROUP):
            one = jnp.sum(jnp.where(head_id == h * ATT_GROUP + g, sink, 0.0), axis=-1, keepdims=True)
            sk = jnp.where(group == g, one, sk)
        q = qs[h] * scale
        s_w = jnp.where(mask, _nt(q, kws[h]), NEG_INF)
        s_c = _nt(q, kcs[h])
        m = lax.stop_gradient(jnp.maximum(jnp.maximum(jnp.max(s_w, axis=-1, keepdims=True),
                                                      jnp.max(s_c, axis=-1, keepdims=True)), sk))
        pw, pc = jnp.exp(s_w - m), jnp.exp(s_c - m)
        den = jnp.sum(pw, axis=-1, keepdims=True) + jnp.sum(pc, axis=-1, keepdims=True) + jnp.exp(sk - m)
        outs.append((_nn(pw, vws[h]) + _nn(pc, vcs[h])) / den)
    return tuple(outs)


def _group_rows(ref, h):
    hs = lambda hq: slice(hq * HEAD_DIM, (hq + 1) * HEAD_DIM)
    return jnp.concatenate([ref[:, hs(h * ATT_GROUP + g)].astype(F32) for g in range(ATT_GROUP)], axis=0)


def _ungroup_rows(ref, h, val):
    for g in range(ATT_GROUP):
        hq = h * ATT_GROUP + g
        ref[:, hq * HEAD_DIM:(hq + 1) * HEAD_DIM] = val[g * BLOCK:(g + 1) * BLOCK].astype(ref.dtype)


def _attn_loads(n, q_ref, kp_ref, vp_ref, kc_ref, vc_ref):
    r0 = pl.multiple_of(n * BLOCK, BLOCK)
    hs = lambda h: slice(h * HEAD_DIM, (h + 1) * HEAD_DIM)
    qs = [_group_rows(q_ref, h) for h in range(ATT_KV_HEADS)]
    kws = [kp_ref[pl.ds(r0, 3 * BLOCK), hs(h)].astype(F32) for h in range(ATT_KV_HEADS)]
    vws = [vp_ref[pl.ds(r0, 3 * BLOCK), hs(h)].astype(F32) for h in range(ATT_KV_HEADS)]
    kcs = [kc_ref[:, hs(h)].astype(F32) for h in range(ATT_KV_HEADS)]
    vcs = [vc_ref[:, hs(h)].astype(F32) for h in range(ATT_KV_HEADS)]
    return r0, hs, qs, kws, vws, kcs, vcs


def _attn_specs(s, c):
    full = lambda shape: pl.BlockSpec(shape, lambda n: (0, 0))
    return [pl.BlockSpec((BLOCK, QW), lambda n: (n, 0)), full((s + 2 * BLOCK, KVW)), full((s + 2 * BLOCK, KVW)),
            full((c, KVW)), full((c, KVW)), full((1, ATT_HEADS))]


def _attn_fwd(q, kp, vp, kc, vc, sink):
    s, c = q.shape[0], kc.shape[0]

    def body(q_ref, kp_ref, vp_ref, kc_ref, vc_ref, sink_ref, o_ref):
        n = pl.program_id(0)
        _, hs, qs, kws, vws, kcs, vcs = _attn_loads(n, q_ref, kp_ref, vp_ref, kc_ref, vc_ref)
        outs = _f_attn(qs, kws, vws, kcs, vcs, sink_ref[...], n, s)
        for h in range(ATT_KV_HEADS):
            _ungroup_rows(o_ref, h, outs[h])

    return pl.pallas_call(
        body, name="attn_fwd", grid=(s // BLOCK,), in_specs=_attn_specs(s, c),
        out_specs=pl.BlockSpec((BLOCK, QW), lambda n: (n, 0)), out_shape=jax.ShapeDtypeStruct((s, QW), BF16),
        compiler_params=_cp("parallel"),
    )(q, kp, vp, kc, vc, sink)


def _attn_bwd(do, q, kp, vp, kc, vc, sink):
    s, c = q.shape[0], kc.shape[0]

    def body(do_ref, q_ref, kp_ref, vp_ref, kc_ref, vc_ref, sink_ref, dq_ref, dkp_ref, dvp_ref, dkc_ref, dvc_ref,
             dsink_ref):
        n = pl.program_id(0)

        @pl.when(n == 0)
        def _():
            for r in (dkp_ref, dvp_ref, dkc_ref, dvc_ref, dsink_ref):
                r[...] = jnp.zeros_like(r)

        r0, hs, qs, kws, vws, kcs, vcs = _attn_loads(n, q_ref, kp_ref, vp_ref, kc_ref, vc_ref)
        _, vjp = jax.vjp(lambda qs, kws, vws, kcs, vcs, sink: _f_attn(qs, kws, vws, kcs, vcs, sink, n, s),
                         qs, kws, vws, kcs, vcs, sink_ref[...])
        dqs, dkws, dvws, dkcs, dvcs, dsink = vjp(tuple(_group_rows(do_ref, h) for h in range(ATT_KV_HEADS)))
        for h in range(ATT_KV_HEADS):
            _ungroup_rows(dq_ref, h, dqs[h])
            dkp_ref[pl.ds(r0, 3 * BLOCK), hs(h)] += dkws[h]
            dvp_ref[pl.ds(r0, 3 * BLOCK), hs(h)] += dvws[h]
            dkc_ref[:, hs(h)] += dkcs[h]
            dvc_ref[:, hs(h)] += dvcs[h]
        dsink_ref[...] += dsink

    full = lambda shape: pl.BlockSpec(shape, lambda n: (0, 0))
    return pl.pallas_call(
        body, name="attn_bwd", grid=(s // BLOCK,),
        in_specs=[pl.BlockSpec((BLOCK, QW), lambda n: (n, 0))] + _attn_specs(s, c),
        out_specs=[pl.BlockSpec((BLOCK, QW), lambda n: (n, 0)), full((s + 2 * BLOCK, KVW)), full((s + 2 * BLOCK, KVW)),
                   full((c, KVW)), full((c, KVW)), full((1, ATT_HEADS))],
        out_shape=[jax.ShapeDtypeStruct((s, QW), F32), jax.ShapeDtypeStruct((s + 2 * BLOCK, KVW), F32),
                   jax.ShapeDtypeStruct((s + 2 * BLOCK, KVW), F32), jax.ShapeDtypeStruct((c, KVW), F32),
                   jax.ShapeDtypeStruct((c, KVW), F32), jax.ShapeDtypeStruct((1, ATT_HEADS), F32)],
        compiler_params=_cp("arbitrary"),
    )(do, q, kp, vp, kc, vc, sink)


GLA_GROUPS = 1
GLA_GROUP_HEADS = GLA_HEADS // GLA_GROUPS
GKG, GVG = GKW // GLA_GROUPS, GVW // GLA_GROUPS


def _gla_masks(heads=GLA_HEADS):
    hk = np.arange(heads * GLA_DK) // GLA_DK
    hv = np.arange(heads * GLA_DV) // GLA_DV
    head_k = (np.arange(heads)[:, None] == hk[None, :]).astype(np.float32)
    head_v = (np.arange(heads)[:, None] == hv[None, :]).astype(np.float32)
    bd_t = (hv[:, None] == hk[None, :]).astype(np.float32)
    return jnp.asarray(head_k), jnp.asarray(head_v), jnp.asarray(bd_t)


def _group_states(st):
    return jnp.stack([st[g * GVG:(g + 1) * GVG, g * GKG:(g + 1) * GKG] for g in range(GLA_GROUPS)])


def _ungroup_states(st):
    out = jnp.zeros((GVW, GKW), st.dtype)
    for g in range(GLA_GROUPS):
        out = out.at[g * GVG:(g + 1) * GVG, g * GKG:(g + 1) * GKG].set(st[g])
    return out


def _tri(n, rev, strict=False):
    i = lax.broadcasted_iota(jnp.int32, (n, n), 0)
    j = lax.broadcasted_iota(jnp.int32, (n, n), 1)
    if strict:
        keep = (j > i) if rev else (j < i)
    else:
        keep = (j >= i) if rev else (j <= i)
    return keep


def _f_gla_chunk(q, k, v, la, st, head_k, head_v, bd_t, rev):
    return _f_gla_carry(*_f_gla_intra(q, k, v, la, head_k, head_v, rev), v, st, bd_t)


def _f_gla_intra(q, k, v, la, head_k, head_v, rev):
    heads, kw, vw = head_k.shape[0], q.shape[1], v.shape[1]
    keep = _tri(GLA_CHUNK, rev)
    b = _nn_hi(keep.astype(F32), la)
    bl = jnp.sum(la, axis=0, keepdims=True)
    qd = q * (GLA_DK ** -0.5) * jnp.exp(b)
    ki = k * jnp.exp(-b)
    kd = k * jnp.exp(bl - b)
    q_heads = (qd[None, :, :] * head_k[:, None, :]).reshape(heads * GLA_CHUNK, kw)
    a_all = _nt(q_heads, ki).reshape(heads, GLA_CHUNK, GLA_CHUNK)
    a_all = jnp.where(keep[None, :, :], a_all, 0.0).reshape(heads * GLA_CHUNK, GLA_CHUNK)
    o_all = _nn(a_all, v).reshape(heads, GLA_CHUNK, vw)
    return jnp.sum(o_all * head_v[:, None, :], axis=0), qd, kd, bl


def _f_gla_carry(intra, qd, kd, bl, v, st, bd_t):
    return intra + _nt(qd, st), st * jnp.exp(bl) + bd_t * _tn(v, kd)


def _gla_specs(s, tb, order):
    return [pl.BlockSpec((tb, GKW), lambda i: (order(i), C_GQ // GKW)),
            pl.BlockSpec((tb, GKW), lambda i: (order(i), C_GK // GKW)),
            pl.BlockSpec((tb, GVW), lambda i: (order(i), C_GV // GVW)),
            pl.BlockSpec((tb, GKW), lambda i: (order(i), 0))]


GLA_BLOCK_CHUNKS = 4


def _gla_fwd(p, la_f, la_b, st_f0, st_b0):
    s = p.shape[0]
    tb = GLA_BLOCK_CHUNKS * GLA_CHUNK
    nblk = s // tb
    up, down = (lambda i: i), (lambda i: nblk - 1 - i)
    masks = _gla_masks(GLA_GROUP_HEADS)

    def scan(rev, q_ref, k_ref, v_ref, la_ref, o_ref, sts_ref, st_ref, consts):
        for g in range(GLA_GROUPS):
            gk, gv = slice(g * GKG, (g + 1) * GKG), slice(g * GVG, (g + 1) * GVG)
            st = st_ref[g]
            sts_ref[0, g] = st
            chunks = range(GLA_BLOCK_CHUNKS)
            for ci in (reversed(chunks) if rev else chunks):
                rows = slice(ci * GLA_CHUNK, (ci + 1) * GLA_CHUNK)
                o, st = _f_gla_chunk(q_ref[rows, gk], k_ref[rows, gk], v_ref[rows, gv], la_ref[rows, gk], st, *consts,
                                     rev)
                o_ref[rows, gv] = o
            st_ref[g] = st

    def body(qf, kf, vf, laf, qb, kb, vb, lab, stf0, stb0, hk_ref, hv_ref, bd_ref, of_ref, stsf_ref, ob_ref, stsb_ref,
             stf_ref, stb_ref):
        @pl.when(pl.program_id(0) == 0)
        def _():
            stf_ref[...] = stf0[...]
            stb_ref[...] = stb0[...]

        consts = (hk_ref[...], hv_ref[...], bd_ref[...])
        scan(False, qf, kf, vf, laf, of_ref, stsf_ref, stf_ref, consts)
        scan(True, qb, kb, vb, lab, ob_ref, stsb_ref, stb_ref, consts)

    full = lambda a: pl.BlockSpec(a.shape, lambda i: (0,) * a.ndim)
    outs = lambda order: [pl.BlockSpec((tb, GVW), lambda i: (order(i), 0)),
                          pl.BlockSpec((1, GLA_GROUPS, GVG, GKG), lambda i: (order(i), 0, 0, 0))]
    return pl.pallas_call(
        body, name="gla_fwd", grid=(nblk,),
        in_specs=_gla_specs(s, tb, up) + _gla_specs(s, tb, down) + [full(st_f0), full(st_b0)]
        + [full(m) for m in masks],
        out_specs=outs(up) + outs(down),
        out_shape=[jax.ShapeDtypeStruct((s, GVW), F32), jax.ShapeDtypeStruct((nblk, GLA_GROUPS, GVG, GKG), F32)] * 2,
        scratch_shapes=[pltpu.VMEM((GLA_GROUPS, GVG, GKG), F32)] * 2,
        compiler_params=_cp("arbitrary"),
    )(p, p, p, la_f, p, p, p, la_b, st_f0, st_b0, *masks)


def _gla_bwd(p, la_f, la_b, sts_f, sts_b, do, after=None):
    s = p.shape[0]
    tb = GLA_BLOCK_CHUNKS * GLA_CHUNK
    nblk = s // tb
    up, down = (lambda i: i), (lambda i: nblk - 1 - i)
    masks = _gla_masks(GLA_GROUP_HEADS)
    follow = () if after is None else (after,)

    def back(rev, q_ref, k_ref, v_ref, la_ref, sts_ref, do_ref, dq_ref, dk_ref, dv_ref, dla_ref, dst0_ref, dst_ref,
             consts):
        def block(q, k, v, la, st):
            outs = [None] * GLA_BLOCK_CHUNKS
            chunks = range(GLA_BLOCK_CHUNKS)
            for ci in (reversed(chunks) if rev else chunks):
                outs[ci], st = _f_gla_chunk(q[ci], k[ci], v[ci], la[ci], st, *consts, rev)
            return tuple(outs), st

        for g in range(GLA_GROUPS):
            gk, gv = slice(g * GKG, (g + 1) * GKG), slice(g * GVG, (g + 1) * GVG)
            split = lambda r, cols: tuple(r[ci * GLA_CHUNK:(ci + 1) * GLA_CHUNK, cols].astype(F32)
                                          for ci in range(GLA_BLOCK_CHUNKS))
            _, vjp = jax.vjp(block, split(q_ref, gk), split(k_ref, gk), split(v_ref, gv), split(la_ref, gk),
                             sts_ref[0, g])
            dq, dk, dv, dla, dst = vjp((split(do_ref, gv), dst_ref[g]))
            for ci in range(GLA_BLOCK_CHUNKS):
                rows = slice(ci * GLA_CHUNK, (ci + 1) * GLA_CHUNK)
                dq_ref[rows, gk], dk_ref[rows, gk], dv_ref[rows, gv], dla_ref[rows, gk] = dq[ci], dk[ci], dv[ci], dla[ci]
            dst_ref[g] = dst
            dst0_ref[g] = dst

    def body(*refs):
        ins, (hk_ref, hv_ref, bd_ref) = refs[:12], refs[12:15]
        outs = refs[15 + len(follow):]

        @pl.when(pl.program_id(0) == 0)
        def _():
            outs[10][...] = jnp.zeros_like(outs[10])
            outs[11][...] = jnp.zeros_like(outs[11])

        consts = (hk_ref[...], hv_ref[...], bd_ref[...])
        back(False, *ins[:6], *outs[:5], outs[10], consts)
        back(True, *ins[6:], *outs[5:10], outs[11], consts)

    full = lambda a: pl.BlockSpec(a.shape, lambda i: (0,) * a.ndim)

    def ins(order):
        return _gla_specs(s, tb, order) + [pl.BlockSpec((1, GLA_GROUPS, GVG, GKG), lambda i: (order(i), 0, 0, 0)),
                                           pl.BlockSpec((tb, GVW), lambda i: (order(i), 0))]

    def outs(order):
        blk = lambda w: pl.BlockSpec((tb, w), lambda i: (order(i), 0))
        return [blk(GKW), blk(GKW), blk(GVW), blk(GKW), pl.BlockSpec((GLA_GROUPS, GVG, GKG), lambda i: (0, 0, 0))]

    shapes = [jax.ShapeDtypeStruct((s, GKW), F32), jax.ShapeDtypeStruct((s, GKW), F32),
              jax.ShapeDtypeStruct((s, GVW), F32), jax.ShapeDtypeStruct((s, GKW), F32),
              jax.ShapeDtypeStruct((GLA_GROUPS, GVG, GKG), F32)]
    both = pl.pallas_call(
        body, name="gla_bwd", grid=(nblk,),
        in_specs=ins(down) + ins(up) + [full(m) for m in masks] + [pl.BlockSpec(memory_space=pl.ANY)] * len(follow),
        out_specs=outs(down) + outs(up), out_shape=shapes * 2,
        scratch_shapes=[pltpu.VMEM((GLA_GROUPS, GVG, GKG), F32)] * 2,
        compiler_params=_cp("arbitrary"),
    )(p, p, p, la_f, sts_f, do, p, p, p, la_b, sts_b, do, *masks, *follow)
    return both[:5], both[5:]


def _f_ctx_state(k, v, la_f, la_b, bd_t):
    c = k.shape[0]
    after = _nn_hi(_tri(c, True, strict=True).astype(F32), la_f)
    before = _nn_hi(_tri(c, False, strict=True).astype(F32), la_b)
    return bd_t * _tn(v, k * jnp.exp(after)), bd_t * _tn(v, k * jnp.exp(before))


def _ctx_state(pc, la_f, la_b):
    c = pc.shape[0]
    bd_t = _gla_masks()[2]

    def body(k_ref, v_ref, lf_ref, lb_ref, bd_ref, sf_ref, sb_ref):
        sf_ref[...], sb_ref[...] = _f_ctx_state(k_ref[...], v_ref[...], lf_ref[...], lb_ref[...], bd_ref[...])

    full = lambda a: pl.BlockSpec(a.shape, lambda i: (0, 0))
    return pl.pallas_call(
        body, name="ctx_state_fwd", grid=(1,),
        in_specs=[pl.BlockSpec((c, GKW), lambda i: (0, C_GK // GKW)), pl.BlockSpec((c, GVW), lambda i: (0, C_GV // GVW)),
                  full(la_f), full(la_b), full(bd_t)],
        out_specs=[pl.BlockSpec((GVW, GKW), lambda i: (0, 0))] * 2,
        out_shape=[jax.ShapeDtypeStruct((GVW, GKW), F32)] * 2,
        compiler_params=_cp("arbitrary"),
    )(pc, pc, la_f, la_b, bd_t)


def _ctx_state_bwd(pc, la_f, la_b, dsf, dsb):
    c = pc.shape[0]
    bd_t = _gla_masks()[2]

    def body(k_ref, v_ref, lf_ref, lb_ref, bd_ref, dsf_ref, dsb_ref, dk_ref, dv_ref, dlf_ref, dlb_ref):
        _, vjp = jax.vjp(lambda k, v, lf, lb: _f_ctx_state(k, v, lf, lb, bd_ref[...]),
                         k_ref[...], v_ref[...], lf_ref[...], lb_ref[...])
        dk, dv, dlf, dlb = vjp((dsf_ref[...], dsb_ref[...]))
        dk_ref[...], dv_ref[...] = dk.astype(BF16), dv.astype(BF16)
        dlf_ref[...], dlb_ref[...] = dlf, dlb

    full = lambda a: pl.BlockSpec(a.shape, lambda i: (0, 0))
    return pl.pallas_call(
        body, name="ctx_state_bwd", grid=(1,),
        in_specs=[pl.BlockSpec((c, GKW), lambda i: (0, C_GK // GKW)), pl.BlockSpec((c, GVW), lambda i: (0, C_GV // GVW)),
                  full(la_f), full(la_b), full(bd_t), full(dsf), full(dsb)],
        out_specs=[pl.BlockSpec((c, GKW), lambda i: (0, 0)), pl.BlockSpec((c, GVW), lambda i: (0, 0)),
                   pl.BlockSpec((c, GKW), lambda i: (0, 0)), pl.BlockSpec((c, GKW), lambda i: (0, 0))],
        out_shape=[jax.ShapeDtypeStruct((c, GKW), BF16), jax.ShapeDtypeStruct((c, GVW), BF16),
                   jax.ShapeDtypeStruct((c, GKW), F32), jax.ShapeDtypeStruct((c, GKW), F32)],
        compiler_params=_cp("arbitrary"),
    )(pc, pc, la_f, la_b, bd_t, dsf, dsb)


_SRC_COLS = ((0, QW), (QW + 2 * KVW + 2 * GKW, GVW), (QW + 2 * KVW + 2 * GKW + GVW, GVW), (QW, KVW), (QW + KVW, KVW),
             (QW + 2 * KVW, GKW), (QW + 2 * KVW + GKW, GKW), (IN_COLS - 2 * GATE_RANK, 2 * GATE_RANK))
_DST_COLS = (C_Q, C_GV, C_GG, C_K, C_V, C_GQ, C_GK, C_Z)


def _pack_w_in(w_in):
    parts = [w_in[:, s:s + n] for s, n in _SRC_COLS]
    parts.append(jnp.zeros((w_in.shape[0], IN_PAD - C_Z - 2 * GATE_RANK), w_in.dtype))
    return jnp.concatenate(parts, axis=1)


def _unpack_w_in_grad(g):
    by_src = sorted(zip(_SRC_COLS, _DST_COLS))
    return jnp.concatenate([g[:, d:d + n] for (_, n), d in by_src], axis=1)


def _prep_gate_weights(w_gate_fwd, w_gate_bwd):
    pad_rows = lambda w, at: jnp.zeros((LANES, GKW), F32).at[at:at + GATE_RANK].set(w)
    return {"wg_f": pad_rows(w_gate_fwd, 0), "wg_b": pad_rows(w_gate_bwd, GATE_RANK)}


def _local_step(x, ctx, target, ada, ada_c, w, late_weights, reduce_behind=None, reduce_w_in=None):
    s, d = x.shape
    sh1, sc1, gt1, sh2, sc2, gt2 = [ada[:, i * d:(i + 1) * d] for i in range(6)]
    sh1c, sc1c = ada_c[:, :d], ada_c[:, d:2 * d]
    cos, sin = _rope_tables(s)
    gt = jnp.tile(w["g_gla_norm"], (1, GLA_HEADS))

    h = _norm_mod("pre_mix", x, w["g_pre_mix"], sh1, sc1)
    hc = _norm_mod("pre_mix_ctx", ctx, w["g_pre_mix"], sh1c, sc1c)
    w_in, token = w["w_in"](h, cos, sin)
    p = _mm("proj_in", h, w_in, "nn", after=token)
    pc = _mm("proj_in_ctx", hc, w_in, "nn")
    q_rot, k_rot, v_b = _rope_fwd("rope", p, cos, sin)
    pad = ((BLOCK, BLOCK), (0, 0))
    kp, vp = jnp.pad(k_rot, pad), jnp.pad(v_b, pad)
    kc, vc = pc[:, C_K:C_K + KVW].astype(BF16), pc[:, C_V:C_V + KVW].astype(BF16)
    attn = _attn_fwd(q_rot, kp, vp, kc, vc, w["attn_sink"])
    gate_w = (w["wg_f"], w["wg_b"], w["b_gate_fwd"], w["b_gate_bwd"])
    la_f, la_b = _gate_fwd("gate", p, *gate_w)
    la_fc, la_bc = _gate_fwd("gate_ctx", pc, *gate_w)
    st_f0, st_b0 = _ctx_state(pc, la_fc, la_bc)
    o_f, sts_f, o_b, sts_b = _gla_fwd(p, la_f, la_b, _group_states(st_f0), _group_states(st_b0))
    mix = _gla_out("gla_out", attn, o_f, o_b, p, gt)
    w_out, w_ffn_in_t, w_ffn_out = late_weights(attn)
    y = _mm("proj_out", mix, w_out, "nn", BF16)
    x1, h2 = _post_res_norm_mod("post_mix_pre_ffn", x, y, w["g_post_mix"], gt1, w["g_pre_ffn"], sh2, sc2)
    u, a = _ffn_in_swiglu("ffn_in", h2, w_ffn_in_t)
    f = _mm("ffn_out", a, w_ffn_out, "nn", BF16)
    g = {}
    dx2, df, loss, g["g_post_ffn"], dgt2 = _post_res_loss("post_ffn_loss", x1, f, w["g_post_ffn"], gt2, target)

    g["w_ffn_out"] = _mm("ffn_out_dw", a, df, "tn")
    du = _ffn_out_dx_swiglu_bwd("ffn_out_dx", df, w_ffn_out, u)
    dh2 = _mm("ffn_in_dx", du, w_ffn_in_t, "nn")
    g["w_ffn_in_t"] = _mm("ffn_in_dw", du, h2, "tn")
    dx1, dy, g["g_pre_ffn"], dsh2, dsc2, g["g_post_mix"], dgt1 = _norm_mod_post_res_bwd(
        "pre_ffn_post_mix_bwd", dh2, dx2, x1, y, w["g_pre_ffn"], sh2, sc2, w["g_post_mix"], gt1)
    dmix = _mm("proj_out_dx", dy, w_out, "nt", BF16)
    g["w_out"] = _mm("proj_out_dw", mix, dy, "tn")
    rb, sink, token = reduce_behind, w["attn_sink"], None
    if rb is not None:
        gt = _behind(gt, rb.start(g["w_ffn_in_t"], g["w_ffn_out"], g["w_out"]))
    d_o, dgg, dgt = _gla_out_bwd("gla_out_bwd", dmix, o_f, o_b, p, gt)
    g["g_gla_norm"] = jnp.sum(dgt.reshape(GLA_HEADS, GLA_DV), axis=0, keepdims=True)
    if rb is not None:
        token = rb.pair(dgg)
    gla_f, gla_b = _gla_bwd(p, la_f, la_b, sts_f, sts_b, d_o, token)
    (dla_f, dst_f0), (dla_b, dst_b0) = gla_f[3:], gla_b[3:]
    dst_f0, dst_b0 = _ungroup_states(dst_f0), _ungroup_states(dst_b0)
    if rb is not None:
        sink = _behind(sink, rb.total(dla_b))
    dgkc, dgvc, dla_fc, dla_bc = _ctx_state_bwd(pc, la_fc, la_bc, dst_f0, dst_b0)
    dz, dwf, dwb, dbf, dbb = _gate_bwd("gate_bwd", p, dla_f, dla_b, *gate_w)
    dzc, dwfc, dwbc, dbfc, dbbc = _gate_bwd("gate_ctx_bwd", pc, dla_fc, dla_bc, *gate_w)
    g["w_gate_fwd"] = (dwf + dwfc)[:GATE_RANK]
    g["w_gate_bwd"] = (dwb + dwbc)[GATE_RANK:2 * GATE_RANK]
    g["b_gate_fwd"], g["b_gate_bwd"] = dbf + dbfc, dbb + dbbc
    dq_rot, dkp, dvp, dkc, dvc, g["attn_sink"] = _attn_bwd(dmix, q_rot, kp, vp, kc, vc, sink)
    if rb is not None:
        g["behind"] = rb.result(dq_rot)
    dp = _proj_grad("proj_grad", dq_rot, dkp[BLOCK:BLOCK + s], dvp[BLOCK:BLOCK + s], cos, sin, gla_f[:3], gla_b[:3],
                    dgg, dz)
    c_rows = ctx.shape[0]
    zeros = lambda n: jnp.zeros((c_rows, n), BF16)
    dpc = jnp.concatenate([zeros(QW), dgvc, zeros(GVW), dkc.astype(BF16), dvc.astype(BF16), zeros(GKW), dgkc, dzc],
                          axis=1)
    g["w_in"] = _mm("proj_in_dw", h, dp, "tn", init=_mm("proj_in_ctx_dw", hc, dpc, "tn"))
    token = None if reduce_w_in is None else reduce_w_in.start(g["w_in"])
    dh = _mm("proj_in_dx", dp, w_in, "nt", after=token)
    dhc = _mm("proj_in_ctx_dx", dpc, w_in, "nt")
    if reduce_w_in is not None:
        sh1 = _behind(sh1, reduce_w_in.pair(dh))
    dx, dg_a, dsh1, dsc1 = _norm_mod_bwd("pre_mix_bwd", dh, dx1, x, w["g_pre_mix"], sh1, sc1)
    if reduce_w_in is not None:
        dsh1 = _behind(dsh1, reduce_w_in.total(dx))
    _, dg_b, dsh1c, dsc1c = _norm_mod_bwd("pre_mix_ctx_bwd", dhc, jnp.zeros_like(dhc), ctx, w["g_pre_mix"], sh1c,
                                          sc1c)
    g["g_pre_mix"] = dg_a + dg_b
    d_ada = jnp.concatenate([dsh1, dsc1, dgt1, dsh2, dsc2, dgt2], axis=1)
    d_ada_c = jnp.concatenate([dsh1c, dsc1c, jnp.zeros((1, 4 * d), F32)], axis=1)
    return loss, dx, g, d_ada, d_ada_c


HBM = pl.BlockSpec(memory_space=pltpu.HBM)
N_DEV, N_CHIP = 8, 4


def _place():
    x, y, c = lax.axis_index("x"), lax.axis_index("y"), lax.axis_index("c")
    return x, y, c, [(1 - x, y), (x, 1 - y), (1 - x, 1 - y)]


def _row_tile(n, mult, cap):
    return max(t for t in range(mult, min(n, cap) + 1, mult) if n % t == 0)


def _ag_small(name, v, after=None):
    follow = () if after is None else (after,)

    def body(v_ref, *rest):
        out_ref, send_sems, recv_sems = rest[len(follow):]
        x, y, c, _ = _place()
        out_ref[4 * x + 2 * y + c] = v_ref[...]

        def peer(r):
            return ((1 - x) if r & 4 else x, (1 - y) if r & 2 else y, (1 - c) if r & 1 else c)

        def copy(r, block):
            px, py, pc = block
            return pltpu.make_async_remote_copy(
                src_ref=v_ref, dst_ref=out_ref.at[4 * px + 2 * py + pc], send_sem=send_sems.at[r - 1],
                recv_sem=recv_sems.at[r - 1], device_id=peer(r), device_id_type=MESH)

        sends = [copy(r, (x, y, c)) for r in range(1, N_DEV)]
        for cp in sends:
            cp.start()
        for r in range(1, N_DEV):
            copy(r, peer(r)).wait_recv()
        for cp in sends:
            cp.wait_send()

    return pl.pallas_call(
        body, name=name, out_shape=jax.ShapeDtypeStruct((N_DEV,) + v.shape, v.dtype),
        in_specs=[pl.BlockSpec(memory_space=pltpu.VMEM)] + [pl.BlockSpec(memory_space=pl.ANY)] * len(follow),
        out_specs=pl.BlockSpec(memory_space=pltpu.VMEM),
        scratch_shapes=[pltpu.SemaphoreType.DMA((N_DEV - 1,)), pltpu.SemaphoreType.DMA((N_DEV - 1,))],
    )(v, *follow)


def _halves(c, rows, mult):
    hr = rows // 2
    return pl.ds(pl.multiple_of(c * hr, mult), hr), pl.ds(pl.multiple_of((1 - c) * hr, mult), hr)


def _add_half(name, g, a, c_idx):
    n_sh, hr, n = a.shape
    tr = _row_tile(hr, 16, 1024)
    nb = hr // tr

    def body(c_ref, g_ref, a_ref, o_ref):
        o_ref[...] = (g_ref[...] + a_ref[...]).astype(o_ref.dtype)

    return pl.pallas_call(
        body, name=name, out_shape=jax.ShapeDtypeStruct(a.shape, BF16),
        grid_spec=pltpu.PrefetchScalarGridSpec(
            num_scalar_prefetch=1, grid=(n_sh, nb),
            in_specs=[pl.BlockSpec((1, tr, n), lambda s, i, c_ref: (s, c_ref[0] * nb + i, 0)),
                      pl.BlockSpec((1, tr, n), lambda s, i, c_ref: (s, i, 0))],
            out_specs=pl.BlockSpec((1, tr, n), lambda s, i, c_ref: (s, i, 0))),
        compiler_params=_cp("parallel", "parallel"),
    )(c_idx, g, a)


def _sum_chips(name, b):
    n_sh, hr, n = b.shape
    tr = _row_tile(hr, 16, 1024)

    def body(b0, b1, b2, b3, o_ref):
        o_ref[...] = ((b0[0].astype(F32) + b1[0].astype(F32)) + b2[0].astype(F32)) + b3[0].astype(F32)

    return pl.pallas_call(
        body, name=name, grid=(hr // tr,), out_shape=jax.ShapeDtypeStruct((hr, n), F32),
        in_specs=[pl.BlockSpec((1, tr, n), functools.partial(lambda i, k: (k, i, 0), k=k)) for k in range(n_sh)],
        out_specs=pl.BlockSpec((tr, n), lambda i: (i, 0)), compiler_params=_cp("parallel"),
    )(b, b, b, b)


SEM = pl.BlockSpec(memory_space=pltpu.SEMAPHORE)
ANY = pl.BlockSpec(memory_space=pl.ANY)
DATAFLOW = pltpu.SideEffectType.DATAFLOW_SIDE_EFFECTING


def _remote(src, dst, send_sems, recv_sems, k, to):
    return pltpu.make_async_remote_copy(src_ref=src, dst_ref=dst, send_sem=send_sems.at[k], recv_sem=recv_sems.at[k],
                                        device_id=to, device_id_type=MESH)


def _split_copy(name, src, land_shape, land_dtype, n, plan, after=None):
    after = jnp.zeros((8, LANES), F32) if after is None else after

    def start_body(src_ref, land_ref, after_ref, send_sems, recv_sems, src_thru, land_thru, token):
        for cp in plan(src_ref, land_ref, send_sems, recv_sems)[0]:
            cp.start()
        token[...] = jnp.zeros_like(token)

    sems = pltpu.SemaphoreType.DMA((n,))
    send_sems, recv_sems, src_thru, land_thru, token = pl.pallas_call(
        start_body, name=name + "_start",
        out_shape=(sems, sems, pltpu.HBM(src.shape, src.dtype), pltpu.HBM(land_shape, land_dtype),
                   jax.ShapeDtypeStruct((8, LANES), F32)),
        in_specs=(HBM, HBM, ANY), out_specs=(SEM, SEM, HBM, HBM, pl.BlockSpec(memory_space=pltpu.VMEM)),
        input_output_aliases={0: 2, 1: 3}, compiler_params=pltpu.CompilerParams(has_side_effects=DATAFLOW),
    )(pltpu.with_memory_space_constraint(src, pltpu.HBM),
      pltpu.with_memory_space_constraint(lax.empty(land_shape, land_dtype), pltpu.HBM), after)

    def wait(*after):
        def wait_body(src_ref, land_ref, send_sems, recv_sems, *rest):
            sent, received = plan(src_ref, land_ref, send_sems, recv_sems)
            for cp in sent:
                cp.wait_send()
            for cp in received:
                cp.wait_recv()

        return pl.pallas_call(
            wait_body, name=name + "_wait",
            out_shape=(pltpu.HBM(src.shape, src.dtype), pltpu.HBM(land_shape, land_dtype)),
            in_specs=(HBM, HBM, SEM, SEM) + (ANY,) * len(after), out_specs=(HBM, HBM),
            input_output_aliases={0: 0, 1: 1}, compiler_params=pltpu.CompilerParams(has_side_effects=DATAFLOW),
        )(src_thru, land_thru, send_sems, recv_sems, *after)

    return token, wait


def _behind(x, token):
    return x + token[0, 0]


def _plan_gather(src_ref, land_ref, send_sems, recv_sems):
    x, y, c, chips = _place()
    sent = [_remote(src_ref, land_ref.at[2 * x + y], send_sems, recv_sems, j, (px, py, c))
            for j, (px, py) in enumerate(chips)]
    received = [_remote(src_ref, land_ref.at[2 * px + py], send_sems, recv_sems, j, (px, py, c))
                for j, (px, py) in enumerate(chips)]
    return sent, received


def _plan_swap(src_ref, land_ref, send_sems, recv_sems):
    x, y, c, _ = _place()
    _, other_half = _halves(c, src_ref.shape[1], 8)
    cp = _remote(src_ref.at[pl.ds(0, src_ref.shape[0]), other_half], land_ref, send_sems, recv_sems, 0, (x, y, 1 - c))
    return [cp], [cp]


def _plan_scatter(src_ref, land_ref, send_sems, recv_sems):
    x, y, c, chips = _place()
    sent = [_remote(src_ref.at[2 * px + py], land_ref.at[2 * x + y], send_sems, recv_sems, j, (px, py, c))
            for j, (px, py) in enumerate(chips)]
    received = [_remote(src_ref.at[2 * px + py], land_ref.at[2 * px + py], send_sems, recv_sems, j, (px, py, c))
                for j, (px, py) in enumerate(chips)]
    return sent, received


def _plan_share(src_ref, land_ref, send_sems, recv_sems):
    x, y, c, _ = _place()
    mine_half, other_half = _halves(c, land_ref.shape[0], 8)
    return ([_remote(src_ref, land_ref.at[mine_half], send_sems, recv_sems, 0, (x, y, 1 - c))],
            [_remote(src_ref, land_ref.at[other_half], send_sems, recv_sems, 0, (x, y, 1 - c))])


def _pack_shard_rows(name, parts):
    rows = [t.shape[0] // N_CHIP for t in parts]
    n, total = parts[0].shape[1], sum(t.shape[0] // N_CHIP for t in parts)
    slab, at = None, 0
    for i, (t, r) in enumerate(zip(parts, rows)):
        tr = max(c for c in range(8, min(r, 512) + 1, 8) if r % c == 0 and at % c == 0)
        nb, ob = r // tr, at // tr

        def body(t_ref, *rest):
            rest[-1][0] = t_ref[...]

        slab = pl.pallas_call(
            body, name=f"{name}_{i}", grid=(N_CHIP, nb), out_shape=jax.ShapeDtypeStruct((N_CHIP, total, n), t.dtype),
            in_specs=[pl.BlockSpec((tr, n), functools.partial(lambda k, j, nb: (k * nb + j, 0), nb=nb))]
            + ([] if slab is None else [pl.BlockSpec(memory_space=pl.ANY)]),
            out_specs=pl.BlockSpec((1, tr, n), functools.partial(lambda k, j, ob: (k, ob + j, 0), ob=ob)),
            input_output_aliases={} if slab is None else {1: 0}, compiler_params=_cp("parallel", "parallel"),
        )(*((t,) if slab is None else (t, slab)))
        at += r
    return slab


class _GatherBehind:
    def __init__(self, name, shard, chip, after=None):
        self.chip = chip
        self.token, self.wait = _split_copy(name, shard, (N_CHIP,) + shard.shape, shard.dtype, 3, _plan_gather,
                                            after)

    def result(self, *after):
        shard, land = self.wait(*after)
        return lax.dynamic_update_slice(land, shard[None], (self.chip, 0, 0))


class _ReduceBehind:
    def __init__(self, name, chip, c, c_idx):
        self.name, self.chip, self.c, self.c_idx = name, chip, c, c_idx

    def start(self, *grads):
        return self.start_slab(_pack_shard_rows(self.name + "_pack", grads))

    def start_slab(self, g):
        n_sh, rows, n = g.shape
        token, self.wait = _split_copy(self.name + "_swap", g, (n_sh, rows // 2, n), g.dtype, 1, _plan_swap)
        return token

    def pair(self, after):
        g, a = self.wait(after)
        h = _add_half(self.name + "_pair", g, a, self.c_idx)
        token, self.wait = _split_copy(self.name + "_scatter", h, h.shape, h.dtype, 3, _plan_scatter)
        return token

    def total(self, after):
        h, b = self.wait(after)
        b = lax.dynamic_update_slice(b, lax.dynamic_slice_in_dim(h, self.chip, 1, axis=0), (self.chip, 0, 0))
        f = _sum_chips(self.name + "_sum", b)
        token, self.wait = _split_copy(self.name + "_share", f, (2 * f.shape[0], f.shape[1]), f.dtype, 1,
                                       _plan_share)
        return token

    def result(self, after):
        f, out = self.wait(after)
        return lax.dynamic_update_slice(out, f, (self.c * f.shape[0], 0))


class _ReduceColsBehind(_ReduceBehind):
    def start(self, g_padded):
        g = _unpack_w_in_grad(g_padded)
        n = g.shape[1] // N_CHIP
        return self.start_slab(jnp.stack([g[:, k * n:(k + 1) * n] for k in range(N_CHIP)]))


def _f_adamw(w, g, m, v):
    m = ADAM_B1 * m + (1.0 - ADAM_B1) * g
    v = ADAM_B2 * v + (1.0 - ADAM_B2) * (g * g)
    m_hat = m / (1.0 - ADAM_B1 ** ADAM_STEP)
    v_hat = v / (1.0 - ADAM_B2 ** ADAM_STEP)
    return -ADAM_LR * (m_hat / (jnp.sqrt(v_hat) + ADAM_EPS) + ADAM_WD * w), m, v


def _adamw(name, w, g, m, v):
    rows, n = w.shape
    return _rowwise(name, lambda w, g, m, v: (_f_adamw(w, g, m, v), ()), rows, [(t, n, 0) for t in (w, g, m, v)], [],
                    [(n, F32)] * 3, [], tm=_row_tile(rows, 8, 256))


def _pack_rows(parts):
    rows = []
    for t in parts:
        t = t.reshape(-1)
        rows.append(jnp.pad(t, (0, -t.shape[0] % LANES)).reshape(-1, LANES))
    out = jnp.concatenate(rows, axis=0)
    return jnp.pad(out, ((0, -out.shape[0] % 8), (0, 0)))


def _unpack_rows(packed, shapes):
    out, r = [], 0
    for shp in shapes:
        n = int(np.prod(shp))
        nr = -(-n // LANES)
        out.append(packed[r:r + nr].reshape(-1)[:n].reshape(shp))
        r += nr
    return out


def _sum_blocks(name, g):
    def body(g_ref, o_ref):
        acc = g_ref[0]
        for k in range(1, g.shape[0]):
            acc = acc + g_ref[k]
        o_ref[...] = acc

    return pl.pallas_call(body, name=name, out_shape=jax.ShapeDtypeStruct(g.shape[1:], F32))(g)


def _silu(t):
    return t * _sigmoid(t)


def _ada_fwd(cc, w_ada):
    n = w_ada.shape[1]
    tn = _row_tile(n, LANES, 512)

    def body(cc_ref, w_ref, o_ref):
        o_ref[...] = _nn(_silu(cc_ref[...]), w_ref[...])

    return pl.pallas_call(
        body, name="ada_fwd", grid=(n // tn,), out_shape=jax.ShapeDtypeStruct((cc.shape[0], n), F32),
        in_specs=[pl.BlockSpec(cc.shape, lambda j: (0, 0)), pl.BlockSpec((w_ada.shape[0], tn), lambda j: (0, j))],
        out_specs=pl.BlockSpec((cc.shape[0], tn), lambda j: (0, j)), compiler_params=_cp("parallel"),
    )(cc, w_ada)


def _ada_bwd(cc, dm, w_ada):
    d, n = w_ada.shape
    tn = _row_tile(n, LANES, 512)

    def body(cc_ref, dm_ref, w_ref, gw_ref, ds_ref):
        @pl.when(pl.program_id(0) == 0)
        def _():
            ds_ref[...] = jnp.zeros_like(ds_ref)

        gw_ref[...] = _raw_dot("tn", _silu(cc_ref[...]), dm_ref[...], True)
        ds_ref[...] += _raw_dot("nt", dm_ref[...], w_ref[...], False)

    return pl.pallas_call(
        body, name="ada_bwd", grid=(n // tn,),
        out_shape=[jax.ShapeDtypeStruct((d, n), F32), jax.ShapeDtypeStruct(cc.shape, F32)],
        in_specs=[pl.BlockSpec(cc.shape, lambda j: (0, 0)), pl.BlockSpec((cc.shape[0], tn), lambda j: (0, j)),
                  pl.BlockSpec((d, tn), lambda j: (0, j))],
        out_specs=[pl.BlockSpec((d, tn), lambda j: (0, j)), pl.BlockSpec(cc.shape, lambda j: (0, 0))],
        compiler_params=_cp("arbitrary"),
    )(cc, dm, w_ada)


def _c_ctx_grad(parts, c_ctx):
    def body(p_ref, c_ref, o_ref):
        ds = ((p_ref[0] + p_ref[1]) + p_ref[2]) + p_ref[3]
        _, vjp = jax.vjp(_silu, c_ref[...])
        o_ref[...] = vjp(ds)[0]

    return pl.pallas_call(body, name="c_ctx_grad", out_shape=jax.ShapeDtypeStruct(c_ctx.shape, F32))(parts, c_ctx)


def kernel(x, c, ctx, c_ctx, w_ada, b_ada, g_pre_mix, g_post_mix, g_pre_ffn, g_post_ffn, w_in, attn_sink, w_gate_fwd, b_gate_fwd, w_gate_bwd, b_gate_bwd, g_gla_norm, w_out, w_ffn_in, w_ffn_out, loss_target, m_c_ctx, m_w_ada, m_b_ada, m_g_pre_mix, m_g_post_mix, m_g_pre_ffn, m_g_post_ffn, m_w_in, m_attn_sink, m_w_gate_fwd, m_b_gate_fwd, m_w_gate_bwd, m_b_gate_bwd, m_g_gla_norm, m_w_out, m_w_ffn_in, m_w_ffn_out, v_c_ctx, v_w_ada, v_b_ada, v_g_pre_mix, v_g_post_mix, v_g_pre_ffn, v_g_post_ffn, v_w_in, v_attn_sink, v_w_gate_fwd, v_b_gate_fwd, v_w_gate_bwd, v_b_gate_bwd, v_g_gla_norm, v_w_out, v_w_ffn_in, v_w_ffn_out):
    xi, yi, ci = lax.axis_index("x"), lax.axis_index("y"), lax.axis_index("c")
    dev, chip = 4 * xi + 2 * yi + ci, 2 * xi + yi
    c_idx = jnp.reshape(ci, (1,)).astype(jnp.int32)
    d = x.shape[-1]
    n_ada, n_in, n_f = w_ada.shape[-1], w_in.shape[-1], w_ffn_in.shape[-1]
    r_out, r_f = w_out.shape[1], w_ffn_out.shape[1]
    n_gate = w_gate_fwd.shape[-1]
    by_chip = lambda t: t[0::2]

    rc = -(-d // LANES)
    g1 = _ag_small("gather_cond", _pack_rows([c[0], w_gate_fwd[0], w_gate_bwd[0]]))
    c_all = g1[:, :rc].reshape(N_DEV, -1)[:, :d]
    gr = GATE_RANK * n_gate // LANES
    gate_full = lambda off: jnp.transpose(by_chip(g1)[:, off:off + gr].reshape(N_CHIP, GATE_RANK, n_gate),
                                          (1, 0, 2)).reshape(GATE_RANK, N_CHIP * n_gate)
    wgf, wgb = gate_full(rc), gate_full(rc + gr)
    cc = jnp.concatenate([c_all, c_ctx[None, :], jnp.zeros((7, d), F32)], axis=0)

    g2 = _ag_small("gather_ada", _ada_fwd(cc, w_ada[0]).reshape(-1, LANES))
    ada_all = jnp.transpose(by_chip(g2).reshape(N_CHIP, 16, n_ada), (1, 0, 2)).reshape(16, N_CHIP * n_ada) + b_ada
    first = _GatherBehind("gather_w_in", w_in[0].astype(BF16), chip, g2)
    late_slab = jnp.concatenate([w_out[0], w_ffn_out[0], jnp.transpose(w_ffn_in[0])], axis=0).astype(BF16)
    late = []

    def first_weights(*after):
        w_in_g = first.result(*after, late_slab)
        late.append(_GatherBehind("gather_late", late_slab, chip, w_in_g))
        return _pack_w_in(jnp.concatenate([w_in_g[k] for k in range(N_CHIP)], axis=1)), late[0].token

    def late_weights(after):
        t = late[0].result(after)
        r1, r2 = r_out, r_out + r_f
        return (t[:, :r1].reshape(N_CHIP * r_out, d), t[:, r2:].reshape(N_CHIP * n_f, d),
                t[:, r1:r2].reshape(N_CHIP * r_f, d))

    ada_all = _behind(ada_all, first.token)
    ada = lax.dynamic_slice(ada_all, (dev, 0), (1, N_CHIP * n_ada))
    ada_c = ada_all[N_DEV:N_DEV + 1]

    w = _prep_gate_weights(wgf, wgb)
    w.update(w_in=first_weights, g_pre_mix=g_pre_mix, g_post_mix=g_post_mix, g_pre_ffn=g_pre_ffn, g_post_ffn=g_post_ffn,
             attn_sink=attn_sink, b_gate_fwd=b_gate_fwd, b_gate_bwd=b_gate_bwd, g_gla_norm=g_gla_norm)

    reduce_behind = _ReduceBehind("reduce_late", chip, ci, c_idx)
    reduce_w_in = _ReduceColsBehind("reduce_w_in", chip, ci, c_idx)
    loss_lanes, grad_x, g, d_ada, d_ada_c = _local_step(x[0], ctx[0], loss_target[0], ada, ada_c, w, late_weights,
                                                        reduce_behind, reduce_w_in)

    small = ("g_pre_mix", "g_post_mix", "g_pre_ffn", "g_post_ffn", "attn_sink", "b_gate_fwd", "b_gate_bwd",
             "g_gla_norm", "w_gate_fwd", "w_gate_bwd")
    shapes = [(1, 6 * d)] * 2 + [g[n].shape for n in small] + [(1, LANES)]
    g3 = _ag_small("gather_small_grads", _pack_rows([d_ada, d_ada_c] + [g[n] for n in small] + [loss_lanes]))
    tot = dict(zip(("d_ada", "d_ada_c") + small + ("loss",),
                   _unpack_rows(_sum_blocks("sum_small_grads", g3), shapes)))
    r_ada = 6 * d // LANES
    dm = jnp.concatenate([g3[:, :r_ada].reshape(N_DEV, 6 * d), tot["d_ada_c"], jnp.zeros((7, 6 * d), F32)], axis=0)
    grads = {n: tot[n] for n in small[:8]}
    grads["b_ada"] = _sum_blocks("sum_b_ada", dm.reshape(16, r_ada, LANES)).reshape(1, 6 * d)
    grads["w_gate_fwd"] = lax.dynamic_slice(tot["w_gate_fwd"], (0, chip * n_gate), (GATE_RANK, n_gate))[None]
    grads["w_gate_bwd"] = lax.dynamic_slice(tot["w_gate_bwd"], (0, chip * n_gate), (GATE_RANK, n_gate))[None]
    gw_ada, dsc = _ada_bwd(cc, lax.dynamic_slice(dm, (0, chip * n_ada), (16, n_ada)), w_ada[0])
    grads["w_ada"] = gw_ada[None]
    g4 = _ag_small("gather_c_ctx", _pack_rows([dsc[N_DEV]]))
    grads["c_ctx"] = _c_ctx_grad(by_chip(g4), _pack_rows([c_ctx])).reshape(-1)[:d]

    grads["w_in"] = reduce_w_in.result(g4)[None]
    behind = g["behind"]
    grads["w_ffn_in"] = jnp.transpose(behind[:n_f])[None]
    grads["w_ffn_out"], grads["w_out"] = behind[None, n_f:n_f + r_f], behind[None, n_f + r_f:]

    names = ("c_ctx", "w_ada", "b_ada", "g_pre_mix", "g_post_mix", "g_pre_ffn", "g_post_ffn", "w_in", "attn_sink",
             "w_gate_fwd", "b_gate_fwd", "w_gate_bwd", "b_gate_bwd", "g_gla_norm", "w_out", "w_ffn_in", "w_ffn_out")
    weights = dict(zip(names, (c_ctx, w_ada, b_ada, g_pre_mix, g_post_mix, g_pre_ffn, g_post_ffn, w_in, attn_sink,
                               w_gate_fwd, b_gate_fwd, w_gate_bwd, b_gate_bwd, g_gla_norm, w_out, w_ffn_in,
                               w_ffn_out)))
    m_in = dict(zip(names, (m_c_ctx, m_w_ada, m_b_ada, m_g_pre_mix, m_g_post_mix, m_g_pre_ffn, m_g_post_ffn, m_w_in,
                            m_attn_sink, m_w_gate_fwd, m_b_gate_fwd, m_w_gate_bwd, m_b_gate_bwd, m_g_gla_norm,
                            m_w_out, m_w_ffn_in, m_w_ffn_out)))
    v_in = dict(zip(names, (v_c_ctx, v_w_ada, v_b_ada, v_g_pre_mix, v_g_post_mix, v_g_pre_ffn, v_g_post_ffn, v_w_in,
                            v_attn_sink, v_w_gate_fwd, v_b_gate_fwd, v_w_gate_bwd, v_b_gate_bwd, v_g_gla_norm,
                            v_w_out, v_w_ffn_in, v_w_ffn_out)))
    large = ("w_ada", "w_in", "w_out", "w_ffn_in", "w_ffn_out")
    tiny = tuple(n for n in names if n not in large)
    delta, new_m, new_v = {}, {}, {}
    for n in large:
        dl, nm, nv = _adamw("adamw_" + n, weights[n][0], grads[n][0], m_in[n][0], v_in[n][0])
        delta[n], new_m[n], new_v[n] = dl[None], nm[None], nv[None]
    tiny_shapes = [weights[n].shape for n in tiny]
    packed = [_pack_rows([t[n] for n in tiny]) for t in (weights, grads, m_in, v_in)]
    for out, res in zip((delta, new_m, new_v), _adamw("adamw_small", *packed)):
        out.update(zip(tiny, _unpack_rows(res, tiny_shapes)))
    for n in tiny:
        grads[n] = grads[n].reshape(weights[n].shape)

    return (tot["loss"][0, 0], grad_x[None], *[grads[n] for n in names], *[delta[n] for n in names], *[new_m[n] for n in names],
            *[new_v[n] for n in names])
```

```python
import functools

import jax
import jax.numpy as jnp
import numpy as np
from jax import lax
from jax.experimental import pallas as pl
from jax.experimental.pallas import tpu as pltpu

F32 = jnp.float32
BF16 = jnp.bfloat16
MESH = pl.DeviceIdType.MESH

HEAD_DIM = 64
ATT_HEADS = 8
ATT_KV_HEADS = 2
ATT_GROUP = ATT_HEADS // ATT_KV_HEADS
WINDOW = 128
BLOCK = 128
GRID_W = 64
ROPE_BASE = 10000.0
GLA_HEADS = 8
GLA_DK = 32
GLA_DV = 64
GLA_CHUNK = 64
GATE_RANK = 16
GATE_TAU = 16.0
NEG_INF = -1e30
QW = ATT_HEADS * HEAD_DIM
KVW = ATT_KV_HEADS * HEAD_DIM
GKW = GLA_HEADS * GLA_DK
GVW = GLA_HEADS * GLA_DV
IN_COLS = QW + 2 * KVW + 2 * GKW + 2 * GVW + 2 * GATE_RANK
LANES = 128
IN_PAD = IN_COLS + LANES - 2 * GATE_RANK
C_Q, C_GV, C_GG = 0, QW, QW + GVW
C_K = C_GG + GVW
C_V = C_K + KVW
C_GQ = C_V + KVW
C_GK = C_GQ + GKW
C_Z = C_GK + GKW
MIX = QW + GVW

ADAM_LR, ADAM_B1, ADAM_B2, ADAM_EPS, ADAM_WD, ADAM_STEP = 0.001, 0.9, 0.999, 1e-08, 0.01, 10

VMEM_LIMIT = 56 * 1024 * 1024


def _cp(*sem):
    return pltpu.CompilerParams(dimension_semantics=sem, vmem_limit_bytes=VMEM_LIMIT)


def _pick(n, cands):
    for t in cands:
        if n % t == 0:
            return t
    return n


_DIMS = {"nn": (((1,), (0,)), ((), ())), "nt": (((1,), (1,)), ((), ())), "tn": (((0,), (0,)), ((), ()))}


def _raw_dot(mode, a, b, hi):
    dot = lambda u, v: lax.dot_general(u, v, _DIMS[mode], preferred_element_type=F32)
    if hi:
        a, b = a.astype(F32), b.astype(F32)
        a_hi, b_hi = a.astype(BF16), b.astype(BF16)
        a_lo, b_lo = (a - a_hi.astype(F32)).astype(BF16), (b - b_hi.astype(F32)).astype(BF16)
        return dot(a_hi, b_hi) + (dot(a_lo, b_hi) + dot(a_hi, b_lo))
    return dot(a.astype(BF16), b.astype(BF16))


def _make_dot(mode, hi):
    @jax.custom_vjp
    def dot(a, b):
        return _raw_dot(mode, a, b, hi)

    def fwd(a, b):
        return _raw_dot(mode, a, b, hi), (a, b)

    def bwd(res, dc):
        a, b = res
        if mode == "nn":
            return _raw_dot("nt", dc, b, hi), _raw_dot("tn", a, dc, hi)
        if mode == "nt":
            return _raw_dot("nn", dc, b, hi), _raw_dot("tn", dc, a, hi)
        return _raw_dot("nt", b, dc, hi), _raw_dot("nn", a, dc, hi)

    dot.defvjp(fwd, bwd)
    return dot


_nn, _nt, _tn = _make_dot("nn", False), _make_dot("nt", False), _make_dot("tn", False)
_nn_hi = _make_dot("nn", True)


MM_VMEM_BUDGET = 44 * 1024 * 1024


def _halvings(n):
    out = [n]
    while out[-1] % (2 * LANES) == 0:
        out.append(out[-1] // 2)
    return out


def _mm_tiles(mode, m, n, k, a_bytes, b_bytes, o_bytes, init_bytes=0):
    tms = [t for t in dict.fromkeys((m, m // 2, m // 4, 2048, 1024, 512, 256, 128))
           if m % t == 0 and t % (LANES if mode == "tn" else 16) == 0 and t <= 4096] or [m]
    if mode == "tn":
        fits = [(k // tk + 0.5 * (m // tm), tm, tk)
                for tk in (4096, 2048, 1024, 512, 256, 128) if k % tk == 0 for tm in tms
                if 2 * (tk * tm * a_bytes + tk * n * b_bytes + tm * n * (o_bytes + init_bytes)) <= MM_VMEM_BUDGET]
        if fits:
            _, tm, tk = min(fits)
            return tm, n, tk
    tks = ([t for t in (512, 256, 128) if k % t == 0] or [k]) if mode == "tn" else _halvings(k)
    for tn in _halvings(n):
        for tk in tks:
            for tm in tms:
                acc = tm * tn * 4 if (k // tk > 1 and o_bytes != 4) else 0
                tiles = tm * tk * a_bytes + tk * tn * b_bytes + tm * tn * (o_bytes + init_bytes)
                if 2 * tiles + acc <= MM_VMEM_BUDGET:
                    return tm, tn, tk
    return tms[-1], _halvings(n)[-1], tks[-1]


def _mm(name, a, b, mode, out_dtype=F32, init=None, after=None):
    follow = () if after is None else (after,)
    if mode == "nn":
        (m, k), n = a.shape, b.shape[1]
    elif mode == "nt":
        (m, k), n = a.shape, b.shape[0]
    else:
        (k, m), n = a.shape, b.shape[1]
    tm, tn, tk = _mm_tiles(mode, m, n, k, a.dtype.itemsize, b.dtype.itemsize, jnp.dtype(out_dtype).itemsize,
                           0 if init is None else 4)
    nk = k // tk
    use_acc = nk > 1 and out_dtype != F32

    inits = () if init is None else (init,)

    def body(a_ref, b_ref, *rest):
        rest = rest[:len(inits)] + rest[len(inits) + len(follow):]
        o_ref, acc = rest[len(inits)], rest[len(inits) + 1:]
        part = _raw_dot(mode, a_ref[...], b_ref[...], False)
        first = lambda: part + rest[0][...] if inits else part
        if nk == 1:
            o_ref[...] = first().astype(o_ref.dtype)
            return
        acc_ref = acc[0] if use_acc else o_ref
        kk = pl.program_id(2)

        @pl.when(kk == 0)
        def _():
            acc_ref[...] = first()

        @pl.when(kk > 0)
        def _():
            acc_ref[...] += part

        if use_acc:
            @pl.when(kk == nk - 1)
            def _():
                o_ref[...] = acc_ref[...].astype(o_ref.dtype)

    if mode == "nn":
        a_spec = pl.BlockSpec((tm, tk), lambda i, j, kk: (i, kk))
        b_spec = pl.BlockSpec((tk, tn), lambda i, j, kk: (kk, j))
    elif mode == "nt":
        a_spec = pl.BlockSpec((tm, tk), lambda i, j, kk: (i, kk))
        b_spec = pl.BlockSpec((tn, tk), lambda i, j, kk: (j, kk))
    else:
        a_spec = pl.BlockSpec((tk, tm), lambda i, j, kk: (kk, i))
        b_spec = pl.BlockSpec((tk, tn), lambda i, j, kk: (kk, j))
    return pl.pallas_call(
        body, name=name, grid=(m // tm, n // tn, nk),
        in_specs=[a_spec, b_spec] + [pl.BlockSpec((tm, tn), lambda i, j, kk: (i, j))] * len(inits)
        + [pl.BlockSpec(memory_space=pl.ANY)] * len(follow),
        out_specs=pl.BlockSpec((tm, tn), lambda i, j, kk: (i, j)),
        out_shape=jax.ShapeDtypeStruct((m, n), out_dtype),
        scratch_shapes=[pltpu.VMEM((tm, tn), F32)] if use_acc else [],
        compiler_params=_cp("parallel", "parallel", "arbitrary"),
    )(a, b, *inits, *follow)


def _rowwise(name, fn, rows, row_ins, full_ins, row_outs, acc_outs, tm=None):
    tm = tm or _pick(rows, (512, 256, 128))
    n_r, n_f, n_o, n_a = len(row_ins), len(full_ins), len(row_outs), len(acc_outs)

    def body(*refs):
        ins, outs = refs[:n_r + n_f], refs[n_r + n_f:]
        vals = [r[...].astype(F32) for r in ins]
        ro, ao = fn(*vals)
        for r, val in zip(outs[:n_o], ro):
            r[...] = val.astype(r.dtype)
        if n_a:
            @pl.when(pl.program_id(0) == 0)
            def _():
                for r in outs[n_o:]:
                    r[...] = jnp.zeros_like(r)

            for r, val in zip(outs[n_o:], ao):
                r[...] += val

    in_specs = [pl.BlockSpec((tm, w), functools.partial(lambda i, cb: (i, cb), cb=cb)) for _, w, cb in row_ins]
    in_specs += [pl.BlockSpec(a.shape, lambda i: (0, 0)) for a in full_ins]
    out_specs = [pl.BlockSpec((tm, w), lambda i: (i, 0)) for w, _ in row_outs]
    out_specs += [pl.BlockSpec(s, lambda i: (0, 0)) for s in acc_outs]
    out_shape = [jax.ShapeDtypeStruct((rows, w), dt) for w, dt in row_outs]
    out_shape += [jax.ShapeDtypeStruct(s, F32) for s in acc_outs]
    return pl.pallas_call(
        body, name=name, grid=(rows // tm,), in_specs=in_specs, out_specs=out_specs, out_shape=out_shape,
        compiler_params=_cp("arbitrary" if n_a else "parallel"),
    )(*[a for a, _, _ in row_ins], *full_ins)


def _rn(x):
    return x * lax.rsqrt(jnp.mean(x * x, axis=-1, keepdims=True) + 1e-6)


def _sigmoid(t):
    return 1.0 / (1.0 + jnp.exp(-t))


def _f_norm_mod(x, g, sh, sc):
    return _rn(x) * g * (1.0 + sc) + sh


def _f_post_res(xr, y, g, gate):
    return xr + gate * (_rn(y) * g)


@jax.custom_vjp
def _f_swiglu(g, u):
    return g * _sigmoid(g) * u


def _f_swiglu_fwd(g, u):
    s = _sigmoid(g)
    return g * s * u, (g, u, s)


def _f_swiglu_bwd(res, da):
    g, u, s = res
    gs = g * s
    return da * u * (s + gs * (1.0 - s)), da * gs


_f_swiglu.defvjp(_f_swiglu_fwd, _f_swiglu_bwd)


def _logsig(u):
    return jnp.minimum(u, 0.0) - jnp.log(1.0 + jnp.exp(-jnp.abs(u)))


def _f_gate(z, wf, wb, bf, bb):
    return _logsig(_nn(z, wf) + bf) / GATE_TAU, _logsig(_nn(z, wb) + bb) / GATE_TAU


def _f_gla_out(of, ob, gg, gt, bd):
    o = of + ob
    ms = _nn_hi(o * o, bd)
    return o * lax.rsqrt(ms + 1e-6) * gt * (gg * _sigmoid(gg))


def _norm_mod(name, x, g, sh, sc):
    rows, d = x.shape
    return _rowwise(name, lambda x, g, sh, sc: ((_f_norm_mod(x, g, sh, sc),), ()), rows,
                    [(x, d, 0)], [g, sh, sc], [(d, BF16)], [])[0]


def _norm_mod_bwd(name, dh, dres, x, g, sh, sc):
    rows, d = x.shape

    def fn(dh, dres, x, g, sh, sc):
        _, vjp = jax.vjp(_f_norm_mod, x, g, sh, sc)
        dx, dg, dsh, dsc = vjp(dh)
        return (dx + dres,), (dg, dsh, dsc)

    return _rowwise(name, fn, rows, [(dh, d, 0), (dres, d, 0), (x, d, 0)], [g, sh, sc], [(d, F32)],
                    [(1, d)] * 3)


def _post_res_norm_mod(name, xr, y, g_post, gate, g_pre, sh, sc):
    rows, d = xr.shape

    def fn(xr, y, g_post, gate, g_pre, sh, sc):
        x1 = _f_post_res(xr, y, g_post, gate)
        return (x1, _f_norm_mod(x1, g_pre, sh, sc)), ()

    return _rowwise(name, fn, rows, [(xr, d, 0), (y, d, 0)], [g_post, gate, g_pre, sh, sc], [(d, F32), (d, BF16)], [])


def _norm_mod_post_res_bwd(name, dh, dres, x1, y, g_pre, sh, sc, g_post, gate):
    rows, d = x1.shape

    def fn(dh, dres, x1, y, g_pre, sh, sc, g_post, gate):
        _, vjp_norm = jax.vjp(_f_norm_mod, x1, g_pre, sh, sc)
        dx1, dg_pre, dsh, dsc = vjp_norm(dh)
        dx1 = dx1 + dres
        _, vjp_res = jax.vjp(lambda y, g, gate: _f_post_res(jnp.zeros_like(y), y, g, gate), y, g_post, gate)
        dy, dg_post, dgate = vjp_res(dx1)
        return (dx1, dy), (dg_pre, dsh, dsc, dg_post, dgate)

    return _rowwise(name, fn, rows, [(dh, d, 0), (dres, d, 0), (x1, d, 0), (y, d, 0)], [g_pre, sh, sc, g_post, gate],
                    [(d, F32), (d, BF16)], [(1, d)] * 5, tm=_pick(rows, (256, 128)))


def _post_res_loss(name, xr, y, g, gate, target):
    rows, d = xr.shape

    def fn(xr, y, target, g, gate):
        x2, vjp = jax.vjp(lambda y, g, gate: _f_post_res(xr, y, g, gate), y, g, gate)
        diff = x2 - target
        part = 0.5 * jnp.sum(jnp.mean(diff * diff, axis=-1, keepdims=True), axis=0, keepdims=True)
        dx2 = diff * (1.0 / d)
        dy, dg, dgate = vjp(dx2)
        return (dx2, dy), (jnp.broadcast_to(part, (1, LANES)), dg, dgate)

    return _rowwise(name, fn, rows, [(xr, d, 0), (y, d, 0), (target, d, 0)], [g, gate], [(d, F32), (d, BF16)],
                    [(1, LANES), (1, d), (1, d)])


def _mm_rows(name, a, b, mode, fn, extras, outs):
    m, k = a.shape
    tm = _pick(m, (256, 128))

    def body(a_ref, b_ref, *rest):
        tiles = fn(_raw_dot(mode, a_ref[...], b_ref[...], False), *[e[...] for e in rest[:len(extras)]])
        for r, val in zip(rest[len(extras):], tiles):
            r[...] = val.astype(r.dtype)

    row = lambda w: pl.BlockSpec((tm, w), lambda i: (i, 0))
    return pl.pallas_call(
        body, name=name, grid=(m // tm,),
        in_specs=[row(k), pl.BlockSpec(b.shape, lambda i: (0, 0))] + [row(e.shape[1]) for e in extras],
        out_specs=[row(w) for w, _ in outs], out_shape=[jax.ShapeDtypeStruct((m, w), dt) for w, dt in outs],
        compiler_params=_cp("parallel"),
    )(a, b, *extras)


def _ffn_in_swiglu(name, h, w_t):
    f = w_t.shape[0] // 2
    fn = lambda u: (u, _f_swiglu(u[:, :f], u[:, f:]))
    return _mm_rows(name, h, w_t, "nt", fn, [], [(2 * f, BF16), (f, BF16)])


def _ffn_out_dx_swiglu_bwd(name, df, w_out, u):
    f = w_out.shape[0]

    def fn(da, u):
        u = u.astype(F32)
        _, vjp = jax.vjp(_f_swiglu, u[:, :f], u[:, f:])
        return (jnp.concatenate(vjp(da), axis=1),)

    return _mm_rows(name, df, w_out, "nt", fn, [u], [(2 * f, BF16)])[0]


def _gate_fwd(name, p, wf, wb, bf, bb):
    rows = p.shape[0]
    return _rowwise(name, lambda z, wf, wb, bf, bb: (_f_gate(z, wf, wb, bf, bb), ()), rows,
                    [(p, LANES, C_Z // LANES)], [wf, wb, bf, bb], [(GKW, F32)] * 2, [])


def _gate_bwd(name, p, dla_f, dla_b, wf, wb, bf, bb):
    rows = p.shape[0]

    def fn(z, dlf, dlb, wf, wb, bf, bb):
        _, vjp = jax.vjp(_f_gate, z, wf, wb, bf, bb)
        dz, dwf, dwb, dbf, dbb = vjp((dlf, dlb))
        return (dz,), (dwf, dwb, dbf, dbb)

    return _rowwise(name, fn, rows, [(p, LANES, C_Z // LANES), (dla_f, GKW, 0), (dla_b, GKW, 0)],
                    [wf, wb, bf, bb], [(LANES, BF16)], [(LANES, GKW), (LANES, GKW), (1, GKW), (1, GKW)])


def _head_mean_matrix():
    h = np.arange(GVW) // GLA_DV
    return jnp.asarray((h[:, None] == h[None, :]).astype(np.float32) / GLA_DV)


def _gla_out(name, attn, of, ob, p, gt):
    rows = of.shape[0]
    bd = _head_mean_matrix()
    fn = lambda attn, of, ob, gg, gt, bd: ((jnp.concatenate([attn, _f_gla_out(of, ob, gg, gt, bd)], axis=1),), ())
    return _rowwise(name, fn, rows, [(attn, QW, 0), (of, GVW, 0), (ob, GVW, 0), (p, GVW, C_GG // GVW)], [gt, bd],
                    [(MIX, BF16)], [])[0]


def _gla_out_bwd(name, dmix, of, ob, p, gt):
    rows = of.shape[0]
    bd = _head_mean_matrix()

    def fn(dm, of, ob, gg, gt, bd):
        _, vjp = jax.vjp(lambda of, gg, gt: _f_gla_out(of, ob, gg, gt, bd), of, gg, gt)
        do, dgg, dgt = vjp(dm)
        return (do, dgg), (dgt,)

    return _rowwise(name, fn, rows, [(dmix, GVW, 1), (of, GVW, 0), (ob, GVW, 0), (p, GVW, C_GG // GVW)], [gt, bd],
                    [(GVW, F32), (GVW, BF16)], [(1, GVW)])


def _rope_tables(n_tokens):
    t = jnp.arange(n_tokens)
    row = (t // GRID_W).astype(F32)
    col = (t % GRID_W).astype(F32)
    half = HEAD_DIM // 2
    inv_freq = ROPE_BASE ** (-jnp.arange(0, half, 2, dtype=F32) / half)
    ang_r = row[:, None] * inv_freq[None, :]
    ang_c = col[:, None] * inv_freq[None, :]
    ang = jnp.concatenate([ang_r, ang_r, ang_c, ang_c], axis=-1)
    sign = jnp.concatenate([-jnp.ones((16,), F32), jnp.ones((16,), F32)] * 2)
    cos, sin = jnp.cos(ang), jnp.sin(ang) * sign[None, :]
    return jnp.tile(cos, (1, 2)), jnp.tile(sin, (1, 2))


def _rot_pairs(x):
    w = x.shape[-1]
    lane = lax.broadcasted_iota(jnp.int32, x.shape, x.ndim - 1)
    return jnp.where((lane % 32) < 16, pltpu.roll(x, w - 16, x.ndim - 1), pltpu.roll(x, 16, x.ndim - 1))


def _rope_apply(x, cos, sin_signed, inverse):
    reps = x.shape[-1] // LANES
    cos = jnp.concatenate([cos] * reps, axis=-1) if reps > 1 else cos
    sin = jnp.concatenate([sin_signed] * reps, axis=-1) if reps > 1 else sin_signed
    if inverse:
        return x * cos + _rot_pairs(x * sin)
    return x * cos + _rot_pairs(x) * sin


def _rope_fwd(name, p, cos, sin):
    rows = p.shape[0]

    def fn(q, k, v, cos, sin):
        return (_rope_apply(q, cos, sin, False), _rope_apply(k, cos, sin, False), v), ()

    return _rowwise(name, fn, rows, [(p, QW, 0), (p, KVW, C_K // KVW), (p, KVW, C_V // KVW), (cos, LANES, 0),
                                     (sin, LANES, 0)], [], [(QW, BF16), (KVW, BF16), (KVW, BF16)], [])


def _proj_grad(name, dq_rot, dk_rot, dv, cos, sin, gla_f, gla_b, dgg, dz):
    rows = dq_rot.shape[0]

    def fn(dq, dk, dv, cos, sin, gqf, gkf, gvf, gqb, gkb, gvb, dgg, dz):
        parts = [_rope_apply(dq, cos, sin, True), gvf + gvb, dgg, _rope_apply(dk, cos, sin, True), dv, gqf + gqb,
                 gkf + gkb, dz]
        return (jnp.concatenate(parts, axis=1),), ()

    ins = [(dq_rot, QW), (dk_rot, KVW), (dv, KVW), (cos, LANES), (sin, LANES)]
    ins += [(t, t.shape[1]) for t in (*gla_f, *gla_b)] + [(dgg, GVW), (dz, LANES)]
    return _rowwise(name, fn, rows, [(t, w, 0) for t, w in ins], [], [(IN_PAD, BF16)], [],
                    tm=_pick(rows, (256, 128)))[0]


GROUP_ROWS = ATT_GROUP * BLOCK


def _f_attn(qs, kws, vws, kcs, vcs, sink, n, n_tokens):
    row = lax.broadcasted_iota(jnp.int32, (GROUP_ROWS, 1), 0)
    group = sum((row >= g * BLOCK).astype(jnp.int32) for g in range(1, ATT_GROUP))
    i = lax.broadcasted_iota(jnp.int32, (GROUP_ROWS, 3 * BLOCK), 0) - BLOCK * group
    j = lax.broadcasted_iota(jnp.int32, (GROUP_ROWS, 3 * BLOCK), 1)
    kpos = (n - 1) * BLOCK + j
    mask = (jnp.abs(j - BLOCK - i) <= WINDOW) & (kpos >= 0) & (kpos < n_tokens)
    head_id = lax.broadcasted_iota(jnp.int32, (1, ATT_HEADS), 1)
    scale = HEAD_DIM ** -0.5
    outs = []
    for h in range(ATT_KV_HEADS):
        sk = jnp.zeros((GROUP_ROWS, 1), F32)
        for g in range(ATT_GROUP):
            one = jnp.sum(jnp.where(head_id == h * ATT_GROUP + g, sink, 0.0), axis=-1, keepdims=True)
            sk = jnp.where(group == g, one, sk)
        q = qs[h] * scale
        s_w = jnp.where(mask, _nt(q, kws[h]), NEG_INF)
        s_c = _nt(q, kcs[h])
        m = lax.stop_gradient(jnp.maximum(jnp.maximum(jnp.max(s_w, axis=-1, keepdims=True),
                                                      jnp.max(s_c, axis=-1, keepdims=True)), sk))
        pw, pc = jnp.exp(s_w - m), jnp.exp(s_c - m)
        den = jnp.sum(pw, axis=-1, keepdims=True) + jnp.sum(pc, axis=-1, keepdims=True) + jnp.exp(sk - m)
        outs.append((_nn(pw, vws[h]) + _nn(pc, vcs[h])) / den)
    return tuple(outs)


def _group_rows(ref, h):
    hs = lambda hq: slice(hq * HEAD_DIM, (hq + 1) * HEAD_DIM)
    return jnp.concatenate([ref[:, hs(h * ATT_GROUP + g)].astype(F32) for g in range(ATT_GROUP)], axis=0)


def _ungroup_rows(ref, h, val):
    for g in range(ATT_GROUP):
        hq = h * ATT_GROUP + g
        ref[:, hq * HEAD_DIM:(hq + 1) * HEAD_DIM] = val[g * BLOCK:(g + 1) * BLOCK].astype(ref.dtype)


def _attn_loads(n, q_ref, kp_ref, vp_ref, kc_ref, vc_ref):
    r0 = pl.multiple_of(n * BLOCK, BLOCK)
    hs = lambda h: slice(h * HEAD_DIM, (h + 1) * HEAD_DIM)
    qs = [_group_rows(q_ref, h) for h in range(ATT_KV_HEADS)]
    kws = [kp_ref[pl.ds(r0, 3 * BLOCK), hs(h)].astype(F32) for h in range(ATT_KV_HEADS)]
    vws = [vp_ref[pl.ds(r0, 3 * BLOCK), hs(h)].astype(F32) for h in range(ATT_KV_HEADS)]
    kcs = [kc_ref[:, hs(h)].astype(F32) for h in range(ATT_KV_HEADS)]
    vcs = [vc_ref[:, hs(h)].astype(F32) for h in range(ATT_KV_HEADS)]
    return r0, hs, qs, kws, vws, kcs, vcs


def _attn_specs(s, c):
    full = lambda shape: pl.BlockSpec(shape, lambda n: (0, 0))
    return [pl.BlockSpec((BLOCK, QW), lambda n: (n, 0)), full((s + 2 * BLOCK, KVW)), full((s + 2 * BLOCK, KVW)),
            full((c, KVW)), full((c, KVW)), full((1, ATT_HEADS))]


def _attn_fwd(q, kp, vp, kc, vc, sink):
    s, c = q.shape[0], kc.shape[0]

    def body(q_ref, kp_ref, vp_ref, kc_ref, vc_ref, sink_ref, o_ref):
        n = pl.program_id(0)
        _, hs, qs, kws, vws, kcs, vcs = _attn_loads(n, q_ref, kp_ref, vp_ref, kc_ref, vc_ref)
        outs = _f_attn(qs, kws, vws, kcs, vcs, sink_ref[...], n, s)
        for h in range(ATT_KV_HEADS):
            _ungroup_rows(o_ref, h, outs[h])

    return pl.pallas_call(
        body, name="attn_fwd", grid=(s // BLOCK,), in_specs=_attn_specs(s, c),
        out_specs=pl.BlockSpec((BLOCK, QW), lambda n: (n, 0)), out_shape=jax.ShapeDtypeStruct((s, QW), BF16),
        compiler_params=_cp("parallel"),
    )(q, kp, vp, kc, vc, sink)


def _attn_bwd(do, q, kp, vp, kc, vc, sink):
    s, c = q.shape[0], kc.shape[0]

    def body(do_ref, q_ref, kp_ref, vp_ref, kc_ref, vc_ref, sink_ref, dq_ref, dkp_ref, dvp_ref, dkc_ref, dvc_ref,
             dsink_ref):
        n = pl.program_id(0)

        @pl.when(n == 0)
        def _():
            for r in (dkp_ref, dvp_ref, dkc_ref, dvc_ref, dsink_ref):
                r[...] = jnp.zeros_like(r)

        r0, hs, qs, kws, vws, kcs, vcs = _attn_loads(n, q_ref, kp_ref, vp_ref, kc_ref, vc_ref)
        _, vjp = jax.vjp(lambda qs, kws, vws, kcs, vcs, sink: _f_attn(qs, kws, vws, kcs, vcs, sink, n, s),
                         qs, kws, vws, kcs, vcs, sink_ref[...])
        dqs, dkws, dvws, dkcs, dvcs, dsink = vjp(tuple(_group_rows(do_ref, h) for h in range(ATT_KV_HEADS)))
        for h in range(ATT_KV_HEADS):
            _ungroup_rows(dq_ref, h, dqs[h])
            dkp_ref[pl.ds(r0, 3 * BLOCK), hs(h)] += dkws[h]
            dvp_ref[pl.ds(r0, 3 * BLOCK), hs(h)] += dvws[h]
            dkc_ref[:, hs(h)] += dkcs[h]
            dvc_ref[:, hs(h)] += dvcs[h]
        dsink_ref[...] += dsink

    full = lambda shape: pl.BlockSpec(shape, lambda n: (0, 0))
    return pl.pallas_call(
        body, name="attn_bwd", grid=(s // BLOCK,),
        in_specs=[pl.BlockSpec((BLOCK, QW), lambda n: (n, 0))] + _attn_specs(s, c),
        out_specs=[pl.BlockSpec((BLOCK, QW), lambda n: (n, 0)), full((s + 2 * BLOCK, KVW)), full((s + 2 * BLOCK, KVW)),
                   full((c, KVW)), full((c, KVW)), full((1, ATT_HEADS))],
        out_shape=[jax.ShapeDtypeStruct((s, QW), F32), jax.ShapeDtypeStruct((s + 2 * BLOCK, KVW), F32),
                   jax.ShapeDtypeStruct((s + 2 * BLOCK, KVW), F32), jax.ShapeDtypeStruct((c, KVW), F32),
                   jax.ShapeDtypeStruct((c, KVW), F32), jax.ShapeDtypeStruct((1, ATT_HEADS), F32)],
        compiler_params=_cp("arbitrary"),
    )(do, q, kp, vp, kc, vc, sink)


GLA_GROUPS = 1
GLA_GROUP_HEADS = GLA_HEADS // GLA_GROUPS
GKG, GVG = GKW // GLA_GROUPS, GVW // GLA_GROUPS


def _gla_masks(heads=GLA_HEADS):
    hk = np.arange(heads * GLA_DK) // GLA_DK
    hv = np.arange(heads * GLA_DV) // GLA_DV
    head_k = (np.arange(heads)[:, None] == hk[None, :]).astype(np.float32)
    head_v = (np.arange(heads)[:, None] == hv[None, :]).astype(np.float32)
    bd_t = (hv[:, None] == hk[None, :]).astype(np.float32)
    return jnp.asarray(head_k), jnp.asarray(head_v), jnp.asarray(bd_t)


def _group_states(st):
    return jnp.stack([st[g * GVG:(g + 1) * GVG, g * GKG:(g + 1) * GKG] for g in range(GLA_GROUPS)])


def _ungroup_states(st):
    out = jnp.zeros((GVW, GKW), st.dtype)
    for g in range(GLA_GROUPS):
        out = out.at[g * GVG:(g + 1) * GVG, g * GKG:(g + 1) * GKG].set(st[g])
    return out


def _tri(n, rev, strict=False):
    i = lax.broadcasted_iota(jnp.int32, (n, n), 0)
    j = lax.broadcasted_iota(jnp.int32, (n, n), 1)
    if strict:
        keep = (j > i) if rev else (j < i)
    else:
        keep = (j >= i) if rev else (j <= i)
    return keep


def _f_gla_chunk(q, k, v, la, st, head_k, head_v, bd_t, rev):
    return _f_gla_carry(*_f_gla_intra(q, k, v, la, head_k, head_v, rev), v, st, bd_t)


def _f_gla_intra(q, k, v, la, head_k, head_v, rev):
    heads, kw, vw = head_k.shape[0], q.shape[1], v.shape[1]
    keep = _tri(GLA_CHUNK, rev)
    b = _nn_hi(keep.astype(F32), la)
    bl = jnp.sum(la, axis=0, keepdims=True)
    qd = q * (GLA_DK ** -0.5) * jnp.exp(b)
    ki = k * jnp.exp(-b)
    kd = k * jnp.exp(bl - b)
    q_heads = (qd[None, :, :] * head_k[:, None, :]).reshape(heads * GLA_CHUNK, kw)
    a_all = _nt(q_heads, ki).reshape(heads, GLA_CHUNK, GLA_CHUNK)
    a_all = jnp.where(keep[None, :, :], a_all, 0.0).reshape(heads * GLA_CHUNK, GLA_CHUNK)
    o_all = _nn(a_all, v).reshape(heads, GLA_CHUNK, vw)
    return jnp.sum(o_all * head_v[:, None, :], axis=0), qd, kd, bl


def _f_gla_carry(intra, qd, kd, bl, v, st, bd_t):
    return intra + _nt(qd, st), st * jnp.exp(bl) + bd_t * _tn(v, kd)


def _gla_specs(s, tb, order):
    return [pl.BlockSpec((tb, GKW), lambda i: (order(i), C_GQ // GKW)),
            pl.BlockSpec((tb, GKW), lambda i: (order(i), C_GK // GKW)),
            pl.BlockSpec((tb, GVW), lambda i: (order(i), C_GV // GVW)),
            pl.BlockSpec((tb, GKW), lambda i: (order(i), 0))]


GLA_BLOCK_CHUNKS = 4


def _gla_fwd(p, la_f, la_b, st_f0, st_b0):
    s = p.shape[0]
    tb = GLA_BLOCK_CHUNKS * GLA_CHUNK
    nblk = s // tb
    up, down = (lambda i: i), (lambda i: nblk - 1 - i)
    masks = _gla_masks(GLA_GROUP_HEADS)

    def scan(rev, q_ref, k_ref, v_ref, la_ref, o_ref, sts_ref, st_ref, consts):
        for g in range(GLA_GROUPS):
            gk, gv = slice(g * GKG, (g + 1) * GKG), slice(g * GVG, (g + 1) * GVG)
            st = st_ref[g]
            sts_ref[0, g] = st
            chunks = range(GLA_BLOCK_CHUNKS)
            for ci in (reversed(chunks) if rev else chunks):
                rows = slice(ci * GLA_CHUNK, (ci + 1) * GLA_CHUNK)
                o, st = _f_gla_chunk(q_ref[rows, gk], k_ref[rows, gk], v_ref[rows, gv], la_ref[rows, gk], st, *consts,
                                     rev)
                o_ref[rows, gv] = o
            st_ref[g] = st

    def body(qf, kf, vf, laf, qb, kb, vb, lab, stf0, stb0, hk_ref, hv_ref, bd_ref, of_ref, stsf_ref, ob_ref, stsb_ref,
             stf_ref, stb_ref):
        @pl.when(pl.program_id(0) == 0)
        def _():
            stf_ref[...] = stf0[...]
            stb_ref[...] = stb0[...]

        consts = (hk_ref[...], hv_ref[...], bd_ref[...])
        scan(False, qf, kf, vf, laf, of_ref, stsf_ref, stf_ref, consts)
        scan(True, qb, kb, vb, lab, ob_ref, stsb_ref, stb_ref, consts)

    full = lambda a: pl.BlockSpec(a.shape, lambda i: (0,) * a.ndim)
    outs = lambda order: [pl.BlockSpec((tb, GVW), lambda i: (order(i), 0)),
                          pl.BlockSpec((1, GLA_GROUPS, GVG, GKG), lambda i: (order(i), 0, 0, 0))]
    return pl.pallas_call(
        body, name="gla_fwd", grid=(nblk,),
        in_specs=_gla_specs(s, tb, up) + _gla_specs(s, tb, down) + [full(st_f0), full(st_b0)]
        + [full(m) for m in masks],
        out_specs=outs(up) + outs(down),
        out_shape=[jax.ShapeDtypeStruct((s, GVW), F32), jax.ShapeDtypeStruct((nblk, GLA_GROUPS, GVG, GKG), F32)] * 2,
        scratch_shapes=[pltpu.VMEM((GLA_GROUPS, GVG, GKG), F32)] * 2,
        compiler_params=_cp("arbitrary"),
    )(p, p, p, la_f, p, p, p, la_b, st_f0, st_b0, *masks)


def _gla_bwd(p, la_f, la_b, sts_f, sts_b, do, after=None):
    s = p.shape[0]
    tb = GLA_BLOCK_CHUNKS * GLA_CHUNK
    nblk = s // tb
    up, down = (lambda i: i), (lambda i: nblk - 1 - i)
    masks = _gla_masks(GLA_GROUP_HEADS)
    follow = () if after is None else (after,)

    def back(rev, q_ref, k_ref, v_ref, la_ref, sts_ref, do_ref, dq_ref, dk_ref, dv_ref, dla_ref, dst0_ref, dst_ref,
             consts):
        def block(q, k, v, la, st):
            outs = [None] * GLA_BLOCK_CHUNKS
            chunks = range(GLA_BLOCK_CHUNKS)
            for ci in (reversed(chunks) if rev else chunks):
                outs[ci], st = _f_gla_chunk(q[ci], k[ci], v[ci], la[ci], st, *consts, rev)
            return tuple(outs), st

        for g in range(GLA_GROUPS):
            gk, gv = slice(g * GKG, (g + 1) * GKG), slice(g * GVG, (g + 1) * GVG)
            split = lambda r, cols: tuple(r[ci * GLA_CHUNK:(ci + 1) * GLA_CHUNK, cols].astype(F32)
                                          for ci in range(GLA_BLOCK_CHUNKS))
            _, vjp = jax.vjp(block, split(q_ref, gk), split(k_ref, gk), split(v_ref, gv), split(la_ref, gk),
                             sts_ref[0, g])
            dq, dk, dv, dla, dst = vjp((split(do_ref, gv), dst_ref[g]))
            for ci in range(GLA_BLOCK_CHUNKS):
                rows = slice(ci * GLA_CHUNK, (ci + 1) * GLA_CHUNK)
                dq_ref[rows, gk], dk_ref[rows, gk], dv_ref[rows, gv], dla_ref[rows, gk] = dq[ci], dk[ci], dv[ci], dla[ci]
            dst_ref[g] = dst
            dst0_ref[g] = dst

    def body(*refs):
        ins, (hk_ref, hv_ref, bd_ref) = refs[:12], refs[12:15]
        outs = refs[15 + len(follow):]

        @pl.when(pl.program_id(0) == 0)
        def _():
            outs[10][...] = jnp.zeros_like(outs[10])
            outs[11][...] = jnp.zeros_like(outs[11])

        consts = (hk_ref[...], hv_ref[...], bd_ref[...])
        back(False, *ins[:6], *outs[:5], outs[10], consts)
        back(True, *ins[6:], *outs[5:10], outs[11], consts)

    full = lambda a: pl.BlockSpec(a.shape, lambda i: (0,) * a.ndim)

    def ins(order):
        return _gla_specs(s, tb, order) + [pl.BlockSpec((1, GLA_GROUPS, GVG, GKG), lambda i: (order(i), 0, 0, 0)),
                                           pl.BlockSpec((tb, GVW), lambda i: (order(i), 0))]

    def outs(order):
        blk = lambda w: pl.BlockSpec((tb, w), lambda i: (order(i), 0))
        return [blk(GKW), blk(GKW), blk(GVW), blk(GKW), pl.BlockSpec((GLA_GROUPS, GVG, GKG), lambda i: (0, 0, 0))]

    shapes = [jax.ShapeDtypeStruct((s, GKW), F32), jax.ShapeDtypeStruct((s, GKW), F32),
              jax.ShapeDtypeStruct((s, GVW), F32), jax.ShapeDtypeStruct((s, GKW), F32),
              jax.ShapeDtypeStruct((GLA_GROUPS, GVG, GKG), F32)]
    both = pl.pallas_call(
        body, name="gla_bwd", grid=(nblk,),
        in_specs=ins(down) + ins(up) + [full(m) for m in masks] + [pl.BlockSpec(memory_space=pl.ANY)] * len(follow),
        out_specs=outs(down) + outs(up), out_shape=shapes * 2,
        scratch_shapes=[pltpu.VMEM((GLA_GROUPS, GVG, GKG), F32)] * 2,
        compiler_params=_cp("arbitrary"),
    )(p, p, p, la_f, sts_f, do, p, p, p, la_b, sts_b, do, *masks, *follow)
    return both[:5], both[5:]


def _f_ctx_state(k, v, la_f, la_b, bd_t):
    c = k.shape[0]
    after = _nn_hi(_tri(c, True, strict=True).astype(F32), la_f)
    before = _nn_hi(_tri(c, False, strict=True).astype(F32), la_b)
    return bd_t * _tn(v, k * jnp.exp(after)), bd_t * _tn(v, k * jnp.exp(before))


def _ctx_state(pc, la_f, la_b):
    c = pc.shape[0]
    bd_t = _gla_masks()[2]

    def body(k_ref, v_ref, lf_ref, lb_ref, bd_ref, sf_ref, sb_ref):
        sf_ref[...], sb_ref[...] = _f_ctx_state(k_ref[...], v_ref[...], lf_ref[...], lb_ref[...], bd_ref[...])

    full = lambda a: pl.BlockSpec(a.shape, lambda i: (0, 0))
    return pl.pallas_call(
        body, name="ctx_state_fwd", grid=(1,),
        in_specs=[pl.BlockSpec((c, GKW), lambda i: (0, C_GK // GKW)), pl.BlockSpec((c, GVW), lambda i: (0, C_GV // GVW)),
                  full(la_f), full(la_b), full(bd_t)],
        out_specs=[pl.BlockSpec((GVW, GKW), lambda i: (0, 0))] * 2,
        out_shape=[jax.ShapeDtypeStruct((GVW, GKW), F32)] * 2,
        compiler_params=_cp("arbitrary"),
    )(pc, pc, la_f, la_b, bd_t)


def _ctx_state_bwd(pc, la_f, la_b, dsf, dsb):
    c = pc.shape[0]
    bd_t = _gla_masks()[2]

    def body(k_ref, v_ref, lf_ref, lb_ref, bd_ref, dsf_ref, dsb_ref, dk_ref, dv_ref, dlf_ref, dlb_ref):
        _, vjp = jax.vjp(lambda k, v, lf, lb: _f_ctx_state(k, v, lf, lb, bd_ref[...]),
                         k_ref[...], v_ref[...], lf_ref[...], lb_ref[...])
        dk, dv, dlf, dlb = vjp((dsf_ref[...], dsb_ref[...]))
        dk_ref[...], dv_ref[...] = dk.astype(BF16), dv.astype(BF16)
        dlf_ref[...], dlb_ref[...] = dlf, dlb

    full = lambda a: pl.BlockSpec(a.shape, lambda i: (0, 0))
    return pl.pallas_call(
        body, name="ctx_state_bwd", grid=(1,),
        in_specs=[pl.BlockSpec((c, GKW), lambda i: (0, C_GK // GKW)), pl.BlockSpec((c, GVW), lambda i: (0, C_GV // GVW)),
                  full(la_f), full(la_b), full(bd_t), full(dsf), full(dsb)],
        out_specs=[pl.BlockSpec((c, GKW), lambda i: (0, 0)), pl.BlockSpec((c, GVW), lambda i: (0, 0)),
                   pl.BlockSpec((c, GKW), lambda i: (0, 0)), pl.BlockSpec((c, GKW), lambda i: (0, 0))],
        out_shape=[jax.ShapeDtypeStruct((c, GKW), BF16), jax.ShapeDtypeStruct((c, GVW), BF16),
                   jax.ShapeDtypeStruct((c, GKW), F32), jax.ShapeDtypeStruct((c, GKW), F32)],
        compiler_params=_cp("arbitrary"),
    )(pc, pc, la_f, la_b, bd_t, dsf, dsb)


_SRC_COLS = ((0, QW), (QW + 2 * KVW + 2 * GKW, GVW), (QW + 2 * KVW + 2 * GKW + GVW, GVW), (QW, KVW), (QW + KVW, KVW),
             (QW + 2 * KVW, GKW), (QW + 2 * KVW + GKW, GKW), (IN_COLS - 2 * GATE_RANK, 2 * GATE_RANK))
_DST_COLS = (C_Q, C_GV, C_GG, C_K, C_V, C_GQ, C_GK, C_Z)


def _pack_w_in(w_in):
    parts = [w_in[:, s:s + n] for s, n in _SRC_COLS]
    parts.append(jnp.zeros((w_in.shape[0], IN_PAD - C_Z - 2 * GATE_RANK), w_in.dtype))
    return jnp.concatenate(parts, axis=1)


def _unpack_w_in_grad(g):
    by_src = sorted(zip(_SRC_COLS, _DST_COLS))
    return jnp.concatenate([g[:, d:d + n] for (_, n), d in by_src], axis=1)


def _prep_gate_weights(w_gate_fwd, w_gate_bwd):
    pad_rows = lambda w, at: jnp.zeros((LANES, GKW), F32).at[at:at + GATE_RANK].set(w)
    return {"wg_f": pad_rows(w_gate_fwd, 0), "wg_b": pad_rows(w_gate_bwd, GATE_RANK)}


def _local_step(x, ctx, target, ada, ada_c, w, late_weights, reduce_behind=None, reduce_w_in=None):
    s, d = x.shape
    sh1, sc1, gt1, sh2, sc2, gt2 = [ada[:, i * d:(i + 1) * d] for i in range(6)]
    sh1c, sc1c = ada_c[:, :d], ada_c[:, d:2 * d]
    cos, sin = _rope_tables(s)
    gt = jnp.tile(w["g_gla_norm"], (1, GLA_HEADS))

    h = _norm_mod("pre_mix", x, w["g_pre_mix"], sh1, sc1)
    hc = _norm_mod("pre_mix_ctx", ctx, w["g_pre_mix"], sh1c, sc1c)
    w_in, token = w["w_in"](h, cos, sin)
    p = _mm("proj_in", h, w_in, "nn", after=token)
    pc = _mm("proj_in_ctx", hc, w_in, "nn")
    q_rot, k_rot, v_b = _rope_fwd("rope", p, cos, sin)
    pad = ((BLOCK, BLOCK), (0, 0))
    kp, vp = jnp.pad(k_rot, pad), jnp.pad(v_b, pad)
    kc, vc = pc[:, C_K:C_K + KVW].astype(BF16), pc[:, C_V:C_V + KVW].astype(BF16)
    attn = _attn_fwd(q_rot, kp, vp, kc, vc, w["attn_sink"])
    gate_w = (w["wg_f"], w["wg_b"], w["b_gate_fwd"], w["b_gate_bwd"])
    la_f, la_b = _gate_fwd("gate", p, *gate_w)
    la_fc, la_bc = _gate_fwd("gate_ctx", pc, *gate_w)
    st_f0, st_b0 = _ctx_state(pc, la_fc, la_bc)
    o_f, sts_f, o_b, sts_b = _gla_fwd(p, la_f, la_b, _group_states(st_f0), _group_states(st_b0))
    mix = _gla_out("gla_out", attn, o_f, o_b, p, gt)
    w_out, w_ffn_in_t, w_ffn_out = late_weights(attn)
    y = _mm("proj_out", mix, w_out, "nn", BF16)
    x1, h2 = _post_res_norm_mod("post_mix_pre_ffn", x, y, w["g_post_mix"], gt1, w["g_pre_ffn"], sh2, sc2)
    u, a = _ffn_in_swiglu("ffn_in", h2, w_ffn_in_t)
    f = _mm("ffn_out", a, w_ffn_out, "nn", BF16)
    g = {}
    dx2, df, loss, g["g_post_ffn"], dgt2 = _post_res_loss("post_ffn_loss", x1, f, w["g_post_ffn"], gt2, target)

    g["w_ffn_out"] = _mm("ffn_out_dw", a, df, "tn")
    du = _ffn_out_dx_swiglu_bwd("ffn_out_dx", df, w_ffn_out, u)
    dh2 = _mm("ffn_in_dx", du, w_ffn_in_t, "nn", BF16)
    g["w_ffn_in_t"] = _mm("ffn_in_dw", du, h2, "tn")
    dx1, dy, g["g_pre_ffn"], dsh2, dsc2, g["g_post_mix"], dgt1 = _norm_mod_post_res_bwd(
        "pre_ffn_post_mix_bwd", dh2, dx2, x1, y, w["g_pre_ffn"], sh2, sc2, w["g_post_mix"], gt1)
    dmix = _mm("proj_out_dx", dy, w_out, "nt", BF16)
    g["w_out"] = _mm("proj_out_dw", mix, dy, "tn")
    rb, sink, token = reduce_behind, w["attn_sink"], None
    if rb is not None:
        gt = _behind(gt, rb.start(g["w_ffn_in_t"], g["w_ffn_out"], g["w_out"]))
    d_o, dgg, dgt = _gla_out_bwd("gla_out_bwd", dmix, o_f, o_b, p, gt)
    g["g_gla_norm"] = jnp.sum(dgt.reshape(GLA_HEADS, GLA_DV), axis=0, keepdims=True)
    if rb is not None:
        token = rb.pair(dgg)
    gla_f, gla_b = _gla_bwd(p, la_f, la_b, sts_f, sts_b, d_o, token)
    (dla_f, dst_f0), (dla_b, dst_b0) = gla_f[3:], gla_b[3:]
    dst_f0, dst_b0 = _ungroup_states(dst_f0), _ungroup_states(dst_b0)
    if rb is not None:
        sink = _behind(sink, rb.total(dla_b))
    dgkc, dgvc, dla_fc, dla_bc = _ctx_state_bwd(pc, la_fc, la_bc, dst_f0, dst_b0)
    dz, dwf, dwb, dbf, dbb = _gate_bwd("gate_bwd", p, dla_f, dla_b, *gate_w)
    dzc, dwfc, dwbc, dbfc, dbbc = _gate_bwd("gate_ctx_bwd", pc, dla_fc, dla_bc, *gate_w)
    g["w_gate_fwd"] = (dwf + dwfc)[:GATE_RANK]
    g["w_gate_bwd"] = (dwb + dwbc)[GATE_RANK:2 * GATE_RANK]
    g["b_gate_fwd"], g["b_gate_bwd"] = dbf + dbfc, dbb + dbbc
    dq_rot, dkp, dvp, dkc, dvc, g["attn_sink"] = _attn_bwd(dmix, q_rot, kp, vp, kc, vc, sink)
    if rb is not None:
        g["behind"] = rb.result(dq_rot)
    dp = _proj_grad("proj_grad", dq_rot, dkp[BLOCK:BLOCK + s], dvp[BLOCK:BLOCK + s], cos, sin, gla_f[:3], gla_b[:3],
                    dgg, dz)
    c_rows = ctx.shape[0]
    zeros = lambda n: jnp.zeros((c_rows, n), BF16)
    dpc = jnp.concatenate([zeros(QW), dgvc, zeros(GVW), dkc.astype(BF16), dvc.astype(BF16), zeros(GKW), dgkc, dzc],
                          axis=1)
    g["w_in"] = _mm("proj_in_dw", h, dp, "tn", init=_mm("proj_in_ctx_dw", hc, dpc, "tn"))
    token = None if reduce_w_in is None else reduce_w_in.start(g["w_in"])
    dh = _mm("proj_in_dx", dp, w_in, "nt", BF16, after=token)
    dhc = _mm("proj_in_ctx_dx", dpc, w_in, "nt")
    if reduce_w_in is not None:
        sh1 = _behind(sh1, reduce_w_in.pair(dh))
    dx, dg_a, dsh1, dsc1 = _norm_mod_bwd("pre_mix_bwd", dh, dx1, x, w["g_pre_mix"], sh1, sc1)
    if reduce_w_in is not None:
        dsh1 = _behind(dsh1, reduce_w_in.total(dx))
    _, dg_b, dsh1c, dsc1c = _norm_mod_bwd("pre_mix_ctx_bwd", dhc, jnp.zeros_like(dhc), ctx, w["g_pre_mix"], sh1c,
                                          sc1c)
    g["g_pre_mix"] = dg_a + dg_b
    d_ada = jnp.concatenate([dsh1, dsc1, dgt1, dsh2, dsc2, dgt2], axis=1)
    d_ada_c = jnp.concatenate([dsh1c, dsc1c, jnp.zeros((1, 4 * d), F32)], axis=1)
    return loss, dx, g, d_ada, d_ada_c


HBM = pl.BlockSpec(memory_space=pltpu.HBM)
N_DEV, N_CHIP = 8, 4


def _place():
    x, y, c = lax.axis_index("x"), lax.axis_index("y"), lax.axis_index("c")
    return x, y, c, [(1 - x, y), (x, 1 - y), (1 - x, 1 - y)]


def _row_tile(n, mult, cap):
    return max(t for t in range(mult, min(n, cap) + 1, mult) if n % t == 0)


def _ag_small(name, v, after=None):
    follow = () if after is None else (after,)

    def body(v_ref, *rest):
        out_ref, send_sems, recv_sems = rest[len(follow):]
        x, y, c, _ = _place()
        out_ref[4 * x + 2 * y + c] = v_ref[...]

        def peer(r):
            return ((1 - x) if r & 4 else x, (1 - y) if r & 2 else y, (1 - c) if r & 1 else c)

        def copy(r, block):
            px, py, pc = block
            return pltpu.make_async_remote_copy(
                src_ref=v_ref, dst_ref=out_ref.at[4 * px + 2 * py + pc], send_sem=send_sems.at[r - 1],
                recv_sem=recv_sems.at[r - 1], device_id=peer(r), device_id_type=MESH)

        sends = [copy(r, (x, y, c)) for r in range(1, N_DEV)]
        for cp in sends:
            cp.start()
        for r in range(1, N_DEV):
            copy(r, peer(r)).wait_recv()
        for cp in sends:
            cp.wait_send()

    return pl.pallas_call(
        body, name=name, out_shape=jax.ShapeDtypeStruct((N_DEV,) + v.shape, v.dtype),
        in_specs=[pl.BlockSpec(memory_space=pltpu.VMEM)] + [pl.BlockSpec(memory_space=pl.ANY)] * len(follow),
        out_specs=pl.BlockSpec(memory_space=pltpu.VMEM),
        scratch_shapes=[pltpu.SemaphoreType.DMA((N_DEV - 1,)), pltpu.SemaphoreType.DMA((N_DEV - 1,))],
    )(v, *follow)


def _halves(c, rows, mult):
    hr = rows // 2
    return pl.ds(pl.multiple_of(c * hr, mult), hr), pl.ds(pl.multiple_of((1 - c) * hr, mult), hr)


def _add_half(name, g, a, c_idx):
    n_sh, hr, n = a.shape
    tr = _row_tile(hr, 16, 1024)
    nb = hr // tr

    def body(c_ref, g_ref, a_ref, o_ref):
        o_ref[...] = (g_ref[...] + a_ref[...]).astype(o_ref.dtype)

    return pl.pallas_call(
        body, name=name, out_shape=jax.ShapeDtypeStruct(a.shape, BF16),
        grid_spec=pltpu.PrefetchScalarGridSpec(
            num_scalar_prefetch=1, grid=(n_sh, nb),
            in_specs=[pl.BlockSpec((1, tr, n), lambda s, i, c_ref: (s, c_ref[0] * nb + i, 0)),
                      pl.BlockSpec((1, tr, n), lambda s, i, c_ref: (s, i, 0))],
            out_specs=pl.BlockSpec((1, tr, n), lambda s, i, c_ref: (s, i, 0))),
        compiler_params=_cp("parallel", "parallel"),
    )(c_idx, g, a)


def _sum_chips(name, b):
    n_sh, hr, n = b.shape
    tr = _row_tile(hr, 16, 1024)

    def body(b0, b1, b2, b3, o_ref):
        o_ref[...] = ((b0[0].astype(F32) + b1[0].astype(F32)) + b2[0].astype(F32)) + b3[0].astype(F32)

    return pl.pallas_call(
        body, name=name, grid=(hr // tr,), out_shape=jax.ShapeDtypeStruct((hr, n), F32),
        in_specs=[pl.BlockSpec((1, tr, n), functools.partial(lambda i, k: (k, i, 0), k=k)) for k in range(n_sh)],
        out_specs=pl.BlockSpec((tr, n), lambda i: (i, 0)), compiler_params=_cp("parallel"),
    )(b, b, b, b)


SEM = pl.BlockSpec(memory_space=pltpu.SEMAPHORE)
ANY = pl.BlockSpec(memory_space=pl.ANY)
DATAFLOW = pltpu.SideEffectType.DATAFLOW_SIDE_EFFECTING


def _remote(src, dst, send_sems, recv_sems, k, to):
    return pltpu.make_async_remote_copy(src_ref=src, dst_ref=dst, send_sem=send_sems.at[k], recv_sem=recv_sems.at[k],
                                        device_id=to, device_id_type=MESH)


def _split_copy(name, src, land_shape, land_dtype, n, plan, after=None):
    after = jnp.zeros((8, LANES), F32) if after is None else after

    def start_body(src_ref, land_ref, after_ref, send_sems, recv_sems, src_thru, land_thru, token):
        for cp in plan(src_ref, land_ref, send_sems, recv_sems)[0]:
            cp.start()
        token[...] = jnp.zeros_like(token)

    sems = pltpu.SemaphoreType.DMA((n,))
    send_sems, recv_sems, src_thru, land_thru, token = pl.pallas_call(
        start_body, name=name + "_start",
        out_shape=(sems, sems, pltpu.HBM(src.shape, src.dtype), pltpu.HBM(land_shape, land_dtype),
                   jax.ShapeDtypeStruct((8, LANES), F32)),
        in_specs=(HBM, HBM, ANY), out_specs=(SEM, SEM, HBM, HBM, pl.BlockSpec(memory_space=pltpu.VMEM)),
        input_output_aliases={0: 2, 1: 3}, compiler_params=pltpu.CompilerParams(has_side_effects=DATAFLOW),
    )(pltpu.with_memory_space_constraint(src, pltpu.HBM),
      pltpu.with_memory_space_constraint(lax.empty(land_shape, land_dtype), pltpu.HBM), after)

    def wait(*after):
        def wait_body(src_ref, land_ref, send_sems, recv_sems, *rest):
            sent, received = plan(src_ref, land_ref, send_sems, recv_sems)
            for cp in sent:
                cp.wait_send()
            for cp in received:
                cp.wait_recv()

        return pl.pallas_call(
            wait_body, name=name + "_wait",
            out_shape=(pltpu.HBM(src.shape, src.dtype), pltpu.HBM(land_shape, land_dtype)),
            in_specs=(HBM, HBM, SEM, SEM) + (ANY,) * len(after), out_specs=(HBM, HBM),
            input_output_aliases={0: 0, 1: 1}, compiler_params=pltpu.CompilerParams(has_side_effects=DATAFLOW),
        )(src_thru, land_thru, send_sems, recv_sems, *after)

    return token, wait


def _behind(x, token):
    return x + token[0, 0]


def _plan_gather(src_ref, land_ref, send_sems, recv_sems):
    x, y, c, chips = _place()
    sent = [_remote(src_ref, land_ref.at[2 * x + y], send_sems, recv_sems, j, (px, py, c))
            for j, (px, py) in enumerate(chips)]
    received = [_remote(src_ref, land_ref.at[2 * px + py], send_sems, recv_sems, j, (px, py, c))
                for j, (px, py) in enumerate(chips)]
    return sent, received


def _plan_swap(src_ref, land_ref, send_sems, recv_sems):
    x, y, c, _ = _place()
    _, other_half = _halves(c, src_ref.shape[1], 8)
    cp = _remote(src_ref.at[pl.ds(0, src_ref.shape[0]), other_half], land_ref, send_sems, recv_sems, 0, (x, y, 1 - c))
    return [cp], [cp]


def _plan_scatter(src_ref, land_ref, send_sems, recv_sems):
    x, y, c, chips = _place()
    sent = [_remote(src_ref.at[2 * px + py], land_ref.at[2 * x + y], send_sems, recv_sems, j, (px, py, c))
            for j, (px, py) in enumerate(chips)]
    received = [_remote(src_ref.at[2 * px + py], land_ref.at[2 * px + py], send_sems, recv_sems, j, (px, py, c))
                for j, (px, py) in enumerate(chips)]
    return sent, received


def _plan_share(src_ref, land_ref, send_sems, recv_sems):
    x, y, c, _ = _place()
    mine_half, other_half = _halves(c, land_ref.shape[0], 8)
    return ([_remote(src_ref, land_ref.at[mine_half], send_sems, recv_sems, 0, (x, y, 1 - c))],
            [_remote(src_ref, land_ref.at[other_half], send_sems, recv_sems, 0, (x, y, 1 - c))])


def _pack_shard_rows(name, parts):
    rows = [t.shape[0] // N_CHIP for t in parts]
    n, total = parts[0].shape[1], sum(t.shape[0] // N_CHIP for t in parts)
    slab, at = None, 0
    for i, (t, r) in enumerate(zip(parts, rows)):
        tr = max(c for c in range(8, min(r, 512) + 1, 8) if r % c == 0 and at % c == 0)
        nb, ob = r // tr, at // tr

        def body(t_ref, *rest):
            rest[-1][0] = t_ref[...]

        slab = pl.pallas_call(
            body, name=f"{name}_{i}", grid=(N_CHIP, nb), out_shape=jax.ShapeDtypeStruct((N_CHIP, total, n), t.dtype),
            in_specs=[pl.BlockSpec((tr, n), functools.partial(lambda k, j, nb: (k * nb + j, 0), nb=nb))]
            + ([] if slab is None else [pl.BlockSpec(memory_space=pl.ANY)]),
            out_specs=pl.BlockSpec((1, tr, n), functools.partial(lambda k, j, ob: (k, ob + j, 0), ob=ob)),
            input_output_aliases={} if slab is None else {1: 0}, compiler_params=_cp("parallel", "parallel"),
        )(*((t,) if slab is None else (t, slab)))
        at += r
    return slab


class _GatherBehind:
    def __init__(self, name, shard, chip, after=None):
        self.chip = chip
        self.token, self.wait = _split_copy(name, shard, (N_CHIP,) + shard.shape, shard.dtype, 3, _plan_gather,
                                            after)

    def result(self, *after):
        shard, land = self.wait(*after)
        return lax.dynamic_update_slice(land, shard[None], (self.chip, 0, 0))


class _ReduceBehind:
    def __init__(self, name, chip, c, c_idx):
        self.name, self.chip, self.c, self.c_idx = name, chip, c, c_idx

    def start(self, *grads):
        return self.start_slab(_pack_shard_rows(self.name + "_pack", grads))

    def start_slab(self, g):
        n_sh, rows, n = g.shape
        token, self.wait = _split_copy(self.name + "_swap", g, (n_sh, rows // 2, n), g.dtype, 1, _plan_swap)
        return token

    def pair(self, after):
        g, a = self.wait(after)
        h = _add_half(self.name + "_pair", g, a, self.c_idx)
        token, self.wait = _split_copy(self.name + "_scatter", h, h.shape, h.dtype, 3, _plan_scatter)
        return token

    def total(self, after):
        h, b = self.wait(after)
        b = lax.dynamic_update_slice(b, lax.dynamic_slice_in_dim(h, self.chip, 1, axis=0), (self.chip, 0, 0))
        f = _sum_chips(self.name + "_sum", b)
        token, self.wait = _split_copy(self.name + "_share", f, (2 * f.shape[0], f.shape[1]), f.dtype, 1,
                                       _plan_share)
        return token

    def result(self, after):
        f, out = self.wait(after)
        return lax.dynamic_update_slice(out, f, (self.c * f.shape[0], 0))


class _ReduceColsBehind(_ReduceBehind):
    def start(self, g_padded):
        g = _unpack_w_in_grad(g_padded)
        n = g.shape[1] // N_CHIP
        return self.start_slab(jnp.stack([g[:, k * n:(k + 1) * n] for k in range(N_CHIP)]))


def _f_adamw(w, g, m, v):
    m = ADAM_B1 * m + (1.0 - ADAM_B1) * g
    v = ADAM_B2 * v + (1.0 - ADAM_B2) * (g * g)
    m_hat = m / (1.0 - ADAM_B1 ** ADAM_STEP)
    v_hat = v / (1.0 - ADAM_B2 ** ADAM_STEP)
    return -ADAM_LR * (m_hat / (jnp.sqrt(v_hat) + ADAM_EPS) + ADAM_WD * w), m, v


def _adamw(name, w, g, m, v):
    rows, n = w.shape
    return _rowwise(name, lambda w, g, m, v: (_f_adamw(w, g, m, v), ()), rows, [(t, n, 0) for t in (w, g, m, v)], [],
                    [(n, F32)] * 3, [], tm=_row_tile(rows, 8, 256))


def _pack_rows(parts):
    rows = []
    for t in parts:
        t = t.reshape(-1)
        rows.append(jnp.pad(t, (0, -t.shape[0] % LANES)).reshape(-1, LANES))
    out = jnp.concatenate(rows, axis=0)
    return jnp.pad(out, ((0, -out.shape[0] % 8), (0, 0)))


def _unpack_rows(packed, shapes):
    out, r = [], 0
    for shp in shapes:
        n = int(np.prod(shp))
        nr = -(-n // LANES)
        out.append(packed[r:r + nr].reshape(-1)[:n].reshape(shp))
        r += nr
    return out


def _sum_blocks(name, g):
    def body(g_ref, o_ref):
        acc = g_ref[0]
        for k in range(1, g.shape[0]):
            acc = acc + g_ref[k]
        o_ref[...] = acc

    return pl.pallas_call(body, name=name, out_shape=jax.ShapeDtypeStruct(g.shape[1:], F32))(g)


def _silu(t):
    return t * _sigmoid(t)


def _ada_fwd(cc, w_ada):
    n = w_ada.shape[1]
    tn = _row_tile(n, LANES, 512)

    def body(cc_ref, w_ref, o_ref):
        o_ref[...] = _nn(_silu(cc_ref[...]), w_ref[...])

    return pl.pallas_call(
        body, name="ada_fwd", grid=(n // tn,), out_shape=jax.ShapeDtypeStruct((cc.shape[0], n), F32),
        in_specs=[pl.BlockSpec(cc.shape, lambda j: (0, 0)), pl.BlockSpec((w_ada.shape[0], tn), lambda j: (0, j))],
        out_specs=pl.BlockSpec((cc.shape[0], tn), lambda j: (0, j)), compiler_params=_cp("parallel"),
    )(cc, w_ada)


def _ada_bwd(cc, dm, w_ada):
    d, n = w_ada.shape
    tn = _row_tile(n, LANES, 512)

    def body(cc_ref, dm_ref, w_ref, gw_ref, ds_ref):
        @pl.when(pl.program_id(0) == 0)
        def _():
            ds_ref[...] = jnp.zeros_like(ds_ref)

        gw_ref[...] = _raw_dot("tn", _silu(cc_ref[...]), dm_ref[...], True)
        ds_ref[...] += _raw_dot("nt", dm_ref[...], w_ref[...], False)

    return pl.pallas_call(
        body, name="ada_bwd", grid=(n // tn,),
        out_shape=[jax.ShapeDtypeStruct((d, n), F32), jax.ShapeDtypeStruct(cc.shape, F32)],
        in_specs=[pl.BlockSpec(cc.shape, lambda j: (0, 0)), pl.BlockSpec((cc.shape[0], tn), lambda j: (0, j)),
                  pl.BlockSpec((d, tn), lambda j: (0, j))],
        out_specs=[pl.BlockSpec((d, tn), lambda j: (0, j)), pl.BlockSpec(cc.shape, lambda j: (0, 0))],
        compiler_params=_cp("arbitrary"),
    )(cc, dm, w_ada)


def _c_ctx_grad(parts, c_ctx):
    def body(p_ref, c_ref, o_ref):
        ds = ((p_ref[0] + p_ref[1]) + p_ref[2]) + p_ref[3]
        _, vjp = jax.vjp(_silu, c_ref[...])
        o_ref[...] = vjp(ds)[0]

    return pl.pallas_call(body, name="c_ctx_grad", out_shape=jax.ShapeDtypeStruct(c_ctx.shape, F32))(parts, c_ctx)


def kernel(x, c, ctx, c_ctx, w_ada, b_ada, g_pre_mix, g_post_mix, g_pre_ffn, g_post_ffn, w_in, attn_sink, w_gate_fwd, b_gate_fwd, w_gate_bwd, b_gate_bwd, g_gla_norm, w_out, w_ffn_in, w_ffn_out, loss_target, m_c_ctx, m_w_ada, m_b_ada, m_g_pre_mix, m_g_post_mix, m_g_pre_ffn, m_g_post_ffn, m_w_in, m_attn_sink, m_w_gate_fwd, m_b_gate_fwd, m_w_gate_bwd, m_b_gate_bwd, m_g_gla_norm, m_w_out, m_w_ffn_in, m_w_ffn_out, v_c_ctx, v_w_ada, v_b_ada, v_g_pre_mix, v_g_post_mix, v_g_pre_ffn, v_g_post_ffn, v_w_in, v_attn_sink, v_w_gate_fwd, v_b_gate_fwd, v_w_gate_bwd, v_b_gate_bwd, v_g_gla_norm, v_w_out, v_w_ffn_in, v_w_ffn_out):
    xi, yi, ci = lax.axis_index("x"), lax.axis_index("y"), lax.axis_index("c")
    dev, chip = 4 * xi + 2 * yi + ci, 2 * xi + yi
    c_idx = jnp.reshape(ci, (1,)).astype(jnp.int32)
    d = x.shape[-1]
    n_ada, n_in, n_f = w_ada.shape[-1], w_in.shape[-1], w_ffn_in.shape[-1]
    r_out, r_f = w_out.shape[1], w_ffn_out.shape[1]
    n_gate = w_gate_fwd.shape[-1]
    by_chip = lambda t: t[0::2]

    rc = -(-d // LANES)
    g1 = _ag_small("gather_cond", _pack_rows([c[0], w_gate_fwd[0], w_gate_bwd[0]]))
    c_all = g1[:, :rc].reshape(N_DEV, -1)[:, :d]
    gr = GATE_RANK * n_gate // LANES
    gate_full = lambda off: jnp.transpose(by_chip(g1)[:, off:off + gr].reshape(N_CHIP, GATE_RANK, n_gate),
                                          (1, 0, 2)).reshape(GATE_RANK, N_CHIP * n_gate)
    wgf, wgb = gate_full(rc), gate_full(rc + gr)
    cc = jnp.concatenate([c_all, c_ctx[None, :], jnp.zeros((7, d), F32)], axis=0)

    g2 = _ag_small("gather_ada", _ada_fwd(cc, w_ada[0]).reshape(-1, LANES))
    ada_all = jnp.transpose(by_chip(g2).reshape(N_CHIP, 16, n_ada), (1, 0, 2)).reshape(16, N_CHIP * n_ada) + b_ada
    first = _GatherBehind("gather_w_in", w_in[0].astype(BF16), chip, g2)
    late_slab = jnp.concatenate([w_out[0], w_ffn_out[0], jnp.transpose(w_ffn_in[0])], axis=0).astype(BF16)
    late = []

    def first_weights(*after):
        w_in_g = first.result(*after, late_slab)
        late.append(_GatherBehind("gather_late", late_slab, chip, w_in_g))
        return _pack_w_in(jnp.concatenate([w_in_g[k] for k in range(N_CHIP)], axis=1)), late[0].token

    def late_weights(after):
        t = late[0].result(after)
        r1, r2 = r_out, r_out + r_f
        return (t[:, :r1].reshape(N_CHIP * r_out, d), t[:, r2:].reshape(N_CHIP * n_f, d),
                t[:, r1:r2].reshape(N_CHIP * r_f, d))

    ada_all = _behind(ada_all, first.token)
    ada = lax.dynamic_slice(ada_all, (dev, 0), (1, N_CHIP * n_ada))
    ada_c = ada_all[N_DEV:N_DEV + 1]

    w = _prep_gate_weights(wgf, wgb)
    w.update(w_in=first_weights, g_pre_mix=g_pre_mix, g_post_mix=g_post_mix, g_pre_ffn=g_pre_ffn, g_post_ffn=g_post_ffn,
             attn_sink=attn_sink, b_gate_fwd=b_gate_fwd, b_gate_bwd=b_gate_bwd, g_gla_norm=g_gla_norm)

    reduce_behind = _ReduceBehind("reduce_late", chip, ci, c_idx)
    reduce_w_in = _ReduceColsBehind("reduce_w_in", chip, ci, c_idx)
    loss_lanes, grad_x, g, d_ada, d_ada_c = _local_step(x[0], ctx[0], loss_target[0], ada, ada_c, w, late_weights,
                                                        reduce_behind, reduce_w_in)

    small = ("g_pre_mix", "g_post_mix", "g_pre_ffn", "g_post_ffn", "attn_sink", "b_gate_fwd", "b_gate_bwd",
             "g_gla_norm", "w_gate_fwd", "w_gate_bwd")
    shapes = [(1, 6 * d)] * 2 + [g[n].shape for n in small] + [(1, LANES)]
    g3 = _ag_small("gather_small_grads", _pack_rows([d_ada, d_ada_c] + [g[n] for n in small] + [loss_lanes]))
    tot = dict(zip(("d_ada", "d_ada_c") + small + ("loss",),
                   _unpack_rows(_sum_blocks("sum_small_grads", g3), shapes)))
    r_ada = 6 * d // LANES
    dm = jnp.concatenate([g3[:, :r_ada].reshape(N_DEV, 6 * d), tot["d_ada_c"], jnp.zeros((7, 6 * d), F32)], axis=0)
    grads = {n: tot[n] for n in small[:8]}
    grads["b_ada"] = _sum_blocks("sum_b_ada", dm.reshape(16, r_ada, LANES)).reshape(1, 6 * d)
    grads["w_gate_fwd"] = lax.dynamic_slice(tot["w_gate_fwd"], (0, chip * n_gate), (GATE_RANK, n_gate))[None]
    grads["w_gate_bwd"] = lax.dynamic_slice(tot["w_gate_bwd"], (0, chip * n_gate), (GATE_RANK, n_gate))[None]
    gw_ada, dsc = _ada_bwd(cc, lax.dynamic_slice(dm, (0, chip * n_ada), (16, n_ada)), w_ada[0])
    grads["w_ada"] = gw_ada[None]
    g4 = _ag_small("gather_c_ctx", _pack_rows([dsc[N_DEV]]))
    grads["c_ctx"] = _c_ctx_grad(by_chip(g4), _pack_rows([c_ctx])).reshape(-1)[:d]

    grads["w_in"] = reduce_w_in.result(g4)[None]
    behind = g["behind"]
    grads["w_ffn_in"] = jnp.transpose(behind[:n_f])[None]
    grads["w_ffn_out"], grads["w_out"] = behind[None, n_f:n_f + r_f], behind[None, n_f + r_f:]

    names = ("c_ctx", "w_ada", "b_ada", "g_pre_mix", "g_post_mix", "g_pre_ffn", "g_post_ffn", "w_in", "attn_sink",
             "w_gate_fwd", "b_gate_fwd", "w_gate_bwd", "b_gate_bwd", "g_gla_norm", "w_out", "w_ffn_in", "w_ffn_out")
    weights = dict(zip(names, (c_ctx, w_ada, b_ada, g_pre_mix, g_post_mix, g_pre_ffn, g_post_ffn, w_in, attn_sink,
                               w_gate_fwd, b_gate_fwd, w_gate_bwd, b_gate_bwd, g_gla_norm, w_out, w_ffn_in,
                               w_ffn_out)))
    m_in = dict(zip(names, (m_c_ctx, m_w_ada, m_b_ada, m_g_pre_mix, m_g_post_mix, m_g_pre_ffn, m_g_post_ffn, m_w_in,
                            m_attn_sink, m_w_gate_fwd, m_b_gate_fwd, m_w_gate_bwd, m_b_gate_bwd, m_g_gla_norm,
                            m_w_out, m_w_ffn_in, m_w_ffn_out)))
    v_in = dict(zip(names, (v_c_ctx, v_w_ada, v_b_ada, v_g_pre_mix, v_g_post_mix, v_g_pre_ffn, v_g_post_ffn, v_w_in,
                            v_attn_sink, v_w_gate_fwd, v_b_gate_fwd, v_w_gate_bwd, v_b_gate_bwd, v_g_gla_norm,
                            v_w_out, v_w_ffn_in, v_w_ffn_out)))
    large = ("w_ada", "w_in", "w_out", "w_ffn_in", "w_ffn_out")
    tiny = tuple(n for n in names if n not in large)
    delta, new_m, new_v = {}, {}, {}
    for n in large:
        dl, nm, nv = _adamw("adamw_" + n, weights[n][0], grads[n][0], m_in[n][0], v_in[n][0])
        delta[n], new_m[n], new_v[n] = dl[None], nm[None], nv[None]
    tiny_shapes = [weights[n].shape for n in tiny]
    packed = [_pack_rows([t[n] for n in tiny]) for t in (weights, grads, m_in, v_in)]
    for out, res in zip((delta, new_m, new_v), _adamw("adamw_small", *packed)):
        out.update(zip(tiny, _unpack_rows(res, tiny_shapes)))
    for n in tiny:
        grads[n] = grads[n].reshape(weights[n].shape)

    return (tot["loss"][0, 0], grad_x[None], *[grads[n] for n in names], *[delta[n] for n in names], *[new_m[n] for n in names],
            *[new_v[n] for n in names])
```

```python
import functools

import jax
import jax.numpy as jnp
import numpy as np
from jax import lax
from jax.experimental import pallas as pl
from jax.experimental.pallas import tpu as pltpu

F32 = jnp.float32
BF16 = jnp.bfloat16
MESH = pl.DeviceIdType.MESH

HEAD_DIM = 64
ATT_HEADS = 8
ATT_KV_HEADS = 2
ATT_GROUP = ATT_HEADS // ATT_KV_HEADS
WINDOW = 128
BLOCK = 128
GRID_W = 64
ROPE_BASE = 10000.0
GLA_HEADS = 8
GLA_DK = 32
GLA_DV = 64
GLA_CHUNK = 64
GATE_RANK = 16
GATE_TAU = 16.0
NEG_INF = -1e30
QW = ATT_HEADS * HEAD_DIM
KVW = ATT_KV_HEADS * HEAD_DIM
GKW = GLA_HEADS * GLA_DK
GVW = GLA_HEADS * GLA_DV
IN_COLS = QW + 2 * KVW + 2 * GKW + 2 * GVW + 2 * GATE_RANK
LANES = 128
IN_PAD = IN_COLS + LANES - 2 * GATE_RANK
C_Q, C_GV, C_GG = 0, QW, QW + GVW
C_K = C_GG + GVW
C_V = C_K + KVW
C_GQ = C_V + KVW
C_GK = C_GQ + GKW
C_Z = C_GK + GKW
MIX = QW + GVW

ADAM_LR, ADAM_B1, ADAM_B2, ADAM_EPS, ADAM_WD, ADAM_STEP = 0.001, 0.9, 0.999, 1e-08, 0.01, 10

VMEM_LIMIT = 56 * 1024 * 1024


def _cp(*sem):
    return pltpu.CompilerParams(dimension_semantics=sem, vmem_limit_bytes=VMEM_LIMIT)


def _pick(n, cands):
    for t in cands:
        if n % t == 0:
            return t
    return n


_DIMS = {"nn": (((1,), (0,)), ((), ())), "nt": (((1,), (1,)), ((), ())), "tn": (((0,), (0,)), ((), ()))}


def _raw_dot(mode, a, b, hi):
    dot = lambda u, v: lax.dot_general(u, v, _DIMS[mode], preferred_element_type=F32)
    if hi:
        a, b = a.astype(F32), b.astype(F32)
        a_hi, b_hi = a.astype(BF16), b.astype(BF16)
        a_lo, b_lo = (a - a_hi.astype(F32)).astype(BF16), (b - b_hi.astype(F32)).astype(BF16)
        return dot(a_hi, b_hi) + (dot(a_lo, b_hi) + dot(a_hi, b_lo))
    return dot(a.astype(BF16), b.astype(BF16))


def _make_dot(mode, hi):
    @jax.custom_vjp
    def dot(a, b):
        return _raw_dot(mode, a, b, hi)

    def fwd(a, b):
        return _raw_dot(mode, a, b, hi), (a, b)

    def bwd(res, dc):
        a, b = res
        if mode == "nn":
            return _raw_dot("nt", dc, b, hi), _raw_dot("tn", a, dc, hi)
        if mode == "nt":
            return _raw_dot("nn", dc, b, hi), _raw_dot("tn", dc, a, hi)
        return _raw_dot("nt", b, dc, hi), _raw_dot("nn", a, dc, hi)

    dot.defvjp(fwd, bwd)
    return dot


_nn, _nt, _tn = _make_dot("nn", False), _make_dot("nt", False), _make_dot("tn", False)
_nn_hi = _make_dot("nn", True)


MM_VMEM_BUDGET = 44 * 1024 * 1024


def _halvings(n):
    out = [n]
    while out[-1] % (2 * LANES) == 0:
        out.append(out[-1] // 2)
    return out


def _mm_tiles(mode, m, n, k, a_bytes, b_bytes, o_bytes, init_bytes=0):
    tms = [t for t in dict.fromkeys((m, m // 2, m // 4, 2048, 1024, 512, 256, 128))
           if m % t == 0 and t % (LANES if mode == "tn" else 16) == 0 and t <= 4096] or [m]
    if mode == "tn":
        fits = [(k // tk + 0.5 * (m // tm), tm, tk)
                for tk in (4096, 2048, 1024, 512, 256, 128) if k % tk == 0 for tm in tms
                if 2 * (tk * tm * a_bytes + tk * n * b_bytes + tm * n * (o_bytes + init_bytes)) <= MM_VMEM_BUDGET]
        if fits:
            _, tm, tk = min(fits)
            return tm, n, tk
    tks = ([t for t in (512, 256, 128) if k % t == 0] or [k]) if mode == "tn" else _halvings(k)
    for tn in _halvings(n):
        for tk in tks:
            for tm in tms:
                acc = tm * tn * 4 if (k // tk > 1 and o_bytes != 4) else 0
                tiles = tm * tk * a_bytes + tk * tn * b_bytes + tm * tn * (o_bytes + init_bytes)
                if 2 * tiles + acc <= MM_VMEM_BUDGET:
                    return tm, tn, tk
    return tms[-1], _halvings(n)[-1], tks[-1]


def _mm(name, a, b, mode, out_dtype=F32, init=None, after=None):
    follow = () if after is None else (after,)
    if mode == "nn":
        (m, k), n = a.shape, b.shape[1]
    elif mode == "nt":
        (m, k), n = a.shape, b.shape[0]
    else:
        (k, m), n = a.shape, b.shape[1]
    tm, tn, tk = _mm_tiles(mode, m, n, k, a.dtype.itemsize, b.dtype.itemsize, jnp.dtype(out_dtype).itemsize,
                           0 if init is None else 4)
    nk = k // tk
    use_acc = nk > 1 and out_dtype != F32

    inits = () if init is None else (init,)

    def body(a_ref, b_ref, *rest):
        rest = rest[:len(inits)] + rest[len(inits) + len(follow):]
        o_ref, acc = rest[len(inits)], rest[len(inits) + 1:]
        part = _raw_dot(mode, a_ref[...], b_ref[...], False)
        first = lambda: part + rest[0][...] if inits else part
        if nk == 1:
            o_ref[...] = first().astype(o_ref.dtype)
            return
        acc_ref = acc[0] if use_acc else o_ref
        kk = pl.program_id(2)

        @pl.when(kk == 0)
        def _():
            acc_ref[...] = first()

        @pl.when(kk > 0)
        def _():
            acc_ref[...] += part

        if use_acc:
            @pl.when(kk == nk - 1)
            def _():
                o_ref[...] = acc_ref[...].astype(o_ref.dtype)

    if mode == "nn":
        a_spec = pl.BlockSpec((tm, tk), lambda i, j, kk: (i, kk))
        b_spec = pl.BlockSpec((tk, tn), lambda i, j, kk: (kk, j))
    elif mode == "nt":
        a_spec = pl.BlockSpec((tm, tk), lambda i, j, kk: (i, kk))
        b_spec = pl.BlockSpec((tn, tk), lambda i, j, kk: (j, kk))
    else:
        a_spec = pl.BlockSpec((tk, tm), lambda i, j, kk: (kk, i))
        b_spec = pl.BlockSpec((tk, tn), lambda i, j, kk: (kk, j))
    return pl.pallas_call(
        body, name=name, grid=(m // tm, n // tn, nk),
        in_specs=[a_spec, b_spec] + [pl.BlockSpec((tm, tn), lambda i, j, kk: (i, j))] * len(inits)
        + [pl.BlockSpec(memory_space=pl.ANY)] * len(follow),
        out_specs=pl.BlockSpec((tm, tn), lambda i, j, kk: (i, j)),
        out_shape=jax.ShapeDtypeStruct((m, n), out_dtype),
        scratch_shapes=[pltpu.VMEM((tm, tn), F32)] if use_acc else [],
        compiler_params=_cp("parallel", "parallel", "arbitrary"),
    )(a, b, *inits, *follow)


def _rowwise(name, fn, rows, row_ins, full_ins, row_outs, acc_outs, tm=None):
    tm = tm or _pick(rows, (512, 256, 128))
    n_r, n_f, n_o, n_a = len(row_ins), len(full_ins), len(row_outs), len(acc_outs)

    def body(*refs):
        ins, outs = refs[:n_r + n_f], refs[n_r + n_f:]
        vals = [r[...].astype(F32) for r in ins]
        ro, ao = fn(*vals)
        for r, val in zip(outs[:n_o], ro):
            r[...] = val.astype(r.dtype)
        if n_a:
            @pl.when(pl.program_id(0) == 0)
            def _():
                for r in outs[n_o:]:
                    r[...] = jnp.zeros_like(r)

            for r, val in zip(outs[n_o:], ao):
                r[...] += val

    in_specs = [pl.BlockSpec((tm, w), functools.partial(lambda i, cb: (i, cb), cb=cb)) for _, w, cb in row_ins]
    in_specs += [pl.BlockSpec(a.shape, lambda i: (0, 0)) for a in full_ins]
    out_specs = [pl.BlockSpec((tm, w), lambda i: (i, 0)) for w, _ in row_outs]
    out_specs += [pl.BlockSpec(s, lambda i: (0, 0)) for s in acc_outs]
    out_shape = [jax.ShapeDtypeStruct((rows, w), dt) for w, dt in row_outs]
    out_shape += [jax.ShapeDtypeStruct(s, F32) for s in acc_outs]
    return pl.pallas_call(
        body, name=name, grid=(rows // tm,), in_specs=in_specs, out_specs=out_specs, out_shape=out_shape,
        compiler_params=_cp("arbitrary" if n_a else "parallel"),
    )(*[a for a, _, _ in row_ins], *full_ins)


def _rn(x):
    return x * lax.rsqrt(jnp.mean(x * x, axis=-1, keepdims=True) + 1e-6)


def _sigmoid(t):
    return 1.0 / (1.0 + jnp.exp(-t))


def _f_norm_mod(x, g, sh, sc):
    return _rn(x) * g * (1.0 + sc) + sh


def _f_post_res(xr, y, g, gate):
    return xr + gate * (_rn(y) * g)


@jax.custom_vjp
def _f_swiglu(g, u):
    return g * _sigmoid(g) * u


def _f_swiglu_fwd(g, u):
    s = _sigmoid(g)
    return g * s * u, (g, u, s)


def _f_swiglu_bwd(res, da):
    g, u, s = res
    gs = g * s
    return da * u * (s + gs * (1.0 - s)), da * gs


_f_swiglu.defvjp(_f_swiglu_fwd, _f_swiglu_bwd)


def _logsig(u):
    return jnp.minimum(u, 0.0) - jnp.log(1.0 + jnp.exp(-jnp.abs(u)))


def _f_gate(z, wf, wb, bf, bb):
    return _logsig(_nn(z, wf) + bf) / GATE_TAU, _logsig(_nn(z, wb) + bb) / GATE_TAU


def _f_gla_out(of, ob, gg, gt, bd):
    o = of + ob
    ms = _nn_hi(o * o, bd)
    return o * lax.rsqrt(ms + 1e-6) * gt * (gg * _sigmoid(gg))


def _norm_mod(name, x, g, sh, sc):
    rows, d = x.shape
    return _rowwise(name, lambda x, g, sh, sc: ((_f_norm_mod(x, g, sh, sc),), ()), rows,
                    [(x, d, 0)], [g, sh, sc], [(d, BF16)], [])[0]


def _norm_mod_bwd(name, dh, dres, x, g, sh, sc):
    rows, d = x.shape

    def fn(dh, dres, x, g, sh, sc):
        _, vjp = jax.vjp(_f_norm_mod, x, g, sh, sc)
        dx, dg, dsh, dsc = vjp(dh)
        return (dx + dres,), (dg, dsh, dsc)

    return _rowwise(name, fn, rows, [(dh, d, 0), (dres, d, 0), (x, d, 0)], [g, sh, sc], [(d, F32)],
                    [(1, d)] * 3)


def _post_res_norm_mod(name, xr, y, g_post, gate, g_pre, sh, sc):
    rows, d = xr.shape

    def fn(xr, y, g_post, gate, g_pre, sh, sc):
        x1 = _f_post_res(xr, y, g_post, gate)
        return (x1, _f_norm_mod(x1, g_pre, sh, sc)), ()

    return _rowwise(name, fn, rows, [(xr, d, 0), (y, d, 0)], [g_post, gate, g_pre, sh, sc], [(d, F32), (d, BF16)], [])


def _norm_mod_post_res_bwd(name, dh, dres, x1, y, g_pre, sh, sc, g_post, gate):
    rows, d = x1.shape

    def fn(dh, dres, x1, y, g_pre, sh, sc, g_post, gate):
        _, vjp_norm = jax.vjp(_f_norm_mod, x1, g_pre, sh, sc)
        dx1, dg_pre, dsh, dsc = vjp_norm(dh)
        dx1 = dx1 + dres
        _, vjp_res = jax.vjp(lambda y, g, gate: _f_post_res(jnp.zeros_like(y), y, g, gate), y, g_post, gate)
        dy, dg_post, dgate = vjp_res(dx1)
        return (dx1, dy), (dg_pre, dsh, dsc, dg_post, dgate)

    return _rowwise(name, fn, rows, [(dh, d, 0), (dres, d, 0), (x1, d, 0), (y, d, 0)], [g_pre, sh, sc, g_post, gate],
                    [(d, F32), (d, BF16)], [(1, d)] * 5, tm=_pick(rows, (256, 128)))


def _post_res_loss(name, xr, y, g, gate, target):
    rows, d = xr.shape

    def fn(xr, y, target, g, gate):
        x2, vjp = jax.vjp(lambda y, g, gate: _f_post_res(xr, y, g, gate), y, g, gate)
        diff = x2 - target
        part = 0.5 * jnp.sum(jnp.mean(diff * diff, axis=-1, keepdims=True), axis=0, keepdims=True)
        dx2 = diff * (1.0 / d)
        dy, dg, dgate = vjp(dx2)
        return (dx2, dy), (jnp.broadcast_to(part, (1, LANES)), dg, dgate)

    return _rowwise(name, fn, rows, [(xr, d, 0), (y, d, 0), (target, d, 0)], [g, gate], [(d, F32), (d, BF16)],
                    [(1, LANES), (1, d), (1, d)])


def _mm_rows(name, a, b, mode, fn, extras, outs):
    m, k = a.shape
    tm = _pick(m, (256, 128))

    def body(a_ref, b_ref, *rest):
        tiles = fn(_raw_dot(mode, a_ref[...], b_ref[...], False), *[e[...] for e in rest[:len(extras)]])
        for r, val in zip(rest[len(extras):], tiles):
            r[...] = val.astype(r.dtype)

    row = lambda w: pl.BlockSpec((tm, w), lambda i: (i, 0))
    return pl.pallas_call(
        body, name=name, grid=(m // tm,),
        in_specs=[row(k), pl.BlockSpec(b.shape, lambda i: (0, 0))] + [row(e.shape[1]) for e in extras],
        out_specs=[row(w) for w, _ in outs], out_shape=[jax.ShapeDtypeStruct((m, w), dt) for w, dt in outs],
        compiler_params=_cp("parallel"),
    )(a, b, *extras)


def _ffn_in_swiglu(name, h, w_t):
    f = w_t.shape[0] // 2
    fn = lambda u: (u, _f_swiglu(u[:, :f], u[:, f:]))
    return _mm_rows(name, h, w_t, "nt", fn, [], [(2 * f, BF16), (f, BF16)])


def _ffn_out_dx_swiglu_bwd(name, df, w_out, u):
    f = w_out.shape[0]

    def fn(da, u):
        u = u.astype(F32)
        _, vjp = jax.vjp(_f_swiglu, u[:, :f], u[:, f:])
        return (jnp.concatenate(vjp(da), axis=1),)

    return _mm_rows(name, df, w_out, "nt", fn, [u], [(2 * f, BF16)])[0]


def _gate_fwd(name, p, wf, wb, bf, bb):
    rows = p.shape[0]
    return _rowwise(name, lambda z, wf, wb, bf, bb: (_f_gate(z, wf, wb, bf, bb), ()), rows,
                    [(p, LANES, C_Z // LANES)], [wf, wb, bf, bb], [(GKW, F32)] * 2, [])


def _gate_bwd(name, p, dla_f, dla_b, wf, wb, bf, bb):
    rows = p.shape[0]

    def fn(z, dlf, dlb, wf, wb, bf, bb):
        _, vjp = jax.vjp(_f_gate, z, wf, wb, bf, bb)
        dz, dwf, dwb, dbf, dbb = vjp((dlf, dlb))
        return (dz,), (dwf, dwb, dbf, dbb)

    return _rowwise(name, fn, rows, [(p, LANES, C_Z // LANES), (dla_f, GKW, 0), (dla_b, GKW, 0)],
                    [wf, wb, bf, bb], [(LANES, BF16)], [(LANES, GKW), (LANES, GKW), (1, GKW), (1, GKW)])


def _head_mean_matrix():
    h = np.arange(GVW) // GLA_DV
    return jnp.asarray((h[:, None] == h[None, :]).astype(np.float32) / GLA_DV)


def _gla_out(name, attn, of, ob, p, gt):
    rows = of.shape[0]
    bd = _head_mean_matrix()
    fn = lambda attn, of, ob, gg, gt, bd: ((jnp.concatenate([attn, _f_gla_out(of, ob, gg, gt, bd)], axis=1),), ())
    return _rowwise(name, fn, rows, [(attn, QW, 0), (of, GVW, 0), (ob, GVW, 0), (p, GVW, C_GG // GVW)], [gt, bd],
                    [(MIX, BF16)], [])[0]


def _gla_out_bwd(name, dmix, of, ob, p, gt):
    rows = of.shape[0]
    bd = _head_mean_matrix()

    def fn(dm, of, ob, gg, gt, bd):
        _, vjp = jax.vjp(lambda of, gg, gt: _f_gla_out(of, ob, gg, gt, bd), of, gg, gt)
        do, dgg, dgt = vjp(dm)
        return (do, dgg), (dgt,)

    return _rowwise(name, fn, rows, [(dmix, GVW, 1), (of, GVW, 0), (ob, GVW, 0), (p, GVW, C_GG // GVW)], [gt, bd],
                    [(GVW, F32), (GVW, BF16)], [(1, GVW)])


def _rope_tables(n_tokens):
    t = jnp.arange(n_tokens)
    row = (t // GRID_W).astype(F32)
    col = (t % GRID_W).astype(F32)
    half = HEAD_DIM // 2
    inv_freq = ROPE_BASE ** (-jnp.arange(0, half, 2, dtype=F32) / half)
    ang_r = row[:, None] * inv_freq[None, :]
    ang_c = col[:, None] * inv_freq[None, :]
    ang = jnp.concatenate([ang_r, ang_r, ang_c, ang_c], axis=-1)
    sign = jnp.concatenate([-jnp.ones((16,), F32), jnp.ones((16,), F32)] * 2)
    cos, sin = jnp.cos(ang), jnp.sin(ang) * sign[None, :]
    return jnp.tile(cos, (1, 2)), jnp.tile(sin, (1, 2))


def _rot_pairs(x):
    w = x.shape[-1]
    lane = lax.broadcasted_iota(jnp.int32, x.shape, x.ndim - 1)
    return jnp.where((lane % 32) < 16, pltpu.roll(x, w - 16, x.ndim - 1), pltpu.roll(x, 16, x.ndim - 1))


def _rope_apply(x, cos, sin_signed, inverse):
    reps = x.shape[-1] // LANES
    cos = jnp.concatenate([cos] * reps, axis=-1) if reps > 1 else cos
    sin = jnp.concatenate([sin_signed] * reps, axis=-1) if reps > 1 else sin_signed
    if inverse:
        return x * cos + _rot_pairs(x * sin)
    return x * cos + _rot_pairs(x) * sin


def _rope_fwd(name, p, cos, sin):
    rows = p.shape[0]

    def fn(q, k, v, cos, sin):
        return (_rope_apply(q, cos, sin, False), _rope_apply(k, cos, sin, False), v), ()

    return _rowwise(name, fn, rows, [(p, QW, 0), (p, KVW, C_K // KVW), (p, KVW, C_V // KVW), (cos, LANES, 0),
                                     (sin, LANES, 0)], [], [(QW, BF16), (KVW, BF16), (KVW, BF16)], [])


def _proj_grad(name, dq_rot, dk_rot, dv, cos, sin, gla_f, gla_b, dgg, dz):
    rows = dq_rot.shape[0]

    def fn(dq, dk, dv, cos, sin, gqf, gkf, gvf, gqb, gkb, gvb, dgg, dz):
        parts = [_rope_apply(dq, cos, sin, True), gvf + gvb, dgg, _rope_apply(dk, cos, sin, True), dv, gqf + gqb,
                 gkf + gkb, dz]
        return (jnp.concatenate(parts, axis=1),), ()

    ins = [(dq_rot, QW), (dk_rot, KVW), (dv, KVW), (cos, LANES), (sin, LANES)]
    ins += [(t, t.shape[1]) for t in (*gla_f, *gla_b)] + [(dgg, GVW), (dz, LANES)]
    return _rowwise(name, fn, rows, [(t, w, 0) for t, w in ins], [], [(IN_PAD, BF16)], [],
                    tm=_pick(rows, (256, 128)))[0]


GROUP_ROWS = ATT_GROUP * BLOCK


def _f_attn(qs, kws, vws, kcs, vcs, sink, n, n_tokens):
    row = lax.broadcasted_iota(jnp.int32, (GROUP_ROWS, 1), 0)
    group = sum((row >= g * BLOCK).astype(jnp.int32) for g in range(1, ATT_GROUP))
    i = lax.broadcasted_iota(jnp.int32, (GROUP_ROWS, 3 * BLOCK), 0) - BLOCK * group
    j = lax.broadcasted_iota(jnp.int32, (GROUP_ROWS, 3 * BLOCK), 1)
    kpos = (n - 1) * BLOCK + j
    mask = (jnp.abs(j - BLOCK - i) <= WINDOW) & (kpos >= 0) & (kpos < n_tokens)
    head_id = lax.broadcasted_iota(jnp.int32, (1, ATT_HEADS), 1)
    scale = HEAD_DIM ** -0.5
    outs = []
    for h in range(ATT_KV_HEADS):
        sk = jnp.zeros((GROUP_ROWS, 1), F32)
        for g in range(ATT_GROUP):
            one = jnp.sum(jnp.where(head_id == h * ATT_GROUP + g, sink, 0.0), axis=-1, keepdims=True)
            sk = jnp.where(group == g, one, sk)
        q = qs[h] * scale
        s_w = jnp.where(mask, _nt(q, kws[h]), NEG_INF)
        s_c = _nt(q, kcs[h])
        m = lax.stop_gradient(jnp.maximum(jnp.maximum(jnp.max(s_w, axis=-1, keepdims=True),
                                                      jnp.max(s_c, axis=-1, keepdims=True)), sk))
        pw, pc = jnp.exp(s_w - m), jnp.exp(s_c - m)
        den = jnp.sum(pw, axis=-1, keepdims=True) + jnp.sum(pc, axis=-1, keepdims=True) + jnp.exp(sk - m)
        outs.append((_nn(pw, vws[h]) + _nn(pc, vcs[h])) / den)
    return tuple(outs)


def _group_rows(ref, h):
    hs = lambda hq: slice(hq * HEAD_DIM, (hq + 1) * HEAD_DIM)
    return jnp.concatenate([ref[:, hs(h * ATT_GROUP + g)].astype(F32) for g in range(ATT_GROUP)], axis=0)


def _ungroup_rows(ref, h, val):
    for g in range(ATT_GROUP):
        hq = h * ATT_GROUP + g
        ref[:, hq * HEAD_DIM:(hq + 1) * HEAD_DIM] = val[g * BLOCK:(g + 1) * BLOCK].astype(ref.dtype)


def _attn_loads(n, q_ref, kp_ref, vp_ref, kc_ref, vc_ref):
    r0 = pl.multiple_of(n * BLOCK, BLOCK)
    hs = lambda h: slice(h * HEAD_DIM, (h + 1) * HEAD_DIM)
    qs = [_group_rows(q_ref, h) for h in range(ATT_KV_HEADS)]
    kws = [kp_ref[pl.ds(r0, 3 * BLOCK), hs(h)].astype(F32) for h in range(ATT_KV_HEADS)]
    vws = [vp_ref[pl.ds(r0, 3 * BLOCK), hs(h)].astype(F32) for h in range(ATT_KV_HEADS)]
    kcs = [kc_ref[:, hs(h)].astype(F32) for h in range(ATT_KV_HEADS)]
    vcs = [vc_ref[:, hs(h)].astype(F32) for h in range(ATT_KV_HEADS)]
    return r0, hs, qs, kws, vws, kcs, vcs


def _attn_specs(s, c):
    full = lambda shape: pl.BlockSpec(shape, lambda n: (0, 0))
    return [pl.BlockSpec((BLOCK, QW), lambda n: (n, 0)), full((s + 2 * BLOCK, KVW)), full((s + 2 * BLOCK, KVW)),
            full((c, KVW)), full((c, KVW)), full((1, ATT_HEADS))]


def _attn_fwd(q, kp, vp, kc, vc, sink):
    s, c = q.shape[0], kc.shape[0]

    def body(q_ref, kp_ref, vp_ref, kc_ref, vc_ref, sink_ref, o_ref):
        n = pl.program_id(0)
        _, hs, qs, kws, vws, kcs, vcs = _attn_loads(n, q_ref, kp_ref, vp_ref, kc_ref, vc_ref)
        outs = _f_attn(qs, kws, vws, kcs, vcs, sink_ref[...], n, s)
        for h in range(ATT_KV_HEADS):
            _ungroup_rows(o_ref, h, outs[h])

    return pl.pallas_call(
        body, name="attn_fwd", grid=(s // BLOCK,), in_specs=_attn_specs(s, c),
        out_specs=pl.BlockSpec((BLOCK, QW), lambda n: (n, 0)), out_shape=jax.ShapeDtypeStruct((s, QW), BF16),
        compiler_params=_cp("parallel"),
    )(q, kp, vp, kc, vc, sink)


def _attn_bwd(do, q, kp, vp, kc, vc, sink):
    s, c = q.shape[0], kc.shape[0]

    def body(do_ref, q_ref, kp_ref, vp_ref, kc_ref, vc_ref, sink_ref, dq_ref, dkp_ref, dvp_ref, dkc_ref, dvc_ref,
             dsink_ref):
        n = pl.program_id(0)

        @pl.when(n == 0)
        def _():
            for r in (dkp_ref, dvp_ref, dkc_ref, dvc_ref, dsink_ref):
                r[...] = jnp.zeros_like(r)

        r0, hs, qs, kws, vws, kcs, vcs = _attn_loads(n, q_ref, kp_ref, vp_ref, kc_ref, vc_ref)
        _, vjp = jax.vjp(lambda qs, kws, vws, kcs, vcs, sink: _f_attn(qs, kws, vws, kcs, vcs, sink, n, s),
                         qs, kws, vws, kcs, vcs, sink_ref[...])
        dqs, dkws, dvws, dkcs, dvcs, dsink = vjp(tuple(_group_rows(do_ref, h) for h in range(ATT_KV_HEADS)))
        for h in range(ATT_KV_HEADS):
            _ungroup_rows(dq_ref, h, dqs[h])
            dkp_ref[pl.ds(r0, 3 * BLOCK), hs(h)] += dkws[h]
            dvp_ref[pl.ds(r0, 3 * BLOCK), hs(h)] += dvws[h]
            dkc_ref[:, hs(h)] += dkcs[h]
            dvc_ref[:, hs(h)] += dvcs[h]
        dsink_ref[...] += dsink

    full = lambda shape: pl.BlockSpec(shape, lambda n: (0, 0))
    return pl.pallas_call(
        body, name="attn_bwd", grid=(s // BLOCK,),
        in_specs=[pl.BlockSpec((BLOCK, QW), lambda n: (n, 0))] + _attn_specs(s, c),
        out_specs=[pl.BlockSpec((BLOCK, QW), lambda n: (n, 0)), full((s + 2 * BLOCK, KVW)), full((s + 2 * BLOCK, KVW)),
                   full((c, KVW)), full((c, KVW)), full((1, ATT_HEADS))],
        out_shape=[jax.ShapeDtypeStruct((s, QW), F32), jax.ShapeDtypeStruct((s + 2 * BLOCK, KVW), F32),
                   jax.ShapeDtypeStruct((s + 2 * BLOCK, KVW), F32), jax.ShapeDtypeStruct((c, KVW), F32),
                   jax.ShapeDtypeStruct((c, KVW), F32), jax.ShapeDtypeStruct((1, ATT_HEADS), F32)],
        compiler_params=_cp("arbitrary"),
    )(do, q, kp, vp, kc, vc, sink)


GLA_GROUPS = 1
GLA_GROUP_HEADS = GLA_HEADS // GLA_GROUPS
GKG, GVG = GKW // GLA_GROUPS, GVW // GLA_GROUPS


def _gla_masks(heads=GLA_HEADS):
    hk = np.arange(heads * GLA_DK) // GLA_DK
    hv = np.arange(heads * GLA_DV) // GLA_DV
    head_k = (np.arange(heads)[:, None] == hk[None, :]).astype(np.float32)
    head_v = (np.arange(heads)[:, None] == hv[None, :]).astype(np.float32)
    bd_t = (hv[:, None] == hk[None, :]).astype(np.float32)
    return jnp.asarray(head_k), jnp.asarray(head_v), jnp.asarray(bd_t)


def _group_states(st):
    return jnp.stack([st[g * GVG:(g + 1) * GVG, g * GKG:(g + 1) * GKG] for g in range(GLA_GROUPS)])


def _ungroup_states(st):
    out = jnp.zeros((GVW, GKW), st.dtype)
    for g in range(GLA_GROUPS):
        out = out.at[g * GVG:(g + 1) * GVG, g * GKG:(g + 1) * GKG].set(st[g])
    return out


def _tri(n, rev, strict=False):
    i = lax.broadcasted_iota(jnp.int32, (n, n), 0)
    j = lax.broadcasted_iota(jnp.int32, (n, n), 1)
    if strict:
        keep = (j > i) if rev else (j < i)
    else:
        keep = (j >= i) if rev else (j <= i)
    return keep


def _f_gla_chunk(q, k, v, la, st, head_k, head_v, bd_t, rev):
    return _f_gla_carry(*_f_gla_intra(q, k, v, la, head_k, head_v, rev), v, st, bd_t)


def _f_gla_intra(q, k, v, la, head_k, head_v, rev):
    heads, kw, vw = head_k.shape[0], q.shape[1], v.shape[1]
    keep = _tri(GLA_CHUNK, rev)
    b = _nn_hi(keep.astype(F32), la)
    bl = jnp.sum(la, axis=0, keepdims=True)
    qd = q * (GLA_DK ** -0.5) * jnp.exp(b)
    ki = k * jnp.exp(-b)
    kd = k * jnp.exp(bl - b)
    q_heads = (qd[None, :, :] * head_k[:, None, :]).reshape(heads * GLA_CHUNK, kw)
    a_all = _nt(q_heads, ki).reshape(heads, GLA_CHUNK, GLA_CHUNK)
    a_all = jnp.where(keep[None, :, :], a_all, 0.0).reshape(heads * GLA_CHUNK, GLA_CHUNK)
    o_all = _nn(a_all, v).reshape(heads, GLA_CHUNK, vw)
    return jnp.sum(o_all * head_v[:, None, :], axis=0), qd, kd, bl


def _f_gla_carry(intra, qd, kd, bl, v, st, bd_t):
    return intra + _nt(qd, st), st * jnp.exp(bl) + bd_t * _tn(v, kd)


def _gla_specs(s, tb, order):
    return [pl.BlockSpec((tb, GKW), lambda i: (order(i), C_GQ // GKW)),
            pl.BlockSpec((tb, GKW), lambda i: (order(i), C_GK // GKW)),
            pl.BlockSpec((tb, GVW), lambda i: (order(i), C_GV // GVW)),
            pl.BlockSpec((tb, GKW), lambda i: (order(i), 0))]


GLA_BLOCK_CHUNKS = 4


def _gla_fwd(p, la_f, la_b, st_f0, st_b0):
    s = p.shape[0]
    tb = GLA_BLOCK_CHUNKS * GLA_CHUNK
    nblk = s // tb
    up, down = (lambda i: i), (lambda i: nblk - 1 - i)
    masks = _gla_masks(GLA_GROUP_HEADS)

    def scan(rev, q_ref, k_ref, v_ref, la_ref, o_ref, sts_ref, st_ref, consts):
        for g in range(GLA_GROUPS):
            gk, gv = slice(g * GKG, (g + 1) * GKG), slice(g * GVG, (g + 1) * GVG)
            st = st_ref[g]
            sts_ref[0, g] = st
            chunks = range(GLA_BLOCK_CHUNKS)
            for ci in (reversed(chunks) if rev else chunks):
                rows = slice(ci * GLA_CHUNK, (ci + 1) * GLA_CHUNK)
                o, st = _f_gla_chunk(q_ref[rows, gk], k_ref[rows, gk], v_ref[rows, gv], la_ref[rows, gk], st, *consts,
                                     rev)
                o_ref[rows, gv] = o
            st_ref[g] = st

    def body(qf, kf, vf, laf, qb, kb, vb, lab, stf0, stb0, hk_ref, hv_ref, bd_ref, of_ref, stsf_ref, ob_ref, stsb_ref,
             stf_ref, stb_ref):
        @pl.when(pl.program_id(0) == 0)
        def _():
            stf_ref[...] = stf0[...]
            stb_ref[...] = stb0[...]

        consts = (hk_ref[...], hv_ref[...], bd_ref[...])
        scan(False, qf, kf, vf, laf, of_ref, stsf_ref, stf_ref, consts)
        scan(True, qb, kb, vb, lab, ob_ref, stsb_ref, stb_ref, consts)

    full = lambda a: pl.BlockSpec(a.shape, lambda i: (0,) * a.ndim)
    outs = lambda order: [pl.BlockSpec((tb, GVW), lambda i: (order(i), 0)),
                          pl.BlockSpec((1, GLA_GROUPS, GVG, GKG), lambda i: (order(i), 0, 0, 0))]
    return pl.pallas_call(
        body, name="gla_fwd", grid=(nblk,),
        in_specs=_gla_specs(s, tb, up) + _gla_specs(s, tb, down) + [full(st_f0), full(st_b0)]
        + [full(m) for m in masks],
        out_specs=outs(up) + outs(down),
        out_shape=[jax.ShapeDtypeStruct((s, GVW), F32), jax.ShapeDtypeStruct((nblk, GLA_GROUPS, GVG, GKG), F32)] * 2,
        scratch_shapes=[pltpu.VMEM((GLA_GROUPS, GVG, GKG), F32)] * 2,
        compiler_params=_cp("arbitrary"),
    )(p, p, p, la_f, p, p, p, la_b, st_f0, st_b0, *masks)


def _gla_bwd(p, la_f, la_b, sts_f, sts_b, do, after=None):
    s = p.shape[0]
    tb = GLA_BLOCK_CHUNKS * GLA_CHUNK
    nblk = s // tb
    up, down = (lambda i: i), (lambda i: nblk - 1 - i)
    masks = _gla_masks(GLA_GROUP_HEADS)
    follow = () if after is None else (after,)

    def back(rev, q_ref, k_ref, v_ref, la_ref, sts_ref, do_ref, dq_ref, dk_ref, dv_ref, dla_ref, dst0_ref, dst_ref,
             consts):
        def block(q, k, v, la, st):
            outs = [None] * GLA_BLOCK_CHUNKS
            chunks = range(GLA_BLOCK_CHUNKS)
            for ci in (reversed(chunks) if rev else chunks):
                outs[ci], st = _f_gla_chunk(q[ci], k[ci], v[ci], la[ci], st, *consts, rev)
            return tuple(outs), st

        for g in range(GLA_GROUPS):
            gk, gv = slice(g * GKG, (g + 1) * GKG), slice(g * GVG, (g + 1) * GVG)
            split = lambda r, cols: tuple(r[ci * GLA_CHUNK:(ci + 1) * GLA_CHUNK, cols].astype(F32)
                                          for ci in range(GLA_BLOCK_CHUNKS))
            _, vjp = jax.vjp(block, split(q_ref, gk), split(k_ref, gk), split(v_ref, gv), split(la_ref, gk),
                             sts_ref[0, g])
            dq, dk, dv, dla, dst = vjp((split(do_ref, gv), dst_ref[g]))
            for ci in range(GLA_BLOCK_CHUNKS):
                rows = slice(ci * GLA_CHUNK, (ci + 1) * GLA_CHUNK)
                dq_ref[rows, gk], dk_ref[rows, gk], dv_ref[rows, gv], dla_ref[rows, gk] = dq[ci], dk[ci], dv[ci], dla[ci]
            dst_ref[g] = dst
            dst0_ref[g] = dst

    def body(*refs):
        ins, (hk_ref, hv_ref, bd_ref) = refs[:12], refs[12:15]
        outs = refs[15 + len(follow):]

        @pl.when(pl.program_id(0) == 0)
        def _():
            outs[10][...] = jnp.zeros_like(outs[10])
            outs[11][...] = jnp.zeros_like(outs[11])

        consts = (hk_ref[...], hv_ref[...], bd_ref[...])
        back(False, *ins[:6], *outs[:5], outs[10], consts)
        back(True, *ins[6:], *outs[5:10], outs[11], consts)

    full = lambda a: pl.BlockSpec(a.shape, lambda i: (0,) * a.ndim)

    def ins(order):
        return _gla_specs(s, tb, order) + [pl.BlockSpec((1, GLA_GROUPS, GVG, GKG), lambda i: (order(i), 0, 0, 0)),
                                           pl.BlockSpec((tb, GVW), lambda i: (order(i), 0))]

    def outs(order):
        blk = lambda w: pl.BlockSpec((tb, w), lambda i: (order(i), 0))
        return [blk(GKW), blk(GKW), blk(GVW), blk(GKW), pl.BlockSpec((GLA_GROUPS, GVG, GKG), lambda i: (0, 0, 0))]

    shapes = [jax.ShapeDtypeStruct((s, GKW), F32), jax.ShapeDtypeStruct((s, GKW), F32),
              jax.ShapeDtypeStruct((s, GVW), F32), jax.ShapeDtypeStruct((s, GKW), F32),
              jax.ShapeDtypeStruct((GLA_GROUPS, GVG, GKG), F32)]
    both = pl.pallas_call(
        body, name="gla_bwd", grid=(nblk,),
        in_specs=ins(down) + ins(up) + [full(m) for m in masks] + [pl.BlockSpec(memory_space=pl.ANY)] * len(follow),
        out_specs=outs(down) + outs(up), out_shape=shapes * 2,
        scratch_shapes=[pltpu.VMEM((GLA_GROUPS, GVG, GKG), F32)] * 2,
        compiler_params=_cp("arbitrary"),
    )(p, p, p, la_f, sts_f, do, p, p, p, la_b, sts_b, do, *masks, *follow)
    return both[:5], both[5:]


def _f_ctx_state(k, v, la_f, la_b, bd_t):
    c = k.shape[0]
    after = _nn_hi(_tri(c, True, strict=True).astype(F32), la_f)
    before = _nn_hi(_tri(c, False, strict=True).astype(F32), la_b)
    return bd_t * _tn(v, k * jnp.exp(after)), bd_t * _tn(v, k * jnp.exp(before))


def _ctx_state(pc, la_f, la_b):
    c = pc.shape[0]
    bd_t = _gla_masks()[2]

    def body(k_ref, v_ref, lf_ref, lb_ref, bd_ref, sf_ref, sb_ref):
        sf_ref[...], sb_ref[...] = _f_ctx_state(k_ref[...], v_ref[...], lf_ref[...], lb_ref[...], bd_ref[...])

    full = lambda a: pl.BlockSpec(a.shape, lambda i: (0, 0))
    return pl.pallas_call(
        body, name="ctx_state_fwd", grid=(1,),
        in_specs=[pl.BlockSpec((c, GKW), lambda i: (0, C_GK // GKW)), pl.BlockSpec((c, GVW), lambda i: (0, C_GV // GVW)),
                  full(la_f), full(la_b), full(bd_t)],
        out_specs=[pl.BlockSpec((GVW, GKW), lambda i: (0, 0))] * 2,
        out_shape=[jax.ShapeDtypeStruct((GVW, GKW), F32)] * 2,
        compiler_params=_cp("arbitrary"),
    )(pc, pc, la_f, la_b, bd_t)


def _ctx_state_bwd(pc, la_f, la_b, dsf, dsb):
    c = pc.shape[0]
    bd_t = _gla_masks()[2]

    def body(k_ref, v_ref, lf_ref, lb_ref, bd_ref, dsf_ref, dsb_ref, dk_ref, dv_ref, dlf_ref, dlb_ref):
        _, vjp = jax.vjp(lambda k, v, lf, lb: _f_ctx_state(k, v, lf, lb, bd_ref[...]),
                         k_ref[...], v_ref[...], lf_ref[...], lb_ref[...])
        dk, dv, dlf, dlb = vjp((dsf_ref[...], dsb_ref[...]))
        dk_ref[...], dv_ref[...] = dk.astype(BF16), dv.astype(BF16)
        dlf_ref[...], dlb_ref[...] = dlf, dlb

    full = lambda a: pl.BlockSpec(a.shape, lambda i: (0, 0))
    return pl.pallas_call(
        body, name="ctx_state_bwd", grid=(1,),
        in_specs=[pl.BlockSpec((c, GKW), lambda i: (0, C_GK // GKW)), pl.BlockSpec((c, GVW), lambda i: (0, C_GV // GVW)),
                  full(la_f), full(la_b), full(bd_t), full(dsf), full(dsb)],
        out_specs=[pl.BlockSpec((c, GKW), lambda i: (0, 0)), pl.BlockSpec((c, GVW), lambda i: (0, 0)),
                   pl.BlockSpec((c, GKW), lambda i: (0, 0)), pl.BlockSpec((c, GKW), lambda i: (0, 0))],
        out_shape=[jax.ShapeDtypeStruct((c, GKW), BF16), jax.ShapeDtypeStruct((c, GVW), BF16),
                   jax.ShapeDtypeStruct((c, GKW), F32), jax.ShapeDtypeStruct((c, GKW), F32)],
        compiler_params=_cp("arbitrary"),
    )(pc, pc, la_f, la_b, bd_t, dsf, dsb)


_SRC_COLS = ((0, QW), (QW + 2 * KVW + 2 * GKW, GVW), (QW + 2 * KVW + 2 * GKW + GVW, GVW), (QW, KVW), (QW + KVW, KVW),
             (QW + 2 * KVW, GKW), (QW + 2 * KVW + GKW, GKW), (IN_COLS - 2 * GATE_RANK, 2 * GATE_RANK))
_DST_COLS = (C_Q, C_GV, C_GG, C_K, C_V, C_GQ, C_GK, C_Z)


def _pack_w_in(w_in):
    parts = [w_in[:, s:s + n] for s, n in _SRC_COLS]
    parts.append(jnp.zeros((w_in.shape[0], IN_PAD - C_Z - 2 * GATE_RANK), w_in.dtype))
    return jnp.concatenate(parts, axis=1)


def _unpack_w_in_grad(g):
    by_src = sorted(zip(_SRC_COLS, _DST_COLS))
    return jnp.concatenate([g[:, d:d + n] for (_, n), d in by_src], axis=1)


def _prep_gate_weights(w_gate_fwd, w_gate_bwd):
    pad_rows = lambda w, at: jnp.zeros((LANES, GKW), F32).at[at:at + GATE_RANK].set(w)
    return {"wg_f": pad_rows(w_gate_fwd, 0), "wg_b": pad_rows(w_gate_bwd, GATE_RANK)}


def _local_step(x, ctx, target, ada, ada_c, w, late_weights, reduce_behind=None, reduce_w_in=None):
    s, d = x.shape
    sh1, sc1, gt1, sh2, sc2, gt2 = [ada[:, i * d:(i + 1) * d] for i in range(6)]
    sh1c, sc1c = ada_c[:, :d], ada_c[:, d:2 * d]
    cos, sin = _rope_tables(s)
    gt = jnp.tile(w["g_gla_norm"], (1, GLA_HEADS))

    h = _norm_mod("pre_mix", x, w["g_pre_mix"], sh1, sc1)
    hc = _norm_mod("pre_mix_ctx", ctx, w["g_pre_mix"], sh1c, sc1c)
    w_in, token = w["w_in"](h, cos, sin)
    p = _mm("proj_in", h, w_in, "nn", after=token)
    pc = _mm("proj_in_ctx", hc, w_in, "nn")
    q_rot, k_rot, v_b = _rope_fwd("rope", p, cos, sin)
    pad = ((BLOCK, BLOCK), (0, 0))
    kp, vp = jnp.pad(k_rot, pad), jnp.pad(v_b, pad)
    kc, vc = pc[:, C_K:C_K + KVW].astype(BF16), pc[:, C_V:C_V + KVW].astype(BF16)
    attn = _attn_fwd(q_rot, kp, vp, kc, vc, w["attn_sink"])
    gate_w = (w["wg_f"], w["wg_b"], w["b_gate_fwd"], w["b_gate_bwd"])
    la_f, la_b = _gate_fwd("gate", p, *gate_w)
    la_fc, la_bc = _gate_fwd("gate_ctx", pc, *gate_w)
    st_f0, st_b0 = _ctx_state(pc, la_fc, la_bc)
    o_f, sts_f, o_b, sts_b = _gla_fwd(p, la_f, la_b, _group_states(st_f0), _group_states(st_b0))
    mix = _gla_out("gla_out", attn, o_f, o_b, p, gt)
    w_out, w_ffn_in_t, w_ffn_out = late_weights(attn)
    y = _mm("proj_out", mix, w_out, "nn", BF16)
    x1, h2 = _post_res_norm_mod("post_mix_pre_ffn", x, y, w["g_post_mix"], gt1, w["g_pre_ffn"], sh2, sc2)
    u, a = _ffn_in_swiglu("ffn_in", h2, w_ffn_in_t)
    f = _mm("ffn_out", a, w_ffn_out, "nn", BF16)
    g = {}
    dx2, df, loss, g["g_post_ffn"], dgt2 = _post_res_loss("post_ffn_loss", x1, f, w["g_post_ffn"], gt2, target)

    g["w_ffn_out"] = _mm("ffn_out_dw", a, df, "tn")
    du = _ffn_out_dx_swiglu_bwd("ffn_out_dx", df, w_ffn_out, u)
    dh2 = _mm("ffn_in_dx", du, w_ffn_in_t, "nn", BF16)
    g["w_ffn_in_t"] = _mm("ffn_in_dw", du, h2, "tn")
    dx1, dy, g["g_pre_ffn"], dsh2, dsc2, g["g_post_mix"], dgt1 = _norm_mod_post_res_bwd(
        "pre_ffn_post_mix_bwd", dh2, dx2, x1, y, w["g_pre_ffn"], sh2, sc2, w["g_post_mix"], gt1)
    dmix = _mm("proj_out_dx", dy, w_out, "nt", BF16)
    g["w_out"] = _mm("proj_out_dw", mix, dy, "tn")
    rb, sink, token = reduce_behind, w["attn_sink"], None
    if rb is not None:
        gt = _behind(gt, rb.start(g["w_ffn_in_t"], g["w_ffn_out"], g["w_out"]))
    d_o, dgg, dgt = _gla_out_bwd("gla_out_bwd", dmix, o_f, o_b, p, gt)
    g["g_gla_norm"] = jnp.sum(dgt.reshape(GLA_HEADS, GLA_DV), axis=0, keepdims=True)
    if rb is not None:
        token = rb.pair(dgg)
    gla_f, gla_b = _gla_bwd(p, la_f, la_b, sts_f, sts_b, d_o, token)
    (dla_f, dst_f0), (dla_b, dst_b0) = gla_f[3:], gla_b[3:]
    dst_f0, dst_b0 = _ungroup_states(dst_f0), _ungroup_states(dst_b0)
    if rb is not None:
        sink = _behind(sink, rb.total(dla_b))
    dgkc, dgvc, dla_fc, dla_bc = _ctx_state_bwd(pc, la_fc, la_bc, dst_f0, dst_b0)
    dz, dwf, dwb, dbf, dbb = _gate_bwd("gate_bwd", p, dla_f, dla_b, *gate_w)
    dzc, dwfc, dwbc, dbfc, dbbc = _gate_bwd("gate_ctx_bwd", pc, dla_fc, dla_bc, *gate_w)
    g["w_gate_fwd"] = (dwf + dwfc)[:GATE_RANK]
    g["w_gate_bwd"] = (dwb + dwbc)[GATE_RANK:2 * GATE_RANK]
    g["b_gate_fwd"], g["b_gate_bwd"] = dbf + dbfc, dbb + dbbc
    dq_rot, dkp, dvp, dkc, dvc, g["attn_sink"] = _attn_bwd(dmix, q_rot, kp, vp, kc, vc, sink)
    if rb is not None:
        g["behind"] = rb.result(dq_rot)
    dp = _proj_grad("proj_grad", dq_rot, dkp[BLOCK:BLOCK + s], dvp[BLOCK:BLOCK + s], cos, sin, gla_f[:3], gla_b[:3],
                    dgg, dz)
    c_rows = ctx.shape[0]
    zeros = lambda n: jnp.zeros((c_rows, n), BF16)
    dpc = jnp.concatenate([zeros(QW), dgvc, zeros(GVW), dkc.astype(BF16), dvc.astype(BF16), zeros(GKW), dgkc, dzc],
                          axis=1)
    g["w_in"] = _mm("proj_in_dw", h, dp, "tn", init=_mm("proj_in_ctx_dw", hc, dpc, "tn"))
    token = None if reduce_w_in is None else reduce_w_in.start(g["w_in"])
    dh = _mm("proj_in_dx", dp, w_in, "nt", BF16, after=token)
    dhc = _mm("proj_in_ctx_dx", dpc, w_in, "nt")
    if reduce_w_in is not None:
        sh1 = _behind(sh1, reduce_w_in.pair(dh))
    dx, dg_a, dsh1, dsc1 = _norm_mod_bwd("pre_mix_bwd", dh, dx1, x, w["g_pre_mix"], sh1, sc1)
    if reduce_w_in is not None:
        dsh1 = _behind(dsh1, reduce_w_in.total(dx))
    _, dg_b, dsh1c, dsc1c = _norm_mod_bwd("pre_mix_ctx_bwd", dhc, jnp.zeros_like(dhc), ctx, w["g_pre_mix"], sh1c,
                                          sc1c)
    g["g_pre_mix"] = dg_a + dg_b
    d_ada = jnp.concatenate([dsh1, dsc1, dgt1, dsh2, dsc2, dgt2], axis=1)
    d_ada_c = jnp.concatenate([dsh1c, dsc1c, jnp.zeros((1, 4 * d), F32)], axis=1)
    return loss, dx, g, d_ada, d_ada_c


HBM = pl.BlockSpec(memory_space=pltpu.HBM)
N_DEV, N_CHIP = 8, 4


def _place():
    x, y, c = lax.axis_index("x"), lax.axis_index("y"), lax.axis_index("c")
    return x, y, c, [(1 - x, y), (x, 1 - y), (1 - x, 1 - y)]


def _row_tile(n, mult, cap):
    return max(t for t in range(mult, min(n, cap) + 1, mult) if n % t == 0)


def _ag_small(name, v, after=None):
    follow = () if after is None else (after,)

    def body(v_ref, *rest):
        out_ref, send_sems, recv_sems = rest[len(follow):]
        x, y, c, _ = _place()
        out_ref[4 * x + 2 * y + c] = v_ref[...]

        def peer(r):
            return ((1 - x) if r & 4 else x, (1 - y) if r & 2 else y, (1 - c) if r & 1 else c)

        def copy(r, block):
            px, py, pc = block
            return pltpu.make_async_remote_copy(
                src_ref=v_ref, dst_ref=out_ref.at[4 * px + 2 * py + pc], send_sem=send_sems.at[r - 1],
                recv_sem=recv_sems.at[r - 1], device_id=peer(r), device_id_type=MESH)

        sends = [copy(r, (x, y, c)) for r in range(1, N_DEV)]
        for cp in sends:
            cp.start()
        for r in range(1, N_DEV):
            copy(r, peer(r)).wait_recv()
        for cp in sends:
            cp.wait_send()

    return pl.pallas_call(
        body, name=name, out_shape=jax.ShapeDtypeStruct((N_DEV,) + v.shape, v.dtype),
        in_specs=[pl.BlockSpec(memory_space=pltpu.VMEM)] + [pl.BlockSpec(memory_space=pl.ANY)] * len(follow),
        out_specs=pl.BlockSpec(memory_space=pltpu.VMEM),
        scratch_shapes=[pltpu.SemaphoreType.DMA((N_DEV - 1,)), pltpu.SemaphoreType.DMA((N_DEV - 1,))],
    )(v, *follow)


def _halves(c, rows, mult):
    hr = rows // 2
    return pl.ds(pl.multiple_of(c * hr, mult), hr), pl.ds(pl.multiple_of((1 - c) * hr, mult), hr)


def _add_half(name, g, a, c_idx):
    n_sh, hr, n = a.shape
    tr = _row_tile(hr, 16, 1024)
    nb = hr // tr

    def body(c_ref, g_ref, a_ref, o_ref):
        o_ref[...] = (g_ref[...] + a_ref[...]).astype(o_ref.dtype)

    return pl.pallas_call(
        body, name=name, out_shape=jax.ShapeDtypeStruct(a.shape, BF16),
        grid_spec=pltpu.PrefetchScalarGridSpec(
            num_scalar_prefetch=1, grid=(n_sh, nb),
            in_specs=[pl.BlockSpec((1, tr, n), lambda s, i, c_ref: (s, c_ref[0] * nb + i, 0)),
                      pl.BlockSpec((1, tr, n), lambda s, i, c_ref: (s, i, 0))],
            out_specs=pl.BlockSpec((1, tr, n), lambda s, i, c_ref: (s, i, 0))),
        compiler_params=_cp("parallel", "parallel"),
    )(c_idx, g, a)


def _sum_chips(name, b):
    n_sh, hr, n = b.shape
    tr = _row_tile(hr, 16, 1024)

    def body(b0, b1, b2, b3, o_ref):
        o_ref[...] = ((b0[0].astype(F32) + b1[0].astype(F32)) + b2[0].astype(F32)) + b3[0].astype(F32)

    return pl.pallas_call(
        body, name=name, grid=(hr // tr,), out_shape=jax.ShapeDtypeStruct((hr, n), F32),
        in_specs=[pl.BlockSpec((1, tr, n), functools.partial(lambda i, k: (k, i, 0), k=k)) for k in range(n_sh)],
        out_specs=pl.BlockSpec((tr, n), lambda i: (i, 0)), compiler_params=_cp("parallel"),
    )(b, b, b, b)


SEM = pl.BlockSpec(memory_space=pltpu.SEMAPHORE)
ANY = pl.BlockSpec(memory_space=pl.ANY)
DATAFLOW = pltpu.SideEffectType.DATAFLOW_SIDE_EFFECTING


def _remote(src, dst, send_sems, recv_sems, k, to):
    return pltpu.make_async_remote_copy(src_ref=src, dst_ref=dst, send_sem=send_sems.at[k], recv_sem=recv_sems.at[k],
                                        device_id=to, device_id_type=MESH)


def _split_copy(name, src, land_shape, land_dtype, n, plan, after=None):
    after = jnp.zeros((8, LANES), F32) if after is None else after

    def start_body(src_ref, land_ref, after_ref, send_sems, recv_sems, src_thru, land_thru, token):
        for cp in plan(src_ref, land_ref, send_sems, recv_sems)[0]:
            cp.start()
        token[...] = jnp.zeros_like(token)

    sems = pltpu.SemaphoreType.DMA((n,))
    send_sems, recv_sems, src_thru, land_thru, token = pl.pallas_call(
        start_body, name=name + "_start",
        out_shape=(sems, sems, pltpu.HBM(src.shape, src.dtype), pltpu.HBM(land_shape, land_dtype),
                   jax.ShapeDtypeStruct((8, LANES), F32)),
        in_specs=(HBM, HBM, ANY), out_specs=(SEM, SEM, HBM, HBM, pl.BlockSpec(memory_space=pltpu.VMEM)),
        input_output_aliases={0: 2, 1: 3}, compiler_params=pltpu.CompilerParams(has_side_effects=DATAFLOW),
    )(pltpu.with_memory_space_constraint(src, pltpu.HBM),
      pltpu.with_memory_space_constraint(lax.empty(land_shape, land_dtype), pltpu.HBM), after)

    def wait(*after):
        def wait_body(src_ref, land_ref, send_sems, recv_sems, *rest):
            sent, received = plan(src_ref, land_ref, send_sems, recv_sems)
            for cp in sent:
                cp.wait_send()
            for cp in received:
                cp.wait_recv()

        return pl.pallas_call(
            wait_body, name=name + "_wait",
            out_shape=(pltpu.HBM(src.shape, src.dtype), pltpu.HBM(land_shape, land_dtype)),
            in_specs=(HBM, HBM, SEM, SEM) + (ANY,) * len(after), out_specs=(HBM, HBM),
            input_output_aliases={0: 0, 1: 1}, compiler_params=pltpu.CompilerParams(has_side_effects=DATAFLOW),
        )(src_thru, land_thru, send_sems, recv_sems, *after)

    return token, wait


def _split_gather(name, shards, after):
    k, n, plan = len(shards), 3 * len(shards), _plan_gather

    def start_body(*refs):
        for cp in plan(refs[:k], refs[k:2 * k], refs[2 * k + 1], refs[2 * k + 2])[0]:
            cp.start()
        refs[-1][...] = jnp.zeros_like(refs[-1])

    sems = pltpu.SemaphoreType.DMA((n,))
    bufs = [pltpu.HBM(s.shape, s.dtype) for s in shards] + [pltpu.HBM((N_CHIP,) + s.shape, s.dtype) for s in shards]
    hbm = lambda t: pltpu.with_memory_space_constraint(t, pltpu.HBM)
    outs = pl.pallas_call(
        start_body, name=name + "_start", out_shape=(sems, sems, *bufs, jax.ShapeDtypeStruct((8, LANES), F32)),
        in_specs=(HBM,) * (2 * k) + (ANY,),
        out_specs=(SEM, SEM) + (HBM,) * (2 * k) + (pl.BlockSpec(memory_space=pltpu.VMEM),),
        input_output_aliases={i: 2 + i for i in range(2 * k)},
        compiler_params=pltpu.CompilerParams(has_side_effects=DATAFLOW),
    )(*[hbm(s) for s in shards], *[hbm(lax.empty((N_CHIP,) + s.shape, s.dtype)) for s in shards], after)
    send_sems, recv_sems, thru, token = outs[0], outs[1], outs[2:2 + 2 * k], outs[-1]

    def wait(*after):
        def wait_body(*refs):
            sent, received = plan(refs[:k], refs[k:2 * k], refs[2 * k], refs[2 * k + 1])
            for cp in sent:
                cp.wait_send()
            for cp in received:
                cp.wait_recv()

        res = pl.pallas_call(
            wait_body, name=name + "_wait", out_shape=tuple(bufs),
            in_specs=(HBM,) * (2 * k) + (SEM, SEM) + (ANY,) * len(after), out_specs=(HBM,) * (2 * k),
            input_output_aliases={i: i for i in range(2 * k)},
            compiler_params=pltpu.CompilerParams(has_side_effects=DATAFLOW),
        )(*thru, send_sems, recv_sems, *after)
        return res[:k], res[k:]

    return token, wait


def _behind(x, token):
    return x + token[0, 0]


def _plan_gather(src_refs, land_refs, send_sems, recv_sems):
    x, y, c, chips = _place()
    pairs = list(enumerate(zip(src_refs, land_refs)))
    sent = [_remote(s, l.at[2 * x + y], send_sems, recv_sems, 3 * i + j, (px, py, c))
            for i, (s, l) in pairs for j, (px, py) in enumerate(chips)]
    received = [_remote(s, l.at[2 * px + py], send_sems, recv_sems, 3 * i + j, (px, py, c))
                for i, (s, l) in pairs for j, (px, py) in enumerate(chips)]
    return sent, received


def _plan_swap(src_ref, land_ref, send_sems, recv_sems):
    x, y, c, _ = _place()
    _, other_half = _halves(c, src_ref.shape[1], 8)
    cp = _remote(src_ref.at[pl.ds(0, src_ref.shape[0]), other_half], land_ref, send_sems, recv_sems, 0, (x, y, 1 - c))
    return [cp], [cp]


def _plan_scatter(src_ref, land_ref, send_sems, recv_sems):
    x, y, c, chips = _place()
    sent = [_remote(src_ref.at[2 * px + py], land_ref.at[2 * x + y], send_sems, recv_sems, j, (px, py, c))
            for j, (px, py) in enumerate(chips)]
    received = [_remote(src_ref.at[2 * px + py], land_ref.at[2 * px + py], send_sems, recv_sems, j, (px, py, c))
                for j, (px, py) in enumerate(chips)]
    return sent, received


def _plan_share(src_ref, land_ref, send_sems, recv_sems):
    x, y, c, _ = _place()
    mine_half, other_half = _halves(c, land_ref.shape[0], 8)
    return ([_remote(src_ref, land_ref.at[mine_half], send_sems, recv_sems, 0, (x, y, 1 - c))],
            [_remote(src_ref, land_ref.at[other_half], send_sems, recv_sems, 0, (x, y, 1 - c))])


def _pack_shard_rows(name, parts):
    rows = [t.shape[0] // N_CHIP for t in parts]
    n, total = parts[0].shape[1], sum(t.shape[0] // N_CHIP for t in parts)
    slab, at = None, 0
    for i, (t, r) in enumerate(zip(parts, rows)):
        tr = max(c for c in range(8, min(r, 512) + 1, 8) if r % c == 0 and at % c == 0)
        nb, ob = r // tr, at // tr

        def body(t_ref, *rest):
            rest[-1][0] = t_ref[...]

        slab = pl.pallas_call(
            body, name=f"{name}_{i}", grid=(N_CHIP, nb), out_shape=jax.ShapeDtypeStruct((N_CHIP, total, n), t.dtype),
            in_specs=[pl.BlockSpec((tr, n), functools.partial(lambda k, j, nb: (k * nb + j, 0), nb=nb))]
            + ([] if slab is None else [pl.BlockSpec(memory_space=pl.ANY)]),
            out_specs=pl.BlockSpec((1, tr, n), functools.partial(lambda k, j, ob: (k, ob + j, 0), ob=ob)),
            input_output_aliases={} if slab is None else {1: 0}, compiler_params=_cp("parallel", "parallel"),
        )(*((t,) if slab is None else (t, slab)))
        at += r
    return slab


class _GatherBehind:
    def __init__(self, name, shards, chip, after):
        self.chip = chip
        self.token, self.wait = _split_gather(name, shards, after)

    def result(self, *after):
        shards, lands = self.wait(*after)
        return [lax.dynamic_update_slice(land, shard[None], (self.chip, 0, 0)) for shard, land in zip(shards, lands)]


class _ReduceBehind:
    def __init__(self, name, chip, c, c_idx):
        self.name, self.chip, self.c, self.c_idx = name, chip, c, c_idx

    def start(self, *grads):
        return self.start_slab(_pack_shard_rows(self.name + "_pack", grads))

    def start_slab(self, g):
        n_sh, rows, n = g.shape
        token, self.wait = _split_copy(self.name + "_swap", g, (n_sh, rows // 2, n), g.dtype, 1, _plan_swap)
        return token

    def pair(self, after):
        g, a = self.wait(after)
        h = _add_half(self.name + "_pair", g, a, self.c_idx)
        token, self.wait = _split_copy(self.name + "_scatter", h, h.shape, h.dtype, 3, _plan_scatter)
        return token

    def total(self, after):
        h, b = self.wait(after)
        b = lax.dynamic_update_slice(b, lax.dynamic_slice_in_dim(h, self.chip, 1, axis=0), (self.chip, 0, 0))
        f = _sum_chips(self.name + "_sum", b)
        token, self.wait = _split_copy(self.name + "_share", f, (2 * f.shape[0], f.shape[1]), f.dtype, 1,
                                       _plan_share)
        return token

    def result(self, after):
        f, out = self.wait(after)
        return lax.dynamic_update_slice(out, f, (self.c * f.shape[0], 0))


class _ReduceColsBehind(_ReduceBehind):
    def start(self, g_padded):
        g = _unpack_w_in_grad(g_padded)
        n = g.shape[1] // N_CHIP
        return self.start_slab(jnp.stack([g[:, k * n:(k + 1) * n] for k in range(N_CHIP)]))


def _f_adamw(w, g, m, v):
    m = ADAM_B1 * m + (1.0 - ADAM_B1) * g
    v = ADAM_B2 * v + (1.0 - ADAM_B2) * (g * g)
    m_hat = m / (1.0 - ADAM_B1 ** ADAM_STEP)
    v_hat = v / (1.0 - ADAM_B2 ** ADAM_STEP)
    return -ADAM_LR * (m_hat / (jnp.sqrt(v_hat) + ADAM_EPS) + ADAM_WD * w), m, v


def _adamw(name, w, g, m, v):
    rows, n = w.shape
    return _rowwise(name, lambda w, g, m, v: (_f_adamw(w, g, m, v), ()), rows, [(t, n, 0) for t in (w, g, m, v)], [],
                    [(n, F32)] * 3, [], tm=_row_tile(rows, 8, 256))


def _pack_rows(parts):
    rows = []
    for t in parts:
        t = t.reshape(-1)
        rows.append(jnp.pad(t, (0, -t.shape[0] % LANES)).reshape(-1, LANES))
    out = jnp.concatenate(rows, axis=0)
    return jnp.pad(out, ((0, -out.shape[0] % 8), (0, 0)))


def _unpack_rows(packed, shapes):
    out, r = [], 0
    for shp in shapes:
        n = int(np.prod(shp))
        nr = -(-n // LANES)
        out.append(packed[r:r + nr].reshape(-1)[:n].reshape(shp))
        r += nr
    return out


def _sum_blocks(name, g):
    def body(g_ref, o_ref):
        acc = g_ref[0]
        for k in range(1, g.shape[0]):
            acc = acc + g_ref[k]
        o_ref[...] = acc

    return pl.pallas_call(body, name=name, out_shape=jax.ShapeDtypeStruct(g.shape[1:], F32))(g)


def _silu(t):
    return t * _sigmoid(t)


def _ada_fwd(cc, w_ada):
    n = w_ada.shape[1]
    tn = _row_tile(n, LANES, 512)

    def body(cc_ref, w_ref, o_ref):
        o_ref[...] = _nn(_silu(cc_ref[...]), w_ref[...])

    return pl.pallas_call(
        body, name="ada_fwd", grid=(n // tn,), out_shape=jax.ShapeDtypeStruct((cc.shape[0], n), F32),
        in_specs=[pl.BlockSpec(cc.shape, lambda j: (0, 0)), pl.BlockSpec((w_ada.shape[0], tn), lambda j: (0, j))],
        out_specs=pl.BlockSpec((cc.shape[0], tn), lambda j: (0, j)), compiler_params=_cp("parallel"),
    )(cc, w_ada)


def _ada_bwd(cc, dm, w_ada):
    d, n = w_ada.shape
    tn = _row_tile(n, LANES, 512)

    def body(cc_ref, dm_ref, w_ref, gw_ref, ds_ref):
        @pl.when(pl.program_id(0) == 0)
        def _():
            ds_ref[...] = jnp.zeros_like(ds_ref)

        gw_ref[...] = _raw_dot("tn", _silu(cc_ref[...]), dm_ref[...], True)
        ds_ref[...] += _raw_dot("nt", dm_ref[...], w_ref[...], False)

    return pl.pallas_call(
        body, name="ada_bwd", grid=(n // tn,),
        out_shape=[jax.ShapeDtypeStruct((d, n), F32), jax.ShapeDtypeStruct(cc.shape, F32)],
        in_specs=[pl.BlockSpec(cc.shape, lambda j: (0, 0)), pl.BlockSpec((cc.shape[0], tn), lambda j: (0, j)),
                  pl.BlockSpec((d, tn), lambda j: (0, j))],
        out_specs=[pl.BlockSpec((d, tn), lambda j: (0, j)), pl.BlockSpec(cc.shape, lambda j: (0, 0))],
        compiler_params=_cp("arbitrary"),
    )(cc, dm, w_ada)


def _c_ctx_grad(parts, c_ctx):
    def body(p_ref, c_ref, o_ref):
        ds = ((p_ref[0] + p_ref[1]) + p_ref[2]) + p_ref[3]
        _, vjp = jax.vjp(_silu, c_ref[...])
        o_ref[...] = vjp(ds)[0]

    return pl.pallas_call(body, name="c_ctx_grad", out_shape=jax.ShapeDtypeStruct(c_ctx.shape, F32))(parts, c_ctx)


def kernel(x, c, ctx, c_ctx, w_ada, b_ada, g_pre_mix, g_post_mix, g_pre_ffn, g_post_ffn, w_in, attn_sink, w_gate_fwd, b_gate_fwd, w_gate_bwd, b_gate_bwd, g_gla_norm, w_out, w_ffn_in, w_ffn_out, loss_target, m_c_ctx, m_w_ada, m_b_ada, m_g_pre_mix, m_g_post_mix, m_g_pre_ffn, m_g_post_ffn, m_w_in, m_attn_sink, m_w_gate_fwd, m_b_gate_fwd, m_w_gate_bwd, m_b_gate_bwd, m_g_gla_norm, m_w_out, m_w_ffn_in, m_w_ffn_out, v_c_ctx, v_w_ada, v_b_ada, v_g_pre_mix, v_g_post_mix, v_g_pre_ffn, v_g_post_ffn, v_w_in, v_attn_sink, v_w_gate_fwd, v_b_gate_fwd, v_w_gate_bwd, v_b_gate_bwd, v_g_gla_norm, v_w_out, v_w_ffn_in, v_w_ffn_out):
    xi, yi, ci = lax.axis_index("x"), lax.axis_index("y"), lax.axis_index("c")
    dev, chip = 4 * xi + 2 * yi + ci, 2 * xi + yi
    c_idx = jnp.reshape(ci, (1,)).astype(jnp.int32)
    d = x.shape[-1]
    n_ada, n_in, n_f = w_ada.shape[-1], w_in.shape[-1], w_ffn_in.shape[-1]
    r_out, r_f = w_out.shape[1], w_ffn_out.shape[1]
    n_gate = w_gate_fwd.shape[-1]
    by_chip = lambda t: t[0::2]

    rc = -(-d // LANES)
    g1 = _ag_small("gather_cond", _pack_rows([c[0], w_gate_fwd[0], w_gate_bwd[0]]))
    c_all = g1[:, :rc].reshape(N_DEV, -1)[:, :d]
    gr = GATE_RANK * n_gate // LANES
    gate_full = lambda off: jnp.transpose(by_chip(g1)[:, off:off + gr].reshape(N_CHIP, GATE_RANK, n_gate),
                                          (1, 0, 2)).reshape(GATE_RANK, N_CHIP * n_gate)
    wgf, wgb = gate_full(rc), gate_full(rc + gr)
    cc = jnp.concatenate([c_all, c_ctx[None, :], jnp.zeros((7, d), F32)], axis=0)

    g2 = _ag_small("gather_ada", _ada_fwd(cc, w_ada[0]).reshape(-1, LANES))
    ada_all = jnp.transpose(by_chip(g2).reshape(N_CHIP, 16, n_ada), (1, 0, 2)).reshape(16, N_CHIP * n_ada) + b_ada
    first = _GatherBehind("gather_w_in", [w_in[0].astype(BF16)], chip, g2)
    late_shards = [w_out[0].astype(BF16), jnp.transpose(w_ffn_in[0]).astype(BF16), w_ffn_out[0].astype(BF16)]
    late = []

    def first_weights(*after):
        w_in_g, = first.result(*after, *late_shards)
        late.append(_GatherBehind("gather_late", late_shards, chip, w_in_g))
        return _pack_w_in(jnp.concatenate([w_in_g[k] for k in range(N_CHIP)], axis=1)), late[0].token

    def late_weights(after):
        return [t.reshape(-1, d) for t in late[0].result(after)]

    ada_all = _behind(ada_all, first.token)
    ada = lax.dynamic_slice(ada_all, (dev, 0), (1, N_CHIP * n_ada))
    ada_c = ada_all[N_DEV:N_DEV + 1]

    w = _prep_gate_weights(wgf, wgb)
    w.update(w_in=first_weights, g_pre_mix=g_pre_mix, g_post_mix=g_post_mix, g_pre_ffn=g_pre_ffn, g_post_ffn=g_post_ffn,
             attn_sink=attn_sink, b_gate_fwd=b_gate_fwd, b_gate_bwd=b_gate_bwd, g_gla_norm=g_gla_norm)

    reduce_behind = _ReduceBehind("reduce_late", chip, ci, c_idx)
    reduce_w_in = _ReduceColsBehind("reduce_w_in", chip, ci, c_idx)
    loss_lanes, grad_x, g, d_ada, d_ada_c = _local_step(x[0], ctx[0], loss_target[0], ada, ada_c, w, late_weights,
                                                        reduce_behind, reduce_w_in)

    small = ("g_pre_mix", "g_post_mix", "g_pre_ffn", "g_post_ffn", "attn_sink", "b_gate_fwd", "b_gate_bwd",
             "g_gla_norm", "w_gate_fwd", "w_gate_bwd")
    shapes = [(1, 6 * d)] * 2 + [g[n].shape for n in small] + [(1, LANES)]
    g3 = _ag_small("gather_small_grads", _pack_rows([d_ada, d_ada_c] + [g[n] for n in small] + [loss_lanes]))
    tot = dict(zip(("d_ada", "d_ada_c") + small + ("loss",),
                   _unpack_rows(_sum_blocks("sum_small_grads", g3), shapes)))
    r_ada = 6 * d // LANES
    dm = jnp.concatenate([g3[:, :r_ada].reshape(N_DEV, 6 * d), tot["d_ada_c"], jnp.zeros((7, 6 * d), F32)], axis=0)
    grads = {n: tot[n] for n in small[:8]}
    grads["b_ada"] = _sum_blocks("sum_b_ada", dm.reshape(16, r_ada, LANES)).reshape(1, 6 * d)
    grads["w_gate_fwd"] = lax.dynamic_slice(tot["w_gate_fwd"], (0, chip * n_gate), (GATE_RANK, n_gate))[None]
    grads["w_gate_bwd"] = lax.dynamic_slice(tot["w_gate_bwd"], (0, chip * n_gate), (GATE_RANK, n_gate))[None]
    gw_ada, dsc = _ada_bwd(cc, lax.dynamic_slice(dm, (0, chip * n_ada), (16, n_ada)), w_ada[0])
    grads["w_ada"] = gw_ada[None]
    g4 = _ag_small("gather_c_ctx", _pack_rows([dsc[N_DEV]]))
    grads["c_ctx"] = _c_ctx_grad(by_chip(g4), _pack_rows([c_ctx])).reshape(-1)[:d]

    grads["w_in"] = reduce_w_in.result(g4)[None]
    behind = g["behind"]
    grads["w_ffn_in"] = jnp.transpose(behind[:n_f])[None]
    grads["w_ffn_out"], grads["w_out"] = behind[None, n_f:n_f + r_f], behind[None, n_f + r_f:]

    names = ("c_ctx", "w_ada", "b_ada", "g_pre_mix", "g_post_mix", "g_pre_ffn", "g_post_ffn", "w_in", "attn_sink",
             "w_gate_fwd", "b_gate_fwd", "w_gate_bwd", "b_gate_bwd", "g_gla_norm", "w_out", "w_ffn_in", "w_ffn_out")
    weights = dict(zip(names, (c_ctx, w_ada, b_ada, g_pre_mix, g_post_mix, g_pre_ffn, g_post_ffn, w_in, attn_sink,
                               w_gate_fwd, b_gate_fwd, w_gate_bwd, b_gate_bwd, g_gla_norm, w_out, w_ffn_in,
                               w_ffn_out)))
    m_in = dict(zip(names, (m_c_ctx, m_w_ada, m_b_ada, m_g_pre_mix, m_g_post_mix, m_g_pre_ffn, m_g_post_ffn, m_w_in,
                            m_attn_sink, m_w_gate_fwd, m_b_gate_fwd, m_w_gate_bwd, m_b_gate_bwd, m_g_gla_norm,
                            m_w_out, m_w_ffn_in, m_w_ffn_out)))
    v_in = dict(zip(names, (v_c_ctx, v_w_ada, v_b_ada, v_g_pre_mix, v_g_post_mix, v_g_pre_ffn, v_g_post_ffn, v_w_in,
                            v_attn_sink, v_w_gate_fwd, v_b_gate_fwd, v_w_gate_bwd, v_b_gate_bwd, v_g_gla_norm,
                            v_w_out, v_w_ffn_in, v_w_ffn_out)))
    large = ("w_ada", "w_in", "w_out", "w_ffn_in", "w_ffn_out")
    tiny = tuple(n for n in names if n not in large)
    delta, new_m, new_v = {}, {}, {}
    for n in large:
        dl, nm, nv = _adamw("adamw_" + n, weights[n][0], grads[n][0], m_in[n][0], v_in[n][0])
        delta[n], new_m[n], new_v[n] = dl[None], nm[None], nv[None]
    tiny_shapes = [weights[n].shape for n in tiny]
    packed = [_pack_rows([t[n] for n in tiny]) for t in (weights, grads, m_in, v_in)]
    for out, res in zip((delta, new_m, new_v), _adamw("adamw_small", *packed)):
        out.update(zip(tiny, _unpack_rows(res, tiny_shapes)))
    for n in tiny:
        grads[n] = grads[n].reshape(weights[n].shape)

    return (tot["loss"][0, 0], grad_x[None], *[grads[n] for n in names], *[delta[n] for n in names], *[new_m[n] for n in names],
            *[new_v[n] for n in names])
```

```python
import functools

import jax
import jax.numpy as jnp
import numpy as np
from jax import lax
from jax.experimental import pallas as pl
from jax.experimental.pallas import tpu as pltpu

F32 = jnp.float32
BF16 = jnp.bfloat16
MESH = pl.DeviceIdType.MESH

HEAD_DIM = 64
ATT_HEADS = 8
ATT_KV_HEADS = 2
ATT_GROUP = ATT_HEADS // ATT_KV_HEADS
WINDOW = 128
BLOCK = 128
GRID_W = 64
ROPE_BASE = 10000.0
GLA_HEADS = 8
GLA_DK = 32
GLA_DV = 64
GLA_CHUNK = 64
GATE_RANK = 16
GATE_TAU = 16.0
NEG_INF = -1e30
QW = ATT_HEADS * HEAD_DIM
KVW = ATT_KV_HEADS * HEAD_DIM
GKW = GLA_HEADS * GLA_DK
GVW = GLA_HEADS * GLA_DV
IN_COLS = QW + 2 * KVW + 2 * GKW + 2 * GVW + 2 * GATE_RANK
LANES = 128
IN_PAD = IN_COLS + LANES - 2 * GATE_RANK
C_Q, C_GV, C_GG = 0, QW, QW + GVW
C_K = C_GG + GVW
C_V = C_K + KVW
C_GQ = C_V + KVW
C_GK = C_GQ + GKW
C_Z = C_GK + GKW
MIX = QW + GVW

ADAM_LR, ADAM_B1, ADAM_B2, ADAM_EPS, ADAM_WD, ADAM_STEP = 0.001, 0.9, 0.999, 1e-08, 0.01, 10

VMEM_LIMIT = 56 * 1024 * 1024


def _cp(*sem):
    return pltpu.CompilerParams(dimension_semantics=sem, vmem_limit_bytes=VMEM_LIMIT)


def _pick(n, cands):
    for t in cands:
        if n % t == 0:
            return t
    return n


_DIMS = {"nn": (((1,), (0,)), ((), ())), "nt": (((1,), (1,)), ((), ())), "tn": (((0,), (0,)), ((), ()))}


def _raw_dot(mode, a, b, hi):
    dot = lambda u, v: lax.dot_general(u, v, _DIMS[mode], preferred_element_type=F32)
    if hi:
        a, b = a.astype(F32), b.astype(F32)
        a_hi, b_hi = a.astype(BF16), b.astype(BF16)
        a_lo, b_lo = (a - a_hi.astype(F32)).astype(BF16), (b - b_hi.astype(F32)).astype(BF16)
        return dot(a_hi, b_hi) + (dot(a_lo, b_hi) + dot(a_hi, b_lo))
    return dot(a.astype(BF16), b.astype(BF16))


def _make_dot(mode, hi):
    @jax.custom_vjp
    def dot(a, b):
        return _raw_dot(mode, a, b, hi)

    def fwd(a, b):
        return _raw_dot(mode, a, b, hi), (a, b)

    def bwd(res, dc):
        a, b = res
        if mode == "nn":
            return _raw_dot("nt", dc, b, hi), _raw_dot("tn", a, dc, hi)
        if mode == "nt":
            return _raw_dot("nn", dc, b, hi), _raw_dot("tn", dc, a, hi)
        return _raw_dot("nt", b, dc, hi), _raw_dot("nn", a, dc, hi)

    dot.defvjp(fwd, bwd)
    return dot


_nn, _nt, _tn = _make_dot("nn", False), _make_dot("nt", False), _make_dot("tn", False)
_nn_hi = _make_dot("nn", True)


MM_VMEM_BUDGET = 44 * 1024 * 1024


def _halvings(n):
    out = [n]
    while out[-1] % (2 * LANES) == 0:
        out.append(out[-1] // 2)
    return out


def _mm_tiles(mode, m, n, k, a_bytes, b_bytes, o_bytes, init_bytes=0):
    tms = [t for t in dict.fromkeys((m, m // 2, m // 4, 2048, 1024, 512, 256, 128))
           if m % t == 0 and t % (LANES if mode == "tn" else 16) == 0 and t <= 4096] or [m]
    if mode == "tn":
        fits = [(k // tk + 0.5 * (m // tm), tm, tk)
                for tk in (4096, 2048, 1024, 512, 256, 128) if k % tk == 0 for tm in tms
                if 2 * (tk * tm * a_bytes + tk * n * b_bytes + tm * n * (o_bytes + init_bytes)) <= MM_VMEM_BUDGET]
        if fits:
            _, tm, tk = min(fits)
            return tm, n, tk
    tks = ([t for t in (512, 256, 128) if k % t == 0] or [k]) if mode == "tn" else _halvings(k)
    for tn in _halvings(n):
        for tk in tks:
            for tm in tms:
                acc = tm * tn * 4 if (k // tk > 1 and o_bytes != 4) else 0
                tiles = tm * tk * a_bytes + tk * tn * b_bytes + tm * tn * (o_bytes + init_bytes)
                if 2 * tiles + acc <= MM_VMEM_BUDGET:
                    return tm, tn, tk
    return tms[-1], _halvings(n)[-1], tks[-1]


def _mm(name, a, b, mode, out_dtype=F32, init=None, after=None):
    follow = () if after is None else (after,)
    if mode == "nn":
        (m, k), n = a.shape, b.shape[1]
    elif mode == "nt":
        (m, k), n = a.shape, b.shape[0]
    else:
        (k, m), n = a.shape, b.shape[1]
    tm, tn, tk = _mm_tiles(mode, m, n, k, a.dtype.itemsize, b.dtype.itemsize, jnp.dtype(out_dtype).itemsize,
                           0 if init is None else 4)
    nk = k // tk
    use_acc = nk > 1 and out_dtype != F32

    inits = () if init is None else (init,)

    def body(a_ref, b_ref, *rest):
        rest = rest[:len(inits)] + rest[len(inits) + len(follow):]
        o_ref, acc = rest[len(inits)], rest[len(inits) + 1:]
        part = _raw_dot(mode, a_ref[...], b_ref[...], False)
        first = lambda: part + rest[0][...] if inits else part
        if nk == 1:
            o_ref[...] = first().astype(o_ref.dtype)
            return
        acc_ref = acc[0] if use_acc else o_ref
        kk = pl.program_id(2)

        @pl.when(kk == 0)
        def _():
            acc_ref[...] = first()

        @pl.when(kk > 0)
        def _():
            acc_ref[...] += part

        if use_acc:
            @pl.when(kk == nk - 1)
            def _():
                o_ref[...] = acc_ref[...].astype(o_ref.dtype)

    if mode == "nn":
        a_spec = pl.BlockSpec((tm, tk), lambda i, j, kk: (i, kk))
        b_spec = pl.BlockSpec((tk, tn), lambda i, j, kk: (kk, j))
    elif mode == "nt":
        a_spec = pl.BlockSpec((tm, tk), lambda i, j, kk: (i, kk))
        b_spec = pl.BlockSpec((tn, tk), lambda i, j, kk: (j, kk))
    else:
        a_spec = pl.BlockSpec((tk, tm), lambda i, j, kk: (kk, i))
        b_spec = pl.BlockSpec((tk, tn), lambda i, j, kk: (kk, j))
    return pl.pallas_call(
        body, name=name, grid=(m // tm, n // tn, nk),
        in_specs=[a_spec, b_spec] + [pl.BlockSpec((tm, tn), lambda i, j, kk: (i, j))] * len(inits)
        + [pl.BlockSpec(memory_space=pl.ANY)] * len(follow),
        out_specs=pl.BlockSpec((tm, tn), lambda i, j, kk: (i, j)),
        out_shape=jax.ShapeDtypeStruct((m, n), out_dtype),
        scratch_shapes=[pltpu.VMEM((tm, tn), F32)] if use_acc else [],
        compiler_params=_cp("parallel", "parallel", "arbitrary"),
    )(a, b, *inits, *follow)


def _slab_layout(rows):
    offsets, at = [], 0
    for r in rows:
        at = -(-at // r) * r
        offsets.append(at)
        at += r
    return offsets, -(-at // 32) * 32


def _slab_zero_gaps(name, shape, rows, offsets):
    gaps = [(o + r, nxt) for o, r, nxt in zip(offsets, rows, offsets[1:] + [shape[1]]) if nxt > o + r]
    slab = None
    for i, (lo, hi) in enumerate(gaps):
        step = int(np.gcd(lo, hi - lo))

        def body(*refs):
            refs[-1][...] = jnp.zeros_like(refs[-1])

        slab = pl.pallas_call(
            body, name=f"{name}_{i}", grid=(shape[0], (hi - lo) // step), out_shape=jax.ShapeDtypeStruct(shape, F32),
            in_specs=[] if slab is None else [pl.BlockSpec(memory_space=pl.ANY)],
            out_specs=pl.BlockSpec((1, step, shape[2]), functools.partial(lambda k, j, b: (k, b + j, 0), b=lo // step)),
            input_output_aliases={} if slab is None else {0: 0}, compiler_params=_cp("parallel", "parallel"),
        )(*(() if slab is None else (slab,)))
    return slab


def _dw_into_slab(name, a, b, slab, shape, at):
    (k, m), n = a.shape, b.shape[1]
    r = m // N_CHIP
    fits = [(k // tk + 0.5 * (m // tm), tm, tk)
            for tk in (4096, 2048, 1024, 512, 256, 128) if k % tk == 0 for tm in (m, m // 2, r) if tm % LANES == 0
            if 2 * (tk * tm * a.dtype.itemsize + tk * n * b.dtype.itemsize + tm * n * 4) <= MM_VMEM_BUDGET]
    _, tm, tk = min(fits)
    per, nk = tm // r, k // tk

    def body(a_ref, b_ref, *rest):
        o_ref = rest[-1]
        part = _raw_dot("tn", a_ref[...], b_ref[...], False).reshape(o_ref.shape)
        if nk == 1:
            o_ref[...] = part
            return
        kk = pl.program_id(1)

        @pl.when(kk == 0)
        def _():
            o_ref[...] = part

        @pl.when(kk > 0)
        def _():
            o_ref[...] += part

    prev = () if slab is None else (slab,)
    return pl.pallas_call(
        body, name=name, grid=(m // tm, nk), out_shape=jax.ShapeDtypeStruct(shape, F32),
        in_specs=[pl.BlockSpec((tk, tm), lambda i, kk: (kk, i)), pl.BlockSpec((tk, n), lambda i, kk: (kk, 0))]
        + [pl.BlockSpec(memory_space=pl.ANY)] * len(prev),
        out_specs=pl.BlockSpec((per, r, n), lambda i, kk: (i, at // r, 0)),
        input_output_aliases={2: 0} if prev else {}, compiler_params=_cp("parallel", "arbitrary"),
    )(a, b, *prev)


def _rowwise(name, fn, rows, row_ins, full_ins, row_outs, acc_outs, tm=None):
    tm = tm or _pick(rows, (512, 256, 128))
    n_r, n_f, n_o, n_a = len(row_ins), len(full_ins), len(row_outs), len(acc_outs)

    def body(*refs):
        ins, outs = refs[:n_r + n_f], refs[n_r + n_f:]
        vals = [r[...].astype(F32) for r in ins]
        ro, ao = fn(*vals)
        for r, val in zip(outs[:n_o], ro):
            r[...] = val.astype(r.dtype)
        if n_a:
            @pl.when(pl.program_id(0) == 0)
            def _():
                for r in outs[n_o:]:
                    r[...] = jnp.zeros_like(r)

            for r, val in zip(outs[n_o:], ao):
                r[...] += val

    in_specs = [pl.BlockSpec((tm, w), functools.partial(lambda i, cb: (i, cb), cb=cb)) for _, w, cb in row_ins]
    in_specs += [pl.BlockSpec(a.shape, lambda i: (0, 0)) for a in full_ins]
    out_specs = [pl.BlockSpec((tm, w), lambda i: (i, 0)) for w, _ in row_outs]
    out_specs += [pl.BlockSpec(s, lambda i: (0, 0)) for s in acc_outs]
    out_shape = [jax.ShapeDtypeStruct((rows, w), dt) for w, dt in row_outs]
    out_shape += [jax.ShapeDtypeStruct(s, F32) for s in acc_outs]
    return pl.pallas_call(
        body, name=name, grid=(rows // tm,), in_specs=in_specs, out_specs=out_specs, out_shape=out_shape,
        compiler_params=_cp("arbitrary" if n_a else "parallel"),
    )(*[a for a, _, _ in row_ins], *full_ins)


def _rn(x):
    return x * lax.rsqrt(jnp.mean(x * x, axis=-1, keepdims=True) + 1e-6)


def _sigmoid(t):
    return 1.0 / (1.0 + jnp.exp(-t))


def _f_norm_mod(x, g, sh, sc):
    return _rn(x) * g * (1.0 + sc) + sh


def _f_post_res(xr, y, g, gate):
    return xr + gate * (_rn(y) * g)


@jax.custom_vjp
def _f_swiglu(g, u):
    return g * _sigmoid(g) * u


def _f_swiglu_fwd(g, u):
    s = _sigmoid(g)
    return g * s * u, (g, u, s)


def _f_swiglu_bwd(res, da):
    g, u, s = res
    gs = g * s
    return da * u * (s + gs * (1.0 - s)), da * gs


_f_swiglu.defvjp(_f_swiglu_fwd, _f_swiglu_bwd)


def _logsig(u):
    return jnp.minimum(u, 0.0) - jnp.log(1.0 + jnp.exp(-jnp.abs(u)))


def _f_gate(z, wf, wb, bf, bb):
    return _logsig(_nn(z, wf) + bf) / GATE_TAU, _logsig(_nn(z, wb) + bb) / GATE_TAU


def _f_gla_out(of, ob, gg, gt, bd):
    o = of + ob
    ms = _nn_hi(o * o, bd)
    return o * lax.rsqrt(ms + 1e-6) * gt * (gg * _sigmoid(gg))


def _norm_mod(name, x, g, sh, sc):
    rows, d = x.shape
    return _rowwise(name, lambda x, g, sh, sc: ((_f_norm_mod(x, g, sh, sc),), ()), rows,
                    [(x, d, 0)], [g, sh, sc], [(d, BF16)], [])[0]


def _norm_mod_bwd(name, dh, dres, x, g, sh, sc):
    rows, d = x.shape

    def fn(dh, dres, x, g, sh, sc):
        _, vjp = jax.vjp(_f_norm_mod, x, g, sh, sc)
        dx, dg, dsh, dsc = vjp(dh)
        return (dx + dres,), (dg, dsh, dsc)

    return _rowwise(name, fn, rows, [(dh, d, 0), (dres, d, 0), (x, d, 0)], [g, sh, sc], [(d, F32)],
                    [(1, d)] * 3)


def _post_res_norm_mod(name, xr, y, g_post, gate, g_pre, sh, sc):
    rows, d = xr.shape

    def fn(xr, y, g_post, gate, g_pre, sh, sc):
        x1 = _f_post_res(xr, y, g_post, gate)
        return (x1, _f_norm_mod(x1, g_pre, sh, sc)), ()

    return _rowwise(name, fn, rows, [(xr, d, 0), (y, d, 0)], [g_post, gate, g_pre, sh, sc], [(d, F32), (d, BF16)], [])


def _norm_mod_post_res_bwd(name, dh, dres, x1, y, g_pre, sh, sc, g_post, gate):
    rows, d = x1.shape

    def fn(dh, dres, x1, y, g_pre, sh, sc, g_post, gate):
        _, vjp_norm = jax.vjp(_f_norm_mod, x1, g_pre, sh, sc)
        dx1, dg_pre, dsh, dsc = vjp_norm(dh)
        dx1 = dx1 + dres
        _, vjp_res = jax.vjp(lambda y, g, gate: _f_post_res(jnp.zeros_like(y), y, g, gate), y, g_post, gate)
        dy, dg_post, dgate = vjp_res(dx1)
        return (dx1, dy), (dg_pre, dsh, dsc, dg_post, dgate)

    return _rowwise(name, fn, rows, [(dh, d, 0), (dres, d, 0), (x1, d, 0), (y, d, 0)], [g_pre, sh, sc, g_post, gate],
                    [(d, F32), (d, BF16)], [(1, d)] * 5, tm=_pick(rows, (256, 128)))


def _post_res_loss(name, xr, y, g, gate, target):
    rows, d = xr.shape

    def fn(xr, y, target, g, gate):
        x2, vjp = jax.vjp(lambda y, g, gate: _f_post_res(xr, y, g, gate), y, g, gate)
        diff = x2 - target
        part = 0.5 * jnp.sum(jnp.mean(diff * diff, axis=-1, keepdims=True), axis=0, keepdims=True)
        dx2 = diff * (1.0 / d)
        dy, dg, dgate = vjp(dx2)
        return (dx2, dy), (jnp.broadcast_to(part, (1, LANES)), dg, dgate)

    return _rowwise(name, fn, rows, [(xr, d, 0), (y, d, 0), (target, d, 0)], [g, gate], [(d, F32), (d, BF16)],
                    [(1, LANES), (1, d), (1, d)])


def _mm_rows(name, a, b, mode, fn, extras, outs):
    m, k = a.shape
    tm = _pick(m, (256, 128))

    def body(a_ref, b_ref, *rest):
        tiles = fn(_raw_dot(mode, a_ref[...], b_ref[...], False), *[e[...] for e in rest[:len(extras)]])
        for r, val in zip(rest[len(extras):], tiles):
            r[...] = val.astype(r.dtype)

    row = lambda w: pl.BlockSpec((tm, w), lambda i: (i, 0))
    return pl.pallas_call(
        body, name=name, grid=(m // tm,),
        in_specs=[row(k), pl.BlockSpec(b.shape, lambda i: (0, 0))] + [row(e.shape[1]) for e in extras],
        out_specs=[row(w) for w, _ in outs], out_shape=[jax.ShapeDtypeStruct((m, w), dt) for w, dt in outs],
        compiler_params=_cp("parallel"),
    )(a, b, *extras)


def _ffn_in_swiglu(name, h, w_t):
    f = w_t.shape[0] // 2
    fn = lambda u: (u, _f_swiglu(u[:, :f], u[:, f:]))
    return _mm_rows(name, h, w_t, "nt", fn, [], [(2 * f, BF16), (f, BF16)])


def _ffn_out_dx_swiglu_bwd(name, df, w_out, u):
    f = w_out.shape[0]

    def fn(da, u):
        u = u.astype(F32)
        _, vjp = jax.vjp(_f_swiglu, u[:, :f], u[:, f:])
        return (jnp.concatenate(vjp(da), axis=1),)

    return _mm_rows(name, df, w_out, "nt", fn, [u], [(2 * f, BF16)])[0]


def _gate_fwd(name, p, wf, wb, bf, bb):
    rows = p.shape[0]
    return _rowwise(name, lambda z, wf, wb, bf, bb: (_f_gate(z, wf, wb, bf, bb), ()), rows,
                    [(p, LANES, C_Z // LANES)], [wf, wb, bf, bb], [(GKW, F32)] * 2, [])


def _gate_bwd(name, p, dla_f, dla_b, wf, wb, bf, bb):
    rows = p.shape[0]

    def fn(z, dlf, dlb, wf, wb, bf, bb):
        _, vjp = jax.vjp(_f_gate, z, wf, wb, bf, bb)
        dz, dwf, dwb, dbf, dbb = vjp((dlf, dlb))
        return (dz,), (dwf, dwb, dbf, dbb)

    return _rowwise(name, fn, rows, [(p, LANES, C_Z // LANES), (dla_f, GKW, 0), (dla_b, GKW, 0)],
                    [wf, wb, bf, bb], [(LANES, BF16)], [(LANES, GKW), (LANES, GKW), (1, GKW), (1, GKW)])


def _head_mean_matrix():
    h = np.arange(GVW) // GLA_DV
    return jnp.asarray((h[:, None] == h[None, :]).astype(np.float32) / GLA_DV)


def _gla_out(name, attn, of, ob, p, gt):
    rows = of.shape[0]
    bd = _head_mean_matrix()
    fn = lambda attn, of, ob, gg, gt, bd: ((jnp.concatenate([attn, _f_gla_out(of, ob, gg, gt, bd)], axis=1),), ())
    return _rowwise(name, fn, rows, [(attn, QW, 0), (of, GVW, 0), (ob, GVW, 0), (p, GVW, C_GG // GVW)], [gt, bd],
                    [(MIX, BF16)], [])[0]


def _gla_out_bwd(name, dmix, of, ob, p, gt):
    rows = of.shape[0]
    bd = _head_mean_matrix()

    def fn(dm, of, ob, gg, gt, bd):
        _, vjp = jax.vjp(lambda of, gg, gt: _f_gla_out(of, ob, gg, gt, bd), of, gg, gt)
        do, dgg, dgt = vjp(dm)
        return (do, dgg), (dgt,)

    return _rowwise(name, fn, rows, [(dmix, GVW, 1), (of, GVW, 0), (ob, GVW, 0), (p, GVW, C_GG // GVW)], [gt, bd],
                    [(GVW, F32), (GVW, BF16)], [(1, GVW)])


def _rope_tables(n_tokens):
    t = jnp.arange(n_tokens)
    row = (t // GRID_W).astype(F32)
    col = (t % GRID_W).astype(F32)
    half = HEAD_DIM // 2
    inv_freq = ROPE_BASE ** (-jnp.arange(0, half, 2, dtype=F32) / half)
    ang_r = row[:, None] * inv_freq[None, :]
    ang_c = col[:, None] * inv_freq[None, :]
    ang = jnp.concatenate([ang_r, ang_r, ang_c, ang_c], axis=-1)
    sign = jnp.concatenate([-jnp.ones((16,), F32), jnp.ones((16,), F32)] * 2)
    cos, sin = jnp.cos(ang), jnp.sin(ang) * sign[None, :]
    return jnp.tile(cos, (1, 2)), jnp.tile(sin, (1, 2))


def _rot_pairs(x):
    w = x.shape[-1]
    lane = lax.broadcasted_iota(jnp.int32, x.shape, x.ndim - 1)
    return jnp.where((lane % 32) < 16, pltpu.roll(x, w - 16, x.ndim - 1), pltpu.roll(x, 16, x.ndim - 1))


def _rope_apply(x, cos, sin_signed, inverse):
    reps = x.shape[-1] // LANES
    cos = jnp.concatenate([cos] * reps, axis=-1) if reps > 1 else cos
    sin = jnp.concatenate([sin_signed] * reps, axis=-1) if reps > 1 else sin_signed
    if inverse:
        return x * cos + _rot_pairs(x * sin)
    return x * cos + _rot_pairs(x) * sin


def _rope_fwd(name, p, cos, sin):
    rows = p.shape[0]

    def fn(q, k, v, cos, sin):
        return (_rope_apply(q, cos, sin, False), _rope_apply(k, cos, sin, False), v), ()

    return _rowwise(name, fn, rows, [(p, QW, 0), (p, KVW, C_K // KVW), (p, KVW, C_V // KVW), (cos, LANES, 0),
                                     (sin, LANES, 0)], [], [(QW, BF16), (KVW, BF16), (KVW, BF16)], [])


def _proj_grad(name, dq_rot, dk_rot, dv, cos, sin, gla_f, gla_b, dgg, dz):
    rows = dq_rot.shape[0]

    def fn(dq, dk, dv, cos, sin, gqf, gkf, gvf, gqb, gkb, gvb, dgg, dz):
        parts = [_rope_apply(dq, cos, sin, True), gvf + gvb, dgg, _rope_apply(dk, cos, sin, True), dv, gqf + gqb,
                 gkf + gkb, dz]
        return (jnp.concatenate(parts, axis=1),), ()

    ins = [(dq_rot, QW), (dk_rot, KVW), (dv, KVW), (cos, LANES), (sin, LANES)]
    ins += [(t, t.shape[1]) for t in (*gla_f, *gla_b)] + [(dgg, GVW), (dz, LANES)]
    return _rowwise(name, fn, rows, [(t, w, 0) for t, w in ins], [], [(IN_PAD, BF16)], [],
                    tm=_pick(rows, (256, 128)))[0]


GROUP_ROWS = ATT_GROUP * BLOCK


def _f_attn(qs, kws, vws, kcs, vcs, sink, n, n_tokens):
    row = lax.broadcasted_iota(jnp.int32, (GROUP_ROWS, 1), 0)
    group = sum((row >= g * BLOCK).astype(jnp.int32) for g in range(1, ATT_GROUP))
    i = lax.broadcasted_iota(jnp.int32, (GROUP_ROWS, 3 * BLOCK), 0) - BLOCK * group
    j = lax.broadcasted_iota(jnp.int32, (GROUP_ROWS, 3 * BLOCK), 1)
    kpos = (n - 1) * BLOCK + j
    mask = (jnp.abs(j - BLOCK - i) <= WINDOW) & (kpos >= 0) & (kpos < n_tokens)
    head_id = lax.broadcasted_iota(jnp.int32, (1, ATT_HEADS), 1)
    scale = HEAD_DIM ** -0.5
    outs = []
    for h in range(ATT_KV_HEADS):
        sk = jnp.zeros((GROUP_ROWS, 1), F32)
        for g in range(ATT_GROUP):
            one = jnp.sum(jnp.where(head_id == h * ATT_GROUP + g, sink, 0.0), axis=-1, keepdims=True)
            sk = jnp.where(group == g, one, sk)
        q = qs[h] * scale
        s_w = jnp.where(mask, _nt(q, kws[h]), NEG_INF)
        s_c = _nt(q, kcs[h])
        m = lax.stop_gradient(jnp.maximum(jnp.maximum(jnp.max(s_w, axis=-1, keepdims=True),
                                                      jnp.max(s_c, axis=-1, keepdims=True)), sk))
        pw, pc = jnp.exp(s_w - m), jnp.exp(s_c - m)
        den = jnp.sum(pw, axis=-1, keepdims=True) + jnp.sum(pc, axis=-1, keepdims=True) + jnp.exp(sk - m)
        outs.append((_nn(pw, vws[h]) + _nn(pc, vcs[h])) / den)
    return tuple(outs)


def _group_rows(ref, h):
    hs = lambda hq: slice(hq * HEAD_DIM, (hq + 1) * HEAD_DIM)
    return jnp.concatenate([ref[:, hs(h * ATT_GROUP + g)].astype(F32) for g in range(ATT_GROUP)], axis=0)


def _ungroup_rows(ref, h, val):
    for g in range(ATT_GROUP):
        hq = h * ATT_GROUP + g
        ref[:, hq * HEAD_DIM:(hq + 1) * HEAD_DIM] = val[g * BLOCK:(g + 1) * BLOCK].astype(ref.dtype)


def _attn_loads(n, q_ref, kp_ref, vp_ref, kc_ref, vc_ref):
    r0 = pl.multiple_of(n * BLOCK, BLOCK)
    hs = lambda h: slice(h * HEAD_DIM, (h + 1) * HEAD_DIM)
    qs = [_group_rows(q_ref, h) for h in range(ATT_KV_HEADS)]
    kws = [kp_ref[pl.ds(r0, 3 * BLOCK), hs(h)].astype(F32) for h in range(ATT_KV_HEADS)]
    vws = [vp_ref[pl.ds(r0, 3 * BLOCK), hs(h)].astype(F32) for h in range(ATT_KV_HEADS)]
    kcs = [kc_ref[:, hs(h)].astype(F32) for h in range(ATT_KV_HEADS)]
    vcs = [vc_ref[:, hs(h)].astype(F32) for h in range(ATT_KV_HEADS)]
    return r0, hs, qs, kws, vws, kcs, vcs


def _attn_specs(s, c):
    full = lambda shape: pl.BlockSpec(shape, lambda n: (0, 0))
    return [pl.BlockSpec((BLOCK, QW), lambda n: (n, 0)), full((s + 2 * BLOCK, KVW)), full((s + 2 * BLOCK, KVW)),
            full((c, KVW)), full((c, KVW)), full((1, ATT_HEADS))]


def _attn_fwd(q, kp, vp, kc, vc, sink):
    s, c = q.shape[0], kc.shape[0]

    def body(q_ref, kp_ref, vp_ref, kc_ref, vc_ref, sink_ref, o_ref):
        n = pl.program_id(0)
        _, hs, qs, kws, vws, kcs, vcs = _attn_loads(n, q_ref, kp_ref, vp_ref, kc_ref, vc_ref)
        outs = _f_attn(qs, kws, vws, kcs, vcs, sink_ref[...], n, s)
        for h in range(ATT_KV_HEADS):
            _ungroup_rows(o_ref, h, outs[h])

    return pl.pallas_call(
        body, name="attn_fwd", grid=(s // BLOCK,), in_specs=_attn_specs(s, c),
        out_specs=pl.BlockSpec((BLOCK, QW), lambda n: (n, 0)), out_shape=jax.ShapeDtypeStruct((s, QW), BF16),
        compiler_params=_cp("parallel"),
    )(q, kp, vp, kc, vc, sink)


def _attn_bwd(do, q, kp, vp, kc, vc, sink):
    s, c = q.shape[0], kc.shape[0]

    def body(do_ref, q_ref, kp_ref, vp_ref, kc_ref, vc_ref, sink_ref, dq_ref, dkp_ref, dvp_ref, dkc_ref, dvc_ref,
             dsink_ref):
        n = pl.program_id(0)

        @pl.when(n == 0)
        def _():
            for r in (dkp_ref, dvp_ref, dkc_ref, dvc_ref, dsink_ref):
                r[...] = jnp.zeros_like(r)

        r0, hs, qs, kws, vws, kcs, vcs = _attn_loads(n, q_ref, kp_ref, vp_ref, kc_ref, vc_ref)
        _, vjp = jax.vjp(lambda qs, kws, vws, kcs, vcs, sink: _f_attn(qs, kws, vws, kcs, vcs, sink, n, s),
                         qs, kws, vws, kcs, vcs, sink_ref[...])
        dqs, dkws, dvws, dkcs, dvcs, dsink = vjp(tuple(_group_rows(do_ref, h) for h in range(ATT_KV_HEADS)))
        for h in range(ATT_KV_HEADS):
            _ungroup_rows(dq_ref, h, dqs[h])
            dkp_ref[pl.ds(r0, 3 * BLOCK), hs(h)] += dkws[h]
            dvp_ref[pl.ds(r0, 3 * BLOCK), hs(h)] += dvws[h]
            dkc_ref[:, hs(h)] += dkcs[h]
            dvc_ref[:, hs(h)] += dvcs[h]
        dsink_ref[...] += dsink

    full = lambda shape: pl.BlockSpec(shape, lambda n: (0, 0))
    return pl.pallas_call(
        body, name="attn_bwd", grid=(s // BLOCK,),
        in_specs=[pl.BlockSpec((BLOCK, QW), lambda n: (n, 0))] + _attn_specs(s, c),
        out_specs=[pl.BlockSpec((BLOCK, QW), lambda n: (n, 0)), full((s + 2 * BLOCK, KVW)), full((s + 2 * BLOCK, KVW)),
                   full((c, KVW)), full((c, KVW)), full((1, ATT_HEADS))],
        out_shape=[jax.ShapeDtypeStruct((s, QW), F32), jax.ShapeDtypeStruct((s + 2 * BLOCK, KVW), F32),
                   jax.ShapeDtypeStruct((s + 2 * BLOCK, KVW), F32), jax.ShapeDtypeStruct((c, KVW), F32),
                   jax.ShapeDtypeStruct((c, KVW), F32), jax.ShapeDtypeStruct((1, ATT_HEADS), F32)],
        compiler_params=_cp("arbitrary"),
    )(do, q, kp, vp, kc, vc, sink)


GLA_GROUPS = 1
GLA_GROUP_HEADS = GLA_HEADS // GLA_GROUPS
GKG, GVG = GKW // GLA_GROUPS, GVW // GLA_GROUPS


def _gla_masks(heads=GLA_HEADS):
    hk = np.arange(heads * GLA_DK) // GLA_DK
    hv = np.arange(heads * GLA_DV) // GLA_DV
    head_k = (np.arange(heads)[:, None] == hk[None, :]).astype(np.float32)
    head_v = (np.arange(heads)[:, None] == hv[None, :]).astype(np.float32)
    bd_t = (hv[:, None] == hk[None, :]).astype(np.float32)
    return jnp.asarray(head_k), jnp.asarray(head_v), jnp.asarray(bd_t)


def _group_states(st):
    return jnp.stack([st[g * GVG:(g + 1) * GVG, g * GKG:(g + 1) * GKG] for g in range(GLA_GROUPS)])


def _ungroup_states(st):
    out = jnp.zeros((GVW, GKW), st.dtype)
    for g in range(GLA_GROUPS):
        out = out.at[g * GVG:(g + 1) * GVG, g * GKG:(g + 1) * GKG].set(st[g])
    return out


def _tri(n, rev, strict=False):
    i = lax.broadcasted_iota(jnp.int32, (n, n), 0)
    j = lax.broadcasted_iota(jnp.int32, (n, n), 1)
    if strict:
        keep = (j > i) if rev else (j < i)
    else:
        keep = (j >= i) if rev else (j <= i)
    return keep


def _f_gla_chunk(q, k, v, la, st, head_k, head_v, bd_t, rev):
    return _f_gla_carry(*_f_gla_intra(q, k, v, la, head_k, head_v, rev), v, st, bd_t)


def _f_gla_intra(q, k, v, la, head_k, head_v, rev):
    heads, kw, vw = head_k.shape[0], q.shape[1], v.shape[1]
    keep = _tri(GLA_CHUNK, rev)
    b = _nn_hi(keep.astype(F32), la)
    bl = jnp.sum(la, axis=0, keepdims=True)
    qd = q * (GLA_DK ** -0.5) * jnp.exp(b)
    ki = k * jnp.exp(-b)
    kd = k * jnp.exp(bl - b)
    q_heads = (qd[None, :, :] * head_k[:, None, :]).reshape(heads * GLA_CHUNK, kw)
    a_all = _nt(q_heads, ki).reshape(heads, GLA_CHUNK, GLA_CHUNK)
    a_all = jnp.where(keep[None, :, :], a_all, 0.0).reshape(heads * GLA_CHUNK, GLA_CHUNK)
    o_all = _nn(a_all, v).reshape(heads, GLA_CHUNK, vw)
    return jnp.sum(o_all * head_v[:, None, :], axis=0), qd, kd, bl


def _f_gla_carry(intra, qd, kd, bl, v, st, bd_t):
    return intra + _nt(qd, st), st * jnp.exp(bl) + bd_t * _tn(v, kd)


def _gla_specs(s, tb, order):
    return [pl.BlockSpec((tb, GKW), lambda i: (order(i), C_GQ // GKW)),
            pl.BlockSpec((tb, GKW), lambda i: (order(i), C_GK // GKW)),
            pl.BlockSpec((tb, GVW), lambda i: (order(i), C_GV // GVW)),
            pl.BlockSpec((tb, GKW), lambda i: (order(i), 0))]


GLA_BLOCK_CHUNKS = 4


def _gla_fwd(p, la_f, la_b, st_f0, st_b0):
    s = p.shape[0]
    tb = GLA_BLOCK_CHUNKS * GLA_CHUNK
    nblk = s // tb
    up, down = (lambda i: i), (lambda i: nblk - 1 - i)
    masks = _gla_masks(GLA_GROUP_HEADS)

    def scan(rev, q_ref, k_ref, v_ref, la_ref, o_ref, sts_ref, st_ref, consts):
        for g in range(GLA_GROUPS):
            gk, gv = slice(g * GKG, (g + 1) * GKG), slice(g * GVG, (g + 1) * GVG)
            st = st_ref[g]
            sts_ref[0, g] = st
            chunks = range(GLA_BLOCK_CHUNKS)
            for ci in (reversed(chunks) if rev else chunks):
                rows = slice(ci * GLA_CHUNK, (ci + 1) * GLA_CHUNK)
                o, st = _f_gla_chunk(q_ref[rows, gk], k_ref[rows, gk], v_ref[rows, gv], la_ref[rows, gk], st, *consts,
                                     rev)
                o_ref[rows, gv] = o
            st_ref[g] = st

    def body(qf, kf, vf, laf, qb, kb, vb, lab, stf0, stb0, hk_ref, hv_ref, bd_ref, of_ref, stsf_ref, ob_ref, stsb_ref,
             stf_ref, stb_ref):
        @pl.when(pl.program_id(0) == 0)
        def _():
            stf_ref[...] = stf0[...]
            stb_ref[...] = stb0[...]

        consts = (hk_ref[...], hv_ref[...], bd_ref[...])
        scan(False, qf, kf, vf, laf, of_ref, stsf_ref, stf_ref, consts)
        scan(True, qb, kb, vb, lab, ob_ref, stsb_ref, stb_ref, consts)

    full = lambda a: pl.BlockSpec(a.shape, lambda i: (0,) * a.ndim)
    outs = lambda order: [pl.BlockSpec((tb, GVW), lambda i: (order(i), 0)),
                          pl.BlockSpec((1, GLA_GROUPS, GVG, GKG), lambda i: (order(i), 0, 0, 0))]
    return pl.pallas_call(
        body, name="gla_fwd", grid=(nblk,),
        in_specs=_gla_specs(s, tb, up) + _gla_specs(s, tb, down) + [full(st_f0), full(st_b0)]
        + [full(m) for m in masks],
        out_specs=outs(up) + outs(down),
        out_shape=[jax.ShapeDtypeStruct((s, GVW), F32), jax.ShapeDtypeStruct((nblk, GLA_GROUPS, GVG, GKG), F32)] * 2,
        scratch_shapes=[pltpu.VMEM((GLA_GROUPS, GVG, GKG), F32)] * 2,
        compiler_params=_cp("arbitrary"),
    )(p, p, p, la_f, p, p, p, la_b, st_f0, st_b0, *masks)


def _gla_bwd(p, la_f, la_b, sts_f, sts_b, do, after=None):
    s = p.shape[0]
    tb = GLA_BLOCK_CHUNKS * GLA_CHUNK
    nblk = s // tb
    up, down = (lambda i: i), (lambda i: nblk - 1 - i)
    masks = _gla_masks(GLA_GROUP_HEADS)
    follow = () if after is None else (after,)

    def back(rev, q_ref, k_ref, v_ref, la_ref, sts_ref, do_ref, dq_ref, dk_ref, dv_ref, dla_ref, dst0_ref, dst_ref,
             consts):
        def block(q, k, v, la, st):
            outs = [None] * GLA_BLOCK_CHUNKS
            chunks = range(GLA_BLOCK_CHUNKS)
            for ci in (reversed(chunks) if rev else chunks):
                outs[ci], st = _f_gla_chunk(q[ci], k[ci], v[ci], la[ci], st, *consts, rev)
            return tuple(outs), st

        for g in range(GLA_GROUPS):
            gk, gv = slice(g * GKG, (g + 1) * GKG), slice(g * GVG, (g + 1) * GVG)
            split = lambda r, cols: tuple(r[ci * GLA_CHUNK:(ci + 1) * GLA_CHUNK, cols].astype(F32)
                                          for ci in range(GLA_BLOCK_CHUNKS))
            _, vjp = jax.vjp(block, split(q_ref, gk), split(k_ref, gk), split(v_ref, gv), split(la_ref, gk),
                             sts_ref[0, g])
            dq, dk, dv, dla, dst = vjp((split(do_ref, gv), dst_ref[g]))
            for ci in range(GLA_BLOCK_CHUNKS):
                rows = slice(ci * GLA_CHUNK, (ci + 1) * GLA_CHUNK)
                dq_ref[rows, gk], dk_ref[rows, gk], dv_ref[rows, gv], dla_ref[rows, gk] = dq[ci], dk[ci], dv[ci], dla[ci]
            dst_ref[g] = dst
            dst0_ref[g] = dst

    def body(*refs):
        ins, (hk_ref, hv_ref, bd_ref) = refs[:12], refs[12:15]
        outs = refs[15 + len(follow):]

        @pl.when(pl.program_id(0) == 0)
        def _():
            outs[10][...] = jnp.zeros_like(outs[10])
            outs[11][...] = jnp.zeros_like(outs[11])

        consts = (hk_ref[...], hv_ref[...], bd_ref[...])
        back(False, *ins[:6], *outs[:5], outs[10], consts)
        back(True, *ins[6:], *outs[5:10], outs[11], consts)

    full = lambda a: pl.BlockSpec(a.shape, lambda i: (0,) * a.ndim)

    def ins(order):
        return _gla_specs(s, tb, order) + [pl.BlockSpec((1, GLA_GROUPS, GVG, GKG), lambda i: (order(i), 0, 0, 0)),
                                           pl.BlockSpec((tb, GVW), lambda i: (order(i), 0))]

    def outs(order):
        blk = lambda w: pl.BlockSpec((tb, w), lambda i: (order(i), 0))
        return [blk(GKW), blk(GKW), blk(GVW), blk(GKW), pl.BlockSpec((GLA_GROUPS, GVG, GKG), lambda i: (0, 0, 0))]

    shapes = [jax.ShapeDtypeStruct((s, GKW), F32), jax.ShapeDtypeStruct((s, GKW), F32),
              jax.ShapeDtypeStruct((s, GVW), F32), jax.ShapeDtypeStruct((s, GKW), F32),
              jax.ShapeDtypeStruct((GLA_GROUPS, GVG, GKG), F32)]
    both = pl.pallas_call(
        body, name="gla_bwd", grid=(nblk,),
        in_specs=ins(down) + ins(up) + [full(m) for m in masks] + [pl.BlockSpec(memory_space=pl.ANY)] * len(follow),
        out_specs=outs(down) + outs(up), out_shape=shapes * 2,
        scratch_shapes=[pltpu.VMEM((GLA_GROUPS, GVG, GKG), F32)] * 2,
        compiler_params=_cp("arbitrary"),
    )(p, p, p, la_f, sts_f, do, p, p, p, la_b, sts_b, do, *masks, *follow)
    return both[:5], both[5:]


def _f_ctx_state(k, v, la_f, la_b, bd_t):
    c = k.shape[0]
    after = _nn_hi(_tri(c, True, strict=True).astype(F32), la_f)
    before = _nn_hi(_tri(c, False, strict=True).astype(F32), la_b)
    return bd_t * _tn(v, k * jnp.exp(after)), bd_t * _tn(v, k * jnp.exp(before))


def _ctx_state(pc, la_f, la_b):
    c = pc.shape[0]
    bd_t = _gla_masks()[2]

    def body(k_ref, v_ref, lf_ref, lb_ref, bd_ref, sf_ref, sb_ref):
        sf_ref[...], sb_ref[...] = _f_ctx_state(k_ref[...], v_ref[...], lf_ref[...], lb_ref[...], bd_ref[...])

    full = lambda a: pl.BlockSpec(a.shape, lambda i: (0, 0))
    return pl.pallas_call(
        body, name="ctx_state_fwd", grid=(1,),
        in_specs=[pl.BlockSpec((c, GKW), lambda i: (0, C_GK // GKW)), pl.BlockSpec((c, GVW), lambda i: (0, C_GV // GVW)),
                  full(la_f), full(la_b), full(bd_t)],
        out_specs=[pl.BlockSpec((GVW, GKW), lambda i: (0, 0))] * 2,
        out_shape=[jax.ShapeDtypeStruct((GVW, GKW), F32)] * 2,
        compiler_params=_cp("arbitrary"),
    )(pc, pc, la_f, la_b, bd_t)


def _ctx_state_bwd(pc, la_f, la_b, dsf, dsb):
    c = pc.shape[0]
    bd_t = _gla_masks()[2]

    def body(k_ref, v_ref, lf_ref, lb_ref, bd_ref, dsf_ref, dsb_ref, dk_ref, dv_ref, dlf_ref, dlb_ref):
        _, vjp = jax.vjp(lambda k, v, lf, lb: _f_ctx_state(k, v, lf, lb, bd_ref[...]),
                         k_ref[...], v_ref[...], lf_ref[...], lb_ref[...])
        dk, dv, dlf, dlb = vjp((dsf_ref[...], dsb_ref[...]))
        dk_ref[...], dv_ref[...] = dk.astype(BF16), dv.astype(BF16)
        dlf_ref[...], dlb_ref[...] = dlf, dlb

    full = lambda a: pl.BlockSpec(a.shape, lambda i: (0, 0))
    return pl.pallas_call(
        body, name="ctx_state_bwd", grid=(1,),
        in_specs=[pl.BlockSpec((c, GKW), lambda i: (0, C_GK // GKW)), pl.BlockSpec((c, GVW), lambda i: (0, C_GV // GVW)),
                  full(la_f), full(la_b), full(bd_t), full(dsf), full(dsb)],
        out_specs=[pl.BlockSpec((c, GKW), lambda i: (0, 0)), pl.BlockSpec((c, GVW), lambda i: (0, 0)),
                   pl.BlockSpec((c, GKW), lambda i: (0, 0)), pl.BlockSpec((c, GKW), lambda i: (0, 0))],
        out_shape=[jax.ShapeDtypeStruct((c, GKW), BF16), jax.ShapeDtypeStruct((c, GVW), BF16),
                   jax.ShapeDtypeStruct((c, GKW), F32), jax.ShapeDtypeStruct((c, GKW), F32)],
        compiler_params=_cp("arbitrary"),
    )(pc, pc, la_f, la_b, bd_t, dsf, dsb)


_SRC_COLS = ((0, QW), (QW + 2 * KVW + 2 * GKW, GVW), (QW + 2 * KVW + 2 * GKW + GVW, GVW), (QW, KVW), (QW + KVW, KVW),
             (QW + 2 * KVW, GKW), (QW + 2 * KVW + GKW, GKW), (IN_COLS - 2 * GATE_RANK, 2 * GATE_RANK))
_DST_COLS = (C_Q, C_GV, C_GG, C_K, C_V, C_GQ, C_GK, C_Z)


def _pack_w_in(w_in):
    parts = [w_in[:, s:s + n] for s, n in _SRC_COLS]
    parts.append(jnp.zeros((w_in.shape[0], IN_PAD - C_Z - 2 * GATE_RANK), w_in.dtype))
    return jnp.concatenate(parts, axis=1)


def _unpack_w_in_grad(g):
    by_src = sorted(zip(_SRC_COLS, _DST_COLS))
    return jnp.concatenate([g[:, d:d + n] for (_, n), d in by_src], axis=1)


def _prep_gate_weights(w_gate_fwd, w_gate_bwd):
    pad_rows = lambda w, at: jnp.zeros((LANES, GKW), F32).at[at:at + GATE_RANK].set(w)
    return {"wg_f": pad_rows(w_gate_fwd, 0), "wg_b": pad_rows(w_gate_bwd, GATE_RANK)}


def _local_step(x, ctx, target, ada, ada_c, w, late_weights, reduce_behind=None, reduce_w_in=None):
    s, d = x.shape
    sh1, sc1, gt1, sh2, sc2, gt2 = [ada[:, i * d:(i + 1) * d] for i in range(6)]
    sh1c, sc1c = ada_c[:, :d], ada_c[:, d:2 * d]
    cos, sin = _rope_tables(s)
    gt = jnp.tile(w["g_gla_norm"], (1, GLA_HEADS))

    h = _norm_mod("pre_mix", x, w["g_pre_mix"], sh1, sc1)
    hc = _norm_mod("pre_mix_ctx", ctx, w["g_pre_mix"], sh1c, sc1c)
    w_in, token = w["w_in"](h, cos, sin)
    p = _mm("proj_in", h, w_in, "nn", after=token)
    pc = _mm("proj_in_ctx", hc, w_in, "nn")
    q_rot, k_rot, v_b = _rope_fwd("rope", p, cos, sin)
    pad = ((BLOCK, BLOCK), (0, 0))
    kp, vp = jnp.pad(k_rot, pad), jnp.pad(v_b, pad)
    kc, vc = pc[:, C_K:C_K + KVW].astype(BF16), pc[:, C_V:C_V + KVW].astype(BF16)
    attn = _attn_fwd(q_rot, kp, vp, kc, vc, w["attn_sink"])
    gate_w = (w["wg_f"], w["wg_b"], w["b_gate_fwd"], w["b_gate_bwd"])
    la_f, la_b = _gate_fwd("gate", p, *gate_w)
    la_fc, la_bc = _gate_fwd("gate_ctx", pc, *gate_w)
    st_f0, st_b0 = _ctx_state(pc, la_fc, la_bc)
    o_f, sts_f, o_b, sts_b = _gla_fwd(p, la_f, la_b, _group_states(st_f0), _group_states(st_b0))
    mix = _gla_out("gla_out", attn, o_f, o_b, p, gt)
    w_out, w_ffn_in_t, w_ffn_out = late_weights(attn)
    y = _mm("proj_out", mix, w_out, "nn", BF16)
    x1, h2 = _post_res_norm_mod("post_mix_pre_ffn", x, y, w["g_post_mix"], gt1, w["g_pre_ffn"], sh2, sc2)
    u, a = _ffn_in_swiglu("ffn_in", h2, w_ffn_in_t)
    f = _mm("ffn_out", a, w_ffn_out, "nn", BF16)
    g = {}
    dx2, df, loss, g["g_post_ffn"], dgt2 = _post_res_loss("post_ffn_loss", x1, f, w["g_post_ffn"], gt2, target)

    late_rows = {"w_ffn_in_t": w_ffn_in_t.shape[0] // N_CHIP, "w_ffn_out": w_ffn_out.shape[0] // N_CHIP,
                 "w_out": w_out.shape[0] // N_CHIP}
    order = sorted(late_rows, key=lambda n: -late_rows[n])
    offsets, slab_rows = _slab_layout([late_rows[n] for n in order])
    late_at, slab_shape = dict(zip(order, offsets)), (N_CHIP, slab_rows, d)
    slab = _slab_zero_gaps("late_grads_gaps", slab_shape, [late_rows[n] for n in order], offsets)
    slab = _dw_into_slab("ffn_out_dw", a, df, slab, slab_shape, late_at["w_ffn_out"])
    du = _ffn_out_dx_swiglu_bwd("ffn_out_dx", df, w_ffn_out, u)
    dh2 = _mm("ffn_in_dx", du, w_ffn_in_t, "nn", BF16)
    slab = _dw_into_slab("ffn_in_dw", du, h2, slab, slab_shape, late_at["w_ffn_in_t"])
    dx1, dy, g["g_pre_ffn"], dsh2, dsc2, g["g_post_mix"], dgt1 = _norm_mod_post_res_bwd(
        "pre_ffn_post_mix_bwd", dh2, dx2, x1, y, w["g_pre_ffn"], sh2, sc2, w["g_post_mix"], gt1)
    dmix = _mm("proj_out_dx", dy, w_out, "nt", BF16)
    slab = _dw_into_slab("proj_out_dw", mix, dy, slab, slab_shape, late_at["w_out"])
    g["late"], g["late_at"], g["late_rows"] = slab, late_at, late_rows
    rb, sink, token = reduce_behind, w["attn_sink"], None
    if rb is not None:
        gt = _behind(gt, rb.start_slab(slab))
    d_o, dgg, dgt = _gla_out_bwd("gla_out_bwd", dmix, o_f, o_b, p, gt)
    g["g_gla_norm"] = jnp.sum(dgt.reshape(GLA_HEADS, GLA_DV), axis=0, keepdims=True)
    if rb is not None:
        token = rb.pair(dgg)
    gla_f, gla_b = _gla_bwd(p, la_f, la_b, sts_f, sts_b, d_o, token)
    (dla_f, dst_f0), (dla_b, dst_b0) = gla_f[3:], gla_b[3:]
    dst_f0, dst_b0 = _ungroup_states(dst_f0), _ungroup_states(dst_b0)
    if rb is not None:
        sink = _behind(sink, rb.total(dla_b))
    dgkc, dgvc, dla_fc, dla_bc = _ctx_state_bwd(pc, la_fc, la_bc, dst_f0, dst_b0)
    dz, dwf, dwb, dbf, dbb = _gate_bwd("gate_bwd", p, dla_f, dla_b, *gate_w)
    dzc, dwfc, dwbc, dbfc, dbbc = _gate_bwd("gate_ctx_bwd", pc, dla_fc, dla_bc, *gate_w)
    g["w_gate_fwd"] = (dwf + dwfc)[:GATE_RANK]
    g["w_gate_bwd"] = (dwb + dwbc)[GATE_RANK:2 * GATE_RANK]
    g["b_gate_fwd"], g["b_gate_bwd"] = dbf + dbfc, dbb + dbbc
    dq_rot, dkp, dvp, dkc, dvc, g["attn_sink"] = _attn_bwd(dmix, q_rot, kp, vp, kc, vc, sink)
    if rb is not None:
        g["late"] = rb.result(dq_rot)
    dp = _proj_grad("proj_grad", dq_rot, dkp[BLOCK:BLOCK + s], dvp[BLOCK:BLOCK + s], cos, sin, gla_f[:3], gla_b[:3],
                    dgg, dz)
    c_rows = ctx.shape[0]
    zeros = lambda n: jnp.zeros((c_rows, n), BF16)
    dpc = jnp.concatenate([zeros(QW), dgvc, zeros(GVW), dkc.astype(BF16), dvc.astype(BF16), zeros(GKW), dgkc, dzc],
                          axis=1)
    g["w_in"] = _mm("proj_in_dw", h, dp, "tn", init=_mm("proj_in_ctx_dw", hc, dpc, "tn"))
    token = None if reduce_w_in is None else reduce_w_in.start(g["w_in"])
    dh = _mm("proj_in_dx", dp, w_in, "nt", BF16, after=token)
    dhc = _mm("proj_in_ctx_dx", dpc, w_in, "nt")
    if reduce_w_in is not None:
        sh1 = _behind(sh1, reduce_w_in.pair(dh))
    dx, dg_a, dsh1, dsc1 = _norm_mod_bwd("pre_mix_bwd", dh, dx1, x, w["g_pre_mix"], sh1, sc1)
    if reduce_w_in is not None:
        dsh1 = _behind(dsh1, reduce_w_in.total(dx))
    _, dg_b, dsh1c, dsc1c = _norm_mod_bwd("pre_mix_ctx_bwd", dhc, jnp.zeros_like(dhc), ctx, w["g_pre_mix"], sh1c,
                                          sc1c)
    g["g_pre_mix"] = dg_a + dg_b
    d_ada = jnp.concatenate([dsh1, dsc1, dgt1, dsh2, dsc2, dgt2], axis=1)
    d_ada_c = jnp.concatenate([dsh1c, dsc1c, jnp.zeros((1, 4 * d), F32)], axis=1)
    return loss, dx, g, d_ada, d_ada_c


HBM = pl.BlockSpec(memory_space=pltpu.HBM)
N_DEV, N_CHIP = 8, 4


def _place():
    x, y, c = lax.axis_index("x"), lax.axis_index("y"), lax.axis_index("c")
    return x, y, c, [(1 - x, y), (x, 1 - y), (1 - x, 1 - y)]


def _row_tile(n, mult, cap):
    return max(t for t in range(mult, min(n, cap) + 1, mult) if n % t == 0)


def _ag_small(name, v, after=None):
    follow = () if after is None else (after,)

    def body(v_ref, *rest):
        out_ref, send_sems, recv_sems = rest[len(follow):]
        x, y, c, _ = _place()
        out_ref[4 * x + 2 * y + c] = v_ref[...]

        def peer(r):
            return ((1 - x) if r & 4 else x, (1 - y) if r & 2 else y, (1 - c) if r & 1 else c)

        def copy(r, block):
            px, py, pc = block
            return pltpu.make_async_remote_copy(
                src_ref=v_ref, dst_ref=out_ref.at[4 * px + 2 * py + pc], send_sem=send_sems.at[r - 1],
                recv_sem=recv_sems.at[r - 1], device_id=peer(r), device_id_type=MESH)

        sends = [copy(r, (x, y, c)) for r in range(1, N_DEV)]
        for cp in sends:
            cp.start()
        for r in range(1, N_DEV):
            copy(r, peer(r)).wait_recv()
        for cp in sends:
            cp.wait_send()

    return pl.pallas_call(
        body, name=name, out_shape=jax.ShapeDtypeStruct((N_DEV,) + v.shape, v.dtype),
        in_specs=[pl.BlockSpec(memory_space=pltpu.VMEM)] + [pl.BlockSpec(memory_space=pl.ANY)] * len(follow),
        out_specs=pl.BlockSpec(memory_space=pltpu.VMEM),
        scratch_shapes=[pltpu.SemaphoreType.DMA((N_DEV - 1,)), pltpu.SemaphoreType.DMA((N_DEV - 1,))],
    )(v, *follow)


def _halves(c, rows, mult):
    hr = rows // 2
    return pl.ds(pl.multiple_of(c * hr, mult), hr), pl.ds(pl.multiple_of((1 - c) * hr, mult), hr)


def _add_half(name, g, a, c_idx):
    n_sh, hr, n = a.shape
    tr = _row_tile(hr, 16, 1024)
    nb = hr // tr

    def body(c_ref, g_ref, a_ref, o_ref):
        o_ref[...] = (g_ref[...] + a_ref[...]).astype(o_ref.dtype)

    return pl.pallas_call(
        body, name=name, out_shape=jax.ShapeDtypeStruct(a.shape, BF16),
        grid_spec=pltpu.PrefetchScalarGridSpec(
            num_scalar_prefetch=1, grid=(n_sh, nb),
            in_specs=[pl.BlockSpec((1, tr, n), lambda s, i, c_ref: (s, c_ref[0] * nb + i, 0)),
                      pl.BlockSpec((1, tr, n), lambda s, i, c_ref: (s, i, 0))],
            out_specs=pl.BlockSpec((1, tr, n), lambda s, i, c_ref: (s, i, 0))),
        compiler_params=_cp("parallel", "parallel"),
    )(c_idx, g, a)


def _sum_chips(name, b):
    n_sh, hr, n = b.shape
    tr = _row_tile(hr, 16, 1024)

    def body(b0, b1, b2, b3, o_ref):
        o_ref[...] = ((b0[0].astype(F32) + b1[0].astype(F32)) + b2[0].astype(F32)) + b3[0].astype(F32)

    return pl.pallas_call(
        body, name=name, grid=(hr // tr,), out_shape=jax.ShapeDtypeStruct((hr, n), F32),
        in_specs=[pl.BlockSpec((1, tr, n), functools.partial(lambda i, k: (k, i, 0), k=k)) for k in range(n_sh)],
        out_specs=pl.BlockSpec((tr, n), lambda i: (i, 0)), compiler_params=_cp("parallel"),
    )(b, b, b, b)


SEM = pl.BlockSpec(memory_space=pltpu.SEMAPHORE)
ANY = pl.BlockSpec(memory_space=pl.ANY)
DATAFLOW = pltpu.SideEffectType.DATAFLOW_SIDE_EFFECTING


def _remote(src, dst, send_sems, recv_sems, k, to):
    return pltpu.make_async_remote_copy(src_ref=src, dst_ref=dst, send_sem=send_sems.at[k], recv_sem=recv_sems.at[k],
                                        device_id=to, device_id_type=MESH)


def _split_copy(name, src, land_shape, land_dtype, n, plan, after=None):
    after = jnp.zeros((8, LANES), F32) if after is None else after

    def start_body(src_ref, land_ref, after_ref, send_sems, recv_sems, src_thru, land_thru, token):
        for cp in plan(src_ref, land_ref, send_sems, recv_sems)[0]:
            cp.start()
        token[...] = jnp.zeros_like(token)

    sems = pltpu.SemaphoreType.DMA((n,))
    send_sems, recv_sems, src_thru, land_thru, token = pl.pallas_call(
        start_body, name=name + "_start",
        out_shape=(sems, sems, pltpu.HBM(src.shape, src.dtype), pltpu.HBM(land_shape, land_dtype),
                   jax.ShapeDtypeStruct((8, LANES), F32)),
        in_specs=(HBM, HBM, ANY), out_specs=(SEM, SEM, HBM, HBM, pl.BlockSpec(memory_space=pltpu.VMEM)),
        input_output_aliases={0: 2, 1: 3}, compiler_params=pltpu.CompilerParams(has_side_effects=DATAFLOW),
    )(pltpu.with_memory_space_constraint(src, pltpu.HBM),
      pltpu.with_memory_space_constraint(lax.empty(land_shape, land_dtype), pltpu.HBM), after)

    def wait(*after):
        def wait_body(src_ref, land_ref, send_sems, recv_sems, *rest):
            sent, received = plan(src_ref, land_ref, send_sems, recv_sems)
            for cp in sent:
                cp.wait_send()
            for cp in received:
                cp.wait_recv()

        return pl.pallas_call(
            wait_body, name=name + "_wait",
            out_shape=(pltpu.HBM(src.shape, src.dtype), pltpu.HBM(land_shape, land_dtype)),
            in_specs=(HBM, HBM, SEM, SEM) + (ANY,) * len(after), out_specs=(HBM, HBM),
            input_output_aliases={0: 0, 1: 1}, compiler_params=pltpu.CompilerParams(has_side_effects=DATAFLOW),
        )(src_thru, land_thru, send_sems, recv_sems, *after)

    return token, wait


def _split_gather(name, shards, after):
    k, n, plan = len(shards), 3 * len(shards), _plan_gather

    def start_body(*refs):
        for cp in plan(refs[:k], refs[k:2 * k], refs[2 * k + 1], refs[2 * k + 2])[0]:
            cp.start()
        refs[-1][...] = jnp.zeros_like(refs[-1])

    sems = pltpu.SemaphoreType.DMA((n,))
    bufs = [pltpu.HBM(s.shape, s.dtype) for s in shards] + [pltpu.HBM((N_CHIP,) + s.shape, s.dtype) for s in shards]
    hbm = lambda t: pltpu.with_memory_space_constraint(t, pltpu.HBM)
    outs = pl.pallas_call(
        start_body, name=name + "_start", out_shape=(sems, sems, *bufs, jax.ShapeDtypeStruct((8, LANES), F32)),
        in_specs=(HBM,) * (2 * k) + (ANY,),
        out_specs=(SEM, SEM) + (HBM,) * (2 * k) + (pl.BlockSpec(memory_space=pltpu.VMEM),),
        input_output_aliases={i: 2 + i for i in range(2 * k)},
        compiler_params=pltpu.CompilerParams(has_side_effects=DATAFLOW),
    )(*[hbm(s) for s in shards], *[hbm(lax.empty((N_CHIP,) + s.shape, s.dtype)) for s in shards], after)
    send_sems, recv_sems, thru, token = outs[0], outs[1], outs[2:2 + 2 * k], outs[-1]

    def wait(*after):
        def wait_body(*refs):
            sent, received = plan(refs[:k], refs[k:2 * k], refs[2 * k], refs[2 * k + 1])
            for cp in sent:
                cp.wait_send()
            for cp in received:
                cp.wait_recv()

        res = pl.pallas_call(
            wait_body, name=name + "_wait", out_shape=tuple(bufs),
            in_specs=(HBM,) * (2 * k) + (SEM, SEM) + (ANY,) * len(after), out_specs=(HBM,) * (2 * k),
            input_output_aliases={i: i for i in range(2 * k)},
            compiler_params=pltpu.CompilerParams(has_side_effects=DATAFLOW),
        )(*thru, send_sems, recv_sems, *after)
        return res[:k], res[k:]

    return token, wait


def _behind(x, token):
    return x + token[0, 0]


def _plan_gather(src_refs, land_refs, send_sems, recv_sems):
    x, y, c, chips = _place()
    pairs = list(enumerate(zip(src_refs, land_refs)))
    sent = [_remote(s, l.at[2 * x + y], send_sems, recv_sems, 3 * i + j, (px, py, c))
            for i, (s, l) in pairs for j, (px, py) in enumerate(chips)]
    received = [_remote(s, l.at[2 * px + py], send_sems, recv_sems, 3 * i + j, (px, py, c))
                for i, (s, l) in pairs for j, (px, py) in enumerate(chips)]
    return sent, received


def _plan_swap(src_ref, land_ref, send_sems, recv_sems):
    x, y, c, _ = _place()
    _, other_half = _halves(c, src_ref.shape[1], 8)
    cp = _remote(src_ref.at[pl.ds(0, src_ref.shape[0]), other_half], land_ref, send_sems, recv_sems, 0, (x, y, 1 - c))
    return [cp], [cp]


def _plan_scatter(src_ref, land_ref, send_sems, recv_sems):
    x, y, c, chips = _place()
    sent = [_remote(src_ref.at[2 * px + py], land_ref.at[2 * x + y], send_sems, recv_sems, j, (px, py, c))
            for j, (px, py) in enumerate(chips)]
    received = [_remote(src_ref.at[2 * px + py], land_ref.at[2 * px + py], send_sems, recv_sems, j, (px, py, c))
                for j, (px, py) in enumerate(chips)]
    return sent, received


def _plan_share(src_ref, land_ref, send_sems, recv_sems):
    x, y, c, _ = _place()
    mine_half, other_half = _halves(c, land_ref.shape[0], 8)
    return ([_remote(src_ref, land_ref.at[mine_half], send_sems, recv_sems, 0, (x, y, 1 - c))],
            [_remote(src_ref, land_ref.at[other_half], send_sems, recv_sems, 0, (x, y, 1 - c))])


class _GatherBehind:
    def __init__(self, name, shards, chip, after):
        self.chip = chip
        self.token, self.wait = _split_gather(name, shards, after)

    def result(self, *after):
        shards, lands = self.wait(*after)
        return [lax.dynamic_update_slice(land, shard[None], (self.chip, 0, 0)) for shard, land in zip(shards, lands)]


class _ReduceBehind:
    def __init__(self, name, chip, c, c_idx):
        self.name, self.chip, self.c, self.c_idx = name, chip, c, c_idx

    def start_slab(self, g):
        n_sh, rows, n = g.shape
        token, self.wait = _split_copy(self.name + "_swap", g, (n_sh, rows // 2, n), g.dtype, 1, _plan_swap)
        return token

    def pair(self, after):
        g, a = self.wait(after)
        h = _add_half(self.name + "_pair", g, a, self.c_idx)
        token, self.wait = _split_copy(self.name + "_scatter", h, h.shape, h.dtype, 3, _plan_scatter)
        return token

    def total(self, after):
        h, b = self.wait(after)
        b = lax.dynamic_update_slice(b, lax.dynamic_slice_in_dim(h, self.chip, 1, axis=0), (self.chip, 0, 0))
        f = _sum_chips(self.name + "_sum", b)
        token, self.wait = _split_copy(self.name + "_share", f, (2 * f.shape[0], f.shape[1]), f.dtype, 1,
                                       _plan_share)
        return token

    def result(self, after):
        f, out = self.wait(after)
        return lax.dynamic_update_slice(out, f, (self.c * f.shape[0], 0))


class _ReduceColsBehind(_ReduceBehind):
    def start(self, g_padded):
        g = _unpack_w_in_grad(g_padded)
        n = g.shape[1] // N_CHIP
        return self.start_slab(jnp.stack([g[:, k * n:(k + 1) * n] for k in range(N_CHIP)]))


def _f_adamw(w, g, m, v):
    m = ADAM_B1 * m + (1.0 - ADAM_B1) * g
    v = ADAM_B2 * v + (1.0 - ADAM_B2) * (g * g)
    m_hat = m / (1.0 - ADAM_B1 ** ADAM_STEP)
    v_hat = v / (1.0 - ADAM_B2 ** ADAM_STEP)
    return -ADAM_LR * (m_hat / (jnp.sqrt(v_hat) + ADAM_EPS) + ADAM_WD * w), m, v


def _adamw(name, w, g, m, v):
    rows, n = w.shape
    return _rowwise(name, lambda w, g, m, v: (_f_adamw(w, g, m, v), ()), rows, [(t, n, 0) for t in (w, g, m, v)], [],
                    [(n, F32)] * 3, [], tm=_row_tile(rows, 8, 256))


def _pack_rows(parts):
    rows = []
    for t in parts:
        t = t.reshape(-1)
        rows.append(jnp.pad(t, (0, -t.shape[0] % LANES)).reshape(-1, LANES))
    out = jnp.concatenate(rows, axis=0)
    return jnp.pad(out, ((0, -out.shape[0] % 8), (0, 0)))


def _unpack_rows(packed, shapes):
    out, r = [], 0
    for shp in shapes:
        n = int(np.prod(shp))
        nr = -(-n // LANES)
        out.append(packed[r:r + nr].reshape(-1)[:n].reshape(shp))
        r += nr
    return out


def _sum_blocks(name, g):
    def body(g_ref, o_ref):
        acc = g_ref[0]
        for k in range(1, g.shape[0]):
            acc = acc + g_ref[k]
        o_ref[...] = acc

    return pl.pallas_call(body, name=name, out_shape=jax.ShapeDtypeStruct(g.shape[1:], F32))(g)


def _silu(t):
    return t * _sigmoid(t)


def _ada_fwd(cc, w_ada):
    n = w_ada.shape[1]
    tn = _row_tile(n, LANES, 512)

    def body(cc_ref, w_ref, o_ref):
        o_ref[...] = _nn(_silu(cc_ref[...]), w_ref[...])

    return pl.pallas_call(
        body, name="ada_fwd", grid=(n // tn,), out_shape=jax.ShapeDtypeStruct((cc.shape[0], n), F32),
        in_specs=[pl.BlockSpec(cc.shape, lambda j: (0, 0)), pl.BlockSpec((w_ada.shape[0], tn), lambda j: (0, j))],
        out_specs=pl.BlockSpec((cc.shape[0], tn), lambda j: (0, j)), compiler_params=_cp("parallel"),
    )(cc, w_ada)


def _ada_bwd(cc, dm, w_ada):
    d, n = w_ada.shape
    tn = _row_tile(n, LANES, 512)

    def body(cc_ref, dm_ref, w_ref, gw_ref, ds_ref):
        @pl.when(pl.program_id(0) == 0)
        def _():
            ds_ref[...] = jnp.zeros_like(ds_ref)

        gw_ref[...] = _raw_dot("tn", _silu(cc_ref[...]), dm_ref[...], True)
        ds_ref[...] += _raw_dot("nt", dm_ref[...], w_ref[...], False)

    return pl.pallas_call(
        body, name="ada_bwd", grid=(n // tn,),
        out_shape=[jax.ShapeDtypeStruct((d, n), F32), jax.ShapeDtypeStruct(cc.shape, F32)],
        in_specs=[pl.BlockSpec(cc.shape, lambda j: (0, 0)), pl.BlockSpec((cc.shape[0], tn), lambda j: (0, j)),
                  pl.BlockSpec((d, tn), lambda j: (0, j))],
        out_specs=[pl.BlockSpec((d, tn), lambda j: (0, j)), pl.BlockSpec(cc.shape, lambda j: (0, 0))],
        compiler_params=_cp("arbitrary"),
    )(cc, dm, w_ada)


def _c_ctx_grad(parts, c_ctx):
    def body(p_ref, c_ref, o_ref):
        ds = ((p_ref[0] + p_ref[1]) + p_ref[2]) + p_ref[3]
        _, vjp = jax.vjp(_silu, c_ref[...])
        o_ref[...] = vjp(ds)[0]

    return pl.pallas_call(body, name="c_ctx_grad", out_shape=jax.ShapeDtypeStruct(c_ctx.shape, F32))(parts, c_ctx)


def kernel(x, c, ctx, c_ctx, w_ada, b_ada, g_pre_mix, g_post_mix, g_pre_ffn, g_post_ffn, w_in, attn_sink, w_gate_fwd, b_gate_fwd, w_gate_bwd, b_gate_bwd, g_gla_norm, w_out, w_ffn_in, w_ffn_out, loss_target, m_c_ctx, m_w_ada, m_b_ada, m_g_pre_mix, m_g_post_mix, m_g_pre_ffn, m_g_post_ffn, m_w_in, m_attn_sink, m_w_gate_fwd, m_b_gate_fwd, m_w_gate_bwd, m_b_gate_bwd, m_g_gla_norm, m_w_out, m_w_ffn_in, m_w_ffn_out, v_c_ctx, v_w_ada, v_b_ada, v_g_pre_mix, v_g_post_mix, v_g_pre_ffn, v_g_post_ffn, v_w_in, v_attn_sink, v_w_gate_fwd, v_b_gate_fwd, v_w_gate_bwd, v_b_gate_bwd, v_g_gla_norm, v_w_out, v_w_ffn_in, v_w_ffn_out):
    xi, yi, ci = lax.axis_index("x"), lax.axis_index("y"), lax.axis_index("c")
    dev, chip = 4 * xi + 2 * yi + ci, 2 * xi + yi
    c_idx = jnp.reshape(ci, (1,)).astype(jnp.int32)
    d = x.shape[-1]
    n_ada, n_in, n_f = w_ada.shape[-1], w_in.shape[-1], w_ffn_in.shape[-1]
    r_out, r_f = w_out.shape[1], w_ffn_out.shape[1]
    n_gate = w_gate_fwd.shape[-1]
    by_chip = lambda t: t[0::2]

    rc = -(-d // LANES)
    g1 = _ag_small("gather_cond", _pack_rows([c[0], w_gate_fwd[0], w_gate_bwd[0]]))
    c_all = g1[:, :rc].reshape(N_DEV, -1)[:, :d]
    gr = GATE_RANK * n_gate // LANES
    gate_full = lambda off: jnp.transpose(by_chip(g1)[:, off:off + gr].reshape(N_CHIP, GATE_RANK, n_gate),
                                          (1, 0, 2)).reshape(GATE_RANK, N_CHIP * n_gate)
    wgf, wgb = gate_full(rc), gate_full(rc + gr)
    cc = jnp.concatenate([c_all, c_ctx[None, :], jnp.zeros((7, d), F32)], axis=0)

    g2 = _ag_small("gather_ada", _ada_fwd(cc, w_ada[0]).reshape(-1, LANES))
    ada_all = jnp.transpose(by_chip(g2).reshape(N_CHIP, 16, n_ada), (1, 0, 2)).reshape(16, N_CHIP * n_ada) + b_ada
    first = _GatherBehind("gather_w_in", [w_in[0].astype(BF16)], chip, g2)
    late_shards = [w_out[0].astype(BF16), jnp.transpose(w_ffn_in[0]).astype(BF16), w_ffn_out[0].astype(BF16)]
    late = []

    def first_weights(*after):
        w_in_g, = first.result(*after, *late_shards)
        late.append(_GatherBehind("gather_late", late_shards, chip, w_in_g))
        return _pack_w_in(jnp.concatenate([w_in_g[k] for k in range(N_CHIP)], axis=1)), late[0].token

    def late_weights(after):
        return [t.reshape(-1, d) for t in late[0].result(after)]

    ada_all = _behind(ada_all, first.token)
    ada = lax.dynamic_slice(ada_all, (dev, 0), (1, N_CHIP * n_ada))
    ada_c = ada_all[N_DEV:N_DEV + 1]

    w = _prep_gate_weights(wgf, wgb)
    w.update(w_in=first_weights, g_pre_mix=g_pre_mix, g_post_mix=g_post_mix, g_pre_ffn=g_pre_ffn, g_post_ffn=g_post_ffn,
             attn_sink=attn_sink, b_gate_fwd=b_gate_fwd, b_gate_bwd=b_gate_bwd, g_gla_norm=g_gla_norm)

    reduce_behind = _ReduceBehind("reduce_late", chip, ci, c_idx)
    reduce_w_in = _ReduceColsBehind("reduce_w_in", chip, ci, c_idx)
    loss_lanes, grad_x, g, d_ada, d_ada_c = _local_step(x[0], ctx[0], loss_target[0], ada, ada_c, w, late_weights,
                                                        reduce_behind, reduce_w_in)

    small = ("g_pre_mix", "g_post_mix", "g_pre_ffn", "g_post_ffn", "attn_sink", "b_gate_fwd", "b_gate_bwd",
             "g_gla_norm", "w_gate_fwd", "w_gate_bwd")
    shapes = [(1, 6 * d)] * 2 + [g[n].shape for n in small] + [(1, LANES)]
    g3 = _ag_small("gather_small_grads", _pack_rows([d_ada, d_ada_c] + [g[n] for n in small] + [loss_lanes]))
    tot = dict(zip(("d_ada", "d_ada_c") + small + ("loss",),
                   _unpack_rows(_sum_blocks("sum_small_grads", g3), shapes)))
    r_ada = 6 * d // LANES
    dm = jnp.concatenate([g3[:, :r_ada].reshape(N_DEV, 6 * d), tot["d_ada_c"], jnp.zeros((7, 6 * d), F32)], axis=0)
    grads = {n: tot[n] for n in small[:8]}
    grads["b_ada"] = _sum_blocks("sum_b_ada", dm.reshape(16, r_ada, LANES)).reshape(1, 6 * d)
    grads["w_gate_fwd"] = lax.dynamic_slice(tot["w_gate_fwd"], (0, chip * n_gate), (GATE_RANK, n_gate))[None]
    grads["w_gate_bwd"] = lax.dynamic_slice(tot["w_gate_bwd"], (0, chip * n_gate), (GATE_RANK, n_gate))[None]
    gw_ada, dsc = _ada_bwd(cc, lax.dynamic_slice(dm, (0, chip * n_ada), (16, n_ada)), w_ada[0])
    grads["w_ada"] = gw_ada[None]
    g4 = _ag_small("gather_c_ctx", _pack_rows([dsc[N_DEV]]))
    grads["c_ctx"] = _c_ctx_grad(by_chip(g4), _pack_rows([c_ctx])).reshape(-1)[:d]

    grads["w_in"] = reduce_w_in.result(g4)[None]
    part = lambda n: g["late"][g["late_at"][n]:g["late_at"][n] + g["late_rows"][n]]
    grads["w_ffn_in"], grads["w_ffn_out"], grads["w_out"] = (jnp.transpose(part("w_ffn_in_t"))[None],
                                                            part("w_ffn_out")[None], part("w_out")[None])

    names = ("c_ctx", "w_ada", "b_ada", "g_pre_mix", "g_post_mix", "g_pre_ffn", "g_post_ffn", "w_in", "attn_sink",
             "w_gate_fwd", "b_gate_fwd", "w_gate_bwd", "b_gate_bwd", "g_gla_norm", "w_out", "w_ffn_in", "w_ffn_out")
    weights = dict(zip(names, (c_ctx, w_ada, b_ada, g_pre_mix, g_post_mix, g_pre_ffn, g_post_ffn, w_in, attn_sink,
                               w_gate_fwd, b_gate_fwd, w_gate_bwd, b_gate_bwd, g_gla_norm, w_out, w_ffn_in,
                               w_ffn_out)))
    m_in = dict(zip(names, (m_c_ctx, m_w_ada, m_b_ada, m_g_pre_mix, m_g_post_mix, m_g_pre_ffn, m_g_post_ffn, m_w_in,
                            m_attn_sink, m_w_gate_fwd, m_b_gate_fwd, m_w_gate_bwd, m_b_gate_bwd, m_g_gla_norm,
                            m_w_out, m_w_ffn_in, m_w_ffn_out)))
    v_in = dict(zip(names, (v_c_ctx, v_w_ada, v_b_ada, v_g_pre_mix, v_g_post_mix, v_g_pre_ffn, v_g_post_ffn, v_w_in,
                            v_attn_sink, v_w_gate_fwd, v_b_gate_fwd, v_w_gate_bwd, v_b_gate_bwd, v_g_gla_norm,
                            v_w_out, v_w_ffn_in, v_w_ffn_out)))
    large = ("w_ada", "w_in", "w_out", "w_ffn_in", "w_ffn_out")
    tiny = tuple(n for n in names if n not in large)
    delta, new_m, new_v = {}, {}, {}
    for n in large:
        dl, nm, nv = _adamw("adamw_" + n, weights[n][0], grads[n][0], m_in[n][0], v_in[n][0])
        delta[n], new_m[n], new_v[n] = dl[None], nm[None], nv[None]
    tiny_shapes = [weights[n].shape for n in tiny]
    packed = [_pack_rows([t[n] for n in tiny]) for t in (weights, grads, m_in, v_in)]
    for out, res in zip((delta, new_m, new_v), _adamw("adamw_small", *packed)):
        out.update(zip(tiny, _unpack_rows(res, tiny_shapes)))
    for n in tiny:
        grads[n] = grads[n].reshape(weights[n].shape)

    return (tot["loss"][0, 0], grad_x[None], *[grads[n] for n in names], *[delta[n] for n in names], *[new_m[n] for n in names],
            *[new_v[n] for n in names])
```

```python
import functools

import jax
import jax.numpy as jnp
import numpy as np
from jax import lax
from jax.experimental import pallas as pl
from jax.experimental.pallas import tpu as pltpu

F32 = jnp.float32
BF16 = jnp.bfloat16
MESH = pl.DeviceIdType.MESH

HEAD_DIM = 64
ATT_HEADS = 8
ATT_KV_HEADS = 2
ATT_GROUP = ATT_HEADS // ATT_KV_HEADS
WINDOW = 128
BLOCK = 128
GRID_W = 64
ROPE_BASE = 10000.0
GLA_HEADS = 8
GLA_DK = 32
GLA_DV = 64
GLA_CHUNK = 64
GATE_RANK = 16
GATE_TAU = 16.0
NEG_INF = -1e30
QW = ATT_HEADS * HEAD_DIM
KVW = ATT_KV_HEADS * HEAD_DIM
GKW = GLA_HEADS * GLA_DK
GVW = GLA_HEADS * GLA_DV
IN_COLS = QW + 2 * KVW + 2 * GKW + 2 * GVW + 2 * GATE_RANK
LANES = 128
IN_PAD = IN_COLS + LANES - 2 * GATE_RANK
C_Q, C_GV, C_GG = 0, QW, QW + GVW
C_K = C_GG + GVW
C_V = C_K + KVW
C_GQ = C_V + KVW
C_GK = C_GQ + GKW
C_Z = C_GK + GKW
MIX = QW + GVW

ADAM_LR, ADAM_B1, ADAM_B2, ADAM_EPS, ADAM_WD, ADAM_STEP = 0.001, 0.9, 0.999, 1e-08, 0.01, 10

VMEM_LIMIT = 56 * 1024 * 1024


def _cp(*sem):
    return pltpu.CompilerParams(dimension_semantics=sem, vmem_limit_bytes=VMEM_LIMIT)


def _pick(n, cands):
    for t in cands:
        if n % t == 0:
            return t
    return n


_DIMS = {"nn": (((1,), (0,)), ((), ())), "nt": (((1,), (1,)), ((), ())), "tn": (((0,), (0,)), ((), ()))}


def _raw_dot(mode, a, b, hi):
    dot = lambda u, v: lax.dot_general(u, v, _DIMS[mode], preferred_element_type=F32)
    if hi:
        a, b = a.astype(F32), b.astype(F32)
        a_hi, b_hi = a.astype(BF16), b.astype(BF16)
        a_lo, b_lo = (a - a_hi.astype(F32)).astype(BF16), (b - b_hi.astype(F32)).astype(BF16)
        return dot(a_hi, b_hi) + (dot(a_lo, b_hi) + dot(a_hi, b_lo))
    return dot(a.astype(BF16), b.astype(BF16))


def _make_dot(mode, hi):
    @jax.custom_vjp
    def dot(a, b):
        return _raw_dot(mode, a, b, hi)

    def fwd(a, b):
        return _raw_dot(mode, a, b, hi), (a, b)

    def bwd(res, dc):
        a, b = res
        if mode == "nn":
            return _raw_dot("nt", dc, b, hi), _raw_dot("tn", a, dc, hi)
        if mode == "nt":
            return _raw_dot("nn", dc, b, hi), _raw_dot("tn", dc, a, hi)
        return _raw_dot("nt", b, dc, hi), _raw_dot("nn", a, dc, hi)

    dot.defvjp(fwd, bwd)
    return dot


_nn, _nt, _tn = _make_dot("nn", False), _make_dot("nt", False), _make_dot("tn", False)
_nn_hi = _make_dot("nn", True)


MM_VMEM_BUDGET = 44 * 1024 * 1024


def _halvings(n):
    out = [n]
    while out[-1] % (2 * LANES) == 0:
        out.append(out[-1] // 2)
    return out


def _mm_tiles(mode, m, n, k, a_bytes, b_bytes, o_bytes, init_bytes=0):
    tms = [t for t in dict.fromkeys((m, m // 2, m // 4, 2048, 1024, 512, 256, 128))
           if m % t == 0 and t % (LANES if mode == "tn" else 16) == 0 and t <= 4096] or [m]
    if mode == "tn":
        fits = [(k // tk + 0.5 * (m // tm), tm, tk)
                for tk in (4096, 2048, 1024, 512, 256, 128) if k % tk == 0 for tm in tms
                if 2 * (tk * tm * a_bytes + tk * n * b_bytes + tm * n * (o_bytes + init_bytes)) <= MM_VMEM_BUDGET]
        if fits:
            _, tm, tk = min(fits)
            return tm, n, tk
    tks = ([t for t in (512, 256, 128) if k % t == 0] or [k]) if mode == "tn" else _halvings(k)
    for tn in _halvings(n):
        for tk in tks:
            for tm in tms:
                acc = tm * tn * 4 if (k // tk > 1 and o_bytes != 4) else 0
                tiles = tm * tk * a_bytes + tk * tn * b_bytes + tm * tn * (o_bytes + init_bytes)
                if 2 * tiles + acc <= MM_VMEM_BUDGET:
                    return tm, tn, tk
    return tms[-1], _halvings(n)[-1], tks[-1]


def _mm(name, a, b, mode, out_dtype=F32, init=None, after=None):
    follow = () if after is None else (after,)
    if mode == "nn":
        (m, k), n = a.shape, b.shape[1]
    elif mode == "nt":
        (m, k), n = a.shape, b.shape[0]
    else:
        (k, m), n = a.shape, b.shape[1]
    tm, tn, tk = _mm_tiles(mode, m, n, k, a.dtype.itemsize, b.dtype.itemsize, jnp.dtype(out_dtype).itemsize,
                           0 if init is None else 4)
    nk = k // tk
    use_acc = nk > 1 and out_dtype != F32

    inits = () if init is None else (init,)

    def body(a_ref, b_ref, *rest):
        rest = rest[:len(inits)] + rest[len(inits) + len(follow):]
        o_ref, acc = rest[len(inits)], rest[len(inits) + 1:]
        part = _raw_dot(mode, a_ref[...], b_ref[...], False)
        first = lambda: part + rest[0][...] if inits else part
        if nk == 1:
            o_ref[...] = first().astype(o_ref.dtype)
            return
        acc_ref = acc[0] if use_acc else o_ref
        kk = pl.program_id(2)

        @pl.when(kk == 0)
        def _():
            acc_ref[...] = first()

        @pl.when(kk > 0)
        def _():
            acc_ref[...] += part

        if use_acc:
            @pl.when(kk == nk - 1)
            def _():
                o_ref[...] = acc_ref[...].astype(o_ref.dtype)

    if mode == "nn":
        a_spec = pl.BlockSpec((tm, tk), lambda i, j, kk: (i, kk))
        b_spec = pl.BlockSpec((tk, tn), lambda i, j, kk: (kk, j))
    elif mode == "nt":
        a_spec = pl.BlockSpec((tm, tk), lambda i, j, kk: (i, kk))
        b_spec = pl.BlockSpec((tn, tk), lambda i, j, kk: (j, kk))
    else:
        a_spec = pl.BlockSpec((tk, tm), lambda i, j, kk: (kk, i))
        b_spec = pl.BlockSpec((tk, tn), lambda i, j, kk: (kk, j))
    return pl.pallas_call(
        body, name=name, grid=(m // tm, n // tn, nk),
        in_specs=[a_spec, b_spec] + [pl.BlockSpec((tm, tn), lambda i, j, kk: (i, j))] * len(inits)
        + [pl.BlockSpec(memory_space=pl.ANY)] * len(follow),
        out_specs=pl.BlockSpec((tm, tn), lambda i, j, kk: (i, j)),
        out_shape=jax.ShapeDtypeStruct((m, n), out_dtype),
        scratch_shapes=[pltpu.VMEM((tm, tn), F32)] if use_acc else [],
        compiler_params=_cp("parallel", "parallel", "arbitrary"),
    )(a, b, *inits, *follow)


def _slab_layout(rows):
    offsets, at = [], 0
    for r in rows:
        at = -(-at // r) * r
        offsets.append(at)
        at += r
    return offsets, -(-at // 32) * 32


def _slab_zero_gaps(name, shape, rows, offsets):
    gaps = [(o + r, nxt) for o, r, nxt in zip(offsets, rows, offsets[1:] + [shape[1]]) if nxt > o + r]
    slab = None
    for i, (lo, hi) in enumerate(gaps):
        step = int(np.gcd(lo, hi - lo))

        def body(*refs):
            refs[-1][...] = jnp.zeros_like(refs[-1])

        slab = pl.pallas_call(
            body, name=f"{name}_{i}", grid=(shape[0], (hi - lo) // step), out_shape=jax.ShapeDtypeStruct(shape, F32),
            in_specs=[] if slab is None else [pl.BlockSpec(memory_space=pl.ANY)],
            out_specs=pl.BlockSpec((1, step, shape[2]), functools.partial(lambda k, j, b: (k, b + j, 0), b=lo // step)),
            input_output_aliases={} if slab is None else {0: 0}, compiler_params=_cp("parallel", "parallel"),
        )(*(() if slab is None else (slab,)))
    return slab


def _dw_into_slab(name, a, b, slab, shape, at):
    (k, m), n = a.shape, b.shape[1]
    r = m // N_CHIP
    fits = [(k // tk + 0.5 * (m // tm), tm, tk)
            for tk in (4096, 2048, 1024, 512, 256, 128) if k % tk == 0 for tm in (m, m // 2, r) if tm % LANES == 0
            if 2 * (tk * tm * a.dtype.itemsize + tk * n * b.dtype.itemsize + tm * n * 4) <= MM_VMEM_BUDGET]
    _, tm, tk = min(fits)
    per, nk = tm // r, k // tk

    def body(a_ref, b_ref, *rest):
        o_ref = rest[-1]
        part = _raw_dot("tn", a_ref[...], b_ref[...], False).reshape(o_ref.shape)
        if nk == 1:
            o_ref[...] = part
            return
        kk = pl.program_id(1)

        @pl.when(kk == 0)
        def _():
            o_ref[...] = part

        @pl.when(kk > 0)
        def _():
            o_ref[...] += part

    prev = () if slab is None else (slab,)
    return pl.pallas_call(
        body, name=name, grid=(m // tm, nk), out_shape=jax.ShapeDtypeStruct(shape, F32),
        in_specs=[pl.BlockSpec((tk, tm), lambda i, kk: (kk, i)), pl.BlockSpec((tk, n), lambda i, kk: (kk, 0))]
        + [pl.BlockSpec(memory_space=pl.ANY)] * len(prev),
        out_specs=pl.BlockSpec((per, r, n), lambda i, kk: (i, at // r, 0)),
        input_output_aliases={2: 0} if prev else {}, compiler_params=_cp("parallel", "arbitrary"),
    )(a, b, *prev)


def _rowwise(name, fn, rows, row_ins, full_ins, row_outs, acc_outs, tm=None):
    tm = tm or _pick(rows, (512, 256, 128))
    n_r, n_f, n_o, n_a = len(row_ins), len(full_ins), len(row_outs), len(acc_outs)

    def body(*refs):
        ins, outs = refs[:n_r + n_f], refs[n_r + n_f:]
        vals = [r[...].astype(F32) for r in ins]
        ro, ao = fn(*vals)
        for r, val in zip(outs[:n_o], ro):
            r[...] = val.astype(r.dtype)
        if n_a:
            @pl.when(pl.program_id(0) == 0)
            def _():
                for r in outs[n_o:]:
                    r[...] = jnp.zeros_like(r)

            for r, val in zip(outs[n_o:], ao):
                r[...] += val

    in_specs = [pl.BlockSpec((tm, w), functools.partial(lambda i, cb: (i, cb), cb=cb)) for _, w, cb in row_ins]
    in_specs += [pl.BlockSpec(a.shape, lambda i: (0, 0)) for a in full_ins]
    out_specs = [pl.BlockSpec((tm, w), lambda i: (i, 0)) for w, _ in row_outs]
    out_specs += [pl.BlockSpec(s, lambda i: (0, 0)) for s in acc_outs]
    out_shape = [jax.ShapeDtypeStruct((rows, w), dt) for w, dt in row_outs]
    out_shape += [jax.ShapeDtypeStruct(s, F32) for s in acc_outs]
    return pl.pallas_call(
        body, name=name, grid=(rows // tm,), in_specs=in_specs, out_specs=out_specs, out_shape=out_shape,
        compiler_params=_cp("arbitrary" if n_a else "parallel"),
    )(*[a for a, _, _ in row_ins], *full_ins)


def _rn(x):
    return x * lax.rsqrt(jnp.mean(x * x, axis=-1, keepdims=True) + 1e-6)


def _sigmoid(t):
    return 1.0 / (1.0 + jnp.exp(-t))


def _f_norm_mod(x, g, sh, sc):
    return _rn(x) * g * (1.0 + sc) + sh


def _f_post_res(xr, y, g, gate):
    return xr + gate * (_rn(y) * g)


@jax.custom_vjp
def _f_swiglu(g, u):
    return g * _sigmoid(g) * u


def _f_swiglu_fwd(g, u):
    s = _sigmoid(g)
    return g * s * u, (g, u, s)


def _f_swiglu_bwd(res, da):
    g, u, s = res
    gs = g * s
    return da * u * (s + gs * (1.0 - s)), da * gs


_f_swiglu.defvjp(_f_swiglu_fwd, _f_swiglu_bwd)


def _logsig(u):
    return jnp.minimum(u, 0.0) - jnp.log(1.0 + jnp.exp(-jnp.abs(u)))


def _f_gate(z, wf, wb, bf, bb):
    return _logsig(_nn(z, wf) + bf) / GATE_TAU, _logsig(_nn(z, wb) + bb) / GATE_TAU


def _f_gla_out(of, ob, gg, gt, bd):
    o = of + ob
    ms = _nn_hi(o * o, bd)
    return o * lax.rsqrt(ms + 1e-6) * gt * (gg * _sigmoid(gg))


def _norm_mod(name, x, g, sh, sc):
    rows, d = x.shape
    return _rowwise(name, lambda x, g, sh, sc: ((_f_norm_mod(x, g, sh, sc),), ()), rows,
                    [(x, d, 0)], [g, sh, sc], [(d, BF16)], [])[0]


def _norm_mod_bwd(name, dh, dres, x, g, sh, sc):
    rows, d = x.shape

    def fn(dh, dres, x, g, sh, sc):
        _, vjp = jax.vjp(_f_norm_mod, x, g, sh, sc)
        dx, dg, dsh, dsc = vjp(dh)
        return (dx + dres,), (dg, dsh, dsc)

    return _rowwise(name, fn, rows, [(dh, d, 0), (dres, d, 0), (x, d, 0)], [g, sh, sc], [(d, F32)],
                    [(1, d)] * 3)


def _post_res_norm_mod(name, xr, y, g_post, gate, g_pre, sh, sc):
    rows, d = xr.shape

    def fn(xr, y, g_post, gate, g_pre, sh, sc):
        x1 = _f_post_res(xr, y, g_post, gate)
        return (x1, _f_norm_mod(x1, g_pre, sh, sc)), ()

    return _rowwise(name, fn, rows, [(xr, d, 0), (y, d, 0)], [g_post, gate, g_pre, sh, sc], [(d, F32), (d, BF16)], [])


def _norm_mod_post_res_bwd(name, dh, dres, x1, y, g_pre, sh, sc, g_post, gate):
    rows, d = x1.shape

    def fn(dh, dres, x1, y, g_pre, sh, sc, g_post, gate):
        _, vjp_norm = jax.vjp(_f_norm_mod, x1, g_pre, sh, sc)
        dx1, dg_pre, dsh, dsc = vjp_norm(dh)
        dx1 = dx1 + dres
        _, vjp_res = jax.vjp(lambda y, g, gate: _f_post_res(jnp.zeros_like(y), y, g, gate), y, g_post, gate)
        dy, dg_post, dgate = vjp_res(dx1)
        return (dx1, dy), (dg_pre, dsh, dsc, dg_post, dgate)

    return _rowwise(name, fn, rows, [(dh, d, 0), (dres, d, 0), (x1, d, 0), (y, d, 0)], [g_pre, sh, sc, g_post, gate],
                    [(d, F32), (d, BF16)], [(1, d)] * 5, tm=_pick(rows, (256, 128)))


def _post_res_loss(name, xr, y, g, gate, target):
    rows, d = xr.shape

    def fn(xr, y, target, g, gate):
        x2, vjp = jax.vjp(lambda y, g, gate: _f_post_res(xr, y, g, gate), y, g, gate)
        diff = x2 - target
        part = 0.5 * jnp.sum(jnp.mean(diff * diff, axis=-1, keepdims=True), axis=0, keepdims=True)
        dx2 = diff * (1.0 / d)
        dy, dg, dgate = vjp(dx2)
        return (dx2, dy), (jnp.broadcast_to(part, (1, LANES)), dg, dgate)

    return _rowwise(name, fn, rows, [(xr, d, 0), (y, d, 0), (target, d, 0)], [g, gate], [(d, F32), (d, BF16)],
                    [(1, LANES), (1, d), (1, d)])


def _mm_rows(name, a, b, mode, fn, extras, outs):
    m, k = a.shape
    tm = _pick(m, (256, 128))

    def body(a_ref, b_ref, *rest):
        tiles = fn(_raw_dot(mode, a_ref[...], b_ref[...], False), *[e[...] for e in rest[:len(extras)]])
        for r, val in zip(rest[len(extras):], tiles):
            r[...] = val.astype(r.dtype)

    row = lambda w: pl.BlockSpec((tm, w), lambda i: (i, 0))
    return pl.pallas_call(
        body, name=name, grid=(m // tm,),
        in_specs=[row(k), pl.BlockSpec(b.shape, lambda i: (0, 0))] + [row(e.shape[1]) for e in extras],
        out_specs=[row(w) for w, _ in outs], out_shape=[jax.ShapeDtypeStruct((m, w), dt) for w, dt in outs],
        compiler_params=_cp("parallel"),
    )(a, b, *extras)


def _ffn_in_swiglu(name, h, w_t):
    f = w_t.shape[0] // 2
    fn = lambda u: (u, _f_swiglu(u[:, :f], u[:, f:]))
    return _mm_rows(name, h, w_t, "nt", fn, [], [(2 * f, BF16), (f, BF16)])


def _ffn_out_dx_swiglu_bwd(name, df, w_out, u):
    f = w_out.shape[0]

    def fn(da, u):
        u = u.astype(F32)
        _, vjp = jax.vjp(_f_swiglu, u[:, :f], u[:, f:])
        return (jnp.concatenate(vjp(da), axis=1),)

    return _mm_rows(name, df, w_out, "nt", fn, [u], [(2 * f, BF16)])[0]


def _gate_fwd(name, p, wf, wb, bf, bb):
    rows = p.shape[0]
    return _rowwise(name, lambda z, wf, wb, bf, bb: (_f_gate(z, wf, wb, bf, bb), ()), rows,
                    [(p, LANES, C_Z // LANES)], [wf, wb, bf, bb], [(GKW, F32)] * 2, [])


def _gate_bwd(name, p, dla_f, dla_b, wf, wb, bf, bb):
    rows = p.shape[0]

    def fn(z, dlf, dlb, wf, wb, bf, bb):
        _, vjp = jax.vjp(_f_gate, z, wf, wb, bf, bb)
        dz, dwf, dwb, dbf, dbb = vjp((dlf, dlb))
        return (dz,), (dwf, dwb, dbf, dbb)

    return _rowwise(name, fn, rows, [(p, LANES, C_Z // LANES), (dla_f, GKW, 0), (dla_b, GKW, 0)],
                    [wf, wb, bf, bb], [(LANES, BF16)], [(LANES, GKW), (LANES, GKW), (1, GKW), (1, GKW)])


def _head_mean_matrix():
    h = np.arange(GVW) // GLA_DV
    return jnp.asarray((h[:, None] == h[None, :]).astype(np.float32) / GLA_DV)


def _gla_out(name, attn, of, ob, p, gt):
    rows = of.shape[0]
    bd = _head_mean_matrix()
    fn = lambda attn, of, ob, gg, gt, bd: ((jnp.concatenate([attn, _f_gla_out(of, ob, gg, gt, bd)], axis=1),), ())
    return _rowwise(name, fn, rows, [(attn, QW, 0), (of, GVW, 0), (ob, GVW, 0), (p, GVW, C_GG // GVW)], [gt, bd],
                    [(MIX, BF16)], [])[0]


def _gla_out_bwd(name, dmix, of, ob, p, gt):
    rows = of.shape[0]
    bd = _head_mean_matrix()

    def fn(dm, of, ob, gg, gt, bd):
        _, vjp = jax.vjp(lambda of, gg, gt: _f_gla_out(of, ob, gg, gt, bd), of, gg, gt)
        do, dgg, dgt = vjp(dm)
        return (do, dgg), (dgt,)

    return _rowwise(name, fn, rows, [(dmix, GVW, 1), (of, GVW, 0), (ob, GVW, 0), (p, GVW, C_GG // GVW)], [gt, bd],
                    [(GVW, F32), (GVW, BF16)], [(1, GVW)])


def _rope_tables(n_tokens):
    t = jnp.arange(n_tokens)
    row = (t // GRID_W).astype(F32)
    col = (t % GRID_W).astype(F32)
    half = HEAD_DIM // 2
    inv_freq = ROPE_BASE ** (-jnp.arange(0, half, 2, dtype=F32) / half)
    ang_r = row[:, None] * inv_freq[None, :]
    ang_c = col[:, None] * inv_freq[None, :]
    ang = jnp.concatenate([ang_r, ang_r, ang_c, ang_c], axis=-1)
    sign = jnp.concatenate([-jnp.ones((16,), F32), jnp.ones((16,), F32)] * 2)
    cos, sin = jnp.cos(ang), jnp.sin(ang) * sign[None, :]
    return jnp.tile(cos, (1, 2)), jnp.tile(sin, (1, 2))


def _rot_pairs(x):
    w = x.shape[-1]
    lane = lax.broadcasted_iota(jnp.int32, x.shape, x.ndim - 1)
    return jnp.where((lane % 32) < 16, pltpu.roll(x, w - 16, x.ndim - 1), pltpu.roll(x, 16, x.ndim - 1))


def _rope_apply(x, cos, sin_signed, inverse):
    reps = x.shape[-1] // LANES
    cos = jnp.concatenate([cos] * reps, axis=-1) if reps > 1 else cos
    sin = jnp.concatenate([sin_signed] * reps, axis=-1) if reps > 1 else sin_signed
    if inverse:
        return x * cos + _rot_pairs(x * sin)
    return x * cos + _rot_pairs(x) * sin


def _rope_fwd(name, p, cos, sin):
    rows = p.shape[0]

    def fn(q, k, v, cos, sin):
        return (_rope_apply(q, cos, sin, False), _rope_apply(k, cos, sin, False), v), ()

    return _rowwise(name, fn, rows, [(p, QW, 0), (p, KVW, C_K // KVW), (p, KVW, C_V // KVW), (cos, LANES, 0),
                                     (sin, LANES, 0)], [], [(QW, BF16), (KVW, BF16), (KVW, BF16)], [])


def _proj_grad(name, dq_rot, dk_rot, dv, cos, sin, gla_f, gla_b, dgg, dz):
    rows = dq_rot.shape[0]

    def fn(dq, dk, dv, cos, sin, gqf, gkf, gvf, gqb, gkb, gvb, dgg, dz):
        parts = [_rope_apply(dq, cos, sin, True), gvf + gvb, dgg, _rope_apply(dk, cos, sin, True), dv, gqf + gqb,
                 gkf + gkb, dz]
        return (jnp.concatenate(parts, axis=1),), ()

    ins = [(dq_rot, QW), (dk_rot, KVW), (dv, KVW), (cos, LANES), (sin, LANES)]
    ins += [(t, t.shape[1]) for t in (*gla_f, *gla_b)] + [(dgg, GVW), (dz, LANES)]
    return _rowwise(name, fn, rows, [(t, w, 0) for t, w in ins], [], [(IN_PAD, BF16)], [],
                    tm=_pick(rows, (256, 128)))[0]


GROUP_ROWS = ATT_GROUP * BLOCK


def _f_attn(qs, kws, vws, kcs, vcs, sink, n, n_tokens):
    row = lax.broadcasted_iota(jnp.int32, (GROUP_ROWS, 1), 0)
    group = sum((row >= g * BLOCK).astype(jnp.int32) for g in range(1, ATT_GROUP))
    i = lax.broadcasted_iota(jnp.int32, (GROUP_ROWS, 3 * BLOCK), 0) - BLOCK * group
    j = lax.broadcasted_iota(jnp.int32, (GROUP_ROWS, 3 * BLOCK), 1)
    kpos = (n - 1) * BLOCK + j
    mask = (jnp.abs(j - BLOCK - i) <= WINDOW) & (kpos >= 0) & (kpos < n_tokens)
    head_id = lax.broadcasted_iota(jnp.int32, (1, ATT_HEADS), 1)
    scale = HEAD_DIM ** -0.5
    outs = []
    for h in range(ATT_KV_HEADS):
        sk = jnp.zeros((GROUP_ROWS, 1), F32)
        for g in range(ATT_GROUP):
            one = jnp.sum(jnp.where(head_id == h * ATT_GROUP + g, sink, 0.0), axis=-1, keepdims=True)
            sk = jnp.where(group == g, one, sk)
        q = qs[h] * scale
        s_w = jnp.where(mask, _nt(q, kws[h]), NEG_INF)
        s_c = _nt(q, kcs[h])
        m = lax.stop_gradient(jnp.maximum(jnp.maximum(jnp.max(s_w, axis=-1, keepdims=True),
                                                      jnp.max(s_c, axis=-1, keepdims=True)), sk))
        pw, pc = jnp.exp(s_w - m), jnp.exp(s_c - m)
        den = jnp.sum(pw, axis=-1, keepdims=True) + jnp.sum(pc, axis=-1, keepdims=True) + jnp.exp(sk - m)
        outs.append((_nn(pw, vws[h]) + _nn(pc, vcs[h])) / den)
    return tuple(outs)


def _group_rows(ref, h):
    hs = lambda hq: slice(hq * HEAD_DIM, (hq + 1) * HEAD_DIM)
    return jnp.concatenate([ref[:, hs(h * ATT_GROUP + g)].astype(F32) for g in range(ATT_GROUP)], axis=0)


def _ungroup_rows(ref, h, val):
    for g in range(ATT_GROUP):
        hq = h * ATT_GROUP + g
        ref[:, hq * HEAD_DIM:(hq + 1) * HEAD_DIM] = val[g * BLOCK:(g + 1) * BLOCK].astype(ref.dtype)


def _attn_loads(n, q_ref, kp_ref, vp_ref, kc_ref, vc_ref):
    r0 = pl.multiple_of(n * BLOCK, BLOCK)
    hs = lambda h: slice(h * HEAD_DIM, (h + 1) * HEAD_DIM)
    qs = [_group_rows(q_ref, h) for h in range(ATT_KV_HEADS)]
    kws = [kp_ref[pl.ds(r0, 3 * BLOCK), hs(h)].astype(F32) for h in range(ATT_KV_HEADS)]
    vws = [vp_ref[pl.ds(r0, 3 * BLOCK), hs(h)].astype(F32) for h in range(ATT_KV_HEADS)]
    kcs = [kc_ref[:, hs(h)].astype(F32) for h in range(ATT_KV_HEADS)]
    vcs = [vc_ref[:, hs(h)].astype(F32) for h in range(ATT_KV_HEADS)]
    return r0, hs, qs, kws, vws, kcs, vcs


def _attn_specs(s, c):
    full = lambda shape: pl.BlockSpec(shape, lambda n: (0, 0))
    return [pl.BlockSpec((BLOCK, QW), lambda n: (n, 0)), full((s + 2 * BLOCK, KVW)), full((s + 2 * BLOCK, KVW)),
            full((c, KVW)), full((c, KVW)), full((1, ATT_HEADS))]


def _attn_fwd(q, kp, vp, kc, vc, sink):
    s, c = q.shape[0], kc.shape[0]

    def body(q_ref, kp_ref, vp_ref, kc_ref, vc_ref, sink_ref, o_ref):
        n = pl.program_id(0)
        _, hs, qs, kws, vws, kcs, vcs = _attn_loads(n, q_ref, kp_ref, vp_ref, kc_ref, vc_ref)
        outs = _f_attn(qs, kws, vws, kcs, vcs, sink_ref[...], n, s)
        for h in range(ATT_KV_HEADS):
            _ungroup_rows(o_ref, h, outs[h])

    return pl.pallas_call(
        body, name="attn_fwd", grid=(s // BLOCK,), in_specs=_attn_specs(s, c),
        out_specs=pl.BlockSpec((BLOCK, QW), lambda n: (n, 0)), out_shape=jax.ShapeDtypeStruct((s, QW), BF16),
        compiler_params=_cp("parallel"),
    )(q, kp, vp, kc, vc, sink)


def _attn_bwd(do, q, kp, vp, kc, vc, sink):
    s, c = q.shape[0], kc.shape[0]

    def body(do_ref, q_ref, kp_ref, vp_ref, kc_ref, vc_ref, sink_ref, dq_ref, dkp_ref, dvp_ref, dkc_ref, dvc_ref,
             dsink_ref):
        n = pl.program_id(0)

        @pl.when(n == 0)
        def _():
            for r in (dkp_ref, dvp_ref, dkc_ref, dvc_ref, dsink_ref):
                r[...] = jnp.zeros_like(r)

        r0, hs, qs, kws, vws, kcs, vcs = _attn_loads(n, q_ref, kp_ref, vp_ref, kc_ref, vc_ref)
        _, vjp = jax.vjp(lambda qs, kws, vws, kcs, vcs, sink: _f_attn(qs, kws, vws, kcs, vcs, sink, n, s),
                         qs, kws, vws, kcs, vcs, sink_ref[...])
        dqs, dkws, dvws, dkcs, dvcs, dsink = vjp(tuple(_group_rows(do_ref, h) for h in range(ATT_KV_HEADS)))
        for h in range(ATT_KV_HEADS):
            _ungroup_rows(dq_ref, h, dqs[h])
            dkp_ref[pl.ds(r0, 3 * BLOCK), hs(h)] += dkws[h]
            dvp_ref[pl.ds(r0, 3 * BLOCK), hs(h)] += dvws[h]
            dkc_ref[:, hs(h)] += dkcs[h]
            dvc_ref[:, hs(h)] += dvcs[h]
        dsink_ref[...] += dsink

    full = lambda shape: pl.BlockSpec(shape, lambda n: (0, 0))
    return pl.pallas_call(
        body, name="attn_bwd", grid=(s // BLOCK,),
        in_specs=[pl.BlockSpec((BLOCK, QW), lambda n: (n, 0))] + _attn_specs(s, c),
        out_specs=[pl.BlockSpec((BLOCK, QW), lambda n: (n, 0)), full((s + 2 * BLOCK, KVW)), full((s + 2 * BLOCK, KVW)),
                   full((c, KVW)), full((c, KVW)), full((1, ATT_HEADS))],
        out_shape=[jax.ShapeDtypeStruct((s, QW), F32), jax.ShapeDtypeStruct((s + 2 * BLOCK, KVW), F32),
                   jax.ShapeDtypeStruct((s + 2 * BLOCK, KVW), F32), jax.ShapeDtypeStruct((c, KVW), F32),
                   jax.ShapeDtypeStruct((c, KVW), F32), jax.ShapeDtypeStruct((1, ATT_HEADS), F32)],
        compiler_params=_cp("arbitrary"),
    )(do, q, kp, vp, kc, vc, sink)


GLA_GROUPS = 1
GLA_GROUP_HEADS = GLA_HEADS // GLA_GROUPS
GKG, GVG = GKW // GLA_GROUPS, GVW // GLA_GROUPS


def _gla_masks(heads=GLA_HEADS):
    hk = np.arange(heads * GLA_DK) // GLA_DK
    hv = np.arange(heads * GLA_DV) // GLA_DV
    head_k = (np.arange(heads)[:, None] == hk[None, :]).astype(np.float32)
    head_v = (np.arange(heads)[:, None] == hv[None, :]).astype(np.float32)
    bd_t = (hv[:, None] == hk[None, :]).astype(np.float32)
    return jnp.asarray(head_k), jnp.asarray(head_v), jnp.asarray(bd_t)


def _group_states(st):
    return jnp.stack([st[g * GVG:(g + 1) * GVG, g * GKG:(g + 1) * GKG] for g in range(GLA_GROUPS)])


def _ungroup_states(st):
    out = jnp.zeros((GVW, GKW), st.dtype)
    for g in range(GLA_GROUPS):
        out = out.at[g * GVG:(g + 1) * GVG, g * GKG:(g + 1) * GKG].set(st[g])
    return out


def _tri(n, rev, strict=False):
    i = lax.broadcasted_iota(jnp.int32, (n, n), 0)
    j = lax.broadcasted_iota(jnp.int32, (n, n), 1)
    if strict:
        keep = (j > i) if rev else (j < i)
    else:
        keep = (j >= i) if rev else (j <= i)
    return keep


def _f_gla_chunk(q, k, v, la, st, head_k, head_v, bd_t, rev):
    return _f_gla_carry(*_f_gla_intra(q, k, v, la, head_k, head_v, rev), v, st, bd_t)


def _f_gla_intra(q, k, v, la, head_k, head_v, rev):
    heads, kw, vw = head_k.shape[0], q.shape[1], v.shape[1]
    keep = _tri(GLA_CHUNK, rev)
    b = _nn_hi(keep.astype(F32), la)
    bl = jnp.sum(la, axis=0, keepdims=True)
    qd = q * (GLA_DK ** -0.5) * jnp.exp(b)
    ki = k * jnp.exp(-b)
    kd = k * jnp.exp(bl - b)
    q_heads = (qd[None, :, :] * head_k[:, None, :]).reshape(heads * GLA_CHUNK, kw)
    a_all = _nt(q_heads, ki).reshape(heads, GLA_CHUNK, GLA_CHUNK)
    a_all = jnp.where(keep[None, :, :], a_all, 0.0).reshape(heads * GLA_CHUNK, GLA_CHUNK)
    o_all = _nn(a_all, v).reshape(heads, GLA_CHUNK, vw)
    return jnp.sum(o_all * head_v[:, None, :], axis=0), qd, kd, bl


def _f_gla_carry(intra, qd, kd, bl, v, st, bd_t):
    return intra + _nt(qd, st), st * jnp.exp(bl) + bd_t * _tn(v, kd)


def _gla_specs(s, tb, order):
    return [pl.BlockSpec((tb, GKW), lambda i: (order(i), C_GQ // GKW)),
            pl.BlockSpec((tb, GKW), lambda i: (order(i), C_GK // GKW)),
            pl.BlockSpec((tb, GVW), lambda i: (order(i), C_GV // GVW)),
            pl.BlockSpec((tb, GKW), lambda i: (order(i), 0))]


GLA_BLOCK_CHUNKS = 4


def _gla_fwd(p, la_f, la_b, st_f0, st_b0):
    s = p.shape[0]
    tb = GLA_BLOCK_CHUNKS * GLA_CHUNK
    nblk = s // tb
    up, down = (lambda i: i), (lambda i: nblk - 1 - i)
    masks = _gla_masks(GLA_GROUP_HEADS)

    def scan(rev, q_ref, k_ref, v_ref, la_ref, o_ref, sts_ref, st_ref, consts):
        for g in range(GLA_GROUPS):
            gk, gv = slice(g * GKG, (g + 1) * GKG), slice(g * GVG, (g + 1) * GVG)
            st = st_ref[g]
            sts_ref[0, g] = st
            chunks = range(GLA_BLOCK_CHUNKS)
            for ci in (reversed(chunks) if rev else chunks):
                rows = slice(ci * GLA_CHUNK, (ci + 1) * GLA_CHUNK)
                o, st = _f_gla_chunk(q_ref[rows, gk], k_ref[rows, gk], v_ref[rows, gv], la_ref[rows, gk], st, *consts,
                                     rev)
                o_ref[rows, gv] = o
            st_ref[g] = st

    def body(qf, kf, vf, laf, qb, kb, vb, lab, stf0, stb0, hk_ref, hv_ref, bd_ref, of_ref, stsf_ref, ob_ref, stsb_ref,
             stf_ref, stb_ref):
        @pl.when(pl.program_id(0) == 0)
        def _():
            stf_ref[...] = stf0[...]
            stb_ref[...] = stb0[...]

        consts = (hk_ref[...], hv_ref[...], bd_ref[...])
        scan(False, qf, kf, vf, laf, of_ref, stsf_ref, stf_ref, consts)
        scan(True, qb, kb, vb, lab, ob_ref, stsb_ref, stb_ref, consts)

    full = lambda a: pl.BlockSpec(a.shape, lambda i: (0,) * a.ndim)
    outs = lambda order: [pl.BlockSpec((tb, GVW), lambda i: (order(i), 0)),
                          pl.BlockSpec((1, GLA_GROUPS, GVG, GKG), lambda i: (order(i), 0, 0, 0))]
    return pl.pallas_call(
        body, name="gla_fwd", grid=(nblk,),
        in_specs=_gla_specs(s, tb, up) + _gla_specs(s, tb, down) + [full(st_f0), full(st_b0)]
        + [full(m) for m in masks],
        out_specs=outs(up) + outs(down),
        out_shape=[jax.ShapeDtypeStruct((s, GVW), F32), jax.ShapeDtypeStruct((nblk, GLA_GROUPS, GVG, GKG), F32)] * 2,
        scratch_shapes=[pltpu.VMEM((GLA_GROUPS, GVG, GKG), F32)] * 2,
        compiler_params=_cp("arbitrary"),
    )(p, p, p, la_f, p, p, p, la_b, st_f0, st_b0, *masks)


def _gla_bwd(p, la_f, la_b, sts_f, sts_b, do, after=None):
    s = p.shape[0]
    tb = GLA_BLOCK_CHUNKS * GLA_CHUNK
    nblk = s // tb
    up, down = (lambda i: i), (lambda i: nblk - 1 - i)
    masks = _gla_masks(GLA_GROUP_HEADS)
    follow = () if after is None else (after,)

    def back(rev, q_ref, k_ref, v_ref, la_ref, sts_ref, do_ref, dq_ref, dk_ref, dv_ref, dla_ref, dst0_ref, dst_ref,
             consts):
        def block(q, k, v, la, st):
            outs = [None] * GLA_BLOCK_CHUNKS
            chunks = range(GLA_BLOCK_CHUNKS)
            for ci in (reversed(chunks) if rev else chunks):
                outs[ci], st = _f_gla_chunk(q[ci], k[ci], v[ci], la[ci], st, *consts, rev)
            return tuple(outs), st

        for g in range(GLA_GROUPS):
            gk, gv = slice(g * GKG, (g + 1) * GKG), slice(g * GVG, (g + 1) * GVG)
            split = lambda r, cols: tuple(r[ci * GLA_CHUNK:(ci + 1) * GLA_CHUNK, cols].astype(F32)
                                          for ci in range(GLA_BLOCK_CHUNKS))
            _, vjp = jax.vjp(block, split(q_ref, gk), split(k_ref, gk), split(v_ref, gv), split(la_ref, gk),
                             sts_ref[0, g])
            dq, dk, dv, dla, dst = vjp((split(do_ref, gv), dst_ref[g]))
            for ci in range(GLA_BLOCK_CHUNKS):
                rows = slice(ci * GLA_CHUNK, (ci + 1) * GLA_CHUNK)
                dq_ref[rows, gk], dk_ref[rows, gk], dv_ref[rows, gv], dla_ref[rows, gk] = dq[ci], dk[ci], dv[ci], dla[ci]
            dst_ref[g] = dst
            dst0_ref[g] = dst

    def body(*refs):
        ins, (hk_ref, hv_ref, bd_ref) = refs[:12], refs[12:15]
        outs = refs[15 + len(follow):]

        @pl.when(pl.program_id(0) == 0)
        def _():
            outs[10][...] = jnp.zeros_like(outs[10])
            outs[11][...] = jnp.zeros_like(outs[11])

        consts = (hk_ref[...], hv_ref[...], bd_ref[...])
        back(False, *ins[:6], *outs[:5], outs[10], consts)
        back(True, *ins[6:], *outs[5:10], outs[11], consts)

    full = lambda a: pl.BlockSpec(a.shape, lambda i: (0,) * a.ndim)

    def ins(order):
        return _gla_specs(s, tb, order) + [pl.BlockSpec((1, GLA_GROUPS, GVG, GKG), lambda i: (order(i), 0, 0, 0)),
                                           pl.BlockSpec((tb, GVW), lambda i: (order(i), 0))]

    def outs(order):
        blk = lambda w: pl.BlockSpec((tb, w), lambda i: (order(i), 0))
        return [blk(GKW), blk(GKW), blk(GVW), blk(GKW), pl.BlockSpec((GLA_GROUPS, GVG, GKG), lambda i: (0, 0, 0))]

    shapes = [jax.ShapeDtypeStruct((s, GKW), F32), jax.ShapeDtypeStruct((s, GKW), F32),
              jax.ShapeDtypeStruct((s, GVW), F32), jax.ShapeDtypeStruct((s, GKW), F32),
              jax.ShapeDtypeStruct((GLA_GROUPS, GVG, GKG), F32)]
    both = pl.pallas_call(
        body, name="gla_bwd", grid=(nblk,),
        in_specs=ins(down) + ins(up) + [full(m) for m in masks] + [pl.BlockSpec(memory_space=pl.ANY)] * len(follow),
        out_specs=outs(down) + outs(up), out_shape=shapes * 2,
        scratch_shapes=[pltpu.VMEM((GLA_GROUPS, GVG, GKG), F32)] * 2,
        compiler_params=_cp("arbitrary"),
    )(p, p, p, la_f, sts_f, do, p, p, p, la_b, sts_b, do, *masks, *follow)
    return both[:5], both[5:]


def _f_ctx_state(k, v, la_f, la_b, bd_t):
    c = k.shape[0]
    after = _nn_hi(_tri(c, True, strict=True).astype(F32), la_f)
    before = _nn_hi(_tri(c, False, strict=True).astype(F32), la_b)
    return bd_t * _tn(v, k * jnp.exp(after)), bd_t * _tn(v, k * jnp.exp(before))


def _ctx_state(pc, la_f, la_b):
    c = pc.shape[0]
    bd_t = _gla_masks()[2]

    def body(k_ref, v_ref, lf_ref, lb_ref, bd_ref, sf_ref, sb_ref):
        sf_ref[...], sb_ref[...] = _f_ctx_state(k_ref[...], v_ref[...], lf_ref[...], lb_ref[...], bd_ref[...])

    full = lambda a: pl.BlockSpec(a.shape, lambda i: (0, 0))
    return pl.pallas_call(
        body, name="ctx_state_fwd", grid=(1,),
        in_specs=[pl.BlockSpec((c, GKW), lambda i: (0, C_GK // GKW)), pl.BlockSpec((c, GVW), lambda i: (0, C_GV // GVW)),
                  full(la_f), full(la_b), full(bd_t)],
        out_specs=[pl.BlockSpec((GVW, GKW), lambda i: (0, 0))] * 2,
        out_shape=[jax.ShapeDtypeStruct((GVW, GKW), F32)] * 2,
        compiler_params=_cp("arbitrary"),
    )(pc, pc, la_f, la_b, bd_t)


def _ctx_state_bwd(pc, la_f, la_b, dsf, dsb):
    c = pc.shape[0]
    bd_t = _gla_masks()[2]

    def body(k_ref, v_ref, lf_ref, lb_ref, bd_ref, dsf_ref, dsb_ref, dk_ref, dv_ref, dlf_ref, dlb_ref):
        _, vjp = jax.vjp(lambda k, v, lf, lb: _f_ctx_state(k, v, lf, lb, bd_ref[...]),
                         k_ref[...], v_ref[...], lf_ref[...], lb_ref[...])
        dk, dv, dlf, dlb = vjp((dsf_ref[...], dsb_ref[...]))
        dk_ref[...], dv_ref[...] = dk.astype(BF16), dv.astype(BF16)
        dlf_ref[...], dlb_ref[...] = dlf, dlb

    full = lambda a: pl.BlockSpec(a.shape, lambda i: (0, 0))
    return pl.pallas_call(
        body, name="ctx_state_bwd", grid=(1,),
        in_specs=[pl.BlockSpec((c, GKW), lambda i: (0, C_GK // GKW)), pl.BlockSpec((c, GVW), lambda i: (0, C_GV // GVW)),
                  full(la_f), full(la_b), full(bd_t), full(dsf), full(dsb)],
        out_specs=[pl.BlockSpec((c, GKW), lambda i: (0, 0)), pl.BlockSpec((c, GVW), lambda i: (0, 0)),
                   pl.BlockSpec((c, GKW), lambda i: (0, 0)), pl.BlockSpec((c, GKW), lambda i: (0, 0))],
        out_shape=[jax.ShapeDtypeStruct((c, GKW), BF16), jax.ShapeDtypeStruct((c, GVW), BF16),
                   jax.ShapeDtypeStruct((c, GKW), F32), jax.ShapeDtypeStruct((c, GKW), F32)],
        compiler_params=_cp("arbitrary"),
    )(pc, pc, la_f, la_b, bd_t, dsf, dsb)


_SRC_COLS = ((0, QW), (QW + 2 * KVW + 2 * GKW, GVW), (QW + 2 * KVW + 2 * GKW + GVW, GVW), (QW, KVW), (QW + KVW, KVW),
             (QW + 2 * KVW, GKW), (QW + 2 * KVW + GKW, GKW), (IN_COLS - 2 * GATE_RANK, 2 * GATE_RANK))
_DST_COLS = (C_Q, C_GV, C_GG, C_K, C_V, C_GQ, C_GK, C_Z)


def _pack_w_in(w_in):
    parts = [w_in[:, s:s + n] for s, n in _SRC_COLS]
    parts.append(jnp.zeros((w_in.shape[0], IN_PAD - C_Z - 2 * GATE_RANK), w_in.dtype))
    return jnp.concatenate(parts, axis=1)


def _unpack_w_in_grad(g):
    by_src = sorted(zip(_SRC_COLS, _DST_COLS))
    return jnp.concatenate([g[:, d:d + n] for (_, n), d in by_src], axis=1)


def _prep_gate_weights(w_gate_fwd, w_gate_bwd):
    pad_rows = lambda w, at: jnp.zeros((LANES, GKW), F32).at[at:at + GATE_RANK].set(w)
    return {"wg_f": pad_rows(w_gate_fwd, 0), "wg_b": pad_rows(w_gate_bwd, GATE_RANK)}


def _local_step(x, ctx, target, ada, ada_c, w, late_weights, reduce_behind=None, reduce_w_in=None):
    s, d = x.shape
    sh1, sc1, gt1, sh2, sc2, gt2 = [ada[:, i * d:(i + 1) * d] for i in range(6)]
    sh1c, sc1c = ada_c[:, :d], ada_c[:, d:2 * d]
    cos, sin = _rope_tables(s)
    gt = jnp.tile(w["g_gla_norm"], (1, GLA_HEADS))

    h = _norm_mod("pre_mix", x, w["g_pre_mix"], sh1, sc1)
    hc = _norm_mod("pre_mix_ctx", ctx, w["g_pre_mix"], sh1c, sc1c)
    w_in, token = w["w_in"](h, cos, sin)
    p = _mm("proj_in", h, w_in, "nn", after=token)
    pc = _mm("proj_in_ctx", hc, w_in, "nn")
    q_rot, k_rot, v_b = _rope_fwd("rope", p, cos, sin)
    pad = ((BLOCK, BLOCK), (0, 0))
    kp, vp = jnp.pad(k_rot, pad), jnp.pad(v_b, pad)
    kc, vc = pc[:, C_K:C_K + KVW].astype(BF16), pc[:, C_V:C_V + KVW].astype(BF16)
    attn = _attn_fwd(q_rot, kp, vp, kc, vc, w["attn_sink"])
    gate_w = (w["wg_f"], w["wg_b"], w["b_gate_fwd"], w["b_gate_bwd"])
    la_f, la_b = _gate_fwd("gate", p, *gate_w)
    la_fc, la_bc = _gate_fwd("gate_ctx", pc, *gate_w)
    st_f0, st_b0 = _ctx_state(pc, la_fc, la_bc)
    o_f, sts_f, o_b, sts_b = _gla_fwd(p, la_f, la_b, _group_states(st_f0), _group_states(st_b0))
    mix = _gla_out("gla_out", attn, o_f, o_b, p, gt)
    w_out, w_ffn_in_t, w_ffn_out = late_weights(attn)
    y = _mm("proj_out", mix, w_out, "nn", BF16)
    x1, h2 = _post_res_norm_mod("post_mix_pre_ffn", x, y, w["g_post_mix"], gt1, w["g_pre_ffn"], sh2, sc2)
    u, a = _ffn_in_swiglu("ffn_in", h2, w_ffn_in_t)
    f = _mm("ffn_out", a, w_ffn_out, "nn", BF16)
    g = {}
    dx2, df, loss, g["g_post_ffn"], dgt2 = _post_res_loss("post_ffn_loss", x1, f, w["g_post_ffn"], gt2, target)

    late_rows = {"w_ffn_in_t": w_ffn_in_t.shape[0] // N_CHIP, "w_ffn_out": w_ffn_out.shape[0] // N_CHIP,
                 "w_out": w_out.shape[0] // N_CHIP}
    order = sorted(late_rows, key=lambda n: -late_rows[n])
    offsets, slab_rows = _slab_layout([late_rows[n] for n in order])
    late_at, slab_shape = dict(zip(order, offsets)), (N_CHIP, slab_rows, d)
    slab = _slab_zero_gaps("late_grads_gaps", slab_shape, [late_rows[n] for n in order], offsets)
    slab = _dw_into_slab("ffn_out_dw", a, df, slab, slab_shape, late_at["w_ffn_out"])
    du = _ffn_out_dx_swiglu_bwd("ffn_out_dx", df, w_ffn_out, u)
    dh2 = _mm("ffn_in_dx", du, w_ffn_in_t, "nn", BF16)
    slab = _dw_into_slab("ffn_in_dw", du, h2, slab, slab_shape, late_at["w_ffn_in_t"])
    dx1, dy, g["g_pre_ffn"], dsh2, dsc2, g["g_post_mix"], dgt1 = _norm_mod_post_res_bwd(
        "pre_ffn_post_mix_bwd", dh2, dx2, x1, y, w["g_pre_ffn"], sh2, sc2, w["g_post_mix"], gt1)
    dmix = _mm("proj_out_dx", dy, w_out, "nt", BF16)
    slab = _dw_into_slab("proj_out_dw", mix, dy, slab, slab_shape, late_at["w_out"])
    g["late"], g["late_at"], g["late_rows"] = slab, late_at, late_rows
    rb, sink, token = reduce_behind, w["attn_sink"], None
    if rb is not None:
        gt = _behind(gt, rb.start_slab(slab))
    d_o, dgg, dgt = _gla_out_bwd("gla_out_bwd", dmix, o_f, o_b, p, gt)
    g["g_gla_norm"] = jnp.sum(dgt.reshape(GLA_HEADS, GLA_DV), axis=0, keepdims=True)
    if rb is not None:
        token = rb.pair(dgg)
    gla_f, gla_b = _gla_bwd(p, la_f, la_b, sts_f, sts_b, d_o, token)
    (dla_f, dst_f0), (dla_b, dst_b0) = gla_f[3:], gla_b[3:]
    dst_f0, dst_b0 = _ungroup_states(dst_f0), _ungroup_states(dst_b0)
    if rb is not None:
        sink = _behind(sink, rb.total(dla_b))
    dgkc, dgvc, dla_fc, dla_bc = _ctx_state_bwd(pc, la_fc, la_bc, dst_f0, dst_b0)
    dz, dwf, dwb, dbf, dbb = _gate_bwd("gate_bwd", p, dla_f, dla_b, *gate_w)
    dzc, dwfc, dwbc, dbfc, dbbc = _gate_bwd("gate_ctx_bwd", pc, dla_fc, dla_bc, *gate_w)
    g["w_gate_fwd"] = (dwf + dwfc)[:GATE_RANK]
    g["w_gate_bwd"] = (dwb + dwbc)[GATE_RANK:2 * GATE_RANK]
    g["b_gate_fwd"], g["b_gate_bwd"] = dbf + dbfc, dbb + dbbc
    dq_rot, dkp, dvp, dkc, dvc, g["attn_sink"] = _attn_bwd(dmix, q_rot, kp, vp, kc, vc, sink)
    if rb is not None:
        g["late"] = rb.result(dq_rot)
    dp = _proj_grad("proj_grad", dq_rot, dkp[BLOCK:BLOCK + s], dvp[BLOCK:BLOCK + s], cos, sin, gla_f[:3], gla_b[:3],
                    dgg, dz)
    c_rows = ctx.shape[0]
    zeros = lambda n: jnp.zeros((c_rows, n), BF16)
    dpc = jnp.concatenate([zeros(QW), dgvc, zeros(GVW), dkc.astype(BF16), dvc.astype(BF16), zeros(GKW), dgkc, dzc],
                          axis=1)
    g["w_in"] = _mm("proj_in_dw", h, dp, "tn", init=_mm("proj_in_ctx_dw", hc, dpc, "tn"))
    token = None if reduce_w_in is None else reduce_w_in.start(g["w_in"])
    dh = _mm("proj_in_dx", dp, w_in, "nt", BF16, after=token)
    dhc = _mm("proj_in_ctx_dx", dpc, w_in, "nt")
    if reduce_w_in is not None:
        sh1 = _behind(sh1, reduce_w_in.pair(dh))
    dx, dg_a, dsh1, dsc1 = _norm_mod_bwd("pre_mix_bwd", dh, dx1, x, w["g_pre_mix"], sh1, sc1)
    if reduce_w_in is not None:
        dsh1 = _behind(dsh1, reduce_w_in.total(dx))
    _, dg_b, dsh1c, dsc1c = _norm_mod_bwd("pre_mix_ctx_bwd", dhc, jnp.zeros_like(dhc), ctx, w["g_pre_mix"], sh1c,
                                          sc1c)
    g["g_pre_mix"] = dg_a + dg_b
    d_ada = jnp.concatenate([dsh1, dsc1, dgt1, dsh2, dsc2, dgt2], axis=1)
    d_ada_c = jnp.concatenate([dsh1c, dsc1c, jnp.zeros((1, 4 * d), F32)], axis=1)
    return loss, dx, g, d_ada, d_ada_c


HBM = pl.BlockSpec(memory_space=pltpu.HBM)
N_DEV, N_CHIP = 8, 4


def _place():
    x, y, c = lax.axis_index("x"), lax.axis_index("y"), lax.axis_index("c")
    return x, y, c, [(1 - x, y), (x, 1 - y), (1 - x, 1 - y)]


def _row_tile(n, mult, cap):
    return max(t for t in range(mult, min(n, cap) + 1, mult) if n % t == 0)


def _ag_small(name, v, after=None):
    follow = () if after is None else (after,)

    def body(v_ref, *rest):
        out_ref, send_sems, recv_sems = rest[len(follow):]
        x, y, c, _ = _place()
        out_ref[4 * x + 2 * y + c] = v_ref[...]

        def peer(r):
            return ((1 - x) if r & 4 else x, (1 - y) if r & 2 else y, (1 - c) if r & 1 else c)

        def copy(r, block):
            px, py, pc = block
            return pltpu.make_async_remote_copy(
                src_ref=v_ref, dst_ref=out_ref.at[4 * px + 2 * py + pc], send_sem=send_sems.at[r - 1],
                recv_sem=recv_sems.at[r - 1], device_id=peer(r), device_id_type=MESH)

        sends = [copy(r, (x, y, c)) for r in range(1, N_DEV)]
        for cp in sends:
            cp.start()
        for r in range(1, N_DEV):
            copy(r, peer(r)).wait_recv()
        for cp in sends:
            cp.wait_send()

    return pl.pallas_call(
        body, name=name, out_shape=jax.ShapeDtypeStruct((N_DEV,) + v.shape, v.dtype),
        in_specs=[pl.BlockSpec(memory_space=pltpu.VMEM)] + [pl.BlockSpec(memory_space=pl.ANY)] * len(follow),
        out_specs=pl.BlockSpec(memory_space=pltpu.VMEM),
        scratch_shapes=[pltpu.SemaphoreType.DMA((N_DEV - 1,)), pltpu.SemaphoreType.DMA((N_DEV - 1,))],
    )(v, *follow)


def _halves(c, rows, mult):
    hr = rows // 2
    return pl.ds(pl.multiple_of(c * hr, mult), hr), pl.ds(pl.multiple_of((1 - c) * hr, mult), hr)


def _add_half(name, g, a, c_idx):
    n_sh, hr, n = a.shape
    tr = _row_tile(hr, 16, 1024)
    nb = hr // tr

    def body(c_ref, g_ref, a_ref, o_ref):
        o_ref[...] = (g_ref[...] + a_ref[...]).astype(o_ref.dtype)

    return pl.pallas_call(
        body, name=name, out_shape=jax.ShapeDtypeStruct(a.shape, BF16),
        grid_spec=pltpu.PrefetchScalarGridSpec(
            num_scalar_prefetch=1, grid=(n_sh, nb),
            in_specs=[pl.BlockSpec((1, tr, n), lambda s, i, c_ref: (s, c_ref[0] * nb + i, 0)),
                      pl.BlockSpec((1, tr, n), lambda s, i, c_ref: (s, i, 0))],
            out_specs=pl.BlockSpec((1, tr, n), lambda s, i, c_ref: (s, i, 0))),
        compiler_params=_cp("parallel", "parallel"),
    )(c_idx, g, a)


def _sum_chips(name, b, c_idx):
    n_sh, hr, n = b.shape
    tr = _row_tile(hr, 16, 1024)
    nb = hr // tr

    def body(c_ref, b0, b1, b2, b3, o_ref):
        o_ref[...] = ((b0[0].astype(F32) + b1[0].astype(F32)) + b2[0].astype(F32)) + b3[0].astype(F32)

    return pl.pallas_call(
        body, name=name, out_shape=jax.ShapeDtypeStruct((2 * hr, n), F32),
        grid_spec=pltpu.PrefetchScalarGridSpec(
            num_scalar_prefetch=1, grid=(nb,),
            in_specs=[pl.BlockSpec((1, tr, n), functools.partial(lambda i, c_ref, k: (k, i, 0), k=k))
                      for k in range(n_sh)],
            out_specs=pl.BlockSpec((tr, n), lambda i, c_ref: (c_ref[0] * nb + i, 0))),
        compiler_params=_cp("parallel"),
    )(c_idx, b, b, b, b)


SEM = pl.BlockSpec(memory_space=pltpu.SEMAPHORE)
ANY = pl.BlockSpec(memory_space=pl.ANY)
DATAFLOW = pltpu.SideEffectType.DATAFLOW_SIDE_EFFECTING


def _remote(src, dst, send_sems, recv_sems, k, to):
    return pltpu.make_async_remote_copy(src_ref=src, dst_ref=dst, send_sem=send_sems.at[k], recv_sem=recv_sems.at[k],
                                        device_id=to, device_id_type=MESH)


def _split_copy(name, src, land_shape, land_dtype, n, plan, after=None):
    after = jnp.zeros((8, LANES), F32) if after is None else after

    def start_body(src_ref, land_ref, after_ref, send_sems, recv_sems, src_thru, land_thru, token):
        for cp in plan(src_ref, land_ref, send_sems, recv_sems)[0]:
            cp.start()
        token[...] = jnp.zeros_like(token)

    sems = pltpu.SemaphoreType.DMA((n,))
    send_sems, recv_sems, src_thru, land_thru, token = pl.pallas_call(
        start_body, name=name + "_start",
        out_shape=(sems, sems, pltpu.HBM(src.shape, src.dtype), pltpu.HBM(land_shape, land_dtype),
                   jax.ShapeDtypeStruct((8, LANES), F32)),
        in_specs=(HBM, HBM, ANY), out_specs=(SEM, SEM, HBM, HBM, pl.BlockSpec(memory_space=pltpu.VMEM)),
        input_output_aliases={0: 2, 1: 3}, compiler_params=pltpu.CompilerParams(has_side_effects=DATAFLOW),
    )(pltpu.with_memory_space_constraint(src, pltpu.HBM),
      pltpu.with_memory_space_constraint(lax.empty(land_shape, land_dtype), pltpu.HBM), after)

    def wait(*after):
        def wait_body(src_ref, land_ref, send_sems, recv_sems, *rest):
            sent, received = plan(src_ref, land_ref, send_sems, recv_sems)
            for cp in sent:
                cp.wait_send()
            for cp in received:
                cp.wait_recv()

        return pl.pallas_call(
            wait_body, name=name + "_wait",
            out_shape=(pltpu.HBM(src.shape, src.dtype), pltpu.HBM(land_shape, land_dtype)),
            in_specs=(HBM, HBM, SEM, SEM) + (ANY,) * len(after), out_specs=(HBM, HBM),
            input_output_aliases={0: 0, 1: 1}, compiler_params=pltpu.CompilerParams(has_side_effects=DATAFLOW),
        )(src_thru, land_thru, send_sems, recv_sems, *after)

    return token, wait


def _split_gather(name, shards, after):
    k, n, plan = len(shards), 3 * len(shards), _plan_gather

    def start_body(*refs):
        for cp in plan(refs[:k], refs[k:2 * k], refs[2 * k + 1], refs[2 * k + 2])[0]:
            cp.start()
        refs[-1][...] = jnp.zeros_like(refs[-1])

    sems = pltpu.SemaphoreType.DMA((n,))
    bufs = [pltpu.HBM(s.shape, s.dtype) for s in shards] + [pltpu.HBM((N_CHIP,) + s.shape, s.dtype) for s in shards]
    hbm = lambda t: pltpu.with_memory_space_constraint(t, pltpu.HBM)
    outs = pl.pallas_call(
        start_body, name=name + "_start", out_shape=(sems, sems, *bufs, jax.ShapeDtypeStruct((8, LANES), F32)),
        in_specs=(HBM,) * (2 * k) + (ANY,),
        out_specs=(SEM, SEM) + (HBM,) * (2 * k) + (pl.BlockSpec(memory_space=pltpu.VMEM),),
        input_output_aliases={i: 2 + i for i in range(2 * k)},
        compiler_params=pltpu.CompilerParams(has_side_effects=DATAFLOW),
    )(*[hbm(s) for s in shards], *[hbm(lax.empty((N_CHIP,) + s.shape, s.dtype)) for s in shards], after)
    send_sems, recv_sems, thru, token = outs[0], outs[1], outs[2:2 + 2 * k], outs[-1]

    def wait(*after):
        def wait_body(*refs):
            sent, received = plan(refs[:k], refs[k:2 * k], refs[2 * k], refs[2 * k + 1])
            for cp in sent:
                cp.wait_send()
            for cp in received:
                cp.wait_recv()

        res = pl.pallas_call(
            wait_body, name=name + "_wait", out_shape=tuple(bufs),
            in_specs=(HBM,) * (2 * k) + (SEM, SEM) + (ANY,) * len(after), out_specs=(HBM,) * (2 * k),
            input_output_aliases={i: i for i in range(2 * k)},
            compiler_params=pltpu.CompilerParams(has_side_effects=DATAFLOW),
        )(*thru, send_sems, recv_sems, *after)
        return res[:k], res[k:]

    return token, wait


def _behind(x, token):
    return x + token[0, 0]


def _plan_gather(src_refs, land_refs, send_sems, recv_sems):
    x, y, c, chips = _place()
    pairs = list(enumerate(zip(src_refs, land_refs)))
    sent = [_remote(s, l.at[2 * x + y], send_sems, recv_sems, 3 * i + j, (px, py, c))
            for i, (s, l) in pairs for j, (px, py) in enumerate(chips)]
    received = [_remote(s, l.at[2 * px + py], send_sems, recv_sems, 3 * i + j, (px, py, c))
                for i, (s, l) in pairs for j, (px, py) in enumerate(chips)]
    return sent, received


def _plan_swap(src_ref, land_ref, send_sems, recv_sems):
    x, y, c, _ = _place()
    _, other_half = _halves(c, src_ref.shape[1], 8)
    cp = _remote(src_ref.at[pl.ds(0, src_ref.shape[0]), other_half], land_ref, send_sems, recv_sems, 0, (x, y, 1 - c))
    return [cp], [cp]


def _plan_scatter(src_ref, land_ref, send_sems, recv_sems):
    x, y, c, chips = _place()
    sent = [_remote(src_ref.at[2 * px + py], land_ref.at[2 * x + y], send_sems, recv_sems, j, (px, py, c))
            for j, (px, py) in enumerate(chips)]
    received = [_remote(src_ref.at[2 * px + py], land_ref.at[2 * px + py], send_sems, recv_sems, j, (px, py, c))
                for j, (px, py) in enumerate(chips)]
    return sent, received


def _plan_share(src_ref, land_ref, send_sems, recv_sems):
    x, y, c, _ = _place()
    mine_half, other_half = _halves(c, src_ref.shape[0], 8)
    return ([_remote(src_ref.at[mine_half], src_ref.at[mine_half], send_sems, recv_sems, 0, (x, y, 1 - c))],
            [_remote(src_ref.at[other_half], src_ref.at[other_half], send_sems, recv_sems, 0, (x, y, 1 - c))])


class _GatherBehind:
    def __init__(self, name, shards, chip, after):
        self.chip = chip
        self.token, self.wait = _split_gather(name, shards, after)

    def result(self, *after):
        shards, lands = self.wait(*after)
        return [lax.dynamic_update_slice(land, shard[None], (self.chip, 0, 0)) for shard, land in zip(shards, lands)]


class _ReduceBehind:
    def __init__(self, name, chip, c_idx):
        self.name, self.chip, self.c_idx = name, chip, c_idx

    def start_slab(self, g):
        n_sh, rows, n = g.shape
        token, self.wait = _split_copy(self.name + "_swap", g, (n_sh, rows // 2, n), g.dtype, 1, _plan_swap)
        return token

    def pair(self, after):
        g, a = self.wait(after)
        h = _add_half(self.name + "_pair", g, a, self.c_idx)
        token, self.wait = _split_copy(self.name + "_scatter", h, h.shape, h.dtype, 3, _plan_scatter)
        return token

    def total(self, after):
        h, b = self.wait(after)
        b = lax.dynamic_update_slice(b, lax.dynamic_slice_in_dim(h, self.chip, 1, axis=0), (self.chip, 0, 0))
        f = _sum_chips(self.name + "_sum", b, self.c_idx)
        token, self.wait = _split_copy(self.name + "_share", f, (8, LANES), f.dtype, 1, _plan_share)
        return token

    def result(self, after):
        return self.wait(after)[0]


class _ReduceColsBehind(_ReduceBehind):
    def start(self, g_padded):
        g = _unpack_w_in_grad(g_padded)
        n = g.shape[1] // N_CHIP
        return self.start_slab(jnp.stack([g[:, k * n:(k + 1) * n] for k in range(N_CHIP)]))


def _f_adamw(w, g, m, v):
    m = ADAM_B1 * m + (1.0 - ADAM_B1) * g
    v = ADAM_B2 * v + (1.0 - ADAM_B2) * (g * g)
    m_hat = m / (1.0 - ADAM_B1 ** ADAM_STEP)
    v_hat = v / (1.0 - ADAM_B2 ** ADAM_STEP)
    return -ADAM_LR * (m_hat / (jnp.sqrt(v_hat) + ADAM_EPS) + ADAM_WD * w), m, v


def _adamw(name, w, g, m, v):
    rows, n = w.shape
    return _rowwise(name, lambda w, g, m, v: (_f_adamw(w, g, m, v), ()), rows, [(t, n, 0) for t in (w, g, m, v)], [],
                    [(n, F32)] * 3, [], tm=_row_tile(rows, 8, 256))


def _pack_rows(parts):
    rows = []
    for t in parts:
        t = t.reshape(-1)
        rows.append(jnp.pad(t, (0, -t.shape[0] % LANES)).reshape(-1, LANES))
    out = jnp.concatenate(rows, axis=0)
    return jnp.pad(out, ((0, -out.shape[0] % 8), (0, 0)))


def _unpack_rows(packed, shapes):
    out, r = [], 0
    for shp in shapes:
        n = int(np.prod(shp))
        nr = -(-n // LANES)
        out.append(packed[r:r + nr].reshape(-1)[:n].reshape(shp))
        r += nr
    return out


def _sum_blocks(name, g):
    def body(g_ref, o_ref):
        acc = g_ref[0]
        for k in range(1, g.shape[0]):
            acc = acc + g_ref[k]
        o_ref[...] = acc

    return pl.pallas_call(body, name=name, out_shape=jax.ShapeDtypeStruct(g.shape[1:], F32))(g)


def _silu(t):
    return t * _sigmoid(t)


def _ada_fwd(cc, w_ada):
    n = w_ada.shape[1]
    tn = _row_tile(n, LANES, 512)

    def body(cc_ref, w_ref, o_ref):
        o_ref[...] = _nn(_silu(cc_ref[...]), w_ref[...])

    return pl.pallas_call(
        body, name="ada_fwd", grid=(n // tn,), out_shape=jax.ShapeDtypeStruct((cc.shape[0], n), F32),
        in_specs=[pl.BlockSpec(cc.shape, lambda j: (0, 0)), pl.BlockSpec((w_ada.shape[0], tn), lambda j: (0, j))],
        out_specs=pl.BlockSpec((cc.shape[0], tn), lambda j: (0, j)), compiler_params=_cp("parallel"),
    )(cc, w_ada)


def _ada_bwd(cc, dm, w_ada):
    d, n = w_ada.shape
    tn = _row_tile(n, LANES, 512)

    def body(cc_ref, dm_ref, w_ref, gw_ref, ds_ref):
        @pl.when(pl.program_id(0) == 0)
        def _():
            ds_ref[...] = jnp.zeros_like(ds_ref)

        gw_ref[...] = _raw_dot("tn", _silu(cc_ref[...]), dm_ref[...], True)
        ds_ref[...] += _raw_dot("nt", dm_ref[...], w_ref[...], False)

    return pl.pallas_call(
        body, name="ada_bwd", grid=(n // tn,),
        out_shape=[jax.ShapeDtypeStruct((d, n), F32), jax.ShapeDtypeStruct(cc.shape, F32)],
        in_specs=[pl.BlockSpec(cc.shape, lambda j: (0, 0)), pl.BlockSpec((cc.shape[0], tn), lambda j: (0, j)),
                  pl.BlockSpec((d, tn), lambda j: (0, j))],
        out_specs=[pl.BlockSpec((d, tn), lambda j: (0, j)), pl.BlockSpec(cc.shape, lambda j: (0, 0))],
        compiler_params=_cp("arbitrary"),
    )(cc, dm, w_ada)


def _c_ctx_grad(parts, c_ctx):
    def body(p_ref, c_ref, o_ref):
        ds = ((p_ref[0] + p_ref[1]) + p_ref[2]) + p_ref[3]
        _, vjp = jax.vjp(_silu, c_ref[...])
        o_ref[...] = vjp(ds)[0]

    return pl.pallas_call(body, name="c_ctx_grad", out_shape=jax.ShapeDtypeStruct(c_ctx.shape, F32))(parts, c_ctx)


def kernel(x, c, ctx, c_ctx, w_ada, b_ada, g_pre_mix, g_post_mix, g_pre_ffn, g_post_ffn, w_in, attn_sink, w_gate_fwd, b_gate_fwd, w_gate_bwd, b_gate_bwd, g_gla_norm, w_out, w_ffn_in, w_ffn_out, loss_target, m_c_ctx, m_w_ada, m_b_ada, m_g_pre_mix, m_g_post_mix, m_g_pre_ffn, m_g_post_ffn, m_w_in, m_attn_sink, m_w_gate_fwd, m_b_gate_fwd, m_w_gate_bwd, m_b_gate_bwd, m_g_gla_norm, m_w_out, m_w_ffn_in, m_w_ffn_out, v_c_ctx, v_w_ada, v_b_ada, v_g_pre_mix, v_g_post_mix, v_g_pre_ffn, v_g_post_ffn, v_w_in, v_attn_sink, v_w_gate_fwd, v_b_gate_fwd, v_w_gate_bwd, v_b_gate_bwd, v_g_gla_norm, v_w_out, v_w_ffn_in, v_w_ffn_out):
    xi, yi, ci = lax.axis_index("x"), lax.axis_index("y"), lax.axis_index("c")
    dev, chip = 4 * xi + 2 * yi + ci, 2 * xi + yi
    c_idx = jnp.reshape(ci, (1,)).astype(jnp.int32)
    d = x.shape[-1]
    n_ada, n_in, n_f = w_ada.shape[-1], w_in.shape[-1], w_ffn_in.shape[-1]
    r_out, r_f = w_out.shape[1], w_ffn_out.shape[1]
    n_gate = w_gate_fwd.shape[-1]
    by_chip = lambda t: t[0::2]

    rc = -(-d // LANES)
    g1 = _ag_small("gather_cond", _pack_rows([c[0], w_gate_fwd[0], w_gate_bwd[0]]))
    c_all = g1[:, :rc].reshape(N_DEV, -1)[:, :d]
    gr = GATE_RANK * n_gate // LANES
    gate_full = lambda off: jnp.transpose(by_chip(g1)[:, off:off + gr].reshape(N_CHIP, GATE_RANK, n_gate),
                                          (1, 0, 2)).reshape(GATE_RANK, N_CHIP * n_gate)
    wgf, wgb = gate_full(rc), gate_full(rc + gr)
    cc = jnp.concatenate([c_all, c_ctx[None, :], jnp.zeros((7, d), F32)], axis=0)

    g2 = _ag_small("gather_ada", _ada_fwd(cc, w_ada[0]).reshape(-1, LANES))
    ada_all = jnp.transpose(by_chip(g2).reshape(N_CHIP, 16, n_ada), (1, 0, 2)).reshape(16, N_CHIP * n_ada) + b_ada
    first = _GatherBehind("gather_w_in", [w_in[0].astype(BF16)], chip, g2)
    late_shards = [w_out[0].astype(BF16), jnp.transpose(w_ffn_in[0]).astype(BF16), w_ffn_out[0].astype(BF16)]
    late = []

    def first_weights(*after):
        w_in_g, = first.result(*after, *late_shards)
        late.append(_GatherBehind("gather_late", late_shards, chip, w_in_g))
        return _pack_w_in(jnp.concatenate([w_in_g[k] for k in range(N_CHIP)], axis=1)), late[0].token

    def late_weights(after):
        return [t.reshape(-1, d) for t in late[0].result(after)]

    ada_all = _behind(ada_all, first.token)
    ada = lax.dynamic_slice(ada_all, (dev, 0), (1, N_CHIP * n_ada))
    ada_c = ada_all[N_DEV:N_DEV + 1]

    w = _prep_gate_weights(wgf, wgb)
    w.update(w_in=first_weights, g_pre_mix=g_pre_mix, g_post_mix=g_post_mix, g_pre_ffn=g_pre_ffn, g_post_ffn=g_post_ffn,
             attn_sink=attn_sink, b_gate_fwd=b_gate_fwd, b_gate_bwd=b_gate_bwd, g_gla_norm=g_gla_norm)

    reduce_behind = _ReduceBehind("reduce_late", chip, c_idx)
    reduce_w_in = _ReduceColsBehind("reduce_w_in", chip, c_idx)
    loss_lanes, grad_x, g, d_ada, d_ada_c = _local_step(x[0], ctx[0], loss_target[0], ada, ada_c, w, late_weights,
                                                        reduce_behind, reduce_w_in)

    small = ("g_pre_mix", "g_post_mix", "g_pre_ffn", "g_post_ffn", "attn_sink", "b_gate_fwd", "b_gate_bwd",
             "g_gla_norm", "w_gate_fwd", "w_gate_bwd")
    shapes = [(1, 6 * d)] * 2 + [g[n].shape for n in small] + [(1, LANES)]
    g3 = _ag_small("gather_small_grads", _pack_rows([d_ada, d_ada_c] + [g[n] for n in small] + [loss_lanes]))
    tot = dict(zip(("d_ada", "d_ada_c") + small + ("loss",),
                   _unpack_rows(_sum_blocks("sum_small_grads", g3), shapes)))
    r_ada = 6 * d // LANES
    dm = jnp.concatenate([g3[:, :r_ada].reshape(N_DEV, 6 * d), tot["d_ada_c"], jnp.zeros((7, 6 * d), F32)], axis=0)
    grads = {n: tot[n] for n in small[:8]}
    grads["b_ada"] = _sum_blocks("sum_b_ada", dm.reshape(16, r_ada, LANES)).reshape(1, 6 * d)
    grads["w_gate_fwd"] = lax.dynamic_slice(tot["w_gate_fwd"], (0, chip * n_gate), (GATE_RANK, n_gate))[None]
    grads["w_gate_bwd"] = lax.dynamic_slice(tot["w_gate_bwd"], (0, chip * n_gate), (GATE_RANK, n_gate))[None]
    gw_ada, dsc = _ada_bwd(cc, lax.dynamic_slice(dm, (0, chip * n_ada), (16, n_ada)), w_ada[0])
    grads["w_ada"] = gw_ada[None]
    g4 = _ag_small("gather_c_ctx", _pack_rows([dsc[N_DEV]]))
    grads["c_ctx"] = _c_ctx_grad(by_chip(g4), _pack_rows([c_ctx])).reshape(-1)[:d]

    grads["w_in"] = reduce_w_in.result(g4)[None]
    part = lambda n: g["late"][g["late_at"][n]:g["late_at"][n] + g["late_rows"][n]]
    grads["w_ffn_in"], grads["w_ffn_out"], grads["w_out"] = (jnp.transpose(part("w_ffn_in_t"))[None],
                                                            part("w_ffn_out")[None], part("w_out")[None])

    names = ("c_ctx", "w_ada", "b_ada", "g_pre_mix", "g_post_mix", "g_pre_ffn", "g_post_ffn", "w_in", "attn_sink",
             "w_gate_fwd", "b_gate_fwd", "w_gate_bwd", "b_gate_bwd", "g_gla_norm", "w_out", "w_ffn_in", "w_ffn_out")
    weights = dict(zip(names, (c_ctx, w_ada, b_ada, g_pre_mix, g_post_mix, g_pre_ffn, g_post_ffn, w_in, attn_sink,
                               w_gate_fwd, b_gate_fwd, w_gate_bwd, b_gate_bwd, g_gla_norm, w_out, w_ffn_in,
                               w_ffn_out)))
    m_in = dict(zip(names, (m_c_ctx, m_w_ada, m_b_ada, m_g_pre_mix, m_g_post_mix, m_g_pre_ffn, m_g_post_ffn, m_w_in,
                            m_attn_sink, m_w_gate_fwd, m_b_gate_fwd, m_w_gate_bwd, m_b_gate_bwd, m_g_gla_norm,
                            m_w_out, m_w_ffn_in, m_w_ffn_out)))
    v_in = dict(zip(names, (v_c_ctx, v_w_ada, v_b_ada, v_g_pre_mix, v_g_post_mix, v_g_pre_ffn, v_g_post_ffn, v_w_in,
                            v_attn_sink, v_w_gate_fwd, v_b_gate_fwd, v_w_gate_bwd, v_b_gate_bwd, v_g_gla_norm,
                            v_w_out, v_w_ffn_in, v_w_ffn_out)))
    large = ("w_ada", "w_in", "w_out", "w_ffn_in", "w_ffn_out")
    tiny = tuple(n for n in names if n not in large)
    delta, new_m, new_v = {}, {}, {}
    for n in large:
        dl, nm, nv = _adamw("adamw_" + n, weights[n][0], grads[n][0], m_in[n][0], v_in[n][0])
        delta[n], new_m[n], new_v[n] = dl[None], nm[None], nv[None]
    tiny_shapes = [weights[n].shape for n in tiny]
    packed = [_pack_rows([t[n] for n in tiny]) for t in (weights, grads, m_in, v_in)]
    for out, res in zip((delta, new_m, new_v), _adamw("adamw_small", *packed)):
        out.update(zip(tiny, _unpack_rows(res, tiny_shapes)))
    for n in tiny:
        grads[n] = grads[n].reshape(weights[n].shape)

    return (tot["loss"][0, 0], grad_x[None], *[grads[n] for n in names], *[delta[n] for n in names], *[new_m[n] for n in names],
            *[new_v[n] for n in names])
```

```python
import functools

import jax
import jax.numpy as jnp
import numpy as np
from jax import lax
from jax.experimental import pallas as pl
from jax.experimental.pallas import tpu as pltpu

F32 = jnp.float32
BF16 = jnp.bfloat16
MESH = pl.DeviceIdType.MESH

HEAD_DIM = 64
ATT_HEADS = 8
ATT_KV_HEADS = 2
ATT_GROUP = ATT_HEADS // ATT_KV_HEADS
WINDOW = 128
BLOCK = 128
GRID_W = 64
ROPE_BASE = 10000.0
GLA_HEADS = 8
GLA_DK = 32
GLA_DV = 64
GLA_CHUNK = 64
GATE_RANK = 16
GATE_TAU = 16.0
NEG_INF = -1e30
QW = ATT_HEADS * HEAD_DIM
KVW = ATT_KV_HEADS * HEAD_DIM
GKW = GLA_HEADS * GLA_DK
GVW = GLA_HEADS * GLA_DV
IN_COLS = QW + 2 * KVW + 2 * GKW + 2 * GVW + 2 * GATE_RANK
LANES = 128
IN_PAD = IN_COLS + LANES - 2 * GATE_RANK
C_Q, C_GV, C_GG = 0, QW, QW + GVW
C_K = C_GG + GVW
C_V = C_K + KVW
C_GQ = C_V + KVW
C_GK = C_GQ + GKW
C_Z = C_GK + GKW
MIX = QW + GVW

ADAM_LR, ADAM_B1, ADAM_B2, ADAM_EPS, ADAM_WD, ADAM_STEP = 0.001, 0.9, 0.999, 1e-08, 0.01, 10

VMEM_LIMIT = 56 * 1024 * 1024


def _cp(*sem):
    return pltpu.CompilerParams(dimension_semantics=sem, vmem_limit_bytes=VMEM_LIMIT)


def _pick(n, cands):
    for t in cands:
        if n % t == 0:
            return t
    return n


_DIMS = {"nn": (((1,), (0,)), ((), ())), "nt": (((1,), (1,)), ((), ())), "tn": (((0,), (0,)), ((), ()))}


def _raw_dot(mode, a, b, hi):
    dot = lambda u, v: lax.dot_general(u, v, _DIMS[mode], preferred_element_type=F32)
    if hi:
        a, b = a.astype(F32), b.astype(F32)
        a_hi, b_hi = a.astype(BF16), b.astype(BF16)
        a_lo, b_lo = (a - a_hi.astype(F32)).astype(BF16), (b - b_hi.astype(F32)).astype(BF16)
        return dot(a_hi, b_hi) + (dot(a_lo, b_hi) + dot(a_hi, b_lo))
    return dot(a.astype(BF16), b.astype(BF16))


def _make_dot(mode, hi):
    @jax.custom_vjp
    def dot(a, b):
        return _raw_dot(mode, a, b, hi)

    def fwd(a, b):
        return _raw_dot(mode, a, b, hi), (a, b)

    def bwd(res, dc):
        a, b = res
        if mode == "nn":
            return _raw_dot("nt", dc, b, hi), _raw_dot("tn", a, dc, hi)
        if mode == "nt":
            return _raw_dot("nn", dc, b, hi), _raw_dot("tn", dc, a, hi)
        return _raw_dot("nt", b, dc, hi), _raw_dot("nn", a, dc, hi)

    dot.defvjp(fwd, bwd)
    return dot


_nn, _nt, _tn = _make_dot("nn", False), _make_dot("nt", False), _make_dot("tn", False)
_nn_hi = _make_dot("nn", True)


MM_VMEM_BUDGET = 44 * 1024 * 1024


def _halvings(n):
    out = [n]
    while out[-1] % (2 * LANES) == 0:
        out.append(out[-1] // 2)
    return out


def _mm_tiles(mode, m, n, k, a_bytes, b_bytes, o_bytes, init_bytes=0):
    tms = [t for t in dict.fromkeys((m, m // 2, m // 4, 2048, 1024, 512, 256, 128))
           if m % t == 0 and t % (LANES if mode == "tn" else 16) == 0 and t <= 4096] or [m]
    if mode == "tn":
        fits = [(k // tk + 0.5 * (m // tm), tm, tk)
                for tk in (4096, 2048, 1024, 512, 256, 128) if k % tk == 0 for tm in tms
                if 2 * (tk * tm * a_bytes + tk * n * b_bytes + tm * n * (o_bytes + init_bytes)) <= MM_VMEM_BUDGET]
        if fits:
            _, tm, tk = min(fits)
            return tm, n, tk
    tks = ([t for t in (512, 256, 128) if k % t == 0] or [k]) if mode == "tn" else _halvings(k)
    for tn in _halvings(n):
        for tk in tks:
            for tm in tms:
                acc = tm * tn * 4 if (k // tk > 1 and o_bytes != 4) else 0
                tiles = tm * tk * a_bytes + tk * tn * b_bytes + tm * tn * (o_bytes + init_bytes)
                if 2 * tiles + acc <= MM_VMEM_BUDGET:
                    return tm, tn, tk
    return tms[-1], _halvings(n)[-1], tks[-1]


def _mm(name, a, b, mode, out_dtype=F32, init=None, after=None):
    follow = () if after is None else (after,)
    if mode == "nn":
        (m, k), n = a.shape, b.shape[1]
    elif mode == "nt":
        (m, k), n = a.shape, b.shape[0]
    else:
        (k, m), n = a.shape, b.shape[1]
    tm, tn, tk = _mm_tiles(mode, m, n, k, a.dtype.itemsize, b.dtype.itemsize, jnp.dtype(out_dtype).itemsize,
                           0 if init is None else 4)
    nk = k // tk
    use_acc = nk > 1 and out_dtype != F32

    inits = () if init is None else (init,)

    def body(a_ref, b_ref, *rest):
        rest = rest[:len(inits)] + rest[len(inits) + len(follow):]
        o_ref, acc = rest[len(inits)], rest[len(inits) + 1:]
        part = _raw_dot(mode, a_ref[...], b_ref[...], False)
        first = lambda: part + rest[0][...] if inits else part
        if nk == 1:
            o_ref[...] = first().astype(o_ref.dtype)
            return
        acc_ref = acc[0] if use_acc else o_ref
        kk = pl.program_id(2)

        @pl.when(kk == 0)
        def _():
            acc_ref[...] = first()

        @pl.when(kk > 0)
        def _():
            acc_ref[...] += part

        if use_acc:
            @pl.when(kk == nk - 1)
            def _():
                o_ref[...] = acc_ref[...].astype(o_ref.dtype)

    if mode == "nn":
        a_spec = pl.BlockSpec((tm, tk), lambda i, j, kk: (i, kk))
        b_spec = pl.BlockSpec((tk, tn), lambda i, j, kk: (kk, j))
    elif mode == "nt":
        a_spec = pl.BlockSpec((tm, tk), lambda i, j, kk: (i, kk))
        b_spec = pl.BlockSpec((tn, tk), lambda i, j, kk: (j, kk))
    else:
        a_spec = pl.BlockSpec((tk, tm), lambda i, j, kk: (kk, i))
        b_spec = pl.BlockSpec((tk, tn), lambda i, j, kk: (kk, j))
    return pl.pallas_call(
        body, name=name, grid=(m // tm, n // tn, nk),
        in_specs=[a_spec, b_spec] + [pl.BlockSpec((tm, tn), lambda i, j, kk: (i, j))] * len(inits)
        + [pl.BlockSpec(memory_space=pl.ANY)] * len(follow),
        out_specs=pl.BlockSpec((tm, tn), lambda i, j, kk: (i, j)),
        out_shape=jax.ShapeDtypeStruct((m, n), out_dtype),
        scratch_shapes=[pltpu.VMEM((tm, tn), F32)] if use_acc else [],
        compiler_params=_cp("parallel", "parallel", "arbitrary"),
    )(a, b, *inits, *follow)


def _slab_layout(rows):
    offsets, at = [], 0
    for r in rows:
        at = -(-at // r) * r
        offsets.append(at)
        at += r
    return offsets, -(-at // 32) * 32


def _slab_zero_gaps(name, shape, rows, offsets):
    gaps = [(o + r, nxt) for o, r, nxt in zip(offsets, rows, offsets[1:] + [shape[1]]) if nxt > o + r]
    slab = None
    for i, (lo, hi) in enumerate(gaps):
        step = int(np.gcd(lo, hi - lo))

        def body(*refs):
            refs[-1][...] = jnp.zeros_like(refs[-1])

        slab = pl.pallas_call(
            body, name=f"{name}_{i}", grid=(shape[0], (hi - lo) // step), out_shape=jax.ShapeDtypeStruct(shape, F32),
            in_specs=[] if slab is None else [pl.BlockSpec(memory_space=pl.ANY)],
            out_specs=pl.BlockSpec((1, step, shape[2]), functools.partial(lambda k, j, b: (k, b + j, 0), b=lo // step)),
            input_output_aliases={} if slab is None else {0: 0}, compiler_params=_cp("parallel", "parallel"),
        )(*(() if slab is None else (slab,)))
    return slab


def _dw_into_slab(name, a, b, slab, shape, at):
    (k, m), n = a.shape, b.shape[1]
    r = m // N_CHIP
    fits = [(k // tk + 0.5 * (m // tm), tm, tk)
            for tk in (4096, 2048, 1024, 512, 256, 128) if k % tk == 0 for tm in (m, m // 2, r) if tm % LANES == 0
            if 2 * (tk * tm * a.dtype.itemsize + tk * n * b.dtype.itemsize + tm * n * 4) <= MM_VMEM_BUDGET]
    _, tm, tk = min(fits)
    per, nk = tm // r, k // tk

    def body(a_ref, b_ref, *rest):
        o_ref = rest[-1]
        part = _raw_dot("tn", a_ref[...], b_ref[...], False).reshape(o_ref.shape)
        if nk == 1:
            o_ref[...] = part
            return
        kk = pl.program_id(1)

        @pl.when(kk == 0)
        def _():
            o_ref[...] = part

        @pl.when(kk > 0)
        def _():
            o_ref[...] += part

    prev = () if slab is None else (slab,)
    return pl.pallas_call(
        body, name=name, grid=(m // tm, nk), out_shape=jax.ShapeDtypeStruct(shape, F32),
        in_specs=[pl.BlockSpec((tk, tm), lambda i, kk: (kk, i)), pl.BlockSpec((tk, n), lambda i, kk: (kk, 0))]
        + [pl.BlockSpec(memory_space=pl.ANY)] * len(prev),
        out_specs=pl.BlockSpec((per, r, n), lambda i, kk: (i, at // r, 0)),
        input_output_aliases={2: 0} if prev else {}, compiler_params=_cp("parallel", "arbitrary"),
    )(a, b, *prev)


def _rowwise(name, fn, rows, row_ins, full_ins, row_outs, acc_outs, tm=None):
    tm = tm or _pick(rows, (512, 256, 128))
    n_r, n_f, n_o, n_a = len(row_ins), len(full_ins), len(row_outs), len(acc_outs)

    def body(*refs):
        ins, outs = refs[:n_r + n_f], refs[n_r + n_f:]
        vals = [r[...].astype(F32) for r in ins]
        ro, ao = fn(*vals)
        for r, val in zip(outs[:n_o], ro):
            r[...] = val.astype(r.dtype)
        if n_a:
            @pl.when(pl.program_id(0) == 0)
            def _():
                for r in outs[n_o:]:
                    r[...] = jnp.zeros_like(r)

            for r, val in zip(outs[n_o:], ao):
                r[...] += val

    in_specs = [pl.BlockSpec((tm, w), functools.partial(lambda i, cb: (i, cb), cb=cb)) for _, w, cb in row_ins]
    in_specs += [pl.BlockSpec(a.shape, lambda i: (0, 0)) for a in full_ins]
    out_specs = [pl.BlockSpec((tm, w), lambda i: (i, 0)) for w, _ in row_outs]
    out_specs += [pl.BlockSpec(s, lambda i: (0, 0)) for s in acc_outs]
    out_shape = [jax.ShapeDtypeStruct((rows, w), dt) for w, dt in row_outs]
    out_shape += [jax.ShapeDtypeStruct(s, F32) for s in acc_outs]
    return pl.pallas_call(
        body, name=name, grid=(rows // tm,), in_specs=in_specs, out_specs=out_specs, out_shape=out_shape,
        compiler_params=_cp("arbitrary" if n_a else "parallel"),
    )(*[a for a, _, _ in row_ins], *full_ins)


def _rn(x):
    return x * lax.rsqrt(jnp.mean(x * x, axis=-1, keepdims=True) + 1e-6)


def _sigmoid(t):
    return 1.0 / (1.0 + jnp.exp(-t))


def _f_norm_mod(x, g, sh, sc):
    return _rn(x) * g * (1.0 + sc) + sh


def _f_post_res(xr, y, g, gate):
    return xr + gate * (_rn(y) * g)


@jax.custom_vjp
def _f_swiglu(g, u):
    return g * _sigmoid(g) * u


def _f_swiglu_fwd(g, u):
    s = _sigmoid(g)
    return g * s * u, (g, u, s)


def _f_swiglu_bwd(res, da):
    g, u, s = res
    gs = g * s
    return da * u * (s + gs * (1.0 - s)), da * gs


_f_swiglu.defvjp(_f_swiglu_fwd, _f_swiglu_bwd)


def _logsig(u):
    return jnp.minimum(u, 0.0) - jnp.log(1.0 + jnp.exp(-jnp.abs(u)))


def _f_gate(z, wf, wb, bf, bb):
    return _logsig(_nn(z, wf) + bf) / GATE_TAU, _logsig(_nn(z, wb) + bb) / GATE_TAU


def _f_gla_out(of, ob, gg, gt, bd):
    o = of + ob
    ms = _nn_hi(o * o, bd)
    return o * lax.rsqrt(ms + 1e-6) * gt * (gg * _sigmoid(gg))


def _norm_mod(name, x, g, sh, sc):
    rows, d = x.shape
    return _rowwise(name, lambda x, g, sh, sc: ((_f_norm_mod(x, g, sh, sc),), ()), rows,
                    [(x, d, 0)], [g, sh, sc], [(d, BF16)], [])[0]


def _norm_mod_bwd(name, dh, dres, x, g, sh, sc):
    rows, d = x.shape

    def fn(dh, dres, x, g, sh, sc):
        _, vjp = jax.vjp(_f_norm_mod, x, g, sh, sc)
        dx, dg, dsh, dsc = vjp(dh)
        return (dx + dres,), (dg, dsh, dsc)

    return _rowwise(name, fn, rows, [(dh, d, 0), (dres, d, 0), (x, d, 0)], [g, sh, sc], [(d, F32)],
                    [(1, d)] * 3)


def _post_res_norm_mod(name, xr, y, g_post, gate, g_pre, sh, sc):
    rows, d = xr.shape

    def fn(xr, y, g_post, gate, g_pre, sh, sc):
        x1 = _f_post_res(xr, y, g_post, gate)
        return (x1, _f_norm_mod(x1, g_pre, sh, sc)), ()

    return _rowwise(name, fn, rows, [(xr, d, 0), (y, d, 0)], [g_post, gate, g_pre, sh, sc], [(d, F32), (d, BF16)], [])


def _norm_mod_post_res_bwd(name, dh, dres, x1, y, g_pre, sh, sc, g_post, gate):
    rows, d = x1.shape

    def fn(dh, dres, x1, y, g_pre, sh, sc, g_post, gate):
        _, vjp_norm = jax.vjp(_f_norm_mod, x1, g_pre, sh, sc)
        dx1, dg_pre, dsh, dsc = vjp_norm(dh)
        dx1 = dx1 + dres
        _, vjp_res = jax.vjp(lambda y, g, gate: _f_post_res(jnp.zeros_like(y), y, g, gate), y, g_post, gate)
        dy, dg_post, dgate = vjp_res(dx1)
        return (dx1, dy), (dg_pre, dsh, dsc, dg_post, dgate)

    return _rowwise(name, fn, rows, [(dh, d, 0), (dres, d, 0), (x1, d, 0), (y, d, 0)], [g_pre, sh, sc, g_post, gate],
                    [(d, F32), (d, BF16)], [(1, d)] * 5, tm=_pick(rows, (256, 128)))


def _post_res_loss(name, xr, y, g, gate, target):
    rows, d = xr.shape

    def fn(xr, y, target, g, gate):
        x2, vjp = jax.vjp(lambda y, g, gate: _f_post_res(xr, y, g, gate), y, g, gate)
        diff = x2 - target
        part = 0.5 * jnp.sum(jnp.mean(diff * diff, axis=-1, keepdims=True), axis=0, keepdims=True)
        dx2 = diff * (1.0 / d)
        dy, dg, dgate = vjp(dx2)
        return (dx2, dy), (jnp.broadcast_to(part, (1, LANES)), dg, dgate)

    return _rowwise(name, fn, rows, [(xr, d, 0), (y, d, 0), (target, d, 0)], [g, gate], [(d, F32), (d, BF16)],
                    [(1, LANES), (1, d), (1, d)])


def _mm_rows(name, a, b, mode, fn, extras, outs):
    m, k = a.shape
    tm = _pick(m, (256, 128))

    def body(a_ref, b_ref, *rest):
        tiles = fn(_raw_dot(mode, a_ref[...], b_ref[...], False), *[e[...] for e in rest[:len(extras)]])
        for r, val in zip(rest[len(extras):], tiles):
            r[...] = val.astype(r.dtype)

    row = lambda w: pl.BlockSpec((tm, w), lambda i: (i, 0))
    return pl.pallas_call(
        body, name=name, grid=(m // tm,),
        in_specs=[row(k), pl.BlockSpec(b.shape, lambda i: (0, 0))] + [row(e.shape[1]) for e in extras],
        out_specs=[row(w) for w, _ in outs], out_shape=[jax.ShapeDtypeStruct((m, w), dt) for w, dt in outs],
        compiler_params=_cp("parallel"),
    )(a, b, *extras)


def _ffn_in_swiglu(name, h, w_t):
    f = w_t.shape[0] // 2
    fn = lambda u: (u, _f_swiglu(u[:, :f], u[:, f:]))
    return _mm_rows(name, h, w_t, "nt", fn, [], [(2 * f, BF16), (f, BF16)])


def _ffn_out_dx_swiglu_bwd(name, df, w_out, u):
    f = w_out.shape[0]

    def fn(da, u):
        u = u.astype(F32)
        _, vjp = jax.vjp(_f_swiglu, u[:, :f], u[:, f:])
        return (jnp.concatenate(vjp(da), axis=1),)

    return _mm_rows(name, df, w_out, "nt", fn, [u], [(2 * f, BF16)])[0]


def _gate_fwd(name, p, wf, wb, bf, bb):
    rows = p.shape[0]
    return _rowwise(name, lambda z, wf, wb, bf, bb: (_f_gate(z, wf, wb, bf, bb), ()), rows,
                    [(p, LANES, C_Z // LANES)], [wf, wb, bf, bb], [(GKW, F32)] * 2, [])


def _gate_bwd(name, p, dla_f, dla_b, wf, wb, bf, bb):
    rows = p.shape[0]

    def fn(z, dlf, dlb, wf, wb, bf, bb):
        _, vjp = jax.vjp(_f_gate, z, wf, wb, bf, bb)
        dz, dwf, dwb, dbf, dbb = vjp((dlf, dlb))
        return (dz,), (dwf, dwb, dbf, dbb)

    return _rowwise(name, fn, rows, [(p, LANES, C_Z // LANES), (dla_f, GKW, 0), (dla_b, GKW, 0)],
                    [wf, wb, bf, bb], [(LANES, BF16)], [(LANES, GKW), (LANES, GKW), (1, GKW), (1, GKW)])


def _head_mean_matrix():
    h = np.arange(GVW) // GLA_DV
    return jnp.asarray((h[:, None] == h[None, :]).astype(np.float32) / GLA_DV)


def _gla_out(name, attn, of, ob, p, gt):
    rows = of.shape[0]
    bd = _head_mean_matrix()
    fn = lambda attn, of, ob, gg, gt, bd: ((jnp.concatenate([attn, _f_gla_out(of, ob, gg, gt, bd)], axis=1),), ())
    return _rowwise(name, fn, rows, [(attn, QW, 0), (of, GVW, 0), (ob, GVW, 0), (p, GVW, C_GG // GVW)], [gt, bd],
                    [(MIX, BF16)], [])[0]


def _gla_out_bwd(name, dmix, of, ob, p, gt):
    rows = of.shape[0]
    bd = _head_mean_matrix()

    def fn(dm, of, ob, gg, gt, bd):
        _, vjp = jax.vjp(lambda of, gg, gt: _f_gla_out(of, ob, gg, gt, bd), of, gg, gt)
        do, dgg, dgt = vjp(dm)
        return (do, dgg), (dgt,)

    return _rowwise(name, fn, rows, [(dmix, GVW, 1), (of, GVW, 0), (ob, GVW, 0), (p, GVW, C_GG // GVW)], [gt, bd],
                    [(GVW, F32), (GVW, BF16)], [(1, GVW)])


def _rope_tables(n_tokens):
    t = jnp.arange(n_tokens)
    row = (t // GRID_W).astype(F32)
    col = (t % GRID_W).astype(F32)
    half = HEAD_DIM // 2
    inv_freq = ROPE_BASE ** (-jnp.arange(0, half, 2, dtype=F32) / half)
    ang_r = row[:, None] * inv_freq[None, :]
    ang_c = col[:, None] * inv_freq[None, :]
    ang = jnp.concatenate([ang_r, ang_r, ang_c, ang_c], axis=-1)
    sign = jnp.concatenate([-jnp.ones((16,), F32), jnp.ones((16,), F32)] * 2)
    cos, sin = jnp.cos(ang), jnp.sin(ang) * sign[None, :]
    return jnp.tile(cos, (1, 2)), jnp.tile(sin, (1, 2))


def _rot_pairs(x):
    w = x.shape[-1]
    lane = lax.broadcasted_iota(jnp.int32, x.shape, x.ndim - 1)
    return jnp.where((lane % 32) < 16, pltpu.roll(x, w - 16, x.ndim - 1), pltpu.roll(x, 16, x.ndim - 1))


def _rope_apply(x, cos, sin_signed, inverse):
    reps = x.shape[-1] // LANES
    cos = jnp.concatenate([cos] * reps, axis=-1) if reps > 1 else cos
    sin = jnp.concatenate([sin_signed] * reps, axis=-1) if reps > 1 else sin_signed
    if inverse:
        return x * cos + _rot_pairs(x * sin)
    return x * cos + _rot_pairs(x) * sin


def _rope_fwd(name, p, cos, sin):
    rows = p.shape[0]

    def fn(q, k, v, cos, sin):
        return (_rope_apply(q, cos, sin, False), _rope_apply(k, cos, sin, False), v), ()

    return _rowwise(name, fn, rows, [(p, QW, 0), (p, KVW, C_K // KVW), (p, KVW, C_V // KVW), (cos, LANES, 0),
                                     (sin, LANES, 0)], [], [(QW, BF16), (KVW, BF16), (KVW, BF16)], [])


def _proj_grad(name, dq_rot, dk_rot, dv, cos, sin, gla_f, gla_b, dgg, dz):
    rows = dq_rot.shape[0]

    def fn(dq, dk, dv, cos, sin, gqf, gkf, gvf, gqb, gkb, gvb, dgg, dz):
        parts = [_rope_apply(dq, cos, sin, True), gvf + gvb, dgg, _rope_apply(dk, cos, sin, True), dv, gqf + gqb,
                 gkf + gkb, dz]
        return (jnp.concatenate(parts, axis=1),), ()

    ins = [(dq_rot, QW), (dk_rot, KVW), (dv, KVW), (cos, LANES), (sin, LANES)]
    ins += [(t, t.shape[1]) for t in (*gla_f, *gla_b)] + [(dgg, GVW), (dz, LANES)]
    return _rowwise(name, fn, rows, [(t, w, 0) for t, w in ins], [], [(IN_PAD, BF16)], [],
                    tm=_pick(rows, (256, 128)))[0]


GROUP_ROWS = ATT_GROUP * BLOCK


def _f_attn(qs, kws, vws, kcs, vcs, sink, n, n_tokens):
    row = lax.broadcasted_iota(jnp.int32, (GROUP_ROWS, 1), 0)
    group = sum((row >= g * BLOCK).astype(jnp.int32) for g in range(1, ATT_GROUP))
    i = lax.broadcasted_iota(jnp.int32, (GROUP_ROWS, 3 * BLOCK), 0) - BLOCK * group
    j = lax.broadcasted_iota(jnp.int32, (GROUP_ROWS, 3 * BLOCK), 1)
    kpos = (n - 1) * BLOCK + j
    mask = (jnp.abs(j - BLOCK - i) <= WINDOW) & (kpos >= 0) & (kpos < n_tokens)
    head_id = lax.broadcasted_iota(jnp.int32, (1, ATT_HEADS), 1)
    scale = HEAD_DIM ** -0.5
    outs = []
    for h in range(ATT_KV_HEADS):
        sk = jnp.zeros((GROUP_ROWS, 1), F32)
        for g in range(ATT_GROUP):
            one = jnp.sum(jnp.where(head_id == h * ATT_GROUP + g, sink, 0.0), axis=-1, keepdims=True)
            sk = jnp.where(group == g, one, sk)
        q = qs[h] * scale
        s_w = jnp.where(mask, _nt(q, kws[h]), NEG_INF)
        s_c = _nt(q, kcs[h])
        m = lax.stop_gradient(jnp.maximum(jnp.maximum(jnp.max(s_w, axis=-1, keepdims=True),
                                                      jnp.max(s_c, axis=-1, keepdims=True)), sk))
        pw, pc = jnp.exp(s_w - m), jnp.exp(s_c - m)
        den = jnp.sum(pw, axis=-1, keepdims=True) + jnp.sum(pc, axis=-1, keepdims=True) + jnp.exp(sk - m)
        outs.append((_nn(pw, vws[h]) + _nn(pc, vcs[h])) / den)
    return tuple(outs)


def _group_rows(ref, h):
    hs = lambda hq: slice(hq * HEAD_DIM, (hq + 1) * HEAD_DIM)
    return jnp.concatenate([ref[:, hs(h * ATT_GROUP + g)].astype(F32) for g in range(ATT_GROUP)], axis=0)


def _ungroup_rows(ref, h, val):
    for g in range(ATT_GROUP):
        hq = h * ATT_GROUP + g
        ref[:, hq * HEAD_DIM:(hq + 1) * HEAD_DIM] = val[g * BLOCK:(g + 1) * BLOCK].astype(ref.dtype)


def _attn_loads(n, q_ref, kp_ref, vp_ref, kc_ref, vc_ref):
    r0 = pl.multiple_of(n * BLOCK, BLOCK)
    hs = lambda h: slice(h * HEAD_DIM, (h + 1) * HEAD_DIM)
    qs = [_group_rows(q_ref, h) for h in range(ATT_KV_HEADS)]
    kws = [kp_ref[pl.ds(r0, 3 * BLOCK), hs(h)].astype(F32) for h in range(ATT_KV_HEADS)]
    vws = [vp_ref[pl.ds(r0, 3 * BLOCK), hs(h)].astype(F32) for h in range(ATT_KV_HEADS)]
    kcs = [kc_ref[:, hs(h)].astype(F32) for h in range(ATT_KV_HEADS)]
    vcs = [vc_ref[:, hs(h)].astype(F32) for h in range(ATT_KV_HEADS)]
    return r0, hs, qs, kws, vws, kcs, vcs


def _attn_specs(s, c):
    full = lambda shape: pl.BlockSpec(shape, lambda n: (0, 0))
    return [pl.BlockSpec((BLOCK, QW), lambda n: (n, 0)), full((s + 2 * BLOCK, KVW)), full((s + 2 * BLOCK, KVW)),
            full((c, KVW)), full((c, KVW)), full((1, ATT_HEADS))]


def _attn_fwd(q, kp, vp, kc, vc, sink):
    s, c = q.shape[0], kc.shape[0]

    def body(q_ref, kp_ref, vp_ref, kc_ref, vc_ref, sink_ref, o_ref):
        n = pl.program_id(0)
        _, hs, qs, kws, vws, kcs, vcs = _attn_loads(n, q_ref, kp_ref, vp_ref, kc_ref, vc_ref)
        outs = _f_attn(qs, kws, vws, kcs, vcs, sink_ref[...], n, s)
        for h in range(ATT_KV_HEADS):
            _ungroup_rows(o_ref, h, outs[h])

    return pl.pallas_call(
        body, name="attn_fwd", grid=(s // BLOCK,), in_specs=_attn_specs(s, c),
        out_specs=pl.BlockSpec((BLOCK, QW), lambda n: (n, 0)), out_shape=jax.ShapeDtypeStruct((s, QW), BF16),
        compiler_params=_cp("parallel"),
    )(q, kp, vp, kc, vc, sink)


def _attn_bwd(do, q, kp, vp, kc, vc, sink):
    s, c = q.shape[0], kc.shape[0]

    def body(do_ref, q_ref, kp_ref, vp_ref, kc_ref, vc_ref, sink_ref, dq_ref, dkp_ref, dvp_ref, dkc_ref, dvc_ref,
             dsink_ref):
        n = pl.program_id(0)

        @pl.when(n == 0)
        def _():
            for r in (dkp_ref, dvp_ref, dkc_ref, dvc_ref, dsink_ref):
                r[...] = jnp.zeros_like(r)

        r0, hs, qs, kws, vws, kcs, vcs = _attn_loads(n, q_ref, kp_ref, vp_ref, kc_ref, vc_ref)
        _, vjp = jax.vjp(lambda qs, kws, vws, kcs, vcs, sink: _f_attn(qs, kws, vws, kcs, vcs, sink, n, s),
                         qs, kws, vws, kcs, vcs, sink_ref[...])
        dqs, dkws, dvws, dkcs, dvcs, dsink = vjp(tuple(_group_rows(do_ref, h) for h in range(ATT_KV_HEADS)))
        for h in range(ATT_KV_HEADS):
            _ungroup_rows(dq_ref, h, dqs[h])
            dkp_ref[pl.ds(r0, 3 * BLOCK), hs(h)] += dkws[h]
            dvp_ref[pl.ds(r0, 3 * BLOCK), hs(h)] += dvws[h]
            dkc_ref[:, hs(h)] += dkcs[h]
            dvc_ref[:, hs(h)] += dvcs[h]
        dsink_ref[...] += dsink

    full = lambda shape: pl.BlockSpec(shape, lambda n: (0, 0))
    return pl.pallas_call(
        body, name="attn_bwd", grid=(s // BLOCK,),
        in_specs=[pl.BlockSpec((BLOCK, QW), lambda n: (n, 0))] + _attn_specs(s, c),
        out_specs=[pl.BlockSpec((BLOCK, QW), lambda n: (n, 0)), full((s + 2 * BLOCK, KVW)), full((s + 2 * BLOCK, KVW)),
                   full((c, KVW)), full((c, KVW)), full((1, ATT_HEADS))],
        out_shape=[jax.ShapeDtypeStruct((s, QW), F32), jax.ShapeDtypeStruct((s + 2 * BLOCK, KVW), F32),
                   jax.ShapeDtypeStruct((s + 2 * BLOCK, KVW), F32), jax.ShapeDtypeStruct((c, KVW), F32),
                   jax.ShapeDtypeStruct((c, KVW), F32), jax.ShapeDtypeStruct((1, ATT_HEADS), F32)],
        compiler_params=_cp("arbitrary"),
    )(do, q, kp, vp, kc, vc, sink)


GLA_GROUPS = 1
GLA_GROUP_HEADS = GLA_HEADS // GLA_GROUPS
GKG, GVG = GKW // GLA_GROUPS, GVW // GLA_GROUPS


def _gla_masks(heads=GLA_HEADS):
    hk = np.arange(heads * GLA_DK) // GLA_DK
    hv = np.arange(heads * GLA_DV) // GLA_DV
    head_k = (np.arange(heads)[:, None] == hk[None, :]).astype(np.float32)
    head_v = (np.arange(heads)[:, None] == hv[None, :]).astype(np.float32)
    bd_t = (hv[:, None] == hk[None, :]).astype(np.float32)
    return jnp.asarray(head_k), jnp.asarray(head_v), jnp.asarray(bd_t)


def _group_states(st):
    return jnp.stack([st[g * GVG:(g + 1) * GVG, g * GKG:(g + 1) * GKG] for g in range(GLA_GROUPS)])


def _ungroup_states(st):
    out = jnp.zeros((GVW, GKW), st.dtype)
    for g in range(GLA_GROUPS):
        out = out.at[g * GVG:(g + 1) * GVG, g * GKG:(g + 1) * GKG].set(st[g])
    return out


def _tri(n, rev, strict=False):
    i = lax.broadcasted_iota(jnp.int32, (n, n), 0)
    j = lax.broadcasted_iota(jnp.int32, (n, n), 1)
    if strict:
        keep = (j > i) if rev else (j < i)
    else:
        keep = (j >= i) if rev else (j <= i)
    return keep


def _f_gla_chunk(q, k, v, la, st, head_k, head_v, bd_t, rev):
    return _f_gla_carry(*_f_gla_intra(q, k, v, la, head_k, head_v, rev), v, st, bd_t)


def _f_gla_intra(q, k, v, la, head_k, head_v, rev):
    heads, kw, vw = head_k.shape[0], q.shape[1], v.shape[1]
    keep = _tri(GLA_CHUNK, rev)
    b = _nn_hi(keep.astype(F32), la)
    bl = jnp.sum(la, axis=0, keepdims=True)
    qd = q * (GLA_DK ** -0.5) * jnp.exp(b)
    ki = k * jnp.exp(-b)
    kd = k * jnp.exp(bl - b)
    q_heads = (qd[None, :, :] * head_k[:, None, :]).reshape(heads * GLA_CHUNK, kw)
    a_all = _nt(q_heads, ki).reshape(heads, GLA_CHUNK, GLA_CHUNK)
    a_all = jnp.where(keep[None, :, :], a_all, 0.0).reshape(heads * GLA_CHUNK, GLA_CHUNK)
    o_all = _nn(a_all, v).reshape(heads, GLA_CHUNK, vw)
    return jnp.sum(o_all * head_v[:, None, :], axis=0), qd, kd, bl


def _f_gla_carry(intra, qd, kd, bl, v, st, bd_t):
    return intra + _nt(qd, st), st * jnp.exp(bl) + bd_t * _tn(v, kd)


def _gla_specs(s, tb, order):
    return [pl.BlockSpec((tb, GKW), lambda i: (order(i), C_GQ // GKW)),
            pl.BlockSpec((tb, GKW), lambda i: (order(i), C_GK // GKW)),
            pl.BlockSpec((tb, GVW), lambda i: (order(i), C_GV // GVW)),
            pl.BlockSpec((tb, GKW), lambda i: (order(i), 0))]


GLA_BLOCK_CHUNKS = 4


def _gla_fwd(p, la_f, la_b, st_f0, st_b0):
    s = p.shape[0]
    tb = GLA_BLOCK_CHUNKS * GLA_CHUNK
    nblk = s // tb
    up, down = (lambda i: i), (lambda i: nblk - 1 - i)
    masks = _gla_masks(GLA_GROUP_HEADS)

    def scan(rev, q_ref, k_ref, v_ref, la_ref, o_ref, sts_ref, st_ref, consts):
        for g in range(GLA_GROUPS):
            gk, gv = slice(g * GKG, (g + 1) * GKG), slice(g * GVG, (g + 1) * GVG)
            st = st_ref[g]
            sts_ref[0, g] = st
            chunks = range(GLA_BLOCK_CHUNKS)
            for ci in (reversed(chunks) if rev else chunks):
                rows = slice(ci * GLA_CHUNK, (ci + 1) * GLA_CHUNK)
                o, st = _f_gla_chunk(q_ref[rows, gk], k_ref[rows, gk], v_ref[rows, gv], la_ref[rows, gk], st, *consts,
                                     rev)
                o_ref[rows, gv] = o
            st_ref[g] = st

    def body(qf, kf, vf, laf, qb, kb, vb, lab, stf0, stb0, hk_ref, hv_ref, bd_ref, of_ref, stsf_ref, ob_ref, stsb_ref,
             stf_ref, stb_ref):
        @pl.when(pl.program_id(0) == 0)
        def _():
            stf_ref[...] = stf0[...]
            stb_ref[...] = stb0[...]

        consts = (hk_ref[...], hv_ref[...], bd_ref[...])
        scan(False, qf, kf, vf, laf, of_ref, stsf_ref, stf_ref, consts)
        scan(True, qb, kb, vb, lab, ob_ref, stsb_ref, stb_ref, consts)

    full = lambda a: pl.BlockSpec(a.shape, lambda i: (0,) * a.ndim)
    outs = lambda order: [pl.BlockSpec((tb, GVW), lambda i: (order(i), 0)),
                          pl.BlockSpec((1, GLA_GROUPS, GVG, GKG), lambda i: (order(i), 0, 0, 0))]
    return pl.pallas_call(
        body, name="gla_fwd", grid=(nblk,),
        in_specs=_gla_specs(s, tb, up) + _gla_specs(s, tb, down) + [full(st_f0), full(st_b0)]
        + [full(m) for m in masks],
        out_specs=outs(up) + outs(down),
        out_shape=[jax.ShapeDtypeStruct((s, GVW), F32), jax.ShapeDtypeStruct((nblk, GLA_GROUPS, GVG, GKG), F32)] * 2,
        scratch_shapes=[pltpu.VMEM((GLA_GROUPS, GVG, GKG), F32)] * 2,
        compiler_params=_cp("arbitrary"),
    )(p, p, p, la_f, p, p, p, la_b, st_f0, st_b0, *masks)


def _gla_bwd(p, la_f, la_b, sts_f, sts_b, do, after=None):
    s = p.shape[0]
    tb = GLA_BLOCK_CHUNKS * GLA_CHUNK
    nblk = s // tb
    up, down = (lambda i: i), (lambda i: nblk - 1 - i)
    masks = _gla_masks(GLA_GROUP_HEADS)
    follow = () if after is None else (after,)

    def back(rev, q_ref, k_ref, v_ref, la_ref, sts_ref, do_ref, dq_ref, dk_ref, dv_ref, dla_ref, dst0_ref, dst_ref,
             consts):
        def block(q, k, v, la, st):
            outs = [None] * GLA_BLOCK_CHUNKS
            chunks = range(GLA_BLOCK_CHUNKS)
            for ci in (reversed(chunks) if rev else chunks):
                outs[ci], st = _f_gla_chunk(q[ci], k[ci], v[ci], la[ci], st, *consts, rev)
            return tuple(outs), st

        for g in range(GLA_GROUPS):
            gk, gv = slice(g * GKG, (g + 1) * GKG), slice(g * GVG, (g + 1) * GVG)
            split = lambda r, cols: tuple(r[ci * GLA_CHUNK:(ci + 1) * GLA_CHUNK, cols].astype(F32)
                                          for ci in range(GLA_BLOCK_CHUNKS))
            _, vjp = jax.vjp(block, split(q_ref, gk), split(k_ref, gk), split(v_ref, gv), split(la_ref, gk),
                             sts_ref[0, g])
            dq, dk, dv, dla, dst = vjp((split(do_ref, gv), dst_ref[g]))
            for ci in range(GLA_BLOCK_CHUNKS):
                rows = slice(ci * GLA_CHUNK, (ci + 1) * GLA_CHUNK)
                dq_ref[rows, gk], dk_ref[rows, gk], dv_ref[rows, gv], dla_ref[rows, gk] = dq[ci], dk[ci], dv[ci], dla[ci]
            dst_ref[g] = dst
            dst0_ref[g] = dst

    def body(*refs):
        ins, (hk_ref, hv_ref, bd_ref) = refs[:12], refs[12:15]
        outs = refs[15 + len(follow):]

        @pl.when(pl.program_id(0) == 0)
        def _():
            outs[10][...] = jnp.zeros_like(outs[10])
            outs[11][...] = jnp.zeros_like(outs[11])

        consts = (hk_ref[...], hv_ref[...], bd_ref[...])
        back(False, *ins[:6], *outs[:5], outs[10], consts)
        back(True, *ins[6:], *outs[5:10], outs[11], consts)

    full = lambda a: pl.BlockSpec(a.shape, lambda i: (0,) * a.ndim)

    def ins(order):
        return _gla_specs(s, tb, order) + [pl.BlockSpec((1, GLA_GROUPS, GVG, GKG), lambda i: (order(i), 0, 0, 0)),
                                           pl.BlockSpec((tb, GVW), lambda i: (order(i), 0))]

    def outs(order):
        blk = lambda w: pl.BlockSpec((tb, w), lambda i: (order(i), 0))
        return [blk(GKW), blk(GKW), blk(GVW), blk(GKW), pl.BlockSpec((GLA_GROUPS, GVG, GKG), lambda i: (0, 0, 0))]

    shapes = [jax.ShapeDtypeStruct((s, GKW), F32), jax.ShapeDtypeStruct((s, GKW), F32),
              jax.ShapeDtypeStruct((s, GVW), F32), jax.ShapeDtypeStruct((s, GKW), F32),
              jax.ShapeDtypeStruct((GLA_GROUPS, GVG, GKG), F32)]
    both = pl.pallas_call(
        body, name="gla_bwd", grid=(nblk,),
        in_specs=ins(down) + ins(up) + [full(m) for m in masks] + [pl.BlockSpec(memory_space=pl.ANY)] * len(follow),
        out_specs=outs(down) + outs(up), out_shape=shapes * 2,
        scratch_shapes=[pltpu.VMEM((GLA_GROUPS, GVG, GKG), F32)] * 2,
        compiler_params=_cp("arbitrary"),
    )(p, p, p, la_f, sts_f, do, p, p, p, la_b, sts_b, do, *masks, *follow)
    return both[:5], both[5:]


def _f_ctx_state(k, v, la_f, la_b, bd_t):
    c = k.shape[0]
    after = _nn_hi(_tri(c, True, strict=True).astype(F32), la_f)
    before = _nn_hi(_tri(c, False, strict=True).astype(F32), la_b)
    return bd_t * _tn(v, k * jnp.exp(after)), bd_t * _tn(v, k * jnp.exp(before))


def _ctx_state(pc, la_f, la_b):
    c = pc.shape[0]
    bd_t = _gla_masks()[2]

    def body(k_ref, v_ref, lf_ref, lb_ref, bd_ref, sf_ref, sb_ref):
        sf_ref[...], sb_ref[...] = _f_ctx_state(k_ref[...], v_ref[...], lf_ref[...], lb_ref[...], bd_ref[...])

    full = lambda a: pl.BlockSpec(a.shape, lambda i: (0, 0))
    return pl.pallas_call(
        body, name="ctx_state_fwd", grid=(1,),
        in_specs=[pl.BlockSpec((c, GKW), lambda i: (0, C_GK // GKW)), pl.BlockSpec((c, GVW), lambda i: (0, C_GV // GVW)),
                  full(la_f), full(la_b), full(bd_t)],
        out_specs=[pl.BlockSpec((GVW, GKW), lambda i: (0, 0))] * 2,
        out_shape=[jax.ShapeDtypeStruct((GVW, GKW), F32)] * 2,
        compiler_params=_cp("arbitrary"),
    )(pc, pc, la_f, la_b, bd_t)


def _ctx_state_bwd(pc, la_f, la_b, dsf, dsb):
    c = pc.shape[0]
    bd_t = _gla_masks()[2]

    def body(k_ref, v_ref, lf_ref, lb_ref, bd_ref, dsf_ref, dsb_ref, dk_ref, dv_ref, dlf_ref, dlb_ref):
        _, vjp = jax.vjp(lambda k, v, lf, lb: _f_ctx_state(k, v, lf, lb, bd_ref[...]),
                         k_ref[...], v_ref[...], lf_ref[...], lb_ref[...])
        dk, dv, dlf, dlb = vjp((dsf_ref[...], dsb_ref[...]))
        dk_ref[...], dv_ref[...] = dk.astype(BF16), dv.astype(BF16)
        dlf_ref[...], dlb_ref[...] = dlf, dlb

    full = lambda a: pl.BlockSpec(a.shape, lambda i: (0, 0))
    return pl.pallas_call(
        body, name="ctx_state_bwd", grid=(1,),
        in_specs=[pl.BlockSpec((c, GKW), lambda i: (0, C_GK // GKW)), pl.BlockSpec((c, GVW), lambda i: (0, C_GV // GVW)),
                  full(la_f), full(la_b), full(bd_t), full(dsf), full(dsb)],
        out_specs=[pl.BlockSpec((c, GKW), lambda i: (0, 0)), pl.BlockSpec((c, GVW), lambda i: (0, 0)),
                   pl.BlockSpec((c, GKW), lambda i: (0, 0)), pl.BlockSpec((c, GKW), lambda i: (0, 0))],
        out_shape=[jax.ShapeDtypeStruct((c, GKW), BF16), jax.ShapeDtypeStruct((c, GVW), BF16),
                   jax.ShapeDtypeStruct((c, GKW), F32), jax.ShapeDtypeStruct((c, GKW), F32)],
        compiler_params=_cp("arbitrary"),
    )(pc, pc, la_f, la_b, bd_t, dsf, dsb)


_SRC_COLS = ((0, QW), (QW + 2 * KVW + 2 * GKW, GVW), (QW + 2 * KVW + 2 * GKW + GVW, GVW), (QW, KVW), (QW + KVW, KVW),
             (QW + 2 * KVW, GKW), (QW + 2 * KVW + GKW, GKW), (IN_COLS - 2 * GATE_RANK, 2 * GATE_RANK))
_DST_COLS = (C_Q, C_GV, C_GG, C_K, C_V, C_GQ, C_GK, C_Z)


def _pack_w_in(w_in):
    parts = [w_in[:, s:s + n] for s, n in _SRC_COLS]
    parts.append(jnp.zeros((w_in.shape[0], IN_PAD - C_Z - 2 * GATE_RANK), w_in.dtype))
    return jnp.concatenate(parts, axis=1)


def _unpack_w_in_grad(g):
    by_src = sorted(zip(_SRC_COLS, _DST_COLS))
    return jnp.concatenate([g[:, d:d + n] for (_, n), d in by_src], axis=1)


def _prep_gate_weights(w_gate_fwd, w_gate_bwd):
    pad_rows = lambda w, at: jnp.zeros((LANES, GKW), F32).at[at:at + GATE_RANK].set(w)
    return {"wg_f": pad_rows(w_gate_fwd, 0), "wg_b": pad_rows(w_gate_bwd, GATE_RANK)}


def _local_step(x, ctx, target, ada, ada_c, w, late_weights, reduce_behind=None, reduce_w_in=None):
    s, d = x.shape
    sh1, sc1, gt1, sh2, sc2, gt2 = [ada[:, i * d:(i + 1) * d] for i in range(6)]
    sh1c, sc1c = ada_c[:, :d], ada_c[:, d:2 * d]
    cos, sin = _rope_tables(s)
    gt = jnp.tile(w["g_gla_norm"], (1, GLA_HEADS))

    h = _norm_mod("pre_mix", x, w["g_pre_mix"], sh1, sc1)
    hc = _norm_mod("pre_mix_ctx", ctx, w["g_pre_mix"], sh1c, sc1c)
    w_in, token = w["w_in"](h, cos, sin)
    p = _mm("proj_in", h, w_in, "nn", after=token)
    pc = _mm("proj_in_ctx", hc, w_in, "nn")
    q_rot, k_rot, v_b = _rope_fwd("rope", p, cos, sin)
    pad = ((BLOCK, BLOCK), (0, 0))
    kp, vp = jnp.pad(k_rot, pad), jnp.pad(v_b, pad)
    kc, vc = pc[:, C_K:C_K + KVW].astype(BF16), pc[:, C_V:C_V + KVW].astype(BF16)
    attn = _attn_fwd(q_rot, kp, vp, kc, vc, w["attn_sink"])
    gate_w = (w["wg_f"], w["wg_b"], w["b_gate_fwd"], w["b_gate_bwd"])
    la_f, la_b = _gate_fwd("gate", p, *gate_w)
    la_fc, la_bc = _gate_fwd("gate_ctx", pc, *gate_w)
    st_f0, st_b0 = _ctx_state(pc, la_fc, la_bc)
    o_f, sts_f, o_b, sts_b = _gla_fwd(p, la_f, la_b, _group_states(st_f0), _group_states(st_b0))
    mix = _gla_out("gla_out", attn, o_f, o_b, p, gt)
    w_out, w_ffn_in_t, w_ffn_out = late_weights(attn)
    y = _mm("proj_out", mix, w_out, "nn", BF16)
    x1, h2 = _post_res_norm_mod("post_mix_pre_ffn", x, y, w["g_post_mix"], gt1, w["g_pre_ffn"], sh2, sc2)
    u, a = _ffn_in_swiglu("ffn_in", h2, w_ffn_in_t)
    f = _mm("ffn_out", a, w_ffn_out, "nn", BF16)
    g = {}
    dx2, df, loss, g["g_post_ffn"], dgt2 = _post_res_loss("post_ffn_loss", x1, f, w["g_post_ffn"], gt2, target)

    late_rows = {"w_ffn_in_t": w_ffn_in_t.shape[0] // N_CHIP, "w_ffn_out": w_ffn_out.shape[0] // N_CHIP,
                 "w_out": w_out.shape[0] // N_CHIP}
    order = sorted(late_rows, key=lambda n: -late_rows[n])
    offsets, slab_rows = _slab_layout([late_rows[n] for n in order])
    late_at, slab_shape = dict(zip(order, offsets)), (N_CHIP, slab_rows, d)
    slab = _slab_zero_gaps("late_grads_gaps", slab_shape, [late_rows[n] for n in order], offsets)
    slab = _dw_into_slab("ffn_out_dw", a, df, slab, slab_shape, late_at["w_ffn_out"])
    du = _ffn_out_dx_swiglu_bwd("ffn_out_dx", df, w_ffn_out, u)
    dh2 = _mm("ffn_in_dx", du, w_ffn_in_t, "nn", BF16)
    slab = _dw_into_slab("ffn_in_dw", du, h2, slab, slab_shape, late_at["w_ffn_in_t"])
    dx1, dy, g["g_pre_ffn"], dsh2, dsc2, g["g_post_mix"], dgt1 = _norm_mod_post_res_bwd(
        "pre_ffn_post_mix_bwd", dh2, dx2, x1, y, w["g_pre_ffn"], sh2, sc2, w["g_post_mix"], gt1)
    dmix = _mm("proj_out_dx", dy, w_out, "nt", BF16)
    slab = _dw_into_slab("proj_out_dw", mix, dy, slab, slab_shape, late_at["w_out"])
    g["late"], g["late_at"], g["late_rows"] = slab, late_at, late_rows
    rb, sink, token = reduce_behind, w["attn_sink"], None
    if rb is not None:
        gt = _behind(gt, rb.start_slab(slab))
    d_o, dgg, dgt = _gla_out_bwd("gla_out_bwd", dmix, o_f, o_b, p, gt)
    g["g_gla_norm"] = jnp.sum(dgt.reshape(GLA_HEADS, GLA_DV), axis=0, keepdims=True)
    if rb is not None:
        token = rb.pair(dgg)
    gla_f, gla_b = _gla_bwd(p, la_f, la_b, sts_f, sts_b, d_o, token)
    (dla_f, dst_f0), (dla_b, dst_b0) = gla_f[3:], gla_b[3:]
    dst_f0, dst_b0 = _ungroup_states(dst_f0), _ungroup_states(dst_b0)
    if rb is not None:
        sink = _behind(sink, rb.total(dla_b))
    dgkc, dgvc, dla_fc, dla_bc = _ctx_state_bwd(pc, la_fc, la_bc, dst_f0, dst_b0)
    dz, dwf, dwb, dbf, dbb = _gate_bwd("gate_bwd", p, dla_f, dla_b, *gate_w)
    dzc, dwfc, dwbc, dbfc, dbbc = _gate_bwd("gate_ctx_bwd", pc, dla_fc, dla_bc, *gate_w)
    g["w_gate_fwd"] = (dwf + dwfc)[:GATE_RANK]
    g["w_gate_bwd"] = (dwb + dwbc)[GATE_RANK:2 * GATE_RANK]
    g["b_gate_fwd"], g["b_gate_bwd"] = dbf + dbfc, dbb + dbbc
    dq_rot, dkp, dvp, dkc, dvc, g["attn_sink"] = _attn_bwd(dmix, q_rot, kp, vp, kc, vc, sink)
    if rb is not None:
        g["late"] = rb.result(dq_rot)
    dp = _proj_grad("proj_grad", dq_rot, dkp[BLOCK:BLOCK + s], dvp[BLOCK:BLOCK + s], cos, sin, gla_f[:3], gla_b[:3],
                    dgg, dz)
    c_rows = ctx.shape[0]
    zeros = lambda n: jnp.zeros((c_rows, n), BF16)
    dpc = jnp.concatenate([zeros(QW), dgvc, zeros(GVW), dkc.astype(BF16), dvc.astype(BF16), zeros(GKW), dgkc, dzc],
                          axis=1)
    g["w_in"] = _mm("proj_in_dw", h, dp, "tn", init=_mm("proj_in_ctx_dw", hc, dpc, "tn"))
    token = None if reduce_w_in is None else reduce_w_in.start(g["w_in"])
    dh = _mm("proj_in_dx", dp, w_in, "nt", BF16, after=token)
    dhc = _mm("proj_in_ctx_dx", dpc, w_in, "nt")
    if reduce_w_in is not None:
        sh1 = _behind(sh1, reduce_w_in.pair(dh))
    dx, dg_a, dsh1, dsc1 = _norm_mod_bwd("pre_mix_bwd", dh, dx1, x, w["g_pre_mix"], sh1, sc1)
    if reduce_w_in is not None:
        dsh1 = _behind(dsh1, reduce_w_in.total(dx))
    _, dg_b, dsh1c, dsc1c = _norm_mod_bwd("pre_mix_ctx_bwd", dhc, jnp.zeros_like(dhc), ctx, w["g_pre_mix"], sh1c,
                                          sc1c)
    g["g_pre_mix"] = dg_a + dg_b
    d_ada = jnp.concatenate([dsh1, dsc1, dgt1, dsh2, dsc2, dgt2], axis=1)
    d_ada_c = jnp.concatenate([dsh1c, dsc1c, jnp.zeros((1, 4 * d), F32)], axis=1)
    return loss, dx, g, d_ada, d_ada_c


HBM = pl.BlockSpec(memory_space=pltpu.HBM)
N_DEV, N_CHIP = 8, 4


def _place():
    x, y, c = lax.axis_index("x"), lax.axis_index("y"), lax.axis_index("c")
    return x, y, c, [(1 - x, y), (x, 1 - y), (1 - x, 1 - y)]


def _row_tile(n, mult, cap):
    return max(t for t in range(mult, min(n, cap) + 1, mult) if n % t == 0)


def _ag_small(name, v, after=None):
    follow = () if after is None else (after,)

    def body(v_ref, *rest):
        out_ref, send_sems, recv_sems = rest[len(follow):]
        x, y, c, _ = _place()
        out_ref[4 * x + 2 * y + c] = v_ref[...]

        def peer(r):
            return ((1 - x) if r & 4 else x, (1 - y) if r & 2 else y, (1 - c) if r & 1 else c)

        def copy(r, block):
            px, py, pc = block
            return pltpu.make_async_remote_copy(
                src_ref=v_ref, dst_ref=out_ref.at[4 * px + 2 * py + pc], send_sem=send_sems.at[r - 1],
                recv_sem=recv_sems.at[r - 1], device_id=peer(r), device_id_type=MESH)

        sends = [copy(r, (x, y, c)) for r in range(1, N_DEV)]
        for cp in sends:
            cp.start()
        for r in range(1, N_DEV):
            copy(r, peer(r)).wait_recv()
        for cp in sends:
            cp.wait_send()

    return pl.pallas_call(
        body, name=name, out_shape=jax.ShapeDtypeStruct((N_DEV,) + v.shape, v.dtype),
        in_specs=[pl.BlockSpec(memory_space=pltpu.VMEM)] + [pl.BlockSpec(memory_space=pl.ANY)] * len(follow),
        out_specs=pl.BlockSpec(memory_space=pltpu.VMEM),
        scratch_shapes=[pltpu.SemaphoreType.DMA((N_DEV - 1,)), pltpu.SemaphoreType.DMA((N_DEV - 1,))],
    )(v, *follow)


def _halves(c, rows, mult):
    hr = rows // 2
    return pl.ds(pl.multiple_of(c * hr, mult), hr), pl.ds(pl.multiple_of((1 - c) * hr, mult), hr)


def _add_half(name, g, a, c_idx):
    n_sh, hr, n = a.shape
    tr = _row_tile(hr, 16, 1024)
    nb = hr // tr

    def body(c_ref, g_ref, a_ref, o_ref):
        o_ref[...] = (g_ref[...] + a_ref[...]).astype(o_ref.dtype)

    return pl.pallas_call(
        body, name=name, out_shape=jax.ShapeDtypeStruct(a.shape, BF16),
        grid_spec=pltpu.PrefetchScalarGridSpec(
            num_scalar_prefetch=1, grid=(n_sh, nb),
            in_specs=[pl.BlockSpec((1, tr, n), lambda s, i, c_ref: (s, c_ref[0] * nb + i, 0)),
                      pl.BlockSpec((1, tr, n), lambda s, i, c_ref: (s, i, 0))],
            out_specs=pl.BlockSpec((1, tr, n), lambda s, i, c_ref: (s, i, 0))),
        compiler_params=_cp("parallel", "parallel"),
    )(c_idx, g, a)


def _sum_chips(name, b, c_idx):
    n_sh, hr, n = b.shape
    tr = _row_tile(hr, 16, 1024)
    nb = hr // tr

    def body(c_ref, b0, b1, b2, b3, o_ref):
        o_ref[...] = ((b0[0].astype(F32) + b1[0].astype(F32)) + b2[0].astype(F32)) + b3[0].astype(F32)

    return pl.pallas_call(
        body, name=name, out_shape=jax.ShapeDtypeStruct((2 * hr, n), F32),
        grid_spec=pltpu.PrefetchScalarGridSpec(
            num_scalar_prefetch=1, grid=(nb,),
            in_specs=[pl.BlockSpec((1, tr, n), functools.partial(lambda i, c_ref, k: (k, i, 0), k=k))
                      for k in range(n_sh)],
            out_specs=pl.BlockSpec((tr, n), lambda i, c_ref: (c_ref[0] * nb + i, 0))),
        compiler_params=_cp("parallel"),
    )(c_idx, b, b, b, b)


SEM = pl.BlockSpec(memory_space=pltpu.SEMAPHORE)
ANY = pl.BlockSpec(memory_space=pl.ANY)
DATAFLOW = pltpu.SideEffectType.DATAFLOW_SIDE_EFFECTING


def _remote(src, dst, send_sems, recv_sems, k, to):
    return pltpu.make_async_remote_copy(src_ref=src, dst_ref=dst, send_sem=send_sems.at[k], recv_sem=recv_sems.at[k],
                                        device_id=to, device_id_type=MESH)


def _split_copy(name, src, land_shape, land_dtype, n, plan, after=None):
    after = jnp.zeros((8, LANES), F32) if after is None else after

    def start_body(src_ref, land_ref, after_ref, send_sems, recv_sems, src_thru, land_thru, token):
        for cp in plan(src_ref, land_ref, send_sems, recv_sems)[0]:
            cp.start()
        token[...] = jnp.zeros_like(token)

    sems = pltpu.SemaphoreType.DMA((n,))
    send_sems, recv_sems, src_thru, land_thru, token = pl.pallas_call(
        start_body, name=name + "_start",
        out_shape=(sems, sems, pltpu.HBM(src.shape, src.dtype), pltpu.HBM(land_shape, land_dtype),
                   jax.ShapeDtypeStruct((8, LANES), F32)),
        in_specs=(HBM, HBM, ANY), out_specs=(SEM, SEM, HBM, HBM, pl.BlockSpec(memory_space=pltpu.VMEM)),
        input_output_aliases={0: 2, 1: 3}, compiler_params=pltpu.CompilerParams(has_side_effects=DATAFLOW),
    )(pltpu.with_memory_space_constraint(src, pltpu.HBM),
      pltpu.with_memory_space_constraint(lax.empty(land_shape, land_dtype), pltpu.HBM), after)

    def wait(*after):
        def wait_body(src_ref, land_ref, send_sems, recv_sems, *rest):
            sent, received = plan(src_ref, land_ref, send_sems, recv_sems)
            for cp in sent:
                cp.wait_send()
            for cp in received:
                cp.wait_recv()

        return pl.pallas_call(
            wait_body, name=name + "_wait",
            out_shape=(pltpu.HBM(src.shape, src.dtype), pltpu.HBM(land_shape, land_dtype)),
            in_specs=(HBM, HBM, SEM, SEM) + (ANY,) * len(after), out_specs=(HBM, HBM),
            input_output_aliases={0: 0, 1: 1}, compiler_params=pltpu.CompilerParams(has_side_effects=DATAFLOW),
        )(src_thru, land_thru, send_sems, recv_sems, *after)

    return token, wait


def _split_gather(name, shards, after):
    k, n, plan = len(shards), 3 * len(shards), _plan_gather

    def start_body(*refs):
        for cp in plan(refs[:k], refs[k:2 * k], refs[2 * k + 1], refs[2 * k + 2])[0]:
            cp.start()
        refs[-1][...] = jnp.zeros_like(refs[-1])

    sems = pltpu.SemaphoreType.DMA((n,))
    bufs = [pltpu.HBM(s.shape, s.dtype) for s in shards] + [pltpu.HBM((N_CHIP,) + s.shape, s.dtype) for s in shards]
    hbm = lambda t: pltpu.with_memory_space_constraint(t, pltpu.HBM)
    outs = pl.pallas_call(
        start_body, name=name + "_start", out_shape=(sems, sems, *bufs, jax.ShapeDtypeStruct((8, LANES), F32)),
        in_specs=(HBM,) * (2 * k) + (ANY,),
        out_specs=(SEM, SEM) + (HBM,) * (2 * k) + (pl.BlockSpec(memory_space=pltpu.VMEM),),
        input_output_aliases={i: 2 + i for i in range(2 * k)},
        compiler_params=pltpu.CompilerParams(has_side_effects=DATAFLOW),
    )(*[hbm(s) for s in shards], *[hbm(lax.empty((N_CHIP,) + s.shape, s.dtype)) for s in shards], after)
    send_sems, recv_sems, thru, token = outs[0], outs[1], outs[2:2 + 2 * k], outs[-1]

    def wait(*after):
        def wait_body(*refs):
            sent, received = plan(refs[:k], refs[k:2 * k], refs[2 * k], refs[2 * k + 1])
            for cp in sent:
                cp.wait_send()
            for cp in received:
                cp.wait_recv()

        res = pl.pallas_call(
            wait_body, name=name + "_wait", out_shape=tuple(bufs),
            in_specs=(HBM,) * (2 * k) + (SEM, SEM) + (ANY,) * len(after), out_specs=(HBM,) * (2 * k),
            input_output_aliases={i: i for i in range(2 * k)},
            compiler_params=pltpu.CompilerParams(has_side_effects=DATAFLOW),
        )(*thru, send_sems, recv_sems, *after)
        return res[:k], res[k:]

    return token, wait


def _behind(x, token):
    return x + token[0, 0]


def _plan_gather(src_refs, land_refs, send_sems, recv_sems):
    x, y, c, chips = _place()
    pairs = list(enumerate(zip(src_refs, land_refs)))
    sent = [_remote(s, l.at[2 * x + y], send_sems, recv_sems, 3 * i + j, (px, py, c))
            for i, (s, l) in pairs for j, (px, py) in enumerate(chips)]
    received = [_remote(s, l.at[2 * px + py], send_sems, recv_sems, 3 * i + j, (px, py, c))
                for i, (s, l) in pairs for j, (px, py) in enumerate(chips)]
    return sent, received


def _plan_swap(src_ref, land_ref, send_sems, recv_sems):
    x, y, c, _ = _place()
    _, other_half = _halves(c, src_ref.shape[1], 8)
    cp = _remote(src_ref.at[pl.ds(0, src_ref.shape[0]), other_half], land_ref, send_sems, recv_sems, 0, (x, y, 1 - c))
    return [cp], [cp]


def _plan_scatter(src_ref, land_ref, send_sems, recv_sems):
    x, y, c, chips = _place()
    sent = [_remote(src_ref.at[2 * px + py], land_ref.at[2 * x + y], send_sems, recv_sems, j, (px, py, c))
            for j, (px, py) in enumerate(chips)]
    received = [_remote(src_ref.at[2 * px + py], land_ref.at[2 * px + py], send_sems, recv_sems, j, (px, py, c))
                for j, (px, py) in enumerate(chips)]
    return sent, received


def _plan_share(src_ref, land_ref, send_sems, recv_sems):
    x, y, c, _ = _place()
    mine_half, other_half = _halves(c, src_ref.shape[0], 8)
    return ([_remote(src_ref.at[mine_half], src_ref.at[mine_half], send_sems, recv_sems, 0, (x, y, 1 - c))],
            [_remote(src_ref.at[other_half], src_ref.at[other_half], send_sems, recv_sems, 0, (x, y, 1 - c))])


class _GatherBehind:
    def __init__(self, name, shards, chip, after):
        self.chip = chip
        self.token, self.wait = _split_gather(name, shards, after)

    def result(self, *after):
        shards, lands = self.wait(*after)
        return [lax.dynamic_update_slice(land, shard[None], (self.chip, 0, 0)) for shard, land in zip(shards, lands)]


class _ReduceBehind:
    def __init__(self, name, chip, c_idx):
        self.name, self.chip, self.c_idx = name, chip, c_idx

    def start_slab(self, g):
        n_sh, rows, n = g.shape
        token, self.wait = _split_copy(self.name + "_swap", g, (n_sh, rows // 2, n), g.dtype, 1, _plan_swap)
        return token

    def pair(self, after):
        g, a = self.wait(after)
        h = _add_half(self.name + "_pair", g, a, self.c_idx)
        token, self.wait = _split_copy(self.name + "_scatter", h, h.shape, h.dtype, 3, _plan_scatter)
        return token

    def total(self, after):
        h, b = self.wait(after)
        b = lax.dynamic_update_slice(b, lax.dynamic_slice_in_dim(h, self.chip, 1, axis=0), (self.chip, 0, 0))
        f = _sum_chips(self.name + "_sum", b, self.c_idx)
        token, self.wait = _split_copy(self.name + "_share", f, (8, LANES), f.dtype, 1, _plan_share)
        return token

    def result(self, after):
        return self.wait(after)[0]


class _ReduceColsBehind(_ReduceBehind):
    def start(self, g_padded):
        g = _unpack_w_in_grad(g_padded)
        n = g.shape[1] // N_CHIP
        return self.start_slab(jnp.stack([g[:, k * n:(k + 1) * n] for k in range(N_CHIP)]))


def _f_adamw(w, g, m, v):
    m = ADAM_B1 * m + (1.0 - ADAM_B1) * g
    v = ADAM_B2 * v + (1.0 - ADAM_B2) * (g * g)
    m_hat = m / (1.0 - ADAM_B1 ** ADAM_STEP)
    v_hat = v / (1.0 - ADAM_B2 ** ADAM_STEP)
    return -ADAM_LR * (m_hat / (jnp.sqrt(v_hat) + ADAM_EPS) + ADAM_WD * w), m, v


def _adamw(name, w, g, m, v):
    rows, n = w.shape
    return _rowwise(name, lambda w, g, m, v: (_f_adamw(w, g, m, v), ()), rows, [(t, n, 0) for t in (w, g, m, v)], [],
                    [(n, F32)] * 3, [], tm=_row_tile(rows, 8, 256))


def _adamw_many(name, ws, gs, ms, vs):
    k = len(ws)

    def body(*refs):
        ins, outs = refs[:4 * k], refs[4 * k:]
        for i in range(k):
            res = _f_adamw(ins[i][...], ins[k + i][...], ins[2 * k + i][...], ins[3 * k + i][...])
            for j in range(3):
                outs[j * k + i][...] = res[j]

    out = pl.pallas_call(body, name=name, out_shape=[jax.ShapeDtypeStruct(w.shape, F32) for w in ws] * 3)(
        *ws, *gs, *ms, *vs)
    return out[:k], out[k:2 * k], out[2 * k:]


def _pack_rows(parts):
    rows = []
    for t in parts:
        t = t.reshape(-1)
        rows.append(jnp.pad(t, (0, -t.shape[0] % LANES)).reshape(-1, LANES))
    out = jnp.concatenate(rows, axis=0)
    return jnp.pad(out, ((0, -out.shape[0] % 8), (0, 0)))


def _unpack_rows(packed, shapes):
    out, r = [], 0
    for shp in shapes:
        n = int(np.prod(shp))
        nr = -(-n // LANES)
        out.append(packed[r:r + nr].reshape(-1)[:n].reshape(shp))
        r += nr
    return out


def _sum_blocks(name, g):
    def body(g_ref, o_ref):
        acc = g_ref[0]
        for k in range(1, g.shape[0]):
            acc = acc + g_ref[k]
        o_ref[...] = acc

    return pl.pallas_call(body, name=name, out_shape=jax.ShapeDtypeStruct(g.shape[1:], F32))(g)


def _silu(t):
    return t * _sigmoid(t)


def _ada_fwd(cc, w_ada):
    n = w_ada.shape[1]
    tn = _row_tile(n, LANES, 512)

    def body(cc_ref, w_ref, o_ref):
        o_ref[...] = _nn(_silu(cc_ref[...]), w_ref[...])

    return pl.pallas_call(
        body, name="ada_fwd", grid=(n // tn,), out_shape=jax.ShapeDtypeStruct((cc.shape[0], n), F32),
        in_specs=[pl.BlockSpec(cc.shape, lambda j: (0, 0)), pl.BlockSpec((w_ada.shape[0], tn), lambda j: (0, j))],
        out_specs=pl.BlockSpec((cc.shape[0], tn), lambda j: (0, j)), compiler_params=_cp("parallel"),
    )(cc, w_ada)


def _ada_bwd(cc, dm, w_ada):
    d, n = w_ada.shape
    tn = _row_tile(n, LANES, 512)

    def body(cc_ref, dm_ref, w_ref, gw_ref, ds_ref):
        @pl.when(pl.program_id(0) == 0)
        def _():
            ds_ref[...] = jnp.zeros_like(ds_ref)

        gw_ref[...] = _raw_dot("tn", _silu(cc_ref[...]), dm_ref[...], True)
        ds_ref[...] += _raw_dot("nt", dm_ref[...], w_ref[...], False)

    return pl.pallas_call(
        body, name="ada_bwd", grid=(n // tn,),
        out_shape=[jax.ShapeDtypeStruct((d, n), F32), jax.ShapeDtypeStruct(cc.shape, F32)],
        in_specs=[pl.BlockSpec(cc.shape, lambda j: (0, 0)), pl.BlockSpec((cc.shape[0], tn), lambda j: (0, j)),
                  pl.BlockSpec((d, tn), lambda j: (0, j))],
        out_specs=[pl.BlockSpec((d, tn), lambda j: (0, j)), pl.BlockSpec(cc.shape, lambda j: (0, 0))],
        compiler_params=_cp("arbitrary"),
    )(cc, dm, w_ada)


def _c_ctx_grad(parts, c_ctx):
    def body(p_ref, c_ref, o_ref):
        ds = ((p_ref[0] + p_ref[1]) + p_ref[2]) + p_ref[3]
        _, vjp = jax.vjp(_silu, c_ref[...])
        o_ref[...] = vjp(ds)[0]

    return pl.pallas_call(body, name="c_ctx_grad", out_shape=jax.ShapeDtypeStruct(c_ctx.shape, F32))(parts, c_ctx)


def kernel(x, c, ctx, c_ctx, w_ada, b_ada, g_pre_mix, g_post_mix, g_pre_ffn, g_post_ffn, w_in, attn_sink, w_gate_fwd, b_gate_fwd, w_gate_bwd, b_gate_bwd, g_gla_norm, w_out, w_ffn_in, w_ffn_out, loss_target, m_c_ctx, m_w_ada, m_b_ada, m_g_pre_mix, m_g_post_mix, m_g_pre_ffn, m_g_post_ffn, m_w_in, m_attn_sink, m_w_gate_fwd, m_b_gate_fwd, m_w_gate_bwd, m_b_gate_bwd, m_g_gla_norm, m_w_out, m_w_ffn_in, m_w_ffn_out, v_c_ctx, v_w_ada, v_b_ada, v_g_pre_mix, v_g_post_mix, v_g_pre_ffn, v_g_post_ffn, v_w_in, v_attn_sink, v_w_gate_fwd, v_b_gate_fwd, v_w_gate_bwd, v_b_gate_bwd, v_g_gla_norm, v_w_out, v_w_ffn_in, v_w_ffn_out):
    xi, yi, ci = lax.axis_index("x"), lax.axis_index("y"), lax.axis_index("c")
    dev, chip = 4 * xi + 2 * yi + ci, 2 * xi + yi
    c_idx = jnp.reshape(ci, (1,)).astype(jnp.int32)
    d = x.shape[-1]
    n_ada, n_in, n_f = w_ada.shape[-1], w_in.shape[-1], w_ffn_in.shape[-1]
    r_out, r_f = w_out.shape[1], w_ffn_out.shape[1]
    n_gate = w_gate_fwd.shape[-1]
    by_chip = lambda t: t[0::2]

    rc = -(-d // LANES)
    g1 = _ag_small("gather_cond", _pack_rows([c[0], w_gate_fwd[0], w_gate_bwd[0]]))
    c_all = g1[:, :rc].reshape(N_DEV, -1)[:, :d]
    gr = GATE_RANK * n_gate // LANES
    gate_full = lambda off: jnp.transpose(by_chip(g1)[:, off:off + gr].reshape(N_CHIP, GATE_RANK, n_gate),
                                          (1, 0, 2)).reshape(GATE_RANK, N_CHIP * n_gate)
    wgf, wgb = gate_full(rc), gate_full(rc + gr)
    cc = jnp.concatenate([c_all, c_ctx[None, :], jnp.zeros((7, d), F32)], axis=0)

    g2 = _ag_small("gather_ada", _ada_fwd(cc, w_ada[0]).reshape(-1, LANES))
    ada_all = jnp.transpose(by_chip(g2).reshape(N_CHIP, 16, n_ada), (1, 0, 2)).reshape(16, N_CHIP * n_ada) + b_ada
    first = _GatherBehind("gather_w_in", [w_in[0].astype(BF16)], chip, g2)
    late_shards = [w_out[0].astype(BF16), jnp.transpose(w_ffn_in[0]).astype(BF16), w_ffn_out[0].astype(BF16)]
    late = []

    def first_weights(*after):
        w_in_g, = first.result(*after, *late_shards)
        late.append(_GatherBehind("gather_late", late_shards, chip, w_in_g))
        return _pack_w_in(jnp.concatenate([w_in_g[k] for k in range(N_CHIP)], axis=1)), late[0].token

    def late_weights(after):
        return [t.reshape(-1, d) for t in late[0].result(after)]

    ada_all = _behind(ada_all, first.token)
    ada = lax.dynamic_slice(ada_all, (dev, 0), (1, N_CHIP * n_ada))
    ada_c = ada_all[N_DEV:N_DEV + 1]

    w = _prep_gate_weights(wgf, wgb)
    w.update(w_in=first_weights, g_pre_mix=g_pre_mix, g_post_mix=g_post_mix, g_pre_ffn=g_pre_ffn, g_post_ffn=g_post_ffn,
             attn_sink=attn_sink, b_gate_fwd=b_gate_fwd, b_gate_bwd=b_gate_bwd, g_gla_norm=g_gla_norm)

    reduce_behind = _ReduceBehind("reduce_late", chip, c_idx)
    reduce_w_in = _ReduceColsBehind("reduce_w_in", chip, c_idx)
    loss_lanes, grad_x, g, d_ada, d_ada_c = _local_step(x[0], ctx[0], loss_target[0], ada, ada_c, w, late_weights,
                                                        reduce_behind, reduce_w_in)

    small = ("g_pre_mix", "g_post_mix", "g_pre_ffn", "g_post_ffn", "attn_sink", "b_gate_fwd", "b_gate_bwd",
             "g_gla_norm", "w_gate_fwd", "w_gate_bwd")
    shapes = [(1, 6 * d)] * 2 + [g[n].shape for n in small] + [(1, LANES)]
    g3 = _ag_small("gather_small_grads", _pack_rows([d_ada, d_ada_c] + [g[n] for n in small] + [loss_lanes]))
    tot = dict(zip(("d_ada", "d_ada_c") + small + ("loss",),
                   _unpack_rows(_sum_blocks("sum_small_grads", g3), shapes)))
    r_ada = 6 * d // LANES
    dm = jnp.concatenate([g3[:, :r_ada].reshape(N_DEV, 6 * d), tot["d_ada_c"], jnp.zeros((7, 6 * d), F32)], axis=0)
    grads = {n: tot[n] for n in small[:8]}
    grads["b_ada"] = _sum_blocks("sum_b_ada", dm.reshape(16, r_ada, LANES)).reshape(1, 6 * d)
    grads["w_gate_fwd"] = lax.dynamic_slice(tot["w_gate_fwd"], (0, chip * n_gate), (GATE_RANK, n_gate))[None]
    grads["w_gate_bwd"] = lax.dynamic_slice(tot["w_gate_bwd"], (0, chip * n_gate), (GATE_RANK, n_gate))[None]
    gw_ada, dsc = _ada_bwd(cc, lax.dynamic_slice(dm, (0, chip * n_ada), (16, n_ada)), w_ada[0])
    grads["w_ada"] = gw_ada[None]
    g4 = _ag_small("gather_c_ctx", _pack_rows([dsc[N_DEV]]))
    grads["c_ctx"] = _c_ctx_grad(by_chip(g4), _pack_rows([c_ctx])).reshape(-1)[:d]

    grads["w_in"] = reduce_w_in.result(g4)[None]
    part = lambda n: g["late"][g["late_at"][n]:g["late_at"][n] + g["late_rows"][n]]
    grads["w_ffn_in"], grads["w_ffn_out"], grads["w_out"] = (jnp.transpose(part("w_ffn_in_t"))[None],
                                                            part("w_ffn_out")[None], part("w_out")[None])

    names = ("c_ctx", "w_ada", "b_ada", "g_pre_mix", "g_post_mix", "g_pre_ffn", "g_post_ffn", "w_in", "attn_sink",
             "w_gate_fwd", "b_gate_fwd", "w_gate_bwd", "b_gate_bwd", "g_gla_norm", "w_out", "w_ffn_in", "w_ffn_out")
    weights = dict(zip(names, (c_ctx, w_ada, b_ada, g_pre_mix, g_post_mix, g_pre_ffn, g_post_ffn, w_in, attn_sink,
                               w_gate_fwd, b_gate_fwd, w_gate_bwd, b_gate_bwd, g_gla_norm, w_out, w_ffn_in,
                               w_ffn_out)))
    m_in = dict(zip(names, (m_c_ctx, m_w_ada, m_b_ada, m_g_pre_mix, m_g_post_mix, m_g_pre_ffn, m_g_post_ffn, m_w_in,
                            m_attn_sink, m_w_gate_fwd, m_b_gate_fwd, m_w_gate_bwd, m_b_gate_bwd, m_g_gla_norm,
                            m_w_out, m_w_ffn_in, m_w_ffn_out)))
    v_in = dict(zip(names, (v_c_ctx, v_w_ada, v_b_ada, v_g_pre_mix, v_g_post_mix, v_g_pre_ffn, v_g_post_ffn, v_w_in,
                            v_attn_sink, v_w_gate_fwd, v_b_gate_fwd, v_w_gate_bwd, v_b_gate_bwd, v_g_gla_norm,
                            v_w_out, v_w_ffn_in, v_w_ffn_out)))
    large = ("w_ada", "w_in", "w_out", "w_ffn_in", "w_ffn_out")
    tiny = tuple(n for n in names if n not in large)
    delta, new_m, new_v = {}, {}, {}
    for n in large:
        dl, nm, nv = _adamw("adamw_" + n, weights[n][0], grads[n][0], m_in[n][0], v_in[n][0])
        delta[n], new_m[n], new_v[n] = dl[None], nm[None], nv[None]
    for n in tiny:
        grads[n] = grads[n].reshape(weights[n].shape)
    as_rows = lambda t: t.reshape(-1, t.shape[-1])
    res = _adamw_many("adamw_small", *[[as_rows(t[n]) for n in tiny] for t in (weights, grads, m_in, v_in)])
    for out, vals in zip((delta, new_m, new_v), res):
        out.update({n: val.reshape(weights[n].shape) for n, val in zip(tiny, vals)})

    return (tot["loss"][0, 0], grad_x[None], *[grads[n] for n in names], *[delta[n] for n in names], *[new_m[n] for n in names],
            *[new_v[n] for n in names])
```

```python
import functools

import jax
import jax.numpy as jnp
import numpy as np
from jax import lax
from jax.experimental import pallas as pl
from jax.experimental.pallas import tpu as pltpu

F32 = jnp.float32
BF16 = jnp.bfloat16
MESH = pl.DeviceIdType.MESH

HEAD_DIM = 64
ATT_HEADS = 8
ATT_KV_HEADS = 2
ATT_GROUP = ATT_HEADS // ATT_KV_HEADS
WINDOW = 128
BLOCK = 128
GRID_W = 64
ROPE_BASE = 10000.0
GLA_HEADS = 8
GLA_DK = 32
GLA_DV = 64
GLA_CHUNK = 64
GATE_RANK = 16
GATE_TAU = 16.0
NEG_INF = -1e30
QW = ATT_HEADS * HEAD_DIM
KVW = ATT_KV_HEADS * HEAD_DIM
GKW = GLA_HEADS * GLA_DK
GVW = GLA_HEADS * GLA_DV
IN_COLS = QW + 2 * KVW + 2 * GKW + 2 * GVW + 2 * GATE_RANK
LANES = 128
IN_PAD = IN_COLS + LANES - 2 * GATE_RANK
C_Q, C_GV, C_GG = 0, QW, QW + GVW
C_K = C_GG + GVW
C_V = C_K + KVW
C_GQ = C_V + KVW
C_GK = C_GQ + GKW
C_Z = C_GK + GKW
MIX = QW + GVW

ADAM_LR, ADAM_B1, ADAM_B2, ADAM_EPS, ADAM_WD, ADAM_STEP = 0.001, 0.9, 0.999, 1e-08, 0.01, 10

VMEM_LIMIT = 56 * 1024 * 1024


def _cp(*sem):
    return pltpu.CompilerParams(dimension_semantics=sem, vmem_limit_bytes=VMEM_LIMIT)


def _pick(n, cands):
    for t in cands:
        if n % t == 0:
            return t
    return n


_DIMS = {"nn": (((1,), (0,)), ((), ())), "nt": (((1,), (1,)), ((), ())), "tn": (((0,), (0,)), ((), ()))}


def _raw_dot(mode, a, b, hi):
    dot = lambda u, v: lax.dot_general(u, v, _DIMS[mode], preferred_element_type=F32)
    if hi:
        a, b = a.astype(F32), b.astype(F32)
        a_hi, b_hi = a.astype(BF16), b.astype(BF16)
        a_lo, b_lo = (a - a_hi.astype(F32)).astype(BF16), (b - b_hi.astype(F32)).astype(BF16)
        return dot(a_hi, b_hi) + (dot(a_lo, b_hi) + dot(a_hi, b_lo))
    return dot(a.astype(BF16), b.astype(BF16))


def _make_dot(mode, hi):
    @jax.custom_vjp
    def dot(a, b):
        return _raw_dot(mode, a, b, hi)

    def fwd(a, b):
        return _raw_dot(mode, a, b, hi), (a, b)

    def bwd(res, dc):
        a, b = res
        if mode == "nn":
            return _raw_dot("nt", dc, b, hi), _raw_dot("tn", a, dc, hi)
        if mode == "nt":
            return _raw_dot("nn", dc, b, hi), _raw_dot("tn", dc, a, hi)
        return _raw_dot("nt", b, dc, hi), _raw_dot("nn", a, dc, hi)

    dot.defvjp(fwd, bwd)
    return dot


_nn, _nt, _tn = _make_dot("nn", False), _make_dot("nt", False), _make_dot("tn", False)
_nn_hi = _make_dot("nn", True)


MM_VMEM_BUDGET = 44 * 1024 * 1024


def _halvings(n):
    out = [n]
    while out[-1] % (2 * LANES) == 0:
        out.append(out[-1] // 2)
    return out


def _mm_tiles(mode, m, n, k, a_bytes, b_bytes, o_bytes, init_bytes=0):
    tms = [t for t in dict.fromkeys((m, m // 2, m // 4, 2048, 1024, 512, 256, 128))
           if m % t == 0 and t % (LANES if mode == "tn" else 16) == 0 and t <= 4096] or [m]
    if mode == "tn":
        fits = [(k // tk + 0.5 * (m // tm), tm, tk)
                for tk in (4096, 2048, 1024, 512, 256, 128) if k % tk == 0 for tm in tms
                if 2 * (tk * tm * a_bytes + tk * n * b_bytes + tm * n * (o_bytes + init_bytes)) <= MM_VMEM_BUDGET]
        if fits:
            _, tm, tk = min(fits)
            return tm, n, tk
    tks = ([t for t in (512, 256, 128) if k % t == 0] or [k]) if mode == "tn" else _halvings(k)
    for tn in _halvings(n):
        for tk in tks:
            for tm in tms:
                acc = tm * tn * 4 if (k // tk > 1 and o_bytes != 4) else 0
                tiles = tm * tk * a_bytes + tk * tn * b_bytes + tm * tn * (o_bytes + init_bytes)
                if 2 * tiles + acc <= MM_VMEM_BUDGET:
                    return tm, tn, tk
    return tms[-1], _halvings(n)[-1], tks[-1]


def _mm(name, a, b, mode, out_dtype=F32, init=None, after=None):
    follow = () if after is None else (after,)
    if mode == "nn":
        (m, k), n = a.shape, b.shape[1]
    elif mode == "nt":
        (m, k), n = a.shape, b.shape[0]
    else:
        (k, m), n = a.shape, b.shape[1]
    tm, tn, tk = _mm_tiles(mode, m, n, k, a.dtype.itemsize, b.dtype.itemsize, jnp.dtype(out_dtype).itemsize,
                           0 if init is None else 4)
    nk = k // tk
    use_acc = nk > 1 and out_dtype != F32

    inits = () if init is None else (init,)

    def body(a_ref, b_ref, *rest):
        rest = rest[:len(inits)] + rest[len(inits) + len(follow):]
        o_ref, acc = rest[len(inits)], rest[len(inits) + 1:]
        part = _raw_dot(mode, a_ref[...], b_ref[...], False)
        first = lambda: part + rest[0][...] if inits else part
        if nk == 1:
            o_ref[...] = first().astype(o_ref.dtype)
            return
        acc_ref = acc[0] if use_acc else o_ref
        kk = pl.program_id(2)

        @pl.when(kk == 0)
        def _():
            acc_ref[...] = first()

        @pl.when(kk > 0)
        def _():
            acc_ref[...] += part

        if use_acc:
            @pl.when(kk == nk - 1)
            def _():
                o_ref[...] = acc_ref[...].astype(o_ref.dtype)

    if mode == "nn":
        a_spec = pl.BlockSpec((tm, tk), lambda i, j, kk: (i, kk))
        b_spec = pl.BlockSpec((tk, tn), lambda i, j, kk: (kk, j))
    elif mode == "nt":
        a_spec = pl.BlockSpec((tm, tk), lambda i, j, kk: (i, kk))
        b_spec = pl.BlockSpec((tn, tk), lambda i, j, kk: (j, kk))
    else:
        a_spec = pl.BlockSpec((tk, tm), lambda i, j, kk: (kk, i))
        b_spec = pl.BlockSpec((tk, tn), lambda i, j, kk: (kk, j))
    return pl.pallas_call(
        body, name=name, grid=(m // tm, n // tn, nk),
        in_specs=[a_spec, b_spec] + [pl.BlockSpec((tm, tn), lambda i, j, kk: (i, j))] * len(inits)
        + [pl.BlockSpec(memory_space=pl.ANY)] * len(follow),
        out_specs=pl.BlockSpec((tm, tn), lambda i, j, kk: (i, j)),
        out_shape=jax.ShapeDtypeStruct((m, n), out_dtype),
        scratch_shapes=[pltpu.VMEM((tm, tn), F32)] if use_acc else [],
        compiler_params=_cp("parallel", "parallel", "arbitrary"),
    )(a, b, *inits, *follow)


def _slab_layout(rows):
    offsets, at = [], 0
    for r in rows:
        at = -(-at // r) * r
        offsets.append(at)
        at += r
    return offsets, -(-at // 32) * 32


def _slab_zero_gaps(name, shape, rows, offsets):
    gaps = [(o + r, nxt) for o, r, nxt in zip(offsets, rows, offsets[1:] + [shape[1]]) if nxt > o + r]
    slab = None
    for i, (lo, hi) in enumerate(gaps):
        step = int(np.gcd(lo, hi - lo))

        def body(*refs):
            refs[-1][...] = jnp.zeros_like(refs[-1])

        slab = pl.pallas_call(
            body, name=f"{name}_{i}", grid=(shape[0], (hi - lo) // step), out_shape=jax.ShapeDtypeStruct(shape, F32),
            in_specs=[] if slab is None else [pl.BlockSpec(memory_space=pl.ANY)],
            out_specs=pl.BlockSpec((1, step, shape[2]), functools.partial(lambda k, j, b: (k, b + j, 0), b=lo // step)),
            input_output_aliases={} if slab is None else {0: 0}, compiler_params=_cp("parallel", "parallel"),
        )(*(() if slab is None else (slab,)))
    return slab


def _dw_into_slab(name, a, b, slab, shape, at):
    (k, m), n = a.shape, b.shape[1]
    r = m // N_CHIP
    fits = [(k // tk + 0.5 * (m // tm), tm, tk)
            for tk in (4096, 2048, 1024, 512, 256, 128) if k % tk == 0 for tm in (m, m // 2, r) if tm % LANES == 0
            if 2 * (tk * tm * a.dtype.itemsize + tk * n * b.dtype.itemsize + tm * n * 4) <= MM_VMEM_BUDGET]
    _, tm, tk = min(fits)
    per, nk = tm // r, k // tk

    def body(a_ref, b_ref, *rest):
        o_ref = rest[-1]
        part = _raw_dot("tn", a_ref[...], b_ref[...], False).reshape(o_ref.shape)
        if nk == 1:
            o_ref[...] = part
            return
        kk = pl.program_id(1)

        @pl.when(kk == 0)
        def _():
            o_ref[...] = part

        @pl.when(kk > 0)
        def _():
            o_ref[...] += part

    prev = () if slab is None else (slab,)
    return pl.pallas_call(
        body, name=name, grid=(m // tm, nk), out_shape=jax.ShapeDtypeStruct(shape, F32),
        in_specs=[pl.BlockSpec((tk, tm), lambda i, kk: (kk, i)), pl.BlockSpec((tk, n), lambda i, kk: (kk, 0))]
        + [pl.BlockSpec(memory_space=pl.ANY)] * len(prev),
        out_specs=pl.BlockSpec((per, r, n), lambda i, kk: (i, at // r, 0)),
        input_output_aliases={2: 0} if prev else {}, compiler_params=_cp("parallel", "arbitrary"),
    )(a, b, *prev)


def _rowwise(name, fn, rows, row_ins, full_ins, row_outs, acc_outs, tm=None):
    tm = tm or _pick(rows, (512, 256, 128))
    n_r, n_f, n_o, n_a = len(row_ins), len(full_ins), len(row_outs), len(acc_outs)

    def body(*refs):
        ins, outs = refs[:n_r + n_f], refs[n_r + n_f:]
        vals = [r[...].astype(F32) for r in ins]
        ro, ao = fn(*vals)
        for r, val in zip(outs[:n_o], ro):
            r[...] = val.astype(r.dtype)
        if n_a:
            @pl.when(pl.program_id(0) == 0)
            def _():
                for r in outs[n_o:]:
                    r[...] = jnp.zeros_like(r)

            for r, val in zip(outs[n_o:], ao):
                r[...] += val

    in_specs = [pl.BlockSpec((tm, w), functools.partial(lambda i, cb: (i, cb), cb=cb)) for _, w, cb in row_ins]
    in_specs += [pl.BlockSpec(a.shape, lambda i: (0, 0)) for a in full_ins]
    out_specs = [pl.BlockSpec((tm, w), lambda i: (i, 0)) for w, _ in row_outs]
    out_specs += [pl.BlockSpec(s, lambda i: (0, 0)) for s in acc_outs]
    out_shape = [jax.ShapeDtypeStruct((rows, w), dt) for w, dt in row_outs]
    out_shape += [jax.ShapeDtypeStruct(s, F32) for s in acc_outs]
    return pl.pallas_call(
        body, name=name, grid=(rows // tm,), in_specs=in_specs, out_specs=out_specs, out_shape=out_shape,
        compiler_params=_cp("arbitrary" if n_a else "parallel"),
    )(*[a for a, _, _ in row_ins], *full_ins)


def _rn(x):
    return x * lax.rsqrt(jnp.mean(x * x, axis=-1, keepdims=True) + 1e-6)


def _sigmoid(t):
    return 1.0 / (1.0 + jnp.exp(-t))


def _f_norm_mod(x, g, sh, sc):
    return _rn(x) * g * (1.0 + sc) + sh


def _f_post_res(xr, y, g, gate):
    return xr + gate * (_rn(y) * g)


@jax.custom_vjp
def _f_swiglu(g, u):
    return g * _sigmoid(g) * u


def _f_swiglu_fwd(g, u):
    s = _sigmoid(g)
    return g * s * u, (g, u, s)


def _f_swiglu_bwd(res, da):
    g, u, s = res
    gs = g * s
    return da * u * (s + gs * (1.0 - s)), da * gs


_f_swiglu.defvjp(_f_swiglu_fwd, _f_swiglu_bwd)


def _logsig(u):
    return jnp.minimum(u, 0.0) - jnp.log(1.0 + jnp.exp(-jnp.abs(u)))


def _f_gate(z, wf, wb, bf, bb):
    return _logsig(_nn(z, wf) + bf) / GATE_TAU, _logsig(_nn(z, wb) + bb) / GATE_TAU


def _f_gla_out(of, ob, gg, gt, bd):
    o = of + ob
    ms = _nn_hi(o * o, bd)
    return o * lax.rsqrt(ms + 1e-6) * gt * (gg * _sigmoid(gg))


def _norm_mod(name, x, g, sh, sc):
    rows, d = x.shape
    return _rowwise(name, lambda x, g, sh, sc: ((_f_norm_mod(x, g, sh, sc),), ()), rows,
                    [(x, d, 0)], [g, sh, sc], [(d, BF16)], [])[0]


def _norm_mod_bwd(name, dh, dres, x, g, sh, sc):
    rows, d = x.shape

    def fn(dh, dres, x, g, sh, sc):
        _, vjp = jax.vjp(_f_norm_mod, x, g, sh, sc)
        dx, dg, dsh, dsc = vjp(dh)
        return (dx + dres,), (dg, dsh, dsc)

    return _rowwise(name, fn, rows, [(dh, d, 0), (dres, d, 0), (x, d, 0)], [g, sh, sc], [(d, F32)],
                    [(1, d)] * 3)


def _post_res_norm_mod(name, xr, y, g_post, gate, g_pre, sh, sc):
    rows, d = xr.shape

    def fn(xr, y, g_post, gate, g_pre, sh, sc):
        x1 = _f_post_res(xr, y, g_post, gate)
        return (x1, _f_norm_mod(x1, g_pre, sh, sc)), ()

    return _rowwise(name, fn, rows, [(xr, d, 0), (y, d, 0)], [g_post, gate, g_pre, sh, sc], [(d, F32), (d, BF16)], [])


def _norm_mod_post_res_bwd(name, dh, dres, x1, y, g_pre, sh, sc, g_post, gate):
    rows, d = x1.shape

    def fn(dh, dres, x1, y, g_pre, sh, sc, g_post, gate):
        _, vjp_norm = jax.vjp(_f_norm_mod, x1, g_pre, sh, sc)
        dx1, dg_pre, dsh, dsc = vjp_norm(dh)
        dx1 = dx1 + dres
        _, vjp_res = jax.vjp(lambda y, g, gate: _f_post_res(jnp.zeros_like(y), y, g, gate), y, g_post, gate)
        dy, dg_post, dgate = vjp_res(dx1)
        return (dx1, dy), (dg_pre, dsh, dsc, dg_post, dgate)

    return _rowwise(name, fn, rows, [(dh, d, 0), (dres, d, 0), (x1, d, 0), (y, d, 0)], [g_pre, sh, sc, g_post, gate],
                    [(d, F32), (d, BF16)], [(1, d)] * 5, tm=_pick(rows, (256, 128)))


def _post_res_loss(name, xr, y, g, gate, target):
    rows, d = xr.shape

    def fn(xr, y, target, g, gate):
        x2, vjp = jax.vjp(lambda y, g, gate: _f_post_res(xr, y, g, gate), y, g, gate)
        diff = x2 - target
        part = 0.5 * jnp.sum(jnp.mean(diff * diff, axis=-1, keepdims=True), axis=0, keepdims=True)
        dx2 = diff * (1.0 / d)
        dy, dg, dgate = vjp(dx2)
        return (dx2, dy), (jnp.broadcast_to(part, (1, LANES)), dg, dgate)

    return _rowwise(name, fn, rows, [(xr, d, 0), (y, d, 0), (target, d, 0)], [g, gate], [(d, F32), (d, BF16)],
                    [(1, LANES), (1, d), (1, d)])


def _mm_rows(name, a, b, mode, fn, extras, outs):
    m, k = a.shape
    tm = _pick(m, (256, 128))

    def body(a_ref, b_ref, *rest):
        tiles = fn(_raw_dot(mode, a_ref[...], b_ref[...], False), *[e[...] for e in rest[:len(extras)]])
        for r, val in zip(rest[len(extras):], tiles):
            r[...] = val.astype(r.dtype)

    row = lambda w: pl.BlockSpec((tm, w), lambda i: (i, 0))
    return pl.pallas_call(
        body, name=name, grid=(m // tm,),
        in_specs=[row(k), pl.BlockSpec(b.shape, lambda i: (0, 0))] + [row(e.shape[1]) for e in extras],
        out_specs=[row(w) for w, _ in outs], out_shape=[jax.ShapeDtypeStruct((m, w), dt) for w, dt in outs],
        compiler_params=_cp("parallel"),
    )(a, b, *extras)


def _ffn_in_swiglu(name, h, w_t):
    f = w_t.shape[0] // 2
    fn = lambda u: (u, _f_swiglu(u[:, :f], u[:, f:]))
    return _mm_rows(name, h, w_t, "nt", fn, [], [(2 * f, BF16), (f, BF16)])


def _ffn_out_dx_swiglu_bwd(name, df, w_out, u):
    f = w_out.shape[0]

    def fn(da, u):
        u = u.astype(F32)
        _, vjp = jax.vjp(_f_swiglu, u[:, :f], u[:, f:])
        return (jnp.concatenate(vjp(da), axis=1),)

    return _mm_rows(name, df, w_out, "nt", fn, [u], [(2 * f, BF16)])[0]


def _gate_fwd(name, p, wf, wb, bf, bb):
    rows = p.shape[0]
    return _rowwise(name, lambda z, wf, wb, bf, bb: (_f_gate(z, wf, wb, bf, bb), ()), rows,
                    [(p, LANES, C_Z // LANES)], [wf, wb, bf, bb], [(GKW, F32)] * 2, [])


def _gate_bwd(name, p, dla_f, dla_b, wf, wb, bf, bb):
    rows = p.shape[0]

    def fn(z, dlf, dlb, wf, wb, bf, bb):
        _, vjp = jax.vjp(_f_gate, z, wf, wb, bf, bb)
        dz, dwf, dwb, dbf, dbb = vjp((dlf, dlb))
        return (dz,), (dwf, dwb, dbf, dbb)

    return _rowwise(name, fn, rows, [(p, LANES, C_Z // LANES), (dla_f, GKW, 0), (dla_b, GKW, 0)],
                    [wf, wb, bf, bb], [(LANES, BF16)], [(LANES, GKW), (LANES, GKW), (1, GKW), (1, GKW)])


def _head_mean_matrix():
    h = np.arange(GVW) // GLA_DV
    return jnp.asarray((h[:, None] == h[None, :]).astype(np.float32) / GLA_DV)


def _gla_out(name, attn, of, ob, p, gt):
    rows = of.shape[0]
    bd = _head_mean_matrix()
    fn = lambda attn, of, ob, gg, gt, bd: ((jnp.concatenate([attn, _f_gla_out(of, ob, gg, gt, bd)], axis=1),), ())
    return _rowwise(name, fn, rows, [(attn, QW, 0), (of, GVW, 0), (ob, GVW, 0), (p, GVW, C_GG // GVW)], [gt, bd],
                    [(MIX, BF16)], [])[0]


def _gla_out_bwd(name, dmix, of, ob, p, gt):
    rows = of.shape[0]
    bd = _head_mean_matrix()

    def fn(dm, of, ob, gg, gt, bd):
        _, vjp = jax.vjp(lambda of, gg, gt: _f_gla_out(of, ob, gg, gt, bd), of, gg, gt)
        do, dgg, dgt = vjp(dm)
        return (do, dgg), (dgt,)

    return _rowwise(name, fn, rows, [(dmix, GVW, 1), (of, GVW, 0), (ob, GVW, 0), (p, GVW, C_GG // GVW)], [gt, bd],
                    [(GVW, F32), (GVW, BF16)], [(1, GVW)])


def _rope_tables(n_tokens):
    t = jnp.arange(n_tokens)
    row = (t // GRID_W).astype(F32)
    col = (t % GRID_W).astype(F32)
    half = HEAD_DIM // 2
    inv_freq = ROPE_BASE ** (-jnp.arange(0, half, 2, dtype=F32) / half)
    ang_r = row[:, None] * inv_freq[None, :]
    ang_c = col[:, None] * inv_freq[None, :]
    ang = jnp.concatenate([ang_r, ang_r, ang_c, ang_c], axis=-1)
    sign = jnp.concatenate([-jnp.ones((16,), F32), jnp.ones((16,), F32)] * 2)
    cos, sin = jnp.cos(ang), jnp.sin(ang) * sign[None, :]
    return jnp.tile(cos, (1, 2)), jnp.tile(sin, (1, 2))


def _rot_pairs(x):
    w = x.shape[-1]
    lane = lax.broadcasted_iota(jnp.int32, x.shape, x.ndim - 1)
    return jnp.where((lane % 32) < 16, pltpu.roll(x, w - 16, x.ndim - 1), pltpu.roll(x, 16, x.ndim - 1))


def _rope_apply(x, cos, sin_signed, inverse):
    reps = x.shape[-1] // LANES
    cos = jnp.concatenate([cos] * reps, axis=-1) if reps > 1 else cos
    sin = jnp.concatenate([sin_signed] * reps, axis=-1) if reps > 1 else sin_signed
    if inverse:
        return x * cos + _rot_pairs(x * sin)
    return x * cos + _rot_pairs(x) * sin


def _rope_fwd(name, p, cos, sin):
    rows = p.shape[0]

    def fn(q, k, v, cos, sin):
        return (_rope_apply(q, cos, sin, False), _rope_apply(k, cos, sin, False), v), ()

    return _rowwise(name, fn, rows, [(p, QW, 0), (p, KVW, C_K // KVW), (p, KVW, C_V // KVW), (cos, LANES, 0),
                                     (sin, LANES, 0)], [], [(QW, BF16), (KVW, BF16), (KVW, BF16)], [])


def _proj_grad(name, dq_rot, dk_rot, dv, cos, sin, gla_f, gla_b, dgg, dz):
    rows = dq_rot.shape[0]

    def fn(dq, dk, dv, cos, sin, gqf, gkf, gvf, gqb, gkb, gvb, dgg, dz):
        parts = [_rope_apply(dq, cos, sin, True), gvf + gvb, dgg, _rope_apply(dk, cos, sin, True), dv, gqf + gqb,
                 gkf + gkb, dz]
        return (jnp.concatenate(parts, axis=1),), ()

    ins = [(dq_rot, QW), (dk_rot, KVW), (dv, KVW), (cos, LANES), (sin, LANES)]
    ins += [(t, t.shape[1]) for t in (*gla_f, *gla_b)] + [(dgg, GVW), (dz, LANES)]
    return _rowwise(name, fn, rows, [(t, w, 0) for t, w in ins], [], [(IN_PAD, BF16)], [],
                    tm=_pick(rows, (256, 128)))[0]


GROUP_ROWS = ATT_GROUP * BLOCK


ATT_SCALE = HEAD_DIM ** -0.5


def _attn_setup(sink, n, n_tokens):
    row = lax.broadcasted_iota(jnp.int32, (GROUP_ROWS, 1), 0)
    group = sum((row >= g * BLOCK).astype(jnp.int32) for g in range(1, ATT_GROUP))
    i = lax.broadcasted_iota(jnp.int32, (GROUP_ROWS, 3 * BLOCK), 0) - BLOCK * group
    j = lax.broadcasted_iota(jnp.int32, (GROUP_ROWS, 3 * BLOCK), 1)
    kpos = (n - 1) * BLOCK + j
    mask = (jnp.abs(j - BLOCK - i) <= WINDOW) & (kpos >= 0) & (kpos < n_tokens)
    head_id = lax.broadcasted_iota(jnp.int32, (1, ATT_HEADS), 1)
    sks = []
    for h in range(ATT_KV_HEADS):
        sk = jnp.zeros((GROUP_ROWS, 1), F32)
        for g in range(ATT_GROUP):
            one = jnp.sum(jnp.where(head_id == h * ATT_GROUP + g, sink, 0.0), axis=-1, keepdims=True)
            sk = jnp.where(group == g, one, sk)
        sks.append(sk)
    return mask, group, sks


def _attn_weights(q, kw, kc, sk, mask):
    q = q * ATT_SCALE
    s_w = jnp.where(mask, _raw_dot("nt", q, kw, False), NEG_INF)
    s_c = _raw_dot("nt", q, kc, False)
    m = jnp.maximum(jnp.maximum(jnp.max(s_w, axis=-1, keepdims=True), jnp.max(s_c, axis=-1, keepdims=True)), sk)
    pw, pc, ps = jnp.exp(s_w - m), jnp.exp(s_c - m), jnp.exp(sk - m)
    return q, pw, pc, ps, jnp.sum(pw, axis=-1, keepdims=True) + jnp.sum(pc, axis=-1, keepdims=True) + ps


def _f_attn(qs, kws, vws, kcs, vcs, sink, n, n_tokens):
    mask, _, sks = _attn_setup(sink, n, n_tokens)
    outs = []
    for h in range(ATT_KV_HEADS):
        _, pw, pc, _, den = _attn_weights(qs[h], kws[h], kcs[h], sks[h], mask)
        outs.append((_raw_dot("nn", pw, vws[h], False) + _raw_dot("nn", pc, vcs[h], False)) / den)
    return tuple(outs)


def _f_attn_bwd(qs, kws, vws, kcs, vcs, sink, outs, douts, n, n_tokens):
    mask, group, sks = _attn_setup(sink, n, n_tokens)
    head_id = lax.broadcasted_iota(jnp.int32, (1, ATT_HEADS), 1)
    dot = lambda mode, a, b: _raw_dot(mode, a, b, False)
    dqs, dkws, dvws, dkcs, dvcs, dsink = [], [], [], [], [], jnp.zeros((1, ATT_HEADS), F32)
    for h in range(ATT_KV_HEADS):
        q, pw, pc, ps, den = _attn_weights(qs[h], kws[h], kcs[h], sks[h], mask)
        inv = 1.0 / den
        pw, pc = pw * inv, pc * inv
        dd = jnp.sum(douts[h] * outs[h], axis=-1, keepdims=True)
        dsw = pw * (dot("nt", douts[h], vws[h]) - dd)
        dsc = pc * (dot("nt", douts[h], vcs[h]) - dd)
        dqs.append((dot("nn", dsw, kws[h]) + dot("nn", dsc, kcs[h])) * ATT_SCALE)
        dkws.append(dot("tn", dsw, q))
        dkcs.append(dot("tn", dsc, q))
        dvws.append(dot("tn", pw, douts[h]))
        dvcs.append(dot("tn", pc, douts[h]))
        dsk = -(ps * inv) * dd
        for g in range(ATT_GROUP):
            one = jnp.sum(jnp.where(group == g, dsk, 0.0), axis=0, keepdims=True)
            dsink = dsink + jnp.where(head_id == h * ATT_GROUP + g, one, 0.0)
    return dqs, dkws, dvws, dkcs, dvcs, dsink


def _group_rows(ref, h):
    hs = lambda hq: slice(hq * HEAD_DIM, (hq + 1) * HEAD_DIM)
    return jnp.concatenate([ref[:, hs(h * ATT_GROUP + g)].astype(F32) for g in range(ATT_GROUP)], axis=0)


def _ungroup_rows(ref, h, val):
    for g in range(ATT_GROUP):
        hq = h * ATT_GROUP + g
        ref[:, hq * HEAD_DIM:(hq + 1) * HEAD_DIM] = val[g * BLOCK:(g + 1) * BLOCK].astype(ref.dtype)


def _attn_loads(n, q_ref, kp_ref, vp_ref, kc_ref, vc_ref):
    r0 = pl.multiple_of(n * BLOCK, BLOCK)
    hs = lambda h: slice(h * HEAD_DIM, (h + 1) * HEAD_DIM)
    qs = [_group_rows(q_ref, h) for h in range(ATT_KV_HEADS)]
    kws = [kp_ref[pl.ds(r0, 3 * BLOCK), hs(h)].astype(F32) for h in range(ATT_KV_HEADS)]
    vws = [vp_ref[pl.ds(r0, 3 * BLOCK), hs(h)].astype(F32) for h in range(ATT_KV_HEADS)]
    kcs = [kc_ref[:, hs(h)].astype(F32) for h in range(ATT_KV_HEADS)]
    vcs = [vc_ref[:, hs(h)].astype(F32) for h in range(ATT_KV_HEADS)]
    return r0, hs, qs, kws, vws, kcs, vcs


def _attn_specs(s, c):
    full = lambda shape: pl.BlockSpec(shape, lambda n: (0, 0))
    return [pl.BlockSpec((BLOCK, QW), lambda n: (n, 0)), full((s + 2 * BLOCK, KVW)), full((s + 2 * BLOCK, KVW)),
            full((c, KVW)), full((c, KVW)), full((1, ATT_HEADS))]


def _attn_fwd(q, kp, vp, kc, vc, sink):
    s, c = q.shape[0], kc.shape[0]

    def body(q_ref, kp_ref, vp_ref, kc_ref, vc_ref, sink_ref, o_ref):
        n = pl.program_id(0)
        _, hs, qs, kws, vws, kcs, vcs = _attn_loads(n, q_ref, kp_ref, vp_ref, kc_ref, vc_ref)
        outs = _f_attn(qs, kws, vws, kcs, vcs, sink_ref[...], n, s)
        for h in range(ATT_KV_HEADS):
            _ungroup_rows(o_ref, h, outs[h])

    return pl.pallas_call(
        body, name="attn_fwd", grid=(s // BLOCK,), in_specs=_attn_specs(s, c),
        out_specs=pl.BlockSpec((BLOCK, QW), lambda n: (n, 0)), out_shape=jax.ShapeDtypeStruct((s, QW), BF16),
        compiler_params=_cp("parallel"),
    )(q, kp, vp, kc, vc, sink)


def _attn_bwd(do, o, q, kp, vp, kc, vc, sink):
    s, c = q.shape[0], kc.shape[0]

    def body(do_ref, o_ref, q_ref, kp_ref, vp_ref, kc_ref, vc_ref, sink_ref, dq_ref, dkp_ref, dvp_ref, dkc_ref,
             dvc_ref, dsink_ref):
        n = pl.program_id(0)

        @pl.when(n == 0)
        def _():
            for r in (dkp_ref, dvp_ref, dkc_ref, dvc_ref, dsink_ref):
                r[...] = jnp.zeros_like(r)

        r0, hs, qs, kws, vws, kcs, vcs = _attn_loads(n, q_ref, kp_ref, vp_ref, kc_ref, vc_ref)
        heads = range(ATT_KV_HEADS)
        dqs, dkws, dvws, dkcs, dvcs, dsink = _f_attn_bwd(
            qs, kws, vws, kcs, vcs, sink_ref[...], [_group_rows(o_ref, h) for h in heads],
            [_group_rows(do_ref, h) for h in heads], n, s)
        for h in heads:
            _ungroup_rows(dq_ref, h, dqs[h])
            dkp_ref[pl.ds(r0, 3 * BLOCK), hs(h)] += dkws[h]
            dvp_ref[pl.ds(r0, 3 * BLOCK), hs(h)] += dvws[h]
            dkc_ref[:, hs(h)] += dkcs[h]
            dvc_ref[:, hs(h)] += dvcs[h]
        dsink_ref[...] += dsink

    full = lambda shape: pl.BlockSpec(shape, lambda n: (0, 0))
    return pl.pallas_call(
        body, name="attn_bwd", grid=(s // BLOCK,),
        in_specs=[pl.BlockSpec((BLOCK, QW), lambda n: (n, 0))] * 2 + _attn_specs(s, c),
        out_specs=[pl.BlockSpec((BLOCK, QW), lambda n: (n, 0)), full((s + 2 * BLOCK, KVW)), full((s + 2 * BLOCK, KVW)),
                   full((c, KVW)), full((c, KVW)), full((1, ATT_HEADS))],
        out_shape=[jax.ShapeDtypeStruct((s, QW), F32), jax.ShapeDtypeStruct((s + 2 * BLOCK, KVW), F32),
                   jax.ShapeDtypeStruct((s + 2 * BLOCK, KVW), F32), jax.ShapeDtypeStruct((c, KVW), F32),
                   jax.ShapeDtypeStruct((c, KVW), F32), jax.ShapeDtypeStruct((1, ATT_HEADS), F32)],
        compiler_params=_cp("arbitrary"),
    )(do, o, q, kp, vp, kc, vc, sink)


GLA_GROUPS = 1
GLA_GROUP_HEADS = GLA_HEADS // GLA_GROUPS
GKG, GVG = GKW // GLA_GROUPS, GVW // GLA_GROUPS


def _gla_masks(heads=GLA_HEADS):
    hk = np.arange(heads * GLA_DK) // GLA_DK
    hv = np.arange(heads * GLA_DV) // GLA_DV
    head_k = (np.arange(heads)[:, None] == hk[None, :]).astype(np.float32)
    head_v = (np.arange(heads)[:, None] == hv[None, :]).astype(np.float32)
    bd_t = (hv[:, None] == hk[None, :]).astype(np.float32)
    return jnp.asarray(head_k), jnp.asarray(head_v), jnp.asarray(bd_t)


def _group_states(st):
    return jnp.stack([st[g * GVG:(g + 1) * GVG, g * GKG:(g + 1) * GKG] for g in range(GLA_GROUPS)])


def _ungroup_states(st):
    out = jnp.zeros((GVW, GKW), st.dtype)
    for g in range(GLA_GROUPS):
        out = out.at[g * GVG:(g + 1) * GVG, g * GKG:(g + 1) * GKG].set(st[g])
    return out


def _tri(n, rev, strict=False):
    i = lax.broadcasted_iota(jnp.int32, (n, n), 0)
    j = lax.broadcasted_iota(jnp.int32, (n, n), 1)
    if strict:
        keep = (j > i) if rev else (j < i)
    else:
        keep = (j >= i) if rev else (j <= i)
    return keep


def _f_gla_chunk(q, k, v, la, st, head_k, head_v, bd_t, rev):
    return _f_gla_carry(*_f_gla_intra(q, k, v, la, head_k, head_v, rev), v, st, bd_t)


def _f_gla_intra(q, k, v, la, head_k, head_v, rev):
    heads, kw, vw = head_k.shape[0], q.shape[1], v.shape[1]
    keep = _tri(GLA_CHUNK, rev)
    b = _nn_hi(keep.astype(F32), la)
    bl = jnp.sum(la, axis=0, keepdims=True)
    qd = q * (GLA_DK ** -0.5) * jnp.exp(b)
    ki = k * jnp.exp(-b)
    kd = k * jnp.exp(bl - b)
    q_heads = (qd[None, :, :] * head_k[:, None, :]).reshape(heads * GLA_CHUNK, kw)
    a_all = _nt(q_heads, ki).reshape(heads, GLA_CHUNK, GLA_CHUNK)
    a_all = jnp.where(keep[None, :, :], a_all, 0.0).reshape(heads * GLA_CHUNK, GLA_CHUNK)
    o_all = _nn(a_all, v).reshape(heads, GLA_CHUNK, vw)
    return jnp.sum(o_all * head_v[:, None, :], axis=0), qd, kd, bl


def _f_gla_carry(intra, qd, kd, bl, v, st, bd_t):
    return intra + _nt(qd, st), st * jnp.exp(bl) + bd_t * _tn(v, kd)


def _gla_specs(s, tb, order):
    return [pl.BlockSpec((tb, GKW), lambda i: (order(i), C_GQ // GKW)),
            pl.BlockSpec((tb, GKW), lambda i: (order(i), C_GK // GKW)),
            pl.BlockSpec((tb, GVW), lambda i: (order(i), C_GV // GVW)),
            pl.BlockSpec((tb, GKW), lambda i: (order(i), 0))]


GLA_BLOCK_CHUNKS = 4


def _gla_fwd(p, la_f, la_b, st_f0, st_b0):
    s = p.shape[0]
    tb = GLA_BLOCK_CHUNKS * GLA_CHUNK
    nblk = s // tb
    up, down = (lambda i: i), (lambda i: nblk - 1 - i)
    masks = _gla_masks(GLA_GROUP_HEADS)

    def scan(rev, q_ref, k_ref, v_ref, la_ref, o_ref, sts_ref, st_ref, consts):
        for g in range(GLA_GROUPS):
            gk, gv = slice(g * GKG, (g + 1) * GKG), slice(g * GVG, (g + 1) * GVG)
            st = st_ref[g]
            sts_ref[0, g] = st
            chunks = range(GLA_BLOCK_CHUNKS)
            for ci in (reversed(chunks) if rev else chunks):
                rows = slice(ci * GLA_CHUNK, (ci + 1) * GLA_CHUNK)
                o, st = _f_gla_chunk(q_ref[rows, gk], k_ref[rows, gk], v_ref[rows, gv], la_ref[rows, gk], st, *consts,
                                     rev)
                o_ref[rows, gv] = o
            st_ref[g] = st

    def body(qf, kf, vf, laf, qb, kb, vb, lab, stf0, stb0, hk_ref, hv_ref, bd_ref, of_ref, stsf_ref, ob_ref, stsb_ref,
             stf_ref, stb_ref):
        @pl.when(pl.program_id(0) == 0)
        def _():
            stf_ref[...] = stf0[...]
            stb_ref[...] = stb0[...]

        consts = (hk_ref[...], hv_ref[...], bd_ref[...])
        scan(False, qf, kf, vf, laf, of_ref, stsf_ref, stf_ref, consts)
        scan(True, qb, kb, vb, lab, ob_ref, stsb_ref, stb_ref, consts)

    full = lambda a: pl.BlockSpec(a.shape, lambda i: (0,) * a.ndim)
    outs = lambda order: [pl.BlockSpec((tb, GVW), lambda i: (order(i), 0)),
                          pl.BlockSpec((1, GLA_GROUPS, GVG, GKG), lambda i: (order(i), 0, 0, 0))]
    return pl.pallas_call(
        body, name="gla_fwd", grid=(nblk,),
        in_specs=_gla_specs(s, tb, up) + _gla_specs(s, tb, down) + [full(st_f0), full(st_b0)]
        + [full(m) for m in masks],
        out_specs=outs(up) + outs(down),
        out_shape=[jax.ShapeDtypeStruct((s, GVW), F32), jax.ShapeDtypeStruct((nblk, GLA_GROUPS, GVG, GKG), F32)] * 2,
        scratch_shapes=[pltpu.VMEM((GLA_GROUPS, GVG, GKG), F32)] * 2,
        compiler_params=_cp("arbitrary"),
    )(p, p, p, la_f, p, p, p, la_b, st_f0, st_b0, *masks)


def _gla_bwd(p, la_f, la_b, sts_f, sts_b, do, after=None):
    s = p.shape[0]
    tb = GLA_BLOCK_CHUNKS * GLA_CHUNK
    nblk = s // tb
    up, down = (lambda i: i), (lambda i: nblk - 1 - i)
    masks = _gla_masks(GLA_GROUP_HEADS)
    follow = () if after is None else (after,)

    def back(rev, q_ref, k_ref, v_ref, la_ref, sts_ref, do_ref, dq_ref, dk_ref, dv_ref, dla_ref, dst0_ref, dst_ref,
             consts):
        def block(q, k, v, la, st):
            outs = [None] * GLA_BLOCK_CHUNKS
            chunks = range(GLA_BLOCK_CHUNKS)
            for ci in (reversed(chunks) if rev else chunks):
                outs[ci], st = _f_gla_chunk(q[ci], k[ci], v[ci], la[ci], st, *consts, rev)
            return tuple(outs), st

        for g in range(GLA_GROUPS):
            gk, gv = slice(g * GKG, (g + 1) * GKG), slice(g * GVG, (g + 1) * GVG)
            split = lambda r, cols: tuple(r[ci * GLA_CHUNK:(ci + 1) * GLA_CHUNK, cols].astype(F32)
                                          for ci in range(GLA_BLOCK_CHUNKS))
            _, vjp = jax.vjp(block, split(q_ref, gk), split(k_ref, gk), split(v_ref, gv), split(la_ref, gk),
                             sts_ref[0, g])
            dq, dk, dv, dla, dst = vjp((split(do_ref, gv), dst_ref[g]))
            for ci in range(GLA_BLOCK_CHUNKS):
                rows = slice(ci * GLA_CHUNK, (ci + 1) * GLA_CHUNK)
                dq_ref[rows, gk], dk_ref[rows, gk], dv_ref[rows, gv], dla_ref[rows, gk] = dq[ci], dk[ci], dv[ci], dla[ci]
            dst_ref[g] = dst
            dst0_ref[g] = dst

    def body(*refs):
        ins, (hk_ref, hv_ref, bd_ref) = refs[:12], refs[12:15]
        outs = refs[15 + len(follow):]

        @pl.when(pl.program_id(0) == 0)
        def _():
            outs[10][...] = jnp.zeros_like(outs[10])
            outs[11][...] = jnp.zeros_like(outs[11])

        consts = (hk_ref[...], hv_ref[...], bd_ref[...])
        back(False, *ins[:6], *outs[:5], outs[10], consts)
        back(True, *ins[6:], *outs[5:10], outs[11], consts)

    full = lambda a: pl.BlockSpec(a.shape, lambda i: (0,) * a.ndim)

    def ins(order):
        return _gla_specs(s, tb, order) + [pl.BlockSpec((1, GLA_GROUPS, GVG, GKG), lambda i: (order(i), 0, 0, 0)),
                                           pl.BlockSpec((tb, GVW), lambda i: (order(i), 0))]

    def outs(order):
        blk = lambda w: pl.BlockSpec((tb, w), lambda i: (order(i), 0))
        return [blk(GKW), blk(GKW), blk(GVW), blk(GKW), pl.BlockSpec((GLA_GROUPS, GVG, GKG), lambda i: (0, 0, 0))]

    shapes = [jax.ShapeDtypeStruct((s, GKW), F32), jax.ShapeDtypeStruct((s, GKW), F32),
              jax.ShapeDtypeStruct((s, GVW), F32), jax.ShapeDtypeStruct((s, GKW), F32),
              jax.ShapeDtypeStruct((GLA_GROUPS, GVG, GKG), F32)]
    both = pl.pallas_call(
        body, name="gla_bwd", grid=(nblk,),
        in_specs=ins(down) + ins(up) + [full(m) for m in masks] + [pl.BlockSpec(memory_space=pl.ANY)] * len(follow),
        out_specs=outs(down) + outs(up), out_shape=shapes * 2,
        scratch_shapes=[pltpu.VMEM((GLA_GROUPS, GVG, GKG), F32)] * 2,
        compiler_params=_cp("arbitrary"),
    )(p, p, p, la_f, sts_f, do, p, p, p, la_b, sts_b, do, *masks, *follow)
    return both[:5], both[5:]


def _f_ctx_state(k, v, la_f, la_b, bd_t):
    c = k.shape[0]
    after = _nn_hi(_tri(c, True, strict=True).astype(F32), la_f)
    before = _nn_hi(_tri(c, False, strict=True).astype(F32), la_b)
    return bd_t * _tn(v, k * jnp.exp(after)), bd_t * _tn(v, k * jnp.exp(before))


def _ctx_state(pc, la_f, la_b):
    c = pc.shape[0]
    bd_t = _gla_masks()[2]

    def body(k_ref, v_ref, lf_ref, lb_ref, bd_ref, sf_ref, sb_ref):
        sf_ref[...], sb_ref[...] = _f_ctx_state(k_ref[...], v_ref[...], lf_ref[...], lb_ref[...], bd_ref[...])

    full = lambda a: pl.BlockSpec(a.shape, lambda i: (0, 0))
    return pl.pallas_call(
        body, name="ctx_state_fwd", grid=(1,),
        in_specs=[pl.BlockSpec((c, GKW), lambda i: (0, C_GK // GKW)), pl.BlockSpec((c, GVW), lambda i: (0, C_GV // GVW)),
                  full(la_f), full(la_b), full(bd_t)],
        out_specs=[pl.BlockSpec((GVW, GKW), lambda i: (0, 0))] * 2,
        out_shape=[jax.ShapeDtypeStruct((GVW, GKW), F32)] * 2,
        compiler_params=_cp("arbitrary"),
    )(pc, pc, la_f, la_b, bd_t)


def _ctx_state_bwd(pc, la_f, la_b, dsf, dsb):
    c = pc.shape[0]
    bd_t = _gla_masks()[2]

    def body(k_ref, v_ref, lf_ref, lb_ref, bd_ref, dsf_ref, dsb_ref, dk_ref, dv_ref, dlf_ref, dlb_ref):
        _, vjp = jax.vjp(lambda k, v, lf, lb: _f_ctx_state(k, v, lf, lb, bd_ref[...]),
                         k_ref[...], v_ref[...], lf_ref[...], lb_ref[...])
        dk, dv, dlf, dlb = vjp((dsf_ref[...], dsb_ref[...]))
        dk_ref[...], dv_ref[...] = dk.astype(BF16), dv.astype(BF16)
        dlf_ref[...], dlb_ref[...] = dlf, dlb

    full = lambda a: pl.BlockSpec(a.shape, lambda i: (0, 0))
    return pl.pallas_call(
        body, name="ctx_state_bwd", grid=(1,),
        in_specs=[pl.BlockSpec((c, GKW), lambda i: (0, C_GK // GKW)), pl.BlockSpec((c, GVW), lambda i: (0, C_GV // GVW)),
                  full(la_f), full(la_b), full(bd_t), full(dsf), full(dsb)],
        out_specs=[pl.BlockSpec((c, GKW), lambda i: (0, 0)), pl.BlockSpec((c, GVW), lambda i: (0, 0)),
                   pl.BlockSpec((c, GKW), lambda i: (0, 0)), pl.BlockSpec((c, GKW), lambda i: (0, 0))],
        out_shape=[jax.ShapeDtypeStruct((c, GKW), BF16), jax.ShapeDtypeStruct((c, GVW), BF16),
                   jax.ShapeDtypeStruct((c, GKW), F32), jax.ShapeDtypeStruct((c, GKW), F32)],
        compiler_params=_cp("arbitrary"),
    )(pc, pc, la_f, la_b, bd_t, dsf, dsb)


_SRC_COLS = ((0, QW), (QW + 2 * KVW + 2 * GKW, GVW), (QW + 2 * KVW + 2 * GKW + GVW, GVW), (QW, KVW), (QW + KVW, KVW),
             (QW + 2 * KVW, GKW), (QW + 2 * KVW + GKW, GKW), (IN_COLS - 2 * GATE_RANK, 2 * GATE_RANK))
_DST_COLS = (C_Q, C_GV, C_GG, C_K, C_V, C_GQ, C_GK, C_Z)


def _pack_w_in(w_in):
    parts = [w_in[:, s:s + n] for s, n in _SRC_COLS]
    parts.append(jnp.zeros((w_in.shape[0], IN_PAD - C_Z - 2 * GATE_RANK), w_in.dtype))
    return jnp.concatenate(parts, axis=1)


def _unpack_w_in_grad(g):
    by_src = sorted(zip(_SRC_COLS, _DST_COLS))
    return jnp.concatenate([g[:, d:d + n] for (_, n), d in by_src], axis=1)


def _prep_gate_weights(w_gate_fwd, w_gate_bwd):
    pad_rows = lambda w, at: jnp.zeros((LANES, GKW), F32).at[at:at + GATE_RANK].set(w)
    return {"wg_f": pad_rows(w_gate_fwd, 0), "wg_b": pad_rows(w_gate_bwd, GATE_RANK)}


def _local_step(x, ctx, target, ada, ada_c, w, late_weights, reduce_behind=None, reduce_w_in=None):
    s, d = x.shape
    sh1, sc1, gt1, sh2, sc2, gt2 = [ada[:, i * d:(i + 1) * d] for i in range(6)]
    sh1c, sc1c = ada_c[:, :d], ada_c[:, d:2 * d]
    cos, sin = _rope_tables(s)
    gt = jnp.tile(w["g_gla_norm"], (1, GLA_HEADS))

    h = _norm_mod("pre_mix", x, w["g_pre_mix"], sh1, sc1)
    hc = _norm_mod("pre_mix_ctx", ctx, w["g_pre_mix"], sh1c, sc1c)
    w_in, token = w["w_in"](h, cos, sin)
    p = _mm("proj_in", h, w_in, "nn", after=token)
    pc = _mm("proj_in_ctx", hc, w_in, "nn")
    q_rot, k_rot, v_b = _rope_fwd("rope", p, cos, sin)
    pad = ((BLOCK, BLOCK), (0, 0))
    kp, vp = jnp.pad(k_rot, pad), jnp.pad(v_b, pad)
    kc, vc = pc[:, C_K:C_K + KVW].astype(BF16), pc[:, C_V:C_V + KVW].astype(BF16)
    attn = _attn_fwd(q_rot, kp, vp, kc, vc, w["attn_sink"])
    gate_w = (w["wg_f"], w["wg_b"], w["b_gate_fwd"], w["b_gate_bwd"])
    la_f, la_b = _gate_fwd("gate", p, *gate_w)
    la_fc, la_bc = _gate_fwd("gate_ctx", pc, *gate_w)
    st_f0, st_b0 = _ctx_state(pc, la_fc, la_bc)
    o_f, sts_f, o_b, sts_b = _gla_fwd(p, la_f, la_b, _group_states(st_f0), _group_states(st_b0))
    mix = _gla_out("gla_out", attn, o_f, o_b, p, gt)
    w_out, w_ffn_in_t, w_ffn_out = late_weights(attn)
    y = _mm("proj_out", mix, w_out, "nn", BF16)
    x1, h2 = _post_res_norm_mod("post_mix_pre_ffn", x, y, w["g_post_mix"], gt1, w["g_pre_ffn"], sh2, sc2)
    u, a = _ffn_in_swiglu("ffn_in", h2, w_ffn_in_t)
    f = _mm("ffn_out", a, w_ffn_out, "nn", BF16)
    g = {}
    dx2, df, loss, g["g_post_ffn"], dgt2 = _post_res_loss("post_ffn_loss", x1, f, w["g_post_ffn"], gt2, target)

    late_rows = {"w_ffn_in_t": w_ffn_in_t.shape[0] // N_CHIP, "w_ffn_out": w_ffn_out.shape[0] // N_CHIP,
                 "w_out": w_out.shape[0] // N_CHIP}
    order = sorted(late_rows, key=lambda n: -late_rows[n])
    offsets, slab_rows = _slab_layout([late_rows[n] for n in order])
    late_at, slab_shape = dict(zip(order, offsets)), (N_CHIP, slab_rows, d)
    slab = _slab_zero_gaps("late_grads_gaps", slab_shape, [late_rows[n] for n in order], offsets)
    slab = _dw_into_slab("ffn_out_dw", a, df, slab, slab_shape, late_at["w_ffn_out"])
    du = _ffn_out_dx_swiglu_bwd("ffn_out_dx", df, w_ffn_out, u)
    dh2 = _mm("ffn_in_dx", du, w_ffn_in_t, "nn", BF16)
    slab = _dw_into_slab("ffn_in_dw", du, h2, slab, slab_shape, late_at["w_ffn_in_t"])
    dx1, dy, g["g_pre_ffn"], dsh2, dsc2, g["g_post_mix"], dgt1 = _norm_mod_post_res_bwd(
        "pre_ffn_post_mix_bwd", dh2, dx2, x1, y, w["g_pre_ffn"], sh2, sc2, w["g_post_mix"], gt1)
    dmix = _mm("proj_out_dx", dy, w_out, "nt", BF16)
    slab = _dw_into_slab("proj_out_dw", mix, dy, slab, slab_shape, late_at["w_out"])
    g["late"], g["late_at"], g["late_rows"] = slab, late_at, late_rows
    rb, sink, token = reduce_behind, w["attn_sink"], None
    if rb is not None:
        gt = _behind(gt, rb.start_slab(slab))
    d_o, dgg, dgt = _gla_out_bwd("gla_out_bwd", dmix, o_f, o_b, p, gt)
    g["g_gla_norm"] = jnp.sum(dgt.reshape(GLA_HEADS, GLA_DV), axis=0, keepdims=True)
    if rb is not None:
        token = rb.pair(dgg)
    gla_f, gla_b = _gla_bwd(p, la_f, la_b, sts_f, sts_b, d_o, token)
    (dla_f, dst_f0), (dla_b, dst_b0) = gla_f[3:], gla_b[3:]
    dst_f0, dst_b0 = _ungroup_states(dst_f0), _ungroup_states(dst_b0)
    if rb is not None:
        sink = _behind(sink, rb.total(dla_b))
    dgkc, dgvc, dla_fc, dla_bc = _ctx_state_bwd(pc, la_fc, la_bc, dst_f0, dst_b0)
    dz, dwf, dwb, dbf, dbb = _gate_bwd("gate_bwd", p, dla_f, dla_b, *gate_w)
    dzc, dwfc, dwbc, dbfc, dbbc = _gate_bwd("gate_ctx_bwd", pc, dla_fc, dla_bc, *gate_w)
    g["w_gate_fwd"] = (dwf + dwfc)[:GATE_RANK]
    g["w_gate_bwd"] = (dwb + dwbc)[GATE_RANK:2 * GATE_RANK]
    g["b_gate_fwd"], g["b_gate_bwd"] = dbf + dbfc, dbb + dbbc
    dq_rot, dkp, dvp, dkc, dvc, g["attn_sink"] = _attn_bwd(dmix, attn, q_rot, kp, vp, kc, vc, sink)
    if rb is not None:
        g["late"] = rb.result(dq_rot)
    dp = _proj_grad("proj_grad", dq_rot, dkp[BLOCK:BLOCK + s], dvp[BLOCK:BLOCK + s], cos, sin, gla_f[:3], gla_b[:3],
                    dgg, dz)
    c_rows = ctx.shape[0]
    zeros = lambda n: jnp.zeros((c_rows, n), BF16)
    dpc = jnp.concatenate([zeros(QW), dgvc, zeros(GVW), dkc.astype(BF16), dvc.astype(BF16), zeros(GKW), dgkc, dzc],
                          axis=1)
    g["w_in"] = _mm("proj_in_dw", h, dp, "tn", init=_mm("proj_in_ctx_dw", hc, dpc, "tn"))
    token = None if reduce_w_in is None else reduce_w_in.start(g["w_in"])
    dh = _mm("proj_in_dx", dp, w_in, "nt", BF16, after=token)
    dhc = _mm("proj_in_ctx_dx", dpc, w_in, "nt")
    if reduce_w_in is not None:
        sh1 = _behind(sh1, reduce_w_in.pair(dh))
    dx, dg_a, dsh1, dsc1 = _norm_mod_bwd("pre_mix_bwd", dh, dx1, x, w["g_pre_mix"], sh1, sc1)
    if reduce_w_in is not None:
        dsh1 = _behind(dsh1, reduce_w_in.total(dx))
    _, dg_b, dsh1c, dsc1c = _norm_mod_bwd("pre_mix_ctx_bwd", dhc, jnp.zeros_like(dhc), ctx, w["g_pre_mix"], sh1c,
                                          sc1c)
    g["g_pre_mix"] = dg_a + dg_b
    d_ada = jnp.concatenate([dsh1, dsc1, dgt1, dsh2, dsc2, dgt2], axis=1)
    d_ada_c = jnp.concatenate([dsh1c, dsc1c, jnp.zeros((1, 4 * d), F32)], axis=1)
    return loss, dx, g, d_ada, d_ada_c


HBM = pl.BlockSpec(memory_space=pltpu.HBM)
N_DEV, N_CHIP = 8, 4


def _place():
    x, y, c = lax.axis_index("x"), lax.axis_index("y"), lax.axis_index("c")
    return x, y, c, [(1 - x, y), (x, 1 - y), (1 - x, 1 - y)]


def _row_tile(n, mult, cap):
    return max(t for t in range(mult, min(n, cap) + 1, mult) if n % t == 0)


def _ag_small(name, v, after=None):
    follow = () if after is None else (after,)

    def body(v_ref, *rest):
        out_ref, send_sems, recv_sems = rest[len(follow):]
        x, y, c, _ = _place()
        out_ref[4 * x + 2 * y + c] = v_ref[...]

        def peer(r):
            return ((1 - x) if r & 4 else x, (1 - y) if r & 2 else y, (1 - c) if r & 1 else c)

        def copy(r, block):
            px, py, pc = block
            return pltpu.make_async_remote_copy(
                src_ref=v_ref, dst_ref=out_ref.at[4 * px + 2 * py + pc], send_sem=send_sems.at[r - 1],
                recv_sem=recv_sems.at[r - 1], device_id=peer(r), device_id_type=MESH)

        sends = [copy(r, (x, y, c)) for r in range(1, N_DEV)]
        for cp in sends:
            cp.start()
        for r in range(1, N_DEV):
            copy(r, peer(r)).wait_recv()
        for cp in sends:
            cp.wait_send()

    return pl.pallas_call(
        body, name=name, out_shape=jax.ShapeDtypeStruct((N_DEV,) + v.shape, v.dtype),
        in_specs=[pl.BlockSpec(memory_space=pltpu.VMEM)] + [pl.BlockSpec(memory_space=pl.ANY)] * len(follow),
        out_specs=pl.BlockSpec(memory_space=pltpu.VMEM),
        scratch_shapes=[pltpu.SemaphoreType.DMA((N_DEV - 1,)), pltpu.SemaphoreType.DMA((N_DEV - 1,))],
    )(v, *follow)


def _halves(c, rows, mult):
    hr = rows // 2
    return pl.ds(pl.multiple_of(c * hr, mult), hr), pl.ds(pl.multiple_of((1 - c) * hr, mult), hr)


def _add_half(name, g, a, c_idx):
    n_sh, hr, n = a.shape
    tr = _row_tile(hr, 16, 1024)
    nb = hr // tr

    def body(c_ref, g_ref, a_ref, o_ref):
        o_ref[...] = (g_ref[...] + a_ref[...]).astype(o_ref.dtype)

    return pl.pallas_call(
        body, name=name, out_shape=jax.ShapeDtypeStruct(a.shape, BF16),
        grid_spec=pltpu.PrefetchScalarGridSpec(
            num_scalar_prefetch=1, grid=(n_sh, nb),
            in_specs=[pl.BlockSpec((1, tr, n), lambda s, i, c_ref: (s, c_ref[0] * nb + i, 0)),
                      pl.BlockSpec((1, tr, n), lambda s, i, c_ref: (s, i, 0))],
            out_specs=pl.BlockSpec((1, tr, n), lambda s, i, c_ref: (s, i, 0))),
        compiler_params=_cp("parallel", "parallel"),
    )(c_idx, g, a)


def _sum_chips(name, b, c_idx):
    n_sh, hr, n = b.shape
    tr = _row_tile(hr, 16, 1024)
    nb = hr // tr

    def body(c_ref, b0, b1, b2, b3, o_ref):
        o_ref[...] = ((b0[0].astype(F32) + b1[0].astype(F32)) + b2[0].astype(F32)) + b3[0].astype(F32)

    return pl.pallas_call(
        body, name=name, out_shape=jax.ShapeDtypeStruct((2 * hr, n), F32),
        grid_spec=pltpu.PrefetchScalarGridSpec(
            num_scalar_prefetch=1, grid=(nb,),
            in_specs=[pl.BlockSpec((1, tr, n), functools.partial(lambda i, c_ref, k: (k, i, 0), k=k))
                      for k in range(n_sh)],
            out_specs=pl.BlockSpec((tr, n), lambda i, c_ref: (c_ref[0] * nb + i, 0))),
        compiler_params=_cp("parallel"),
    )(c_idx, b, b, b, b)


SEM = pl.BlockSpec(memory_space=pltpu.SEMAPHORE)
ANY = pl.BlockSpec(memory_space=pl.ANY)
DATAFLOW = pltpu.SideEffectType.DATAFLOW_SIDE_EFFECTING


def _remote(src, dst, send_sems, recv_sems, k, to):
    return pltpu.make_async_remote_copy(src_ref=src, dst_ref=dst, send_sem=send_sems.at[k], recv_sem=recv_sems.at[k],
                                        device_id=to, device_id_type=MESH)


def _split_copy(name, src, land_shape, land_dtype, n, plan, after=None):
    after = jnp.zeros((8, LANES), F32) if after is None else after

    def start_body(src_ref, land_ref, after_ref, send_sems, recv_sems, src_thru, land_thru, token):
        for cp in plan(src_ref, land_ref, send_sems, recv_sems)[0]:
            cp.start()
        token[...] = jnp.zeros_like(token)

    sems = pltpu.SemaphoreType.DMA((n,))
    send_sems, recv_sems, src_thru, land_thru, token = pl.pallas_call(
        start_body, name=name + "_start",
        out_shape=(sems, sems, pltpu.HBM(src.shape, src.dtype), pltpu.HBM(land_shape, land_dtype),
                   jax.ShapeDtypeStruct((8, LANES), F32)),
        in_specs=(HBM, HBM, ANY), out_specs=(SEM, SEM, HBM, HBM, pl.BlockSpec(memory_space=pltpu.VMEM)),
        input_output_aliases={0: 2, 1: 3}, compiler_params=pltpu.CompilerParams(has_side_effects=DATAFLOW),
    )(pltpu.with_memory_space_constraint(src, pltpu.HBM),
      pltpu.with_memory_space_constraint(lax.empty(land_shape, land_dtype), pltpu.HBM), after)

    def wait(*after):
        def wait_body(src_ref, land_ref, send_sems, recv_sems, *rest):
            sent, received = plan(src_ref, land_ref, send_sems, recv_sems)
            for cp in sent:
                cp.wait_send()
            for cp in received:
                cp.wait_recv()

        return pl.pallas_call(
            wait_body, name=name + "_wait",
            out_shape=(pltpu.HBM(src.shape, src.dtype), pltpu.HBM(land_shape, land_dtype)),
            in_specs=(HBM, HBM, SEM, SEM) + (ANY,) * len(after), out_specs=(HBM, HBM),
            input_output_aliases={0: 0, 1: 1}, compiler_params=pltpu.CompilerParams(has_side_effects=DATAFLOW),
        )(src_thru, land_thru, send_sems, recv_sems, *after)

    return token, wait


def _split_gather(name, shards, after):
    k, n, plan = len(shards), 3 * len(shards), _plan_gather

    def start_body(*refs):
        for cp in plan(refs[:k], refs[k:2 * k], refs[2 * k + 1], refs[2 * k + 2])[0]:
            cp.start()
        refs[-1][...] = jnp.zeros_like(refs[-1])

    sems = pltpu.SemaphoreType.DMA((n,))
    bufs = [pltpu.HBM(s.shape, s.dtype) for s in shards] + [pltpu.HBM((N_CHIP,) + s.shape, s.dtype) for s in shards]
    hbm = lambda t: pltpu.with_memory_space_constraint(t, pltpu.HBM)
    outs = pl.pallas_call(
        start_body, name=name + "_start", out_shape=(sems, sems, *bufs, jax.ShapeDtypeStruct((8, LANES), F32)),
        in_specs=(HBM,) * (2 * k) + (ANY,),
        out_specs=(SEM, SEM) + (HBM,) * (2 * k) + (pl.BlockSpec(memory_space=pltpu.VMEM),),
        input_output_aliases={i: 2 + i for i in range(2 * k)},
        compiler_params=pltpu.CompilerParams(has_side_effects=DATAFLOW),
    )(*[hbm(s) for s in shards], *[hbm(lax.empty((N_CHIP,) + s.shape, s.dtype)) for s in shards], after)
    send_sems, recv_sems, thru, token = outs[0], outs[1], outs[2:2 + 2 * k], outs[-1]

    def wait(*after):
        def wait_body(*refs):
            sent, received = plan(refs[:k], refs[k:2 * k], refs[2 * k], refs[2 * k + 1])
            for cp in sent:
                cp.wait_send()
            for cp in received:
                cp.wait_recv()

        res = pl.pallas_call(
            wait_body, name=name + "_wait", out_shape=tuple(bufs),
            in_specs=(HBM,) * (2 * k) + (SEM, SEM) + (ANY,) * len(after), out_specs=(HBM,) * (2 * k),
            input_output_aliases={i: i for i in range(2 * k)},
            compiler_params=pltpu.CompilerParams(has_side_effects=DATAFLOW),
        )(*thru, send_sems, recv_sems, *after)
        return res[:k], res[k:]

    return token, wait


def _behind(x, token):
    return x + token[0, 0]


def _plan_gather(src_refs, land_refs, send_sems, recv_sems):
    x, y, c, chips = _place()
    pairs = list(enumerate(zip(src_refs, land_refs)))
    sent = [_remote(s, l.at[2 * x + y], send_sems, recv_sems, 3 * i + j, (px, py, c))
            for i, (s, l) in pairs for j, (px, py) in enumerate(chips)]
    received = [_remote(s, l.at[2 * px + py], send_sems, recv_sems, 3 * i + j, (px, py, c))
                for i, (s, l) in pairs for j, (px, py) in enumerate(chips)]
    return sent, received


def _plan_swap(src_ref, land_ref, send_sems, recv_sems):
    x, y, c, _ = _place()
    _, other_half = _halves(c, src_ref.shape[1], 8)
    cp = _remote(src_ref.at[pl.ds(0, src_ref.shape[0]), other_half], land_ref, send_sems, recv_sems, 0, (x, y, 1 - c))
    return [cp], [cp]


def _plan_scatter(src_ref, land_ref, send_sems, recv_sems):
    x, y, c, chips = _place()
    sent = [_remote(src_ref.at[2 * px + py], land_ref.at[2 * x + y], send_sems, recv_sems, j, (px, py, c))
            for j, (px, py) in enumerate(chips)]
    received = [_remote(src_ref.at[2 * px + py], land_ref.at[2 * px + py], send_sems, recv_sems, j, (px, py, c))
                for j, (px, py) in enumerate(chips)]
    return sent, received


def _plan_share(src_ref, land_ref, send_sems, recv_sems):
    x, y, c, _ = _place()
    mine_half, other_half = _halves(c, src_ref.shape[0], 8)
    return ([_remote(src_ref.at[mine_half], src_ref.at[mine_half], send_sems, recv_sems, 0, (x, y, 1 - c))],
            [_remote(src_ref.at[other_half], src_ref.at[other_half], send_sems, recv_sems, 0, (x, y, 1 - c))])


class _GatherBehind:
    def __init__(self, name, shards, chip, after):
        self.chip = chip
        self.token, self.wait = _split_gather(name, shards, after)

    def result(self, *after):
        shards, lands = self.wait(*after)
        return [lax.dynamic_update_slice(land, shard[None], (self.chip, 0, 0)) for shard, land in zip(shards, lands)]


class _ReduceBehind:
    def __init__(self, name, chip, c_idx):
        self.name, self.chip, self.c_idx = name, chip, c_idx

    def start_slab(self, g):
        n_sh, rows, n = g.shape
        token, self.wait = _split_copy(self.name + "_swap", g, (n_sh, rows // 2, n), g.dtype, 1, _plan_swap)
        return token

    def pair(self, after):
        g, a = self.wait(after)
        h = _add_half(self.name + "_pair", g, a, self.c_idx)
        token, self.wait = _split_copy(self.name + "_scatter", h, h.shape, h.dtype, 3, _plan_scatter)
        return token

    def total(self, after):
        h, b = self.wait(after)
        b = lax.dynamic_update_slice(b, lax.dynamic_slice_in_dim(h, self.chip, 1, axis=0), (self.chip, 0, 0))
        f = _sum_chips(self.name + "_sum", b, self.c_idx)
        token, self.wait = _split_copy(self.name + "_share", f, (8, LANES), f.dtype, 1, _plan_share)
        return token

    def result(self, after):
        return self.wait(after)[0]


class _ReduceColsBehind(_ReduceBehind):
    def start(self, g_padded):
        g = _unpack_w_in_grad(g_padded)
        n = g.shape[1] // N_CHIP
        return self.start_slab(jnp.stack([g[:, k * n:(k + 1) * n] for k in range(N_CHIP)]))


def _f_adamw(w, g, m, v):
    m = ADAM_B1 * m + (1.0 - ADAM_B1) * g
    v = ADAM_B2 * v + (1.0 - ADAM_B2) * (g * g)
    m_hat = m / (1.0 - ADAM_B1 ** ADAM_STEP)
    v_hat = v / (1.0 - ADAM_B2 ** ADAM_STEP)
    return -ADAM_LR * (m_hat / (jnp.sqrt(v_hat) + ADAM_EPS) + ADAM_WD * w), m, v


def _adamw(name, w, g, m, v):
    rows, n = w.shape
    return _rowwise(name, lambda w, g, m, v: (_f_adamw(w, g, m, v), ()), rows, [(t, n, 0) for t in (w, g, m, v)], [],
                    [(n, F32)] * 3, [], tm=_row_tile(rows, 8, 256))


def _adamw_many(name, ws, gs, ms, vs):
    k = len(ws)

    def body(*refs):
        ins, outs = refs[:4 * k], refs[4 * k:]
        for i in range(k):
            res = _f_adamw(ins[i][...], ins[k + i][...], ins[2 * k + i][...], ins[3 * k + i][...])
            for j in range(3):
                outs[j * k + i][...] = res[j]

    out = pl.pallas_call(body, name=name, out_shape=[jax.ShapeDtypeStruct(w.shape, F32) for w in ws] * 3)(
        *ws, *gs, *ms, *vs)
    return out[:k], out[k:2 * k], out[2 * k:]


def _pack_rows(parts):
    rows = []
    for t in parts:
        t = t.reshape(-1)
        rows.append(jnp.pad(t, (0, -t.shape[0] % LANES)).reshape(-1, LANES))
    out = jnp.concatenate(rows, axis=0)
    return jnp.pad(out, ((0, -out.shape[0] % 8), (0, 0)))


def _unpack_rows(packed, shapes):
    out, r = [], 0
    for shp in shapes:
        n = int(np.prod(shp))
        nr = -(-n // LANES)
        out.append(packed[r:r + nr].reshape(-1)[:n].reshape(shp))
        r += nr
    return out


def _sum_blocks(name, g):
    def body(g_ref, o_ref):
        acc = g_ref[0]
        for k in range(1, g.shape[0]):
            acc = acc + g_ref[k]
        o_ref[...] = acc

    return pl.pallas_call(body, name=name, out_shape=jax.ShapeDtypeStruct(g.shape[1:], F32))(g)


def _silu(t):
    return t * _sigmoid(t)


def _ada_fwd(cc, w_ada):
    n = w_ada.shape[1]
    tn = _row_tile(n, LANES, 512)

    def body(cc_ref, w_ref, o_ref):
        o_ref[...] = _nn(_silu(cc_ref[...]), w_ref[...])

    return pl.pallas_call(
        body, name="ada_fwd", grid=(n // tn,), out_shape=jax.ShapeDtypeStruct((cc.shape[0], n), F32),
        in_specs=[pl.BlockSpec(cc.shape, lambda j: (0, 0)), pl.BlockSpec((w_ada.shape[0], tn), lambda j: (0, j))],
        out_specs=pl.BlockSpec((cc.shape[0], tn), lambda j: (0, j)), compiler_params=_cp("parallel"),
    )(cc, w_ada)


def _ada_bwd(cc, dm, w_ada):
    d, n = w_ada.shape
    tn = _row_tile(n, LANES, 512)

    def body(cc_ref, dm_ref, w_ref, gw_ref, ds_ref):
        @pl.when(pl.program_id(0) == 0)
        def _():
            ds_ref[...] = jnp.zeros_like(ds_ref)

        gw_ref[...] = _raw_dot("tn", _silu(cc_ref[...]), dm_ref[...], True)
        ds_ref[...] += _raw_dot("nt", dm_ref[...], w_ref[...], False)

    return pl.pallas_call(
        body, name="ada_bwd", grid=(n // tn,),
        out_shape=[jax.ShapeDtypeStruct((d, n), F32), jax.ShapeDtypeStruct(cc.shape, F32)],
        in_specs=[pl.BlockSpec(cc.shape, lambda j: (0, 0)), pl.BlockSpec((cc.shape[0], tn), lambda j: (0, j)),
                  pl.BlockSpec((d, tn), lambda j: (0, j))],
        out_specs=[pl.BlockSpec((d, tn), lambda j: (0, j)), pl.BlockSpec(cc.shape, lambda j: (0, 0))],
        compiler_params=_cp("arbitrary"),
    )(cc, dm, w_ada)


def _c_ctx_grad(parts, c_ctx):
    def body(p_ref, c_ref, o_ref):
        ds = ((p_ref[0] + p_ref[1]) + p_ref[2]) + p_ref[3]
        _, vjp = jax.vjp(_silu, c_ref[...])
        o_ref[...] = vjp(ds)[0]

    return pl.pallas_call(body, name="c_ctx_grad", out_shape=jax.ShapeDtypeStruct(c_ctx.shape, F32))(parts, c_ctx)


def kernel(x, c, ctx, c_ctx, w_ada, b_ada, g_pre_mix, g_post_mix, g_pre_ffn, g_post_ffn, w_in, attn_sink, w_gate_fwd, b_gate_fwd, w_gate_bwd, b_gate_bwd, g_gla_norm, w_out, w_ffn_in, w_ffn_out, loss_target, m_c_ctx, m_w_ada, m_b_ada, m_g_pre_mix, m_g_post_mix, m_g_pre_ffn, m_g_post_ffn, m_w_in, m_attn_sink, m_w_gate_fwd, m_b_gate_fwd, m_w_gate_bwd, m_b_gate_bwd, m_g_gla_norm, m_w_out, m_w_ffn_in, m_w_ffn_out, v_c_ctx, v_w_ada, v_b_ada, v_g_pre_mix, v_g_post_mix, v_g_pre_ffn, v_g_post_ffn, v_w_in, v_attn_sink, v_w_gate_fwd, v_b_gate_fwd, v_w_gate_bwd, v_b_gate_bwd, v_g_gla_norm, v_w_out, v_w_ffn_in, v_w_ffn_out):
    xi, yi, ci = lax.axis_index("x"), lax.axis_index("y"), lax.axis_index("c")
    dev, chip = 4 * xi + 2 * yi + ci, 2 * xi + yi
    c_idx = jnp.reshape(ci, (1,)).astype(jnp.int32)
    d = x.shape[-1]
    n_ada, n_in, n_f = w_ada.shape[-1], w_in.shape[-1], w_ffn_in.shape[-1]
    r_out, r_f = w_out.shape[1], w_ffn_out.shape[1]
    n_gate = w_gate_fwd.shape[-1]
    by_chip = lambda t: t[0::2]

    rc = -(-d // LANES)
    g1 = _ag_small("gather_cond", _pack_rows([c[0], w_gate_fwd[0], w_gate_bwd[0]]))
    c_all = g1[:, :rc].reshape(N_DEV, -1)[:, :d]
    gr = GATE_RANK * n_gate // LANES
    gate_full = lambda off: jnp.transpose(by_chip(g1)[:, off:off + gr].reshape(N_CHIP, GATE_RANK, n_gate),
                                          (1, 0, 2)).reshape(GATE_RANK, N_CHIP * n_gate)
    wgf, wgb = gate_full(rc), gate_full(rc + gr)
    cc = jnp.concatenate([c_all, c_ctx[None, :], jnp.zeros((7, d), F32)], axis=0)

    g2 = _ag_small("gather_ada", _ada_fwd(cc, w_ada[0]).reshape(-1, LANES))
    ada_all = jnp.transpose(by_chip(g2).reshape(N_CHIP, 16, n_ada), (1, 0, 2)).reshape(16, N_CHIP * n_ada) + b_ada
    first = _GatherBehind("gather_w_in", [w_in[0].astype(BF16)], chip, g2)
    late_shards = [w_out[0].astype(BF16), jnp.transpose(w_ffn_in[0]).astype(BF16), w_ffn_out[0].astype(BF16)]
    late = []

    def first_weights(*after):
        w_in_g, = first.result(*after, *late_shards)
        late.append(_GatherBehind("gather_late", late_shards, chip, w_in_g))
        return _pack_w_in(jnp.concatenate([w_in_g[k] for k in range(N_CHIP)], axis=1)), late[0].token

    def late_weights(after):
        return [t.reshape(-1, d) for t in late[0].result(after)]

    ada_all = _behind(ada_all, first.token)
    ada = lax.dynamic_slice(ada_all, (dev, 0), (1, N_CHIP * n_ada))
    ada_c = ada_all[N_DEV:N_DEV + 1]

    w = _prep_gate_weights(wgf, wgb)
    w.update(w_in=first_weights, g_pre_mix=g_pre_mix, g_post_mix=g_post_mix, g_pre_ffn=g_pre_ffn, g_post_ffn=g_post_ffn,
             attn_sink=attn_sink, b_gate_fwd=b_gate_fwd, b_gate_bwd=b_gate_bwd, g_gla_norm=g_gla_norm)

    reduce_behind = _ReduceBehind("reduce_late", chip, c_idx)
    reduce_w_in = _ReduceColsBehind("reduce_w_in", chip, c_idx)
    loss_lanes, grad_x, g, d_ada, d_ada_c = _local_step(x[0], ctx[0], loss_target[0], ada, ada_c, w, late_weights,
                                                        reduce_behind, reduce_w_in)

    small = ("g_pre_mix", "g_post_mix", "g_pre_ffn", "g_post_ffn", "attn_sink", "b_gate_fwd", "b_gate_bwd",
             "g_gla_norm", "w_gate_fwd", "w_gate_bwd")
    shapes = [(1, 6 * d)] * 2 + [g[n].shape for n in small] + [(1, LANES)]
    g3 = _ag_small("gather_small_grads", _pack_rows([d_ada, d_ada_c] + [g[n] for n in small] + [loss_lanes]))
    tot = dict(zip(("d_ada", "d_ada_c") + small + ("loss",),
                   _unpack_rows(_sum_blocks("sum_small_grads", g3), shapes)))
    r_ada = 6 * d // LANES
    dm = jnp.concatenate([g3[:, :r_ada].reshape(N_DEV, 6 * d), tot["d_ada_c"], jnp.zeros((7, 6 * d), F32)], axis=0)
    grads = {n: tot[n] for n in small[:8]}
    grads["b_ada"] = _sum_blocks("sum_b_ada", dm.reshape(16, r_ada, LANES)).reshape(1, 6 * d)
    grads["w_gate_fwd"] = lax.dynamic_slice(tot["w_gate_fwd"], (0, chip * n_gate), (GATE_RANK, n_gate))[None]
    grads["w_gate_bwd"] = lax.dynamic_slice(tot["w_gate_bwd"], (0, chip * n_gate), (GATE_RANK, n_gate))[None]
    gw_ada, dsc = _ada_bwd(cc, lax.dynamic_slice(dm, (0, chip * n_ada), (16, n_ada)), w_ada[0])
    grads["w_ada"] = gw_ada[None]
    g4 = _ag_small("gather_c_ctx", _pack_rows([dsc[N_DEV]]))
    grads["c_ctx"] = _c_ctx_grad(by_chip(g4), _pack_rows([c_ctx])).reshape(-1)[:d]

    grads["w_in"] = reduce_w_in.result(g4)[None]
    part = lambda n: g["late"][g["late_at"][n]:g["late_at"][n] + g["late_rows"][n]]
    grads["w_ffn_in"], grads["w_ffn_out"], grads["w_out"] = (jnp.transpose(part("w_ffn_in_t"))[None],
                                                            part("w_ffn_out")[None], part("w_out")[None])

    names = ("c_ctx", "w_ada", "b_ada", "g_pre_mix", "g_post_mix", "g_pre_ffn", "g_post_ffn", "w_in", "attn_sink",
             "w_gate_fwd", "b_gate_fwd", "w_gate_bwd", "b_gate_bwd", "g_gla_norm", "w_out", "w_ffn_in", "w_ffn_out")
    weights = dict(zip(names, (c_ctx, w_ada, b_ada, g_pre_mix, g_post_mix, g_pre_ffn, g_post_ffn, w_in, attn_sink,
                               w_gate_fwd, b_gate_fwd, w_gate_bwd, b_gate_bwd, g_gla_norm, w_out, w_ffn_in,
                               w_ffn_out)))
    m_in = dict(zip(names, (m_c_ctx, m_w_ada, m_b_ada, m_g_pre_mix, m_g_post_mix, m_g_pre_ffn, m_g_post_ffn, m_w_in,
                            m_attn_sink, m_w_gate_fwd, m_b_gate_fwd, m_w_gate_bwd, m_b_gate_bwd, m_g_gla_norm,
                            m_w_out, m_w_ffn_in, m_w_ffn_out)))
    v_in = dict(zip(names, (v_c_ctx, v_w_ada, v_b_ada, v_g_pre_mix, v_g_post_mix, v_g_pre_ffn, v_g_post_ffn, v_w_in,
                            v_attn_sink, v_w_gate_fwd, v_b_gate_fwd, v_w_gate_bwd, v_b_gate_bwd, v_g_gla_norm,
                            v_w_out, v_w_ffn_in, v_w_ffn_out)))
    large = ("w_ada", "w_in", "w_out", "w_ffn_in", "w_ffn_out")
    tiny = tuple(n for n in names if n not in large)
    delta, new_m, new_v = {}, {}, {}
    for n in large:
        dl, nm, nv = _adamw("adamw_" + n, weights[n][0], grads[n][0], m_in[n][0], v_in[n][0])
        delta[n], new_m[n], new_v[n] = dl[None], nm[None], nv[None]
    for n in tiny:
        grads[n] = grads[n].reshape(weights[n].shape)
    as_rows = lambda t: t.reshape(-1, t.shape[-1])
    res = _adamw_many("adamw_small", *[[as_rows(t[n]) for n in tiny] for t in (weights, grads, m_in, v_in)])
    for out, vals in zip((delta, new_m, new_v), res):
        out.update({n: val.reshape(weights[n].shape) for n, val in zip(tiny, vals)})

    return (tot["loss"][0, 0], grad_x[None], *[grads[n] for n in names], *[delta[n] for n in names], *[new_m[n] for n in names],
            *[new_v[n] for n in names])
```

```python
import functools

import jax
import jax.numpy as jnp
import numpy as np
from jax import lax
from jax.experimental import pallas as pl
from jax.experimental.pallas import tpu as pltpu

F32 = jnp.float32
BF16 = jnp.bfloat16
MESH = pl.DeviceIdType.MESH

HEAD_DIM = 64
ATT_HEADS = 8
ATT_KV_HEADS = 2
ATT_GROUP = ATT_HEADS // ATT_KV_HEADS
WINDOW = 128
BLOCK = 128
GRID_W = 64
ROPE_BASE = 10000.0
GLA_HEADS = 8
GLA_DK = 32
GLA_DV = 64
GLA_CHUNK = 64
GATE_RANK = 16
GATE_TAU = 16.0
NEG_INF = -1e30
QW = ATT_HEADS * HEAD_DIM
KVW = ATT_KV_HEADS * HEAD_DIM
GKW = GLA_HEADS * GLA_DK
GVW = GLA_HEADS * GLA_DV
IN_COLS = QW + 2 * KVW + 2 * GKW + 2 * GVW + 2 * GATE_RANK
LANES = 128
IN_PAD = IN_COLS + LANES - 2 * GATE_RANK
C_Q, C_GV, C_GG = 0, QW, QW + GVW
C_K = C_GG + GVW
C_V = C_K + KVW
C_GQ = C_V + KVW
C_GK = C_GQ + GKW
C_Z = C_GK + GKW
MIX = QW + GVW

ADAM_LR, ADAM_B1, ADAM_B2, ADAM_EPS, ADAM_WD, ADAM_STEP = 0.001, 0.9, 0.999, 1e-08, 0.01, 10

VMEM_LIMIT = 56 * 1024 * 1024


def _cp(*sem):
    return pltpu.CompilerParams(dimension_semantics=sem, vmem_limit_bytes=VMEM_LIMIT)


def _pick(n, cands):
    for t in cands:
        if n % t == 0:
            return t
    return n


_DIMS = {"nn": (((1,), (0,)), ((), ())), "nt": (((1,), (1,)), ((), ())), "tn": (((0,), (0,)), ((), ()))}


def _raw_dot(mode, a, b, hi):
    dot = lambda u, v: lax.dot_general(u, v, _DIMS[mode], preferred_element_type=F32)
    if hi:
        a, b = a.astype(F32), b.astype(F32)
        a_hi, b_hi = a.astype(BF16), b.astype(BF16)
        a_lo, b_lo = (a - a_hi.astype(F32)).astype(BF16), (b - b_hi.astype(F32)).astype(BF16)
        return dot(a_hi, b_hi) + (dot(a_lo, b_hi) + dot(a_hi, b_lo))
    return dot(a.astype(BF16), b.astype(BF16))


def _make_dot(mode, hi):
    @jax.custom_vjp
    def dot(a, b):
        return _raw_dot(mode, a, b, hi)

    def fwd(a, b):
        return _raw_dot(mode, a, b, hi), (a, b)

    def bwd(res, dc):
        a, b = res
        if mode == "nn":
            return _raw_dot("nt", dc, b, hi), _raw_dot("tn", a, dc, hi)
        if mode == "nt":
            return _raw_dot("nn", dc, b, hi), _raw_dot("tn", dc, a, hi)
        return _raw_dot("nt", b, dc, hi), _raw_dot("nn", a, dc, hi)

    dot.defvjp(fwd, bwd)
    return dot


_nn, _nt, _tn = _make_dot("nn", False), _make_dot("nt", False), _make_dot("tn", False)
_nn_hi = _make_dot("nn", True)


MM_VMEM_BUDGET = 44 * 1024 * 1024


def _halvings(n):
    out = [n]
    while out[-1] % (2 * LANES) == 0:
        out.append(out[-1] // 2)
    return out


def _mm_tiles(mode, m, n, k, a_bytes, b_bytes, o_bytes, init_bytes=0):
    tms = [t for t in dict.fromkeys((m, m // 2, m // 4, 2048, 1024, 512, 256, 128))
           if m % t == 0 and t % (LANES if mode == "tn" else 16) == 0 and t <= 4096] or [m]
    if mode == "tn":
        fits = [(k // tk + 0.5 * (m // tm), tm, tk)
                for tk in (4096, 2048, 1024, 512, 256, 128) if k % tk == 0 for tm in tms
                if 2 * (tk * tm * a_bytes + tk * n * b_bytes + tm * n * (o_bytes + init_bytes)) <= MM_VMEM_BUDGET]
        if fits:
            _, tm, tk = min(fits)
            return tm, n, tk
    tks = ([t for t in (512, 256, 128) if k % t == 0] or [k]) if mode == "tn" else _halvings(k)
    for tn in _halvings(n):
        for tk in tks:
            for tm in tms:
                acc = tm * tn * 4 if (k // tk > 1 and o_bytes != 4) else 0
                tiles = tm * tk * a_bytes + tk * tn * b_bytes + tm * tn * (o_bytes + init_bytes)
                if 2 * tiles + acc <= MM_VMEM_BUDGET:
                    return tm, tn, tk
    return tms[-1], _halvings(n)[-1], tks[-1]


def _mm(name, a, b, mode, out_dtype=F32, init=None, after=None):
    follow = () if after is None else (after,)
    if mode == "nn":
        (m, k), n = a.shape, b.shape[1]
    elif mode == "nt":
        (m, k), n = a.shape, b.shape[0]
    else:
        (k, m), n = a.shape, b.shape[1]
    tm, tn, tk = _mm_tiles(mode, m, n, k, a.dtype.itemsize, b.dtype.itemsize, jnp.dtype(out_dtype).itemsize,
                           0 if init is None else 4)
    nk = k // tk
    use_acc = nk > 1 and out_dtype != F32

    inits = () if init is None else (init,)

    def body(a_ref, b_ref, *rest):
        rest = rest[:len(inits)] + rest[len(inits) + len(follow):]
        o_ref, acc = rest[len(inits)], rest[len(inits) + 1:]
        part = _raw_dot(mode, a_ref[...], b_ref[...], False)
        first = lambda: part + rest[0][...] if inits else part
        if nk == 1:
            o_ref[...] = first().astype(o_ref.dtype)
            return
        acc_ref = acc[0] if use_acc else o_ref
        kk = pl.program_id(2)

        @pl.when(kk == 0)
        def _():
            acc_ref[...] = first()

        @pl.when(kk > 0)
        def _():
            acc_ref[...] += part

        if use_acc:
            @pl.when(kk == nk - 1)
            def _():
                o_ref[...] = acc_ref[...].astype(o_ref.dtype)

    if mode == "nn":
        a_spec = pl.BlockSpec((tm, tk), lambda i, j, kk: (i, kk))
        b_spec = pl.BlockSpec((tk, tn), lambda i, j, kk: (kk, j))
    elif mode == "nt":
        a_spec = pl.BlockSpec((tm, tk), lambda i, j, kk: (i, kk))
        b_spec = pl.BlockSpec((tn, tk), lambda i, j, kk: (j, kk))
    else:
        a_spec = pl.BlockSpec((tk, tm), lambda i, j, kk: (kk, i))
        b_spec = pl.BlockSpec((tk, tn), lambda i, j, kk: (kk, j))
    return pl.pallas_call(
        body, name=name, grid=(m // tm, n // tn, nk),
        in_specs=[a_spec, b_spec] + [pl.BlockSpec((tm, tn), lambda i, j, kk: (i, j))] * len(inits)
        + [pl.BlockSpec(memory_space=pl.ANY)] * len(follow),
        out_specs=pl.BlockSpec((tm, tn), lambda i, j, kk: (i, j)),
        out_shape=jax.ShapeDtypeStruct((m, n), out_dtype),
        scratch_shapes=[pltpu.VMEM((tm, tn), F32)] if use_acc else [],
        compiler_params=_cp("parallel", "parallel", "arbitrary"),
    )(a, b, *inits, *follow)


def _slab_layout(rows):
    offsets, at = [], 0
    for r in rows:
        at = -(-at // r) * r
        offsets.append(at)
        at += r
    return offsets, -(-at // 32) * 32


def _slab_zero_gaps(name, shape, rows, offsets):
    gaps = [(o + r, nxt) for o, r, nxt in zip(offsets, rows, offsets[1:] + [shape[1]]) if nxt > o + r]
    slab = None
    for i, (lo, hi) in enumerate(gaps):
        step = int(np.gcd(lo, hi - lo))

        def body(*refs):
            refs[-1][...] = jnp.zeros_like(refs[-1])

        slab = pl.pallas_call(
            body, name=f"{name}_{i}", grid=(shape[0], (hi - lo) // step), out_shape=jax.ShapeDtypeStruct(shape, F32),
            in_specs=[] if slab is None else [pl.BlockSpec(memory_space=pl.ANY)],
            out_specs=pl.BlockSpec((1, step, shape[2]), functools.partial(lambda k, j, b: (k, b + j, 0), b=lo // step)),
            input_output_aliases={} if slab is None else {0: 0}, compiler_params=_cp("parallel", "parallel"),
        )(*(() if slab is None else (slab,)))
    return slab


def _dw_into_slab(name, a, b, slab, shape, at):
    (k, m), n = a.shape, b.shape[1]
    r = m // N_CHIP
    fits = [(k // tk + 0.5 * (m // tm), tm, tk)
            for tk in (4096, 2048, 1024, 512, 256, 128) if k % tk == 0 for tm in (m, m // 2, r) if tm % LANES == 0
            if 2 * (tk * tm * a.dtype.itemsize + tk * n * b.dtype.itemsize + tm * n * 4) <= MM_VMEM_BUDGET]
    _, tm, tk = min(fits)
    per, nk = tm // r, k // tk

    def body(a_ref, b_ref, *rest):
        o_ref = rest[-1]
        part = _raw_dot("tn", a_ref[...], b_ref[...], False).reshape(o_ref.shape)
        if nk == 1:
            o_ref[...] = part
            return
        kk = pl.program_id(1)

        @pl.when(kk == 0)
        def _():
            o_ref[...] = part

        @pl.when(kk > 0)
        def _():
            o_ref[...] += part

    prev = () if slab is None else (slab,)
    return pl.pallas_call(
        body, name=name, grid=(m // tm, nk), out_shape=jax.ShapeDtypeStruct(shape, F32),
        in_specs=[pl.BlockSpec((tk, tm), lambda i, kk: (kk, i)), pl.BlockSpec((tk, n), lambda i, kk: (kk, 0))]
        + [pl.BlockSpec(memory_space=pl.ANY)] * len(prev),
        out_specs=pl.BlockSpec((per, r, n), lambda i, kk: (i, at // r, 0)),
        input_output_aliases={2: 0} if prev else {}, compiler_params=_cp("parallel", "arbitrary"),
    )(a, b, *prev)


def _rowwise(name, fn, rows, row_ins, full_ins, row_outs, acc_outs, tm=None):
    tm = tm or _pick(rows, (512, 256, 128))
    n_r, n_f, n_o, n_a = len(row_ins), len(full_ins), len(row_outs), len(acc_outs)

    def body(*refs):
        ins, outs = refs[:n_r + n_f], refs[n_r + n_f:]
        vals = [r[...].astype(F32) for r in ins]
        ro, ao = fn(*vals)
        for r, val in zip(outs[:n_o], ro):
            r[...] = val.astype(r.dtype)
        if n_a:
            @pl.when(pl.program_id(0) == 0)
            def _():
                for r in outs[n_o:]:
                    r[...] = jnp.zeros_like(r)

            for r, val in zip(outs[n_o:], ao):
                r[...] += val

    in_specs = [pl.BlockSpec((tm, w), functools.partial(lambda i, cb: (i, cb), cb=cb)) for _, w, cb in row_ins]
    in_specs += [pl.BlockSpec(a.shape, lambda i: (0, 0)) for a in full_ins]
    out_specs = [pl.BlockSpec((tm, w), lambda i: (i, 0)) for w, _ in row_outs]
    out_specs += [pl.BlockSpec(s, lambda i: (0, 0)) for s in acc_outs]
    out_shape = [jax.ShapeDtypeStruct((rows, w), dt) for w, dt in row_outs]
    out_shape += [jax.ShapeDtypeStruct(s, F32) for s in acc_outs]
    return pl.pallas_call(
        body, name=name, grid=(rows // tm,), in_specs=in_specs, out_specs=out_specs, out_shape=out_shape,
        compiler_params=_cp("arbitrary" if n_a else "parallel"),
    )(*[a for a, _, _ in row_ins], *full_ins)


def _rn(x):
    return x * lax.rsqrt(jnp.mean(x * x, axis=-1, keepdims=True) + 1e-6)


def _sigmoid(t):
    return 1.0 / (1.0 + jnp.exp(-t))


def _f_norm_mod(x, g, sh, sc):
    return _rn(x) * g * (1.0 + sc) + sh


def _f_post_res(xr, y, g, gate):
    return xr + gate * (_rn(y) * g)


@jax.custom_vjp
def _f_swiglu(g, u):
    return g * _sigmoid(g) * u


def _f_swiglu_fwd(g, u):
    s = _sigmoid(g)
    return g * s * u, (g, u, s)


def _f_swiglu_bwd(res, da):
    g, u, s = res
    gs = g * s
    return da * u * (s + gs * (1.0 - s)), da * gs


_f_swiglu.defvjp(_f_swiglu_fwd, _f_swiglu_bwd)


def _logsig(u):
    return jnp.minimum(u, 0.0) - jnp.log(1.0 + jnp.exp(-jnp.abs(u)))


def _f_gate(z, wf, wb, bf, bb):
    return _logsig(_nn(z, wf) + bf) / GATE_TAU, _logsig(_nn(z, wb) + bb) / GATE_TAU


def _f_gla_out(of, ob, gg, gt, bd):
    o = of + ob
    ms = _nn_hi(o * o, bd)
    return o * lax.rsqrt(ms + 1e-6) * gt * (gg * _sigmoid(gg))


def _norm_mod(name, x, g, sh, sc):
    rows, d = x.shape
    return _rowwise(name, lambda x, g, sh, sc: ((_f_norm_mod(x, g, sh, sc),), ()), rows,
                    [(x, d, 0)], [g, sh, sc], [(d, BF16)], [])[0]


def _rn_bwd(x, dn):
    r = lax.rsqrt(jnp.mean(x * x, axis=-1, keepdims=True) + 1e-6)
    n = x * r
    return r * (dn - n * jnp.mean(dn * n, axis=-1, keepdims=True)), n


def _norm_mod_grads(dh, x, g, sc):
    dx, n = _rn_bwd(x, dh * (g * (1.0 + sc)))
    t = jnp.sum(dh * n, axis=0, keepdims=True)
    return dx, (1.0 + sc) * t, jnp.sum(dh, axis=0, keepdims=True), g * t


def _post_res_grads(dout, y, g, gate):
    dy, n = _rn_bwd(y, dout * (gate * g))
    t = jnp.sum(dout * n, axis=0, keepdims=True)
    return dy, gate * t, g * t


def _norm_mod_bwd(name, dh, dres, x, g, sh, sc):
    rows, d = x.shape

    def fn(dh, dres, x, g, sh, sc):
        dx, dg, dsh, dsc = _norm_mod_grads(dh, x, g, sc)
        return (dx + dres,), (dg, dsh, dsc)

    return _rowwise(name, fn, rows, [(dh, d, 0), (dres, d, 0), (x, d, 0)], [g, sh, sc], [(d, F32)],
                    [(1, d)] * 3)


def _post_res_norm_mod(name, xr, y, g_post, gate, g_pre, sh, sc):
    rows, d = xr.shape

    def fn(xr, y, g_post, gate, g_pre, sh, sc):
        x1 = _f_post_res(xr, y, g_post, gate)
        return (x1, _f_norm_mod(x1, g_pre, sh, sc)), ()

    return _rowwise(name, fn, rows, [(xr, d, 0), (y, d, 0)], [g_post, gate, g_pre, sh, sc], [(d, F32), (d, BF16)], [])


def _norm_mod_post_res_bwd(name, dh, dres, x1, y, g_pre, sh, sc, g_post, gate):
    rows, d = x1.shape

    def fn(dh, dres, x1, y, g_pre, sh, sc, g_post, gate):
        dx1, dg_pre, dsh, dsc = _norm_mod_grads(dh, x1, g_pre, sc)
        dx1 = dx1 + dres
        dy, dg_post, dgate = _post_res_grads(dx1, y, g_post, gate)
        return (dx1, dy), (dg_pre, dsh, dsc, dg_post, dgate)

    return _rowwise(name, fn, rows, [(dh, d, 0), (dres, d, 0), (x1, d, 0), (y, d, 0)], [g_pre, sh, sc, g_post, gate],
                    [(d, F32), (d, BF16)], [(1, d)] * 5, tm=_pick(rows, (256, 128)))


def _post_res_loss(name, xr, y, g, gate, target):
    rows, d = xr.shape

    def fn(xr, y, target, g, gate):
        diff = _f_post_res(xr, y, g, gate) - target
        part = 0.5 * jnp.sum(jnp.mean(diff * diff, axis=-1, keepdims=True), axis=0, keepdims=True)
        dx2 = diff * (1.0 / d)
        dy, dg, dgate = _post_res_grads(dx2, y, g, gate)
        return (dx2, dy), (jnp.broadcast_to(part, (1, LANES)), dg, dgate)

    return _rowwise(name, fn, rows, [(xr, d, 0), (y, d, 0), (target, d, 0)], [g, gate], [(d, F32), (d, BF16)],
                    [(1, LANES), (1, d), (1, d)])


def _mm_rows(name, a, b, mode, fn, extras, outs):
    m, k = a.shape
    tm = _pick(m, (256, 128))

    def body(a_ref, b_ref, *rest):
        tiles = fn(_raw_dot(mode, a_ref[...], b_ref[...], False), *[e[...] for e in rest[:len(extras)]])
        for r, val in zip(rest[len(extras):], tiles):
            r[...] = val.astype(r.dtype)

    row = lambda w: pl.BlockSpec((tm, w), lambda i: (i, 0))
    return pl.pallas_call(
        body, name=name, grid=(m // tm,),
        in_specs=[row(k), pl.BlockSpec(b.shape, lambda i: (0, 0))] + [row(e.shape[1]) for e in extras],
        out_specs=[row(w) for w, _ in outs], out_shape=[jax.ShapeDtypeStruct((m, w), dt) for w, dt in outs],
        compiler_params=_cp("parallel"),
    )(a, b, *extras)


def _ffn_in_swiglu(name, h, w_t):
    f = w_t.shape[0] // 2
    fn = lambda u: (u, _f_swiglu(u[:, :f], u[:, f:]))
    return _mm_rows(name, h, w_t, "nt", fn, [], [(2 * f, BF16), (f, BF16)])


def _ffn_out_dx_swiglu_bwd(name, df, w_out, u):
    f = w_out.shape[0]

    def fn(da, u):
        u = u.astype(F32)
        _, vjp = jax.vjp(_f_swiglu, u[:, :f], u[:, f:])
        return (jnp.concatenate(vjp(da), axis=1),)

    return _mm_rows(name, df, w_out, "nt", fn, [u], [(2 * f, BF16)])[0]


def _gate_fwd(name, p, wf, wb, bf, bb):
    rows = p.shape[0]
    return _rowwise(name, lambda z, wf, wb, bf, bb: (_f_gate(z, wf, wb, bf, bb), ()), rows,
                    [(p, LANES, C_Z // LANES)], [wf, wb, bf, bb], [(GKW, F32)] * 2, [])


def _gate_bwd(name, p, dla_f, dla_b, wf, wb, bf, bb):
    rows = p.shape[0]

    def fn(z, dlf, dlb, wf, wb, bf, bb):
        _, vjp = jax.vjp(_f_gate, z, wf, wb, bf, bb)
        dz, dwf, dwb, dbf, dbb = vjp((dlf, dlb))
        return (dz,), (dwf, dwb, dbf, dbb)

    return _rowwise(name, fn, rows, [(p, LANES, C_Z // LANES), (dla_f, GKW, 0), (dla_b, GKW, 0)],
                    [wf, wb, bf, bb], [(LANES, BF16)], [(LANES, GKW), (LANES, GKW), (1, GKW), (1, GKW)])


def _head_mean_matrix():
    h = np.arange(GVW) // GLA_DV
    return jnp.asarray((h[:, None] == h[None, :]).astype(np.float32) / GLA_DV)


def _gla_out(name, attn, of, ob, p, gt):
    rows = of.shape[0]
    bd = _head_mean_matrix()
    fn = lambda attn, of, ob, gg, gt, bd: ((jnp.concatenate([attn, _f_gla_out(of, ob, gg, gt, bd)], axis=1),), ())
    return _rowwise(name, fn, rows, [(attn, QW, 0), (of, GVW, 0), (ob, GVW, 0), (p, GVW, C_GG // GVW)], [gt, bd],
                    [(MIX, BF16)], [])[0]


def _gla_out_bwd(name, dmix, of, ob, p, gt):
    rows = of.shape[0]
    bd = _head_mean_matrix()

    def fn(dm, of, ob, gg, gt, bd):
        _, vjp = jax.vjp(lambda of, gg, gt: _f_gla_out(of, ob, gg, gt, bd), of, gg, gt)
        do, dgg, dgt = vjp(dm)
        return (do, dgg), (dgt,)

    return _rowwise(name, fn, rows, [(dmix, GVW, 1), (of, GVW, 0), (ob, GVW, 0), (p, GVW, C_GG // GVW)], [gt, bd],
                    [(GVW, F32), (GVW, BF16)], [(1, GVW)])


def _rope_tables(n_tokens):
    t = jnp.arange(n_tokens)
    row = (t // GRID_W).astype(F32)
    col = (t % GRID_W).astype(F32)
    half = HEAD_DIM // 2
    inv_freq = ROPE_BASE ** (-jnp.arange(0, half, 2, dtype=F32) / half)
    ang_r = row[:, None] * inv_freq[None, :]
    ang_c = col[:, None] * inv_freq[None, :]
    ang = jnp.concatenate([ang_r, ang_r, ang_c, ang_c], axis=-1)
    sign = jnp.concatenate([-jnp.ones((16,), F32), jnp.ones((16,), F32)] * 2)
    cos, sin = jnp.cos(ang), jnp.sin(ang) * sign[None, :]
    return jnp.tile(cos, (1, 2)), jnp.tile(sin, (1, 2))


def _rot_pairs(x):
    w = x.shape[-1]
    lane = lax.broadcasted_iota(jnp.int32, x.shape, x.ndim - 1)
    return jnp.where((lane % 32) < 16, pltpu.roll(x, w - 16, x.ndim - 1), pltpu.roll(x, 16, x.ndim - 1))


def _rope_apply(x, cos, sin_signed, inverse):
    reps = x.shape[-1] // LANES
    cos = jnp.concatenate([cos] * reps, axis=-1) if reps > 1 else cos
    sin = jnp.concatenate([sin_signed] * reps, axis=-1) if reps > 1 else sin_signed
    if inverse:
        return x * cos + _rot_pairs(x * sin)
    return x * cos + _rot_pairs(x) * sin


def _rope_fwd(name, p, cos, sin):
    rows = p.shape[0]

    def fn(q, k, v, cos, sin):
        return (_rope_apply(q, cos, sin, False), _rope_apply(k, cos, sin, False), v), ()

    return _rowwise(name, fn, rows, [(p, QW, 0), (p, KVW, C_K // KVW), (p, KVW, C_V // KVW), (cos, LANES, 0),
                                     (sin, LANES, 0)], [], [(QW, BF16), (KVW, BF16), (KVW, BF16)], [])


def _proj_grad(name, dq_rot, dk_rot, dv, cos, sin, gla_f, gla_b, dgg, dz):
    rows = dq_rot.shape[0]

    def fn(dq, dk, dv, cos, sin, gqf, gkf, gvf, gqb, gkb, gvb, dgg, dz):
        parts = [_rope_apply(dq, cos, sin, True), gvf + gvb, dgg, _rope_apply(dk, cos, sin, True), dv, gqf + gqb,
                 gkf + gkb, dz]
        return (jnp.concatenate(parts, axis=1),), ()

    ins = [(dq_rot, QW), (dk_rot, KVW), (dv, KVW), (cos, LANES), (sin, LANES)]
    ins += [(t, t.shape[1]) for t in (*gla_f, *gla_b)] + [(dgg, GVW), (dz, LANES)]
    return _rowwise(name, fn, rows, [(t, w, 0) for t, w in ins], [], [(IN_PAD, BF16)], [],
                    tm=_pick(rows, (256, 128)))[0]


GROUP_ROWS = ATT_GROUP * BLOCK


ATT_SCALE = HEAD_DIM ** -0.5


def _attn_setup(sink, n, n_tokens):
    row = lax.broadcasted_iota(jnp.int32, (GROUP_ROWS, 1), 0)
    group = sum((row >= g * BLOCK).astype(jnp.int32) for g in range(1, ATT_GROUP))
    i = lax.broadcasted_iota(jnp.int32, (GROUP_ROWS, 3 * BLOCK), 0) - BLOCK * group
    j = lax.broadcasted_iota(jnp.int32, (GROUP_ROWS, 3 * BLOCK), 1)
    kpos = (n - 1) * BLOCK + j
    mask = (jnp.abs(j - BLOCK - i) <= WINDOW) & (kpos >= 0) & (kpos < n_tokens)
    head_id = lax.broadcasted_iota(jnp.int32, (1, ATT_HEADS), 1)
    sks = []
    for h in range(ATT_KV_HEADS):
        sk = jnp.zeros((GROUP_ROWS, 1), F32)
        for g in range(ATT_GROUP):
            one = jnp.sum(jnp.where(head_id == h * ATT_GROUP + g, sink, 0.0), axis=-1, keepdims=True)
            sk = jnp.where(group == g, one, sk)
        sks.append(sk)
    return mask, group, sks


def _attn_weights(q, kw, kc, sk, mask):
    q = q * ATT_SCALE
    s_w = jnp.where(mask, _raw_dot("nt", q, kw, False), NEG_INF)
    s_c = _raw_dot("nt", q, kc, False)
    m = jnp.maximum(jnp.maximum(jnp.max(s_w, axis=-1, keepdims=True), jnp.max(s_c, axis=-1, keepdims=True)), sk)
    pw, pc, ps = jnp.exp(s_w - m), jnp.exp(s_c - m), jnp.exp(sk - m)
    return q, pw, pc, ps, jnp.sum(pw, axis=-1, keepdims=True) + jnp.sum(pc, axis=-1, keepdims=True) + ps


def _f_attn(qs, kws, vws, kcs, vcs, sink, n, n_tokens):
    mask, _, sks = _attn_setup(sink, n, n_tokens)
    outs = []
    for h in range(ATT_KV_HEADS):
        _, pw, pc, _, den = _attn_weights(qs[h], kws[h], kcs[h], sks[h], mask)
        outs.append((_raw_dot("nn", pw, vws[h], False) + _raw_dot("nn", pc, vcs[h], False)) / den)
    return tuple(outs)


def _f_attn_bwd(qs, kws, vws, kcs, vcs, sink, outs, douts, n, n_tokens):
    mask, group, sks = _attn_setup(sink, n, n_tokens)
    head_id = lax.broadcasted_iota(jnp.int32, (1, ATT_HEADS), 1)
    dot = lambda mode, a, b: _raw_dot(mode, a, b, False)
    dqs, dkws, dvws, dkcs, dvcs, dsink = [], [], [], [], [], jnp.zeros((1, ATT_HEADS), F32)
    for h in range(ATT_KV_HEADS):
        q, pw, pc, ps, den = _attn_weights(qs[h], kws[h], kcs[h], sks[h], mask)
        inv = 1.0 / den
        pw, pc = pw * inv, pc * inv
        dd = jnp.sum(douts[h] * outs[h], axis=-1, keepdims=True)
        dsw = pw * (dot("nt", douts[h], vws[h]) - dd)
        dsc = pc * (dot("nt", douts[h], vcs[h]) - dd)
        dqs.append((dot("nn", dsw, kws[h]) + dot("nn", dsc, kcs[h])) * ATT_SCALE)
        dkws.append(dot("tn", dsw, q))
        dkcs.append(dot("tn", dsc, q))
        dvws.append(dot("tn", pw, douts[h]))
        dvcs.append(dot("tn", pc, douts[h]))
        dsk = -(ps * inv) * dd
        for g in range(ATT_GROUP):
            one = jnp.sum(jnp.where(group == g, dsk, 0.0), axis=0, keepdims=True)
            dsink = dsink + jnp.where(head_id == h * ATT_GROUP + g, one, 0.0)
    return dqs, dkws, dvws, dkcs, dvcs, dsink


def _group_rows(ref, h):
    hs = lambda hq: slice(hq * HEAD_DIM, (hq + 1) * HEAD_DIM)
    return jnp.concatenate([ref[:, hs(h * ATT_GROUP + g)].astype(F32) for g in range(ATT_GROUP)], axis=0)


def _ungroup_rows(ref, h, val):
    for g in range(ATT_GROUP):
        hq = h * ATT_GROUP + g
        ref[:, hq * HEAD_DIM:(hq + 1) * HEAD_DIM] = val[g * BLOCK:(g + 1) * BLOCK].astype(ref.dtype)


def _attn_loads(n, q_ref, kp_ref, vp_ref, kc_ref, vc_ref):
    r0 = pl.multiple_of(n * BLOCK, BLOCK)
    hs = lambda h: slice(h * HEAD_DIM, (h + 1) * HEAD_DIM)
    qs = [_group_rows(q_ref, h) for h in range(ATT_KV_HEADS)]
    kws = [kp_ref[pl.ds(r0, 3 * BLOCK), hs(h)].astype(F32) for h in range(ATT_KV_HEADS)]
    vws = [vp_ref[pl.ds(r0, 3 * BLOCK), hs(h)].astype(F32) for h in range(ATT_KV_HEADS)]
    kcs = [kc_ref[:, hs(h)].astype(F32) for h in range(ATT_KV_HEADS)]
    vcs = [vc_ref[:, hs(h)].astype(F32) for h in range(ATT_KV_HEADS)]
    return r0, hs, qs, kws, vws, kcs, vcs


def _attn_specs(s, c):
    full = lambda shape: pl.BlockSpec(shape, lambda n: (0, 0))
    return [pl.BlockSpec((BLOCK, QW), lambda n: (n, 0)), full((s + 2 * BLOCK, KVW)), full((s + 2 * BLOCK, KVW)),
            full((c, KVW)), full((c, KVW)), full((1, ATT_HEADS))]


def _attn_fwd(q, kp, vp, kc, vc, sink):
    s, c = q.shape[0], kc.shape[0]

    def body(q_ref, kp_ref, vp_ref, kc_ref, vc_ref, sink_ref, o_ref):
        n = pl.program_id(0)
        _, hs, qs, kws, vws, kcs, vcs = _attn_loads(n, q_ref, kp_ref, vp_ref, kc_ref, vc_ref)
        outs = _f_attn(qs, kws, vws, kcs, vcs, sink_ref[...], n, s)
        for h in range(ATT_KV_HEADS):
            _ungroup_rows(o_ref, h, outs[h])

    return pl.pallas_call(
        body, name="attn_fwd", grid=(s // BLOCK,), in_specs=_attn_specs(s, c),
        out_specs=pl.BlockSpec((BLOCK, QW), lambda n: (n, 0)), out_shape=jax.ShapeDtypeStruct((s, QW), BF16),
        compiler_params=_cp("parallel"),
    )(q, kp, vp, kc, vc, sink)


def _attn_bwd(do, o, q, kp, vp, kc, vc, sink):
    s, c = q.shape[0], kc.shape[0]

    def body(do_ref, o_ref, q_ref, kp_ref, vp_ref, kc_ref, vc_ref, sink_ref, dq_ref, dkp_ref, dvp_ref, dkc_ref,
             dvc_ref, dsink_ref):
        n = pl.program_id(0)

        @pl.when(n == 0)
        def _():
            for r in (dkp_ref, dvp_ref, dkc_ref, dvc_ref, dsink_ref):
                r[...] = jnp.zeros_like(r)

        r0, hs, qs, kws, vws, kcs, vcs = _attn_loads(n, q_ref, kp_ref, vp_ref, kc_ref, vc_ref)
        heads = range(ATT_KV_HEADS)
        dqs, dkws, dvws, dkcs, dvcs, dsink = _f_attn_bwd(
            qs, kws, vws, kcs, vcs, sink_ref[...], [_group_rows(o_ref, h) for h in heads],
            [_group_rows(do_ref, h) for h in heads], n, s)
        for h in heads:
            _ungroup_rows(dq_ref, h, dqs[h])
            dkp_ref[pl.ds(r0, 3 * BLOCK), hs(h)] += dkws[h]
            dvp_ref[pl.ds(r0, 3 * BLOCK), hs(h)] += dvws[h]
            dkc_ref[:, hs(h)] += dkcs[h]
            dvc_ref[:, hs(h)] += dvcs[h]
        dsink_ref[...] += dsink

    full = lambda shape: pl.BlockSpec(shape, lambda n: (0, 0))
    return pl.pallas_call(
        body, name="attn_bwd", grid=(s // BLOCK,),
        in_specs=[pl.BlockSpec((BLOCK, QW), lambda n: (n, 0))] * 2 + _attn_specs(s, c),
        out_specs=[pl.BlockSpec((BLOCK, QW), lambda n: (n, 0)), full((s + 2 * BLOCK, KVW)), full((s + 2 * BLOCK, KVW)),
                   full((c, KVW)), full((c, KVW)), full((1, ATT_HEADS))],
        out_shape=[jax.ShapeDtypeStruct((s, QW), F32), jax.ShapeDtypeStruct((s + 2 * BLOCK, KVW), F32),
                   jax.ShapeDtypeStruct((s + 2 * BLOCK, KVW), F32), jax.ShapeDtypeStruct((c, KVW), F32),
                   jax.ShapeDtypeStruct((c, KVW), F32), jax.ShapeDtypeStruct((1, ATT_HEADS), F32)],
        compiler_params=_cp("arbitrary"),
    )(do, o, q, kp, vp, kc, vc, sink)


GLA_GROUPS = 1
GLA_GROUP_HEADS = GLA_HEADS // GLA_GROUPS
GKG, GVG = GKW // GLA_GROUPS, GVW // GLA_GROUPS


def _gla_masks(heads=GLA_HEADS):
    hk = np.arange(heads * GLA_DK) // GLA_DK
    hv = np.arange(heads * GLA_DV) // GLA_DV
    head_k = (np.arange(heads)[:, None] == hk[None, :]).astype(np.float32)
    head_v = (np.arange(heads)[:, None] == hv[None, :]).astype(np.float32)
    bd_t = (hv[:, None] == hk[None, :]).astype(np.float32)
    return jnp.asarray(head_k), jnp.asarray(head_v), jnp.asarray(bd_t)


def _group_states(st):
    return jnp.stack([st[g * GVG:(g + 1) * GVG, g * GKG:(g + 1) * GKG] for g in range(GLA_GROUPS)])


def _ungroup_states(st):
    out = jnp.zeros((GVW, GKW), st.dtype)
    for g in range(GLA_GROUPS):
        out = out.at[g * GVG:(g + 1) * GVG, g * GKG:(g + 1) * GKG].set(st[g])
    return out


def _tri(n, rev, strict=False):
    i = lax.broadcasted_iota(jnp.int32, (n, n), 0)
    j = lax.broadcasted_iota(jnp.int32, (n, n), 1)
    if strict:
        keep = (j > i) if rev else (j < i)
    else:
        keep = (j >= i) if rev else (j <= i)
    return keep


def _f_gla_chunk(q, k, v, la, st, head_k, head_v, bd_t, rev):
    return _f_gla_carry(*_f_gla_intra(q, k, v, la, head_k, head_v, rev), v, st, bd_t)


def _f_gla_intra(q, k, v, la, head_k, head_v, rev):
    heads, kw, vw = head_k.shape[0], q.shape[1], v.shape[1]
    keep = _tri(GLA_CHUNK, rev)
    b = _nn_hi(keep.astype(F32), la)
    bl = jnp.sum(la, axis=0, keepdims=True)
    qd = q * (GLA_DK ** -0.5) * jnp.exp(b)
    ki = k * jnp.exp(-b)
    kd = k * jnp.exp(bl - b)
    q_heads = (qd[None, :, :] * head_k[:, None, :]).reshape(heads * GLA_CHUNK, kw)
    a_all = _nt(q_heads, ki).reshape(heads, GLA_CHUNK, GLA_CHUNK)
    a_all = jnp.where(keep[None, :, :], a_all, 0.0).reshape(heads * GLA_CHUNK, GLA_CHUNK)
    o_all = _nn(a_all, v).reshape(heads, GLA_CHUNK, vw)
    return jnp.sum(o_all * head_v[:, None, :], axis=0), qd, kd, bl


def _f_gla_carry(intra, qd, kd, bl, v, st, bd_t):
    return intra + _nt(qd, st), st * jnp.exp(bl) + bd_t * _tn(v, kd)


def _gla_specs(s, tb, order):
    return [pl.BlockSpec((tb, GKW), lambda i: (order(i), C_GQ // GKW)),
            pl.BlockSpec((tb, GKW), lambda i: (order(i), C_GK // GKW)),
            pl.BlockSpec((tb, GVW), lambda i: (order(i), C_GV // GVW)),
            pl.BlockSpec((tb, GKW), lambda i: (order(i), 0))]


GLA_BLOCK_CHUNKS = 4


def _gla_fwd(p, la_f, la_b, st_f0, st_b0):
    s = p.shape[0]
    tb = GLA_BLOCK_CHUNKS * GLA_CHUNK
    nblk = s // tb
    up, down = (lambda i: i), (lambda i: nblk - 1 - i)
    masks = _gla_masks(GLA_GROUP_HEADS)

    def scan(rev, q_ref, k_ref, v_ref, la_ref, o_ref, sts_ref, st_ref, consts):
        for g in range(GLA_GROUPS):
            gk, gv = slice(g * GKG, (g + 1) * GKG), slice(g * GVG, (g + 1) * GVG)
            st = st_ref[g]
            sts_ref[0, g] = st
            chunks = range(GLA_BLOCK_CHUNKS)
            for ci in (reversed(chunks) if rev else chunks):
                rows = slice(ci * GLA_CHUNK, (ci + 1) * GLA_CHUNK)
                o, st = _f_gla_chunk(q_ref[rows, gk], k_ref[rows, gk], v_ref[rows, gv], la_ref[rows, gk], st, *consts,
                                     rev)
                o_ref[rows, gv] = o
            st_ref[g] = st

    def body(qf, kf, vf, laf, qb, kb, vb, lab, stf0, stb0, hk_ref, hv_ref, bd_ref, of_ref, stsf_ref, ob_ref, stsb_ref,
             stf_ref, stb_ref):
        @pl.when(pl.program_id(0) == 0)
        def _():
            stf_ref[...] = stf0[...]
            stb_ref[...] = stb0[...]

        consts = (hk_ref[...], hv_ref[...], bd_ref[...])
        scan(False, qf, kf, vf, laf, of_ref, stsf_ref, stf_ref, consts)
        scan(True, qb, kb, vb, lab, ob_ref, stsb_ref, stb_ref, consts)

    full = lambda a: pl.BlockSpec(a.shape, lambda i: (0,) * a.ndim)
    outs = lambda order: [pl.BlockSpec((tb, GVW), lambda i: (order(i), 0)),
                          pl.BlockSpec((1, GLA_GROUPS, GVG, GKG), lambda i: (order(i), 0, 0, 0))]
    return pl.pallas_call(
        body, name="gla_fwd", grid=(nblk,),
        in_specs=_gla_specs(s, tb, up) + _gla_specs(s, tb, down) + [full(st_f0), full(st_b0)]
        + [full(m) for m in masks],
        out_specs=outs(up) + outs(down),
        out_shape=[jax.ShapeDtypeStruct((s, GVW), F32), jax.ShapeDtypeStruct((nblk, GLA_GROUPS, GVG, GKG), F32)] * 2,
        scratch_shapes=[pltpu.VMEM((GLA_GROUPS, GVG, GKG), F32)] * 2,
        compiler_params=_cp("arbitrary"),
    )(p, p, p, la_f, p, p, p, la_b, st_f0, st_b0, *masks)


def _gla_bwd(p, la_f, la_b, sts_f, sts_b, do, after=None):
    s = p.shape[0]
    tb = GLA_BLOCK_CHUNKS * GLA_CHUNK
    nblk = s // tb
    up, down = (lambda i: i), (lambda i: nblk - 1 - i)
    masks = _gla_masks(GLA_GROUP_HEADS)
    follow = () if after is None else (after,)

    def back(rev, q_ref, k_ref, v_ref, la_ref, sts_ref, do_ref, dq_ref, dk_ref, dv_ref, dla_ref, dst0_ref, dst_ref,
             consts):
        def block(q, k, v, la, st):
            outs = [None] * GLA_BLOCK_CHUNKS
            chunks = range(GLA_BLOCK_CHUNKS)
            for ci in (reversed(chunks) if rev else chunks):
                outs[ci], st = _f_gla_chunk(q[ci], k[ci], v[ci], la[ci], st, *consts, rev)
            return tuple(outs), st

        for g in range(GLA_GROUPS):
            gk, gv = slice(g * GKG, (g + 1) * GKG), slice(g * GVG, (g + 1) * GVG)
            split = lambda r, cols: tuple(r[ci * GLA_CHUNK:(ci + 1) * GLA_CHUNK, cols].astype(F32)
                                          for ci in range(GLA_BLOCK_CHUNKS))
            _, vjp = jax.vjp(block, split(q_ref, gk), split(k_ref, gk), split(v_ref, gv), split(la_ref, gk),
                             sts_ref[0, g])
            dq, dk, dv, dla, dst = vjp((split(do_ref, gv), dst_ref[g]))
            for ci in range(GLA_BLOCK_CHUNKS):
                rows = slice(ci * GLA_CHUNK, (ci + 1) * GLA_CHUNK)
                dq_ref[rows, gk], dk_ref[rows, gk], dv_ref[rows, gv], dla_ref[rows, gk] = dq[ci], dk[ci], dv[ci], dla[ci]
            dst_ref[g] = dst
            dst0_ref[g] = dst

    def body(*refs):
        ins, (hk_ref, hv_ref, bd_ref) = refs[:12], refs[12:15]
        outs = refs[15 + len(follow):]

        @pl.when(pl.program_id(0) == 0)
        def _():
            outs[10][...] = jnp.zeros_like(outs[10])
            outs[11][...] = jnp.zeros_like(outs[11])

        consts = (hk_ref[...], hv_ref[...], bd_ref[...])
        back(False, *ins[:6], *outs[:5], outs[10], consts)
        back(True, *ins[6:], *outs[5:10], outs[11], consts)

    full = lambda a: pl.BlockSpec(a.shape, lambda i: (0,) * a.ndim)

    def ins(order):
        return _gla_specs(s, tb, order) + [pl.BlockSpec((1, GLA_GROUPS, GVG, GKG), lambda i: (order(i), 0, 0, 0)),
                                           pl.BlockSpec((tb, GVW), lambda i: (order(i), 0))]

    def outs(order):
        blk = lambda w: pl.BlockSpec((tb, w), lambda i: (order(i), 0))
        return [blk(GKW), blk(GKW), blk(GVW), blk(GKW), pl.BlockSpec((GLA_GROUPS, GVG, GKG), lambda i: (0, 0, 0))]

    shapes = [jax.ShapeDtypeStruct((s, GKW), F32), jax.ShapeDtypeStruct((s, GKW), F32),
              jax.ShapeDtypeStruct((s, GVW), F32), jax.ShapeDtypeStruct((s, GKW), F32),
              jax.ShapeDtypeStruct((GLA_GROUPS, GVG, GKG), F32)]
    both = pl.pallas_call(
        body, name="gla_bwd", grid=(nblk,),
        in_specs=ins(down) + ins(up) + [full(m) for m in masks] + [pl.BlockSpec(memory_space=pl.ANY)] * len(follow),
        out_specs=outs(down) + outs(up), out_shape=shapes * 2,
        scratch_shapes=[pltpu.VMEM((GLA_GROUPS, GVG, GKG), F32)] * 2,
        compiler_params=_cp("arbitrary"),
    )(p, p, p, la_f, sts_f, do, p, p, p, la_b, sts_b, do, *masks, *follow)
    return both[:5], both[5:]


def _f_ctx_state(k, v, la_f, la_b, bd_t):
    c = k.shape[0]
    after = _nn_hi(_tri(c, True, strict=True).astype(F32), la_f)
    before = _nn_hi(_tri(c, False, strict=True).astype(F32), la_b)
    return bd_t * _tn(v, k * jnp.exp(after)), bd_t * _tn(v, k * jnp.exp(before))


def _ctx_state(pc, la_f, la_b):
    c = pc.shape[0]
    bd_t = _gla_masks()[2]

    def body(k_ref, v_ref, lf_ref, lb_ref, bd_ref, sf_ref, sb_ref):
        sf_ref[...], sb_ref[...] = _f_ctx_state(k_ref[...], v_ref[...], lf_ref[...], lb_ref[...], bd_ref[...])

    full = lambda a: pl.BlockSpec(a.shape, lambda i: (0, 0))
    return pl.pallas_call(
        body, name="ctx_state_fwd", grid=(1,),
        in_specs=[pl.BlockSpec((c, GKW), lambda i: (0, C_GK // GKW)), pl.BlockSpec((c, GVW), lambda i: (0, C_GV // GVW)),
                  full(la_f), full(la_b), full(bd_t)],
        out_specs=[pl.BlockSpec((GVW, GKW), lambda i: (0, 0))] * 2,
        out_shape=[jax.ShapeDtypeStruct((GVW, GKW), F32)] * 2,
        compiler_params=_cp("arbitrary"),
    )(pc, pc, la_f, la_b, bd_t)


def _ctx_state_bwd(pc, la_f, la_b, dsf, dsb):
    c = pc.shape[0]
    bd_t = _gla_masks()[2]

    def body(k_ref, v_ref, lf_ref, lb_ref, bd_ref, dsf_ref, dsb_ref, dk_ref, dv_ref, dlf_ref, dlb_ref):
        _, vjp = jax.vjp(lambda k, v, lf, lb: _f_ctx_state(k, v, lf, lb, bd_ref[...]),
                         k_ref[...], v_ref[...], lf_ref[...], lb_ref[...])
        dk, dv, dlf, dlb = vjp((dsf_ref[...], dsb_ref[...]))
        dk_ref[...], dv_ref[...] = dk.astype(BF16), dv.astype(BF16)
        dlf_ref[...], dlb_ref[...] = dlf, dlb

    full = lambda a: pl.BlockSpec(a.shape, lambda i: (0, 0))
    return pl.pallas_call(
        body, name="ctx_state_bwd", grid=(1,),
        in_specs=[pl.BlockSpec((c, GKW), lambda i: (0, C_GK // GKW)), pl.BlockSpec((c, GVW), lambda i: (0, C_GV // GVW)),
                  full(la_f), full(la_b), full(bd_t), full(dsf), full(dsb)],
        out_specs=[pl.BlockSpec((c, GKW), lambda i: (0, 0)), pl.BlockSpec((c, GVW), lambda i: (0, 0)),
                   pl.BlockSpec((c, GKW), lambda i: (0, 0)), pl.BlockSpec((c, GKW), lambda i: (0, 0))],
        out_shape=[jax.ShapeDtypeStruct((c, GKW), BF16), jax.ShapeDtypeStruct((c, GVW), BF16),
                   jax.ShapeDtypeStruct((c, GKW), F32), jax.ShapeDtypeStruct((c, GKW), F32)],
        compiler_params=_cp("arbitrary"),
    )(pc, pc, la_f, la_b, bd_t, dsf, dsb)


_SRC_COLS = ((0, QW), (QW + 2 * KVW + 2 * GKW, GVW), (QW + 2 * KVW + 2 * GKW + GVW, GVW), (QW, KVW), (QW + KVW, KVW),
             (QW + 2 * KVW, GKW), (QW + 2 * KVW + GKW, GKW), (IN_COLS - 2 * GATE_RANK, 2 * GATE_RANK))
_DST_COLS = (C_Q, C_GV, C_GG, C_K, C_V, C_GQ, C_GK, C_Z)


def _pack_w_in(w_in):
    parts = [w_in[:, s:s + n] for s, n in _SRC_COLS]
    parts.append(jnp.zeros((w_in.shape[0], IN_PAD - C_Z - 2 * GATE_RANK), w_in.dtype))
    return jnp.concatenate(parts, axis=1)


def _unpack_w_in_grad(g):
    by_src = sorted(zip(_SRC_COLS, _DST_COLS))
    return jnp.concatenate([g[:, d:d + n] for (_, n), d in by_src], axis=1)


def _prep_gate_weights(w_gate_fwd, w_gate_bwd):
    pad_rows = lambda w, at: jnp.zeros((LANES, GKW), F32).at[at:at + GATE_RANK].set(w)
    return {"wg_f": pad_rows(w_gate_fwd, 0), "wg_b": pad_rows(w_gate_bwd, GATE_RANK)}


def _local_step(x, ctx, target, ada, ada_c, w, late_weights, reduce_behind=None, reduce_w_in=None):
    s, d = x.shape
    sh1, sc1, gt1, sh2, sc2, gt2 = [ada[:, i * d:(i + 1) * d] for i in range(6)]
    sh1c, sc1c = ada_c[:, :d], ada_c[:, d:2 * d]
    cos, sin = _rope_tables(s)
    gt = jnp.tile(w["g_gla_norm"], (1, GLA_HEADS))

    h = _norm_mod("pre_mix", x, w["g_pre_mix"], sh1, sc1)
    hc = _norm_mod("pre_mix_ctx", ctx, w["g_pre_mix"], sh1c, sc1c)
    w_in, token = w["w_in"](h, cos, sin)
    p = _mm("proj_in", h, w_in, "nn", after=token)
    pc = _mm("proj_in_ctx", hc, w_in, "nn")
    q_rot, k_rot, v_b = _rope_fwd("rope", p, cos, sin)
    pad = ((BLOCK, BLOCK), (0, 0))
    kp, vp = jnp.pad(k_rot, pad), jnp.pad(v_b, pad)
    kc, vc = pc[:, C_K:C_K + KVW].astype(BF16), pc[:, C_V:C_V + KVW].astype(BF16)
    attn = _attn_fwd(q_rot, kp, vp, kc, vc, w["attn_sink"])
    gate_w = (w["wg_f"], w["wg_b"], w["b_gate_fwd"], w["b_gate_bwd"])
    la_f, la_b = _gate_fwd("gate", p, *gate_w)
    la_fc, la_bc = _gate_fwd("gate_ctx", pc, *gate_w)
    st_f0, st_b0 = _ctx_state(pc, la_fc, la_bc)
    o_f, sts_f, o_b, sts_b = _gla_fwd(p, la_f, la_b, _group_states(st_f0), _group_states(st_b0))
    mix = _gla_out("gla_out", attn, o_f, o_b, p, gt)
    w_out, w_ffn_in_t, w_ffn_out = late_weights(attn)
    y = _mm("proj_out", mix, w_out, "nn", BF16)
    x1, h2 = _post_res_norm_mod("post_mix_pre_ffn", x, y, w["g_post_mix"], gt1, w["g_pre_ffn"], sh2, sc2)
    u, a = _ffn_in_swiglu("ffn_in", h2, w_ffn_in_t)
    f = _mm("ffn_out", a, w_ffn_out, "nn", BF16)
    g = {}
    dx2, df, loss, g["g_post_ffn"], dgt2 = _post_res_loss("post_ffn_loss", x1, f, w["g_post_ffn"], gt2, target)

    late_rows = {"w_ffn_in_t": w_ffn_in_t.shape[0] // N_CHIP, "w_ffn_out": w_ffn_out.shape[0] // N_CHIP,
                 "w_out": w_out.shape[0] // N_CHIP}
    order = sorted(late_rows, key=lambda n: -late_rows[n])
    offsets, slab_rows = _slab_layout([late_rows[n] for n in order])
    late_at, slab_shape = dict(zip(order, offsets)), (N_CHIP, slab_rows, d)
    slab = _slab_zero_gaps("late_grads_gaps", slab_shape, [late_rows[n] for n in order], offsets)
    slab = _dw_into_slab("ffn_out_dw", a, df, slab, slab_shape, late_at["w_ffn_out"])
    du = _ffn_out_dx_swiglu_bwd("ffn_out_dx", df, w_ffn_out, u)
    dh2 = _mm("ffn_in_dx", du, w_ffn_in_t, "nn", BF16)
    slab = _dw_into_slab("ffn_in_dw", du, h2, slab, slab_shape, late_at["w_ffn_in_t"])
    dx1, dy, g["g_pre_ffn"], dsh2, dsc2, g["g_post_mix"], dgt1 = _norm_mod_post_res_bwd(
        "pre_ffn_post_mix_bwd", dh2, dx2, x1, y, w["g_pre_ffn"], sh2, sc2, w["g_post_mix"], gt1)
    dmix = _mm("proj_out_dx", dy, w_out, "nt", BF16)
    slab = _dw_into_slab("proj_out_dw", mix, dy, slab, slab_shape, late_at["w_out"])
    g["late"], g["late_at"], g["late_rows"] = slab, late_at, late_rows
    rb, sink, token = reduce_behind, w["attn_sink"], None
    if rb is not None:
        gt = _behind(gt, rb.start_slab(slab))
    d_o, dgg, dgt = _gla_out_bwd("gla_out_bwd", dmix, o_f, o_b, p, gt)
    g["g_gla_norm"] = jnp.sum(dgt.reshape(GLA_HEADS, GLA_DV), axis=0, keepdims=True)
    if rb is not None:
        token = rb.pair(dgg)
    gla_f, gla_b = _gla_bwd(p, la_f, la_b, sts_f, sts_b, d_o, token)
    (dla_f, dst_f0), (dla_b, dst_b0) = gla_f[3:], gla_b[3:]
    dst_f0, dst_b0 = _ungroup_states(dst_f0), _ungroup_states(dst_b0)
    if rb is not None:
        sink = _behind(sink, rb.total(dla_b))
    dgkc, dgvc, dla_fc, dla_bc = _ctx_state_bwd(pc, la_fc, la_bc, dst_f0, dst_b0)
    dz, dwf, dwb, dbf, dbb = _gate_bwd("gate_bwd", p, dla_f, dla_b, *gate_w)
    dzc, dwfc, dwbc, dbfc, dbbc = _gate_bwd("gate_ctx_bwd", pc, dla_fc, dla_bc, *gate_w)
    g["w_gate_fwd"] = (dwf + dwfc)[:GATE_RANK]
    g["w_gate_bwd"] = (dwb + dwbc)[GATE_RANK:2 * GATE_RANK]
    g["b_gate_fwd"], g["b_gate_bwd"] = dbf + dbfc, dbb + dbbc
    dq_rot, dkp, dvp, dkc, dvc, g["attn_sink"] = _attn_bwd(dmix, attn, q_rot, kp, vp, kc, vc, sink)
    if rb is not None:
        g["late"] = rb.result(dq_rot)
    dp = _proj_grad("proj_grad", dq_rot, dkp[BLOCK:BLOCK + s], dvp[BLOCK:BLOCK + s], cos, sin, gla_f[:3], gla_b[:3],
                    dgg, dz)
    c_rows = ctx.shape[0]
    zeros = lambda n: jnp.zeros((c_rows, n), BF16)
    dpc = jnp.concatenate([zeros(QW), dgvc, zeros(GVW), dkc.astype(BF16), dvc.astype(BF16), zeros(GKW), dgkc, dzc],
                          axis=1)
    g["w_in"] = _mm("proj_in_dw", h, dp, "tn", init=_mm("proj_in_ctx_dw", hc, dpc, "tn"))
    token = None if reduce_w_in is None else reduce_w_in.start(g["w_in"])
    dh = _mm("proj_in_dx", dp, w_in, "nt", BF16, after=token)
    dhc = _mm("proj_in_ctx_dx", dpc, w_in, "nt")
    if reduce_w_in is not None:
        sh1 = _behind(sh1, reduce_w_in.pair(dh))
    dx, dg_a, dsh1, dsc1 = _norm_mod_bwd("pre_mix_bwd", dh, dx1, x, w["g_pre_mix"], sh1, sc1)
    if reduce_w_in is not None:
        dsh1 = _behind(dsh1, reduce_w_in.total(dx))
    _, dg_b, dsh1c, dsc1c = _norm_mod_bwd("pre_mix_ctx_bwd", dhc, jnp.zeros_like(dhc), ctx, w["g_pre_mix"], sh1c,
                                          sc1c)
    g["g_pre_mix"] = dg_a + dg_b
    d_ada = jnp.concatenate([dsh1, dsc1, dgt1, dsh2, dsc2, dgt2], axis=1)
    d_ada_c = jnp.concatenate([dsh1c, dsc1c, jnp.zeros((1, 4 * d), F32)], axis=1)
    return loss, dx, g, d_ada, d_ada_c


HBM = pl.BlockSpec(memory_space=pltpu.HBM)
N_DEV, N_CHIP = 8, 4


def _place():
    x, y, c = lax.axis_index("x"), lax.axis_index("y"), lax.axis_index("c")
    return x, y, c, [(1 - x, y), (x, 1 - y), (1 - x, 1 - y)]


def _row_tile(n, mult, cap):
    return max(t for t in range(mult, min(n, cap) + 1, mult) if n % t == 0)


def _ag_small(name, v, after=None):
    follow = () if after is None else (after,)

    def body(v_ref, *rest):
        out_ref, send_sems, recv_sems = rest[len(follow):]
        x, y, c, _ = _place()
        out_ref[4 * x + 2 * y + c] = v_ref[...]

        def peer(r):
            return ((1 - x) if r & 4 else x, (1 - y) if r & 2 else y, (1 - c) if r & 1 else c)

        def copy(r, block):
            px, py, pc = block
            return pltpu.make_async_remote_copy(
                src_ref=v_ref, dst_ref=out_ref.at[4 * px + 2 * py + pc], send_sem=send_sems.at[r - 1],
                recv_sem=recv_sems.at[r - 1], device_id=peer(r), device_id_type=MESH)

        sends = [copy(r, (x, y, c)) for r in range(1, N_DEV)]
        for cp in sends:
            cp.start()
        for r in range(1, N_DEV):
            copy(r, peer(r)).wait_recv()
        for cp in sends:
            cp.wait_send()

    return pl.pallas_call(
        body, name=name, out_shape=jax.ShapeDtypeStruct((N_DEV,) + v.shape, v.dtype),
        in_specs=[pl.BlockSpec(memory_space=pltpu.VMEM)] + [pl.BlockSpec(memory_space=pl.ANY)] * len(follow),
        out_specs=pl.BlockSpec(memory_space=pltpu.VMEM),
        scratch_shapes=[pltpu.SemaphoreType.DMA((N_DEV - 1,)), pltpu.SemaphoreType.DMA((N_DEV - 1,))],
    )(v, *follow)


def _halves(c, rows, mult):
    hr = rows // 2
    return pl.ds(pl.multiple_of(c * hr, mult), hr), pl.ds(pl.multiple_of((1 - c) * hr, mult), hr)


def _add_half(name, g, a, c_idx):
    n_sh, hr, n = a.shape
    tr = _row_tile(hr, 16, 1024)
    nb = hr // tr

    def body(c_ref, g_ref, a_ref, o_ref):
        o_ref[...] = (g_ref[...] + a_ref[...]).astype(o_ref.dtype)

    return pl.pallas_call(
        body, name=name, out_shape=jax.ShapeDtypeStruct(a.shape, BF16),
        grid_spec=pltpu.PrefetchScalarGridSpec(
            num_scalar_prefetch=1, grid=(n_sh, nb),
            in_specs=[pl.BlockSpec((1, tr, n), lambda s, i, c_ref: (s, c_ref[0] * nb + i, 0)),
                      pl.BlockSpec((1, tr, n), lambda s, i, c_ref: (s, i, 0))],
            out_specs=pl.BlockSpec((1, tr, n), lambda s, i, c_ref: (s, i, 0))),
        compiler_params=_cp("parallel", "parallel"),
    )(c_idx, g, a)


def _sum_chips(name, b, c_idx):
    n_sh, hr, n = b.shape
    tr = _row_tile(hr, 16, 1024)
    nb = hr // tr

    def body(c_ref, b0, b1, b2, b3, o_ref):
        o_ref[...] = ((b0[0].astype(F32) + b1[0].astype(F32)) + b2[0].astype(F32)) + b3[0].astype(F32)

    return pl.pallas_call(
        body, name=name, out_shape=jax.ShapeDtypeStruct((2 * hr, n), F32),
        grid_spec=pltpu.PrefetchScalarGridSpec(
            num_scalar_prefetch=1, grid=(nb,),
            in_specs=[pl.BlockSpec((1, tr, n), functools.partial(lambda i, c_ref, k: (k, i, 0), k=k))
                      for k in range(n_sh)],
            out_specs=pl.BlockSpec((tr, n), lambda i, c_ref: (c_ref[0] * nb + i, 0))),
        compiler_params=_cp("parallel"),
    )(c_idx, b, b, b, b)


SEM = pl.BlockSpec(memory_space=pltpu.SEMAPHORE)
ANY = pl.BlockSpec(memory_space=pl.ANY)
DATAFLOW = pltpu.SideEffectType.DATAFLOW_SIDE_EFFECTING


def _remote(src, dst, send_sems, recv_sems, k, to):
    return pltpu.make_async_remote_copy(src_ref=src, dst_ref=dst, send_sem=send_sems.at[k], recv_sem=recv_sems.at[k],
                                        device_id=to, device_id_type=MESH)


def _split_copy(name, src, land_shape, land_dtype, n, plan, after=None):
    after = jnp.zeros((8, LANES), F32) if after is None else after

    def start_body(src_ref, land_ref, after_ref, send_sems, recv_sems, src_thru, land_thru, token):
        for cp in plan(src_ref, land_ref, send_sems, recv_sems)[0]:
            cp.start()
        token[...] = jnp.zeros_like(token)

    sems = pltpu.SemaphoreType.DMA((n,))
    send_sems, recv_sems, src_thru, land_thru, token = pl.pallas_call(
        start_body, name=name + "_start",
        out_shape=(sems, sems, pltpu.HBM(src.shape, src.dtype), pltpu.HBM(land_shape, land_dtype),
                   jax.ShapeDtypeStruct((8, LANES), F32)),
        in_specs=(HBM, HBM, ANY), out_specs=(SEM, SEM, HBM, HBM, pl.BlockSpec(memory_space=pltpu.VMEM)),
        input_output_aliases={0: 2, 1: 3}, compiler_params=pltpu.CompilerParams(has_side_effects=DATAFLOW),
    )(pltpu.with_memory_space_constraint(src, pltpu.HBM),
      pltpu.with_memory_space_constraint(lax.empty(land_shape, land_dtype), pltpu.HBM), after)

    def wait(*after):
        def wait_body(src_ref, land_ref, send_sems, recv_sems, *rest):
            sent, received = plan(src_ref, land_ref, send_sems, recv_sems)
            for cp in sent:
                cp.wait_send()
            for cp in received:
                cp.wait_recv()

        return pl.pallas_call(
            wait_body, name=name + "_wait",
            out_shape=(pltpu.HBM(src.shape, src.dtype), pltpu.HBM(land_shape, land_dtype)),
            in_specs=(HBM, HBM, SEM, SEM) + (ANY,) * len(after), out_specs=(HBM, HBM),
            input_output_aliases={0: 0, 1: 1}, compiler_params=pltpu.CompilerParams(has_side_effects=DATAFLOW),
        )(src_thru, land_thru, send_sems, recv_sems, *after)

    return token, wait


def _split_gather(name, shards, after):
    k, n, plan = len(shards), 3 * len(shards), _plan_gather

    def start_body(*refs):
        for cp in plan(refs[:k], refs[k:2 * k], refs[2 * k + 1], refs[2 * k + 2])[0]:
            cp.start()
        refs[-1][...] = jnp.zeros_like(refs[-1])

    sems = pltpu.SemaphoreType.DMA((n,))
    bufs = [pltpu.HBM(s.shape, s.dtype) for s in shards] + [pltpu.HBM((N_CHIP,) + s.shape, s.dtype) for s in shards]
    hbm = lambda t: pltpu.with_memory_space_constraint(t, pltpu.HBM)
    outs = pl.pallas_call(
        start_body, name=name + "_start", out_shape=(sems, sems, *bufs, jax.ShapeDtypeStruct((8, LANES), F32)),
        in_specs=(HBM,) * (2 * k) + (ANY,),
        out_specs=(SEM, SEM) + (HBM,) * (2 * k) + (pl.BlockSpec(memory_space=pltpu.VMEM),),
        input_output_aliases={i: 2 + i for i in range(2 * k)},
        compiler_params=pltpu.CompilerParams(has_side_effects=DATAFLOW),
    )(*[hbm(s) for s in shards], *[hbm(lax.empty((N_CHIP,) + s.shape, s.dtype)) for s in shards], after)
    send_sems, recv_sems, thru, token = outs[0], outs[1], outs[2:2 + 2 * k], outs[-1]

    def wait(*after):
        def wait_body(*refs):
            sent, received = plan(refs[:k], refs[k:2 * k], refs[2 * k], refs[2 * k + 1])
            for cp in sent:
                cp.wait_send()
            for cp in received:
                cp.wait_recv()

        res = pl.pallas_call(
            wait_body, name=name + "_wait", out_shape=tuple(bufs),
            in_specs=(HBM,) * (2 * k) + (SEM, SEM) + (ANY,) * len(after), out_specs=(HBM,) * (2 * k),
            input_output_aliases={i: i for i in range(2 * k)},
            compiler_params=pltpu.CompilerParams(has_side_effects=DATAFLOW),
        )(*thru, send_sems, recv_sems, *after)
        return res[:k], res[k:]

    return token, wait


def _behind(x, token):
    return x + token[0, 0]


def _plan_gather(src_refs, land_refs, send_sems, recv_sems):
    x, y, c, chips = _place()
    pairs = list(enumerate(zip(src_refs, land_refs)))
    sent = [_remote(s, l.at[2 * x + y], send_sems, recv_sems, 3 * i + j, (px, py, c))
            for i, (s, l) in pairs for j, (px, py) in enumerate(chips)]
    received = [_remote(s, l.at[2 * px + py], send_sems, recv_sems, 3 * i + j, (px, py, c))
                for i, (s, l) in pairs for j, (px, py) in enumerate(chips)]
    return sent, received


def _plan_swap(src_ref, land_ref, send_sems, recv_sems):
    x, y, c, _ = _place()
    _, other_half = _halves(c, src_ref.shape[1], 8)
    cp = _remote(src_ref.at[pl.ds(0, src_ref.shape[0]), other_half], land_ref, send_sems, recv_sems, 0, (x, y, 1 - c))
    return [cp], [cp]


def _plan_scatter(src_ref, land_ref, send_sems, recv_sems):
    x, y, c, chips = _place()
    sent = [_remote(src_ref.at[2 * px + py], land_ref.at[2 * x + y], send_sems, recv_sems, j, (px, py, c))
            for j, (px, py) in enumerate(chips)]
    received = [_remote(src_ref.at[2 * px + py], land_ref.at[2 * px + py], send_sems, recv_sems, j, (px, py, c))
                for j, (px, py) in enumerate(chips)]
    return sent, received


def _plan_share(src_ref, land_ref, send_sems, recv_sems):
    x, y, c, _ = _place()
    mine_half, other_half = _halves(c, src_ref.shape[0], 8)
    return ([_remote(src_ref.at[mine_half], src_ref.at[mine_half], send_sems, recv_sems, 0, (x, y, 1 - c))],
            [_remote(src_ref.at[other_half], src_ref.at[other_half], send_sems, recv_sems, 0, (x, y, 1 - c))])


class _GatherBehind:
    def __init__(self, name, shards, chip, after):
        self.chip = chip
        self.token, self.wait = _split_gather(name, shards, after)

    def result(self, *after):
        shards, lands = self.wait(*after)
        return [lax.dynamic_update_slice(land, shard[None], (self.chip, 0, 0)) for shard, land in zip(shards, lands)]


class _ReduceBehind:
    def __init__(self, name, chip, c_idx):
        self.name, self.chip, self.c_idx = name, chip, c_idx

    def start_slab(self, g):
        n_sh, rows, n = g.shape
        token, self.wait = _split_copy(self.name + "_swap", g, (n_sh, rows // 2, n), g.dtype, 1, _plan_swap)
        return token

    def pair(self, after):
        g, a = self.wait(after)
        h = _add_half(self.name + "_pair", g, a, self.c_idx)
        token, self.wait = _split_copy(self.name + "_scatter", h, h.shape, h.dtype, 3, _plan_scatter)
        return token

    def total(self, after):
        h, b = self.wait(after)
        b = lax.dynamic_update_slice(b, lax.dynamic_slice_in_dim(h, self.chip, 1, axis=0), (self.chip, 0, 0))
        f = _sum_chips(self.name + "_sum", b, self.c_idx)
        token, self.wait = _split_copy(self.name + "_share", f, (8, LANES), f.dtype, 1, _plan_share)
        return token

    def result(self, after):
        return self.wait(after)[0]


class _ReduceColsBehind(_ReduceBehind):
    def start(self, g_padded):
        g = _unpack_w_in_grad(g_padded)
        n = g.shape[1] // N_CHIP
        return self.start_slab(jnp.stack([g[:, k * n:(k + 1) * n] for k in range(N_CHIP)]))


def _f_adamw(w, g, m, v):
    m = ADAM_B1 * m + (1.0 - ADAM_B1) * g
    v = ADAM_B2 * v + (1.0 - ADAM_B2) * (g * g)
    m_hat = m / (1.0 - ADAM_B1 ** ADAM_STEP)
    v_hat = v / (1.0 - ADAM_B2 ** ADAM_STEP)
    return -ADAM_LR * (m_hat / (jnp.sqrt(v_hat) + ADAM_EPS) + ADAM_WD * w), m, v


def _adamw(name, w, g, m, v):
    rows, n = w.shape
    return _rowwise(name, lambda w, g, m, v: (_f_adamw(w, g, m, v), ()), rows, [(t, n, 0) for t in (w, g, m, v)], [],
                    [(n, F32)] * 3, [], tm=_row_tile(rows, 8, 256))


def _adamw_many(name, ws, gs, ms, vs):
    k = len(ws)

    def body(*refs):
        ins, outs = refs[:4 * k], refs[4 * k:]
        for i in range(k):
            res = _f_adamw(ins[i][...], ins[k + i][...], ins[2 * k + i][...], ins[3 * k + i][...])
            for j in range(3):
                outs[j * k + i][...] = res[j]

    out = pl.pallas_call(body, name=name, out_shape=[jax.ShapeDtypeStruct(w.shape, F32) for w in ws] * 3)(
        *ws, *gs, *ms, *vs)
    return out[:k], out[k:2 * k], out[2 * k:]


def _pack_rows(parts):
    rows = []
    for t in parts:
        t = t.reshape(-1)
        rows.append(jnp.pad(t, (0, -t.shape[0] % LANES)).reshape(-1, LANES))
    out = jnp.concatenate(rows, axis=0)
    return jnp.pad(out, ((0, -out.shape[0] % 8), (0, 0)))


def _unpack_rows(packed, shapes):
    out, r = [], 0
    for shp in shapes:
        n = int(np.prod(shp))
        nr = -(-n // LANES)
        out.append(packed[r:r + nr].reshape(-1)[:n].reshape(shp))
        r += nr
    return out


def _sum_blocks(name, g):
    def body(g_ref, o_ref):
        acc = g_ref[0]
        for k in range(1, g.shape[0]):
            acc = acc + g_ref[k]
        o_ref[...] = acc

    return pl.pallas_call(body, name=name, out_shape=jax.ShapeDtypeStruct(g.shape[1:], F32))(g)


def _silu(t):
    return t * _sigmoid(t)


def _ada_fwd(cc, w_ada):
    n = w_ada.shape[1]
    tn = _row_tile(n, LANES, 512)

    def body(cc_ref, w_ref, o_ref):
        o_ref[...] = _nn(_silu(cc_ref[...]), w_ref[...])

    return pl.pallas_call(
        body, name="ada_fwd", grid=(n // tn,), out_shape=jax.ShapeDtypeStruct((cc.shape[0], n), F32),
        in_specs=[pl.BlockSpec(cc.shape, lambda j: (0, 0)), pl.BlockSpec((w_ada.shape[0], tn), lambda j: (0, j))],
        out_specs=pl.BlockSpec((cc.shape[0], tn), lambda j: (0, j)), compiler_params=_cp("parallel"),
    )(cc, w_ada)


def _ada_bwd(cc, dm, w_ada):
    d, n = w_ada.shape
    tn = _row_tile(n, LANES, 512)

    def body(cc_ref, dm_ref, w_ref, gw_ref, ds_ref):
        @pl.when(pl.program_id(0) == 0)
        def _():
            ds_ref[...] = jnp.zeros_like(ds_ref)

        gw_ref[...] = _raw_dot("tn", _silu(cc_ref[...]), dm_ref[...], True)
        ds_ref[...] += _raw_dot("nt", dm_ref[...], w_ref[...], False)

    return pl.pallas_call(
        body, name="ada_bwd", grid=(n // tn,),
        out_shape=[jax.ShapeDtypeStruct((d, n), F32), jax.ShapeDtypeStruct(cc.shape, F32)],
        in_specs=[pl.BlockSpec(cc.shape, lambda j: (0, 0)), pl.BlockSpec((cc.shape[0], tn), lambda j: (0, j)),
                  pl.BlockSpec((d, tn), lambda j: (0, j))],
        out_specs=[pl.BlockSpec((d, tn), lambda j: (0, j)), pl.BlockSpec(cc.shape, lambda j: (0, 0))],
        compiler_params=_cp("arbitrary"),
    )(cc, dm, w_ada)


def _c_ctx_grad(parts, c_ctx):
    def body(p_ref, c_ref, o_ref):
        ds = ((p_ref[0] + p_ref[1]) + p_ref[2]) + p_ref[3]
        _, vjp = jax.vjp(_silu, c_ref[...])
        o_ref[...] = vjp(ds)[0]

    return pl.pallas_call(body, name="c_ctx_grad", out_shape=jax.ShapeDtypeStruct(c_ctx.shape, F32))(parts, c_ctx)


def kernel(x, c, ctx, c_ctx, w_ada, b_ada, g_pre_mix, g_post_mix, g_pre_ffn, g_post_ffn, w_in, attn_sink, w_gate_fwd, b_gate_fwd, w_gate_bwd, b_gate_bwd, g_gla_norm, w_out, w_ffn_in, w_ffn_out, loss_target, m_c_ctx, m_w_ada, m_b_ada, m_g_pre_mix, m_g_post_mix, m_g_pre_ffn, m_g_post_ffn, m_w_in, m_attn_sink, m_w_gate_fwd, m_b_gate_fwd, m_w_gate_bwd, m_b_gate_bwd, m_g_gla_norm, m_w_out, m_w_ffn_in, m_w_ffn_out, v_c_ctx, v_w_ada, v_b_ada, v_g_pre_mix, v_g_post_mix, v_g_pre_ffn, v_g_post_ffn, v_w_in, v_attn_sink, v_w_gate_fwd, v_b_gate_fwd, v_w_gate_bwd, v_b_gate_bwd, v_g_gla_norm, v_w_out, v_w_ffn_in, v_w_ffn_out):
    xi, yi, ci = lax.axis_index("x"), lax.axis_index("y"), lax.axis_index("c")
    dev, chip = 4 * xi + 2 * yi + ci, 2 * xi + yi
    c_idx = jnp.reshape(ci, (1,)).astype(jnp.int32)
    d = x.shape[-1]
    n_ada, n_in, n_f = w_ada.shape[-1], w_in.shape[-1], w_ffn_in.shape[-1]
    r_out, r_f = w_out.shape[1], w_ffn_out.shape[1]
    n_gate = w_gate_fwd.shape[-1]
    by_chip = lambda t: t[0::2]

    rc = -(-d // LANES)
    g1 = _ag_small("gather_cond", _pack_rows([c[0], w_gate_fwd[0], w_gate_bwd[0]]))
    c_all = g1[:, :rc].reshape(N_DEV, -1)[:, :d]
    gr = GATE_RANK * n_gate // LANES
    gate_full = lambda off: jnp.transpose(by_chip(g1)[:, off:off + gr].reshape(N_CHIP, GATE_RANK, n_gate),
                                          (1, 0, 2)).reshape(GATE_RANK, N_CHIP * n_gate)
    wgf, wgb = gate_full(rc), gate_full(rc + gr)
    cc = jnp.concatenate([c_all, c_ctx[None, :], jnp.zeros((7, d), F32)], axis=0)

    g2 = _ag_small("gather_ada", _ada_fwd(cc, w_ada[0]).reshape(-1, LANES))
    ada_all = jnp.transpose(by_chip(g2).reshape(N_CHIP, 16, n_ada), (1, 0, 2)).reshape(16, N_CHIP * n_ada) + b_ada
    first = _GatherBehind("gather_w_in", [w_in[0].astype(BF16)], chip, g2)
    late_shards = [w_out[0].astype(BF16), jnp.transpose(w_ffn_in[0]).astype(BF16), w_ffn_out[0].astype(BF16)]
    late = []

    def first_weights(*after):
        w_in_g, = first.result(*after, *late_shards)
        late.append(_GatherBehind("gather_late", late_shards, chip, w_in_g))
        return _pack_w_in(jnp.concatenate([w_in_g[k] for k in range(N_CHIP)], axis=1)), late[0].token

    def late_weights(after):
        return [t.reshape(-1, d) for t in late[0].result(after)]

    ada_all = _behind(ada_all, first.token)
    ada = lax.dynamic_slice(ada_all, (dev, 0), (1, N_CHIP * n_ada))
    ada_c = ada_all[N_DEV:N_DEV + 1]

    w = _prep_gate_weights(wgf, wgb)
    w.update(w_in=first_weights, g_pre_mix=g_pre_mix, g_post_mix=g_post_mix, g_pre_ffn=g_pre_ffn, g_post_ffn=g_post_ffn,
             attn_sink=attn_sink, b_gate_fwd=b_gate_fwd, b_gate_bwd=b_gate_bwd, g_gla_norm=g_gla_norm)

    reduce_behind = _ReduceBehind("reduce_late", chip, c_idx)
    reduce_w_in = _ReduceColsBehind("reduce_w_in", chip, c_idx)
    loss_lanes, grad_x, g, d_ada, d_ada_c = _local_step(x[0], ctx[0], loss_target[0], ada, ada_c, w, late_weights,
                                                        reduce_behind, reduce_w_in)

    small = ("g_pre_mix", "g_post_mix", "g_pre_ffn", "g_post_ffn", "attn_sink", "b_gate_fwd", "b_gate_bwd",
             "g_gla_norm", "w_gate_fwd", "w_gate_bwd")
    shapes = [(1, 6 * d)] * 2 + [g[n].shape for n in small] + [(1, LANES)]
    g3 = _ag_small("gather_small_grads", _pack_rows([d_ada, d_ada_c] + [g[n] for n in small] + [loss_lanes]))
    tot = dict(zip(("d_ada", "d_ada_c") + small + ("loss",),
                   _unpack_rows(_sum_blocks("sum_small_grads", g3), shapes)))
    r_ada = 6 * d // LANES
    dm = jnp.concatenate([g3[:, :r_ada].reshape(N_DEV, 6 * d), tot["d_ada_c"], jnp.zeros((7, 6 * d), F32)], axis=0)
    grads = {n: tot[n] for n in small[:8]}
    grads["b_ada"] = _sum_blocks("sum_b_ada", dm.reshape(16, r_ada, LANES)).reshape(1, 6 * d)
    grads["w_gate_fwd"] = lax.dynamic_slice(tot["w_gate_fwd"], (0, chip * n_gate), (GATE_RANK, n_gate))[None]
    grads["w_gate_bwd"] = lax.dynamic_slice(tot["w_gate_bwd"], (0, chip * n_gate), (GATE_RANK, n_gate))[None]
    gw_ada, dsc = _ada_bwd(cc, lax.dynamic_slice(dm, (0, chip * n_ada), (16, n_ada)), w_ada[0])
    grads["w_ada"] = gw_ada[None]
    g4 = _ag_small("gather_c_ctx", _pack_rows([dsc[N_DEV]]))
    grads["c_ctx"] = _c_ctx_grad(by_chip(g4), _pack_rows([c_ctx])).reshape(-1)[:d]

    grads["w_in"] = reduce_w_in.result(g4)[None]
    part = lambda n: g["late"][g["late_at"][n]:g["late_at"][n] + g["late_rows"][n]]
    grads["w_ffn_in"], grads["w_ffn_out"], grads["w_out"] = (jnp.transpose(part("w_ffn_in_t"))[None],
                                                            part("w_ffn_out")[None], part("w_out")[None])

    names = ("c_ctx", "w_ada", "b_ada", "g_pre_mix", "g_post_mix", "g_pre_ffn", "g_post_ffn", "w_in", "attn_sink",
             "w_gate_fwd", "b_gate_fwd", "w_gate_bwd", "b_gate_bwd", "g_gla_norm", "w_out", "w_ffn_in", "w_ffn_out")
    weights = dict(zip(names, (c_ctx, w_ada, b_ada, g_pre_mix, g_post_mix, g_pre_ffn, g_post_ffn, w_in, attn_sink,
                               w_gate_fwd, b_gate_fwd, w_gate_bwd, b_gate_bwd, g_gla_norm, w_out, w_ffn_in,
                               w_ffn_out)))
    m_in = dict(zip(names, (m_c_ctx, m_w_ada, m_b_ada, m_g_pre_mix, m_g_post_mix, m_g_pre_ffn, m_g_post_ffn, m_w_in,
                            m_attn_sink, m_w_gate_fwd, m_b_gate_fwd, m_w_gate_bwd, m_b_gate_bwd, m_g_gla_norm,
                            m_w_out, m_w_ffn_in, m_w_ffn_out)))
    v_in = dict(zip(names, (v_c_ctx, v_w_ada, v_b_ada, v_g_pre_mix, v_g_post_mix, v_g_pre_ffn, v_g_post_ffn, v_w_in,
                            v_attn_sink, v_w_gate_fwd, v_b_gate_fwd, v_w_gate_bwd, v_b_gate_bwd, v_g_gla_norm,
                            v_w_out, v_w_ffn_in, v_w_ffn_out)))
    large = ("w_ada", "w_in", "w_out", "w_ffn_in", "w_ffn_out")
    tiny = tuple(n for n in names if n not in large)
    delta, new_m, new_v = {}, {}, {}
    for n in large:
        dl, nm, nv = _adamw("adamw_" + n, weights[n][0], grads[n][0], m_in[n][0], v_in[n][0])
        delta[n], new_m[n], new_v[n] = dl[None], nm[None], nv[None]
    for n in tiny:
        grads[n] = grads[n].reshape(weights[n].shape)
    as_rows = lambda t: t.reshape(-1, t.shape[-1])
    res = _adamw_many("adamw_small", *[[as_rows(t[n]) for n in tiny] for t in (weights, grads, m_in, v_in)])
    for out, vals in zip((delta, new_m, new_v), res):
        out.update({n: val.reshape(weights[n].shape) for n, val in zip(tiny, vals)})

    return (tot["loss"][0, 0], grad_x[None], *[grads[n] for n in names], *[delta[n] for n in names], *[new_m[n] for n in names],
            *[new_v[n] for n in names])
```

```python
import functools

import jax
import jax.numpy as jnp
import numpy as np
from jax import lax
from jax.experimental import pallas as pl
from jax.experimental.pallas import tpu as pltpu

F32 = jnp.float32
BF16 = jnp.bfloat16
MESH = pl.DeviceIdType.MESH

HEAD_DIM = 64
ATT_HEADS = 8
ATT_KV_HEADS = 2
ATT_GROUP = ATT_HEADS // ATT_KV_HEADS
WINDOW = 128
BLOCK = 128
GRID_W = 64
ROPE_BASE = 10000.0
GLA_HEADS = 8
GLA_DK = 32
GLA_DV = 64
GLA_CHUNK = 64
GATE_RANK = 16
GATE_TAU = 16.0
NEG_INF = -1e30
QW = ATT_HEADS * HEAD_DIM
KVW = ATT_KV_HEADS * HEAD_DIM
GKW = GLA_HEADS * GLA_DK
GVW = GLA_HEADS * GLA_DV
IN_COLS = QW + 2 * KVW + 2 * GKW + 2 * GVW + 2 * GATE_RANK
LANES = 128
IN_PAD = IN_COLS + LANES - 2 * GATE_RANK
C_Q, C_GV, C_GG = 0, QW, QW + GVW
C_K = C_GG + GVW
C_V = C_K + KVW
C_GQ = C_V + KVW
C_GK = C_GQ + GKW
C_Z = C_GK + GKW
MIX = QW + GVW

ADAM_LR, ADAM_B1, ADAM_B2, ADAM_EPS, ADAM_WD, ADAM_STEP = 0.001, 0.9, 0.999, 1e-08, 0.01, 10

VMEM_LIMIT = 56 * 1024 * 1024


def _cp(*sem):
    return pltpu.CompilerParams(dimension_semantics=sem, vmem_limit_bytes=VMEM_LIMIT)


def _pick(n, cands):
    for t in cands:
        if n % t == 0:
            return t
    return n


_DIMS = {"nn": (((1,), (0,)), ((), ())), "nt": (((1,), (1,)), ((), ())), "tn": (((0,), (0,)), ((), ()))}


def _raw_dot(mode, a, b, hi):
    dot = lambda u, v: lax.dot_general(u, v, _DIMS[mode], preferred_element_type=F32)
    if not hi:
        return dot(a.astype(BF16), b.astype(BF16))
    a, b = a.astype(F32), b.astype(F32)
    a_hi, b_hi = a.astype(BF16), b.astype(BF16)
    out = dot(a_hi, b_hi)
    if hi != "a":
        out = out + dot((a - a_hi.astype(F32)).astype(BF16), b_hi)
    if hi != "b":
        out = out + dot(a_hi, (b - b_hi.astype(F32)).astype(BF16))
    return out


def _make_dot(mode, hi):
    @jax.custom_vjp
    def dot(a, b):
        return _raw_dot(mode, a, b, hi)

    def fwd(a, b):
        return _raw_dot(mode, a, b, hi), (a, b)

    def bwd(res, dc):
        a, b = res
        if mode == "nn":
            return (_raw_dot("nt", dc, b, "b" if hi == "b" else bool(hi)),
                    _raw_dot("tn", a, dc, "a" if hi == "a" else bool(hi)))
        if mode == "nt":
            return _raw_dot("nn", dc, b, bool(hi)), _raw_dot("tn", dc, a, bool(hi))
        return _raw_dot("nt", b, dc, bool(hi)), _raw_dot("nn", a, dc, bool(hi))

    dot.defvjp(fwd, bwd)
    return dot


_nn, _nt, _tn = _make_dot("nn", False), _make_dot("nt", False), _make_dot("tn", False)
_nn_mask, _nn_by_exact = _make_dot("nn", "a"), _make_dot("nn", "b")


MM_VMEM_BUDGET = 44 * 1024 * 1024


def _halvings(n):
    out = [n]
    while out[-1] % (2 * LANES) == 0:
        out.append(out[-1] // 2)
    return out


def _mm_tiles(mode, m, n, k, a_bytes, b_bytes, o_bytes, init_bytes=0):
    tms = [t for t in dict.fromkeys((m, m // 2, m // 4, 2048, 1024, 512, 256, 128))
           if m % t == 0 and t % (LANES if mode == "tn" else 16) == 0 and t <= 4096] or [m]
    if mode == "tn":
        fits = [(k // tk + 0.5 * (m // tm), tm, tk)
                for tk in (4096, 2048, 1024, 512, 256, 128) if k % tk == 0 for tm in tms
                if 2 * (tk * tm * a_bytes + tk * n * b_bytes + tm * n * (o_bytes + init_bytes)) <= MM_VMEM_BUDGET]
        if fits:
            _, tm, tk = min(fits)
            return tm, n, tk
    tks = ([t for t in (512, 256, 128) if k % t == 0] or [k]) if mode == "tn" else _halvings(k)
    for tn in _halvings(n):
        for tk in tks:
            for tm in tms:
                acc = tm * tn * 4 if (k // tk > 1 and o_bytes != 4) else 0
                tiles = tm * tk * a_bytes + tk * tn * b_bytes + tm * tn * (o_bytes + init_bytes)
                if 2 * tiles + acc <= MM_VMEM_BUDGET:
                    return tm, tn, tk
    return tms[-1], _halvings(n)[-1], tks[-1]


def _mm(name, a, b, mode, out_dtype=F32, init=None, after=None):
    follow = () if after is None else (after,)
    if mode == "nn":
        (m, k), n = a.shape, b.shape[1]
    elif mode == "nt":
        (m, k), n = a.shape, b.shape[0]
    else:
        (k, m), n = a.shape, b.shape[1]
    tm, tn, tk = _mm_tiles(mode, m, n, k, a.dtype.itemsize, b.dtype.itemsize, jnp.dtype(out_dtype).itemsize,
                           0 if init is None else 4)
    nk = k // tk
    use_acc = nk > 1 and out_dtype != F32

    inits = () if init is None else (init,)

    def body(a_ref, b_ref, *rest):
        rest = rest[:len(inits)] + rest[len(inits) + len(follow):]
        o_ref, acc = rest[len(inits)], rest[len(inits) + 1:]
        part = _raw_dot(mode, a_ref[...], b_ref[...], False)
        first = lambda: part + rest[0][...] if inits else part
        if nk == 1:
            o_ref[...] = first().astype(o_ref.dtype)
            return
        acc_ref = acc[0] if use_acc else o_ref
        kk = pl.program_id(2)

        @pl.when(kk == 0)
        def _():
            acc_ref[...] = first()

        @pl.when(kk > 0)
        def _():
            acc_ref[...] += part

        if use_acc:
            @pl.when(kk == nk - 1)
            def _():
                o_ref[...] = acc_ref[...].astype(o_ref.dtype)

    if mode == "nn":
        a_spec = pl.BlockSpec((tm, tk), lambda i, j, kk: (i, kk))
        b_spec = pl.BlockSpec((tk, tn), lambda i, j, kk: (kk, j))
    elif mode == "nt":
        a_spec = pl.BlockSpec((tm, tk), lambda i, j, kk: (i, kk))
        b_spec = pl.BlockSpec((tn, tk), lambda i, j, kk: (j, kk))
    else:
        a_spec = pl.BlockSpec((tk, tm), lambda i, j, kk: (kk, i))
        b_spec = pl.BlockSpec((tk, tn), lambda i, j, kk: (kk, j))
    return pl.pallas_call(
        body, name=name, grid=(m // tm, n // tn, nk),
        in_specs=[a_spec, b_spec] + [pl.BlockSpec((tm, tn), lambda i, j, kk: (i, j))] * len(inits)
        + [pl.BlockSpec(memory_space=pl.ANY)] * len(follow),
        out_specs=pl.BlockSpec((tm, tn), lambda i, j, kk: (i, j)),
        out_shape=jax.ShapeDtypeStruct((m, n), out_dtype),
        scratch_shapes=[pltpu.VMEM((tm, tn), F32)] if use_acc else [],
        compiler_params=_cp("parallel", "parallel", "arbitrary"),
    )(a, b, *inits, *follow)


def _slab_layout(rows):
    offsets, at = [], 0
    for r in rows:
        at = -(-at // r) * r
        offsets.append(at)
        at += r
    return offsets, -(-at // 32) * 32


def _slab_zero_gaps(name, shape, rows, offsets):
    gaps = [(o + r, nxt) for o, r, nxt in zip(offsets, rows, offsets[1:] + [shape[1]]) if nxt > o + r]
    slab = None
    for i, (lo, hi) in enumerate(gaps):
        step = int(np.gcd(lo, hi - lo))

        def body(*refs):
            refs[-1][...] = jnp.zeros_like(refs[-1])

        slab = pl.pallas_call(
            body, name=f"{name}_{i}", grid=(shape[0], (hi - lo) // step), out_shape=jax.ShapeDtypeStruct(shape, F32),
            in_specs=[] if slab is None else [pl.BlockSpec(memory_space=pl.ANY)],
            out_specs=pl.BlockSpec((1, step, shape[2]), functools.partial(lambda k, j, b: (k, b + j, 0), b=lo // step)),
            input_output_aliases={} if slab is None else {0: 0}, compiler_params=_cp("parallel", "parallel"),
        )(*(() if slab is None else (slab,)))
    return slab


def _dw_into_slab(name, a, b, slab, shape, at):
    (k, m), n = a.shape, b.shape[1]
    r = m // N_CHIP
    fits = [(k // tk + 0.5 * (m // tm), tm, tk)
            for tk in (4096, 2048, 1024, 512, 256, 128) if k % tk == 0 for tm in (m, m // 2, r) if tm % LANES == 0
            if 2 * (tk * tm * a.dtype.itemsize + tk * n * b.dtype.itemsize + tm * n * 4) <= MM_VMEM_BUDGET]
    _, tm, tk = min(fits)
    per, nk = tm // r, k // tk

    def body(a_ref, b_ref, *rest):
        o_ref = rest[-1]
        part = _raw_dot("tn", a_ref[...], b_ref[...], False).reshape(o_ref.shape)
        if nk == 1:
            o_ref[...] = part
            return
        kk = pl.program_id(1)

        @pl.when(kk == 0)
        def _():
            o_ref[...] = part

        @pl.when(kk > 0)
        def _():
            o_ref[...] += part

    prev = () if slab is None else (slab,)
    return pl.pallas_call(
        body, name=name, grid=(m // tm, nk), out_shape=jax.ShapeDtypeStruct(shape, F32),
        in_specs=[pl.BlockSpec((tk, tm), lambda i, kk: (kk, i)), pl.BlockSpec((tk, n), lambda i, kk: (kk, 0))]
        + [pl.BlockSpec(memory_space=pl.ANY)] * len(prev),
        out_specs=pl.BlockSpec((per, r, n), lambda i, kk: (i, at // r, 0)),
        input_output_aliases={2: 0} if prev else {}, compiler_params=_cp("parallel", "arbitrary"),
    )(a, b, *prev)


def _rowwise(name, fn, rows, row_ins, full_ins, row_outs, acc_outs, tm=None):
    tm = tm or _pick(rows, (512, 256, 128))
    n_r, n_f, n_o, n_a = len(row_ins), len(full_ins), len(row_outs), len(acc_outs)

    def body(*refs):
        ins, outs = refs[:n_r + n_f], refs[n_r + n_f:]
        vals = [r[...].astype(F32) for r in ins]
        ro, ao = fn(*vals)
        for r, val in zip(outs[:n_o], ro):
            r[...] = val.astype(r.dtype)
        if n_a:
            @pl.when(pl.program_id(0) == 0)
            def _():
                for r in outs[n_o:]:
                    r[...] = jnp.zeros_like(r)

            for r, val in zip(outs[n_o:], ao):
                r[...] += val

    in_specs = [pl.BlockSpec((tm, w), functools.partial(lambda i, cb: (i, cb), cb=cb)) for _, w, cb in row_ins]
    in_specs += [pl.BlockSpec(a.shape, lambda i: (0, 0)) for a in full_ins]
    out_specs = [pl.BlockSpec((tm, w), lambda i: (i, 0)) for w, _ in row_outs]
    out_specs += [pl.BlockSpec(s, lambda i: (0, 0)) for s in acc_outs]
    out_shape = [jax.ShapeDtypeStruct((rows, w), dt) for w, dt in row_outs]
    out_shape += [jax.ShapeDtypeStruct(s, F32) for s in acc_outs]
    return pl.pallas_call(
        body, name=name, grid=(rows // tm,), in_specs=in_specs, out_specs=out_specs, out_shape=out_shape,
        compiler_params=_cp("arbitrary" if n_a else "parallel"),
    )(*[a for a, _, _ in row_ins], *full_ins)


def _rn(x):
    return x * lax.rsqrt(jnp.mean(x * x, axis=-1, keepdims=True) + 1e-6)


def _sigmoid(t):
    return 1.0 / (1.0 + jnp.exp(-t))


def _f_norm_mod(x, g, sh, sc):
    return _rn(x) * g * (1.0 + sc) + sh


def _f_post_res(xr, y, g, gate):
    return xr + gate * (_rn(y) * g)


@jax.custom_vjp
def _f_swiglu(g, u):
    return g * _sigmoid(g) * u


def _f_swiglu_fwd(g, u):
    s = _sigmoid(g)
    return g * s * u, (g, u, s)


def _f_swiglu_bwd(res, da):
    g, u, s = res
    gs = g * s
    return da * u * (s + gs * (1.0 - s)), da * gs


_f_swiglu.defvjp(_f_swiglu_fwd, _f_swiglu_bwd)


def _logsig(u):
    return jnp.minimum(u, 0.0) - jnp.log(1.0 + jnp.exp(-jnp.abs(u)))


def _f_gate(z, wf, wb, bf, bb):
    return _logsig(_nn(z, wf) + bf) / GATE_TAU, _logsig(_nn(z, wb) + bb) / GATE_TAU


def _f_gla_out(of, ob, gg, gt, bd):
    o = of + ob
    ms = _nn_by_exact(o * o, bd)
    return o * lax.rsqrt(ms + 1e-6) * gt * (gg * _sigmoid(gg))


def _norm_mod(name, x, g, sh, sc):
    rows, d = x.shape
    return _rowwise(name, lambda x, g, sh, sc: ((_f_norm_mod(x, g, sh, sc),), ()), rows,
                    [(x, d, 0)], [g, sh, sc], [(d, BF16)], [])[0]


def _rn_bwd(x, dn):
    r = lax.rsqrt(jnp.mean(x * x, axis=-1, keepdims=True) + 1e-6)
    n = x * r
    return r * (dn - n * jnp.mean(dn * n, axis=-1, keepdims=True)), n


def _norm_mod_grads(dh, x, g, sc):
    dx, n = _rn_bwd(x, dh * (g * (1.0 + sc)))
    t = jnp.sum(dh * n, axis=0, keepdims=True)
    return dx, (1.0 + sc) * t, jnp.sum(dh, axis=0, keepdims=True), g * t


def _post_res_grads(dout, y, g, gate):
    dy, n = _rn_bwd(y, dout * (gate * g))
    t = jnp.sum(dout * n, axis=0, keepdims=True)
    return dy, gate * t, g * t


def _norm_mod_bwd(name, dh, dres, x, g, sh, sc):
    rows, d = x.shape

    def fn(dh, dres, x, g, sh, sc):
        dx, dg, dsh, dsc = _norm_mod_grads(dh, x, g, sc)
        return (dx + dres,), (dg, dsh, dsc)

    return _rowwise(name, fn, rows, [(dh, d, 0), (dres, d, 0), (x, d, 0)], [g, sh, sc], [(d, F32)],
                    [(1, d)] * 3)


def _post_res_norm_mod(name, xr, y, g_post, gate, g_pre, sh, sc):
    rows, d = xr.shape

    def fn(xr, y, g_post, gate, g_pre, sh, sc):
        x1 = _f_post_res(xr, y, g_post, gate)
        return (x1, _f_norm_mod(x1, g_pre, sh, sc)), ()

    return _rowwise(name, fn, rows, [(xr, d, 0), (y, d, 0)], [g_post, gate, g_pre, sh, sc], [(d, F32), (d, BF16)], [])


def _norm_mod_post_res_bwd(name, dh, dres, x1, y, g_pre, sh, sc, g_post, gate):
    rows, d = x1.shape

    def fn(dh, dres, x1, y, g_pre, sh, sc, g_post, gate):
        dx1, dg_pre, dsh, dsc = _norm_mod_grads(dh, x1, g_pre, sc)
        dx1 = dx1 + dres
        dy, dg_post, dgate = _post_res_grads(dx1, y, g_post, gate)
        return (dx1, dy), (dg_pre, dsh, dsc, dg_post, dgate)

    return _rowwise(name, fn, rows, [(dh, d, 0), (dres, d, 0), (x1, d, 0), (y, d, 0)], [g_pre, sh, sc, g_post, gate],
                    [(d, F32), (d, BF16)], [(1, d)] * 5, tm=_pick(rows, (256, 128)))


def _post_res_loss(name, xr, y, g, gate, target):
    rows, d = xr.shape

    def fn(xr, y, target, g, gate):
        diff = _f_post_res(xr, y, g, gate) - target
        part = 0.5 * jnp.sum(jnp.mean(diff * diff, axis=-1, keepdims=True), axis=0, keepdims=True)
        dx2 = diff * (1.0 / d)
        dy, dg, dgate = _post_res_grads(dx2, y, g, gate)
        return (dx2, dy), (jnp.broadcast_to(part, (1, LANES)), dg, dgate)

    return _rowwise(name, fn, rows, [(xr, d, 0), (y, d, 0), (target, d, 0)], [g, gate], [(d, F32), (d, BF16)],
                    [(1, LANES), (1, d), (1, d)])


def _mm_rows(name, a, b, mode, fn, extras, outs):
    m, k = a.shape
    tm = _pick(m, (256, 128))

    def body(a_ref, b_ref, *rest):
        tiles = fn(_raw_dot(mode, a_ref[...], b_ref[...], False), *[e[...] for e in rest[:len(extras)]])
        for r, val in zip(rest[len(extras):], tiles):
            r[...] = val.astype(r.dtype)

    row = lambda w: pl.BlockSpec((tm, w), lambda i: (i, 0))
    return pl.pallas_call(
        body, name=name, grid=(m // tm,),
        in_specs=[row(k), pl.BlockSpec(b.shape, lambda i: (0, 0))] + [row(e.shape[1]) for e in extras],
        out_specs=[row(w) for w, _ in outs], out_shape=[jax.ShapeDtypeStruct((m, w), dt) for w, dt in outs],
        compiler_params=_cp("parallel"),
    )(a, b, *extras)


def _ffn_in_swiglu(name, h, w_t):
    f = w_t.shape[0] // 2
    fn = lambda u: (u, _f_swiglu(u[:, :f], u[:, f:]))
    return _mm_rows(name, h, w_t, "nt", fn, [], [(2 * f, BF16), (f, BF16)])


def _ffn_out_dx_swiglu_bwd(name, df, w_out, u):
    f = w_out.shape[0]

    def fn(da, u):
        u = u.astype(F32)
        _, vjp = jax.vjp(_f_swiglu, u[:, :f], u[:, f:])
        return (jnp.concatenate(vjp(da), axis=1),)

    return _mm_rows(name, df, w_out, "nt", fn, [u], [(2 * f, BF16)])[0]


def _gate_fwd(name, p, wf, wb, bf, bb):
    rows = p.shape[0]
    return _rowwise(name, lambda z, wf, wb, bf, bb: (_f_gate(z, wf, wb, bf, bb), ()), rows,
                    [(p, LANES, C_Z // LANES)], [wf, wb, bf, bb], [(GKW, F32)] * 2, [])


def _gate_bwd(name, p, dla_f, dla_b, wf, wb, bf, bb):
    rows = p.shape[0]

    def fn(z, dlf, dlb, wf, wb, bf, bb):
        _, vjp = jax.vjp(_f_gate, z, wf, wb, bf, bb)
        dz, dwf, dwb, dbf, dbb = vjp((dlf, dlb))
        return (dz,), (dwf, dwb, dbf, dbb)

    return _rowwise(name, fn, rows, [(p, LANES, C_Z // LANES), (dla_f, GKW, 0), (dla_b, GKW, 0)],
                    [wf, wb, bf, bb], [(LANES, BF16)], [(LANES, GKW), (LANES, GKW), (1, GKW), (1, GKW)])


def _head_mean_matrix():
    h = np.arange(GVW) // GLA_DV
    return jnp.asarray((h[:, None] == h[None, :]).astype(np.float32) / GLA_DV)


def _gla_out(name, attn, of, ob, p, gt):
    rows = of.shape[0]
    bd = _head_mean_matrix()
    fn = lambda attn, of, ob, gg, gt, bd: ((jnp.concatenate([attn, _f_gla_out(of, ob, gg, gt, bd)], axis=1),), ())
    return _rowwise(name, fn, rows, [(attn, QW, 0), (of, GVW, 0), (ob, GVW, 0), (p, GVW, C_GG // GVW)], [gt, bd],
                    [(MIX, BF16)], [])[0]


def _gla_out_bwd(name, dmix, of, ob, p, gt):
    rows = of.shape[0]
    bd = _head_mean_matrix()

    def fn(dm, of, ob, gg, gt, bd):
        _, vjp = jax.vjp(lambda of, gg, gt: _f_gla_out(of, ob, gg, gt, bd), of, gg, gt)
        do, dgg, dgt = vjp(dm)
        return (do, dgg), (dgt,)

    return _rowwise(name, fn, rows, [(dmix, GVW, 1), (of, GVW, 0), (ob, GVW, 0), (p, GVW, C_GG // GVW)], [gt, bd],
                    [(GVW, F32), (GVW, BF16)], [(1, GVW)])


def _rope_tables(n_tokens):
    t = jnp.arange(n_tokens)
    row = (t // GRID_W).astype(F32)
    col = (t % GRID_W).astype(F32)
    half = HEAD_DIM // 2
    inv_freq = ROPE_BASE ** (-jnp.arange(0, half, 2, dtype=F32) / half)
    ang_r = row[:, None] * inv_freq[None, :]
    ang_c = col[:, None] * inv_freq[None, :]
    ang = jnp.concatenate([ang_r, ang_r, ang_c, ang_c], axis=-1)
    sign = jnp.concatenate([-jnp.ones((16,), F32), jnp.ones((16,), F32)] * 2)
    cos, sin = jnp.cos(ang), jnp.sin(ang) * sign[None, :]
    return jnp.tile(cos, (1, 2)), jnp.tile(sin, (1, 2))


def _rot_pairs(x):
    w = x.shape[-1]
    lane = lax.broadcasted_iota(jnp.int32, x.shape, x.ndim - 1)
    return jnp.where((lane % 32) < 16, pltpu.roll(x, w - 16, x.ndim - 1), pltpu.roll(x, 16, x.ndim - 1))


def _rope_apply(x, cos, sin_signed, inverse):
    reps = x.shape[-1] // LANES
    cos = jnp.concatenate([cos] * reps, axis=-1) if reps > 1 else cos
    sin = jnp.concatenate([sin_signed] * reps, axis=-1) if reps > 1 else sin_signed
    if inverse:
        return x * cos + _rot_pairs(x * sin)
    return x * cos + _rot_pairs(x) * sin


def _rope_fwd(name, p, cos, sin):
    rows = p.shape[0]

    def fn(q, k, v, cos, sin):
        return (_rope_apply(q, cos, sin, False), _rope_apply(k, cos, sin, False), v), ()

    return _rowwise(name, fn, rows, [(p, QW, 0), (p, KVW, C_K // KVW), (p, KVW, C_V // KVW), (cos, LANES, 0),
                                     (sin, LANES, 0)], [], [(QW, BF16), (KVW, BF16), (KVW, BF16)], [])


def _proj_grad(name, dq_rot, dk_rot, dv, cos, sin, gla_f, gla_b, dgg, dz):
    rows = dq_rot.shape[0]

    def fn(dq, dk, dv, cos, sin, gqf, gkf, gvf, gqb, gkb, gvb, dgg, dz):
        parts = [_rope_apply(dq, cos, sin, True), gvf + gvb, dgg, _rope_apply(dk, cos, sin, True), dv, gqf + gqb,
                 gkf + gkb, dz]
        return (jnp.concatenate(parts, axis=1),), ()

    ins = [(dq_rot, QW), (dk_rot, KVW), (dv, KVW), (cos, LANES), (sin, LANES)]
    ins += [(t, t.shape[1]) for t in (*gla_f, *gla_b)] + [(dgg, GVW), (dz, LANES)]
    return _rowwise(name, fn, rows, [(t, w, 0) for t, w in ins], [], [(IN_PAD, BF16)], [],
                    tm=_pick(rows, (256, 128)))[0]


GROUP_ROWS = ATT_GROUP * BLOCK


ATT_SCALE = HEAD_DIM ** -0.5


def _attn_setup(sink, n, n_tokens):
    row = lax.broadcasted_iota(jnp.int32, (GROUP_ROWS, 1), 0)
    group = sum((row >= g * BLOCK).astype(jnp.int32) for g in range(1, ATT_GROUP))
    i = lax.broadcasted_iota(jnp.int32, (GROUP_ROWS, 3 * BLOCK), 0) - BLOCK * group
    j = lax.broadcasted_iota(jnp.int32, (GROUP_ROWS, 3 * BLOCK), 1)
    kpos = (n - 1) * BLOCK + j
    mask = (jnp.abs(j - BLOCK - i) <= WINDOW) & (kpos >= 0) & (kpos < n_tokens)
    head_id = lax.broadcasted_iota(jnp.int32, (1, ATT_HEADS), 1)
    sks = []
    for h in range(ATT_KV_HEADS):
        sk = jnp.zeros((GROUP_ROWS, 1), F32)
        for g in range(ATT_GROUP):
            one = jnp.sum(jnp.where(head_id == h * ATT_GROUP + g, sink, 0.0), axis=-1, keepdims=True)
            sk = jnp.where(group == g, one, sk)
        sks.append(sk)
    return mask, group, sks


def _attn_weights(q, kw, kc, sk, mask):
    q = q * ATT_SCALE
    s_w = jnp.where(mask, _raw_dot("nt", q, kw, False), NEG_INF)
    s_c = _raw_dot("nt", q, kc, False)
    m = jnp.maximum(jnp.maximum(jnp.max(s_w, axis=-1, keepdims=True), jnp.max(s_c, axis=-1, keepdims=True)), sk)
    pw, pc, ps = jnp.exp(s_w - m), jnp.exp(s_c - m), jnp.exp(sk - m)
    return q, pw, pc, ps, jnp.sum(pw, axis=-1, keepdims=True) + jnp.sum(pc, axis=-1, keepdims=True) + ps


def _f_attn(qs, kws, vws, kcs, vcs, sink, n, n_tokens):
    mask, _, sks = _attn_setup(sink, n, n_tokens)
    outs = []
    for h in range(ATT_KV_HEADS):
        _, pw, pc, _, den = _attn_weights(qs[h], kws[h], kcs[h], sks[h], mask)
        outs.append((_raw_dot("nn", pw, vws[h], False) + _raw_dot("nn", pc, vcs[h], False)) / den)
    return tuple(outs)


def _f_attn_bwd(qs, kws, vws, kcs, vcs, sink, outs, douts, n, n_tokens):
    mask, group, sks = _attn_setup(sink, n, n_tokens)
    head_id = lax.broadcasted_iota(jnp.int32, (1, ATT_HEADS), 1)
    dot = lambda mode, a, b: _raw_dot(mode, a, b, False)
    dqs, dkws, dvws, dkcs, dvcs, dsink = [], [], [], [], [], jnp.zeros((1, ATT_HEADS), F32)
    for h in range(ATT_KV_HEADS):
        q, pw, pc, ps, den = _attn_weights(qs[h], kws[h], kcs[h], sks[h], mask)
        inv = 1.0 / den
        pw, pc = pw * inv, pc * inv
        dd = jnp.sum(douts[h] * outs[h], axis=-1, keepdims=True)
        dsw = pw * (dot("nt", douts[h], vws[h]) - dd)
        dsc = pc * (dot("nt", douts[h], vcs[h]) - dd)
        dqs.append((dot("nn", dsw, kws[h]) + dot("nn", dsc, kcs[h])) * ATT_SCALE)
        dkws.append(dot("tn", dsw, q))
        dkcs.append(dot("tn", dsc, q))
        dvws.append(dot("tn", pw, douts[h]))
        dvcs.append(dot("tn", pc, douts[h]))
        dsk = -(ps * inv) * dd
        for g in range(ATT_GROUP):
            one = jnp.sum(jnp.where(group == g, dsk, 0.0), axis=0, keepdims=True)
            dsink = dsink + jnp.where(head_id == h * ATT_GROUP + g, one, 0.0)
    return dqs, dkws, dvws, dkcs, dvcs, dsink


def _group_rows(ref, h):
    hs = lambda hq: slice(hq * HEAD_DIM, (hq + 1) * HEAD_DIM)
    return jnp.concatenate([ref[:, hs(h * ATT_GROUP + g)].astype(F32) for g in range(ATT_GROUP)], axis=0)


def _ungroup_rows(ref, h, val):
    for g in range(ATT_GROUP):
        hq = h * ATT_GROUP + g
        ref[:, hq * HEAD_DIM:(hq + 1) * HEAD_DIM] = val[g * BLOCK:(g + 1) * BLOCK].astype(ref.dtype)


def _attn_loads(n, q_ref, kp_ref, vp_ref, kc_ref, vc_ref):
    r0 = pl.multiple_of(n * BLOCK, BLOCK)
    hs = lambda h: slice(h * HEAD_DIM, (h + 1) * HEAD_DIM)
    qs = [_group_rows(q_ref, h) for h in range(ATT_KV_HEADS)]
    kws = [kp_ref[pl.ds(r0, 3 * BLOCK), hs(h)].astype(F32) for h in range(ATT_KV_HEADS)]
    vws = [vp_ref[pl.ds(r0, 3 * BLOCK), hs(h)].astype(F32) for h in range(ATT_KV_HEADS)]
    kcs = [kc_ref[:, hs(h)].astype(F32) for h in range(ATT_KV_HEADS)]
    vcs = [vc_ref[:, hs(h)].astype(F32) for h in range(ATT_KV_HEADS)]
    return r0, hs, qs, kws, vws, kcs, vcs


def _attn_specs(s, c):
    full = lambda shape: pl.BlockSpec(shape, lambda n: (0, 0))
    return [pl.BlockSpec((BLOCK, QW), lambda n: (n, 0)), full((s + 2 * BLOCK, KVW)), full((s + 2 * BLOCK, KVW)),
            full((c, KVW)), full((c, KVW)), full((1, ATT_HEADS))]


def _attn_fwd(q, kp, vp, kc, vc, sink):
    s, c = q.shape[0], kc.shape[0]

    def body(q_ref, kp_ref, vp_ref, kc_ref, vc_ref, sink_ref, o_ref):
        n = pl.program_id(0)
        _, hs, qs, kws, vws, kcs, vcs = _attn_loads(n, q_ref, kp_ref, vp_ref, kc_ref, vc_ref)
        outs = _f_attn(qs, kws, vws, kcs, vcs, sink_ref[...], n, s)
        for h in range(ATT_KV_HEADS):
            _ungroup_rows(o_ref, h, outs[h])

    return pl.pallas_call(
        body, name="attn_fwd", grid=(s // BLOCK,), in_specs=_attn_specs(s, c),
        out_specs=pl.BlockSpec((BLOCK, QW), lambda n: (n, 0)), out_shape=jax.ShapeDtypeStruct((s, QW), BF16),
        compiler_params=_cp("parallel"),
    )(q, kp, vp, kc, vc, sink)


def _attn_bwd(do, o, q, kp, vp, kc, vc, sink):
    s, c = q.shape[0], kc.shape[0]

    def body(do_ref, o_ref, q_ref, kp_ref, vp_ref, kc_ref, vc_ref, sink_ref, dq_ref, dkp_ref, dvp_ref, dkc_ref,
             dvc_ref, dsink_ref):
        n = pl.program_id(0)

        @pl.when(n == 0)
        def _():
            for r in (dkp_ref, dvp_ref, dkc_ref, dvc_ref, dsink_ref):
                r[...] = jnp.zeros_like(r)

        r0, hs, qs, kws, vws, kcs, vcs = _attn_loads(n, q_ref, kp_ref, vp_ref, kc_ref, vc_ref)
        heads = range(ATT_KV_HEADS)
        dqs, dkws, dvws, dkcs, dvcs, dsink = _f_attn_bwd(
            qs, kws, vws, kcs, vcs, sink_ref[...], [_group_rows(o_ref, h) for h in heads],
            [_group_rows(do_ref, h) for h in heads], n, s)
        for h in heads:
            _ungroup_rows(dq_ref, h, dqs[h])
            dkp_ref[pl.ds(r0, 3 * BLOCK), hs(h)] += dkws[h]
            dvp_ref[pl.ds(r0, 3 * BLOCK), hs(h)] += dvws[h]
            dkc_ref[:, hs(h)] += dkcs[h]
            dvc_ref[:, hs(h)] += dvcs[h]
        dsink_ref[...] += dsink

    full = lambda shape: pl.BlockSpec(shape, lambda n: (0, 0))
    return pl.pallas_call(
        body, name="attn_bwd", grid=(s // BLOCK,),
        in_specs=[pl.BlockSpec((BLOCK, QW), lambda n: (n, 0))] * 2 + _attn_specs(s, c),
        out_specs=[pl.BlockSpec((BLOCK, QW), lambda n: (n, 0)), full((s + 2 * BLOCK, KVW)), full((s + 2 * BLOCK, KVW)),
                   full((c, KVW)), full((c, KVW)), full((1, ATT_HEADS))],
        out_shape=[jax.ShapeDtypeStruct((s, QW), F32), jax.ShapeDtypeStruct((s + 2 * BLOCK, KVW), F32),
                   jax.ShapeDtypeStruct((s + 2 * BLOCK, KVW), F32), jax.ShapeDtypeStruct((c, KVW), F32),
                   jax.ShapeDtypeStruct((c, KVW), F32), jax.ShapeDtypeStruct((1, ATT_HEADS), F32)],
        compiler_params=_cp("arbitrary"),
    )(do, o, q, kp, vp, kc, vc, sink)


GLA_GROUPS = 1
GLA_GROUP_HEADS = GLA_HEADS // GLA_GROUPS
GKG, GVG = GKW // GLA_GROUPS, GVW // GLA_GROUPS


def _gla_masks(heads=GLA_HEADS):
    hk = np.arange(heads * GLA_DK) // GLA_DK
    hv = np.arange(heads * GLA_DV) // GLA_DV
    head_k = (np.arange(heads)[:, None] == hk[None, :]).astype(np.float32)
    head_v = (np.arange(heads)[:, None] == hv[None, :]).astype(np.float32)
    bd_t = (hv[:, None] == hk[None, :]).astype(np.float32)
    return jnp.asarray(head_k), jnp.asarray(head_v), jnp.asarray(bd_t)


def _group_states(st):
    return jnp.stack([st[g * GVG:(g + 1) * GVG, g * GKG:(g + 1) * GKG] for g in range(GLA_GROUPS)])


def _ungroup_states(st):
    out = jnp.zeros((GVW, GKW), st.dtype)
    for g in range(GLA_GROUPS):
        out = out.at[g * GVG:(g + 1) * GVG, g * GKG:(g + 1) * GKG].set(st[g])
    return out


def _tri(n, rev, strict=False):
    i = lax.broadcasted_iota(jnp.int32, (n, n), 0)
    j = lax.broadcasted_iota(jnp.int32, (n, n), 1)
    if strict:
        keep = (j > i) if rev else (j < i)
    else:
        keep = (j >= i) if rev else (j <= i)
    return keep


def _f_gla_chunk(q, k, v, la, st, head_k, head_v, bd_t, rev):
    return _f_gla_carry(*_f_gla_intra(q, k, v, la, head_k, head_v, rev), v, st, bd_t)


def _f_gla_intra(q, k, v, la, head_k, head_v, rev):
    heads, kw, vw = head_k.shape[0], q.shape[1], v.shape[1]
    keep = _tri(GLA_CHUNK, rev)
    b = _nn_mask(keep.astype(F32), la)
    bl = jnp.sum(la, axis=0, keepdims=True)
    qd = q * (GLA_DK ** -0.5) * jnp.exp(b)
    ki = k * jnp.exp(-b)
    kd = k * jnp.exp(bl - b)
    q_heads = (qd[None, :, :] * head_k[:, None, :]).reshape(heads * GLA_CHUNK, kw)
    a_all = _nt(q_heads, ki).reshape(heads, GLA_CHUNK, GLA_CHUNK)
    a_all = jnp.where(keep[None, :, :], a_all, 0.0).reshape(heads * GLA_CHUNK, GLA_CHUNK)
    o_all = _nn(a_all, v).reshape(heads, GLA_CHUNK, vw)
    return jnp.sum(o_all * head_v[:, None, :], axis=0), qd, kd, bl


def _f_gla_carry(intra, qd, kd, bl, v, st, bd_t):
    return intra + _nt(qd, st), st * jnp.exp(bl) + bd_t * _tn(v, kd)


def _gla_specs(s, tb, order):
    return [pl.BlockSpec((tb, GKW), lambda i: (order(i), C_GQ // GKW)),
            pl.BlockSpec((tb, GKW), lambda i: (order(i), C_GK // GKW)),
            pl.BlockSpec((tb, GVW), lambda i: (order(i), C_GV // GVW)),
            pl.BlockSpec((tb, GKW), lambda i: (order(i), 0))]


GLA_BLOCK_CHUNKS = 4


def _gla_fwd(p, la_f, la_b, st_f0, st_b0):
    s = p.shape[0]
    tb = GLA_BLOCK_CHUNKS * GLA_CHUNK
    nblk = s // tb
    up, down = (lambda i: i), (lambda i: nblk - 1 - i)
    masks = _gla_masks(GLA_GROUP_HEADS)

    def scan(rev, q_ref, k_ref, v_ref, la_ref, o_ref, sts_ref, st_ref, consts):
        for g in range(GLA_GROUPS):
            gk, gv = slice(g * GKG, (g + 1) * GKG), slice(g * GVG, (g + 1) * GVG)
            st = st_ref[g]
            sts_ref[0, g] = st
            chunks = range(GLA_BLOCK_CHUNKS)
            for ci in (reversed(chunks) if rev else chunks):
                rows = slice(ci * GLA_CHUNK, (ci + 1) * GLA_CHUNK)
                o, st = _f_gla_chunk(q_ref[rows, gk], k_ref[rows, gk], v_ref[rows, gv], la_ref[rows, gk], st, *consts,
                                     rev)
                o_ref[rows, gv] = o
            st_ref[g] = st

    def body(qf, kf, vf, laf, qb, kb, vb, lab, stf0, stb0, hk_ref, hv_ref, bd_ref, of_ref, stsf_ref, ob_ref, stsb_ref,
             stf_ref, stb_ref):
        @pl.when(pl.program_id(0) == 0)
        def _():
            stf_ref[...] = stf0[...]
            stb_ref[...] = stb0[...]

        consts = (hk_ref[...], hv_ref[...], bd_ref[...])
        scan(False, qf, kf, vf, laf, of_ref, stsf_ref, stf_ref, consts)
        scan(True, qb, kb, vb, lab, ob_ref, stsb_ref, stb_ref, consts)

    full = lambda a: pl.BlockSpec(a.shape, lambda i: (0,) * a.ndim)
    outs = lambda order: [pl.BlockSpec((tb, GVW), lambda i: (order(i), 0)),
                          pl.BlockSpec((1, GLA_GROUPS, GVG, GKG), lambda i: (order(i), 0, 0, 0))]
    return pl.pallas_call(
        body, name="gla_fwd", grid=(nblk,),
        in_specs=_gla_specs(s, tb, up) + _gla_specs(s, tb, down) + [full(st_f0), full(st_b0)]
        + [full(m) for m in masks],
        out_specs=outs(up) + outs(down),
        out_shape=[jax.ShapeDtypeStruct((s, GVW), F32), jax.ShapeDtypeStruct((nblk, GLA_GROUPS, GVG, GKG), F32)] * 2,
        scratch_shapes=[pltpu.VMEM((GLA_GROUPS, GVG, GKG), F32)] * 2,
        compiler_params=_cp("arbitrary"),
    )(p, p, p, la_f, p, p, p, la_b, st_f0, st_b0, *masks)


def _gla_bwd(p, la_f, la_b, sts_f, sts_b, do, after=None):
    s = p.shape[0]
    tb = GLA_BLOCK_CHUNKS * GLA_CHUNK
    nblk = s // tb
    up, down = (lambda i: i), (lambda i: nblk - 1 - i)
    masks = _gla_masks(GLA_GROUP_HEADS)
    follow = () if after is None else (after,)

    def back(rev, q_ref, k_ref, v_ref, la_ref, sts_ref, do_ref, dq_ref, dk_ref, dv_ref, dla_ref, dst0_ref, dst_ref,
             consts):
        def block(q, k, v, la, st):
            outs = [None] * GLA_BLOCK_CHUNKS
            chunks = range(GLA_BLOCK_CHUNKS)
            for ci in (reversed(chunks) if rev else chunks):
                outs[ci], st = _f_gla_chunk(q[ci], k[ci], v[ci], la[ci], st, *consts, rev)
            return tuple(outs), st

        for g in range(GLA_GROUPS):
            gk, gv = slice(g * GKG, (g + 1) * GKG), slice(g * GVG, (g + 1) * GVG)
            split = lambda r, cols: tuple(r[ci * GLA_CHUNK:(ci + 1) * GLA_CHUNK, cols].astype(F32)
                                          for ci in range(GLA_BLOCK_CHUNKS))
            _, vjp = jax.vjp(block, split(q_ref, gk), split(k_ref, gk), split(v_ref, gv), split(la_ref, gk),
                             sts_ref[0, g])
            dq, dk, dv, dla, dst = vjp((split(do_ref, gv), dst_ref[g]))
            for ci in range(GLA_BLOCK_CHUNKS):
                rows = slice(ci * GLA_CHUNK, (ci + 1) * GLA_CHUNK)
                dq_ref[rows, gk], dk_ref[rows, gk] = dq[ci].astype(BF16), dk[ci].astype(BF16)
                dv_ref[rows, gv], dla_ref[rows, gk] = dv[ci].astype(BF16), dla[ci]
            dst_ref[g] = dst
            dst0_ref[g] = dst

    def body(*refs):
        ins, (hk_ref, hv_ref, bd_ref) = refs[:12], refs[12:15]
        outs = refs[15 + len(follow):]

        @pl.when(pl.program_id(0) == 0)
        def _():
            outs[10][...] = jnp.zeros_like(outs[10])
            outs[11][...] = jnp.zeros_like(outs[11])

        consts = (hk_ref[...], hv_ref[...], bd_ref[...])
        back(False, *ins[:6], *outs[:5], outs[10], consts)
        back(True, *ins[6:], *outs[5:10], outs[11], consts)

    full = lambda a: pl.BlockSpec(a.shape, lambda i: (0,) * a.ndim)

    def ins(order):
        return _gla_specs(s, tb, order) + [pl.BlockSpec((1, GLA_GROUPS, GVG, GKG), lambda i: (order(i), 0, 0, 0)),
                                           pl.BlockSpec((tb, GVW), lambda i: (order(i), 0))]

    def outs(order):
        blk = lambda w: pl.BlockSpec((tb, w), lambda i: (order(i), 0))
        return [blk(GKW), blk(GKW), blk(GVW), blk(GKW), pl.BlockSpec((GLA_GROUPS, GVG, GKG), lambda i: (0, 0, 0))]

    shapes = [jax.ShapeDtypeStruct((s, GKW), BF16), jax.ShapeDtypeStruct((s, GKW), BF16),
              jax.ShapeDtypeStruct((s, GVW), BF16), jax.ShapeDtypeStruct((s, GKW), F32),
              jax.ShapeDtypeStruct((GLA_GROUPS, GVG, GKG), F32)]
    both = pl.pallas_call(
        body, name="gla_bwd", grid=(nblk,),
        in_specs=ins(down) + ins(up) + [full(m) for m in masks] + [pl.BlockSpec(memory_space=pl.ANY)] * len(follow),
        out_specs=outs(down) + outs(up), out_shape=shapes * 2,
        scratch_shapes=[pltpu.VMEM((GLA_GROUPS, GVG, GKG), F32)] * 2,
        compiler_params=_cp("arbitrary"),
    )(p, p, p, la_f, sts_f, do, p, p, p, la_b, sts_b, do, *masks, *follow)
    return both[:5], both[5:]


def _f_ctx_state(k, v, la_f, la_b, bd_t):
    c = k.shape[0]
    after = _nn_mask(_tri(c, True, strict=True).astype(F32), la_f)
    before = _nn_mask(_tri(c, False, strict=True).astype(F32), la_b)
    return bd_t * _tn(v, k * jnp.exp(after)), bd_t * _tn(v, k * jnp.exp(before))


def _ctx_state(pc, la_f, la_b):
    c = pc.shape[0]
    bd_t = _gla_masks()[2]

    def body(k_ref, v_ref, lf_ref, lb_ref, bd_ref, sf_ref, sb_ref):
        sf_ref[...], sb_ref[...] = _f_ctx_state(k_ref[...], v_ref[...], lf_ref[...], lb_ref[...], bd_ref[...])

    full = lambda a: pl.BlockSpec(a.shape, lambda i: (0, 0))
    return pl.pallas_call(
        body, name="ctx_state_fwd", grid=(1,),
        in_specs=[pl.BlockSpec((c, GKW), lambda i: (0, C_GK // GKW)), pl.BlockSpec((c, GVW), lambda i: (0, C_GV // GVW)),
                  full(la_f), full(la_b), full(bd_t)],
        out_specs=[pl.BlockSpec((GVW, GKW), lambda i: (0, 0))] * 2,
        out_shape=[jax.ShapeDtypeStruct((GVW, GKW), F32)] * 2,
        compiler_params=_cp("arbitrary"),
    )(pc, pc, la_f, la_b, bd_t)


def _ctx_state_bwd(pc, la_f, la_b, dsf, dsb):
    c = pc.shape[0]
    bd_t = _gla_masks()[2]

    def body(k_ref, v_ref, lf_ref, lb_ref, bd_ref, dsf_ref, dsb_ref, dk_ref, dv_ref, dlf_ref, dlb_ref):
        _, vjp = jax.vjp(lambda k, v, lf, lb: _f_ctx_state(k, v, lf, lb, bd_ref[...]),
                         k_ref[...], v_ref[...], lf_ref[...], lb_ref[...])
        dk, dv, dlf, dlb = vjp((dsf_ref[...], dsb_ref[...]))
        dk_ref[...], dv_ref[...] = dk.astype(BF16), dv.astype(BF16)
        dlf_ref[...], dlb_ref[...] = dlf, dlb

    full = lambda a: pl.BlockSpec(a.shape, lambda i: (0, 0))
    return pl.pallas_call(
        body, name="ctx_state_bwd", grid=(1,),
        in_specs=[pl.BlockSpec((c, GKW), lambda i: (0, C_GK // GKW)), pl.BlockSpec((c, GVW), lambda i: (0, C_GV // GVW)),
                  full(la_f), full(la_b), full(bd_t), full(dsf), full(dsb)],
        out_specs=[pl.BlockSpec((c, GKW), lambda i: (0, 0)), pl.BlockSpec((c, GVW), lambda i: (0, 0)),
                   pl.BlockSpec((c, GKW), lambda i: (0, 0)), pl.BlockSpec((c, GKW), lambda i: (0, 0))],
        out_shape=[jax.ShapeDtypeStruct((c, GKW), BF16), jax.ShapeDtypeStruct((c, GVW), BF16),
                   jax.ShapeDtypeStruct((c, GKW), F32), jax.ShapeDtypeStruct((c, GKW), F32)],
        compiler_params=_cp("arbitrary"),
    )(pc, pc, la_f, la_b, bd_t, dsf, dsb)


_SRC_COLS = ((0, QW), (QW + 2 * KVW + 2 * GKW, GVW), (QW + 2 * KVW + 2 * GKW + GVW, GVW), (QW, KVW), (QW + KVW, KVW),
             (QW + 2 * KVW, GKW), (QW + 2 * KVW + GKW, GKW), (IN_COLS - 2 * GATE_RANK, 2 * GATE_RANK))
_DST_COLS = (C_Q, C_GV, C_GG, C_K, C_V, C_GQ, C_GK, C_Z)


def _pack_w_in(w_in):
    parts = [w_in[:, s:s + n] for s, n in _SRC_COLS]
    parts.append(jnp.zeros((w_in.shape[0], IN_PAD - C_Z - 2 * GATE_RANK), w_in.dtype))
    return jnp.concatenate(parts, axis=1)


def _unpack_w_in_grad(g):
    by_src = sorted(zip(_SRC_COLS, _DST_COLS))
    return jnp.concatenate([g[:, d:d + n] for (_, n), d in by_src], axis=1)


def _prep_gate_weights(w_gate_fwd, w_gate_bwd):
    pad_rows = lambda w, at: jnp.zeros((LANES, GKW), F32).at[at:at + GATE_RANK].set(w)
    return {"wg_f": pad_rows(w_gate_fwd, 0), "wg_b": pad_rows(w_gate_bwd, GATE_RANK)}


def _local_step(x, ctx, target, ada, ada_c, w, late_weights, reduce_behind=None, reduce_w_in=None):
    s, d = x.shape
    sh1, sc1, gt1, sh2, sc2, gt2 = [ada[:, i * d:(i + 1) * d] for i in range(6)]
    sh1c, sc1c = ada_c[:, :d], ada_c[:, d:2 * d]
    cos, sin = _rope_tables(s)
    gt = jnp.tile(w["g_gla_norm"], (1, GLA_HEADS))

    h = _norm_mod("pre_mix", x, w["g_pre_mix"], sh1, sc1)
    hc = _norm_mod("pre_mix_ctx", ctx, w["g_pre_mix"], sh1c, sc1c)
    w_in, token = w["w_in"](h, cos, sin)
    p = _mm("proj_in", h, w_in, "nn", after=token)
    pc = _mm("proj_in_ctx", hc, w_in, "nn")
    q_rot, k_rot, v_b = _rope_fwd("rope", p, cos, sin)
    pad = ((BLOCK, BLOCK), (0, 0))
    kp, vp = jnp.pad(k_rot, pad), jnp.pad(v_b, pad)
    kc, vc = pc[:, C_K:C_K + KVW].astype(BF16), pc[:, C_V:C_V + KVW].astype(BF16)
    attn = _attn_fwd(q_rot, kp, vp, kc, vc, w["attn_sink"])
    gate_w = (w["wg_f"], w["wg_b"], w["b_gate_fwd"], w["b_gate_bwd"])
    la_f, la_b = _gate_fwd("gate", p, *gate_w)
    la_fc, la_bc = _gate_fwd("gate_ctx", pc, *gate_w)
    st_f0, st_b0 = _ctx_state(pc, la_fc, la_bc)
    o_f, sts_f, o_b, sts_b = _gla_fwd(p, la_f, la_b, _group_states(st_f0), _group_states(st_b0))
    mix = _gla_out("gla_out", attn, o_f, o_b, p, gt)
    w_out, w_ffn_in_t, w_ffn_out = late_weights(attn)
    y = _mm("proj_out", mix, w_out, "nn", BF16)
    x1, h2 = _post_res_norm_mod("post_mix_pre_ffn", x, y, w["g_post_mix"], gt1, w["g_pre_ffn"], sh2, sc2)
    u, a = _ffn_in_swiglu("ffn_in", h2, w_ffn_in_t)
    f = _mm("ffn_out", a, w_ffn_out, "nn", BF16)
    g = {}
    dx2, df, loss, g["g_post_ffn"], dgt2 = _post_res_loss("post_ffn_loss", x1, f, w["g_post_ffn"], gt2, target)

    late_rows = {"w_ffn_in_t": w_ffn_in_t.shape[0] // N_CHIP, "w_ffn_out": w_ffn_out.shape[0] // N_CHIP,
                 "w_out": w_out.shape[0] // N_CHIP}
    order = sorted(late_rows, key=lambda n: -late_rows[n])
    offsets, slab_rows = _slab_layout([late_rows[n] for n in order])
    late_at, slab_shape = dict(zip(order, offsets)), (N_CHIP, slab_rows, d)
    slab = _slab_zero_gaps("late_grads_gaps", slab_shape, [late_rows[n] for n in order], offsets)
    slab = _dw_into_slab("ffn_out_dw", a, df, slab, slab_shape, late_at["w_ffn_out"])
    du = _ffn_out_dx_swiglu_bwd("ffn_out_dx", df, w_ffn_out, u)
    dh2 = _mm("ffn_in_dx", du, w_ffn_in_t, "nn", BF16)
    slab = _dw_into_slab("ffn_in_dw", du, h2, slab, slab_shape, late_at["w_ffn_in_t"])
    dx1, dy, g["g_pre_ffn"], dsh2, dsc2, g["g_post_mix"], dgt1 = _norm_mod_post_res_bwd(
        "pre_ffn_post_mix_bwd", dh2, dx2, x1, y, w["g_pre_ffn"], sh2, sc2, w["g_post_mix"], gt1)
    dmix = _mm("proj_out_dx", dy, w_out, "nt", BF16)
    slab = _dw_into_slab("proj_out_dw", mix, dy, slab, slab_shape, late_at["w_out"])
    g["late"], g["late_at"], g["late_rows"] = slab, late_at, late_rows
    rb, sink, token = reduce_behind, w["attn_sink"], None
    if rb is not None:
        gt = _behind(gt, rb.start_slab(slab))
    d_o, dgg, dgt = _gla_out_bwd("gla_out_bwd", dmix, o_f, o_b, p, gt)
    g["g_gla_norm"] = jnp.sum(dgt.reshape(GLA_HEADS, GLA_DV), axis=0, keepdims=True)
    if rb is not None:
        token = rb.pair(dgg)
    gla_f, gla_b = _gla_bwd(p, la_f, la_b, sts_f, sts_b, d_o, token)
    (dla_f, dst_f0), (dla_b, dst_b0) = gla_f[3:], gla_b[3:]
    dst_f0, dst_b0 = _ungroup_states(dst_f0), _ungroup_states(dst_b0)
    if rb is not None:
        sink = _behind(sink, rb.total(dla_b))
    dgkc, dgvc, dla_fc, dla_bc = _ctx_state_bwd(pc, la_fc, la_bc, dst_f0, dst_b0)
    dz, dwf, dwb, dbf, dbb = _gate_bwd("gate_bwd", p, dla_f, dla_b, *gate_w)
    dzc, dwfc, dwbc, dbfc, dbbc = _gate_bwd("gate_ctx_bwd", pc, dla_fc, dla_bc, *gate_w)
    g["w_gate_fwd"] = (dwf + dwfc)[:GATE_RANK]
    g["w_gate_bwd"] = (dwb + dwbc)[GATE_RANK:2 * GATE_RANK]
    g["b_gate_fwd"], g["b_gate_bwd"] = dbf + dbfc, dbb + dbbc
    dq_rot, dkp, dvp, dkc, dvc, g["attn_sink"] = _attn_bwd(dmix, attn, q_rot, kp, vp, kc, vc, sink)
    if rb is not None:
        g["late"] = rb.result(dq_rot)
    dp = _proj_grad("proj_grad", dq_rot, dkp[BLOCK:BLOCK + s], dvp[BLOCK:BLOCK + s], cos, sin, gla_f[:3], gla_b[:3],
                    dgg, dz)
    c_rows = ctx.shape[0]
    zeros = lambda n: jnp.zeros((c_rows, n), BF16)
    dpc = jnp.concatenate([zeros(QW), dgvc, zeros(GVW), dkc.astype(BF16), dvc.astype(BF16), zeros(GKW), dgkc, dzc],
                          axis=1)
    g["w_in"] = _mm("proj_in_dw", h, dp, "tn", init=_mm("proj_in_ctx_dw", hc, dpc, "tn"))
    token = None if reduce_w_in is None else reduce_w_in.start(g["w_in"])
    dh = _mm("proj_in_dx", dp, w_in, "nt", BF16, after=token)
    dhc = _mm("proj_in_ctx_dx", dpc, w_in, "nt")
    if reduce_w_in is not None:
        sh1 = _behind(sh1, reduce_w_in.pair(dh))
    dx, dg_a, dsh1, dsc1 = _norm_mod_bwd("pre_mix_bwd", dh, dx1, x, w["g_pre_mix"], sh1, sc1)
    if reduce_w_in is not None:
        dsh1 = _behind(dsh1, reduce_w_in.total(dx))
    _, dg_b, dsh1c, dsc1c = _norm_mod_bwd("pre_mix_ctx_bwd", dhc, jnp.zeros_like(dhc), ctx, w["g_pre_mix"], sh1c,
                                          sc1c)
    g["g_pre_mix"] = dg_a + dg_b
    d_ada = jnp.concatenate([dsh1, dsc1, dgt1, dsh2, dsc2, dgt2], axis=1)
    d_ada_c = jnp.concatenate([dsh1c, dsc1c, jnp.zeros((1, 4 * d), F32)], axis=1)
    return loss, dx, g, d_ada, d_ada_c


HBM = pl.BlockSpec(memory_space=pltpu.HBM)
N_DEV, N_CHIP = 8, 4


def _place():
    x, y, c = lax.axis_index("x"), lax.axis_index("y"), lax.axis_index("c")
    return x, y, c, [(1 - x, y), (x, 1 - y), (1 - x, 1 - y)]


def _row_tile(n, mult, cap):
    return max(t for t in range(mult, min(n, cap) + 1, mult) if n % t == 0)


def _ag_small(name, v, after=None):
    follow = () if after is None else (after,)

    def body(v_ref, *rest):
        out_ref, send_sems, recv_sems = rest[len(follow):]
        x, y, c, _ = _place()
        out_ref[4 * x + 2 * y + c] = v_ref[...]

        def peer(r):
            return ((1 - x) if r & 4 else x, (1 - y) if r & 2 else y, (1 - c) if r & 1 else c)

        def copy(r, block):
            px, py, pc = block
            return pltpu.make_async_remote_copy(
                src_ref=v_ref, dst_ref=out_ref.at[4 * px + 2 * py + pc], send_sem=send_sems.at[r - 1],
                recv_sem=recv_sems.at[r - 1], device_id=peer(r), device_id_type=MESH)

        sends = [copy(r, (x, y, c)) for r in range(1, N_DEV)]
        for cp in sends:
            cp.start()
        for r in range(1, N_DEV):
            copy(r, peer(r)).wait_recv()
        for cp in sends:
            cp.wait_send()

    return pl.pallas_call(
        body, name=name, out_shape=jax.ShapeDtypeStruct((N_DEV,) + v.shape, v.dtype),
        in_specs=[pl.BlockSpec(memory_space=pltpu.VMEM)] + [pl.BlockSpec(memory_space=pl.ANY)] * len(follow),
        out_specs=pl.BlockSpec(memory_space=pltpu.VMEM),
        scratch_shapes=[pltpu.SemaphoreType.DMA((N_DEV - 1,)), pltpu.SemaphoreType.DMA((N_DEV - 1,))],
    )(v, *follow)


def _halves(c, rows, mult):
    hr = rows // 2
    return pl.ds(pl.multiple_of(c * hr, mult), hr), pl.ds(pl.multiple_of((1 - c) * hr, mult), hr)


def _add_half(name, g, a, c_idx):
    n_sh, hr, n = a.shape
    tr = _row_tile(hr, 16, 1024)
    nb = hr // tr

    def body(c_ref, g_ref, a_ref, o_ref):
        o_ref[...] = (g_ref[...] + a_ref[...]).astype(o_ref.dtype)

    return pl.pallas_call(
        body, name=name, out_shape=jax.ShapeDtypeStruct(a.shape, BF16),
        grid_spec=pltpu.PrefetchScalarGridSpec(
            num_scalar_prefetch=1, grid=(n_sh, nb),
            in_specs=[pl.BlockSpec((1, tr, n), lambda s, i, c_ref: (s, c_ref[0] * nb + i, 0)),
                      pl.BlockSpec((1, tr, n), lambda s, i, c_ref: (s, i, 0))],
            out_specs=pl.BlockSpec((1, tr, n), lambda s, i, c_ref: (s, i, 0))),
        compiler_params=_cp("parallel", "parallel"),
    )(c_idx, g, a)


def _sum_chips(name, b, c_idx):
    n_sh, hr, n = b.shape
    tr = _row_tile(hr, 16, 1024)
    nb = hr // tr

    def body(c_ref, b0, b1, b2, b3, o_ref):
        o_ref[...] = ((b0[0].astype(F32) + b1[0].astype(F32)) + b2[0].astype(F32)) + b3[0].astype(F32)

    return pl.pallas_call(
        body, name=name, out_shape=jax.ShapeDtypeStruct((2 * hr, n), F32),
        grid_spec=pltpu.PrefetchScalarGridSpec(
            num_scalar_prefetch=1, grid=(nb,),
            in_specs=[pl.BlockSpec((1, tr, n), functools.partial(lambda i, c_ref, k: (k, i, 0), k=k))
                      for k in range(n_sh)],
            out_specs=pl.BlockSpec((tr, n), lambda i, c_ref: (c_ref[0] * nb + i, 0))),
        compiler_params=_cp("parallel"),
    )(c_idx, b, b, b, b)


SEM = pl.BlockSpec(memory_space=pltpu.SEMAPHORE)
ANY = pl.BlockSpec(memory_space=pl.ANY)
DATAFLOW = pltpu.SideEffectType.DATAFLOW_SIDE_EFFECTING


def _remote(src, dst, send_sems, recv_sems, k, to):
    return pltpu.make_async_remote_copy(src_ref=src, dst_ref=dst, send_sem=send_sems.at[k], recv_sem=recv_sems.at[k],
                                        device_id=to, device_id_type=MESH)


def _split_copy(name, src, land_shape, land_dtype, n, plan, after=None):
    after = jnp.zeros((8, LANES), F32) if after is None else after

    def start_body(src_ref, land_ref, after_ref, send_sems, recv_sems, src_thru, land_thru, token):
        for cp in plan(src_ref, land_ref, send_sems, recv_sems)[0]:
            cp.start()
        token[...] = jnp.zeros_like(token)

    sems = pltpu.SemaphoreType.DMA((n,))
    send_sems, recv_sems, src_thru, land_thru, token = pl.pallas_call(
        start_body, name=name + "_start",
        out_shape=(sems, sems, pltpu.HBM(src.shape, src.dtype), pltpu.HBM(land_shape, land_dtype),
                   jax.ShapeDtypeStruct((8, LANES), F32)),
        in_specs=(HBM, HBM, ANY), out_specs=(SEM, SEM, HBM, HBM, pl.BlockSpec(memory_space=pltpu.VMEM)),
        input_output_aliases={0: 2, 1: 3}, compiler_params=pltpu.CompilerParams(has_side_effects=DATAFLOW),
    )(pltpu.with_memory_space_constraint(src, pltpu.HBM),
      pltpu.with_memory_space_constraint(lax.empty(land_shape, land_dtype), pltpu.HBM), after)

    def wait(*after):
        def wait_body(src_ref, land_ref, send_sems, recv_sems, *rest):
            sent, received = plan(src_ref, land_ref, send_sems, recv_sems)
            for cp in sent:
                cp.wait_send()
            for cp in received:
                cp.wait_recv()

        return pl.pallas_call(
            wait_body, name=name + "_wait",
            out_shape=(pltpu.HBM(src.shape, src.dtype), pltpu.HBM(land_shape, land_dtype)),
            in_specs=(HBM, HBM, SEM, SEM) + (ANY,) * len(after), out_specs=(HBM, HBM),
            input_output_aliases={0: 0, 1: 1}, compiler_params=pltpu.CompilerParams(has_side_effects=DATAFLOW),
        )(src_thru, land_thru, send_sems, recv_sems, *after)

    return token, wait


def _split_gather(name, shards, after):
    k, n, plan = len(shards), 3 * len(shards), _plan_gather

    def start_body(*refs):
        for cp in plan(refs[:k], refs[k:2 * k], refs[2 * k + 1], refs[2 * k + 2])[0]:
            cp.start()
        refs[-1][...] = jnp.zeros_like(refs[-1])

    sems = pltpu.SemaphoreType.DMA((n,))
    bufs = [pltpu.HBM(s.shape, s.dtype) for s in shards] + [pltpu.HBM((N_CHIP,) + s.shape, s.dtype) for s in shards]
    hbm = lambda t: pltpu.with_memory_space_constraint(t, pltpu.HBM)
    outs = pl.pallas_call(
        start_body, name=name + "_start", out_shape=(sems, sems, *bufs, jax.ShapeDtypeStruct((8, LANES), F32)),
        in_specs=(HBM,) * (2 * k) + (ANY,),
        out_specs=(SEM, SEM) + (HBM,) * (2 * k) + (pl.BlockSpec(memory_space=pltpu.VMEM),),
        input_output_aliases={i: 2 + i for i in range(2 * k)},
        compiler_params=pltpu.CompilerParams(has_side_effects=DATAFLOW),
    )(*[hbm(s) for s in shards], *[hbm(lax.empty((N_CHIP,) + s.shape, s.dtype)) for s in shards], after)
    send_sems, recv_sems, thru, token = outs[0], outs[1], outs[2:2 + 2 * k], outs[-1]

    def wait(*after):
        def wait_body(*refs):
            sent, received = plan(refs[:k], refs[k:2 * k], refs[2 * k], refs[2 * k + 1])
            for cp in sent:
                cp.wait_send()
            for cp in received:
                cp.wait_recv()

        res = pl.pallas_call(
            wait_body, name=name + "_wait", out_shape=tuple(bufs),
            in_specs=(HBM,) * (2 * k) + (SEM, SEM) + (ANY,) * len(after), out_specs=(HBM,) * (2 * k),
            input_output_aliases={i: i for i in range(2 * k)},
            compiler_params=pltpu.CompilerParams(has_side_effects=DATAFLOW),
        )(*thru, send_sems, recv_sems, *after)
        return res[:k], res[k:]

    return token, wait


def _behind(x, token):
    return x + token[0, 0]


def _plan_gather(src_refs, land_refs, send_sems, recv_sems):
    x, y, c, chips = _place()
    pairs = list(enumerate(zip(src_refs, land_refs)))
    sent = [_remote(s, l.at[2 * x + y], send_sems, recv_sems, 3 * i + j, (px, py, c))
            for i, (s, l) in pairs for j, (px, py) in enumerate(chips)]
    received = [_remote(s, l.at[2 * px + py], send_sems, recv_sems, 3 * i + j, (px, py, c))
                for i, (s, l) in pairs for j, (px, py) in enumerate(chips)]
    return sent, received


def _plan_swap(src_ref, land_ref, send_sems, recv_sems):
    x, y, c, _ = _place()
    _, other_half = _halves(c, src_ref.shape[1], 8)
    cp = _remote(src_ref.at[pl.ds(0, src_ref.shape[0]), other_half], land_ref, send_sems, recv_sems, 0, (x, y, 1 - c))
    return [cp], [cp]


def _plan_scatter(src_ref, land_ref, send_sems, recv_sems):
    x, y, c, chips = _place()
    sent = [_remote(src_ref.at[2 * px + py], land_ref.at[2 * x + y], send_sems, recv_sems, j, (px, py, c))
            for j, (px, py) in enumerate(chips)]
    received = [_remote(src_ref.at[2 * px + py], land_ref.at[2 * px + py], send_sems, recv_sems, j, (px, py, c))
                for j, (px, py) in enumerate(chips)]
    return sent, received


def _plan_share(src_ref, land_ref, send_sems, recv_sems):
    x, y, c, _ = _place()
    mine_half, other_half = _halves(c, src_ref.shape[0], 8)
    return ([_remote(src_ref.at[mine_half], src_ref.at[mine_half], send_sems, recv_sems, 0, (x, y, 1 - c))],
            [_remote(src_ref.at[other_half], src_ref.at[other_half], send_sems, recv_sems, 0, (x, y, 1 - c))])


class _GatherBehind:
    def __init__(self, name, shards, chip, after):
        self.chip = chip
        self.token, self.wait = _split_gather(name, shards, after)

    def result(self, *after):
        shards, lands = self.wait(*after)
        return [lax.dynamic_update_slice(land, shard[None], (self.chip, 0, 0)) for shard, land in zip(shards, lands)]


class _ReduceBehind:
    def __init__(self, name, chip, c_idx):
        self.name, self.chip, self.c_idx = name, chip, c_idx

    def start_slab(self, g):
        n_sh, rows, n = g.shape
        token, self.wait = _split_copy(self.name + "_swap", g, (n_sh, rows // 2, n), g.dtype, 1, _plan_swap)
        return token

    def pair(self, after):
        g, a = self.wait(after)
        h = _add_half(self.name + "_pair", g, a, self.c_idx)
        token, self.wait = _split_copy(self.name + "_scatter", h, h.shape, h.dtype, 3, _plan_scatter)
        return token

    def total(self, after):
        h, b = self.wait(after)
        b = lax.dynamic_update_slice(b, lax.dynamic_slice_in_dim(h, self.chip, 1, axis=0), (self.chip, 0, 0))
        f = _sum_chips(self.name + "_sum", b, self.c_idx)
        token, self.wait = _split_copy(self.name + "_share", f, (8, LANES), f.dtype, 1, _plan_share)
        return token

    def result(self, after):
        return self.wait(after)[0]


class _ReduceColsBehind(_ReduceBehind):
    def start(self, g_padded):
        g = _unpack_w_in_grad(g_padded)
        n = g.shape[1] // N_CHIP
        return self.start_slab(jnp.stack([g[:, k * n:(k + 1) * n] for k in range(N_CHIP)]))


def _f_adamw(w, g, m, v):
    m = ADAM_B1 * m + (1.0 - ADAM_B1) * g
    v = ADAM_B2 * v + (1.0 - ADAM_B2) * (g * g)
    m_hat = m / (1.0 - ADAM_B1 ** ADAM_STEP)
    v_hat = v / (1.0 - ADAM_B2 ** ADAM_STEP)
    return -ADAM_LR * (m_hat / (jnp.sqrt(v_hat) + ADAM_EPS) + ADAM_WD * w), m, v


def _adamw(name, w, g, m, v):
    rows, n = w.shape
    return _rowwise(name, lambda w, g, m, v: (_f_adamw(w, g, m, v), ()), rows, [(t, n, 0) for t in (w, g, m, v)], [],
                    [(n, F32)] * 3, [], tm=_row_tile(rows, 8, 256))


def _adamw_many(name, ws, gs, ms, vs):
    k = len(ws)

    def body(*refs):
        ins, outs = refs[:4 * k], refs[4 * k:]
        for i in range(k):
            res = _f_adamw(ins[i][...], ins[k + i][...], ins[2 * k + i][...], ins[3 * k + i][...])
            for j in range(3):
                outs[j * k + i][...] = res[j]

    out = pl.pallas_call(body, name=name, out_shape=[jax.ShapeDtypeStruct(w.shape, F32) for w in ws] * 3)(
        *ws, *gs, *ms, *vs)
    return out[:k], out[k:2 * k], out[2 * k:]


def _pack_rows(parts):
    rows = []
    for t in parts:
        t = t.reshape(-1)
        rows.append(jnp.pad(t, (0, -t.shape[0] % LANES)).reshape(-1, LANES))
    out = jnp.concatenate(rows, axis=0)
    return jnp.pad(out, ((0, -out.shape[0] % 8), (0, 0)))


def _unpack_rows(packed, shapes):
    out, r = [], 0
    for shp in shapes:
        n = int(np.prod(shp))
        nr = -(-n // LANES)
        out.append(packed[r:r + nr].reshape(-1)[:n].reshape(shp))
        r += nr
    return out


def _sum_blocks(name, g):
    def body(g_ref, o_ref):
        acc = g_ref[0]
        for k in range(1, g.shape[0]):
            acc = acc + g_ref[k]
        o_ref[...] = acc

    return pl.pallas_call(body, name=name, out_shape=jax.ShapeDtypeStruct(g.shape[1:], F32))(g)


def _silu(t):
    return t * _sigmoid(t)


def _ada_fwd(cc, w_ada):
    n = w_ada.shape[1]
    tn = _row_tile(n, LANES, 512)

    def body(cc_ref, w_ref, o_ref):
        o_ref[...] = _nn(_silu(cc_ref[...]), w_ref[...])

    return pl.pallas_call(
        body, name="ada_fwd", grid=(n // tn,), out_shape=jax.ShapeDtypeStruct((cc.shape[0], n), F32),
        in_specs=[pl.BlockSpec(cc.shape, lambda j: (0, 0)), pl.BlockSpec((w_ada.shape[0], tn), lambda j: (0, j))],
        out_specs=pl.BlockSpec((cc.shape[0], tn), lambda j: (0, j)), compiler_params=_cp("parallel"),
    )(cc, w_ada)


def _ada_bwd(cc, dm, w_ada):
    d, n = w_ada.shape
    tn = _row_tile(n, LANES, 512)

    def body(cc_ref, dm_ref, w_ref, gw_ref, ds_ref):
        @pl.when(pl.program_id(0) == 0)
        def _():
            ds_ref[...] = jnp.zeros_like(ds_ref)

        gw_ref[...] = _raw_dot("tn", _silu(cc_ref[...]), dm_ref[...], True)
        ds_ref[...] += _raw_dot("nt", dm_ref[...], w_ref[...], False)

    return pl.pallas_call(
        body, name="ada_bwd", grid=(n // tn,),
        out_shape=[jax.ShapeDtypeStruct((d, n), F32), jax.ShapeDtypeStruct(cc.shape, F32)],
        in_specs=[pl.BlockSpec(cc.shape, lambda j: (0, 0)), pl.BlockSpec((cc.shape[0], tn), lambda j: (0, j)),
                  pl.BlockSpec((d, tn), lambda j: (0, j))],
        out_specs=[pl.BlockSpec((d, tn), lambda j: (0, j)), pl.BlockSpec(cc.shape, lambda j: (0, 0))],
        compiler_params=_cp("arbitrary"),
    )(cc, dm, w_ada)


def _c_ctx_grad(parts, c_ctx):
    def body(p_ref, c_ref, o_ref):
        ds = ((p_ref[0] + p_ref[1]) + p_ref[2]) + p_ref[3]
        _, vjp = jax.vjp(_silu, c_ref[...])
        o_ref[...] = vjp(ds)[0]

    return pl.pallas_call(body, name="c_ctx_grad", out_shape=jax.ShapeDtypeStruct(c_ctx.shape, F32))(parts, c_ctx)


def kernel(x, c, ctx, c_ctx, w_ada, b_ada, g_pre_mix, g_post_mix, g_pre_ffn, g_post_ffn, w_in, attn_sink, w_gate_fwd, b_gate_fwd, w_gate_bwd, b_gate_bwd, g_gla_norm, w_out, w_ffn_in, w_ffn_out, loss_target, m_c_ctx, m_w_ada, m_b_ada, m_g_pre_mix, m_g_post_mix, m_g_pre_ffn, m_g_post_ffn, m_w_in, m_attn_sink, m_w_gate_fwd, m_b_gate_fwd, m_w_gate_bwd, m_b_gate_bwd, m_g_gla_norm, m_w_out, m_w_ffn_in, m_w_ffn_out, v_c_ctx, v_w_ada, v_b_ada, v_g_pre_mix, v_g_post_mix, v_g_pre_ffn, v_g_post_ffn, v_w_in, v_attn_sink, v_w_gate_fwd, v_b_gate_fwd, v_w_gate_bwd, v_b_gate_bwd, v_g_gla_norm, v_w_out, v_w_ffn_in, v_w_ffn_out):
    xi, yi, ci = lax.axis_index("x"), lax.axis_index("y"), lax.axis_index("c")
    dev, chip = 4 * xi + 2 * yi + ci, 2 * xi + yi
    c_idx = jnp.reshape(ci, (1,)).astype(jnp.int32)
    d = x.shape[-1]
    n_ada, n_in, n_f = w_ada.shape[-1], w_in.shape[-1], w_ffn_in.shape[-1]
    r_out, r_f = w_out.shape[1], w_ffn_out.shape[1]
    n_gate = w_gate_fwd.shape[-1]
    by_chip = lambda t: t[0::2]

    rc = -(-d // LANES)
    g1 = _ag_small("gather_cond", _pack_rows([c[0], w_gate_fwd[0], w_gate_bwd[0]]))
    c_all = g1[:, :rc].reshape(N_DEV, -1)[:, :d]
    gr = GATE_RANK * n_gate // LANES
    gate_full = lambda off: jnp.transpose(by_chip(g1)[:, off:off + gr].reshape(N_CHIP, GATE_RANK, n_gate),
                                          (1, 0, 2)).reshape(GATE_RANK, N_CHIP * n_gate)
    wgf, wgb = gate_full(rc), gate_full(rc + gr)
    cc = jnp.concatenate([c_all, c_ctx[None, :], jnp.zeros((7, d), F32)], axis=0)

    g2 = _ag_small("gather_ada", _ada_fwd(cc, w_ada[0]).reshape(-1, LANES))
    ada_all = jnp.transpose(by_chip(g2).reshape(N_CHIP, 16, n_ada), (1, 0, 2)).reshape(16, N_CHIP * n_ada) + b_ada
    first = _GatherBehind("gather_w_in", [w_in[0].astype(BF16)], chip, g2)
    late_shards = [w_out[0].astype(BF16), jnp.transpose(w_ffn_in[0]).astype(BF16), w_ffn_out[0].astype(BF16)]
    late = []

    def first_weights(*after):
        w_in_g, = first.result(*after, *late_shards)
        late.append(_GatherBehind("gather_late", late_shards, chip, w_in_g))
        return _pack_w_in(jnp.concatenate([w_in_g[k] for k in range(N_CHIP)], axis=1)), late[0].token

    def late_weights(after):
        return [t.reshape(-1, d) for t in late[0].result(after)]

    ada_all = _behind(ada_all, first.token)
    ada = lax.dynamic_slice(ada_all, (dev, 0), (1, N_CHIP * n_ada))
    ada_c = ada_all[N_DEV:N_DEV + 1]

    w = _prep_gate_weights(wgf, wgb)
    w.update(w_in=first_weights, g_pre_mix=g_pre_mix, g_post_mix=g_post_mix, g_pre_ffn=g_pre_ffn, g_post_ffn=g_post_ffn,
             attn_sink=attn_sink, b_gate_fwd=b_gate_fwd, b_gate_bwd=b_gate_bwd, g_gla_norm=g_gla_norm)

    reduce_behind = _ReduceBehind("reduce_late", chip, c_idx)
    reduce_w_in = _ReduceColsBehind("reduce_w_in", chip, c_idx)
    loss_lanes, grad_x, g, d_ada, d_ada_c = _local_step(x[0], ctx[0], loss_target[0], ada, ada_c, w, late_weights,
                                                        reduce_behind, reduce_w_in)

    small = ("g_pre_mix", "g_post_mix", "g_pre_ffn", "g_post_ffn", "attn_sink", "b_gate_fwd", "b_gate_bwd",
             "g_gla_norm", "w_gate_fwd", "w_gate_bwd")
    shapes = [(1, 6 * d)] * 2 + [g[n].shape for n in small] + [(1, LANES)]
    g3 = _ag_small("gather_small_grads", _pack_rows([d_ada, d_ada_c] + [g[n] for n in small] + [loss_lanes]))
    tot = dict(zip(("d_ada", "d_ada_c") + small + ("loss",),
                   _unpack_rows(_sum_blocks("sum_small_grads", g3), shapes)))
    r_ada = 6 * d // LANES
    dm = jnp.concatenate([g3[:, :r_ada].reshape(N_DEV, 6 * d), tot["d_ada_c"], jnp.zeros((7, 6 * d), F32)], axis=0)
    grads = {n: tot[n] for n in small[:8]}
    grads["b_ada"] = _sum_blocks("sum_b_ada", dm.reshape(16, r_ada, LANES)).reshape(1, 6 * d)
    grads["w_gate_fwd"] = lax.dynamic_slice(tot["w_gate_fwd"], (0, chip * n_gate), (GATE_RANK, n_gate))[None]
    grads["w_gate_bwd"] = lax.dynamic_slice(tot["w_gate_bwd"], (0, chip * n_gate), (GATE_RANK, n_gate))[None]
    gw_ada, dsc = _ada_bwd(cc, lax.dynamic_slice(dm, (0, chip * n_ada), (16, n_ada)), w_ada[0])
    grads["w_ada"] = gw_ada[None]
    g4 = _ag_small("gather_c_ctx", _pack_rows([dsc[N_DEV]]))
    grads["c_ctx"] = _c_ctx_grad(by_chip(g4), _pack_rows([c_ctx])).reshape(-1)[:d]

    grads["w_in"] = reduce_w_in.result(g4)[None]
    part = lambda n: g["late"][g["late_at"][n]:g["late_at"][n] + g["late_rows"][n]]
    grads["w_ffn_in"], grads["w_ffn_out"], grads["w_out"] = (jnp.transpose(part("w_ffn_in_t"))[None],
                                                            part("w_ffn_out")[None], part("w_out")[None])

    names = ("c_ctx", "w_ada", "b_ada", "g_pre_mix", "g_post_mix", "g_pre_ffn", "g_post_ffn", "w_in", "attn_sink",
             "w_gate_fwd", "b_gate_fwd", "w_gate_bwd", "b_gate_bwd", "g_gla_norm", "w_out", "w_ffn_in", "w_ffn_out")
    weights = dict(zip(names, (c_ctx, w_ada, b_ada, g_pre_mix, g_post_mix, g_pre_ffn, g_post_ffn, w_in, attn_sink,
                               w_gate_fwd, b_gate_fwd, w_gate_bwd, b_gate_bwd, g_gla_norm, w_out, w_ffn_in,
                               w_ffn_out)))
    m_in = dict(zip(names, (m_c_ctx, m_w_ada, m_b_ada, m_g_pre_mix, m_g_post_mix, m_g_pre_ffn, m_g_post_ffn, m_w_in,
                            m_attn_sink, m_w_gate_fwd, m_b_gate_fwd, m_w_gate_bwd, m_b_gate_bwd, m_g_gla_norm,
                            m_w_out, m_w_ffn_in, m_w_ffn_out)))
    v_in = dict(zip(names, (v_c_ctx, v_w_ada, v_b_ada, v_g_pre_mix, v_g_post_mix, v_g_pre_ffn, v_g_post_ffn, v_w_in,
                            v_attn_sink, v_w_gate_fwd, v_b_gate_fwd, v_w_gate_bwd, v_b_gate_bwd, v_g_gla_norm,
                            v_w_out, v_w_ffn_in, v_w_ffn_out)))
    large = ("w_ada", "w_in", "w_out", "w_ffn_in", "w_ffn_out")
    tiny = tuple(n for n in names if n not in large)
    delta, new_m, new_v = {}, {}, {}
    for n in large:
        dl, nm, nv = _adamw("adamw_" + n, weights[n][0], grads[n][0], m_in[n][0], v_in[n][0])
        delta[n], new_m[n], new_v[n] = dl[None], nm[None], nv[None]
    for n in tiny:
        grads[n] = grads[n].reshape(weights[n].shape)
    as_rows = lambda t: t.reshape(-1, t.shape[-1])
    res = _adamw_many("adamw_small", *[[as_rows(t[n]) for n in tiny] for t in (weights, grads, m_in, v_in)])
    for out, vals in zip((delta, new_m, new_v), res):
        out.update({n: val.reshape(weights[n].shape) for n, val in zip(tiny, vals)})

    return (tot["loss"][0, 0], grad_x[None], *[grads[n] for n in names], *[delta[n] for n in names], *[new_m[n] for n in names],
            *[new_v[n] for n in names])
```

```python
import functools

import jax
import jax.numpy as jnp
import numpy as np
from jax import lax
from jax.experimental import pallas as pl
from jax.experimental.pallas import tpu as pltpu

F32 = jnp.float32
BF16 = jnp.bfloat16
MESH = pl.DeviceIdType.MESH

HEAD_DIM = 64
ATT_HEADS = 8
ATT_KV_HEADS = 2
ATT_GROUP = ATT_HEADS // ATT_KV_HEADS
WINDOW = 128
BLOCK = 128
GRID_W = 64
ROPE_BASE = 10000.0
GLA_HEADS = 8
GLA_DK = 32
GLA_DV = 64
GLA_CHUNK = 64
GATE_RANK = 16
GATE_TAU = 16.0
NEG_INF = -1e30
QW = ATT_HEADS * HEAD_DIM
KVW = ATT_KV_HEADS * HEAD_DIM
GKW = GLA_HEADS * GLA_DK
GVW = GLA_HEADS * GLA_DV
IN_COLS = QW + 2 * KVW + 2 * GKW + 2 * GVW + 2 * GATE_RANK
LANES = 128
IN_PAD = IN_COLS + LANES - 2 * GATE_RANK
C_Q, C_GV, C_GG = 0, QW, QW + GVW
C_K = C_GG + GVW
C_V = C_K + KVW
C_GQ = C_V + KVW
C_GK = C_GQ + GKW
C_Z = C_GK + GKW
MIX = QW + GVW

ADAM_LR, ADAM_B1, ADAM_B2, ADAM_EPS, ADAM_WD, ADAM_STEP = 0.001, 0.9, 0.999, 1e-08, 0.01, 10

VMEM_LIMIT = 56 * 1024 * 1024


def _cp(*sem):
    return pltpu.CompilerParams(dimension_semantics=sem, vmem_limit_bytes=VMEM_LIMIT)


def _pick(n, cands):
    for t in cands:
        if n % t == 0:
            return t
    return n


_DIMS = {"nn": (((1,), (0,)), ((), ())), "nt": (((1,), (1,)), ((), ())), "tn": (((0,), (0,)), ((), ()))}


def _raw_dot(mode, a, b, hi):
    dot = lambda u, v: lax.dot_general(u, v, _DIMS[mode], preferred_element_type=F32)
    if not hi:
        return dot(a.astype(BF16), b.astype(BF16))
    a, b = a.astype(F32), b.astype(F32)
    a_hi, b_hi = a.astype(BF16), b.astype(BF16)
    out = dot(a_hi, b_hi)
    if hi != "a":
        out = out + dot((a - a_hi.astype(F32)).astype(BF16), b_hi)
    if hi != "b":
        out = out + dot(a_hi, (b - b_hi.astype(F32)).astype(BF16))
    return out


def _make_dot(mode, hi):
    @jax.custom_vjp
    def dot(a, b):
        return _raw_dot(mode, a, b, hi)

    def fwd(a, b):
        return _raw_dot(mode, a, b, hi), (a, b)

    def bwd(res, dc):
        a, b = res
        if mode == "nn":
            return (_raw_dot("nt", dc, b, "b" if hi == "b" else bool(hi)),
                    _raw_dot("tn", a, dc, "a" if hi == "a" else bool(hi)))
        if mode == "nt":
            return _raw_dot("nn", dc, b, bool(hi)), _raw_dot("tn", dc, a, bool(hi))
        return _raw_dot("nt", b, dc, bool(hi)), _raw_dot("nn", a, dc, bool(hi))

    dot.defvjp(fwd, bwd)
    return dot


_nn, _nt, _tn = _make_dot("nn", False), _make_dot("nt", False), _make_dot("tn", False)
_nn_mask, _nn_by_exact = _make_dot("nn", "a"), _make_dot("nn", "b")


MM_VMEM_BUDGET = 44 * 1024 * 1024


def _halvings(n):
    out = [n]
    while out[-1] % (2 * LANES) == 0:
        out.append(out[-1] // 2)
    return out


def _mm_tiles(mode, m, n, k, a_bytes, b_bytes, o_bytes, init_bytes=0):
    tms = [t for t in dict.fromkeys((m, m // 2, m // 4, 2048, 1024, 512, 256, 128))
           if m % t == 0 and t % (LANES if mode == "tn" else 16) == 0 and t <= 4096] or [m]
    if mode == "tn":
        fits = [(k // tk + 0.5 * (m // tm), tm, tk)
                for tk in (4096, 2048, 1024, 512, 256, 128) if k % tk == 0 for tm in tms
                if 2 * (tk * tm * a_bytes + tk * n * b_bytes + tm * n * (o_bytes + init_bytes)) <= MM_VMEM_BUDGET]
        if fits:
            _, tm, tk = min(fits)
            return tm, n, tk
    tks = ([t for t in (512, 256, 128) if k % t == 0] or [k]) if mode == "tn" else _halvings(k)
    for tn in _halvings(n):
        for tk in tks:
            for tm in tms:
                acc = tm * tn * 4 if (k // tk > 1 and o_bytes != 4) else 0
                tiles = tm * tk * a_bytes + tk * tn * b_bytes + tm * tn * (o_bytes + init_bytes)
                if 2 * tiles + acc <= MM_VMEM_BUDGET:
                    return tm, tn, tk
    return tms[-1], _halvings(n)[-1], tks[-1]


def _mm(name, a, b, mode, out_dtype=F32, init=None, after=None):
    follow = () if after is None else (after,)
    if mode == "nn":
        (m, k), n = a.shape, b.shape[1]
    elif mode == "nt":
        (m, k), n = a.shape, b.shape[0]
    else:
        (k, m), n = a.shape, b.shape[1]
    tm, tn, tk = _mm_tiles(mode, m, n, k, a.dtype.itemsize, b.dtype.itemsize, jnp.dtype(out_dtype).itemsize,
                           0 if init is None else 4)
    nk = k // tk
    use_acc = nk > 1 and out_dtype != F32

    inits = () if init is None else (init,)

    def body(a_ref, b_ref, *rest):
        rest = rest[:len(inits)] + rest[len(inits) + len(follow):]
        o_ref, acc = rest[len(inits)], rest[len(inits) + 1:]
        part = _raw_dot(mode, a_ref[...], b_ref[...], False)
        first = lambda: part + rest[0][...] if inits else part
        if nk == 1:
            o_ref[...] = first().astype(o_ref.dtype)
            return
        acc_ref = acc[0] if use_acc else o_ref
        kk = pl.program_id(2)

        @pl.when(kk == 0)
        def _():
            acc_ref[...] = first()

        @pl.when(kk > 0)
        def _():
            acc_ref[...] += part

        if use_acc:
            @pl.when(kk == nk - 1)
            def _():
                o_ref[...] = acc_ref[...].astype(o_ref.dtype)

    if mode == "nn":
        a_spec = pl.BlockSpec((tm, tk), lambda i, j, kk: (i, kk))
        b_spec = pl.BlockSpec((tk, tn), lambda i, j, kk: (kk, j))
    elif mode == "nt":
        a_spec = pl.BlockSpec((tm, tk), lambda i, j, kk: (i, kk))
        b_spec = pl.BlockSpec((tn, tk), lambda i, j, kk: (j, kk))
    else:
        a_spec = pl.BlockSpec((tk, tm), lambda i, j, kk: (kk, i))
        b_spec = pl.BlockSpec((tk, tn), lambda i, j, kk: (kk, j))
    return pl.pallas_call(
        body, name=name, grid=(m // tm, n // tn, nk),
        in_specs=[a_spec, b_spec] + [pl.BlockSpec((tm, tn), lambda i, j, kk: (i, j))] * len(inits)
        + [pl.BlockSpec(memory_space=pl.ANY)] * len(follow),
        out_specs=pl.BlockSpec((tm, tn), lambda i, j, kk: (i, j)),
        out_shape=jax.ShapeDtypeStruct((m, n), out_dtype),
        scratch_shapes=[pltpu.VMEM((tm, tn), F32)] if use_acc else [],
        compiler_params=_cp("parallel", "parallel", "arbitrary"),
    )(a, b, *inits, *follow)


def _slab_layout(rows):
    offsets, at = [], 0
    for r in rows:
        at = -(-at // r) * r
        offsets.append(at)
        at += r
    return offsets, -(-at // 32) * 32


def _slab_zero_gaps(name, shape, rows, offsets):
    gaps = [(o + r, nxt) for o, r, nxt in zip(offsets, rows, offsets[1:] + [shape[1]]) if nxt > o + r]
    slab = None
    for i, (lo, hi) in enumerate(gaps):
        step = int(np.gcd(lo, hi - lo))

        def body(*refs):
            refs[-1][...] = jnp.zeros_like(refs[-1])

        slab = pl.pallas_call(
            body, name=f"{name}_{i}", grid=(shape[0], (hi - lo) // step), out_shape=jax.ShapeDtypeStruct(shape, F32),
            in_specs=[] if slab is None else [pl.BlockSpec(memory_space=pl.ANY)],
            out_specs=pl.BlockSpec((1, step, shape[2]), functools.partial(lambda k, j, b: (k, b + j, 0), b=lo // step)),
            input_output_aliases={} if slab is None else {0: 0}, compiler_params=_cp("parallel", "parallel"),
        )(*(() if slab is None else (slab,)))
    return slab


def _dw_into_slab(name, a, b, slab, shape, at):
    (k, m), n = a.shape, b.shape[1]
    r = m // N_CHIP
    fits = [(k // tk + 0.5 * (m // tm), tm, tk)
            for tk in (4096, 2048, 1024, 512, 256, 128) if k % tk == 0 for tm in (m, m // 2, r) if tm % LANES == 0
            if 2 * (tk * tm * a.dtype.itemsize + tk * n * b.dtype.itemsize + tm * n * 4) <= MM_VMEM_BUDGET]
    _, tm, tk = min(fits)
    per, nk = tm // r, k // tk

    def body(a_ref, b_ref, *rest):
        o_ref = rest[-1]
        part = _raw_dot("tn", a_ref[...], b_ref[...], False).reshape(o_ref.shape)
        if nk == 1:
            o_ref[...] = part
            return
        kk = pl.program_id(1)

        @pl.when(kk == 0)
        def _():
            o_ref[...] = part

        @pl.when(kk > 0)
        def _():
            o_ref[...] += part

    prev = () if slab is None else (slab,)
    return pl.pallas_call(
        body, name=name, grid=(m // tm, nk), out_shape=jax.ShapeDtypeStruct(shape, F32),
        in_specs=[pl.BlockSpec((tk, tm), lambda i, kk: (kk, i)), pl.BlockSpec((tk, n), lambda i, kk: (kk, 0))]
        + [pl.BlockSpec(memory_space=pl.ANY)] * len(prev),
        out_specs=pl.BlockSpec((per, r, n), lambda i, kk: (i, at // r, 0)),
        input_output_aliases={2: 0} if prev else {}, compiler_params=_cp("parallel", "arbitrary"),
    )(a, b, *prev)


def _rowwise(name, fn, rows, row_ins, full_ins, row_outs, acc_outs, tm=None):
    tm = tm or _pick(rows, (512, 256, 128))
    n_r, n_f, n_o, n_a = len(row_ins), len(full_ins), len(row_outs), len(acc_outs)

    def body(*refs):
        ins, outs = refs[:n_r + n_f], refs[n_r + n_f:]
        vals = [r[...].astype(F32) for r in ins]
        ro, ao = fn(*vals)
        for r, val in zip(outs[:n_o], ro):
            r[...] = val.astype(r.dtype)
        if n_a:
            @pl.when(pl.program_id(0) == 0)
            def _():
                for r in outs[n_o:]:
                    r[...] = jnp.zeros_like(r)

            for r, val in zip(outs[n_o:], ao):
                r[...] += val

    in_specs = [pl.BlockSpec((tm, w), functools.partial(lambda i, cb: (i, cb), cb=cb)) for _, w, cb in row_ins]
    in_specs += [pl.BlockSpec(a.shape, lambda i: (0, 0)) for a in full_ins]
    out_specs = [pl.BlockSpec((tm, w), lambda i: (i, 0)) for w, _ in row_outs]
    out_specs += [pl.BlockSpec(s, lambda i: (0, 0)) for s in acc_outs]
    out_shape = [jax.ShapeDtypeStruct((rows, w), dt) for w, dt in row_outs]
    out_shape += [jax.ShapeDtypeStruct(s, F32) for s in acc_outs]
    return pl.pallas_call(
        body, name=name, grid=(rows // tm,), in_specs=in_specs, out_specs=out_specs, out_shape=out_shape,
        compiler_params=_cp("arbitrary" if n_a else "parallel"),
    )(*[a for a, _, _ in row_ins], *full_ins)


def _rn(x):
    return x * lax.rsqrt(jnp.mean(x * x, axis=-1, keepdims=True) + 1e-6)


def _sigmoid(t):
    return 1.0 / (1.0 + jnp.exp(-t))


def _f_norm_mod(x, g, sh, sc):
    return _rn(x) * g * (1.0 + sc) + sh


def _f_post_res(xr, y, g, gate):
    return xr + gate * (_rn(y) * g)


@jax.custom_vjp
def _f_swiglu(g, u):
    return g * _sigmoid(g) * u


def _f_swiglu_fwd(g, u):
    s = _sigmoid(g)
    return g * s * u, (g, u, s)


def _f_swiglu_bwd(res, da):
    g, u, s = res
    gs = g * s
    return da * u * (s + gs * (1.0 - s)), da * gs


_f_swiglu.defvjp(_f_swiglu_fwd, _f_swiglu_bwd)


def _logsig(u):
    return jnp.minimum(u, 0.0) - jnp.log(1.0 + jnp.exp(-jnp.abs(u)))


def _f_gate(z, wf, wb, bf, bb):
    return _logsig(_nn(z, wf) + bf) / GATE_TAU, _logsig(_nn(z, wb) + bb) / GATE_TAU


def _f_gla_out(of, ob, gg, gt, bd):
    o = of + ob
    ms = _nn_by_exact(o * o, bd)
    return o * lax.rsqrt(ms + 1e-6) * gt * (gg * _sigmoid(gg))


def _norm_mod(name, x, g, sh, sc):
    rows, d = x.shape
    return _rowwise(name, lambda x, g, sh, sc: ((_f_norm_mod(x, g, sh, sc),), ()), rows,
                    [(x, d, 0)], [g, sh, sc], [(d, BF16)], [])[0]


def _rn_bwd(x, dn):
    r = lax.rsqrt(jnp.mean(x * x, axis=-1, keepdims=True) + 1e-6)
    n = x * r
    return r * (dn - n * jnp.mean(dn * n, axis=-1, keepdims=True)), n


def _norm_mod_grads(dh, x, g, sc):
    dx, n = _rn_bwd(x, dh * (g * (1.0 + sc)))
    t = jnp.sum(dh * n, axis=0, keepdims=True)
    return dx, (1.0 + sc) * t, jnp.sum(dh, axis=0, keepdims=True), g * t


def _post_res_grads(dout, y, g, gate):
    dy, n = _rn_bwd(y, dout * (gate * g))
    t = jnp.sum(dout * n, axis=0, keepdims=True)
    return dy, gate * t, g * t


def _norm_mod_bwd(name, dh, dres, x, g, sh, sc):
    rows, d = x.shape

    def fn(dh, dres, x, g, sh, sc):
        dx, dg, dsh, dsc = _norm_mod_grads(dh, x, g, sc)
        return (dx + dres,), (dg, dsh, dsc)

    return _rowwise(name, fn, rows, [(dh, d, 0), (dres, d, 0), (x, d, 0)], [g, sh, sc], [(d, F32)],
                    [(1, d)] * 3)


def _post_res_norm_mod(name, xr, y, g_post, gate, g_pre, sh, sc):
    rows, d = xr.shape

    def fn(xr, y, g_post, gate, g_pre, sh, sc):
        x1 = _f_post_res(xr, y, g_post, gate)
        return (x1, _f_norm_mod(x1, g_pre, sh, sc)), ()

    return _rowwise(name, fn, rows, [(xr, d, 0), (y, d, 0)], [g_post, gate, g_pre, sh, sc], [(d, F32), (d, BF16)], [])


def _norm_mod_post_res_bwd(name, dh, dres, x1, y, g_pre, sh, sc, g_post, gate):
    rows, d = x1.shape

    def fn(dh, dres, x1, y, g_pre, sh, sc, g_post, gate):
        dx1, dg_pre, dsh, dsc = _norm_mod_grads(dh, x1, g_pre, sc)
        dx1 = dx1 + dres
        dy, dg_post, dgate = _post_res_grads(dx1, y, g_post, gate)
        return (dx1, dy), (dg_pre, dsh, dsc, dg_post, dgate)

    return _rowwise(name, fn, rows, [(dh, d, 0), (dres, d, 0), (x1, d, 0), (y, d, 0)], [g_pre, sh, sc, g_post, gate],
                    [(d, F32), (d, BF16)], [(1, d)] * 5, tm=_pick(rows, (256, 128)))


def _post_res_loss(name, xr, y, g, gate, target):
    rows, d = xr.shape

    def fn(xr, y, target, g, gate):
        diff = _f_post_res(xr, y, g, gate) - target
        part = 0.5 * jnp.sum(jnp.mean(diff * diff, axis=-1, keepdims=True), axis=0, keepdims=True)
        dx2 = diff * (1.0 / d)
        dy, dg, dgate = _post_res_grads(dx2, y, g, gate)
        return (dx2, dy), (jnp.broadcast_to(part, (1, LANES)), dg, dgate)

    return _rowwise(name, fn, rows, [(xr, d, 0), (y, d, 0), (target, d, 0)], [g, gate], [(d, F32), (d, BF16)],
                    [(1, LANES), (1, d), (1, d)])


def _mm_rows(name, a, b, mode, fn, extras, outs):
    m, k = a.shape
    tm = _pick(m, (256, 128))

    def body(a_ref, b_ref, *rest):
        tiles = fn(_raw_dot(mode, a_ref[...], b_ref[...], False), *[e[...] for e in rest[:len(extras)]])
        for r, val in zip(rest[len(extras):], tiles):
            r[...] = val.astype(r.dtype)

    row = lambda w: pl.BlockSpec((tm, w), lambda i: (i, 0))
    return pl.pallas_call(
        body, name=name, grid=(m // tm,),
        in_specs=[row(k), pl.BlockSpec(b.shape, lambda i: (0, 0))] + [row(e.shape[1]) for e in extras],
        out_specs=[row(w) for w, _ in outs], out_shape=[jax.ShapeDtypeStruct((m, w), dt) for w, dt in outs],
        compiler_params=_cp("parallel"),
    )(a, b, *extras)


def _ffn_in_swiglu(name, h, w_t):
    f = w_t.shape[0] // 2
    fn = lambda u: (u, _f_swiglu(u[:, :f], u[:, f:]))
    return _mm_rows(name, h, w_t, "nt", fn, [], [(2 * f, BF16), (f, BF16)])


def _ffn_out_dx_swiglu_bwd(name, df, w_out, u):
    f = w_out.shape[0]

    def fn(da, u):
        u = u.astype(F32)
        _, vjp = jax.vjp(_f_swiglu, u[:, :f], u[:, f:])
        return (jnp.concatenate(vjp(da), axis=1),)

    return _mm_rows(name, df, w_out, "nt", fn, [u], [(2 * f, BF16)])[0]


def _gate_fwd(name, p, wf, wb, bf, bb):
    rows = p.shape[0]
    return _rowwise(name, lambda z, wf, wb, bf, bb: (_f_gate(z, wf, wb, bf, bb), ()), rows,
                    [(p, LANES, C_Z // LANES)], [wf, wb, bf, bb], [(GKW, F32)] * 2, [])


def _gate_bwd(name, p, dla_f, dla_b, wf, wb, bf, bb):
    rows = p.shape[0]

    def fn(z, dlf, dlb, wf, wb, bf, bb):
        _, vjp = jax.vjp(_f_gate, z, wf, wb, bf, bb)
        dz, dwf, dwb, dbf, dbb = vjp((dlf, dlb))
        return (dz,), (dwf, dwb, dbf, dbb)

    return _rowwise(name, fn, rows, [(p, LANES, C_Z // LANES), (dla_f, GKW, 0), (dla_b, GKW, 0)],
                    [wf, wb, bf, bb], [(LANES, BF16)], [(LANES, GKW), (LANES, GKW), (1, GKW), (1, GKW)])


def _head_mean_matrix():
    h = np.arange(GVW) // GLA_DV
    return jnp.asarray((h[:, None] == h[None, :]).astype(np.float32) / GLA_DV)


def _gla_out(name, attn, of, ob, p, gt):
    rows = of.shape[0]
    bd = _head_mean_matrix()
    fn = lambda attn, of, ob, gg, gt, bd: ((jnp.concatenate([attn, _f_gla_out(of, ob, gg, gt, bd)], axis=1),), ())
    return _rowwise(name, fn, rows, [(attn, QW, 0), (of, GVW, 0), (ob, GVW, 0), (p, GVW, C_GG // GVW)], [gt, bd],
                    [(MIX, BF16)], [])[0]


def _gla_out_bwd(name, dmix, of, ob, p, gt):
    rows = of.shape[0]
    bd = _head_mean_matrix()

    def fn(dm, of, ob, gg, gt, bd):
        _, vjp = jax.vjp(lambda of, gg, gt: _f_gla_out(of, ob, gg, gt, bd), of, gg, gt)
        do, dgg, dgt = vjp(dm)
        return (do, dgg), (dgt,)

    return _rowwise(name, fn, rows, [(dmix, GVW, 1), (of, GVW, 0), (ob, GVW, 0), (p, GVW, C_GG // GVW)], [gt, bd],
                    [(GVW, BF16), (GVW, BF16)], [(1, GVW)])


def _rope_tables(n_tokens):
    t = jnp.arange(n_tokens)
    row = (t // GRID_W).astype(F32)
    col = (t % GRID_W).astype(F32)
    half = HEAD_DIM // 2
    inv_freq = ROPE_BASE ** (-jnp.arange(0, half, 2, dtype=F32) / half)
    ang_r = row[:, None] * inv_freq[None, :]
    ang_c = col[:, None] * inv_freq[None, :]
    ang = jnp.concatenate([ang_r, ang_r, ang_c, ang_c], axis=-1)
    sign = jnp.concatenate([-jnp.ones((16,), F32), jnp.ones((16,), F32)] * 2)
    cos, sin = jnp.cos(ang), jnp.sin(ang) * sign[None, :]
    return jnp.tile(cos, (1, 2)), jnp.tile(sin, (1, 2))


def _rot_pairs(x):
    w = x.shape[-1]
    lane = lax.broadcasted_iota(jnp.int32, x.shape, x.ndim - 1)
    return jnp.where((lane % 32) < 16, pltpu.roll(x, w - 16, x.ndim - 1), pltpu.roll(x, 16, x.ndim - 1))


def _rope_apply(x, cos, sin_signed, inverse):
    reps = x.shape[-1] // LANES
    cos = jnp.concatenate([cos] * reps, axis=-1) if reps > 1 else cos
    sin = jnp.concatenate([sin_signed] * reps, axis=-1) if reps > 1 else sin_signed
    if inverse:
        return x * cos + _rot_pairs(x * sin)
    return x * cos + _rot_pairs(x) * sin


def _rope_fwd(name, p, cos, sin):
    rows = p.shape[0]

    def fn(q, k, v, cos, sin):
        return (_rope_apply(q, cos, sin, False), _rope_apply(k, cos, sin, False), v), ()

    return _rowwise(name, fn, rows, [(p, QW, 0), (p, KVW, C_K // KVW), (p, KVW, C_V // KVW), (cos, LANES, 0),
                                     (sin, LANES, 0)], [], [(QW, BF16), (KVW, BF16), (KVW, BF16)], [])


def _proj_grad(name, dq_rot, dk_rot, dv, cos, sin, gla_f, gla_b, dgg, dz):
    rows = dq_rot.shape[0]

    def fn(dq, dk, dv, cos, sin, gqf, gkf, gvf, gqb, gkb, gvb, dgg, dz):
        parts = [_rope_apply(dq, cos, sin, True), gvf + gvb, dgg, _rope_apply(dk, cos, sin, True), dv, gqf + gqb,
                 gkf + gkb, dz]
        return (jnp.concatenate(parts, axis=1),), ()

    ins = [(dq_rot, QW), (dk_rot, KVW), (dv, KVW), (cos, LANES), (sin, LANES)]
    ins += [(t, t.shape[1]) for t in (*gla_f, *gla_b)] + [(dgg, GVW), (dz, LANES)]
    return _rowwise(name, fn, rows, [(t, w, 0) for t, w in ins], [], [(IN_PAD, BF16)], [],
                    tm=_pick(rows, (256, 128)))[0]


GROUP_ROWS = ATT_GROUP * BLOCK


ATT_SCALE = HEAD_DIM ** -0.5


def _attn_setup(sink, n, n_tokens):
    row = lax.broadcasted_iota(jnp.int32, (GROUP_ROWS, 1), 0)
    group = sum((row >= g * BLOCK).astype(jnp.int32) for g in range(1, ATT_GROUP))
    i = lax.broadcasted_iota(jnp.int32, (GROUP_ROWS, 3 * BLOCK), 0) - BLOCK * group
    j = lax.broadcasted_iota(jnp.int32, (GROUP_ROWS, 3 * BLOCK), 1)
    kpos = (n - 1) * BLOCK + j
    mask = (jnp.abs(j - BLOCK - i) <= WINDOW) & (kpos >= 0) & (kpos < n_tokens)
    head_id = lax.broadcasted_iota(jnp.int32, (1, ATT_HEADS), 1)
    sks = []
    for h in range(ATT_KV_HEADS):
        sk = jnp.zeros((GROUP_ROWS, 1), F32)
        for g in range(ATT_GROUP):
            one = jnp.sum(jnp.where(head_id == h * ATT_GROUP + g, sink, 0.0), axis=-1, keepdims=True)
            sk = jnp.where(group == g, one, sk)
        sks.append(sk)
    return mask, group, sks


def _attn_weights(q, kw, kc, sk, mask):
    q = q * ATT_SCALE
    s_w = jnp.where(mask, _raw_dot("nt", q, kw, False), NEG_INF)
    s_c = _raw_dot("nt", q, kc, False)
    m = jnp.maximum(jnp.maximum(jnp.max(s_w, axis=-1, keepdims=True), jnp.max(s_c, axis=-1, keepdims=True)), sk)
    pw, pc, ps = jnp.exp(s_w - m), jnp.exp(s_c - m), jnp.exp(sk - m)
    return q, pw, pc, ps, jnp.sum(pw, axis=-1, keepdims=True) + jnp.sum(pc, axis=-1, keepdims=True) + ps


def _f_attn(qs, kws, vws, kcs, vcs, sink, n, n_tokens):
    mask, _, sks = _attn_setup(sink, n, n_tokens)
    outs = []
    for h in range(ATT_KV_HEADS):
        _, pw, pc, _, den = _attn_weights(qs[h], kws[h], kcs[h], sks[h], mask)
        outs.append((_raw_dot("nn", pw, vws[h], False) + _raw_dot("nn", pc, vcs[h], False)) / den)
    return tuple(outs)


def _f_attn_bwd(qs, kws, vws, kcs, vcs, sink, outs, douts, n, n_tokens):
    mask, group, sks = _attn_setup(sink, n, n_tokens)
    head_id = lax.broadcasted_iota(jnp.int32, (1, ATT_HEADS), 1)
    dot = lambda mode, a, b: _raw_dot(mode, a, b, False)
    dqs, dkws, dvws, dkcs, dvcs, dsink = [], [], [], [], [], jnp.zeros((1, ATT_HEADS), F32)
    for h in range(ATT_KV_HEADS):
        q, pw, pc, ps, den = _attn_weights(qs[h], kws[h], kcs[h], sks[h], mask)
        inv = 1.0 / den
        pw, pc = pw * inv, pc * inv
        dd = jnp.sum(douts[h] * outs[h], axis=-1, keepdims=True)
        dsw = pw * (dot("nt", douts[h], vws[h]) - dd)
        dsc = pc * (dot("nt", douts[h], vcs[h]) - dd)
        dqs.append((dot("nn", dsw, kws[h]) + dot("nn", dsc, kcs[h])) * ATT_SCALE)
        dkws.append(dot("tn", dsw, q))
        dkcs.append(dot("tn", dsc, q))
        dvws.append(dot("tn", pw, douts[h]))
        dvcs.append(dot("tn", pc, douts[h]))
        dsk = -(ps * inv) * dd
        for g in range(ATT_GROUP):
            one = jnp.sum(jnp.where(group == g, dsk, 0.0), axis=0, keepdims=True)
            dsink = dsink + jnp.where(head_id == h * ATT_GROUP + g, one, 0.0)
    return dqs, dkws, dvws, dkcs, dvcs, dsink


def _group_rows(ref, h):
    hs = lambda hq: slice(hq * HEAD_DIM, (hq + 1) * HEAD_DIM)
    return jnp.concatenate([ref[:, hs(h * ATT_GROUP + g)].astype(F32) for g in range(ATT_GROUP)], axis=0)


def _ungroup_rows(ref, h, val):
    for g in range(ATT_GROUP):
        hq = h * ATT_GROUP + g
        ref[:, hq * HEAD_DIM:(hq + 1) * HEAD_DIM] = val[g * BLOCK:(g + 1) * BLOCK].astype(ref.dtype)


def _attn_loads(n, q_ref, kp_ref, vp_ref, kc_ref, vc_ref):
    r0 = pl.multiple_of(n * BLOCK, BLOCK)
    hs = lambda h: slice(h * HEAD_DIM, (h + 1) * HEAD_DIM)
    qs = [_group_rows(q_ref, h) for h in range(ATT_KV_HEADS)]
    kws = [kp_ref[pl.ds(r0, 3 * BLOCK), hs(h)].astype(F32) for h in range(ATT_KV_HEADS)]
    vws = [vp_ref[pl.ds(r0, 3 * BLOCK), hs(h)].astype(F32) for h in range(ATT_KV_HEADS)]
    kcs = [kc_ref[:, hs(h)].astype(F32) for h in range(ATT_KV_HEADS)]
    vcs = [vc_ref[:, hs(h)].astype(F32) for h in range(ATT_KV_HEADS)]
    return r0, hs, qs, kws, vws, kcs, vcs


def _attn_specs(s, c):
    full = lambda shape: pl.BlockSpec(shape, lambda n: (0, 0))
    return [pl.BlockSpec((BLOCK, QW), lambda n: (n, 0)), full((s + 2 * BLOCK, KVW)), full((s + 2 * BLOCK, KVW)),
            full((c, KVW)), full((c, KVW)), full((1, ATT_HEADS))]


def _attn_fwd(q, kp, vp, kc, vc, sink):
    s, c = q.shape[0], kc.shape[0]

    def body(q_ref, kp_ref, vp_ref, kc_ref, vc_ref, sink_ref, o_ref):
        n = pl.program_id(0)
        _, hs, qs, kws, vws, kcs, vcs = _attn_loads(n, q_ref, kp_ref, vp_ref, kc_ref, vc_ref)
        outs = _f_attn(qs, kws, vws, kcs, vcs, sink_ref[...], n, s)
        for h in range(ATT_KV_HEADS):
            _ungroup_rows(o_ref, h, outs[h])

    return pl.pallas_call(
        body, name="attn_fwd", grid=(s // BLOCK,), in_specs=_attn_specs(s, c),
        out_specs=pl.BlockSpec((BLOCK, QW), lambda n: (n, 0)), out_shape=jax.ShapeDtypeStruct((s, QW), BF16),
        compiler_params=_cp("parallel"),
    )(q, kp, vp, kc, vc, sink)


def _attn_bwd(do, o, q, kp, vp, kc, vc, sink):
    s, c = q.shape[0], kc.shape[0]

    def body(do_ref, o_ref, q_ref, kp_ref, vp_ref, kc_ref, vc_ref, sink_ref, dq_ref, dkp_ref, dvp_ref, dkc_ref,
             dvc_ref, dsink_ref):
        n = pl.program_id(0)

        @pl.when(n == 0)
        def _():
            for r in (dkp_ref, dvp_ref, dkc_ref, dvc_ref, dsink_ref):
                r[...] = jnp.zeros_like(r)

        r0, hs, qs, kws, vws, kcs, vcs = _attn_loads(n, q_ref, kp_ref, vp_ref, kc_ref, vc_ref)
        heads = range(ATT_KV_HEADS)
        dqs, dkws, dvws, dkcs, dvcs, dsink = _f_attn_bwd(
            qs, kws, vws, kcs, vcs, sink_ref[...], [_group_rows(o_ref, h) for h in heads],
            [_group_rows(do_ref, h) for h in heads], n, s)
        for h in heads:
            _ungroup_rows(dq_ref, h, dqs[h])
            dkp_ref[pl.ds(r0, 3 * BLOCK), hs(h)] += dkws[h]
            dvp_ref[pl.ds(r0, 3 * BLOCK), hs(h)] += dvws[h]
            dkc_ref[:, hs(h)] += dkcs[h]
            dvc_ref[:, hs(h)] += dvcs[h]
        dsink_ref[...] += dsink

    full = lambda shape: pl.BlockSpec(shape, lambda n: (0, 0))
    return pl.pallas_call(
        body, name="attn_bwd", grid=(s // BLOCK,),
        in_specs=[pl.BlockSpec((BLOCK, QW), lambda n: (n, 0))] * 2 + _attn_specs(s, c),
        out_specs=[pl.BlockSpec((BLOCK, QW), lambda n: (n, 0)), full((s + 2 * BLOCK, KVW)), full((s + 2 * BLOCK, KVW)),
                   full((c, KVW)), full((c, KVW)), full((1, ATT_HEADS))],
        out_shape=[jax.ShapeDtypeStruct((s, QW), BF16), jax.ShapeDtypeStruct((s + 2 * BLOCK, KVW), F32),
                   jax.ShapeDtypeStruct((s + 2 * BLOCK, KVW), F32), jax.ShapeDtypeStruct((c, KVW), F32),
                   jax.ShapeDtypeStruct((c, KVW), F32), jax.ShapeDtypeStruct((1, ATT_HEADS), F32)],
        compiler_params=_cp("arbitrary"),
    )(do, o, q, kp, vp, kc, vc, sink)


GLA_GROUPS = 1
GLA_GROUP_HEADS = GLA_HEADS // GLA_GROUPS
GKG, GVG = GKW // GLA_GROUPS, GVW // GLA_GROUPS


def _gla_masks(heads=GLA_HEADS):
    hk = np.arange(heads * GLA_DK) // GLA_DK
    hv = np.arange(heads * GLA_DV) // GLA_DV
    head_k = (np.arange(heads)[:, None] == hk[None, :]).astype(np.float32)
    head_v = (np.arange(heads)[:, None] == hv[None, :]).astype(np.float32)
    bd_t = (hv[:, None] == hk[None, :]).astype(np.float32)
    return jnp.asarray(head_k), jnp.asarray(head_v), jnp.asarray(bd_t)


def _group_states(st):
    return jnp.stack([st[g * GVG:(g + 1) * GVG, g * GKG:(g + 1) * GKG] for g in range(GLA_GROUPS)])


def _ungroup_states(st):
    out = jnp.zeros((GVW, GKW), st.dtype)
    for g in range(GLA_GROUPS):
        out = out.at[g * GVG:(g + 1) * GVG, g * GKG:(g + 1) * GKG].set(st[g])
    return out


def _tri(n, rev, strict=False):
    i = lax.broadcasted_iota(jnp.int32, (n, n), 0)
    j = lax.broadcasted_iota(jnp.int32, (n, n), 1)
    if strict:
        keep = (j > i) if rev else (j < i)
    else:
        keep = (j >= i) if rev else (j <= i)
    return keep


def _f_gla_chunk(q, k, v, la, st, head_k, head_v, bd_t, rev):
    return _f_gla_carry(*_f_gla_intra(q, k, v, la, head_k, head_v, rev), v, st, bd_t)


def _f_gla_intra(q, k, v, la, head_k, head_v, rev):
    heads, kw, vw = head_k.shape[0], q.shape[1], v.shape[1]
    keep = _tri(GLA_CHUNK, rev)
    b = _nn_mask(keep.astype(F32), la)
    bl = jnp.sum(la, axis=0, keepdims=True)
    qd = q * (GLA_DK ** -0.5) * jnp.exp(b)
    ki = k * jnp.exp(-b)
    kd = k * jnp.exp(bl - b)
    q_heads = (qd[None, :, :] * head_k[:, None, :]).reshape(heads * GLA_CHUNK, kw)
    a_all = _nt(q_heads, ki).reshape(heads, GLA_CHUNK, GLA_CHUNK)
    a_all = jnp.where(keep[None, :, :], a_all, 0.0).reshape(heads * GLA_CHUNK, GLA_CHUNK)
    o_all = _nn(a_all, v).reshape(heads, GLA_CHUNK, vw)
    return jnp.sum(o_all * head_v[:, None, :], axis=0), qd, kd, bl


def _f_gla_carry(intra, qd, kd, bl, v, st, bd_t):
    return intra + _nt(qd, st), st * jnp.exp(bl) + bd_t * _tn(v, kd)


def _gla_specs(s, tb, order):
    return [pl.BlockSpec((tb, GKW), lambda i: (order(i), C_GQ // GKW)),
            pl.BlockSpec((tb, GKW), lambda i: (order(i), C_GK // GKW)),
            pl.BlockSpec((tb, GVW), lambda i: (order(i), C_GV // GVW)),
            pl.BlockSpec((tb, GKW), lambda i: (order(i), 0))]


GLA_BLOCK_CHUNKS = 4


def _gla_fwd(p, la_f, la_b, st_f0, st_b0):
    s = p.shape[0]
    tb = GLA_BLOCK_CHUNKS * GLA_CHUNK
    nblk = s // tb
    up, down = (lambda i: i), (lambda i: nblk - 1 - i)
    masks = _gla_masks(GLA_GROUP_HEADS)

    def scan(rev, q_ref, k_ref, v_ref, la_ref, o_ref, sts_ref, st_ref, consts):
        for g in range(GLA_GROUPS):
            gk, gv = slice(g * GKG, (g + 1) * GKG), slice(g * GVG, (g + 1) * GVG)
            st = st_ref[g]
            sts_ref[0, g] = st
            chunks = range(GLA_BLOCK_CHUNKS)
            for ci in (reversed(chunks) if rev else chunks):
                rows = slice(ci * GLA_CHUNK, (ci + 1) * GLA_CHUNK)
                o, st = _f_gla_chunk(q_ref[rows, gk], k_ref[rows, gk], v_ref[rows, gv], la_ref[rows, gk], st, *consts,
                                     rev)
                o_ref[rows, gv] = o
            st_ref[g] = st

    def body(qf, kf, vf, laf, qb, kb, vb, lab, stf0, stb0, hk_ref, hv_ref, bd_ref, of_ref, stsf_ref, ob_ref, stsb_ref,
             stf_ref, stb_ref):
        @pl.when(pl.program_id(0) == 0)
        def _():
            stf_ref[...] = stf0[...]
            stb_ref[...] = stb0[...]

        consts = (hk_ref[...], hv_ref[...], bd_ref[...])
        scan(False, qf, kf, vf, laf, of_ref, stsf_ref, stf_ref, consts)
        scan(True, qb, kb, vb, lab, ob_ref, stsb_ref, stb_ref, consts)

    full = lambda a: pl.BlockSpec(a.shape, lambda i: (0,) * a.ndim)
    outs = lambda order: [pl.BlockSpec((tb, GVW), lambda i: (order(i), 0)),
                          pl.BlockSpec((1, GLA_GROUPS, GVG, GKG), lambda i: (order(i), 0, 0, 0))]
    return pl.pallas_call(
        body, name="gla_fwd", grid=(nblk,),
        in_specs=_gla_specs(s, tb, up) + _gla_specs(s, tb, down) + [full(st_f0), full(st_b0)]
        + [full(m) for m in masks],
        out_specs=outs(up) + outs(down),
        out_shape=[jax.ShapeDtypeStruct((s, GVW), F32), jax.ShapeDtypeStruct((nblk, GLA_GROUPS, GVG, GKG), F32)] * 2,
        scratch_shapes=[pltpu.VMEM((GLA_GROUPS, GVG, GKG), F32)] * 2,
        compiler_params=_cp("arbitrary"),
    )(p, p, p, la_f, p, p, p, la_b, st_f0, st_b0, *masks)


def _gla_bwd(p, la_f, la_b, sts_f, sts_b, do, after=None):
    s = p.shape[0]
    tb = GLA_BLOCK_CHUNKS * GLA_CHUNK
    nblk = s // tb
    up, down = (lambda i: i), (lambda i: nblk - 1 - i)
    masks = _gla_masks(GLA_GROUP_HEADS)
    follow = () if after is None else (after,)

    def back(rev, q_ref, k_ref, v_ref, la_ref, sts_ref, do_ref, dq_ref, dk_ref, dv_ref, dla_ref, dst0_ref, dst_ref,
             consts):
        def block(q, k, v, la, st):
            outs = [None] * GLA_BLOCK_CHUNKS
            chunks = range(GLA_BLOCK_CHUNKS)
            for ci in (reversed(chunks) if rev else chunks):
                outs[ci], st = _f_gla_chunk(q[ci], k[ci], v[ci], la[ci], st, *consts, rev)
            return tuple(outs), st

        for g in range(GLA_GROUPS):
            gk, gv = slice(g * GKG, (g + 1) * GKG), slice(g * GVG, (g + 1) * GVG)
            split = lambda r, cols: tuple(r[ci * GLA_CHUNK:(ci + 1) * GLA_CHUNK, cols].astype(F32)
                                          for ci in range(GLA_BLOCK_CHUNKS))
            _, vjp = jax.vjp(block, split(q_ref, gk), split(k_ref, gk), split(v_ref, gv), split(la_ref, gk),
                             sts_ref[0, g])
            dq, dk, dv, dla, dst = vjp((split(do_ref, gv), dst_ref[g]))
            for ci in range(GLA_BLOCK_CHUNKS):
                rows = slice(ci * GLA_CHUNK, (ci + 1) * GLA_CHUNK)
                dq_ref[rows, gk], dk_ref[rows, gk] = dq[ci].astype(BF16), dk[ci].astype(BF16)
                dv_ref[rows, gv], dla_ref[rows, gk] = dv[ci].astype(BF16), dla[ci]
            dst_ref[g] = dst
            dst0_ref[g] = dst

    def body(*refs):
        ins, (hk_ref, hv_ref, bd_ref) = refs[:12], refs[12:15]
        outs = refs[15 + len(follow):]

        @pl.when(pl.program_id(0) == 0)
        def _():
            outs[10][...] = jnp.zeros_like(outs[10])
            outs[11][...] = jnp.zeros_like(outs[11])

        consts = (hk_ref[...], hv_ref[...], bd_ref[...])
        back(False, *ins[:6], *outs[:5], outs[10], consts)
        back(True, *ins[6:], *outs[5:10], outs[11], consts)

    full = lambda a: pl.BlockSpec(a.shape, lambda i: (0,) * a.ndim)

    def ins(order):
        return _gla_specs(s, tb, order) + [pl.BlockSpec((1, GLA_GROUPS, GVG, GKG), lambda i: (order(i), 0, 0, 0)),
                                           pl.BlockSpec((tb, GVW), lambda i: (order(i), 0))]

    def outs(order):
        blk = lambda w: pl.BlockSpec((tb, w), lambda i: (order(i), 0))
        return [blk(GKW), blk(GKW), blk(GVW), blk(GKW), pl.BlockSpec((GLA_GROUPS, GVG, GKG), lambda i: (0, 0, 0))]

    shapes = [jax.ShapeDtypeStruct((s, GKW), BF16), jax.ShapeDtypeStruct((s, GKW), BF16),
              jax.ShapeDtypeStruct((s, GVW), BF16), jax.ShapeDtypeStruct((s, GKW), F32),
              jax.ShapeDtypeStruct((GLA_GROUPS, GVG, GKG), F32)]
    both = pl.pallas_call(
        body, name="gla_bwd", grid=(nblk,),
        in_specs=ins(down) + ins(up) + [full(m) for m in masks] + [pl.BlockSpec(memory_space=pl.ANY)] * len(follow),
        out_specs=outs(down) + outs(up), out_shape=shapes * 2,
        scratch_shapes=[pltpu.VMEM((GLA_GROUPS, GVG, GKG), F32)] * 2,
        compiler_params=_cp("arbitrary"),
    )(p, p, p, la_f, sts_f, do, p, p, p, la_b, sts_b, do, *masks, *follow)
    return both[:5], both[5:]


def _f_ctx_state(k, v, la_f, la_b, bd_t):
    c = k.shape[0]
    after = _nn_mask(_tri(c, True, strict=True).astype(F32), la_f)
    before = _nn_mask(_tri(c, False, strict=True).astype(F32), la_b)
    return bd_t * _tn(v, k * jnp.exp(after)), bd_t * _tn(v, k * jnp.exp(before))


def _ctx_state(pc, la_f, la_b):
    c = pc.shape[0]
    bd_t = _gla_masks()[2]

    def body(k_ref, v_ref, lf_ref, lb_ref, bd_ref, sf_ref, sb_ref):
        sf_ref[...], sb_ref[...] = _f_ctx_state(k_ref[...], v_ref[...], lf_ref[...], lb_ref[...], bd_ref[...])

    full = lambda a: pl.BlockSpec(a.shape, lambda i: (0, 0))
    return pl.pallas_call(
        body, name="ctx_state_fwd", grid=(1,),
        in_specs=[pl.BlockSpec((c, GKW), lambda i: (0, C_GK // GKW)), pl.BlockSpec((c, GVW), lambda i: (0, C_GV // GVW)),
                  full(la_f), full(la_b), full(bd_t)],
        out_specs=[pl.BlockSpec((GVW, GKW), lambda i: (0, 0))] * 2,
        out_shape=[jax.ShapeDtypeStruct((GVW, GKW), F32)] * 2,
        compiler_params=_cp("arbitrary"),
    )(pc, pc, la_f, la_b, bd_t)


def _ctx_state_bwd(pc, la_f, la_b, dsf, dsb):
    c = pc.shape[0]
    bd_t = _gla_masks()[2]

    def body(k_ref, v_ref, lf_ref, lb_ref, bd_ref, dsf_ref, dsb_ref, dk_ref, dv_ref, dlf_ref, dlb_ref):
        _, vjp = jax.vjp(lambda k, v, lf, lb: _f_ctx_state(k, v, lf, lb, bd_ref[...]),
                         k_ref[...], v_ref[...], lf_ref[...], lb_ref[...])
        dk, dv, dlf, dlb = vjp((dsf_ref[...], dsb_ref[...]))
        dk_ref[...], dv_ref[...] = dk.astype(BF16), dv.astype(BF16)
        dlf_ref[...], dlb_ref[...] = dlf, dlb

    full = lambda a: pl.BlockSpec(a.shape, lambda i: (0, 0))
    return pl.pallas_call(
        body, name="ctx_state_bwd", grid=(1,),
        in_specs=[pl.BlockSpec((c, GKW), lambda i: (0, C_GK // GKW)), pl.BlockSpec((c, GVW), lambda i: (0, C_GV // GVW)),
                  full(la_f), full(la_b), full(bd_t), full(dsf), full(dsb)],
        out_specs=[pl.BlockSpec((c, GKW), lambda i: (0, 0)), pl.BlockSpec((c, GVW), lambda i: (0, 0)),
                   pl.BlockSpec((c, GKW), lambda i: (0, 0)), pl.BlockSpec((c, GKW), lambda i: (0, 0))],
        out_shape=[jax.ShapeDtypeStruct((c, GKW), BF16), jax.ShapeDtypeStruct((c, GVW), BF16),
                   jax.ShapeDtypeStruct((c, GKW), F32), jax.ShapeDtypeStruct((c, GKW), F32)],
        compiler_params=_cp("arbitrary"),
    )(pc, pc, la_f, la_b, bd_t, dsf, dsb)


_SRC_COLS = ((0, QW), (QW + 2 * KVW + 2 * GKW, GVW), (QW + 2 * KVW + 2 * GKW + GVW, GVW), (QW, KVW), (QW + KVW, KVW),
             (QW + 2 * KVW, GKW), (QW + 2 * KVW + GKW, GKW), (IN_COLS - 2 * GATE_RANK, 2 * GATE_RANK))
_DST_COLS = (C_Q, C_GV, C_GG, C_K, C_V, C_GQ, C_GK, C_Z)


def _pack_w_in(w_in):
    parts = [w_in[:, s:s + n] for s, n in _SRC_COLS]
    parts.append(jnp.zeros((w_in.shape[0], IN_PAD - C_Z - 2 * GATE_RANK), w_in.dtype))
    return jnp.concatenate(parts, axis=1)


def _unpack_w_in_grad(g):
    by_src = sorted(zip(_SRC_COLS, _DST_COLS))
    return jnp.concatenate([g[:, d:d + n] for (_, n), d in by_src], axis=1)


def _prep_gate_weights(w_gate_fwd, w_gate_bwd):
    pad_rows = lambda w, at: jnp.zeros((LANES, GKW), F32).at[at:at + GATE_RANK].set(w)
    return {"wg_f": pad_rows(w_gate_fwd, 0), "wg_b": pad_rows(w_gate_bwd, GATE_RANK)}


def _local_step(x, ctx, target, ada, ada_c, w, late_weights, reduce_behind=None, reduce_w_in=None):
    s, d = x.shape
    sh1, sc1, gt1, sh2, sc2, gt2 = [ada[:, i * d:(i + 1) * d] for i in range(6)]
    sh1c, sc1c = ada_c[:, :d], ada_c[:, d:2 * d]
    cos, sin = _rope_tables(s)
    gt = jnp.tile(w["g_gla_norm"], (1, GLA_HEADS))

    h = _norm_mod("pre_mix", x, w["g_pre_mix"], sh1, sc1)
    hc = _norm_mod("pre_mix_ctx", ctx, w["g_pre_mix"], sh1c, sc1c)
    w_in, token = w["w_in"](h, cos, sin)
    p = _mm("proj_in", h, w_in, "nn", after=token)
    pc = _mm("proj_in_ctx", hc, w_in, "nn")
    q_rot, k_rot, v_b = _rope_fwd("rope", p, cos, sin)
    pad = ((BLOCK, BLOCK), (0, 0))
    kp, vp = jnp.pad(k_rot, pad), jnp.pad(v_b, pad)
    kc, vc = pc[:, C_K:C_K + KVW].astype(BF16), pc[:, C_V:C_V + KVW].astype(BF16)
    attn = _attn_fwd(q_rot, kp, vp, kc, vc, w["attn_sink"])
    gate_w = (w["wg_f"], w["wg_b"], w["b_gate_fwd"], w["b_gate_bwd"])
    la_f, la_b = _gate_fwd("gate", p, *gate_w)
    la_fc, la_bc = _gate_fwd("gate_ctx", pc, *gate_w)
    st_f0, st_b0 = _ctx_state(pc, la_fc, la_bc)
    o_f, sts_f, o_b, sts_b = _gla_fwd(p, la_f, la_b, _group_states(st_f0), _group_states(st_b0))
    mix = _gla_out("gla_out", attn, o_f, o_b, p, gt)
    w_out, w_ffn_in_t, w_ffn_out = late_weights(attn)
    y = _mm("proj_out", mix, w_out, "nn", BF16)
    x1, h2 = _post_res_norm_mod("post_mix_pre_ffn", x, y, w["g_post_mix"], gt1, w["g_pre_ffn"], sh2, sc2)
    u, a = _ffn_in_swiglu("ffn_in", h2, w_ffn_in_t)
    f = _mm("ffn_out", a, w_ffn_out, "nn", BF16)
    g = {}
    dx2, df, loss, g["g_post_ffn"], dgt2 = _post_res_loss("post_ffn_loss", x1, f, w["g_post_ffn"], gt2, target)

    late_rows = {"w_ffn_in_t": w_ffn_in_t.shape[0] // N_CHIP, "w_ffn_out": w_ffn_out.shape[0] // N_CHIP,
                 "w_out": w_out.shape[0] // N_CHIP}
    order = sorted(late_rows, key=lambda n: -late_rows[n])
    offsets, slab_rows = _slab_layout([late_rows[n] for n in order])
    late_at, slab_shape = dict(zip(order, offsets)), (N_CHIP, slab_rows, d)
    slab = _slab_zero_gaps("late_grads_gaps", slab_shape, [late_rows[n] for n in order], offsets)
    slab = _dw_into_slab("ffn_out_dw", a, df, slab, slab_shape, late_at["w_ffn_out"])
    du = _ffn_out_dx_swiglu_bwd("ffn_out_dx", df, w_ffn_out, u)
    dh2 = _mm("ffn_in_dx", du, w_ffn_in_t, "nn", BF16)
    slab = _dw_into_slab("ffn_in_dw", du, h2, slab, slab_shape, late_at["w_ffn_in_t"])
    dx1, dy, g["g_pre_ffn"], dsh2, dsc2, g["g_post_mix"], dgt1 = _norm_mod_post_res_bwd(
        "pre_ffn_post_mix_bwd", dh2, dx2, x1, y, w["g_pre_ffn"], sh2, sc2, w["g_post_mix"], gt1)
    dmix = _mm("proj_out_dx", dy, w_out, "nt", BF16)
    slab = _dw_into_slab("proj_out_dw", mix, dy, slab, slab_shape, late_at["w_out"])
    g["late"], g["late_at"], g["late_rows"] = slab, late_at, late_rows
    rb, sink, token = reduce_behind, w["attn_sink"], None
    if rb is not None:
        gt = _behind(gt, rb.start_slab(slab))
    d_o, dgg, dgt = _gla_out_bwd("gla_out_bwd", dmix, o_f, o_b, p, gt)
    g["g_gla_norm"] = jnp.sum(dgt.reshape(GLA_HEADS, GLA_DV), axis=0, keepdims=True)
    if rb is not None:
        token = rb.pair(dgg)
    gla_f, gla_b = _gla_bwd(p, la_f, la_b, sts_f, sts_b, d_o, token)
    (dla_f, dst_f0), (dla_b, dst_b0) = gla_f[3:], gla_b[3:]
    dst_f0, dst_b0 = _ungroup_states(dst_f0), _ungroup_states(dst_b0)
    if rb is not None:
        sink = _behind(sink, rb.total(dla_b))
    dgkc, dgvc, dla_fc, dla_bc = _ctx_state_bwd(pc, la_fc, la_bc, dst_f0, dst_b0)
    dz, dwf, dwb, dbf, dbb = _gate_bwd("gate_bwd", p, dla_f, dla_b, *gate_w)
    dzc, dwfc, dwbc, dbfc, dbbc = _gate_bwd("gate_ctx_bwd", pc, dla_fc, dla_bc, *gate_w)
    g["w_gate_fwd"] = (dwf + dwfc)[:GATE_RANK]
    g["w_gate_bwd"] = (dwb + dwbc)[GATE_RANK:2 * GATE_RANK]
    g["b_gate_fwd"], g["b_gate_bwd"] = dbf + dbfc, dbb + dbbc
    dq_rot, dkp, dvp, dkc, dvc, g["attn_sink"] = _attn_bwd(dmix, attn, q_rot, kp, vp, kc, vc, sink)
    if rb is not None:
        g["late"] = rb.result(dq_rot)
    dp = _proj_grad("proj_grad", dq_rot, dkp[BLOCK:BLOCK + s], dvp[BLOCK:BLOCK + s], cos, sin, gla_f[:3], gla_b[:3],
                    dgg, dz)
    c_rows = ctx.shape[0]
    zeros = lambda n: jnp.zeros((c_rows, n), BF16)
    dpc = jnp.concatenate([zeros(QW), dgvc, zeros(GVW), dkc.astype(BF16), dvc.astype(BF16), zeros(GKW), dgkc, dzc],
                          axis=1)
    g["w_in"] = _mm("proj_in_dw", h, dp, "tn", init=_mm("proj_in_ctx_dw", hc, dpc, "tn"))
    token = None if reduce_w_in is None else reduce_w_in.start(g["w_in"])
    dh = _mm("proj_in_dx", dp, w_in, "nt", BF16, after=token)
    dhc = _mm("proj_in_ctx_dx", dpc, w_in, "nt")
    if reduce_w_in is not None:
        sh1 = _behind(sh1, reduce_w_in.pair(dh))
    dx, dg_a, dsh1, dsc1 = _norm_mod_bwd("pre_mix_bwd", dh, dx1, x, w["g_pre_mix"], sh1, sc1)
    if reduce_w_in is not None:
        dsh1 = _behind(dsh1, reduce_w_in.total(dx))
    _, dg_b, dsh1c, dsc1c = _norm_mod_bwd("pre_mix_ctx_bwd", dhc, jnp.zeros_like(dhc), ctx, w["g_pre_mix"], sh1c,
                                          sc1c)
    g["g_pre_mix"] = dg_a + dg_b
    d_ada = jnp.concatenate([dsh1, dsc1, dgt1, dsh2, dsc2, dgt2], axis=1)
    d_ada_c = jnp.concatenate([dsh1c, dsc1c, jnp.zeros((1, 4 * d), F32)], axis=1)
    return loss, dx, g, d_ada, d_ada_c


HBM = pl.BlockSpec(memory_space=pltpu.HBM)
N_DEV, N_CHIP = 8, 4


def _place():
    x, y, c = lax.axis_index("x"), lax.axis_index("y"), lax.axis_index("c")
    return x, y, c, [(1 - x, y), (x, 1 - y), (1 - x, 1 - y)]


def _row_tile(n, mult, cap):
    return max(t for t in range(mult, min(n, cap) + 1, mult) if n % t == 0)


def _ag_small(name, v, after=None):
    follow = () if after is None else (after,)

    def body(v_ref, *rest):
        out_ref, send_sems, recv_sems = rest[len(follow):]
        x, y, c, _ = _place()
        out_ref[4 * x + 2 * y + c] = v_ref[...]

        def peer(r):
            return ((1 - x) if r & 4 else x, (1 - y) if r & 2 else y, (1 - c) if r & 1 else c)

        def copy(r, block):
            px, py, pc = block
            return pltpu.make_async_remote_copy(
                src_ref=v_ref, dst_ref=out_ref.at[4 * px + 2 * py + pc], send_sem=send_sems.at[r - 1],
                recv_sem=recv_sems.at[r - 1], device_id=peer(r), device_id_type=MESH)

        sends = [copy(r, (x, y, c)) for r in range(1, N_DEV)]
        for cp in sends:
            cp.start()
        for r in range(1, N_DEV):
            copy(r, peer(r)).wait_recv()
        for cp in sends:
            cp.wait_send()

    return pl.pallas_call(
        body, name=name, out_shape=jax.ShapeDtypeStruct((N_DEV,) + v.shape, v.dtype),
        in_specs=[pl.BlockSpec(memory_space=pltpu.VMEM)] + [pl.BlockSpec(memory_space=pl.ANY)] * len(follow),
        out_specs=pl.BlockSpec(memory_space=pltpu.VMEM),
        scratch_shapes=[pltpu.SemaphoreType.DMA((N_DEV - 1,)), pltpu.SemaphoreType.DMA((N_DEV - 1,))],
    )(v, *follow)


def _halves(c, rows, mult):
    hr = rows // 2
    return pl.ds(pl.multiple_of(c * hr, mult), hr), pl.ds(pl.multiple_of((1 - c) * hr, mult), hr)


def _add_half(name, g, a, c_idx):
    n_sh, hr, n = a.shape
    tr = _row_tile(hr, 16, 1024)
    nb = hr // tr

    def body(c_ref, g_ref, a_ref, o_ref):
        o_ref[...] = (g_ref[...] + a_ref[...]).astype(o_ref.dtype)

    return pl.pallas_call(
        body, name=name, out_shape=jax.ShapeDtypeStruct(a.shape, BF16),
        grid_spec=pltpu.PrefetchScalarGridSpec(
            num_scalar_prefetch=1, grid=(n_sh, nb),
            in_specs=[pl.BlockSpec((1, tr, n), lambda s, i, c_ref: (s, c_ref[0] * nb + i, 0)),
                      pl.BlockSpec((1, tr, n), lambda s, i, c_ref: (s, i, 0))],
            out_specs=pl.BlockSpec((1, tr, n), lambda s, i, c_ref: (s, i, 0))),
        compiler_params=_cp("parallel", "parallel"),
    )(c_idx, g, a)


def _sum_chips(name, b, c_idx):
    n_sh, hr, n = b.shape
    tr = _row_tile(hr, 16, 1024)
    nb = hr // tr

    def body(c_ref, b0, b1, b2, b3, o_ref):
        o_ref[...] = ((b0[0].astype(F32) + b1[0].astype(F32)) + b2[0].astype(F32)) + b3[0].astype(F32)

    return pl.pallas_call(
        body, name=name, out_shape=jax.ShapeDtypeStruct((2 * hr, n), F32),
        grid_spec=pltpu.PrefetchScalarGridSpec(
            num_scalar_prefetch=1, grid=(nb,),
            in_specs=[pl.BlockSpec((1, tr, n), functools.partial(lambda i, c_ref, k: (k, i, 0), k=k))
                      for k in range(n_sh)],
            out_specs=pl.BlockSpec((tr, n), lambda i, c_ref: (c_ref[0] * nb + i, 0))),
        compiler_params=_cp("parallel"),
    )(c_idx, b, b, b, b)


SEM = pl.BlockSpec(memory_space=pltpu.SEMAPHORE)
ANY = pl.BlockSpec(memory_space=pl.ANY)
DATAFLOW = pltpu.SideEffectType.DATAFLOW_SIDE_EFFECTING


def _remote(src, dst, send_sems, recv_sems, k, to):
    return pltpu.make_async_remote_copy(src_ref=src, dst_ref=dst, send_sem=send_sems.at[k], recv_sem=recv_sems.at[k],
                                        device_id=to, device_id_type=MESH)


def _split_copy(name, src, land_shape, land_dtype, n, plan, after=None):
    after = jnp.zeros((8, LANES), F32) if after is None else after

    def start_body(src_ref, land_ref, after_ref, send_sems, recv_sems, src_thru, land_thru, token):
        for cp in plan(src_ref, land_ref, send_sems, recv_sems)[0]:
            cp.start()
        token[...] = jnp.zeros_like(token)

    sems = pltpu.SemaphoreType.DMA((n,))
    send_sems, recv_sems, src_thru, land_thru, token = pl.pallas_call(
        start_body, name=name + "_start",
        out_shape=(sems, sems, pltpu.HBM(src.shape, src.dtype), pltpu.HBM(land_shape, land_dtype),
                   jax.ShapeDtypeStruct((8, LANES), F32)),
        in_specs=(HBM, HBM, ANY), out_specs=(SEM, SEM, HBM, HBM, pl.BlockSpec(memory_space=pltpu.VMEM)),
        input_output_aliases={0: 2, 1: 3}, compiler_params=pltpu.CompilerParams(has_side_effects=DATAFLOW),
    )(pltpu.with_memory_space_constraint(src, pltpu.HBM),
      pltpu.with_memory_space_constraint(lax.empty(land_shape, land_dtype), pltpu.HBM), after)

    def wait(*after):
        def wait_body(src_ref, land_ref, send_sems, recv_sems, *rest):
            sent, received = plan(src_ref, land_ref, send_sems, recv_sems)
            for cp in sent:
                cp.wait_send()
            for cp in received:
                cp.wait_recv()

        return pl.pallas_call(
            wait_body, name=name + "_wait",
            out_shape=(pltpu.HBM(src.shape, src.dtype), pltpu.HBM(land_shape, land_dtype)),
            in_specs=(HBM, HBM, SEM, SEM) + (ANY,) * len(after), out_specs=(HBM, HBM),
            input_output_aliases={0: 0, 1: 1}, compiler_params=pltpu.CompilerParams(has_side_effects=DATAFLOW),
        )(src_thru, land_thru, send_sems, recv_sems, *after)

    return token, wait


def _split_gather(name, shards, after):
    k, n, plan = len(shards), 3 * len(shards), _plan_gather

    def start_body(*refs):
        for cp in plan(refs[:k], refs[k:2 * k], refs[2 * k + 1], refs[2 * k + 2])[0]:
            cp.start()
        refs[-1][...] = jnp.zeros_like(refs[-1])

    sems = pltpu.SemaphoreType.DMA((n,))
    bufs = [pltpu.HBM(s.shape, s.dtype) for s in shards] + [pltpu.HBM((N_CHIP,) + s.shape, s.dtype) for s in shards]
    hbm = lambda t: pltpu.with_memory_space_constraint(t, pltpu.HBM)
    outs = pl.pallas_call(
        start_body, name=name + "_start", out_shape=(sems, sems, *bufs, jax.ShapeDtypeStruct((8, LANES), F32)),
        in_specs=(HBM,) * (2 * k) + (ANY,),
        out_specs=(SEM, SEM) + (HBM,) * (2 * k) + (pl.BlockSpec(memory_space=pltpu.VMEM),),
        input_output_aliases={i: 2 + i for i in range(2 * k)},
        compiler_params=pltpu.CompilerParams(has_side_effects=DATAFLOW),
    )(*[hbm(s) for s in shards], *[hbm(lax.empty((N_CHIP,) + s.shape, s.dtype)) for s in shards], after)
    send_sems, recv_sems, thru, token = outs[0], outs[1], outs[2:2 + 2 * k], outs[-1]

    def wait(*after):
        def wait_body(*refs):
            sent, received = plan(refs[:k], refs[k:2 * k], refs[2 * k], refs[2 * k + 1])
            for cp in sent:
                cp.wait_send()
            for cp in received:
                cp.wait_recv()

        res = pl.pallas_call(
            wait_body, name=name + "_wait", out_shape=tuple(bufs),
            in_specs=(HBM,) * (2 * k) + (SEM, SEM) + (ANY,) * len(after), out_specs=(HBM,) * (2 * k),
            input_output_aliases={i: i for i in range(2 * k)},
            compiler_params=pltpu.CompilerParams(has_side_effects=DATAFLOW),
        )(*thru, send_sems, recv_sems, *after)
        return res[:k], res[k:]

    return token, wait


def _behind(x, token):
    return x + token[0, 0]


def _plan_gather(src_refs, land_refs, send_sems, recv_sems):
    x, y, c, chips = _place()
    pairs = list(enumerate(zip(src_refs, land_refs)))
    sent = [_remote(s, l.at[2 * x + y], send_sems, recv_sems, 3 * i + j, (px, py, c))
            for i, (s, l) in pairs for j, (px, py) in enumerate(chips)]
    received = [_remote(s, l.at[2 * px + py], send_sems, recv_sems, 3 * i + j, (px, py, c))
                for i, (s, l) in pairs for j, (px, py) in enumerate(chips)]
    return sent, received


def _plan_swap(src_ref, land_ref, send_sems, recv_sems):
    x, y, c, _ = _place()
    _, other_half = _halves(c, src_ref.shape[1], 8)
    cp = _remote(src_ref.at[pl.ds(0, src_ref.shape[0]), other_half], land_ref, send_sems, recv_sems, 0, (x, y, 1 - c))
    return [cp], [cp]


def _plan_scatter(src_ref, land_ref, send_sems, recv_sems):
    x, y, c, chips = _place()
    sent = [_remote(src_ref.at[2 * px + py], land_ref.at[2 * x + y], send_sems, recv_sems, j, (px, py, c))
            for j, (px, py) in enumerate(chips)]
    received = [_remote(src_ref.at[2 * px + py], land_ref.at[2 * px + py], send_sems, recv_sems, j, (px, py, c))
                for j, (px, py) in enumerate(chips)]
    return sent, received


def _plan_share(src_ref, land_ref, send_sems, recv_sems):
    x, y, c, _ = _place()
    mine_half, other_half = _halves(c, src_ref.shape[0], 8)
    return ([_remote(src_ref.at[mine_half], src_ref.at[mine_half], send_sems, recv_sems, 0, (x, y, 1 - c))],
            [_remote(src_ref.at[other_half], src_ref.at[other_half], send_sems, recv_sems, 0, (x, y, 1 - c))])


class _GatherBehind:
    def __init__(self, name, shards, chip, after):
        self.chip = chip
        self.token, self.wait = _split_gather(name, shards, after)

    def result(self, *after):
        shards, lands = self.wait(*after)
        return [lax.dynamic_update_slice(land, shard[None], (self.chip, 0, 0)) for shard, land in zip(shards, lands)]


class _ReduceBehind:
    def __init__(self, name, chip, c_idx):
        self.name, self.chip, self.c_idx = name, chip, c_idx

    def start_slab(self, g):
        n_sh, rows, n = g.shape
        token, self.wait = _split_copy(self.name + "_swap", g, (n_sh, rows // 2, n), g.dtype, 1, _plan_swap)
        return token

    def pair(self, after):
        g, a = self.wait(after)
        h = _add_half(self.name + "_pair", g, a, self.c_idx)
        token, self.wait = _split_copy(self.name + "_scatter", h, h.shape, h.dtype, 3, _plan_scatter)
        return token

    def total(self, after):
        h, b = self.wait(after)
        b = lax.dynamic_update_slice(b, lax.dynamic_slice_in_dim(h, self.chip, 1, axis=0), (self.chip, 0, 0))
        f = _sum_chips(self.name + "_sum", b, self.c_idx)
        token, self.wait = _split_copy(self.name + "_share", f, (8, LANES), f.dtype, 1, _plan_share)
        return token

    def result(self, after):
        return self.wait(after)[0]


class _ReduceColsBehind(_ReduceBehind):
    def start(self, g_padded):
        g = _unpack_w_in_grad(g_padded)
        n = g.shape[1] // N_CHIP
        return self.start_slab(jnp.stack([g[:, k * n:(k + 1) * n] for k in range(N_CHIP)]))


def _f_adamw(w, g, m, v):
    m = ADAM_B1 * m + (1.0 - ADAM_B1) * g
    v = ADAM_B2 * v + (1.0 - ADAM_B2) * (g * g)
    m_hat = m / (1.0 - ADAM_B1 ** ADAM_STEP)
    v_hat = v / (1.0 - ADAM_B2 ** ADAM_STEP)
    return -ADAM_LR * (m_hat / (jnp.sqrt(v_hat) + ADAM_EPS) + ADAM_WD * w), m, v


def _adamw(name, w, g, m, v):
    rows, n = w.shape
    return _rowwise(name, lambda w, g, m, v: (_f_adamw(w, g, m, v), ()), rows, [(t, n, 0) for t in (w, g, m, v)], [],
                    [(n, F32)] * 3, [], tm=_row_tile(rows, 8, 256))


def _adamw_many(name, ws, gs, ms, vs):
    k = len(ws)

    def body(*refs):
        ins, outs = refs[:4 * k], refs[4 * k:]
        for i in range(k):
            res = _f_adamw(ins[i][...], ins[k + i][...], ins[2 * k + i][...], ins[3 * k + i][...])
            for j in range(3):
                outs[j * k + i][...] = res[j]

    out = pl.pallas_call(body, name=name, out_shape=[jax.ShapeDtypeStruct(w.shape, F32) for w in ws] * 3)(
        *ws, *gs, *ms, *vs)
    return out[:k], out[k:2 * k], out[2 * k:]


def _pack_rows(parts):
    rows = []
    for t in parts:
        t = t.reshape(-1)
        rows.append(jnp.pad(t, (0, -t.shape[0] % LANES)).reshape(-1, LANES))
    out = jnp.concatenate(rows, axis=0)
    return jnp.pad(out, ((0, -out.shape[0] % 8), (0, 0)))


def _unpack_rows(packed, shapes):
    out, r = [], 0
    for shp in shapes:
        n = int(np.prod(shp))
        nr = -(-n // LANES)
        out.append(packed[r:r + nr].reshape(-1)[:n].reshape(shp))
        r += nr
    return out


def _sum_blocks(name, g):
    def body(g_ref, o_ref):
        acc = g_ref[0]
        for k in range(1, g.shape[0]):
            acc = acc + g_ref[k]
        o_ref[...] = acc

    return pl.pallas_call(body, name=name, out_shape=jax.ShapeDtypeStruct(g.shape[1:], F32))(g)


def _silu(t):
    return t * _sigmoid(t)


def _ada_fwd(cc, w_ada):
    n = w_ada.shape[1]
    tn = _row_tile(n, LANES, 512)

    def body(cc_ref, w_ref, o_ref):
        o_ref[...] = _nn(_silu(cc_ref[...]), w_ref[...])

    return pl.pallas_call(
        body, name="ada_fwd", grid=(n // tn,), out_shape=jax.ShapeDtypeStruct((cc.shape[0], n), F32),
        in_specs=[pl.BlockSpec(cc.shape, lambda j: (0, 0)), pl.BlockSpec((w_ada.shape[0], tn), lambda j: (0, j))],
        out_specs=pl.BlockSpec((cc.shape[0], tn), lambda j: (0, j)), compiler_params=_cp("parallel"),
    )(cc, w_ada)


def _ada_bwd(cc, dm, w_ada):
    d, n = w_ada.shape
    tn = _row_tile(n, LANES, 512)

    def body(cc_ref, dm_ref, w_ref, gw_ref, ds_ref):
        @pl.when(pl.program_id(0) == 0)
        def _():
            ds_ref[...] = jnp.zeros_like(ds_ref)

        gw_ref[...] = _raw_dot("tn", _silu(cc_ref[...]), dm_ref[...], True)
        ds_ref[...] += _raw_dot("nt", dm_ref[...], w_ref[...], False)

    return pl.pallas_call(
        body, name="ada_bwd", grid=(n // tn,),
        out_shape=[jax.ShapeDtypeStruct((d, n), F32), jax.ShapeDtypeStruct(cc.shape, F32)],
        in_specs=[pl.BlockSpec(cc.shape, lambda j: (0, 0)), pl.BlockSpec((cc.shape[0], tn), lambda j: (0, j)),
                  pl.BlockSpec((d, tn), lambda j: (0, j))],
        out_specs=[pl.BlockSpec((d, tn), lambda j: (0, j)), pl.BlockSpec(cc.shape, lambda j: (0, 0))],
        compiler_params=_cp("arbitrary"),
    )(cc, dm, w_ada)


def _c_ctx_grad(parts, c_ctx):
    def body(p_ref, c_ref, o_ref):
        ds = ((p_ref[0] + p_ref[1]) + p_ref[2]) + p_ref[3]
        _, vjp = jax.vjp(_silu, c_ref[...])
        o_ref[...] = vjp(ds)[0]

    return pl.pallas_call(body, name="c_ctx_grad", out_shape=jax.ShapeDtypeStruct(c_ctx.shape, F32))(parts, c_ctx)


def kernel(x, c, ctx, c_ctx, w_ada, b_ada, g_pre_mix, g_post_mix, g_pre_ffn, g_post_ffn, w_in, attn_sink, w_gate_fwd, b_gate_fwd, w_gate_bwd, b_gate_bwd, g_gla_norm, w_out, w_ffn_in, w_ffn_out, loss_target, m_c_ctx, m_w_ada, m_b_ada, m_g_pre_mix, m_g_post_mix, m_g_pre_ffn, m_g_post_ffn, m_w_in, m_attn_sink, m_w_gate_fwd, m_b_gate_fwd, m_w_gate_bwd, m_b_gate_bwd, m_g_gla_norm, m_w_out, m_w_ffn_in, m_w_ffn_out, v_c_ctx, v_w_ada, v_b_ada, v_g_pre_mix, v_g_post_mix, v_g_pre_ffn, v_g_post_ffn, v_w_in, v_attn_sink, v_w_gate_fwd, v_b_gate_fwd, v_w_gate_bwd, v_b_gate_bwd, v_g_gla_norm, v_w_out, v_w_ffn_in, v_w_ffn_out):
    xi, yi, ci = lax.axis_index("x"), lax.axis_index("y"), lax.axis_index("c")
    dev, chip = 4 * xi + 2 * yi + ci, 2 * xi + yi
    c_idx = jnp.reshape(ci, (1,)).astype(jnp.int32)
    d = x.shape[-1]
    n_ada, n_in, n_f = w_ada.shape[-1], w_in.shape[-1], w_ffn_in.shape[-1]
    r_out, r_f = w_out.shape[1], w_ffn_out.shape[1]
    n_gate = w_gate_fwd.shape[-1]
    by_chip = lambda t: t[0::2]

    rc = -(-d // LANES)
    g1 = _ag_small("gather_cond", _pack_rows([c[0], w_gate_fwd[0], w_gate_bwd[0]]))
    c_all = g1[:, :rc].reshape(N_DEV, -1)[:, :d]
    gr = GATE_RANK * n_gate // LANES
    gate_full = lambda off: jnp.transpose(by_chip(g1)[:, off:off + gr].reshape(N_CHIP, GATE_RANK, n_gate),
                                          (1, 0, 2)).reshape(GATE_RANK, N_CHIP * n_gate)
    wgf, wgb = gate_full(rc), gate_full(rc + gr)
    cc = jnp.concatenate([c_all, c_ctx[None, :], jnp.zeros((7, d), F32)], axis=0)

    g2 = _ag_small("gather_ada", _ada_fwd(cc, w_ada[0]).reshape(-1, LANES))
    ada_all = jnp.transpose(by_chip(g2).reshape(N_CHIP, 16, n_ada), (1, 0, 2)).reshape(16, N_CHIP * n_ada) + b_ada
    first = _GatherBehind("gather_w_in", [w_in[0].astype(BF16)], chip, g2)
    late_shards = [w_out[0].astype(BF16), jnp.transpose(w_ffn_in[0]).astype(BF16), w_ffn_out[0].astype(BF16)]
    late = []

    def first_weights(*after):
        w_in_g, = first.result(*after, *late_shards)
        late.append(_GatherBehind("gather_late", late_shards, chip, w_in_g))
        return _pack_w_in(jnp.concatenate([w_in_g[k] for k in range(N_CHIP)], axis=1)), late[0].token

    def late_weights(after):
        return [t.reshape(-1, d) for t in late[0].result(after)]

    ada_all = _behind(ada_all, first.token)
    ada = lax.dynamic_slice(ada_all, (dev, 0), (1, N_CHIP * n_ada))
    ada_c = ada_all[N_DEV:N_DEV + 1]

    w = _prep_gate_weights(wgf, wgb)
    w.update(w_in=first_weights, g_pre_mix=g_pre_mix, g_post_mix=g_post_mix, g_pre_ffn=g_pre_ffn, g_post_ffn=g_post_ffn,
             attn_sink=attn_sink, b_gate_fwd=b_gate_fwd, b_gate_bwd=b_gate_bwd, g_gla_norm=g_gla_norm)

    reduce_behind = _ReduceBehind("reduce_late", chip, c_idx)
    reduce_w_in = _ReduceColsBehind("reduce_w_in", chip, c_idx)
    loss_lanes, grad_x, g, d_ada, d_ada_c = _local_step(x[0], ctx[0], loss_target[0], ada, ada_c, w, late_weights,
                                                        reduce_behind, reduce_w_in)

    small = ("g_pre_mix", "g_post_mix", "g_pre_ffn", "g_post_ffn", "attn_sink", "b_gate_fwd", "b_gate_bwd",
             "g_gla_norm", "w_gate_fwd", "w_gate_bwd")
    shapes = [(1, 6 * d)] * 2 + [g[n].shape for n in small] + [(1, LANES)]
    g3 = _ag_small("gather_small_grads", _pack_rows([d_ada, d_ada_c] + [g[n] for n in small] + [loss_lanes]))
    tot = dict(zip(("d_ada", "d_ada_c") + small + ("loss",),
                   _unpack_rows(_sum_blocks("sum_small_grads", g3), shapes)))
    r_ada = 6 * d // LANES
    dm = jnp.concatenate([g3[:, :r_ada].reshape(N_DEV, 6 * d), tot["d_ada_c"], jnp.zeros((7, 6 * d), F32)], axis=0)
    grads = {n: tot[n] for n in small[:8]}
    grads["b_ada"] = _sum_blocks("sum_b_ada", dm.reshape(16, r_ada, LANES)).reshape(1, 6 * d)
    grads["w_gate_fwd"] = lax.dynamic_slice(tot["w_gate_fwd"], (0, chip * n_gate), (GATE_RANK, n_gate))[None]
    grads["w_gate_bwd"] = lax.dynamic_slice(tot["w_gate_bwd"], (0, chip * n_gate), (GATE_RANK, n_gate))[None]
    gw_ada, dsc = _ada_bwd(cc, lax.dynamic_slice(dm, (0, chip * n_ada), (16, n_ada)), w_ada[0])
    grads["w_ada"] = gw_ada[None]
    g4 = _ag_small("gather_c_ctx", _pack_rows([dsc[N_DEV]]))
    grads["c_ctx"] = _c_ctx_grad(by_chip(g4), _pack_rows([c_ctx])).reshape(-1)[:d]

    grads["w_in"] = reduce_w_in.result(g4)[None]
    part = lambda n: g["late"][g["late_at"][n]:g["late_at"][n] + g["late_rows"][n]]
    grads["w_ffn_in"], grads["w_ffn_out"], grads["w_out"] = (jnp.transpose(part("w_ffn_in_t"))[None],
                                                            part("w_ffn_out")[None], part("w_out")[None])

    names = ("c_ctx", "w_ada", "b_ada", "g_pre_mix", "g_post_mix", "g_pre_ffn", "g_post_ffn", "w_in", "attn_sink",
             "w_gate_fwd", "b_gate_fwd", "w_gate_bwd", "b_gate_bwd", "g_gla_norm", "w_out", "w_ffn_in", "w_ffn_out")
    weights = dict(zip(names, (c_ctx, w_ada, b_ada, g_pre_mix, g_post_mix, g_pre_ffn, g_post_ffn, w_in, attn_sink,
                               w_gate_fwd, b_gate_fwd, w_gate_bwd, b_gate_bwd, g_gla_norm, w_out, w_ffn_in,
                               w_ffn_out)))
    m_in = dict(zip(names, (m_c_ctx, m_w_ada, m_b_ada, m_g_pre_mix, m_g_post_mix, m_g_pre_ffn, m_g_post_ffn, m_w_in,
                            m_attn_sink, m_w_gate_fwd, m_b_gate_fwd, m_w_gate_bwd, m_b_gate_bwd, m_g_gla_norm,
                            m_w_out, m_w_ffn_in, m_w_ffn_out)))
    v_in = dict(zip(names, (v_c_ctx, v_w_ada, v_b_ada, v_g_pre_mix, v_g_post_mix, v_g_pre_ffn, v_g_post_ffn, v_w_in,
                            v_attn_sink, v_w_gate_fwd, v_b_gate_fwd, v_w_gate_bwd, v_b_gate_bwd, v_g_gla_norm,
                            v_w_out, v_w_ffn_in, v_w_ffn_out)))
    large = ("w_ada", "w_in", "w_out", "w_ffn_in", "w_ffn_out")
    tiny = tuple(n for n in names if n not in large)
    delta, new_m, new_v = {}, {}, {}
    for n in large:
        dl, nm, nv = _adamw("adamw_" + n, weights[n][0], grads[n][0], m_in[n][0], v_in[n][0])
        delta[n], new_m[n], new_v[n] = dl[None], nm[None], nv[None]
    for n in tiny:
        grads[n] = grads[n].reshape(weights[n].shape)
    as_rows = lambda t: t.reshape(-1, t.shape[-1])
    res = _adamw_many("adamw_small", *[[as_rows(t[n]) for n in tiny] for t in (weights, grads, m_in, v_in)])
    for out, vals in zip((delta, new_m, new_v), res):
        out.update({n: val.reshape(weights[n].shape) for n, val in zip(tiny, vals)})

    return (tot["loss"][0, 0], grad_x[None], *[grads[n] for n in names], *[delta[n] for n in names], *[new_m[n] for n in names],
            *[new_v[n] for n in names])
```

```python
import functools

import jax
import jax.numpy as jnp
import numpy as np
from jax import lax
from jax.experimental import pallas as pl
from jax.experimental.pallas import tpu as pltpu

F32 = jnp.float32
BF16 = jnp.bfloat16
MESH = pl.DeviceIdType.MESH

HEAD_DIM = 64
ATT_HEADS = 8
ATT_KV_HEADS = 2
ATT_GROUP = ATT_HEADS // ATT_KV_HEADS
WINDOW = 128
BLOCK = 128
GRID_W = 64
ROPE_BASE = 10000.0
GLA_HEADS = 8
GLA_DK = 32
GLA_DV = 64
GLA_CHUNK = 64
GATE_RANK = 16
GATE_TAU = 16.0
NEG_INF = -1e30
QW = ATT_HEADS * HEAD_DIM
KVW = ATT_KV_HEADS * HEAD_DIM
GKW = GLA_HEADS * GLA_DK
GVW = GLA_HEADS * GLA_DV
IN_COLS = QW + 2 * KVW + 2 * GKW + 2 * GVW + 2 * GATE_RANK
LANES = 128
IN_PAD = IN_COLS + LANES - 2 * GATE_RANK
C_Q, C_GV, C_GG = 0, QW, QW + GVW
C_K = C_GG + GVW
C_V = C_K + KVW
C_GQ = C_V + KVW
C_GK = C_GQ + GKW
C_Z = C_GK + GKW
MIX = QW + GVW

ADAM_LR, ADAM_B1, ADAM_B2, ADAM_EPS, ADAM_WD, ADAM_STEP = 0.001, 0.9, 0.999, 1e-08, 0.01, 10

VMEM_LIMIT = 56 * 1024 * 1024


def _cp(*sem):
    return pltpu.CompilerParams(dimension_semantics=sem, vmem_limit_bytes=VMEM_LIMIT)


def _pick(n, cands):
    for t in cands:
        if n % t == 0:
            return t
    return n


_DIMS = {"nn": (((1,), (0,)), ((), ())), "nt": (((1,), (1,)), ((), ())), "tn": (((0,), (0,)), ((), ()))}


def _raw_dot(mode, a, b, hi):
    dot = lambda u, v: lax.dot_general(u, v, _DIMS[mode], preferred_element_type=F32)
    if not hi:
        return dot(a.astype(BF16), b.astype(BF16))
    a, b = a.astype(F32), b.astype(F32)
    a_hi, b_hi = a.astype(BF16), b.astype(BF16)
    out = dot(a_hi, b_hi)
    if hi != "a":
        out = out + dot((a - a_hi.astype(F32)).astype(BF16), b_hi)
    if hi != "b":
        out = out + dot(a_hi, (b - b_hi.astype(F32)).astype(BF16))
    return out


def _make_dot(mode, hi):
    @jax.custom_vjp
    def dot(a, b):
        return _raw_dot(mode, a, b, hi)

    def fwd(a, b):
        return _raw_dot(mode, a, b, hi), (a, b)

    def bwd(res, dc):
        a, b = res
        if mode == "nn":
            return (_raw_dot("nt", dc, b, "b" if hi == "b" else bool(hi)),
                    _raw_dot("tn", a, dc, "a" if hi == "a" else bool(hi)))
        if mode == "nt":
            return _raw_dot("nn", dc, b, bool(hi)), _raw_dot("tn", dc, a, bool(hi))
        return _raw_dot("nt", b, dc, bool(hi)), _raw_dot("nn", a, dc, bool(hi))

    dot.defvjp(fwd, bwd)
    return dot


_nn, _nt, _tn = _make_dot("nn", False), _make_dot("nt", False), _make_dot("tn", False)
_nn_mask, _nn_by_exact = _make_dot("nn", "a"), _make_dot("nn", "b")


MM_VMEM_BUDGET = 44 * 1024 * 1024


def _halvings(n):
    out = [n]
    while out[-1] % (2 * LANES) == 0:
        out.append(out[-1] // 2)
    return out


def _mm_tiles(mode, m, n, k, a_bytes, b_bytes, o_bytes, init_bytes=0):
    tms = [t for t in dict.fromkeys((m, m // 2, m // 4, 2048, 1024, 512, 256, 128))
           if m % t == 0 and t % (LANES if mode == "tn" else 16) == 0 and t <= 4096] or [m]
    if mode == "tn":
        fits = [(k // tk + 0.5 * (m // tm), tm, tk)
                for tk in (4096, 2048, 1024, 512, 256, 128) if k % tk == 0 for tm in tms
                if 2 * (tk * tm * a_bytes + tk * n * b_bytes + tm * n * (o_bytes + init_bytes)) <= MM_VMEM_BUDGET]
        if fits:
            _, tm, tk = min(fits)
            return tm, n, tk
    tks = ([t for t in (512, 256, 128) if k % t == 0] or [k]) if mode == "tn" else _halvings(k)
    for tn in _halvings(n):
        for tk in tks:
            for tm in tms:
                acc = tm * tn * 4 if (k // tk > 1 and o_bytes != 4) else 0
                tiles = tm * tk * a_bytes + tk * tn * b_bytes + tm * tn * (o_bytes + init_bytes)
                if 2 * tiles + acc <= MM_VMEM_BUDGET:
                    return tm, tn, tk
    return tms[-1], _halvings(n)[-1], tks[-1]


def _mm(name, a, b, mode, out_dtype=F32, init=None, after=None):
    follow = () if after is None else (after,)
    if mode == "nn":
        (m, k), n = a.shape, b.shape[1]
    elif mode == "nt":
        (m, k), n = a.shape, b.shape[0]
    else:
        (k, m), n = a.shape, b.shape[1]
    tm, tn, tk = _mm_tiles(mode, m, n, k, a.dtype.itemsize, b.dtype.itemsize, jnp.dtype(out_dtype).itemsize,
                           0 if init is None else 4)
    nk = k // tk
    use_acc = nk > 1 and out_dtype != F32

    inits = () if init is None else (init,)

    def body(a_ref, b_ref, *rest):
        rest = rest[:len(inits)] + rest[len(inits) + len(follow):]
        o_ref, acc = rest[len(inits)], rest[len(inits) + 1:]
        part = _raw_dot(mode, a_ref[...], b_ref[...], False)
        first = lambda: part + rest[0][...] if inits else part
        if nk == 1:
            o_ref[...] = first().astype(o_ref.dtype)
            return
        acc_ref = acc[0] if use_acc else o_ref
        kk = pl.program_id(2)

        @pl.when(kk == 0)
        def _():
            acc_ref[...] = first()

        @pl.when(kk > 0)
        def _():
            acc_ref[...] += part

        if use_acc:
            @pl.when(kk == nk - 1)
            def _():
                o_ref[...] = acc_ref[...].astype(o_ref.dtype)

    if mode == "nn":
        a_spec = pl.BlockSpec((tm, tk), lambda i, j, kk: (i, kk))
        b_spec = pl.BlockSpec((tk, tn), lambda i, j, kk: (kk, j))
    elif mode == "nt":
        a_spec = pl.BlockSpec((tm, tk), lambda i, j, kk: (i, kk))
        b_spec = pl.BlockSpec((tn, tk), lambda i, j, kk: (j, kk))
    else:
        a_spec = pl.BlockSpec((tk, tm), lambda i, j, kk: (kk, i))
        b_spec = pl.BlockSpec((tk, tn), lambda i, j, kk: (kk, j))
    return pl.pallas_call(
        body, name=name, grid=(m // tm, n // tn, nk),
        in_specs=[a_spec, b_spec] + [pl.BlockSpec((tm, tn), lambda i, j, kk: (i, j))] * len(inits)
        + [pl.BlockSpec(memory_space=pl.ANY)] * len(follow),
        out_specs=pl.BlockSpec((tm, tn), lambda i, j, kk: (i, j)),
        out_shape=jax.ShapeDtypeStruct((m, n), out_dtype),
        scratch_shapes=[pltpu.VMEM((tm, tn), F32)] if use_acc else [],
        compiler_params=_cp("parallel", "parallel", "arbitrary"),
    )(a, b, *inits, *follow)


def _slab_layout(rows):
    offsets, at = [], 0
    for r in rows:
        at = -(-at // r) * r
        offsets.append(at)
        at += r
    return offsets, -(-at // 32) * 32


def _slab_zero_gaps(name, shape, rows, offsets):
    gaps = [(o + r, nxt) for o, r, nxt in zip(offsets, rows, offsets[1:] + [shape[1]]) if nxt > o + r]
    slab = None
    for i, (lo, hi) in enumerate(gaps):
        step = int(np.gcd(lo, hi - lo))

        def body(*refs):
            refs[-1][...] = jnp.zeros_like(refs[-1])

        slab = pl.pallas_call(
            body, name=f"{name}_{i}", grid=(shape[0], (hi - lo) // step), out_shape=jax.ShapeDtypeStruct(shape, F32),
            in_specs=[] if slab is None else [pl.BlockSpec(memory_space=pl.ANY)],
            out_specs=pl.BlockSpec((1, step, shape[2]), functools.partial(lambda k, j, b: (k, b + j, 0), b=lo // step)),
            input_output_aliases={} if slab is None else {0: 0}, compiler_params=_cp("parallel", "parallel"),
        )(*(() if slab is None else (slab,)))
    return slab


def _dw_into_slab(name, a, b, slab, shape, at):
    (k, m), n = a.shape, b.shape[1]
    r = m // N_CHIP
    fits = [(k // tk + 0.5 * (m // tm), tm, tk)
            for tk in (4096, 2048, 1024, 512, 256, 128) if k % tk == 0 for tm in (m, m // 2, r) if tm % LANES == 0
            if 2 * (tk * tm * a.dtype.itemsize + tk * n * b.dtype.itemsize + tm * n * 4) <= MM_VMEM_BUDGET]
    _, tm, tk = min(fits)
    per, nk = tm // r, k // tk

    def body(a_ref, b_ref, *rest):
        o_ref = rest[-1]
        part = _raw_dot("tn", a_ref[...], b_ref[...], False).reshape(o_ref.shape)
        if nk == 1:
            o_ref[...] = part
            return
        kk = pl.program_id(1)

        @pl.when(kk == 0)
        def _():
            o_ref[...] = part

        @pl.when(kk > 0)
        def _():
            o_ref[...] += part

    prev = () if slab is None else (slab,)
    return pl.pallas_call(
        body, name=name, grid=(m // tm, nk), out_shape=jax.ShapeDtypeStruct(shape, F32),
        in_specs=[pl.BlockSpec((tk, tm), lambda i, kk: (kk, i)), pl.BlockSpec((tk, n), lambda i, kk: (kk, 0))]
        + [pl.BlockSpec(memory_space=pl.ANY)] * len(prev),
        out_specs=pl.BlockSpec((per, r, n), lambda i, kk: (i, at // r, 0)),
        input_output_aliases={2: 0} if prev else {}, compiler_params=_cp("parallel", "arbitrary"),
    )(a, b, *prev)


def _rowwise(name, fn, rows, row_ins, full_ins, row_outs, acc_outs, tm=None):
    tm = tm or _pick(rows, (512, 256, 128))
    n_r, n_f, n_o, n_a = len(row_ins), len(full_ins), len(row_outs), len(acc_outs)

    def body(*refs):
        ins, outs = refs[:n_r + n_f], refs[n_r + n_f:]
        vals = [r[...].astype(F32) for r in ins]
        ro, ao = fn(*vals)
        for r, val in zip(outs[:n_o], ro):
            r[...] = val.astype(r.dtype)
        if n_a:
            @pl.when(pl.program_id(0) == 0)
            def _():
                for r in outs[n_o:]:
                    r[...] = jnp.zeros_like(r)

            for r, val in zip(outs[n_o:], ao):
                r[...] += val

    in_specs = [pl.BlockSpec((tm, w), functools.partial(lambda i, cb: (i, cb), cb=cb)) for _, w, cb in row_ins]
    in_specs += [pl.BlockSpec(a.shape, lambda i: (0, 0)) for a in full_ins]
    out_specs = [pl.BlockSpec((tm, w), lambda i: (i, 0)) for w, _ in row_outs]
    out_specs += [pl.BlockSpec(s, lambda i: (0, 0)) for s in acc_outs]
    out_shape = [jax.ShapeDtypeStruct((rows, w), dt) for w, dt in row_outs]
    out_shape += [jax.ShapeDtypeStruct(s, F32) for s in acc_outs]
    return pl.pallas_call(
        body, name=name, grid=(rows // tm,), in_specs=in_specs, out_specs=out_specs, out_shape=out_shape,
        compiler_params=_cp("arbitrary" if n_a else "parallel"),
    )(*[a for a, _, _ in row_ins], *full_ins)


def _rn(x):
    return x * lax.rsqrt(jnp.mean(x * x, axis=-1, keepdims=True) + 1e-6)


def _sigmoid(t):
    return 1.0 / (1.0 + jnp.exp(-t))


def _f_norm_mod(x, g, sh, sc):
    return _rn(x) * g * (1.0 + sc) + sh


def _f_post_res(xr, y, g, gate):
    return xr + gate * (_rn(y) * g)


@jax.custom_vjp
def _f_swiglu(g, u):
    return g * _sigmoid(g) * u


def _f_swiglu_fwd(g, u):
    s = _sigmoid(g)
    return g * s * u, (g, u, s)


def _f_swiglu_bwd(res, da):
    g, u, s = res
    gs = g * s
    return da * u * (s + gs * (1.0 - s)), da * gs


_f_swiglu.defvjp(_f_swiglu_fwd, _f_swiglu_bwd)


def _logsig(u):
    return jnp.minimum(u, 0.0) - jnp.log(1.0 + jnp.exp(-jnp.abs(u)))


def _f_gate(z, wf, wb, bf, bb):
    return _logsig(_nn(z, wf) + bf) / GATE_TAU, _logsig(_nn(z, wb) + bb) / GATE_TAU


def _f_gla_out(of, ob, gg, gt, bd):
    o = of + ob
    ms = _nn_by_exact(o * o, bd)
    return o * lax.rsqrt(ms + 1e-6) * gt * (gg * _sigmoid(gg))


def _norm_mod(name, x, g, sh, sc):
    rows, d = x.shape
    return _rowwise(name, lambda x, g, sh, sc: ((_f_norm_mod(x, g, sh, sc),), ()), rows,
                    [(x, d, 0)], [g, sh, sc], [(d, BF16)], [])[0]


def _rn_bwd(x, dn):
    r = lax.rsqrt(jnp.mean(x * x, axis=-1, keepdims=True) + 1e-6)
    n = x * r
    return r * (dn - n * jnp.mean(dn * n, axis=-1, keepdims=True)), n


def _norm_mod_grads(dh, x, g, sc):
    dx, n = _rn_bwd(x, dh * (g * (1.0 + sc)))
    t = jnp.sum(dh * n, axis=0, keepdims=True)
    return dx, (1.0 + sc) * t, jnp.sum(dh, axis=0, keepdims=True), g * t


def _post_res_grads(dout, y, g, gate):
    dy, n = _rn_bwd(y, dout * (gate * g))
    t = jnp.sum(dout * n, axis=0, keepdims=True)
    return dy, gate * t, g * t


def _norm_mod_bwd(name, dh, dres, x, g, sh, sc):
    rows, d = x.shape

    def fn(dh, dres, x, g, sh, sc):
        dx, dg, dsh, dsc = _norm_mod_grads(dh, x, g, sc)
        return (dx + dres,), (dg, dsh, dsc)

    return _rowwise(name, fn, rows, [(dh, d, 0), (dres, d, 0), (x, d, 0)], [g, sh, sc], [(d, F32)],
                    [(1, d)] * 3)


def _post_res_norm_mod(name, xr, y, g_post, gate, g_pre, sh, sc):
    rows, d = xr.shape

    def fn(xr, y, g_post, gate, g_pre, sh, sc):
        x1 = _f_post_res(xr, y, g_post, gate)
        return (x1, _f_norm_mod(x1, g_pre, sh, sc)), ()

    return _rowwise(name, fn, rows, [(xr, d, 0), (y, d, 0)], [g_post, gate, g_pre, sh, sc], [(d, F32), (d, BF16)], [])


def _norm_mod_post_res_bwd(name, dh, dres, x1, y, g_pre, sh, sc, g_post, gate):
    rows, d = x1.shape

    def fn(dh, dres, x1, y, g_pre, sh, sc, g_post, gate):
        dx1, dg_pre, dsh, dsc = _norm_mod_grads(dh, x1, g_pre, sc)
        dx1 = dx1 + dres
        dy, dg_post, dgate = _post_res_grads(dx1, y, g_post, gate)
        return (dx1, dy), (dg_pre, dsh, dsc, dg_post, dgate)

    return _rowwise(name, fn, rows, [(dh, d, 0), (dres, d, 0), (x1, d, 0), (y, d, 0)], [g_pre, sh, sc, g_post, gate],
                    [(d, F32), (d, BF16)], [(1, d)] * 5, tm=_pick(rows, (256, 128)))


def _post_res_loss(name, xr, y, g, gate, target):
    rows, d = xr.shape

    def fn(xr, y, target, g, gate):
        diff = _f_post_res(xr, y, g, gate) - target
        part = 0.5 * jnp.sum(jnp.mean(diff * diff, axis=-1, keepdims=True), axis=0, keepdims=True)
        dx2 = diff * (1.0 / d)
        dy, dg, dgate = _post_res_grads(dx2, y, g, gate)
        return (dx2, dy), (jnp.broadcast_to(part, (1, LANES)), dg, dgate)

    return _rowwise(name, fn, rows, [(xr, d, 0), (y, d, 0), (target, d, 0)], [g, gate], [(d, F32), (d, BF16)],
                    [(1, LANES), (1, d), (1, d)])


def _mm_rows(name, a, b, mode, fn, extras, outs):
    m, k = a.shape
    tm = _pick(m, (256, 128))

    def body(a_ref, b_ref, *rest):
        tiles = fn(_raw_dot(mode, a_ref[...], b_ref[...], False), *[e[...] for e in rest[:len(extras)]])
        for r, val in zip(rest[len(extras):], tiles):
            r[...] = val.astype(r.dtype)

    row = lambda w: pl.BlockSpec((tm, w), lambda i: (i, 0))
    return pl.pallas_call(
        body, name=name, grid=(m // tm,),
        in_specs=[row(k), pl.BlockSpec(b.shape, lambda i: (0, 0))] + [row(e.shape[1]) for e in extras],
        out_specs=[row(w) for w, _ in outs], out_shape=[jax.ShapeDtypeStruct((m, w), dt) for w, dt in outs],
        compiler_params=_cp("parallel"),
    )(a, b, *extras)


def _ffn_in_swiglu(name, h, w_t):
    f = w_t.shape[0] // 2
    fn = lambda u: (u, _f_swiglu(u[:, :f], u[:, f:]))
    return _mm_rows(name, h, w_t, "nt", fn, [], [(2 * f, BF16), (f, BF16)])


def _ffn_out_dx_swiglu_bwd(name, df, w_out, u):
    f = w_out.shape[0]

    def fn(da, u):
        u = u.astype(F32)
        _, vjp = jax.vjp(_f_swiglu, u[:, :f], u[:, f:])
        return (jnp.concatenate(vjp(da), axis=1),)

    return _mm_rows(name, df, w_out, "nt", fn, [u], [(2 * f, BF16)])[0]


def _gate_fwd(name, p, wf, wb, bf, bb):
    rows = p.shape[0]
    return _rowwise(name, lambda z, wf, wb, bf, bb: (_f_gate(z, wf, wb, bf, bb), ()), rows,
                    [(p, LANES, C_Z // LANES)], [wf, wb, bf, bb], [(GKW, F32)] * 2, [])


def _gate_bwd(name, p, dla_f, dla_b, wf, wb, bf, bb):
    rows = p.shape[0]

    def fn(z, dlf, dlb, wf, wb, bf, bb):
        _, vjp = jax.vjp(_f_gate, z, wf, wb, bf, bb)
        dz, dwf, dwb, dbf, dbb = vjp((dlf, dlb))
        return (dz,), (dwf, dwb, dbf, dbb)

    return _rowwise(name, fn, rows, [(p, LANES, C_Z // LANES), (dla_f, GKW, 0), (dla_b, GKW, 0)],
                    [wf, wb, bf, bb], [(LANES, BF16)], [(LANES, GKW), (LANES, GKW), (1, GKW), (1, GKW)])


def _head_mean_matrix():
    h = np.arange(GVW) // GLA_DV
    return jnp.asarray((h[:, None] == h[None, :]).astype(np.float32) / GLA_DV)


def _gla_out(name, attn, of, ob, p, gt):
    rows = of.shape[0]
    bd = _head_mean_matrix()
    fn = lambda attn, of, ob, gg, gt, bd: ((jnp.concatenate([attn, _f_gla_out(of, ob, gg, gt, bd)], axis=1),), ())
    return _rowwise(name, fn, rows, [(attn, QW, 0), (of, GVW, 0), (ob, GVW, 0), (p, GVW, C_GG // GVW)], [gt, bd],
                    [(MIX, BF16)], [])[0]


def _gla_out_bwd(name, dmix, of, ob, p, gt):
    rows = of.shape[0]
    bd = _head_mean_matrix()

    def fn(dm, of, ob, gg, gt, bd):
        _, vjp = jax.vjp(lambda of, gg, gt: _f_gla_out(of, ob, gg, gt, bd), of, gg, gt)
        do, dgg, dgt = vjp(dm)
        return (do, dgg), (dgt,)

    return _rowwise(name, fn, rows, [(dmix, GVW, 1), (of, GVW, 0), (ob, GVW, 0), (p, GVW, C_GG // GVW)], [gt, bd],
                    [(GVW, BF16), (GVW, BF16)], [(1, GVW)])


def _rope_tables(n_tokens):
    t = jnp.arange(n_tokens)
    row = (t // GRID_W).astype(F32)
    col = (t % GRID_W).astype(F32)
    half = HEAD_DIM // 2
    inv_freq = ROPE_BASE ** (-jnp.arange(0, half, 2, dtype=F32) / half)
    ang_r = row[:, None] * inv_freq[None, :]
    ang_c = col[:, None] * inv_freq[None, :]
    ang = jnp.concatenate([ang_r, ang_r, ang_c, ang_c], axis=-1)
    sign = jnp.concatenate([-jnp.ones((16,), F32), jnp.ones((16,), F32)] * 2)
    cos, sin = jnp.cos(ang), jnp.sin(ang) * sign[None, :]
    return jnp.tile(cos, (1, 2)), jnp.tile(sin, (1, 2))


def _rot_pairs(x):
    w = x.shape[-1]
    lane = lax.broadcasted_iota(jnp.int32, x.shape, x.ndim - 1)
    return jnp.where((lane % 32) < 16, pltpu.roll(x, w - 16, x.ndim - 1), pltpu.roll(x, 16, x.ndim - 1))


def _rope_apply(x, cos, sin_signed, inverse):
    reps = x.shape[-1] // LANES
    cos = jnp.concatenate([cos] * reps, axis=-1) if reps > 1 else cos
    sin = jnp.concatenate([sin_signed] * reps, axis=-1) if reps > 1 else sin_signed
    if inverse:
        return x * cos + _rot_pairs(x * sin)
    return x * cos + _rot_pairs(x) * sin


def _rope_fwd(name, p, cos, sin):
    rows = p.shape[0]

    def fn(q, k, v, cos, sin):
        return (_rope_apply(q, cos, sin, False), _rope_apply(k, cos, sin, False), v), ()

    return _rowwise(name, fn, rows, [(p, QW, 0), (p, KVW, C_K // KVW), (p, KVW, C_V // KVW), (cos, LANES, 0),
                                     (sin, LANES, 0)], [], [(QW, BF16), (KVW, BF16), (KVW, BF16)], [])


def _proj_grad(name, dq_rot, dk_rot, dv, cos, sin, gla_f, gla_b, dgg, dz):
    rows = dq_rot.shape[0]

    def fn(dq, dk, dv, cos, sin, gqf, gkf, gvf, gqb, gkb, gvb, dgg, dz):
        parts = [_rope_apply(dq, cos, sin, True), gvf + gvb, dgg, _rope_apply(dk, cos, sin, True), dv, gqf + gqb,
                 gkf + gkb, dz]
        return (jnp.concatenate(parts, axis=1),), ()

    ins = [(dq_rot, QW), (dk_rot, KVW), (dv, KVW), (cos, LANES), (sin, LANES)]
    ins += [(t, t.shape[1]) for t in (*gla_f, *gla_b)] + [(dgg, GVW), (dz, LANES)]
    return _rowwise(name, fn, rows, [(t, w, 0) for t, w in ins], [], [(IN_PAD, BF16)], [],
                    tm=_pick(rows, (256, 128)))[0]


GROUP_ROWS = ATT_GROUP * BLOCK


ATT_SCALE = HEAD_DIM ** -0.5


def _attn_bias(n_tokens):
    nb = n_tokens // BLOCK
    i = (jnp.arange(GROUP_ROWS) % BLOCK)[:, None]
    j = jnp.arange(3 * BLOCK)[None, :]

    def one(n):
        kpos = (n - 1) * BLOCK + j
        return jnp.where((jnp.abs(j - BLOCK - i) <= WINDOW) & (kpos >= 0) & (kpos < n_tokens), 0.0, NEG_INF)

    return jnp.stack([one(0), one(1), one(nb - 1)]).astype(F32)


def _attn_bias_spec(n_tokens):
    nb = n_tokens // BLOCK
    return pl.BlockSpec((1, GROUP_ROWS, 3 * BLOCK), lambda n: (jnp.where(n == 0, 0, jnp.where(n == nb - 1, 2, 1)), 0, 0))


def _attn_setup(sink):
    row = lax.broadcasted_iota(jnp.int32, (GROUP_ROWS, 1), 0)
    group = sum((row >= g * BLOCK).astype(jnp.int32) for g in range(1, ATT_GROUP))
    head_id = lax.broadcasted_iota(jnp.int32, (1, ATT_HEADS), 1)
    sks = []
    for h in range(ATT_KV_HEADS):
        sk = jnp.zeros((GROUP_ROWS, 1), F32)
        for g in range(ATT_GROUP):
            one = jnp.sum(jnp.where(head_id == h * ATT_GROUP + g, sink, 0.0), axis=-1, keepdims=True)
            sk = jnp.where(group == g, one, sk)
        sks.append(sk)
    return group, sks


def _attn_weights(q, kw, kc, sk, bias):
    q = q * ATT_SCALE
    s_w = _raw_dot("nt", q, kw, False) + bias
    s_c = _raw_dot("nt", q, kc, False)
    m = jnp.maximum(jnp.maximum(jnp.max(s_w, axis=-1, keepdims=True), jnp.max(s_c, axis=-1, keepdims=True)), sk)
    pw, pc, ps = jnp.exp(s_w - m), jnp.exp(s_c - m), jnp.exp(sk - m)
    return q, pw, pc, ps, jnp.sum(pw, axis=-1, keepdims=True) + jnp.sum(pc, axis=-1, keepdims=True) + ps


def _f_attn(qs, kws, vws, kcs, vcs, sink, bias):
    _, sks = _attn_setup(sink)
    outs = []
    for h in range(ATT_KV_HEADS):
        _, pw, pc, _, den = _attn_weights(qs[h], kws[h], kcs[h], sks[h], bias)
        outs.append((_raw_dot("nn", pw, vws[h], False) + _raw_dot("nn", pc, vcs[h], False)) / den)
    return tuple(outs)


def _f_attn_bwd(qs, kws, vws, kcs, vcs, sink, bias, outs, douts):
    group, sks = _attn_setup(sink)
    head_id = lax.broadcasted_iota(jnp.int32, (1, ATT_HEADS), 1)
    dot = lambda mode, a, b: _raw_dot(mode, a, b, False)
    dqs, dkws, dvws, dkcs, dvcs, dsink = [], [], [], [], [], jnp.zeros((1, ATT_HEADS), F32)
    for h in range(ATT_KV_HEADS):
        q, pw, pc, ps, den = _attn_weights(qs[h], kws[h], kcs[h], sks[h], bias)
        inv = 1.0 / den
        pw, pc = pw * inv, pc * inv
        dd = jnp.sum(douts[h] * outs[h], axis=-1, keepdims=True)
        dsw = pw * (dot("nt", douts[h], vws[h]) - dd)
        dsc = pc * (dot("nt", douts[h], vcs[h]) - dd)
        dqs.append((dot("nn", dsw, kws[h]) + dot("nn", dsc, kcs[h])) * ATT_SCALE)
        dkws.append(dot("tn", dsw, q))
        dkcs.append(dot("tn", dsc, q))
        dvws.append(dot("tn", pw, douts[h]))
        dvcs.append(dot("tn", pc, douts[h]))
        dsk = -(ps * inv) * dd
        for g in range(ATT_GROUP):
            one = jnp.sum(jnp.where(group == g, dsk, 0.0), axis=0, keepdims=True)
            dsink = dsink + jnp.where(head_id == h * ATT_GROUP + g, one, 0.0)
    return dqs, dkws, dvws, dkcs, dvcs, dsink


def _group_rows(ref, h):
    hs = lambda hq: slice(hq * HEAD_DIM, (hq + 1) * HEAD_DIM)
    return jnp.concatenate([ref[:, hs(h * ATT_GROUP + g)].astype(F32) for g in range(ATT_GROUP)], axis=0)


def _ungroup_rows(ref, h, val):
    for g in range(ATT_GROUP):
        hq = h * ATT_GROUP + g
        ref[:, hq * HEAD_DIM:(hq + 1) * HEAD_DIM] = val[g * BLOCK:(g + 1) * BLOCK].astype(ref.dtype)


def _attn_loads(n, q_ref, kp_ref, vp_ref, kc_ref, vc_ref):
    r0 = pl.multiple_of(n * BLOCK, BLOCK)
    hs = lambda h: slice(h * HEAD_DIM, (h + 1) * HEAD_DIM)
    qs = [_group_rows(q_ref, h) for h in range(ATT_KV_HEADS)]
    kws = [kp_ref[pl.ds(r0, 3 * BLOCK), hs(h)].astype(F32) for h in range(ATT_KV_HEADS)]
    vws = [vp_ref[pl.ds(r0, 3 * BLOCK), hs(h)].astype(F32) for h in range(ATT_KV_HEADS)]
    kcs = [kc_ref[:, hs(h)].astype(F32) for h in range(ATT_KV_HEADS)]
    vcs = [vc_ref[:, hs(h)].astype(F32) for h in range(ATT_KV_HEADS)]
    return r0, hs, qs, kws, vws, kcs, vcs


def _attn_specs(s, c):
    full = lambda shape: pl.BlockSpec(shape, lambda n: (0, 0))
    return [pl.BlockSpec((BLOCK, QW), lambda n: (n, 0)), full((s + 2 * BLOCK, KVW)), full((s + 2 * BLOCK, KVW)),
            full((c, KVW)), full((c, KVW)), full((1, ATT_HEADS)), _attn_bias_spec(s)]


def _attn_fwd(q, kp, vp, kc, vc, sink):
    s, c = q.shape[0], kc.shape[0]

    def body(q_ref, kp_ref, vp_ref, kc_ref, vc_ref, sink_ref, bias_ref, o_ref):
        n = pl.program_id(0)
        _, hs, qs, kws, vws, kcs, vcs = _attn_loads(n, q_ref, kp_ref, vp_ref, kc_ref, vc_ref)
        outs = _f_attn(qs, kws, vws, kcs, vcs, sink_ref[...], bias_ref[0])
        for h in range(ATT_KV_HEADS):
            _ungroup_rows(o_ref, h, outs[h])

    return pl.pallas_call(
        body, name="attn_fwd", grid=(s // BLOCK,), in_specs=_attn_specs(s, c),
        out_specs=pl.BlockSpec((BLOCK, QW), lambda n: (n, 0)), out_shape=jax.ShapeDtypeStruct((s, QW), BF16),
        compiler_params=_cp("parallel"),
    )(q, kp, vp, kc, vc, sink, _attn_bias(s))


def _attn_bwd(do, o, q, kp, vp, kc, vc, sink):
    s, c = q.shape[0], kc.shape[0]

    def body(do_ref, o_ref, q_ref, kp_ref, vp_ref, kc_ref, vc_ref, sink_ref, bias_ref, dq_ref, dkp_ref, dvp_ref,
             dkc_ref, dvc_ref, dsink_ref):
        n = pl.program_id(0)

        @pl.when(n == 0)
        def _():
            for r in (dkp_ref, dvp_ref, dkc_ref, dvc_ref, dsink_ref):
                r[...] = jnp.zeros_like(r)

        r0, hs, qs, kws, vws, kcs, vcs = _attn_loads(n, q_ref, kp_ref, vp_ref, kc_ref, vc_ref)
        heads = range(ATT_KV_HEADS)
        dqs, dkws, dvws, dkcs, dvcs, dsink = _f_attn_bwd(
            qs, kws, vws, kcs, vcs, sink_ref[...], bias_ref[0], [_group_rows(o_ref, h) for h in heads],
            [_group_rows(do_ref, h) for h in heads])
        for h in heads:
            _ungroup_rows(dq_ref, h, dqs[h])
            dkp_ref[pl.ds(r0, 3 * BLOCK), hs(h)] += dkws[h]
            dvp_ref[pl.ds(r0, 3 * BLOCK), hs(h)] += dvws[h]
            dkc_ref[:, hs(h)] += dkcs[h]
            dvc_ref[:, hs(h)] += dvcs[h]
        dsink_ref[...] += dsink

    full = lambda shape: pl.BlockSpec(shape, lambda n: (0, 0))
    return pl.pallas_call(
        body, name="attn_bwd", grid=(s // BLOCK,),
        in_specs=[pl.BlockSpec((BLOCK, QW), lambda n: (n, 0))] * 2 + _attn_specs(s, c),
        out_specs=[pl.BlockSpec((BLOCK, QW), lambda n: (n, 0)), full((s + 2 * BLOCK, KVW)), full((s + 2 * BLOCK, KVW)),
                   full((c, KVW)), full((c, KVW)), full((1, ATT_HEADS))],
        out_shape=[jax.ShapeDtypeStruct((s, QW), BF16), jax.ShapeDtypeStruct((s + 2 * BLOCK, KVW), F32),
                   jax.ShapeDtypeStruct((s + 2 * BLOCK, KVW), F32), jax.ShapeDtypeStruct((c, KVW), F32),
                   jax.ShapeDtypeStruct((c, KVW), F32), jax.ShapeDtypeStruct((1, ATT_HEADS), F32)],
        compiler_params=_cp("arbitrary"),
    )(do, o, q, kp, vp, kc, vc, sink, _attn_bias(s))


GLA_GROUPS = 1
GLA_GROUP_HEADS = GLA_HEADS // GLA_GROUPS
GKG, GVG = GKW // GLA_GROUPS, GVW // GLA_GROUPS


def _gla_masks(heads=GLA_HEADS):
    hk = np.arange(heads * GLA_DK) // GLA_DK
    hv = np.arange(heads * GLA_DV) // GLA_DV
    head_k = (np.arange(heads)[:, None] == hk[None, :]).astype(np.float32)
    head_v = (np.arange(heads)[:, None] == hv[None, :]).astype(np.float32)
    bd_t = (hv[:, None] == hk[None, :]).astype(np.float32)
    return jnp.asarray(head_k), jnp.asarray(head_v), jnp.asarray(bd_t)


def _group_states(st):
    return jnp.stack([st[g * GVG:(g + 1) * GVG, g * GKG:(g + 1) * GKG] for g in range(GLA_GROUPS)])


def _ungroup_states(st):
    out = jnp.zeros((GVW, GKW), st.dtype)
    for g in range(GLA_GROUPS):
        out = out.at[g * GVG:(g + 1) * GVG, g * GKG:(g + 1) * GKG].set(st[g])
    return out


def _tri(n, rev, strict=False):
    i = lax.broadcasted_iota(jnp.int32, (n, n), 0)
    j = lax.broadcasted_iota(jnp.int32, (n, n), 1)
    if strict:
        keep = (j > i) if rev else (j < i)
    else:
        keep = (j >= i) if rev else (j <= i)
    return keep


def _f_gla_chunk(q, k, v, la, st, head_k, head_v, bd_t, rev):
    return _f_gla_carry(*_f_gla_intra(q, k, v, la, head_k, head_v, rev), v, st, bd_t)


def _f_gla_intra(q, k, v, la, head_k, head_v, rev):
    heads, kw, vw = head_k.shape[0], q.shape[1], v.shape[1]
    keep = _tri(GLA_CHUNK, rev)
    b = _nn_mask(keep.astype(F32), la)
    bl = jnp.sum(la, axis=0, keepdims=True)
    qd = q * (GLA_DK ** -0.5) * jnp.exp(b)
    ki = k * jnp.exp(-b)
    kd = k * jnp.exp(bl - b)
    q_heads = (qd[None, :, :] * head_k[:, None, :]).reshape(heads * GLA_CHUNK, kw)
    a_all = _nt(q_heads, ki).reshape(heads, GLA_CHUNK, GLA_CHUNK)
    a_all = jnp.where(keep[None, :, :], a_all, 0.0).reshape(heads * GLA_CHUNK, GLA_CHUNK)
    o_all = _nn(a_all, v).reshape(heads, GLA_CHUNK, vw)
    return jnp.sum(o_all * head_v[:, None, :], axis=0), qd, kd, bl


def _f_gla_carry(intra, qd, kd, bl, v, st, bd_t):
    return intra + _nt(qd, st), st * jnp.exp(bl) + bd_t * _tn(v, kd)


def _gla_specs(s, tb, order):
    return [pl.BlockSpec((tb, GKW), lambda i: (order(i), C_GQ // GKW)),
            pl.BlockSpec((tb, GKW), lambda i: (order(i), C_GK // GKW)),
            pl.BlockSpec((tb, GVW), lambda i: (order(i), C_GV // GVW)),
            pl.BlockSpec((tb, GKW), lambda i: (order(i), 0))]


GLA_BLOCK_CHUNKS = 4


def _gla_fwd(p, la_f, la_b, st_f0, st_b0):
    s = p.shape[0]
    tb = GLA_BLOCK_CHUNKS * GLA_CHUNK
    nblk = s // tb
    up, down = (lambda i: i), (lambda i: nblk - 1 - i)
    masks = _gla_masks(GLA_GROUP_HEADS)

    def scan(rev, q_ref, k_ref, v_ref, la_ref, o_ref, sts_ref, st_ref, consts):
        for g in range(GLA_GROUPS):
            gk, gv = slice(g * GKG, (g + 1) * GKG), slice(g * GVG, (g + 1) * GVG)
            st = st_ref[g]
            sts_ref[0, g] = st
            chunks = range(GLA_BLOCK_CHUNKS)
            for ci in (reversed(chunks) if rev else chunks):
                rows = slice(ci * GLA_CHUNK, (ci + 1) * GLA_CHUNK)
                o, st = _f_gla_chunk(q_ref[rows, gk], k_ref[rows, gk], v_ref[rows, gv], la_ref[rows, gk], st, *consts,
                                     rev)
                o_ref[rows, gv] = o
            st_ref[g] = st

    def body(qf, kf, vf, laf, qb, kb, vb, lab, stf0, stb0, hk_ref, hv_ref, bd_ref, of_ref, stsf_ref, ob_ref, stsb_ref,
             stf_ref, stb_ref):
        @pl.when(pl.program_id(0) == 0)
        def _():
            stf_ref[...] = stf0[...]
            stb_ref[...] = stb0[...]

        consts = (hk_ref[...], hv_ref[...], bd_ref[...])
        scan(False, qf, kf, vf, laf, of_ref, stsf_ref, stf_ref, consts)
        scan(True, qb, kb, vb, lab, ob_ref, stsb_ref, stb_ref, consts)

    full = lambda a: pl.BlockSpec(a.shape, lambda i: (0,) * a.ndim)
    outs = lambda order: [pl.BlockSpec((tb, GVW), lambda i: (order(i), 0)),
                          pl.BlockSpec((1, GLA_GROUPS, GVG, GKG), lambda i: (order(i), 0, 0, 0))]
    return pl.pallas_call(
        body, name="gla_fwd", grid=(nblk,),
        in_specs=_gla_specs(s, tb, up) + _gla_specs(s, tb, down) + [full(st_f0), full(st_b0)]
        + [full(m) for m in masks],
        out_specs=outs(up) + outs(down),
        out_shape=[jax.ShapeDtypeStruct((s, GVW), F32), jax.ShapeDtypeStruct((nblk, GLA_GROUPS, GVG, GKG), F32)] * 2,
        scratch_shapes=[pltpu.VMEM((GLA_GROUPS, GVG, GKG), F32)] * 2,
        compiler_params=_cp("arbitrary"),
    )(p, p, p, la_f, p, p, p, la_b, st_f0, st_b0, *masks)


def _gla_bwd(p, la_f, la_b, sts_f, sts_b, do, after=None):
    s = p.shape[0]
    tb = GLA_BLOCK_CHUNKS * GLA_CHUNK
    nblk = s // tb
    up, down = (lambda i: i), (lambda i: nblk - 1 - i)
    masks = _gla_masks(GLA_GROUP_HEADS)
    follow = () if after is None else (after,)

    def back(rev, q_ref, k_ref, v_ref, la_ref, sts_ref, do_ref, dq_ref, dk_ref, dv_ref, dla_ref, dst0_ref, dst_ref,
             consts):
        def block(q, k, v, la, st):
            outs = [None] * GLA_BLOCK_CHUNKS
            chunks = range(GLA_BLOCK_CHUNKS)
            for ci in (reversed(chunks) if rev else chunks):
                outs[ci], st = _f_gla_chunk(q[ci], k[ci], v[ci], la[ci], st, *consts, rev)
            return tuple(outs), st

        for g in range(GLA_GROUPS):
            gk, gv = slice(g * GKG, (g + 1) * GKG), slice(g * GVG, (g + 1) * GVG)
            split = lambda r, cols: tuple(r[ci * GLA_CHUNK:(ci + 1) * GLA_CHUNK, cols].astype(F32)
                                          for ci in range(GLA_BLOCK_CHUNKS))
            _, vjp = jax.vjp(block, split(q_ref, gk), split(k_ref, gk), split(v_ref, gv), split(la_ref, gk),
                             sts_ref[0, g])
            dq, dk, dv, dla, dst = vjp((split(do_ref, gv), dst_ref[g]))
            for ci in range(GLA_BLOCK_CHUNKS):
                rows = slice(ci * GLA_CHUNK, (ci + 1) * GLA_CHUNK)
                dq_ref[rows, gk], dk_ref[rows, gk] = dq[ci].astype(BF16), dk[ci].astype(BF16)
                dv_ref[rows, gv], dla_ref[rows, gk] = dv[ci].astype(BF16), dla[ci]
            dst_ref[g] = dst
            dst0_ref[g] = dst

    def body(*refs):
        ins, (hk_ref, hv_ref, bd_ref) = refs[:12], refs[12:15]
        outs = refs[15 + len(follow):]

        @pl.when(pl.program_id(0) == 0)
        def _():
            outs[10][...] = jnp.zeros_like(outs[10])
            outs[11][...] = jnp.zeros_like(outs[11])

        consts = (hk_ref[...], hv_ref[...], bd_ref[...])
        back(False, *ins[:6], *outs[:5], outs[10], consts)
        back(True, *ins[6:], *outs[5:10], outs[11], consts)

    full = lambda a: pl.BlockSpec(a.shape, lambda i: (0,) * a.ndim)

    def ins(order):
        return _gla_specs(s, tb, order) + [pl.BlockSpec((1, GLA_GROUPS, GVG, GKG), lambda i: (order(i), 0, 0, 0)),
                                           pl.BlockSpec((tb, GVW), lambda i: (order(i), 0))]

    def outs(order):
        blk = lambda w: pl.BlockSpec((tb, w), lambda i: (order(i), 0))
        return [blk(GKW), blk(GKW), blk(GVW), blk(GKW), pl.BlockSpec((GLA_GROUPS, GVG, GKG), lambda i: (0, 0, 0))]

    shapes = [jax.ShapeDtypeStruct((s, GKW), BF16), jax.ShapeDtypeStruct((s, GKW), BF16),
              jax.ShapeDtypeStruct((s, GVW), BF16), jax.ShapeDtypeStruct((s, GKW), F32),
              jax.ShapeDtypeStruct((GLA_GROUPS, GVG, GKG), F32)]
    both = pl.pallas_call(
        body, name="gla_bwd", grid=(nblk,),
        in_specs=ins(down) + ins(up) + [full(m) for m in masks] + [pl.BlockSpec(memory_space=pl.ANY)] * len(follow),
        out_specs=outs(down) + outs(up), out_shape=shapes * 2,
        scratch_shapes=[pltpu.VMEM((GLA_GROUPS, GVG, GKG), F32)] * 2,
        compiler_params=_cp("arbitrary"),
    )(p, p, p, la_f, sts_f, do, p, p, p, la_b, sts_b, do, *masks, *follow)
    return both[:5], both[5:]


def _f_ctx_state(k, v, la_f, la_b, bd_t):
    c = k.shape[0]
    after = _nn_mask(_tri(c, True, strict=True).astype(F32), la_f)
    before = _nn_mask(_tri(c, False, strict=True).astype(F32), la_b)
    return bd_t * _tn(v, k * jnp.exp(after)), bd_t * _tn(v, k * jnp.exp(before))


def _ctx_state(pc, la_f, la_b):
    c = pc.shape[0]
    bd_t = _gla_masks()[2]

    def body(k_ref, v_ref, lf_ref, lb_ref, bd_ref, sf_ref, sb_ref):
        sf_ref[...], sb_ref[...] = _f_ctx_state(k_ref[...], v_ref[...], lf_ref[...], lb_ref[...], bd_ref[...])

    full = lambda a: pl.BlockSpec(a.shape, lambda i: (0, 0))
    return pl.pallas_call(
        body, name="ctx_state_fwd", grid=(1,),
        in_specs=[pl.BlockSpec((c, GKW), lambda i: (0, C_GK // GKW)), pl.BlockSpec((c, GVW), lambda i: (0, C_GV // GVW)),
                  full(la_f), full(la_b), full(bd_t)],
        out_specs=[pl.BlockSpec((GVW, GKW), lambda i: (0, 0))] * 2,
        out_shape=[jax.ShapeDtypeStruct((GVW, GKW), F32)] * 2,
        compiler_params=_cp("arbitrary"),
    )(pc, pc, la_f, la_b, bd_t)


def _ctx_state_bwd(pc, la_f, la_b, dsf, dsb):
    c = pc.shape[0]
    bd_t = _gla_masks()[2]

    def body(k_ref, v_ref, lf_ref, lb_ref, bd_ref, dsf_ref, dsb_ref, dk_ref, dv_ref, dlf_ref, dlb_ref):
        _, vjp = jax.vjp(lambda k, v, lf, lb: _f_ctx_state(k, v, lf, lb, bd_ref[...]),
                         k_ref[...], v_ref[...], lf_ref[...], lb_ref[...])
        dk, dv, dlf, dlb = vjp((dsf_ref[...], dsb_ref[...]))
        dk_ref[...], dv_ref[...] = dk.astype(BF16), dv.astype(BF16)
        dlf_ref[...], dlb_ref[...] = dlf, dlb

    full = lambda a: pl.BlockSpec(a.shape, lambda i: (0, 0))
    return pl.pallas_call(
        body, name="ctx_state_bwd", grid=(1,),
        in_specs=[pl.BlockSpec((c, GKW), lambda i: (0, C_GK // GKW)), pl.BlockSpec((c, GVW), lambda i: (0, C_GV // GVW)),
                  full(la_f), full(la_b), full(bd_t), full(dsf), full(dsb)],
        out_specs=[pl.BlockSpec((c, GKW), lambda i: (0, 0)), pl.BlockSpec((c, GVW), lambda i: (0, 0)),
                   pl.BlockSpec((c, GKW), lambda i: (0, 0)), pl.BlockSpec((c, GKW), lambda i: (0, 0))],
        out_shape=[jax.ShapeDtypeStruct((c, GKW), BF16), jax.ShapeDtypeStruct((c, GVW), BF16),
                   jax.ShapeDtypeStruct((c, GKW), F32), jax.ShapeDtypeStruct((c, GKW), F32)],
        compiler_params=_cp("arbitrary"),
    )(pc, pc, la_f, la_b, bd_t, dsf, dsb)


_SRC_COLS = ((0, QW), (QW + 2 * KVW + 2 * GKW, GVW), (QW + 2 * KVW + 2 * GKW + GVW, GVW), (QW, KVW), (QW + KVW, KVW),
             (QW + 2 * KVW, GKW), (QW + 2 * KVW + GKW, GKW), (IN_COLS - 2 * GATE_RANK, 2 * GATE_RANK))
_DST_COLS = (C_Q, C_GV, C_GG, C_K, C_V, C_GQ, C_GK, C_Z)


def _pack_w_in(w_in):
    parts = [w_in[:, s:s + n] for s, n in _SRC_COLS]
    parts.append(jnp.zeros((w_in.shape[0], IN_PAD - C_Z - 2 * GATE_RANK), w_in.dtype))
    return jnp.concatenate(parts, axis=1)


def _unpack_w_in_grad(g):
    by_src = sorted(zip(_SRC_COLS, _DST_COLS))
    return jnp.concatenate([g[:, d:d + n] for (_, n), d in by_src], axis=1)


def _prep_gate_weights(w_gate_fwd, w_gate_bwd):
    pad_rows = lambda w, at: jnp.zeros((LANES, GKW), F32).at[at:at + GATE_RANK].set(w)
    return {"wg_f": pad_rows(w_gate_fwd, 0), "wg_b": pad_rows(w_gate_bwd, GATE_RANK)}


def _local_step(x, ctx, target, ada, ada_c, w, late_weights, reduce_behind=None, reduce_w_in=None):
    s, d = x.shape
    sh1, sc1, gt1, sh2, sc2, gt2 = [ada[:, i * d:(i + 1) * d] for i in range(6)]
    sh1c, sc1c = ada_c[:, :d], ada_c[:, d:2 * d]
    cos, sin = _rope_tables(s)
    gt = jnp.tile(w["g_gla_norm"], (1, GLA_HEADS))

    h = _norm_mod("pre_mix", x, w["g_pre_mix"], sh1, sc1)
    hc = _norm_mod("pre_mix_ctx", ctx, w["g_pre_mix"], sh1c, sc1c)
    w_in, token = w["w_in"](h, cos, sin)
    p = _mm("proj_in", h, w_in, "nn", after=token)
    pc = _mm("proj_in_ctx", hc, w_in, "nn")
    q_rot, k_rot, v_b = _rope_fwd("rope", p, cos, sin)
    pad = ((BLOCK, BLOCK), (0, 0))
    kp, vp = jnp.pad(k_rot, pad), jnp.pad(v_b, pad)
    kc, vc = pc[:, C_K:C_K + KVW].astype(BF16), pc[:, C_V:C_V + KVW].astype(BF16)
    attn = _attn_fwd(q_rot, kp, vp, kc, vc, w["attn_sink"])
    gate_w = (w["wg_f"], w["wg_b"], w["b_gate_fwd"], w["b_gate_bwd"])
    la_f, la_b = _gate_fwd("gate", p, *gate_w)
    la_fc, la_bc = _gate_fwd("gate_ctx", pc, *gate_w)
    st_f0, st_b0 = _ctx_state(pc, la_fc, la_bc)
    o_f, sts_f, o_b, sts_b = _gla_fwd(p, la_f, la_b, _group_states(st_f0), _group_states(st_b0))
    mix = _gla_out("gla_out", attn, o_f, o_b, p, gt)
    w_out, w_ffn_in_t, w_ffn_out = late_weights(attn)
    y = _mm("proj_out", mix, w_out, "nn", BF16)
    x1, h2 = _post_res_norm_mod("post_mix_pre_ffn", x, y, w["g_post_mix"], gt1, w["g_pre_ffn"], sh2, sc2)
    u, a = _ffn_in_swiglu("ffn_in", h2, w_ffn_in_t)
    f = _mm("ffn_out", a, w_ffn_out, "nn", BF16)
    g = {}
    dx2, df, loss, g["g_post_ffn"], dgt2 = _post_res_loss("post_ffn_loss", x1, f, w["g_post_ffn"], gt2, target)

    late_rows = {"w_ffn_in_t": w_ffn_in_t.shape[0] // N_CHIP, "w_ffn_out": w_ffn_out.shape[0] // N_CHIP,
                 "w_out": w_out.shape[0] // N_CHIP}
    order = sorted(late_rows, key=lambda n: -late_rows[n])
    offsets, slab_rows = _slab_layout([late_rows[n] for n in order])
    late_at, slab_shape = dict(zip(order, offsets)), (N_CHIP, slab_rows, d)
    slab = _slab_zero_gaps("late_grads_gaps", slab_shape, [late_rows[n] for n in order], offsets)
    slab = _dw_into_slab("ffn_out_dw", a, df, slab, slab_shape, late_at["w_ffn_out"])
    du = _ffn_out_dx_swiglu_bwd("ffn_out_dx", df, w_ffn_out, u)
    dh2 = _mm("ffn_in_dx", du, w_ffn_in_t, "nn", BF16)
    slab = _dw_into_slab("ffn_in_dw", du, h2, slab, slab_shape, late_at["w_ffn_in_t"])
    dx1, dy, g["g_pre_ffn"], dsh2, dsc2, g["g_post_mix"], dgt1 = _norm_mod_post_res_bwd(
        "pre_ffn_post_mix_bwd", dh2, dx2, x1, y, w["g_pre_ffn"], sh2, sc2, w["g_post_mix"], gt1)
    dmix = _mm("proj_out_dx", dy, w_out, "nt", BF16)
    slab = _dw_into_slab("proj_out_dw", mix, dy, slab, slab_shape, late_at["w_out"])
    g["late"], g["late_at"], g["late_rows"] = slab, late_at, late_rows
    rb, sink, token = reduce_behind, w["attn_sink"], None
    if rb is not None:
        gt = _behind(gt, rb.start_slab(slab))
    d_o, dgg, dgt = _gla_out_bwd("gla_out_bwd", dmix, o_f, o_b, p, gt)
    g["g_gla_norm"] = jnp.sum(dgt.reshape(GLA_HEADS, GLA_DV), axis=0, keepdims=True)
    if rb is not None:
        token = rb.pair(dgg)
    gla_f, gla_b = _gla_bwd(p, la_f, la_b, sts_f, sts_b, d_o, token)
    (dla_f, dst_f0), (dla_b, dst_b0) = gla_f[3:], gla_b[3:]
    dst_f0, dst_b0 = _ungroup_states(dst_f0), _ungroup_states(dst_b0)
    if rb is not None:
        sink = _behind(sink, rb.total(dla_b))
    dgkc, dgvc, dla_fc, dla_bc = _ctx_state_bwd(pc, la_fc, la_bc, dst_f0, dst_b0)
    dz, dwf, dwb, dbf, dbb = _gate_bwd("gate_bwd", p, dla_f, dla_b, *gate_w)
    dzc, dwfc, dwbc, dbfc, dbbc = _gate_bwd("gate_ctx_bwd", pc, dla_fc, dla_bc, *gate_w)
    g["w_gate_fwd"] = (dwf + dwfc)[:GATE_RANK]
    g["w_gate_bwd"] = (dwb + dwbc)[GATE_RANK:2 * GATE_RANK]
    g["b_gate_fwd"], g["b_gate_bwd"] = dbf + dbfc, dbb + dbbc
    dq_rot, dkp, dvp, dkc, dvc, g["attn_sink"] = _attn_bwd(dmix, attn, q_rot, kp, vp, kc, vc, sink)
    if rb is not None:
        g["late"] = rb.result(dq_rot)
    dp = _proj_grad("proj_grad", dq_rot, dkp[BLOCK:BLOCK + s], dvp[BLOCK:BLOCK + s], cos, sin, gla_f[:3], gla_b[:3],
                    dgg, dz)
    c_rows = ctx.shape[0]
    zeros = lambda n: jnp.zeros((c_rows, n), BF16)
    dpc = jnp.concatenate([zeros(QW), dgvc, zeros(GVW), dkc.astype(BF16), dvc.astype(BF16), zeros(GKW), dgkc, dzc],
                          axis=1)
    g["w_in"] = _mm("proj_in_dw", h, dp, "tn", init=_mm("proj_in_ctx_dw", hc, dpc, "tn"))
    token = None if reduce_w_in is None else reduce_w_in.start(g["w_in"])
    dh = _mm("proj_in_dx", dp, w_in, "nt", BF16, after=token)
    dhc = _mm("proj_in_ctx_dx", dpc, w_in, "nt")
    if reduce_w_in is not None:
        sh1 = _behind(sh1, reduce_w_in.pair(dh))
    dx, dg_a, dsh1, dsc1 = _norm_mod_bwd("pre_mix_bwd", dh, dx1, x, w["g_pre_mix"], sh1, sc1)
    if reduce_w_in is not None:
        dsh1 = _behind(dsh1, reduce_w_in.total(dx))
    _, dg_b, dsh1c, dsc1c = _norm_mod_bwd("pre_mix_ctx_bwd", dhc, jnp.zeros_like(dhc), ctx, w["g_pre_mix"], sh1c,
                                          sc1c)
    g["g_pre_mix"] = dg_a + dg_b
    d_ada = jnp.concatenate([dsh1, dsc1, dgt1, dsh2, dsc2, dgt2], axis=1)
    d_ada_c = jnp.concatenate([dsh1c, dsc1c, jnp.zeros((1, 4 * d), F32)], axis=1)
    return loss, dx, g, d_ada, d_ada_c


HBM = pl.BlockSpec(memory_space=pltpu.HBM)
N_DEV, N_CHIP = 8, 4


def _place():
    x, y, c = lax.axis_index("x"), lax.axis_index("y"), lax.axis_index("c")
    return x, y, c, [(1 - x, y), (x, 1 - y), (1 - x, 1 - y)]


def _row_tile(n, mult, cap):
    return max(t for t in range(mult, min(n, cap) + 1, mult) if n % t == 0)


def _ag_small(name, v, after=None):
    follow = () if after is None else (after,)

    def body(v_ref, *rest):
        out_ref, send_sems, recv_sems = rest[len(follow):]
        x, y, c, _ = _place()
        out_ref[4 * x + 2 * y + c] = v_ref[...]

        def peer(r):
            return ((1 - x) if r & 4 else x, (1 - y) if r & 2 else y, (1 - c) if r & 1 else c)

        def copy(r, block):
            px, py, pc = block
            return pltpu.make_async_remote_copy(
                src_ref=v_ref, dst_ref=out_ref.at[4 * px + 2 * py + pc], send_sem=send_sems.at[r - 1],
                recv_sem=recv_sems.at[r - 1], device_id=peer(r), device_id_type=MESH)

        sends = [copy(r, (x, y, c)) for r in range(1, N_DEV)]
        for cp in sends:
            cp.start()
        for r in range(1, N_DEV):
            copy(r, peer(r)).wait_recv()
        for cp in sends:
            cp.wait_send()

    return pl.pallas_call(
        body, name=name, out_shape=jax.ShapeDtypeStruct((N_DEV,) + v.shape, v.dtype),
        in_specs=[pl.BlockSpec(memory_space=pltpu.VMEM)] + [pl.BlockSpec(memory_space=pl.ANY)] * len(follow),
        out_specs=pl.BlockSpec(memory_space=pltpu.VMEM),
        scratch_shapes=[pltpu.SemaphoreType.DMA((N_DEV - 1,)), pltpu.SemaphoreType.DMA((N_DEV - 1,))],
    )(v, *follow)


def _halves(c, rows, mult):
    hr = rows // 2
    return pl.ds(pl.multiple_of(c * hr, mult), hr), pl.ds(pl.multiple_of((1 - c) * hr, mult), hr)


def _add_half(name, g, a, c_idx):
    n_sh, hr, n = a.shape
    tr = _row_tile(hr, 16, 1024)
    nb = hr // tr

    def body(c_ref, g_ref, a_ref, o_ref):
        o_ref[...] = (g_ref[...] + a_ref[...]).astype(o_ref.dtype)

    return pl.pallas_call(
        body, name=name, out_shape=jax.ShapeDtypeStruct(a.shape, BF16),
        grid_spec=pltpu.PrefetchScalarGridSpec(
            num_scalar_prefetch=1, grid=(n_sh, nb),
            in_specs=[pl.BlockSpec((1, tr, n), lambda s, i, c_ref: (s, c_ref[0] * nb + i, 0)),
                      pl.BlockSpec((1, tr, n), lambda s, i, c_ref: (s, i, 0))],
            out_specs=pl.BlockSpec((1, tr, n), lambda s, i, c_ref: (s, i, 0))),
        compiler_params=_cp("parallel", "parallel"),
    )(c_idx, g, a)


def _sum_chips(name, b, c_idx):
    n_sh, hr, n = b.shape
    tr = _row_tile(hr, 16, 1024)
    nb = hr // tr

    def body(c_ref, b0, b1, b2, b3, o_ref):
        o_ref[...] = ((b0[0].astype(F32) + b1[0].astype(F32)) + b2[0].astype(F32)) + b3[0].astype(F32)

    return pl.pallas_call(
        body, name=name, out_shape=jax.ShapeDtypeStruct((2 * hr, n), F32),
        grid_spec=pltpu.PrefetchScalarGridSpec(
            num_scalar_prefetch=1, grid=(nb,),
            in_specs=[pl.BlockSpec((1, tr, n), functools.partial(lambda i, c_ref, k: (k, i, 0), k=k))
                      for k in range(n_sh)],
            out_specs=pl.BlockSpec((tr, n), lambda i, c_ref: (c_ref[0] * nb + i, 0))),
        compiler_params=_cp("parallel"),
    )(c_idx, b, b, b, b)


SEM = pl.BlockSpec(memory_space=pltpu.SEMAPHORE)
ANY = pl.BlockSpec(memory_space=pl.ANY)
DATAFLOW = pltpu.SideEffectType.DATAFLOW_SIDE_EFFECTING


def _remote(src, dst, send_sems, recv_sems, k, to):
    return pltpu.make_async_remote_copy(src_ref=src, dst_ref=dst, send_sem=send_sems.at[k], recv_sem=recv_sems.at[k],
                                        device_id=to, device_id_type=MESH)


def _split_copy(name, src, land_shape, land_dtype, n, plan, after=None):
    after = jnp.zeros((8, LANES), F32) if after is None else after

    def start_body(src_ref, land_ref, after_ref, send_sems, recv_sems, src_thru, land_thru, token):
        for cp in plan(src_ref, land_ref, send_sems, recv_sems)[0]:
            cp.start()
        token[...] = jnp.zeros_like(token)

    sems = pltpu.SemaphoreType.DMA((n,))
    send_sems, recv_sems, src_thru, land_thru, token = pl.pallas_call(
        start_body, name=name + "_start",
        out_shape=(sems, sems, pltpu.HBM(src.shape, src.dtype), pltpu.HBM(land_shape, land_dtype),
                   jax.ShapeDtypeStruct((8, LANES), F32)),
        in_specs=(HBM, HBM, ANY), out_specs=(SEM, SEM, HBM, HBM, pl.BlockSpec(memory_space=pltpu.VMEM)),
        input_output_aliases={0: 2, 1: 3}, compiler_params=pltpu.CompilerParams(has_side_effects=DATAFLOW),
    )(pltpu.with_memory_space_constraint(src, pltpu.HBM),
      pltpu.with_memory_space_constraint(lax.empty(land_shape, land_dtype), pltpu.HBM), after)

    def wait(*after):
        def wait_body(src_ref, land_ref, send_sems, recv_sems, *rest):
            sent, received = plan(src_ref, land_ref, send_sems, recv_sems)
            for cp in sent:
                cp.wait_send()
            for cp in received:
                cp.wait_recv()

        return pl.pallas_call(
            wait_body, name=name + "_wait",
            out_shape=(pltpu.HBM(src.shape, src.dtype), pltpu.HBM(land_shape, land_dtype)),
            in_specs=(HBM, HBM, SEM, SEM) + (ANY,) * len(after), out_specs=(HBM, HBM),
            input_output_aliases={0: 0, 1: 1}, compiler_params=pltpu.CompilerParams(has_side_effects=DATAFLOW),
        )(src_thru, land_thru, send_sems, recv_sems, *after)

    return token, wait


def _split_gather(name, shards, after):
    k, n, plan = len(shards), 3 * len(shards), _plan_gather

    def start_body(*refs):
        for cp in plan(refs[:k], refs[k:2 * k], refs[2 * k + 1], refs[2 * k + 2])[0]:
            cp.start()
        refs[-1][...] = jnp.zeros_like(refs[-1])

    sems = pltpu.SemaphoreType.DMA((n,))
    bufs = [pltpu.HBM(s.shape, s.dtype) for s in shards] + [pltpu.HBM((N_CHIP,) + s.shape, s.dtype) for s in shards]
    hbm = lambda t: pltpu.with_memory_space_constraint(t, pltpu.HBM)
    outs = pl.pallas_call(
        start_body, name=name + "_start", out_shape=(sems, sems, *bufs, jax.ShapeDtypeStruct((8, LANES), F32)),
        in_specs=(HBM,) * (2 * k) + (ANY,),
        out_specs=(SEM, SEM) + (HBM,) * (2 * k) + (pl.BlockSpec(memory_space=pltpu.VMEM),),
        input_output_aliases={i: 2 + i for i in range(2 * k)},
        compiler_params=pltpu.CompilerParams(has_side_effects=DATAFLOW),
    )(*[hbm(s) for s in shards], *[hbm(lax.empty((N_CHIP,) + s.shape, s.dtype)) for s in shards], after)
    send_sems, recv_sems, thru, token = outs[0], outs[1], outs[2:2 + 2 * k], outs[-1]

    def wait(*after):
        def wait_body(*refs):
            sent, received = plan(refs[:k], refs[k:2 * k], refs[2 * k], refs[2 * k + 1])
            for cp in sent:
                cp.wait_send()
            for cp in received:
                cp.wait_recv()

        res = pl.pallas_call(
            wait_body, name=name + "_wait", out_shape=tuple(bufs),
            in_specs=(HBM,) * (2 * k) + (SEM, SEM) + (ANY,) * len(after), out_specs=(HBM,) * (2 * k),
            input_output_aliases={i: i for i in range(2 * k)},
            compiler_params=pltpu.CompilerParams(has_side_effects=DATAFLOW),
        )(*thru, send_sems, recv_sems, *after)
        return res[:k], res[k:]

    return token, wait


def _behind(x, token):
    return x + token[0, 0]


def _plan_gather(src_refs, land_refs, send_sems, recv_sems):
    x, y, c, chips = _place()
    pairs = list(enumerate(zip(src_refs, land_refs)))
    sent = [_remote(s, l.at[2 * x + y], send_sems, recv_sems, 3 * i + j, (px, py, c))
            for i, (s, l) in pairs for j, (px, py) in enumerate(chips)]
    received = [_remote(s, l.at[2 * px + py], send_sems, recv_sems, 3 * i + j, (px, py, c))
                for i, (s, l) in pairs for j, (px, py) in enumerate(chips)]
    return sent, received


def _plan_swap(src_ref, land_ref, send_sems, recv_sems):
    x, y, c, _ = _place()
    _, other_half = _halves(c, src_ref.shape[1], 8)
    cp = _remote(src_ref.at[pl.ds(0, src_ref.shape[0]), other_half], land_ref, send_sems, recv_sems, 0, (x, y, 1 - c))
    return [cp], [cp]


def _plan_scatter(src_ref, land_ref, send_sems, recv_sems):
    x, y, c, chips = _place()
    sent = [_remote(src_ref.at[2 * px + py], land_ref.at[2 * x + y], send_sems, recv_sems, j, (px, py, c))
            for j, (px, py) in enumerate(chips)]
    received = [_remote(src_ref.at[2 * px + py], land_ref.at[2 * px + py], send_sems, recv_sems, j, (px, py, c))
                for j, (px, py) in enumerate(chips)]
    return sent, received


def _plan_share(src_ref, land_ref, send_sems, recv_sems):
    x, y, c, _ = _place()
    mine_half, other_half = _halves(c, src_ref.shape[0], 8)
    return ([_remote(src_ref.at[mine_half], src_ref.at[mine_half], send_sems, recv_sems, 0, (x, y, 1 - c))],
            [_remote(src_ref.at[other_half], src_ref.at[other_half], send_sems, recv_sems, 0, (x, y, 1 - c))])


class _GatherBehind:
    def __init__(self, name, shards, chip, after):
        self.chip = chip
        self.token, self.wait = _split_gather(name, shards, after)

    def result(self, *after):
        shards, lands = self.wait(*after)
        return [lax.dynamic_update_slice(land, shard[None], (self.chip, 0, 0)) for shard, land in zip(shards, lands)]


class _ReduceBehind:
    def __init__(self, name, chip, c_idx):
        self.name, self.chip, self.c_idx = name, chip, c_idx

    def start_slab(self, g):
        n_sh, rows, n = g.shape
        token, self.wait = _split_copy(self.name + "_swap", g, (n_sh, rows // 2, n), g.dtype, 1, _plan_swap)
        return token

    def pair(self, after):
        g, a = self.wait(after)
        h = _add_half(self.name + "_pair", g, a, self.c_idx)
        token, self.wait = _split_copy(self.name + "_scatter", h, h.shape, h.dtype, 3, _plan_scatter)
        return token

    def total(self, after):
        h, b = self.wait(after)
        b = lax.dynamic_update_slice(b, lax.dynamic_slice_in_dim(h, self.chip, 1, axis=0), (self.chip, 0, 0))
        f = _sum_chips(self.name + "_sum", b, self.c_idx)
        token, self.wait = _split_copy(self.name + "_share", f, (8, LANES), f.dtype, 1, _plan_share)
        return token

    def result(self, after):
        return self.wait(after)[0]


class _ReduceColsBehind(_ReduceBehind):
    def start(self, g_padded):
        g = _unpack_w_in_grad(g_padded)
        n = g.shape[1] // N_CHIP
        return self.start_slab(jnp.stack([g[:, k * n:(k + 1) * n] for k in range(N_CHIP)]))


def _f_adamw(w, g, m, v):
    m = ADAM_B1 * m + (1.0 - ADAM_B1) * g
    v = ADAM_B2 * v + (1.0 - ADAM_B2) * (g * g)
    m_hat = m / (1.0 - ADAM_B1 ** ADAM_STEP)
    v_hat = v / (1.0 - ADAM_B2 ** ADAM_STEP)
    return -ADAM_LR * (m_hat / (jnp.sqrt(v_hat) + ADAM_EPS) + ADAM_WD * w), m, v


def _adamw(name, w, g, m, v):
    rows, n = w.shape
    return _rowwise(name, lambda w, g, m, v: (_f_adamw(w, g, m, v), ()), rows, [(t, n, 0) for t in (w, g, m, v)], [],
                    [(n, F32)] * 3, [], tm=_row_tile(rows, 8, 256))


def _adamw_many(name, ws, gs, ms, vs):
    k = len(ws)

    def body(*refs):
        ins, outs = refs[:4 * k], refs[4 * k:]
        for i in range(k):
            res = _f_adamw(ins[i][...], ins[k + i][...], ins[2 * k + i][...], ins[3 * k + i][...])
            for j in range(3):
                outs[j * k + i][...] = res[j]

    out = pl.pallas_call(body, name=name, out_shape=[jax.ShapeDtypeStruct(w.shape, F32) for w in ws] * 3)(
        *ws, *gs, *ms, *vs)
    return out[:k], out[k:2 * k], out[2 * k:]


def _pack_rows(parts):
    rows = []
    for t in parts:
        t = t.reshape(-1)
        rows.append(jnp.pad(t, (0, -t.shape[0] % LANES)).reshape(-1, LANES))
    out = jnp.concatenate(rows, axis=0)
    return jnp.pad(out, ((0, -out.shape[0] % 8), (0, 0)))


def _unpack_rows(packed, shapes):
    out, r = [], 0
    for shp in shapes:
        n = int(np.prod(shp))
        nr = -(-n // LANES)
        out.append(packed[r:r + nr].reshape(-1)[:n].reshape(shp))
        r += nr
    return out


def _sum_blocks(name, g):
    def body(g_ref, o_ref):
        acc = g_ref[0]
        for k in range(1, g.shape[0]):
            acc = acc + g_ref[k]
        o_ref[...] = acc

    return pl.pallas_call(body, name=name, out_shape=jax.ShapeDtypeStruct(g.shape[1:], F32))(g)


def _silu(t):
    return t * _sigmoid(t)


def _ada_fwd(cc, w_ada):
    n = w_ada.shape[1]
    tn = _row_tile(n, LANES, 512)

    def body(cc_ref, w_ref, o_ref):
        o_ref[...] = _nn(_silu(cc_ref[...]), w_ref[...])

    return pl.pallas_call(
        body, name="ada_fwd", grid=(n // tn,), out_shape=jax.ShapeDtypeStruct((cc.shape[0], n), F32),
        in_specs=[pl.BlockSpec(cc.shape, lambda j: (0, 0)), pl.BlockSpec((w_ada.shape[0], tn), lambda j: (0, j))],
        out_specs=pl.BlockSpec((cc.shape[0], tn), lambda j: (0, j)), compiler_params=_cp("parallel"),
    )(cc, w_ada)


def _ada_bwd(cc, dm, w_ada):
    d, n = w_ada.shape
    tn = _row_tile(n, LANES, 512)

    def body(cc_ref, dm_ref, w_ref, gw_ref, ds_ref):
        @pl.when(pl.program_id(0) == 0)
        def _():
            ds_ref[...] = jnp.zeros_like(ds_ref)

        gw_ref[...] = _raw_dot("tn", _silu(cc_ref[...]), dm_ref[...], True)
        ds_ref[...] += _raw_dot("nt", dm_ref[...], w_ref[...], False)

    return pl.pallas_call(
        body, name="ada_bwd", grid=(n // tn,),
        out_shape=[jax.ShapeDtypeStruct((d, n), F32), jax.ShapeDtypeStruct(cc.shape, F32)],
        in_specs=[pl.BlockSpec(cc.shape, lambda j: (0, 0)), pl.BlockSpec((cc.shape[0], tn), lambda j: (0, j)),
                  pl.BlockSpec((d, tn), lambda j: (0, j))],
        out_specs=[pl.BlockSpec((d, tn), lambda j: (0, j)), pl.BlockSpec(cc.shape, lambda j: (0, 0))],
        compiler_params=_cp("arbitrary"),
    )(cc, dm, w_ada)


def _c_ctx_grad(parts, c_ctx):
    def body(p_ref, c_ref, o_ref):
        ds = ((p_ref[0] + p_ref[1]) + p_ref[2]) + p_ref[3]
        _, vjp = jax.vjp(_silu, c_ref[...])
        o_ref[...] = vjp(ds)[0]

    return pl.pallas_call(body, name="c_ctx_grad", out_shape=jax.ShapeDtypeStruct(c_ctx.shape, F32))(parts, c_ctx)


def kernel(x, c, ctx, c_ctx, w_ada, b_ada, g_pre_mix, g_post_mix, g_pre_ffn, g_post_ffn, w_in, attn_sink, w_gate_fwd, b_gate_fwd, w_gate_bwd, b_gate_bwd, g_gla_norm, w_out, w_ffn_in, w_ffn_out, loss_target, m_c_ctx, m_w_ada, m_b_ada, m_g_pre_mix, m_g_post_mix, m_g_pre_ffn, m_g_post_ffn, m_w_in, m_attn_sink, m_w_gate_fwd, m_b_gate_fwd, m_w_gate_bwd, m_b_gate_bwd, m_g_gla_norm, m_w_out, m_w_ffn_in, m_w_ffn_out, v_c_ctx, v_w_ada, v_b_ada, v_g_pre_mix, v_g_post_mix, v_g_pre_ffn, v_g_post_ffn, v_w_in, v_attn_sink, v_w_gate_fwd, v_b_gate_fwd, v_w_gate_bwd, v_b_gate_bwd, v_g_gla_norm, v_w_out, v_w_ffn_in, v_w_ffn_out):
    xi, yi, ci = lax.axis_index("x"), lax.axis_index("y"), lax.axis_index("c")
    dev, chip = 4 * xi + 2 * yi + ci, 2 * xi + yi
    c_idx = jnp.reshape(ci, (1,)).astype(jnp.int32)
    d = x.shape[-1]
    n_ada, n_in, n_f = w_ada.shape[-1], w_in.shape[-1], w_ffn_in.shape[-1]
    r_out, r_f = w_out.shape[1], w_ffn_out.shape[1]
    n_gate = w_gate_fwd.shape[-1]
    by_chip = lambda t: t[0::2]

    rc = -(-d // LANES)
    g1 = _ag_small("gather_cond", _pack_rows([c[0], w_gate_fwd[0], w_gate_bwd[0]]))
    c_all = g1[:, :rc].reshape(N_DEV, -1)[:, :d]
    gr = GATE_RANK * n_gate // LANES
    gate_full = lambda off: jnp.transpose(by_chip(g1)[:, off:off + gr].reshape(N_CHIP, GATE_RANK, n_gate),
                                          (1, 0, 2)).reshape(GATE_RANK, N_CHIP * n_gate)
    wgf, wgb = gate_full(rc), gate_full(rc + gr)
    cc = jnp.concatenate([c_all, c_ctx[None, :], jnp.zeros((7, d), F32)], axis=0)

    g2 = _ag_small("gather_ada", _ada_fwd(cc, w_ada[0]).reshape(-1, LANES))
    ada_all = jnp.transpose(by_chip(g2).reshape(N_CHIP, 16, n_ada), (1, 0, 2)).reshape(16, N_CHIP * n_ada) + b_ada
    first = _GatherBehind("gather_w_in", [w_in[0].astype(BF16)], chip, g2)
    late_shards = [w_out[0].astype(BF16), jnp.transpose(w_ffn_in[0]).astype(BF16), w_ffn_out[0].astype(BF16)]
    late = []

    def first_weights(*after):
        w_in_g, = first.result(*after, *late_shards)
        late.append(_GatherBehind("gather_late", late_shards, chip, w_in_g))
        return _pack_w_in(jnp.concatenate([w_in_g[k] for k in range(N_CHIP)], axis=1)), late[0].token

    def late_weights(after):
        return [t.reshape(-1, d) for t in late[0].result(after)]

    ada_all = _behind(ada_all, first.token)
    ada = lax.dynamic_slice(ada_all, (dev, 0), (1, N_CHIP * n_ada))
    ada_c = ada_all[N_DEV:N_DEV + 1]

    w = _prep_gate_weights(wgf, wgb)
    w.update(w_in=first_weights, g_pre_mix=g_pre_mix, g_post_mix=g_post_mix, g_pre_ffn=g_pre_ffn, g_post_ffn=g_post_ffn,
             attn_sink=attn_sink, b_gate_fwd=b_gate_fwd, b_gate_bwd=b_gate_bwd, g_gla_norm=g_gla_norm)

    reduce_behind = _ReduceBehind("reduce_late", chip, c_idx)
    reduce_w_in = _ReduceColsBehind("reduce_w_in", chip, c_idx)
    loss_lanes, grad_x, g, d_ada, d_ada_c = _local_step(x[0], ctx[0], loss_target[0], ada, ada_c, w, late_weights,
                                                        reduce_behind, reduce_w_in)

    small = ("g_pre_mix", "g_post_mix", "g_pre_ffn", "g_post_ffn", "attn_sink", "b_gate_fwd", "b_gate_bwd",
             "g_gla_norm", "w_gate_fwd", "w_gate_bwd")
    shapes = [(1, 6 * d)] * 2 + [g[n].shape for n in small] + [(1, LANES)]
    g3 = _ag_small("gather_small_grads", _pack_rows([d_ada, d_ada_c] + [g[n] for n in small] + [loss_lanes]))
    tot = dict(zip(("d_ada", "d_ada_c") + small + ("loss",),
                   _unpack_rows(_sum_blocks("sum_small_grads", g3), shapes)))
    r_ada = 6 * d // LANES
    dm = jnp.concatenate([g3[:, :r_ada].reshape(N_DEV, 6 * d), tot["d_ada_c"], jnp.zeros((7, 6 * d), F32)], axis=0)
    grads = {n: tot[n] for n in small[:8]}
    grads["b_ada"] = _sum_blocks("sum_b_ada", dm.reshape(16, r_ada, LANES)).reshape(1, 6 * d)
    grads["w_gate_fwd"] = lax.dynamic_slice(tot["w_gate_fwd"], (0, chip * n_gate), (GATE_RANK, n_gate))[None]
    grads["w_gate_bwd"] = lax.dynamic_slice(tot["w_gate_bwd"], (0, chip * n_gate), (GATE_RANK, n_gate))[None]
    gw_ada, dsc = _ada_bwd(cc, lax.dynamic_slice(dm, (0, chip * n_ada), (16, n_ada)), w_ada[0])
    grads["w_ada"] = gw_ada[None]
    g4 = _ag_small("gather_c_ctx", _pack_rows([dsc[N_DEV]]))
    grads["c_ctx"] = _c_ctx_grad(by_chip(g4), _pack_rows([c_ctx])).reshape(-1)[:d]

    grads["w_in"] = reduce_w_in.result(g4)[None]
    part = lambda n: g["late"][g["late_at"][n]:g["late_at"][n] + g["late_rows"][n]]
    grads["w_ffn_in"], grads["w_ffn_out"], grads["w_out"] = (jnp.transpose(part("w_ffn_in_t"))[None],
                                                            part("w_ffn_out")[None], part("w_out")[None])

    names = ("c_ctx", "w_ada", "b_ada", "g_pre_mix", "g_post_mix", "g_pre_ffn", "g_post_ffn", "w_in", "attn_sink",
             "w_gate_fwd", "b_gate_fwd", "w_gate_bwd", "b_gate_bwd", "g_gla_norm", "w_out", "w_ffn_in", "w_ffn_out")
    weights = dict(zip(names, (c_ctx, w_ada, b_ada, g_pre_mix, g_post_mix, g_pre_ffn, g_post_ffn, w_in, attn_sink,
                               w_gate_fwd, b_gate_fwd, w_gate_bwd, b_gate_bwd, g_gla_norm, w_out, w_ffn_in,
                               w_ffn_out)))
    m_in = dict(zip(names, (m_c_ctx, m_w_ada, m_b_ada, m_g_pre_mix, m_g_post_mix, m_g_pre_ffn, m_g_post_ffn, m_w_in,
                            m_attn_sink, m_w_gate_fwd, m_b_gate_fwd, m_w_gate_bwd, m_b_gate_bwd, m_g_gla_norm,
                            m_w_out, m_w_ffn_in, m_w_ffn_out)))
    v_in = dict(zip(names, (v_c_ctx, v_w_ada, v_b_ada, v_g_pre_mix, v_g_post_mix, v_g_pre_ffn, v_g_post_ffn, v_w_in,
                            v_attn_sink, v_w_gate_fwd, v_b_gate_fwd, v_w_gate_bwd, v_b_gate_bwd, v_g_gla_norm,
                            v_w_out, v_w_ffn_in, v_w_ffn_out)))
    large = ("w_ada", "w_in", "w_out", "w_ffn_in", "w_ffn_out")
    tiny = tuple(n for n in names if n not in large)
    delta, new_m, new_v = {}, {}, {}
    for n in large:
        dl, nm, nv = _adamw("adamw_" + n, weights[n][0], grads[n][0], m_in[n][0], v_in[n][0])
        delta[n], new_m[n], new_v[n] = dl[None], nm[None], nv[None]
    for n in tiny:
        grads[n] = grads[n].reshape(weights[n].shape)
    as_rows = lambda t: t.reshape(-1, t.shape[-1])
    res = _adamw_many("adamw_small", *[[as_rows(t[n]) for n in tiny] for t in (weights, grads, m_in, v_in)])
    for out, vals in zip((delta, new_m, new_v), res):
        out.update({n: val.reshape(weights[n].shape) for n, val in zip(tiny, vals)})

    return (tot["loss"][0, 0], grad_x[None], *[grads[n] for n in names], *[delta[n] for n in names], *[new_m[n] for n in names],
            *[new_v[n] for n in names])
```

```python
import functools

import jax
import jax.numpy as jnp
import numpy as np
from jax import lax
from jax.experimental import pallas as pl
from jax.experimental.pallas import tpu as pltpu

F32 = jnp.float32
BF16 = jnp.bfloat16
MESH = pl.DeviceIdType.MESH

HEAD_DIM = 64
ATT_HEADS = 8
ATT_KV_HEADS = 2
ATT_GROUP = ATT_HEADS // ATT_KV_HEADS
WINDOW = 128
BLOCK = 128
GRID_W = 64
ROPE_BASE = 10000.0
GLA_HEADS = 8
GLA_DK = 32
GLA_DV = 64
GLA_CHUNK = 64
GATE_RANK = 16
GATE_TAU = 16.0
NEG_INF = -1e30
QW = ATT_HEADS * HEAD_DIM
KVW = ATT_KV_HEADS * HEAD_DIM
GKW = GLA_HEADS * GLA_DK
GVW = GLA_HEADS * GLA_DV
IN_COLS = QW + 2 * KVW + 2 * GKW + 2 * GVW + 2 * GATE_RANK
LANES = 128
IN_PAD = IN_COLS + LANES - 2 * GATE_RANK
C_Q, C_GV, C_GG = 0, QW, QW + GVW
C_K = C_GG + GVW
C_V = C_K + KVW
C_GQ = C_V + KVW
C_GK = C_GQ + GKW
C_Z = C_GK + GKW
MIX = QW + GVW

ADAM_LR, ADAM_B1, ADAM_B2, ADAM_EPS, ADAM_WD, ADAM_STEP = 0.001, 0.9, 0.999, 1e-08, 0.01, 10

VMEM_LIMIT = 56 * 1024 * 1024


def _cp(*sem):
    return pltpu.CompilerParams(dimension_semantics=sem, vmem_limit_bytes=VMEM_LIMIT)


def _pick(n, cands):
    for t in cands:
        if n % t == 0:
            return t
    return n


_DIMS = {"nn": (((1,), (0,)), ((), ())), "nt": (((1,), (1,)), ((), ())), "tn": (((0,), (0,)), ((), ()))}


def _raw_dot(mode, a, b, hi):
    dot = lambda u, v: lax.dot_general(u, v, _DIMS[mode], preferred_element_type=F32)
    if not hi:
        return dot(a.astype(BF16), b.astype(BF16))
    a, b = a.astype(F32), b.astype(F32)
    a_hi, b_hi = a.astype(BF16), b.astype(BF16)
    out = dot(a_hi, b_hi)
    if hi != "a":
        out = out + dot((a - a_hi.astype(F32)).astype(BF16), b_hi)
    if hi != "b":
        out = out + dot(a_hi, (b - b_hi.astype(F32)).astype(BF16))
    return out


def _make_dot(mode, hi):
    @jax.custom_vjp
    def dot(a, b):
        return _raw_dot(mode, a, b, hi)

    def fwd(a, b):
        return _raw_dot(mode, a, b, hi), (a, b)

    def bwd(res, dc):
        a, b = res
        if mode == "nn":
            return (_raw_dot("nt", dc, b, "b" if hi == "b" else bool(hi)),
                    _raw_dot("tn", a, dc, "a" if hi == "a" else bool(hi)))
        if mode == "nt":
            return _raw_dot("nn", dc, b, bool(hi)), _raw_dot("tn", dc, a, bool(hi))
        return _raw_dot("nt", b, dc, bool(hi)), _raw_dot("nn", a, dc, bool(hi))

    dot.defvjp(fwd, bwd)
    return dot


_nn, _nt, _tn = _make_dot("nn", False), _make_dot("nt", False), _make_dot("tn", False)
_nn_mask, _nn_by_exact = _make_dot("nn", "a"), _make_dot("nn", "b")


MM_VMEM_BUDGET = 44 * 1024 * 1024


def _halvings(n):
    out = [n]
    while out[-1] % (2 * LANES) == 0:
        out.append(out[-1] // 2)
    return out


def _mm_tiles(mode, m, n, k, a_bytes, b_bytes, o_bytes, init_bytes=0):
    tms = [t for t in dict.fromkeys((m, m // 2, m // 4, 2048, 1024, 512, 256, 128))
           if m % t == 0 and t % (LANES if mode == "tn" else 16) == 0 and t <= 4096] or [m]
    if mode == "tn":
        fits = [(k // tk + 0.5 * (m // tm), tm, tk)
                for tk in (4096, 2048, 1024, 512, 256, 128) if k % tk == 0 for tm in tms
                if 2 * (tk * tm * a_bytes + tk * n * b_bytes + tm * n * (o_bytes + init_bytes)) <= MM_VMEM_BUDGET]
        if fits:
            _, tm, tk = min(fits)
            return tm, n, tk
    tks = ([t for t in (512, 256, 128) if k % t == 0] or [k]) if mode == "tn" else _halvings(k)
    for tn in _halvings(n):
        for tk in tks:
            for tm in tms:
                acc = tm * tn * 4 if (k // tk > 1 and o_bytes != 4) else 0
                tiles = tm * tk * a_bytes + tk * tn * b_bytes + tm * tn * (o_bytes + init_bytes)
                if 2 * tiles + acc <= MM_VMEM_BUDGET:
                    return tm, tn, tk
    return tms[-1], _halvings(n)[-1], tks[-1]


def _mm(name, a, b, mode, out_dtype=F32, init=None, after=None):
    follow = () if after is None else (after,)
    if mode == "nn":
        (m, k), n = a.shape, b.shape[1]
    elif mode == "nt":
        (m, k), n = a.shape, b.shape[0]
    else:
        (k, m), n = a.shape, b.shape[1]
    tm, tn, tk = _mm_tiles(mode, m, n, k, a.dtype.itemsize, b.dtype.itemsize, jnp.dtype(out_dtype).itemsize,
                           0 if init is None else 4)
    nk = k // tk
    use_acc = nk > 1 and out_dtype != F32

    inits = () if init is None else (init,)

    def body(a_ref, b_ref, *rest):
        rest = rest[:len(inits)] + rest[len(inits) + len(follow):]
        o_ref, acc = rest[len(inits)], rest[len(inits) + 1:]
        part = _raw_dot(mode, a_ref[...], b_ref[...], False)
        first = lambda: part + rest[0][...] if inits else part
        if nk == 1:
            o_ref[...] = first().astype(o_ref.dtype)
            return
        acc_ref = acc[0] if use_acc else o_ref
        kk = pl.program_id(2)

        @pl.when(kk == 0)
        def _():
            acc_ref[...] = first()

        @pl.when(kk > 0)
        def _():
            acc_ref[...] += part

        if use_acc:
            @pl.when(kk == nk - 1)
            def _():
                o_ref[...] = acc_ref[...].astype(o_ref.dtype)

    if mode == "nn":
        a_spec = pl.BlockSpec((tm, tk), lambda i, j, kk: (i, kk))
        b_spec = pl.BlockSpec((tk, tn), lambda i, j, kk: (kk, j))
    elif mode == "nt":
        a_spec = pl.BlockSpec((tm, tk), lambda i, j, kk: (i, kk))
        b_spec = pl.BlockSpec((tn, tk), lambda i, j, kk: (j, kk))
    else:
        a_spec = pl.BlockSpec((tk, tm), lambda i, j, kk: (kk, i))
        b_spec = pl.BlockSpec((tk, tn), lambda i, j, kk: (kk, j))
    return pl.pallas_call(
        body, name=name, grid=(m // tm, n // tn, nk),
        in_specs=[a_spec, b_spec] + [pl.BlockSpec((tm, tn), lambda i, j, kk: (i, j))] * len(inits)
        + [pl.BlockSpec(memory_space=pl.ANY)] * len(follow),
        out_specs=pl.BlockSpec((tm, tn), lambda i, j, kk: (i, j)),
        out_shape=jax.ShapeDtypeStruct((m, n), out_dtype),
        scratch_shapes=[pltpu.VMEM((tm, tn), F32)] if use_acc else [],
        compiler_params=_cp("parallel", "parallel", "arbitrary"),
    )(a, b, *inits, *follow)


def _slab_layout(rows):
    offsets, at = [], 0
    for r in rows:
        at = -(-at // r) * r
        offsets.append(at)
        at += r
    return offsets, -(-at // 32) * 32


def _slab_zero_gaps(name, shape, rows, offsets):
    gaps = [(o + r, nxt) for o, r, nxt in zip(offsets, rows, offsets[1:] + [shape[1]]) if nxt > o + r]
    slab = None
    for i, (lo, hi) in enumerate(gaps):
        step = int(np.gcd(lo, hi - lo))

        def body(*refs):
            refs[-1][...] = jnp.zeros_like(refs[-1])

        slab = pl.pallas_call(
            body, name=f"{name}_{i}", grid=(shape[0], (hi - lo) // step), out_shape=jax.ShapeDtypeStruct(shape, F32),
            in_specs=[] if slab is None else [pl.BlockSpec(memory_space=pl.ANY)],
            out_specs=pl.BlockSpec((1, step, shape[2]), functools.partial(lambda k, j, b: (k, b + j, 0), b=lo // step)),
            input_output_aliases={} if slab is None else {0: 0}, compiler_params=_cp("parallel", "parallel"),
        )(*(() if slab is None else (slab,)))
    return slab


def _dw_into_slab(name, a, b, slab, shape, at):
    (k, m), n = a.shape, b.shape[1]
    r = m // N_CHIP
    fits = [(k // tk + 0.5 * (m // tm), tm, tk)
            for tk in (4096, 2048, 1024, 512, 256, 128) if k % tk == 0 for tm in (m, m // 2, r) if tm % LANES == 0
            if 2 * (tk * tm * a.dtype.itemsize + tk * n * b.dtype.itemsize + tm * n * 4) <= MM_VMEM_BUDGET]
    _, tm, tk = min(fits)
    per, nk = tm // r, k // tk

    def body(a_ref, b_ref, *rest):
        o_ref = rest[-1]
        part = _raw_dot("tn", a_ref[...], b_ref[...], False).reshape(o_ref.shape)
        if nk == 1:
            o_ref[...] = part
            return
        kk = pl.program_id(1)

        @pl.when(kk == 0)
        def _():
            o_ref[...] = part

        @pl.when(kk > 0)
        def _():
            o_ref[...] += part

    prev = () if slab is None else (slab,)
    return pl.pallas_call(
        body, name=name, grid=(m // tm, nk), out_shape=jax.ShapeDtypeStruct(shape, F32),
        in_specs=[pl.BlockSpec((tk, tm), lambda i, kk: (kk, i)), pl.BlockSpec((tk, n), lambda i, kk: (kk, 0))]
        + [pl.BlockSpec(memory_space=pl.ANY)] * len(prev),
        out_specs=pl.BlockSpec((per, r, n), lambda i, kk: (i, at // r, 0)),
        input_output_aliases={2: 0} if prev else {}, compiler_params=_cp("parallel", "arbitrary"),
    )(a, b, *prev)


def _rowwise(name, fn, rows, row_ins, full_ins, row_outs, acc_outs, tm=None):
    tm = tm or _pick(rows, (512, 256, 128))
    n_r, n_f, n_o, n_a = len(row_ins), len(full_ins), len(row_outs), len(acc_outs)

    def body(*refs):
        ins, outs = refs[:n_r + n_f], refs[n_r + n_f:]
        vals = [r[...].astype(F32) for r in ins]
        ro, ao = fn(*vals)
        for r, val in zip(outs[:n_o], ro):
            r[...] = val.astype(r.dtype)
        if n_a:
            @pl.when(pl.program_id(0) == 0)
            def _():
                for r in outs[n_o:]:
                    r[...] = jnp.zeros_like(r)

            for r, val in zip(outs[n_o:], ao):
                r[...] += val

    in_specs = [pl.BlockSpec((tm, w), functools.partial(lambda i, cb: (i, cb), cb=cb)) for _, w, cb in row_ins]
    in_specs += [pl.BlockSpec(a.shape, lambda i: (0, 0)) for a in full_ins]
    out_specs = [pl.BlockSpec((tm, w), lambda i: (i, 0)) for w, _ in row_outs]
    out_specs += [pl.BlockSpec(s, lambda i: (0, 0)) for s in acc_outs]
    out_shape = [jax.ShapeDtypeStruct((rows, w), dt) for w, dt in row_outs]
    out_shape += [jax.ShapeDtypeStruct(s, F32) for s in acc_outs]
    return pl.pallas_call(
        body, name=name, grid=(rows // tm,), in_specs=in_specs, out_specs=out_specs, out_shape=out_shape,
        compiler_params=_cp("arbitrary" if n_a else "parallel"),
    )(*[a for a, _, _ in row_ins], *full_ins)


def _rn(x):
    return x * lax.rsqrt(jnp.mean(x * x, axis=-1, keepdims=True) + 1e-6)


def _sigmoid(t):
    return 1.0 / (1.0 + jnp.exp(-t))


def _f_norm_mod(x, g, sh, sc):
    return _rn(x) * g * (1.0 + sc) + sh


def _f_post_res(xr, y, g, gate):
    return xr + gate * (_rn(y) * g)


@jax.custom_vjp
def _f_swiglu(g, u):
    return g * _sigmoid(g) * u


def _f_swiglu_fwd(g, u):
    s = _sigmoid(g)
    return g * s * u, (g, u, s)


def _f_swiglu_bwd(res, da):
    g, u, s = res
    gs = g * s
    return da * u * (s + gs * (1.0 - s)), da * gs


_f_swiglu.defvjp(_f_swiglu_fwd, _f_swiglu_bwd)


def _logsig(u):
    return jnp.minimum(u, 0.0) - jnp.log(1.0 + jnp.exp(-jnp.abs(u)))


def _f_gate(z, wf, wb, bf, bb):
    return _logsig(_nn(z, wf) + bf) / GATE_TAU, _logsig(_nn(z, wb) + bb) / GATE_TAU


def _f_gla_out(of, ob, gg, gt, bd):
    o = of + ob
    ms = _nn_by_exact(o * o, bd)
    return o * lax.rsqrt(ms + 1e-6) * gt * (gg * _sigmoid(gg))


def _norm_mod(name, x, g, sh, sc):
    rows, d = x.shape
    return _rowwise(name, lambda x, g, sh, sc: ((_f_norm_mod(x, g, sh, sc),), ()), rows,
                    [(x, d, 0)], [g, sh, sc], [(d, BF16)], [])[0]


def _rn_bwd(x, dn):
    r = lax.rsqrt(jnp.mean(x * x, axis=-1, keepdims=True) + 1e-6)
    n = x * r
    return r * (dn - n * jnp.mean(dn * n, axis=-1, keepdims=True)), n


def _norm_mod_grads(dh, x, g, sc):
    dx, n = _rn_bwd(x, dh * (g * (1.0 + sc)))
    t = jnp.sum(dh * n, axis=0, keepdims=True)
    return dx, (1.0 + sc) * t, jnp.sum(dh, axis=0, keepdims=True), g * t


def _post_res_grads(dout, y, g, gate):
    dy, n = _rn_bwd(y, dout * (gate * g))
    t = jnp.sum(dout * n, axis=0, keepdims=True)
    return dy, gate * t, g * t


def _norm_mod_bwd(name, dh, dres, x, g, sh, sc):
    rows, d = x.shape

    def fn(dh, dres, x, g, sh, sc):
        dx, dg, dsh, dsc = _norm_mod_grads(dh, x, g, sc)
        return (dx + dres,), (dg, dsh, dsc)

    return _rowwise(name, fn, rows, [(dh, d, 0), (dres, d, 0), (x, d, 0)], [g, sh, sc], [(d, F32)],
                    [(1, d)] * 3)


def _post_res_norm_mod(name, xr, y, g_post, gate, g_pre, sh, sc):
    rows, d = xr.shape

    def fn(xr, y, g_post, gate, g_pre, sh, sc):
        x1 = _f_post_res(xr, y, g_post, gate)
        return (x1, _f_norm_mod(x1, g_pre, sh, sc)), ()

    return _rowwise(name, fn, rows, [(xr, d, 0), (y, d, 0)], [g_post, gate, g_pre, sh, sc], [(d, F32), (d, BF16)], [])


def _norm_mod_post_res_bwd(name, dh, dres, x1, y, g_pre, sh, sc, g_post, gate):
    rows, d = x1.shape

    def fn(dh, dres, x1, y, g_pre, sh, sc, g_post, gate):
        dx1, dg_pre, dsh, dsc = _norm_mod_grads(dh, x1, g_pre, sc)
        dx1 = dx1 + dres
        dy, dg_post, dgate = _post_res_grads(dx1, y, g_post, gate)
        return (dx1, dy), (dg_pre, dsh, dsc, dg_post, dgate)

    return _rowwise(name, fn, rows, [(dh, d, 0), (dres, d, 0), (x1, d, 0), (y, d, 0)], [g_pre, sh, sc, g_post, gate],
                    [(d, F32), (d, BF16)], [(1, d)] * 5, tm=_pick(rows, (256, 128)))


def _post_res_loss(name, xr, y, g, gate, target):
    rows, d = xr.shape

    def fn(xr, y, target, g, gate):
        diff = _f_post_res(xr, y, g, gate) - target
        part = 0.5 * jnp.sum(jnp.mean(diff * diff, axis=-1, keepdims=True), axis=0, keepdims=True)
        dx2 = diff * (1.0 / d)
        dy, dg, dgate = _post_res_grads(dx2, y, g, gate)
        return (dx2, dy), (jnp.broadcast_to(part, (1, LANES)), dg, dgate)

    return _rowwise(name, fn, rows, [(xr, d, 0), (y, d, 0), (target, d, 0)], [g, gate], [(d, F32), (d, BF16)],
                    [(1, LANES), (1, d), (1, d)])


def _mm_rows(name, a, b, mode, fn, extras, outs):
    m, k = a.shape
    tm = _pick(m, (256, 128))

    def body(a_ref, b_ref, *rest):
        tiles = fn(_raw_dot(mode, a_ref[...], b_ref[...], False), *[e[...] for e in rest[:len(extras)]])
        for r, val in zip(rest[len(extras):], tiles):
            r[...] = val.astype(r.dtype)

    row = lambda w: pl.BlockSpec((tm, w), lambda i: (i, 0))
    return pl.pallas_call(
        body, name=name, grid=(m // tm,),
        in_specs=[row(k), pl.BlockSpec(b.shape, lambda i: (0, 0))] + [row(e.shape[1]) for e in extras],
        out_specs=[row(w) for w, _ in outs], out_shape=[jax.ShapeDtypeStruct((m, w), dt) for w, dt in outs],
        compiler_params=_cp("parallel"),
    )(a, b, *extras)


def _ffn_in_swiglu(name, h, w_t):
    f = w_t.shape[0] // 2
    fn = lambda u: (u, _f_swiglu(u[:, :f], u[:, f:]))
    return _mm_rows(name, h, w_t, "nt", fn, [], [(2 * f, BF16), (f, BF16)])


def _ffn_out_dx_swiglu_bwd(name, df, w_out, u):
    f = w_out.shape[0]

    def fn(da, u):
        u = u.astype(F32)
        _, vjp = jax.vjp(_f_swiglu, u[:, :f], u[:, f:])
        return (jnp.concatenate(vjp(da), axis=1),)

    return _mm_rows(name, df, w_out, "nt", fn, [u], [(2 * f, BF16)])[0]


def _gate_fwd(name, p, wf, wb, bf, bb):
    rows = p.shape[0]
    return _rowwise(name, lambda z, wf, wb, bf, bb: (_f_gate(z, wf, wb, bf, bb), ()), rows,
                    [(p, LANES, C_Z // LANES)], [wf, wb, bf, bb], [(GKW, F32)] * 2, [])


def _gate_bwd(name, p, dla_f, dla_b, wf, wb, bf, bb):
    rows = p.shape[0]

    def fn(z, dlf, dlb, wf, wb, bf, bb):
        _, vjp = jax.vjp(_f_gate, z, wf, wb, bf, bb)
        dz, dwf, dwb, dbf, dbb = vjp((dlf, dlb))
        return (dz,), (dwf, dwb, dbf, dbb)

    return _rowwise(name, fn, rows, [(p, LANES, C_Z // LANES), (dla_f, GKW, 0), (dla_b, GKW, 0)],
                    [wf, wb, bf, bb], [(LANES, BF16)], [(LANES, GKW), (LANES, GKW), (1, GKW), (1, GKW)])


def _head_mean_matrix():
    h = np.arange(GVW) // GLA_DV
    return jnp.asarray((h[:, None] == h[None, :]).astype(np.float32) / GLA_DV)


def _gla_out(name, attn, of, ob, p, gt):
    rows = of.shape[0]
    bd = _head_mean_matrix()
    fn = lambda attn, of, ob, gg, gt, bd: ((jnp.concatenate([attn, _f_gla_out(of, ob, gg, gt, bd)], axis=1),), ())
    return _rowwise(name, fn, rows, [(attn, QW, 0), (of, GVW, 0), (ob, GVW, 0), (p, GVW, C_GG // GVW)], [gt, bd],
                    [(MIX, BF16)], [])[0]


def _gla_out_bwd(name, dmix, of, ob, p, gt):
    rows = of.shape[0]
    bd = _head_mean_matrix()

    def fn(dm, of, ob, gg, gt, bd):
        _, vjp = jax.vjp(lambda of, gg, gt: _f_gla_out(of, ob, gg, gt, bd), of, gg, gt)
        do, dgg, dgt = vjp(dm)
        return (do, dgg), (dgt,)

    return _rowwise(name, fn, rows, [(dmix, GVW, 1), (of, GVW, 0), (ob, GVW, 0), (p, GVW, C_GG // GVW)], [gt, bd],
                    [(GVW, BF16), (GVW, BF16)], [(1, GVW)])


def _rope_tables(n_tokens):
    t = jnp.arange(n_tokens)
    row = (t // GRID_W).astype(F32)
    col = (t % GRID_W).astype(F32)
    half = HEAD_DIM // 2
    inv_freq = ROPE_BASE ** (-jnp.arange(0, half, 2, dtype=F32) / half)
    ang_r = row[:, None] * inv_freq[None, :]
    ang_c = col[:, None] * inv_freq[None, :]
    ang = jnp.concatenate([ang_r, ang_r, ang_c, ang_c], axis=-1)
    sign = jnp.concatenate([-jnp.ones((16,), F32), jnp.ones((16,), F32)] * 2)
    cos, sin = jnp.cos(ang), jnp.sin(ang) * sign[None, :]
    return jnp.tile(cos, (1, 2)), jnp.tile(sin, (1, 2))


def _rot_pairs(x):
    w = x.shape[-1]
    lane = lax.broadcasted_iota(jnp.int32, x.shape, x.ndim - 1)
    return jnp.where((lane % 32) < 16, pltpu.roll(x, w - 16, x.ndim - 1), pltpu.roll(x, 16, x.ndim - 1))


def _rope_apply(x, cos, sin_signed, inverse):
    reps = x.shape[-1] // LANES
    cos = jnp.concatenate([cos] * reps, axis=-1) if reps > 1 else cos
    sin = jnp.concatenate([sin_signed] * reps, axis=-1) if reps > 1 else sin_signed
    if inverse:
        return x * cos + _rot_pairs(x * sin)
    return x * cos + _rot_pairs(x) * sin


def _rope_fwd(name, p, cos, sin):
    rows = p.shape[0]

    def fn(q, k, v, cos, sin):
        return (_rope_apply(q, cos, sin, False), _rope_apply(k, cos, sin, False), v), ()

    return _rowwise(name, fn, rows, [(p, QW, 0), (p, KVW, C_K // KVW), (p, KVW, C_V // KVW), (cos, LANES, 0),
                                     (sin, LANES, 0)], [], [(QW, BF16), (KVW, BF16), (KVW, BF16)], [])


def _proj_grad(name, dq_rot, dk_rot, dv, cos, sin, gla_f, gla_b, dgg, dz):
    rows = dq_rot.shape[0]

    def fn(dq, dk, dv, cos, sin, gqf, gkf, gvf, gqb, gkb, gvb, dgg, dz):
        parts = [_rope_apply(dq, cos, sin, True), gvf + gvb, dgg, _rope_apply(dk, cos, sin, True), dv, gqf + gqb,
                 gkf + gkb, dz]
        return (jnp.concatenate(parts, axis=1),), ()

    ins = [(dq_rot, QW), (dk_rot, KVW), (dv, KVW), (cos, LANES), (sin, LANES)]
    ins += [(t, t.shape[1]) for t in (*gla_f, *gla_b)] + [(dgg, GVW), (dz, LANES)]
    return _rowwise(name, fn, rows, [(t, w, 0) for t, w in ins], [], [(IN_PAD, BF16)], [],
                    tm=_pick(rows, (256, 128)))[0]


GROUP_ROWS = ATT_GROUP * BLOCK


ATT_SCALE = HEAD_DIM ** -0.5


def _attn_bias(n_tokens):
    nb = n_tokens // BLOCK
    i = (jnp.arange(GROUP_ROWS) % BLOCK)[:, None]
    j = jnp.arange(3 * BLOCK)[None, :]

    def one(n):
        kpos = (n - 1) * BLOCK + j
        return jnp.where((jnp.abs(j - BLOCK - i) <= WINDOW) & (kpos >= 0) & (kpos < n_tokens), 0.0, NEG_INF)

    return jnp.stack([one(0), one(1), one(nb - 1)]).astype(F32)


def _attn_bias_spec(n_tokens):
    nb = n_tokens // BLOCK
    return pl.BlockSpec((1, GROUP_ROWS, 3 * BLOCK), lambda n: (jnp.where(n == 0, 0, jnp.where(n == nb - 1, 2, 1)), 0, 0))


def _attn_setup(sink):
    row = lax.broadcasted_iota(jnp.int32, (GROUP_ROWS, 1), 0)
    group = sum((row >= g * BLOCK).astype(jnp.int32) for g in range(1, ATT_GROUP))
    head_id = lax.broadcasted_iota(jnp.int32, (1, ATT_HEADS), 1)
    sks = []
    for h in range(ATT_KV_HEADS):
        sk = jnp.zeros((GROUP_ROWS, 1), F32)
        for g in range(ATT_GROUP):
            one = jnp.sum(jnp.where(head_id == h * ATT_GROUP + g, sink, 0.0), axis=-1, keepdims=True)
            sk = jnp.where(group == g, one, sk)
        sks.append(sk)
    return group, sks


def _attn_weights(q, kw, kc, sk, bias):
    q = q * ATT_SCALE
    s_w = _raw_dot("nt", q, kw, False) + bias
    s_c = _raw_dot("nt", q, kc, False)
    m = jnp.maximum(jnp.maximum(jnp.max(s_w, axis=-1, keepdims=True), jnp.max(s_c, axis=-1, keepdims=True)), sk)
    pw, pc, ps = jnp.exp(s_w - m), jnp.exp(s_c - m), jnp.exp(sk - m)
    return q, pw, pc, ps, jnp.sum(pw, axis=-1, keepdims=True) + jnp.sum(pc, axis=-1, keepdims=True) + ps


def _f_attn(qs, kws, vws, kcs, vcs, sink, bias):
    _, sks = _attn_setup(sink)
    outs = []
    for h in range(ATT_KV_HEADS):
        _, pw, pc, _, den = _attn_weights(qs[h], kws[h], kcs[h], sks[h], bias)
        outs.append((_raw_dot("nn", pw, vws[h], False) + _raw_dot("nn", pc, vcs[h], False)) / den)
    return tuple(outs)


def _f_attn_bwd(qs, kws, vws, kcs, vcs, sink, bias, outs, douts):
    group, sks = _attn_setup(sink)
    head_id = lax.broadcasted_iota(jnp.int32, (1, ATT_HEADS), 1)
    dot = lambda mode, a, b: _raw_dot(mode, a, b, False)
    dqs, dkws, dvws, dkcs, dvcs, dsink = [], [], [], [], [], jnp.zeros((1, ATT_HEADS), F32)
    for h in range(ATT_KV_HEADS):
        q, pw, pc, ps, den = _attn_weights(qs[h], kws[h], kcs[h], sks[h], bias)
        inv = 1.0 / den
        pw, pc = pw * inv, pc * inv
        dd = jnp.sum(douts[h] * outs[h], axis=-1, keepdims=True)
        dsw = pw * (dot("nt", douts[h], vws[h]) - dd)
        dsc = pc * (dot("nt", douts[h], vcs[h]) - dd)
        dqs.append((dot("nn", dsw, kws[h]) + dot("nn", dsc, kcs[h])) * ATT_SCALE)
        dkws.append(dot("tn", dsw, q))
        dkcs.append(dot("tn", dsc, q))
        dvws.append(dot("tn", pw, douts[h]))
        dvcs.append(dot("tn", pc, douts[h]))
        dsk = -(ps * inv) * dd
        for g in range(ATT_GROUP):
            one = jnp.sum(jnp.where(group == g, dsk, 0.0), axis=0, keepdims=True)
            dsink = dsink + jnp.where(head_id == h * ATT_GROUP + g, one, 0.0)
    return dqs, dkws, dvws, dkcs, dvcs, dsink


def _group_rows(ref, h):
    hs = lambda hq: slice(hq * HEAD_DIM, (hq + 1) * HEAD_DIM)
    return jnp.concatenate([ref[:, hs(h * ATT_GROUP + g)].astype(F32) for g in range(ATT_GROUP)], axis=0)


def _ungroup_rows(ref, h, val):
    for g in range(ATT_GROUP):
        hq = h * ATT_GROUP + g
        ref[:, hq * HEAD_DIM:(hq + 1) * HEAD_DIM] = val[g * BLOCK:(g + 1) * BLOCK].astype(ref.dtype)


def _attn_loads(n, q_ref, kp_ref, vp_ref, kc_ref, vc_ref):
    r0 = pl.multiple_of(n * BLOCK, BLOCK)
    hs = lambda h: slice(h * HEAD_DIM, (h + 1) * HEAD_DIM)
    qs = [_group_rows(q_ref, h) for h in range(ATT_KV_HEADS)]
    kws = [kp_ref[pl.ds(r0, 3 * BLOCK), hs(h)].astype(F32) for h in range(ATT_KV_HEADS)]
    vws = [vp_ref[pl.ds(r0, 3 * BLOCK), hs(h)].astype(F32) for h in range(ATT_KV_HEADS)]
    kcs = [kc_ref[:, hs(h)].astype(F32) for h in range(ATT_KV_HEADS)]
    vcs = [vc_ref[:, hs(h)].astype(F32) for h in range(ATT_KV_HEADS)]
    return r0, hs, qs, kws, vws, kcs, vcs


def _attn_specs(s, c):
    full = lambda shape: pl.BlockSpec(shape, lambda n: (0, 0))
    return [pl.BlockSpec((BLOCK, QW), lambda n: (n, 0)), full((s + 2 * BLOCK, KVW)), full((s + 2 * BLOCK, KVW)),
            full((c, KVW)), full((c, KVW)), full((1, ATT_HEADS)), _attn_bias_spec(s)]


def _attn_fwd(q, kp, vp, kc, vc, sink):
    s, c = q.shape[0], kc.shape[0]

    def body(q_ref, kp_ref, vp_ref, kc_ref, vc_ref, sink_ref, bias_ref, o_ref):
        n = pl.program_id(0)
        _, hs, qs, kws, vws, kcs, vcs = _attn_loads(n, q_ref, kp_ref, vp_ref, kc_ref, vc_ref)
        outs = _f_attn(qs, kws, vws, kcs, vcs, sink_ref[...], bias_ref[0])
        for h in range(ATT_KV_HEADS):
            _ungroup_rows(o_ref, h, outs[h])

    return pl.pallas_call(
        body, name="attn_fwd", grid=(s // BLOCK,), in_specs=_attn_specs(s, c),
        out_specs=pl.BlockSpec((BLOCK, QW), lambda n: (n, 0)), out_shape=jax.ShapeDtypeStruct((s, QW), BF16),
        compiler_params=_cp("parallel"),
    )(q, kp, vp, kc, vc, sink, _attn_bias(s))


def _attn_bwd(do, o, q, kp, vp, kc, vc, sink):
    s, c = q.shape[0], kc.shape[0]

    def body(do_ref, o_ref, q_ref, kp_ref, vp_ref, kc_ref, vc_ref, sink_ref, bias_ref, dq_ref, dkp_ref, dvp_ref,
             dkc_ref, dvc_ref, dsink_ref):
        n = pl.program_id(0)

        @pl.when(n == 0)
        def _():
            for r in (dkp_ref, dvp_ref, dkc_ref, dvc_ref, dsink_ref):
                r[...] = jnp.zeros_like(r)

        r0, hs, qs, kws, vws, kcs, vcs = _attn_loads(n, q_ref, kp_ref, vp_ref, kc_ref, vc_ref)
        heads = range(ATT_KV_HEADS)
        dqs, dkws, dvws, dkcs, dvcs, dsink = _f_attn_bwd(
            qs, kws, vws, kcs, vcs, sink_ref[...], bias_ref[0], [_group_rows(o_ref, h) for h in heads],
            [_group_rows(do_ref, h) for h in heads])
        for h in heads:
            _ungroup_rows(dq_ref, h, dqs[h])
            dkp_ref[pl.ds(r0, 3 * BLOCK), hs(h)] += dkws[h]
            dvp_ref[pl.ds(r0, 3 * BLOCK), hs(h)] += dvws[h]
            dkc_ref[:, hs(h)] += dkcs[h]
            dvc_ref[:, hs(h)] += dvcs[h]
        dsink_ref[...] += dsink

    full = lambda shape: pl.BlockSpec(shape, lambda n: (0, 0))
    return pl.pallas_call(
        body, name="attn_bwd", grid=(s // BLOCK,),
        in_specs=[pl.BlockSpec((BLOCK, QW), lambda n: (n, 0))] * 2 + _attn_specs(s, c),
        out_specs=[pl.BlockSpec((BLOCK, QW), lambda n: (n, 0)), full((s + 2 * BLOCK, KVW)), full((s + 2 * BLOCK, KVW)),
                   full((c, KVW)), full((c, KVW)), full((1, ATT_HEADS))],
        out_shape=[jax.ShapeDtypeStruct((s, QW), BF16), jax.ShapeDtypeStruct((s + 2 * BLOCK, KVW), F32),
                   jax.ShapeDtypeStruct((s + 2 * BLOCK, KVW), F32), jax.ShapeDtypeStruct((c, KVW), F32),
                   jax.ShapeDtypeStruct((c, KVW), F32), jax.ShapeDtypeStruct((1, ATT_HEADS), F32)],
        compiler_params=_cp("arbitrary"),
    )(do, o, q, kp, vp, kc, vc, sink, _attn_bias(s))


GLA_GROUPS = 1
GLA_GROUP_HEADS = GLA_HEADS // GLA_GROUPS
GKG, GVG = GKW // GLA_GROUPS, GVW // GLA_GROUPS


def _gla_masks(heads=GLA_HEADS):
    hk = np.arange(heads * GLA_DK) // GLA_DK
    hv = np.arange(heads * GLA_DV) // GLA_DV
    head_k = (np.arange(heads)[:, None] == hk[None, :]).astype(np.float32)
    head_v = (np.arange(heads)[:, None] == hv[None, :]).astype(np.float32)
    bd_t = (hv[:, None] == hk[None, :]).astype(np.float32)
    return jnp.asarray(head_k), jnp.asarray(head_v), jnp.asarray(bd_t)


def _group_states(st):
    return jnp.stack([st[g * GVG:(g + 1) * GVG, g * GKG:(g + 1) * GKG] for g in range(GLA_GROUPS)])


def _ungroup_states(st):
    out = jnp.zeros((GVW, GKW), st.dtype)
    for g in range(GLA_GROUPS):
        out = out.at[g * GVG:(g + 1) * GVG, g * GKG:(g + 1) * GKG].set(st[g])
    return out


def _tri(n, rev, strict=False):
    i = lax.broadcasted_iota(jnp.int32, (n, n), 0)
    j = lax.broadcasted_iota(jnp.int32, (n, n), 1)
    if strict:
        keep = (j > i) if rev else (j < i)
    else:
        keep = (j >= i) if rev else (j <= i)
    return keep


def _f_gla_chunk(q, k, v, la, st, head_k, head_v, bd_t, rev):
    return _f_gla_carry(*_f_gla_intra(q, k, v, la, head_k, head_v, rev), v, st, bd_t)


def _f_gla_intra(q, k, v, la, head_k, head_v, rev):
    heads, kw, vw = head_k.shape[0], q.shape[1], v.shape[1]
    keep = _tri(GLA_CHUNK, rev)
    b = _nn_mask(keep.astype(F32), la)
    bl = jnp.sum(la, axis=0, keepdims=True)
    qd = q * (GLA_DK ** -0.5) * jnp.exp(b)
    ki = k * jnp.exp(-b)
    kd = k * jnp.exp(bl - b)
    q_heads = (qd[None, :, :] * head_k[:, None, :]).reshape(heads * GLA_CHUNK, kw)
    a_all = _nt(q_heads, ki).reshape(heads, GLA_CHUNK, GLA_CHUNK)
    a_all = jnp.where(keep[None, :, :], a_all, 0.0).reshape(heads * GLA_CHUNK, GLA_CHUNK)
    o_all = _nn(a_all, v).reshape(heads, GLA_CHUNK, vw)
    return jnp.sum(o_all * head_v[:, None, :], axis=0), qd, kd, bl


def _f_gla_carry(intra, qd, kd, bl, v, st, bd_t):
    return intra + _nt(qd, st), st * jnp.exp(bl) + bd_t * _tn(v, kd)


def _gla_specs(s, tb, order):
    return [pl.BlockSpec((tb, GKW), lambda i: (order(i), C_GQ // GKW)),
            pl.BlockSpec((tb, GKW), lambda i: (order(i), C_GK // GKW)),
            pl.BlockSpec((tb, GVW), lambda i: (order(i), C_GV // GVW)),
            pl.BlockSpec((tb, GKW), lambda i: (order(i), 0))]


GLA_BLOCK_CHUNKS = 4


def _gla_fwd(p, la_f, la_b, st_f0, st_b0):
    s = p.shape[0]
    tb = GLA_BLOCK_CHUNKS * GLA_CHUNK
    nblk = s // tb
    up, down = (lambda i: i), (lambda i: nblk - 1 - i)
    masks = _gla_masks(GLA_GROUP_HEADS)

    def scan(rev, q_ref, k_ref, v_ref, la_ref, o_ref, sts_ref, st_ref, consts):
        for g in range(GLA_GROUPS):
            gk, gv = slice(g * GKG, (g + 1) * GKG), slice(g * GVG, (g + 1) * GVG)
            st = st_ref[g]
            sts_ref[0, g] = st
            chunks = range(GLA_BLOCK_CHUNKS)
            for ci in (reversed(chunks) if rev else chunks):
                rows = slice(ci * GLA_CHUNK, (ci + 1) * GLA_CHUNK)
                o, st = _f_gla_chunk(q_ref[rows, gk], k_ref[rows, gk], v_ref[rows, gv], la_ref[rows, gk], st, *consts,
                                     rev)
                o_ref[rows, gv] = o
            st_ref[g] = st

    def body(qf, kf, vf, laf, qb, kb, vb, lab, stf0, stb0, hk_ref, hv_ref, bd_ref, of_ref, stsf_ref, ob_ref, stsb_ref,
             stf_ref, stb_ref):
        @pl.when(pl.program_id(0) == 0)
        def _():
            stf_ref[...] = stf0[...]
            stb_ref[...] = stb0[...]

        consts = (hk_ref[...], hv_ref[...], bd_ref[...])
        scan(False, qf, kf, vf, laf, of_ref, stsf_ref, stf_ref, consts)
        scan(True, qb, kb, vb, lab, ob_ref, stsb_ref, stb_ref, consts)

    full = lambda a: pl.BlockSpec(a.shape, lambda i: (0,) * a.ndim)
    outs = lambda order: [pl.BlockSpec((tb, GVW), lambda i: (order(i), 0)),
                          pl.BlockSpec((1, GLA_GROUPS, GVG, GKG), lambda i: (order(i), 0, 0, 0))]
    return pl.pallas_call(
        body, name="gla_fwd", grid=(nblk,),
        in_specs=_gla_specs(s, tb, up) + _gla_specs(s, tb, down) + [full(st_f0), full(st_b0)]
        + [full(m) for m in masks],
        out_specs=outs(up) + outs(down),
        out_shape=[jax.ShapeDtypeStruct((s, GVW), F32), jax.ShapeDtypeStruct((nblk, GLA_GROUPS, GVG, GKG), F32)] * 2,
        scratch_shapes=[pltpu.VMEM((GLA_GROUPS, GVG, GKG), F32)] * 2,
        compiler_params=_cp("arbitrary"),
    )(p, p, p, la_f, p, p, p, la_b, st_f0, st_b0, *masks)


def _gla_bwd(p, la_f, la_b, sts_f, sts_b, do, after=None):
    s = p.shape[0]
    tb = GLA_BLOCK_CHUNKS * GLA_CHUNK
    nblk = s // tb
    up, down = (lambda i: i), (lambda i: nblk - 1 - i)
    masks = _gla_masks(GLA_GROUP_HEADS)
    follow = () if after is None else (after,)

    def back(rev, q_ref, k_ref, v_ref, la_ref, sts_ref, do_ref, dq_ref, dk_ref, dv_ref, dla_ref, dst0_ref, dst_ref,
             consts):
        def block(q, k, v, la, st):
            outs = [None] * GLA_BLOCK_CHUNKS
            chunks = range(GLA_BLOCK_CHUNKS)
            for ci in (reversed(chunks) if rev else chunks):
                outs[ci], st = _f_gla_chunk(q[ci], k[ci], v[ci], la[ci], st, *consts, rev)
            return tuple(outs), st

        for g in range(GLA_GROUPS):
            gk, gv = slice(g * GKG, (g + 1) * GKG), slice(g * GVG, (g + 1) * GVG)
            split = lambda r, cols: tuple(r[ci * GLA_CHUNK:(ci + 1) * GLA_CHUNK, cols].astype(F32)
                                          for ci in range(GLA_BLOCK_CHUNKS))
            _, vjp = jax.vjp(block, split(q_ref, gk), split(k_ref, gk), split(v_ref, gv), split(la_ref, gk),
                             sts_ref[0, g])
            dq, dk, dv, dla, dst = vjp((split(do_ref, gv), dst_ref[g]))
            for ci in range(GLA_BLOCK_CHUNKS):
                rows = slice(ci * GLA_CHUNK, (ci + 1) * GLA_CHUNK)
                dq_ref[rows, gk], dk_ref[rows, gk] = dq[ci].astype(BF16), dk[ci].astype(BF16)
                dv_ref[rows, gv], dla_ref[rows, gk] = dv[ci].astype(BF16), dla[ci]
            dst_ref[g] = dst
            dst0_ref[g] = dst

    def body(*refs):
        ins, (hk_ref, hv_ref, bd_ref) = refs[:12], refs[12:15]
        outs = refs[15 + len(follow):]

        @pl.when(pl.program_id(0) == 0)
        def _():
            outs[10][...] = jnp.zeros_like(outs[10])
            outs[11][...] = jnp.zeros_like(outs[11])

        consts = (hk_ref[...], hv_ref[...], bd_ref[...])
        back(False, *ins[:6], *outs[:5], outs[10], consts)
        back(True, *ins[6:], *outs[5:10], outs[11], consts)

    full = lambda a: pl.BlockSpec(a.shape, lambda i: (0,) * a.ndim)

    def ins(order):
        return _gla_specs(s, tb, order) + [pl.BlockSpec((1, GLA_GROUPS, GVG, GKG), lambda i: (order(i), 0, 0, 0)),
                                           pl.BlockSpec((tb, GVW), lambda i: (order(i), 0))]

    def outs(order):
        blk = lambda w: pl.BlockSpec((tb, w), lambda i: (order(i), 0))
        return [blk(GKW), blk(GKW), blk(GVW), blk(GKW), pl.BlockSpec((GLA_GROUPS, GVG, GKG), lambda i: (0, 0, 0))]

    shapes = [jax.ShapeDtypeStruct((s, GKW), BF16), jax.ShapeDtypeStruct((s, GKW), BF16),
              jax.ShapeDtypeStruct((s, GVW), BF16), jax.ShapeDtypeStruct((s, GKW), F32),
              jax.ShapeDtypeStruct((GLA_GROUPS, GVG, GKG), F32)]
    both = pl.pallas_call(
        body, name="gla_bwd", grid=(nblk,),
        in_specs=ins(down) + ins(up) + [full(m) for m in masks] + [pl.BlockSpec(memory_space=pl.ANY)] * len(follow),
        out_specs=outs(down) + outs(up), out_shape=shapes * 2,
        scratch_shapes=[pltpu.VMEM((GLA_GROUPS, GVG, GKG), F32)] * 2,
        compiler_params=_cp("arbitrary"),
    )(p, p, p, la_f, sts_f, do, p, p, p, la_b, sts_b, do, *masks, *follow)
    return both[:5], both[5:]


def _f_ctx_state(k, v, la_f, la_b, bd_t):
    c = k.shape[0]
    after = _nn_mask(_tri(c, True, strict=True).astype(F32), la_f)
    before = _nn_mask(_tri(c, False, strict=True).astype(F32), la_b)
    return bd_t * _tn(v, k * jnp.exp(after)), bd_t * _tn(v, k * jnp.exp(before))


def _ctx_state(pc, la_f, la_b):
    c = pc.shape[0]
    bd_t = _gla_masks()[2]

    def body(k_ref, v_ref, lf_ref, lb_ref, bd_ref, sf_ref, sb_ref):
        sf_ref[...], sb_ref[...] = _f_ctx_state(k_ref[...], v_ref[...], lf_ref[...], lb_ref[...], bd_ref[...])

    full = lambda a: pl.BlockSpec(a.shape, lambda i: (0, 0))
    return pl.pallas_call(
        body, name="ctx_state_fwd", grid=(1,),
        in_specs=[pl.BlockSpec((c, GKW), lambda i: (0, C_GK // GKW)), pl.BlockSpec((c, GVW), lambda i: (0, C_GV // GVW)),
                  full(la_f), full(la_b), full(bd_t)],
        out_specs=[pl.BlockSpec((GVW, GKW), lambda i: (0, 0))] * 2,
        out_shape=[jax.ShapeDtypeStruct((GVW, GKW), F32)] * 2,
        compiler_params=_cp("arbitrary"),
    )(pc, pc, la_f, la_b, bd_t)


def _ctx_state_bwd(pc, la_f, la_b, dsf, dsb):
    c = pc.shape[0]
    bd_t = _gla_masks()[2]

    def body(k_ref, v_ref, lf_ref, lb_ref, bd_ref, dsf_ref, dsb_ref, dk_ref, dv_ref, dlf_ref, dlb_ref):
        _, vjp = jax.vjp(lambda k, v, lf, lb: _f_ctx_state(k, v, lf, lb, bd_ref[...]),
                         k_ref[...], v_ref[...], lf_ref[...], lb_ref[...])
        dk, dv, dlf, dlb = vjp((dsf_ref[...], dsb_ref[...]))
        dk_ref[...], dv_ref[...] = dk.astype(BF16), dv.astype(BF16)
        dlf_ref[...], dlb_ref[...] = dlf, dlb

    full = lambda a: pl.BlockSpec(a.shape, lambda i: (0, 0))
    return pl.pallas_call(
        body, name="ctx_state_bwd", grid=(1,),
        in_specs=[pl.BlockSpec((c, GKW), lambda i: (0, C_GK // GKW)), pl.BlockSpec((c, GVW), lambda i: (0, C_GV // GVW)),
                  full(la_f), full(la_b), full(bd_t), full(dsf), full(dsb)],
        out_specs=[pl.BlockSpec((c, GKW), lambda i: (0, 0)), pl.BlockSpec((c, GVW), lambda i: (0, 0)),
                   pl.BlockSpec((c, GKW), lambda i: (0, 0)), pl.BlockSpec((c, GKW), lambda i: (0, 0))],
        out_shape=[jax.ShapeDtypeStruct((c, GKW), BF16), jax.ShapeDtypeStruct((c, GVW), BF16),
                   jax.ShapeDtypeStruct((c, GKW), F32), jax.ShapeDtypeStruct((c, GKW), F32)],
        compiler_params=_cp("arbitrary"),
    )(pc, pc, la_f, la_b, bd_t, dsf, dsb)


_SRC_COLS = ((0, QW), (QW + 2 * KVW + 2 * GKW, GVW), (QW + 2 * KVW + 2 * GKW + GVW, GVW), (QW, KVW), (QW + KVW, KVW),
             (QW + 2 * KVW, GKW), (QW + 2 * KVW + GKW, GKW), (IN_COLS - 2 * GATE_RANK, 2 * GATE_RANK))
_DST_COLS = (C_Q, C_GV, C_GG, C_K, C_V, C_GQ, C_GK, C_Z)


def _pack_w_in(w_in):
    parts = [w_in[:, s:s + n] for s, n in _SRC_COLS]
    parts.append(jnp.zeros((w_in.shape[0], IN_PAD - C_Z - 2 * GATE_RANK), w_in.dtype))
    return jnp.concatenate(parts, axis=1)


def _unpack_w_in_grad(g):
    by_src = sorted(zip(_SRC_COLS, _DST_COLS))
    return jnp.concatenate([g[:, d:d + n] for (_, n), d in by_src], axis=1)


def _prep_gate_weights(w_gate_fwd, w_gate_bwd):
    pad_rows = lambda w, at: jnp.zeros((LANES, GKW), F32).at[at:at + GATE_RANK].set(w)
    return {"wg_f": pad_rows(w_gate_fwd, 0), "wg_b": pad_rows(w_gate_bwd, GATE_RANK)}


def _local_step(x, ctx, target, ada, ada_c, w, late_weights, reduce_behind=None, reduce_w_in=None):
    s, d = x.shape
    sh1, sc1, gt1, sh2, sc2, gt2 = [ada[:, i * d:(i + 1) * d] for i in range(6)]
    sh1c, sc1c = ada_c[:, :d], ada_c[:, d:2 * d]
    cos, sin = _rope_tables(s)
    gt = jnp.tile(w["g_gla_norm"], (1, GLA_HEADS))

    h = _norm_mod("pre_mix", x, w["g_pre_mix"], sh1, sc1)
    hc = _norm_mod("pre_mix_ctx", ctx, w["g_pre_mix"], sh1c, sc1c)
    w_in, token = w["w_in"](h, cos, sin)
    p = _mm("proj_in", h, w_in, "nn", after=token)
    pc = _mm("proj_in_ctx", hc, w_in, "nn")
    q_rot, k_rot, v_b = _rope_fwd("rope", p, cos, sin)
    pad = ((BLOCK, BLOCK), (0, 0))
    kp, vp = jnp.pad(k_rot, pad), jnp.pad(v_b, pad)
    kc, vc = pc[:, C_K:C_K + KVW].astype(BF16), pc[:, C_V:C_V + KVW].astype(BF16)
    attn = _attn_fwd(q_rot, kp, vp, kc, vc, w["attn_sink"])
    gate_w = (w["wg_f"], w["wg_b"], w["b_gate_fwd"], w["b_gate_bwd"])
    la_f, la_b = _gate_fwd("gate", p, *gate_w)
    la_fc, la_bc = _gate_fwd("gate_ctx", pc, *gate_w)
    st_f0, st_b0 = _ctx_state(pc, la_fc, la_bc)
    o_f, sts_f, o_b, sts_b = _gla_fwd(p, la_f, la_b, _group_states(st_f0), _group_states(st_b0))
    mix = _gla_out("gla_out", attn, o_f, o_b, p, gt)
    w_out, w_ffn_in_t, w_ffn_out = late_weights(mix)
    y = _mm("proj_out", mix, w_out, "nn", BF16)
    x1, h2 = _post_res_norm_mod("post_mix_pre_ffn", x, y, w["g_post_mix"], gt1, w["g_pre_ffn"], sh2, sc2)
    u, a = _ffn_in_swiglu("ffn_in", h2, w_ffn_in_t)
    f = _mm("ffn_out", a, w_ffn_out, "nn", BF16)
    g = {}
    dx2, df, loss, g["g_post_ffn"], dgt2 = _post_res_loss("post_ffn_loss", x1, f, w["g_post_ffn"], gt2, target)

    late_rows = {"w_ffn_in_t": w_ffn_in_t.shape[0] // N_CHIP, "w_ffn_out": w_ffn_out.shape[0] // N_CHIP,
                 "w_out": w_out.shape[0] // N_CHIP}
    order = sorted(late_rows, key=lambda n: -late_rows[n])
    offsets, slab_rows = _slab_layout([late_rows[n] for n in order])
    late_at, slab_shape = dict(zip(order, offsets)), (N_CHIP, slab_rows, d)
    slab = _slab_zero_gaps("late_grads_gaps", slab_shape, [late_rows[n] for n in order], offsets)
    slab = _dw_into_slab("ffn_out_dw", a, df, slab, slab_shape, late_at["w_ffn_out"])
    du = _ffn_out_dx_swiglu_bwd("ffn_out_dx", df, w_ffn_out, u)
    dh2 = _mm("ffn_in_dx", du, w_ffn_in_t, "nn", BF16)
    slab = _dw_into_slab("ffn_in_dw", du, h2, slab, slab_shape, late_at["w_ffn_in_t"])
    dx1, dy, g["g_pre_ffn"], dsh2, dsc2, g["g_post_mix"], dgt1 = _norm_mod_post_res_bwd(
        "pre_ffn_post_mix_bwd", dh2, dx2, x1, y, w["g_pre_ffn"], sh2, sc2, w["g_post_mix"], gt1)
    dmix = _mm("proj_out_dx", dy, w_out, "nt", BF16)
    slab = _dw_into_slab("proj_out_dw", mix, dy, slab, slab_shape, late_at["w_out"])
    g["late"], g["late_at"], g["late_rows"] = slab, late_at, late_rows
    rb, sink, token = reduce_behind, w["attn_sink"], None
    if rb is not None:
        gt = _behind(gt, rb.start_slab(slab))
    d_o, dgg, dgt = _gla_out_bwd("gla_out_bwd", dmix, o_f, o_b, p, gt)
    g["g_gla_norm"] = jnp.sum(dgt.reshape(GLA_HEADS, GLA_DV), axis=0, keepdims=True)
    if rb is not None:
        token = rb.pair(dgg)
    gla_f, gla_b = _gla_bwd(p, la_f, la_b, sts_f, sts_b, d_o, token)
    (dla_f, dst_f0), (dla_b, dst_b0) = gla_f[3:], gla_b[3:]
    dst_f0, dst_b0 = _ungroup_states(dst_f0), _ungroup_states(dst_b0)
    if rb is not None:
        sink = _behind(sink, rb.total(dla_b))
    dgkc, dgvc, dla_fc, dla_bc = _ctx_state_bwd(pc, la_fc, la_bc, dst_f0, dst_b0)
    dz, dwf, dwb, dbf, dbb = _gate_bwd("gate_bwd", p, dla_f, dla_b, *gate_w)
    dzc, dwfc, dwbc, dbfc, dbbc = _gate_bwd("gate_ctx_bwd", pc, dla_fc, dla_bc, *gate_w)
    g["w_gate_fwd"] = (dwf + dwfc)[:GATE_RANK]
    g["w_gate_bwd"] = (dwb + dwbc)[GATE_RANK:2 * GATE_RANK]
    g["b_gate_fwd"], g["b_gate_bwd"] = dbf + dbfc, dbb + dbbc
    dq_rot, dkp, dvp, dkc, dvc, g["attn_sink"] = _attn_bwd(dmix, attn, q_rot, kp, vp, kc, vc, sink)
    if rb is not None:
        g["late"] = rb.result(dq_rot)
    dp = _proj_grad("proj_grad", dq_rot, dkp[BLOCK:BLOCK + s], dvp[BLOCK:BLOCK + s], cos, sin, gla_f[:3], gla_b[:3],
                    dgg, dz)
    c_rows = ctx.shape[0]
    zeros = lambda n: jnp.zeros((c_rows, n), BF16)
    dpc = jnp.concatenate([zeros(QW), dgvc, zeros(GVW), dkc.astype(BF16), dvc.astype(BF16), zeros(GKW), dgkc, dzc],
                          axis=1)
    g["w_in"] = _mm("proj_in_dw", h, dp, "tn", init=_mm("proj_in_ctx_dw", hc, dpc, "tn"))
    token = None if reduce_w_in is None else reduce_w_in.start(g["w_in"])
    dh = _mm("proj_in_dx", dp, w_in, "nt", BF16, after=token)
    dhc = _mm("proj_in_ctx_dx", dpc, w_in, "nt")
    if reduce_w_in is not None:
        sh1 = _behind(sh1, reduce_w_in.pair(dh))
    dx, dg_a, dsh1, dsc1 = _norm_mod_bwd("pre_mix_bwd", dh, dx1, x, w["g_pre_mix"], sh1, sc1)
    if reduce_w_in is not None:
        dsh1 = _behind(dsh1, reduce_w_in.total(dx))
    _, dg_b, dsh1c, dsc1c = _norm_mod_bwd("pre_mix_ctx_bwd", dhc, jnp.zeros_like(dhc), ctx, w["g_pre_mix"], sh1c,
                                          sc1c)
    g["g_pre_mix"] = dg_a + dg_b
    d_ada = jnp.concatenate([dsh1, dsc1, dgt1, dsh2, dsc2, dgt2], axis=1)
    d_ada_c = jnp.concatenate([dsh1c, dsc1c, jnp.zeros((1, 4 * d), F32)], axis=1)
    return loss, dx, g, d_ada, d_ada_c


HBM = pl.BlockSpec(memory_space=pltpu.HBM)
N_DEV, N_CHIP = 8, 4


def _place():
    x, y, c = lax.axis_index("x"), lax.axis_index("y"), lax.axis_index("c")
    return x, y, c, [(1 - x, y), (x, 1 - y), (1 - x, 1 - y)]


def _row_tile(n, mult, cap):
    return max(t for t in range(mult, min(n, cap) + 1, mult) if n % t == 0)


def _ag_small(name, v, after=None):
    follow = () if after is None else (after,)

    def body(v_ref, *rest):
        out_ref, send_sems, recv_sems = rest[len(follow):]
        x, y, c, _ = _place()
        out_ref[4 * x + 2 * y + c] = v_ref[...]

        def peer(r):
            return ((1 - x) if r & 4 else x, (1 - y) if r & 2 else y, (1 - c) if r & 1 else c)

        def copy(r, block):
            px, py, pc = block
            return pltpu.make_async_remote_copy(
                src_ref=v_ref, dst_ref=out_ref.at[4 * px + 2 * py + pc], send_sem=send_sems.at[r - 1],
                recv_sem=recv_sems.at[r - 1], device_id=peer(r), device_id_type=MESH)

        sends = [copy(r, (x, y, c)) for r in range(1, N_DEV)]
        for cp in sends:
            cp.start()
        for r in range(1, N_DEV):
            copy(r, peer(r)).wait_recv()
        for cp in sends:
            cp.wait_send()

    return pl.pallas_call(
        body, name=name, out_shape=jax.ShapeDtypeStruct((N_DEV,) + v.shape, v.dtype),
        in_specs=[pl.BlockSpec(memory_space=pltpu.VMEM)] + [pl.BlockSpec(memory_space=pl.ANY)] * len(follow),
        out_specs=pl.BlockSpec(memory_space=pltpu.VMEM),
        scratch_shapes=[pltpu.SemaphoreType.DMA((N_DEV - 1,)), pltpu.SemaphoreType.DMA((N_DEV - 1,))],
    )(v, *follow)


def _halves(c, rows, mult):
    hr = rows // 2
    return pl.ds(pl.multiple_of(c * hr, mult), hr), pl.ds(pl.multiple_of((1 - c) * hr, mult), hr)


def _add_half(name, g, a, c_idx):
    n_sh, hr, n = a.shape
    tr = _row_tile(hr, 16, 1024)
    nb = hr // tr

    def body(c_ref, g_ref, a_ref, o_ref):
        o_ref[...] = (g_ref[...] + a_ref[...]).astype(o_ref.dtype)

    return pl.pallas_call(
        body, name=name, out_shape=jax.ShapeDtypeStruct(a.shape, BF16),
        grid_spec=pltpu.PrefetchScalarGridSpec(
            num_scalar_prefetch=1, grid=(n_sh, nb),
            in_specs=[pl.BlockSpec((1, tr, n), lambda s, i, c_ref: (s, c_ref[0] * nb + i, 0)),
                      pl.BlockSpec((1, tr, n), lambda s, i, c_ref: (s, i, 0))],
            out_specs=pl.BlockSpec((1, tr, n), lambda s, i, c_ref: (s, i, 0))),
        compiler_params=_cp("parallel", "parallel"),
    )(c_idx, g, a)


def _sum_chips(name, b, c_idx):
    n_sh, hr, n = b.shape
    tr = _row_tile(hr, 16, 1024)
    nb = hr // tr

    def body(c_ref, b0, b1, b2, b3, o_ref):
        o_ref[...] = ((b0[0].astype(F32) + b1[0].astype(F32)) + b2[0].astype(F32)) + b3[0].astype(F32)

    return pl.pallas_call(
        body, name=name, out_shape=jax.ShapeDtypeStruct((2 * hr, n), F32),
        grid_spec=pltpu.PrefetchScalarGridSpec(
            num_scalar_prefetch=1, grid=(nb,),
            in_specs=[pl.BlockSpec((1, tr, n), functools.partial(lambda i, c_ref, k: (k, i, 0), k=k))
                      for k in range(n_sh)],
            out_specs=pl.BlockSpec((tr, n), lambda i, c_ref: (c_ref[0] * nb + i, 0))),
        compiler_params=_cp("parallel"),
    )(c_idx, b, b, b, b)


SEM = pl.BlockSpec(memory_space=pltpu.SEMAPHORE)
ANY = pl.BlockSpec(memory_space=pl.ANY)
DATAFLOW = pltpu.SideEffectType.DATAFLOW_SIDE_EFFECTING


def _remote(src, dst, send_sems, recv_sems, k, to):
    return pltpu.make_async_remote_copy(src_ref=src, dst_ref=dst, send_sem=send_sems.at[k], recv_sem=recv_sems.at[k],
                                        device_id=to, device_id_type=MESH)


def _split_copy(name, src, land_shape, land_dtype, n, plan, after=None):
    after = jnp.zeros((8, LANES), F32) if after is None else after

    def start_body(src_ref, land_ref, after_ref, send_sems, recv_sems, src_thru, land_thru, token):
        for cp in plan(src_ref, land_ref, send_sems, recv_sems)[0]:
            cp.start()
        token[...] = jnp.zeros_like(token)

    sems = pltpu.SemaphoreType.DMA((n,))
    send_sems, recv_sems, src_thru, land_thru, token = pl.pallas_call(
        start_body, name=name + "_start",
        out_shape=(sems, sems, pltpu.HBM(src.shape, src.dtype), pltpu.HBM(land_shape, land_dtype),
                   jax.ShapeDtypeStruct((8, LANES), F32)),
        in_specs=(HBM, HBM, ANY), out_specs=(SEM, SEM, HBM, HBM, pl.BlockSpec(memory_space=pltpu.VMEM)),
        input_output_aliases={0: 2, 1: 3}, compiler_params=pltpu.CompilerParams(has_side_effects=DATAFLOW),
    )(pltpu.with_memory_space_constraint(src, pltpu.HBM),
      pltpu.with_memory_space_constraint(lax.empty(land_shape, land_dtype), pltpu.HBM), after)

    def wait(*after):
        def wait_body(src_ref, land_ref, send_sems, recv_sems, *rest):
            sent, received = plan(src_ref, land_ref, send_sems, recv_sems)
            for cp in sent:
                cp.wait_send()
            for cp in received:
                cp.wait_recv()

        return pl.pallas_call(
            wait_body, name=name + "_wait",
            out_shape=(pltpu.HBM(src.shape, src.dtype), pltpu.HBM(land_shape, land_dtype)),
            in_specs=(HBM, HBM, SEM, SEM) + (ANY,) * len(after), out_specs=(HBM, HBM),
            input_output_aliases={0: 0, 1: 1}, compiler_params=pltpu.CompilerParams(has_side_effects=DATAFLOW),
        )(src_thru, land_thru, send_sems, recv_sems, *after)

    return token, wait


def _split_gather(name, shards, after):
    k, n, plan = len(shards), 3 * len(shards), _plan_gather

    def start_body(*refs):
        for cp in plan(refs[:k], refs[k:2 * k], refs[2 * k + 1], refs[2 * k + 2])[0]:
            cp.start()
        refs[-1][...] = jnp.zeros_like(refs[-1])

    sems = pltpu.SemaphoreType.DMA((n,))
    bufs = [pltpu.HBM(s.shape, s.dtype) for s in shards] + [pltpu.HBM((N_CHIP,) + s.shape, s.dtype) for s in shards]
    hbm = lambda t: pltpu.with_memory_space_constraint(t, pltpu.HBM)
    outs = pl.pallas_call(
        start_body, name=name + "_start", out_shape=(sems, sems, *bufs, jax.ShapeDtypeStruct((8, LANES), F32)),
        in_specs=(HBM,) * (2 * k) + (ANY,),
        out_specs=(SEM, SEM) + (HBM,) * (2 * k) + (pl.BlockSpec(memory_space=pltpu.VMEM),),
        input_output_aliases={i: 2 + i for i in range(2 * k)},
        compiler_params=pltpu.CompilerParams(has_side_effects=DATAFLOW),
    )(*[hbm(s) for s in shards], *[hbm(lax.empty((N_CHIP,) + s.shape, s.dtype)) for s in shards], after)
    send_sems, recv_sems, thru, token = outs[0], outs[1], outs[2:2 + 2 * k], outs[-1]

    def wait(*after):
        def wait_body(*refs):
            sent, received = plan(refs[:k], refs[k:2 * k], refs[2 * k], refs[2 * k + 1])
            for cp in sent:
                cp.wait_send()
            for cp in received:
                cp.wait_recv()

        res = pl.pallas_call(
            wait_body, name=name + "_wait", out_shape=tuple(bufs),
            in_specs=(HBM,) * (2 * k) + (SEM, SEM) + (ANY,) * len(after), out_specs=(HBM,) * (2 * k),
            input_output_aliases={i: i for i in range(2 * k)},
            compiler_params=pltpu.CompilerParams(has_side_effects=DATAFLOW),
        )(*thru, send_sems, recv_sems, *after)
        return res[:k], res[k:]

    return token, wait


def _behind(x, token):
    return x + token[0, 0]


def _plan_gather(src_refs, land_refs, send_sems, recv_sems):
    x, y, c, chips = _place()
    pairs = list(enumerate(zip(src_refs, land_refs)))
    sent = [_remote(s, l.at[2 * x + y], send_sems, recv_sems, 3 * i + j, (px, py, c))
            for i, (s, l) in pairs for j, (px, py) in enumerate(chips)]
    received = [_remote(s, l.at[2 * px + py], send_sems, recv_sems, 3 * i + j, (px, py, c))
                for i, (s, l) in pairs for j, (px, py) in enumerate(chips)]
    return sent, received


def _plan_swap(src_ref, land_ref, send_sems, recv_sems):
    x, y, c, _ = _place()
    _, other_half = _halves(c, src_ref.shape[1], 8)
    cp = _remote(src_ref.at[pl.ds(0, src_ref.shape[0]), other_half], land_ref, send_sems, recv_sems, 0, (x, y, 1 - c))
    return [cp], [cp]


def _plan_scatter(src_ref, land_ref, send_sems, recv_sems):
    x, y, c, chips = _place()
    sent = [_remote(src_ref.at[2 * px + py], land_ref.at[2 * x + y], send_sems, recv_sems, j, (px, py, c))
            for j, (px, py) in enumerate(chips)]
    received = [_remote(src_ref.at[2 * px + py], land_ref.at[2 * px + py], send_sems, recv_sems, j, (px, py, c))
                for j, (px, py) in enumerate(chips)]
    return sent, received


def _plan_share(src_ref, land_ref, send_sems, recv_sems):
    x, y, c, _ = _place()
    mine_half, other_half = _halves(c, src_ref.shape[0], 8)
    return ([_remote(src_ref.at[mine_half], src_ref.at[mine_half], send_sems, recv_sems, 0, (x, y, 1 - c))],
            [_remote(src_ref.at[other_half], src_ref.at[other_half], send_sems, recv_sems, 0, (x, y, 1 - c))])


class _GatherBehind:
    def __init__(self, name, shards, chip, after):
        self.chip = chip
        self.token, self.wait = _split_gather(name, shards, after)

    def result(self, *after):
        shards, lands = self.wait(*after)
        return [lax.dynamic_update_slice(land, shard[None], (self.chip, 0, 0)) for shard, land in zip(shards, lands)]


class _ReduceBehind:
    def __init__(self, name, chip, c_idx):
        self.name, self.chip, self.c_idx = name, chip, c_idx

    def start_slab(self, g):
        n_sh, rows, n = g.shape
        token, self.wait = _split_copy(self.name + "_swap", g, (n_sh, rows // 2, n), g.dtype, 1, _plan_swap)
        return token

    def pair(self, after):
        g, a = self.wait(after)
        h = _add_half(self.name + "_pair", g, a, self.c_idx)
        token, self.wait = _split_copy(self.name + "_scatter", h, h.shape, h.dtype, 3, _plan_scatter)
        return token

    def total(self, after):
        h, b = self.wait(after)
        b = lax.dynamic_update_slice(b, lax.dynamic_slice_in_dim(h, self.chip, 1, axis=0), (self.chip, 0, 0))
        f = _sum_chips(self.name + "_sum", b, self.c_idx)
        token, self.wait = _split_copy(self.name + "_share", f, (8, LANES), f.dtype, 1, _plan_share)
        return token

    def result(self, after):
        return self.wait(after)[0]


class _ReduceColsBehind(_ReduceBehind):
    def start(self, g_padded):
        g = _unpack_w_in_grad(g_padded)
        n = g.shape[1] // N_CHIP
        return self.start_slab(jnp.stack([g[:, k * n:(k + 1) * n] for k in range(N_CHIP)]))


def _f_adamw(w, g, m, v):
    m = ADAM_B1 * m + (1.0 - ADAM_B1) * g
    v = ADAM_B2 * v + (1.0 - ADAM_B2) * (g * g)
    m_hat = m / (1.0 - ADAM_B1 ** ADAM_STEP)
    v_hat = v / (1.0 - ADAM_B2 ** ADAM_STEP)
    return -ADAM_LR * (m_hat / (jnp.sqrt(v_hat) + ADAM_EPS) + ADAM_WD * w), m, v


def _adamw(name, w, g, m, v):
    rows, n = w.shape
    return _rowwise(name, lambda w, g, m, v: (_f_adamw(w, g, m, v), ()), rows, [(t, n, 0) for t in (w, g, m, v)], [],
                    [(n, F32)] * 3, [], tm=_row_tile(rows, 8, 256))


def _adamw_many(name, ws, gs, ms, vs):
    k = len(ws)

    def body(*refs):
        ins, outs = refs[:4 * k], refs[4 * k:]
        for i in range(k):
            res = _f_adamw(ins[i][...], ins[k + i][...], ins[2 * k + i][...], ins[3 * k + i][...])
            for j in range(3):
                outs[j * k + i][...] = res[j]

    out = pl.pallas_call(body, name=name, out_shape=[jax.ShapeDtypeStruct(w.shape, F32) for w in ws] * 3)(
        *ws, *gs, *ms, *vs)
    return out[:k], out[k:2 * k], out[2 * k:]


def _pack_rows(parts):
    rows = []
    for t in parts:
        t = t.reshape(-1)
        rows.append(jnp.pad(t, (0, -t.shape[0] % LANES)).reshape(-1, LANES))
    out = jnp.concatenate(rows, axis=0)
    return jnp.pad(out, ((0, -out.shape[0] % 8), (0, 0)))


def _unpack_rows(packed, shapes):
    out, r = [], 0
    for shp in shapes:
        n = int(np.prod(shp))
        nr = -(-n // LANES)
        out.append(packed[r:r + nr].reshape(-1)[:n].reshape(shp))
        r += nr
    return out


def _sum_blocks(name, g):
    def body(g_ref, o_ref):
        acc = g_ref[0]
        for k in range(1, g.shape[0]):
            acc = acc + g_ref[k]
        o_ref[...] = acc

    return pl.pallas_call(body, name=name, out_shape=jax.ShapeDtypeStruct(g.shape[1:], F32))(g)


def _silu(t):
    return t * _sigmoid(t)


def _ada_fwd(cc, w_ada):
    n = w_ada.shape[1]
    tn = _row_tile(n, LANES, 512)

    def body(cc_ref, w_ref, o_ref):
        o_ref[...] = _nn(_silu(cc_ref[...]), w_ref[...])

    return pl.pallas_call(
        body, name="ada_fwd", grid=(n // tn,), out_shape=jax.ShapeDtypeStruct((cc.shape[0], n), F32),
        in_specs=[pl.BlockSpec(cc.shape, lambda j: (0, 0)), pl.BlockSpec((w_ada.shape[0], tn), lambda j: (0, j))],
        out_specs=pl.BlockSpec((cc.shape[0], tn), lambda j: (0, j)), compiler_params=_cp("parallel"),
    )(cc, w_ada)


def _ada_bwd(cc, dm, w_ada):
    d, n = w_ada.shape
    tn = _row_tile(n, LANES, 512)

    def body(cc_ref, dm_ref, w_ref, gw_ref, ds_ref):
        @pl.when(pl.program_id(0) == 0)
        def _():
            ds_ref[...] = jnp.zeros_like(ds_ref)

        gw_ref[...] = _raw_dot("tn", _silu(cc_ref[...]), dm_ref[...], True)
        ds_ref[...] += _raw_dot("nt", dm_ref[...], w_ref[...], False)

    return pl.pallas_call(
        body, name="ada_bwd", grid=(n // tn,),
        out_shape=[jax.ShapeDtypeStruct((d, n), F32), jax.ShapeDtypeStruct(cc.shape, F32)],
        in_specs=[pl.BlockSpec(cc.shape, lambda j: (0, 0)), pl.BlockSpec((cc.shape[0], tn), lambda j: (0, j)),
                  pl.BlockSpec((d, tn), lambda j: (0, j))],
        out_specs=[pl.BlockSpec((d, tn), lambda j: (0, j)), pl.BlockSpec(cc.shape, lambda j: (0, 0))],
        compiler_params=_cp("arbitrary"),
    )(cc, dm, w_ada)


def _c_ctx_grad(parts, c_ctx):
    def body(p_ref, c_ref, o_ref):
        ds = ((p_ref[0] + p_ref[1]) + p_ref[2]) + p_ref[3]
        _, vjp = jax.vjp(_silu, c_ref[...])
        o_ref[...] = vjp(ds)[0]

    return pl.pallas_call(body, name="c_ctx_grad", out_shape=jax.ShapeDtypeStruct(c_ctx.shape, F32))(parts, c_ctx)


def kernel(x, c, ctx, c_ctx, w_ada, b_ada, g_pre_mix, g_post_mix, g_pre_ffn, g_post_ffn, w_in, attn_sink, w_gate_fwd, b_gate_fwd, w_gate_bwd, b_gate_bwd, g_gla_norm, w_out, w_ffn_in, w_ffn_out, loss_target, m_c_ctx, m_w_ada, m_b_ada, m_g_pre_mix, m_g_post_mix, m_g_pre_ffn, m_g_post_ffn, m_w_in, m_attn_sink, m_w_gate_fwd, m_b_gate_fwd, m_w_gate_bwd, m_b_gate_bwd, m_g_gla_norm, m_w_out, m_w_ffn_in, m_w_ffn_out, v_c_ctx, v_w_ada, v_b_ada, v_g_pre_mix, v_g_post_mix, v_g_pre_ffn, v_g_post_ffn, v_w_in, v_attn_sink, v_w_gate_fwd, v_b_gate_fwd, v_w_gate_bwd, v_b_gate_bwd, v_g_gla_norm, v_w_out, v_w_ffn_in, v_w_ffn_out):
    xi, yi, ci = lax.axis_index("x"), lax.axis_index("y"), lax.axis_index("c")
    dev, chip = 4 * xi + 2 * yi + ci, 2 * xi + yi
    c_idx = jnp.reshape(ci, (1,)).astype(jnp.int32)
    d = x.shape[-1]
    n_ada, n_in, n_f = w_ada.shape[-1], w_in.shape[-1], w_ffn_in.shape[-1]
    r_out, r_f = w_out.shape[1], w_ffn_out.shape[1]
    n_gate = w_gate_fwd.shape[-1]
    by_chip = lambda t: t[0::2]

    rc = -(-d // LANES)
    g1 = _ag_small("gather_cond", _pack_rows([c[0], w_gate_fwd[0], w_gate_bwd[0]]))
    c_all = g1[:, :rc].reshape(N_DEV, -1)[:, :d]
    gr = GATE_RANK * n_gate // LANES
    gate_full = lambda off: jnp.transpose(by_chip(g1)[:, off:off + gr].reshape(N_CHIP, GATE_RANK, n_gate),
                                          (1, 0, 2)).reshape(GATE_RANK, N_CHIP * n_gate)
    wgf, wgb = gate_full(rc), gate_full(rc + gr)
    cc = jnp.concatenate([c_all, c_ctx[None, :], jnp.zeros((7, d), F32)], axis=0)

    g2 = _ag_small("gather_ada", _ada_fwd(cc, w_ada[0]).reshape(-1, LANES))
    ada_all = jnp.transpose(by_chip(g2).reshape(N_CHIP, 16, n_ada), (1, 0, 2)).reshape(16, N_CHIP * n_ada) + b_ada
    first = _GatherBehind("gather_w_in", [w_in[0].astype(BF16)], chip, g2)
    late_shards = [w_out[0].astype(BF16), jnp.transpose(w_ffn_in[0]).astype(BF16), w_ffn_out[0].astype(BF16)]
    late = []

    def first_weights(*after):
        w_in_g, = first.result(*after, *late_shards)
        late.append(_GatherBehind("gather_late", late_shards, chip, w_in_g))
        return _pack_w_in(jnp.concatenate([w_in_g[k] for k in range(N_CHIP)], axis=1)), late[0].token

    def late_weights(after):
        return [t.reshape(-1, d) for t in late[0].result(after)]

    ada_all = _behind(ada_all, first.token)
    ada = lax.dynamic_slice(ada_all, (dev, 0), (1, N_CHIP * n_ada))
    ada_c = ada_all[N_DEV:N_DEV + 1]

    w = _prep_gate_weights(wgf, wgb)
    w.update(w_in=first_weights, g_pre_mix=g_pre_mix, g_post_mix=g_post_mix, g_pre_ffn=g_pre_ffn, g_post_ffn=g_post_ffn,
             attn_sink=attn_sink, b_gate_fwd=b_gate_fwd, b_gate_bwd=b_gate_bwd, g_gla_norm=g_gla_norm)

    reduce_behind = _ReduceBehind("reduce_late", chip, c_idx)
    reduce_w_in = _ReduceColsBehind("reduce_w_in", chip, c_idx)
    loss_lanes, grad_x, g, d_ada, d_ada_c = _local_step(x[0], ctx[0], loss_target[0], ada, ada_c, w, late_weights,
                                                        reduce_behind, reduce_w_in)

    small = ("g_pre_mix", "g_post_mix", "g_pre_ffn", "g_post_ffn", "attn_sink", "b_gate_fwd", "b_gate_bwd",
             "g_gla_norm", "w_gate_fwd", "w_gate_bwd")
    shapes = [(1, 6 * d)] * 2 + [g[n].shape for n in small] + [(1, LANES)]
    g3 = _ag_small("gather_small_grads", _pack_rows([d_ada, d_ada_c] + [g[n] for n in small] + [loss_lanes]))
    tot = dict(zip(("d_ada", "d_ada_c") + small + ("loss",),
                   _unpack_rows(_sum_blocks("sum_small_grads", g3), shapes)))
    r_ada = 6 * d // LANES
    dm = jnp.concatenate([g3[:, :r_ada].reshape(N_DEV, 6 * d), tot["d_ada_c"], jnp.zeros((7, 6 * d), F32)], axis=0)
    grads = {n: tot[n] for n in small[:8]}
    grads["b_ada"] = _sum_blocks("sum_b_ada", dm.reshape(16, r_ada, LANES)).reshape(1, 6 * d)
    grads["w_gate_fwd"] = lax.dynamic_slice(tot["w_gate_fwd"], (0, chip * n_gate), (GATE_RANK, n_gate))[None]
    grads["w_gate_bwd"] = lax.dynamic_slice(tot["w_gate_bwd"], (0, chip * n_gate), (GATE_RANK, n_gate))[None]
    gw_ada, dsc = _ada_bwd(cc, lax.dynamic_slice(dm, (0, chip * n_ada), (16, n_ada)), w_ada[0])
    grads["w_ada"] = gw_ada[None]
    g4 = _ag_small("gather_c_ctx", _pack_rows([dsc[N_DEV]]))
    grads["c_ctx"] = _c_ctx_grad(by_chip(g4), _pack_rows([c_ctx])).reshape(-1)[:d]

    grads["w_in"] = reduce_w_in.result(g4)[None]
    part = lambda n: g["late"][g["late_at"][n]:g["late_at"][n] + g["late_rows"][n]]
    grads["w_ffn_in"], grads["w_ffn_out"], grads["w_out"] = (jnp.transpose(part("w_ffn_in_t"))[None],
                                                            part("w_ffn_out")[None], part("w_out")[None])

    names = ("c_ctx", "w_ada", "b_ada", "g_pre_mix", "g_post_mix", "g_pre_ffn", "g_post_ffn", "w_in", "attn_sink",
             "w_gate_fwd", "b_gate_fwd", "w_gate_bwd", "b_gate_bwd", "g_gla_norm", "w_out", "w_ffn_in", "w_ffn_out")
    weights = dict(zip(names, (c_ctx, w_ada, b_ada, g_pre_mix, g_post_mix, g_pre_ffn, g_post_ffn, w_in, attn_sink,
                               w_gate_fwd, b_gate_fwd, w_gate_bwd, b_gate_bwd, g_gla_norm, w_out, w_ffn_in,
                               w_ffn_out)))
    m_in = dict(zip(names, (m_c_ctx, m_w_ada, m_b_ada, m_g_pre_mix, m_g_post_mix, m_g_pre_ffn, m_g_post_ffn, m_w_in,
                            m_attn_sink, m_w_gate_fwd, m_b_gate_fwd, m_w_gate_bwd, m_b_gate_bwd, m_g_gla_norm,
                            m_w_out, m_w_ffn_in, m_w_ffn_out)))
    v_in = dict(zip(names, (v_c_ctx, v_w_ada, v_b_ada, v_g_pre_mix, v_g_post_mix, v_g_pre_ffn, v_g_post_ffn, v_w_in,
                            v_attn_sink, v_w_gate_fwd, v_b_gate_fwd, v_w_gate_bwd, v_b_gate_bwd, v_g_gla_norm,
                            v_w_out, v_w_ffn_in, v_w_ffn_out)))
    large = ("w_ada", "w_in", "w_out", "w_ffn_in", "w_ffn_out")
    tiny = tuple(n for n in names if n not in large)
    delta, new_m, new_v = {}, {}, {}
    for n in large:
        dl, nm, nv = _adamw("adamw_" + n, weights[n][0], grads[n][0], m_in[n][0], v_in[n][0])
        delta[n], new_m[n], new_v[n] = dl[None], nm[None], nv[None]
    for n in tiny:
        grads[n] = grads[n].reshape(weights[n].shape)
    as_rows = lambda t: t.reshape(-1, t.shape[-1])
    res = _adamw_many("adamw_small", *[[as_rows(t[n]) for n in tiny] for t in (weights, grads, m_in, v_in)])
    for out, vals in zip((delta, new_m, new_v), res):
        out.update({n: val.reshape(weights[n].shape) for n, val in zip(tiny, vals)})

    return (tot["loss"][0, 0], grad_x[None], *[grads[n] for n in names], *[delta[n] for n in names], *[new_m[n] for n in names],
            *[new_v[n] for n in names])
```

```python
import functools

import jax
import jax.numpy as jnp
import numpy as np
from jax import lax
from jax.experimental import pallas as pl
from jax.experimental.pallas import tpu as pltpu

F32 = jnp.float32
BF16 = jnp.bfloat16
MESH = pl.DeviceIdType.MESH

HEAD_DIM = 64
ATT_HEADS = 8
ATT_KV_HEADS = 2
ATT_GROUP = ATT_HEADS // ATT_KV_HEADS
WINDOW = 128
BLOCK = 128
GRID_W = 64
ROPE_BASE = 10000.0
GLA_HEADS = 8
GLA_DK = 32
GLA_DV = 64
GLA_CHUNK = 64
GATE_RANK = 16
GATE_TAU = 16.0
NEG_INF = -1e30
QW = ATT_HEADS * HEAD_DIM
KVW = ATT_KV_HEADS * HEAD_DIM
GKW = GLA_HEADS * GLA_DK
GVW = GLA_HEADS * GLA_DV
IN_COLS = QW + 2 * KVW + 2 * GKW + 2 * GVW + 2 * GATE_RANK
LANES = 128
IN_PAD = IN_COLS + LANES - 2 * GATE_RANK
C_Q, C_GV, C_GG = 0, QW, QW + GVW
C_K = C_GG + GVW
C_V = C_K + KVW
C_GQ = C_V + KVW
C_GK = C_GQ + GKW
C_Z = C_GK + GKW
MIX = QW + GVW

ADAM_LR, ADAM_B1, ADAM_B2, ADAM_EPS, ADAM_WD, ADAM_STEP = 0.001, 0.9, 0.999, 1e-08, 0.01, 10

VMEM_LIMIT = 56 * 1024 * 1024


def _cp(*sem):
    return pltpu.CompilerParams(dimension_semantics=sem, vmem_limit_bytes=VMEM_LIMIT)


def _pick(n, cands):
    for t in cands:
        if n % t == 0:
            return t
    return n


_DIMS = {"nn": (((1,), (0,)), ((), ())), "nt": (((1,), (1,)), ((), ())), "tn": (((0,), (0,)), ((), ()))}


def _raw_dot(mode, a, b, hi):
    dot = lambda u, v: lax.dot_general(u, v, _DIMS[mode], preferred_element_type=F32)
    if not hi:
        return dot(a.astype(BF16), b.astype(BF16))
    a, b = a.astype(F32), b.astype(F32)
    a_hi, b_hi = a.astype(BF16), b.astype(BF16)
    out = dot(a_hi, b_hi)
    if hi != "a":
        out = out + dot((a - a_hi.astype(F32)).astype(BF16), b_hi)
    if hi != "b":
        out = out + dot(a_hi, (b - b_hi.astype(F32)).astype(BF16))
    return out


def _make_dot(mode, hi):
    @jax.custom_vjp
    def dot(a, b):
        return _raw_dot(mode, a, b, hi)

    def fwd(a, b):
        return _raw_dot(mode, a, b, hi), (a, b)

    def bwd(res, dc):
        a, b = res
        if mode == "nn":
            return (_raw_dot("nt", dc, b, "b" if hi == "b" else bool(hi)),
                    _raw_dot("tn", a, dc, "a" if hi == "a" else bool(hi)))
        if mode == "nt":
            return _raw_dot("nn", dc, b, bool(hi)), _raw_dot("tn", dc, a, bool(hi))
        return _raw_dot("nt", b, dc, bool(hi)), _raw_dot("nn", a, dc, bool(hi))

    dot.defvjp(fwd, bwd)
    return dot


_nn, _nt, _tn = _make_dot("nn", False), _make_dot("nt", False), _make_dot("tn", False)
_nn_mask, _nn_by_exact = _make_dot("nn", "a"), _make_dot("nn", "b")


MM_VMEM_BUDGET = 44 * 1024 * 1024


def _halvings(n):
    out = [n]
    while out[-1] % (2 * LANES) == 0:
        out.append(out[-1] // 2)
    return out


def _mm_tiles(mode, m, n, k, a_bytes, b_bytes, o_bytes, init_bytes=0):
    tms = [t for t in dict.fromkeys((m, m // 2, m // 4, 2048, 1024, 512, 256, 128))
           if m % t == 0 and t % (LANES if mode == "tn" else 16) == 0 and t <= 4096] or [m]
    if mode == "tn":
        fits = [(k // tk + 0.5 * (m // tm), tm, tk)
                for tk in (4096, 2048, 1024, 512, 256, 128) if k % tk == 0 for tm in tms
                if 2 * (tk * tm * a_bytes + tk * n * b_bytes + tm * n * (o_bytes + init_bytes)) <= MM_VMEM_BUDGET]
        if fits:
            _, tm, tk = min(fits)
            return tm, n, tk
    tks = ([t for t in (512, 256, 128) if k % t == 0] or [k]) if mode == "tn" else _halvings(k)
    for tn in _halvings(n):
        for tk in tks:
            for tm in tms:
                acc = tm * tn * 4 if (k // tk > 1 and o_bytes != 4) else 0
                tiles = tm * tk * a_bytes + tk * tn * b_bytes + tm * tn * (o_bytes + init_bytes)
                if 2 * tiles + acc <= MM_VMEM_BUDGET:
                    return tm, tn, tk
    return tms[-1], _halvings(n)[-1], tks[-1]


def _mm(name, a, b, mode, out_dtype=F32, init=None, after=None):
    follow = () if after is None else (after,)
    if mode == "nn":
        (m, k), n = a.shape, b.shape[1]
    elif mode == "nt":
        (m, k), n = a.shape, b.shape[0]
    else:
        (k, m), n = a.shape, b.shape[1]
    tm, tn, tk = _mm_tiles(mode, m, n, k, a.dtype.itemsize, b.dtype.itemsize, jnp.dtype(out_dtype).itemsize,
                           0 if init is None else 4)
    nk = k // tk
    use_acc = nk > 1 and out_dtype != F32

    inits = () if init is None else (init,)

    def body(a_ref, b_ref, *rest):
        rest = rest[:len(inits)] + rest[len(inits) + len(follow):]
        o_ref, acc = rest[len(inits)], rest[len(inits) + 1:]
        part = _raw_dot(mode, a_ref[...], b_ref[...], False)
        first = lambda: part + rest[0][...] if inits else part
        if nk == 1:
            o_ref[...] = first().astype(o_ref.dtype)
            return
        acc_ref = acc[0] if use_acc else o_ref
        kk = pl.program_id(2)

        @pl.when(kk == 0)
        def _():
            acc_ref[...] = first()

        @pl.when(kk > 0)
        def _():
            acc_ref[...] += part

        if use_acc:
            @pl.when(kk == nk - 1)
            def _():
                o_ref[...] = acc_ref[...].astype(o_ref.dtype)

    if mode == "nn":
        a_spec = pl.BlockSpec((tm, tk), lambda i, j, kk: (i, kk))
        b_spec = pl.BlockSpec((tk, tn), lambda i, j, kk: (kk, j))
    elif mode == "nt":
        a_spec = pl.BlockSpec((tm, tk), lambda i, j, kk: (i, kk))
        b_spec = pl.BlockSpec((tn, tk), lambda i, j, kk: (j, kk))
    else:
        a_spec = pl.BlockSpec((tk, tm), lambda i, j, kk: (kk, i))
        b_spec = pl.BlockSpec((tk, tn), lambda i, j, kk: (kk, j))
    return pl.pallas_call(
        body, name=name, grid=(m // tm, n // tn, nk),
        in_specs=[a_spec, b_spec] + [pl.BlockSpec((tm, tn), lambda i, j, kk: (i, j))] * len(inits)
        + [pl.BlockSpec(memory_space=pl.ANY)] * len(follow),
        out_specs=pl.BlockSpec((tm, tn), lambda i, j, kk: (i, j)),
        out_shape=jax.ShapeDtypeStruct((m, n), out_dtype),
        scratch_shapes=[pltpu.VMEM((tm, tn), F32)] if use_acc else [],
        compiler_params=_cp("parallel", "parallel", "arbitrary"),
    )(a, b, *inits, *follow)


def _slab_layout(rows):
    offsets, at = [], 0
    for r in rows:
        at = -(-at // r) * r
        offsets.append(at)
        at += r
    return offsets, -(-at // 32) * 32


def _slab_zero_gaps(name, shape, rows, offsets):
    gaps = [(o + r, nxt) for o, r, nxt in zip(offsets, rows, offsets[1:] + [shape[1]]) if nxt > o + r]
    slab = None
    for i, (lo, hi) in enumerate(gaps):
        step = int(np.gcd(lo, hi - lo))

        def body(*refs):
            refs[-1][...] = jnp.zeros_like(refs[-1])

        slab = pl.pallas_call(
            body, name=f"{name}_{i}", grid=(shape[0], (hi - lo) // step), out_shape=jax.ShapeDtypeStruct(shape, F32),
            in_specs=[] if slab is None else [pl.BlockSpec(memory_space=pl.ANY)],
            out_specs=pl.BlockSpec((1, step, shape[2]), functools.partial(lambda k, j, b: (k, b + j, 0), b=lo // step)),
            input_output_aliases={} if slab is None else {0: 0}, compiler_params=_cp("parallel", "parallel"),
        )(*(() if slab is None else (slab,)))
    return slab


def _dw_into_slab(name, a, b, slab, shape, at):
    (k, m), n = a.shape, b.shape[1]
    r = m // N_CHIP
    fits = [(k // tk + 0.5 * (m // tm), tm, tk)
            for tk in (4096, 2048, 1024, 512, 256, 128) if k % tk == 0 for tm in (m, m // 2, r) if tm % LANES == 0
            if 2 * (tk * tm * a.dtype.itemsize + tk * n * b.dtype.itemsize + tm * n * 4) <= MM_VMEM_BUDGET]
    _, tm, tk = min(fits)
    per, nk = tm // r, k // tk

    def body(a_ref, b_ref, *rest):
        o_ref = rest[-1]
        part = _raw_dot("tn", a_ref[...], b_ref[...], False).reshape(o_ref.shape)
        if nk == 1:
            o_ref[...] = part
            return
        kk = pl.program_id(1)

        @pl.when(kk == 0)
        def _():
            o_ref[...] = part

        @pl.when(kk > 0)
        def _():
            o_ref[...] += part

    prev = () if slab is None else (slab,)
    return pl.pallas_call(
        body, name=name, grid=(m // tm, nk), out_shape=jax.ShapeDtypeStruct(shape, F32),
        in_specs=[pl.BlockSpec((tk, tm), lambda i, kk: (kk, i)), pl.BlockSpec((tk, n), lambda i, kk: (kk, 0))]
        + [pl.BlockSpec(memory_space=pl.ANY)] * len(prev),
        out_specs=pl.BlockSpec((per, r, n), lambda i, kk: (i, at // r, 0)),
        input_output_aliases={2: 0} if prev else {}, compiler_params=_cp("parallel", "arbitrary"),
    )(a, b, *prev)


def _rowwise(name, fn, rows, row_ins, full_ins, row_outs, acc_outs, tm=None):
    tm = tm or _pick(rows, (512, 256, 128))
    n_r, n_f, n_o, n_a = len(row_ins), len(full_ins), len(row_outs), len(acc_outs)

    def body(*refs):
        ins, outs = refs[:n_r + n_f], refs[n_r + n_f:]
        vals = [r[...].astype(F32) for r in ins]
        ro, ao = fn(*vals)
        for r, val in zip(outs[:n_o], ro):
            r[...] = val.astype(r.dtype)
        if n_a:
            @pl.when(pl.program_id(0) == 0)
            def _():
                for r in outs[n_o:]:
                    r[...] = jnp.zeros_like(r)

            for r, val in zip(outs[n_o:], ao):
                r[...] += val

    in_specs = [pl.BlockSpec((tm, w), functools.partial(lambda i, cb: (i, cb), cb=cb)) for _, w, cb in row_ins]
    in_specs += [pl.BlockSpec(a.shape, lambda i: (0, 0)) for a in full_ins]
    out_specs = [pl.BlockSpec((tm, w), lambda i: (i, 0)) for w, _ in row_outs]
    out_specs += [pl.BlockSpec(s, lambda i: (0, 0)) for s in acc_outs]
    out_shape = [jax.ShapeDtypeStruct((rows, w), dt) for w, dt in row_outs]
    out_shape += [jax.ShapeDtypeStruct(s, F32) for s in acc_outs]
    return pl.pallas_call(
        body, name=name, grid=(rows // tm,), in_specs=in_specs, out_specs=out_specs, out_shape=out_shape,
        compiler_params=_cp("arbitrary" if n_a else "parallel"),
    )(*[a for a, _, _ in row_ins], *full_ins)


def _rn(x):
    return x * lax.rsqrt(jnp.mean(x * x, axis=-1, keepdims=True) + 1e-6)


def _sigmoid(t):
    return 1.0 / (1.0 + jnp.exp(-t))


def _f_norm_mod(x, g, sh, sc):
    return _rn(x) * g * (1.0 + sc) + sh


def _f_post_res(xr, y, g, gate):
    return xr + gate * (_rn(y) * g)


@jax.custom_vjp
def _f_swiglu(g, u):
    return g * _sigmoid(g) * u


def _f_swiglu_fwd(g, u):
    s = _sigmoid(g)
    return g * s * u, (g, u, s)


def _f_swiglu_bwd(res, da):
    g, u, s = res
    gs = g * s
    return da * u * (s + gs * (1.0 - s)), da * gs


_f_swiglu.defvjp(_f_swiglu_fwd, _f_swiglu_bwd)


def _logsig(u):
    return jnp.minimum(u, 0.0) - jnp.log(1.0 + jnp.exp(-jnp.abs(u)))


def _f_gate(z, wf, wb, bf, bb):
    return _logsig(_nn(z, wf) + bf) / GATE_TAU, _logsig(_nn(z, wb) + bb) / GATE_TAU


def _f_gla_out(of, ob, gg, gt, bd):
    o = of + ob
    ms = _nn_by_exact(o * o, bd)
    return o * lax.rsqrt(ms + 1e-6) * gt * (gg * _sigmoid(gg))


def _norm_mod(name, x, g, sh, sc):
    rows, d = x.shape
    return _rowwise(name, lambda x, g, sh, sc: ((_f_norm_mod(x, g, sh, sc),), ()), rows,
                    [(x, d, 0)], [g, sh, sc], [(d, BF16)], [])[0]


def _rn_bwd(x, dn):
    r = lax.rsqrt(jnp.mean(x * x, axis=-1, keepdims=True) + 1e-6)
    n = x * r
    return r * (dn - n * jnp.mean(dn * n, axis=-1, keepdims=True)), n


def _norm_mod_grads(dh, x, g, sc):
    dx, n = _rn_bwd(x, dh * (g * (1.0 + sc)))
    t = jnp.sum(dh * n, axis=0, keepdims=True)
    return dx, (1.0 + sc) * t, jnp.sum(dh, axis=0, keepdims=True), g * t


def _post_res_grads(dout, y, g, gate):
    dy, n = _rn_bwd(y, dout * (gate * g))
    t = jnp.sum(dout * n, axis=0, keepdims=True)
    return dy, gate * t, g * t


def _norm_mod_bwd(name, dh, dres, x, g, sh, sc):
    rows, d = x.shape

    def fn(dh, dres, x, g, sh, sc):
        dx, dg, dsh, dsc = _norm_mod_grads(dh, x, g, sc)
        return (dx + dres,), (dg, dsh, dsc)

    return _rowwise(name, fn, rows, [(dh, d, 0), (dres, d, 0), (x, d, 0)], [g, sh, sc], [(d, F32)],
                    [(1, d)] * 3)


def _post_res_norm_mod(name, xr, y, g_post, gate, g_pre, sh, sc):
    rows, d = xr.shape

    def fn(xr, y, g_post, gate, g_pre, sh, sc):
        x1 = _f_post_res(xr, y, g_post, gate)
        return (x1, _f_norm_mod(x1, g_pre, sh, sc)), ()

    return _rowwise(name, fn, rows, [(xr, d, 0), (y, d, 0)], [g_post, gate, g_pre, sh, sc], [(d, F32), (d, BF16)], [])


def _norm_mod_post_res_bwd(name, dh, dres, x1, y, g_pre, sh, sc, g_post, gate):
    rows, d = x1.shape

    def fn(dh, dres, x1, y, g_pre, sh, sc, g_post, gate):
        dx1, dg_pre, dsh, dsc = _norm_mod_grads(dh, x1, g_pre, sc)
        dx1 = dx1 + dres
        dy, dg_post, dgate = _post_res_grads(dx1, y, g_post, gate)
        return (dx1, dy), (dg_pre, dsh, dsc, dg_post, dgate)

    return _rowwise(name, fn, rows, [(dh, d, 0), (dres, d, 0), (x1, d, 0), (y, d, 0)], [g_pre, sh, sc, g_post, gate],
                    [(d, F32), (d, BF16)], [(1, d)] * 5)


def _post_res_loss(name, xr, y, g, gate, target):
    rows, d = xr.shape

    def fn(xr, y, target, g, gate):
        diff = _f_post_res(xr, y, g, gate) - target
        part = 0.5 * jnp.sum(jnp.mean(diff * diff, axis=-1, keepdims=True), axis=0, keepdims=True)
        dx2 = diff * (1.0 / d)
        dy, dg, dgate = _post_res_grads(dx2, y, g, gate)
        return (dx2, dy), (jnp.broadcast_to(part, (1, LANES)), dg, dgate)

    return _rowwise(name, fn, rows, [(xr, d, 0), (y, d, 0), (target, d, 0)], [g, gate], [(d, F32), (d, BF16)],
                    [(1, LANES), (1, d), (1, d)])


def _mm_rows(name, a, b, mode, fn, extras, outs):
    m, k = a.shape
    tm = _pick(m, (256, 128))

    def body(a_ref, b_ref, *rest):
        tiles = fn(_raw_dot(mode, a_ref[...], b_ref[...], False), *[e[...] for e in rest[:len(extras)]])
        for r, val in zip(rest[len(extras):], tiles):
            r[...] = val.astype(r.dtype)

    row = lambda w: pl.BlockSpec((tm, w), lambda i: (i, 0))
    return pl.pallas_call(
        body, name=name, grid=(m // tm,),
        in_specs=[row(k), pl.BlockSpec(b.shape, lambda i: (0, 0))] + [row(e.shape[1]) for e in extras],
        out_specs=[row(w) for w, _ in outs], out_shape=[jax.ShapeDtypeStruct((m, w), dt) for w, dt in outs],
        compiler_params=_cp("parallel"),
    )(a, b, *extras)


def _ffn_in_swiglu(name, h, w_t):
    f = w_t.shape[0] // 2
    fn = lambda u: (u, _f_swiglu(u[:, :f], u[:, f:]))
    return _mm_rows(name, h, w_t, "nt", fn, [], [(2 * f, BF16), (f, BF16)])


def _ffn_out_dx_swiglu_bwd(name, df, w_out, u):
    f = w_out.shape[0]

    def fn(da, u):
        u = u.astype(F32)
        _, vjp = jax.vjp(_f_swiglu, u[:, :f], u[:, f:])
        return (jnp.concatenate(vjp(da), axis=1),)

    return _mm_rows(name, df, w_out, "nt", fn, [u], [(2 * f, BF16)])[0]


def _gate_fwd(name, p, wf, wb, bf, bb):
    rows = p.shape[0]
    return _rowwise(name, lambda z, wf, wb, bf, bb: (_f_gate(z, wf, wb, bf, bb), ()), rows,
                    [(p, LANES, C_Z // LANES)], [wf, wb, bf, bb], [(GKW, F32)] * 2, [])


def _gate_bwd(name, p, dla_f, dla_b, wf, wb, bf, bb):
    rows = p.shape[0]

    def fn(z, dlf, dlb, wf, wb, bf, bb):
        _, vjp = jax.vjp(_f_gate, z, wf, wb, bf, bb)
        dz, dwf, dwb, dbf, dbb = vjp((dlf, dlb))
        return (dz,), (dwf, dwb, dbf, dbb)

    return _rowwise(name, fn, rows, [(p, LANES, C_Z // LANES), (dla_f, GKW, 0), (dla_b, GKW, 0)],
                    [wf, wb, bf, bb], [(LANES, BF16)], [(LANES, GKW), (LANES, GKW), (1, GKW), (1, GKW)])


def _head_mean_matrix():
    h = np.arange(GVW) // GLA_DV
    return jnp.asarray((h[:, None] == h[None, :]).astype(np.float32) / GLA_DV)


def _gla_out(name, attn, of, ob, p, gt):
    rows = of.shape[0]
    bd = _head_mean_matrix()
    fn = lambda attn, of, ob, gg, gt, bd: ((jnp.concatenate([attn, _f_gla_out(of, ob, gg, gt, bd)], axis=1),), ())
    return _rowwise(name, fn, rows, [(attn, QW, 0), (of, GVW, 0), (ob, GVW, 0), (p, GVW, C_GG // GVW)], [gt, bd],
                    [(MIX, BF16)], [])[0]


def _gla_out_bwd(name, dmix, of, ob, p, gt):
    rows = of.shape[0]
    bd = _head_mean_matrix()

    def fn(dm, of, ob, gg, gt, bd):
        _, vjp = jax.vjp(lambda of, gg, gt: _f_gla_out(of, ob, gg, gt, bd), of, gg, gt)
        do, dgg, dgt = vjp(dm)
        return (do, dgg), (dgt,)

    return _rowwise(name, fn, rows, [(dmix, GVW, 1), (of, GVW, 0), (ob, GVW, 0), (p, GVW, C_GG // GVW)], [gt, bd],
                    [(GVW, BF16), (GVW, BF16)], [(1, GVW)])


def _rope_tables(n_tokens):
    t = jnp.arange(n_tokens)
    row = (t // GRID_W).astype(F32)
    col = (t % GRID_W).astype(F32)
    half = HEAD_DIM // 2
    inv_freq = ROPE_BASE ** (-jnp.arange(0, half, 2, dtype=F32) / half)
    ang_r = row[:, None] * inv_freq[None, :]
    ang_c = col[:, None] * inv_freq[None, :]
    ang = jnp.concatenate([ang_r, ang_r, ang_c, ang_c], axis=-1)
    sign = jnp.concatenate([-jnp.ones((16,), F32), jnp.ones((16,), F32)] * 2)
    cos, sin = jnp.cos(ang), jnp.sin(ang) * sign[None, :]
    return jnp.tile(cos, (1, 2)), jnp.tile(sin, (1, 2))


def _rot_pairs(x):
    w = x.shape[-1]
    lane = lax.broadcasted_iota(jnp.int32, x.shape, x.ndim - 1)
    return jnp.where((lane % 32) < 16, pltpu.roll(x, w - 16, x.ndim - 1), pltpu.roll(x, 16, x.ndim - 1))


def _rope_apply(x, cos, sin_signed, inverse):
    reps = x.shape[-1] // LANES
    cos = jnp.concatenate([cos] * reps, axis=-1) if reps > 1 else cos
    sin = jnp.concatenate([sin_signed] * reps, axis=-1) if reps > 1 else sin_signed
    if inverse:
        return x * cos + _rot_pairs(x * sin)
    return x * cos + _rot_pairs(x) * sin


def _rope_fwd(name, p, cos, sin):
    rows = p.shape[0]

    def fn(q, k, v, cos, sin):
        return (_rope_apply(q, cos, sin, False), _rope_apply(k, cos, sin, False), v), ()

    return _rowwise(name, fn, rows, [(p, QW, 0), (p, KVW, C_K // KVW), (p, KVW, C_V // KVW), (cos, LANES, 0),
                                     (sin, LANES, 0)], [], [(QW, BF16), (KVW, BF16), (KVW, BF16)], [])


def _proj_grad(name, dq_rot, dk_rot, dv, cos, sin, gla_f, gla_b, dgg, dz):
    rows = dq_rot.shape[0]

    def fn(dq, dk, dv, cos, sin, gqf, gkf, gvf, gqb, gkb, gvb, dgg, dz):
        parts = [_rope_apply(dq, cos, sin, True), gvf + gvb, dgg, _rope_apply(dk, cos, sin, True), dv, gqf + gqb,
                 gkf + gkb, dz]
        return (jnp.concatenate(parts, axis=1),), ()

    ins = [(dq_rot, QW), (dk_rot, KVW), (dv, KVW), (cos, LANES), (sin, LANES)]
    ins += [(t, t.shape[1]) for t in (*gla_f, *gla_b)] + [(dgg, GVW), (dz, LANES)]
    return _rowwise(name, fn, rows, [(t, w, 0) for t, w in ins], [], [(IN_PAD, BF16)], [])[0]


GROUP_ROWS = ATT_GROUP * BLOCK


ATT_SCALE = HEAD_DIM ** -0.5


def _attn_bias(n_tokens):
    nb = n_tokens // BLOCK
    i = (jnp.arange(GROUP_ROWS) % BLOCK)[:, None]
    j = jnp.arange(3 * BLOCK)[None, :]

    def one(n):
        kpos = (n - 1) * BLOCK + j
        return jnp.where((jnp.abs(j - BLOCK - i) <= WINDOW) & (kpos >= 0) & (kpos < n_tokens), 0.0, NEG_INF)

    return jnp.stack([one(0), one(1), one(nb - 1)]).astype(F32)


def _attn_bias_spec(n_tokens):
    nb = n_tokens // BLOCK
    return pl.BlockSpec((1, GROUP_ROWS, 3 * BLOCK), lambda n: (jnp.where(n == 0, 0, jnp.where(n == nb - 1, 2, 1)), 0, 0))


def _attn_setup(sink):
    row = lax.broadcasted_iota(jnp.int32, (GROUP_ROWS, 1), 0)
    group = sum((row >= g * BLOCK).astype(jnp.int32) for g in range(1, ATT_GROUP))
    head_id = lax.broadcasted_iota(jnp.int32, (1, ATT_HEADS), 1)
    sks = []
    for h in range(ATT_KV_HEADS):
        sk = jnp.zeros((GROUP_ROWS, 1), F32)
        for g in range(ATT_GROUP):
            one = jnp.sum(jnp.where(head_id == h * ATT_GROUP + g, sink, 0.0), axis=-1, keepdims=True)
            sk = jnp.where(group == g, one, sk)
        sks.append(sk)
    return group, sks


def _attn_weights(q, kw, kc, sk, bias):
    q = q * ATT_SCALE
    s_w = _raw_dot("nt", q, kw, False) + bias
    s_c = _raw_dot("nt", q, kc, False)
    m = jnp.maximum(jnp.maximum(jnp.max(s_w, axis=-1, keepdims=True), jnp.max(s_c, axis=-1, keepdims=True)), sk)
    pw, pc, ps = jnp.exp(s_w - m), jnp.exp(s_c - m), jnp.exp(sk - m)
    return q, pw, pc, ps, jnp.sum(pw, axis=-1, keepdims=True) + jnp.sum(pc, axis=-1, keepdims=True) + ps


def _f_attn(qs, kws, vws, kcs, vcs, sink, bias):
    _, sks = _attn_setup(sink)
    outs = []
    for h in range(ATT_KV_HEADS):
        _, pw, pc, _, den = _attn_weights(qs[h], kws[h], kcs[h], sks[h], bias)
        outs.append((_raw_dot("nn", pw, vws[h], False) + _raw_dot("nn", pc, vcs[h], False)) / den)
    return tuple(outs)


def _f_attn_bwd(qs, kws, vws, kcs, vcs, sink, bias, outs, douts):
    group, sks = _attn_setup(sink)
    head_id = lax.broadcasted_iota(jnp.int32, (1, ATT_HEADS), 1)
    dot = lambda mode, a, b: _raw_dot(mode, a, b, False)
    dqs, dkws, dvws, dkcs, dvcs, dsink = [], [], [], [], [], jnp.zeros((1, ATT_HEADS), F32)
    for h in range(ATT_KV_HEADS):
        q, pw, pc, ps, den = _attn_weights(qs[h], kws[h], kcs[h], sks[h], bias)
        inv = 1.0 / den
        pw, pc = pw * inv, pc * inv
        dd = jnp.sum(douts[h] * outs[h], axis=-1, keepdims=True)
        dsw = pw * (dot("nt", douts[h], vws[h]) - dd)
        dsc = pc * (dot("nt", douts[h], vcs[h]) - dd)
        dqs.append((dot("nn", dsw, kws[h]) + dot("nn", dsc, kcs[h])) * ATT_SCALE)
        dkws.append(dot("tn", dsw, q))
        dkcs.append(dot("tn", dsc, q))
        dvws.append(dot("tn", pw, douts[h]))
        dvcs.append(dot("tn", pc, douts[h]))
        dsk = -(ps * inv) * dd
        for g in range(ATT_GROUP):
            one = jnp.sum(jnp.where(group == g, dsk, 0.0), axis=0, keepdims=True)
            dsink = dsink + jnp.where(head_id == h * ATT_GROUP + g, one, 0.0)
    return dqs, dkws, dvws, dkcs, dvcs, dsink


def _group_rows(ref, h):
    hs = lambda hq: slice(hq * HEAD_DIM, (hq + 1) * HEAD_DIM)
    return jnp.concatenate([ref[:, hs(h * ATT_GROUP + g)].astype(F32) for g in range(ATT_GROUP)], axis=0)


def _ungroup_rows(ref, h, val):
    for g in range(ATT_GROUP):
        hq = h * ATT_GROUP + g
        ref[:, hq * HEAD_DIM:(hq + 1) * HEAD_DIM] = val[g * BLOCK:(g + 1) * BLOCK].astype(ref.dtype)


def _attn_loads(n, q_ref, kp_ref, vp_ref, kc_ref, vc_ref):
    r0 = pl.multiple_of(n * BLOCK, BLOCK)
    hs = lambda h: slice(h * HEAD_DIM, (h + 1) * HEAD_DIM)
    qs = [_group_rows(q_ref, h) for h in range(ATT_KV_HEADS)]
    kws = [kp_ref[pl.ds(r0, 3 * BLOCK), hs(h)].astype(F32) for h in range(ATT_KV_HEADS)]
    vws = [vp_ref[pl.ds(r0, 3 * BLOCK), hs(h)].astype(F32) for h in range(ATT_KV_HEADS)]
    kcs = [kc_ref[:, hs(h)].astype(F32) for h in range(ATT_KV_HEADS)]
    vcs = [vc_ref[:, hs(h)].astype(F32) for h in range(ATT_KV_HEADS)]
    return r0, hs, qs, kws, vws, kcs, vcs


def _attn_specs(s, c):
    full = lambda shape: pl.BlockSpec(shape, lambda n: (0, 0))
    return [pl.BlockSpec((BLOCK, QW), lambda n: (n, 0)), full((s + 2 * BLOCK, KVW)), full((s + 2 * BLOCK, KVW)),
            full((c, KVW)), full((c, KVW)), full((1, ATT_HEADS)), _attn_bias_spec(s)]


def _attn_fwd(q, kp, vp, kc, vc, sink):
    s, c = q.shape[0], kc.shape[0]

    def body(q_ref, kp_ref, vp_ref, kc_ref, vc_ref, sink_ref, bias_ref, o_ref):
        n = pl.program_id(0)
        _, hs, qs, kws, vws, kcs, vcs = _attn_loads(n, q_ref, kp_ref, vp_ref, kc_ref, vc_ref)
        outs = _f_attn(qs, kws, vws, kcs, vcs, sink_ref[...], bias_ref[0])
        for h in range(ATT_KV_HEADS):
            _ungroup_rows(o_ref, h, outs[h])

    return pl.pallas_call(
        body, name="attn_fwd", grid=(s // BLOCK,), in_specs=_attn_specs(s, c),
        out_specs=pl.BlockSpec((BLOCK, QW), lambda n: (n, 0)), out_shape=jax.ShapeDtypeStruct((s, QW), BF16),
        compiler_params=_cp("parallel"),
    )(q, kp, vp, kc, vc, sink, _attn_bias(s))


def _attn_bwd(do, o, q, kp, vp, kc, vc, sink):
    s, c = q.shape[0], kc.shape[0]

    def body(do_ref, o_ref, q_ref, kp_ref, vp_ref, kc_ref, vc_ref, sink_ref, bias_ref, dq_ref, dkp_ref, dvp_ref,
             dkc_ref, dvc_ref, dsink_ref):
        n = pl.program_id(0)

        @pl.when(n == 0)
        def _():
            for r in (dkp_ref, dvp_ref, dkc_ref, dvc_ref, dsink_ref):
                r[...] = jnp.zeros_like(r)

        r0, hs, qs, kws, vws, kcs, vcs = _attn_loads(n, q_ref, kp_ref, vp_ref, kc_ref, vc_ref)
        heads = range(ATT_KV_HEADS)
        dqs, dkws, dvws, dkcs, dvcs, dsink = _f_attn_bwd(
            qs, kws, vws, kcs, vcs, sink_ref[...], bias_ref[0], [_group_rows(o_ref, h) for h in heads],
            [_group_rows(do_ref, h) for h in heads])
        for h in heads:
            _ungroup_rows(dq_ref, h, dqs[h])
            dkp_ref[pl.ds(r0, 3 * BLOCK), hs(h)] += dkws[h]
            dvp_ref[pl.ds(r0, 3 * BLOCK), hs(h)] += dvws[h]
            dkc_ref[:, hs(h)] += dkcs[h]
            dvc_ref[:, hs(h)] += dvcs[h]
        dsink_ref[...] += dsink

    full = lambda shape: pl.BlockSpec(shape, lambda n: (0, 0))
    return pl.pallas_call(
        body, name="attn_bwd", grid=(s // BLOCK,),
        in_specs=[pl.BlockSpec((BLOCK, QW), lambda n: (n, 0))] * 2 + _attn_specs(s, c),
        out_specs=[pl.BlockSpec((BLOCK, QW), lambda n: (n, 0)), full((s + 2 * BLOCK, KVW)), full((s + 2 * BLOCK, KVW)),
                   full((c, KVW)), full((c, KVW)), full((1, ATT_HEADS))],
        out_shape=[jax.ShapeDtypeStruct((s, QW), BF16), jax.ShapeDtypeStruct((s + 2 * BLOCK, KVW), F32),
                   jax.ShapeDtypeStruct((s + 2 * BLOCK, KVW), F32), jax.ShapeDtypeStruct((c, KVW), F32),
                   jax.ShapeDtypeStruct((c, KVW), F32), jax.ShapeDtypeStruct((1, ATT_HEADS), F32)],
        compiler_params=_cp("arbitrary"),
    )(do, o, q, kp, vp, kc, vc, sink, _attn_bias(s))


GLA_GROUPS = 1
GLA_GROUP_HEADS = GLA_HEADS // GLA_GROUPS
GKG, GVG = GKW // GLA_GROUPS, GVW // GLA_GROUPS


def _gla_masks(heads=GLA_HEADS):
    hk = np.arange(heads * GLA_DK) // GLA_DK
    hv = np.arange(heads * GLA_DV) // GLA_DV
    head_k = (np.arange(heads)[:, None] == hk[None, :]).astype(np.float32)
    head_v = (np.arange(heads)[:, None] == hv[None, :]).astype(np.float32)
    bd_t = (hv[:, None] == hk[None, :]).astype(np.float32)
    return jnp.asarray(head_k), jnp.asarray(head_v), jnp.asarray(bd_t)


def _group_states(st):
    return jnp.stack([st[g * GVG:(g + 1) * GVG, g * GKG:(g + 1) * GKG] for g in range(GLA_GROUPS)])


def _ungroup_states(st):
    out = jnp.zeros((GVW, GKW), st.dtype)
    for g in range(GLA_GROUPS):
        out = out.at[g * GVG:(g + 1) * GVG, g * GKG:(g + 1) * GKG].set(st[g])
    return out


def _tri(n, rev, strict=False):
    i = lax.broadcasted_iota(jnp.int32, (n, n), 0)
    j = lax.broadcasted_iota(jnp.int32, (n, n), 1)
    if strict:
        keep = (j > i) if rev else (j < i)
    else:
        keep = (j >= i) if rev else (j <= i)
    return keep


def _f_gla_chunk(q, k, v, la, st, head_k, head_v, bd_t, rev):
    return _f_gla_carry(*_f_gla_intra(q, k, v, la, head_k, head_v, rev), v, st, bd_t)


def _f_gla_intra(q, k, v, la, head_k, head_v, rev):
    heads, kw, vw = head_k.shape[0], q.shape[1], v.shape[1]
    keep = _tri(GLA_CHUNK, rev)
    b = _nn_mask(keep.astype(F32), la)
    bl = jnp.sum(la, axis=0, keepdims=True)
    qd = q * (GLA_DK ** -0.5) * jnp.exp(b)
    ki = k * jnp.exp(-b)
    kd = k * jnp.exp(bl - b)
    q_heads = (qd[None, :, :] * head_k[:, None, :]).reshape(heads * GLA_CHUNK, kw)
    a_all = _nt(q_heads, ki).reshape(heads, GLA_CHUNK, GLA_CHUNK)
    a_all = jnp.where(keep[None, :, :], a_all, 0.0).reshape(heads * GLA_CHUNK, GLA_CHUNK)
    o_all = _nn(a_all, v).reshape(heads, GLA_CHUNK, vw)
    return jnp.sum(o_all * head_v[:, None, :], axis=0), qd, kd, bl


def _f_gla_carry(intra, qd, kd, bl, v, st, bd_t):
    return intra + _nt(qd, st), st * jnp.exp(bl) + bd_t * _tn(v, kd)


def _gla_specs(s, tb, order):
    return [pl.BlockSpec((tb, GKW), lambda i: (order(i), C_GQ // GKW)),
            pl.BlockSpec((tb, GKW), lambda i: (order(i), C_GK // GKW)),
            pl.BlockSpec((tb, GVW), lambda i: (order(i), C_GV // GVW)),
            pl.BlockSpec((tb, GKW), lambda i: (order(i), 0))]


GLA_BLOCK_CHUNKS = 4


def _gla_fwd(p, la_f, la_b, st_f0, st_b0):
    s = p.shape[0]
    tb = GLA_BLOCK_CHUNKS * GLA_CHUNK
    nblk = s // tb
    up, down = (lambda i: i), (lambda i: nblk - 1 - i)
    masks = _gla_masks(GLA_GROUP_HEADS)

    def scan(rev, q_ref, k_ref, v_ref, la_ref, o_ref, sts_ref, st_ref, consts):
        for g in range(GLA_GROUPS):
            gk, gv = slice(g * GKG, (g + 1) * GKG), slice(g * GVG, (g + 1) * GVG)
            st = st_ref[g]
            sts_ref[0, g] = st
            chunks = range(GLA_BLOCK_CHUNKS)
            for ci in (reversed(chunks) if rev else chunks):
                rows = slice(ci * GLA_CHUNK, (ci + 1) * GLA_CHUNK)
                o, st = _f_gla_chunk(q_ref[rows, gk], k_ref[rows, gk], v_ref[rows, gv], la_ref[rows, gk], st, *consts,
                                     rev)
                o_ref[rows, gv] = o
            st_ref[g] = st

    def body(qf, kf, vf, laf, qb, kb, vb, lab, stf0, stb0, hk_ref, hv_ref, bd_ref, of_ref, stsf_ref, ob_ref, stsb_ref,
             stf_ref, stb_ref):
        @pl.when(pl.program_id(0) == 0)
        def _():
            stf_ref[...] = stf0[...]
            stb_ref[...] = stb0[...]

        consts = (hk_ref[...], hv_ref[...], bd_ref[...])
        scan(False, qf, kf, vf, laf, of_ref, stsf_ref, stf_ref, consts)
        scan(True, qb, kb, vb, lab, ob_ref, stsb_ref, stb_ref, consts)

    full = lambda a: pl.BlockSpec(a.shape, lambda i: (0,) * a.ndim)
    outs = lambda order: [pl.BlockSpec((tb, GVW), lambda i: (order(i), 0)),
                          pl.BlockSpec((1, GLA_GROUPS, GVG, GKG), lambda i: (order(i), 0, 0, 0))]
    return pl.pallas_call(
        body, name="gla_fwd", grid=(nblk,),
        in_specs=_gla_specs(s, tb, up) + _gla_specs(s, tb, down) + [full(st_f0), full(st_b0)]
        + [full(m) for m in masks],
        out_specs=outs(up) + outs(down),
        out_shape=[jax.ShapeDtypeStruct((s, GVW), F32), jax.ShapeDtypeStruct((nblk, GLA_GROUPS, GVG, GKG), F32)] * 2,
        scratch_shapes=[pltpu.VMEM((GLA_GROUPS, GVG, GKG), F32)] * 2,
        compiler_params=_cp("arbitrary"),
    )(p, p, p, la_f, p, p, p, la_b, st_f0, st_b0, *masks)


def _gla_bwd(p, la_f, la_b, sts_f, sts_b, do, after=None):
    s = p.shape[0]
    tb = GLA_BLOCK_CHUNKS * GLA_CHUNK
    nblk = s // tb
    up, down = (lambda i: i), (lambda i: nblk - 1 - i)
    masks = _gla_masks(GLA_GROUP_HEADS)
    follow = () if after is None else (after,)

    def back(rev, q_ref, k_ref, v_ref, la_ref, sts_ref, do_ref, dq_ref, dk_ref, dv_ref, dla_ref, dst0_ref, dst_ref,
             consts):
        def block(q, k, v, la, st):
            outs = [None] * GLA_BLOCK_CHUNKS
            chunks = range(GLA_BLOCK_CHUNKS)
            for ci in (reversed(chunks) if rev else chunks):
                outs[ci], st = _f_gla_chunk(q[ci], k[ci], v[ci], la[ci], st, *consts, rev)
            return tuple(outs), st

        for g in range(GLA_GROUPS):
            gk, gv = slice(g * GKG, (g + 1) * GKG), slice(g * GVG, (g + 1) * GVG)
            split = lambda r, cols: tuple(r[ci * GLA_CHUNK:(ci + 1) * GLA_CHUNK, cols].astype(F32)
                                          for ci in range(GLA_BLOCK_CHUNKS))
            _, vjp = jax.vjp(block, split(q_ref, gk), split(k_ref, gk), split(v_ref, gv), split(la_ref, gk),
                             sts_ref[0, g])
            dq, dk, dv, dla, dst = vjp((split(do_ref, gv), dst_ref[g]))
            for ci in range(GLA_BLOCK_CHUNKS):
                rows = slice(ci * GLA_CHUNK, (ci + 1) * GLA_CHUNK)
                dq_ref[rows, gk], dk_ref[rows, gk] = dq[ci].astype(BF16), dk[ci].astype(BF16)
                dv_ref[rows, gv], dla_ref[rows, gk] = dv[ci].astype(BF16), dla[ci]
            dst_ref[g] = dst
            dst0_ref[g] = dst

    def body(*refs):
        ins, (hk_ref, hv_ref, bd_ref) = refs[:12], refs[12:15]
        outs = refs[15 + len(follow):]

        @pl.when(pl.program_id(0) == 0)
        def _():
            outs[10][...] = jnp.zeros_like(outs[10])
            outs[11][...] = jnp.zeros_like(outs[11])

        consts = (hk_ref[...], hv_ref[...], bd_ref[...])
        back(False, *ins[:6], *outs[:5], outs[10], consts)
        back(True, *ins[6:], *outs[5:10], outs[11], consts)

    full = lambda a: pl.BlockSpec(a.shape, lambda i: (0,) * a.ndim)

    def ins(order):
        return _gla_specs(s, tb, order) + [pl.BlockSpec((1, GLA_GROUPS, GVG, GKG), lambda i: (order(i), 0, 0, 0)),
                                           pl.BlockSpec((tb, GVW), lambda i: (order(i), 0))]

    def outs(order):
        blk = lambda w: pl.BlockSpec((tb, w), lambda i: (order(i), 0))
        return [blk(GKW), blk(GKW), blk(GVW), blk(GKW), pl.BlockSpec((GLA_GROUPS, GVG, GKG), lambda i: (0, 0, 0))]

    shapes = [jax.ShapeDtypeStruct((s, GKW), BF16), jax.ShapeDtypeStruct((s, GKW), BF16),
              jax.ShapeDtypeStruct((s, GVW), BF16), jax.ShapeDtypeStruct((s, GKW), F32),
              jax.ShapeDtypeStruct((GLA_GROUPS, GVG, GKG), F32)]
    both = pl.pallas_call(
        body, name="gla_bwd", grid=(nblk,),
        in_specs=ins(down) + ins(up) + [full(m) for m in masks] + [pl.BlockSpec(memory_space=pl.ANY)] * len(follow),
        out_specs=outs(down) + outs(up), out_shape=shapes * 2,
        scratch_shapes=[pltpu.VMEM((GLA_GROUPS, GVG, GKG), F32)] * 2,
        compiler_params=_cp("arbitrary"),
    )(p, p, p, la_f, sts_f, do, p, p, p, la_b, sts_b, do, *masks, *follow)
    return both[:5], both[5:]


def _f_ctx_state(k, v, la_f, la_b, bd_t):
    c = k.shape[0]
    after = _nn_mask(_tri(c, True, strict=True).astype(F32), la_f)
    before = _nn_mask(_tri(c, False, strict=True).astype(F32), la_b)
    return bd_t * _tn(v, k * jnp.exp(after)), bd_t * _tn(v, k * jnp.exp(before))


def _ctx_state(pc, la_f, la_b):
    c = pc.shape[0]
    bd_t = _gla_masks()[2]

    def body(k_ref, v_ref, lf_ref, lb_ref, bd_ref, sf_ref, sb_ref):
        sf_ref[...], sb_ref[...] = _f_ctx_state(k_ref[...], v_ref[...], lf_ref[...], lb_ref[...], bd_ref[...])

    full = lambda a: pl.BlockSpec(a.shape, lambda i: (0, 0))
    return pl.pallas_call(
        body, name="ctx_state_fwd", grid=(1,),
        in_specs=[pl.BlockSpec((c, GKW), lambda i: (0, C_GK // GKW)), pl.BlockSpec((c, GVW), lambda i: (0, C_GV // GVW)),
                  full(la_f), full(la_b), full(bd_t)],
        out_specs=[pl.BlockSpec((GVW, GKW), lambda i: (0, 0))] * 2,
        out_shape=[jax.ShapeDtypeStruct((GVW, GKW), F32)] * 2,
        compiler_params=_cp("arbitrary"),
    )(pc, pc, la_f, la_b, bd_t)


def _ctx_state_bwd(pc, la_f, la_b, dsf, dsb):
    c = pc.shape[0]
    bd_t = _gla_masks()[2]

    def body(k_ref, v_ref, lf_ref, lb_ref, bd_ref, dsf_ref, dsb_ref, dk_ref, dv_ref, dlf_ref, dlb_ref):
        _, vjp = jax.vjp(lambda k, v, lf, lb: _f_ctx_state(k, v, lf, lb, bd_ref[...]),
                         k_ref[...], v_ref[...], lf_ref[...], lb_ref[...])
        dk, dv, dlf, dlb = vjp((dsf_ref[...], dsb_ref[...]))
        dk_ref[...], dv_ref[...] = dk.astype(BF16), dv.astype(BF16)
        dlf_ref[...], dlb_ref[...] = dlf, dlb

    full = lambda a: pl.BlockSpec(a.shape, lambda i: (0, 0))
    return pl.pallas_call(
        body, name="ctx_state_bwd", grid=(1,),
        in_specs=[pl.BlockSpec((c, GKW), lambda i: (0, C_GK // GKW)), pl.BlockSpec((c, GVW), lambda i: (0, C_GV // GVW)),
                  full(la_f), full(la_b), full(bd_t), full(dsf), full(dsb)],
        out_specs=[pl.BlockSpec((c, GKW), lambda i: (0, 0)), pl.BlockSpec((c, GVW), lambda i: (0, 0)),
                   pl.BlockSpec((c, GKW), lambda i: (0, 0)), pl.BlockSpec((c, GKW), lambda i: (0, 0))],
        out_shape=[jax.ShapeDtypeStruct((c, GKW), BF16), jax.ShapeDtypeStruct((c, GVW), BF16),
                   jax.ShapeDtypeStruct((c, GKW), F32), jax.ShapeDtypeStruct((c, GKW), F32)],
        compiler_params=_cp("arbitrary"),
    )(pc, pc, la_f, la_b, bd_t, dsf, dsb)


_SRC_COLS = ((0, QW), (QW + 2 * KVW + 2 * GKW, GVW), (QW + 2 * KVW + 2 * GKW + GVW, GVW), (QW, KVW), (QW + KVW, KVW),
             (QW + 2 * KVW, GKW), (QW + 2 * KVW + GKW, GKW), (IN_COLS - 2 * GATE_RANK, 2 * GATE_RANK))
_DST_COLS = (C_Q, C_GV, C_GG, C_K, C_V, C_GQ, C_GK, C_Z)


def _pack_w_in(w_in):
    parts = [w_in[:, s:s + n] for s, n in _SRC_COLS]
    parts.append(jnp.zeros((w_in.shape[0], IN_PAD - C_Z - 2 * GATE_RANK), w_in.dtype))
    return jnp.concatenate(parts, axis=1)


def _unpack_w_in_grad(g):
    by_src = sorted(zip(_SRC_COLS, _DST_COLS))
    return jnp.concatenate([g[:, d:d + n] for (_, n), d in by_src], axis=1)


def _prep_gate_weights(w_gate_fwd, w_gate_bwd):
    pad_rows = lambda w, at: jnp.zeros((LANES, GKW), F32).at[at:at + GATE_RANK].set(w)
    return {"wg_f": pad_rows(w_gate_fwd, 0), "wg_b": pad_rows(w_gate_bwd, GATE_RANK)}


def _local_step(x, ctx, target, ada, ada_c, w, late_weights, reduce_behind=None, reduce_w_in=None):
    s, d = x.shape
    sh1, sc1, gt1, sh2, sc2, gt2 = [ada[:, i * d:(i + 1) * d] for i in range(6)]
    sh1c, sc1c = ada_c[:, :d], ada_c[:, d:2 * d]
    cos, sin = _rope_tables(s)
    gt = jnp.tile(w["g_gla_norm"], (1, GLA_HEADS))

    h = _norm_mod("pre_mix", x, w["g_pre_mix"], sh1, sc1)
    hc = _norm_mod("pre_mix_ctx", ctx, w["g_pre_mix"], sh1c, sc1c)
    w_in, token = w["w_in"](h, cos, sin)
    p = _mm("proj_in", h, w_in, "nn", after=token)
    pc = _mm("proj_in_ctx", hc, w_in, "nn")
    q_rot, k_rot, v_b = _rope_fwd("rope", p, cos, sin)
    pad = ((BLOCK, BLOCK), (0, 0))
    kp, vp = jnp.pad(k_rot, pad), jnp.pad(v_b, pad)
    kc, vc = pc[:, C_K:C_K + KVW].astype(BF16), pc[:, C_V:C_V + KVW].astype(BF16)
    attn = _attn_fwd(q_rot, kp, vp, kc, vc, w["attn_sink"])
    gate_w = (w["wg_f"], w["wg_b"], w["b_gate_fwd"], w["b_gate_bwd"])
    la_f, la_b = _gate_fwd("gate", p, *gate_w)
    la_fc, la_bc = _gate_fwd("gate_ctx", pc, *gate_w)
    st_f0, st_b0 = _ctx_state(pc, la_fc, la_bc)
    o_f, sts_f, o_b, sts_b = _gla_fwd(p, la_f, la_b, _group_states(st_f0), _group_states(st_b0))
    mix = _gla_out("gla_out", attn, o_f, o_b, p, gt)
    w_out, w_ffn_in_t, w_ffn_out = late_weights(mix)
    y = _mm("proj_out", mix, w_out, "nn", BF16)
    x1, h2 = _post_res_norm_mod("post_mix_pre_ffn", x, y, w["g_post_mix"], gt1, w["g_pre_ffn"], sh2, sc2)
    u, a = _ffn_in_swiglu("ffn_in", h2, w_ffn_in_t)
    f = _mm("ffn_out", a, w_ffn_out, "nn", BF16)
    g = {}
    dx2, df, loss, g["g_post_ffn"], dgt2 = _post_res_loss("post_ffn_loss", x1, f, w["g_post_ffn"], gt2, target)

    late_rows = {"w_ffn_in_t": w_ffn_in_t.shape[0] // N_CHIP, "w_ffn_out": w_ffn_out.shape[0] // N_CHIP,
                 "w_out": w_out.shape[0] // N_CHIP}
    order = sorted(late_rows, key=lambda n: -late_rows[n])
    offsets, slab_rows = _slab_layout([late_rows[n] for n in order])
    late_at, slab_shape = dict(zip(order, offsets)), (N_CHIP, slab_rows, d)
    slab = _slab_zero_gaps("late_grads_gaps", slab_shape, [late_rows[n] for n in order], offsets)
    slab = _dw_into_slab("ffn_out_dw", a, df, slab, slab_shape, late_at["w_ffn_out"])
    du = _ffn_out_dx_swiglu_bwd("ffn_out_dx", df, w_ffn_out, u)
    dh2 = _mm("ffn_in_dx", du, w_ffn_in_t, "nn", BF16)
    slab = _dw_into_slab("ffn_in_dw", du, h2, slab, slab_shape, late_at["w_ffn_in_t"])
    dx1, dy, g["g_pre_ffn"], dsh2, dsc2, g["g_post_mix"], dgt1 = _norm_mod_post_res_bwd(
        "pre_ffn_post_mix_bwd", dh2, dx2, x1, y, w["g_pre_ffn"], sh2, sc2, w["g_post_mix"], gt1)
    dmix = _mm("proj_out_dx", dy, w_out, "nt", BF16)
    slab = _dw_into_slab("proj_out_dw", mix, dy, slab, slab_shape, late_at["w_out"])
    g["late"], g["late_at"], g["late_rows"] = slab, late_at, late_rows
    rb, sink, token = reduce_behind, w["attn_sink"], None
    if rb is not None:
        gt = _behind(gt, rb.start_slab(slab))
    d_o, dgg, dgt = _gla_out_bwd("gla_out_bwd", dmix, o_f, o_b, p, gt)
    g["g_gla_norm"] = jnp.sum(dgt.reshape(GLA_HEADS, GLA_DV), axis=0, keepdims=True)
    if rb is not None:
        token = rb.pair(dgg)
    gla_f, gla_b = _gla_bwd(p, la_f, la_b, sts_f, sts_b, d_o, token)
    (dla_f, dst_f0), (dla_b, dst_b0) = gla_f[3:], gla_b[3:]
    dst_f0, dst_b0 = _ungroup_states(dst_f0), _ungroup_states(dst_b0)
    if rb is not None:
        sink = _behind(sink, rb.total(dla_b))
    dgkc, dgvc, dla_fc, dla_bc = _ctx_state_bwd(pc, la_fc, la_bc, dst_f0, dst_b0)
    dz, dwf, dwb, dbf, dbb = _gate_bwd("gate_bwd", p, dla_f, dla_b, *gate_w)
    dzc, dwfc, dwbc, dbfc, dbbc = _gate_bwd("gate_ctx_bwd", pc, dla_fc, dla_bc, *gate_w)
    g["w_gate_fwd"] = (dwf + dwfc)[:GATE_RANK]
    g["w_gate_bwd"] = (dwb + dwbc)[GATE_RANK:2 * GATE_RANK]
    g["b_gate_fwd"], g["b_gate_bwd"] = dbf + dbfc, dbb + dbbc
    dq_rot, dkp, dvp, dkc, dvc, g["attn_sink"] = _attn_bwd(dmix, attn, q_rot, kp, vp, kc, vc, sink)
    if rb is not None:
        g["late"] = rb.result(dq_rot)
    dp = _proj_grad("proj_grad", dq_rot, dkp[BLOCK:BLOCK + s], dvp[BLOCK:BLOCK + s], cos, sin, gla_f[:3], gla_b[:3],
                    dgg, dz)
    c_rows = ctx.shape[0]
    zeros = lambda n: jnp.zeros((c_rows, n), BF16)
    dpc = jnp.concatenate([zeros(QW), dgvc, zeros(GVW), dkc.astype(BF16), dvc.astype(BF16), zeros(GKW), dgkc, dzc],
                          axis=1)
    g["w_in"] = _mm("proj_in_dw", h, dp, "tn", init=_mm("proj_in_ctx_dw", hc, dpc, "tn"))
    token = None if reduce_w_in is None else reduce_w_in.start(g["w_in"])
    dh = _mm("proj_in_dx", dp, w_in, "nt", BF16, after=token)
    dhc = _mm("proj_in_ctx_dx", dpc, w_in, "nt")
    if reduce_w_in is not None:
        sh1 = _behind(sh1, reduce_w_in.pair(dh))
    dx, dg_a, dsh1, dsc1 = _norm_mod_bwd("pre_mix_bwd", dh, dx1, x, w["g_pre_mix"], sh1, sc1)
    if reduce_w_in is not None:
        dsh1 = _behind(dsh1, reduce_w_in.total(dx))
    _, dg_b, dsh1c, dsc1c = _norm_mod_bwd("pre_mix_ctx_bwd", dhc, jnp.zeros_like(dhc), ctx, w["g_pre_mix"], sh1c,
                                          sc1c)
    g["g_pre_mix"] = dg_a + dg_b
    d_ada = jnp.concatenate([dsh1, dsc1, dgt1, dsh2, dsc2, dgt2], axis=1)
    d_ada_c = jnp.concatenate([dsh1c, dsc1c, jnp.zeros((1, 4 * d), F32)], axis=1)
    return loss, dx, g, d_ada, d_ada_c


HBM = pl.BlockSpec(memory_space=pltpu.HBM)
N_DEV, N_CHIP = 8, 4


def _place():
    x, y, c = lax.axis_index("x"), lax.axis_index("y"), lax.axis_index("c")
    return x, y, c, [(1 - x, y), (x, 1 - y), (1 - x, 1 - y)]


def _row_tile(n, mult, cap):
    return max(t for t in range(mult, min(n, cap) + 1, mult) if n % t == 0)


def _ag_small(name, v, after=None):
    follow = () if after is None else (after,)

    def body(v_ref, *rest):
        out_ref, send_sems, recv_sems = rest[len(follow):]
        x, y, c, _ = _place()
        out_ref[4 * x + 2 * y + c] = v_ref[...]

        def peer(r):
            return ((1 - x) if r & 4 else x, (1 - y) if r & 2 else y, (1 - c) if r & 1 else c)

        def copy(r, block):
            px, py, pc = block
            return pltpu.make_async_remote_copy(
                src_ref=v_ref, dst_ref=out_ref.at[4 * px + 2 * py + pc], send_sem=send_sems.at[r - 1],
                recv_sem=recv_sems.at[r - 1], device_id=peer(r), device_id_type=MESH)

        sends = [copy(r, (x, y, c)) for r in range(1, N_DEV)]
        for cp in sends:
            cp.start()
        for r in range(1, N_DEV):
            copy(r, peer(r)).wait_recv()
        for cp in sends:
            cp.wait_send()

    return pl.pallas_call(
        body, name=name, out_shape=jax.ShapeDtypeStruct((N_DEV,) + v.shape, v.dtype),
        in_specs=[pl.BlockSpec(memory_space=pltpu.VMEM)] + [pl.BlockSpec(memory_space=pl.ANY)] * len(follow),
        out_specs=pl.BlockSpec(memory_space=pltpu.VMEM),
        scratch_shapes=[pltpu.SemaphoreType.DMA((N_DEV - 1,)), pltpu.SemaphoreType.DMA((N_DEV - 1,))],
    )(v, *follow)


def _halves(c, rows, mult):
    hr = rows // 2
    return pl.ds(pl.multiple_of(c * hr, mult), hr), pl.ds(pl.multiple_of((1 - c) * hr, mult), hr)


def _add_half(name, g, a, c_idx):
    n_sh, hr, n = a.shape
    tr = _row_tile(hr, 16, 1024)
    nb = hr // tr

    def body(c_ref, g_ref, a_ref, o_ref):
        o_ref[...] = (g_ref[...] + a_ref[...]).astype(o_ref.dtype)

    return pl.pallas_call(
        body, name=name, out_shape=jax.ShapeDtypeStruct(a.shape, BF16),
        grid_spec=pltpu.PrefetchScalarGridSpec(
            num_scalar_prefetch=1, grid=(n_sh, nb),
            in_specs=[pl.BlockSpec((1, tr, n), lambda s, i, c_ref: (s, c_ref[0] * nb + i, 0)),
                      pl.BlockSpec((1, tr, n), lambda s, i, c_ref: (s, i, 0))],
            out_specs=pl.BlockSpec((1, tr, n), lambda s, i, c_ref: (s, i, 0))),
        compiler_params=_cp("parallel", "parallel"),
    )(c_idx, g, a)


def _sum_chips(name, b, c_idx):
    n_sh, hr, n = b.shape
    tr = _row_tile(hr, 16, 1024)
    nb = hr // tr

    def body(c_ref, b0, b1, b2, b3, o_ref):
        o_ref[...] = ((b0[0].astype(F32) + b1[0].astype(F32)) + b2[0].astype(F32)) + b3[0].astype(F32)

    return pl.pallas_call(
        body, name=name, out_shape=jax.ShapeDtypeStruct((2 * hr, n), F32),
        grid_spec=pltpu.PrefetchScalarGridSpec(
            num_scalar_prefetch=1, grid=(nb,),
            in_specs=[pl.BlockSpec((1, tr, n), functools.partial(lambda i, c_ref, k: (k, i, 0), k=k))
                      for k in range(n_sh)],
            out_specs=pl.BlockSpec((tr, n), lambda i, c_ref: (c_ref[0] * nb + i, 0))),
        compiler_params=_cp("parallel"),
    )(c_idx, b, b, b, b)


SEM = pl.BlockSpec(memory_space=pltpu.SEMAPHORE)
ANY = pl.BlockSpec(memory_space=pl.ANY)
DATAFLOW = pltpu.SideEffectType.DATAFLOW_SIDE_EFFECTING


def _remote(src, dst, send_sems, recv_sems, k, to):
    return pltpu.make_async_remote_copy(src_ref=src, dst_ref=dst, send_sem=send_sems.at[k], recv_sem=recv_sems.at[k],
                                        device_id=to, device_id_type=MESH)


def _split_copy(name, src, land_shape, land_dtype, n, plan, after=None):
    after = jnp.zeros((8, LANES), F32) if after is None else after

    def start_body(src_ref, land_ref, after_ref, send_sems, recv_sems, src_thru, land_thru, token):
        for cp in plan(src_ref, land_ref, send_sems, recv_sems)[0]:
            cp.start()
        token[...] = jnp.zeros_like(token)

    sems = pltpu.SemaphoreType.DMA((n,))
    send_sems, recv_sems, src_thru, land_thru, token = pl.pallas_call(
        start_body, name=name + "_start",
        out_shape=(sems, sems, pltpu.HBM(src.shape, src.dtype), pltpu.HBM(land_shape, land_dtype),
                   jax.ShapeDtypeStruct((8, LANES), F32)),
        in_specs=(HBM, HBM, ANY), out_specs=(SEM, SEM, HBM, HBM, pl.BlockSpec(memory_space=pltpu.VMEM)),
        input_output_aliases={0: 2, 1: 3}, compiler_params=pltpu.CompilerParams(has_side_effects=DATAFLOW),
    )(pltpu.with_memory_space_constraint(src, pltpu.HBM),
      pltpu.with_memory_space_constraint(lax.empty(land_shape, land_dtype), pltpu.HBM), after)

    def wait(*after):
        def wait_body(src_ref, land_ref, send_sems, recv_sems, *rest):
            sent, received = plan(src_ref, land_ref, send_sems, recv_sems)
            for cp in sent:
                cp.wait_send()
            for cp in received:
                cp.wait_recv()

        return pl.pallas_call(
            wait_body, name=name + "_wait",
            out_shape=(pltpu.HBM(src.shape, src.dtype), pltpu.HBM(land_shape, land_dtype)),
            in_specs=(HBM, HBM, SEM, SEM) + (ANY,) * len(after), out_specs=(HBM, HBM),
            input_output_aliases={0: 0, 1: 1}, compiler_params=pltpu.CompilerParams(has_side_effects=DATAFLOW),
        )(src_thru, land_thru, send_sems, recv_sems, *after)

    return token, wait


def _split_gather(name, shards, after):
    k, n, plan = len(shards), 3 * len(shards), _plan_gather

    def start_body(*refs):
        for cp in plan(refs[:k], refs[k:2 * k], refs[2 * k + 1], refs[2 * k + 2])[0]:
            cp.start()
        refs[-1][...] = jnp.zeros_like(refs[-1])

    sems = pltpu.SemaphoreType.DMA((n,))
    bufs = [pltpu.HBM(s.shape, s.dtype) for s in shards] + [pltpu.HBM((N_CHIP,) + s.shape, s.dtype) for s in shards]
    hbm = lambda t: pltpu.with_memory_space_constraint(t, pltpu.HBM)
    outs = pl.pallas_call(
        start_body, name=name + "_start", out_shape=(sems, sems, *bufs, jax.ShapeDtypeStruct((8, LANES), F32)),
        in_specs=(HBM,) * (2 * k) + (ANY,),
        out_specs=(SEM, SEM) + (HBM,) * (2 * k) + (pl.BlockSpec(memory_space=pltpu.VMEM),),
        input_output_aliases={i: 2 + i for i in range(2 * k)},
        compiler_params=pltpu.CompilerParams(has_side_effects=DATAFLOW),
    )(*[hbm(s) for s in shards], *[hbm(lax.empty((N_CHIP,) + s.shape, s.dtype)) for s in shards], after)
    send_sems, recv_sems, thru, token = outs[0], outs[1], outs[2:2 + 2 * k], outs[-1]

    def wait(*after):
        def wait_body(*refs):
            sent, received = plan(refs[:k], refs[k:2 * k], refs[2 * k], refs[2 * k + 1])
            for cp in sent:
                cp.wait_send()
            for cp in received:
                cp.wait_recv()

        res = pl.pallas_call(
            wait_body, name=name + "_wait", out_shape=tuple(bufs),
            in_specs=(HBM,) * (2 * k) + (SEM, SEM) + (ANY,) * len(after), out_specs=(HBM,) * (2 * k),
            input_output_aliases={i: i for i in range(2 * k)},
            compiler_params=pltpu.CompilerParams(has_side_effects=DATAFLOW),
        )(*thru, send_sems, recv_sems, *after)
        return res[:k], res[k:]

    return token, wait


def _behind(x, token):
    return x + token[0, 0]


def _plan_gather(src_refs, land_refs, send_sems, recv_sems):
    x, y, c, chips = _place()
    pairs = list(enumerate(zip(src_refs, land_refs)))
    sent = [_remote(s, l.at[2 * x + y], send_sems, recv_sems, 3 * i + j, (px, py, c))
            for i, (s, l) in pairs for j, (px, py) in enumerate(chips)]
    received = [_remote(s, l.at[2 * px + py], send_sems, recv_sems, 3 * i + j, (px, py, c))
                for i, (s, l) in pairs for j, (px, py) in enumerate(chips)]
    return sent, received


def _plan_swap(src_ref, land_ref, send_sems, recv_sems):
    x, y, c, _ = _place()
    _, other_half = _halves(c, src_ref.shape[1], 8)
    cp = _remote(src_ref.at[pl.ds(0, src_ref.shape[0]), other_half], land_ref, send_sems, recv_sems, 0, (x, y, 1 - c))
    return [cp], [cp]


def _plan_scatter(src_ref, land_ref, send_sems, recv_sems):
    x, y, c, chips = _place()
    sent = [_remote(src_ref.at[2 * px + py], land_ref.at[2 * x + y], send_sems, recv_sems, j, (px, py, c))
            for j, (px, py) in enumerate(chips)]
    received = [_remote(src_ref.at[2 * px + py], land_ref.at[2 * px + py], send_sems, recv_sems, j, (px, py, c))
                for j, (px, py) in enumerate(chips)]
    return sent, received


def _plan_share(src_ref, land_ref, send_sems, recv_sems):
    x, y, c, _ = _place()
    mine_half, other_half = _halves(c, src_ref.shape[0], 8)
    return ([_remote(src_ref.at[mine_half], src_ref.at[mine_half], send_sems, recv_sems, 0, (x, y, 1 - c))],
            [_remote(src_ref.at[other_half], src_ref.at[other_half], send_sems, recv_sems, 0, (x, y, 1 - c))])


class _GatherBehind:
    def __init__(self, name, shards, chip, after):
        self.chip = chip
        self.token, self.wait = _split_gather(name, shards, after)

    def result(self, *after):
        shards, lands = self.wait(*after)
        return [lax.dynamic_update_slice(land, shard[None], (self.chip, 0, 0)) for shard, land in zip(shards, lands)]


class _ReduceBehind:
    def __init__(self, name, chip, c_idx):
        self.name, self.chip, self.c_idx = name, chip, c_idx

    def start_slab(self, g):
        n_sh, rows, n = g.shape
        token, self.wait = _split_copy(self.name + "_swap", g, (n_sh, rows // 2, n), g.dtype, 1, _plan_swap)
        return token

    def pair(self, after):
        g, a = self.wait(after)
        h = _add_half(self.name + "_pair", g, a, self.c_idx)
        token, self.wait = _split_copy(self.name + "_scatter", h, h.shape, h.dtype, 3, _plan_scatter)
        return token

    def total(self, after):
        h, b = self.wait(after)
        b = lax.dynamic_update_slice(b, lax.dynamic_slice_in_dim(h, self.chip, 1, axis=0), (self.chip, 0, 0))
        f = _sum_chips(self.name + "_sum", b, self.c_idx)
        token, self.wait = _split_copy(self.name + "_share", f, (8, LANES), f.dtype, 1, _plan_share)
        return token

    def result(self, after):
        return self.wait(after)[0]


class _ReduceColsBehind(_ReduceBehind):
    def start(self, g_padded):
        g = _unpack_w_in_grad(g_padded)
        n = g.shape[1] // N_CHIP
        return self.start_slab(jnp.stack([g[:, k * n:(k + 1) * n] for k in range(N_CHIP)]))


def _f_adamw(w, g, m, v):
    m = ADAM_B1 * m + (1.0 - ADAM_B1) * g
    v = ADAM_B2 * v + (1.0 - ADAM_B2) * (g * g)
    m_hat = m / (1.0 - ADAM_B1 ** ADAM_STEP)
    v_hat = v / (1.0 - ADAM_B2 ** ADAM_STEP)
    return -ADAM_LR * (m_hat / (jnp.sqrt(v_hat) + ADAM_EPS) + ADAM_WD * w), m, v


def _adamw(name, w, g, m, v):
    rows, n = w.shape
    return _rowwise(name, lambda w, g, m, v: (_f_adamw(w, g, m, v), ()), rows, [(t, n, 0) for t in (w, g, m, v)], [],
                    [(n, F32)] * 3, [], tm=_row_tile(rows, 8, 256))


def _adamw_many(name, ws, gs, ms, vs):
    k = len(ws)

    def body(*refs):
        ins, outs = refs[:4 * k], refs[4 * k:]
        for i in range(k):
            res = _f_adamw(ins[i][...], ins[k + i][...], ins[2 * k + i][...], ins[3 * k + i][...])
            for j in range(3):
                outs[j * k + i][...] = res[j]

    out = pl.pallas_call(body, name=name, out_shape=[jax.ShapeDtypeStruct(w.shape, F32) for w in ws] * 3)(
        *ws, *gs, *ms, *vs)
    return out[:k], out[k:2 * k], out[2 * k:]


def _pack_rows(parts):
    rows = []
    for t in parts:
        t = t.reshape(-1)
        rows.append(jnp.pad(t, (0, -t.shape[0] % LANES)).reshape(-1, LANES))
    out = jnp.concatenate(rows, axis=0)
    return jnp.pad(out, ((0, -out.shape[0] % 8), (0, 0)))


def _unpack_rows(packed, shapes):
    out, r = [], 0
    for shp in shapes:
        n = int(np.prod(shp))
        nr = -(-n // LANES)
        out.append(packed[r:r + nr].reshape(-1)[:n].reshape(shp))
        r += nr
    return out


def _sum_blocks(name, g):
    def body(g_ref, o_ref):
        acc = g_ref[0]
        for k in range(1, g.shape[0]):
            acc = acc + g_ref[k]
        o_ref[...] = acc

    return pl.pallas_call(body, name=name, out_shape=jax.ShapeDtypeStruct(g.shape[1:], F32))(g)


def _silu(t):
    return t * _sigmoid(t)


def _ada_fwd(cc, w_ada):
    n = w_ada.shape[1]
    tn = _row_tile(n, LANES, 512)

    def body(cc_ref, w_ref, o_ref):
        o_ref[...] = _nn(_silu(cc_ref[...]), w_ref[...])

    return pl.pallas_call(
        body, name="ada_fwd", grid=(n // tn,), out_shape=jax.ShapeDtypeStruct((cc.shape[0], n), F32),
        in_specs=[pl.BlockSpec(cc.shape, lambda j: (0, 0)), pl.BlockSpec((w_ada.shape[0], tn), lambda j: (0, j))],
        out_specs=pl.BlockSpec((cc.shape[0], tn), lambda j: (0, j)), compiler_params=_cp("parallel"),
    )(cc, w_ada)


def _ada_bwd(cc, dm, w_ada):
    d, n = w_ada.shape
    tn = _row_tile(n, LANES, 512)

    def body(cc_ref, dm_ref, w_ref, gw_ref, ds_ref):
        @pl.when(pl.program_id(0) == 0)
        def _():
            ds_ref[...] = jnp.zeros_like(ds_ref)

        gw_ref[...] = _raw_dot("tn", _silu(cc_ref[...]), dm_ref[...], True)
        ds_ref[...] += _raw_dot("nt", dm_ref[...], w_ref[...], False)

    return pl.pallas_call(
        body, name="ada_bwd", grid=(n // tn,),
        out_shape=[jax.ShapeDtypeStruct((d, n), F32), jax.ShapeDtypeStruct(cc.shape, F32)],
        in_specs=[pl.BlockSpec(cc.shape, lambda j: (0, 0)), pl.BlockSpec((cc.shape[0], tn), lambda j: (0, j)),
                  pl.BlockSpec((d, tn), lambda j: (0, j))],
        out_specs=[pl.BlockSpec((d, tn), lambda j: (0, j)), pl.BlockSpec(cc.shape, lambda j: (0, 0))],
        compiler_params=_cp("arbitrary"),
    )(cc, dm, w_ada)


def _c_ctx_grad(parts, c_ctx):
    def body(p_ref, c_ref, o_ref):
        ds = ((p_ref[0] + p_ref[1]) + p_ref[2]) + p_ref[3]
        _, vjp = jax.vjp(_silu, c_ref[...])
        o_ref[...] = vjp(ds)[0]

    return pl.pallas_call(body, name="c_ctx_grad", out_shape=jax.ShapeDtypeStruct(c_ctx.shape, F32))(parts, c_ctx)


def kernel(x, c, ctx, c_ctx, w_ada, b_ada, g_pre_mix, g_post_mix, g_pre_ffn, g_post_ffn, w_in, attn_sink, w_gate_fwd, b_gate_fwd, w_gate_bwd, b_gate_bwd, g_gla_norm, w_out, w_ffn_in, w_ffn_out, loss_target, m_c_ctx, m_w_ada, m_b_ada, m_g_pre_mix, m_g_post_mix, m_g_pre_ffn, m_g_post_ffn, m_w_in, m_attn_sink, m_w_gate_fwd, m_b_gate_fwd, m_w_gate_bwd, m_b_gate_bwd, m_g_gla_norm, m_w_out, m_w_ffn_in, m_w_ffn_out, v_c_ctx, v_w_ada, v_b_ada, v_g_pre_mix, v_g_post_mix, v_g_pre_ffn, v_g_post_ffn, v_w_in, v_attn_sink, v_w_gate_fwd, v_b_gate_fwd, v_w_gate_bwd, v_b_gate_bwd, v_g_gla_norm, v_w_out, v_w_ffn_in, v_w_ffn_out):
    xi, yi, ci = lax.axis_index("x"), lax.axis_index("y"), lax.axis_index("c")
    dev, chip = 4 * xi + 2 * yi + ci, 2 * xi + yi
    c_idx = jnp.reshape(ci, (1,)).astype(jnp.int32)
    d = x.shape[-1]
    n_ada, n_in, n_f = w_ada.shape[-1], w_in.shape[-1], w_ffn_in.shape[-1]
    r_out, r_f = w_out.shape[1], w_ffn_out.shape[1]
    n_gate = w_gate_fwd.shape[-1]
    by_chip = lambda t: t[0::2]

    rc = -(-d // LANES)
    g1 = _ag_small("gather_cond", _pack_rows([c[0], w_gate_fwd[0], w_gate_bwd[0]]))
    c_all = g1[:, :rc].reshape(N_DEV, -1)[:, :d]
    gr = GATE_RANK * n_gate // LANES
    gate_full = lambda off: jnp.transpose(by_chip(g1)[:, off:off + gr].reshape(N_CHIP, GATE_RANK, n_gate),
                                          (1, 0, 2)).reshape(GATE_RANK, N_CHIP * n_gate)
    wgf, wgb = gate_full(rc), gate_full(rc + gr)
    cc = jnp.concatenate([c_all, c_ctx[None, :], jnp.zeros((7, d), F32)], axis=0)

    g2 = _ag_small("gather_ada", _ada_fwd(cc, w_ada[0]).reshape(-1, LANES))
    ada_all = jnp.transpose(by_chip(g2).reshape(N_CHIP, 16, n_ada), (1, 0, 2)).reshape(16, N_CHIP * n_ada) + b_ada
    first = _GatherBehind("gather_w_in", [w_in[0].astype(BF16)], chip, g2)
    late_shards = [w_out[0].astype(BF16), jnp.transpose(w_ffn_in[0]).astype(BF16), w_ffn_out[0].astype(BF16)]
    late = []

    def first_weights(*after):
        w_in_g, = first.result(*after, *late_shards)
        late.append(_GatherBehind("gather_late", late_shards, chip, w_in_g))
        return _pack_w_in(jnp.concatenate([w_in_g[k] for k in range(N_CHIP)], axis=1)), late[0].token

    def late_weights(after):
        return [t.reshape(-1, d) for t in late[0].result(after)]

    ada_all = _behind(ada_all, first.token)
    ada = lax.dynamic_slice(ada_all, (dev, 0), (1, N_CHIP * n_ada))
    ada_c = ada_all[N_DEV:N_DEV + 1]

    w = _prep_gate_weights(wgf, wgb)
    w.update(w_in=first_weights, g_pre_mix=g_pre_mix, g_post_mix=g_post_mix, g_pre_ffn=g_pre_ffn, g_post_ffn=g_post_ffn,
             attn_sink=attn_sink, b_gate_fwd=b_gate_fwd, b_gate_bwd=b_gate_bwd, g_gla_norm=g_gla_norm)

    reduce_behind = _ReduceBehind("reduce_late", chip, c_idx)
    reduce_w_in = _ReduceColsBehind("reduce_w_in", chip, c_idx)
    loss_lanes, grad_x, g, d_ada, d_ada_c = _local_step(x[0], ctx[0], loss_target[0], ada, ada_c, w, late_weights,
                                                        reduce_behind, reduce_w_in)

    small = ("g_pre_mix", "g_post_mix", "g_pre_ffn", "g_post_ffn", "attn_sink", "b_gate_fwd", "b_gate_bwd",
             "g_gla_norm", "w_gate_fwd", "w_gate_bwd")
    shapes = [(1, 6 * d)] * 2 + [g[n].shape for n in small] + [(1, LANES)]
    g3 = _ag_small("gather_small_grads", _pack_rows([d_ada, d_ada_c] + [g[n] for n in small] + [loss_lanes]))
    tot = dict(zip(("d_ada", "d_ada_c") + small + ("loss",),
                   _unpack_rows(_sum_blocks("sum_small_grads", g3), shapes)))
    r_ada = 6 * d // LANES
    dm = jnp.concatenate([g3[:, :r_ada].reshape(N_DEV, 6 * d), tot["d_ada_c"], jnp.zeros((7, 6 * d), F32)], axis=0)
    grads = {n: tot[n] for n in small[:8]}
    grads["b_ada"] = _sum_blocks("sum_b_ada", dm.reshape(16, r_ada, LANES)).reshape(1, 6 * d)
    grads["w_gate_fwd"] = lax.dynamic_slice(tot["w_gate_fwd"], (0, chip * n_gate), (GATE_RANK, n_gate))[None]
    grads["w_gate_bwd"] = lax.dynamic_slice(tot["w_gate_bwd"], (0, chip * n_gate), (GATE_RANK, n_gate))[None]
    gw_ada, dsc = _ada_bwd(cc, lax.dynamic_slice(dm, (0, chip * n_ada), (16, n_ada)), w_ada[0])
    grads["w_ada"] = gw_ada[None]
    g4 = _ag_small("gather_c_ctx", _pack_rows([dsc[N_DEV]]))
    grads["c_ctx"] = _c_ctx_grad(by_chip(g4), _pack_rows([c_ctx])).reshape(-1)[:d]

    grads["w_in"] = reduce_w_in.result(g4)[None]
    part = lambda n: g["late"][g["late_at"][n]:g["late_at"][n] + g["late_rows"][n]]
    grads["w_ffn_in"], grads["w_ffn_out"], grads["w_out"] = (jnp.transpose(part("w_ffn_in_t"))[None],
                                                            part("w_ffn_out")[None], part("w_out")[None])

    names = ("c_ctx", "w_ada", "b_ada", "g_pre_mix", "g_post_mix", "g_pre_ffn", "g_post_ffn", "w_in", "attn_sink",
             "w_gate_fwd", "b_gate_fwd", "w_gate_bwd", "b_gate_bwd", "g_gla_norm", "w_out", "w_ffn_in", "w_ffn_out")
    weights = dict(zip(names, (c_ctx, w_ada, b_ada, g_pre_mix, g_post_mix, g_pre_ffn, g_post_ffn, w_in, attn_sink,
                               w_gate_fwd, b_gate_fwd, w_gate_bwd, b_gate_bwd, g_gla_norm, w_out, w_ffn_in,
                               w_ffn_out)))
    m_in = dict(zip(names, (m_c_ctx, m_w_ada, m_b_ada, m_g_pre_mix, m_g_post_mix, m_g_pre_ffn, m_g_post_ffn, m_w_in,
                            m_attn_sink, m_w_gate_fwd, m_b_gate_fwd, m_w_gate_bwd, m_b_gate_bwd, m_g_gla_norm,
                            m_w_out, m_w_ffn_in, m_w_ffn_out)))
    v_in = dict(zip(names, (v_c_ctx, v_w_ada, v_b_ada, v_g_pre_mix, v_g_post_mix, v_g_pre_ffn, v_g_post_ffn, v_w_in,
                            v_attn_sink, v_w_gate_fwd, v_b_gate_fwd, v_w_gate_bwd, v_b_gate_bwd, v_g_gla_norm,
                            v_w_out, v_w_ffn_in, v_w_ffn_out)))
    large = ("w_ada", "w_in", "w_out", "w_ffn_in", "w_ffn_out")
    tiny = tuple(n for n in names if n not in large)
    delta, new_m, new_v = {}, {}, {}
    for n in large:
        dl, nm, nv = _adamw("adamw_" + n, weights[n][0], grads[n][0], m_in[n][0], v_in[n][0])
        delta[n], new_m[n], new_v[n] = dl[None], nm[None], nv[None]
    for n in tiny:
        grads[n] = grads[n].reshape(weights[n].shape)
    as_rows = lambda t: t.reshape(-1, t.shape[-1])
    res = _adamw_many("adamw_small", *[[as_rows(t[n]) for n in tiny] for t in (weights, grads, m_in, v_in)])
    for out, vals in zip((delta, new_m, new_v), res):
        out.update({n: val.reshape(weights[n].shape) for n, val in zip(tiny, vals)})

    return (tot["loss"][0, 0], grad_x[None], *[grads[n] for n in names], *[delta[n] for n in names], *[new_m[n] for n in names],
            *[new_v[n] for n in names])
```

```python
import functools

import jax
import jax.numpy as jnp
import numpy as np
from jax import lax
from jax.experimental import pallas as pl
from jax.experimental.pallas import tpu as pltpu

F32 = jnp.float32
BF16 = jnp.bfloat16
MESH = pl.DeviceIdType.MESH

HEAD_DIM = 64
ATT_HEADS = 8
ATT_KV_HEADS = 2
ATT_GROUP = ATT_HEADS // ATT_KV_HEADS
WINDOW = 128
BLOCK = 128
GRID_W = 64
ROPE_BASE = 10000.0
GLA_HEADS = 8
GLA_DK = 32
GLA_DV = 64
GLA_CHUNK = 64
GATE_RANK = 16
GATE_TAU = 16.0
NEG_INF = -1e30
QW = ATT_HEADS * HEAD_DIM
KVW = ATT_KV_HEADS * HEAD_DIM
GKW = GLA_HEADS * GLA_DK
GVW = GLA_HEADS * GLA_DV
IN_COLS = QW + 2 * KVW + 2 * GKW + 2 * GVW + 2 * GATE_RANK
LANES = 128
IN_PAD = IN_COLS + LANES - 2 * GATE_RANK
C_Q, C_GV, C_GG = 0, QW, QW + GVW
C_K = C_GG + GVW
C_V = C_K + KVW
C_GQ = C_V + KVW
C_GK = C_GQ + GKW
C_Z = C_GK + GKW
MIX = QW + GVW

ADAM_LR, ADAM_B1, ADAM_B2, ADAM_EPS, ADAM_WD, ADAM_STEP = 0.001, 0.9, 0.999, 1e-08, 0.01, 10

VMEM_LIMIT = 56 * 1024 * 1024


def _cp(*sem):
    return pltpu.CompilerParams(dimension_semantics=sem, vmem_limit_bytes=VMEM_LIMIT)


def _pick(n, cands):
    for t in cands:
        if n % t == 0:
            return t
    return n


_DIMS = {"nn": (((1,), (0,)), ((), ())), "nt": (((1,), (1,)), ((), ())), "tn": (((0,), (0,)), ((), ()))}


def _raw_dot(mode, a, b, hi):
    dot = lambda u, v: lax.dot_general(u, v, _DIMS[mode], preferred_element_type=F32)
    if not hi:
        return dot(a.astype(BF16), b.astype(BF16))
    a, b = a.astype(F32), b.astype(F32)
    a_hi, b_hi = a.astype(BF16), b.astype(BF16)
    out = dot(a_hi, b_hi)
    if hi != "a":
        out = out + dot((a - a_hi.astype(F32)).astype(BF16), b_hi)
    if hi != "b":
        out = out + dot(a_hi, (b - b_hi.astype(F32)).astype(BF16))
    return out


def _make_dot(mode, hi):
    @jax.custom_vjp
    def dot(a, b):
        return _raw_dot(mode, a, b, hi)

    def fwd(a, b):
        return _raw_dot(mode, a, b, hi), (a, b)

    def bwd(res, dc):
        a, b = res
        if mode == "nn":
            return (_raw_dot("nt", dc, b, "b" if hi == "b" else bool(hi)),
                    _raw_dot("tn", a, dc, "a" if hi == "a" else bool(hi)))
        if mode == "nt":
            return _raw_dot("nn", dc, b, bool(hi)), _raw_dot("tn", dc, a, bool(hi))
        return _raw_dot("nt", b, dc, bool(hi)), _raw_dot("nn", a, dc, bool(hi))

    dot.defvjp(fwd, bwd)
    return dot


_nn, _nt, _tn = _make_dot("nn", False), _make_dot("nt", False), _make_dot("tn", False)
_nn_mask, _nn_by_exact = _make_dot("nn", "a"), _make_dot("nn", "b")


MM_VMEM_BUDGET = 44 * 1024 * 1024


def _halvings(n):
    out = [n]
    while out[-1] % (2 * LANES) == 0:
        out.append(out[-1] // 2)
    return out


def _mm_tiles(mode, m, n, k, a_bytes, b_bytes, o_bytes, init_bytes=0):
    tms = [t for t in dict.fromkeys((m, m // 2, m // 4, 2048, 1024, 512, 256, 128))
           if m % t == 0 and t % (LANES if mode == "tn" else 16) == 0 and t <= 4096] or [m]
    if mode == "tn":
        fits = [(k // tk + 0.5 * (m // tm), tm, tk)
                for tk in (4096, 2048, 1024, 512, 256, 128) if k % tk == 0 for tm in tms
                if 2 * (tk * tm * a_bytes + tk * n * b_bytes + tm * n * (o_bytes + init_bytes)) <= MM_VMEM_BUDGET]
        if fits:
            _, tm, tk = min(fits)
            return tm, n, tk
    tks = ([t for t in (512, 256, 128) if k % t == 0] or [k]) if mode == "tn" else _halvings(k)
    for tn in _halvings(n):
        for tk in tks:
            for tm in tms:
                acc = tm * tn * 4 if (k // tk > 1 and o_bytes != 4) else 0
                tiles = tm * tk * a_bytes + tk * tn * b_bytes + tm * tn * (o_bytes + init_bytes)
                if 2 * tiles + acc <= MM_VMEM_BUDGET:
                    return tm, tn, tk
    return tms[-1], _halvings(n)[-1], tks[-1]


def _mm(name, a, b, mode, out_dtype=F32, init=None, after=None):
    follow = () if after is None else (after,)
    if mode == "nn":
        (m, k), n = a.shape, b.shape[1]
    elif mode == "nt":
        (m, k), n = a.shape, b.shape[0]
    else:
        (k, m), n = a.shape, b.shape[1]
    tm, tn, tk = _mm_tiles(mode, m, n, k, a.dtype.itemsize, b.dtype.itemsize, jnp.dtype(out_dtype).itemsize,
                           0 if init is None else 4)
    nk = k // tk
    use_acc = nk > 1 and out_dtype != F32

    inits = () if init is None else (init,)

    def body(a_ref, b_ref, *rest):
        rest = rest[:len(inits)] + rest[len(inits) + len(follow):]
        o_ref, acc = rest[len(inits)], rest[len(inits) + 1:]
        part = _raw_dot(mode, a_ref[...], b_ref[...], False)
        first = lambda: part + rest[0][...] if inits else part
        if nk == 1:
            o_ref[...] = first().astype(o_ref.dtype)
            return
        acc_ref = acc[0] if use_acc else o_ref
        kk = pl.program_id(2)

        @pl.when(kk == 0)
        def _():
            acc_ref[...] = first()

        @pl.when(kk > 0)
        def _():
            acc_ref[...] += part

        if use_acc:
            @pl.when(kk == nk - 1)
            def _():
                o_ref[...] = acc_ref[...].astype(o_ref.dtype)

    if mode == "nn":
        a_spec = pl.BlockSpec((tm, tk), lambda i, j, kk: (i, kk))
        b_spec = pl.BlockSpec((tk, tn), lambda i, j, kk: (kk, j))
    elif mode == "nt":
        a_spec = pl.BlockSpec((tm, tk), lambda i, j, kk: (i, kk))
        b_spec = pl.BlockSpec((tn, tk), lambda i, j, kk: (j, kk))
    else:
        a_spec = pl.BlockSpec((tk, tm), lambda i, j, kk: (kk, i))
        b_spec = pl.BlockSpec((tk, tn), lambda i, j, kk: (kk, j))
    return pl.pallas_call(
        body, name=name, grid=(m // tm, n // tn, nk),
        in_specs=[a_spec, b_spec] + [pl.BlockSpec((tm, tn), lambda i, j, kk: (i, j))] * len(inits)
        + [pl.BlockSpec(memory_space=pl.ANY)] * len(follow),
        out_specs=pl.BlockSpec((tm, tn), lambda i, j, kk: (i, j)),
        out_shape=jax.ShapeDtypeStruct((m, n), out_dtype),
        scratch_shapes=[pltpu.VMEM((tm, tn), F32)] if use_acc else [],
        compiler_params=_cp("parallel", "parallel", "arbitrary"),
    )(a, b, *inits, *follow)


def _slab_layout(rows):
    offsets, at = [], 0
    for r in rows:
        at = -(-at // r) * r
        offsets.append(at)
        at += r
    return offsets, -(-at // 32) * 32


def _slab_zero_gaps(name, shape, rows, offsets):
    gaps = [(o + r, nxt) for o, r, nxt in zip(offsets, rows, offsets[1:] + [shape[1]]) if nxt > o + r]
    slab = None
    for i, (lo, hi) in enumerate(gaps):
        step = int(np.gcd(lo, hi - lo))

        def body(*refs):
            refs[-1][...] = jnp.zeros_like(refs[-1])

        slab = pl.pallas_call(
            body, name=f"{name}_{i}", grid=(shape[0], (hi - lo) // step), out_shape=jax.ShapeDtypeStruct(shape, F32),
            in_specs=[] if slab is None else [pl.BlockSpec(memory_space=pl.ANY)],
            out_specs=pl.BlockSpec((1, step, shape[2]), functools.partial(lambda k, j, b: (k, b + j, 0), b=lo // step)),
            input_output_aliases={} if slab is None else {0: 0}, compiler_params=_cp("parallel", "parallel"),
        )(*(() if slab is None else (slab,)))
    return slab


def _dw_into_slab(name, a, b, slab, shape, at):
    (k, m), n = a.shape, b.shape[1]
    r = m // N_CHIP
    fits = [(k // tk + 0.5 * (m // tm), tm, tk)
            for tk in (4096, 2048, 1024, 512, 256, 128) if k % tk == 0 for tm in (m, m // 2, r) if tm % LANES == 0
            if 2 * (tk * tm * a.dtype.itemsize + tk * n * b.dtype.itemsize + tm * n * 4) <= MM_VMEM_BUDGET]
    _, tm, tk = min(fits)
    per, nk = tm // r, k // tk

    def body(a_ref, b_ref, *rest):
        o_ref = rest[-1]
        part = _raw_dot("tn", a_ref[...], b_ref[...], False).reshape(o_ref.shape)
        if nk == 1:
            o_ref[...] = part
            return
        kk = pl.program_id(1)

        @pl.when(kk == 0)
        def _():
            o_ref[...] = part

        @pl.when(kk > 0)
        def _():
            o_ref[...] += part

    prev = () if slab is None else (slab,)
    return pl.pallas_call(
        body, name=name, grid=(m // tm, nk), out_shape=jax.ShapeDtypeStruct(shape, F32),
        in_specs=[pl.BlockSpec((tk, tm), lambda i, kk: (kk, i)), pl.BlockSpec((tk, n), lambda i, kk: (kk, 0))]
        + [pl.BlockSpec(memory_space=pl.ANY)] * len(prev),
        out_specs=pl.BlockSpec((per, r, n), lambda i, kk: (i, at // r, 0)),
        input_output_aliases={2: 0} if prev else {}, compiler_params=_cp("parallel", "arbitrary"),
    )(a, b, *prev)


def _rowwise(name, fn, rows, row_ins, full_ins, row_outs, acc_outs, tm=None):
    tm = tm or _pick(rows, (1024, 512, 256, 128))
    n_r, n_f, n_o, n_a = len(row_ins), len(full_ins), len(row_outs), len(acc_outs)

    def body(*refs):
        ins, outs = refs[:n_r + n_f], refs[n_r + n_f:]
        vals = [r[...].astype(F32) for r in ins]
        ro, ao = fn(*vals)
        for r, val in zip(outs[:n_o], ro):
            r[...] = val.astype(r.dtype)
        if n_a:
            @pl.when(pl.program_id(0) == 0)
            def _():
                for r in outs[n_o:]:
                    r[...] = jnp.zeros_like(r)

            for r, val in zip(outs[n_o:], ao):
                r[...] += val

    in_specs = [pl.BlockSpec((tm, w), functools.partial(lambda i, cb: (i, cb), cb=cb)) for _, w, cb in row_ins]
    in_specs += [pl.BlockSpec(a.shape, lambda i: (0, 0)) for a in full_ins]
    out_specs = [pl.BlockSpec((tm, w), lambda i: (i, 0)) for w, _ in row_outs]
    out_specs += [pl.BlockSpec(s, lambda i: (0, 0)) for s in acc_outs]
    out_shape = [jax.ShapeDtypeStruct((rows, w), dt) for w, dt in row_outs]
    out_shape += [jax.ShapeDtypeStruct(s, F32) for s in acc_outs]
    return pl.pallas_call(
        body, name=name, grid=(rows // tm,), in_specs=in_specs, out_specs=out_specs, out_shape=out_shape,
        compiler_params=_cp("arbitrary" if n_a else "parallel"),
    )(*[a for a, _, _ in row_ins], *full_ins)


def _rn(x):
    return x * lax.rsqrt(jnp.mean(x * x, axis=-1, keepdims=True) + 1e-6)


def _sigmoid(t):
    return 1.0 / (1.0 + jnp.exp(-t))


def _f_norm_mod(x, g, sh, sc):
    return _rn(x) * g * (1.0 + sc) + sh


def _f_post_res(xr, y, g, gate):
    return xr + gate * (_rn(y) * g)


@jax.custom_vjp
def _f_swiglu(g, u):
    return g * _sigmoid(g) * u


def _f_swiglu_fwd(g, u):
    s = _sigmoid(g)
    return g * s * u, (g, u, s)


def _f_swiglu_bwd(res, da):
    g, u, s = res
    gs = g * s
    return da * u * (s + gs * (1.0 - s)), da * gs


_f_swiglu.defvjp(_f_swiglu_fwd, _f_swiglu_bwd)


def _logsig(u):
    return jnp.minimum(u, 0.0) - jnp.log(1.0 + jnp.exp(-jnp.abs(u)))


def _f_gate(z, wf, wb, bf, bb):
    return _logsig(_nn(z, wf) + bf) / GATE_TAU, _logsig(_nn(z, wb) + bb) / GATE_TAU


def _f_gla_out(of, ob, gg, gt, bd):
    o = of + ob
    ms = _nn_by_exact(o * o, bd)
    return o * lax.rsqrt(ms + 1e-6) * gt * (gg * _sigmoid(gg))


def _norm_mod(name, x, g, sh, sc):
    rows, d = x.shape
    return _rowwise(name, lambda x, g, sh, sc: ((_f_norm_mod(x, g, sh, sc),), ()), rows,
                    [(x, d, 0)], [g, sh, sc], [(d, BF16)], [])[0]


def _rn_bwd(x, dn):
    r = lax.rsqrt(jnp.mean(x * x, axis=-1, keepdims=True) + 1e-6)
    n = x * r
    return r * (dn - n * jnp.mean(dn * n, axis=-1, keepdims=True)), n


def _norm_mod_grads(dh, x, g, sc):
    dx, n = _rn_bwd(x, dh * (g * (1.0 + sc)))
    t = jnp.sum(dh * n, axis=0, keepdims=True)
    return dx, (1.0 + sc) * t, jnp.sum(dh, axis=0, keepdims=True), g * t


def _post_res_grads(dout, y, g, gate):
    dy, n = _rn_bwd(y, dout * (gate * g))
    t = jnp.sum(dout * n, axis=0, keepdims=True)
    return dy, gate * t, g * t


def _norm_mod_bwd(name, dh, dres, x, g, sh, sc):
    rows, d = x.shape

    def fn(dh, dres, x, g, sh, sc):
        dx, dg, dsh, dsc = _norm_mod_grads(dh, x, g, sc)
        return (dx + dres,), (dg, dsh, dsc)

    return _rowwise(name, fn, rows, [(dh, d, 0), (dres, d, 0), (x, d, 0)], [g, sh, sc], [(d, F32)],
                    [(1, d)] * 3)


def _post_res_norm_mod(name, xr, y, g_post, gate, g_pre, sh, sc):
    rows, d = xr.shape

    def fn(xr, y, g_post, gate, g_pre, sh, sc):
        x1 = _f_post_res(xr, y, g_post, gate)
        return (x1, _f_norm_mod(x1, g_pre, sh, sc)), ()

    return _rowwise(name, fn, rows, [(xr, d, 0), (y, d, 0)], [g_post, gate, g_pre, sh, sc], [(d, F32), (d, BF16)], [])


def _norm_mod_post_res_bwd(name, dh, dres, x1, y, g_pre, sh, sc, g_post, gate):
    rows, d = x1.shape

    def fn(dh, dres, x1, y, g_pre, sh, sc, g_post, gate):
        dx1, dg_pre, dsh, dsc = _norm_mod_grads(dh, x1, g_pre, sc)
        dx1 = dx1 + dres
        dy, dg_post, dgate = _post_res_grads(dx1, y, g_post, gate)
        return (dx1, dy), (dg_pre, dsh, dsc, dg_post, dgate)

    return _rowwise(name, fn, rows, [(dh, d, 0), (dres, d, 0), (x1, d, 0), (y, d, 0)], [g_pre, sh, sc, g_post, gate],
                    [(d, F32), (d, BF16)], [(1, d)] * 5)


def _post_res_loss(name, xr, y, g, gate, target):
    rows, d = xr.shape

    def fn(xr, y, target, g, gate):
        diff = _f_post_res(xr, y, g, gate) - target
        part = 0.5 * jnp.sum(jnp.mean(diff * diff, axis=-1, keepdims=True), axis=0, keepdims=True)
        dx2 = diff * (1.0 / d)
        dy, dg, dgate = _post_res_grads(dx2, y, g, gate)
        return (dx2, dy), (jnp.broadcast_to(part, (1, LANES)), dg, dgate)

    return _rowwise(name, fn, rows, [(xr, d, 0), (y, d, 0), (target, d, 0)], [g, gate], [(d, F32), (d, BF16)],
                    [(1, LANES), (1, d), (1, d)])


def _mm_rows(name, a, b, mode, fn, extras, outs):
    m, k = a.shape
    tm = _pick(m, (256, 128))

    def body(a_ref, b_ref, *rest):
        tiles = fn(_raw_dot(mode, a_ref[...], b_ref[...], False), *[e[...] for e in rest[:len(extras)]])
        for r, val in zip(rest[len(extras):], tiles):
            r[...] = val.astype(r.dtype)

    row = lambda w: pl.BlockSpec((tm, w), lambda i: (i, 0))
    return pl.pallas_call(
        body, name=name, grid=(m // tm,),
        in_specs=[row(k), pl.BlockSpec(b.shape, lambda i: (0, 0))] + [row(e.shape[1]) for e in extras],
        out_specs=[row(w) for w, _ in outs], out_shape=[jax.ShapeDtypeStruct((m, w), dt) for w, dt in outs],
        compiler_params=_cp("parallel"),
    )(a, b, *extras)


def _ffn_in_swiglu(name, h, w_t):
    f = w_t.shape[0] // 2
    fn = lambda u: (u, _f_swiglu(u[:, :f], u[:, f:]))
    return _mm_rows(name, h, w_t, "nt", fn, [], [(2 * f, BF16), (f, BF16)])


def _ffn_out_dx_swiglu_bwd(name, df, w_out, u):
    f = w_out.shape[0]

    def fn(da, u):
        u = u.astype(F32)
        _, vjp = jax.vjp(_f_swiglu, u[:, :f], u[:, f:])
        return (jnp.concatenate(vjp(da), axis=1),)

    return _mm_rows(name, df, w_out, "nt", fn, [u], [(2 * f, BF16)])[0]


def _gate_fwd(name, p, wf, wb, bf, bb):
    rows = p.shape[0]
    return _rowwise(name, lambda z, wf, wb, bf, bb: (_f_gate(z, wf, wb, bf, bb), ()), rows,
                    [(p, LANES, C_Z // LANES)], [wf, wb, bf, bb], [(GKW, F32)] * 2, [])


def _gate_bwd(name, p, dla_f, dla_b, wf, wb, bf, bb):
    rows = p.shape[0]

    def fn(z, dlf, dlb, wf, wb, bf, bb):
        _, vjp = jax.vjp(_f_gate, z, wf, wb, bf, bb)
        dz, dwf, dwb, dbf, dbb = vjp((dlf, dlb))
        return (dz,), (dwf, dwb, dbf, dbb)

    return _rowwise(name, fn, rows, [(p, LANES, C_Z // LANES), (dla_f, GKW, 0), (dla_b, GKW, 0)],
                    [wf, wb, bf, bb], [(LANES, BF16)], [(LANES, GKW), (LANES, GKW), (1, GKW), (1, GKW)])


def _head_mean_matrix():
    h = np.arange(GVW) // GLA_DV
    return jnp.asarray((h[:, None] == h[None, :]).astype(np.float32) / GLA_DV)


def _gla_out(name, attn, of, ob, p, gt):
    rows = of.shape[0]
    bd = _head_mean_matrix()
    fn = lambda attn, of, ob, gg, gt, bd: ((jnp.concatenate([attn, _f_gla_out(of, ob, gg, gt, bd)], axis=1),), ())
    return _rowwise(name, fn, rows, [(attn, QW, 0), (of, GVW, 0), (ob, GVW, 0), (p, GVW, C_GG // GVW)], [gt, bd],
                    [(MIX, BF16)], [])[0]


def _gla_out_bwd(name, dmix, of, ob, p, gt):
    rows = of.shape[0]
    bd = _head_mean_matrix()

    def fn(dm, of, ob, gg, gt, bd):
        _, vjp = jax.vjp(lambda of, gg, gt: _f_gla_out(of, ob, gg, gt, bd), of, gg, gt)
        do, dgg, dgt = vjp(dm)
        return (do, dgg), (dgt,)

    return _rowwise(name, fn, rows, [(dmix, GVW, 1), (of, GVW, 0), (ob, GVW, 0), (p, GVW, C_GG // GVW)], [gt, bd],
                    [(GVW, BF16), (GVW, BF16)], [(1, GVW)])


def _rope_tables(n_tokens):
    t = jnp.arange(n_tokens)
    row = (t // GRID_W).astype(F32)
    col = (t % GRID_W).astype(F32)
    half = HEAD_DIM // 2
    inv_freq = ROPE_BASE ** (-jnp.arange(0, half, 2, dtype=F32) / half)
    ang_r = row[:, None] * inv_freq[None, :]
    ang_c = col[:, None] * inv_freq[None, :]
    ang = jnp.concatenate([ang_r, ang_r, ang_c, ang_c], axis=-1)
    sign = jnp.concatenate([-jnp.ones((16,), F32), jnp.ones((16,), F32)] * 2)
    cos, sin = jnp.cos(ang), jnp.sin(ang) * sign[None, :]
    return jnp.tile(cos, (1, 2)), jnp.tile(sin, (1, 2))


def _rot_pairs(x):
    w = x.shape[-1]
    lane = lax.broadcasted_iota(jnp.int32, x.shape, x.ndim - 1)
    return jnp.where((lane % 32) < 16, pltpu.roll(x, w - 16, x.ndim - 1), pltpu.roll(x, 16, x.ndim - 1))


def _rope_apply(x, cos, sin_signed, inverse):
    reps = x.shape[-1] // LANES
    cos = jnp.concatenate([cos] * reps, axis=-1) if reps > 1 else cos
    sin = jnp.concatenate([sin_signed] * reps, axis=-1) if reps > 1 else sin_signed
    if inverse:
        return x * cos + _rot_pairs(x * sin)
    return x * cos + _rot_pairs(x) * sin


def _rope_fwd(name, p, cos, sin):
    rows = p.shape[0]

    def fn(q, k, v, cos, sin):
        return (_rope_apply(q, cos, sin, False), _rope_apply(k, cos, sin, False), v), ()

    return _rowwise(name, fn, rows, [(p, QW, 0), (p, KVW, C_K // KVW), (p, KVW, C_V // KVW), (cos, LANES, 0),
                                     (sin, LANES, 0)], [], [(QW, BF16), (KVW, BF16), (KVW, BF16)], [])


def _proj_grad(name, dq_rot, dk_rot, dv, cos, sin, gla_f, gla_b, dgg, dz):
    rows = dq_rot.shape[0]

    def fn(dq, dk, dv, cos, sin, gqf, gkf, gvf, gqb, gkb, gvb, dgg, dz):
        parts = [_rope_apply(dq, cos, sin, True), gvf + gvb, dgg, _rope_apply(dk, cos, sin, True), dv, gqf + gqb,
                 gkf + gkb, dz]
        return (jnp.concatenate(parts, axis=1),), ()

    ins = [(dq_rot, QW), (dk_rot, KVW), (dv, KVW), (cos, LANES), (sin, LANES)]
    ins += [(t, t.shape[1]) for t in (*gla_f, *gla_b)] + [(dgg, GVW), (dz, LANES)]
    return _rowwise(name, fn, rows, [(t, w, 0) for t, w in ins], [], [(IN_PAD, BF16)], [])[0]


GROUP_ROWS = ATT_GROUP * BLOCK


ATT_SCALE = HEAD_DIM ** -0.5


def _attn_bias(n_tokens):
    nb = n_tokens // BLOCK
    i = (jnp.arange(GROUP_ROWS) % BLOCK)[:, None]
    j = jnp.arange(3 * BLOCK)[None, :]

    def one(n):
        kpos = (n - 1) * BLOCK + j
        return jnp.where((jnp.abs(j - BLOCK - i) <= WINDOW) & (kpos >= 0) & (kpos < n_tokens), 0.0, NEG_INF)

    return jnp.stack([one(0), one(1), one(nb - 1)]).astype(F32)


def _attn_bias_spec(n_tokens):
    nb = n_tokens // BLOCK
    return pl.BlockSpec((1, GROUP_ROWS, 3 * BLOCK), lambda n: (jnp.where(n == 0, 0, jnp.where(n == nb - 1, 2, 1)), 0, 0))


def _attn_setup(sink):
    row = lax.broadcasted_iota(jnp.int32, (GROUP_ROWS, 1), 0)
    group = sum((row >= g * BLOCK).astype(jnp.int32) for g in range(1, ATT_GROUP))
    head_id = lax.broadcasted_iota(jnp.int32, (1, ATT_HEADS), 1)
    sks = []
    for h in range(ATT_KV_HEADS):
        sk = jnp.zeros((GROUP_ROWS, 1), F32)
        for g in range(ATT_GROUP):
            one = jnp.sum(jnp.where(head_id == h * ATT_GROUP + g, sink, 0.0), axis=-1, keepdims=True)
            sk = jnp.where(group == g, one, sk)
        sks.append(sk)
    return group, sks


def _attn_weights(q, kw, kc, sk, bias):
    q = q * ATT_SCALE
    s_w = _raw_dot("nt", q, kw, False) + bias
    s_c = _raw_dot("nt", q, kc, False)
    m = jnp.maximum(jnp.maximum(jnp.max(s_w, axis=-1, keepdims=True), jnp.max(s_c, axis=-1, keepdims=True)), sk)
    pw, pc, ps = jnp.exp(s_w - m), jnp.exp(s_c - m), jnp.exp(sk - m)
    return q, pw, pc, ps, jnp.sum(pw, axis=-1, keepdims=True) + jnp.sum(pc, axis=-1, keepdims=True) + ps


def _f_attn(qs, kws, vws, kcs, vcs, sink, bias):
    _, sks = _attn_setup(sink)
    outs = []
    for h in range(ATT_KV_HEADS):
        _, pw, pc, _, den = _attn_weights(qs[h], kws[h], kcs[h], sks[h], bias)
        outs.append((_raw_dot("nn", pw, vws[h], False) + _raw_dot("nn", pc, vcs[h], False)) / den)
    return tuple(outs)


def _f_attn_bwd(qs, kws, vws, kcs, vcs, sink, bias, outs, douts):
    group, sks = _attn_setup(sink)
    head_id = lax.broadcasted_iota(jnp.int32, (1, ATT_HEADS), 1)
    dot = lambda mode, a, b: _raw_dot(mode, a, b, False)
    dqs, dkws, dvws, dkcs, dvcs, dsink = [], [], [], [], [], jnp.zeros((1, ATT_HEADS), F32)
    for h in range(ATT_KV_HEADS):
        q, pw, pc, ps, den = _attn_weights(qs[h], kws[h], kcs[h], sks[h], bias)
        inv = 1.0 / den
        pw, pc = pw * inv, pc * inv
        dd = jnp.sum(douts[h] * outs[h], axis=-1, keepdims=True)
        dsw = pw * (dot("nt", douts[h], vws[h]) - dd)
        dsc = pc * (dot("nt", douts[h], vcs[h]) - dd)
        dqs.append((dot("nn", dsw, kws[h]) + dot("nn", dsc, kcs[h])) * ATT_SCALE)
        dkws.append(dot("tn", dsw, q))
        dkcs.append(dot("tn", dsc, q))
        dvws.append(dot("tn", pw, douts[h]))
        dvcs.append(dot("tn", pc, douts[h]))
        dsk = -(ps * inv) * dd
        for g in range(ATT_GROUP):
            one = jnp.sum(jnp.where(group == g, dsk, 0.0), axis=0, keepdims=True)
            dsink = dsink + jnp.where(head_id == h * ATT_GROUP + g, one, 0.0)
    return dqs, dkws, dvws, dkcs, dvcs, dsink


def _group_rows(ref, h):
    hs = lambda hq: slice(hq * HEAD_DIM, (hq + 1) * HEAD_DIM)
    return jnp.concatenate([ref[:, hs(h * ATT_GROUP + g)].astype(F32) for g in range(ATT_GROUP)], axis=0)


def _ungroup_rows(ref, h, val):
    for g in range(ATT_GROUP):
        hq = h * ATT_GROUP + g
        ref[:, hq * HEAD_DIM:(hq + 1) * HEAD_DIM] = val[g * BLOCK:(g + 1) * BLOCK].astype(ref.dtype)


def _attn_loads(n, q_ref, kp_ref, vp_ref, kc_ref, vc_ref):
    r0 = pl.multiple_of(n * BLOCK, BLOCK)
    hs = lambda h: slice(h * HEAD_DIM, (h + 1) * HEAD_DIM)
    qs = [_group_rows(q_ref, h) for h in range(ATT_KV_HEADS)]
    kws = [kp_ref[pl.ds(r0, 3 * BLOCK), hs(h)].astype(F32) for h in range(ATT_KV_HEADS)]
    vws = [vp_ref[pl.ds(r0, 3 * BLOCK), hs(h)].astype(F32) for h in range(ATT_KV_HEADS)]
    kcs = [kc_ref[:, hs(h)].astype(F32) for h in range(ATT_KV_HEADS)]
    vcs = [vc_ref[:, hs(h)].astype(F32) for h in range(ATT_KV_HEADS)]
    return r0, hs, qs, kws, vws, kcs, vcs


def _attn_specs(s, c):
    full = lambda shape: pl.BlockSpec(shape, lambda n: (0, 0))
    return [pl.BlockSpec((BLOCK, QW), lambda n: (n, 0)), full((s + 2 * BLOCK, KVW)), full((s + 2 * BLOCK, KVW)),
            full((c, KVW)), full((c, KVW)), full((1, ATT_HEADS)), _attn_bias_spec(s)]


def _attn_fwd(q, kp, vp, kc, vc, sink):
    s, c = q.shape[0], kc.shape[0]

    def body(q_ref, kp_ref, vp_ref, kc_ref, vc_ref, sink_ref, bias_ref, o_ref):
        n = pl.program_id(0)
        _, hs, qs, kws, vws, kcs, vcs = _attn_loads(n, q_ref, kp_ref, vp_ref, kc_ref, vc_ref)
        outs = _f_attn(qs, kws, vws, kcs, vcs, sink_ref[...], bias_ref[0])
        for h in range(ATT_KV_HEADS):
            _ungroup_rows(o_ref, h, outs[h])

    return pl.pallas_call(
        body, name="attn_fwd", grid=(s // BLOCK,), in_specs=_attn_specs(s, c),
        out_specs=pl.BlockSpec((BLOCK, QW), lambda n: (n, 0)), out_shape=jax.ShapeDtypeStruct((s, QW), BF16),
        compiler_params=_cp("parallel"),
    )(q, kp, vp, kc, vc, sink, _attn_bias(s))


def _attn_bwd(do, o, q, kp, vp, kc, vc, sink):
    s, c = q.shape[0], kc.shape[0]

    def body(do_ref, o_ref, q_ref, kp_ref, vp_ref, kc_ref, vc_ref, sink_ref, bias_ref, dq_ref, dkp_ref, dvp_ref,
             dkc_ref, dvc_ref, dsink_ref):
        n = pl.program_id(0)

        @pl.when(n == 0)
        def _():
            for r in (dkp_ref, dvp_ref, dkc_ref, dvc_ref, dsink_ref):
                r[...] = jnp.zeros_like(r)

        r0, hs, qs, kws, vws, kcs, vcs = _attn_loads(n, q_ref, kp_ref, vp_ref, kc_ref, vc_ref)
        heads = range(ATT_KV_HEADS)
        dqs, dkws, dvws, dkcs, dvcs, dsink = _f_attn_bwd(
            qs, kws, vws, kcs, vcs, sink_ref[...], bias_ref[0], [_group_rows(o_ref, h) for h in heads],
            [_group_rows(do_ref, h) for h in heads])
        for h in heads:
            _ungroup_rows(dq_ref, h, dqs[h])
            dkp_ref[pl.ds(r0, 3 * BLOCK), hs(h)] += dkws[h]
            dvp_ref[pl.ds(r0, 3 * BLOCK), hs(h)] += dvws[h]
            dkc_ref[:, hs(h)] += dkcs[h]
            dvc_ref[:, hs(h)] += dvcs[h]
        dsink_ref[...] += dsink

    full = lambda shape: pl.BlockSpec(shape, lambda n: (0, 0))
    return pl.pallas_call(
        body, name="attn_bwd", grid=(s // BLOCK,),
        in_specs=[pl.BlockSpec((BLOCK, QW), lambda n: (n, 0))] * 2 + _attn_specs(s, c),
        out_specs=[pl.BlockSpec((BLOCK, QW), lambda n: (n, 0)), full((s + 2 * BLOCK, KVW)), full((s + 2 * BLOCK, KVW)),
                   full((c, KVW)), full((c, KVW)), full((1, ATT_HEADS))],
        out_shape=[jax.ShapeDtypeStruct((s, QW), BF16), jax.ShapeDtypeStruct((s + 2 * BLOCK, KVW), F32),
                   jax.ShapeDtypeStruct((s + 2 * BLOCK, KVW), F32), jax.ShapeDtypeStruct((c, KVW), F32),
                   jax.ShapeDtypeStruct((c, KVW), F32), jax.ShapeDtypeStruct((1, ATT_HEADS), F32)],
        compiler_params=_cp("arbitrary"),
    )(do, o, q, kp, vp, kc, vc, sink, _attn_bias(s))


GLA_GROUPS = 1
GLA_GROUP_HEADS = GLA_HEADS // GLA_GROUPS
GKG, GVG = GKW // GLA_GROUPS, GVW // GLA_GROUPS


def _gla_masks(heads=GLA_HEADS):
    hk = np.arange(heads * GLA_DK) // GLA_DK
    hv = np.arange(heads * GLA_DV) // GLA_DV
    head_k = (np.arange(heads)[:, None] == hk[None, :]).astype(np.float32)
    head_v = (np.arange(heads)[:, None] == hv[None, :]).astype(np.float32)
    bd_t = (hv[:, None] == hk[None, :]).astype(np.float32)
    return jnp.asarray(head_k), jnp.asarray(head_v), jnp.asarray(bd_t)


def _group_states(st):
    return jnp.stack([st[g * GVG:(g + 1) * GVG, g * GKG:(g + 1) * GKG] for g in range(GLA_GROUPS)])


def _ungroup_states(st):
    out = jnp.zeros((GVW, GKW), st.dtype)
    for g in range(GLA_GROUPS):
        out = out.at[g * GVG:(g + 1) * GVG, g * GKG:(g + 1) * GKG].set(st[g])
    return out


def _tri(n, rev, strict=False):
    i = lax.broadcasted_iota(jnp.int32, (n, n), 0)
    j = lax.broadcasted_iota(jnp.int32, (n, n), 1)
    if strict:
        keep = (j > i) if rev else (j < i)
    else:
        keep = (j >= i) if rev else (j <= i)
    return keep


def _f_gla_chunk(q, k, v, la, st, head_k, head_v, bd_t, rev):
    return _f_gla_carry(*_f_gla_intra(q, k, v, la, head_k, head_v, rev), v, st, bd_t)


def _f_gla_intra(q, k, v, la, head_k, head_v, rev):
    heads, kw, vw = head_k.shape[0], q.shape[1], v.shape[1]
    keep = _tri(GLA_CHUNK, rev)
    b = _nn_mask(keep.astype(F32), la)
    bl = jnp.sum(la, axis=0, keepdims=True)
    qd = q * (GLA_DK ** -0.5) * jnp.exp(b)
    ki = k * jnp.exp(-b)
    kd = k * jnp.exp(bl - b)
    q_heads = (qd[None, :, :] * head_k[:, None, :]).reshape(heads * GLA_CHUNK, kw)
    a_all = _nt(q_heads, ki).reshape(heads, GLA_CHUNK, GLA_CHUNK)
    a_all = jnp.where(keep[None, :, :], a_all, 0.0).reshape(heads * GLA_CHUNK, GLA_CHUNK)
    o_all = _nn(a_all, v).reshape(heads, GLA_CHUNK, vw)
    return jnp.sum(o_all * head_v[:, None, :], axis=0), qd, kd, bl


def _f_gla_carry(intra, qd, kd, bl, v, st, bd_t):
    return intra + _nt(qd, st), st * jnp.exp(bl) + bd_t * _tn(v, kd)


def _gla_specs(s, tb, order):
    return [pl.BlockSpec((tb, GKW), lambda i: (order(i), C_GQ // GKW)),
            pl.BlockSpec((tb, GKW), lambda i: (order(i), C_GK // GKW)),
            pl.BlockSpec((tb, GVW), lambda i: (order(i), C_GV // GVW)),
            pl.BlockSpec((tb, GKW), lambda i: (order(i), 0))]


GLA_BLOCK_CHUNKS = 4


def _gla_fwd(p, la_f, la_b, st_f0, st_b0):
    s = p.shape[0]
    tb = GLA_BLOCK_CHUNKS * GLA_CHUNK
    nblk = s // tb
    up, down = (lambda i: i), (lambda i: nblk - 1 - i)
    masks = _gla_masks(GLA_GROUP_HEADS)

    def scan(rev, q_ref, k_ref, v_ref, la_ref, o_ref, sts_ref, st_ref, consts):
        for g in range(GLA_GROUPS):
            gk, gv = slice(g * GKG, (g + 1) * GKG), slice(g * GVG, (g + 1) * GVG)
            st = st_ref[g]
            sts_ref[0, g] = st
            chunks = range(GLA_BLOCK_CHUNKS)
            for ci in (reversed(chunks) if rev else chunks):
                rows = slice(ci * GLA_CHUNK, (ci + 1) * GLA_CHUNK)
                o, st = _f_gla_chunk(q_ref[rows, gk], k_ref[rows, gk], v_ref[rows, gv], la_ref[rows, gk], st, *consts,
                                     rev)
                o_ref[rows, gv] = o
            st_ref[g] = st

    def body(qf, kf, vf, laf, qb, kb, vb, lab, stf0, stb0, hk_ref, hv_ref, bd_ref, of_ref, stsf_ref, ob_ref, stsb_ref,
             stf_ref, stb_ref):
        @pl.when(pl.program_id(0) == 0)
        def _():
            stf_ref[...] = stf0[...]
            stb_ref[...] = stb0[...]

        consts = (hk_ref[...], hv_ref[...], bd_ref[...])
        scan(False, qf, kf, vf, laf, of_ref, stsf_ref, stf_ref, consts)
        scan(True, qb, kb, vb, lab, ob_ref, stsb_ref, stb_ref, consts)

    full = lambda a: pl.BlockSpec(a.shape, lambda i: (0,) * a.ndim)
    outs = lambda order: [pl.BlockSpec((tb, GVW), lambda i: (order(i), 0)),
                          pl.BlockSpec((1, GLA_GROUPS, GVG, GKG), lambda i: (order(i), 0, 0, 0))]
    return pl.pallas_call(
        body, name="gla_fwd", grid=(nblk,),
        in_specs=_gla_specs(s, tb, up) + _gla_specs(s, tb, down) + [full(st_f0), full(st_b0)]
        + [full(m) for m in masks],
        out_specs=outs(up) + outs(down),
        out_shape=[jax.ShapeDtypeStruct((s, GVW), F32), jax.ShapeDtypeStruct((nblk, GLA_GROUPS, GVG, GKG), F32)] * 2,
        scratch_shapes=[pltpu.VMEM((GLA_GROUPS, GVG, GKG), F32)] * 2,
        compiler_params=_cp("arbitrary"),
    )(p, p, p, la_f, p, p, p, la_b, st_f0, st_b0, *masks)


def _gla_bwd(p, la_f, la_b, sts_f, sts_b, do, after=None):
    s = p.shape[0]
    tb = GLA_BLOCK_CHUNKS * GLA_CHUNK
    nblk = s // tb
    up, down = (lambda i: i), (lambda i: nblk - 1 - i)
    masks = _gla_masks(GLA_GROUP_HEADS)
    follow = () if after is None else (after,)

    def back(rev, q_ref, k_ref, v_ref, la_ref, sts_ref, do_ref, dq_ref, dk_ref, dv_ref, dla_ref, dst0_ref, dst_ref,
             consts):
        def block(q, k, v, la, st):
            outs = [None] * GLA_BLOCK_CHUNKS
            chunks = range(GLA_BLOCK_CHUNKS)
            for ci in (reversed(chunks) if rev else chunks):
                outs[ci], st = _f_gla_chunk(q[ci], k[ci], v[ci], la[ci], st, *consts, rev)
            return tuple(outs), st

        for g in range(GLA_GROUPS):
            gk, gv = slice(g * GKG, (g + 1) * GKG), slice(g * GVG, (g + 1) * GVG)
            split = lambda r, cols: tuple(r[ci * GLA_CHUNK:(ci + 1) * GLA_CHUNK, cols].astype(F32)
                                          for ci in range(GLA_BLOCK_CHUNKS))
            _, vjp = jax.vjp(block, split(q_ref, gk), split(k_ref, gk), split(v_ref, gv), split(la_ref, gk),
                             sts_ref[0, g])
            dq, dk, dv, dla, dst = vjp((split(do_ref, gv), dst_ref[g]))
            for ci in range(GLA_BLOCK_CHUNKS):
                rows = slice(ci * GLA_CHUNK, (ci + 1) * GLA_CHUNK)
                dq_ref[rows, gk], dk_ref[rows, gk] = dq[ci].astype(BF16), dk[ci].astype(BF16)
                dv_ref[rows, gv], dla_ref[rows, gk] = dv[ci].astype(BF16), dla[ci]
            dst_ref[g] = dst
            dst0_ref[g] = dst

    def body(*refs):
        ins, (hk_ref, hv_ref, bd_ref) = refs[:12], refs[12:15]
        outs = refs[15 + len(follow):]

        @pl.when(pl.program_id(0) == 0)
        def _():
            outs[10][...] = jnp.zeros_like(outs[10])
            outs[11][...] = jnp.zeros_like(outs[11])

        consts = (hk_ref[...], hv_ref[...], bd_ref[...])
        back(False, *ins[:6], *outs[:5], outs[10], consts)
        back(True, *ins[6:], *outs[5:10], outs[11], consts)

    full = lambda a: pl.BlockSpec(a.shape, lambda i: (0,) * a.ndim)

    def ins(order):
        return _gla_specs(s, tb, order) + [pl.BlockSpec((1, GLA_GROUPS, GVG, GKG), lambda i: (order(i), 0, 0, 0)),
                                           pl.BlockSpec((tb, GVW), lambda i: (order(i), 0))]

    def outs(order):
        blk = lambda w: pl.BlockSpec((tb, w), lambda i: (order(i), 0))
        return [blk(GKW), blk(GKW), blk(GVW), blk(GKW), pl.BlockSpec((GLA_GROUPS, GVG, GKG), lambda i: (0, 0, 0))]

    shapes = [jax.ShapeDtypeStruct((s, GKW), BF16), jax.ShapeDtypeStruct((s, GKW), BF16),
              jax.ShapeDtypeStruct((s, GVW), BF16), jax.ShapeDtypeStruct((s, GKW), F32),
              jax.ShapeDtypeStruct((GLA_GROUPS, GVG, GKG), F32)]
    both = pl.pallas_call(
        body, name="gla_bwd", grid=(nblk,),
        in_specs=ins(down) + ins(up) + [full(m) for m in masks] + [pl.BlockSpec(memory_space=pl.ANY)] * len(follow),
        out_specs=outs(down) + outs(up), out_shape=shapes * 2,
        scratch_shapes=[pltpu.VMEM((GLA_GROUPS, GVG, GKG), F32)] * 2,
        compiler_params=_cp("arbitrary"),
    )(p, p, p, la_f, sts_f, do, p, p, p, la_b, sts_b, do, *masks, *follow)
    return both[:5], both[5:]


def _f_ctx_state(k, v, la_f, la_b, bd_t):
    c = k.shape[0]
    after = _nn_mask(_tri(c, True, strict=True).astype(F32), la_f)
    before = _nn_mask(_tri(c, False, strict=True).astype(F32), la_b)
    return bd_t * _tn(v, k * jnp.exp(after)), bd_t * _tn(v, k * jnp.exp(before))


def _ctx_state(pc, la_f, la_b):
    c = pc.shape[0]
    bd_t = _gla_masks()[2]

    def body(k_ref, v_ref, lf_ref, lb_ref, bd_ref, sf_ref, sb_ref):
        sf_ref[...], sb_ref[...] = _f_ctx_state(k_ref[...], v_ref[...], lf_ref[...], lb_ref[...], bd_ref[...])

    full = lambda a: pl.BlockSpec(a.shape, lambda i: (0, 0))
    return pl.pallas_call(
        body, name="ctx_state_fwd", grid=(1,),
        in_specs=[pl.BlockSpec((c, GKW), lambda i: (0, C_GK // GKW)), pl.BlockSpec((c, GVW), lambda i: (0, C_GV // GVW)),
                  full(la_f), full(la_b), full(bd_t)],
        out_specs=[pl.BlockSpec((GVW, GKW), lambda i: (0, 0))] * 2,
        out_shape=[jax.ShapeDtypeStruct((GVW, GKW), F32)] * 2,
        compiler_params=_cp("arbitrary"),
    )(pc, pc, la_f, la_b, bd_t)


def _ctx_state_bwd(pc, la_f, la_b, dsf, dsb):
    c = pc.shape[0]
    bd_t = _gla_masks()[2]

    def body(k_ref, v_ref, lf_ref, lb_ref, bd_ref, dsf_ref, dsb_ref, dk_ref, dv_ref, dlf_ref, dlb_ref):
        _, vjp = jax.vjp(lambda k, v, lf, lb: _f_ctx_state(k, v, lf, lb, bd_ref[...]),
                         k_ref[...], v_ref[...], lf_ref[...], lb_ref[...])
        dk, dv, dlf, dlb = vjp((dsf_ref[...], dsb_ref[...]))
        dk_ref[...], dv_ref[...] = dk.astype(BF16), dv.astype(BF16)
        dlf_ref[...], dlb_ref[...] = dlf, dlb

    full = lambda a: pl.BlockSpec(a.shape, lambda i: (0, 0))
    return pl.pallas_call(
        body, name="ctx_state_bwd", grid=(1,),
        in_specs=[pl.BlockSpec((c, GKW), lambda i: (0, C_GK // GKW)), pl.BlockSpec((c, GVW), lambda i: (0, C_GV // GVW)),
                  full(la_f), full(la_b), full(bd_t), full(dsf), full(dsb)],
        out_specs=[pl.BlockSpec((c, GKW), lambda i: (0, 0)), pl.BlockSpec((c, GVW), lambda i: (0, 0)),
                   pl.BlockSpec((c, GKW), lambda i: (0, 0)), pl.BlockSpec((c, GKW), lambda i: (0, 0))],
        out_shape=[jax.ShapeDtypeStruct((c, GKW), BF16), jax.ShapeDtypeStruct((c, GVW), BF16),
                   jax.ShapeDtypeStruct((c, GKW), F32), jax.ShapeDtypeStruct((c, GKW), F32)],
        compiler_params=_cp("arbitrary"),
    )(pc, pc, la_f, la_b, bd_t, dsf, dsb)


_SRC_COLS = ((0, QW), (QW + 2 * KVW + 2 * GKW, GVW), (QW + 2 * KVW + 2 * GKW + GVW, GVW), (QW, KVW), (QW + KVW, KVW),
             (QW + 2 * KVW, GKW), (QW + 2 * KVW + GKW, GKW), (IN_COLS - 2 * GATE_RANK, 2 * GATE_RANK))
_DST_COLS = (C_Q, C_GV, C_GG, C_K, C_V, C_GQ, C_GK, C_Z)


def _pack_w_in(w_in):
    parts = [w_in[:, s:s + n] for s, n in _SRC_COLS]
    parts.append(jnp.zeros((w_in.shape[0], IN_PAD - C_Z - 2 * GATE_RANK), w_in.dtype))
    return jnp.concatenate(parts, axis=1)


def _unpack_w_in_grad(g):
    by_src = sorted(zip(_SRC_COLS, _DST_COLS))
    return jnp.concatenate([g[:, d:d + n] for (_, n), d in by_src], axis=1)


def _prep_gate_weights(w_gate_fwd, w_gate_bwd):
    pad_rows = lambda w, at: jnp.zeros((LANES, GKW), F32).at[at:at + GATE_RANK].set(w)
    return {"wg_f": pad_rows(w_gate_fwd, 0), "wg_b": pad_rows(w_gate_bwd, GATE_RANK)}


def _local_step(x, ctx, target, ada, ada_c, w, late_weights, reduce_behind=None, reduce_w_in=None):
    s, d = x.shape
    sh1, sc1, gt1, sh2, sc2, gt2 = [ada[:, i * d:(i + 1) * d] for i in range(6)]
    sh1c, sc1c = ada_c[:, :d], ada_c[:, d:2 * d]
    cos, sin = _rope_tables(s)
    gt = jnp.tile(w["g_gla_norm"], (1, GLA_HEADS))

    h = _norm_mod("pre_mix", x, w["g_pre_mix"], sh1, sc1)
    hc = _norm_mod("pre_mix_ctx", ctx, w["g_pre_mix"], sh1c, sc1c)
    w_in, token = w["w_in"](h, cos, sin)
    p = _mm("proj_in", h, w_in, "nn", after=token)
    pc = _mm("proj_in_ctx", hc, w_in, "nn")
    q_rot, k_rot, v_b = _rope_fwd("rope", p, cos, sin)
    pad = ((BLOCK, BLOCK), (0, 0))
    kp, vp = jnp.pad(k_rot, pad), jnp.pad(v_b, pad)
    kc, vc = pc[:, C_K:C_K + KVW].astype(BF16), pc[:, C_V:C_V + KVW].astype(BF16)
    attn = _attn_fwd(q_rot, kp, vp, kc, vc, w["attn_sink"])
    gate_w = (w["wg_f"], w["wg_b"], w["b_gate_fwd"], w["b_gate_bwd"])
    la_f, la_b = _gate_fwd("gate", p, *gate_w)
    la_fc, la_bc = _gate_fwd("gate_ctx", pc, *gate_w)
    st_f0, st_b0 = _ctx_state(pc, la_fc, la_bc)
    o_f, sts_f, o_b, sts_b = _gla_fwd(p, la_f, la_b, _group_states(st_f0), _group_states(st_b0))
    mix = _gla_out("gla_out", attn, o_f, o_b, p, gt)
    w_out, w_ffn_in_t, w_ffn_out = late_weights(mix)
    y = _mm("proj_out", mix, w_out, "nn", BF16)
    x1, h2 = _post_res_norm_mod("post_mix_pre_ffn", x, y, w["g_post_mix"], gt1, w["g_pre_ffn"], sh2, sc2)
    u, a = _ffn_in_swiglu("ffn_in", h2, w_ffn_in_t)
    f = _mm("ffn_out", a, w_ffn_out, "nn", BF16)
    g = {}
    dx2, df, loss, g["g_post_ffn"], dgt2 = _post_res_loss("post_ffn_loss", x1, f, w["g_post_ffn"], gt2, target)

    late_rows = {"w_ffn_in_t": w_ffn_in_t.shape[0] // N_CHIP, "w_ffn_out": w_ffn_out.shape[0] // N_CHIP,
                 "w_out": w_out.shape[0] // N_CHIP}
    order = sorted(late_rows, key=lambda n: -late_rows[n])
    offsets, slab_rows = _slab_layout([late_rows[n] for n in order])
    late_at, slab_shape = dict(zip(order, offsets)), (N_CHIP, slab_rows, d)
    slab = _slab_zero_gaps("late_grads_gaps", slab_shape, [late_rows[n] for n in order], offsets)
    slab = _dw_into_slab("ffn_out_dw", a, df, slab, slab_shape, late_at["w_ffn_out"])
    du = _ffn_out_dx_swiglu_bwd("ffn_out_dx", df, w_ffn_out, u)
    dh2 = _mm("ffn_in_dx", du, w_ffn_in_t, "nn", BF16)
    slab = _dw_into_slab("ffn_in_dw", du, h2, slab, slab_shape, late_at["w_ffn_in_t"])
    dx1, dy, g["g_pre_ffn"], dsh2, dsc2, g["g_post_mix"], dgt1 = _norm_mod_post_res_bwd(
        "pre_ffn_post_mix_bwd", dh2, dx2, x1, y, w["g_pre_ffn"], sh2, sc2, w["g_post_mix"], gt1)
    dmix = _mm("proj_out_dx", dy, w_out, "nt", BF16)
    slab = _dw_into_slab("proj_out_dw", mix, dy, slab, slab_shape, late_at["w_out"])
    g["late"], g["late_at"], g["late_rows"] = slab, late_at, late_rows
    rb, sink, token = reduce_behind, w["attn_sink"], None
    if rb is not None:
        gt = _behind(gt, rb.start_slab(slab))
    d_o, dgg, dgt = _gla_out_bwd("gla_out_bwd", dmix, o_f, o_b, p, gt)
    g["g_gla_norm"] = jnp.sum(dgt.reshape(GLA_HEADS, GLA_DV), axis=0, keepdims=True)
    if rb is not None:
        token = rb.pair(dgg)
    gla_f, gla_b = _gla_bwd(p, la_f, la_b, sts_f, sts_b, d_o, token)
    (dla_f, dst_f0), (dla_b, dst_b0) = gla_f[3:], gla_b[3:]
    dst_f0, dst_b0 = _ungroup_states(dst_f0), _ungroup_states(dst_b0)
    if rb is not None:
        sink = _behind(sink, rb.total(dla_b))
    dgkc, dgvc, dla_fc, dla_bc = _ctx_state_bwd(pc, la_fc, la_bc, dst_f0, dst_b0)
    dz, dwf, dwb, dbf, dbb = _gate_bwd("gate_bwd", p, dla_f, dla_b, *gate_w)
    dzc, dwfc, dwbc, dbfc, dbbc = _gate_bwd("gate_ctx_bwd", pc, dla_fc, dla_bc, *gate_w)
    g["w_gate_fwd"] = (dwf + dwfc)[:GATE_RANK]
    g["w_gate_bwd"] = (dwb + dwbc)[GATE_RANK:2 * GATE_RANK]
    g["b_gate_fwd"], g["b_gate_bwd"] = dbf + dbfc, dbb + dbbc
    dq_rot, dkp, dvp, dkc, dvc, g["attn_sink"] = _attn_bwd(dmix, attn, q_rot, kp, vp, kc, vc, sink)
    if rb is not None:
        g["late"] = rb.result(dq_rot)
    dp = _proj_grad("proj_grad", dq_rot, dkp[BLOCK:BLOCK + s], dvp[BLOCK:BLOCK + s], cos, sin, gla_f[:3], gla_b[:3],
                    dgg, dz)
    c_rows = ctx.shape[0]
    zeros = lambda n: jnp.zeros((c_rows, n), BF16)
    dpc = jnp.concatenate([zeros(QW), dgvc, zeros(GVW), dkc.astype(BF16), dvc.astype(BF16), zeros(GKW), dgkc, dzc],
                          axis=1)
    g["w_in"] = _mm("proj_in_dw", h, dp, "tn", init=_mm("proj_in_ctx_dw", hc, dpc, "tn"))
    token = None if reduce_w_in is None else reduce_w_in.start(g["w_in"])
    dh = _mm("proj_in_dx", dp, w_in, "nt", BF16, after=token)
    dhc = _mm("proj_in_ctx_dx", dpc, w_in, "nt")
    if reduce_w_in is not None:
        sh1 = _behind(sh1, reduce_w_in.pair(dh))
    dx, dg_a, dsh1, dsc1 = _norm_mod_bwd("pre_mix_bwd", dh, dx1, x, w["g_pre_mix"], sh1, sc1)
    if reduce_w_in is not None:
        dsh1 = _behind(dsh1, reduce_w_in.total(dx))
    _, dg_b, dsh1c, dsc1c = _norm_mod_bwd("pre_mix_ctx_bwd", dhc, jnp.zeros_like(dhc), ctx, w["g_pre_mix"], sh1c,
                                          sc1c)
    g["g_pre_mix"] = dg_a + dg_b
    d_ada = jnp.concatenate([dsh1, dsc1, dgt1, dsh2, dsc2, dgt2], axis=1)
    d_ada_c = jnp.concatenate([dsh1c, dsc1c, jnp.zeros((1, 4 * d), F32)], axis=1)
    return loss, dx, g, d_ada, d_ada_c


HBM = pl.BlockSpec(memory_space=pltpu.HBM)
N_DEV, N_CHIP = 8, 4


def _place():
    x, y, c = lax.axis_index("x"), lax.axis_index("y"), lax.axis_index("c")
    return x, y, c, [(1 - x, y), (x, 1 - y), (1 - x, 1 - y)]


def _row_tile(n, mult, cap):
    return max(t for t in range(mult, min(n, cap) + 1, mult) if n % t == 0)


def _ag_small(name, v, after=None):
    follow = () if after is None else (after,)

    def body(v_ref, *rest):
        out_ref, send_sems, recv_sems = rest[len(follow):]
        x, y, c, _ = _place()
        out_ref[4 * x + 2 * y + c] = v_ref[...]

        def peer(r):
            return ((1 - x) if r & 4 else x, (1 - y) if r & 2 else y, (1 - c) if r & 1 else c)

        def copy(r, block):
            px, py, pc = block
            return pltpu.make_async_remote_copy(
                src_ref=v_ref, dst_ref=out_ref.at[4 * px + 2 * py + pc], send_sem=send_sems.at[r - 1],
                recv_sem=recv_sems.at[r - 1], device_id=peer(r), device_id_type=MESH)

        sends = [copy(r, (x, y, c)) for r in range(1, N_DEV)]
        for cp in sends:
            cp.start()
        for r in range(1, N_DEV):
            copy(r, peer(r)).wait_recv()
        for cp in sends:
            cp.wait_send()

    return pl.pallas_call(
        body, name=name, out_shape=jax.ShapeDtypeStruct((N_DEV,) + v.shape, v.dtype),
        in_specs=[pl.BlockSpec(memory_space=pltpu.VMEM)] + [pl.BlockSpec(memory_space=pl.ANY)] * len(follow),
        out_specs=pl.BlockSpec(memory_space=pltpu.VMEM),
        scratch_shapes=[pltpu.SemaphoreType.DMA((N_DEV - 1,)), pltpu.SemaphoreType.DMA((N_DEV - 1,))],
    )(v, *follow)


def _halves(c, rows, mult):
    hr = rows // 2
    return pl.ds(pl.multiple_of(c * hr, mult), hr), pl.ds(pl.multiple_of((1 - c) * hr, mult), hr)


def _add_half(name, g, a, c_idx):
    n_sh, hr, n = a.shape
    tr = _row_tile(hr, 16, 1024)
    nb = hr // tr

    def body(c_ref, g_ref, a_ref, o_ref):
        o_ref[...] = (g_ref[...] + a_ref[...]).astype(o_ref.dtype)

    return pl.pallas_call(
        body, name=name, out_shape=jax.ShapeDtypeStruct(a.shape, BF16),
        grid_spec=pltpu.PrefetchScalarGridSpec(
            num_scalar_prefetch=1, grid=(n_sh, nb),
            in_specs=[pl.BlockSpec((1, tr, n), lambda s, i, c_ref: (s, c_ref[0] * nb + i, 0)),
                      pl.BlockSpec((1, tr, n), lambda s, i, c_ref: (s, i, 0))],
            out_specs=pl.BlockSpec((1, tr, n), lambda s, i, c_ref: (s, i, 0))),
        compiler_params=_cp("parallel", "parallel"),
    )(c_idx, g, a)


def _sum_chips(name, b, c_idx):
    n_sh, hr, n = b.shape
    tr = _row_tile(hr, 16, 1024)
    nb = hr // tr

    def body(c_ref, b0, b1, b2, b3, o_ref):
        o_ref[...] = ((b0[0].astype(F32) + b1[0].astype(F32)) + b2[0].astype(F32)) + b3[0].astype(F32)

    return pl.pallas_call(
        body, name=name, out_shape=jax.ShapeDtypeStruct((2 * hr, n), F32),
        grid_spec=pltpu.PrefetchScalarGridSpec(
            num_scalar_prefetch=1, grid=(nb,),
            in_specs=[pl.BlockSpec((1, tr, n), functools.partial(lambda i, c_ref, k: (k, i, 0), k=k))
                      for k in range(n_sh)],
            out_specs=pl.BlockSpec((tr, n), lambda i, c_ref: (c_ref[0] * nb + i, 0))),
        compiler_params=_cp("parallel"),
    )(c_idx, b, b, b, b)


SEM = pl.BlockSpec(memory_space=pltpu.SEMAPHORE)
ANY = pl.BlockSpec(memory_space=pl.ANY)
DATAFLOW = pltpu.SideEffectType.DATAFLOW_SIDE_EFFECTING


def _remote(src, dst, send_sems, recv_sems, k, to):
    return pltpu.make_async_remote_copy(src_ref=src, dst_ref=dst, send_sem=send_sems.at[k], recv_sem=recv_sems.at[k],
                                        device_id=to, device_id_type=MESH)


def _split_copy(name, src, land_shape, land_dtype, n, plan, after=None):
    after = jnp.zeros((8, LANES), F32) if after is None else after

    def start_body(src_ref, land_ref, after_ref, send_sems, recv_sems, src_thru, land_thru, token):
        for cp in plan(src_ref, land_ref, send_sems, recv_sems)[0]:
            cp.start()
        token[...] = jnp.zeros_like(token)

    sems = pltpu.SemaphoreType.DMA((n,))
    send_sems, recv_sems, src_thru, land_thru, token = pl.pallas_call(
        start_body, name=name + "_start",
        out_shape=(sems, sems, pltpu.HBM(src.shape, src.dtype), pltpu.HBM(land_shape, land_dtype),
                   jax.ShapeDtypeStruct((8, LANES), F32)),
        in_specs=(HBM, HBM, ANY), out_specs=(SEM, SEM, HBM, HBM, pl.BlockSpec(memory_space=pltpu.VMEM)),
        input_output_aliases={0: 2, 1: 3}, compiler_params=pltpu.CompilerParams(has_side_effects=DATAFLOW),
    )(pltpu.with_memory_space_constraint(src, pltpu.HBM),
      pltpu.with_memory_space_constraint(lax.empty(land_shape, land_dtype), pltpu.HBM), after)

    def wait(*after):
        def wait_body(src_ref, land_ref, send_sems, recv_sems, *rest):
            sent, received = plan(src_ref, land_ref, send_sems, recv_sems)
            for cp in sent:
                cp.wait_send()
            for cp in received:
                cp.wait_recv()

        return pl.pallas_call(
            wait_body, name=name + "_wait",
            out_shape=(pltpu.HBM(src.shape, src.dtype), pltpu.HBM(land_shape, land_dtype)),
            in_specs=(HBM, HBM, SEM, SEM) + (ANY,) * len(after), out_specs=(HBM, HBM),
            input_output_aliases={0: 0, 1: 1}, compiler_params=pltpu.CompilerParams(has_side_effects=DATAFLOW),
        )(src_thru, land_thru, send_sems, recv_sems, *after)

    return token, wait


def _split_gather(name, shards, after):
    k, n, plan = len(shards), 3 * len(shards), _plan_gather

    def start_body(*refs):
        for cp in plan(refs[:k], refs[k:2 * k], refs[2 * k + 1], refs[2 * k + 2])[0]:
            cp.start()
        refs[-1][...] = jnp.zeros_like(refs[-1])

    sems = pltpu.SemaphoreType.DMA((n,))
    bufs = [pltpu.HBM(s.shape, s.dtype) for s in shards] + [pltpu.HBM((N_CHIP,) + s.shape, s.dtype) for s in shards]
    hbm = lambda t: pltpu.with_memory_space_constraint(t, pltpu.HBM)
    outs = pl.pallas_call(
        start_body, name=name + "_start", out_shape=(sems, sems, *bufs, jax.ShapeDtypeStruct((8, LANES), F32)),
        in_specs=(HBM,) * (2 * k) + (ANY,),
        out_specs=(SEM, SEM) + (HBM,) * (2 * k) + (pl.BlockSpec(memory_space=pltpu.VMEM),),
        input_output_aliases={i: 2 + i for i in range(2 * k)},
        compiler_params=pltpu.CompilerParams(has_side_effects=DATAFLOW),
    )(*[hbm(s) for s in shards], *[hbm(lax.empty((N_CHIP,) + s.shape, s.dtype)) for s in shards], after)
    send_sems, recv_sems, thru, token = outs[0], outs[1], outs[2:2 + 2 * k], outs[-1]

    def wait(*after):
        def wait_body(*refs):
            sent, received = plan(refs[:k], refs[k:2 * k], refs[2 * k], refs[2 * k + 1])
            for cp in sent:
                cp.wait_send()
            for cp in received:
                cp.wait_recv()

        res = pl.pallas_call(
            wait_body, name=name + "_wait", out_shape=tuple(bufs),
            in_specs=(HBM,) * (2 * k) + (SEM, SEM) + (ANY,) * len(after), out_specs=(HBM,) * (2 * k),
            input_output_aliases={i: i for i in range(2 * k)},
            compiler_params=pltpu.CompilerParams(has_side_effects=DATAFLOW),
        )(*thru, send_sems, recv_sems, *after)
        return res[:k], res[k:]

    return token, wait


def _behind(x, token):
    return x + token[0, 0]


def _plan_gather(src_refs, land_refs, send_sems, recv_sems):
    x, y, c, chips = _place()
    pairs = list(enumerate(zip(src_refs, land_refs)))
    sent = [_remote(s, l.at[2 * x + y], send_sems, recv_sems, 3 * i + j, (px, py, c))
            for i, (s, l) in pairs for j, (px, py) in enumerate(chips)]
    received = [_remote(s, l.at[2 * px + py], send_sems, recv_sems, 3 * i + j, (px, py, c))
                for i, (s, l) in pairs for j, (px, py) in enumerate(chips)]
    return sent, received


def _plan_swap(src_ref, land_ref, send_sems, recv_sems):
    x, y, c, _ = _place()
    _, other_half = _halves(c, src_ref.shape[1], 8)
    cp = _remote(src_ref.at[pl.ds(0, src_ref.shape[0]), other_half], land_ref, send_sems, recv_sems, 0, (x, y, 1 - c))
    return [cp], [cp]


def _plan_scatter(src_ref, land_ref, send_sems, recv_sems):
    x, y, c, chips = _place()
    sent = [_remote(src_ref.at[2 * px + py], land_ref.at[2 * x + y], send_sems, recv_sems, j, (px, py, c))
            for j, (px, py) in enumerate(chips)]
    received = [_remote(src_ref.at[2 * px + py], land_ref.at[2 * px + py], send_sems, recv_sems, j, (px, py, c))
                for j, (px, py) in enumerate(chips)]
    return sent, received


def _plan_share(src_ref, land_ref, send_sems, recv_sems):
    x, y, c, _ = _place()
    mine_half, other_half = _halves(c, src_ref.shape[0], 8)
    return ([_remote(src_ref.at[mine_half], src_ref.at[mine_half], send_sems, recv_sems, 0, (x, y, 1 - c))],
            [_remote(src_ref.at[other_half], src_ref.at[other_half], send_sems, recv_sems, 0, (x, y, 1 - c))])


class _GatherBehind:
    def __init__(self, name, shards, chip, after):
        self.chip = chip
        self.token, self.wait = _split_gather(name, shards, after)

    def result(self, *after):
        shards, lands = self.wait(*after)
        return [lax.dynamic_update_slice(land, shard[None], (self.chip, 0, 0)) for shard, land in zip(shards, lands)]


class _ReduceBehind:
    def __init__(self, name, chip, c_idx):
        self.name, self.chip, self.c_idx = name, chip, c_idx

    def start_slab(self, g):
        n_sh, rows, n = g.shape
        token, self.wait = _split_copy(self.name + "_swap", g, (n_sh, rows // 2, n), g.dtype, 1, _plan_swap)
        return token

    def pair(self, after):
        g, a = self.wait(after)
        h = _add_half(self.name + "_pair", g, a, self.c_idx)
        token, self.wait = _split_copy(self.name + "_scatter", h, h.shape, h.dtype, 3, _plan_scatter)
        return token

    def total(self, after):
        h, b = self.wait(after)
        b = lax.dynamic_update_slice(b, lax.dynamic_slice_in_dim(h, self.chip, 1, axis=0), (self.chip, 0, 0))
        f = _sum_chips(self.name + "_sum", b, self.c_idx)
        token, self.wait = _split_copy(self.name + "_share", f, (8, LANES), f.dtype, 1, _plan_share)
        return token

    def result(self, after):
        return self.wait(after)[0]


class _ReduceColsBehind(_ReduceBehind):
    def start(self, g_padded):
        g = _unpack_w_in_grad(g_padded)
        n = g.shape[1] // N_CHIP
        return self.start_slab(jnp.stack([g[:, k * n:(k + 1) * n] for k in range(N_CHIP)]))


def _f_adamw(w, g, m, v):
    m = ADAM_B1 * m + (1.0 - ADAM_B1) * g
    v = ADAM_B2 * v + (1.0 - ADAM_B2) * (g * g)
    m_hat = m / (1.0 - ADAM_B1 ** ADAM_STEP)
    v_hat = v / (1.0 - ADAM_B2 ** ADAM_STEP)
    return -ADAM_LR * (m_hat / (jnp.sqrt(v_hat) + ADAM_EPS) + ADAM_WD * w), m, v


def _adamw(name, w, g, m, v):
    rows, n = w.shape
    return _rowwise(name, lambda w, g, m, v: (_f_adamw(w, g, m, v), ()), rows, [(t, n, 0) for t in (w, g, m, v)], [],
                    [(n, F32)] * 3, [], tm=_row_tile(rows, 8, 256))


def _adamw_many(name, ws, gs, ms, vs):
    k = len(ws)

    def body(*refs):
        ins, outs = refs[:4 * k], refs[4 * k:]
        for i in range(k):
            res = _f_adamw(ins[i][...], ins[k + i][...], ins[2 * k + i][...], ins[3 * k + i][...])
            for j in range(3):
                outs[j * k + i][...] = res[j]

    out = pl.pallas_call(body, name=name, out_shape=[jax.ShapeDtypeStruct(w.shape, F32) for w in ws] * 3)(
        *ws, *gs, *ms, *vs)
    return out[:k], out[k:2 * k], out[2 * k:]


def _pack_rows(parts):
    rows = []
    for t in parts:
        t = t.reshape(-1)
        rows.append(jnp.pad(t, (0, -t.shape[0] % LANES)).reshape(-1, LANES))
    out = jnp.concatenate(rows, axis=0)
    return jnp.pad(out, ((0, -out.shape[0] % 8), (0, 0)))


def _unpack_rows(packed, shapes):
    out, r = [], 0
    for shp in shapes:
        n = int(np.prod(shp))
        nr = -(-n // LANES)
        out.append(packed[r:r + nr].reshape(-1)[:n].reshape(shp))
        r += nr
    return out


def _sum_blocks(name, g):
    def body(g_ref, o_ref):
        acc = g_ref[0]
        for k in range(1, g.shape[0]):
            acc = acc + g_ref[k]
        o_ref[...] = acc

    return pl.pallas_call(body, name=name, out_shape=jax.ShapeDtypeStruct(g.shape[1:], F32))(g)


def _silu(t):
    return t * _sigmoid(t)


def _ada_fwd(cc, w_ada):
    n = w_ada.shape[1]
    tn = _row_tile(n, LANES, 512)

    def body(cc_ref, w_ref, o_ref):
        o_ref[...] = _nn(_silu(cc_ref[...]), w_ref[...])

    return pl.pallas_call(
        body, name="ada_fwd", grid=(n // tn,), out_shape=jax.ShapeDtypeStruct((cc.shape[0], n), F32),
        in_specs=[pl.BlockSpec(cc.shape, lambda j: (0, 0)), pl.BlockSpec((w_ada.shape[0], tn), lambda j: (0, j))],
        out_specs=pl.BlockSpec((cc.shape[0], tn), lambda j: (0, j)), compiler_params=_cp("parallel"),
    )(cc, w_ada)


def _ada_bwd(cc, dm, w_ada):
    d, n = w_ada.shape
    tn = _row_tile(n, LANES, 512)

    def body(cc_ref, dm_ref, w_ref, gw_ref, ds_ref):
        @pl.when(pl.program_id(0) == 0)
        def _():
            ds_ref[...] = jnp.zeros_like(ds_ref)

        gw_ref[...] = _raw_dot("tn", _silu(cc_ref[...]), dm_ref[...], True)
        ds_ref[...] += _raw_dot("nt", dm_ref[...], w_ref[...], False)

    return pl.pallas_call(
        body, name="ada_bwd", grid=(n // tn,),
        out_shape=[jax.ShapeDtypeStruct((d, n), F32), jax.ShapeDtypeStruct(cc.shape, F32)],
        in_specs=[pl.BlockSpec(cc.shape, lambda j: (0, 0)), pl.BlockSpec((cc.shape[0], tn), lambda j: (0, j)),
                  pl.BlockSpec((d, tn), lambda j: (0, j))],
        out_specs=[pl.BlockSpec((d, tn), lambda j: (0, j)), pl.BlockSpec(cc.shape, lambda j: (0, 0))],
        compiler_params=_cp("arbitrary"),
    )(cc, dm, w_ada)


def _c_ctx_grad(parts, c_ctx):
    def body(p_ref, c_ref, o_ref):
        ds = ((p_ref[0] + p_ref[1]) + p_ref[2]) + p_ref[3]
        _, vjp = jax.vjp(_silu, c_ref[...])
        o_ref[...] = vjp(ds)[0]

    return pl.pallas_call(body, name="c_ctx_grad", out_shape=jax.ShapeDtypeStruct(c_ctx.shape, F32))(parts, c_ctx)


def kernel(x, c, ctx, c_ctx, w_ada, b_ada, g_pre_mix, g_post_mix, g_pre_ffn, g_post_ffn, w_in, attn_sink, w_gate_fwd, b_gate_fwd, w_gate_bwd, b_gate_bwd, g_gla_norm, w_out, w_ffn_in, w_ffn_out, loss_target, m_c_ctx, m_w_ada, m_b_ada, m_g_pre_mix, m_g_post_mix, m_g_pre_ffn, m_g_post_ffn, m_w_in, m_attn_sink, m_w_gate_fwd, m_b_gate_fwd, m_w_gate_bwd, m_b_gate_bwd, m_g_gla_norm, m_w_out, m_w_ffn_in, m_w_ffn_out, v_c_ctx, v_w_ada, v_b_ada, v_g_pre_mix, v_g_post_mix, v_g_pre_ffn, v_g_post_ffn, v_w_in, v_attn_sink, v_w_gate_fwd, v_b_gate_fwd, v_w_gate_bwd, v_b_gate_bwd, v_g_gla_norm, v_w_out, v_w_ffn_in, v_w_ffn_out):
    xi, yi, ci = lax.axis_index("x"), lax.axis_index("y"), lax.axis_index("c")
    dev, chip = 4 * xi + 2 * yi + ci, 2 * xi + yi
    c_idx = jnp.reshape(ci, (1,)).astype(jnp.int32)
    d = x.shape[-1]
    n_ada, n_in, n_f = w_ada.shape[-1], w_in.shape[-1], w_ffn_in.shape[-1]
    r_out, r_f = w_out.shape[1], w_ffn_out.shape[1]
    n_gate = w_gate_fwd.shape[-1]
    by_chip = lambda t: t[0::2]

    rc = -(-d // LANES)
    g1 = _ag_small("gather_cond", _pack_rows([c[0], w_gate_fwd[0], w_gate_bwd[0]]))
    c_all = g1[:, :rc].reshape(N_DEV, -1)[:, :d]
    gr = GATE_RANK * n_gate // LANES
    gate_full = lambda off: jnp.transpose(by_chip(g1)[:, off:off + gr].reshape(N_CHIP, GATE_RANK, n_gate),
                                          (1, 0, 2)).reshape(GATE_RANK, N_CHIP * n_gate)
    wgf, wgb = gate_full(rc), gate_full(rc + gr)
    cc = jnp.concatenate([c_all, c_ctx[None, :], jnp.zeros((7, d), F32)], axis=0)

    g2 = _ag_small("gather_ada", _ada_fwd(cc, w_ada[0]).reshape(-1, LANES))
    ada_all = jnp.transpose(by_chip(g2).reshape(N_CHIP, 16, n_ada), (1, 0, 2)).reshape(16, N_CHIP * n_ada) + b_ada
    first = _GatherBehind("gather_w_in", [w_in[0].astype(BF16)], chip, g2)
    late_shards = [w_out[0].astype(BF16), jnp.transpose(w_ffn_in[0]).astype(BF16), w_ffn_out[0].astype(BF16)]
    late = []

    def first_weights(*after):
        w_in_g, = first.result(*after, *late_shards)
        late.append(_GatherBehind("gather_late", late_shards, chip, w_in_g))
        return _pack_w_in(jnp.concatenate([w_in_g[k] for k in range(N_CHIP)], axis=1)), late[0].token

    def late_weights(after):
        return [t.reshape(-1, d) for t in late[0].result(after)]

    ada_all = _behind(ada_all, first.token)
    ada = lax.dynamic_slice(ada_all, (dev, 0), (1, N_CHIP * n_ada))
    ada_c = ada_all[N_DEV:N_DEV + 1]

    w = _prep_gate_weights(wgf, wgb)
    w.update(w_in=first_weights, g_pre_mix=g_pre_mix, g_post_mix=g_post_mix, g_pre_ffn=g_pre_ffn, g_post_ffn=g_post_ffn,
             attn_sink=attn_sink, b_gate_fwd=b_gate_fwd, b_gate_bwd=b_gate_bwd, g_gla_norm=g_gla_norm)

    reduce_behind = _ReduceBehind("reduce_late", chip, c_idx)
    reduce_w_in = _ReduceColsBehind("reduce_w_in", chip, c_idx)
    loss_lanes, grad_x, g, d_ada, d_ada_c = _local_step(x[0], ctx[0], loss_target[0], ada, ada_c, w, late_weights,
                                                        reduce_behind, reduce_w_in)

    small = ("g_pre_mix", "g_post_mix", "g_pre_ffn", "g_post_ffn", "attn_sink", "b_gate_fwd", "b_gate_bwd",
             "g_gla_norm", "w_gate_fwd", "w_gate_bwd")
    shapes = [(1, 6 * d)] * 2 + [g[n].shape for n in small] + [(1, LANES)]
    g3 = _ag_small("gather_small_grads", _pack_rows([d_ada, d_ada_c] + [g[n] for n in small] + [loss_lanes]))
    tot = dict(zip(("d_ada", "d_ada_c") + small + ("loss",),
                   _unpack_rows(_sum_blocks("sum_small_grads", g3), shapes)))
    r_ada = 6 * d // LANES
    dm = jnp.concatenate([g3[:, :r_ada].reshape(N_DEV, 6 * d), tot["d_ada_c"], jnp.zeros((7, 6 * d), F32)], axis=0)
    grads = {n: tot[n] for n in small[:8]}
    grads["b_ada"] = _sum_blocks("sum_b_ada", dm.reshape(16, r_ada, LANES)).reshape(1, 6 * d)
    grads["w_gate_fwd"] = lax.dynamic_slice(tot["w_gate_fwd"], (0, chip * n_gate), (GATE_RANK, n_gate))[None]
    grads["w_gate_bwd"] = lax.dynamic_slice(tot["w_gate_bwd"], (0, chip * n_gate), (GATE_RANK, n_gate))[None]
    gw_ada, dsc = _ada_bwd(cc, lax.dynamic_slice(dm, (0, chip * n_ada), (16, n_ada)), w_ada[0])
    grads["w_ada"] = gw_ada[None]
    g4 = _ag_small("gather_c_ctx", _pack_rows([dsc[N_DEV]]))
    grads["c_ctx"] = _c_ctx_grad(by_chip(g4), _pack_rows([c_ctx])).reshape(-1)[:d]

    grads["w_in"] = reduce_w_in.result(g4)[None]
    part = lambda n: g["late"][g["late_at"][n]:g["late_at"][n] + g["late_rows"][n]]
    grads["w_ffn_in"], grads["w_ffn_out"], grads["w_out"] = (jnp.transpose(part("w_ffn_in_t"))[None],
                                                            part("w_ffn_out")[None], part("w_out")[None])

    names = ("c_ctx", "w_ada", "b_ada", "g_pre_mix", "g_post_mix", "g_pre_ffn", "g_post_ffn", "w_in", "attn_sink",
             "w_gate_fwd", "b_gate_fwd", "w_gate_bwd", "b_gate_bwd", "g_gla_norm", "w_out", "w_ffn_in", "w_ffn_out")
    weights = dict(zip(names, (c_ctx, w_ada, b_ada, g_pre_mix, g_post_mix, g_pre_ffn, g_post_ffn, w_in, attn_sink,
                               w_gate_fwd, b_gate_fwd, w_gate_bwd, b_gate_bwd, g_gla_norm, w_out, w_ffn_in,
                               w_ffn_out)))
    m_in = dict(zip(names, (m_c_ctx, m_w_ada, m_b_ada, m_g_pre_mix, m_g_post_mix, m_g_pre_ffn, m_g_post_ffn, m_w_in,
                            m_attn_sink, m_w_gate_fwd, m_b_gate_fwd, m_w_gate_bwd, m_b_gate_bwd, m_g_gla_norm,
                            m_w_out, m_w_ffn_in, m_w_ffn_out)))
    v_in = dict(zip(names, (v_c_ctx, v_w_ada, v_b_ada, v_g_pre_mix, v_g_post_mix, v_g_pre_ffn, v_g_post_ffn, v_w_in,
                            v_attn_sink, v_w_gate_fwd, v_b_gate_fwd, v_w_gate_bwd, v_b_gate_bwd, v_g_gla_norm,
                            v_w_out, v_w_ffn_in, v_w_ffn_out)))
    large = ("w_ada", "w_in", "w_out", "w_ffn_in", "w_ffn_out")
    tiny = tuple(n for n in names if n not in large)
    delta, new_m, new_v = {}, {}, {}
    for n in large:
        dl, nm, nv = _adamw("adamw_" + n, weights[n][0], grads[n][0], m_in[n][0], v_in[n][0])
        delta[n], new_m[n], new_v[n] = dl[None], nm[None], nv[None]
    for n in tiny:
        grads[n] = grads[n].reshape(weights[n].shape)
    as_rows = lambda t: t.reshape(-1, t.shape[-1])
    res = _adamw_many("adamw_small", *[[as_rows(t[n]) for n in tiny] for t in (weights, grads, m_in, v_in)])
    for out, vals in zip((delta, new_m, new_v), res):
        out.update({n: val.reshape(weights[n].shape) for n, val in zip(tiny, vals)})

    return (tot["loss"][0, 0], grad_x[None], *[grads[n] for n in names], *[delta[n] for n in names], *[new_m[n] for n in names],
            *[new_v[n] for n in names])
```

```python
import functools

import jax
import jax.numpy as jnp
import numpy as np
from jax import lax
from jax.experimental import pallas as pl
from jax.experimental.pallas import tpu as pltpu

F32 = jnp.float32
BF16 = jnp.bfloat16
MESH = pl.DeviceIdType.MESH

HEAD_DIM = 64
ATT_HEADS = 8
ATT_KV_HEADS = 2
ATT_GROUP = ATT_HEADS // ATT_KV_HEADS
WINDOW = 128
BLOCK = 128
GRID_W = 64
ROPE_BASE = 10000.0
GLA_HEADS = 8
GLA_DK = 32
GLA_DV = 64
GLA_CHUNK = 64
GATE_RANK = 16
GATE_TAU = 16.0
NEG_INF = -1e30
QW = ATT_HEADS * HEAD_DIM
KVW = ATT_KV_HEADS * HEAD_DIM
GKW = GLA_HEADS * GLA_DK
GVW = GLA_HEADS * GLA_DV
IN_COLS = QW + 2 * KVW + 2 * GKW + 2 * GVW + 2 * GATE_RANK
LANES = 128
IN_PAD = IN_COLS + LANES - 2 * GATE_RANK
C_Q, C_GV, C_GG = 0, QW, QW + GVW
C_K = C_GG + GVW
C_V = C_K + KVW
C_GQ = C_V + KVW
C_GK = C_GQ + GKW
C_Z = C_GK + GKW
MIX = QW + GVW

ADAM_LR, ADAM_B1, ADAM_B2, ADAM_EPS, ADAM_WD, ADAM_STEP = 0.001, 0.9, 0.999, 1e-08, 0.01, 10

VMEM_LIMIT = 56 * 1024 * 1024


def _cp(*sem):
    return pltpu.CompilerParams(dimension_semantics=sem, vmem_limit_bytes=VMEM_LIMIT)


def _pick(n, cands):
    for t in cands:
        if n % t == 0:
            return t
    return n


_DIMS = {"nn": (((1,), (0,)), ((), ())), "nt": (((1,), (1,)), ((), ())), "tn": (((0,), (0,)), ((), ()))}


def _raw_dot(mode, a, b, hi):
    dot = lambda u, v: lax.dot_general(u, v, _DIMS[mode], preferred_element_type=F32)
    if not hi:
        return dot(a.astype(BF16), b.astype(BF16))
    a, b = a.astype(F32), b.astype(F32)
    a_hi, b_hi = a.astype(BF16), b.astype(BF16)
    out = dot(a_hi, b_hi)
    if hi != "a":
        out = out + dot((a - a_hi.astype(F32)).astype(BF16), b_hi)
    if hi != "b":
        out = out + dot(a_hi, (b - b_hi.astype(F32)).astype(BF16))
    return out


def _make_dot(mode, hi):
    @jax.custom_vjp
    def dot(a, b):
        return _raw_dot(mode, a, b, hi)

    def fwd(a, b):
        return _raw_dot(mode, a, b, hi), (a, b)

    def bwd(res, dc):
        a, b = res
        if mode == "nn":
            return (_raw_dot("nt", dc, b, "b" if hi == "b" else bool(hi)),
                    _raw_dot("tn", a, dc, "a" if hi == "a" else bool(hi)))
        if mode == "nt":
            return _raw_dot("nn", dc, b, bool(hi)), _raw_dot("tn", dc, a, bool(hi))
        return _raw_dot("nt", b, dc, bool(hi)), _raw_dot("nn", a, dc, bool(hi))

    dot.defvjp(fwd, bwd)
    return dot


_nn, _nt, _tn = _make_dot("nn", False), _make_dot("nt", False), _make_dot("tn", False)
_nn_mask, _nn_by_exact = _make_dot("nn", "a"), _make_dot("nn", "b")


MM_VMEM_BUDGET = 44 * 1024 * 1024


def _halvings(n):
    out = [n]
    while out[-1] % (2 * LANES) == 0:
        out.append(out[-1] // 2)
    return out


def _mm_tiles(mode, m, n, k, a_bytes, b_bytes, o_bytes, init_bytes=0):
    tms = [t for t in dict.fromkeys((m, m // 2, m // 4, 2048, 1024, 512, 256, 128))
           if m % t == 0 and t % (LANES if mode == "tn" else 16) == 0 and t <= 4096] or [m]
    if mode == "tn":
        fits = [(k // tk + 0.5 * (m // tm), tm, tk)
                for tk in (4096, 2048, 1024, 512, 256, 128) if k % tk == 0 for tm in tms
                if 2 * (tk * tm * a_bytes + tk * n * b_bytes + tm * n * (o_bytes + init_bytes)) <= MM_VMEM_BUDGET]
        if fits:
            _, tm, tk = min(fits)
            return tm, n, tk
    tks = ([t for t in (512, 256, 128) if k % t == 0] or [k]) if mode == "tn" else _halvings(k)
    for tn in _halvings(n):
        for tk in tks:
            for tm in tms:
                acc = tm * tn * 4 if (k // tk > 1 and o_bytes != 4) else 0
                tiles = tm * tk * a_bytes + tk * tn * b_bytes + tm * tn * (o_bytes + init_bytes)
                if 2 * tiles + acc <= MM_VMEM_BUDGET:
                    return tm, tn, tk
    return tms[-1], _halvings(n)[-1], tks[-1]


def _mm(name, a, b, mode, out_dtype=F32, init=None, after=None):
    follow = () if after is None else (after,)
    if mode == "nn":
        (m, k), n = a.shape, b.shape[1]
    elif mode == "nt":
        (m, k), n = a.shape, b.shape[0]
    else:
        (k, m), n = a.shape, b.shape[1]
    tm, tn, tk = _mm_tiles(mode, m, n, k, a.dtype.itemsize, b.dtype.itemsize, jnp.dtype(out_dtype).itemsize,
                           0 if init is None else 4)
    nk = k // tk
    use_acc = nk > 1 and out_dtype != F32

    inits = () if init is None else (init,)

    def body(a_ref, b_ref, *rest):
        rest = rest[:len(inits)] + rest[len(inits) + len(follow):]
        o_ref, acc = rest[len(inits)], rest[len(inits) + 1:]
        part = _raw_dot(mode, a_ref[...], b_ref[...], False)
        first = lambda: part + rest[0][...] if inits else part
        if nk == 1:
            o_ref[...] = first().astype(o_ref.dtype)
            return
        acc_ref = acc[0] if use_acc else o_ref
        kk = pl.program_id(2)

        @pl.when(kk == 0)
        def _():
            acc_ref[...] = first()

        @pl.when(kk > 0)
        def _():
            acc_ref[...] += part

        if use_acc:
            @pl.when(kk == nk - 1)
            def _():
                o_ref[...] = acc_ref[...].astype(o_ref.dtype)

    if mode == "nn":
        a_spec = pl.BlockSpec((tm, tk), lambda i, j, kk: (i, kk))
        b_spec = pl.BlockSpec((tk, tn), lambda i, j, kk: (kk, j))
    elif mode == "nt":
        a_spec = pl.BlockSpec((tm, tk), lambda i, j, kk: (i, kk))
        b_spec = pl.BlockSpec((tn, tk), lambda i, j, kk: (j, kk))
    else:
        a_spec = pl.BlockSpec((tk, tm), lambda i, j, kk: (kk, i))
        b_spec = pl.BlockSpec((tk, tn), lambda i, j, kk: (kk, j))
    return pl.pallas_call(
        body, name=name, grid=(m // tm, n // tn, nk),
        in_specs=[a_spec, b_spec] + [pl.BlockSpec((tm, tn), lambda i, j, kk: (i, j))] * len(inits)
        + [pl.BlockSpec(memory_space=pl.ANY)] * len(follow),
        out_specs=pl.BlockSpec((tm, tn), lambda i, j, kk: (i, j)),
        out_shape=jax.ShapeDtypeStruct((m, n), out_dtype),
        scratch_shapes=[pltpu.VMEM((tm, tn), F32)] if use_acc else [],
        compiler_params=_cp("parallel", "parallel", "arbitrary"),
    )(a, b, *inits, *follow)


def _slab_layout(rows):
    offsets, at = [], 0
    for r in rows:
        at = -(-at // r) * r
        offsets.append(at)
        at += r
    return offsets, -(-at // 32) * 32


def _slab_zero_gaps(name, shape, rows, offsets):
    gaps = [(o + r, nxt) for o, r, nxt in zip(offsets, rows, offsets[1:] + [shape[1]]) if nxt > o + r]
    slab = None
    for i, (lo, hi) in enumerate(gaps):
        step = int(np.gcd(lo, hi - lo))

        def body(*refs):
            refs[-1][...] = jnp.zeros_like(refs[-1])

        slab = pl.pallas_call(
            body, name=f"{name}_{i}", grid=(shape[0], (hi - lo) // step), out_shape=jax.ShapeDtypeStruct(shape, F32),
            in_specs=[] if slab is None else [pl.BlockSpec(memory_space=pl.ANY)],
            out_specs=pl.BlockSpec((1, step, shape[2]), functools.partial(lambda k, j, b: (k, b + j, 0), b=lo // step)),
            input_output_aliases={} if slab is None else {0: 0}, compiler_params=_cp("parallel", "parallel"),
        )(*(() if slab is None else (slab,)))
    return slab


def _dw_into_slab(name, a, b, slab, shape, at):
    (k, m), n = a.shape, b.shape[1]
    r = m // N_CHIP
    fits = [(k // tk + 0.5 * (m // tm), tm, tk)
            for tk in (4096, 2048, 1024, 512, 256, 128) if k % tk == 0 for tm in (m, m // 2, r) if tm % LANES == 0
            if 2 * (tk * tm * a.dtype.itemsize + tk * n * b.dtype.itemsize + tm * n * 4) <= MM_VMEM_BUDGET]
    _, tm, tk = min(fits)
    per, nk = tm // r, k // tk

    def body(a_ref, b_ref, *rest):
        o_ref = rest[-1]
        part = _raw_dot("tn", a_ref[...], b_ref[...], False).reshape(o_ref.shape)
        if nk == 1:
            o_ref[...] = part
            return
        kk = pl.program_id(1)

        @pl.when(kk == 0)
        def _():
            o_ref[...] = part

        @pl.when(kk > 0)
        def _():
            o_ref[...] += part

    prev = () if slab is None else (slab,)
    return pl.pallas_call(
        body, name=name, grid=(m // tm, nk), out_shape=jax.ShapeDtypeStruct(shape, F32),
        in_specs=[pl.BlockSpec((tk, tm), lambda i, kk: (kk, i)), pl.BlockSpec((tk, n), lambda i, kk: (kk, 0))]
        + [pl.BlockSpec(memory_space=pl.ANY)] * len(prev),
        out_specs=pl.BlockSpec((per, r, n), lambda i, kk: (i, at // r, 0)),
        input_output_aliases={2: 0} if prev else {}, compiler_params=_cp("parallel", "arbitrary"),
    )(a, b, *prev)


def _rowwise(name, fn, rows, row_ins, full_ins, row_outs, acc_outs, tm=None):
    tm = tm or _pick(rows, (1024, 512, 256, 128))
    n_r, n_f, n_o, n_a = len(row_ins), len(full_ins), len(row_outs), len(acc_outs)

    def body(*refs):
        ins, outs = refs[:n_r + n_f], refs[n_r + n_f:]
        vals = [r[...].astype(F32) for r in ins]
        ro, ao = fn(*vals)
        for r, val in zip(outs[:n_o], ro):
            r[...] = val.astype(r.dtype)
        if n_a:
            @pl.when(pl.program_id(0) == 0)
            def _():
                for r in outs[n_o:]:
                    r[...] = jnp.zeros_like(r)

            for r, val in zip(outs[n_o:], ao):
                r[...] += val

    in_specs = [pl.BlockSpec((tm, w), functools.partial(lambda i, cb: (i, cb), cb=cb)) for _, w, cb in row_ins]
    in_specs += [pl.BlockSpec(a.shape, lambda i: (0, 0)) for a in full_ins]
    out_specs = [pl.BlockSpec((tm, w), lambda i: (i, 0)) for w, _ in row_outs]
    out_specs += [pl.BlockSpec(s, lambda i: (0, 0)) for s in acc_outs]
    out_shape = [jax.ShapeDtypeStruct((rows, w), dt) for w, dt in row_outs]
    out_shape += [jax.ShapeDtypeStruct(s, F32) for s in acc_outs]
    return pl.pallas_call(
        body, name=name, grid=(rows // tm,), in_specs=in_specs, out_specs=out_specs, out_shape=out_shape,
        compiler_params=_cp("arbitrary" if n_a else "parallel"),
    )(*[a for a, _, _ in row_ins], *full_ins)


def _rn(x):
    return x * lax.rsqrt(jnp.mean(x * x, axis=-1, keepdims=True) + 1e-6)


def _sigmoid(t):
    return 1.0 / (1.0 + jnp.exp(-t))


def _f_norm_mod(x, g, sh, sc):
    return _rn(x) * g * (1.0 + sc) + sh


def _f_post_res(xr, y, g, gate):
    return xr + gate * (_rn(y) * g)


@jax.custom_vjp
def _f_swiglu(g, u):
    return g * _sigmoid(g) * u


def _f_swiglu_fwd(g, u):
    s = _sigmoid(g)
    return g * s * u, (g, u, s)


def _f_swiglu_bwd(res, da):
    g, u, s = res
    gs = g * s
    return da * u * (s + gs * (1.0 - s)), da * gs


_f_swiglu.defvjp(_f_swiglu_fwd, _f_swiglu_bwd)


def _logsig(u):
    return jnp.minimum(u, 0.0) - jnp.log(1.0 + jnp.exp(-jnp.abs(u)))


def _f_gate(z, wf, wb, bf, bb):
    return _logsig(_nn(z, wf) + bf) / GATE_TAU, _logsig(_nn(z, wb) + bb) / GATE_TAU


def _f_gla_out(of, ob, gg, gt, bd):
    o = of + ob
    ms = _nn_by_exact(o * o, bd)
    return o * lax.rsqrt(ms + 1e-6) * gt * (gg * _sigmoid(gg))


def _norm_mod(name, x, g, sh, sc):
    rows, d = x.shape
    return _rowwise(name, lambda x, g, sh, sc: ((_f_norm_mod(x, g, sh, sc),), ()), rows,
                    [(x, d, 0)], [g, sh, sc], [(d, BF16)], [])[0]


def _rn_bwd(x, dn):
    r = lax.rsqrt(jnp.mean(x * x, axis=-1, keepdims=True) + 1e-6)
    n = x * r
    return r * (dn - n * jnp.mean(dn * n, axis=-1, keepdims=True)), n


def _norm_mod_grads(dh, x, g, sc):
    dx, n = _rn_bwd(x, dh * (g * (1.0 + sc)))
    t = jnp.sum(dh * n, axis=0, keepdims=True)
    return dx, (1.0 + sc) * t, jnp.sum(dh, axis=0, keepdims=True), g * t


def _post_res_grads(dout, y, g, gate):
    dy, n = _rn_bwd(y, dout * (gate * g))
    t = jnp.sum(dout * n, axis=0, keepdims=True)
    return dy, gate * t, g * t


def _norm_mod_bwd(name, dh, dres, x, g, sh, sc):
    rows, d = x.shape

    def fn(dh, dres, x, g, sh, sc):
        dx, dg, dsh, dsc = _norm_mod_grads(dh, x, g, sc)
        return (dx + dres,), (dg, dsh, dsc)

    return _rowwise(name, fn, rows, [(dh, d, 0), (dres, d, 0), (x, d, 0)], [g, sh, sc], [(d, F32)],
                    [(1, d)] * 3)


def _post_res_norm_mod(name, xr, y, g_post, gate, g_pre, sh, sc):
    rows, d = xr.shape

    def fn(xr, y, g_post, gate, g_pre, sh, sc):
        x1 = _f_post_res(xr, y, g_post, gate)
        return (x1, _f_norm_mod(x1, g_pre, sh, sc)), ()

    return _rowwise(name, fn, rows, [(xr, d, 0), (y, d, 0)], [g_post, gate, g_pre, sh, sc], [(d, F32), (d, BF16)], [])


def _norm_mod_post_res_bwd(name, dh, dres, x1, y, g_pre, sh, sc, g_post, gate):
    rows, d = x1.shape

    def fn(dh, dres, x1, y, g_pre, sh, sc, g_post, gate):
        dx1, dg_pre, dsh, dsc = _norm_mod_grads(dh, x1, g_pre, sc)
        dx1 = dx1 + dres
        dy, dg_post, dgate = _post_res_grads(dx1, y, g_post, gate)
        return (dx1, dy), (dg_pre, dsh, dsc, dg_post, dgate)

    return _rowwise(name, fn, rows, [(dh, d, 0), (dres, d, 0), (x1, d, 0), (y, d, 0)], [g_pre, sh, sc, g_post, gate],
                    [(d, F32), (d, BF16)], [(1, d)] * 5)


def _post_res_loss(name, xr, y, g, gate, target):
    rows, d = xr.shape

    def fn(xr, y, target, g, gate):
        diff = _f_post_res(xr, y, g, gate) - target
        part = 0.5 * jnp.sum(jnp.mean(diff * diff, axis=-1, keepdims=True), axis=0, keepdims=True)
        dx2 = diff * (1.0 / d)
        dy, dg, dgate = _post_res_grads(dx2, y, g, gate)
        return (dx2, dy), (jnp.broadcast_to(part, (1, LANES)), dg, dgate)

    return _rowwise(name, fn, rows, [(xr, d, 0), (y, d, 0), (target, d, 0)], [g, gate], [(d, F32), (d, BF16)],
                    [(1, LANES), (1, d), (1, d)])


def _mm_rows(name, a, b, mode, fn, extras, outs):
    m, k = a.shape
    tm = _pick(m, (256, 128))

    def body(a_ref, b_ref, *rest):
        tiles = fn(_raw_dot(mode, a_ref[...], b_ref[...], False), *[e[...] for e in rest[:len(extras)]])
        for r, val in zip(rest[len(extras):], tiles):
            r[...] = val.astype(r.dtype)

    row = lambda w: pl.BlockSpec((tm, w), lambda i: (i, 0))
    return pl.pallas_call(
        body, name=name, grid=(m // tm,),
        in_specs=[row(k), pl.BlockSpec(b.shape, lambda i: (0, 0))] + [row(e.shape[1]) for e in extras],
        out_specs=[row(w) for w, _ in outs], out_shape=[jax.ShapeDtypeStruct((m, w), dt) for w, dt in outs],
        compiler_params=_cp("parallel"),
    )(a, b, *extras)


def _ffn_in_swiglu(name, h, w_t):
    f = w_t.shape[0] // 2
    fn = lambda u: (u, _f_swiglu(u[:, :f], u[:, f:]))
    return _mm_rows(name, h, w_t, "nt", fn, [], [(2 * f, BF16), (f, BF16)])


def _ffn_out_dx_swiglu_bwd(name, df, w_out, u):
    f = w_out.shape[0]

    def fn(da, u):
        u = u.astype(F32)
        _, vjp = jax.vjp(_f_swiglu, u[:, :f], u[:, f:])
        return (jnp.concatenate(vjp(da), axis=1),)

    return _mm_rows(name, df, w_out, "nt", fn, [u], [(2 * f, BF16)])[0]


def _gate_fwd(name, p, wf, wb, bf, bb):
    rows = p.shape[0]
    return _rowwise(name, lambda z, wf, wb, bf, bb: (_f_gate(z, wf, wb, bf, bb), ()), rows,
                    [(p, LANES, C_Z // LANES)], [wf, wb, bf, bb], [(GKW, F32)] * 2, [])


def _gate_bwd(name, p, dla_f, dla_b, wf, wb, bf, bb):
    rows = p.shape[0]

    def fn(z, dlf, dlb, wf, wb, bf, bb):
        _, vjp = jax.vjp(_f_gate, z, wf, wb, bf, bb)
        dz, dwf, dwb, dbf, dbb = vjp((dlf, dlb))
        return (dz,), (dwf, dwb, dbf, dbb)

    return _rowwise(name, fn, rows, [(p, LANES, C_Z // LANES), (dla_f, GKW, 0), (dla_b, GKW, 0)],
                    [wf, wb, bf, bb], [(LANES, BF16)], [(LANES, GKW), (LANES, GKW), (1, GKW), (1, GKW)])


def _head_mean_matrix():
    h = np.arange(GVW) // GLA_DV
    return jnp.asarray((h[:, None] == h[None, :]).astype(np.float32) / GLA_DV)


def _gla_out(name, attn, of, ob, p, gt):
    rows = of.shape[0]
    bd = _head_mean_matrix()
    fn = lambda attn, of, ob, gg, gt, bd: ((jnp.concatenate([attn, _f_gla_out(of, ob, gg, gt, bd)], axis=1),), ())
    return _rowwise(name, fn, rows, [(attn, QW, 0), (of, GVW, 0), (ob, GVW, 0), (p, GVW, C_GG // GVW)], [gt, bd],
                    [(MIX, BF16)], [])[0]


def _gla_out_bwd(name, dmix, of, ob, p, gt):
    rows = of.shape[0]
    bd = _head_mean_matrix()

    def fn(dm, of, ob, gg, gt, bd):
        _, vjp = jax.vjp(lambda of, gg, gt: _f_gla_out(of, ob, gg, gt, bd), of, gg, gt)
        do, dgg, dgt = vjp(dm)
        return (do, dgg), (dgt,)

    return _rowwise(name, fn, rows, [(dmix, GVW, 1), (of, GVW, 0), (ob, GVW, 0), (p, GVW, C_GG // GVW)], [gt, bd],
                    [(GVW, BF16), (GVW, BF16)], [(1, GVW)])


def _rope_tables(n_tokens):
    t = jnp.arange(n_tokens)
    row = (t // GRID_W).astype(F32)
    col = (t % GRID_W).astype(F32)
    half = HEAD_DIM // 2
    inv_freq = ROPE_BASE ** (-jnp.arange(0, half, 2, dtype=F32) / half)
    ang_r = row[:, None] * inv_freq[None, :]
    ang_c = col[:, None] * inv_freq[None, :]
    ang = jnp.concatenate([ang_r, ang_r, ang_c, ang_c], axis=-1)
    sign = jnp.concatenate([-jnp.ones((16,), F32), jnp.ones((16,), F32)] * 2)
    cos, sin = jnp.cos(ang), jnp.sin(ang) * sign[None, :]
    return jnp.tile(cos, (1, 2)), jnp.tile(sin, (1, 2))


def _rot_pairs(x):
    w = x.shape[-1]
    lane = lax.broadcasted_iota(jnp.int32, x.shape, x.ndim - 1)
    return jnp.where((lane % 32) < 16, pltpu.roll(x, w - 16, x.ndim - 1), pltpu.roll(x, 16, x.ndim - 1))


def _rope_apply(x, cos, sin_signed, inverse):
    reps = x.shape[-1] // LANES
    cos = jnp.concatenate([cos] * reps, axis=-1) if reps > 1 else cos
    sin = jnp.concatenate([sin_signed] * reps, axis=-1) if reps > 1 else sin_signed
    if inverse:
        return x * cos + _rot_pairs(x * sin)
    return x * cos + _rot_pairs(x) * sin


def _rope_fwd(name, p, cos, sin):
    rows = p.shape[0]

    def fn(q, k, v, cos, sin):
        return (_rope_apply(q, cos, sin, False), _rope_apply(k, cos, sin, False), v), ()

    return _rowwise(name, fn, rows, [(p, QW, 0), (p, KVW, C_K // KVW), (p, KVW, C_V // KVW), (cos, LANES, 0),
                                     (sin, LANES, 0)], [], [(QW, BF16), (KVW, BF16), (KVW, BF16)], [])


def _proj_grad(name, dq_rot, dk_rot, dv, cos, sin, gla_f, gla_b, dgg, dz):
    rows = dq_rot.shape[0]

    def fn(dq, dk, dv, cos, sin, gqf, gkf, gvf, gqb, gkb, gvb, dgg, dz):
        parts = [_rope_apply(dq, cos, sin, True), gvf + gvb, dgg, _rope_apply(dk, cos, sin, True), dv, gqf + gqb,
                 gkf + gkb, dz]
        return (jnp.concatenate(parts, axis=1),), ()

    ins = [(dq_rot, QW), (dk_rot, KVW), (dv, KVW), (cos, LANES), (sin, LANES)]
    ins += [(t, t.shape[1]) for t in (*gla_f, *gla_b)] + [(dgg, GVW), (dz, LANES)]
    return _rowwise(name, fn, rows, [(t, w, 0) for t, w in ins], [], [(IN_PAD, BF16)], [])[0]


ATT_STACK = 2
N_STACKS = ATT_HEADS // ATT_STACK
GROUP_ROWS = ATT_STACK * BLOCK


def _kv_of(t):
    return t * ATT_STACK // ATT_GROUP


ATT_SCALE = HEAD_DIM ** -0.5


def _attn_bias(n_tokens):
    nb = n_tokens // BLOCK
    i = (jnp.arange(GROUP_ROWS) % BLOCK)[:, None]
    j = jnp.arange(3 * BLOCK)[None, :]

    def one(n):
        kpos = (n - 1) * BLOCK + j
        return jnp.where((jnp.abs(j - BLOCK - i) <= WINDOW) & (kpos >= 0) & (kpos < n_tokens), 0.0, NEG_INF)

    return jnp.stack([one(0), one(1), one(nb - 1)]).astype(F32)


def _attn_bias_spec(n_tokens):
    nb = n_tokens // BLOCK
    return pl.BlockSpec((1, GROUP_ROWS, 3 * BLOCK), lambda n: (jnp.where(n == 0, 0, jnp.where(n == nb - 1, 2, 1)), 0, 0))


def _attn_setup(sink):
    row = lax.broadcasted_iota(jnp.int32, (GROUP_ROWS, 1), 0)
    group = sum((row >= g * BLOCK).astype(jnp.int32) for g in range(1, ATT_STACK))
    head_id = lax.broadcasted_iota(jnp.int32, (1, ATT_HEADS), 1)
    sks = []
    for h in range(N_STACKS):
        sk = jnp.zeros((GROUP_ROWS, 1), F32)
        for g in range(ATT_STACK):
            one = jnp.sum(jnp.where(head_id == h * ATT_STACK + g, sink, 0.0), axis=-1, keepdims=True)
            sk = jnp.where(group == g, one, sk)
        sks.append(sk)
    return group, sks


def _attn_weights(q, kw, kc, sk, bias):
    q = q * ATT_SCALE
    s_w = _raw_dot("nt", q, kw, False) + bias
    s_c = _raw_dot("nt", q, kc, False)
    m = jnp.maximum(jnp.maximum(jnp.max(s_w, axis=-1, keepdims=True), jnp.max(s_c, axis=-1, keepdims=True)), sk)
    pw, pc, ps = jnp.exp(s_w - m), jnp.exp(s_c - m), jnp.exp(sk - m)
    return q, pw, pc, ps, jnp.sum(pw, axis=-1, keepdims=True) + jnp.sum(pc, axis=-1, keepdims=True) + ps


def _f_attn(qs, kws, vws, kcs, vcs, sink, bias):
    _, sks = _attn_setup(sink)
    outs = []
    for t in range(N_STACKS):
        h = _kv_of(t)
        _, pw, pc, _, den = _attn_weights(qs[t], kws[h], kcs[h], sks[t], bias)
        outs.append((_raw_dot("nn", pw, vws[h], False) + _raw_dot("nn", pc, vcs[h], False)) / den)
    return tuple(outs)


def _f_attn_bwd(qs, kws, vws, kcs, vcs, sink, bias, outs, douts):
    group, sks = _attn_setup(sink)
    head_id = lax.broadcasted_iota(jnp.int32, (1, ATT_HEADS), 1)
    dot = lambda mode, a, b: _raw_dot(mode, a, b, False)
    dqs, dkws, dvws, dkcs, dvcs, dsink = [], [], [], [], [], jnp.zeros((1, ATT_HEADS), F32)
    for t in range(N_STACKS):
        h = _kv_of(t)
        q, pw, pc, ps, den = _attn_weights(qs[t], kws[h], kcs[h], sks[t], bias)
        inv = 1.0 / den
        pw, pc = pw * inv, pc * inv
        dd = jnp.sum(douts[t] * outs[t], axis=-1, keepdims=True)
        dsw = pw * (dot("nt", douts[t], vws[h]) - dd)
        dsc = pc * (dot("nt", douts[t], vcs[h]) - dd)
        dqs.append((dot("nn", dsw, kws[h]) + dot("nn", dsc, kcs[h])) * ATT_SCALE)
        dkws.append(dot("tn", dsw, q))
        dkcs.append(dot("tn", dsc, q))
        dvws.append(dot("tn", pw, douts[t]))
        dvcs.append(dot("tn", pc, douts[t]))
        dsk = -(ps * inv) * dd
        for g in range(ATT_STACK):
            one = jnp.sum(jnp.where(group == g, dsk, 0.0), axis=0, keepdims=True)
            dsink = dsink + jnp.where(head_id == t * ATT_STACK + g, one, 0.0)
    return dqs, dkws, dvws, dkcs, dvcs, dsink


def _group_rows(ref, h):
    hs = lambda hq: slice(hq * HEAD_DIM, (hq + 1) * HEAD_DIM)
    return jnp.concatenate([ref[:, hs(h * ATT_STACK + g)].astype(F32) for g in range(ATT_STACK)], axis=0)


def _ungroup_rows(ref, h, val):
    for g in range(ATT_STACK):
        hq = h * ATT_STACK + g
        ref[:, hq * HEAD_DIM:(hq + 1) * HEAD_DIM] = val[g * BLOCK:(g + 1) * BLOCK].astype(ref.dtype)


def _attn_loads(n, q_ref, kp_ref, vp_ref, kc_ref, vc_ref):
    r0 = pl.multiple_of(n * BLOCK, BLOCK)
    hs = lambda h: slice(h * HEAD_DIM, (h + 1) * HEAD_DIM)
    qs = [_group_rows(q_ref, t) for t in range(N_STACKS)]
    kws = [kp_ref[pl.ds(r0, 3 * BLOCK), hs(h)].astype(F32) for h in range(ATT_KV_HEADS)]
    vws = [vp_ref[pl.ds(r0, 3 * BLOCK), hs(h)].astype(F32) for h in range(ATT_KV_HEADS)]
    kcs = [kc_ref[:, hs(h)].astype(F32) for h in range(ATT_KV_HEADS)]
    vcs = [vc_ref[:, hs(h)].astype(F32) for h in range(ATT_KV_HEADS)]
    return r0, hs, qs, kws, vws, kcs, vcs


def _attn_specs(s, c):
    full = lambda shape: pl.BlockSpec(shape, lambda n: (0, 0))
    return [pl.BlockSpec((BLOCK, QW), lambda n: (n, 0)), full((s + 2 * BLOCK, KVW)), full((s + 2 * BLOCK, KVW)),
            full((c, KVW)), full((c, KVW)), full((1, ATT_HEADS)), _attn_bias_spec(s)]


def _attn_fwd(q, kp, vp, kc, vc, sink):
    s, c = q.shape[0], kc.shape[0]

    def body(q_ref, kp_ref, vp_ref, kc_ref, vc_ref, sink_ref, bias_ref, o_ref):
        n = pl.program_id(0)
        _, hs, qs, kws, vws, kcs, vcs = _attn_loads(n, q_ref, kp_ref, vp_ref, kc_ref, vc_ref)
        outs = _f_attn(qs, kws, vws, kcs, vcs, sink_ref[...], bias_ref[0])
        for t in range(N_STACKS):
            _ungroup_rows(o_ref, t, outs[t])

    return pl.pallas_call(
        body, name="attn_fwd", grid=(s // BLOCK,), in_specs=_attn_specs(s, c),
        out_specs=pl.BlockSpec((BLOCK, QW), lambda n: (n, 0)), out_shape=jax.ShapeDtypeStruct((s, QW), BF16),
        compiler_params=_cp("parallel"),
    )(q, kp, vp, kc, vc, sink, _attn_bias(s))


def _attn_bwd(do, o, q, kp, vp, kc, vc, sink):
    s, c = q.shape[0], kc.shape[0]

    def body(do_ref, o_ref, q_ref, kp_ref, vp_ref, kc_ref, vc_ref, sink_ref, bias_ref, dq_ref, dkp_ref, dvp_ref,
             dkc_ref, dvc_ref, dsink_ref):
        n = pl.program_id(0)

        @pl.when(n == 0)
        def _():
            for r in (dkp_ref, dvp_ref, dkc_ref, dvc_ref, dsink_ref):
                r[...] = jnp.zeros_like(r)

        r0, hs, qs, kws, vws, kcs, vcs = _attn_loads(n, q_ref, kp_ref, vp_ref, kc_ref, vc_ref)
        heads = range(N_STACKS)
        dqs, dkws, dvws, dkcs, dvcs, dsink = _f_attn_bwd(
            qs, kws, vws, kcs, vcs, sink_ref[...], bias_ref[0], [_group_rows(o_ref, h) for h in heads],
            [_group_rows(do_ref, h) for h in heads])
        for t in heads:
            _ungroup_rows(dq_ref, t, dqs[t])
            kv = hs(_kv_of(t))
            dkp_ref[pl.ds(r0, 3 * BLOCK), kv] += dkws[t]
            dvp_ref[pl.ds(r0, 3 * BLOCK), kv] += dvws[t]
            dkc_ref[:, kv] += dkcs[t]
            dvc_ref[:, kv] += dvcs[t]
        dsink_ref[...] += dsink

    full = lambda shape: pl.BlockSpec(shape, lambda n: (0, 0))
    return pl.pallas_call(
        body, name="attn_bwd", grid=(s // BLOCK,),
        in_specs=[pl.BlockSpec((BLOCK, QW), lambda n: (n, 0))] * 2 + _attn_specs(s, c),
        out_specs=[pl.BlockSpec((BLOCK, QW), lambda n: (n, 0)), full((s + 2 * BLOCK, KVW)), full((s + 2 * BLOCK, KVW)),
                   full((c, KVW)), full((c, KVW)), full((1, ATT_HEADS))],
        out_shape=[jax.ShapeDtypeStruct((s, QW), BF16), jax.ShapeDtypeStruct((s + 2 * BLOCK, KVW), F32),
                   jax.ShapeDtypeStruct((s + 2 * BLOCK, KVW), F32), jax.ShapeDtypeStruct((c, KVW), F32),
                   jax.ShapeDtypeStruct((c, KVW), F32), jax.ShapeDtypeStruct((1, ATT_HEADS), F32)],
        compiler_params=_cp("arbitrary"),
    )(do, o, q, kp, vp, kc, vc, sink, _attn_bias(s))


GLA_GROUPS = 1
GLA_GROUP_HEADS = GLA_HEADS // GLA_GROUPS
GKG, GVG = GKW // GLA_GROUPS, GVW // GLA_GROUPS


def _gla_masks(heads=GLA_HEADS):
    hk = np.arange(heads * GLA_DK) // GLA_DK
    hv = np.arange(heads * GLA_DV) // GLA_DV
    head_k = (np.arange(heads)[:, None] == hk[None, :]).astype(np.float32)
    head_v = (np.arange(heads)[:, None] == hv[None, :]).astype(np.float32)
    bd_t = (hv[:, None] == hk[None, :]).astype(np.float32)
    return jnp.asarray(head_k), jnp.asarray(head_v), jnp.asarray(bd_t)


def _group_states(st):
    return jnp.stack([st[g * GVG:(g + 1) * GVG, g * GKG:(g + 1) * GKG] for g in range(GLA_GROUPS)])


def _ungroup_states(st):
    out = jnp.zeros((GVW, GKW), st.dtype)
    for g in range(GLA_GROUPS):
        out = out.at[g * GVG:(g + 1) * GVG, g * GKG:(g + 1) * GKG].set(st[g])
    return out


def _tri(n, rev, strict=False):
    i = lax.broadcasted_iota(jnp.int32, (n, n), 0)
    j = lax.broadcasted_iota(jnp.int32, (n, n), 1)
    if strict:
        keep = (j > i) if rev else (j < i)
    else:
        keep = (j >= i) if rev else (j <= i)
    return keep


def _f_gla_chunk(q, k, v, la, st, head_k, head_v, bd_t, rev):
    return _f_gla_carry(*_f_gla_intra(q, k, v, la, head_k, head_v, rev), v, st, bd_t)


def _f_gla_intra(q, k, v, la, head_k, head_v, rev):
    heads, kw, vw = head_k.shape[0], q.shape[1], v.shape[1]
    keep = _tri(GLA_CHUNK, rev)
    b = _nn_mask(keep.astype(F32), la)
    bl = jnp.sum(la, axis=0, keepdims=True)
    qd = q * (GLA_DK ** -0.5) * jnp.exp(b)
    ki = k * jnp.exp(-b)
    kd = k * jnp.exp(bl - b)
    q_heads = (qd[None, :, :] * head_k[:, None, :]).reshape(heads * GLA_CHUNK, kw)
    a_all = _nt(q_heads, ki).reshape(heads, GLA_CHUNK, GLA_CHUNK)
    a_all = jnp.where(keep[None, :, :], a_all, 0.0).reshape(heads * GLA_CHUNK, GLA_CHUNK)
    o_all = _nn(a_all, v).reshape(heads, GLA_CHUNK, vw)
    return jnp.sum(o_all * head_v[:, None, :], axis=0), qd, kd, bl


def _f_gla_carry(intra, qd, kd, bl, v, st, bd_t):
    return intra + _nt(qd, st), st * jnp.exp(bl) + bd_t * _tn(v, kd)


def _gla_specs(s, tb, order):
    return [pl.BlockSpec((tb, GKW), lambda i: (order(i), C_GQ // GKW)),
            pl.BlockSpec((tb, GKW), lambda i: (order(i), C_GK // GKW)),
            pl.BlockSpec((tb, GVW), lambda i: (order(i), C_GV // GVW)),
            pl.BlockSpec((tb, GKW), lambda i: (order(i), 0))]


GLA_BLOCK_CHUNKS = 4


def _gla_fwd(p, la_f, la_b, st_f0, st_b0):
    s = p.shape[0]
    tb = GLA_BLOCK_CHUNKS * GLA_CHUNK
    nblk = s // tb
    up, down = (lambda i: i), (lambda i: nblk - 1 - i)
    masks = _gla_masks(GLA_GROUP_HEADS)

    def scan(rev, q_ref, k_ref, v_ref, la_ref, o_ref, sts_ref, st_ref, consts):
        for g in range(GLA_GROUPS):
            gk, gv = slice(g * GKG, (g + 1) * GKG), slice(g * GVG, (g + 1) * GVG)
            st = st_ref[g]
            sts_ref[0, g] = st
            chunks = range(GLA_BLOCK_CHUNKS)
            for ci in (reversed(chunks) if rev else chunks):
                rows = slice(ci * GLA_CHUNK, (ci + 1) * GLA_CHUNK)
                o, st = _f_gla_chunk(q_ref[rows, gk], k_ref[rows, gk], v_ref[rows, gv], la_ref[rows, gk], st, *consts,
                                     rev)
                o_ref[rows, gv] = o
            st_ref[g] = st

    def body(qf, kf, vf, laf, qb, kb, vb, lab, stf0, stb0, hk_ref, hv_ref, bd_ref, of_ref, stsf_ref, ob_ref, stsb_ref,
             stf_ref, stb_ref):
        @pl.when(pl.program_id(0) == 0)
        def _():
            stf_ref[...] = stf0[...]
            stb_ref[...] = stb0[...]

        consts = (hk_ref[...], hv_ref[...], bd_ref[...])
        scan(False, qf, kf, vf, laf, of_ref, stsf_ref, stf_ref, consts)
        scan(True, qb, kb, vb, lab, ob_ref, stsb_ref, stb_ref, consts)

    full = lambda a: pl.BlockSpec(a.shape, lambda i: (0,) * a.ndim)
    outs = lambda order: [pl.BlockSpec((tb, GVW), lambda i: (order(i), 0)),
                          pl.BlockSpec((1, GLA_GROUPS, GVG, GKG), lambda i: (order(i), 0, 0, 0))]
    return pl.pallas_call(
        body, name="gla_fwd", grid=(nblk,),
        in_specs=_gla_specs(s, tb, up) + _gla_specs(s, tb, down) + [full(st_f0), full(st_b0)]
        + [full(m) for m in masks],
        out_specs=outs(up) + outs(down),
        out_shape=[jax.ShapeDtypeStruct((s, GVW), F32), jax.ShapeDtypeStruct((nblk, GLA_GROUPS, GVG, GKG), F32)] * 2,
        scratch_shapes=[pltpu.VMEM((GLA_GROUPS, GVG, GKG), F32)] * 2,
        compiler_params=_cp("arbitrary"),
    )(p, p, p, la_f, p, p, p, la_b, st_f0, st_b0, *masks)


def _gla_bwd(p, la_f, la_b, sts_f, sts_b, do, after=None):
    s = p.shape[0]
    tb = GLA_BLOCK_CHUNKS * GLA_CHUNK
    nblk = s // tb
    up, down = (lambda i: i), (lambda i: nblk - 1 - i)
    masks = _gla_masks(GLA_GROUP_HEADS)
    follow = () if after is None else (after,)

    def back(rev, q_ref, k_ref, v_ref, la_ref, sts_ref, do_ref, dq_ref, dk_ref, dv_ref, dla_ref, dst0_ref, dst_ref,
             consts):
        def block(q, k, v, la, st):
            outs = [None] * GLA_BLOCK_CHUNKS
            chunks = range(GLA_BLOCK_CHUNKS)
            for ci in (reversed(chunks) if rev else chunks):
                outs[ci], st = _f_gla_chunk(q[ci], k[ci], v[ci], la[ci], st, *consts, rev)
            return tuple(outs), st

        for g in range(GLA_GROUPS):
            gk, gv = slice(g * GKG, (g + 1) * GKG), slice(g * GVG, (g + 1) * GVG)
            split = lambda r, cols: tuple(r[ci * GLA_CHUNK:(ci + 1) * GLA_CHUNK, cols].astype(F32)
                                          for ci in range(GLA_BLOCK_CHUNKS))
            _, vjp = jax.vjp(block, split(q_ref, gk), split(k_ref, gk), split(v_ref, gv), split(la_ref, gk),
                             sts_ref[0, g])
            dq, dk, dv, dla, dst = vjp((split(do_ref, gv), dst_ref[g]))
            for ci in range(GLA_BLOCK_CHUNKS):
                rows = slice(ci * GLA_CHUNK, (ci + 1) * GLA_CHUNK)
                dq_ref[rows, gk], dk_ref[rows, gk] = dq[ci].astype(BF16), dk[ci].astype(BF16)
                dv_ref[rows, gv], dla_ref[rows, gk] = dv[ci].astype(BF16), dla[ci]
            dst_ref[g] = dst
            dst0_ref[g] = dst

    def body(*refs):
        ins, (hk_ref, hv_ref, bd_ref) = refs[:12], refs[12:15]
        outs = refs[15 + len(follow):]

        @pl.when(pl.program_id(0) == 0)
        def _():
            outs[10][...] = jnp.zeros_like(outs[10])
            outs[11][...] = jnp.zeros_like(outs[11])

        consts = (hk_ref[...], hv_ref[...], bd_ref[...])
        back(False, *ins[:6], *outs[:5], outs[10], consts)
        back(True, *ins[6:], *outs[5:10], outs[11], consts)

    full = lambda a: pl.BlockSpec(a.shape, lambda i: (0,) * a.ndim)

    def ins(order):
        return _gla_specs(s, tb, order) + [pl.BlockSpec((1, GLA_GROUPS, GVG, GKG), lambda i: (order(i), 0, 0, 0)),
                                           pl.BlockSpec((tb, GVW), lambda i: (order(i), 0))]

    def outs(order):
        blk = lambda w: pl.BlockSpec((tb, w), lambda i: (order(i), 0))
        return [blk(GKW), blk(GKW), blk(GVW), blk(GKW), pl.BlockSpec((GLA_GROUPS, GVG, GKG), lambda i: (0, 0, 0))]

    shapes = [jax.ShapeDtypeStruct((s, GKW), BF16), jax.ShapeDtypeStruct((s, GKW), BF16),
              jax.ShapeDtypeStruct((s, GVW), BF16), jax.ShapeDtypeStruct((s, GKW), F32),
              jax.ShapeDtypeStruct((GLA_GROUPS, GVG, GKG), F32)]
    both = pl.pallas_call(
        body, name="gla_bwd", grid=(nblk,),
        in_specs=ins(down) + ins(up) + [full(m) for m in masks] + [pl.BlockSpec(memory_space=pl.ANY)] * len(follow),
        out_specs=outs(down) + outs(up), out_shape=shapes * 2,
        scratch_shapes=[pltpu.VMEM((GLA_GROUPS, GVG, GKG), F32)] * 2,
        compiler_params=_cp("arbitrary"),
    )(p, p, p, la_f, sts_f, do, p, p, p, la_b, sts_b, do, *masks, *follow)
    return both[:5], both[5:]


def _f_ctx_state(k, v, la_f, la_b, bd_t):
    c = k.shape[0]
    after = _nn_mask(_tri(c, True, strict=True).astype(F32), la_f)
    before = _nn_mask(_tri(c, False, strict=True).astype(F32), la_b)
    return bd_t * _tn(v, k * jnp.exp(after)), bd_t * _tn(v, k * jnp.exp(before))


def _ctx_state(pc, la_f, la_b):
    c = pc.shape[0]
    bd_t = _gla_masks()[2]

    def body(k_ref, v_ref, lf_ref, lb_ref, bd_ref, sf_ref, sb_ref):
        sf_ref[...], sb_ref[...] = _f_ctx_state(k_ref[...], v_ref[...], lf_ref[...], lb_ref[...], bd_ref[...])

    full = lambda a: pl.BlockSpec(a.shape, lambda i: (0, 0))
    return pl.pallas_call(
        body, name="ctx_state_fwd", grid=(1,),
        in_specs=[pl.BlockSpec((c, GKW), lambda i: (0, C_GK // GKW)), pl.BlockSpec((c, GVW), lambda i: (0, C_GV // GVW)),
                  full(la_f), full(la_b), full(bd_t)],
        out_specs=[pl.BlockSpec((GVW, GKW), lambda i: (0, 0))] * 2,
        out_shape=[jax.ShapeDtypeStruct((GVW, GKW), F32)] * 2,
        compiler_params=_cp("arbitrary"),
    )(pc, pc, la_f, la_b, bd_t)


def _ctx_state_bwd(pc, la_f, la_b, dsf, dsb):
    c = pc.shape[0]
    bd_t = _gla_masks()[2]

    def body(k_ref, v_ref, lf_ref, lb_ref, bd_ref, dsf_ref, dsb_ref, dk_ref, dv_ref, dlf_ref, dlb_ref):
        _, vjp = jax.vjp(lambda k, v, lf, lb: _f_ctx_state(k, v, lf, lb, bd_ref[...]),
                         k_ref[...], v_ref[...], lf_ref[...], lb_ref[...])
        dk, dv, dlf, dlb = vjp((dsf_ref[...], dsb_ref[...]))
        dk_ref[...], dv_ref[...] = dk.astype(BF16), dv.astype(BF16)
        dlf_ref[...], dlb_ref[...] = dlf, dlb

    full = lambda a: pl.BlockSpec(a.shape, lambda i: (0, 0))
    return pl.pallas_call(
        body, name="ctx_state_bwd", grid=(1,),
        in_specs=[pl.BlockSpec((c, GKW), lambda i: (0, C_GK // GKW)), pl.BlockSpec((c, GVW), lambda i: (0, C_GV // GVW)),
                  full(la_f), full(la_b), full(bd_t), full(dsf), full(dsb)],
        out_specs=[pl.BlockSpec((c, GKW), lambda i: (0, 0)), pl.BlockSpec((c, GVW), lambda i: (0, 0)),
                   pl.BlockSpec((c, GKW), lambda i: (0, 0)), pl.BlockSpec((c, GKW), lambda i: (0, 0))],
        out_shape=[jax.ShapeDtypeStruct((c, GKW), BF16), jax.ShapeDtypeStruct((c, GVW), BF16),
                   jax.ShapeDtypeStruct((c, GKW), F32), jax.ShapeDtypeStruct((c, GKW), F32)],
        compiler_params=_cp("arbitrary"),
    )(pc, pc, la_f, la_b, bd_t, dsf, dsb)


_SRC_COLS = ((0, QW), (QW + 2 * KVW + 2 * GKW, GVW), (QW + 2 * KVW + 2 * GKW + GVW, GVW), (QW, KVW), (QW + KVW, KVW),
             (QW + 2 * KVW, GKW), (QW + 2 * KVW + GKW, GKW), (IN_COLS - 2 * GATE_RANK, 2 * GATE_RANK))
_DST_COLS = (C_Q, C_GV, C_GG, C_K, C_V, C_GQ, C_GK, C_Z)


def _pack_w_in(w_in):
    parts = [w_in[:, s:s + n] for s, n in _SRC_COLS]
    parts.append(jnp.zeros((w_in.shape[0], IN_PAD - C_Z - 2 * GATE_RANK), w_in.dtype))
    return jnp.concatenate(parts, axis=1)


def _unpack_w_in_grad(g):
    by_src = sorted(zip(_SRC_COLS, _DST_COLS))
    return jnp.concatenate([g[:, d:d + n] for (_, n), d in by_src], axis=1)


def _prep_gate_weights(w_gate_fwd, w_gate_bwd):
    pad_rows = lambda w, at: jnp.zeros((LANES, GKW), F32).at[at:at + GATE_RANK].set(w)
    return {"wg_f": pad_rows(w_gate_fwd, 0), "wg_b": pad_rows(w_gate_bwd, GATE_RANK)}


def _local_step(x, ctx, target, ada, ada_c, w, late_weights, reduce_behind=None, reduce_w_in=None):
    s, d = x.shape
    sh1, sc1, gt1, sh2, sc2, gt2 = [ada[:, i * d:(i + 1) * d] for i in range(6)]
    sh1c, sc1c = ada_c[:, :d], ada_c[:, d:2 * d]
    cos, sin = _rope_tables(s)
    gt = jnp.tile(w["g_gla_norm"], (1, GLA_HEADS))

    h = _norm_mod("pre_mix", x, w["g_pre_mix"], sh1, sc1)
    hc = _norm_mod("pre_mix_ctx", ctx, w["g_pre_mix"], sh1c, sc1c)
    w_in, token = w["w_in"](h, cos, sin)
    p = _mm("proj_in", h, w_in, "nn", after=token)
    pc = _mm("proj_in_ctx", hc, w_in, "nn")
    q_rot, k_rot, v_b = _rope_fwd("rope", p, cos, sin)
    pad = ((BLOCK, BLOCK), (0, 0))
    kp, vp = jnp.pad(k_rot, pad), jnp.pad(v_b, pad)
    kc, vc = pc[:, C_K:C_K + KVW].astype(BF16), pc[:, C_V:C_V + KVW].astype(BF16)
    attn = _attn_fwd(q_rot, kp, vp, kc, vc, w["attn_sink"])
    gate_w = (w["wg_f"], w["wg_b"], w["b_gate_fwd"], w["b_gate_bwd"])
    la_f, la_b = _gate_fwd("gate", p, *gate_w)
    la_fc, la_bc = _gate_fwd("gate_ctx", pc, *gate_w)
    st_f0, st_b0 = _ctx_state(pc, la_fc, la_bc)
    o_f, sts_f, o_b, sts_b = _gla_fwd(p, la_f, la_b, _group_states(st_f0), _group_states(st_b0))
    mix = _gla_out("gla_out", attn, o_f, o_b, p, gt)
    w_out, w_ffn_in_t, w_ffn_out = late_weights(mix)
    y = _mm("proj_out", mix, w_out, "nn", BF16)
    x1, h2 = _post_res_norm_mod("post_mix_pre_ffn", x, y, w["g_post_mix"], gt1, w["g_pre_ffn"], sh2, sc2)
    u, a = _ffn_in_swiglu("ffn_in", h2, w_ffn_in_t)
    f = _mm("ffn_out", a, w_ffn_out, "nn", BF16)
    g = {}
    dx2, df, loss, g["g_post_ffn"], dgt2 = _post_res_loss("post_ffn_loss", x1, f, w["g_post_ffn"], gt2, target)

    late_rows = {"w_ffn_in_t": w_ffn_in_t.shape[0] // N_CHIP, "w_ffn_out": w_ffn_out.shape[0] // N_CHIP,
                 "w_out": w_out.shape[0] // N_CHIP}
    order = sorted(late_rows, key=lambda n: -late_rows[n])
    offsets, slab_rows = _slab_layout([late_rows[n] for n in order])
    late_at, slab_shape = dict(zip(order, offsets)), (N_CHIP, slab_rows, d)
    slab = _slab_zero_gaps("late_grads_gaps", slab_shape, [late_rows[n] for n in order], offsets)
    slab = _dw_into_slab("ffn_out_dw", a, df, slab, slab_shape, late_at["w_ffn_out"])
    du = _ffn_out_dx_swiglu_bwd("ffn_out_dx", df, w_ffn_out, u)
    dh2 = _mm("ffn_in_dx", du, w_ffn_in_t, "nn", BF16)
    slab = _dw_into_slab("ffn_in_dw", du, h2, slab, slab_shape, late_at["w_ffn_in_t"])
    dx1, dy, g["g_pre_ffn"], dsh2, dsc2, g["g_post_mix"], dgt1 = _norm_mod_post_res_bwd(
        "pre_ffn_post_mix_bwd", dh2, dx2, x1, y, w["g_pre_ffn"], sh2, sc2, w["g_post_mix"], gt1)
    dmix = _mm("proj_out_dx", dy, w_out, "nt", BF16)
    slab = _dw_into_slab("proj_out_dw", mix, dy, slab, slab_shape, late_at["w_out"])
    g["late"], g["late_at"], g["late_rows"] = slab, late_at, late_rows
    rb, sink, token = reduce_behind, w["attn_sink"], None
    if rb is not None:
        gt = _behind(gt, rb.start_slab(slab))
    d_o, dgg, dgt = _gla_out_bwd("gla_out_bwd", dmix, o_f, o_b, p, gt)
    g["g_gla_norm"] = jnp.sum(dgt.reshape(GLA_HEADS, GLA_DV), axis=0, keepdims=True)
    if rb is not None:
        token = rb.pair(dgg)
    gla_f, gla_b = _gla_bwd(p, la_f, la_b, sts_f, sts_b, d_o, token)
    (dla_f, dst_f0), (dla_b, dst_b0) = gla_f[3:], gla_b[3:]
    dst_f0, dst_b0 = _ungroup_states(dst_f0), _ungroup_states(dst_b0)
    if rb is not None:
        sink = _behind(sink, rb.total(dla_b))
    dgkc, dgvc, dla_fc, dla_bc = _ctx_state_bwd(pc, la_fc, la_bc, dst_f0, dst_b0)
    dz, dwf, dwb, dbf, dbb = _gate_bwd("gate_bwd", p, dla_f, dla_b, *gate_w)
    dzc, dwfc, dwbc, dbfc, dbbc = _gate_bwd("gate_ctx_bwd", pc, dla_fc, dla_bc, *gate_w)
    g["w_gate_fwd"] = (dwf + dwfc)[:GATE_RANK]
    g["w_gate_bwd"] = (dwb + dwbc)[GATE_RANK:2 * GATE_RANK]
    g["b_gate_fwd"], g["b_gate_bwd"] = dbf + dbfc, dbb + dbbc
    dq_rot, dkp, dvp, dkc, dvc, g["attn_sink"] = _attn_bwd(dmix, attn, q_rot, kp, vp, kc, vc, sink)
    if rb is not None:
        g["late"] = rb.result(dq_rot)
    dp = _proj_grad("proj_grad", dq_rot, dkp[BLOCK:BLOCK + s], dvp[BLOCK:BLOCK + s], cos, sin, gla_f[:3], gla_b[:3],
                    dgg, dz)
    c_rows = ctx.shape[0]
    zeros = lambda n: jnp.zeros((c_rows, n), BF16)
    dpc = jnp.concatenate([zeros(QW), dgvc, zeros(GVW), dkc.astype(BF16), dvc.astype(BF16), zeros(GKW), dgkc, dzc],
                          axis=1)
    g["w_in"] = _mm("proj_in_dw", h, dp, "tn", init=_mm("proj_in_ctx_dw", hc, dpc, "tn"))
    token = None if reduce_w_in is None else reduce_w_in.start(g["w_in"])
    dh = _mm("proj_in_dx", dp, w_in, "nt", BF16, after=token)
    dhc = _mm("proj_in_ctx_dx", dpc, w_in, "nt")
    if reduce_w_in is not None:
        sh1 = _behind(sh1, reduce_w_in.pair(dh))
    dx, dg_a, dsh1, dsc1 = _norm_mod_bwd("pre_mix_bwd", dh, dx1, x, w["g_pre_mix"], sh1, sc1)
    if reduce_w_in is not None:
        dsh1 = _behind(dsh1, reduce_w_in.total(dx))
    _, dg_b, dsh1c, dsc1c = _norm_mod_bwd("pre_mix_ctx_bwd", dhc, jnp.zeros_like(dhc), ctx, w["g_pre_mix"], sh1c,
                                          sc1c)
    g["g_pre_mix"] = dg_a + dg_b
    d_ada = jnp.concatenate([dsh1, dsc1, dgt1, dsh2, dsc2, dgt2], axis=1)
    d_ada_c = jnp.concatenate([dsh1c, dsc1c, jnp.zeros((1, 4 * d), F32)], axis=1)
    return loss, dx, g, d_ada, d_ada_c


HBM = pl.BlockSpec(memory_space=pltpu.HBM)
N_DEV, N_CHIP = 8, 4


def _place():
    x, y, c = lax.axis_index("x"), lax.axis_index("y"), lax.axis_index("c")
    return x, y, c, [(1 - x, y), (x, 1 - y), (1 - x, 1 - y)]


def _row_tile(n, mult, cap):
    return max(t for t in range(mult, min(n, cap) + 1, mult) if n % t == 0)


def _ag_small(name, v, after=None):
    follow = () if after is None else (after,)

    def body(v_ref, *rest):
        out_ref, send_sems, recv_sems = rest[len(follow):]
        x, y, c, _ = _place()
        out_ref[4 * x + 2 * y + c] = v_ref[...]

        def peer(r):
            return ((1 - x) if r & 4 else x, (1 - y) if r & 2 else y, (1 - c) if r & 1 else c)

        def copy(r, block):
            px, py, pc = block
            return pltpu.make_async_remote_copy(
                src_ref=v_ref, dst_ref=out_ref.at[4 * px + 2 * py + pc], send_sem=send_sems.at[r - 1],
                recv_sem=recv_sems.at[r - 1], device_id=peer(r), device_id_type=MESH)

        sends = [copy(r, (x, y, c)) for r in range(1, N_DEV)]
        for cp in sends:
            cp.start()
        for r in range(1, N_DEV):
            copy(r, peer(r)).wait_recv()
        for cp in sends:
            cp.wait_send()

    return pl.pallas_call(
        body, name=name, out_shape=jax.ShapeDtypeStruct((N_DEV,) + v.shape, v.dtype),
        in_specs=[pl.BlockSpec(memory_space=pltpu.VMEM)] + [pl.BlockSpec(memory_space=pl.ANY)] * len(follow),
        out_specs=pl.BlockSpec(memory_space=pltpu.VMEM),
        scratch_shapes=[pltpu.SemaphoreType.DMA((N_DEV - 1,)), pltpu.SemaphoreType.DMA((N_DEV - 1,))],
    )(v, *follow)


def _halves(c, rows, mult):
    hr = rows // 2
    return pl.ds(pl.multiple_of(c * hr, mult), hr), pl.ds(pl.multiple_of((1 - c) * hr, mult), hr)


def _add_half(name, g, a, c_idx):
    n_sh, hr, n = a.shape
    tr = _row_tile(hr, 16, 1024)
    nb = hr // tr

    def body(c_ref, g_ref, a_ref, o_ref):
        o_ref[...] = (g_ref[...] + a_ref[...]).astype(o_ref.dtype)

    return pl.pallas_call(
        body, name=name, out_shape=jax.ShapeDtypeStruct(a.shape, BF16),
        grid_spec=pltpu.PrefetchScalarGridSpec(
            num_scalar_prefetch=1, grid=(n_sh, nb),
            in_specs=[pl.BlockSpec((1, tr, n), lambda s, i, c_ref: (s, c_ref[0] * nb + i, 0)),
                      pl.BlockSpec((1, tr, n), lambda s, i, c_ref: (s, i, 0))],
            out_specs=pl.BlockSpec((1, tr, n), lambda s, i, c_ref: (s, i, 0))),
        compiler_params=_cp("parallel", "parallel"),
    )(c_idx, g, a)


def _sum_chips(name, b, c_idx):
    n_sh, hr, n = b.shape
    tr = _row_tile(hr, 16, 1024)
    nb = hr // tr

    def body(c_ref, b0, b1, b2, b3, o_ref):
        o_ref[...] = ((b0[0].astype(F32) + b1[0].astype(F32)) + b2[0].astype(F32)) + b3[0].astype(F32)

    return pl.pallas_call(
        body, name=name, out_shape=jax.ShapeDtypeStruct((2 * hr, n), F32),
        grid_spec=pltpu.PrefetchScalarGridSpec(
            num_scalar_prefetch=1, grid=(nb,),
            in_specs=[pl.BlockSpec((1, tr, n), functools.partial(lambda i, c_ref, k: (k, i, 0), k=k))
                      for k in range(n_sh)],
            out_specs=pl.BlockSpec((tr, n), lambda i, c_ref: (c_ref[0] * nb + i, 0))),
        compiler_params=_cp("parallel"),
    )(c_idx, b, b, b, b)


SEM = pl.BlockSpec(memory_space=pltpu.SEMAPHORE)
ANY = pl.BlockSpec(memory_space=pl.ANY)
DATAFLOW = pltpu.SideEffectType.DATAFLOW_SIDE_EFFECTING


def _remote(src, dst, send_sems, recv_sems, k, to):
    return pltpu.make_async_remote_copy(src_ref=src, dst_ref=dst, send_sem=send_sems.at[k], recv_sem=recv_sems.at[k],
                                        device_id=to, device_id_type=MESH)


def _split_copy(name, src, land_shape, land_dtype, n, plan, after=None):
    after = jnp.zeros((8, LANES), F32) if after is None else after

    def start_body(src_ref, land_ref, after_ref, send_sems, recv_sems, src_thru, land_thru, token):
        for cp in plan(src_ref, land_ref, send_sems, recv_sems)[0]:
            cp.start()
        token[...] = jnp.zeros_like(token)

    sems = pltpu.SemaphoreType.DMA((n,))
    send_sems, recv_sems, src_thru, land_thru, token = pl.pallas_call(
        start_body, name=name + "_start",
        out_shape=(sems, sems, pltpu.HBM(src.shape, src.dtype), pltpu.HBM(land_shape, land_dtype),
                   jax.ShapeDtypeStruct((8, LANES), F32)),
        in_specs=(HBM, HBM, ANY), out_specs=(SEM, SEM, HBM, HBM, pl.BlockSpec(memory_space=pltpu.VMEM)),
        input_output_aliases={0: 2, 1: 3}, compiler_params=pltpu.CompilerParams(has_side_effects=DATAFLOW),
    )(pltpu.with_memory_space_constraint(src, pltpu.HBM),
      pltpu.with_memory_space_constraint(lax.empty(land_shape, land_dtype), pltpu.HBM), after)

    def wait(*after):
        def wait_body(src_ref, land_ref, send_sems, recv_sems, *rest):
            sent, received = plan(src_ref, land_ref, send_sems, recv_sems)
            for cp in sent:
                cp.wait_send()
            for cp in received:
                cp.wait_recv()

        return pl.pallas_call(
            wait_body, name=name + "_wait",
            out_shape=(pltpu.HBM(src.shape, src.dtype), pltpu.HBM(land_shape, land_dtype)),
            in_specs=(HBM, HBM, SEM, SEM) + (ANY,) * len(after), out_specs=(HBM, HBM),
            input_output_aliases={0: 0, 1: 1}, compiler_params=pltpu.CompilerParams(has_side_effects=DATAFLOW),
        )(src_thru, land_thru, send_sems, recv_sems, *after)

    return token, wait


def _split_gather(name, shards, after):
    k, n, plan = len(shards), 3 * len(shards), _plan_gather

    def start_body(*refs):
        for cp in plan(refs[:k], refs[k:2 * k], refs[2 * k + 1], refs[2 * k + 2])[0]:
            cp.start()
        refs[-1][...] = jnp.zeros_like(refs[-1])

    sems = pltpu.SemaphoreType.DMA((n,))
    bufs = [pltpu.HBM(s.shape, s.dtype) for s in shards] + [pltpu.HBM((N_CHIP,) + s.shape, s.dtype) for s in shards]
    hbm = lambda t: pltpu.with_memory_space_constraint(t, pltpu.HBM)
    outs = pl.pallas_call(
        start_body, name=name + "_start", out_shape=(sems, sems, *bufs, jax.ShapeDtypeStruct((8, LANES), F32)),
        in_specs=(HBM,) * (2 * k) + (ANY,),
        out_specs=(SEM, SEM) + (HBM,) * (2 * k) + (pl.BlockSpec(memory_space=pltpu.VMEM),),
        input_output_aliases={i: 2 + i for i in range(2 * k)},
        compiler_params=pltpu.CompilerParams(has_side_effects=DATAFLOW),
    )(*[hbm(s) for s in shards], *[hbm(lax.empty((N_CHIP,) + s.shape, s.dtype)) for s in shards], after)
    send_sems, recv_sems, thru, token = outs[0], outs[1], outs[2:2 + 2 * k], outs[-1]

    def wait(*after):
        def wait_body(*refs):
            sent, received = plan(refs[:k], refs[k:2 * k], refs[2 * k], refs[2 * k + 1])
            for cp in sent:
                cp.wait_send()
            for cp in received:
                cp.wait_recv()

        res = pl.pallas_call(
            wait_body, name=name + "_wait", out_shape=tuple(bufs),
            in_specs=(HBM,) * (2 * k) + (SEM, SEM) + (ANY,) * len(after), out_specs=(HBM,) * (2 * k),
            input_output_aliases={i: i for i in range(2 * k)},
            compiler_params=pltpu.CompilerParams(has_side_effects=DATAFLOW),
        )(*thru, send_sems, recv_sems, *after)
        return res[:k], res[k:]

    return token, wait


def _behind(x, token):
    return x + token[0, 0]


def _plan_gather(src_refs, land_refs, send_sems, recv_sems):
    x, y, c, chips = _place()
    pairs = list(enumerate(zip(src_refs, land_refs)))
    sent = [_remote(s, l.at[2 * x + y], send_sems, recv_sems, 3 * i + j, (px, py, c))
            for i, (s, l) in pairs for j, (px, py) in enumerate(chips)]
    received = [_remote(s, l.at[2 * px + py], send_sems, recv_sems, 3 * i + j, (px, py, c))
                for i, (s, l) in pairs for j, (px, py) in enumerate(chips)]
    return sent, received


def _plan_swap(src_ref, land_ref, send_sems, recv_sems):
    x, y, c, _ = _place()
    _, other_half = _halves(c, src_ref.shape[1], 8)
    cp = _remote(src_ref.at[pl.ds(0, src_ref.shape[0]), other_half], land_ref, send_sems, recv_sems, 0, (x, y, 1 - c))
    return [cp], [cp]


def _plan_scatter(src_ref, land_ref, send_sems, recv_sems):
    x, y, c, chips = _place()
    sent = [_remote(src_ref.at[2 * px + py], land_ref.at[2 * x + y], send_sems, recv_sems, j, (px, py, c))
            for j, (px, py) in enumerate(chips)]
    received = [_remote(src_ref.at[2 * px + py], land_ref.at[2 * px + py], send_sems, recv_sems, j, (px, py, c))
                for j, (px, py) in enumerate(chips)]
    return sent, received


def _plan_share(src_ref, land_ref, send_sems, recv_sems):
    x, y, c, _ = _place()
    mine_half, other_half = _halves(c, src_ref.shape[0], 8)
    return ([_remote(src_ref.at[mine_half], src_ref.at[mine_half], send_sems, recv_sems, 0, (x, y, 1 - c))],
            [_remote(src_ref.at[other_half], src_ref.at[other_half], send_sems, recv_sems, 0, (x, y, 1 - c))])


class _GatherBehind:
    def __init__(self, name, shards, chip, after):
        self.chip = chip
        self.token, self.wait = _split_gather(name, shards, after)

    def result(self, *after):
        shards, lands = self.wait(*after)
        return [lax.dynamic_update_slice(land, shard[None], (self.chip, 0, 0)) for shard, land in zip(shards, lands)]


class _ReduceBehind:
    def __init__(self, name, chip, c_idx):
        self.name, self.chip, self.c_idx = name, chip, c_idx

    def start_slab(self, g):
        n_sh, rows, n = g.shape
        token, self.wait = _split_copy(self.name + "_swap", g, (n_sh, rows // 2, n), g.dtype, 1, _plan_swap)
        return token

    def pair(self, after):
        g, a = self.wait(after)
        h = _add_half(self.name + "_pair", g, a, self.c_idx)
        token, self.wait = _split_copy(self.name + "_scatter", h, h.shape, h.dtype, 3, _plan_scatter)
        return token

    def total(self, after):
        h, b = self.wait(after)
        b = lax.dynamic_update_slice(b, lax.dynamic_slice_in_dim(h, self.chip, 1, axis=0), (self.chip, 0, 0))
        f = _sum_chips(self.name + "_sum", b, self.c_idx)
        token, self.wait = _split_copy(self.name + "_share", f, (8, LANES), f.dtype, 1, _plan_share)
        return token

    def result(self, after):
        return self.wait(after)[0]


class _ReduceColsBehind(_ReduceBehind):
    def start(self, g_padded):
        g = _unpack_w_in_grad(g_padded)
        n = g.shape[1] // N_CHIP
        return self.start_slab(jnp.stack([g[:, k * n:(k + 1) * n] for k in range(N_CHIP)]))


def _f_adamw(w, g, m, v):
    m = ADAM_B1 * m + (1.0 - ADAM_B1) * g
    v = ADAM_B2 * v + (1.0 - ADAM_B2) * (g * g)
    m_hat = m / (1.0 - ADAM_B1 ** ADAM_STEP)
    v_hat = v / (1.0 - ADAM_B2 ** ADAM_STEP)
    return -ADAM_LR * (m_hat / (jnp.sqrt(v_hat) + ADAM_EPS) + ADAM_WD * w), m, v


def _adamw(name, w, g, m, v):
    rows, n = w.shape
    return _rowwise(name, lambda w, g, m, v: (_f_adamw(w, g, m, v), ()), rows, [(t, n, 0) for t in (w, g, m, v)], [],
                    [(n, F32)] * 3, [], tm=_row_tile(rows, 8, 256))


def _adamw_many(name, ws, gs, ms, vs):
    k = len(ws)

    def body(*refs):
        ins, outs = refs[:4 * k], refs[4 * k:]
        for i in range(k):
            res = _f_adamw(ins[i][...], ins[k + i][...], ins[2 * k + i][...], ins[3 * k + i][...])
            for j in range(3):
                outs[j * k + i][...] = res[j]

    out = pl.pallas_call(body, name=name, out_shape=[jax.ShapeDtypeStruct(w.shape, F32) for w in ws] * 3)(
        *ws, *gs, *ms, *vs)
    return out[:k], out[k:2 * k], out[2 * k:]


def _pack_rows(parts):
    rows = []
    for t in parts:
        t = t.reshape(-1)
        rows.append(jnp.pad(t, (0, -t.shape[0] % LANES)).reshape(-1, LANES))
    out = jnp.concatenate(rows, axis=0)
    return jnp.pad(out, ((0, -out.shape[0] % 8), (0, 0)))


def _unpack_rows(packed, shapes):
    out, r = [], 0
    for shp in shapes:
        n = int(np.prod(shp))
        nr = -(-n // LANES)
        out.append(packed[r:r + nr].reshape(-1)[:n].reshape(shp))
        r += nr
    return out


def _sum_blocks(name, g):
    def body(g_ref, o_ref):
        acc = g_ref[0]
        for k in range(1, g.shape[0]):
            acc = acc + g_ref[k]
        o_ref[...] = acc

    return pl.pallas_call(body, name=name, out_shape=jax.ShapeDtypeStruct(g.shape[1:], F32))(g)


def _silu(t):
    return t * _sigmoid(t)


def _ada_fwd(cc, w_ada):
    n = w_ada.shape[1]
    tn = _row_tile(n, LANES, 512)

    def body(cc_ref, w_ref, o_ref):
        o_ref[...] = _nn(_silu(cc_ref[...]), w_ref[...])

    return pl.pallas_call(
        body, name="ada_fwd", grid=(n // tn,), out_shape=jax.ShapeDtypeStruct((cc.shape[0], n), F32),
        in_specs=[pl.BlockSpec(cc.shape, lambda j: (0, 0)), pl.BlockSpec((w_ada.shape[0], tn), lambda j: (0, j))],
        out_specs=pl.BlockSpec((cc.shape[0], tn), lambda j: (0, j)), compiler_params=_cp("parallel"),
    )(cc, w_ada)


def _ada_bwd(cc, dm, w_ada):
    d, n = w_ada.shape
    tn = _row_tile(n, LANES, 512)

    def body(cc_ref, dm_ref, w_ref, gw_ref, ds_ref):
        @pl.when(pl.program_id(0) == 0)
        def _():
            ds_ref[...] = jnp.zeros_like(ds_ref)

        gw_ref[...] = _raw_dot("tn", _silu(cc_ref[...]), dm_ref[...], True)
        ds_ref[...] += _raw_dot("nt", dm_ref[...], w_ref[...], False)

    return pl.pallas_call(
        body, name="ada_bwd", grid=(n // tn,),
        out_shape=[jax.ShapeDtypeStruct((d, n), F32), jax.ShapeDtypeStruct(cc.shape, F32)],
        in_specs=[pl.BlockSpec(cc.shape, lambda j: (0, 0)), pl.BlockSpec((cc.shape[0], tn), lambda j: (0, j)),
                  pl.BlockSpec((d, tn), lambda j: (0, j))],
        out_specs=[pl.BlockSpec((d, tn), lambda j: (0, j)), pl.BlockSpec(cc.shape, lambda j: (0, 0))],
        compiler_params=_cp("arbitrary"),
    )(cc, dm, w_ada)


def _c_ctx_grad(parts, c_ctx):
    def body(p_ref, c_ref, o_ref):
        ds = ((p_ref[0] + p_ref[1]) + p_ref[2]) + p_ref[3]
        _, vjp = jax.vjp(_silu, c_ref[...])
        o_ref[...] = vjp(ds)[0]

    return pl.pallas_call(body, name="c_ctx_grad", out_shape=jax.ShapeDtypeStruct(c_ctx.shape, F32))(parts, c_ctx)


def kernel(x, c, ctx, c_ctx, w_ada, b_ada, g_pre_mix, g_post_mix, g_pre_ffn, g_post_ffn, w_in, attn_sink, w_gate_fwd, b_gate_fwd, w_gate_bwd, b_gate_bwd, g_gla_norm, w_out, w_ffn_in, w_ffn_out, loss_target, m_c_ctx, m_w_ada, m_b_ada, m_g_pre_mix, m_g_post_mix, m_g_pre_ffn, m_g_post_ffn, m_w_in, m_attn_sink, m_w_gate_fwd, m_b_gate_fwd, m_w_gate_bwd, m_b_gate_bwd, m_g_gla_norm, m_w_out, m_w_ffn_in, m_w_ffn_out, v_c_ctx, v_w_ada, v_b_ada, v_g_pre_mix, v_g_post_mix, v_g_pre_ffn, v_g_post_ffn, v_w_in, v_attn_sink, v_w_gate_fwd, v_b_gate_fwd, v_w_gate_bwd, v_b_gate_bwd, v_g_gla_norm, v_w_out, v_w_ffn_in, v_w_ffn_out):
    xi, yi, ci = lax.axis_index("x"), lax.axis_index("y"), lax.axis_index("c")
    dev, chip = 4 * xi + 2 * yi + ci, 2 * xi + yi
    c_idx = jnp.reshape(ci, (1,)).astype(jnp.int32)
    d = x.shape[-1]
    n_ada, n_in, n_f = w_ada.shape[-1], w_in.shape[-1], w_ffn_in.shape[-1]
    r_out, r_f = w_out.shape[1], w_ffn_out.shape[1]
    n_gate = w_gate_fwd.shape[-1]
    by_chip = lambda t: t[0::2]

    rc = -(-d // LANES)
    g1 = _ag_small("gather_cond", _pack_rows([c[0], w_gate_fwd[0], w_gate_bwd[0]]))
    c_all = g1[:, :rc].reshape(N_DEV, -1)[:, :d]
    gr = GATE_RANK * n_gate // LANES
    gate_full = lambda off: jnp.transpose(by_chip(g1)[:, off:off + gr].reshape(N_CHIP, GATE_RANK, n_gate),
                                          (1, 0, 2)).reshape(GATE_RANK, N_CHIP * n_gate)
    wgf, wgb = gate_full(rc), gate_full(rc + gr)
    cc = jnp.concatenate([c_all, c_ctx[None, :], jnp.zeros((7, d), F32)], axis=0)

    g2 = _ag_small("gather_ada", _ada_fwd(cc, w_ada[0]).reshape(-1, LANES))
    ada_all = jnp.transpose(by_chip(g2).reshape(N_CHIP, 16, n_ada), (1, 0, 2)).reshape(16, N_CHIP * n_ada) + b_ada
    first = _GatherBehind("gather_w_in", [w_in[0].astype(BF16)], chip, g2)
    late_shards = [w_out[0].astype(BF16), jnp.transpose(w_ffn_in[0]).astype(BF16), w_ffn_out[0].astype(BF16)]
    late = []

    def first_weights(*after):
        w_in_g, = first.result(*after, *late_shards)
        late.append(_GatherBehind("gather_late", late_shards, chip, w_in_g))
        return _pack_w_in(jnp.concatenate([w_in_g[k] for k in range(N_CHIP)], axis=1)), late[0].token

    def late_weights(after):
        return [t.reshape(-1, d) for t in late[0].result(after)]

    ada_all = _behind(ada_all, first.token)
    ada = lax.dynamic_slice(ada_all, (dev, 0), (1, N_CHIP * n_ada))
    ada_c = ada_all[N_DEV:N_DEV + 1]

    w = _prep_gate_weights(wgf, wgb)
    w.update(w_in=first_weights, g_pre_mix=g_pre_mix, g_post_mix=g_post_mix, g_pre_ffn=g_pre_ffn, g_post_ffn=g_post_ffn,
             attn_sink=attn_sink, b_gate_fwd=b_gate_fwd, b_gate_bwd=b_gate_bwd, g_gla_norm=g_gla_norm)

    reduce_behind = _ReduceBehind("reduce_late", chip, c_idx)
    reduce_w_in = _ReduceColsBehind("reduce_w_in", chip, c_idx)
    loss_lanes, grad_x, g, d_ada, d_ada_c = _local_step(x[0], ctx[0], loss_target[0], ada, ada_c, w, late_weights,
                                                        reduce_behind, reduce_w_in)

    small = ("g_pre_mix", "g_post_mix", "g_pre_ffn", "g_post_ffn", "attn_sink", "b_gate_fwd", "b_gate_bwd",
             "g_gla_norm", "w_gate_fwd", "w_gate_bwd")
    shapes = [(1, 6 * d)] * 2 + [g[n].shape for n in small] + [(1, LANES)]
    g3 = _ag_small("gather_small_grads", _pack_rows([d_ada, d_ada_c] + [g[n] for n in small] + [loss_lanes]))
    tot = dict(zip(("d_ada", "d_ada_c") + small + ("loss",),
                   _unpack_rows(_sum_blocks("sum_small_grads", g3), shapes)))
    r_ada = 6 * d // LANES
    dm = jnp.concatenate([g3[:, :r_ada].reshape(N_DEV, 6 * d), tot["d_ada_c"], jnp.zeros((7, 6 * d), F32)], axis=0)
    grads = {n: tot[n] for n in small[:8]}
    grads["b_ada"] = _sum_blocks("sum_b_ada", dm.reshape(16, r_ada, LANES)).reshape(1, 6 * d)
    grads["w_gate_fwd"] = lax.dynamic_slice(tot["w_gate_fwd"], (0, chip * n_gate), (GATE_RANK, n_gate))[None]
    grads["w_gate_bwd"] = lax.dynamic_slice(tot["w_gate_bwd"], (0, chip * n_gate), (GATE_RANK, n_gate))[None]
    gw_ada, dsc = _ada_bwd(cc, lax.dynamic_slice(dm, (0, chip * n_ada), (16, n_ada)), w_ada[0])
    grads["w_ada"] = gw_ada[None]
    g4 = _ag_small("gather_c_ctx", _pack_rows([dsc[N_DEV]]))
    grads["c_ctx"] = _c_ctx_grad(by_chip(g4), _pack_rows([c_ctx])).reshape(-1)[:d]

    grads["w_in"] = reduce_w_in.result(g4)[None]
    part = lambda n: g["late"][g["late_at"][n]:g["late_at"][n] + g["late_rows"][n]]
    grads["w_ffn_in"], grads["w_ffn_out"], grads["w_out"] = (jnp.transpose(part("w_ffn_in_t"))[None],
                                                            part("w_ffn_out")[None], part("w_out")[None])

    names = ("c_ctx", "w_ada", "b_ada", "g_pre_mix", "g_post_mix", "g_pre_ffn", "g_post_ffn", "w_in", "attn_sink",
             "w_gate_fwd", "b_gate_fwd", "w_gate_bwd", "b_gate_bwd", "g_gla_norm", "w_out", "w_ffn_in", "w_ffn_out")
    weights = dict(zip(names, (c_ctx, w_ada, b_ada, g_pre_mix, g_post_mix, g_pre_ffn, g_post_ffn, w_in, attn_sink,
                               w_gate_fwd, b_gate_fwd, w_gate_bwd, b_gate_bwd, g_gla_norm, w_out, w_ffn_in,
                               w_ffn_out)))
    m_in = dict(zip(names, (m_c_ctx, m_w_ada, m_b_ada, m_g_pre_mix, m_g_post_mix, m_g_pre_ffn, m_g_post_ffn, m_w_in,
                            m_attn_sink, m_w_gate_fwd, m_b_gate_fwd, m_w_gate_bwd, m_b_gate_bwd, m_g_gla_norm,
                            m_w_out, m_w_ffn_in, m_w_ffn_out)))
    v_in = dict(zip(names, (v_c_ctx, v_w_ada, v_b_ada, v_g_pre_mix, v_g_post_mix, v_g_pre_ffn, v_g_post_ffn, v_w_in,
                            v_attn_sink, v_w_gate_fwd, v_b_gate_fwd, v_w_gate_bwd, v_b_gate_bwd, v_g_gla_norm,
                            v_w_out, v_w_ffn_in, v_w_ffn_out)))
    large = ("w_ada", "w_in", "w_out", "w_ffn_in", "w_ffn_out")
    tiny = tuple(n for n in names if n not in large)
    delta, new_m, new_v = {}, {}, {}
    for n in large:
        dl, nm, nv = _adamw("adamw_" + n, weights[n][0], grads[n][0], m_in[n][0], v_in[n][0])
        delta[n], new_m[n], new_v[n] = dl[None], nm[None], nv[None]
    for n in tiny:
        grads[n] = grads[n].reshape(weights[n].shape)
    as_rows = lambda t: t.reshape(-1, t.shape[-1])
    res = _adamw_many("adamw_small", *[[as_rows(t[n]) for n in tiny] for t in (weights, grads, m_in, v_in)])
    for out, vals in zip((delta, new_m, new_v), res):
        out.update({n: val.reshape(weights[n].shape) for n, val in zip(tiny, vals)})

    return (tot["loss"][0, 0], grad_x[None], *[grads[n] for n in names], *[delta[n] for n in names], *[new_m[n] for n in names],
            *[new_v[n] for n in names])
```
